```python
import jax, jax.numpy as jnp
from jax import lax
import numpy as np

D_MODEL = 1024
BATCH = 8
SEQ = 8192
DEPTH = 1

GRID_W = 64
CTX_LEN = 256
HEAD_DIM = 64
N_HEADS_TOTAL = D_MODEL // HEAD_DIM
ATT_HEADS = N_HEADS_TOTAL // 2
ATT_KV_HEADS = ATT_HEADS // 4
WINDOW = 128
BLOCK = 128
ROPE_BASE = 10000.0
GLA_HEADS = N_HEADS_TOTAL - ATT_HEADS
GLA_DV = HEAD_DIM
GLA_DK = HEAD_DIM // 2
GLA_CHUNK = 64
GATE_RANK = 16
GATE_TAU = 16.0
MIX_WIDTH = ATT_HEADS * HEAD_DIM + GLA_HEADS * GLA_DV
FFN_HIDDEN = -(-8 * D_MODEL // (3 * 256)) * 256
SPLIT_SIZES = (ATT_HEADS * HEAD_DIM, ATT_KV_HEADS * HEAD_DIM, ATT_KV_HEADS * HEAD_DIM,
               GLA_HEADS * GLA_DK, GLA_HEADS * GLA_DK, GLA_HEADS * GLA_DV, GLA_HEADS * GLA_DV,
               GATE_RANK, GATE_RANK)
IN_COLS = sum(SPLIT_SIZES)
NEG_INF = -1e30

kernel_name = 'hybrid_window_gqa_gla_dit_block'


def rmsnorm(x, gain, eps=1e-6):
    x32 = x.astype(jnp.float32)
    y = x32 * lax.rsqrt(jnp.mean(x32 * x32, axis=-1, keepdims=True) + eps)
    return y.astype(x.dtype) * gain


def modulate(h, shift, scale):
    return h * (1.0 + scale) + shift


def heads(t, n):
    return t.reshape(t.shape[:-1] + (n, t.shape[-1] // n))


def flip(t):
    return jnp.flip(t, axis=1)


def split_columns(p):
    idx = np.cumsum(SPLIT_SIZES)[:-1].tolist()
    return jnp.split(p, idx, axis=-1)


def axial_rope_tables(n_tokens):
    ROWS = n_tokens // GRID_W
    row = jnp.repeat(jnp.arange(ROWS), GRID_W).astype(jnp.float32)
    col = jnp.tile(jnp.arange(GRID_W), ROWS).astype(jnp.float32)
    half = HEAD_DIM // 2
    inv_freq = ROPE_BASE ** (-jnp.arange(0, half, 2, dtype=jnp.float32) / half)
    ang_r = row[:, None] * inv_freq[None, :]
    ang_c = col[:, None] * inv_freq[None, :]
    ang = jnp.concatenate([ang_r, ang_r, ang_c, ang_c], axis=-1)
    return jnp.cos(ang), jnp.sin(ang)


def apply_rope(x, cos, sin):
    shp = x.shape
    xr = x.reshape(shp[:-1] + (2, 2, HEAD_DIM // 4))
    rot = jnp.concatenate([-xr[..., 1:2, :], xr[..., 0:1, :]], axis=-2).reshape(shp)
    return x * cos[:, None, :].astype(x.dtype) + rot * sin[:, None, :].astype(x.dtype)


def softmax_with_sink(logits, sink):
    sink_col = jnp.broadcast_to(sink, logits.shape[:-1] + (1,))
    p = jax.nn.softmax(jnp.concatenate([logits, sink_col], axis=-1), axis=-1)
    return p[..., :-1]


def window_attention(q, k, v, k_ctx, v_ctx, sink):
    B, S, H, dh = q.shape
    G = H // ATT_KV_HEADS
    nb = S // BLOCK
    scale = dh ** -0.5
    qb = q.reshape(B, nb, BLOCK, ATT_KV_HEADS, G, dh)
    pad = ((0, 0), (BLOCK, BLOCK), (0, 0), (0, 0))
    kp = jnp.pad(k, pad).reshape(B, nb + 2, BLOCK, ATT_KV_HEADS, dh)
    vp = jnp.pad(v, pad).reshape(B, nb + 2, BLOCK, ATT_KV_HEADS, dh)
    kw = jnp.concatenate([kp[:, :-2], kp[:, 1:-1], kp[:, 2:]], axis=2)
    vw = jnp.concatenate([vp[:, :-2], vp[:, 1:-1], vp[:, 2:]], axis=2)
    s_win = jnp.einsum('bnqhgd,bnkhd->bhgnqk', qb, kw).astype(jnp.float32) * scale
    qpos = jnp.arange(nb)[:, None] * BLOCK + jnp.arange(BLOCK)[None, :]
    kpos = (jnp.arange(nb)[:, None] - 1) * BLOCK + jnp.arange(3 * BLOCK)[None, :]
    rel = kpos[:, None, :] - qpos[:, :, None]
    mask = (jnp.abs(rel) <= WINDOW) & (kpos[:, None, :] >= 0) & (kpos[:, None, :] < S)
    s_win = jnp.where(mask, s_win, NEG_INF)
    s_ctx = jnp.einsum('bnqhgd,bchd->bhgnqc', qb, k_ctx).astype(jnp.float32) * scale
    sink_b = sink.astype(jnp.float32).reshape(ATT_KV_HEADS, G)[None, :, :, None, None, None]
    p = softmax_with_sink(jnp.concatenate([s_win, s_ctx], axis=-1), sink_b).astype(v.dtype)
    n_win = 3 * BLOCK
    o = (jnp.einsum('bhgnqk,bnkhd->bnqhgd', p[..., :n_win], vw)
         + jnp.einsum('bhgnqc,bchd->bnqhgd', p[..., n_win:], v_ctx))
    return o.reshape(B, S, H * dh)


def context_attention(q, k, v, sink):
    B, C, H, dh = q.shape
    G = H // ATT_KV_HEADS
    qg = q.reshape(B, C, ATT_KV_HEADS, G, dh)
    s = jnp.einsum('bqhgd,bkhd->bhgqk', qg, k).astype(jnp.float32) * dh ** -0.5
    sink_b = sink.astype(jnp.float32).reshape(ATT_KV_HEADS, G)[None, :, :, None, None]
    p = softmax_with_sink(s, sink_b).astype(v.dtype)
    return jnp.einsum('bhgqk,bkhd->bqhgd', p, v).reshape(B, C, H * dh)


def log_decay(z, w_gate, b_gate):
    logits = (z @ w_gate + b_gate).astype(jnp.float32)
    return heads(jax.nn.log_sigmoid(logits) / GATE_TAU, GLA_HEADS)


def gla_chunked(q, k, v, log_a, s0):
    B, T, H, DK = q.shape
    DV = v.shape[-1]
    n = T // GLA_CHUNK
    f32 = jnp.float32
    qc = q.astype(f32).reshape(B, n, GLA_CHUNK, H, DK)
    kc = k.astype(f32).reshape(B, n, GLA_CHUNK, H, DK)
    vc = v.astype(f32).reshape(B, n, GLA_CHUNK, H, DV)
    b = jnp.cumsum(log_a.astype(f32).reshape(B, n, GLA_CHUNK, H, DK), axis=2)
    b_last = b[:, :, -1:]
    q_dec = qc * jnp.exp(b)
    k_inv = kc * jnp.exp(-b)
    lower = jnp.tril(jnp.ones((GLA_CHUNK, GLA_CHUNK), dtype=bool))
    A = jnp.where(lower, jnp.einsum('bnihk,bnjhk->bnhij', q_dec, k_inv), 0.0)
    intra = jnp.einsum('bnhij,bnjhv->bnihv', A, vc)
    dS = jnp.einsum('bnjhk,bnjhv->bnhkv', kc * jnp.exp(b_last - b), vc)
    decay = jnp.exp(b_last[:, :, 0])

    def step(state, inp):
        d, ds = inp
        return d[..., None] * state + ds, state

    s_final, s_before = lax.scan(step, s0, (jnp.moveaxis(decay, 1, 0), jnp.moveaxis(dS, 1, 0)))
    inter = jnp.einsum('bnihk,nbhkv->bnihv', q_dec, s_before)
    o = (intra + inter).reshape(B, T, H, DV).astype(v.dtype)
    return o, s_final


def gla_final_state(k, v, log_a):
    b = jnp.cumsum(log_a.astype(jnp.float32), axis=1)
    w = jnp.exp(b[:, -1:] - b)
    return jnp.einsum('bthk,bthv->bhkv', k.astype(jnp.float32) * w, v.astype(jnp.float32))


def gla_output(o, gate, g_norm):
    return rmsnorm(o, g_norm).reshape(gate.shape) * jax.nn.silu(gate)


def swiglu(h, w_in, w_out):
    g, u = jnp.split(h @ w_in, 2, axis=-1)
    return (jax.nn.silu(g) * u) @ w_out


def _fwd_setup_inputs(seed: int = 0) -> dict:
    key = jax.random.key(seed)
    ks = jax.random.split(key, 24)
    L, D = DEPTH, D_MODEL
    nrm = jax.random.normal
    f32 = jnp.float32
    return {
        'x': nrm(ks[0], (BATCH, SEQ, D), f32),
        'c': nrm(ks[1], (BATCH, D), f32),
        'ctx': nrm(ks[2], (BATCH, CTX_LEN, D), f32),
        'c_ctx': nrm(ks[3], (D,), f32),
        'w_ada': nrm(ks[4], (L, D, 6 * D), f32) * (0.5 * D ** -0.5),
        'b_ada': nrm(ks[5], (L, 6 * D), f32) * 0.02,
        'g_pre_mix': 1.0 + 0.05 * nrm(ks[6], (L, D), f32),
        'g_post_mix': 1.0 + 0.05 * nrm(ks[7], (L, D), f32),
        'g_pre_ffn': 1.0 + 0.05 * nrm(ks[8], (L, D), f32),
        'g_post_ffn': 1.0 + 0.05 * nrm(ks[9], (L, D), f32),
        'w_in': nrm(ks[10], (L, D, IN_COLS), f32) * D ** -0.5,
        'attn_sink': 0.5 * nrm(ks[11], (L, ATT_HEADS), f32),
        'w_gate_fwd': nrm(ks[12], (L, GATE_RANK, GLA_HEADS * GLA_DK), f32) * GATE_RANK ** -0.5,
        'b_gate_fwd': 0.5 * nrm(ks[13], (L, GLA_HEADS * GLA_DK), f32),
        'w_gate_bwd': nrm(ks[14], (L, GATE_RANK, GLA_HEADS * GLA_DK), f32) * GATE_RANK ** -0.5,
        'b_gate_bwd': 0.5 * nrm(ks[15], (L, GLA_HEADS * GLA_DK), f32),
        'g_gla_norm': 1.0 + 0.05 * nrm(ks[16], (L, GLA_DV), f32),
        'w_out': nrm(ks[17], (L, MIX_WIDTH, D), f32) * MIX_WIDTH ** -0.5,
        'w_ffn_in': nrm(ks[18], (L, D, 2 * FFN_HIDDEN), f32) * D ** -0.5,
        'w_ffn_out': nrm(ks[19], (L, FFN_HIDDEN, D), f32) * FFN_HIDDEN ** -0.5,
    }


def _fwd_reference(x, c, ctx, c_ctx, w_ada, b_ada, g_pre_mix, g_post_mix, g_pre_ffn, g_post_ffn,
              w_in, attn_sink, w_gate_fwd, b_gate_fwd, w_gate_bwd, b_gate_bwd, g_gla_norm,
              w_out, w_ffn_in, w_ffn_out):
    B, S, _ = x.shape
    cos, sin = axial_rope_tables(S)
    zero_state = jnp.zeros((B, GLA_HEADS, GLA_DK, GLA_DV), jnp.float32)
    for l in range(DEPTH):
        need_ctx_out = l < DEPTH - 1
        ada = jax.nn.silu(c) @ w_ada[l] + b_ada[l]
        ada_c = jax.nn.silu(c_ctx) @ w_ada[l] + b_ada[l]
        sh1, sc1, gt1, sh2, sc2, gt2 = jnp.split(ada[:, None, :], 6, axis=-1)
        sh1c, sc1c, gt1c, sh2c, sc2c, gt2c = jnp.split(ada_c, 6, axis=-1)

        h = modulate(rmsnorm(x, g_pre_mix[l]), sh1, sc1)
        hc = modulate(rmsnorm(ctx, g_pre_mix[l]), sh1c, sc1c)
        q, k, v, gq, gk, gv, gg, zf, zb = split_columns(h @ w_in[l])
        qc, kc, vc, gqc, gkc, gvc, ggc, zfc, zbc = split_columns(hc @ w_in[l])

        q_h = apply_rope(heads(q, ATT_HEADS), cos, sin)
        k_h = apply_rope(heads(k, ATT_KV_HEADS), cos, sin)
        v_h = heads(v, ATT_KV_HEADS)
        kc_h = heads(kc, ATT_KV_HEADS)
        vc_h = heads(vc, ATT_KV_HEADS)
        attn_lat = window_attention(q_h, k_h, v_h, kc_h, vc_h, attn_sink[l])

        gkc_h = heads(gkc, GLA_HEADS)
        gvc_h = heads(gvc, GLA_HEADS)
        la_fc = log_decay(zfc, w_gate_fwd[l], b_gate_fwd[l])
        la_bc = log_decay(zbc, w_gate_bwd[l], b_gate_bwd[l])
        if need_ctx_out:
            gqc_h = heads(gqc, GLA_HEADS) * GLA_DK ** -0.5
            oc_f, s_f = gla_chunked(gqc_h, gkc_h, gvc_h, la_fc, zero_state)
            oc_b, s_b = gla_chunked(flip(gqc_h), flip(gkc_h), flip(gvc_h), flip(la_bc), zero_state)
            gla_ctx = gla_output(oc_f + flip(oc_b), ggc, g_gla_norm[l])
            attn_ctx = context_attention(heads(qc, ATT_HEADS), kc_h, vc_h, attn_sink[l])
        else:
            s_f = gla_final_state(gkc_h, gvc_h, la_fc)
            s_b = gla_final_state(flip(gkc_h), flip(gvc_h), flip(la_bc))

        gq_h = heads(gq, GLA_HEADS) * GLA_DK ** -0.5
        gk_h = heads(gk, GLA_HEADS)
        gv_h = heads(gv, GLA_HEADS)
        la_f = log_decay(zf, w_gate_fwd[l], b_gate_fwd[l])
        la_b = log_decay(zb, w_gate_bwd[l], b_gate_bwd[l])
        o_f, _ = gla_chunked(gq_h, gk_h, gv_h, la_f, s_f)
        o_b, _ = gla_chunked(flip(gq_h), flip(gk_h), flip(gv_h), flip(la_b), s_b)
        gla_lat = gla_output(o_f + flip(o_b), gg, g_gla_norm[l])

        y = jnp.concatenate([attn_lat, gla_lat], axis=-1) @ w_out[l]
        x = x + gt1 * rmsnorm(y, g_post_mix[l])

        f = swiglu(modulate(rmsnorm(x, g_pre_ffn[l]), sh2, sc2), w_ffn_in[l], w_ffn_out[l])
        x = x + gt2 * rmsnorm(f, g_post_ffn[l])

        if need_ctx_out:
            yc = jnp.concatenate([attn_ctx, gla_ctx], axis=-1) @ w_out[l]
            ctx = ctx + gt1c * rmsnorm(yc, g_post_mix[l])
            fc = swiglu(modulate(rmsnorm(ctx, g_pre_ffn[l]), sh2c, sc2c), w_ffn_in[l], w_ffn_out[l])
            ctx = ctx + gt2c * rmsnorm(fc, g_post_ffn[l])
    return x


import jax as _jax
import jax.numpy as _jnp

TWIN_FORMAT = 'train_step'
FWD_PARAMS = ['x', 'c', 'ctx', 'c_ctx', 'w_ada', 'b_ada', 'g_pre_mix', 'g_post_mix', 'g_pre_ffn', 'g_post_ffn', 'w_in', 'attn_sink', 'w_gate_fwd', 'b_gate_fwd', 'w_gate_bwd', 'b_gate_bwd', 'g_gla_norm', 'w_out', 'w_ffn_in', 'w_ffn_out']
TWIN_WEIGHTS = ['c_ctx', 'w_ada', 'b_ada', 'g_pre_mix', 'g_post_mix', 'g_pre_ffn', 'g_post_ffn', 'w_in', 'attn_sink', 'w_gate_fwd', 'b_gate_fwd', 'w_gate_bwd', 'b_gate_bwd', 'g_gla_norm', 'w_out', 'w_ffn_in', 'w_ffn_out']
TWIN_DIFF_INPUT = 'x'
TWIN_INPUTS = ['x', 'c', 'ctx', 'c_ctx', 'w_ada', 'b_ada', 'g_pre_mix', 'g_post_mix', 'g_pre_ffn', 'g_post_ffn', 'w_in', 'attn_sink', 'w_gate_fwd', 'b_gate_fwd', 'w_gate_bwd', 'b_gate_bwd', 'g_gla_norm', 'w_out', 'w_ffn_in', 'w_ffn_out', 'loss_target', 'm_c_ctx', 'm_w_ada', 'm_b_ada', 'm_g_pre_mix', 'm_g_post_mix', 'm_g_pre_ffn', 'm_g_post_ffn', 'm_w_in', 'm_attn_sink', 'm_w_gate_fwd', 'm_b_gate_fwd', 'm_w_gate_bwd', 'm_b_gate_bwd', 'm_g_gla_norm', 'm_w_out', 'm_w_ffn_in', 'm_w_ffn_out', 'v_c_ctx', 'v_w_ada', 'v_b_ada', 'v_g_pre_mix', 'v_g_post_mix', 'v_g_pre_ffn', 'v_g_post_ffn', 'v_w_in', 'v_attn_sink', 'v_w_gate_fwd', 'v_b_gate_fwd', 'v_w_gate_bwd', 'v_b_gate_bwd', 'v_g_gla_norm', 'v_w_out', 'v_w_ffn_in', 'v_w_ffn_out']
TWIN_OUTPUTS = ['loss', 'grad_x', 'grad_c_ctx', 'grad_w_ada', 'grad_b_ada', 'grad_g_pre_mix', 'grad_g_post_mix', 'grad_g_pre_ffn', 'grad_g_post_ffn', 'grad_w_in', 'grad_attn_sink', 'grad_w_gate_fwd', 'grad_b_gate_fwd', 'grad_w_gate_bwd', 'grad_b_gate_bwd', 'grad_g_gla_norm', 'grad_w_out', 'grad_w_ffn_in', 'grad_w_ffn_out', 'delta_c_ctx', 'delta_w_ada', 'delta_b_ada', 'delta_g_pre_mix', 'delta_g_post_mix', 'delta_g_pre_ffn', 'delta_g_post_ffn', 'delta_w_in', 'delta_attn_sink', 'delta_w_gate_fwd', 'delta_b_gate_fwd', 'delta_w_gate_bwd', 'delta_b_gate_bwd', 'delta_g_gla_norm', 'delta_w_out', 'delta_w_ffn_in', 'delta_w_ffn_out', 'new_m_c_ctx', 'new_m_w_ada', 'new_m_b_ada', 'new_m_g_pre_mix', 'new_m_g_post_mix', 'new_m_g_pre_ffn', 'new_m_g_post_ffn', 'new_m_w_in', 'new_m_attn_sink', 'new_m_w_gate_fwd', 'new_m_b_gate_fwd', 'new_m_w_gate_bwd', 'new_m_b_gate_bwd', 'new_m_g_gla_norm', 'new_m_w_out', 'new_m_w_ffn_in', 'new_m_w_ffn_out', 'new_v_c_ctx', 'new_v_w_ada', 'new_v_b_ada', 'new_v_g_pre_mix', 'new_v_g_post_mix', 'new_v_g_pre_ffn', 'new_v_g_post_ffn', 'new_v_w_in', 'new_v_attn_sink', 'new_v_w_gate_fwd', 'new_v_b_gate_fwd', 'new_v_w_gate_bwd', 'new_v_b_gate_bwd', 'new_v_g_gla_norm', 'new_v_w_out', 'new_v_w_ffn_in', 'new_v_w_ffn_out']
TWIN_LEAF_KINDS = {'loss': 'loss', 'grad_x': 'grad_x', 'grad_c_ctx': 'grad_w', 'grad_w_ada': 'grad_w', 'grad_b_ada': 'grad_w', 'grad_g_pre_mix': 'grad_w', 'grad_g_post_mix': 'grad_w', 'grad_g_pre_ffn': 'grad_w', 'grad_g_post_ffn': 'grad_w', 'grad_w_in': 'grad_w', 'grad_attn_sink': 'grad_w', 'grad_w_gate_fwd': 'grad_w', 'grad_b_gate_fwd': 'grad_w', 'grad_w_gate_bwd': 'grad_w', 'grad_b_gate_bwd': 'grad_w', 'grad_g_gla_norm': 'grad_w', 'grad_w_out': 'grad_w', 'grad_w_ffn_in': 'grad_w', 'grad_w_ffn_out': 'grad_w', 'delta_c_ctx': 'delta_w', 'delta_w_ada': 'delta_w', 'delta_b_ada': 'delta_w', 'delta_g_pre_mix': 'delta_w', 'delta_g_post_mix': 'delta_w', 'delta_g_pre_ffn': 'delta_w', 'delta_g_post_ffn': 'delta_w', 'delta_w_in': 'delta_w', 'delta_attn_sink': 'delta_w', 'delta_w_gate_fwd': 'delta_w', 'delta_b_gate_fwd': 'delta_w', 'delta_w_gate_bwd': 'delta_w', 'delta_b_gate_bwd': 'delta_w', 'delta_g_gla_norm': 'delta_w', 'delta_w_out': 'delta_w', 'delta_w_ffn_in': 'delta_w', 'delta_w_ffn_out': 'delta_w', 'new_m_c_ctx': 'new_m', 'new_m_w_ada': 'new_m', 'new_m_b_ada': 'new_m', 'new_m_g_pre_mix': 'new_m', 'new_m_g_post_mix': 'new_m', 'new_m_g_pre_ffn': 'new_m', 'new_m_g_post_ffn': 'new_m', 'new_m_w_in': 'new_m', 'new_m_attn_sink': 'new_m', 'new_m_w_gate_fwd': 'new_m', 'new_m_b_gate_fwd': 'new_m', 'new_m_w_gate_bwd': 'new_m', 'new_m_b_gate_bwd': 'new_m', 'new_m_g_gla_norm': 'new_m', 'new_m_w_out': 'new_m', 'new_m_w_ffn_in': 'new_m', 'new_m_w_ffn_out': 'new_m', 'new_v_c_ctx': 'new_v', 'new_v_w_ada': 'new_v', 'new_v_b_ada': 'new_v', 'new_v_g_pre_mix': 'new_v', 'new_v_g_post_mix': 'new_v', 'new_v_g_pre_ffn': 'new_v', 'new_v_g_post_ffn': 'new_v', 'new_v_w_in': 'new_v', 'new_v_attn_sink': 'new_v', 'new_v_w_gate_fwd': 'new_v', 'new_v_b_gate_fwd': 'new_v', 'new_v_w_gate_bwd': 'new_v', 'new_v_b_gate_bwd': 'new_v', 'new_v_g_gla_norm': 'new_v', 'new_v_w_out': 'new_v', 'new_v_w_ffn_in': 'new_v', 'new_v_w_ffn_out': 'new_v'}


def _forward(args):
    return _fwd_reference(*[args[k] for k in FWD_PARAMS])


def _output_shape():
    def fwd():
        inp = _fwd_setup_inputs(0)
        return _fwd_reference(*[inp[k] for k in FWD_PARAMS])
    out = _jax.eval_shape(fwd)
    return out.shape, out.dtype

N_MICROBATCH = 1
ADAM_LR = 0.001
ADAM_B1 = 0.9
ADAM_B2 = 0.999
ADAM_EPS = 1e-08
ADAM_WD = 0.01
ADAM_STEP = 10
PER_EXAMPLE_BATCH_AXIS = {'x': 0, 'c': 0, 'ctx': 0, 'loss_target': 0}
SHARED_INPUTS = []
_WEIGHT_DTYPES = {'c_ctx': _jnp.float32, 'w_ada': _jnp.float32, 'b_ada': _jnp.float32, 'g_pre_mix': _jnp.float32, 'g_post_mix': _jnp.float32, 'g_pre_ffn': _jnp.float32, 'g_post_ffn': _jnp.float32, 'w_in': _jnp.float32, 'attn_sink': _jnp.float32, 'w_gate_fwd': _jnp.float32, 'b_gate_fwd': _jnp.float32, 'w_gate_bwd': _jnp.float32, 'b_gate_bwd': _jnp.float32, 'g_gla_norm': _jnp.float32, 'w_out': _jnp.float32, 'w_ffn_in': _jnp.float32, 'w_ffn_out': _jnp.float32}
MOMENT_SCALE = {'c_ctx': 1.833828e-01, 'w_ada': 2.431568e+00, 'b_ada': 5.318137e+00, 'g_pre_mix': 2.602808e-01, 'g_post_mix': 6.662049e+00, 'g_pre_ffn': 1.700006e-01, 'g_post_ffn': 6.714643e+00, 'w_in': 2.214845e-01, 'attn_sink': 3.435152e-03, 'w_gate_fwd': 3.289351e-02, 'b_gate_fwd': 9.521478e-02, 'w_gate_bwd': 3.445067e-02, 'b_gate_bwd': 8.172380e-02, 'g_gla_norm': 4.838123e-01, 'w_out': 2.633774e-01, 'w_ffn_in': 8.498164e-02, 'w_ffn_out': 1.552056e-01}


def _to_microbatches(a, axis):
    t = _jnp.moveaxis(a, axis, 0)
    t = t.reshape((N_MICROBATCH, t.shape[0] // N_MICROBATCH) + t.shape[1:])
    return _jnp.moveaxis(t, 1, axis + 1)


def setup_inputs(seed: int = 0) -> dict:
    inp = _fwd_setup_inputs(seed)
    key = _jax.random.fold_in(_jax.random.key(seed), 7919)
    shape, _ = _output_shape()
    out = dict(inp)
    out["loss_target"] = _jax.random.normal(_jax.random.fold_in(key, 0), shape, _jnp.float32)
    for i, name in enumerate(TWIN_WEIGHTS):
        w = inp[name].astype(_jnp.float32)
        if MOMENT_SCALE is None:
            s = _jnp.sqrt(_jnp.mean(_jnp.square(w)) + 1e-30)
        else:
            s = MOMENT_SCALE[name]
        km, kv = _jax.random.split(_jax.random.fold_in(key, i + 1))
        out[name] = w
        out["m_" + name] = s * _jax.random.normal(km, w.shape, _jnp.float32)
        out["v_" + name] = (s * s) * _jax.random.uniform(kv, w.shape, _jnp.float32, 0.5, 1.5)
    if N_MICROBATCH > 1:
        for name, axis in PER_EXAMPLE_BATCH_AXIS.items():
            out[name] = _to_microbatches(out[name], axis)
    return {'x': out['x'], 'c': out['c'], 'ctx': out['ctx'], 'c_ctx': out['c_ctx'], 'w_ada': out['w_ada'], 'b_ada': out['b_ada'], 'g_pre_mix': out['g_pre_mix'], 'g_post_mix': out['g_post_mix'], 'g_pre_ffn': out['g_pre_ffn'], 'g_post_ffn': out['g_post_ffn'], 'w_in': out['w_in'], 'attn_sink': out['attn_sink'], 'w_gate_fwd': out['w_gate_fwd'], 'b_gate_fwd': out['b_gate_fwd'], 'w_gate_bwd': out['w_gate_bwd'], 'b_gate_bwd': out['b_gate_bwd'], 'g_gla_norm': out['g_gla_norm'], 'w_out': out['w_out'], 'w_ffn_in': out['w_ffn_in'], 'w_ffn_out': out['w_ffn_out'], 'loss_target': out['loss_target'], 'm_c_ctx': out['m_c_ctx'], 'm_w_ada': out['m_w_ada'], 'm_b_ada': out['m_b_ada'], 'm_g_pre_mix': out['m_g_pre_mix'], 'm_g_post_mix': out['m_g_post_mix'], 'm_g_pre_ffn': out['m_g_pre_ffn'], 'm_g_post_ffn': out['m_g_post_ffn'], 'm_w_in': out['m_w_in'], 'm_attn_sink': out['m_attn_sink'], 'm_w_gate_fwd': out['m_w_gate_fwd'], 'm_b_gate_fwd': out['m_b_gate_fwd'], 'm_w_gate_bwd': out['m_w_gate_bwd'], 'm_b_gate_bwd': out['m_b_gate_bwd'], 'm_g_gla_norm': out['m_g_gla_norm'], 'm_w_out': out['m_w_out'], 'm_w_ffn_in': out['m_w_ffn_in'], 'm_w_ffn_out': out['m_w_ffn_out'], 'v_c_ctx': out['v_c_ctx'], 'v_w_ada': out['v_w_ada'], 'v_b_ada': out['v_b_ada'], 'v_g_pre_mix': out['v_g_pre_mix'], 'v_g_post_mix': out['v_g_post_mix'], 'v_g_pre_ffn': out['v_g_pre_ffn'], 'v_g_post_ffn': out['v_g_post_ffn'], 'v_w_in': out['v_w_in'], 'v_attn_sink': out['v_attn_sink'], 'v_w_gate_fwd': out['v_w_gate_fwd'], 'v_b_gate_fwd': out['v_b_gate_fwd'], 'v_w_gate_bwd': out['v_w_gate_bwd'], 'v_b_gate_bwd': out['v_b_gate_bwd'], 'v_g_gla_norm': out['v_g_gla_norm'], 'v_w_out': out['v_w_out'], 'v_w_ffn_in': out['v_w_ffn_in'], 'v_w_ffn_out': out['v_w_ffn_out']}


def _loss(weights, diff, rest, loss_target):
    with _jax.named_scope("forward"):
        args = {**rest, TWIN_DIFF_INPUT: diff, **{k: w.astype(_WEIGHT_DTYPES[k]) for k, w in weights.items()}}
        y = _forward(args)
    with _jax.named_scope("loss_head"):
        err = _jnp.square(y.astype(_jnp.float32) - loss_target)
        return 0.5 * _jnp.sum(_jnp.mean(err, axis=-1)) if err.ndim else 0.5 * err


def _adamw(w, g, m, v):
    m = ADAM_B1 * m + (1.0 - ADAM_B1) * g
    v = ADAM_B2 * v + (1.0 - ADAM_B2) * _jnp.square(g)
    m_hat = m / (1.0 - ADAM_B1 ** ADAM_STEP)
    v_hat = v / (1.0 - ADAM_B2 ** ADAM_STEP)
    delta = -ADAM_LR * (m_hat / (_jnp.sqrt(v_hat) + ADAM_EPS) + ADAM_WD * w)
    return delta, m, v


def reference(x, c, ctx, c_ctx, w_ada, b_ada, g_pre_mix, g_post_mix, g_pre_ffn, g_post_ffn, w_in, attn_sink, w_gate_fwd, b_gate_fwd, w_gate_bwd, b_gate_bwd, g_gla_norm, w_out, w_ffn_in, w_ffn_out, loss_target, m_c_ctx, m_w_ada, m_b_ada, m_g_pre_mix, m_g_post_mix, m_g_pre_ffn, m_g_post_ffn, m_w_in, m_attn_sink, m_w_gate_fwd, m_b_gate_fwd, m_w_gate_bwd, m_b_gate_bwd, m_g_gla_norm, m_w_out, m_w_ffn_in, m_w_ffn_out, v_c_ctx, v_w_ada, v_b_ada, v_g_pre_mix, v_g_post_mix, v_g_pre_ffn, v_g_post_ffn, v_w_in, v_attn_sink, v_w_gate_fwd, v_b_gate_fwd, v_w_gate_bwd, v_b_gate_bwd, v_g_gla_norm, v_w_out, v_w_ffn_in, v_w_ffn_out):
    given = dict(x=x, c=c, ctx=ctx, c_ctx=c_ctx, w_ada=w_ada, b_ada=b_ada, g_pre_mix=g_pre_mix, g_post_mix=g_post_mix, g_pre_ffn=g_pre_ffn, g_post_ffn=g_post_ffn, w_in=w_in, attn_sink=attn_sink, w_gate_fwd=w_gate_fwd, b_gate_fwd=b_gate_fwd, w_gate_bwd=w_gate_bwd, b_gate_bwd=b_gate_bwd, g_gla_norm=g_gla_norm, w_out=w_out, w_ffn_in=w_ffn_in, w_ffn_out=w_ffn_out, loss_target=loss_target, m_c_ctx=m_c_ctx, m_w_ada=m_w_ada, m_b_ada=m_b_ada, m_g_pre_mix=m_g_pre_mix, m_g_post_mix=m_g_post_mix, m_g_pre_ffn=m_g_pre_ffn, m_g_post_ffn=m_g_post_ffn, m_w_in=m_w_in, m_attn_sink=m_attn_sink, m_w_gate_fwd=m_w_gate_fwd, m_b_gate_fwd=m_b_gate_fwd, m_w_gate_bwd=m_w_gate_bwd, m_b_gate_bwd=m_b_gate_bwd, m_g_gla_norm=m_g_gla_norm, m_w_out=m_w_out, m_w_ffn_in=m_w_ffn_in, m_w_ffn_out=m_w_ffn_out, v_c_ctx=v_c_ctx, v_w_ada=v_w_ada, v_b_ada=v_b_ada, v_g_pre_mix=v_g_pre_mix, v_g_post_mix=v_g_post_mix, v_g_pre_ffn=v_g_pre_ffn, v_g_post_ffn=v_g_post_ffn, v_w_in=v_w_in, v_attn_sink=v_attn_sink, v_w_gate_fwd=v_w_gate_fwd, v_b_gate_fwd=v_b_gate_fwd, v_w_gate_bwd=v_w_gate_bwd, v_b_gate_bwd=v_b_gate_bwd, v_g_gla_norm=v_g_gla_norm, v_w_out=v_w_out, v_w_ffn_in=v_w_ffn_in, v_w_ffn_out=v_w_ffn_out)
    weights = {n: given[n] for n in TWIN_WEIGHTS}
    shared = {n: given[n] for n in SHARED_INPUTS}
    per_example = {n: given[n] for n in ['x', 'c', 'ctx']}
    grad_fn = _jax.value_and_grad(_loss, argnums=(0, 1))

    def one_microbatch(ex, loss_target):
        ex = dict(ex)
        diff = ex.pop(TWIN_DIFF_INPUT)
        return grad_fn(weights, diff, {**shared, **ex}, loss_target)

    if N_MICROBATCH == 1:
        loss, (grad_w, grad_x) = one_microbatch(per_example, given["loss_target"])
    else:
        def body(carry, xs):
            loss_sum, grad_sum = carry
            l_k, (gw_k, gx_k) = one_microbatch(xs[0], xs[1])
            with _jax.named_scope("update"):
                return (loss_sum + l_k, _jax.tree.map(_jnp.add, grad_sum, gw_k)), gx_k

        init = (_jnp.zeros((), _jnp.float32), _jax.tree.map(_jnp.zeros_like, weights))
        (loss, grad_w), grad_x = _jax.lax.scan(body, init, (per_example, given["loss_target"]))
    with _jax.named_scope("update"):
        delta_w, new_m, new_v = {}, {}, {}
        for n in TWIN_WEIGHTS:
            delta_w[n], new_m[n], new_v[n] = _adamw(weights[n], grad_w[n], given["m_" + n], given["v_" + n])
    return (loss, grad_x, *[grad_w[n] for n in TWIN_WEIGHTS], *[delta_w[n] for n in TWIN_WEIGHTS],
            *[new_m[n] for n in TWIN_WEIGHTS], *[new_v[n] for n in TWIN_WEIGHTS])
```

```python
import functools
import math

import numpy as np
import jax
import jax.numpy as jnp
from jax import lax
from jax.experimental import pallas as pl
from jax.experimental.pallas import tpu as pltpu

F32 = jnp.float32
_BF = jnp.bfloat16

N_DEV = 8
D = 1024
CTX = 256
HD = 64
N_ATT = 8
N_KV = 2
GRP = N_ATT // N_KV
WIN = 128
GRID_W = 64
ROPE_BASE = 10000.0
N_GLA = 8
DK = 32
DV = 64
GATE_RANK = 16
GATE_TAU = 16.0
FFN = 2816
EPS = 1e-6
NEG = -1e30
GLA_T = 128

QP = N_ATT * 128
KP = N_KV * 128
O_Q, O_K, O_V = 0, QP, QP + KP
O_GQ = O_V + KP
O_GK = O_GQ + N_GLA * DK
O_GV = O_GK + N_GLA * DK
O_GG = O_GV + N_GLA * DV
O_Z = O_GG + N_GLA * DV
NP = O_Z + 128
IN_COLS = 2336

ADAM_LR, ADAM_B1, ADAM_B2, ADAM_EPS, ADAM_WD, ADAM_STEP = 0.001, 0.9, 0.999, 1e-08, 0.01, 10

VMEM_BIG = 56 * 1024 * 1024
MESH = pl.DeviceIdType.MESH


def _cp(sem, vmem=None):
    return pltpu.CompilerParams(dimension_semantics=sem, vmem_limit_bytes=vmem)


def _full(shape):
    nd = len(shape)
    return pl.BlockSpec(shape, lambda *a: (0,) * nd)


def _rows(tile, width, off=0):
    return pl.BlockSpec((tile, width), lambda i: (i + off, 0))


def _rows_lat(tile, width):
    return pl.BlockSpec((tile, width), lambda i: (jnp.maximum(i - 1, 0), 0))


def _nt(a, b):
    return lax.dot_general(a, b, (((1,), (1,)), ((), ())), preferred_element_type=F32)


def _tn(a, b):
    return lax.dot_general(a, b, (((0,), (0,)), ((), ())), preferred_element_type=F32)


def _nn(a, b):
    return jnp.dot(a, b, preferred_element_type=F32)


def _nn_hi(a, b):
    return jnp.dot(a, b, preferred_element_type=F32, precision=lax.Precision.HIGHEST)


def _rope(t, cos, sa, sb):
    n = t.shape[1]
    reps = n // 128
    c = jnp.tile(cos, (1, reps))
    a = jnp.tile(sa, (1, reps))
    b = jnp.tile(sb, (1, reps))
    return t * c + pltpu.roll(t, n - 16, 1) * a + pltpu.roll(t, 16, 1) * b


def _unrope(t, cos, sa, sb):
    n = t.shape[1]
    reps = n // 128
    c = jnp.tile(cos, (1, reps))
    a = jnp.tile(sa, (1, reps))
    b = jnp.tile(sb, (1, reps))
    return t * c + pltpu.roll(t * a, 16, 1) + pltpu.roll(t * b, n - 16, 1)


def _sigmoid(x):
    return 1.0 / (1.0 + jnp.exp(-x))


def _inproj_fwd(xe, gml, shl, gmc, shc, win, wg, bg, cos, sa, sb):
    E = xe.shape[0]
    TE = 256

    def body(x_ref, gml_ref, shl_ref, gmc_ref, shc_ref, w_ref, wg_ref, bg_ref, cos_ref, sa_ref, sb_ref,
             h_ref, q_ref, k_ref, v_ref, gq_ref, gk_ref, gv_ref, gg_ref, z_ref, la_ref):
        is_ctx = pl.program_id(0) == 0
        gm = jnp.where(is_ctx, gmc_ref[...], gml_ref[...])
        sh = jnp.where(is_ctx, shc_ref[...], shl_ref[...])
        x = x_ref[...]
        r = lax.rsqrt(jnp.mean(x * x, axis=-1, keepdims=True) + EPS)
        hb = ((x * r) * gm + sh).astype(_BF)
        h_ref[...] = hb
        p = _nn(hb, w_ref[...])
        cos_t, sa_t, sb_t = cos_ref[...], sa_ref[...], sb_ref[...]
        q_ref[...] = (_rope(p[:, O_Q:O_K], cos_t, sa_t, sb_t) * (HD ** -0.5)).astype(_BF)
        k_ref[...] = _rope(p[:, O_K:O_V], cos_t, sa_t, sb_t).astype(_BF)
        v_ref[...] = p[:, O_V:O_GQ].astype(_BF)
        gq_ref[...] = p[:, O_GQ:O_GK] * (DK ** -0.5)
        gk_ref[...] = p[:, O_GK:O_GV]
        gv_ref[...] = p[:, O_GV:O_GG]
        gg_ref[...] = p[:, O_GG:O_Z]
        zb = p[:, O_Z:NP].astype(_BF)
        z_ref[...] = zb
        lg = _nn(zb, wg_ref[...]) + bg_ref[...]
        la_ref[...] = (jnp.minimum(lg, 0.0) - jnp.log(1.0 + jnp.exp(-jnp.abs(lg)))) * (1.0 / GATE_TAU)

    vec = _full((1, D))
    tab = _rows(TE, 128)
    outs = [(D, _BF), (QP, _BF), (KP, _BF), (KP, _BF), (256, F32), (256, F32), (512, F32), (512, F32),
            (128, _BF), (512, F32)]
    return pl.pallas_call(
        body, name="inproj_fwd", grid=(E // TE,),
        in_specs=[_rows(TE, D), vec, vec, vec, vec, _full((D, NP)), _full((128, 512)), _full((1, 512)), tab, tab, tab],
        out_specs=[_rows(TE, w) for w, _ in outs],
        out_shape=[jax.ShapeDtypeStruct((E, w), dt) for w, dt in outs],
        compiler_params=_cp(("arbitrary",), 40 * 1024 * 1024),
    )(xe, gml, shl, gmc, shc, win, wg, bg, cos, sa, sb)


def _attn_specs(E):
    nb = (E - CTX) // WIN
    last = E // WIN - 1
    kc = pl.BlockSpec((CTX, KP), lambda n: (0, 0))
    kp = pl.BlockSpec((WIN, KP), lambda n: (n + 1, 0))
    kk = pl.BlockSpec((WIN, KP), lambda n: (n + 2, 0))
    kn = pl.BlockSpec((WIN, KP), lambda n: (jnp.minimum(n + 3, last), 0))
    return nb, [kc, kp, kk, kn]


def _attn_valid(n, nb):
    rows = lax.broadcasted_iota(jnp.int32, (GRP * WIN, CTX + 3 * WIN), 0) & (WIN - 1)
    cols = lax.broadcasted_iota(jnp.int32, (GRP * WIN, CTX + 3 * WIN), 1)
    j = cols - CTX
    rel = j - WIN - rows
    return (cols < CTX) | ((jnp.abs(rel) <= WIN) & ((j >= WIN) | (n > 0)) & ((j < 2 * WIN) | (n < nb - 1)))


def _attn_fwd(q, k, v, sink):
    E = q.shape[0]
    S = E - CTX
    nb, kspecs = _attn_specs(E)

    def body(q_ref, kc, kp, kk, kn, vc, vp, vk, vn, sink_ref, o_ref, lse_ref):
        n = pl.program_id(0)
        valid = _attn_valid(n, nb)
        lane = lax.broadcasted_iota(jnp.int32, (WIN, 128), 1)
        lse_t = jnp.zeros((WIN, 128), F32)
        for h in range(N_KV):
            hs = slice(128 * h, 128 * h + 128)
            K = jnp.concatenate([kc[:, hs], kp[:, hs], kk[:, hs], kn[:, hs]], axis=0)
            V = jnp.concatenate([vc[:, hs], vp[:, hs], vk[:, hs], vn[:, hs]], axis=0)
            Q = jnp.concatenate([q_ref[:, 128 * (GRP * h + g):128 * (GRP * h + g) + 128] for g in range(GRP)], axis=0)
            sk = jnp.concatenate([jnp.broadcast_to(sink_ref[GRP * h + g:GRP * h + g + 1, 0:1], (WIN, 1))
                                  for g in range(GRP)], axis=0)
            s = jnp.where(valid, _nt(Q, K), NEG)
            m = jnp.maximum(jnp.max(s, axis=1, keepdims=True), sk)
            e = jnp.exp(s - m)
            den = jnp.sum(e, axis=1, keepdims=True) + jnp.exp(sk - m)
            o = _nn((e / den).astype(_BF), V)
            lse = m + jnp.log(den)
            for g in range(GRP):
                lse_t = jnp.where(lane == GRP * h + g, lse[WIN * g:WIN * g + WIN], lse_t)
            for pp in range(GRP // 2):
                a = o[WIN * 2 * pp:WIN * 2 * pp + WIN]
                b = o[WIN * (2 * pp + 1):WIN * (2 * pp + 1) + WIN]
                t = 2 * h + pp
                o_ref[:, 128 * t:128 * t + 128] = (a + pltpu.roll(b, 64, 1)).astype(_BF)
        lse_ref[...] = lse_t

    qs = pl.BlockSpec((WIN, QP), lambda n: (n + 2, 0))
    return pl.pallas_call(
        body, name="attn_fwd", grid=(nb,),
        in_specs=[qs] + kspecs + kspecs + [_full((8, 128))],
        out_specs=[_rows(WIN, 512), _rows(WIN, 128)],
        out_shape=[jax.ShapeDtypeStruct((S, 512), _BF), jax.ShapeDtypeStruct((S, 128), F32)],
        compiler_params=_cp(("arbitrary",)),
    )(q, k, k, k, k, v, v, v, v, sink)


def _attn_bwd(q, k, v, sink, lse, d_attn):
    E = q.shape[0]
    S = E - CTX
    nb, kspecs = _attn_specs(E)
    last = E // WIN - 1

    def body(q_ref, kc, kp, kk, kn, vc, vp, vk, vn, sink_ref, lse_ref, do_ref, dq_ref, dk_ref, dv_ref, ds_ref):
        n = pl.program_id(0)

        @pl.when(n == 0)
        def _():
            dk_ref[...] = jnp.zeros_like(dk_ref)
            dv_ref[...] = jnp.zeros_like(dv_ref)
            ds_ref[...] = jnp.zeros_like(ds_ref)

        valid = _attn_valid(n, nb)
        lane = lax.broadcasted_iota(jnp.int32, (WIN, 128), 1)
        lse_t = lse_ref[...]
        starts = [None, pl.multiple_of((n + 1) * WIN, WIN), pl.multiple_of((n + 2) * WIN, WIN),
                  pl.multiple_of(jnp.minimum(n + 3, last) * WIN, WIN)]
        for h in range(N_KV):
            hs = slice(128 * h, 128 * h + 128)
            K = jnp.concatenate([kc[:, hs], kp[:, hs], kk[:, hs], kn[:, hs]], axis=0)
            V = jnp.concatenate([vc[:, hs], vp[:, hs], vk[:, hs], vn[:, hs]], axis=0)
            Q = jnp.concatenate([q_ref[:, 128 * (GRP * h + g):128 * (GRP * h + g) + 128] for g in range(GRP)], axis=0)
            sk = jnp.concatenate([jnp.broadcast_to(sink_ref[GRP * h + g:GRP * h + g + 1, 0:1], (WIN, 1))
                                  for g in range(GRP)], axis=0)
            ls = jnp.concatenate([jnp.sum(jnp.where(lane == GRP * h + g, lse_t, 0.0), axis=1, keepdims=True)
                                  for g in range(GRP)], axis=0)
            dos = []
            for g in range(GRP):
                j = GRP * h + g
                t = do_ref[:, 128 * (j // 2):128 * (j // 2) + 128].astype(F32)
                if j % 2:
                    t = pltpu.roll(t, 64, 1)
                dos.append(jnp.where(lane < HD, t, 0.0))
            do = jnp.concatenate(dos, axis=0).astype(_BF)
            s = _nt(Q, K)
            p = jnp.where(valid, jnp.exp(s - ls), 0.0)
            dp = _nt(do, V)
            delta = jnp.sum(p * dp, axis=1, keepdims=True)
            dsc = (p * (dp - delta)).astype(_BF)
            dq = _nn(dsc, K) * (HD ** -0.5)
            for g in range(GRP):
                j = GRP * h + g
                dq_ref[:, 128 * j:128 * j + 128] = dq[WIN * g:WIN * g + WIN].astype(_BF)
            dK = _tn(dsc, Q)
            dV = _tn(p.astype(_BF), do)
            dk_ref[0:CTX, hs] += dK[0:CTX]
            dv_ref[0:CTX, hs] += dV[0:CTX]
            for w in range(1, 4):
                lo = CTX + WIN * (w - 1)
                dk_ref[pl.ds(starts[w], WIN), hs] += dK[lo:lo + WIN]
                dv_ref[pl.ds(starts[w], WIN), hs] += dV[lo:lo + WIN]
            psk = -jnp.exp(sk - ls) * delta
            for g in range(GRP):
                j = GRP * h + g
                tot = jnp.sum(psk[WIN * g:WIN * g + WIN], axis=0, keepdims=True)
                ds_ref[j:j + 1, :] += jnp.broadcast_to(tot, (1, 128))

    qs = pl.BlockSpec((WIN, QP), lambda n: (n + 2, 0))
    return pl.pallas_call(
        body, name="attn_bwd", grid=(nb,),
        in_specs=[qs] + kspecs + kspecs + [_full((8, 128)), _rows(WIN, 128), _rows(WIN, 512)],
        out_specs=[_rows(WIN, QP), _full((E, KP)), _full((E, KP)), _full((8, 128))],
        out_shape=[jax.ShapeDtypeStruct((S, QP), _BF), jax.ShapeDtypeStruct((E, KP), F32),
                   jax.ShapeDtypeStruct((E, KP), F32), jax.ShapeDtypeStruct((8, 128), F32)],
        compiler_params=_cp(("arbitrary",), 48 * 1024 * 1024),
    )(q, k, k, k, k, v, v, v, v, sink, lse, d_attn)


def _gla_order(E, reverse, backward):
    nc = CTX // GLA_T
    n = E // GLA_T
    if not reverse:
        fwd = lambda s: s
    else:
        fwd = lambda s: jnp.where(s < nc, nc - 1 - s, n - 1 + nc - s)
    if backward:
        return lambda s: fwd(n - 1 - s)
    return fwd


def _gla_masks():
    T = GLA_T
    l256 = lax.broadcasted_iota(jnp.int32, (1, 256), 1)
    l512 = lax.broadcasted_iota(jnp.int32, (1, 512), 1)
    hm32 = [((l256 >> 5) == h).astype(F32) for h in range(N_GLA)]
    hm64 = [((l512 >> 6) == h).astype(F32) for h in range(N_GLA)]
    bd = ((lax.broadcasted_iota(jnp.int32, (512, 256), 0) >> 6)
          == (lax.broadcasted_iota(jnp.int32, (512, 256), 1) >> 5)).astype(F32)
    ri = lax.broadcasted_iota(jnp.int32, (T, T), 0)
    ci = lax.broadcasted_iota(jnp.int32, (T, T), 1)
    return hm32, hm64, bd, ri, ci


def _gla_decays(la, reverse, ri, ci):
    T = GLA_T
    msk = (ri <= ci) if reverse else (ri >= ci)
    b = _nn_hi(msk.astype(F32), la)
    bT = b[0:1] if reverse else b[T - 1:T]
    bm = b[T // 2:T // 2 + 1]
    return msk, b, bT, bm


def _gla_fwd(gq, gk, gv, la, reverse):
    E = gq.shape[0]
    T = GLA_T
    n = E // T
    order = _gla_order(E, reverse, False)
    col = 1 if reverse else 0

    def body(gq_ref, gk_ref, gv_ref, la_ref, o_ref, st_ref, S_scr):
        @pl.when(pl.program_id(0) == 0)
        def _():
            S_scr[...] = jnp.zeros_like(S_scr)

        hm32, hm64, bd, ri, ci = _gla_masks()
        msk, b, bT, bm = _gla_decays(la_ref[...], reverse, ri, ci)
        q, k, v = gq_ref[...], gk_ref[...], gv_ref[...]
        qd = (q * jnp.exp(b)).astype(_BF)
        qm = q * jnp.exp(b - bm)
        km = (k * jnp.exp(bm - b)).astype(_BF)
        kd = (k * jnp.exp(bT - b)).astype(_BF)
        ST = S_scr[...]
        comp = ST[0:DV]
        for h in range(1, N_GLA):
            comp = comp + ST[DV * h:DV * h + DV]
        st_ref[0] = comp
        o = _nt(qd, ST.astype(_BF))
        for h in range(N_GLA):
            A = jnp.where(msk, _nt((qm * hm32[h]).astype(_BF), km), 0.0).astype(_BF)
            o = o + _nn(A, (v * hm64[h]).astype(_BF))
        o_ref[...] = o
        S_scr[...] = ST * jnp.exp(bT) + bd * _tn(v.astype(_BF), kd)

    blk = lambda w, c=0: pl.BlockSpec((T, w), lambda s: (order(s), c))
    return pl.pallas_call(
        body, name="gla_fwd_rev" if reverse else "gla_fwd", grid=(n,),
        in_specs=[blk(256), blk(256), blk(512), blk(256, col)],
        out_specs=[blk(512), pl.BlockSpec((1, DV, 256), lambda s: (order(s), 0, 0))],
        out_shape=[jax.ShapeDtypeStruct((E, 512), F32), jax.ShapeDtypeStruct((n, DV, 256), F32)],
        scratch_shapes=[pltpu.VMEM((512, 256), F32)],
        compiler_params=_cp(("arbitrary",)),
    )(gq, gk, gv, la)


def _gla_bwd(gq, gk, gv, la, st, do, reverse):
    E = gq.shape[0]
    T = GLA_T
    n = E // T
    nc = CTX // T
    order = _gla_order(E, reverse, True)
    col = 1 if reverse else 0

    def body(gq_ref, gk_ref, gv_ref, la_ref, st_ref, do_ref, dq_ref, dk_ref, dv_ref, dla_ref, dS_scr):
        @pl.when(pl.program_id(0) == 0)
        def _():
            dS_scr[...] = jnp.zeros_like(dS_scr)

        is_lat = order(pl.program_id(0)) >= nc
        hm32, hm64, bd, ri, ci = _gla_masks()
        msk, b, bT, bm = _gla_decays(la_ref[...], reverse, ri, ci)
        mskT = (ri >= ci) if reverse else (ri <= ci)
        q, k, v = gq_ref[...], gk_ref[...], gv_ref[...]
        do = jnp.where(is_lat, do_ref[...].astype(F32), 0.0)
        e_b, e_qm, e_km, e_kd, e_T = jnp.exp(b), jnp.exp(b - bm), jnp.exp(bm - b), jnp.exp(bT - b), jnp.exp(bT)
        qd, qm, km, kd = q * e_b, q * e_qm, k * e_km, k * e_kd
        qdb, qmb, kmb, kdb, vb, dob = (t.astype(_BF) for t in (qd, qm, km, kd, v, do))
        ST = jnp.tile(st_ref[0], (N_GLA, 1)) * bd
        dST = dS_scr[...]
        dSTb = dST.astype(_BF)
        dqd = _nn(dob, ST.astype(_BF))
        dv = _nt(kdb, dSTb)
        dkd = _nn(vb, dSTb)
        dqm = jnp.zeros((T, 256), F32)
        dkm = jnp.zeros((T, 256), F32)
        for h in range(N_GLA):
            qh = (qm * hm32[h]).astype(_BF)
            kh = (km * hm32[h]).astype(_BF)
            doh = (do * hm64[h]).astype(_BF)
            vh = (v * hm64[h]).astype(_BF)
            AT = jnp.where(mskT, _nt(kh, qmb), 0.0).astype(_BF)
            dA = jnp.where(msk, _nt(doh, vb), 0.0).astype(_BF)
            dAT = jnp.where(mskT, _nt(vh, dob), 0.0).astype(_BF)
            dv = dv + _nn(AT, doh)
            dqm = dqm + _nn(dA, kh)
            dkm = dkm + _nn(dAT, qh)
        dq_ref[...] = dqm * e_qm + dqd * e_b
        dk_ref[...] = dkm * e_km + dkd * e_kd
        dv_ref[...] = dv
        db = dqm * qm - dkm * km + dqd * qd - dkd * kd
        dbT = jnp.sum(dkd * kd, axis=0, keepdims=True) + e_T * jnp.sum(dST * ST, axis=0, keepdims=True)
        dla_ref[...] = _nn_hi(mskT.astype(F32), db) + dbT
        dS_scr[...] = dST * e_T + bd * _tn(dob, qdb)

    blk = lambda w, c=0: pl.BlockSpec((T, w), lambda s: (order(s), c))
    do_spec = pl.BlockSpec((T, 512), lambda s: (jnp.maximum(order(s) - nc, 0), 0))
    return pl.pallas_call(
        body, name="gla_bwd_rev" if reverse else "gla_bwd", grid=(n,),
        in_specs=[blk(256), blk(256), blk(512), blk(256, col),
                  pl.BlockSpec((1, DV, 256), lambda s: (order(s), 0, 0)), do_spec],
        out_specs=[blk(256), blk(256), blk(512), blk(256)],
        out_shape=[jax.ShapeDtypeStruct((E, 256), F32), jax.ShapeDtypeStruct((E, 256), F32),
                   jax.ShapeDtypeStruct((E, 512), F32), jax.ShapeDtypeStruct((E, 256), F32)],
        scratch_shapes=[pltpu.VMEM((512, 256), F32)],
        compiler_params=_cp(("arbitrary",)),
    )(gq, gk, gv, la, st, do)


def _gla_out(o_f, o_b, gg, ggla, mavg):
    o = o_f + o_b
    rr = lax.rsqrt(_nn_hi(o * o, mavg) + EPS)
    oh = o * rr
    sg = _sigmoid(gg)
    return oh, rr, sg


def _mix_fwd(x, attn, o_f, o_b, gg, ggla, mavg, wout, gt1, g2):
    S = x.shape[0]
    TM = 256

    def body(x_ref, a_ref, of_ref, ob_ref, gg_ref, ggla_ref, mavg_ref, w_ref, gt1_ref, g2_ref, x1_ref, mix_ref):
        gg_t = gg_ref[...]
        oh, _, sg = _gla_out(of_ref[...], ob_ref[...], gg_t, ggla_ref[...], mavg_ref[...])
        mix_ref[:, 0:512] = a_ref[...]
        mix_ref[:, 512:1024] = (oh * ggla_ref[...] * (gg_t * sg)).astype(_BF)
        y = _nn(mix_ref[...], w_ref[...])
        ry = lax.rsqrt(jnp.mean(y * y, axis=-1, keepdims=True) + EPS)
        x1_ref[...] = x_ref[...] + gt1_ref[...] * ((y * ry) * g2_ref[...])

    return pl.pallas_call(
        body, name="mix_fwd", grid=(S // TM,),
        in_specs=[_rows(TM, D), _rows(TM, 512), _rows(TM, 512, 1), _rows(TM, 512, 1), _rows(TM, 512, 1),
                  _full((1, 512)), _full((512, 512)), _full((D, D)), _full((1, D)), _full((1, D))],
        out_specs=[_rows(TM, D), _rows(TM, D)],
        out_shape=[jax.ShapeDtypeStruct((S, D), F32), jax.ShapeDtypeStruct((S, D), _BF)],
        compiler_params=_cp(("arbitrary",), 40 * 1024 * 1024),
    )(x, attn, o_f, o_b, gg, ggla, mavg, wout, gt1, g2)


def _mix_bwd(dx1, mix, o_f, o_b, gg, ggla, mavg, wout, gt1, g2):
    S = dx1.shape[0]
    TM = 256

    def body(dx_ref, mix_ref, of_ref, ob_ref, gg_ref, ggla_ref, mavg_ref, w_ref, gt1_ref, g2_ref,
             da_ref, do_ref, dgg_ref, dy_ref, sums_ref):
        @pl.when(pl.program_id(0) == 0)
        def _():
            sums_ref[...] = jnp.zeros_like(sums_ref)

        dx = dx_ref[...]
        y = _nn(mix_ref[...], w_ref[...])
        ry = lax.rsqrt(jnp.mean(y * y, axis=-1, keepdims=True) + EPS)
        yh = y * ry
        sums_ref[0:1, :] += jnp.sum(dx * yh, axis=0, keepdims=True)
        dyh = dx * (gt1_ref[...] * g2_ref[...])
        dy = (ry * (dyh - yh * jnp.mean(dyh * yh, axis=-1, keepdims=True))).astype(_BF)
        dy_ref[...] = dy
        dmix = _nt(dy, w_ref[...])
        da_ref[...] = dmix[:, 0:512].astype(_BF)
        dgla = dmix[:, 512:1024]
        gg_t = gg_ref[...]
        ggla_t = ggla_ref[...]
        oh, rr, sg = _gla_out(of_ref[...], ob_ref[...], gg_t, ggla_t, mavg_ref[...])
        dgg_ref[...] = (dgla * oh * ggla_t * (sg * (1.0 + gg_t * (1.0 - sg)))).astype(_BF)
        don = dgla * (gg_t * sg)
        sums_ref[1:2, 0:512] += jnp.sum(don * oh, axis=0, keepdims=True)
        doh = don * ggla_t
        do_ref[...] = (rr * (doh - oh * _nn_hi(doh * oh, mavg_ref[...]))).astype(_BF)

    return pl.pallas_call(
        body, name="mix_bwd", grid=(S // TM,),
        in_specs=[_rows(TM, D), _rows(TM, D), _rows(TM, 512, 1), _rows(TM, 512, 1), _rows(TM, 512, 1),
                  _full((1, 512)), _full((512, 512)), _full((D, D)), _full((1, D)), _full((1, D))],
        out_specs=[_rows(TM, 512), _rows(TM, 512), _rows(TM, 512), _rows(TM, D), _full((8, D))],
        out_shape=[jax.ShapeDtypeStruct((S, 512), _BF), jax.ShapeDtypeStruct((S, 512), _BF),
                   jax.ShapeDtypeStruct((S, 512), _BF), jax.ShapeDtypeStruct((S, D), _BF),
                   jax.ShapeDtypeStruct((8, D), F32)],
        compiler_params=_cp(("arbitrary",), 40 * 1024 * 1024),
    )(dx1, mix, o_f, o_b, gg, ggla, mavg, wout, gt1, g2)


def _ffn(x1, target, gm2, sh2, gt2, g4, wffi, wffo):
    S = x1.shape[0]
    TF = 256

    def body(x_ref, t_ref, gm_ref, sh_ref, gt_ref, g4_ref, wi_hbm, wo_hbm,
             dx_ref, h_ref, du_ref, act_ref, df_ref, sums_ref, loss_ref, wi, wo, sem):
        @pl.when(pl.program_id(0) == 0)
        def _():
            c1 = pltpu.make_async_copy(wi_hbm, wi, sem.at[0])
            c2 = pltpu.make_async_copy(wo_hbm, wo, sem.at[1])
            c1.start()
            c2.start()
            sums_ref[...] = jnp.zeros_like(sums_ref)
            loss_ref[...] = jnp.zeros_like(loss_ref)
            c1.wait()
            c2.wait()

        x = x_ref[...]
        gm = gm_ref[...]
        r = lax.rsqrt(jnp.mean(x * x, axis=-1, keepdims=True) + EPS)
        xh = x * r
        hb = (xh * gm + sh_ref[...]).astype(_BF)
        h_ref[...] = hb
        u = _nn(hb, wi[...])
        g = u[:, 0:FFN]
        up = u[:, FFN:2 * FFN]
        sg = _sigmoid(g)
        sl = g * sg
        ab = (sl * up).astype(_BF)
        act_ref[...] = ab
        f = _nn(ab, wo[...])
        rf = lax.rsqrt(jnp.mean(f * f, axis=-1, keepdims=True) + EPS)
        fh = f * rf
        gt, g4v = gt_ref[...], g4_ref[...]
        err = x + gt * (fh * g4v) - t_ref[...]
        loss_ref[...] += jnp.sum(err * err) * (0.5 / D)
        dout = err * (1.0 / D)
        sums_ref[2:3, :] += jnp.sum(dout * fh, axis=0, keepdims=True)
        dfh = dout * (gt * g4v)
        dfb = (rf * (dfh - fh * jnp.mean(dfh * fh, axis=-1, keepdims=True))).astype(_BF)
        df_ref[...] = dfb
        dact = _nt(dfb, wo[...])
        du_ref[:, 0:FFN] = (dact * up * (sg * (1.0 + g * (1.0 - sg)))).astype(_BF)
        du_ref[:, FFN:2 * FFN] = (dact * sl).astype(_BF)
        dh = _nt(du_ref[...], wi[...])
        sums_ref[0:1, :] += jnp.sum(dh, axis=0, keepdims=True)
        sums_ref[1:2, :] += jnp.sum(dh * xh, axis=0, keepdims=True)
        dxh = dh * gm
        dx_ref[...] = dout + r * (dxh - xh * jnp.mean(dxh * xh, axis=-1, keepdims=True))

    vec = _full((1, D))
    anyspec = pl.BlockSpec(memory_space=pl.ANY)
    return pl.pallas_call(
        body, name="ffn_fwd_bwd", grid=(S // TF,),
        in_specs=[_rows(TF, D), _rows(TF, D), vec, vec, vec, vec, anyspec, anyspec],
        out_specs=[_rows(TF, D), _rows(TF, D), _rows(TF, 2 * FFN), _rows(TF, FFN), _rows(TF, D),
                   _full((8, D)), _full((8, 128))],
        out_shape=[jax.ShapeDtypeStruct((S, D), F32), jax.ShapeDtypeStruct((S, D), _BF),
                   jax.ShapeDtypeStruct((S, 2 * FFN), _BF), jax.ShapeDtypeStruct((S, FFN), _BF),
                   jax.ShapeDtypeStruct((S, D), _BF), jax.ShapeDtypeStruct((8, D), F32),
                   jax.ShapeDtypeStruct((8, 128), F32)],
        scratch_shapes=[pltpu.VMEM((D, 2 * FFN), _BF), pltpu.VMEM((FFN, D), _BF), pltpu.SemaphoreType.DMA((2,))],
        compiler_params=_cp(("arbitrary",), VMEM_BIG),
    )(x1, target, gm2, sh2, gt2, g4, wffi, wffo)


def _inproj_bwd(xe, gml, gmc, win, wg, cos, sa, sb, la, dq, dk, dv, dgq_f, dgq_b, dgk_f, dgk_b, dgv_f, dgv_b,
                dgg, dla_f, dla_b, dx1):
    E = xe.shape[0]
    S = E - CTX
    TE = 256

    def body(x_ref, gml_ref, gmc_ref, w_ref, wg_ref, cos_ref, sa_ref, sb_ref, la_ref, dq_ref, dk_ref, dv_ref,
             gqf, gqb, gkf, gkb, gvf, gvb, dgg_ref, dlf, dlb, dx1_ref,
             dp_ref, dlg_ref, gx_ref, sums_ref, bsum_ref):
        i = pl.program_id(0)
        is_ctx = i == 0

        @pl.when(is_ctx)
        def _():
            sums_ref[...] = jnp.zeros_like(sums_ref)
            bsum_ref[...] = jnp.zeros_like(bsum_ref)

        lat = jnp.where(is_ctx, 0.0, 1.0)
        cos_t, sa_t, sb_t = cos_ref[...], sa_ref[...], sb_ref[...]
        dp_ref[:, O_Q:O_K] = (_unrope(dq_ref[...].astype(F32), cos_t, sa_t, sb_t) * lat).astype(_BF)
        dp_ref[:, O_K:O_V] = _unrope(dk_ref[...], cos_t, sa_t, sb_t).astype(_BF)
        dp_ref[:, O_V:O_GQ] = dv_ref[...].astype(_BF)
        dp_ref[:, O_GQ:O_GK] = ((gqf[...] + gqb[...]) * (DK ** -0.5)).astype(_BF)
        dp_ref[:, O_GK:O_GV] = (gkf[...] + gkb[...]).astype(_BF)
        dp_ref[:, O_GV:O_GG] = (gvf[...] + gvb[...]).astype(_BF)
        dp_ref[:, O_GG:O_Z] = (dgg_ref[...].astype(F32) * lat).astype(_BF)
        la_t = la_ref[...]
        dlg = (jnp.concatenate([dlf[...], dlb[...]], axis=1) * (1.0 - jnp.exp(GATE_TAU * la_t)) * (1.0 / GATE_TAU))
        bsum_ref[0:1, :] += jnp.sum(dlg, axis=0, keepdims=True)
        dlgb = dlg.astype(_BF)
        dlg_ref[...] = dlgb
        dp_ref[:, O_Z:NP] = _nt(dlgb, wg_ref[...]).astype(_BF)
        dh = _nt(dp_ref[...], w_ref[...])
        x = x_ref[...]
        r = lax.rsqrt(jnp.mean(x * x, axis=-1, keepdims=True) + EPS)
        xh = x * r
        sdh = jnp.sum(dh, axis=0, keepdims=True)
        sdx = jnp.sum(dh * xh, axis=0, keepdims=True)
        sums_ref[0:1, :] += sdh * lat
        sums_ref[1:2, :] += sdx * lat
        sums_ref[2:3, :] += sdh * (1.0 - lat)
        sums_ref[3:4, :] += sdx * (1.0 - lat)
        dxh = dh * jnp.where(is_ctx, gmc_ref[...], gml_ref[...])
        gx_ref[...] = dx1_ref[...] + r * (dxh - xh * jnp.mean(dxh * xh, axis=-1, keepdims=True))

    vec = _full((1, D))
    tab = _rows(TE, 128)
    return pl.pallas_call(
        body, name="inproj_bwd", grid=(E // TE,),
        in_specs=[_rows(TE, D), vec, vec, _full((D, NP)), _full((128, 512)), tab, tab, tab, _rows(TE, 512),
                  _rows_lat(TE, QP), _rows(TE, KP), _rows(TE, KP),
                  _rows(TE, 256), _rows(TE, 256), _rows(TE, 256), _rows(TE, 256), _rows(TE, 512), _rows(TE, 512),
                  _rows_lat(TE, 512), _rows(TE, 256), _rows(TE, 256), _rows_lat(TE, D)],
        out_specs=[_rows(TE, NP), _rows(TE, 512), _rows_lat(TE, D), _full((8, D)), _full((8, 512))],
        out_shape=[jax.ShapeDtypeStruct((E, NP), _BF), jax.ShapeDtypeStruct((E, 512), _BF),
                   jax.ShapeDtypeStruct((S, D), F32), jax.ShapeDtypeStruct((8, D), F32),
                   jax.ShapeDtypeStruct((8, 512), F32)],
        compiler_params=_cp(("arbitrary",), VMEM_BIG),
    )(xe, gml, gmc, win, wg, cos, sa, sb, la, dq, dk, dv, dgq_f, dgq_b, dgk_f, dgk_b, dgv_f, dgv_b,
      dgg, dla_f, dla_b, dx1)


def _matmul_tn(a, b, tk, tn, tt, name):
    T, KA = a.shape
    N = b.shape[1]

    def body(a_ref, b_ref, o_ref):
        @pl.when(pl.program_id(2) == 0)
        def _():
            o_ref[...] = jnp.zeros_like(o_ref)

        o_ref[...] += _tn(a_ref[...], b_ref[...])

    return pl.pallas_call(
        body, name=name, grid=(KA // tk, N // tn, T // tt),
        in_specs=[pl.BlockSpec((tt, tk), lambda i, j, t: (t, i)), pl.BlockSpec((tt, tn), lambda i, j, t: (t, j))],
        out_specs=pl.BlockSpec((tk, tn), lambda i, j, t: (i, j)),
        out_shape=jax.ShapeDtypeStruct((KA, N), F32),
        compiler_params=_cp(("parallel", "parallel", "arbitrary"), 40 * 1024 * 1024),
    )(a, b)


def _ada_fwd(c_all, c_ctx, w_ada):
    n = w_ada.shape[1]

    def body(c_ref, cc_ref, w_ref, o_ref):
        c = jnp.concatenate([c_ref[...], jnp.broadcast_to(cc_ref[...], (8, D))], axis=0)
        o_ref[...] = _nn((c * _sigmoid(c)).astype(_BF), w_ref[...].astype(_BF))

    return pl.pallas_call(
        body, name="ada_fwd", in_specs=[_full((8, D)), _full((1, D)), _full((D, n))], out_specs=_full((16, n)),
        out_shape=jax.ShapeDtypeStruct((16, n), F32), grid=(1,), compiler_params=_cp(("arbitrary",)),
    )(c_all, c_ctx, w_ada)


def _ada_bwd(c_all, c_ctx, w_ada, d_all):
    n = w_ada.shape[1]

    def body(c_ref, cc_ref, w_ref, d_ref, gw_ref, t_ref):
        c = jnp.concatenate([c_ref[...], jnp.broadcast_to(cc_ref[...], (8, D))], axis=0)
        db = d_ref[...].astype(_BF)
        gw_ref[...] = _tn((c * _sigmoid(c)).astype(_BF), db)
        t_ref[...] = _nt(db[8:16], w_ref[...].astype(_BF))

    return pl.pallas_call(
        body, name="ada_bwd", in_specs=[_full((8, D)), _full((1, D)), _full((D, n)), _full((16, n))],
        out_specs=[_full((D, n)), _full((8, D))],
        out_shape=[jax.ShapeDtypeStruct((D, n), F32), jax.ShapeDtypeStruct((8, D), F32)], grid=(1,),
        compiler_params=_cp(("arbitrary",)),
    )(c_all, c_ctx, w_ada, d_all)


def _small_grads(s_in, s_ffn, s_mix, ada_l, ada_c, gains):
    def body(si, sf, sm, al, ac, g, o_ref):
        sdh_l, sdx_l, sdh_c, sdx_c = si[0:1], si[1:2], si[2:3], si[3:4]
        sdh2, sdx2, a2 = sf[0:1], sf[1:2], sf[2:3]
        a1 = sm[0:1]
        g1, g2, g3, g4 = g[0:1], g[1:2], g[2:3], g[3:4]
        sc1, gt1, sc2, gt2 = al[1:2], al[2:3], al[4:5], al[5:6]
        sc1c = ac[1:2]
        z = jnp.zeros((1, D), F32)
        rows = [sdh_l, sdx_l * g1, a1 * g2, sdh2, sdx2 * g3, a2 * g4,
                sdh_c, sdx_c * g1, z, z, z, z,
                sdx_l * (1.0 + sc1) + sdx_c * (1.0 + sc1c), a1 * gt1, sdx2 * (1.0 + sc2), a2 * gt2]
        for r, v in enumerate(rows):
            o_ref[r:r + 1, :] = v

    v8 = _full((8, D))
    return pl.pallas_call(
        body, name="small_grads", in_specs=[v8] * 6, out_specs=_full((16, D)),
        out_shape=jax.ShapeDtypeStruct((16, D), F32), grid=(1,), compiler_params=_cp(("arbitrary",)),
    )(s_in, s_ffn, s_mix, ada_l, ada_c, gains)


def _adamw(w, g, m, v, name):
    R, C = w.shape
    tr = R
    for cand in (256, 128, 64, 32, 16, 8):
        if R % cand == 0 and R > cand:
            tr = cand
            break
    c1 = 1.0 / (1.0 - ADAM_B1 ** ADAM_STEP)
    c2 = 1.0 / (1.0 - ADAM_B2 ** ADAM_STEP)

    def body(w_ref, g_ref, m_ref, v_ref, d_ref, nm_ref, nv_ref):
        gg = g_ref[...]
        nm = ADAM_B1 * m_ref[...] + (1.0 - ADAM_B1) * gg
        nv = ADAM_B2 * v_ref[...] + (1.0 - ADAM_B2) * (gg * gg)
        nm_ref[...] = nm
        nv_ref[...] = nv
        d_ref[...] = -ADAM_LR * ((nm * c1) / (jnp.sqrt(nv * c2) + ADAM_EPS) + ADAM_WD * w_ref[...])

    spec = _rows(tr, C)
    sds = jax.ShapeDtypeStruct((R, C), F32)
    return pl.pallas_call(
        body, name=name, grid=(R // tr,), in_specs=[spec] * 4, out_specs=[spec] * 3, out_shape=[sds] * 3,
        compiler_params=_cp(("parallel",)),
    )(w, g, m, v)


def _sum_slots(slots, name):
    _, R, C = slots.shape
    tr = R
    for cand in (256, 128, 64, 32, 16, 8):
        if R % cand == 0 and R > cand:
            tr = cand
            break

    def body(s_ref, o_ref):
        acc = s_ref[0]
        for j in range(1, N_DEV):
            acc = acc + s_ref[j]
        o_ref[...] = acc

    return pl.pallas_call(
        body, name=name, grid=(R // tr,), in_specs=[pl.BlockSpec((N_DEV, tr, C), lambda i: (0, i, 0))],
        out_specs=_rows(tr, C), out_shape=jax.ShapeDtypeStruct((R, C), F32), compiler_params=_cp(("parallel",)),
    )(slots)


def _allgather(x_shard, name):
    m_per, n = x_shard.shape

    def body(x_ref, out_ref, send_sems, recv_sems, local_sem):
        x, y, c = lax.axis_index("x"), lax.axis_index("y"), lax.axis_index("c")
        me, sibling = (x, y, c), (x, y, 1 - c)
        chips = [(1 - x, y), (x, 1 - y), (1 - x, 1 - y)]

        def rows(px, py, pc):
            return out_ref.at[pl.ds((4 * px + 2 * py + pc) * m_per, m_per), :]

        def copy(k, block, to, src=None):
            return pltpu.make_async_remote_copy(
                src_ref=rows(*block) if src is None else src, dst_ref=rows(*block),
                send_sem=send_sems.at[k], recv_sem=recv_sems.at[k], device_id=to, device_id_type=MESH)

        mine = pltpu.make_async_copy(x_ref, rows(*me), local_sem)
        mine.start()
        first = [copy(0, me, sibling, src=x_ref)]
        first += [copy(1 + j, me, (*chip, c), src=x_ref) for j, chip in enumerate(chips)]
        for cp in first:
            cp.start()
        passed = [copy(4 + j, (*chip, c), sibling) for j, chip in enumerate(chips)]
        for j, chip in enumerate(chips):
            copy(1 + j, (*chip, c), me).wait_recv()
            passed[j].start()
        copy(0, sibling, me).wait_recv()
        for j, chip in enumerate(chips):
            copy(4 + j, (*chip, 1 - c), me).wait_recv()
        for cp in first + passed:
            cp.wait_send()
        mine.wait()

    return pl.pallas_call(
        body, name=name, out_shape=jax.ShapeDtypeStruct((N_DEV * m_per, n), x_shard.dtype),
        in_specs=[pl.BlockSpec(memory_space=pltpu.VMEM)], out_specs=pl.BlockSpec(memory_space=pltpu.VMEM),
        scratch_shapes=[pltpu.SemaphoreType.DMA((7,)), pltpu.SemaphoreType.DMA((7,)), pltpu.SemaphoreType.DMA],
        compiler_params=pltpu.CompilerParams(vmem_limit_bytes=VMEM_BIG),
    )(x_shard)


def _alltoall(slabs, name):
    na = len(slabs)

    def body(*refs):
        ins, outs = refs[:na], refs[na:2 * na]
        send_sems, recv_sems, local_sems = refs[2 * na:]
        x, y, c = lax.axis_index("x"), lax.axis_index("y"), lax.axis_index("c")
        me = 4 * x + 2 * y + c
        copies = []
        for a in range(na):
            loc = pltpu.make_async_copy(ins[a].at[me], outs[a].at[me], local_sems.at[a])
            loc.start()
            copies.append(loc)
        rem = []
        for k in range(1, N_DEV):
            px, py, pc = x ^ (k >> 2), y ^ ((k >> 1) & 1), c ^ (k & 1)
            peer = 4 * px + 2 * py + pc
            for a in range(na):
                cp = pltpu.make_async_remote_copy(
                    src_ref=ins[a].at[peer], dst_ref=outs[a].at[me],
                    send_sem=send_sems.at[a, k - 1], recv_sem=recv_sems.at[a, k - 1],
                    device_id=(px, py, pc), device_id_type=MESH)
                cp.start()
                rem.append((a, k, peer, cp))
        for a, k, peer, cp in rem:
            pltpu.make_async_remote_copy(
                src_ref=ins[a].at[me], dst_ref=outs[a].at[peer],
                send_sem=send_sems.at[a, k - 1], recv_sem=recv_sems.at[a, k - 1],
                device_id=(x, y, c), device_id_type=MESH).wait_recv()
        for a, k, peer, cp in rem:
            cp.wait_send()
        for loc in copies:
            loc.wait()

    anyspec = pl.BlockSpec(memory_space=pl.ANY)
    return pl.pallas_call(
        body, name=name, out_shape=[jax.ShapeDtypeStruct(s.shape, s.dtype) for s in slabs],
        in_specs=[anyspec] * na, out_specs=[anyspec] * na,
        scratch_shapes=[pltpu.SemaphoreType.DMA((na, 7)), pltpu.SemaphoreType.DMA((na, 7)),
                        pltpu.SemaphoreType.DMA((na,))],
    )(*slabs)


def _rope_tables(S):
    t = np.arange(S)
    row = (t // GRID_W).astype(np.float32)
    colp = (t % GRID_W).astype(np.float32)
    half = HD // 2
    inv = (ROPE_BASE ** (-np.arange(0, half, 2, dtype=np.float32) / half)).astype(np.float32)
    ar = row[:, None] * inv[None, :]
    ac = colp[:, None] * inv[None, :]
    ang = np.concatenate([ar, ar, ac, ac], axis=-1).astype(np.float32)
    cos = np.cos(ang).astype(np.float32)
    sin = np.sin(ang).astype(np.float32)
    lane = np.arange(HD)
    first = (lane % 32) < 16
    sa = np.where(first[None, :], -sin, 0.0)
    sb = np.where(first[None, :], 0.0, sin)

    def ext(tab, ctx_val):
        full = np.zeros((CTX + S, 128), np.float32)
        full[:CTX, :HD] = ctx_val
        full[CTX:, :HD] = tab
        return jnp.asarray(full)

    return ext(cos, 1.0), ext(sa, 0.0), ext(sb, 0.0)


def _pad_cols_win(w):
    q, k, v, rest, z = w[:, 0:512], w[:, 512:640], w[:, 640:768], w[:, 768:2304], w[:, 2304:2336]

    def padh(t, nh):
        t = t.reshape(D, nh, HD)
        return jnp.pad(t, ((0, 0), (0, 0), (0, 128 - HD))).reshape(D, nh * 128)

    return jnp.concatenate([padh(q, N_ATT), padh(k, N_KV), padh(v, N_KV), rest, jnp.pad(z, ((0, 0), (0, 96)))], axis=1)


def _unpad_cols_win(g):
    def unp(t, nh):
        return t.reshape(D, nh, 128)[:, :, :HD].reshape(D, nh * HD)

    return jnp.concatenate([unp(g[:, O_Q:O_K], N_ATT), unp(g[:, O_K:O_V], N_KV), unp(g[:, O_V:O_GQ], N_KV),
                            g[:, O_GQ:O_Z], g[:, O_Z:O_Z + 32]], axis=1)


def _local_step(x, ctx, target, ada_l, ada_c, gains, sink, win_p, wg_bd, bg, ggla, wout, wffi, wffo):
    S = x.shape[0]
    cos, sa, sb = _rope_tables(S)
    xe = jnp.concatenate([ctx, x], axis=0)
    g1, g2, g3, g4 = (gains[i:i + 1] for i in range(4))
    sh1, sc1, gt1, sh2, sc2, gt2 = (ada_l[i:i + 1] for i in range(6))
    sh1c, sc1c = ada_c[0:1], ada_c[1:2]
    gml, gmc, gm2 = g1 * (1.0 + sc1), g1 * (1.0 + sc1c), g3 * (1.0 + sc2)
    mavg = jnp.asarray(np.kron(np.eye(N_GLA, dtype=np.float32), np.full((DV, DV), 1.0 / DV, np.float32)))

    h, q, k, v, gq, gk, gv, gg, z, la = _inproj_fwd(xe, gml, sh1, gmc, sh1c, win_p, wg_bd, bg, cos, sa, sb)
    attn, lse = _attn_fwd(q, k, v, sink)
    o_f, st_f = _gla_fwd(gq, gk, gv, la, False)
    o_b, st_b = _gla_fwd(gq, gk, gv, la, True)
    x1, mix = _mix_fwd(x, attn, o_f, o_b, gg, ggla, mavg, wout, gt1, g2)
    dx1, h2, du, act, df, s_ffn, loss = _ffn(x1, target, gm2, sh2, gt2, g4, wffi, wffo)
    d_attn, do_gla, dgg, dy, s_mix = _mix_bwd(dx1, mix, o_f, o_b, gg, ggla, mavg, wout, gt1, g2)
    dq, dk, dv, dsink = _attn_bwd(q, k, v, sink, lse, d_attn)
    dgq_f, dgk_f, dgv_f, dla_f = _gla_bwd(gq, gk, gv, la, st_f, do_gla, False)
    dgq_b, dgk_b, dgv_b, dla_b = _gla_bwd(gq, gk, gv, la, st_b, do_gla, True)
    dp, dlg, grad_x, s_in, s_bg = _inproj_bwd(xe, gml, gmc, win_p, wg_bd, cos, sa, sb, la, dq, dk, dv,
                                              dgq_f, dgq_b, dgk_f, dgk_b, dgv_f, dgv_b, dgg, dla_f, dla_b, dx1)
    tt_e = 256
    tt_s = 512 if S % 512 == 0 else 256
    g_win = _matmul_tn(h, dp, 512, 640, tt_e, "grad_w_in")
    g_wg = _matmul_tn(z, dlg, 128, 512, tt_e, "grad_w_gate")
    g_wout = _matmul_tn(mix, dy, 512, 512, tt_s, "grad_w_out")
    g_wffi = _matmul_tn(h2, du, 512, 1408, tt_s, "grad_w_ffn_in")
    g_wffo = _matmul_tn(act, df, 256, 1024, tt_s, "grad_w_ffn_out")
    small = _small_grads(s_in, s_ffn, s_mix, ada_l, ada_c, gains)
    return dict(loss=loss[0, 0], grad_x=grad_x, g_win=g_win, g_wg=g_wg, g_wout=g_wout, g_wffi=g_wffi,
                g_wffo=g_wffo, small=small, dsink=dsink[:, 0], dbg=s_bg[0], dggla=s_mix[1, 0:512])


SMALL_ROWS = 32


def _pack_small(c_ctx, b_ada, g1, g2, g3, g4, sink, bgf, bgb, ggla, wgf, wgb):
    def row(v):
        v = v.reshape(-1)
        return jnp.pad(v, (0, D - v.shape[0])).reshape(1, D)

    rows = [c_ctx.reshape(1, D), b_ada.reshape(6, D), g1.reshape(1, D), g2.reshape(1, D), g3.reshape(1, D),
            g4.reshape(1, D), row(sink), row(jnp.concatenate([bgf.reshape(-1), bgb.reshape(-1)])), row(ggla),
            row(wgf), row(wgb)]
    out = jnp.concatenate(rows, axis=0)
    return jnp.pad(out, ((0, SMALL_ROWS - out.shape[0]), (0, 0)))


def _unpack_small(p):
    return dict(c_ctx=p[0], b_ada=p[1:7].reshape(1, 6 * D), g_pre_mix=p[7:8], g_post_mix=p[8:9],
                g_pre_ffn=p[9:10], g_post_ffn=p[10:11], attn_sink=p[11:12, 0:8],
                b_gate_fwd=p[12:13, 0:256], b_gate_bwd=p[12:13, 256:512], g_gla_norm=p[13:14, 0:64],
                w_gate_fwd=p[14, 0:512].reshape(1, GATE_RANK, 32), w_gate_bwd=p[15, 0:512].reshape(1, GATE_RANK, 32))


def kernel(x, c, ctx, c_ctx, w_ada, b_ada, g_pre_mix, g_post_mix, g_pre_ffn, g_post_ffn, w_in, attn_sink, w_gate_fwd, b_gate_fwd, w_gate_bwd, b_gate_bwd, g_gla_norm, w_out, w_ffn_in, w_ffn_out, loss_target, m_c_ctx, m_w_ada, m_b_ada, m_g_pre_mix, m_g_post_mix, m_g_pre_ffn, m_g_post_ffn, m_w_in, m_attn_sink, m_w_gate_fwd, m_b_gate_fwd, m_w_gate_bwd, m_b_gate_bwd, m_g_gla_norm, m_w_out, m_w_ffn_in, m_w_ffn_out, v_c_ctx, v_w_ada, v_b_ada, v_g_pre_mix, v_g_post_mix, v_g_pre_ffn, v_g_post_ffn, v_w_in, v_attn_sink, v_w_gate_fwd, v_b_gate_fwd, v_w_gate_bwd, v_b_gate_bwd, v_g_gla_norm, v_w_out, v_w_ffn_in, v_w_ffn_out):
    me = 4 * lax.axis_index("x") + 2 * lax.axis_index("y") + lax.axis_index("c")
    S = x.shape[1]
    n_in = w_in.shape[2]
    n_ffi = w_ffn_in.shape[2]
    r_out = w_out.shape[1]
    r_ffo = w_ffn_out.shape[1]
    n_ada = w_ada.shape[2]

    small_in = jnp.concatenate([c.reshape(8, 128), w_gate_fwd.reshape(4, 128), w_gate_bwd.reshape(4, 128)], axis=0)
    sg = _allgather(small_in, "gather_small").reshape(N_DEV, 16, 128)
    c_all = sg[:, 0:8].reshape(N_DEV, D)
    wgf = sg[:, 8:12].reshape(N_DEV, GATE_RANK, 32).transpose(1, 0, 2).reshape(GATE_RANK, 256)
    wgb = sg[:, 12:16].reshape(N_DEV, GATE_RANK, 32).transpose(1, 0, 2).reshape(GATE_RANK, 256)

    pack = jnp.concatenate([w_in[0].astype(_BF).reshape(-1, 128), w_out[0].astype(_BF).reshape(-1, 128),
                            w_ffn_in[0].astype(_BF).reshape(-1, 128), w_ffn_out[0].astype(_BF).reshape(-1, 128)], axis=0)
    rows_per = pack.shape[0]
    wg_all = _allgather(pack, "gather_weights").reshape(N_DEV, rows_per, 128)
    o1 = D * n_in // 128
    o2 = o1 + r_out * D // 128
    o3 = o2 + D * n_ffi // 128
    win_full = wg_all[:, 0:o1].reshape(N_DEV, D, n_in).transpose(1, 0, 2).reshape(D, N_DEV * n_in)
    wout_full = wg_all[:, o1:o2].reshape(N_DEV * r_out, D)
    wffi_full = wg_all[:, o2:o3].reshape(N_DEV, D, n_ffi).transpose(1, 0, 2).reshape(D, N_DEV * n_ffi)
    wffo_full = wg_all[:, o3:].reshape(N_DEV * r_ffo, D)
    win_p = _pad_cols_win(win_full)
    wg_bd = jnp.zeros((128, 512), F32).at[0:16, 0:256].set(wgf).at[16:32, 256:512].set(wgb).astype(_BF)

    ada_part = _ada_fwd(c_all, c_ctx.reshape(1, D), w_ada[0])
    ada_all = _allgather(ada_part, "gather_ada").reshape(N_DEV, 16, n_ada)
    ada_full = ada_all.transpose(1, 0, 2).reshape(16, N_DEV * n_ada) + b_ada
    ada_l = jnp.pad(lax.dynamic_slice_in_dim(ada_full, me, 1, 0).reshape(6, D), ((0, 2), (0, 0)))
    ada_c = jnp.pad(ada_full[8].reshape(6, D), ((0, 2), (0, 0)))
    gains = jnp.pad(jnp.concatenate([g_pre_mix, g_post_mix, g_pre_ffn, g_post_ffn], axis=0), ((0, 4), (0, 0)))
    sink = jnp.broadcast_to(attn_sink.reshape(8, 1), (8, 128))
    bg = jnp.concatenate([b_gate_fwd, b_gate_bwd], axis=1)
    ggla = jnp.tile(g_gla_norm, (1, N_GLA))

    r = _local_step(x[0], ctx[0], loss_target[0], ada_l, ada_c, gains, sink, win_p, wg_bd, bg, ggla,
                    wout_full, wffi_full, wffo_full)

    loss = lax.psum(r["loss"], ("x", "y", "c"))

    g_win = _unpad_cols_win(r["g_win"]).reshape(D, N_DEV, n_in).transpose(1, 0, 2)
    g_wffi = r["g_wffi"].reshape(D, N_DEV, n_ffi).transpose(1, 0, 2)
    g_wout = r["g_wout"].reshape(N_DEV, r_out, D)
    g_wffo = r["g_wffo"].reshape(N_DEV, r_ffo, D)
    slots = _alltoall([g_win, g_wout, g_wffi, g_wffo], "scatter_grads")
    grad_w_in = _sum_slots(slots[0], "sum_w_in")
    grad_w_out = _sum_slots(slots[1], "sum_w_out")
    grad_w_ffn_in = _sum_slots(slots[2], "sum_w_ffn_in")
    grad_w_ffn_out = _sum_slots(slots[3], "sum_w_ffn_out")

    def row(vv):
        vv = vv.reshape(-1)
        return jnp.pad(vv, (0, D - vv.shape[0])).reshape(1, D)

    g_wgf = r["g_wg"][0:16, 0:256]
    g_wgb = r["g_wg"][16:32, 256:512]
    part = jnp.concatenate([r["small"], row(r["dsink"]), row(r["dbg"]), row(r["dggla"].reshape(N_GLA, DV).sum(0)),
                            g_wgf.reshape(4, D), g_wgb.reshape(4, D), jnp.zeros((5, D), F32)], axis=0)
    parts = _allgather(part, "gather_small_grads").reshape(N_DEV, 32, D)
    tot = _sum_slots(parts, "sum_small_grads")
    d_ada_rows = parts[:, 0:6].reshape(N_DEV, 6 * D)
    d_ada_c = tot[6:12].reshape(1, 6 * D)
    my_cols = lax.dynamic_slice_in_dim(jnp.concatenate([d_ada_rows, jnp.broadcast_to(d_ada_c, (1, 6 * D)),
                                                        jnp.zeros((7, 6 * D), F32)], axis=0), me * n_ada, n_ada, 1)
    grad_w_ada, t_part = _ada_bwd(c_all, c_ctx.reshape(1, D), w_ada[0], my_cols)
    t_all = _allgather(t_part, "gather_c_ctx").reshape(N_DEV, 8, D)
    t_tot = _sum_slots(t_all, "sum_c_ctx")[0]
    sc = 1.0 / (1.0 + jnp.exp(-c_ctx))
    grad_c_ctx = t_tot * (sc * (1.0 + c_ctx * (1.0 - sc)))
    grad_b_ada = (tot[0:6] + tot[6:12]).reshape(1, 6 * D)
    g_wgf_tot = tot[19:23].reshape(GATE_RANK, 256)
    g_wgb_tot = tot[23:27].reshape(GATE_RANK, 256)
    grads_small = dict(
        c_ctx=grad_c_ctx, b_ada=grad_b_ada, g_pre_mix=tot[12:13], g_post_mix=tot[13:14], g_pre_ffn=tot[14:15],
        g_post_ffn=tot[15:16], attn_sink=tot[16:17, 0:8], b_gate_fwd=tot[17:18, 0:256], b_gate_bwd=tot[17:18, 256:512],
        g_gla_norm=tot[18:19, 0:64],
        w_gate_fwd=lax.dynamic_slice_in_dim(g_wgf_tot, me * 32, 32, 1).reshape(1, GATE_RANK, 32),
        w_gate_bwd=lax.dynamic_slice_in_dim(g_wgb_tot, me * 32, 32, 1).reshape(1, GATE_RANK, 32))

    names_small = ["c_ctx", "b_ada", "g_pre_mix", "g_post_mix", "g_pre_ffn", "g_post_ffn", "attn_sink",
                   "b_gate_fwd", "b_gate_bwd", "g_gla_norm", "w_gate_fwd", "w_gate_bwd"]
    def packd(dd):
        return _pack_small(dd["c_ctx"], dd["b_ada"], dd["g_pre_mix"], dd["g_post_mix"], dd["g_pre_ffn"],
                           dd["g_post_ffn"], dd["attn_sink"], dd["b_gate_fwd"], dd["b_gate_bwd"], dd["g_gla_norm"],
                           dd["w_gate_fwd"], dd["w_gate_bwd"])

    w_small = dict(c_ctx=c_ctx, b_ada=b_ada, g_pre_mix=g_pre_mix, g_post_mix=g_post_mix, g_pre_ffn=g_pre_ffn,
                   g_post_ffn=g_post_ffn, attn_sink=attn_sink, b_gate_fwd=b_gate_fwd, b_gate_bwd=b_gate_bwd,
                   g_gla_norm=g_gla_norm, w_gate_fwd=w_gate_fwd, w_gate_bwd=w_gate_bwd)
    m_small = dict(c_ctx=m_c_ctx, b_ada=m_b_ada, g_pre_mix=m_g_pre_mix, g_post_mix=m_g_post_mix,
                   g_pre_ffn=m_g_pre_ffn, g_post_ffn=m_g_post_ffn, attn_sink=m_attn_sink, b_gate_fwd=m_b_gate_fwd,
                   b_gate_bwd=m_b_gate_bwd, g_gla_norm=m_g_gla_norm, w_gate_fwd=m_w_gate_fwd, w_gate_bwd=m_w_gate_bwd)
    v_small = dict(c_ctx=v_c_ctx, b_ada=v_b_ada, g_pre_mix=v_g_pre_mix, g_post_mix=v_g_post_mix,
                   g_pre_ffn=v_g_pre_ffn, g_post_ffn=v_g_post_ffn, attn_sink=v_attn_sink, b_gate_fwd=v_b_gate_fwd,
                   b_gate_bwd=v_b_gate_bwd, g_gla_norm=v_g_gla_norm, w_gate_fwd=v_w_gate_fwd, w_gate_bwd=v_w_gate_bwd)
    v_pack = packd(v_small)
    v_pack = jnp.where(packd(jax.tree.map(jnp.ones_like, v_small)) > 0, v_pack, 1.0)
    d_s, nm_s, nv_s = _adamw(packd(w_small), packd(grads_small), packd(m_small), v_pack, "adamw_small")
    d_s, nm_s, nv_s = _unpack_small(d_s), _unpack_small(nm_s), _unpack_small(nv_s)

    big = {}
    for nm, w, g, m, v in [("w_ada", w_ada, grad_w_ada, m_w_ada, v_w_ada), ("w_in", w_in, grad_w_in, m_w_in, v_w_in),
                           ("w_out", w_out, grad_w_out, m_w_out, v_w_out),
                           ("w_ffn_in", w_ffn_in, grad_w_ffn_in, m_w_ffn_in, v_w_ffn_in),
                           ("w_ffn_out", w_ffn_out, grad_w_ffn_out, m_w_ffn_out, v_w_ffn_out)]:
        d_, m_, v_ = _adamw(w[0], g, m[0], v[0], "adamw_" + nm)
        big[nm] = (g[None], d_[None], m_[None], v_[None])

    order = ["c_ctx", "w_ada", "b_ada", "g_pre_mix", "g_post_mix", "g_pre_ffn", "g_post_ffn", "w_in", "attn_sink",
             "w_gate_fwd", "b_gate_fwd", "w_gate_bwd", "b_gate_bwd", "g_gla_norm", "w_out", "w_ffn_in", "w_ffn_out"]
    grads, deltas, new_m, new_v = [], [], [], []
    for nm in order:
        if nm in big:
            g_, d_, m_, v_ = big[nm]
        else:
            g_, d_, m_, v_ = grads_small[nm], d_s[nm], nm_s[nm], nv_s[nm]
        grads.append(g_)
        deltas.append(d_)
        new_m.append(m_)
        new_v.append(v_)
    return (loss, r["grad_x"][None], *grads, *deltas, *new_m, *new_v)
```

```python
import functools
import math

import numpy as np
import jax
import jax.numpy as jnp
from jax import lax
from jax.experimental import pallas as pl
from jax.experimental.pallas import tpu as pltpu

F32 = jnp.float32
_BF = jnp.bfloat16

N_DEV = 8
D = 1024
CTX = 256
HD = 64
N_ATT = 8
N_KV = 2
GRP = N_ATT // N_KV
WIN = 128
GRID_W = 64
ROPE_BASE = 10000.0
N_GLA = 8
DK = 32
DV = 64
GATE_RANK = 16
GATE_TAU = 16.0
FFN = 2816
EPS = 1e-6
NEG = -1e30
GLA_T = 128

QP = N_ATT * 128
KP = N_KV * 128
O_Q, O_K, O_V = 0, QP, QP + KP
O_GQ = O_V + KP
O_GK = O_GQ + N_GLA * DK
O_GV = O_GK + N_GLA * DK
O_GG = O_GV + N_GLA * DV
O_Z = O_GG + N_GLA * DV
NP = O_Z + 128
IN_COLS = 2336

ADAM_LR, ADAM_B1, ADAM_B2, ADAM_EPS, ADAM_WD, ADAM_STEP = 0.001, 0.9, 0.999, 1e-08, 0.01, 10

VMEM_BIG = 56 * 1024 * 1024
MESH = pl.DeviceIdType.MESH


def _cp(sem, vmem=None):
    return pltpu.CompilerParams(dimension_semantics=sem, vmem_limit_bytes=vmem)


def _full(shape):
    nd = len(shape)
    return pl.BlockSpec(shape, lambda *a: (0,) * nd)


def _rows(tile, width, off=0):
    return pl.BlockSpec((tile, width), lambda i: (i + off, 0))


def _rows_lat(tile, width):
    return pl.BlockSpec((tile, width), lambda i: (jnp.maximum(i - 1, 0), 0))


def _nt(a, b):
    return lax.dot_general(a, b, (((1,), (1,)), ((), ())), preferred_element_type=F32)


def _tn(a, b):
    return lax.dot_general(a, b, (((0,), (0,)), ((), ())), preferred_element_type=F32)


def _nn(a, b):
    return jnp.dot(a, b, preferred_element_type=F32)


def _nn_hi(a, b):
    return jnp.dot(a, b, preferred_element_type=F32, precision=lax.Precision.HIGHEST)


def _rope(t, cos, sa, sb):
    n = t.shape[1]
    reps = n // 128
    c = jnp.tile(cos, (1, reps))
    a = jnp.tile(sa, (1, reps))
    b = jnp.tile(sb, (1, reps))
    return t * c + pltpu.roll(t, n - 16, 1) * a + pltpu.roll(t, 16, 1) * b


def _unrope(t, cos, sa, sb):
    n = t.shape[1]
    reps = n // 128
    c = jnp.tile(cos, (1, reps))
    a = jnp.tile(sa, (1, reps))
    b = jnp.tile(sb, (1, reps))
    return t * c + pltpu.roll(t * a, 16, 1) + pltpu.roll(t * b, n - 16, 1)


def _sigmoid(x):
    return 1.0 / (1.0 + jnp.exp(-x))


def _inproj_fwd(xe, gml, shl, gmc, shc, win, wg, bg, cos, sa, sb):
    E = xe.shape[0]
    TE = 256

    def body(x_ref, gml_ref, shl_ref, gmc_ref, shc_ref, w_ref, wg_ref, bg_ref, cos_ref, sa_ref, sb_ref,
             h_ref, q_ref, k_ref, v_ref, gq_ref, gk_ref, gv_ref, gg_ref, z_ref, la_ref):
        is_ctx = pl.program_id(0) == 0
        gm = jnp.where(is_ctx, gmc_ref[...], gml_ref[...])
        sh = jnp.where(is_ctx, shc_ref[...], shl_ref[...])
        x = x_ref[...]
        r = lax.rsqrt(jnp.mean(x * x, axis=-1, keepdims=True) + EPS)
        hb = ((x * r) * gm + sh).astype(_BF)
        h_ref[...] = hb
        p = _nn(hb, w_ref[...])
        cos_t, sa_t, sb_t = cos_ref[...], sa_ref[...], sb_ref[...]
        q_ref[...] = (_rope(p[:, O_Q:O_K], cos_t, sa_t, sb_t) * (HD ** -0.5)).astype(_BF)
        k_ref[...] = _rope(p[:, O_K:O_V], cos_t, sa_t, sb_t).astype(_BF)
        v_ref[...] = p[:, O_V:O_GQ].astype(_BF)
        gq_ref[...] = p[:, O_GQ:O_GK] * (DK ** -0.5)
        gk_ref[...] = p[:, O_GK:O_GV]
        gv_ref[...] = p[:, O_GV:O_GG]
        gg_ref[...] = p[:, O_GG:O_Z]
        zb = p[:, O_Z:NP].astype(_BF)
        z_ref[...] = zb
        lg = _nn(zb, wg_ref[...]) + bg_ref[...]
        la_ref[...] = (jnp.minimum(lg, 0.0) - jnp.log(1.0 + jnp.exp(-jnp.abs(lg)))) * (1.0 / GATE_TAU)

    vec = _full((1, D))
    tab = _rows(TE, 128)
    outs = [(D, _BF), (QP, _BF), (KP, _BF), (KP, _BF), (256, F32), (256, F32), (512, F32), (512, F32),
            (128, _BF), (512, F32)]
    return pl.pallas_call(
        body, name="inproj_fwd", grid=(E // TE,),
        in_specs=[_rows(TE, D), vec, vec, vec, vec, _full((D, NP)), _full((128, 512)), _full((1, 512)), tab, tab, tab],
        out_specs=[_rows(TE, w) for w, _ in outs],
        out_shape=[jax.ShapeDtypeStruct((E, w), dt) for w, dt in outs],
        compiler_params=_cp(("arbitrary",), 40 * 1024 * 1024),
    )(xe, gml, shl, gmc, shc, win, wg, bg, cos, sa, sb)


def _xchg_scratch(na):
    return [pltpu.SemaphoreType.DMA((na, N_DEV - 1)), pltpu.SemaphoreType.DMA((na, N_DEV - 1)),
            pltpu.SemaphoreType.DMA((na,))]


def _xchg_copies(ins, outs, send_sems, recv_sems, local_sems, gather):
    x, y, c = lax.axis_index("x"), lax.axis_index("y"), lax.axis_index("c")
    me = 4 * x + 2 * y + c
    local, sends, recvs = [], [], []
    for a in range(len(ins)):
        local.append(pltpu.make_async_copy(ins[a] if gather else ins[a].at[me], outs[a].at[me], local_sems.at[a]))
    for k in range(1, N_DEV):
        px, py, pc = x ^ (k >> 2), y ^ ((k >> 1) & 1), c ^ (k & 1)
        peer = 4 * px + 2 * py + pc
        for a in range(len(ins)):
            sems = dict(send_sem=send_sems.at[a, k - 1], recv_sem=recv_sems.at[a, k - 1], device_id_type=MESH)
            sends.append(pltpu.make_async_remote_copy(
                src_ref=ins[a] if gather else ins[a].at[peer], dst_ref=outs[a].at[me], device_id=(px, py, pc), **sems))
            recvs.append(pltpu.make_async_remote_copy(
                src_ref=ins[a] if gather else ins[a].at[me], dst_ref=outs[a].at[peer], device_id=(x, y, c), **sems))
    return local, sends, recvs


def _xchg_start(cps):
    local, sends, _ = cps
    for cp in local + sends:
        cp.start()


def _xchg_finish(cps):
    local, sends, recvs = cps
    for cp in recvs:
        cp.wait_recv()
    for cp in sends:
        cp.wait_send()
    for cp in local:
        cp.wait()


def _xchg_out_shapes(ins, gather):
    return [jax.ShapeDtypeStruct(((N_DEV,) + s.shape) if gather else s.shape, s.dtype) for s in ins]


def _attn_specs(E):
    nb = (E - CTX) // WIN
    last = E // WIN - 1
    kc = pl.BlockSpec((CTX, KP), lambda n: (0, 0))
    kp = pl.BlockSpec((WIN, KP), lambda n: (n + 1, 0))
    kk = pl.BlockSpec((WIN, KP), lambda n: (n + 2, 0))
    kn = pl.BlockSpec((WIN, KP), lambda n: (jnp.minimum(n + 3, last), 0))
    return nb, [kc, kp, kk, kn]


def _attn_valid(n, nb):
    rows = lax.broadcasted_iota(jnp.int32, (GRP * WIN, CTX + 3 * WIN), 0) & (WIN - 1)
    cols = lax.broadcasted_iota(jnp.int32, (GRP * WIN, CTX + 3 * WIN), 1)
    j = cols - CTX
    rel = j - WIN - rows
    return (cols < CTX) | ((jnp.abs(rel) <= WIN) & ((j >= WIN) | (n > 0)) & ((j < 2 * WIN) | (n < nb - 1)))


def _attn_fwd(q, k, v, sink, shards):
    E = q.shape[0]
    S = E - CTX
    nb, kspecs = _attn_specs(E)
    na = len(shards)

    def body(q_ref, kc, kp, kk, kn, vc, vp, vk, vn, sink_ref, *rest):
        shard_refs, (o_ref, lse_ref), got_refs = rest[:na], rest[na:na + 2], rest[na + 2:2 * na + 2]
        n = pl.program_id(0)
        cps = _xchg_copies(shard_refs, got_refs, *rest[2 * na + 2:], gather=True)

        @pl.when(n == 0)
        def _():
            _xchg_start(cps)

        valid = _attn_valid(n, nb)
        lane = lax.broadcasted_iota(jnp.int32, (WIN, 128), 1)
        lse_t = jnp.zeros((WIN, 128), F32)
        for h in range(N_KV):
            hs = slice(128 * h, 128 * h + 128)
            K = jnp.concatenate([kc[:, hs], kp[:, hs], kk[:, hs], kn[:, hs]], axis=0)
            V = jnp.concatenate([vc[:, hs], vp[:, hs], vk[:, hs], vn[:, hs]], axis=0)
            Q = jnp.concatenate([q_ref[:, 128 * (GRP * h + g):128 * (GRP * h + g) + 128] for g in range(GRP)], axis=0)
            sk = jnp.concatenate([jnp.broadcast_to(sink_ref[GRP * h + g:GRP * h + g + 1, 0:1], (WIN, 1))
                                  for g in range(GRP)], axis=0)
            s = jnp.where(valid, _nt(Q, K), NEG)
            m = jnp.maximum(jnp.max(s, axis=1, keepdims=True), sk)
            e = jnp.exp(s - m)
            den = jnp.sum(e, axis=1, keepdims=True) + jnp.exp(sk - m)
            o = _nn((e / den).astype(_BF), V)
            lse = m + jnp.log(den)
            for g in range(GRP):
                lse_t = jnp.where(lane == GRP * h + g, lse[WIN * g:WIN * g + WIN], lse_t)
            for pp in range(GRP // 2):
                a = o[WIN * 2 * pp:WIN * 2 * pp + WIN]
                b = o[WIN * (2 * pp + 1):WIN * (2 * pp + 1) + WIN]
                t = 2 * h + pp
                o_ref[:, 128 * t:128 * t + 128] = (a + pltpu.roll(b, 64, 1)).astype(_BF)
        lse_ref[...] = lse_t

        @pl.when(n == nb - 1)
        def _():
            _xchg_finish(cps)

    qs = pl.BlockSpec((WIN, QP), lambda n: (n + 2, 0))
    anyspec = pl.BlockSpec(memory_space=pl.ANY)
    return pl.pallas_call(
        body, name="attn_fwd", grid=(nb,),
        in_specs=[qs] + kspecs + kspecs + [_full((8, 128))] + [anyspec] * na,
        out_specs=[_rows(WIN, 512), _rows(WIN, 128)] + [anyspec] * na,
        out_shape=[jax.ShapeDtypeStruct((S, 512), _BF), jax.ShapeDtypeStruct((S, 128), F32)]
        + _xchg_out_shapes(shards, True),
        scratch_shapes=_xchg_scratch(na),
        compiler_params=_cp(("arbitrary",)),
    )(q, k, k, k, k, v, v, v, v, sink, *shards)


def _attn_bwd(q, k, v, sink, lse, d_attn, slabs):
    E = q.shape[0]
    S = E - CTX
    nb, kspecs = _attn_specs(E)
    last = E // WIN - 1
    na = len(slabs)

    def body(q_ref, kc, kp, kk, kn, vc, vp, vk, vn, sink_ref, lse_ref, do_ref, *rest):
        slab_refs, (dq_ref, dk_ref, dv_ref, ds_ref), got_refs = rest[:na], rest[na:na + 4], rest[na + 4:2 * na + 4]
        n = pl.program_id(0)
        cps = _xchg_copies(slab_refs, got_refs, *rest[2 * na + 4:], gather=False)

        @pl.when(n == 0)
        def _():
            _xchg_start(cps)
            dk_ref[...] = jnp.zeros_like(dk_ref)
            dv_ref[...] = jnp.zeros_like(dv_ref)
            ds_ref[...] = jnp.zeros_like(ds_ref)

        valid = _attn_valid(n, nb)
        lane = lax.broadcasted_iota(jnp.int32, (WIN, 128), 1)
        lse_t = lse_ref[...]
        starts = [None, pl.multiple_of((n + 1) * WIN, WIN), pl.multiple_of((n + 2) * WIN, WIN),
                  pl.multiple_of(jnp.minimum(n + 3, last) * WIN, WIN)]
        for h in range(N_KV):
            hs = slice(128 * h, 128 * h + 128)
            K = jnp.concatenate([kc[:, hs], kp[:, hs], kk[:, hs], kn[:, hs]], axis=0)
            V = jnp.concatenate([vc[:, hs], vp[:, hs], vk[:, hs], vn[:, hs]], axis=0)
            Q = jnp.concatenate([q_ref[:, 128 * (GRP * h + g):128 * (GRP * h + g) + 128] for g in range(GRP)], axis=0)
            sk = jnp.concatenate([jnp.broadcast_to(sink_ref[GRP * h + g:GRP * h + g + 1, 0:1], (WIN, 1))
                                  for g in range(GRP)], axis=0)
            ls = jnp.concatenate([jnp.sum(jnp.where(lane == GRP * h + g, lse_t, 0.0), axis=1, keepdims=True)
                                  for g in range(GRP)], axis=0)
            dos = []
            for g in range(GRP):
                j = GRP * h + g
                t = do_ref[:, 128 * (j // 2):128 * (j // 2) + 128].astype(F32)
                if j % 2:
                    t = pltpu.roll(t, 64, 1)
                dos.append(jnp.where(lane < HD, t, 0.0))
            do = jnp.concatenate(dos, axis=0).astype(_BF)
            s = _nt(Q, K)
            p = jnp.where(valid, jnp.exp(s - ls), 0.0)
            dp = _nt(do, V)
            delta = jnp.sum(p * dp, axis=1, keepdims=True)
            dsc = (p * (dp - delta)).astype(_BF)
            dq = _nn(dsc, K) * (HD ** -0.5)
            for g in range(GRP):
                j = GRP * h + g
                dq_ref[:, 128 * j:128 * j + 128] = dq[WIN * g:WIN * g + WIN].astype(_BF)
            dK = _tn(dsc, Q)
            dV = _tn(p.astype(_BF), do)
            dk_ref[0:CTX, hs] += dK[0:CTX]
            dv_ref[0:CTX, hs] += dV[0:CTX]
            for w in range(1, 4):
                lo = CTX + WIN * (w - 1)
                dk_ref[pl.ds(starts[w], WIN), hs] += dK[lo:lo + WIN]
                dv_ref[pl.ds(starts[w], WIN), hs] += dV[lo:lo + WIN]
            psk = -jnp.exp(sk - ls) * delta
            for g in range(GRP):
                j = GRP * h + g
                tot = jnp.sum(psk[WIN * g:WIN * g + WIN], axis=0, keepdims=True)
                ds_ref[j:j + 1, :] += jnp.broadcast_to(tot, (1, 128))

        @pl.when(n == nb - 1)
        def _():
            _xchg_finish(cps)

    qs = pl.BlockSpec((WIN, QP), lambda n: (n + 2, 0))
    anyspec = pl.BlockSpec(memory_space=pl.ANY)
    return pl.pallas_call(
        body, name="attn_bwd", grid=(nb,),
        in_specs=[qs] + kspecs + kspecs + [_full((8, 128)), _rows(WIN, 128), _rows(WIN, 512)] + [anyspec] * na,
        out_specs=[_rows(WIN, QP), _full((E, KP)), _full((E, KP)), _full((8, 128))] + [anyspec] * na,
        out_shape=[jax.ShapeDtypeStruct((S, QP), _BF), jax.ShapeDtypeStruct((E, KP), F32),
                   jax.ShapeDtypeStruct((E, KP), F32), jax.ShapeDtypeStruct((8, 128), F32)]
        + _xchg_out_shapes(slabs, False),
        scratch_shapes=_xchg_scratch(na),
        compiler_params=_cp(("arbitrary",), 48 * 1024 * 1024),
    )(q, k, k, k, k, v, v, v, v, sink, lse, d_attn, *slabs)


def _gla_order(E, reverse, backward):
    nc = CTX // GLA_T
    n = E // GLA_T
    if not reverse:
        fwd = lambda s: s
    else:
        fwd = lambda s: jnp.where(s < nc, nc - 1 - s, n - 1 + nc - s)
    if backward:
        return lambda s: fwd(n - 1 - s)
    return fwd


def _gla_masks():
    T = GLA_T
    l256 = lax.broadcasted_iota(jnp.int32, (1, 256), 1)
    l512 = lax.broadcasted_iota(jnp.int32, (1, 512), 1)
    hm32 = [((l256 >> 5) == h).astype(F32) for h in range(N_GLA)]
    hm64 = [((l512 >> 6) == h).astype(F32) for h in range(N_GLA)]
    bd = ((lax.broadcasted_iota(jnp.int32, (512, 256), 0) >> 6)
          == (lax.broadcasted_iota(jnp.int32, (512, 256), 1) >> 5)).astype(F32)
    ri = lax.broadcasted_iota(jnp.int32, (T, T), 0)
    ci = lax.broadcasted_iota(jnp.int32, (T, T), 1)
    return hm32, hm64, bd, ri, ci


def _gla_decays(la, reverse, ri, ci):
    T = GLA_T
    msk = (ri <= ci) if reverse else (ri >= ci)
    b = _nn_hi(msk.astype(F32), la)
    bT = b[0:1] if reverse else b[T - 1:T]
    bm = b[T // 2:T // 2 + 1]
    return msk, b, bT, bm


def _gla_fwd(gq, gk, gv, la, reverse):
    E = gq.shape[0]
    T = GLA_T
    n = E // T
    order = _gla_order(E, reverse, False)
    col = 1 if reverse else 0

    def body(gq_ref, gk_ref, gv_ref, la_ref, o_ref, st_ref, S_scr):
        @pl.when(pl.program_id(0) == 0)
        def _():
            S_scr[...] = jnp.zeros_like(S_scr)

        hm32, hm64, bd, ri, ci = _gla_masks()
        msk, b, bT, bm = _gla_decays(la_ref[...], reverse, ri, ci)
        q, k, v = gq_ref[...], gk_ref[...], gv_ref[...]
        qd = (q * jnp.exp(b)).astype(_BF)
        qm = q * jnp.exp(b - bm)
        km = (k * jnp.exp(bm - b)).astype(_BF)
        kd = (k * jnp.exp(bT - b)).astype(_BF)
        ST = S_scr[...]
        comp = ST[0:DV]
        for h in range(1, N_GLA):
            comp = comp + ST[DV * h:DV * h + DV]
        st_ref[0] = comp
        o = _nt(qd, ST.astype(_BF))
        for h in range(N_GLA):
            A = jnp.where(msk, _nt((qm * hm32[h]).astype(_BF), km), 0.0).astype(_BF)
            o = o + _nn(A, (v * hm64[h]).astype(_BF))
        o_ref[...] = o
        S_scr[...] = ST * jnp.exp(bT) + bd * _tn(v.astype(_BF), kd)

    blk = lambda w, c=0: pl.BlockSpec((T, w), lambda s: (order(s), c))
    return pl.pallas_call(
        body, name="gla_fwd_rev" if reverse else "gla_fwd", grid=(n,),
        in_specs=[blk(256), blk(256), blk(512), blk(256, col)],
        out_specs=[blk(512), pl.BlockSpec((1, DV, 256), lambda s: (order(s), 0, 0))],
        out_shape=[jax.ShapeDtypeStruct((E, 512), F32), jax.ShapeDtypeStruct((n, DV, 256), F32)],
        scratch_shapes=[pltpu.VMEM((512, 256), F32)],
        compiler_params=_cp(("arbitrary",)),
    )(gq, gk, gv, la)


def _gla_bwd(gq, gk, gv, la, st, do, reverse):
    E = gq.shape[0]
    T = GLA_T
    n = E // T
    nc = CTX // T
    order = _gla_order(E, reverse, True)
    col = 1 if reverse else 0

    def body(gq_ref, gk_ref, gv_ref, la_ref, st_ref, do_ref, dq_ref, dk_ref, dv_ref, dla_ref, dS_scr):
        @pl.when(pl.program_id(0) == 0)
        def _():
            dS_scr[...] = jnp.zeros_like(dS_scr)

        is_lat = order(pl.program_id(0)) >= nc
        hm32, hm64, bd, ri, ci = _gla_masks()
        msk, b, bT, bm = _gla_decays(la_ref[...], reverse, ri, ci)
        mskT = (ri >= ci) if reverse else (ri <= ci)
        q, k, v = gq_ref[...], gk_ref[...], gv_ref[...]
        do = jnp.where(is_lat, do_ref[...].astype(F32), 0.0)
        e_b, e_qm, e_km, e_kd, e_T = jnp.exp(b), jnp.exp(b - bm), jnp.exp(bm - b), jnp.exp(bT - b), jnp.exp(bT)
        qd, qm, km, kd = q * e_b, q * e_qm, k * e_km, k * e_kd
        qdb, qmb, kmb, kdb, vb, dob = (t.astype(_BF) for t in (qd, qm, km, kd, v, do))
        ST = jnp.tile(st_ref[0], (N_GLA, 1)) * bd
        dST = dS_scr[...]
        dSTb = dST.astype(_BF)
        dqd = _nn(dob, ST.astype(_BF))
        dv = _nt(kdb, dSTb)
        dkd = _nn(vb, dSTb)
        dqm = jnp.zeros((T, 256), F32)
        dkm = jnp.zeros((T, 256), F32)
        for h in range(N_GLA):
            qh = (qm * hm32[h]).astype(_BF)
            kh = (km * hm32[h]).astype(_BF)
            doh = (do * hm64[h]).astype(_BF)
            vh = (v * hm64[h]).astype(_BF)
            AT = jnp.where(mskT, _nt(kh, qmb), 0.0).astype(_BF)
            dA = jnp.where(msk, _nt(doh, vb), 0.0).astype(_BF)
            dAT = jnp.where(mskT, _nt(vh, dob), 0.0).astype(_BF)
            dv = dv + _nn(AT, doh)
            dqm = dqm + _nn(dA, kh)
            dkm = dkm + _nn(dAT, qh)
        dq_ref[...] = dqm * e_qm + dqd * e_b
        dk_ref[...] = dkm * e_km + dkd * e_kd
        dv_ref[...] = dv
        db = dqm * qm - dkm * km + dqd * qd - dkd * kd
        dbT = jnp.sum(dkd * kd, axis=0, keepdims=True) + e_T * jnp.sum(dST * ST, axis=0, keepdims=True)
        dla_ref[...] = _nn_hi(mskT.astype(F32), db) + dbT
        dS_scr[...] = dST * e_T + bd * _tn(dob, qdb)

    blk = lambda w, c=0: pl.BlockSpec((T, w), lambda s: (order(s), c))
    do_spec = pl.BlockSpec((T, 512), lambda s: (jnp.maximum(order(s) - nc, 0), 0))
    return pl.pallas_call(
        body, name="gla_bwd_rev" if reverse else "gla_bwd", grid=(n,),
        in_specs=[blk(256), blk(256), blk(512), blk(256, col),
                  pl.BlockSpec((1, DV, 256), lambda s: (order(s), 0, 0)), do_spec],
        out_specs=[blk(256), blk(256), blk(512), blk(256)],
        out_shape=[jax.ShapeDtypeStruct((E, 256), F32), jax.ShapeDtypeStruct((E, 256), F32),
                   jax.ShapeDtypeStruct((E, 512), F32), jax.ShapeDtypeStruct((E, 256), F32)],
        scratch_shapes=[pltpu.VMEM((512, 256), F32)],
        compiler_params=_cp(("arbitrary",)),
    )(gq, gk, gv, la, st, do)


def _gla_out(o_f, o_b, gg, ggla, mavg):
    o = o_f + o_b
    rr = lax.rsqrt(_nn_hi(o * o, mavg) + EPS)
    oh = o * rr
    sg = _sigmoid(gg)
    return oh, rr, sg


def _mix_fwd(x, attn, o_f, o_b, gg, ggla, mavg, wout, gt1, g2):
    S = x.shape[0]
    TM = 256

    def body(x_ref, a_ref, of_ref, ob_ref, gg_ref, ggla_ref, mavg_ref, w_ref, gt1_ref, g2_ref, x1_ref, mix_ref):
        gg_t = gg_ref[...]
        oh, _, sg = _gla_out(of_ref[...], ob_ref[...], gg_t, ggla_ref[...], mavg_ref[...])
        mix_ref[:, 0:512] = a_ref[...]
        mix_ref[:, 512:1024] = (oh * ggla_ref[...] * (gg_t * sg)).astype(_BF)
        y = _nn(mix_ref[...], w_ref[...])
        ry = lax.rsqrt(jnp.mean(y * y, axis=-1, keepdims=True) + EPS)
        x1_ref[...] = x_ref[...] + gt1_ref[...] * ((y * ry) * g2_ref[...])

    return pl.pallas_call(
        body, name="mix_fwd", grid=(S // TM,),
        in_specs=[_rows(TM, D), _rows(TM, 512), _rows(TM, 512, 1), _rows(TM, 512, 1), _rows(TM, 512, 1),
                  _full((1, 512)), _full((512, 512)), _full((D, D)), _full((1, D)), _full((1, D))],
        out_specs=[_rows(TM, D), _rows(TM, D)],
        out_shape=[jax.ShapeDtypeStruct((S, D), F32), jax.ShapeDtypeStruct((S, D), _BF)],
        compiler_params=_cp(("arbitrary",), 40 * 1024 * 1024),
    )(x, attn, o_f, o_b, gg, ggla, mavg, wout, gt1, g2)


def _mix_bwd(dx1, mix, o_f, o_b, gg, ggla, mavg, wout, gt1, g2):
    S = dx1.shape[0]
    TM = 256

    def body(dx_ref, mix_ref, of_ref, ob_ref, gg_ref, ggla_ref, mavg_ref, w_ref, gt1_ref, g2_ref,
             da_ref, do_ref, dgg_ref, dy_ref, sums_ref):
        @pl.when(pl.program_id(0) == 0)
        def _():
            sums_ref[...] = jnp.zeros_like(sums_ref)

        dx = dx_ref[...]
        y = _nn(mix_ref[...], w_ref[...])
        ry = lax.rsqrt(jnp.mean(y * y, axis=-1, keepdims=True) + EPS)
        yh = y * ry
        sums_ref[0:1, :] += jnp.sum(dx * yh, axis=0, keepdims=True)
        dyh = dx * (gt1_ref[...] * g2_ref[...])
        dy = (ry * (dyh - yh * jnp.mean(dyh * yh, axis=-1, keepdims=True))).astype(_BF)
        dy_ref[...] = dy
        dmix = _nt(dy, w_ref[...])
        da_ref[...] = dmix[:, 0:512].astype(_BF)
        dgla = dmix[:, 512:1024]
        gg_t = gg_ref[...]
        ggla_t = ggla_ref[...]
        oh, rr, sg = _gla_out(of_ref[...], ob_ref[...], gg_t, ggla_t, mavg_ref[...])
        dgg_ref[...] = (dgla * oh * ggla_t * (sg * (1.0 + gg_t * (1.0 - sg)))).astype(_BF)
        don = dgla * (gg_t * sg)
        sums_ref[1:2, 0:512] += jnp.sum(don * oh, axis=0, keepdims=True)
        doh = don * ggla_t
        do_ref[...] = (rr * (doh - oh * _nn_hi(doh * oh, mavg_ref[...]))).astype(_BF)

    return pl.pallas_call(
        body, name="mix_bwd", grid=(S // TM,),
        in_specs=[_rows(TM, D), _rows(TM, D), _rows(TM, 512, 1), _rows(TM, 512, 1), _rows(TM, 512, 1),
                  _full((1, 512)), _full((512, 512)), _full((D, D)), _full((1, D)), _full((1, D))],
        out_specs=[_rows(TM, 512), _rows(TM, 512), _rows(TM, 512), _rows(TM, D), _full((8, D))],
        out_shape=[jax.ShapeDtypeStruct((S, 512), _BF), jax.ShapeDtypeStruct((S, 512), _BF),
                   jax.ShapeDtypeStruct((S, 512), _BF), jax.ShapeDtypeStruct((S, D), _BF),
                   jax.ShapeDtypeStruct((8, D), F32)],
        compiler_params=_cp(("arbitrary",), 40 * 1024 * 1024),
    )(dx1, mix, o_f, o_b, gg, ggla, mavg, wout, gt1, g2)


def _ffn(x1, target, gm2, sh2, gt2, g4, wffi, wffo):
    S = x1.shape[0]
    TF = 256

    def body(x_ref, t_ref, gm_ref, sh_ref, gt_ref, g4_ref, wi_hbm, wo_hbm,
             dx_ref, h_ref, du_ref, act_ref, df_ref, sums_ref, loss_ref, wi, wo, sem):
        @pl.when(pl.program_id(0) == 0)
        def _():
            c1 = pltpu.make_async_copy(wi_hbm, wi, sem.at[0])
            c2 = pltpu.make_async_copy(wo_hbm, wo, sem.at[1])
            c1.start()
            c2.start()
            sums_ref[...] = jnp.zeros_like(sums_ref)
            loss_ref[...] = jnp.zeros_like(loss_ref)
            c1.wait()
            c2.wait()

        x = x_ref[...]
        gm = gm_ref[...]
        r = lax.rsqrt(jnp.mean(x * x, axis=-1, keepdims=True) + EPS)
        xh = x * r
        hb = (xh * gm + sh_ref[...]).astype(_BF)
        h_ref[...] = hb
        u = _nn(hb, wi[...])
        g = u[:, 0:FFN]
        up = u[:, FFN:2 * FFN]
        sg = _sigmoid(g)
        sl = g * sg
        ab = (sl * up).astype(_BF)
        act_ref[...] = ab
        f = _nn(ab, wo[...])
        rf = lax.rsqrt(jnp.mean(f * f, axis=-1, keepdims=True) + EPS)
        fh = f * rf
        gt, g4v = gt_ref[...], g4_ref[...]
        err = x + gt * (fh * g4v) - t_ref[...]
        loss_ref[...] += jnp.sum(err * err) * (0.5 / D)
        dout = err * (1.0 / D)
        sums_ref[2:3, :] += jnp.sum(dout * fh, axis=0, keepdims=True)
        dfh = dout * (gt * g4v)
        dfb = (rf * (dfh - fh * jnp.mean(dfh * fh, axis=-1, keepdims=True))).astype(_BF)
        df_ref[...] = dfb
        dact = _nt(dfb, wo[...])
        du_ref[:, 0:FFN] = (dact * up * (sg * (1.0 + g * (1.0 - sg)))).astype(_BF)
        du_ref[:, FFN:2 * FFN] = (dact * sl).astype(_BF)
        dh = _nt(du_ref[...], wi[...])
        sums_ref[0:1, :] += jnp.sum(dh, axis=0, keepdims=True)
        sums_ref[1:2, :] += jnp.sum(dh * xh, axis=0, keepdims=True)
        dxh = dh * gm
        dx_ref[...] = dout + r * (dxh - xh * jnp.mean(dxh * xh, axis=-1, keepdims=True))

    vec = _full((1, D))
    anyspec = pl.BlockSpec(memory_space=pl.ANY)
    return pl.pallas_call(
        body, name="ffn_fwd_bwd", grid=(S // TF,),
        in_specs=[_rows(TF, D), _rows(TF, D), vec, vec, vec, vec, anyspec, anyspec],
        out_specs=[_rows(TF, D), _rows(TF, D), _rows(TF, 2 * FFN), _rows(TF, FFN), _rows(TF, D),
                   _full((8, D)), _full((8, 128))],
        out_shape=[jax.ShapeDtypeStruct((S, D), F32), jax.ShapeDtypeStruct((S, D), _BF),
                   jax.ShapeDtypeStruct((S, 2 * FFN), _BF), jax.ShapeDtypeStruct((S, FFN), _BF),
                   jax.ShapeDtypeStruct((S, D), _BF), jax.ShapeDtypeStruct((8, D), F32),
                   jax.ShapeDtypeStruct((8, 128), F32)],
        scratch_shapes=[pltpu.VMEM((D, 2 * FFN), _BF), pltpu.VMEM((FFN, D), _BF), pltpu.SemaphoreType.DMA((2,))],
        compiler_params=_cp(("arbitrary",), VMEM_BIG),
    )(x1, target, gm2, sh2, gt2, g4, wffi, wffo)


def _inproj_bwd(xe, gml, gmc, win, wg, cos, sa, sb, la, dq, dk, dv, dgq_f, dgq_b, dgk_f, dgk_b, dgv_f, dgv_b,
                dgg, dla_f, dla_b, dx1):
    E = xe.shape[0]
    S = E - CTX
    TE = 256

    def body(x_ref, gml_ref, gmc_ref, w_ref, wg_ref, cos_ref, sa_ref, sb_ref, la_ref, dq_ref, dk_ref, dv_ref,
             gqf, gqb, gkf, gkb, gvf, gvb, dgg_ref, dlf, dlb, dx1_ref,
             dp_ref, dlg_ref, gx_ref, sums_ref, bsum_ref):
        i = pl.program_id(0)
        is_ctx = i == 0

        @pl.when(is_ctx)
        def _():
            sums_ref[...] = jnp.zeros_like(sums_ref)
            bsum_ref[...] = jnp.zeros_like(bsum_ref)

        lat = jnp.where(is_ctx, 0.0, 1.0)
        cos_t, sa_t, sb_t = cos_ref[...], sa_ref[...], sb_ref[...]
        dp_ref[:, O_Q:O_K] = (_unrope(dq_ref[...].astype(F32), cos_t, sa_t, sb_t) * lat).astype(_BF)
        dp_ref[:, O_K:O_V] = _unrope(dk_ref[...], cos_t, sa_t, sb_t).astype(_BF)
        dp_ref[:, O_V:O_GQ] = dv_ref[...].astype(_BF)
        dp_ref[:, O_GQ:O_GK] = ((gqf[...] + gqb[...]) * (DK ** -0.5)).astype(_BF)
        dp_ref[:, O_GK:O_GV] = (gkf[...] + gkb[...]).astype(_BF)
        dp_ref[:, O_GV:O_GG] = (gvf[...] + gvb[...]).astype(_BF)
        dp_ref[:, O_GG:O_Z] = (dgg_ref[...].astype(F32) * lat).astype(_BF)
        la_t = la_ref[...]
        dlg = (jnp.concatenate([dlf[...], dlb[...]], axis=1) * (1.0 - jnp.exp(GATE_TAU * la_t)) * (1.0 / GATE_TAU))
        bsum_ref[0:1, :] += jnp.sum(dlg, axis=0, keepdims=True)
        dlgb = dlg.astype(_BF)
        dlg_ref[...] = dlgb
        dp_ref[:, O_Z:NP] = _nt(dlgb, wg_ref[...]).astype(_BF)
        dh = _nt(dp_ref[...], w_ref[...])
        x = x_ref[...]
        r = lax.rsqrt(jnp.mean(x * x, axis=-1, keepdims=True) + EPS)
        xh = x * r
        sdh = jnp.sum(dh, axis=0, keepdims=True)
        sdx = jnp.sum(dh * xh, axis=0, keepdims=True)
        sums_ref[0:1, :] += sdh * lat
        sums_ref[1:2, :] += sdx * lat
        sums_ref[2:3, :] += sdh * (1.0 - lat)
        sums_ref[3:4, :] += sdx * (1.0 - lat)
        dxh = dh * jnp.where(is_ctx, gmc_ref[...], gml_ref[...])
        gx_ref[...] = dx1_ref[...] + r * (dxh - xh * jnp.mean(dxh * xh, axis=-1, keepdims=True))

    vec = _full((1, D))
    tab = _rows(TE, 128)
    return pl.pallas_call(
        body, name="inproj_bwd", grid=(E // TE,),
        in_specs=[_rows(TE, D), vec, vec, _full((D, NP)), _full((128, 512)), tab, tab, tab, _rows(TE, 512),
                  _rows_lat(TE, QP), _rows(TE, KP), _rows(TE, KP),
                  _rows(TE, 256), _rows(TE, 256), _rows(TE, 256), _rows(TE, 256), _rows(TE, 512), _rows(TE, 512),
                  _rows_lat(TE, 512), _rows(TE, 256), _rows(TE, 256), _rows_lat(TE, D)],
        out_specs=[_rows(TE, NP), _rows(TE, 512), _rows_lat(TE, D), _full((8, D)), _full((8, 512))],
        out_shape=[jax.ShapeDtypeStruct((E, NP), _BF), jax.ShapeDtypeStruct((E, 512), _BF),
                   jax.ShapeDtypeStruct((S, D), F32), jax.ShapeDtypeStruct((8, D), F32),
                   jax.ShapeDtypeStruct((8, 512), F32)],
        compiler_params=_cp(("arbitrary",), VMEM_BIG),
    )(xe, gml, gmc, win, wg, cos, sa, sb, la, dq, dk, dv, dgq_f, dgq_b, dgk_f, dgk_b, dgv_f, dgv_b,
      dgg, dla_f, dla_b, dx1)


def _matmul_tn(a, b, tk, tn, tt, name):
    T, KA = a.shape
    N = b.shape[1]

    def body(a_ref, b_ref, o_ref):
        @pl.when(pl.program_id(2) == 0)
        def _():
            o_ref[...] = jnp.zeros_like(o_ref)

        o_ref[...] += _tn(a_ref[...], b_ref[...])

    return pl.pallas_call(
        body, name=name, grid=(KA // tk, N // tn, T // tt),
        in_specs=[pl.BlockSpec((tt, tk), lambda i, j, t: (t, i)), pl.BlockSpec((tt, tn), lambda i, j, t: (t, j))],
        out_specs=pl.BlockSpec((tk, tn), lambda i, j, t: (i, j)),
        out_shape=jax.ShapeDtypeStruct((KA, N), F32),
        compiler_params=_cp(("parallel", "parallel", "arbitrary"), VMEM_BIG),
    )(a, b)


def _ada_fwd(c_all, c_ctx, w_ada):
    n = w_ada.shape[1]

    def body(c_ref, cc_ref, w_ref, o_ref):
        c = jnp.concatenate([c_ref[...], jnp.broadcast_to(cc_ref[...], (8, D))], axis=0)
        o_ref[...] = _nn((c * _sigmoid(c)).astype(_BF), w_ref[...].astype(_BF))

    return pl.pallas_call(
        body, name="ada_fwd", in_specs=[_full((8, D)), _full((1, D)), _full((D, n))], out_specs=_full((16, n)),
        out_shape=jax.ShapeDtypeStruct((16, n), F32), grid=(1,), compiler_params=_cp(("arbitrary",)),
    )(c_all, c_ctx, w_ada)


def _ada_bwd(c_all, c_ctx, w_ada, d_all):
    n = w_ada.shape[1]

    def body(c_ref, cc_ref, w_ref, d_ref, gw_ref, t_ref):
        c = jnp.concatenate([c_ref[...], jnp.broadcast_to(cc_ref[...], (8, D))], axis=0)
        db = d_ref[...].astype(_BF)
        gw_ref[...] = _tn((c * _sigmoid(c)).astype(_BF), db)
        t_ref[...] = _nt(db[8:16], w_ref[...].astype(_BF))

    return pl.pallas_call(
        body, name="ada_bwd", in_specs=[_full((8, D)), _full((1, D)), _full((D, n)), _full((16, n))],
        out_specs=[_full((D, n)), _full((8, D))],
        out_shape=[jax.ShapeDtypeStruct((D, n), F32), jax.ShapeDtypeStruct((8, D), F32)], grid=(1,),
        compiler_params=_cp(("arbitrary",)),
    )(c_all, c_ctx, w_ada, d_all)


def _small_grads(s_in, s_ffn, s_mix, ada_l, ada_c, gains):
    def body(si, sf, sm, al, ac, g, o_ref):
        sdh_l, sdx_l, sdh_c, sdx_c = si[0:1], si[1:2], si[2:3], si[3:4]
        sdh2, sdx2, a2 = sf[0:1], sf[1:2], sf[2:3]
        a1 = sm[0:1]
        g1, g2, g3, g4 = g[0:1], g[1:2], g[2:3], g[3:4]
        sc1, gt1, sc2, gt2 = al[1:2], al[2:3], al[4:5], al[5:6]
        sc1c = ac[1:2]
        z = jnp.zeros((1, D), F32)
        rows = [sdh_l, sdx_l * g1, a1 * g2, sdh2, sdx2 * g3, a2 * g4,
                sdh_c, sdx_c * g1, z, z, z, z,
                sdx_l * (1.0 + sc1) + sdx_c * (1.0 + sc1c), a1 * gt1, sdx2 * (1.0 + sc2), a2 * gt2]
        for r, v in enumerate(rows):
            o_ref[r:r + 1, :] = v

    v8 = _full((8, D))
    return pl.pallas_call(
        body, name="small_grads", in_specs=[v8] * 6, out_specs=_full((16, D)),
        out_shape=jax.ShapeDtypeStruct((16, D), F32), grid=(1,), compiler_params=_cp(("arbitrary",)),
    )(s_in, s_ffn, s_mix, ada_l, ada_c, gains)


def _adamw(w, g, m, v, name):
    R, C = w.shape
    tr = R
    for cand in (256, 128, 64, 32, 16, 8):
        if R % cand == 0 and R > cand:
            tr = cand
            break
    c1 = 1.0 / (1.0 - ADAM_B1 ** ADAM_STEP)
    c2 = 1.0 / (1.0 - ADAM_B2 ** ADAM_STEP)

    def body(w_ref, g_ref, m_ref, v_ref, d_ref, nm_ref, nv_ref):
        gg = g_ref[...]
        nm = ADAM_B1 * m_ref[...] + (1.0 - ADAM_B1) * gg
        nv = ADAM_B2 * v_ref[...] + (1.0 - ADAM_B2) * (gg * gg)
        nm_ref[...] = nm
        nv_ref[...] = nv
        d_ref[...] = -ADAM_LR * ((nm * c1) / (jnp.sqrt(nv * c2) + ADAM_EPS) + ADAM_WD * w_ref[...])

    spec = _rows(tr, C)
    sds = jax.ShapeDtypeStruct((R, C), F32)
    return pl.pallas_call(
        body, name=name, grid=(R // tr,), in_specs=[spec] * 4, out_specs=[spec] * 3, out_shape=[sds] * 3,
        compiler_params=_cp(("parallel",)),
    )(w, g, m, v)


def _sum_slots(slots, name):
    _, R, C = slots.shape
    tr = R
    for cand in (256, 128, 64, 32, 16, 8):
        if R % cand == 0 and R > cand:
            tr = cand
            break

    def body(s_ref, o_ref):
        acc = s_ref[0].astype(F32)
        for j in range(1, N_DEV):
            acc = acc + s_ref[j].astype(F32)
        o_ref[...] = acc

    return pl.pallas_call(
        body, name=name, grid=(R // tr,), in_specs=[pl.BlockSpec((N_DEV, tr, C), lambda i: (0, i, 0))],
        out_specs=_rows(tr, C), out_shape=jax.ShapeDtypeStruct((R, C), F32), compiler_params=_cp(("parallel",)),
    )(slots)


def _allgather(x_shard, name):
    m_per, n = x_shard.shape

    def body(x_ref, out_ref, send_sems, recv_sems, local_sem):
        x, y, c = lax.axis_index("x"), lax.axis_index("y"), lax.axis_index("c")
        me, sibling = (x, y, c), (x, y, 1 - c)
        chips = [(1 - x, y), (x, 1 - y), (1 - x, 1 - y)]

        def rows(px, py, pc):
            return out_ref.at[pl.ds((4 * px + 2 * py + pc) * m_per, m_per), :]

        def copy(k, block, to, src=None):
            return pltpu.make_async_remote_copy(
                src_ref=rows(*block) if src is None else src, dst_ref=rows(*block),
                send_sem=send_sems.at[k], recv_sem=recv_sems.at[k], device_id=to, device_id_type=MESH)

        mine = pltpu.make_async_copy(x_ref, rows(*me), local_sem)
        mine.start()
        first = [copy(0, me, sibling, src=x_ref)]
        first += [copy(1 + j, me, (*chip, c), src=x_ref) for j, chip in enumerate(chips)]
        for cp in first:
            cp.start()
        passed = [copy(4 + j, (*chip, c), sibling) for j, chip in enumerate(chips)]
        for j, chip in enumerate(chips):
            copy(1 + j, (*chip, c), me).wait_recv()
            passed[j].start()
        copy(0, sibling, me).wait_recv()
        for j, chip in enumerate(chips):
            copy(4 + j, (*chip, 1 - c), me).wait_recv()
        for cp in first + passed:
            cp.wait_send()
        mine.wait()

    return pl.pallas_call(
        body, name=name, out_shape=jax.ShapeDtypeStruct((N_DEV * m_per, n), x_shard.dtype),
        in_specs=[pl.BlockSpec(memory_space=pltpu.VMEM)], out_specs=pl.BlockSpec(memory_space=pltpu.VMEM),
        scratch_shapes=[pltpu.SemaphoreType.DMA((7,)), pltpu.SemaphoreType.DMA((7,)), pltpu.SemaphoreType.DMA],
        compiler_params=pltpu.CompilerParams(vmem_limit_bytes=VMEM_BIG),
    )(x_shard)


def _alltoall(slabs, name):
    na = len(slabs)

    def body(*refs):
        ins, outs = refs[:na], refs[na:2 * na]
        send_sems, recv_sems, local_sems = refs[2 * na:]
        x, y, c = lax.axis_index("x"), lax.axis_index("y"), lax.axis_index("c")
        me = 4 * x + 2 * y + c
        copies = []
        for a in range(na):
            loc = pltpu.make_async_copy(ins[a].at[me], outs[a].at[me], local_sems.at[a])
            loc.start()
            copies.append(loc)
        rem = []
        for k in range(1, N_DEV):
            px, py, pc = x ^ (k >> 2), y ^ ((k >> 1) & 1), c ^ (k & 1)
            peer = 4 * px + 2 * py + pc
            for a in range(na):
                cp = pltpu.make_async_remote_copy(
                    src_ref=ins[a].at[peer], dst_ref=outs[a].at[me],
                    send_sem=send_sems.at[a, k - 1], recv_sem=recv_sems.at[a, k - 1],
                    device_id=(px, py, pc), device_id_type=MESH)
                cp.start()
                rem.append((a, k, peer, cp))
        for a, k, peer, cp in rem:
            pltpu.make_async_remote_copy(
                src_ref=ins[a].at[me], dst_ref=outs[a].at[peer],
                send_sem=send_sems.at[a, k - 1], recv_sem=recv_sems.at[a, k - 1],
                device_id=(x, y, c), device_id_type=MESH).wait_recv()
        for a, k, peer, cp in rem:
            cp.wait_send()
        for loc in copies:
            loc.wait()

    anyspec = pl.BlockSpec(memory_space=pl.ANY)
    return pl.pallas_call(
        body, name=name, out_shape=[jax.ShapeDtypeStruct(s.shape, s.dtype) for s in slabs],
        in_specs=[anyspec] * na, out_specs=[anyspec] * na,
        scratch_shapes=[pltpu.SemaphoreType.DMA((na, 7)), pltpu.SemaphoreType.DMA((na, 7)),
                        pltpu.SemaphoreType.DMA((na,))],
    )(*slabs)


def _rope_tables(S):
    t = np.arange(S)
    row = (t // GRID_W).astype(np.float32)
    colp = (t % GRID_W).astype(np.float32)
    half = HD // 2
    inv = (ROPE_BASE ** (-np.arange(0, half, 2, dtype=np.float32) / half)).astype(np.float32)
    ar = row[:, None] * inv[None, :]
    ac = colp[:, None] * inv[None, :]
    ang = np.concatenate([ar, ar, ac, ac], axis=-1).astype(np.float32)
    cos = np.cos(ang).astype(np.float32)
    sin = np.sin(ang).astype(np.float32)
    lane = np.arange(HD)
    first = (lane % 32) < 16
    sa = np.where(first[None, :], -sin, 0.0)
    sb = np.where(first[None, :], 0.0, sin)

    def ext(tab, ctx_val):
        full = np.zeros((CTX + S, 128), np.float32)
        full[:CTX, :HD] = ctx_val
        full[CTX:, :HD] = tab
        return jnp.asarray(full)

    return ext(cos, 1.0), ext(sa, 0.0), ext(sb, 0.0)


def _pad_cols_win(w):
    q, k, v, rest, z = w[:, 0:512], w[:, 512:640], w[:, 640:768], w[:, 768:2304], w[:, 2304:2336]

    def padh(t, nh):
        t = t.reshape(D, nh, HD)
        return jnp.pad(t, ((0, 0), (0, 0), (0, 128 - HD))).reshape(D, nh * 128)

    return jnp.concatenate([padh(q, N_ATT), padh(k, N_KV), padh(v, N_KV), rest, jnp.pad(z, ((0, 0), (0, 96)))], axis=1)


def _unpad_cols_win(g):
    def unp(t, nh):
        return t.reshape(D, nh, 128)[:, :, :HD].reshape(D, nh * HD)

    return jnp.concatenate([unp(g[:, O_Q:O_K], N_ATT), unp(g[:, O_K:O_V], N_KV), unp(g[:, O_V:O_GQ], N_KV),
                            g[:, O_GQ:O_Z], g[:, O_Z:O_Z + 32]], axis=1)


def _local_step(x, ctx, target, ada_l, ada_c, gains, sink, win_p, wg_bd, bg, ggla, wout, wffi_sh, wffo_sh):
    S = x.shape[0]
    cos, sa, sb = _rope_tables(S)
    xe = jnp.concatenate([ctx, x], axis=0)
    g1, g2, g3, g4 = (gains[i:i + 1] for i in range(4))
    sh1, sc1, gt1, sh2, sc2, gt2 = (ada_l[i:i + 1] for i in range(6))
    sh1c, sc1c = ada_c[0:1], ada_c[1:2]
    gml, gmc, gm2 = g1 * (1.0 + sc1), g1 * (1.0 + sc1c), g3 * (1.0 + sc2)
    mavg = jnp.asarray(np.kron(np.eye(N_GLA, dtype=np.float32), np.full((DV, DV), 1.0 / DV, np.float32)))

    h, q, k, v, gq, gk, gv, gg, z, la = _inproj_fwd(xe, gml, sh1, gmc, sh1c, win_p, wg_bd, bg, cos, sa, sb)
    n_ffi, r_ffo = wffi_sh.shape[1], wffo_sh.shape[0]
    tt_e = 768 if (S + CTX) % 768 == 0 else 256
    tt_s = 512 if S % 512 == 0 else 256
    attn, lse, wffi_g, wffo_g = _attn_fwd(q, k, v, sink, [wffi_sh, wffo_sh])
    wffi = wffi_g.transpose(1, 0, 2).reshape(D, N_DEV * n_ffi)
    wffo = wffo_g.reshape(N_DEV * r_ffo, D)
    o_f, st_f = _gla_fwd(gq, gk, gv, la, False)
    o_b, st_b = _gla_fwd(gq, gk, gv, la, True)
    x1, mix = _mix_fwd(x, attn, o_f, o_b, gg, ggla, mavg, wout, gt1, g2)
    dx1, h2, du, act, df, s_ffn, loss = _ffn(x1, target, gm2, sh2, gt2, g4, wffi, wffo)
    g_wffi = _matmul_tn(h2, du, 512, 2 * FFN, tt_s, "grad_w_ffn_in")
    g_wffo = _matmul_tn(act, df, FFN, D, tt_s, "grad_w_ffn_out")
    slab_ffi = g_wffi.reshape(D, N_DEV, n_ffi).transpose(1, 0, 2).astype(_BF)
    slab_ffo = g_wffo.reshape(N_DEV, r_ffo, D).astype(_BF)
    d_attn, do_gla, dgg, dy, s_mix = _mix_bwd(dx1, mix, o_f, o_b, gg, ggla, mavg, wout, gt1, g2)
    dq, dk, dv, dsink, got_ffi, got_ffo = _attn_bwd(q, k, v, sink, lse, d_attn, [slab_ffi, slab_ffo])
    dgq_f, dgk_f, dgv_f, dla_f = _gla_bwd(gq, gk, gv, la, st_f, do_gla, False)
    dgq_b, dgk_b, dgv_b, dla_b = _gla_bwd(gq, gk, gv, la, st_b, do_gla, True)
    dp, dlg, grad_x, s_in, s_bg = _inproj_bwd(xe, gml, gmc, win_p, wg_bd, cos, sa, sb, la, dq, dk, dv,
                                              dgq_f, dgq_b, dgk_f, dgk_b, dgv_f, dgv_b, dgg, dla_f, dla_b, dx1)
    g_win = _matmul_tn(h, dp, D, NP, tt_e, "grad_w_in")
    g_wg = _matmul_tn(z, dlg, 128, 512, tt_e, "grad_w_gate")
    g_wout = _matmul_tn(mix, dy, D, D, tt_s, "grad_w_out")
    small = _small_grads(s_in, s_ffn, s_mix, ada_l, ada_c, gains)
    return dict(loss=loss[0, 0], grad_x=grad_x, g_win=g_win, g_wg=g_wg, g_wout=g_wout, got_ffi=got_ffi,
                got_ffo=got_ffo, small=small, dsink=dsink[:, 0], dbg=s_bg[0], dggla=s_mix[1, 0:512])


SMALL_ROWS = 32


def _pack_small(c_ctx, b_ada, g1, g2, g3, g4, sink, bgf, bgb, ggla, wgf, wgb):
    def row(v):
        v = v.reshape(-1)
        return jnp.pad(v, (0, D - v.shape[0])).reshape(1, D)

    rows = [c_ctx.reshape(1, D), b_ada.reshape(6, D), g1.reshape(1, D), g2.reshape(1, D), g3.reshape(1, D),
            g4.reshape(1, D), row(sink), row(jnp.concatenate([bgf.reshape(-1), bgb.reshape(-1)])), row(ggla),
            row(wgf), row(wgb)]
    out = jnp.concatenate(rows, axis=0)
    return jnp.pad(out, ((0, SMALL_ROWS - out.shape[0]), (0, 0)))


def _unpack_small(p):
    return dict(c_ctx=p[0], b_ada=p[1:7].reshape(1, 6 * D), g_pre_mix=p[7:8], g_post_mix=p[8:9],
                g_pre_ffn=p[9:10], g_post_ffn=p[10:11], attn_sink=p[11:12, 0:8],
                b_gate_fwd=p[12:13, 0:256], b_gate_bwd=p[12:13, 256:512], g_gla_norm=p[13:14, 0:64],
                w_gate_fwd=p[14, 0:512].reshape(1, GATE_RANK, 32), w_gate_bwd=p[15, 0:512].reshape(1, GATE_RANK, 32))


def kernel(x, c, ctx, c_ctx, w_ada, b_ada, g_pre_mix, g_post_mix, g_pre_ffn, g_post_ffn, w_in, attn_sink, w_gate_fwd, b_gate_fwd, w_gate_bwd, b_gate_bwd, g_gla_norm, w_out, w_ffn_in, w_ffn_out, loss_target, m_c_ctx, m_w_ada, m_b_ada, m_g_pre_mix, m_g_post_mix, m_g_pre_ffn, m_g_post_ffn, m_w_in, m_attn_sink, m_w_gate_fwd, m_b_gate_fwd, m_w_gate_bwd, m_b_gate_bwd, m_g_gla_norm, m_w_out, m_w_ffn_in, m_w_ffn_out, v_c_ctx, v_w_ada, v_b_ada, v_g_pre_mix, v_g_post_mix, v_g_pre_ffn, v_g_post_ffn, v_w_in, v_attn_sink, v_w_gate_fwd, v_b_gate_fwd, v_w_gate_bwd, v_b_gate_bwd, v_g_gla_norm, v_w_out, v_w_ffn_in, v_w_ffn_out):
    me = 4 * lax.axis_index("x") + 2 * lax.axis_index("y") + lax.axis_index("c")
    S = x.shape[1]
    n_in = w_in.shape[2]
    n_ffi = w_ffn_in.shape[2]
    r_out = w_out.shape[1]
    r_ffo = w_ffn_out.shape[1]
    n_ada = w_ada.shape[2]

    small_in = jnp.concatenate([c.reshape(8, 128), w_gate_fwd.reshape(4, 128), w_gate_bwd.reshape(4, 128)], axis=0)
    sg = _allgather(small_in, "gather_small").reshape(N_DEV, 16, 128)
    c_all = sg[:, 0:8].reshape(N_DEV, D)
    wgf = sg[:, 8:12].reshape(N_DEV, GATE_RANK, 32).transpose(1, 0, 2).reshape(GATE_RANK, 256)
    wgb = sg[:, 12:16].reshape(N_DEV, GATE_RANK, 32).transpose(1, 0, 2).reshape(GATE_RANK, 256)

    pack = jnp.concatenate([w_in[0].astype(_BF).reshape(-1, 128), w_out[0].astype(_BF).reshape(-1, 128)], axis=0)
    rows_per = pack.shape[0]
    wg_all = _allgather(pack, "gather_weights").reshape(N_DEV, rows_per, 128)
    o1 = D * n_in // 128
    win_full = wg_all[:, 0:o1].reshape(N_DEV, D, n_in).transpose(1, 0, 2).reshape(D, N_DEV * n_in)
    wout_full = wg_all[:, o1:].reshape(N_DEV * r_out, D)
    win_p = _pad_cols_win(win_full)
    wg_bd = jnp.zeros((128, 512), F32).at[0:16, 0:256].set(wgf).at[16:32, 256:512].set(wgb).astype(_BF)

    ada_part = _ada_fwd(c_all, c_ctx.reshape(1, D), w_ada[0])
    ada_all = _allgather(ada_part, "gather_ada").reshape(N_DEV, 16, n_ada)
    ada_full = ada_all.transpose(1, 0, 2).reshape(16, N_DEV * n_ada) + b_ada
    ada_l = jnp.pad(lax.dynamic_slice_in_dim(ada_full, me, 1, 0).reshape(6, D), ((0, 2), (0, 0)))
    ada_c = jnp.pad(ada_full[8].reshape(6, D), ((0, 2), (0, 0)))
    gains = jnp.pad(jnp.concatenate([g_pre_mix, g_post_mix, g_pre_ffn, g_post_ffn], axis=0), ((0, 4), (0, 0)))
    sink = jnp.broadcast_to(attn_sink.reshape(8, 1), (8, 128))
    bg = jnp.concatenate([b_gate_fwd, b_gate_bwd], axis=1)
    ggla = jnp.tile(g_gla_norm, (1, N_GLA))

    r = _local_step(x[0], ctx[0], loss_target[0], ada_l, ada_c, gains, sink, win_p, wg_bd, bg, ggla,
                    wout_full, w_ffn_in[0].astype(_BF), w_ffn_out[0].astype(_BF))

    loss = lax.psum(r["loss"], ("x", "y", "c"))

    g_win = _unpad_cols_win(r["g_win"]).reshape(D, N_DEV, n_in).transpose(1, 0, 2).astype(_BF)
    g_wout = r["g_wout"].reshape(N_DEV, r_out, D).astype(_BF)
    slots = _alltoall([g_win, g_wout], "scatter_grads")
    grad_w_in = _sum_slots(slots[0], "sum_w_in")
    grad_w_out = _sum_slots(slots[1], "sum_w_out")
    grad_w_ffn_in = _sum_slots(r["got_ffi"], "sum_w_ffn_in")
    grad_w_ffn_out = _sum_slots(r["got_ffo"], "sum_w_ffn_out")

    def row(vv):
        vv = vv.reshape(-1)
        return jnp.pad(vv, (0, D - vv.shape[0])).reshape(1, D)

    g_wgf = r["g_wg"][0:16, 0:256]
    g_wgb = r["g_wg"][16:32, 256:512]
    part = jnp.concatenate([r["small"], row(r["dsink"]), row(r["dbg"]), row(r["dggla"].reshape(N_GLA, DV).sum(0)),
                            g_wgf.reshape(4, D), g_wgb.reshape(4, D), jnp.zeros((5, D), F32)], axis=0)
    parts = _allgather(part, "gather_small_grads").reshape(N_DEV, 32, D)
    tot = _sum_slots(parts, "sum_small_grads")
    d_ada_rows = parts[:, 0:6].reshape(N_DEV, 6 * D)
    d_ada_c = tot[6:12].reshape(1, 6 * D)
    my_cols = lax.dynamic_slice_in_dim(jnp.concatenate([d_ada_rows, jnp.broadcast_to(d_ada_c, (1, 6 * D)),
                                                        jnp.zeros((7, 6 * D), F32)], axis=0), me * n_ada, n_ada, 1)
    grad_w_ada, t_part = _ada_bwd(c_all, c_ctx.reshape(1, D), w_ada[0], my_cols)
    t_all = _allgather(t_part, "gather_c_ctx").reshape(N_DEV, 8, D)
    t_tot = _sum_slots(t_all, "sum_c_ctx")[0]
    sc = 1.0 / (1.0 + jnp.exp(-c_ctx))
    grad_c_ctx = t_tot * (sc * (1.0 + c_ctx * (1.0 - sc)))
    grad_b_ada = (tot[0:6] + tot[6:12]).reshape(1, 6 * D)
    g_wgf_tot = tot[19:23].reshape(GATE_RANK, 256)
    g_wgb_tot = tot[23:27].reshape(GATE_RANK, 256)
    grads_small = dict(
        c_ctx=grad_c_ctx, b_ada=grad_b_ada, g_pre_mix=tot[12:13], g_post_mix=tot[13:14], g_pre_ffn=tot[14:15],
        g_post_ffn=tot[15:16], attn_sink=tot[16:17, 0:8], b_gate_fwd=tot[17:18, 0:256], b_gate_bwd=tot[17:18, 256:512],
        g_gla_norm=tot[18:19, 0:64],
        w_gate_fwd=lax.dynamic_slice_in_dim(g_wgf_tot, me * 32, 32, 1).reshape(1, GATE_RANK, 32),
        w_gate_bwd=lax.dynamic_slice_in_dim(g_wgb_tot, me * 32, 32, 1).reshape(1, GATE_RANK, 32))

    names_small = ["c_ctx", "b_ada", "g_pre_mix", "g_post_mix", "g_pre_ffn", "g_post_ffn", "attn_sink",
                   "b_gate_fwd", "b_gate_bwd", "g_gla_norm", "w_gate_fwd", "w_gate_bwd"]
    def packd(dd):
        return _pack_small(dd["c_ctx"], dd["b_ada"], dd["g_pre_mix"], dd["g_post_mix"], dd["g_pre_ffn"],
                           dd["g_post_ffn"], dd["attn_sink"], dd["b_gate_fwd"], dd["b_gate_bwd"], dd["g_gla_norm"],
                           dd["w_gate_fwd"], dd["w_gate_bwd"])

    w_small = dict(c_ctx=c_ctx, b_ada=b_ada, g_pre_mix=g_pre_mix, g_post_mix=g_post_mix, g_pre_ffn=g_pre_ffn,
                   g_post_ffn=g_post_ffn, attn_sink=attn_sink, b_gate_fwd=b_gate_fwd, b_gate_bwd=b_gate_bwd,
                   g_gla_norm=g_gla_norm, w_gate_fwd=w_gate_fwd, w_gate_bwd=w_gate_bwd)
    m_small = dict(c_ctx=m_c_ctx, b_ada=m_b_ada, g_pre_mix=m_g_pre_mix, g_post_mix=m_g_post_mix,
                   g_pre_ffn=m_g_pre_ffn, g_post_ffn=m_g_post_ffn, attn_sink=m_attn_sink, b_gate_fwd=m_b_gate_fwd,
                   b_gate_bwd=m_b_gate_bwd, g_gla_norm=m_g_gla_norm, w_gate_fwd=m_w_gate_fwd, w_gate_bwd=m_w_gate_bwd)
    v_small = dict(c_ctx=v_c_ctx, b_ada=v_b_ada, g_pre_mix=v_g_pre_mix, g_post_mix=v_g_post_mix,
                   g_pre_ffn=v_g_pre_ffn, g_post_ffn=v_g_post_ffn, attn_sink=v_attn_sink, b_gate_fwd=v_b_gate_fwd,
                   b_gate_bwd=v_b_gate_bwd, g_gla_norm=v_g_gla_norm, w_gate_fwd=v_w_gate_fwd, w_gate_bwd=v_w_gate_bwd)
    v_pack = packd(v_small)
    v_pack = jnp.where(packd(jax.tree.map(jnp.ones_like, v_small)) > 0, v_pack, 1.0)
    d_s, nm_s, nv_s = _adamw(packd(w_small), packd(grads_small), packd(m_small), v_pack, "adamw_small")
    d_s, nm_s, nv_s = _unpack_small(d_s), _unpack_small(nm_s), _unpack_small(nv_s)

    big = {}
    for nm, w, g, m, v in [("w_ada", w_ada, grad_w_ada, m_w_ada, v_w_ada), ("w_in", w_in, grad_w_in, m_w_in, v_w_in),
                           ("w_out", w_out, grad_w_out, m_w_out, v_w_out),
                           ("w_ffn_in", w_ffn_in, grad_w_ffn_in, m_w_ffn_in, v_w_ffn_in),
                           ("w_ffn_out", w_ffn_out, grad_w_ffn_out, m_w_ffn_out, v_w_ffn_out)]:
        d_, m_, v_ = _adamw(w[0], g, m[0], v[0], "adamw_" + nm)
        big[nm] = (g[None], d_[None], m_[None], v_[None])

    order = ["c_ctx", "w_ada", "b_ada", "g_pre_mix", "g_post_mix", "g_pre_ffn", "g_post_ffn", "w_in", "attn_sink",
             "w_gate_fwd", "b_gate_fwd", "w_gate_bwd", "b_gate_bwd", "g_gla_norm", "w_out", "w_ffn_in", "w_ffn_out"]
    grads, deltas, new_m, new_v = [], [], [], []
    for nm in order:
        if nm in big:
            g_, d_, m_, v_ = big[nm]
        else:
            g_, d_, m_, v_ = grads_small[nm], d_s[nm], nm_s[nm], nv_s[nm]
        grads.append(g_)
        deltas.append(d_)
        new_m.append(m_)
        new_v.append(v_)
    return (loss, r["grad_x"][None], *grads, *deltas, *new_m, *new_v)
```

```python
import functools
import math

import numpy as np
import jax
import jax.numpy as jnp
from jax import lax
from jax.experimental import pallas as pl
from jax.experimental.pallas import tpu as pltpu

F32 = jnp.float32
_BF = jnp.bfloat16

N_DEV = 8
D = 1024
CTX = 256
HD = 64
N_ATT = 8
N_KV = 2
GRP = N_ATT // N_KV
WIN = 128
GRID_W = 64
ROPE_BASE = 10000.0
N_GLA = 8
DK = 32
DV = 64
GATE_RANK = 16
GATE_TAU = 16.0
FFN = 2816
EPS = 1e-6
NEG = -1e30
GLA_T = 128

QP = N_ATT * 128
KP = N_KV * 128
O_Q, O_K, O_V = 0, QP, QP + KP
O_GQ = O_V + KP
O_GK = O_GQ + N_GLA * DK
O_GV = O_GK + N_GLA * DK
O_GG = O_GV + N_GLA * DV
O_Z = O_GG + N_GLA * DV
NP = O_Z + 128
IN_COLS = 2336

ADAM_LR, ADAM_B1, ADAM_B2, ADAM_EPS, ADAM_WD, ADAM_STEP = 0.001, 0.9, 0.999, 1e-08, 0.01, 10

VMEM_BIG = 56 * 1024 * 1024
MESH = pl.DeviceIdType.MESH


def _cp(sem, vmem=None):
    return pltpu.CompilerParams(dimension_semantics=sem, vmem_limit_bytes=vmem)


def _full(shape):
    nd = len(shape)
    return pl.BlockSpec(shape, lambda *a: (0,) * nd)


def _rows(tile, width, off=0):
    return pl.BlockSpec((tile, width), lambda i: (i + off, 0))


def _rows_lat(tile, width):
    return pl.BlockSpec((tile, width), lambda i: (jnp.maximum(i - 1, 0), 0))


def _nt(a, b):
    return lax.dot_general(a, b, (((1,), (1,)), ((), ())), preferred_element_type=F32)


def _tn(a, b):
    return lax.dot_general(a, b, (((0,), (0,)), ((), ())), preferred_element_type=F32)


def _nn(a, b):
    return jnp.dot(a, b, preferred_element_type=F32)


def _head_mean(x, mavg):
    n = x.shape[0]
    hi = x.astype(_BF)
    lo = (x - hi.astype(F32)).astype(_BF)
    y = _nn(jnp.concatenate([hi, lo], axis=0), mavg)
    return y[0:n] + y[n:2 * n]


def _rope(t, cos, sa, sb):
    n = t.shape[1]
    reps = n // 128
    c = jnp.tile(cos, (1, reps))
    a = jnp.tile(sa, (1, reps))
    b = jnp.tile(sb, (1, reps))
    return t * c + pltpu.roll(t, n - 16, 1) * a + pltpu.roll(t, 16, 1) * b


def _unrope(t, cos, sa, sb):
    n = t.shape[1]
    reps = n // 128
    c = jnp.tile(cos, (1, reps))
    a = jnp.tile(sa, (1, reps))
    b = jnp.tile(sb, (1, reps))
    return t * c + pltpu.roll(t * a, 16, 1) + pltpu.roll(t * b, n - 16, 1)


def _sigmoid(x):
    return 1.0 / (1.0 + jnp.exp(-x))


def _inproj_fwd(x, ctx, gml, shl, gmc, shc, win, wg, bg, cos, sa, sb):
    E = x.shape[0] + CTX
    TE = CTX

    def body(x_ref, c_ref, gml_ref, shl_ref, gmc_ref, shc_ref, w_ref, wg_ref, bg_ref, cos_ref, sa_ref, sb_ref,
             h_ref, q_ref, k_ref, v_ref, gq_ref, gk_ref, gv_ref, gg_ref, z_ref, la_ref):
        is_ctx = pl.program_id(0) == 0
        gm = jnp.where(is_ctx, gmc_ref[...], gml_ref[...])
        sh = jnp.where(is_ctx, shc_ref[...], shl_ref[...])
        x = jnp.where(is_ctx, c_ref[...], x_ref[...])
        r = lax.rsqrt(jnp.mean(x * x, axis=-1, keepdims=True) + EPS)
        hb = ((x * r) * gm + sh).astype(_BF)
        h_ref[...] = hb
        p = _nn(hb, w_ref[...])
        cos_t, sa_t, sb_t = cos_ref[...], sa_ref[...], sb_ref[...]
        q_ref[...] = (_rope(p[:, O_Q:O_K], cos_t, sa_t, sb_t) * (HD ** -0.5)).astype(_BF)
        k_ref[...] = _rope(p[:, O_K:O_V], cos_t, sa_t, sb_t).astype(_BF)
        v_ref[...] = p[:, O_V:O_GQ].astype(_BF)
        gq_ref[...] = p[:, O_GQ:O_GK] * (DK ** -0.5)
        gk_ref[...] = p[:, O_GK:O_GV]
        gv_ref[...] = p[:, O_GV:O_GG]
        gg_ref[...] = p[:, O_GG:O_Z]
        zb = p[:, O_Z:NP].astype(_BF)
        z_ref[...] = zb
        lg = _nn(zb, wg_ref[...]) + bg_ref[...]
        la_ref[...] = (jnp.minimum(lg, 0.0) - jnp.log(1.0 + jnp.exp(-jnp.abs(lg)))) * (1.0 / GATE_TAU)

    vec = _full((1, D))
    tab = _rows(TE, 128)
    outs = [(D, _BF), (QP, _BF), (KP, _BF), (KP, _BF), (256, F32), (256, F32), (512, F32), (512, F32),
            (128, _BF), (512, F32)]
    return pl.pallas_call(
        body, name="inproj_fwd", grid=(E // TE,),
        in_specs=[_rows_lat(TE, D), _full((CTX, D)), vec, vec, vec, vec, _full((D, NP)), _full((128, 512)),
                  _full((1, 512)), tab, tab, tab],
        out_specs=[_rows(TE, w) for w, _ in outs],
        out_shape=[jax.ShapeDtypeStruct((E, w), dt) for w, dt in outs],
        compiler_params=_cp(("arbitrary",), 40 * 1024 * 1024),
    )(x, ctx, gml, shl, gmc, shc, win, wg, bg, cos, sa, sb)


def _xchg_scratch(na):
    return [pltpu.SemaphoreType.DMA((na, N_DEV - 1)), pltpu.SemaphoreType.DMA((na, N_DEV - 1)),
            pltpu.SemaphoreType.DMA((na,))]


def _xchg_copies(ins, outs, send_sems, recv_sems, local_sems, gather):
    x, y, c = lax.axis_index("x"), lax.axis_index("y"), lax.axis_index("c")
    me = 4 * x + 2 * y + c
    local, sends, recvs = [], [], []
    for a in range(len(ins)):
        local.append(pltpu.make_async_copy(ins[a] if gather else ins[a].at[me], outs[a].at[me], local_sems.at[a]))
    for k in range(1, N_DEV):
        px, py, pc = x ^ (k >> 2), y ^ ((k >> 1) & 1), c ^ (k & 1)
        peer = 4 * px + 2 * py + pc
        for a in range(len(ins)):
            sems = dict(send_sem=send_sems.at[a, k - 1], recv_sem=recv_sems.at[a, k - 1], device_id_type=MESH)
            sends.append(pltpu.make_async_remote_copy(
                src_ref=ins[a] if gather else ins[a].at[peer], dst_ref=outs[a].at[me], device_id=(px, py, pc), **sems))
            recvs.append(pltpu.make_async_remote_copy(
                src_ref=ins[a] if gather else ins[a].at[me], dst_ref=outs[a].at[peer], device_id=(x, y, c), **sems))
    return local, sends, recvs


def _xchg_start(cps):
    local, sends, _ = cps
    for cp in local + sends:
        cp.start()


def _xchg_finish(cps):
    local, sends, recvs = cps
    for cp in recvs:
        cp.wait_recv()
    for cp in sends:
        cp.wait_send()
    for cp in local:
        cp.wait()


def _xchg_out_shapes(ins, gather):
    return [jax.ShapeDtypeStruct(((N_DEV,) + s.shape) if gather else s.shape, s.dtype) for s in ins]


def _attn_specs(E):
    nb = (E - CTX) // WIN
    last = E // WIN - 1
    kc = pl.BlockSpec((CTX, KP), lambda n: (0, 0))
    kp = pl.BlockSpec((WIN, KP), lambda n: (n + 1, 0))
    kk = pl.BlockSpec((WIN, KP), lambda n: (n + 2, 0))
    kn = pl.BlockSpec((WIN, KP), lambda n: (jnp.minimum(n + 3, last), 0))
    return nb, [kc, kp, kk, kn]


def _attn_valid(n, nb):
    rows = lax.broadcasted_iota(jnp.int32, (GRP * WIN, CTX + 3 * WIN), 0) & (WIN - 1)
    cols = lax.broadcasted_iota(jnp.int32, (GRP * WIN, CTX + 3 * WIN), 1)
    j = cols - CTX
    rel = j - WIN - rows
    return (cols < CTX) | ((jnp.abs(rel) <= WIN) & ((j >= WIN) | (n > 0)) & ((j < 2 * WIN) | (n < nb - 1)))


def _attn_fwd(q, k, v, sink, shards):
    E = q.shape[0]
    S = E - CTX
    nb, kspecs = _attn_specs(E)
    na = len(shards)

    def body(q_ref, kc, kp, kk, kn, vc, vp, vk, vn, sink_ref, *rest):
        shard_refs, (o_ref, lse_ref), got_refs = rest[:na], rest[na:na + 2], rest[na + 2:2 * na + 2]
        n = pl.program_id(0)
        cps = _xchg_copies(shard_refs, got_refs, *rest[2 * na + 2:], gather=True)

        @pl.when(n == 0)
        def _():
            _xchg_start(cps)

        valid = _attn_valid(n, nb)
        lane = lax.broadcasted_iota(jnp.int32, (WIN, 128), 1)
        lse_t = jnp.zeros((WIN, 128), F32)
        for h in range(N_KV):
            hs = slice(128 * h, 128 * h + 128)
            K = jnp.concatenate([kc[:, hs], kp[:, hs], kk[:, hs], kn[:, hs]], axis=0)
            V = jnp.concatenate([vc[:, hs], vp[:, hs], vk[:, hs], vn[:, hs]], axis=0)
            Q = jnp.concatenate([q_ref[:, 128 * (GRP * h + g):128 * (GRP * h + g) + 128] for g in range(GRP)], axis=0)
            sk = jnp.concatenate([jnp.broadcast_to(sink_ref[GRP * h + g:GRP * h + g + 1, 0:1], (WIN, 1))
                                  for g in range(GRP)], axis=0)
            s = jnp.where(valid, _nt(Q, K), NEG)
            m = jnp.maximum(jnp.max(s, axis=1, keepdims=True), sk)
            e = jnp.exp(s - m)
            den = jnp.sum(e, axis=1, keepdims=True) + jnp.exp(sk - m)
            o = _nn((e / den).astype(_BF), V)
            lse = m + jnp.log(den)
            for g in range(GRP):
                lse_t = jnp.where(lane == GRP * h + g, lse[WIN * g:WIN * g + WIN], lse_t)
            for pp in range(GRP // 2):
                a = o[WIN * 2 * pp:WIN * 2 * pp + WIN]
                b = o[WIN * (2 * pp + 1):WIN * (2 * pp + 1) + WIN]
                t = 2 * h + pp
                o_ref[:, 128 * t:128 * t + 128] = (a + pltpu.roll(b, 64, 1)).astype(_BF)
        lse_ref[...] = lse_t

        @pl.when(n == nb - 1)
        def _():
            _xchg_finish(cps)

    qs = pl.BlockSpec((WIN, QP), lambda n: (n + 2, 0))
    anyspec = pl.BlockSpec(memory_space=pl.ANY)
    return pl.pallas_call(
        body, name="attn_fwd", grid=(nb,),
        in_specs=[qs] + kspecs + kspecs + [_full((8, 128))] + [anyspec] * na,
        out_specs=[_rows(WIN, 512), _rows(WIN, 128)] + [anyspec] * na,
        out_shape=[jax.ShapeDtypeStruct((S, 512), _BF), jax.ShapeDtypeStruct((S, 128), F32)]
        + _xchg_out_shapes(shards, True),
        scratch_shapes=_xchg_scratch(na),
        compiler_params=_cp(("arbitrary",)),
    )(q, k, k, k, k, v, v, v, v, sink, *shards)


def _attn_bwd(q, k, v, sink, lse, d_attn, slabs):
    E = q.shape[0]
    S = E - CTX
    nb, kspecs = _attn_specs(E)
    last = E // WIN - 1
    na = len(slabs)

    def body(q_ref, kc, kp, kk, kn, vc, vp, vk, vn, sink_ref, lse_ref, do_ref, *rest):
        slab_refs, (dq_ref, dk_ref, dv_ref, ds_ref), got_refs = rest[:na], rest[na:na + 4], rest[na + 4:2 * na + 4]
        n = pl.program_id(0)
        cps = _xchg_copies(slab_refs, got_refs, *rest[2 * na + 4:], gather=False)

        @pl.when(n == 0)
        def _():
            _xchg_start(cps)
            dk_ref[...] = jnp.zeros_like(dk_ref)
            dv_ref[...] = jnp.zeros_like(dv_ref)
            ds_ref[...] = jnp.zeros_like(ds_ref)

        valid = _attn_valid(n, nb)
        lane = lax.broadcasted_iota(jnp.int32, (WIN, 128), 1)
        lse_t = lse_ref[...]
        starts = [None, pl.multiple_of((n + 1) * WIN, WIN), pl.multiple_of((n + 2) * WIN, WIN),
                  pl.multiple_of(jnp.minimum(n + 3, last) * WIN, WIN)]
        for h in range(N_KV):
            hs = slice(128 * h, 128 * h + 128)
            K = jnp.concatenate([kc[:, hs], kp[:, hs], kk[:, hs], kn[:, hs]], axis=0)
            V = jnp.concatenate([vc[:, hs], vp[:, hs], vk[:, hs], vn[:, hs]], axis=0)
            Q = jnp.concatenate([q_ref[:, 128 * (GRP * h + g):128 * (GRP * h + g) + 128] for g in range(GRP)], axis=0)
            sk = jnp.concatenate([jnp.broadcast_to(sink_ref[GRP * h + g:GRP * h + g + 1, 0:1], (WIN, 1))
                                  for g in range(GRP)], axis=0)
            ls = jnp.concatenate([jnp.sum(jnp.where(lane == GRP * h + g, lse_t, 0.0), axis=1, keepdims=True)
                                  for g in range(GRP)], axis=0)
            dos = []
            for g in range(GRP):
                j = GRP * h + g
                t = do_ref[:, 128 * (j // 2):128 * (j // 2) + 128].astype(F32)
                if j % 2:
                    t = pltpu.roll(t, 64, 1)
                dos.append(jnp.where(lane < HD, t, 0.0))
            do = jnp.concatenate(dos, axis=0).astype(_BF)
            s = _nt(Q, K)
            p = jnp.where(valid, jnp.exp(s - ls), 0.0)
            dp = _nt(do, V)
            delta = jnp.sum(p * dp, axis=1, keepdims=True)
            dsc = (p * (dp - delta)).astype(_BF)
            dq = _nn(dsc, K) * (HD ** -0.5)
            for g in range(GRP):
                j = GRP * h + g
                dq_ref[:, 128 * j:128 * j + 128] = dq[WIN * g:WIN * g + WIN].astype(_BF)
            dK = _tn(dsc, Q)
            dV = _tn(p.astype(_BF), do)
            dk_ref[0:CTX, hs] += dK[0:CTX]
            dv_ref[0:CTX, hs] += dV[0:CTX]
            for w in range(1, 4):
                lo = CTX + WIN * (w - 1)
                dk_ref[pl.ds(starts[w], WIN), hs] += dK[lo:lo + WIN]
                dv_ref[pl.ds(starts[w], WIN), hs] += dV[lo:lo + WIN]
            psk = -jnp.exp(sk - ls) * delta
            for g in range(GRP):
                j = GRP * h + g
                tot = jnp.sum(psk[WIN * g:WIN * g + WIN], axis=0, keepdims=True)
                ds_ref[j:j + 1, :] += jnp.broadcast_to(tot, (1, 128))

        @pl.when(n == nb - 1)
        def _():
            _xchg_finish(cps)

    qs = pl.BlockSpec((WIN, QP), lambda n: (n + 2, 0))
    anyspec = pl.BlockSpec(memory_space=pl.ANY)
    return pl.pallas_call(
        body, name="attn_bwd", grid=(nb,),
        in_specs=[qs] + kspecs + kspecs + [_full((8, 128)), _rows(WIN, 128), _rows(WIN, 512)] + [anyspec] * na,
        out_specs=[_rows(WIN, QP), _full((E, KP)), _full((E, KP)), _full((8, 128))] + [anyspec] * na,
        out_shape=[jax.ShapeDtypeStruct((S, QP), _BF), jax.ShapeDtypeStruct((E, KP), F32),
                   jax.ShapeDtypeStruct((E, KP), F32), jax.ShapeDtypeStruct((8, 128), F32)]
        + _xchg_out_shapes(slabs, False),
        scratch_shapes=_xchg_scratch(na),
        compiler_params=_cp(("arbitrary",), 48 * 1024 * 1024),
    )(q, k, k, k, k, v, v, v, v, sink, lse, d_attn, *slabs)


def _gla_order(E, reverse, backward):
    nc = CTX // GLA_T
    n = E // GLA_T
    if not reverse:
        fwd = lambda s: s
    else:
        fwd = lambda s: jnp.where(s < nc, nc - 1 - s, n - 1 + nc - s)
    if backward:
        return lambda s: fwd(n - 1 - s)
    return fwd


def _gla_masks():
    T = GLA_T
    l128 = lax.broadcasted_iota(jnp.int32, (1, 128), 1)
    qmask = [((l128 >> 5) == j).astype(F32) for j in range(4)]
    vmask = [((l128 >> 6) == j).astype(F32) for j in range(2)]
    bd = ((lax.broadcasted_iota(jnp.int32, (512, 256), 0) >> 6)
          == (lax.broadcasted_iota(jnp.int32, (512, 256), 1) >> 5)).astype(F32)
    ri = lax.broadcasted_iota(jnp.int32, (T, 2 * T), 0)
    ci = lax.broadcasted_iota(jnp.int32, (T, 2 * T), 1) & (T - 1)
    return qmask, vmask, bd, ri, ci


def _tri_sum(tri, x):
    hi = x.astype(_BF)
    r1 = x - hi.astype(F32)
    mid = r1.astype(_BF)
    lo = (r1 - mid.astype(F32)).astype(_BF)
    n = x.shape[1]
    y = _nn(tri.astype(_BF), jnp.concatenate([hi, mid, lo], axis=1))
    return y[:, 0:n] + y[:, n:2 * n] + y[:, 2 * n:3 * n]


def _gla_decays(la, reverse, ri, ci):
    T = GLA_T
    msk2 = (ri <= ci) if reverse else (ri >= ci)
    mskT2 = (ri >= ci) if reverse else (ri <= ci)
    b = _tri_sum(msk2[:, 0:T], la)
    bT = b[0:1] if reverse else b[T - 1:T]
    bm = b[T // 2:T // 2 + 1]
    return msk2, mskT2, b, bT, bm


def _pair_stack(tile, m0, m1):
    return jnp.concatenate([(tile * m0).astype(_BF), (tile * m1).astype(_BF)], axis=0)


def _gla_fwd(gq, gk, gv, la, reverse):
    E = gq.shape[0]
    T = GLA_T
    n = E // T
    order = _gla_order(E, reverse, False)
    col = 1 if reverse else 0

    def body(gq_ref, gk_ref, gv_ref, la_ref, o_ref, st_ref, S_scr):
        @pl.when(pl.program_id(0) == 0)
        def _():
            S_scr[...] = jnp.zeros_like(S_scr)

        qmask, vmask, bd, ri, ci = _gla_masks()
        msk2, _, b, bT, bm = _gla_decays(la_ref[...], reverse, ri, ci)
        q, k, v = gq_ref[...], gk_ref[...], gv_ref[...]
        qd = (q * jnp.exp(b)).astype(_BF)
        qm = (q * jnp.exp(b - bm)).astype(_BF)
        km = k * jnp.exp(bm - b)
        kd = (k * jnp.exp(bT - b)).astype(_BF)
        ST = S_scr[...]
        comp = ST[0:DV]
        for h in range(1, N_GLA):
            comp = comp + ST[DV * h:DV * h + DV]
        st_ref[0] = comp
        inter = _nt(qd, ST.astype(_BF))
        tiles = []
        for p in range(N_GLA // 2):
            qs = slice(128 * (p // 2), 128 * (p // 2) + 128)
            vs = slice(128 * p, 128 * p + 128)
            j0 = (2 * p) % 4
            KS = _pair_stack(km[:, qs], qmask[j0], qmask[j0 + 1])
            VS = _pair_stack(v[:, vs], vmask[0], vmask[1])
            AA = jnp.where(msk2, _nt(qm[:, qs], KS), 0.0).astype(_BF)
            tiles.append(_nn(AA, VS))
        o_ref[...] = inter + jnp.concatenate(tiles, axis=1)
        S_scr[...] = ST * jnp.exp(bT) + bd * _tn(v.astype(_BF), kd)

    blk = lambda w, c=0: pl.BlockSpec((T, w), lambda s: (order(s), c))
    return pl.pallas_call(
        body, name="gla_fwd_rev" if reverse else "gla_fwd", grid=(n,),
        in_specs=[blk(256), blk(256), blk(512), blk(256, col)],
        out_specs=[blk(512), pl.BlockSpec((1, DV, 256), lambda s: (order(s), 0, 0))],
        out_shape=[jax.ShapeDtypeStruct((E, 512), F32), jax.ShapeDtypeStruct((n, DV, 256), F32)],
        scratch_shapes=[pltpu.VMEM((512, 256), F32)],
        compiler_params=_cp(("arbitrary",)),
    )(gq, gk, gv, la)


def _gla_bwd(gq, gk, gv, la, st, do, reverse):
    E = gq.shape[0]
    T = GLA_T
    n = E // T
    nc = CTX // T
    order = _gla_order(E, reverse, True)
    col = 1 if reverse else 0

    def body(gq_ref, gk_ref, gv_ref, la_ref, st_ref, do_ref, dq_ref, dk_ref, dv_ref, dla_ref, dS_scr):
        @pl.when(pl.program_id(0) == 0)
        def _():
            dS_scr[...] = jnp.zeros_like(dS_scr)

        is_lat = order(pl.program_id(0)) >= nc
        qmask, vmask, bd, ri, ci = _gla_masks()
        msk2, mskT2, b, bT, bm = _gla_decays(la_ref[...], reverse, ri, ci)
        q, k, v = gq_ref[...], gk_ref[...], gv_ref[...]
        do = jnp.where(is_lat, do_ref[...].astype(F32), 0.0)
        e_b, e_qm, e_km, e_kd, e_T = jnp.exp(b), jnp.exp(b - bm), jnp.exp(bm - b), jnp.exp(bT - b), jnp.exp(bT)
        qd, qm, km, kd = q * e_b, q * e_qm, k * e_km, k * e_kd
        qdb, qmb, kmb, kdb, vb, dob = (t.astype(_BF) for t in (qd, qm, km, kd, v, do))
        ST = jnp.tile(st_ref[0], (N_GLA, 1)) * bd
        dST = dS_scr[...]
        dSTb = dST.astype(_BF)
        dqd = _nn(dob, ST.astype(_BF))
        dkd = _nn(vb, dSTb)
        dv_t, dqm_t, dkm_t = [], [None, None], [None, None]
        for p in range(N_GLA // 2):
            t = p // 2
            qs = slice(128 * t, 128 * t + 128)
            vs = slice(128 * p, 128 * p + 128)
            j0 = (2 * p) % 4
            QS = _pair_stack(qm[:, qs], qmask[j0], qmask[j0 + 1])
            KS = _pair_stack(km[:, qs], qmask[j0], qmask[j0 + 1])
            VS = _pair_stack(v[:, vs], vmask[0], vmask[1])
            DS = _pair_stack(do[:, vs], vmask[0], vmask[1])
            ATT = jnp.where(mskT2, _nt(kmb[:, qs], QS), 0.0).astype(_BF)
            dAA = jnp.where(msk2, _nt(dob[:, vs], VS), 0.0).astype(_BF)
            dATT = jnp.where(mskT2, _nt(vb[:, vs], DS), 0.0).astype(_BF)
            dv_t.append(_nn(ATT, DS))
            dq_p = _nn(dAA, KS)
            dk_p = _nn(dATT, QS)
            dqm_t[t] = dq_p if dqm_t[t] is None else dqm_t[t] + dq_p
            dkm_t[t] = dk_p if dkm_t[t] is None else dkm_t[t] + dk_p
        dqm = jnp.concatenate(dqm_t, axis=1)
        dkm = jnp.concatenate(dkm_t, axis=1)
        dq_ref[...] = dqm * e_qm + dqd * e_b
        dk_ref[...] = dkm * e_km + dkd * e_kd
        dv_ref[...] = _nt(kdb, dSTb) + jnp.concatenate(dv_t, axis=1)
        db = dqm * qm - dkm * km + dqd * qd - dkd * kd
        dbT = jnp.sum(dkd * kd, axis=0, keepdims=True) + e_T * jnp.sum(dST * ST, axis=0, keepdims=True)
        dla_ref[...] = _tri_sum(mskT2[:, 0:T], db) + dbT
        dS_scr[...] = dST * e_T + bd * _tn(dob, qdb)

    blk = lambda w, c=0: pl.BlockSpec((T, w), lambda s: (order(s), c))
    do_spec = pl.BlockSpec((T, 512), lambda s: (jnp.maximum(order(s) - nc, 0), 0))
    return pl.pallas_call(
        body, name="gla_bwd_rev" if reverse else "gla_bwd", grid=(n,),
        in_specs=[blk(256), blk(256), blk(512), blk(256, col),
                  pl.BlockSpec((1, DV, 256), lambda s: (order(s), 0, 0)), do_spec],
        out_specs=[blk(256), blk(256), blk(512), blk(256)],
        out_shape=[jax.ShapeDtypeStruct((E, 256), F32), jax.ShapeDtypeStruct((E, 256), F32),
                   jax.ShapeDtypeStruct((E, 512), F32), jax.ShapeDtypeStruct((E, 256), F32)],
        scratch_shapes=[pltpu.VMEM((512, 256), F32)],
        compiler_params=_cp(("arbitrary",)),
    )(gq, gk, gv, la, st, do)


def _gla_out(o_f, o_b, gg, ggla, mavg):
    o = o_f + o_b
    rr = lax.rsqrt(_head_mean(o * o, mavg) + EPS)
    oh = o * rr
    sg = _sigmoid(gg)
    return oh, rr, sg


def _mix_fwd(x, attn, o_f, o_b, gg, ggla, mavg, wout, gt1, g2):
    S = x.shape[0]
    TM = 256

    def body(x_ref, a_ref, of_ref, ob_ref, gg_ref, ggla_ref, mavg_ref, w_ref, gt1_ref, g2_ref, x1_ref, mix_ref):
        gg_t = gg_ref[...]
        oh, _, sg = _gla_out(of_ref[...], ob_ref[...], gg_t, ggla_ref[...], mavg_ref[...])
        mix_ref[:, 0:512] = a_ref[...]
        mix_ref[:, 512:1024] = (oh * ggla_ref[...] * (gg_t * sg)).astype(_BF)
        y = _nn(mix_ref[...], w_ref[...])
        ry = lax.rsqrt(jnp.mean(y * y, axis=-1, keepdims=True) + EPS)
        x1_ref[...] = x_ref[...] + gt1_ref[...] * ((y * ry) * g2_ref[...])

    return pl.pallas_call(
        body, name="mix_fwd", grid=(S // TM,),
        in_specs=[_rows(TM, D), _rows(TM, 512), _rows(TM, 512, 1), _rows(TM, 512, 1), _rows(TM, 512, 1),
                  _full((1, 512)), _full((512, 512)), _full((D, D)), _full((1, D)), _full((1, D))],
        out_specs=[_rows(TM, D), _rows(TM, D)],
        out_shape=[jax.ShapeDtypeStruct((S, D), F32), jax.ShapeDtypeStruct((S, D), _BF)],
        compiler_params=_cp(("arbitrary",), 40 * 1024 * 1024),
    )(x, attn, o_f, o_b, gg, ggla, mavg, wout, gt1, g2)


def _mix_bwd(dx1, mix, o_f, o_b, gg, ggla, mavg, wout, gt1, g2):
    S = dx1.shape[0]
    TM = 256

    def body(dx_ref, mix_ref, of_ref, ob_ref, gg_ref, ggla_ref, mavg_ref, w_ref, gt1_ref, g2_ref,
             da_ref, do_ref, dgg_ref, dy_ref, sums_ref):
        @pl.when(pl.program_id(0) == 0)
        def _():
            sums_ref[...] = jnp.zeros_like(sums_ref)

        dx = dx_ref[...]
        y = _nn(mix_ref[...], w_ref[...])
        ry = lax.rsqrt(jnp.mean(y * y, axis=-1, keepdims=True) + EPS)
        yh = y * ry
        sums_ref[0:1, :] += jnp.sum(dx * yh, axis=0, keepdims=True)
        dyh = dx * (gt1_ref[...] * g2_ref[...])
        dy = (ry * (dyh - yh * jnp.mean(dyh * yh, axis=-1, keepdims=True))).astype(_BF)
        dy_ref[...] = dy
        dmix = _nt(dy, w_ref[...])
        da_ref[...] = dmix[:, 0:512].astype(_BF)
        dgla = dmix[:, 512:1024]
        gg_t = gg_ref[...]
        ggla_t = ggla_ref[...]
        oh, rr, sg = _gla_out(of_ref[...], ob_ref[...], gg_t, ggla_t, mavg_ref[...])
        dgg_ref[...] = (dgla * oh * ggla_t * (sg * (1.0 + gg_t * (1.0 - sg)))).astype(_BF)
        don = dgla * (gg_t * sg)
        sums_ref[1:2, 0:512] += jnp.sum(don * oh, axis=0, keepdims=True)
        doh = don * ggla_t
        do_ref[...] = (rr * (doh - oh * _head_mean(doh * oh, mavg_ref[...]))).astype(_BF)

    return pl.pallas_call(
        body, name="mix_bwd", grid=(S // TM,),
        in_specs=[_rows(TM, D), _rows(TM, D), _rows(TM, 512, 1), _rows(TM, 512, 1), _rows(TM, 512, 1),
                  _full((1, 512)), _full((512, 512)), _full((D, D)), _full((1, D)), _full((1, D))],
        out_specs=[_rows(TM, 512), _rows(TM, 512), _rows(TM, 512), _rows(TM, D), _full((8, D))],
        out_shape=[jax.ShapeDtypeStruct((S, 512), _BF), jax.ShapeDtypeStruct((S, 512), _BF),
                   jax.ShapeDtypeStruct((S, 512), _BF), jax.ShapeDtypeStruct((S, D), _BF),
                   jax.ShapeDtypeStruct((8, D), F32)],
        compiler_params=_cp(("arbitrary",), 40 * 1024 * 1024),
    )(dx1, mix, o_f, o_b, gg, ggla, mavg, wout, gt1, g2)


def _ffn(x1, target, gm2, sh2, gt2, g4, wffi, wffo):
    S = x1.shape[0]
    TF = 256

    def body(x_ref, t_ref, gm_ref, sh_ref, gt_ref, g4_ref, wi_hbm, wo_hbm,
             dx_ref, h_ref, du_ref, act_ref, df_ref, sums_ref, loss_ref, wi, wo, sem):
        @pl.when(pl.program_id(0) == 0)
        def _():
            c1 = pltpu.make_async_copy(wi_hbm, wi, sem.at[0])
            c2 = pltpu.make_async_copy(wo_hbm, wo, sem.at[1])
            c1.start()
            c2.start()
            sums_ref[...] = jnp.zeros_like(sums_ref)
            loss_ref[...] = jnp.zeros_like(loss_ref)
            c1.wait()
            c2.wait()

        x = x_ref[...]
        gm = gm_ref[...]
        r = lax.rsqrt(jnp.mean(x * x, axis=-1, keepdims=True) + EPS)
        xh = x * r
        hb = (xh * gm + sh_ref[...]).astype(_BF)
        h_ref[...] = hb
        u = _nn(hb, wi[...])
        g = u[:, 0:FFN]
        up = u[:, FFN:2 * FFN]
        sg = _sigmoid(g)
        sl = g * sg
        ab = (sl * up).astype(_BF)
        act_ref[...] = ab
        f = _nn(ab, wo[...])
        rf = lax.rsqrt(jnp.mean(f * f, axis=-1, keepdims=True) + EPS)
        fh = f * rf
        gt, g4v = gt_ref[...], g4_ref[...]
        err = x + gt * (fh * g4v) - t_ref[...]
        loss_ref[...] += jnp.sum(err * err) * (0.5 / D)
        dout = err * (1.0 / D)
        sums_ref[2:3, :] += jnp.sum(dout * fh, axis=0, keepdims=True)
        dfh = dout * (gt * g4v)
        dfb = (rf * (dfh - fh * jnp.mean(dfh * fh, axis=-1, keepdims=True))).astype(_BF)
        df_ref[...] = dfb
        dact = _nt(dfb, wo[...])
        du_ref[:, 0:FFN] = (dact * up * (sg * (1.0 + g * (1.0 - sg)))).astype(_BF)
        du_ref[:, FFN:2 * FFN] = (dact * sl).astype(_BF)
        dh = _nt(du_ref[...], wi[...])
        sums_ref[0:1, :] += jnp.sum(dh, axis=0, keepdims=True)
        sums_ref[1:2, :] += jnp.sum(dh * xh, axis=0, keepdims=True)
        dxh = dh * gm
        dx_ref[...] = dout + r * (dxh - xh * jnp.mean(dxh * xh, axis=-1, keepdims=True))

    vec = _full((1, D))
    anyspec = pl.BlockSpec(memory_space=pl.ANY)
    return pl.pallas_call(
        body, name="ffn_fwd_bwd", grid=(S // TF,),
        in_specs=[_rows(TF, D), _rows(TF, D), vec, vec, vec, vec, anyspec, anyspec],
        out_specs=[_rows(TF, D), _rows(TF, D), _rows(TF, 2 * FFN), _rows(TF, FFN), _rows(TF, D),
                   _full((8, D)), _full((8, 128))],
        out_shape=[jax.ShapeDtypeStruct((S, D), F32), jax.ShapeDtypeStruct((S, D), _BF),
                   jax.ShapeDtypeStruct((S, 2 * FFN), _BF), jax.ShapeDtypeStruct((S, FFN), _BF),
                   jax.ShapeDtypeStruct((S, D), _BF), jax.ShapeDtypeStruct((8, D), F32),
                   jax.ShapeDtypeStruct((8, 128), F32)],
        scratch_shapes=[pltpu.VMEM((D, 2 * FFN), _BF), pltpu.VMEM((FFN, D), _BF), pltpu.SemaphoreType.DMA((2,))],
        compiler_params=_cp(("arbitrary",), VMEM_BIG),
    )(x1, target, gm2, sh2, gt2, g4, wffi, wffo)


def _inproj_bwd(x, ctx, gml, gmc, win, wg, cos, sa, sb, la, dq, dk, dv, dgq_f, dgq_b, dgk_f, dgk_b, dgv_f, dgv_b,
                dgg, dla_f, dla_b, dx1):
    S = x.shape[0]
    E = S + CTX
    TE = CTX

    def body(x_ref, c_ref, gml_ref, gmc_ref, w_ref, wg_ref, cos_ref, sa_ref, sb_ref, la_ref, dq_ref, dk_ref, dv_ref,
             gqf, gqb, gkf, gkb, gvf, gvb, dgg_ref, dlf, dlb, dx1_ref,
             dp_ref, dlg_ref, gx_ref, sums_ref, bsum_ref):
        i = pl.program_id(0)
        is_ctx = i == 0

        @pl.when(is_ctx)
        def _():
            sums_ref[...] = jnp.zeros_like(sums_ref)
            bsum_ref[...] = jnp.zeros_like(bsum_ref)

        lat = jnp.where(is_ctx, 0.0, 1.0)
        cos_t, sa_t, sb_t = cos_ref[...], sa_ref[...], sb_ref[...]
        dp_ref[:, O_Q:O_K] = (_unrope(dq_ref[...].astype(F32), cos_t, sa_t, sb_t) * lat).astype(_BF)
        dp_ref[:, O_K:O_V] = _unrope(dk_ref[...], cos_t, sa_t, sb_t).astype(_BF)
        dp_ref[:, O_V:O_GQ] = dv_ref[...].astype(_BF)
        dp_ref[:, O_GQ:O_GK] = ((gqf[...] + gqb[...]) * (DK ** -0.5)).astype(_BF)
        dp_ref[:, O_GK:O_GV] = (gkf[...] + gkb[...]).astype(_BF)
        dp_ref[:, O_GV:O_GG] = (gvf[...] + gvb[...]).astype(_BF)
        dp_ref[:, O_GG:O_Z] = (dgg_ref[...].astype(F32) * lat).astype(_BF)
        la_t = la_ref[...]
        dlg = (jnp.concatenate([dlf[...], dlb[...]], axis=1) * (1.0 - jnp.exp(GATE_TAU * la_t)) * (1.0 / GATE_TAU))
        bsum_ref[0:1, :] += jnp.sum(dlg, axis=0, keepdims=True)
        dlgb = dlg.astype(_BF)
        dlg_ref[...] = dlgb
        dp_ref[:, O_Z:NP] = _nt(dlgb, wg_ref[...]).astype(_BF)
        dh = _nt(dp_ref[...], w_ref[...])
        x = jnp.where(is_ctx, c_ref[...], x_ref[...])
        r = lax.rsqrt(jnp.mean(x * x, axis=-1, keepdims=True) + EPS)
        xh = x * r
        sdh = jnp.sum(dh, axis=0, keepdims=True)
        sdx = jnp.sum(dh * xh, axis=0, keepdims=True)
        sums_ref[0:1, :] += sdh * lat
        sums_ref[1:2, :] += sdx * lat
        sums_ref[2:3, :] += sdh * (1.0 - lat)
        sums_ref[3:4, :] += sdx * (1.0 - lat)
        dxh = dh * jnp.where(is_ctx, gmc_ref[...], gml_ref[...])
        gx_ref[...] = dx1_ref[...] + r * (dxh - xh * jnp.mean(dxh * xh, axis=-1, keepdims=True))

    vec = _full((1, D))
    tab = _rows(TE, 128)
    return pl.pallas_call(
        body, name="inproj_bwd", grid=(E // TE,),
        in_specs=[_rows_lat(TE, D), _full((CTX, D)), vec, vec, _full((D, NP)), _full((128, 512)), tab, tab, tab,
                  _rows(TE, 512),
                  _rows_lat(TE, QP), _rows(TE, KP), _rows(TE, KP),
                  _rows(TE, 256), _rows(TE, 256), _rows(TE, 256), _rows(TE, 256), _rows(TE, 512), _rows(TE, 512),
                  _rows_lat(TE, 512), _rows(TE, 256), _rows(TE, 256), _rows_lat(TE, D)],
        out_specs=[_rows(TE, NP), _rows(TE, 512), _rows_lat(TE, D), _full((8, D)), _full((8, 512))],
        out_shape=[jax.ShapeDtypeStruct((E, NP), _BF), jax.ShapeDtypeStruct((E, 512), _BF),
                   jax.ShapeDtypeStruct((S, D), F32), jax.ShapeDtypeStruct((8, D), F32),
                   jax.ShapeDtypeStruct((8, 512), F32)],
        compiler_params=_cp(("arbitrary",), VMEM_BIG),
    )(x, ctx, gml, gmc, win, wg, cos, sa, sb, la, dq, dk, dv, dgq_f, dgq_b, dgk_f, dgk_b, dgv_f, dgv_b,
      dgg, dla_f, dla_b, dx1)


def _matmul_tn(a, b, tk, tn, tt, name):
    T, KA = a.shape
    N = b.shape[1]

    def body(a_ref, b_ref, o_ref):
        @pl.when(pl.program_id(2) == 0)
        def _():
            o_ref[...] = jnp.zeros_like(o_ref)

        o_ref[...] += _tn(a_ref[...], b_ref[...])

    return pl.pallas_call(
        body, name=name, grid=(KA // tk, N // tn, T // tt),
        in_specs=[pl.BlockSpec((tt, tk), lambda i, j, t: (t, i)), pl.BlockSpec((tt, tn), lambda i, j, t: (t, j))],
        out_specs=pl.BlockSpec((tk, tn), lambda i, j, t: (i, j)),
        out_shape=jax.ShapeDtypeStruct((KA, N), F32),
        compiler_params=_cp(("parallel", "parallel", "arbitrary"), VMEM_BIG),
    )(a, b)


def _ada_fwd(c_all, c_ctx, w_ada):
    n = w_ada.shape[1]

    def body(c_ref, cc_ref, w_ref, o_ref):
        c = jnp.concatenate([c_ref[...], jnp.broadcast_to(cc_ref[...], (8, D))], axis=0)
        o_ref[...] = _nn((c * _sigmoid(c)).astype(_BF), w_ref[...].astype(_BF))

    return pl.pallas_call(
        body, name="ada_fwd", in_specs=[_full((8, D)), _full((1, D)), _full((D, n))], out_specs=_full((16, n)),
        out_shape=jax.ShapeDtypeStruct((16, n), F32), grid=(1,), compiler_params=_cp(("arbitrary",)),
    )(c_all, c_ctx, w_ada)


def _ada_bwd(c_all, c_ctx, w_ada, d_all):
    n = w_ada.shape[1]

    def body(c_ref, cc_ref, w_ref, d_ref, gw_ref, t_ref):
        c = jnp.concatenate([c_ref[...], jnp.broadcast_to(cc_ref[...], (8, D))], axis=0)
        db = d_ref[...].astype(_BF)
        gw_ref[0] = _tn((c * _sigmoid(c)).astype(_BF), db)
        t_ref[...] = _nt(db[8:16], w_ref[...].astype(_BF))

    return pl.pallas_call(
        body, name="ada_bwd", in_specs=[_full((8, D)), _full((1, D)), _full((D, n)), _full((16, n))],
        out_specs=[_full((1, D, n)), _full((8, D))],
        out_shape=[jax.ShapeDtypeStruct((1, D, n), F32), jax.ShapeDtypeStruct((8, D), F32)], grid=(1,),
        compiler_params=_cp(("arbitrary",)),
    )(c_all, c_ctx, w_ada, d_all)


def _small_grads(s_in, s_ffn, s_mix, ada_l, ada_c, gains):
    def body(si, sf, sm, al, ac, g, o_ref):
        sdh_l, sdx_l, sdh_c, sdx_c = si[0:1], si[1:2], si[2:3], si[3:4]
        sdh2, sdx2, a2 = sf[0:1], sf[1:2], sf[2:3]
        a1 = sm[0:1]
        g1, g2, g3, g4 = g[0:1], g[1:2], g[2:3], g[3:4]
        sc1, gt1, sc2, gt2 = al[1:2], al[2:3], al[4:5], al[5:6]
        sc1c = ac[1:2]
        z = jnp.zeros((1, D), F32)
        rows = [sdh_l, sdx_l * g1, a1 * g2, sdh2, sdx2 * g3, a2 * g4,
                sdh_c, sdx_c * g1, z, z, z, z,
                sdx_l * (1.0 + sc1) + sdx_c * (1.0 + sc1c), a1 * gt1, sdx2 * (1.0 + sc2), a2 * gt2]
        for r, v in enumerate(rows):
            o_ref[r:r + 1, :] = v

    v8 = _full((8, D))
    return pl.pallas_call(
        body, name="small_grads", in_specs=[v8] * 6, out_specs=_full((16, D)),
        out_shape=jax.ShapeDtypeStruct((16, D), F32), grid=(1,), compiler_params=_cp(("arbitrary",)),
    )(s_in, s_ffn, s_mix, ada_l, ada_c, gains)


def _row_tile(R):
    for cand in (256, 128, 64, 32, 16):
        if R % cand == 0 and R > cand:
            return cand
    return R


def _adamw(w, g, m, v, name):
    _, R, C = w.shape
    tr = _row_tile(R)
    ns = g.shape[0]
    c1 = 1.0 / (1.0 - ADAM_B1 ** ADAM_STEP)
    c2 = 1.0 / (1.0 - ADAM_B2 ** ADAM_STEP)

    def body(w_ref, g_ref, m_ref, v_ref, go_ref, d_ref, nm_ref, nv_ref):
        gg = g_ref[0].astype(F32)
        for j in range(1, ns):
            gg = gg + g_ref[j].astype(F32)
        go_ref[0] = gg
        nm = ADAM_B1 * m_ref[0] + (1.0 - ADAM_B1) * gg
        nv = ADAM_B2 * v_ref[0] + (1.0 - ADAM_B2) * (gg * gg)
        nm_ref[0] = nm
        nv_ref[0] = nv
        d_ref[0] = -ADAM_LR * ((nm * c1) / (jnp.sqrt(nv * c2) + ADAM_EPS) + ADAM_WD * w_ref[0])

    spec = pl.BlockSpec((1, tr, C), lambda i: (0, i, 0))
    sds = jax.ShapeDtypeStruct((1, R, C), F32)
    return pl.pallas_call(
        body, name=name, grid=(R // tr,),
        in_specs=[spec, pl.BlockSpec((ns, tr, C), lambda i: (0, i, 0)), spec, spec],
        out_specs=[spec] * 4, out_shape=[sds] * 4, compiler_params=_cp(("parallel",)),
    )(w, g, m, v)


def _sum_slots(slots, name):
    _, R, C = slots.shape
    tr = _row_tile(R)

    def body(s_ref, o_ref):
        acc = s_ref[0].astype(F32)
        for j in range(1, N_DEV):
            acc = acc + s_ref[j].astype(F32)
        o_ref[...] = acc

    return pl.pallas_call(
        body, name=name, grid=(R // tr,), in_specs=[pl.BlockSpec((N_DEV, tr, C), lambda i: (0, i, 0))],
        out_specs=_rows(tr, C), out_shape=jax.ShapeDtypeStruct((R, C), F32), compiler_params=_cp(("parallel",)),
    )(slots)


def _allgather(x_shard, name):
    m_per, n = x_shard.shape

    def body(x_ref, out_ref, send_sems, recv_sems, local_sem):
        x, y, c = lax.axis_index("x"), lax.axis_index("y"), lax.axis_index("c")
        me, sibling = (x, y, c), (x, y, 1 - c)
        chips = [(1 - x, y), (x, 1 - y), (1 - x, 1 - y)]

        def rows(px, py, pc):
            return out_ref.at[pl.ds((4 * px + 2 * py + pc) * m_per, m_per), :]

        def copy(k, block, to, src=None):
            return pltpu.make_async_remote_copy(
                src_ref=rows(*block) if src is None else src, dst_ref=rows(*block),
                send_sem=send_sems.at[k], recv_sem=recv_sems.at[k], device_id=to, device_id_type=MESH)

        mine = pltpu.make_async_copy(x_ref, rows(*me), local_sem)
        mine.start()
        first = [copy(0, me, sibling, src=x_ref)]
        first += [copy(1 + j, me, (*chip, c), src=x_ref) for j, chip in enumerate(chips)]
        for cp in first:
            cp.start()
        passed = [copy(4 + j, (*chip, c), sibling) for j, chip in enumerate(chips)]
        for j, chip in enumerate(chips):
            copy(1 + j, (*chip, c), me).wait_recv()
            passed[j].start()
        copy(0, sibling, me).wait_recv()
        for j, chip in enumerate(chips):
            copy(4 + j, (*chip, 1 - c), me).wait_recv()
        for cp in first + passed:
            cp.wait_send()
        mine.wait()

    return pl.pallas_call(
        body, name=name, out_shape=jax.ShapeDtypeStruct((N_DEV * m_per, n), x_shard.dtype),
        in_specs=[pl.BlockSpec(memory_space=pltpu.VMEM)], out_specs=pl.BlockSpec(memory_space=pltpu.VMEM),
        scratch_shapes=[pltpu.SemaphoreType.DMA((7,)), pltpu.SemaphoreType.DMA((7,)), pltpu.SemaphoreType.DMA],
        compiler_params=pltpu.CompilerParams(vmem_limit_bytes=VMEM_BIG),
    )(x_shard)


def _alltoall(slabs, name):
    na = len(slabs)

    def body(*refs):
        ins, outs = refs[:na], refs[na:2 * na]
        send_sems, recv_sems, local_sems = refs[2 * na:]
        x, y, c = lax.axis_index("x"), lax.axis_index("y"), lax.axis_index("c")
        me = 4 * x + 2 * y + c
        copies = []
        for a in range(na):
            loc = pltpu.make_async_copy(ins[a].at[me], outs[a].at[me], local_sems.at[a])
            loc.start()
            copies.append(loc)
        rem = []
        for k in range(1, N_DEV):
            px, py, pc = x ^ (k >> 2), y ^ ((k >> 1) & 1), c ^ (k & 1)
            peer = 4 * px + 2 * py + pc
            for a in range(na):
                cp = pltpu.make_async_remote_copy(
                    src_ref=ins[a].at[peer], dst_ref=outs[a].at[me],
                    send_sem=send_sems.at[a, k - 1], recv_sem=recv_sems.at[a, k - 1],
                    device_id=(px, py, pc), device_id_type=MESH)
                cp.start()
                rem.append((a, k, peer, cp))
        for a, k, peer, cp in rem:
            pltpu.make_async_remote_copy(
                src_ref=ins[a].at[me], dst_ref=outs[a].at[peer],
                send_sem=send_sems.at[a, k - 1], recv_sem=recv_sems.at[a, k - 1],
                device_id=(x, y, c), device_id_type=MESH).wait_recv()
        for a, k, peer, cp in rem:
            cp.wait_send()
        for loc in copies:
            loc.wait()

    anyspec = pl.BlockSpec(memory_space=pl.ANY)
    return pl.pallas_call(
        body, name=name, out_shape=[jax.ShapeDtypeStruct(s.shape, s.dtype) for s in slabs],
        in_specs=[anyspec] * na, out_specs=[anyspec] * na,
        scratch_shapes=[pltpu.SemaphoreType.DMA((na, 7)), pltpu.SemaphoreType.DMA((na, 7)),
                        pltpu.SemaphoreType.DMA((na,))],
    )(*slabs)


def _rope_tables(S):
    t = np.arange(S)
    row = (t // GRID_W).astype(np.float32)
    colp = (t % GRID_W).astype(np.float32)
    half = HD // 2
    inv = (ROPE_BASE ** (-np.arange(0, half, 2, dtype=np.float32) / half)).astype(np.float32)
    ar = row[:, None] * inv[None, :]
    ac = colp[:, None] * inv[None, :]
    ang = np.concatenate([ar, ar, ac, ac], axis=-1).astype(np.float32)
    cos = np.cos(ang).astype(np.float32)
    sin = np.sin(ang).astype(np.float32)
    lane = np.arange(HD)
    first = (lane % 32) < 16
    sa = np.where(first[None, :], -sin, 0.0)
    sb = np.where(first[None, :], 0.0, sin)

    def ext(tab, ctx_val):
        full = np.zeros((CTX + S, 128), np.float32)
        full[:CTX, :HD] = ctx_val
        full[CTX:, :HD] = tab
        return jnp.asarray(full)

    return ext(cos, 1.0), ext(sa, 0.0), ext(sb, 0.0)


def _pad_cols_win(w):
    q, k, v, rest, z = w[:, 0:512], w[:, 512:640], w[:, 640:768], w[:, 768:2304], w[:, 2304:2336]

    def padh(t, nh):
        t = t.reshape(D, nh, HD)
        return jnp.pad(t, ((0, 0), (0, 0), (0, 128 - HD))).reshape(D, nh * 128)

    return jnp.concatenate([padh(q, N_ATT), padh(k, N_KV), padh(v, N_KV), rest, jnp.pad(z, ((0, 0), (0, 96)))], axis=1)


def _unpad_cols_win(g):
    def unp(t, nh):
        return t.reshape(D, nh, 128)[:, :, :HD].reshape(D, nh * HD)

    return jnp.concatenate([unp(g[:, O_Q:O_K], N_ATT), unp(g[:, O_K:O_V], N_KV), unp(g[:, O_V:O_GQ], N_KV),
                            g[:, O_GQ:O_Z], g[:, O_Z:O_Z + 32]], axis=1)


def _local_step(x, ctx, target, ada_l, ada_c, gains, sink, win_p, wg_bd, bg, ggla, wout, wffi_sh, wffo_sh):
    S = x.shape[0]
    cos, sa, sb = _rope_tables(S)
    g1, g2, g3, g4 = (gains[i:i + 1] for i in range(4))
    sh1, sc1, gt1, sh2, sc2, gt2 = (ada_l[i:i + 1] for i in range(6))
    sh1c, sc1c = ada_c[0:1], ada_c[1:2]
    gml, gmc, gm2 = g1 * (1.0 + sc1), g1 * (1.0 + sc1c), g3 * (1.0 + sc2)
    mavg = jnp.asarray(np.kron(np.eye(N_GLA, dtype=np.float32), np.full((DV, DV), 1.0 / DV, np.float32))).astype(_BF)

    h, q, k, v, gq, gk, gv, gg, z, la = _inproj_fwd(x, ctx, gml, sh1, gmc, sh1c, win_p, wg_bd, bg, cos, sa, sb)
    n_ffi, r_ffo = wffi_sh.shape[1], wffo_sh.shape[0]
    tt_e = 768 if (S + CTX) % 768 == 0 else 256
    tt_s = 512 if S % 512 == 0 else 256
    attn, lse, wffi_g, wffo_g = _attn_fwd(q, k, v, sink, [wffi_sh, wffo_sh])
    wffi = wffi_g.transpose(1, 0, 2).reshape(D, N_DEV * n_ffi)
    wffo = wffo_g.reshape(N_DEV * r_ffo, D)
    o_f, st_f = _gla_fwd(gq, gk, gv, la, False)
    o_b, st_b = _gla_fwd(gq, gk, gv, la, True)
    x1, mix = _mix_fwd(x, attn, o_f, o_b, gg, ggla, mavg, wout, gt1, g2)
    dx1, h2, du, act, df, s_ffn, loss = _ffn(x1, target, gm2, sh2, gt2, g4, wffi, wffo)
    g_wffi = _matmul_tn(h2, du, 512, 2 * FFN, tt_s, "grad_w_ffn_in")
    g_wffo = _matmul_tn(act, df, FFN, D, tt_s, "grad_w_ffn_out")
    slab_ffi = g_wffi.reshape(D, N_DEV, n_ffi).transpose(1, 0, 2).astype(_BF)
    slab_ffo = g_wffo.reshape(N_DEV, r_ffo, D).astype(_BF)
    d_attn, do_gla, dgg, dy, s_mix = _mix_bwd(dx1, mix, o_f, o_b, gg, ggla, mavg, wout, gt1, g2)
    dq, dk, dv, dsink, got_ffi, got_ffo = _attn_bwd(q, k, v, sink, lse, d_attn, [slab_ffi, slab_ffo])
    dgq_f, dgk_f, dgv_f, dla_f = _gla_bwd(gq, gk, gv, la, st_f, do_gla, False)
    dgq_b, dgk_b, dgv_b, dla_b = _gla_bwd(gq, gk, gv, la, st_b, do_gla, True)
    dp, dlg, grad_x, s_in, s_bg = _inproj_bwd(x, ctx, gml, gmc, win_p, wg_bd, cos, sa, sb, la, dq, dk, dv,
                                              dgq_f, dgq_b, dgk_f, dgk_b, dgv_f, dgv_b, dgg, dla_f, dla_b, dx1)
    g_win = _matmul_tn(h, dp, D, NP, tt_e, "grad_w_in")
    g_wg = _matmul_tn(z, dlg, 128, 512, tt_e, "grad_w_gate")
    g_wout = _matmul_tn(mix, dy, D, D, tt_s, "grad_w_out")
    small = _small_grads(s_in, s_ffn, s_mix, ada_l, ada_c, gains)
    return dict(loss=loss[0, 0], grad_x=grad_x, g_win=g_win, g_wg=g_wg, g_wout=g_wout, got_ffi=got_ffi,
                got_ffo=got_ffo, small=small, dsink=dsink[:, 0], dbg=s_bg[0], dggla=s_mix[1, 0:512])


SMALL_ROWS = 32


def _pack_small(c_ctx, b_ada, g1, g2, g3, g4, sink, bgf, bgb, ggla, wgf, wgb):
    def row(v):
        v = v.reshape(-1)
        return jnp.pad(v, (0, D - v.shape[0])).reshape(1, D)

    rows = [c_ctx.reshape(1, D), b_ada.reshape(6, D), g1.reshape(1, D), g2.reshape(1, D), g3.reshape(1, D),
            g4.reshape(1, D), row(sink), row(jnp.concatenate([bgf.reshape(-1), bgb.reshape(-1)])), row(ggla),
            row(wgf), row(wgb)]
    out = jnp.concatenate(rows, axis=0)
    return jnp.pad(out, ((0, SMALL_ROWS - out.shape[0]), (0, 0)))


def _unpack_small(p):
    return dict(c_ctx=p[0], b_ada=p[1:7].reshape(1, 6 * D), g_pre_mix=p[7:8], g_post_mix=p[8:9],
                g_pre_ffn=p[9:10], g_post_ffn=p[10:11], attn_sink=p[11:12, 0:8],
                b_gate_fwd=p[12:13, 0:256], b_gate_bwd=p[12:13, 256:512], g_gla_norm=p[13:14, 0:64],
                w_gate_fwd=p[14, 0:512].reshape(1, GATE_RANK, 32), w_gate_bwd=p[15, 0:512].reshape(1, GATE_RANK, 32))


def kernel(x, c, ctx, c_ctx, w_ada, b_ada, g_pre_mix, g_post_mix, g_pre_ffn, g_post_ffn, w_in, attn_sink, w_gate_fwd, b_gate_fwd, w_gate_bwd, b_gate_bwd, g_gla_norm, w_out, w_ffn_in, w_ffn_out, loss_target, m_c_ctx, m_w_ada, m_b_ada, m_g_pre_mix, m_g_post_mix, m_g_pre_ffn, m_g_post_ffn, m_w_in, m_attn_sink, m_w_gate_fwd, m_b_gate_fwd, m_w_gate_bwd, m_b_gate_bwd, m_g_gla_norm, m_w_out, m_w_ffn_in, m_w_ffn_out, v_c_ctx, v_w_ada, v_b_ada, v_g_pre_mix, v_g_post_mix, v_g_pre_ffn, v_g_post_ffn, v_w_in, v_attn_sink, v_w_gate_fwd, v_b_gate_fwd, v_w_gate_bwd, v_b_gate_bwd, v_g_gla_norm, v_w_out, v_w_ffn_in, v_w_ffn_out):
    me = 4 * lax.axis_index("x") + 2 * lax.axis_index("y") + lax.axis_index("c")
    S = x.shape[1]
    n_in = w_in.shape[2]
    n_ffi = w_ffn_in.shape[2]
    r_out = w_out.shape[1]
    r_ffo = w_ffn_out.shape[1]
    n_ada = w_ada.shape[2]

    small_in = jnp.concatenate([c.reshape(8, 128), w_gate_fwd.reshape(4, 128), w_gate_bwd.reshape(4, 128)], axis=0)
    sg = _allgather(small_in, "gather_small").reshape(N_DEV, 16, 128)
    c_all = sg[:, 0:8].reshape(N_DEV, D)
    wgf = sg[:, 8:12].reshape(N_DEV, GATE_RANK, 32).transpose(1, 0, 2).reshape(GATE_RANK, 256)
    wgb = sg[:, 12:16].reshape(N_DEV, GATE_RANK, 32).transpose(1, 0, 2).reshape(GATE_RANK, 256)

    pack = jnp.concatenate([w_in[0].astype(_BF).reshape(-1, 128), w_out[0].astype(_BF).reshape(-1, 128)], axis=0)
    rows_per = pack.shape[0]
    wg_all = _allgather(pack, "gather_weights").reshape(N_DEV, rows_per, 128)
    o1 = D * n_in // 128
    win_full = wg_all[:, 0:o1].reshape(N_DEV, D, n_in).transpose(1, 0, 2).reshape(D, N_DEV * n_in)
    wout_full = wg_all[:, o1:].reshape(N_DEV * r_out, D)
    win_p = _pad_cols_win(win_full)
    wg_bd = jnp.zeros((128, 512), F32).at[0:16, 0:256].set(wgf).at[16:32, 256:512].set(wgb).astype(_BF)

    ada_part = _ada_fwd(c_all, c_ctx.reshape(1, D), w_ada[0])
    ada_all = _allgather(ada_part, "gather_ada").reshape(N_DEV, 16, n_ada)
    ada_full = ada_all.transpose(1, 0, 2).reshape(16, N_DEV * n_ada) + b_ada
    ada_l = jnp.pad(lax.dynamic_slice_in_dim(ada_full, me, 1, 0).reshape(6, D), ((0, 2), (0, 0)))
    ada_c = jnp.pad(ada_full[8].reshape(6, D), ((0, 2), (0, 0)))
    gains = jnp.pad(jnp.concatenate([g_pre_mix, g_post_mix, g_pre_ffn, g_post_ffn], axis=0), ((0, 4), (0, 0)))
    sink = jnp.broadcast_to(attn_sink.reshape(8, 1), (8, 128))
    bg = jnp.concatenate([b_gate_fwd, b_gate_bwd], axis=1)
    ggla = jnp.tile(g_gla_norm, (1, N_GLA))

    r = _local_step(x[0], ctx[0], loss_target[0], ada_l, ada_c, gains, sink, win_p, wg_bd, bg, ggla,
                    wout_full, w_ffn_in[0].astype(_BF), w_ffn_out[0].astype(_BF))

    loss = lax.psum(r["loss"], ("x", "y", "c"))

    g_win = _unpad_cols_win(r["g_win"]).reshape(D, N_DEV, n_in).transpose(1, 0, 2).astype(_BF)
    g_wout = r["g_wout"].reshape(N_DEV, r_out, D).astype(_BF)
    slots = _alltoall([g_win, g_wout], "scatter_grads")

    def row(vv):
        vv = vv.reshape(-1)
        return jnp.pad(vv, (0, D - vv.shape[0])).reshape(1, D)

    g_wgf = r["g_wg"][0:16, 0:256]
    g_wgb = r["g_wg"][16:32, 256:512]
    part = jnp.concatenate([r["small"], row(r["dsink"]), row(r["dbg"]), row(r["dggla"].reshape(N_GLA, DV).sum(0)),
                            g_wgf.reshape(4, D), g_wgb.reshape(4, D), jnp.zeros((5, D), F32)], axis=0)
    parts = _allgather(part, "gather_small_grads").reshape(N_DEV, 32, D)
    tot = _sum_slots(parts, "sum_small_grads")
    d_ada_rows = parts[:, 0:6].reshape(N_DEV, 6 * D)
    d_ada_c = tot[6:12].reshape(1, 6 * D)
    my_cols = lax.dynamic_slice_in_dim(jnp.concatenate([d_ada_rows, jnp.broadcast_to(d_ada_c, (1, 6 * D)),
                                                        jnp.zeros((7, 6 * D), F32)], axis=0), me * n_ada, n_ada, 1)
    grad_w_ada, t_part = _ada_bwd(c_all, c_ctx.reshape(1, D), w_ada[0], my_cols)
    t_all = _allgather(t_part, "gather_c_ctx").reshape(N_DEV, 8, D)
    t_tot = _sum_slots(t_all, "sum_c_ctx")[0]
    sc = 1.0 / (1.0 + jnp.exp(-c_ctx))
    grad_c_ctx = t_tot * (sc * (1.0 + c_ctx * (1.0 - sc)))
    grad_b_ada = (tot[0:6] + tot[6:12]).reshape(1, 6 * D)
    g_wgf_tot = tot[19:23].reshape(GATE_RANK, 256)
    g_wgb_tot = tot[23:27].reshape(GATE_RANK, 256)
    grads_small = dict(
        c_ctx=grad_c_ctx, b_ada=grad_b_ada, g_pre_mix=tot[12:13], g_post_mix=tot[13:14], g_pre_ffn=tot[14:15],
        g_post_ffn=tot[15:16], attn_sink=tot[16:17, 0:8], b_gate_fwd=tot[17:18, 0:256], b_gate_bwd=tot[17:18, 256:512],
        g_gla_norm=tot[18:19, 0:64],
        w_gate_fwd=lax.dynamic_slice_in_dim(g_wgf_tot, me * 32, 32, 1).reshape(1, GATE_RANK, 32),
        w_gate_bwd=lax.dynamic_slice_in_dim(g_wgb_tot, me * 32, 32, 1).reshape(1, GATE_RANK, 32))

    names_small = ["c_ctx", "b_ada", "g_pre_mix", "g_post_mix", "g_pre_ffn", "g_post_ffn", "attn_sink",
                   "b_gate_fwd", "b_gate_bwd", "g_gla_norm", "w_gate_fwd", "w_gate_bwd"]
    def packd(dd):
        return _pack_small(dd["c_ctx"], dd["b_ada"], dd["g_pre_mix"], dd["g_post_mix"], dd["g_pre_ffn"],
                           dd["g_post_ffn"], dd["attn_sink"], dd["b_gate_fwd"], dd["b_gate_bwd"], dd["g_gla_norm"],
                           dd["w_gate_fwd"], dd["w_gate_bwd"])

    w_small = dict(c_ctx=c_ctx, b_ada=b_ada, g_pre_mix=g_pre_mix, g_post_mix=g_post_mix, g_pre_ffn=g_pre_ffn,
                   g_post_ffn=g_post_ffn, attn_sink=attn_sink, b_gate_fwd=b_gate_fwd, b_gate_bwd=b_gate_bwd,
                   g_gla_norm=g_gla_norm, w_gate_fwd=w_gate_fwd, w_gate_bwd=w_gate_bwd)
    m_small = dict(c_ctx=m_c_ctx, b_ada=m_b_ada, g_pre_mix=m_g_pre_mix, g_post_mix=m_g_post_mix,
                   g_pre_ffn=m_g_pre_ffn, g_post_ffn=m_g_post_ffn, attn_sink=m_attn_sink, b_gate_fwd=m_b_gate_fwd,
                   b_gate_bwd=m_b_gate_bwd, g_gla_norm=m_g_gla_norm, w_gate_fwd=m_w_gate_fwd, w_gate_bwd=m_w_gate_bwd)
    v_small = dict(c_ctx=v_c_ctx, b_ada=v_b_ada, g_pre_mix=v_g_pre_mix, g_post_mix=v_g_post_mix,
                   g_pre_ffn=v_g_pre_ffn, g_post_ffn=v_g_post_ffn, attn_sink=v_attn_sink, b_gate_fwd=v_b_gate_fwd,
                   b_gate_bwd=v_b_gate_bwd, g_gla_norm=v_g_gla_norm, w_gate_fwd=v_w_gate_fwd, w_gate_bwd=v_w_gate_bwd)
    v_pack = packd(v_small)
    v_pack = jnp.where(packd(jax.tree.map(jnp.ones_like, v_small)) > 0, v_pack, 1.0)
    _, d_s, nm_s, nv_s = _adamw(packd(w_small)[None], packd(grads_small)[None], packd(m_small)[None], v_pack[None],
                                "adamw_small")
    d_s, nm_s, nv_s = _unpack_small(d_s[0]), _unpack_small(nm_s[0]), _unpack_small(nv_s[0])

    big = {}
    for nm, w, g, m, v in [("w_ada", w_ada, grad_w_ada, m_w_ada, v_w_ada), ("w_in", w_in, slots[0], m_w_in, v_w_in),
                           ("w_out", w_out, slots[1], m_w_out, v_w_out),
                           ("w_ffn_in", w_ffn_in, r["got_ffi"], m_w_ffn_in, v_w_ffn_in),
                           ("w_ffn_out", w_ffn_out, r["got_ffo"], m_w_ffn_out, v_w_ffn_out)]:
        big[nm] = _adamw(w, g, m, v, "adamw_" + nm)

    order = ["c_ctx", "w_ada", "b_ada", "g_pre_mix", "g_post_mix", "g_pre_ffn", "g_post_ffn", "w_in", "attn_sink",
             "w_gate_fwd", "b_gate_fwd", "w_gate_bwd", "b_gate_bwd", "g_gla_norm", "w_out", "w_ffn_in", "w_ffn_out"]
    grads, deltas, new_m, new_v = [], [], [], []
    for nm in order:
        if nm in big:
            g_, d_, m_, v_ = big[nm]
        else:
            g_, d_, m_, v_ = grads_small[nm], d_s[nm], nm_s[nm], nv_s[nm]
        grads.append(g_)
        deltas.append(d_)
        new_m.append(m_)
        new_v.append(v_)
    return (loss, r["grad_x"][None], *grads, *deltas, *new_m, *new_v)
```

```python
import functools
import math

import numpy as np
import jax
import jax.numpy as jnp
from jax import lax
from jax.experimental import pallas as pl
from jax.experimental.pallas import tpu as pltpu

F32 = jnp.float32
_BF = jnp.bfloat16

N_DEV = 8
D = 1024
CTX = 256
HD = 64
N_ATT = 8
N_KV = 2
GRP = N_ATT // N_KV
WIN = 128
GRID_W = 64
ROPE_BASE = 10000.0
N_GLA = 8
DK = 32
DV = 64
GATE_RANK = 16
GATE_TAU = 16.0
FFN = 2816
EPS = 1e-6
NEG = -1e30
GLA_T = 128

QP = N_ATT * 128
KP = N_KV * 128
O_Q, O_K, O_V = 0, QP, QP + KP
O_GQ = O_V + KP
O_GK = O_GQ + N_GLA * DK
O_GV = O_GK + N_GLA * DK
O_GG = O_GV + N_GLA * DV
O_Z = O_GG + N_GLA * DV
NP = O_Z + 128
IN_COLS = 2336

ADAM_LR, ADAM_B1, ADAM_B2, ADAM_EPS, ADAM_WD, ADAM_STEP = 0.001, 0.9, 0.999, 1e-08, 0.01, 10

VMEM_BIG = 56 * 1024 * 1024
MESH = pl.DeviceIdType.MESH


def _cp(sem, vmem=None):
    return pltpu.CompilerParams(dimension_semantics=sem, vmem_limit_bytes=vmem)


def _full(shape):
    nd = len(shape)
    return pl.BlockSpec(shape, lambda *a: (0,) * nd)


def _rows(tile, width, off=0):
    return pl.BlockSpec((tile, width), lambda i: (i + off, 0))


def _rows_lat(tile, width):
    return pl.BlockSpec((tile, width), lambda i: (jnp.maximum(i - 1, 0), 0))


def _nt(a, b):
    return lax.dot_general(a, b, (((1,), (1,)), ((), ())), preferred_element_type=F32)


def _tn(a, b):
    return lax.dot_general(a, b, (((0,), (0,)), ((), ())), preferred_element_type=F32)


def _nn(a, b):
    return jnp.dot(a, b, preferred_element_type=F32)


def _head_mean(x, mavg):
    n = x.shape[0]
    hi = x.astype(_BF)
    lo = (x - hi.astype(F32)).astype(_BF)
    y = _nn(jnp.concatenate([hi, lo], axis=0), mavg)
    return y[0:n] + y[n:2 * n]


def _rope(t, cos, sa, sb):
    n = t.shape[1]
    reps = n // 128
    c = jnp.tile(cos, (1, reps))
    a = jnp.tile(sa, (1, reps))
    b = jnp.tile(sb, (1, reps))
    return t * c + pltpu.roll(t, n - 16, 1) * a + pltpu.roll(t, 16, 1) * b


def _unrope(t, cos, sa, sb):
    n = t.shape[1]
    reps = n // 128
    c = jnp.tile(cos, (1, reps))
    a = jnp.tile(sa, (1, reps))
    b = jnp.tile(sb, (1, reps))
    return t * c + pltpu.roll(t * a, 16, 1) + pltpu.roll(t * b, n - 16, 1)


def _sigmoid(x):
    return 1.0 / (1.0 + jnp.exp(-x))


def _inproj_fwd(x, ctx, gml, shl, gmc, shc, win, wg, bg, cos, sa, sb):
    E = x.shape[0] + CTX
    TE = CTX

    def body(x_ref, c_ref, gml_ref, shl_ref, gmc_ref, shc_ref, w_ref, wg_ref, bg_ref, cos_ref, sa_ref, sb_ref,
             h_ref, q_ref, k_ref, v_ref, gq_ref, gk_ref, gv_ref, gg_ref, z_ref, la_ref):
        is_ctx = pl.program_id(0) == 0
        gm = jnp.where(is_ctx, gmc_ref[...], gml_ref[...])
        sh = jnp.where(is_ctx, shc_ref[...], shl_ref[...])
        x = jnp.where(is_ctx, c_ref[...], x_ref[...])
        r = lax.rsqrt(jnp.mean(x * x, axis=-1, keepdims=True) + EPS)
        hb = ((x * r) * gm + sh).astype(_BF)
        h_ref[...] = hb
        p = _nn(hb, w_ref[...])
        cos_t, sa_t, sb_t = cos_ref[...], sa_ref[...], sb_ref[...]
        q_ref[...] = (_rope(p[:, O_Q:O_K], cos_t, sa_t, sb_t) * (HD ** -0.5)).astype(_BF)
        k_ref[...] = _rope(p[:, O_K:O_V], cos_t, sa_t, sb_t).astype(_BF)
        v_ref[...] = p[:, O_V:O_GQ].astype(_BF)
        gq_ref[...] = p[:, O_GQ:O_GK] * (DK ** -0.5)
        gk_ref[...] = p[:, O_GK:O_GV]
        gv_ref[...] = p[:, O_GV:O_GG]
        gg_ref[...] = p[:, O_GG:O_Z]
        zb = p[:, O_Z:NP].astype(_BF)
        z_ref[...] = zb
        lg = _nn(zb, wg_ref[...]) + bg_ref[...]
        la_ref[...] = (jnp.minimum(lg, 0.0) - jnp.log(1.0 + jnp.exp(-jnp.abs(lg)))) * (1.0 / GATE_TAU)

    vec = _full((1, D))
    tab = _rows(TE, 128)
    outs = [(D, _BF), (QP, _BF), (KP, _BF), (KP, _BF), (256, F32), (256, F32), (512, F32), (512, F32),
            (128, _BF), (512, F32)]
    return pl.pallas_call(
        body, name="inproj_fwd", grid=(E // TE,),
        in_specs=[_rows_lat(TE, D), _full((CTX, D)), vec, vec, vec, vec, _full((D, NP)), _full((128, 512)),
                  _full((1, 512)), tab, tab, tab],
        out_specs=[_rows(TE, w) for w, _ in outs],
        out_shape=[jax.ShapeDtypeStruct((E, w), dt) for w, dt in outs],
        compiler_params=_cp(("arbitrary",), 40 * 1024 * 1024),
    )(x, ctx, gml, shl, gmc, shc, win, wg, bg, cos, sa, sb)


def _xchg_scratch(na):
    return [pltpu.SemaphoreType.DMA((na, N_DEV - 1)), pltpu.SemaphoreType.DMA((na, N_DEV - 1)),
            pltpu.SemaphoreType.DMA((na,))]


def _xchg_copies(ins, outs, send_sems, recv_sems, local_sems, gather):
    x, y, c = lax.axis_index("x"), lax.axis_index("y"), lax.axis_index("c")
    me = 4 * x + 2 * y + c
    local, sends, recvs = [], [], []
    for a in range(len(ins)):
        local.append(pltpu.make_async_copy(ins[a] if gather else ins[a].at[me], outs[a].at[me], local_sems.at[a]))
    for k in range(1, N_DEV):
        px, py, pc = x ^ (k >> 2), y ^ ((k >> 1) & 1), c ^ (k & 1)
        peer = 4 * px + 2 * py + pc
        for a in range(len(ins)):
            sems = dict(send_sem=send_sems.at[a, k - 1], recv_sem=recv_sems.at[a, k - 1], device_id_type=MESH)
            sends.append(pltpu.make_async_remote_copy(
                src_ref=ins[a] if gather else ins[a].at[peer], dst_ref=outs[a].at[me], device_id=(px, py, pc), **sems))
            recvs.append(pltpu.make_async_remote_copy(
                src_ref=ins[a] if gather else ins[a].at[me], dst_ref=outs[a].at[peer], device_id=(x, y, c), **sems))
    return local, sends, recvs


def _xchg_start(cps):
    local, sends, _ = cps
    for cp in local + sends:
        cp.start()


def _xchg_finish(cps):
    local, sends, recvs = cps
    for cp in recvs:
        cp.wait_recv()
    for cp in sends:
        cp.wait_send()
    for cp in local:
        cp.wait()


def _xchg_out_shapes(ins, gather):
    return [jax.ShapeDtypeStruct(((N_DEV,) + s.shape) if gather else s.shape, s.dtype) for s in ins]


def _attn_specs(E):
    nb = (E - CTX) // WIN
    last = E // WIN - 1
    kc = pl.BlockSpec((CTX, KP), lambda n: (0, 0))
    kp = pl.BlockSpec((WIN, KP), lambda n: (n + 1, 0))
    kk = pl.BlockSpec((WIN, KP), lambda n: (n + 2, 0))
    kn = pl.BlockSpec((WIN, KP), lambda n: (jnp.minimum(n + 3, last), 0))
    return nb, [kc, kp, kk, kn]


def _attn_bias(nb):
    rows = np.arange(GRP * WIN)[:, None] % WIN
    cols = np.arange(CTX + 3 * WIN)[None, :]
    j = cols - CTX
    band = np.abs(j - WIN - rows) <= WIN
    out = []
    for first, last in ((True, False), (False, False), (False, True)):
        ok = (cols < CTX) | (band & ((j >= WIN) | (not first)) & ((j < 2 * WIN) | (not last)))
        out.append(np.where(ok, 0.0, NEG).astype(np.float32))
    bias = jnp.asarray(np.stack(out))
    spec = pl.BlockSpec((1, GRP * WIN, CTX + 3 * WIN),
                        lambda n: (jnp.where(n == 0, 0, jnp.where(n == nb - 1, 2, 1)), 0, 0))
    return bias, spec


def _attn_fwd(q, k, v, sink, shards):
    E = q.shape[0]
    S = E - CTX
    nb, kspecs = _attn_specs(E)
    na = len(shards)

    def body(q_ref, kc, kp, kk, kn, vc, vp, vk, vn, sink_ref, bias_ref, *rest):
        shard_refs, (o_ref, lse_ref), got_refs = rest[:na], rest[na:na + 2], rest[na + 2:2 * na + 2]
        n = pl.program_id(0)
        cps = _xchg_copies(shard_refs, got_refs, *rest[2 * na + 2:], gather=True)

        @pl.when(n == 0)
        def _():
            _xchg_start(cps)

        lane = lax.broadcasted_iota(jnp.int32, (WIN, 128), 1)
        lse_t = jnp.zeros((WIN, 128), F32)
        hs = [slice(128 * h, 128 * h + 128) for h in range(N_KV)]
        K = [jnp.concatenate([kc[:, s_], kp[:, s_], kk[:, s_], kn[:, s_]], axis=0) for s_ in hs]
        Q = [jnp.concatenate([q_ref[:, 128 * (GRP * h + g):128 * (GRP * h + g) + 128] for g in range(GRP)], axis=0)
             for h in range(N_KV)]
        sk = [jnp.concatenate([jnp.broadcast_to(sink_ref[GRP * h + g:GRP * h + g + 1, 0:1], (WIN, 1))
                               for g in range(GRP)], axis=0) for h in range(N_KV)]
        s = [_nt(Q[h], K[h]) + bias_ref[0] for h in range(N_KV)]
        m = [jnp.maximum(jnp.max(s[h], axis=1, keepdims=True), sk[h]) for h in range(N_KV)]
        e = [jnp.exp(s[h] - m[h]) for h in range(N_KV)]
        den = [jnp.sum(e[h], axis=1, keepdims=True) + jnp.exp(sk[h] - m[h]) for h in range(N_KV)]
        V = [jnp.concatenate([vc[:, s_], vp[:, s_], vk[:, s_], vn[:, s_]], axis=0) for s_ in hs]
        o = [_nn((e[h] * (1.0 / den[h])).astype(_BF), V[h]) for h in range(N_KV)]
        for h in range(N_KV):
            lse = m[h] + jnp.log(den[h])
            for g in range(GRP):
                lse_t = jnp.where(lane == GRP * h + g, lse[WIN * g:WIN * g + WIN], lse_t)
            for pp in range(GRP // 2):
                a = o[h][WIN * 2 * pp:WIN * 2 * pp + WIN]
                b = o[h][WIN * (2 * pp + 1):WIN * (2 * pp + 1) + WIN]
                t = 2 * h + pp
                o_ref[:, 128 * t:128 * t + 128] = (a + pltpu.roll(b, 64, 1)).astype(_BF)
        lse_ref[...] = lse_t

        @pl.when(n == nb - 1)
        def _():
            _xchg_finish(cps)

    qs = pl.BlockSpec((WIN, QP), lambda n: (n + 2, 0))
    anyspec = pl.BlockSpec(memory_space=pl.ANY)
    bias, bias_spec = _attn_bias(nb)
    return pl.pallas_call(
        body, name="attn_fwd", grid=(nb,),
        in_specs=[qs] + kspecs + kspecs + [_full((8, 128)), bias_spec] + [anyspec] * na,
        out_specs=[_rows(WIN, 512), _rows(WIN, 128)] + [anyspec] * na,
        out_shape=[jax.ShapeDtypeStruct((S, 512), _BF), jax.ShapeDtypeStruct((S, 128), F32)]
        + _xchg_out_shapes(shards, True),
        scratch_shapes=_xchg_scratch(na),
        compiler_params=_cp(("arbitrary",)),
    )(q, k, k, k, k, v, v, v, v, sink, bias, *shards)


def _attn_bwd(q, k, v, sink, lse, d_attn, slabs):
    E = q.shape[0]
    S = E - CTX
    nb, kspecs = _attn_specs(E)
    last = E // WIN - 1
    na = len(slabs)

    def body(q_ref, kc, kp, kk, kn, vc, vp, vk, vn, sink_ref, bias_ref, lse_ref, do_ref, *rest):
        slab_refs, (dq_ref, dk_ref, dv_ref, ds_ref), got_refs = rest[:na], rest[na:na + 4], rest[na + 4:2 * na + 4]
        n = pl.program_id(0)
        cps = _xchg_copies(slab_refs, got_refs, *rest[2 * na + 4:], gather=False)

        @pl.when(n == 0)
        def _():
            _xchg_start(cps)
            dk_ref[...] = jnp.zeros_like(dk_ref)
            dv_ref[...] = jnp.zeros_like(dv_ref)
            ds_ref[...] = jnp.zeros_like(ds_ref)

        lane = lax.broadcasted_iota(jnp.int32, (WIN, 128), 1)
        lse_t = lse_ref[...]
        starts = [None, pl.multiple_of((n + 1) * WIN, WIN), pl.multiple_of((n + 2) * WIN, WIN),
                  pl.multiple_of(jnp.minimum(n + 3, last) * WIN, WIN)]
        for h in range(N_KV):
            hs = slice(128 * h, 128 * h + 128)
            K = jnp.concatenate([kc[:, hs], kp[:, hs], kk[:, hs], kn[:, hs]], axis=0)
            V = jnp.concatenate([vc[:, hs], vp[:, hs], vk[:, hs], vn[:, hs]], axis=0)
            Q = jnp.concatenate([q_ref[:, 128 * (GRP * h + g):128 * (GRP * h + g) + 128] for g in range(GRP)], axis=0)
            sk = jnp.concatenate([jnp.broadcast_to(sink_ref[GRP * h + g:GRP * h + g + 1, 0:1], (WIN, 1))
                                  for g in range(GRP)], axis=0)
            ls = jnp.concatenate([jnp.sum(jnp.where(lane == GRP * h + g, lse_t, 0.0), axis=1, keepdims=True)
                                  for g in range(GRP)], axis=0)
            dos = []
            for g in range(GRP):
                j = GRP * h + g
                t = do_ref[:, 128 * (j // 2):128 * (j // 2) + 128].astype(F32)
                if j % 2:
                    t = pltpu.roll(t, 64, 1)
                dos.append(jnp.where(lane < HD, t, 0.0))
            do = jnp.concatenate(dos, axis=0).astype(_BF)
            p = jnp.exp(_nt(Q, K) + bias_ref[0] - ls)
            dp = _nt(do, V)
            delta = jnp.sum(p * dp, axis=1, keepdims=True)
            dsc = (p * (dp - delta)).astype(_BF)
            dq = _nn(dsc, K) * (HD ** -0.5)
            for g in range(GRP):
                j = GRP * h + g
                dq_ref[:, 128 * j:128 * j + 128] = dq[WIN * g:WIN * g + WIN].astype(_BF)
            dK = _tn(Q, dsc)
            dV = _tn(do, p.astype(_BF))
            dk_ref[hs, 0:CTX] += dK[:, 0:CTX]
            dv_ref[hs, 0:CTX] += dV[:, 0:CTX]
            for w in range(1, 4):
                lo = CTX + WIN * (w - 1)
                dk_ref[hs, pl.ds(starts[w], WIN)] += dK[:, lo:lo + WIN]
                dv_ref[hs, pl.ds(starts[w], WIN)] += dV[:, lo:lo + WIN]
            psk = -jnp.exp(sk - ls) * delta
            for g in range(GRP):
                j = GRP * h + g
                tot = jnp.sum(psk[WIN * g:WIN * g + WIN], axis=0, keepdims=True)
                ds_ref[j:j + 1, :] += jnp.broadcast_to(tot, (1, 128))

        @pl.when(n == nb - 1)
        def _():
            _xchg_finish(cps)

    qs = pl.BlockSpec((WIN, QP), lambda n: (n + 2, 0))
    anyspec = pl.BlockSpec(memory_space=pl.ANY)
    bias, bias_spec = _attn_bias(nb)
    return pl.pallas_call(
        body, name="attn_bwd", grid=(nb,),
        in_specs=[qs] + kspecs + kspecs + [_full((8, 128)), bias_spec, _rows(WIN, 128), _rows(WIN, 512)]
        + [anyspec] * na,
        out_specs=[_rows(WIN, QP), _full((KP, E)), _full((KP, E)), _full((8, 128))] + [anyspec] * na,
        out_shape=[jax.ShapeDtypeStruct((S, QP), _BF), jax.ShapeDtypeStruct((KP, E), F32),
                   jax.ShapeDtypeStruct((KP, E), F32), jax.ShapeDtypeStruct((8, 128), F32)]
        + _xchg_out_shapes(slabs, False),
        scratch_shapes=_xchg_scratch(na),
        compiler_params=_cp(("arbitrary",), 48 * 1024 * 1024),
    )(q, k, k, k, k, v, v, v, v, sink, bias, lse, d_attn, *slabs)


def _gla_order(E, reverse, backward):
    nc = CTX // GLA_T
    n = E // GLA_T
    if not reverse:
        fwd = lambda s: s
    else:
        fwd = lambda s: jnp.where(s < nc, nc - 1 - s, n - 1 + nc - s)
    if backward:
        return lambda s: fwd(n - 1 - s)
    return fwd


def _gla_masks():
    T = GLA_T
    l128 = lax.broadcasted_iota(jnp.int32, (1, 128), 1)
    qmask = [((l128 >> 5) == j).astype(F32) for j in range(4)]
    vmask = [((l128 >> 6) == j).astype(F32) for j in range(2)]
    bd = ((lax.broadcasted_iota(jnp.int32, (512, 256), 0) >> 6)
          == (lax.broadcasted_iota(jnp.int32, (512, 256), 1) >> 5)).astype(F32)
    ri = lax.broadcasted_iota(jnp.int32, (T, 2 * T), 0)
    ci = lax.broadcasted_iota(jnp.int32, (T, 2 * T), 1) & (T - 1)
    return qmask, vmask, bd, ri, ci


def _tri_sum(tri, x):
    hi = x.astype(_BF)
    r1 = x - hi.astype(F32)
    mid = r1.astype(_BF)
    lo = (r1 - mid.astype(F32)).astype(_BF)
    n = x.shape[1]
    y = _nn(tri.astype(_BF), jnp.concatenate([hi, mid, lo], axis=1))
    return y[:, 0:n] + y[:, n:2 * n] + y[:, 2 * n:3 * n]


def _gla_decays(la, reverse, ri, ci):
    T = GLA_T
    msk2 = (ri <= ci) if reverse else (ri >= ci)
    mskT2 = (ri >= ci) if reverse else (ri <= ci)
    b = _tri_sum(msk2[:, 0:T], la)
    bT = b[0:1] if reverse else b[T - 1:T]
    bm = b[T // 2:T // 2 + 1]
    return msk2, mskT2, b, bT, bm


def _pair_stack(tile, m0, m1):
    return jnp.concatenate([(tile * m0).astype(_BF), (tile * m1).astype(_BF)], axis=0)


def _gla_fwd(gq, gk, gv, la, reverse):
    E = gq.shape[0]
    T = GLA_T
    n = E // T
    order = _gla_order(E, reverse, False)
    col = 1 if reverse else 0

    def body(gq_ref, gk_ref, gv_ref, la_ref, o_ref, st_ref, S_scr):
        @pl.when(pl.program_id(0) == 0)
        def _():
            S_scr[...] = jnp.zeros_like(S_scr)

        qmask, vmask, bd, ri, ci = _gla_masks()
        msk2, _, b, bT, bm = _gla_decays(la_ref[...], reverse, ri, ci)
        q, k, v = gq_ref[...], gk_ref[...], gv_ref[...]
        qd = (q * jnp.exp(b)).astype(_BF)
        qm = (q * jnp.exp(b - bm)).astype(_BF)
        km = k * jnp.exp(bm - b)
        kd = (k * jnp.exp(bT - b)).astype(_BF)
        ST = S_scr[...]
        comp = ST[0:DV]
        for h in range(1, N_GLA):
            comp = comp + ST[DV * h:DV * h + DV]
        st_ref[0] = comp
        inter = _nt(qd, ST.astype(_BF))
        tiles = []
        for p in range(N_GLA // 2):
            qs = slice(128 * (p // 2), 128 * (p // 2) + 128)
            vs = slice(128 * p, 128 * p + 128)
            j0 = (2 * p) % 4
            KS = _pair_stack(km[:, qs], qmask[j0], qmask[j0 + 1])
            VS = _pair_stack(v[:, vs], vmask[0], vmask[1])
            AA = jnp.where(msk2, _nt(qm[:, qs], KS), 0.0).astype(_BF)
            tiles.append(_nn(AA, VS))
        o_ref[...] = inter + jnp.concatenate(tiles, axis=1)
        S_scr[...] = ST * jnp.exp(bT) + bd * _tn(v.astype(_BF), kd)

    blk = lambda w, c=0: pl.BlockSpec((T, w), lambda s: (order(s), c))
    return pl.pallas_call(
        body, name="gla_fwd_rev" if reverse else "gla_fwd", grid=(n,),
        in_specs=[blk(256), blk(256), blk(512), blk(256, col)],
        out_specs=[blk(512), pl.BlockSpec((1, DV, 256), lambda s: (order(s), 0, 0))],
        out_shape=[jax.ShapeDtypeStruct((E, 512), F32), jax.ShapeDtypeStruct((n, DV, 256), F32)],
        scratch_shapes=[pltpu.VMEM((512, 256), F32)],
        compiler_params=_cp(("arbitrary",)),
    )(gq, gk, gv, la)


def _gla_bwd(gq, gk, gv, la, st, do, reverse):
    E = gq.shape[0]
    T = GLA_T
    n = E // T
    nc = CTX // T
    order = _gla_order(E, reverse, True)
    col = 1 if reverse else 0

    def body(gq_ref, gk_ref, gv_ref, la_ref, st_ref, do_ref, dq_ref, dk_ref, dv_ref, dla_ref, dS_scr):
        @pl.when(pl.program_id(0) == 0)
        def _():
            dS_scr[...] = jnp.zeros_like(dS_scr)

        is_lat = order(pl.program_id(0)) >= nc
        qmask, vmask, bd, ri, ci = _gla_masks()
        msk2, mskT2, b, bT, bm = _gla_decays(la_ref[...], reverse, ri, ci)
        q, k, v = gq_ref[...], gk_ref[...], gv_ref[...]
        do = jnp.where(is_lat, do_ref[...].astype(F32), 0.0)
        e_b, e_qm, e_km, e_kd, e_T = jnp.exp(b), jnp.exp(b - bm), jnp.exp(bm - b), jnp.exp(bT - b), jnp.exp(bT)
        qd, qm, km, kd = q * e_b, q * e_qm, k * e_km, k * e_kd
        qdb, qmb, kmb, kdb, vb, dob = (t.astype(_BF) for t in (qd, qm, km, kd, v, do))
        ST = jnp.tile(st_ref[0], (N_GLA, 1)) * bd
        dST = dS_scr[...]
        dSTb = dST.astype(_BF)
        dqd = _nn(dob, ST.astype(_BF))
        dkd = _nn(vb, dSTb)
        dv_t, dqm_t, dkm_t = [], [None, None], [None, None]
        for p in range(N_GLA // 2):
            t = p // 2
            qs = slice(128 * t, 128 * t + 128)
            vs = slice(128 * p, 128 * p + 128)
            j0 = (2 * p) % 4
            QS = _pair_stack(qm[:, qs], qmask[j0], qmask[j0 + 1])
            KS = _pair_stack(km[:, qs], qmask[j0], qmask[j0 + 1])
            VS = _pair_stack(v[:, vs], vmask[0], vmask[1])
            DS = _pair_stack(do[:, vs], vmask[0], vmask[1])
            ATT = jnp.where(mskT2, _nt(kmb[:, qs], QS), 0.0).astype(_BF)
            dAA = jnp.where(msk2, _nt(dob[:, vs], VS), 0.0).astype(_BF)
            dATT = jnp.where(mskT2, _nt(vb[:, vs], DS), 0.0).astype(_BF)
            dv_t.append(_nn(ATT, DS))
            dq_p = _nn(dAA, KS)
            dk_p = _nn(dATT, QS)
            dqm_t[t] = dq_p if dqm_t[t] is None else dqm_t[t] + dq_p
            dkm_t[t] = dk_p if dkm_t[t] is None else dkm_t[t] + dk_p
        dqm = jnp.concatenate(dqm_t, axis=1)
        dkm = jnp.concatenate(dkm_t, axis=1)
        dq_ref[...] = dqm * e_qm + dqd * e_b
        dk_ref[...] = dkm * e_km + dkd * e_kd
        dv_ref[...] = _nt(kdb, dSTb) + jnp.concatenate(dv_t, axis=1)
        db = dqm * qm - dkm * km + dqd * qd - dkd * kd
        dbT = jnp.sum(dkd * kd, axis=0, keepdims=True) + e_T * jnp.sum(dST * ST, axis=0, keepdims=True)
        dla_ref[...] = _tri_sum(mskT2[:, 0:T], db) + dbT
        dS_scr[...] = dST * e_T + bd * _tn(dob, qdb)

    blk = lambda w, c=0: pl.BlockSpec((T, w), lambda s: (order(s), c))
    do_spec = pl.BlockSpec((T, 512), lambda s: (jnp.maximum(order(s) - nc, 0), 0))
    return pl.pallas_call(
        body, name="gla_bwd_rev" if reverse else "gla_bwd", grid=(n,),
        in_specs=[blk(256), blk(256), blk(512), blk(256, col),
                  pl.BlockSpec((1, DV, 256), lambda s: (order(s), 0, 0)), do_spec],
        out_specs=[blk(256), blk(256), blk(512), blk(256)],
        out_shape=[jax.ShapeDtypeStruct((E, 256), F32), jax.ShapeDtypeStruct((E, 256), F32),
                   jax.ShapeDtypeStruct((E, 512), F32), jax.ShapeDtypeStruct((E, 256), F32)],
        scratch_shapes=[pltpu.VMEM((512, 256), F32)],
        compiler_params=_cp(("arbitrary",)),
    )(gq, gk, gv, la, st, do)


def _gla_out(o_f, o_b, gg, ggla, mavg):
    o = o_f + o_b
    rr = lax.rsqrt(_head_mean(o * o, mavg) + EPS)
    oh = o * rr
    sg = _sigmoid(gg)
    return oh, rr, sg


def _mix_fwd(x, attn, o_f, o_b, gg, ggla, mavg, wout, gt1, g2):
    S = x.shape[0]
    TM = 256

    def body(x_ref, a_ref, of_ref, ob_ref, gg_ref, ggla_ref, mavg_ref, w_ref, gt1_ref, g2_ref, x1_ref, mix_ref):
        gg_t = gg_ref[...]
        oh, _, sg = _gla_out(of_ref[...], ob_ref[...], gg_t, ggla_ref[...], mavg_ref[...])
        mix_ref[:, 0:512] = a_ref[...]
        mix_ref[:, 512:1024] = (oh * ggla_ref[...] * (gg_t * sg)).astype(_BF)
        y = _nn(mix_ref[...], w_ref[...])
        ry = lax.rsqrt(jnp.mean(y * y, axis=-1, keepdims=True) + EPS)
        x1_ref[...] = x_ref[...] + gt1_ref[...] * ((y * ry) * g2_ref[...])

    return pl.pallas_call(
        body, name="mix_fwd", grid=(S // TM,),
        in_specs=[_rows(TM, D), _rows(TM, 512), _rows(TM, 512, 1), _rows(TM, 512, 1), _rows(TM, 512, 1),
                  _full((1, 512)), _full((512, 512)), _full((D, D)), _full((1, D)), _full((1, D))],
        out_specs=[_rows(TM, D), _rows(TM, D)],
        out_shape=[jax.ShapeDtypeStruct((S, D), F32), jax.ShapeDtypeStruct((S, D), _BF)],
        compiler_params=_cp(("arbitrary",), 40 * 1024 * 1024),
    )(x, attn, o_f, o_b, gg, ggla, mavg, wout, gt1, g2)


def _mix_bwd(dx1, mix, o_f, o_b, gg, ggla, mavg, wout, gt1, g2):
    S = dx1.shape[0]
    TM = 256

    def body(dx_ref, mix_ref, of_ref, ob_ref, gg_ref, ggla_ref, mavg_ref, w_ref, gt1_ref, g2_ref,
             da_ref, do_ref, dgg_ref, dy_ref, sums_ref):
        @pl.when(pl.program_id(0) == 0)
        def _():
            sums_ref[...] = jnp.zeros_like(sums_ref)

        dx = dx_ref[...]
        y = _nn(mix_ref[...], w_ref[...])
        ry = lax.rsqrt(jnp.mean(y * y, axis=-1, keepdims=True) + EPS)
        yh = y * ry
        sums_ref[0:1, :] += jnp.sum(dx * yh, axis=0, keepdims=True)
        dyh = dx * (gt1_ref[...] * g2_ref[...])
        dy = (ry * (dyh - yh * jnp.mean(dyh * yh, axis=-1, keepdims=True))).astype(_BF)
        dy_ref[...] = dy
        dmix = _nt(dy, w_ref[...])
        da_ref[...] = dmix[:, 0:512].astype(_BF)
        dgla = dmix[:, 512:1024]
        gg_t = gg_ref[...]
        ggla_t = ggla_ref[...]
        oh, rr, sg = _gla_out(of_ref[...], ob_ref[...], gg_t, ggla_t, mavg_ref[...])
        dgg_ref[...] = (dgla * oh * ggla_t * (sg * (1.0 + gg_t * (1.0 - sg)))).astype(_BF)
        don = dgla * (gg_t * sg)
        sums_ref[1:2, 0:512] += jnp.sum(don * oh, axis=0, keepdims=True)
        doh = don * ggla_t
        do_ref[...] = (rr * (doh - oh * _head_mean(doh * oh, mavg_ref[...]))).astype(_BF)

    return pl.pallas_call(
        body, name="mix_bwd", grid=(S // TM,),
        in_specs=[_rows(TM, D), _rows(TM, D), _rows(TM, 512, 1), _rows(TM, 512, 1), _rows(TM, 512, 1),
                  _full((1, 512)), _full((512, 512)), _full((D, D)), _full((1, D)), _full((1, D))],
        out_specs=[_rows(TM, 512), _rows(TM, 512), _rows(TM, 512), _rows(TM, D), _full((8, D))],
        out_shape=[jax.ShapeDtypeStruct((S, 512), _BF), jax.ShapeDtypeStruct((S, 512), _BF),
                   jax.ShapeDtypeStruct((S, 512), _BF), jax.ShapeDtypeStruct((S, D), _BF),
                   jax.ShapeDtypeStruct((8, D), F32)],
        compiler_params=_cp(("arbitrary",), 40 * 1024 * 1024),
    )(dx1, mix, o_f, o_b, gg, ggla, mavg, wout, gt1, g2)


def _ffn(x1, target, gm2, sh2, gt2, g4, wffi, wffo):
    S = x1.shape[0]
    TF = 256

    def body(x_ref, t_ref, gm_ref, sh_ref, gt_ref, g4_ref, wi_hbm, wo_hbm,
             dx_ref, h_ref, du_ref, act_ref, df_ref, sums_ref, loss_ref, wi, wo, sem):
        @pl.when(pl.program_id(0) == 0)
        def _():
            c1 = pltpu.make_async_copy(wi_hbm, wi, sem.at[0])
            c2 = pltpu.make_async_copy(wo_hbm, wo, sem.at[1])
            c1.start()
            c2.start()
            sums_ref[...] = jnp.zeros_like(sums_ref)
            loss_ref[...] = jnp.zeros_like(loss_ref)
            c1.wait()
            c2.wait()

        x = x_ref[...]
        gm = gm_ref[...]
        r = lax.rsqrt(jnp.mean(x * x, axis=-1, keepdims=True) + EPS)
        xh = x * r
        hb = (xh * gm + sh_ref[...]).astype(_BF)
        h_ref[...] = hb
        u = _nn(hb, wi[...])
        g = u[:, 0:FFN]
        up = u[:, FFN:2 * FFN]
        sg = _sigmoid(g)
        sl = g * sg
        ab = (sl * up).astype(_BF)
        act_ref[...] = ab
        f = _nn(ab, wo[...])
        rf = lax.rsqrt(jnp.mean(f * f, axis=-1, keepdims=True) + EPS)
        fh = f * rf
        gt, g4v = gt_ref[...], g4_ref[...]
        err = x + gt * (fh * g4v) - t_ref[...]
        loss_ref[...] += jnp.sum(err * err) * (0.5 / D)
        dout = err * (1.0 / D)
        sums_ref[2:3, :] += jnp.sum(dout * fh, axis=0, keepdims=True)
        dfh = dout * (gt * g4v)
        dfb = (rf * (dfh - fh * jnp.mean(dfh * fh, axis=-1, keepdims=True))).astype(_BF)
        df_ref[...] = dfb
        dact = _nt(dfb, wo[...])
        du_ref[:, 0:FFN] = (dact * up * (sg * (1.0 + g * (1.0 - sg)))).astype(_BF)
        du_ref[:, FFN:2 * FFN] = (dact * sl).astype(_BF)
        dh = _nt(du_ref[...], wi[...])
        sums_ref[0:1, :] += jnp.sum(dh, axis=0, keepdims=True)
        sums_ref[1:2, :] += jnp.sum(dh * xh, axis=0, keepdims=True)
        dxh = dh * gm
        dx_ref[...] = dout + r * (dxh - xh * jnp.mean(dxh * xh, axis=-1, keepdims=True))

    vec = _full((1, D))
    anyspec = pl.BlockSpec(memory_space=pl.ANY)
    return pl.pallas_call(
        body, name="ffn_fwd_bwd", grid=(S // TF,),
        in_specs=[_rows(TF, D), _rows(TF, D), vec, vec, vec, vec, anyspec, anyspec],
        out_specs=[_rows(TF, D), _rows(TF, D), _rows(TF, 2 * FFN), _rows(TF, FFN), _rows(TF, D),
                   _full((8, D)), _full((8, 128))],
        out_shape=[jax.ShapeDtypeStruct((S, D), F32), jax.ShapeDtypeStruct((S, D), _BF),
                   jax.ShapeDtypeStruct((S, 2 * FFN), _BF), jax.ShapeDtypeStruct((S, FFN), _BF),
                   jax.ShapeDtypeStruct((S, D), _BF), jax.ShapeDtypeStruct((8, D), F32),
                   jax.ShapeDtypeStruct((8, 128), F32)],
        scratch_shapes=[pltpu.VMEM((D, 2 * FFN), _BF), pltpu.VMEM((FFN, D), _BF), pltpu.SemaphoreType.DMA((2,))],
        compiler_params=_cp(("arbitrary",), VMEM_BIG),
    )(x1, target, gm2, sh2, gt2, g4, wffi, wffo)


def _inproj_bwd(x, ctx, gml, gmc, win, wg, cos, sa, sb, la, dq, dk, dv, dgq_f, dgq_b, dgk_f, dgk_b, dgv_f, dgv_b,
                dgg, dla_f, dla_b, dx1):
    S = x.shape[0]
    E = S + CTX
    TE = CTX

    def body(x_ref, c_ref, gml_ref, gmc_ref, w_ref, wg_ref, cos_ref, sa_ref, sb_ref, la_ref, dq_ref, dk_ref, dv_ref,
             gqf, gqb, gkf, gkb, gvf, gvb, dgg_ref, dlf, dlb, dx1_ref,
             dp_ref, dlg_ref, gx_ref, sums_ref, bsum_ref):
        i = pl.program_id(0)
        is_ctx = i == 0

        @pl.when(is_ctx)
        def _():
            sums_ref[...] = jnp.zeros_like(sums_ref)
            bsum_ref[...] = jnp.zeros_like(bsum_ref)

        lat = jnp.where(is_ctx, 0.0, 1.0)
        cos_t, sa_t, sb_t = cos_ref[...], sa_ref[...], sb_ref[...]
        dp_ref[:, O_Q:O_K] = (_unrope(dq_ref[...].astype(F32), cos_t, sa_t, sb_t) * lat).astype(_BF)
        dp_ref[:, O_K:O_V] = _unrope(dk_ref[...].T, cos_t, sa_t, sb_t).astype(_BF)
        dp_ref[:, O_V:O_GQ] = dv_ref[...].T.astype(_BF)
        dp_ref[:, O_GQ:O_GK] = ((gqf[...] + gqb[...]) * (DK ** -0.5)).astype(_BF)
        dp_ref[:, O_GK:O_GV] = (gkf[...] + gkb[...]).astype(_BF)
        dp_ref[:, O_GV:O_GG] = (gvf[...] + gvb[...]).astype(_BF)
        dp_ref[:, O_GG:O_Z] = (dgg_ref[...].astype(F32) * lat).astype(_BF)
        la_t = la_ref[...]
        dlg = (jnp.concatenate([dlf[...], dlb[...]], axis=1) * (1.0 - jnp.exp(GATE_TAU * la_t)) * (1.0 / GATE_TAU))
        bsum_ref[0:1, :] += jnp.sum(dlg, axis=0, keepdims=True)
        dlgb = dlg.astype(_BF)
        dlg_ref[...] = dlgb
        dp_ref[:, O_Z:NP] = _nt(dlgb, wg_ref[...]).astype(_BF)
        dh = _nt(dp_ref[...], w_ref[...])
        x = jnp.where(is_ctx, c_ref[...], x_ref[...])
        r = lax.rsqrt(jnp.mean(x * x, axis=-1, keepdims=True) + EPS)
        xh = x * r
        sdh = jnp.sum(dh, axis=0, keepdims=True)
        sdx = jnp.sum(dh * xh, axis=0, keepdims=True)
        sums_ref[0:1, :] += sdh * lat
        sums_ref[1:2, :] += sdx * lat
        sums_ref[2:3, :] += sdh * (1.0 - lat)
        sums_ref[3:4, :] += sdx * (1.0 - lat)
        dxh = dh * jnp.where(is_ctx, gmc_ref[...], gml_ref[...])
        gx_ref[...] = dx1_ref[...] + r * (dxh - xh * jnp.mean(dxh * xh, axis=-1, keepdims=True))

    vec = _full((1, D))
    tab = _rows(TE, 128)
    return pl.pallas_call(
        body, name="inproj_bwd", grid=(E // TE,),
        in_specs=[_rows_lat(TE, D), _full((CTX, D)), vec, vec, _full((D, NP)), _full((128, 512)), tab, tab, tab,
                  _rows(TE, 512),
                  _rows_lat(TE, QP), pl.BlockSpec((KP, TE), lambda i: (0, i)), pl.BlockSpec((KP, TE), lambda i: (0, i)),
                  _rows(TE, 256), _rows(TE, 256), _rows(TE, 256), _rows(TE, 256), _rows(TE, 512), _rows(TE, 512),
                  _rows_lat(TE, 512), _rows(TE, 256), _rows(TE, 256), _rows_lat(TE, D)],
        out_specs=[_rows(TE, NP), _rows(TE, 512), _rows_lat(TE, D), _full((8, D)), _full((8, 512))],
        out_shape=[jax.ShapeDtypeStruct((E, NP), _BF), jax.ShapeDtypeStruct((E, 512), _BF),
                   jax.ShapeDtypeStruct((S, D), F32), jax.ShapeDtypeStruct((8, D), F32),
                   jax.ShapeDtypeStruct((8, 512), F32)],
        compiler_params=_cp(("arbitrary",), VMEM_BIG),
    )(x, ctx, gml, gmc, win, wg, cos, sa, sb, la, dq, dk, dv, dgq_f, dgq_b, dgk_f, dgk_b, dgv_f, dgv_b,
      dgg, dla_f, dla_b, dx1)


def _matmul_tn(a, b, tk, tn, tt, name):
    T, KA = a.shape
    N = b.shape[1]

    def body(a_ref, b_ref, o_ref):
        @pl.when(pl.program_id(2) == 0)
        def _():
            o_ref[...] = jnp.zeros_like(o_ref)

        o_ref[...] += _tn(a_ref[...], b_ref[...])

    return pl.pallas_call(
        body, name=name, grid=(KA // tk, N // tn, T // tt),
        in_specs=[pl.BlockSpec((tt, tk), lambda i, j, t: (t, i)), pl.BlockSpec((tt, tn), lambda i, j, t: (t, j))],
        out_specs=pl.BlockSpec((tk, tn), lambda i, j, t: (i, j)),
        out_shape=jax.ShapeDtypeStruct((KA, N), F32),
        compiler_params=_cp(("parallel", "parallel", "arbitrary"), VMEM_BIG),
    )(a, b)


def _ada_fwd(c_all, c_ctx, w_ada):
    n = w_ada.shape[1]

    def body(c_ref, cc_ref, w_ref, o_ref):
        c = jnp.concatenate([c_ref[...], jnp.broadcast_to(cc_ref[...], (8, D))], axis=0)
        o_ref[...] = _nn((c * _sigmoid(c)).astype(_BF), w_ref[...].astype(_BF))

    return pl.pallas_call(
        body, name="ada_fwd", in_specs=[_full((8, D)), _full((1, D)), _full((D, n))], out_specs=_full((16, n)),
        out_shape=jax.ShapeDtypeStruct((16, n), F32), grid=(1,), compiler_params=_cp(("arbitrary",)),
    )(c_all, c_ctx, w_ada)


def _ada_bwd(c_all, c_ctx, w_ada, d_all):
    n = w_ada.shape[1]

    def body(c_ref, cc_ref, w_ref, d_ref, gw_ref, t_ref):
        c = jnp.concatenate([c_ref[...], jnp.broadcast_to(cc_ref[...], (8, D))], axis=0)
        db = d_ref[...].astype(_BF)
        gw_ref[0] = _tn((c * _sigmoid(c)).astype(_BF), db)
        t_ref[...] = _nt(db[8:16], w_ref[...].astype(_BF))

    return pl.pallas_call(
        body, name="ada_bwd", in_specs=[_full((8, D)), _full((1, D)), _full((D, n)), _full((16, n))],
        out_specs=[_full((1, D, n)), _full((8, D))],
        out_shape=[jax.ShapeDtypeStruct((1, D, n), F32), jax.ShapeDtypeStruct((8, D), F32)], grid=(1,),
        compiler_params=_cp(("arbitrary",)),
    )(c_all, c_ctx, w_ada, d_all)


def _small_grads(s_in, s_ffn, s_mix, ada_l, ada_c, gains):
    def body(si, sf, sm, al, ac, g, o_ref):
        sdh_l, sdx_l, sdh_c, sdx_c = si[0:1], si[1:2], si[2:3], si[3:4]
        sdh2, sdx2, a2 = sf[0:1], sf[1:2], sf[2:3]
        a1 = sm[0:1]
        g1, g2, g3, g4 = g[0:1], g[1:2], g[2:3], g[3:4]
        sc1, gt1, sc2, gt2 = al[1:2], al[2:3], al[4:5], al[5:6]
        sc1c = ac[1:2]
        z = jnp.zeros((1, D), F32)
        rows = [sdh_l, sdx_l * g1, a1 * g2, sdh2, sdx2 * g3, a2 * g4,
                sdh_c, sdx_c * g1, z, z, z, z,
                sdx_l * (1.0 + sc1) + sdx_c * (1.0 + sc1c), a1 * gt1, sdx2 * (1.0 + sc2), a2 * gt2]
        for r, v in enumerate(rows):
            o_ref[r:r + 1, :] = v

    v8 = _full((8, D))
    return pl.pallas_call(
        body, name="small_grads", in_specs=[v8] * 6, out_specs=_full((16, D)),
        out_shape=jax.ShapeDtypeStruct((16, D), F32), grid=(1,), compiler_params=_cp(("arbitrary",)),
    )(s_in, s_ffn, s_mix, ada_l, ada_c, gains)


def _row_tile(R):
    for cand in (256, 128, 64, 32, 16):
        if R % cand == 0 and R > cand:
            return cand
    return R


def _adamw(w, g, m, v, name):
    _, R, C = w.shape
    tr = _row_tile(R)
    ns = g.shape[0]
    c1 = 1.0 / (1.0 - ADAM_B1 ** ADAM_STEP)
    c2 = 1.0 / (1.0 - ADAM_B2 ** ADAM_STEP)

    def body(w_ref, g_ref, m_ref, v_ref, go_ref, d_ref, nm_ref, nv_ref):
        gg = g_ref[0].astype(F32)
        for j in range(1, ns):
            gg = gg + g_ref[j].astype(F32)
        go_ref[0] = gg
        nm = ADAM_B1 * m_ref[0] + (1.0 - ADAM_B1) * gg
        nv = ADAM_B2 * v_ref[0] + (1.0 - ADAM_B2) * (gg * gg)
        nm_ref[0] = nm
        nv_ref[0] = nv
        d_ref[0] = -ADAM_LR * ((nm * c1) / (jnp.sqrt(nv * c2) + ADAM_EPS) + ADAM_WD * w_ref[0])

    spec = pl.BlockSpec((1, tr, C), lambda i: (0, i, 0))
    sds = jax.ShapeDtypeStruct((1, R, C), F32)
    return pl.pallas_call(
        body, name=name, grid=(R // tr,),
        in_specs=[spec, pl.BlockSpec((ns, tr, C), lambda i: (0, i, 0)), spec, spec],
        out_specs=[spec] * 4, out_shape=[sds] * 4, compiler_params=_cp(("parallel",)),
    )(w, g, m, v)


def _sum_slots(slots, name):
    _, R, C = slots.shape
    tr = _row_tile(R)

    def body(s_ref, o_ref):
        acc = s_ref[0].astype(F32)
        for j in range(1, N_DEV):
            acc = acc + s_ref[j].astype(F32)
        o_ref[...] = acc

    return pl.pallas_call(
        body, name=name, grid=(R // tr,), in_specs=[pl.BlockSpec((N_DEV, tr, C), lambda i: (0, i, 0))],
        out_specs=_rows(tr, C), out_shape=jax.ShapeDtypeStruct((R, C), F32), compiler_params=_cp(("parallel",)),
    )(slots)


def _allgather(x_shard, name):
    m_per, n = x_shard.shape

    def body(x_ref, out_ref, send_sems, recv_sems, local_sem):
        x, y, c = lax.axis_index("x"), lax.axis_index("y"), lax.axis_index("c")
        me, sibling = (x, y, c), (x, y, 1 - c)
        chips = [(1 - x, y), (x, 1 - y), (1 - x, 1 - y)]

        def rows(px, py, pc):
            return out_ref.at[pl.ds((4 * px + 2 * py + pc) * m_per, m_per), :]

        def copy(k, block, to, src=None):
            return pltpu.make_async_remote_copy(
                src_ref=rows(*block) if src is None else src, dst_ref=rows(*block),
                send_sem=send_sems.at[k], recv_sem=recv_sems.at[k], device_id=to, device_id_type=MESH)

        mine = pltpu.make_async_copy(x_ref, rows(*me), local_sem)
        mine.start()
        first = [copy(0, me, sibling, src=x_ref)]
        first += [copy(1 + j, me, (*chip, c), src=x_ref) for j, chip in enumerate(chips)]
        for cp in first:
            cp.start()
        passed = [copy(4 + j, (*chip, c), sibling) for j, chip in enumerate(chips)]
        for j, chip in enumerate(chips):
            copy(1 + j, (*chip, c), me).wait_recv()
            passed[j].start()
        copy(0, sibling, me).wait_recv()
        for j, chip in enumerate(chips):
            copy(4 + j, (*chip, 1 - c), me).wait_recv()
        for cp in first + passed:
            cp.wait_send()
        mine.wait()

    return pl.pallas_call(
        body, name=name, out_shape=jax.ShapeDtypeStruct((N_DEV * m_per, n), x_shard.dtype),
        in_specs=[pl.BlockSpec(memory_space=pltpu.VMEM)], out_specs=pl.BlockSpec(memory_space=pltpu.VMEM),
        scratch_shapes=[pltpu.SemaphoreType.DMA((7,)), pltpu.SemaphoreType.DMA((7,)), pltpu.SemaphoreType.DMA],
        compiler_params=pltpu.CompilerParams(vmem_limit_bytes=VMEM_BIG),
    )(x_shard)


def _alltoall(slabs, name):
    na = len(slabs)

    def body(*refs):
        ins, outs = refs[:na], refs[na:2 * na]
        send_sems, recv_sems, local_sems = refs[2 * na:]
        x, y, c = lax.axis_index("x"), lax.axis_index("y"), lax.axis_index("c")
        me = 4 * x + 2 * y + c
        copies = []
        for a in range(na):
            loc = pltpu.make_async_copy(ins[a].at[me], outs[a].at[me], local_sems.at[a])
            loc.start()
            copies.append(loc)
        rem = []
        for k in range(1, N_DEV):
            px, py, pc = x ^ (k >> 2), y ^ ((k >> 1) & 1), c ^ (k & 1)
            peer = 4 * px + 2 * py + pc
            for a in range(na):
                cp = pltpu.make_async_remote_copy(
                    src_ref=ins[a].at[peer], dst_ref=outs[a].at[me],
                    send_sem=send_sems.at[a, k - 1], recv_sem=recv_sems.at[a, k - 1],
                    device_id=(px, py, pc), device_id_type=MESH)
                cp.start()
                rem.append((a, k, peer, cp))
        for a, k, peer, cp in rem:
            pltpu.make_async_remote_copy(
                src_ref=ins[a].at[me], dst_ref=outs[a].at[peer],
                send_sem=send_sems.at[a, k - 1], recv_sem=recv_sems.at[a, k - 1],
                device_id=(x, y, c), device_id_type=MESH).wait_recv()
        for a, k, peer, cp in rem:
            cp.wait_send()
        for loc in copies:
            loc.wait()

    anyspec = pl.BlockSpec(memory_space=pl.ANY)
    return pl.pallas_call(
        body, name=name, out_shape=[jax.ShapeDtypeStruct(s.shape, s.dtype) for s in slabs],
        in_specs=[anyspec] * na, out_specs=[anyspec] * na,
        scratch_shapes=[pltpu.SemaphoreType.DMA((na, 7)), pltpu.SemaphoreType.DMA((na, 7)),
                        pltpu.SemaphoreType.DMA((na,))],
    )(*slabs)


def _rope_tables(S):
    t = np.arange(S)
    row = (t // GRID_W).astype(np.float32)
    colp = (t % GRID_W).astype(np.float32)
    half = HD // 2
    inv = (ROPE_BASE ** (-np.arange(0, half, 2, dtype=np.float32) / half)).astype(np.float32)
    ar = row[:, None] * inv[None, :]
    ac = colp[:, None] * inv[None, :]
    ang = np.concatenate([ar, ar, ac, ac], axis=-1).astype(np.float32)
    cos = np.cos(ang).astype(np.float32)
    sin = np.sin(ang).astype(np.float32)
    lane = np.arange(HD)
    first = (lane % 32) < 16
    sa = np.where(first[None, :], -sin, 0.0)
    sb = np.where(first[None, :], 0.0, sin)

    def ext(tab, ctx_val):
        full = np.zeros((CTX + S, 128), np.float32)
        full[:CTX, :HD] = ctx_val
        full[CTX:, :HD] = tab
        return jnp.asarray(full)

    return ext(cos, 1.0), ext(sa, 0.0), ext(sb, 0.0)


def _pad_cols_win(w):
    q, k, v, rest, z = w[:, 0:512], w[:, 512:640], w[:, 640:768], w[:, 768:2304], w[:, 2304:2336]

    def padh(t, nh):
        t = t.reshape(D, nh, HD)
        return jnp.pad(t, ((0, 0), (0, 0), (0, 128 - HD))).reshape(D, nh * 128)

    return jnp.concatenate([padh(q, N_ATT), padh(k, N_KV), padh(v, N_KV), rest, jnp.pad(z, ((0, 0), (0, 96)))], axis=1)


def _unpad_cols_win(g):
    def unp(t, nh):
        return t.reshape(D, nh, 128)[:, :, :HD].reshape(D, nh * HD)

    return jnp.concatenate([unp(g[:, O_Q:O_K], N_ATT), unp(g[:, O_K:O_V], N_KV), unp(g[:, O_V:O_GQ], N_KV),
                            g[:, O_GQ:O_Z], g[:, O_Z:O_Z + 32]], axis=1)


def _local_step(x, ctx, target, ada_l, ada_c, gains, sink, win_p, wg_bd, bg, ggla, wout, wffi_sh, wffo_sh):
    S = x.shape[0]
    cos, sa, sb = _rope_tables(S)
    g1, g2, g3, g4 = (gains[i:i + 1] for i in range(4))
    sh1, sc1, gt1, sh2, sc2, gt2 = (ada_l[i:i + 1] for i in range(6))
    sh1c, sc1c = ada_c[0:1], ada_c[1:2]
    gml, gmc, gm2 = g1 * (1.0 + sc1), g1 * (1.0 + sc1c), g3 * (1.0 + sc2)
    mavg = jnp.asarray(np.kron(np.eye(N_GLA, dtype=np.float32), np.full((DV, DV), 1.0 / DV, np.float32))).astype(_BF)

    h, q, k, v, gq, gk, gv, gg, z, la = _inproj_fwd(x, ctx, gml, sh1, gmc, sh1c, win_p, wg_bd, bg, cos, sa, sb)
    n_ffi, r_ffo = wffi_sh.shape[1], wffo_sh.shape[0]
    tt_e = 768 if (S + CTX) % 768 == 0 else 256
    tt_s = 512 if S % 512 == 0 else 256
    attn, lse, wffi_g, wffo_g = _attn_fwd(q, k, v, sink, [wffi_sh, wffo_sh])
    wffi = wffi_g.transpose(1, 0, 2).reshape(D, N_DEV * n_ffi)
    wffo = wffo_g.reshape(N_DEV * r_ffo, D)
    o_f, st_f = _gla_fwd(gq, gk, gv, la, False)
    o_b, st_b = _gla_fwd(gq, gk, gv, la, True)
    x1, mix = _mix_fwd(x, attn, o_f, o_b, gg, ggla, mavg, wout, gt1, g2)
    dx1, h2, du, act, df, s_ffn, loss = _ffn(x1, target, gm2, sh2, gt2, g4, wffi, wffo)
    g_wffi = _matmul_tn(h2, du, 512, 2 * FFN, tt_s, "grad_w_ffn_in")
    g_wffo = _matmul_tn(act, df, FFN, D, tt_s, "grad_w_ffn_out")
    slab_ffi = g_wffi.reshape(D, N_DEV, n_ffi).transpose(1, 0, 2).astype(_BF)
    slab_ffo = g_wffo.reshape(N_DEV, r_ffo, D).astype(_BF)
    d_attn, do_gla, dgg, dy, s_mix = _mix_bwd(dx1, mix, o_f, o_b, gg, ggla, mavg, wout, gt1, g2)
    dq, dk, dv, dsink, got_ffi, got_ffo = _attn_bwd(q, k, v, sink, lse, d_attn, [slab_ffi, slab_ffo])
    dgq_f, dgk_f, dgv_f, dla_f = _gla_bwd(gq, gk, gv, la, st_f, do_gla, False)
    dgq_b, dgk_b, dgv_b, dla_b = _gla_bwd(gq, gk, gv, la, st_b, do_gla, True)
    dp, dlg, grad_x, s_in, s_bg = _inproj_bwd(x, ctx, gml, gmc, win_p, wg_bd, cos, sa, sb, la, dq, dk, dv,
                                              dgq_f, dgq_b, dgk_f, dgk_b, dgv_f, dgv_b, dgg, dla_f, dla_b, dx1)
    g_win = _matmul_tn(h, dp, D, NP, tt_e, "grad_w_in")
    g_wg = _matmul_tn(z, dlg, 128, 512, tt_e, "grad_w_gate")
    g_wout = _matmul_tn(mix, dy, D, D, tt_s, "grad_w_out")
    small = _small_grads(s_in, s_ffn, s_mix, ada_l, ada_c, gains)
    return dict(loss=loss[0, 0], grad_x=grad_x, g_win=g_win, g_wg=g_wg, g_wout=g_wout, got_ffi=got_ffi,
                got_ffo=got_ffo, small=small, dsink=dsink[:, 0], dbg=s_bg[0], dggla=s_mix[1, 0:512])


SMALL_ROWS = 32


def _pack_small(c_ctx, b_ada, g1, g2, g3, g4, sink, bgf, bgb, ggla, wgf, wgb):
    def row(v):
        v = v.reshape(-1)
        return jnp.pad(v, (0, D - v.shape[0])).reshape(1, D)

    rows = [c_ctx.reshape(1, D), b_ada.reshape(6, D), g1.reshape(1, D), g2.reshape(1, D), g3.reshape(1, D),
            g4.reshape(1, D), row(sink), row(jnp.concatenate([bgf.reshape(-1), bgb.reshape(-1)])), row(ggla),
            row(wgf), row(wgb)]
    out = jnp.concatenate(rows, axis=0)
    return jnp.pad(out, ((0, SMALL_ROWS - out.shape[0]), (0, 0)))


def _unpack_small(p):
    return dict(c_ctx=p[0], b_ada=p[1:7].reshape(1, 6 * D), g_pre_mix=p[7:8], g_post_mix=p[8:9],
                g_pre_ffn=p[9:10], g_post_ffn=p[10:11], attn_sink=p[11:12, 0:8],
                b_gate_fwd=p[12:13, 0:256], b_gate_bwd=p[12:13, 256:512], g_gla_norm=p[13:14, 0:64],
                w_gate_fwd=p[14, 0:512].reshape(1, GATE_RANK, 32), w_gate_bwd=p[15, 0:512].reshape(1, GATE_RANK, 32))


def kernel(x, c, ctx, c_ctx, w_ada, b_ada, g_pre_mix, g_post_mix, g_pre_ffn, g_post_ffn, w_in, attn_sink, w_gate_fwd, b_gate_fwd, w_gate_bwd, b_gate_bwd, g_gla_norm, w_out, w_ffn_in, w_ffn_out, loss_target, m_c_ctx, m_w_ada, m_b_ada, m_g_pre_mix, m_g_post_mix, m_g_pre_ffn, m_g_post_ffn, m_w_in, m_attn_sink, m_w_gate_fwd, m_b_gate_fwd, m_w_gate_bwd, m_b_gate_bwd, m_g_gla_norm, m_w_out, m_w_ffn_in, m_w_ffn_out, v_c_ctx, v_w_ada, v_b_ada, v_g_pre_mix, v_g_post_mix, v_g_pre_ffn, v_g_post_ffn, v_w_in, v_attn_sink, v_w_gate_fwd, v_b_gate_fwd, v_w_gate_bwd, v_b_gate_bwd, v_g_gla_norm, v_w_out, v_w_ffn_in, v_w_ffn_out):
    me = 4 * lax.axis_index("x") + 2 * lax.axis_index("y") + lax.axis_index("c")
    S = x.shape[1]
    n_in = w_in.shape[2]
    n_ffi = w_ffn_in.shape[2]
    r_out = w_out.shape[1]
    r_ffo = w_ffn_out.shape[1]
    n_ada = w_ada.shape[2]

    small_in = jnp.concatenate([c.reshape(8, 128), w_gate_fwd.reshape(4, 128), w_gate_bwd.reshape(4, 128)], axis=0)
    sg = _allgather(small_in, "gather_small").reshape(N_DEV, 16, 128)
    c_all = sg[:, 0:8].reshape(N_DEV, D)
    wgf = sg[:, 8:12].reshape(N_DEV, GATE_RANK, 32).transpose(1, 0, 2).reshape(GATE_RANK, 256)
    wgb = sg[:, 12:16].reshape(N_DEV, GATE_RANK, 32).transpose(1, 0, 2).reshape(GATE_RANK, 256)

    pack = jnp.concatenate([w_in[0].astype(_BF).reshape(-1, 128), w_out[0].astype(_BF).reshape(-1, 128)], axis=0)
    rows_per = pack.shape[0]
    wg_all = _allgather(pack, "gather_weights").reshape(N_DEV, rows_per, 128)
    o1 = D * n_in // 128
    win_full = wg_all[:, 0:o1].reshape(N_DEV, D, n_in).transpose(1, 0, 2).reshape(D, N_DEV * n_in)
    wout_full = wg_all[:, o1:].reshape(N_DEV * r_out, D)
    win_p = _pad_cols_win(win_full)
    wg_bd = jnp.zeros((128, 512), F32).at[0:16, 0:256].set(wgf).at[16:32, 256:512].set(wgb).astype(_BF)

    ada_part = _ada_fwd(c_all, c_ctx.reshape(1, D), w_ada[0])
    ada_all = _allgather(ada_part, "gather_ada").reshape(N_DEV, 16, n_ada)
    ada_full = ada_all.transpose(1, 0, 2).reshape(16, N_DEV * n_ada) + b_ada
    ada_l = jnp.pad(lax.dynamic_slice_in_dim(ada_full, me, 1, 0).reshape(6, D), ((0, 2), (0, 0)))
    ada_c = jnp.pad(ada_full[8].reshape(6, D), ((0, 2), (0, 0)))
    gains = jnp.pad(jnp.concatenate([g_pre_mix, g_post_mix, g_pre_ffn, g_post_ffn], axis=0), ((0, 4), (0, 0)))
    sink = jnp.broadcast_to(attn_sink.reshape(8, 1), (8, 128))
    bg = jnp.concatenate([b_gate_fwd, b_gate_bwd], axis=1)
    ggla = jnp.tile(g_gla_norm, (1, N_GLA))

    r = _local_step(x[0], ctx[0], loss_target[0], ada_l, ada_c, gains, sink, win_p, wg_bd, bg, ggla,
                    wout_full, w_ffn_in[0].astype(_BF), w_ffn_out[0].astype(_BF))

    loss = lax.psum(r["loss"], ("x", "y", "c"))

    g_win = _unpad_cols_win(r["g_win"]).reshape(D, N_DEV, n_in).transpose(1, 0, 2).astype(_BF)
    g_wout = r["g_wout"].reshape(N_DEV, r_out, D).astype(_BF)
    slots = _alltoall([g_win, g_wout], "scatter_grads")

    def row(vv):
        vv = vv.reshape(-1)
        return jnp.pad(vv, (0, D - vv.shape[0])).reshape(1, D)

    g_wgf = r["g_wg"][0:16, 0:256]
    g_wgb = r["g_wg"][16:32, 256:512]
    part = jnp.concatenate([r["small"], row(r["dsink"]), row(r["dbg"]), row(r["dggla"].reshape(N_GLA, DV).sum(0)),
                            g_wgf.reshape(4, D), g_wgb.reshape(4, D), jnp.zeros((5, D), F32)], axis=0)
    parts = _allgather(part, "gather_small_grads").reshape(N_DEV, 32, D)
    tot = _sum_slots(parts, "sum_small_grads")
    d_ada_rows = parts[:, 0:6].reshape(N_DEV, 6 * D)
    d_ada_c = tot[6:12].reshape(1, 6 * D)
    my_cols = lax.dynamic_slice_in_dim(jnp.concatenate([d_ada_rows, jnp.broadcast_to(d_ada_c, (1, 6 * D)),
                                                        jnp.zeros((7, 6 * D), F32)], axis=0), me * n_ada, n_ada, 1)
    grad_w_ada, t_part = _ada_bwd(c_all, c_ctx.reshape(1, D), w_ada[0], my_cols)
    t_all = _allgather(t_part, "gather_c_ctx").reshape(N_DEV, 8, D)
    t_tot = _sum_slots(t_all, "sum_c_ctx")[0]
    sc = 1.0 / (1.0 + jnp.exp(-c_ctx))
    grad_c_ctx = t_tot * (sc * (1.0 + c_ctx * (1.0 - sc)))
    grad_b_ada = (tot[0:6] + tot[6:12]).reshape(1, 6 * D)
    g_wgf_tot = tot[19:23].reshape(GATE_RANK, 256)
    g_wgb_tot = tot[23:27].reshape(GATE_RANK, 256)
    grads_small = dict(
        c_ctx=grad_c_ctx, b_ada=grad_b_ada, g_pre_mix=tot[12:13], g_post_mix=tot[13:14], g_pre_ffn=tot[14:15],
        g_post_ffn=tot[15:16], attn_sink=tot[16:17, 0:8], b_gate_fwd=tot[17:18, 0:256], b_gate_bwd=tot[17:18, 256:512],
        g_gla_norm=tot[18:19, 0:64],
        w_gate_fwd=lax.dynamic_slice_in_dim(g_wgf_tot, me * 32, 32, 1).reshape(1, GATE_RANK, 32),
        w_gate_bwd=lax.dynamic_slice_in_dim(g_wgb_tot, me * 32, 32, 1).reshape(1, GATE_RANK, 32))

    names_small = ["c_ctx", "b_ada", "g_pre_mix", "g_post_mix", "g_pre_ffn", "g_post_ffn", "attn_sink",
                   "b_gate_fwd", "b_gate_bwd", "g_gla_norm", "w_gate_fwd", "w_gate_bwd"]
    def packd(dd):
        return _pack_small(dd["c_ctx"], dd["b_ada"], dd["g_pre_mix"], dd["g_post_mix"], dd["g_pre_ffn"],
                           dd["g_post_ffn"], dd["attn_sink"], dd["b_gate_fwd"], dd["b_gate_bwd"], dd["g_gla_norm"],
                           dd["w_gate_fwd"], dd["w_gate_bwd"])

    w_small = dict(c_ctx=c_ctx, b_ada=b_ada, g_pre_mix=g_pre_mix, g_post_mix=g_post_mix, g_pre_ffn=g_pre_ffn,
                   g_post_ffn=g_post_ffn, attn_sink=attn_sink, b_gate_fwd=b_gate_fwd, b_gate_bwd=b_gate_bwd,
                   g_gla_norm=g_gla_norm, w_gate_fwd=w_gate_fwd, w_gate_bwd=w_gate_bwd)
    m_small = dict(c_ctx=m_c_ctx, b_ada=m_b_ada, g_pre_mix=m_g_pre_mix, g_post_mix=m_g_post_mix,
                   g_pre_ffn=m_g_pre_ffn, g_post_ffn=m_g_post_ffn, attn_sink=m_attn_sink, b_gate_fwd=m_b_gate_fwd,
                   b_gate_bwd=m_b_gate_bwd, g_gla_norm=m_g_gla_norm, w_gate_fwd=m_w_gate_fwd, w_gate_bwd=m_w_gate_bwd)
    v_small = dict(c_ctx=v_c_ctx, b_ada=v_b_ada, g_pre_mix=v_g_pre_mix, g_post_mix=v_g_post_mix,
                   g_pre_ffn=v_g_pre_ffn, g_post_ffn=v_g_post_ffn, attn_sink=v_attn_sink, b_gate_fwd=v_b_gate_fwd,
                   b_gate_bwd=v_b_gate_bwd, g_gla_norm=v_g_gla_norm, w_gate_fwd=v_w_gate_fwd, w_gate_bwd=v_w_gate_bwd)
    v_pack = packd(v_small)
    v_pack = jnp.where(packd(jax.tree.map(jnp.ones_like, v_small)) > 0, v_pack, 1.0)
    _, d_s, nm_s, nv_s = _adamw(packd(w_small)[None], packd(grads_small)[None], packd(m_small)[None], v_pack[None],
                                "adamw_small")
    d_s, nm_s, nv_s = _unpack_small(d_s[0]), _unpack_small(nm_s[0]), _unpack_small(nv_s[0])

    big = {}
    for nm, w, g, m, v in [("w_ada", w_ada, grad_w_ada, m_w_ada, v_w_ada), ("w_in", w_in, slots[0], m_w_in, v_w_in),
                           ("w_out", w_out, slots[1], m_w_out, v_w_out),
                           ("w_ffn_in", w_ffn_in, r["got_ffi"], m_w_ffn_in, v_w_ffn_in),
                           ("w_ffn_out", w_ffn_out, r["got_ffo"], m_w_ffn_out, v_w_ffn_out)]:
        big[nm] = _adamw(w, g, m, v, "adamw_" + nm)

    order = ["c_ctx", "w_ada", "b_ada", "g_pre_mix", "g_post_mix", "g_pre_ffn", "g_post_ffn", "w_in", "attn_sink",
             "w_gate_fwd", "b_gate_fwd", "w_gate_bwd", "b_gate_bwd", "g_gla_norm", "w_out", "w_ffn_in", "w_ffn_out"]
    grads, deltas, new_m, new_v = [], [], [], []
    for nm in order:
        if nm in big:
            g_, d_, m_, v_ = big[nm]
        else:
            g_, d_, m_, v_ = grads_small[nm], d_s[nm], nm_s[nm], nv_s[nm]
        grads.append(g_)
        deltas.append(d_)
        new_m.append(m_)
        new_v.append(v_)
    return (loss, r["grad_x"][None], *grads, *deltas, *new_m, *new_v)
```

```python
import functools
import math

import numpy as np
import jax
import jax.numpy as jnp
from jax import lax
from jax.experimental import pallas as pl
from jax.experimental.pallas import tpu as pltpu

F32 = jnp.float32
_BF = jnp.bfloat16

N_DEV = 8
D = 1024
CTX = 256
HD = 64
N_ATT = 8
N_KV = 2
GRP = N_ATT // N_KV
WIN = 128
GRID_W = 64
ROPE_BASE = 10000.0
N_GLA = 8
DK = 32
DV = 64
GATE_RANK = 16
GATE_TAU = 16.0
FFN = 2816
EPS = 1e-6
NEG = -1e30
GLA_T = 128

QP = N_ATT * 128
KP = N_KV * 128
O_Q, O_K, O_V = 0, QP, QP + KP
O_GQ = O_V + KP
O_GK = O_GQ + N_GLA * DK
O_GV = O_GK + N_GLA * DK
O_GG = O_GV + N_GLA * DV
O_Z = O_GG + N_GLA * DV
NP = O_Z + 128
IN_COLS = 2336

ADAM_LR, ADAM_B1, ADAM_B2, ADAM_EPS, ADAM_WD, ADAM_STEP = 0.001, 0.9, 0.999, 1e-08, 0.01, 10

VMEM_BIG = 56 * 1024 * 1024
MESH = pl.DeviceIdType.MESH


def _cp(sem, vmem=None):
    return pltpu.CompilerParams(dimension_semantics=sem, vmem_limit_bytes=vmem)


def _full(shape):
    nd = len(shape)
    return pl.BlockSpec(shape, lambda *a: (0,) * nd)


def _rows(tile, width, off=0):
    return pl.BlockSpec((tile, width), lambda i: (i + off, 0))


def _rows_lat(tile, width):
    return pl.BlockSpec((tile, width), lambda i: (jnp.maximum(i - 1, 0), 0))


def _nt(a, b):
    return lax.dot_general(a, b, (((1,), (1,)), ((), ())), preferred_element_type=F32)


def _tn(a, b):
    return lax.dot_general(a, b, (((0,), (0,)), ((), ())), preferred_element_type=F32)


def _nn(a, b):
    return jnp.dot(a, b, preferred_element_type=F32)


def _head_mean(x, mavg):
    n = x.shape[0]
    hi = x.astype(_BF)
    lo = (x - hi.astype(F32)).astype(_BF)
    y = _nn(jnp.concatenate([hi, lo], axis=0), mavg)
    return y[0:n] + y[n:2 * n]


def _rope(t, cos, sa, sb):
    n = t.shape[1]
    reps = n // 128
    c = jnp.tile(cos, (1, reps))
    a = jnp.tile(sa, (1, reps))
    b = jnp.tile(sb, (1, reps))
    return t * c + pltpu.roll(t, n - 16, 1) * a + pltpu.roll(t, 16, 1) * b


def _unrope(t, cos, sa, sb):
    n = t.shape[1]
    reps = n // 128
    c = jnp.tile(cos, (1, reps))
    a = jnp.tile(sa, (1, reps))
    b = jnp.tile(sb, (1, reps))
    return t * c + pltpu.roll(t * a, 16, 1) + pltpu.roll(t * b, n - 16, 1)


def _sigmoid(x):
    return 1.0 / (1.0 + jnp.exp(-x))


def _inproj_fwd(x, ctx, gml, shl, gmc, shc, win, wg, bg, cos, sa, sb, shards):
    E = x.shape[0] + CTX
    TE = CTX

    def body(x_ref, c_ref, gml_ref, shl_ref, gmc_ref, shc_ref, w_ref, wg_ref, bg_ref, cos_ref, sa_ref, sb_ref,
             h_ref, q_ref, k_ref, v_ref, gq_ref, gk_ref, gv_ref, gg_ref, z_ref, la_ref):
        is_ctx = pl.program_id(0) == 0
        gm = jnp.where(is_ctx, gmc_ref[...], gml_ref[...])
        sh = jnp.where(is_ctx, shc_ref[...], shl_ref[...])
        x = jnp.where(is_ctx, c_ref[...], x_ref[...])
        r = lax.rsqrt(jnp.mean(x * x, axis=-1, keepdims=True) + EPS)
        hb = ((x * r) * gm + sh).astype(_BF)
        h_ref[...] = hb
        p = _nn(hb, w_ref[...])
        cos_t, sa_t, sb_t = cos_ref[...], sa_ref[...], sb_ref[...]
        q_ref[...] = (_rope(p[:, O_Q:O_K], cos_t, sa_t, sb_t) * (HD ** -0.5)).astype(_BF)
        k_ref[...] = _rope(p[:, O_K:O_V], cos_t, sa_t, sb_t).astype(_BF)
        v_ref[...] = p[:, O_V:O_GQ].astype(_BF)
        gq_ref[...] = p[:, O_GQ:O_GK] * (DK ** -0.5)
        gk_ref[...] = p[:, O_GK:O_GV]
        gv_ref[...] = p[:, O_GV:O_GG]
        gg_ref[...] = p[:, O_GG:O_Z]
        zb = p[:, O_Z:NP].astype(_BF)
        z_ref[...] = zb
        lg = _nn(zb, wg_ref[...]) + bg_ref[...]
        la_ref[...] = (jnp.minimum(lg, 0.0) - jnp.log(1.0 + jnp.exp(-jnp.abs(lg)))) * (1.0 / GATE_TAU)

    vec = _full((1, D))
    tab = _rows(TE, 128)
    outs = [(D, _BF), (QP, _BF), (KP, _BF), (KP, _BF), (256, F32), (256, F32), (512, F32), (512, F32),
            (128, _BF), (512, F32)]
    return _hosted_call(
        body, (x, ctx, gml, shl, gmc, shc, win, wg, bg, cos, sa, sb), shards, True,
        name="inproj_fwd", grid=(E // TE,),
        in_specs=[_rows_lat(TE, D), _full((CTX, D)), vec, vec, vec, vec, _full((D, NP)), _full((128, 512)),
                  _full((1, 512)), tab, tab, tab],
        out_specs=[_rows(TE, w) for w, _ in outs],
        out_shape=[jax.ShapeDtypeStruct((E, w), dt) for w, dt in outs],
        compiler_params=_cp(("arbitrary",), 40 * 1024 * 1024))


def _xchg_scratch(na):
    return [pltpu.SemaphoreType.DMA((na, N_DEV - 1)), pltpu.SemaphoreType.DMA((na, N_DEV - 1)),
            pltpu.SemaphoreType.DMA((na,))]


def _xchg_copies(ins, outs, send_sems, recv_sems, local_sems, gather):
    x, y, c = lax.axis_index("x"), lax.axis_index("y"), lax.axis_index("c")
    me = 4 * x + 2 * y + c
    local, sends, recvs = [], [], []
    for a in range(len(ins)):
        local.append(pltpu.make_async_copy(ins[a] if gather else ins[a].at[me], outs[a].at[me], local_sems.at[a]))
    for k in range(1, N_DEV):
        px, py, pc = x ^ (k >> 2), y ^ ((k >> 1) & 1), c ^ (k & 1)
        peer = 4 * px + 2 * py + pc
        for a in range(len(ins)):
            sems = dict(send_sem=send_sems.at[a, k - 1], recv_sem=recv_sems.at[a, k - 1], device_id_type=MESH)
            sends.append(pltpu.make_async_remote_copy(
                src_ref=ins[a] if gather else ins[a].at[peer], dst_ref=outs[a].at[me], device_id=(px, py, pc), **sems))
            recvs.append(pltpu.make_async_remote_copy(
                src_ref=ins[a] if gather else ins[a].at[me], dst_ref=outs[a].at[peer], device_id=(x, y, c), **sems))
    return local, sends, recvs


def _xchg_start(cps):
    local, sends, _ = cps
    for cp in local + sends:
        cp.start()


def _xchg_finish(cps):
    local, sends, recvs = cps
    for cp in recvs:
        cp.wait_recv()
    for cp in sends:
        cp.wait_send()
    for cp in local:
        cp.wait()


def _xchg_out_shapes(ins, gather):
    return [jax.ShapeDtypeStruct(((N_DEV,) + s.shape) if gather else s.shape, s.dtype) for s in ins]


def _hosted_call(body, args, hosted, gather, *, grid, in_specs, out_specs, out_shape, scratch_shapes=(), **kw):
    na = len(hosted)
    if na == 0:
        return pl.pallas_call(body, grid=grid, in_specs=in_specs, out_specs=out_specs, out_shape=out_shape,
                              scratch_shapes=list(scratch_shapes), **kw)(*args)
    n_in, n_out, n_scr = len(in_specs), len(out_specs), len(scratch_shapes)

    def wrapped(*refs):
        ins, h_in = refs[:n_in], refs[n_in:n_in + na]
        outs, h_out = refs[n_in + na:n_in + na + n_out], refs[n_in + na + n_out:n_in + 2 * na + n_out]
        scr = refs[n_in + 2 * na + n_out:]
        cps = _xchg_copies(h_in, h_out, *scr[n_scr:], gather=gather)
        step = pl.program_id(0)

        @pl.when(step == 0)
        def _():
            _xchg_start(cps)

        body(*ins, *outs, *scr[:n_scr])

        @pl.when(step == grid[0] - 1)
        def _():
            _xchg_finish(cps)

    anyspec = pl.BlockSpec(memory_space=pl.ANY)
    return pl.pallas_call(
        wrapped, grid=grid, in_specs=list(in_specs) + [anyspec] * na, out_specs=list(out_specs) + [anyspec] * na,
        out_shape=list(out_shape) + _xchg_out_shapes(hosted, gather),
        scratch_shapes=list(scratch_shapes) + _xchg_scratch(na), **kw)(*args, *hosted)


def _attn_specs(E):
    nb = (E - CTX) // WIN
    last = E // WIN - 1
    kc = pl.BlockSpec((CTX, KP), lambda n: (0, 0))
    kp = pl.BlockSpec((WIN, KP), lambda n: (n + 1, 0))
    kk = pl.BlockSpec((WIN, KP), lambda n: (n + 2, 0))
    kn = pl.BlockSpec((WIN, KP), lambda n: (jnp.minimum(n + 3, last), 0))
    return nb, [kc, kp, kk, kn]


def _attn_bias(nb):
    rows = np.arange(GRP * WIN)[:, None] % WIN
    cols = np.arange(CTX + 3 * WIN)[None, :]
    j = cols - CTX
    band = np.abs(j - WIN - rows) <= WIN
    out = []
    for first, last in ((True, False), (False, False), (False, True)):
        ok = (cols < CTX) | (band & ((j >= WIN) | (not first)) & ((j < 2 * WIN) | (not last)))
        out.append(np.where(ok, 0.0, NEG).astype(np.float32))
    bias = jnp.asarray(np.stack(out))
    spec = pl.BlockSpec((1, GRP * WIN, CTX + 3 * WIN),
                        lambda n: (jnp.where(n == 0, 0, jnp.where(n == nb - 1, 2, 1)), 0, 0))
    return bias, spec


def _attn_fwd(q, k, v, sink, shards):
    E = q.shape[0]
    S = E - CTX
    nb, kspecs = _attn_specs(E)
    na = len(shards)

    def body(q_ref, kc, kp, kk, kn, vc, vp, vk, vn, sink_ref, bias_ref, *rest):
        shard_refs, (o_ref, lse_ref), got_refs = rest[:na], rest[na:na + 2], rest[na + 2:2 * na + 2]
        n = pl.program_id(0)
        cps = _xchg_copies(shard_refs, got_refs, *rest[2 * na + 2:], gather=True)

        @pl.when(n == 0)
        def _():
            _xchg_start(cps)

        lane = lax.broadcasted_iota(jnp.int32, (WIN, 128), 1)
        lse_t = jnp.zeros((WIN, 128), F32)
        hs = [slice(128 * h, 128 * h + 128) for h in range(N_KV)]
        K = [jnp.concatenate([kc[:, s_], kp[:, s_], kk[:, s_], kn[:, s_]], axis=0) for s_ in hs]
        Q = [jnp.concatenate([q_ref[:, 128 * (GRP * h + g):128 * (GRP * h + g) + 128] for g in range(GRP)], axis=0)
             for h in range(N_KV)]
        sk = [jnp.concatenate([jnp.broadcast_to(sink_ref[GRP * h + g:GRP * h + g + 1, 0:1], (WIN, 1))
                               for g in range(GRP)], axis=0) for h in range(N_KV)]
        s = [_nt(Q[h], K[h]) + bias_ref[0] for h in range(N_KV)]
        m = [jnp.maximum(jnp.max(s[h], axis=1, keepdims=True), sk[h]) for h in range(N_KV)]
        e = [jnp.exp(s[h] - m[h]) for h in range(N_KV)]
        den = [jnp.sum(e[h], axis=1, keepdims=True) + jnp.exp(sk[h] - m[h]) for h in range(N_KV)]
        V = [jnp.concatenate([vc[:, s_], vp[:, s_], vk[:, s_], vn[:, s_]], axis=0) for s_ in hs]
        o = [_nn((e[h] * (1.0 / den[h])).astype(_BF), V[h]) for h in range(N_KV)]
        for h in range(N_KV):
            lse = m[h] + jnp.log(den[h])
            for g in range(GRP):
                lse_t = jnp.where(lane == GRP * h + g, lse[WIN * g:WIN * g + WIN], lse_t)
            for pp in range(GRP // 2):
                a = o[h][WIN * 2 * pp:WIN * 2 * pp + WIN]
                b = o[h][WIN * (2 * pp + 1):WIN * (2 * pp + 1) + WIN]
                t = 2 * h + pp
                o_ref[:, 128 * t:128 * t + 128] = (a + pltpu.roll(b, 64, 1)).astype(_BF)
        lse_ref[...] = lse_t

        @pl.when(n == nb - 1)
        def _():
            _xchg_finish(cps)

    qs = pl.BlockSpec((WIN, QP), lambda n: (n + 2, 0))
    anyspec = pl.BlockSpec(memory_space=pl.ANY)
    bias, bias_spec = _attn_bias(nb)
    return pl.pallas_call(
        body, name="attn_fwd", grid=(nb,),
        in_specs=[qs] + kspecs + kspecs + [_full((8, 128)), bias_spec] + [anyspec] * na,
        out_specs=[_rows(WIN, 512), _rows(WIN, 128)] + [anyspec] * na,
        out_shape=[jax.ShapeDtypeStruct((S, 512), _BF), jax.ShapeDtypeStruct((S, 128), F32)]
        + _xchg_out_shapes(shards, True),
        scratch_shapes=_xchg_scratch(na),
        compiler_params=_cp(("arbitrary",)),
    )(q, k, k, k, k, v, v, v, v, sink, bias, *shards)


def _attn_bwd(q, k, v, sink, lse, d_attn, slabs):
    E = q.shape[0]
    S = E - CTX
    nb, kspecs = _attn_specs(E)
    last = E // WIN - 1
    na = len(slabs)

    def body(q_ref, kc, kp, kk, kn, vc, vp, vk, vn, sink_ref, bias_ref, lse_ref, do_ref, *rest):
        slab_refs, (dq_ref, dk_ref, dv_ref, ds_ref), got_refs = rest[:na], rest[na:na + 4], rest[na + 4:2 * na + 4]
        n = pl.program_id(0)
        cps = _xchg_copies(slab_refs, got_refs, *rest[2 * na + 4:], gather=False)

        @pl.when(n == 0)
        def _():
            _xchg_start(cps)
            dk_ref[...] = jnp.zeros_like(dk_ref)
            dv_ref[...] = jnp.zeros_like(dv_ref)
            ds_ref[...] = jnp.zeros_like(ds_ref)

        lane = lax.broadcasted_iota(jnp.int32, (WIN, 128), 1)
        lse_t = lse_ref[...]
        starts = [None, pl.multiple_of((n + 1) * WIN, WIN), pl.multiple_of((n + 2) * WIN, WIN),
                  pl.multiple_of(jnp.minimum(n + 3, last) * WIN, WIN)]
        for h in range(N_KV):
            hs = slice(128 * h, 128 * h + 128)
            K = jnp.concatenate([kc[:, hs], kp[:, hs], kk[:, hs], kn[:, hs]], axis=0)
            V = jnp.concatenate([vc[:, hs], vp[:, hs], vk[:, hs], vn[:, hs]], axis=0)
            Q = jnp.concatenate([q_ref[:, 128 * (GRP * h + g):128 * (GRP * h + g) + 128] for g in range(GRP)], axis=0)
            sk = jnp.concatenate([jnp.broadcast_to(sink_ref[GRP * h + g:GRP * h + g + 1, 0:1], (WIN, 1))
                                  for g in range(GRP)], axis=0)
            ls = jnp.concatenate([jnp.sum(jnp.where(lane == GRP * h + g, lse_t, 0.0), axis=1, keepdims=True)
                                  for g in range(GRP)], axis=0)
            dos = []
            for g in range(GRP):
                j = GRP * h + g
                t = do_ref[:, 128 * (j // 2):128 * (j // 2) + 128].astype(F32)
                if j % 2:
                    t = pltpu.roll(t, 64, 1)
                dos.append(jnp.where(lane < HD, t, 0.0))
            do = jnp.concatenate(dos, axis=0).astype(_BF)
            p = jnp.exp(_nt(Q, K) + bias_ref[0] - ls)
            dp = _nt(do, V)
            delta = jnp.sum(p * dp, axis=1, keepdims=True)
            dsc = (p * (dp - delta)).astype(_BF)
            dq = _nn(dsc, K) * (HD ** -0.5)
            for g in range(GRP):
                j = GRP * h + g
                dq_ref[:, 128 * j:128 * j + 128] = dq[WIN * g:WIN * g + WIN].astype(_BF)
            dK = _tn(Q, dsc)
            dV = _tn(do, p.astype(_BF))
            dk_ref[hs, 0:CTX] += dK[:, 0:CTX]
            dv_ref[hs, 0:CTX] += dV[:, 0:CTX]
            for w in range(1, 4):
                lo = CTX + WIN * (w - 1)
                dk_ref[hs, pl.ds(starts[w], WIN)] += dK[:, lo:lo + WIN]
                dv_ref[hs, pl.ds(starts[w], WIN)] += dV[:, lo:lo + WIN]
            psk = -jnp.exp(sk - ls) * delta
            for g in range(GRP):
                j = GRP * h + g
                tot = jnp.sum(psk[WIN * g:WIN * g + WIN], axis=0, keepdims=True)
                ds_ref[j:j + 1, :] += jnp.broadcast_to(tot, (1, 128))

        @pl.when(n == nb - 1)
        def _():
            _xchg_finish(cps)

    qs = pl.BlockSpec((WIN, QP), lambda n: (n + 2, 0))
    anyspec = pl.BlockSpec(memory_space=pl.ANY)
    bias, bias_spec = _attn_bias(nb)
    return pl.pallas_call(
        body, name="attn_bwd", grid=(nb,),
        in_specs=[qs] + kspecs + kspecs + [_full((8, 128)), bias_spec, _rows(WIN, 128), _rows(WIN, 512)]
        + [anyspec] * na,
        out_specs=[_rows(WIN, QP), _full((KP, E)), _full((KP, E)), _full((8, 128))] + [anyspec] * na,
        out_shape=[jax.ShapeDtypeStruct((S, QP), _BF), jax.ShapeDtypeStruct((KP, E), F32),
                   jax.ShapeDtypeStruct((KP, E), F32), jax.ShapeDtypeStruct((8, 128), F32)]
        + _xchg_out_shapes(slabs, False),
        scratch_shapes=_xchg_scratch(na),
        compiler_params=_cp(("arbitrary",), 48 * 1024 * 1024),
    )(q, k, k, k, k, v, v, v, v, sink, bias, lse, d_attn, *slabs)


def _gla_order(E, reverse, backward):
    nc = CTX // GLA_T
    n = E // GLA_T
    if not reverse:
        fwd = lambda s: s
    else:
        fwd = lambda s: jnp.where(s < nc, nc - 1 - s, n - 1 + nc - s)
    if backward:
        return lambda s: fwd(n - 1 - s)
    return fwd


def _gla_masks():
    T = GLA_T
    l128 = lax.broadcasted_iota(jnp.int32, (1, 128), 1)
    qmask = [((l128 >> 5) == j).astype(F32) for j in range(4)]
    vmask = [((l128 >> 6) == j).astype(F32) for j in range(2)]
    bd = ((lax.broadcasted_iota(jnp.int32, (512, 256), 0) >> 6)
          == (lax.broadcasted_iota(jnp.int32, (512, 256), 1) >> 5)).astype(F32)
    ri = lax.broadcasted_iota(jnp.int32, (T, 2 * T), 0)
    ci = lax.broadcasted_iota(jnp.int32, (T, 2 * T), 1) & (T - 1)
    return qmask, vmask, bd, ri, ci


def _tri_sum(tri, x):
    hi = x.astype(_BF)
    r1 = x - hi.astype(F32)
    mid = r1.astype(_BF)
    lo = (r1 - mid.astype(F32)).astype(_BF)
    n = x.shape[1]
    y = _nn(tri.astype(_BF), jnp.concatenate([hi, mid, lo], axis=1))
    return y[:, 0:n] + y[:, n:2 * n] + y[:, 2 * n:3 * n]


def _gla_decays(la, reverse, ri, ci):
    T = GLA_T
    msk2 = (ri <= ci) if reverse else (ri >= ci)
    mskT2 = (ri >= ci) if reverse else (ri <= ci)
    b = _tri_sum(msk2[:, 0:T], la)
    bT = b[0:1] if reverse else b[T - 1:T]
    bm = b[T // 2:T // 2 + 1]
    return msk2, mskT2, b, bT, bm


def _pair_stack(tile, m0, m1):
    return jnp.concatenate([(tile * m0).astype(_BF), (tile * m1).astype(_BF)], axis=0)


def _gla_fwd(gq, gk, gv, la, reverse, shards=()):
    E = gq.shape[0]
    T = GLA_T
    n = E // T
    order = _gla_order(E, reverse, False)
    col = 1 if reverse else 0

    def body(gq_ref, gk_ref, gv_ref, la_ref, o_ref, st_ref, S_scr):
        @pl.when(pl.program_id(0) == 0)
        def _():
            S_scr[...] = jnp.zeros_like(S_scr)

        qmask, vmask, bd, ri, ci = _gla_masks()
        msk2, _, b, bT, bm = _gla_decays(la_ref[...], reverse, ri, ci)
        q, k, v = gq_ref[...], gk_ref[...], gv_ref[...]
        qd = (q * jnp.exp(b)).astype(_BF)
        qm = (q * jnp.exp(b - bm)).astype(_BF)
        km = k * jnp.exp(bm - b)
        kd = (k * jnp.exp(bT - b)).astype(_BF)
        ST = S_scr[...]
        comp = ST[0:DV]
        for h in range(1, N_GLA):
            comp = comp + ST[DV * h:DV * h + DV]
        st_ref[0] = comp
        inter = _nt(qd, ST.astype(_BF))
        tiles = []
        for p in range(N_GLA // 2):
            qs = slice(128 * (p // 2), 128 * (p // 2) + 128)
            vs = slice(128 * p, 128 * p + 128)
            j0 = (2 * p) % 4
            KS = _pair_stack(km[:, qs], qmask[j0], qmask[j0 + 1])
            VS = _pair_stack(v[:, vs], vmask[0], vmask[1])
            AA = jnp.where(msk2, _nt(qm[:, qs], KS), 0.0).astype(_BF)
            tiles.append(_nn(AA, VS))
        o_ref[...] = inter + jnp.concatenate(tiles, axis=1)
        S_scr[...] = ST * jnp.exp(bT) + bd * _tn(v.astype(_BF), kd)

    blk = lambda w, c=0: pl.BlockSpec((T, w), lambda s: (order(s), c))
    return _hosted_call(
        body, (gq, gk, gv, la), shards, True,
        name="gla_fwd_rev" if reverse else "gla_fwd", grid=(n,),
        in_specs=[blk(256), blk(256), blk(512), blk(256, col)],
        out_specs=[blk(512), pl.BlockSpec((1, DV, 256), lambda s: (order(s), 0, 0))],
        out_shape=[jax.ShapeDtypeStruct((E, 512), F32), jax.ShapeDtypeStruct((n, DV, 256), F32)],
        scratch_shapes=[pltpu.VMEM((512, 256), F32)],
        compiler_params=_cp(("arbitrary",)))


def _gla_bwd(gq, gk, gv, la, st, do, reverse, slabs=()):
    E = gq.shape[0]
    T = GLA_T
    n = E // T
    nc = CTX // T
    order = _gla_order(E, reverse, True)
    col = 1 if reverse else 0

    def body(gq_ref, gk_ref, gv_ref, la_ref, st_ref, do_ref, dq_ref, dk_ref, dv_ref, dla_ref, dS_scr):
        @pl.when(pl.program_id(0) == 0)
        def _():
            dS_scr[...] = jnp.zeros_like(dS_scr)

        is_lat = order(pl.program_id(0)) >= nc
        qmask, vmask, bd, ri, ci = _gla_masks()
        msk2, mskT2, b, bT, bm = _gla_decays(la_ref[...], reverse, ri, ci)
        q, k, v = gq_ref[...], gk_ref[...], gv_ref[...]
        do = jnp.where(is_lat, do_ref[...].astype(F32), 0.0)
        e_b, e_qm, e_km, e_kd, e_T = jnp.exp(b), jnp.exp(b - bm), jnp.exp(bm - b), jnp.exp(bT - b), jnp.exp(bT)
        qd, qm, km, kd = q * e_b, q * e_qm, k * e_km, k * e_kd
        qdb, qmb, kmb, kdb, vb, dob = (t.astype(_BF) for t in (qd, qm, km, kd, v, do))
        ST = jnp.tile(st_ref[0], (N_GLA, 1)) * bd
        dST = dS_scr[...]
        dSTb = dST.astype(_BF)
        dqd = _nn(dob, ST.astype(_BF))
        dkd = _nn(vb, dSTb)
        dv_t, dqm_t, dkm_t = [], [None, None], [None, None]
        for p in range(N_GLA // 2):
            t = p // 2
            qs = slice(128 * t, 128 * t + 128)
            vs = slice(128 * p, 128 * p + 128)
            j0 = (2 * p) % 4
            QS = _pair_stack(qm[:, qs], qmask[j0], qmask[j0 + 1])
            KS = _pair_stack(km[:, qs], qmask[j0], qmask[j0 + 1])
            VS = _pair_stack(v[:, vs], vmask[0], vmask[1])
            DS = _pair_stack(do[:, vs], vmask[0], vmask[1])
            ATT = jnp.where(mskT2, _nt(kmb[:, qs], QS), 0.0).astype(_BF)
            dAA = jnp.where(msk2, _nt(dob[:, vs], VS), 0.0).astype(_BF)
            dATT = jnp.where(mskT2, _nt(vb[:, vs], DS), 0.0).astype(_BF)
            dv_t.append(_nn(ATT, DS))
            dq_p = _nn(dAA, KS)
            dk_p = _nn(dATT, QS)
            dqm_t[t] = dq_p if dqm_t[t] is None else dqm_t[t] + dq_p
            dkm_t[t] = dk_p if dkm_t[t] is None else dkm_t[t] + dk_p
        dqm = jnp.concatenate(dqm_t, axis=1)
        dkm = jnp.concatenate(dkm_t, axis=1)
        dq_ref[...] = dqm * e_qm + dqd * e_b
        dk_ref[...] = dkm * e_km + dkd * e_kd
        dv_ref[...] = _nt(kdb, dSTb) + jnp.concatenate(dv_t, axis=1)
        db = dqm * qm - dkm * km + dqd * qd - dkd * kd
        dbT = jnp.sum(dkd * kd, axis=0, keepdims=True) + e_T * jnp.sum(dST * ST, axis=0, keepdims=True)
        dla_ref[...] = _tri_sum(mskT2[:, 0:T], db) + dbT
        dS_scr[...] = dST * e_T + bd * _tn(dob, qdb)

    blk = lambda w, c=0: pl.BlockSpec((T, w), lambda s: (order(s), c))
    do_spec = pl.BlockSpec((T, 512), lambda s: (jnp.maximum(order(s) - nc, 0), 0))
    return _hosted_call(
        body, (gq, gk, gv, la, st, do), slabs, False,
        name="gla_bwd_rev" if reverse else "gla_bwd", grid=(n,),
        in_specs=[blk(256), blk(256), blk(512), blk(256, col),
                  pl.BlockSpec((1, DV, 256), lambda s: (order(s), 0, 0)), do_spec],
        out_specs=[blk(256), blk(256), blk(512), blk(256)],
        out_shape=[jax.ShapeDtypeStruct((E, 256), F32), jax.ShapeDtypeStruct((E, 256), F32),
                   jax.ShapeDtypeStruct((E, 512), F32), jax.ShapeDtypeStruct((E, 256), F32)],
        scratch_shapes=[pltpu.VMEM((512, 256), F32)],
        compiler_params=_cp(("arbitrary",)))


def _gla_out(o_f, o_b, gg, ggla, mavg):
    o = o_f + o_b
    rr = lax.rsqrt(_head_mean(o * o, mavg) + EPS)
    oh = o * rr
    sg = _sigmoid(gg)
    return oh, rr, sg


def _mix_fwd(x, attn, o_f, o_b, gg, ggla, mavg, wout, gt1, g2):
    S = x.shape[0]
    TM = 256

    def body(x_ref, a_ref, of_ref, ob_ref, gg_ref, ggla_ref, mavg_ref, w_ref, gt1_ref, g2_ref, x1_ref, mix_ref):
        gg_t = gg_ref[...]
        oh, _, sg = _gla_out(of_ref[...], ob_ref[...], gg_t, ggla_ref[...], mavg_ref[...])
        mix_ref[:, 0:512] = a_ref[...]
        mix_ref[:, 512:1024] = (oh * ggla_ref[...] * (gg_t * sg)).astype(_BF)
        y = _nn(mix_ref[...], w_ref[...])
        ry = lax.rsqrt(jnp.mean(y * y, axis=-1, keepdims=True) + EPS)
        x1_ref[...] = x_ref[...] + gt1_ref[...] * ((y * ry) * g2_ref[...])

    return pl.pallas_call(
        body, name="mix_fwd", grid=(S // TM,),
        in_specs=[_rows(TM, D), _rows(TM, 512), _rows(TM, 512, 1), _rows(TM, 512, 1), _rows(TM, 512, 1),
                  _full((1, 512)), _full((512, 512)), _full((D, D)), _full((1, D)), _full((1, D))],
        out_specs=[_rows(TM, D), _rows(TM, D)],
        out_shape=[jax.ShapeDtypeStruct((S, D), F32), jax.ShapeDtypeStruct((S, D), _BF)],
        compiler_params=_cp(("arbitrary",), 40 * 1024 * 1024),
    )(x, attn, o_f, o_b, gg, ggla, mavg, wout, gt1, g2)


def _mix_bwd(dx1, mix, o_f, o_b, gg, ggla, mavg, wout, gt1, g2):
    S = dx1.shape[0]
    TM = 256

    def body(dx_ref, mix_ref, of_ref, ob_ref, gg_ref, ggla_ref, mavg_ref, w_ref, gt1_ref, g2_ref,
             da_ref, do_ref, dgg_ref, dy_ref, sums_ref):
        @pl.when(pl.program_id(0) == 0)
        def _():
            sums_ref[...] = jnp.zeros_like(sums_ref)

        dx = dx_ref[...]
        y = _nn(mix_ref[...], w_ref[...])
        ry = lax.rsqrt(jnp.mean(y * y, axis=-1, keepdims=True) + EPS)
        yh = y * ry
        sums_ref[0:1, :] += jnp.sum(dx * yh, axis=0, keepdims=True)
        dyh = dx * (gt1_ref[...] * g2_ref[...])
        dy = (ry * (dyh - yh * jnp.mean(dyh * yh, axis=-1, keepdims=True))).astype(_BF)
        dy_ref[...] = dy
        dmix = _nt(dy, w_ref[...])
        da_ref[...] = dmix[:, 0:512].astype(_BF)
        dgla = dmix[:, 512:1024]
        gg_t = gg_ref[...]
        ggla_t = ggla_ref[...]
        oh, rr, sg = _gla_out(of_ref[...], ob_ref[...], gg_t, ggla_t, mavg_ref[...])
        dgg_ref[...] = (dgla * oh * ggla_t * (sg * (1.0 + gg_t * (1.0 - sg)))).astype(_BF)
        don = dgla * (gg_t * sg)
        sums_ref[1:2, 0:512] += jnp.sum(don * oh, axis=0, keepdims=True)
        doh = don * ggla_t
        do_ref[...] = (rr * (doh - oh * _head_mean(doh * oh, mavg_ref[...]))).astype(_BF)

    return pl.pallas_call(
        body, name="mix_bwd", grid=(S // TM,),
        in_specs=[_rows(TM, D), _rows(TM, D), _rows(TM, 512, 1), _rows(TM, 512, 1), _rows(TM, 512, 1),
                  _full((1, 512)), _full((512, 512)), _full((D, D)), _full((1, D)), _full((1, D))],
        out_specs=[_rows(TM, 512), _rows(TM, 512), _rows(TM, 512), _rows(TM, D), _full((8, D))],
        out_shape=[jax.ShapeDtypeStruct((S, 512), _BF), jax.ShapeDtypeStruct((S, 512), _BF),
                   jax.ShapeDtypeStruct((S, 512), _BF), jax.ShapeDtypeStruct((S, D), _BF),
                   jax.ShapeDtypeStruct((8, D), F32)],
        compiler_params=_cp(("arbitrary",), 40 * 1024 * 1024),
    )(dx1, mix, o_f, o_b, gg, ggla, mavg, wout, gt1, g2)


def _ffn(x1, target, gm2, sh2, gt2, g4, wffi, wffo):
    S = x1.shape[0]
    TF = 256

    def body(x_ref, t_ref, gm_ref, sh_ref, gt_ref, g4_ref, wi_hbm, wo_hbm,
             dx_ref, h_ref, du_ref, act_ref, df_ref, sums_ref, loss_ref, wi, wo, sem):
        @pl.when(pl.program_id(0) == 0)
        def _():
            c1 = pltpu.make_async_copy(wi_hbm, wi, sem.at[0])
            c2 = pltpu.make_async_copy(wo_hbm, wo, sem.at[1])
            c1.start()
            c2.start()
            sums_ref[...] = jnp.zeros_like(sums_ref)
            loss_ref[...] = jnp.zeros_like(loss_ref)
            c1.wait()
            c2.wait()

        x = x_ref[...]
        gm = gm_ref[...]
        r = lax.rsqrt(jnp.mean(x * x, axis=-1, keepdims=True) + EPS)
        xh = x * r
        hb = (xh * gm + sh_ref[...]).astype(_BF)
        h_ref[...] = hb
        u = _nn(hb, wi[...])
        g = u[:, 0:FFN]
        up = u[:, FFN:2 * FFN]
        sg = _sigmoid(g)
        sl = g * sg
        ab = (sl * up).astype(_BF)
        act_ref[...] = ab
        f = _nn(ab, wo[...])
        rf = lax.rsqrt(jnp.mean(f * f, axis=-1, keepdims=True) + EPS)
        fh = f * rf
        gt, g4v = gt_ref[...], g4_ref[...]
        err = x + gt * (fh * g4v) - t_ref[...]
        loss_ref[...] += jnp.sum(err * err) * (0.5 / D)
        dout = err * (1.0 / D)
        sums_ref[2:3, :] += jnp.sum(dout * fh, axis=0, keepdims=True)
        dfh = dout * (gt * g4v)
        dfb = (rf * (dfh - fh * jnp.mean(dfh * fh, axis=-1, keepdims=True))).astype(_BF)
        df_ref[...] = dfb
        dact = _nt(dfb, wo[...])
        du_ref[:, 0:FFN] = (dact * up * (sg * (1.0 + g * (1.0 - sg)))).astype(_BF)
        du_ref[:, FFN:2 * FFN] = (dact * sl).astype(_BF)
        dh = _nt(du_ref[...], wi[...])
        sums_ref[0:1, :] += jnp.sum(dh, axis=0, keepdims=True)
        sums_ref[1:2, :] += jnp.sum(dh * xh, axis=0, keepdims=True)
        dxh = dh * gm
        dx_ref[...] = dout + r * (dxh - xh * jnp.mean(dxh * xh, axis=-1, keepdims=True))

    vec = _full((1, D))
    anyspec = pl.BlockSpec(memory_space=pl.ANY)
    return pl.pallas_call(
        body, name="ffn_fwd_bwd", grid=(S // TF,),
        in_specs=[_rows(TF, D), _rows(TF, D), vec, vec, vec, vec, anyspec, anyspec],
        out_specs=[_rows(TF, D), _rows(TF, D), _rows(TF, 2 * FFN), _rows(TF, FFN), _rows(TF, D),
                   _full((8, D)), _full((8, 128))],
        out_shape=[jax.ShapeDtypeStruct((S, D), F32), jax.ShapeDtypeStruct((S, D), _BF),
                   jax.ShapeDtypeStruct((S, 2 * FFN), _BF), jax.ShapeDtypeStruct((S, FFN), _BF),
                   jax.ShapeDtypeStruct((S, D), _BF), jax.ShapeDtypeStruct((8, D), F32),
                   jax.ShapeDtypeStruct((8, 128), F32)],
        scratch_shapes=[pltpu.VMEM((D, 2 * FFN), _BF), pltpu.VMEM((FFN, D), _BF), pltpu.SemaphoreType.DMA((2,))],
        compiler_params=_cp(("arbitrary",), VMEM_BIG),
    )(x1, target, gm2, sh2, gt2, g4, wffi, wffo)


def _inproj_bwd(x, ctx, gml, gmc, win, wg, cos, sa, sb, la, dq, dk, dv, dgq_f, dgq_b, dgk_f, dgk_b, dgv_f, dgv_b,
                dgg, dla_f, dla_b, dx1):
    S = x.shape[0]
    E = S + CTX
    TE = CTX

    def body(x_ref, c_ref, gml_ref, gmc_ref, w_ref, wg_ref, cos_ref, sa_ref, sb_ref, la_ref, dq_ref, dk_ref, dv_ref,
             gqf, gqb, gkf, gkb, gvf, gvb, dgg_ref, dlf, dlb, dx1_ref,
             dp_ref, dlg_ref, gx_ref, sums_ref, bsum_ref):
        i = pl.program_id(0)
        is_ctx = i == 0

        @pl.when(is_ctx)
        def _():
            sums_ref[...] = jnp.zeros_like(sums_ref)
            bsum_ref[...] = jnp.zeros_like(bsum_ref)

        lat = jnp.where(is_ctx, 0.0, 1.0)
        cos_t, sa_t, sb_t = cos_ref[...], sa_ref[...], sb_ref[...]
        dp_ref[:, O_Q:O_K] = (_unrope(dq_ref[...].astype(F32), cos_t, sa_t, sb_t) * lat).astype(_BF)
        dp_ref[:, O_K:O_V] = _unrope(dk_ref[...].T, cos_t, sa_t, sb_t).astype(_BF)
        dp_ref[:, O_V:O_GQ] = dv_ref[...].T.astype(_BF)
        dp_ref[:, O_GQ:O_GK] = ((gqf[...] + gqb[...]) * (DK ** -0.5)).astype(_BF)
        dp_ref[:, O_GK:O_GV] = (gkf[...] + gkb[...]).astype(_BF)
        dp_ref[:, O_GV:O_GG] = (gvf[...] + gvb[...]).astype(_BF)
        dp_ref[:, O_GG:O_Z] = (dgg_ref[...].astype(F32) * lat).astype(_BF)
        la_t = la_ref[...]
        dlg = (jnp.concatenate([dlf[...], dlb[...]], axis=1) * (1.0 - jnp.exp(GATE_TAU * la_t)) * (1.0 / GATE_TAU))
        bsum_ref[0:1, :] += jnp.sum(dlg, axis=0, keepdims=True)
        dlgb = dlg.astype(_BF)
        dlg_ref[...] = dlgb
        dp_ref[:, O_Z:NP] = _nt(dlgb, wg_ref[...]).astype(_BF)
        dh = _nt(dp_ref[...], w_ref[...])
        x = jnp.where(is_ctx, c_ref[...], x_ref[...])
        r = lax.rsqrt(jnp.mean(x * x, axis=-1, keepdims=True) + EPS)
        xh = x * r
        sdh = jnp.sum(dh, axis=0, keepdims=True)
        sdx = jnp.sum(dh * xh, axis=0, keepdims=True)
        sums_ref[0:1, :] += sdh * lat
        sums_ref[1:2, :] += sdx * lat
        sums_ref[2:3, :] += sdh * (1.0 - lat)
        sums_ref[3:4, :] += sdx * (1.0 - lat)
        dxh = dh * jnp.where(is_ctx, gmc_ref[...], gml_ref[...])
        gx_ref[...] = dx1_ref[...] + r * (dxh - xh * jnp.mean(dxh * xh, axis=-1, keepdims=True))

    vec = _full((1, D))
    tab = _rows(TE, 128)
    return pl.pallas_call(
        body, name="inproj_bwd", grid=(E // TE,),
        in_specs=[_rows_lat(TE, D), _full((CTX, D)), vec, vec, _full((D, NP)), _full((128, 512)), tab, tab, tab,
                  _rows(TE, 512),
                  _rows_lat(TE, QP), pl.BlockSpec((KP, TE), lambda i: (0, i)), pl.BlockSpec((KP, TE), lambda i: (0, i)),
                  _rows(TE, 256), _rows(TE, 256), _rows(TE, 256), _rows(TE, 256), _rows(TE, 512), _rows(TE, 512),
                  _rows_lat(TE, 512), _rows(TE, 256), _rows(TE, 256), _rows_lat(TE, D)],
        out_specs=[_rows(TE, NP), _rows(TE, 512), _rows_lat(TE, D), _full((8, D)), _full((8, 512))],
        out_shape=[jax.ShapeDtypeStruct((E, NP), _BF), jax.ShapeDtypeStruct((E, 512), _BF),
                   jax.ShapeDtypeStruct((S, D), F32), jax.ShapeDtypeStruct((8, D), F32),
                   jax.ShapeDtypeStruct((8, 512), F32)],
        compiler_params=_cp(("arbitrary",), VMEM_BIG),
    )(x, ctx, gml, gmc, win, wg, cos, sa, sb, la, dq, dk, dv, dgq_f, dgq_b, dgk_f, dgk_b, dgv_f, dgv_b,
      dgg, dla_f, dla_b, dx1)


def _matmul_tn(a, b, tk, tn, tt, name):
    T, KA = a.shape
    N = b.shape[1]

    def body(a_ref, b_ref, o_ref):
        @pl.when(pl.program_id(2) == 0)
        def _():
            o_ref[...] = jnp.zeros_like(o_ref)

        o_ref[...] += _tn(a_ref[...], b_ref[...])

    return pl.pallas_call(
        body, name=name, grid=(KA // tk, N // tn, T // tt),
        in_specs=[pl.BlockSpec((tt, tk), lambda i, j, t: (t, i)), pl.BlockSpec((tt, tn), lambda i, j, t: (t, j))],
        out_specs=pl.BlockSpec((tk, tn), lambda i, j, t: (i, j)),
        out_shape=jax.ShapeDtypeStruct((KA, N), F32),
        compiler_params=_cp(("parallel", "parallel", "arbitrary"), VMEM_BIG),
    )(a, b)


def _ada_fwd(c_all, c_ctx, w_ada):
    n = w_ada.shape[1]

    def body(c_ref, cc_ref, w_ref, o_ref):
        c = jnp.concatenate([c_ref[...], jnp.broadcast_to(cc_ref[...], (8, D))], axis=0)
        o_ref[...] = _nn((c * _sigmoid(c)).astype(_BF), w_ref[...].astype(_BF))

    return pl.pallas_call(
        body, name="ada_fwd", in_specs=[_full((8, D)), _full((1, D)), _full((D, n))], out_specs=_full((16, n)),
        out_shape=jax.ShapeDtypeStruct((16, n), F32), grid=(1,), compiler_params=_cp(("arbitrary",)),
    )(c_all, c_ctx, w_ada)


def _ada_bwd(c_all, c_ctx, w_ada, d_all):
    n = w_ada.shape[1]

    def body(c_ref, cc_ref, w_ref, d_ref, gw_ref, t_ref):
        c = jnp.concatenate([c_ref[...], jnp.broadcast_to(cc_ref[...], (8, D))], axis=0)
        db = d_ref[...].astype(_BF)
        gw_ref[0] = _tn((c * _sigmoid(c)).astype(_BF), db)
        t_ref[...] = _nt(db[8:16], w_ref[...].astype(_BF))

    return pl.pallas_call(
        body, name="ada_bwd", in_specs=[_full((8, D)), _full((1, D)), _full((D, n)), _full((16, n))],
        out_specs=[_full((1, D, n)), _full((8, D))],
        out_shape=[jax.ShapeDtypeStruct((1, D, n), F32), jax.ShapeDtypeStruct((8, D), F32)], grid=(1,),
        compiler_params=_cp(("arbitrary",)),
    )(c_all, c_ctx, w_ada, d_all)


PART_ROWS = 56
R_ADA, R_ADA_C, R_GAIN, R_SINK, R_BG, R_GGLA, R_WG = 0, 6, 12, 16, 17, 18, 24


def _small_grads(s_in, s_ffn, s_mix, ada_l, ada_c, gains, dsink, s_bg, g_wg):
    def body(si, sf, sm, al, ac, g, ds, sbg, gwg, o_ref):
        o_ref[...] = jnp.zeros_like(o_ref)
        sub = lax.broadcasted_iota(jnp.int32, (8, 128), 0)
        lane = lax.broadcasted_iota(jnp.int32, (8, 128), 1)
        o_ref[R_SINK:R_SINK + 1, 0:128] = jnp.sum(jnp.where(sub == lane, ds[...], 0.0), axis=0, keepdims=True)
        o_ref[R_BG:R_BG + 1, 0:512] = sbg[0:1, :]
        y = sm[1:2, 0:128] + sm[1:2, 128:256] + sm[1:2, 256:384] + sm[1:2, 384:512]
        y = y + pltpu.roll(y, 64, 1)
        o_ref[R_GGLA:R_GGLA + 1, 0:128] = jnp.where(lane[0:1] < DV, y, 0.0)
        o_ref[R_WG:R_WG + 16, 0:256] = gwg[0:16, 0:256]
        o_ref[R_WG + 16:R_WG + 32, 0:256] = gwg[16:32, 256:512]
        sdh_l, sdx_l, sdh_c, sdx_c = si[0:1], si[1:2], si[2:3], si[3:4]
        sdh2, sdx2, a2 = sf[0:1], sf[1:2], sf[2:3]
        a1 = sm[0:1]
        g1, g2, g3, g4 = g[0:1], g[1:2], g[2:3], g[3:4]
        sc1, gt1, sc2, gt2 = al[1:2], al[2:3], al[4:5], al[5:6]
        sc1c = ac[1:2]
        z = jnp.zeros((1, D), F32)
        rows = [sdh_l, sdx_l * g1, a1 * g2, sdh2, sdx2 * g3, a2 * g4,
                sdh_c, sdx_c * g1, z, z, z, z,
                sdx_l * (1.0 + sc1) + sdx_c * (1.0 + sc1c), a1 * gt1, sdx2 * (1.0 + sc2), a2 * gt2]
        for r, v in enumerate(rows):
            o_ref[r:r + 1, :] = v

    v8 = _full((8, D))
    return pl.pallas_call(
        body, name="small_grads", in_specs=[v8] * 6 + [_full((8, 128)), _full((8, 512)), _full((128, 512))],
        out_specs=_full((PART_ROWS, D)), out_shape=jax.ShapeDtypeStruct((PART_ROWS, D), F32), grid=(1,),
        compiler_params=_cp(("arbitrary",)),
    )(s_in, s_ffn, s_mix, ada_l, ada_c, gains, dsink, s_bg, g_wg)


def _row_tile(R):
    for cand in (256, 128, 64, 32, 16):
        if R % cand == 0 and R > cand:
            return cand
    return R


def _adamw(w, g, m, v, name):
    _, R, C = w.shape
    tr = _row_tile(R)
    ns = g.shape[0]
    c1 = 1.0 / (1.0 - ADAM_B1 ** ADAM_STEP)
    c2 = 1.0 / (1.0 - ADAM_B2 ** ADAM_STEP)

    def body(w_ref, g_ref, m_ref, v_ref, go_ref, d_ref, nm_ref, nv_ref):
        gg = g_ref[0].astype(F32)
        for j in range(1, ns):
            gg = gg + g_ref[j].astype(F32)
        go_ref[0] = gg
        nm = ADAM_B1 * m_ref[0] + (1.0 - ADAM_B1) * gg
        nv = ADAM_B2 * v_ref[0] + (1.0 - ADAM_B2) * (gg * gg)
        nm_ref[0] = nm
        nv_ref[0] = nv
        d_ref[0] = -ADAM_LR * ((nm * c1) / (jnp.sqrt(nv * c2) + ADAM_EPS) + ADAM_WD * w_ref[0])

    spec = pl.BlockSpec((1, tr, C), lambda i: (0, i, 0))
    sds = jax.ShapeDtypeStruct((1, R, C), F32)
    return pl.pallas_call(
        body, name=name, grid=(R // tr,),
        in_specs=[spec, pl.BlockSpec((ns, tr, C), lambda i: (0, i, 0)), spec, spec],
        out_specs=[spec] * 4, out_shape=[sds] * 4, compiler_params=_cp(("parallel",)),
    )(w, g, m, v)


def _sum_slots(slots, name):
    _, R, C = slots.shape
    tr = _row_tile(R)

    def body(s_ref, o_ref):
        acc = s_ref[0].astype(F32)
        for j in range(1, N_DEV):
            acc = acc + s_ref[j].astype(F32)
        o_ref[...] = acc

    return pl.pallas_call(
        body, name=name, grid=(R // tr,), in_specs=[pl.BlockSpec((N_DEV, tr, C), lambda i: (0, i, 0))],
        out_specs=_rows(tr, C), out_shape=jax.ShapeDtypeStruct((R, C), F32), compiler_params=_cp(("parallel",)),
    )(slots)


def _allgather(x_shard, name):
    m_per, n = x_shard.shape

    def body(x_ref, out_ref, send_sems, recv_sems, local_sem):
        x, y, c = lax.axis_index("x"), lax.axis_index("y"), lax.axis_index("c")
        me, sibling = (x, y, c), (x, y, 1 - c)
        chips = [(1 - x, y), (x, 1 - y), (1 - x, 1 - y)]

        def rows(px, py, pc):
            return out_ref.at[pl.ds((4 * px + 2 * py + pc) * m_per, m_per), :]

        def copy(k, block, to, src=None):
            return pltpu.make_async_remote_copy(
                src_ref=rows(*block) if src is None else src, dst_ref=rows(*block),
                send_sem=send_sems.at[k], recv_sem=recv_sems.at[k], device_id=to, device_id_type=MESH)

        mine = pltpu.make_async_copy(x_ref, rows(*me), local_sem)
        mine.start()
        first = [copy(0, me, sibling, src=x_ref)]
        first += [copy(1 + j, me, (*chip, c), src=x_ref) for j, chip in enumerate(chips)]
        for cp in first:
            cp.start()
        passed = [copy(4 + j, (*chip, c), sibling) for j, chip in enumerate(chips)]
        for j, chip in enumerate(chips):
            copy(1 + j, (*chip, c), me).wait_recv()
            passed[j].start()
        copy(0, sibling, me).wait_recv()
        for j, chip in enumerate(chips):
            copy(4 + j, (*chip, 1 - c), me).wait_recv()
        for cp in first + passed:
            cp.wait_send()
        mine.wait()

    return pl.pallas_call(
        body, name=name, out_shape=jax.ShapeDtypeStruct((N_DEV * m_per, n), x_shard.dtype),
        in_specs=[pl.BlockSpec(memory_space=pltpu.VMEM)], out_specs=pl.BlockSpec(memory_space=pltpu.VMEM),
        scratch_shapes=[pltpu.SemaphoreType.DMA((7,)), pltpu.SemaphoreType.DMA((7,)), pltpu.SemaphoreType.DMA],
        compiler_params=pltpu.CompilerParams(vmem_limit_bytes=VMEM_BIG),
    )(x_shard)


def _alltoall(slabs, name):
    na = len(slabs)

    def body(*refs):
        ins, outs = refs[:na], refs[na:2 * na]
        send_sems, recv_sems, local_sems = refs[2 * na:]
        x, y, c = lax.axis_index("x"), lax.axis_index("y"), lax.axis_index("c")
        me = 4 * x + 2 * y + c
        copies = []
        for a in range(na):
            loc = pltpu.make_async_copy(ins[a].at[me], outs[a].at[me], local_sems.at[a])
            loc.start()
            copies.append(loc)
        rem = []
        for k in range(1, N_DEV):
            px, py, pc = x ^ (k >> 2), y ^ ((k >> 1) & 1), c ^ (k & 1)
            peer = 4 * px + 2 * py + pc
            for a in range(na):
                cp = pltpu.make_async_remote_copy(
                    src_ref=ins[a].at[peer], dst_ref=outs[a].at[me],
                    send_sem=send_sems.at[a, k - 1], recv_sem=recv_sems.at[a, k - 1],
                    device_id=(px, py, pc), device_id_type=MESH)
                cp.start()
                rem.append((a, k, peer, cp))
        for a, k, peer, cp in rem:
            pltpu.make_async_remote_copy(
                src_ref=ins[a].at[me], dst_ref=outs[a].at[peer],
                send_sem=send_sems.at[a, k - 1], recv_sem=recv_sems.at[a, k - 1],
                device_id=(x, y, c), device_id_type=MESH).wait_recv()
        for a, k, peer, cp in rem:
            cp.wait_send()
        for loc in copies:
            loc.wait()

    anyspec = pl.BlockSpec(memory_space=pl.ANY)
    return pl.pallas_call(
        body, name=name, out_shape=[jax.ShapeDtypeStruct(s.shape, s.dtype) for s in slabs],
        in_specs=[anyspec] * na, out_specs=[anyspec] * na,
        scratch_shapes=[pltpu.SemaphoreType.DMA((na, 7)), pltpu.SemaphoreType.DMA((na, 7)),
                        pltpu.SemaphoreType.DMA((na,))],
    )(*slabs)


def _rope_tables(S):
    t = np.arange(S)
    row = (t // GRID_W).astype(np.float32)
    colp = (t % GRID_W).astype(np.float32)
    half = HD // 2
    inv = (ROPE_BASE ** (-np.arange(0, half, 2, dtype=np.float32) / half)).astype(np.float32)
    ar = row[:, None] * inv[None, :]
    ac = colp[:, None] * inv[None, :]
    ang = np.concatenate([ar, ar, ac, ac], axis=-1).astype(np.float32)
    cos = np.cos(ang).astype(np.float32)
    sin = np.sin(ang).astype(np.float32)
    lane = np.arange(HD)
    first = (lane % 32) < 16
    sa = np.where(first[None, :], -sin, 0.0)
    sb = np.where(first[None, :], 0.0, sin)

    def ext(tab, ctx_val):
        full = np.zeros((CTX + S, 128), np.float32)
        full[:CTX, :HD] = ctx_val
        full[CTX:, :HD] = tab
        return jnp.asarray(full)

    return ext(cos, 1.0), ext(sa, 0.0), ext(sb, 0.0)


def _pad_cols_win(w):
    q, k, v, rest, z = w[:, 0:512], w[:, 512:640], w[:, 640:768], w[:, 768:2304], w[:, 2304:2336]

    def padh(t, nh):
        t = t.reshape(D, nh, HD)
        return jnp.pad(t, ((0, 0), (0, 0), (0, 128 - HD))).reshape(D, nh * 128)

    return jnp.concatenate([padh(q, N_ATT), padh(k, N_KV), padh(v, N_KV), rest, jnp.pad(z, ((0, 0), (0, 96)))], axis=1)


def _unpad_cols_win(g):
    def unp(t, nh):
        return t.reshape(D, nh, 128)[:, :, :HD].reshape(D, nh * HD)

    return jnp.concatenate([unp(g[:, O_Q:O_K], N_ATT), unp(g[:, O_K:O_V], N_KV), unp(g[:, O_V:O_GQ], N_KV),
                            g[:, O_GQ:O_Z], g[:, O_Z:O_Z + 32]], axis=1)


def _local_step(x, ctx, target, ada_l, ada_c, gains, sink, win_p, wg_bd, bg, ggla, wout_sh, wffi_sh, wffo_sh):
    S = x.shape[0]
    cos, sa, sb = _rope_tables(S)
    g1, g2, g3, g4 = (gains[i:i + 1] for i in range(4))
    sh1, sc1, gt1, sh2, sc2, gt2 = (ada_l[i:i + 1] for i in range(6))
    sh1c, sc1c = ada_c[0:1], ada_c[1:2]
    gml, gmc, gm2 = g1 * (1.0 + sc1), g1 * (1.0 + sc1c), g3 * (1.0 + sc2)
    mavg = jnp.asarray(np.kron(np.eye(N_GLA, dtype=np.float32), np.full((DV, DV), 1.0 / DV, np.float32))).astype(_BF)

    n_ffi, r_ffo, r_out = wffi_sh.shape[1], wffo_sh.shape[0], wout_sh.shape[0]
    tt_e = 768 if (S + CTX) % 768 == 0 else 256
    tt_s = 512 if S % 512 == 0 else 256
    h, q, k, v, gq, gk, gv, gg, z, la, wout_g = _inproj_fwd(x, ctx, gml, sh1, gmc, sh1c, win_p, wg_bd, bg,
                                                            cos, sa, sb, [wout_sh])
    attn, lse, wffi_g = _attn_fwd(q, k, v, sink, [wffi_sh])
    o_f, st_f, wffo_g = _gla_fwd(gq, gk, gv, la, False, [wffo_sh])
    o_b, st_b = _gla_fwd(gq, gk, gv, la, True)
    wout = wout_g.reshape(N_DEV * r_out, D)
    wffi = wffi_g.transpose(1, 0, 2).reshape(D, N_DEV * n_ffi)
    wffo = wffo_g.reshape(N_DEV * r_ffo, D)
    x1, mix = _mix_fwd(x, attn, o_f, o_b, gg, ggla, mavg, wout, gt1, g2)
    dx1, h2, du, act, df, s_ffn, loss = _ffn(x1, target, gm2, sh2, gt2, g4, wffi, wffo)
    g_wffi = _matmul_tn(h2, du, 512, 2 * FFN, tt_s, "grad_w_ffn_in")
    g_wffo = _matmul_tn(act, df, FFN, D, tt_s, "grad_w_ffn_out")
    slab_ffi = g_wffi.reshape(D, N_DEV, n_ffi).transpose(1, 0, 2).astype(_BF)
    slab_ffo = g_wffo.reshape(N_DEV, r_ffo, D).astype(_BF)
    d_attn, do_gla, dgg, dy, s_mix = _mix_bwd(dx1, mix, o_f, o_b, gg, ggla, mavg, wout, gt1, g2)
    g_wout = _matmul_tn(mix, dy, D, D, tt_s, "grad_w_out")
    slab_out = g_wout.reshape(N_DEV, r_out, D).astype(_BF)
    dq, dk, dv, dsink, got_ffi, got_ffo = _attn_bwd(q, k, v, sink, lse, d_attn, [slab_ffi, slab_ffo])
    dgq_f, dgk_f, dgv_f, dla_f, got_out = _gla_bwd(gq, gk, gv, la, st_f, do_gla, False, [slab_out])
    dgq_b, dgk_b, dgv_b, dla_b = _gla_bwd(gq, gk, gv, la, st_b, do_gla, True)
    dp, dlg, grad_x, s_in, s_bg = _inproj_bwd(x, ctx, gml, gmc, win_p, wg_bd, cos, sa, sb, la, dq, dk, dv,
                                              dgq_f, dgq_b, dgk_f, dgk_b, dgv_f, dgv_b, dgg, dla_f, dla_b, dx1)
    g_win = _matmul_tn(h, dp, D, NP, tt_e, "grad_w_in")
    g_wg = _matmul_tn(z, dlg, 128, 512, tt_e, "grad_w_gate")
    small = _small_grads(s_in, s_ffn, s_mix, ada_l, ada_c, gains, dsink, s_bg, g_wg)
    return dict(loss=loss[0, 0], grad_x=grad_x, g_win=g_win, got_out=got_out, got_ffi=got_ffi, got_ffo=got_ffo,
                small=small)


SMALL_NAMES = ["c_ctx", "b_ada", "g_pre_mix", "g_post_mix", "g_pre_ffn", "g_post_ffn", "attn_sink",
               "b_gate_fwd", "b_gate_bwd", "g_gla_norm", "w_gate_fwd", "w_gate_bwd"]


def _small_update(tot, t_tot, wg_g, w, m, v):
    c1 = 1.0 / (1.0 - ADAM_B1 ** ADAM_STEP)
    c2 = 1.0 / (1.0 - ADAM_B2 ** ADAM_STEP)
    n = len(SMALL_NAMES)

    def body(tot_ref, t_ref, wg_ref, *refs):
        w_r, m_r, v_r = refs[0:n], refs[n:2 * n], refs[2 * n:3 * n]
        g_o, d_o, nm_o, nv_o = refs[3 * n:4 * n], refs[4 * n:5 * n], refs[5 * n:6 * n], refs[6 * n:7 * n]

        def upd(i, idx, g):
            nm = ADAM_B1 * m_r[i][idx] + (1.0 - ADAM_B1) * g
            nv = ADAM_B2 * v_r[i][idx] + (1.0 - ADAM_B2) * (g * g)
            g_o[i][idx] = g
            nm_o[i][idx] = nm
            nv_o[i][idx] = nv
            d_o[i][idx] = -ADAM_LR * ((nm * c1) / (jnp.sqrt(nv * c2) + ADAM_EPS) + ADAM_WD * w_r[i][idx])

        everything = (slice(None), slice(None))
        cc = w_r[0][...]
        sc = _sigmoid(cc)
        upd(0, everything, t_ref[0:1, :] * (sc * (1.0 + cc * (1.0 - sc))))
        for j in range(6):
            upd(1, (slice(None), slice(D * j, D * j + D)),
                tot_ref[R_ADA + j:R_ADA + j + 1, :] + tot_ref[R_ADA_C + j:R_ADA_C + j + 1, :])
        for j in range(4):
            upd(2 + j, everything, tot_ref[R_GAIN + j:R_GAIN + j + 1, :])
        upd(6, everything, tot_ref[R_SINK:R_SINK + 1, 0:N_ATT])
        upd(7, everything, tot_ref[R_BG:R_BG + 1, 0:256])
        upd(8, everything, tot_ref[R_BG:R_BG + 1, 256:512])
        upd(9, everything, tot_ref[R_GGLA:R_GGLA + 1, 0:DV])
        upd(10, (0,), wg_ref[0:GATE_RANK, :])
        upd(11, (0,), wg_ref[GATE_RANK:2 * GATE_RANK, :])

    params = [w[k] for k in SMALL_NAMES] + [m[k] for k in SMALL_NAMES] + [v[k] for k in SMALL_NAMES]
    outs = pl.pallas_call(
        body, name="small_update", grid=(1,),
        in_specs=[_full(tot.shape), _full(t_tot.shape), _full(wg_g.shape)] + [_full(p.shape) for p in params],
        out_specs=[_full(w[k].shape) for k in SMALL_NAMES] * 4,
        out_shape=[jax.ShapeDtypeStruct(w[k].shape, F32) for k in SMALL_NAMES] * 4,
        compiler_params=_cp(("arbitrary",)),
    )(tot, t_tot, wg_g, *params)
    return tuple(dict(zip(SMALL_NAMES, outs[i * n:(i + 1) * n])) for i in range(4))


def kernel(x, c, ctx, c_ctx, w_ada, b_ada, g_pre_mix, g_post_mix, g_pre_ffn, g_post_ffn, w_in, attn_sink, w_gate_fwd, b_gate_fwd, w_gate_bwd, b_gate_bwd, g_gla_norm, w_out, w_ffn_in, w_ffn_out, loss_target, m_c_ctx, m_w_ada, m_b_ada, m_g_pre_mix, m_g_post_mix, m_g_pre_ffn, m_g_post_ffn, m_w_in, m_attn_sink, m_w_gate_fwd, m_b_gate_fwd, m_w_gate_bwd, m_b_gate_bwd, m_g_gla_norm, m_w_out, m_w_ffn_in, m_w_ffn_out, v_c_ctx, v_w_ada, v_b_ada, v_g_pre_mix, v_g_post_mix, v_g_pre_ffn, v_g_post_ffn, v_w_in, v_attn_sink, v_w_gate_fwd, v_b_gate_fwd, v_w_gate_bwd, v_b_gate_bwd, v_g_gla_norm, v_w_out, v_w_ffn_in, v_w_ffn_out):
    me = 4 * lax.axis_index("x") + 2 * lax.axis_index("y") + lax.axis_index("c")
    S = x.shape[1]
    n_in = w_in.shape[2]
    n_ffi = w_ffn_in.shape[2]
    r_out = w_out.shape[1]
    r_ffo = w_ffn_out.shape[1]
    n_ada = w_ada.shape[2]

    small_in = jnp.concatenate([c.reshape(8, 128), w_gate_fwd.reshape(4, 128), w_gate_bwd.reshape(4, 128)], axis=0)
    sg = _allgather(small_in, "gather_small").reshape(N_DEV, 16, 128)
    c_all = sg[:, 0:8].reshape(N_DEV, D)
    wgf = sg[:, 8:12].reshape(N_DEV, GATE_RANK, 32).transpose(1, 0, 2).reshape(GATE_RANK, 256)
    wgb = sg[:, 12:16].reshape(N_DEV, GATE_RANK, 32).transpose(1, 0, 2).reshape(GATE_RANK, 256)

    wg_all = _allgather(w_in[0].astype(_BF).reshape(-1, 128), "gather_weights")
    win_full = wg_all.reshape(N_DEV, D, n_in).transpose(1, 0, 2).reshape(D, N_DEV * n_in)
    win_p = _pad_cols_win(win_full)
    wg_bd = jnp.zeros((128, 512), F32).at[0:16, 0:256].set(wgf).at[16:32, 256:512].set(wgb).astype(_BF)

    ada_part = _ada_fwd(c_all, c_ctx.reshape(1, D), w_ada[0])
    ada_all = _allgather(ada_part, "gather_ada").reshape(N_DEV, 16, n_ada)
    ada_full = ada_all.transpose(1, 0, 2).reshape(16, N_DEV * n_ada) + b_ada
    ada_l = jnp.pad(lax.dynamic_slice_in_dim(ada_full, me, 1, 0).reshape(6, D), ((0, 2), (0, 0)))
    ada_c = jnp.pad(ada_full[8].reshape(6, D), ((0, 2), (0, 0)))
    gains = jnp.pad(jnp.concatenate([g_pre_mix, g_post_mix, g_pre_ffn, g_post_ffn], axis=0), ((0, 4), (0, 0)))
    sink = jnp.broadcast_to(attn_sink.reshape(8, 1), (8, 128))
    bg = jnp.concatenate([b_gate_fwd, b_gate_bwd], axis=1)
    ggla = jnp.tile(g_gla_norm, (1, N_GLA))

    r = _local_step(x[0], ctx[0], loss_target[0], ada_l, ada_c, gains, sink, win_p, wg_bd, bg, ggla,
                    w_out[0].astype(_BF), w_ffn_in[0].astype(_BF), w_ffn_out[0].astype(_BF))

    loss = lax.psum(r["loss"], ("x", "y", "c"))

    g_win = _unpad_cols_win(r["g_win"]).reshape(D, N_DEV, n_in).transpose(1, 0, 2).astype(_BF)
    slots = _alltoall([g_win], "scatter_grads")

    parts = _allgather(r["small"], "gather_small_grads").reshape(N_DEV, PART_ROWS, D)
    tot = _sum_slots(parts, "sum_small_grads")
    d_ada_rows = parts[:, R_ADA:R_ADA + 6].reshape(N_DEV, 6 * D)
    d_ada_c = tot[R_ADA_C:R_ADA_C + 6].reshape(1, 6 * D)
    my_cols = lax.dynamic_slice_in_dim(jnp.concatenate([d_ada_rows, jnp.broadcast_to(d_ada_c, (1, 6 * D)),
                                                        jnp.zeros((7, 6 * D), F32)], axis=0), me * n_ada, n_ada, 1)
    grad_w_ada, t_part = _ada_bwd(c_all, c_ctx.reshape(1, D), w_ada[0], my_cols)
    t_all = _allgather(t_part, "gather_c_ctx").reshape(N_DEV, 8, D)
    t_tot = _sum_slots(t_all, "sum_c_ctx")
    wg_g = lax.dynamic_slice(tot, (R_WG, me * 32), (2 * GATE_RANK, 32))

    w_small = dict(c_ctx=c_ctx.reshape(1, D), b_ada=b_ada, g_pre_mix=g_pre_mix, g_post_mix=g_post_mix, g_pre_ffn=g_pre_ffn,
                   g_post_ffn=g_post_ffn, attn_sink=attn_sink, b_gate_fwd=b_gate_fwd, b_gate_bwd=b_gate_bwd,
                   g_gla_norm=g_gla_norm, w_gate_fwd=w_gate_fwd, w_gate_bwd=w_gate_bwd)
    m_small = dict(c_ctx=m_c_ctx.reshape(1, D), b_ada=m_b_ada, g_pre_mix=m_g_pre_mix, g_post_mix=m_g_post_mix,
                   g_pre_ffn=m_g_pre_ffn, g_post_ffn=m_g_post_ffn, attn_sink=m_attn_sink, b_gate_fwd=m_b_gate_fwd,
                   b_gate_bwd=m_b_gate_bwd, g_gla_norm=m_g_gla_norm, w_gate_fwd=m_w_gate_fwd, w_gate_bwd=m_w_gate_bwd)
    v_small = dict(c_ctx=v_c_ctx.reshape(1, D), b_ada=v_b_ada, g_pre_mix=v_g_pre_mix, g_post_mix=v_g_post_mix,
                   g_pre_ffn=v_g_pre_ffn, g_post_ffn=v_g_post_ffn, attn_sink=v_attn_sink, b_gate_fwd=v_b_gate_fwd,
                   b_gate_bwd=v_b_gate_bwd, g_gla_norm=v_g_gla_norm, w_gate_fwd=v_w_gate_fwd, w_gate_bwd=v_w_gate_bwd)
    grads_small, d_s, nm_s, nv_s = _small_update(tot, t_tot, wg_g, w_small, m_small, v_small)
    for dd in (grads_small, d_s, nm_s, nv_s):
        dd["c_ctx"] = dd["c_ctx"].reshape(D)

    big = {}
    for nm, w, g, m, v in [("w_ada", w_ada, grad_w_ada, m_w_ada, v_w_ada), ("w_in", w_in, slots[0], m_w_in, v_w_in),
                           ("w_out", w_out, r["got_out"], m_w_out, v_w_out),
                           ("w_ffn_in", w_ffn_in, r["got_ffi"], m_w_ffn_in, v_w_ffn_in),
                           ("w_ffn_out", w_ffn_out, r["got_ffo"], m_w_ffn_out, v_w_ffn_out)]:
        big[nm] = _adamw(w, g, m, v, "adamw_" + nm)

    order = ["c_ctx", "w_ada", "b_ada", "g_pre_mix", "g_post_mix", "g_pre_ffn", "g_post_ffn", "w_in", "attn_sink",
             "w_gate_fwd", "b_gate_fwd", "w_gate_bwd", "b_gate_bwd", "g_gla_norm", "w_out", "w_ffn_in", "w_ffn_out"]
    grads, deltas, new_m, new_v = [], [], [], []
    for nm in order:
        if nm in big:
            g_, d_, m_, v_ = big[nm]
        else:
            g_, d_, m_, v_ = grads_small[nm], d_s[nm], nm_s[nm], nv_s[nm]
        grads.append(g_)
        deltas.append(d_)
        new_m.append(m_)
        new_v.append(v_)
    return (loss, r["grad_x"][None], *grads, *deltas, *new_m, *new_v)
```

```python
import functools
import math

import numpy as np
import jax
import jax.numpy as jnp
from jax import lax
from jax.experimental import pallas as pl
from jax.experimental.pallas import tpu as pltpu

F32 = jnp.float32
_BF = jnp.bfloat16

N_DEV = 8
D = 1024
CTX = 256
HD = 64
N_ATT = 8
N_KV = 2
GRP = N_ATT // N_KV
WIN = 128
GRID_W = 64
ROPE_BASE = 10000.0
N_GLA = 8
DK = 32
DV = 64
GATE_RANK = 16
GATE_TAU = 16.0
FFN = 2816
EPS = 1e-6
NEG = -1e30
GLA_T = 128

QP = N_ATT * 128
KP = N_KV * 128
O_Q, O_K, O_V = 0, QP, QP + KP
O_GQ = O_V + KP
O_GK = O_GQ + N_GLA * DK
O_GV = O_GK + N_GLA * DK
O_GG = O_GV + N_GLA * DV
O_Z = O_GG + N_GLA * DV
NP = O_Z + 128
IN_COLS = 2336

ADAM_LR, ADAM_B1, ADAM_B2, ADAM_EPS, ADAM_WD, ADAM_STEP = 0.001, 0.9, 0.999, 1e-08, 0.01, 10

VMEM_BIG = 56 * 1024 * 1024
MESH = pl.DeviceIdType.MESH


def _cp(sem, vmem=None):
    return pltpu.CompilerParams(dimension_semantics=sem, vmem_limit_bytes=vmem)


def _full(shape):
    nd = len(shape)
    return pl.BlockSpec(shape, lambda *a: (0,) * nd)


def _rows(tile, width, off=0):
    return pl.BlockSpec((tile, width), lambda i: (i + off, 0))


def _rows_lat(tile, width):
    return pl.BlockSpec((tile, width), lambda i: (jnp.maximum(i - 1, 0), 0))


def _nt(a, b):
    return lax.dot_general(a, b, (((1,), (1,)), ((), ())), preferred_element_type=F32)


def _tn(a, b):
    return lax.dot_general(a, b, (((0,), (0,)), ((), ())), preferred_element_type=F32)


def _nn(a, b):
    return jnp.dot(a, b, preferred_element_type=F32)


def _head_mean(x, mavg):
    n = x.shape[0]
    hi = x.astype(_BF)
    lo = (x - hi.astype(F32)).astype(_BF)
    y = _nn(jnp.concatenate([hi, lo], axis=0), mavg)
    return y[0:n] + y[n:2 * n]


def _rope(t, cos, sa, sb):
    n = t.shape[1]
    reps = n // 128
    c = jnp.tile(cos, (1, reps))
    a = jnp.tile(sa, (1, reps))
    b = jnp.tile(sb, (1, reps))
    return t * c + pltpu.roll(t, n - 16, 1) * a + pltpu.roll(t, 16, 1) * b


def _unrope(t, cos, sa, sb):
    n = t.shape[1]
    reps = n // 128
    c = jnp.tile(cos, (1, reps))
    a = jnp.tile(sa, (1, reps))
    b = jnp.tile(sb, (1, reps))
    return t * c + pltpu.roll(t * a, 16, 1) + pltpu.roll(t * b, n - 16, 1)


def _sigmoid(x):
    return 1.0 / (1.0 + jnp.exp(-x))


def _inproj_fwd(x, ctx, gml, shl, gmc, shc, win, wg, bg, cos, sa, sb, shards):
    E = x.shape[0] + CTX
    TE = CTX

    def body(x_ref, c_ref, gml_ref, shl_ref, gmc_ref, shc_ref, w_ref, wg_ref, bg_ref, cos_ref, sa_ref, sb_ref,
             h_ref, q_ref, k_ref, v_ref, gq_ref, gk_ref, gv_ref, gg_ref, z_ref, la_ref):
        is_ctx = pl.program_id(0) == 0
        gm = jnp.where(is_ctx, gmc_ref[...], gml_ref[...])
        sh = jnp.where(is_ctx, shc_ref[...], shl_ref[...])
        x = jnp.where(is_ctx, c_ref[...], x_ref[...])
        r = lax.rsqrt(jnp.mean(x * x, axis=-1, keepdims=True) + EPS)
        hb = ((x * r) * gm + sh).astype(_BF)
        h_ref[...] = hb
        p = _nt(hb, w_ref[...])
        cos_t, sa_t, sb_t = cos_ref[...], sa_ref[...], sb_ref[...]
        q_ref[...] = (_rope(p[:, O_Q:O_K], cos_t, sa_t, sb_t) * (HD ** -0.5)).astype(_BF)
        k_ref[...] = _rope(p[:, O_K:O_V], cos_t, sa_t, sb_t).astype(_BF)
        v_ref[...] = p[:, O_V:O_GQ].astype(_BF)
        gq_ref[...] = p[:, O_GQ:O_GK] * (DK ** -0.5)
        gk_ref[...] = p[:, O_GK:O_GV]
        gv_ref[...] = p[:, O_GV:O_GG]
        gg_ref[...] = p[:, O_GG:O_Z]
        zb = p[:, O_Z:NP].astype(_BF)
        z_ref[...] = zb
        lg = _nn(zb, wg_ref[...]) + bg_ref[...]
        la_ref[...] = (jnp.minimum(lg, 0.0) - jnp.log(1.0 + jnp.exp(-jnp.abs(lg)))) * (1.0 / GATE_TAU)

    vec = _full((1, D))
    tab = _rows(TE, 128)
    outs = [(D, _BF), (QP, _BF), (KP, _BF), (KP, _BF), (256, F32), (256, F32), (512, F32), (512, F32),
            (128, _BF), (512, F32)]
    return _hosted_call(
        body, (x, ctx, gml, shl, gmc, shc, win, wg, bg, cos, sa, sb), shards, True,
        name="inproj_fwd", grid=(E // TE,),
        in_specs=[_rows_lat(TE, D), _full((CTX, D)), vec, vec, vec, vec, _full((NP, D)), _full((128, 512)),
                  _full((1, 512)), tab, tab, tab],
        out_specs=[_rows(TE, w) for w, _ in outs],
        out_shape=[jax.ShapeDtypeStruct((E, w), dt) for w, dt in outs],
        compiler_params=_cp(("arbitrary",), 40 * 1024 * 1024))


def _xchg_scratch(na):
    return [pltpu.SemaphoreType.DMA((na, N_DEV - 1)), pltpu.SemaphoreType.DMA((na, N_DEV - 1)),
            pltpu.SemaphoreType.DMA((na,))]


def _xchg_copies(ins, outs, send_sems, recv_sems, local_sems, gather):
    x, y, c = lax.axis_index("x"), lax.axis_index("y"), lax.axis_index("c")
    me = 4 * x + 2 * y + c
    local, sends, recvs = [], [], []
    for a in range(len(ins)):
        local.append(pltpu.make_async_copy(ins[a] if gather else ins[a].at[me], outs[a].at[me], local_sems.at[a]))
    for k in range(1, N_DEV):
        px, py, pc = x ^ (k >> 2), y ^ ((k >> 1) & 1), c ^ (k & 1)
        peer = 4 * px + 2 * py + pc
        for a in range(len(ins)):
            sems = dict(send_sem=send_sems.at[a, k - 1], recv_sem=recv_sems.at[a, k - 1], device_id_type=MESH)
            sends.append(pltpu.make_async_remote_copy(
                src_ref=ins[a] if gather else ins[a].at[peer], dst_ref=outs[a].at[me], device_id=(px, py, pc), **sems))
            recvs.append(pltpu.make_async_remote_copy(
                src_ref=ins[a] if gather else ins[a].at[me], dst_ref=outs[a].at[peer], device_id=(x, y, c), **sems))
    return local, sends, recvs


def _xchg_start(cps):
    local, sends, _ = cps
    for cp in local + sends:
        cp.start()


def _xchg_finish(cps):
    local, sends, recvs = cps
    for cp in recvs:
        cp.wait_recv()
    for cp in sends:
        cp.wait_send()
    for cp in local:
        cp.wait()


def _xchg_out_shapes(ins, gather):
    return [jax.ShapeDtypeStruct(((N_DEV,) + s.shape) if gather else s.shape, s.dtype) for s in ins]


def _hosted_call(body, args, hosted, gather, *, grid, in_specs, out_specs, out_shape, scratch_shapes=(), **kw):
    na = len(hosted)
    if na == 0:
        return pl.pallas_call(body, grid=grid, in_specs=in_specs, out_specs=out_specs, out_shape=out_shape,
                              scratch_shapes=list(scratch_shapes), **kw)(*args)
    n_in, n_out, n_scr = len(in_specs), len(out_specs), len(scratch_shapes)

    def wrapped(*refs):
        ins, h_in = refs[:n_in], refs[n_in:n_in + na]
        outs, h_out = refs[n_in + na:n_in + na + n_out], refs[n_in + na + n_out:n_in + 2 * na + n_out]
        scr = refs[n_in + 2 * na + n_out:]
        cps = _xchg_copies(h_in, h_out, *scr[n_scr:], gather=gather)
        step = pl.program_id(0)

        @pl.when(step == 0)
        def _():
            _xchg_start(cps)

        body(*ins, *outs, *scr[:n_scr])

        @pl.when(step == grid[0] - 1)
        def _():
            _xchg_finish(cps)

    anyspec = pl.BlockSpec(memory_space=pl.ANY)
    return pl.pallas_call(
        wrapped, grid=grid, in_specs=list(in_specs) + [anyspec] * na, out_specs=list(out_specs) + [anyspec] * na,
        out_shape=list(out_shape) + _xchg_out_shapes(hosted, gather),
        scratch_shapes=list(scratch_shapes) + _xchg_scratch(na), **kw)(*args, *hosted)


def _attn_specs(E):
    nb = (E - CTX) // WIN
    last = E // WIN - 1
    kc = pl.BlockSpec((CTX, KP), lambda n: (0, 0))
    kp = pl.BlockSpec((WIN, KP), lambda n: (n + 1, 0))
    kk = pl.BlockSpec((WIN, KP), lambda n: (n + 2, 0))
    kn = pl.BlockSpec((WIN, KP), lambda n: (jnp.minimum(n + 3, last), 0))
    return nb, [kc, kp, kk, kn]


def _attn_bias(nb):
    rows = np.arange(GRP * WIN)[:, None] % WIN
    cols = np.arange(CTX + 3 * WIN)[None, :]
    j = cols - CTX
    band = np.abs(j - WIN - rows) <= WIN
    out = []
    for first, last in ((True, False), (False, False), (False, True)):
        ok = (cols < CTX) | (band & ((j >= WIN) | (not first)) & ((j < 2 * WIN) | (not last)))
        out.append(np.where(ok, 0.0, NEG).astype(np.float32))
    bias = jnp.asarray(np.stack(out))
    spec = pl.BlockSpec((1, GRP * WIN, CTX + 3 * WIN),
                        lambda n: (jnp.where(n == 0, 0, jnp.where(n == nb - 1, 2, 1)), 0, 0))
    return bias, spec


def _attn_fwd(q, k, v, sink, shards):
    E = q.shape[0]
    S = E - CTX
    nb, kspecs = _attn_specs(E)
    na = len(shards)

    def body(q_ref, kc, kp, kk, kn, vc, vp, vk, vn, sink_ref, bias_ref, *rest):
        shard_refs, (o_ref, lse_ref), got_refs = rest[:na], rest[na:na + 2], rest[na + 2:2 * na + 2]
        n = pl.program_id(0)
        cps = _xchg_copies(shard_refs, got_refs, *rest[2 * na + 2:], gather=True)

        @pl.when(n == 0)
        def _():
            _xchg_start(cps)

        lane = lax.broadcasted_iota(jnp.int32, (WIN, 128), 1)
        lse_t = jnp.zeros((WIN, 128), F32)
        hs = [slice(128 * h, 128 * h + 128) for h in range(N_KV)]
        K = [jnp.concatenate([kc[:, s_], kp[:, s_], kk[:, s_], kn[:, s_]], axis=0) for s_ in hs]
        Q = [jnp.concatenate([q_ref[:, 128 * (GRP * h + g):128 * (GRP * h + g) + 128] for g in range(GRP)], axis=0)
             for h in range(N_KV)]
        sk = [jnp.concatenate([jnp.broadcast_to(sink_ref[GRP * h + g:GRP * h + g + 1, 0:1], (WIN, 1))
                               for g in range(GRP)], axis=0) for h in range(N_KV)]
        s = [_nt(Q[h], K[h]) + bias_ref[0] for h in range(N_KV)]
        m = [jnp.maximum(jnp.max(s[h], axis=1, keepdims=True), sk[h]) for h in range(N_KV)]
        e = [jnp.exp(s[h] - m[h]) for h in range(N_KV)]
        den = [jnp.sum(e[h], axis=1, keepdims=True) + jnp.exp(sk[h] - m[h]) for h in range(N_KV)]
        V = [jnp.concatenate([vc[:, s_], vp[:, s_], vk[:, s_], vn[:, s_]], axis=0) for s_ in hs]
        o = [_nn((e[h] * (1.0 / den[h])).astype(_BF), V[h]) for h in range(N_KV)]
        for h in range(N_KV):
            lse = m[h] + jnp.log(den[h])
            for g in range(GRP):
                lse_t = jnp.where(lane == GRP * h + g, lse[WIN * g:WIN * g + WIN], lse_t)
            for pp in range(GRP // 2):
                a = o[h][WIN * 2 * pp:WIN * 2 * pp + WIN]
                b = o[h][WIN * (2 * pp + 1):WIN * (2 * pp + 1) + WIN]
                t = 2 * h + pp
                o_ref[:, 128 * t:128 * t + 128] = (a + pltpu.roll(b, 64, 1)).astype(_BF)
        lse_ref[...] = lse_t

        @pl.when(n == nb - 1)
        def _():
            _xchg_finish(cps)

    qs = pl.BlockSpec((WIN, QP), lambda n: (n + 2, 0))
    anyspec = pl.BlockSpec(memory_space=pl.ANY)
    bias, bias_spec = _attn_bias(nb)
    return pl.pallas_call(
        body, name="attn_fwd", grid=(nb,),
        in_specs=[qs] + kspecs + kspecs + [_full((8, 128)), bias_spec] + [anyspec] * na,
        out_specs=[_rows(WIN, 512), _rows(WIN, 128)] + [anyspec] * na,
        out_shape=[jax.ShapeDtypeStruct((S, 512), _BF), jax.ShapeDtypeStruct((S, 128), F32)]
        + _xchg_out_shapes(shards, True),
        scratch_shapes=_xchg_scratch(na),
        compiler_params=_cp(("arbitrary",)),
    )(q, k, k, k, k, v, v, v, v, sink, bias, *shards)


def _attn_bwd(q, k, v, sink, lse, d_attn, slabs):
    E = q.shape[0]
    S = E - CTX
    nb, kspecs = _attn_specs(E)
    last = E // WIN - 1
    na = len(slabs)

    def body(q_ref, kc, kp, kk, kn, vc, vp, vk, vn, sink_ref, bias_ref, lse_ref, do_ref, *rest):
        slab_refs, (dq_ref, dk_ref, dv_ref, ds_ref), got_refs = rest[:na], rest[na:na + 4], rest[na + 4:2 * na + 4]
        n = pl.program_id(0)
        cps = _xchg_copies(slab_refs, got_refs, *rest[2 * na + 4:], gather=False)

        @pl.when(n == 0)
        def _():
            _xchg_start(cps)
            dk_ref[...] = jnp.zeros_like(dk_ref)
            dv_ref[...] = jnp.zeros_like(dv_ref)
            ds_ref[...] = jnp.zeros_like(ds_ref)

        lane = lax.broadcasted_iota(jnp.int32, (WIN, 128), 1)
        lse_t = lse_ref[...]
        starts = [None, pl.multiple_of((n + 1) * WIN, WIN), pl.multiple_of((n + 2) * WIN, WIN),
                  pl.multiple_of(jnp.minimum(n + 3, last) * WIN, WIN)]
        for h in range(N_KV):
            hs = slice(128 * h, 128 * h + 128)
            K = jnp.concatenate([kc[:, hs], kp[:, hs], kk[:, hs], kn[:, hs]], axis=0)
            V = jnp.concatenate([vc[:, hs], vp[:, hs], vk[:, hs], vn[:, hs]], axis=0)
            Q = jnp.concatenate([q_ref[:, 128 * (GRP * h + g):128 * (GRP * h + g) + 128] for g in range(GRP)], axis=0)
            sk = jnp.concatenate([jnp.broadcast_to(sink_ref[GRP * h + g:GRP * h + g + 1, 0:1], (WIN, 1))
                                  for g in range(GRP)], axis=0)
            ls = jnp.concatenate([jnp.sum(jnp.where(lane == GRP * h + g, lse_t, 0.0), axis=1, keepdims=True)
                                  for g in range(GRP)], axis=0)
            dos = []
            for g in range(GRP):
                j = GRP * h + g
                t = do_ref[:, 128 * (j // 2):128 * (j // 2) + 128].astype(F32)
                if j % 2:
                    t = pltpu.roll(t, 64, 1)
                dos.append(jnp.where(lane < HD, t, 0.0))
            do = jnp.concatenate(dos, axis=0).astype(_BF)
            p = jnp.exp(_nt(Q, K) + bias_ref[0] - ls)
            dp = _nt(do, V)
            delta = jnp.sum(p * dp, axis=1, keepdims=True)
            dsc = (p * (dp - delta)).astype(_BF)
            dq = _nn(dsc, K) * (HD ** -0.5)
            for g in range(GRP):
                j = GRP * h + g
                dq_ref[:, 128 * j:128 * j + 128] = dq[WIN * g:WIN * g + WIN].astype(_BF)
            dK = _tn(Q, dsc)
            dV = _tn(do, p.astype(_BF))
            dk_ref[hs, 0:CTX] += dK[:, 0:CTX]
            dv_ref[hs, 0:CTX] += dV[:, 0:CTX]
            for w in range(1, 4):
                lo = CTX + WIN * (w - 1)
                dk_ref[hs, pl.ds(starts[w], WIN)] += dK[:, lo:lo + WIN]
                dv_ref[hs, pl.ds(starts[w], WIN)] += dV[:, lo:lo + WIN]
            psk = -jnp.exp(sk - ls) * delta
            for g in range(GRP):
                j = GRP * h + g
                tot = jnp.sum(psk[WIN * g:WIN * g + WIN], axis=0, keepdims=True)
                ds_ref[j:j + 1, :] += jnp.broadcast_to(tot, (1, 128))

        @pl.when(n == nb - 1)
        def _():
            _xchg_finish(cps)

    qs = pl.BlockSpec((WIN, QP), lambda n: (n + 2, 0))
    anyspec = pl.BlockSpec(memory_space=pl.ANY)
    bias, bias_spec = _attn_bias(nb)
    return pl.pallas_call(
        body, name="attn_bwd", grid=(nb,),
        in_specs=[qs] + kspecs + kspecs + [_full((8, 128)), bias_spec, _rows(WIN, 128), _rows(WIN, 512)]
        + [anyspec] * na,
        out_specs=[_rows(WIN, QP), _full((KP, E)), _full((KP, E)), _full((8, 128))] + [anyspec] * na,
        out_shape=[jax.ShapeDtypeStruct((S, QP), _BF), jax.ShapeDtypeStruct((KP, E), F32),
                   jax.ShapeDtypeStruct((KP, E), F32), jax.ShapeDtypeStruct((8, 128), F32)]
        + _xchg_out_shapes(slabs, False),
        scratch_shapes=_xchg_scratch(na),
        compiler_params=_cp(("arbitrary",), 48 * 1024 * 1024),
    )(q, k, k, k, k, v, v, v, v, sink, bias, lse, d_attn, *slabs)


def _gla_order(E, reverse, backward):
    nc = CTX // GLA_T
    n = E // GLA_T
    if not reverse:
        fwd = lambda s: s
    else:
        fwd = lambda s: jnp.where(s < nc, nc - 1 - s, n - 1 + nc - s)
    if backward:
        return lambda s: fwd(n - 1 - s)
    return fwd


def _gla_masks():
    T = GLA_T
    l128 = lax.broadcasted_iota(jnp.int32, (1, 128), 1)
    qmask = [((l128 >> 5) == j).astype(F32) for j in range(4)]
    vmask = [((l128 >> 6) == j).astype(F32) for j in range(2)]
    bd = ((lax.broadcasted_iota(jnp.int32, (512, 256), 0) >> 6)
          == (lax.broadcasted_iota(jnp.int32, (512, 256), 1) >> 5)).astype(F32)
    ri = lax.broadcasted_iota(jnp.int32, (T, 2 * T), 0)
    ci = lax.broadcasted_iota(jnp.int32, (T, 2 * T), 1) & (T - 1)
    return qmask, vmask, bd, ri, ci


def _tri_sum(tri, x):
    hi = x.astype(_BF)
    r1 = x - hi.astype(F32)
    mid = r1.astype(_BF)
    lo = (r1 - mid.astype(F32)).astype(_BF)
    n = x.shape[1]
    y = _nn(tri.astype(_BF), jnp.concatenate([hi, mid, lo], axis=1))
    return y[:, 0:n] + y[:, n:2 * n] + y[:, 2 * n:3 * n]


def _gla_decays(la, reverse, ri, ci):
    T = GLA_T
    msk2 = (ri <= ci) if reverse else (ri >= ci)
    mskT2 = (ri >= ci) if reverse else (ri <= ci)
    b = _tri_sum(msk2[:, 0:T], la)
    bT = b[0:1] if reverse else b[T - 1:T]
    bm = b[T // 2:T // 2 + 1]
    return msk2, mskT2, b, bT, bm


def _pair_stack(tile, m0, m1):
    return jnp.concatenate([(tile * m0).astype(_BF), (tile * m1).astype(_BF)], axis=0)


def _gla_fwd(gq, gk, gv, la, reverse, shards=()):
    E = gq.shape[0]
    T = GLA_T
    n = E // T
    order = _gla_order(E, reverse, False)
    col = 1 if reverse else 0

    def body(gq_ref, gk_ref, gv_ref, la_ref, o_ref, st_ref, S_scr):
        @pl.when(pl.program_id(0) == 0)
        def _():
            S_scr[...] = jnp.zeros_like(S_scr)

        qmask, vmask, bd, ri, ci = _gla_masks()
        msk2, _, b, bT, bm = _gla_decays(la_ref[...], reverse, ri, ci)
        q, k, v = gq_ref[...], gk_ref[...], gv_ref[...]
        qd = (q * jnp.exp(b)).astype(_BF)
        qm = (q * jnp.exp(b - bm)).astype(_BF)
        km = k * jnp.exp(bm - b)
        kd = (k * jnp.exp(bT - b)).astype(_BF)
        ST = S_scr[...]
        comp = ST[0:DV]
        for h in range(1, N_GLA):
            comp = comp + ST[DV * h:DV * h + DV]
        st_ref[0] = comp
        inter = _nt(qd, ST.astype(_BF))
        tiles = []
        for p in range(N_GLA // 2):
            qs = slice(128 * (p // 2), 128 * (p // 2) + 128)
            vs = slice(128 * p, 128 * p + 128)
            j0 = (2 * p) % 4
            KS = _pair_stack(km[:, qs], qmask[j0], qmask[j0 + 1])
            VS = _pair_stack(v[:, vs], vmask[0], vmask[1])
            AA = jnp.where(msk2, _nt(qm[:, qs], KS), 0.0).astype(_BF)
            tiles.append(_nn(AA, VS))
        o_ref[...] = inter + jnp.concatenate(tiles, axis=1)
        S_scr[...] = ST * jnp.exp(bT) + bd * _tn(v.astype(_BF), kd)

    blk = lambda w, c=0: pl.BlockSpec((T, w), lambda s: (order(s), c))
    return _hosted_call(
        body, (gq, gk, gv, la), shards, True,
        name="gla_fwd_rev" if reverse else "gla_fwd", grid=(n,),
        in_specs=[blk(256), blk(256), blk(512), blk(256, col)],
        out_specs=[blk(512), pl.BlockSpec((1, DV, 256), lambda s: (order(s), 0, 0))],
        out_shape=[jax.ShapeDtypeStruct((E, 512), F32), jax.ShapeDtypeStruct((n, DV, 256), F32)],
        scratch_shapes=[pltpu.VMEM((512, 256), F32)],
        compiler_params=_cp(("arbitrary",)))


def _gla_bwd(gq, gk, gv, la, st, do, reverse, slabs=()):
    E = gq.shape[0]
    T = GLA_T
    n = E // T
    nc = CTX // T
    order = _gla_order(E, reverse, True)
    col = 1 if reverse else 0

    def body(gq_ref, gk_ref, gv_ref, la_ref, st_ref, do_ref, dq_ref, dk_ref, dv_ref, dla_ref, dS_scr):
        @pl.when(pl.program_id(0) == 0)
        def _():
            dS_scr[...] = jnp.zeros_like(dS_scr)

        is_lat = order(pl.program_id(0)) >= nc
        qmask, vmask, bd, ri, ci = _gla_masks()
        msk2, mskT2, b, bT, bm = _gla_decays(la_ref[...], reverse, ri, ci)
        q, k, v = gq_ref[...], gk_ref[...], gv_ref[...]
        do = jnp.where(is_lat, do_ref[...].astype(F32), 0.0)
        e_b, e_qm, e_km, e_kd, e_T = jnp.exp(b), jnp.exp(b - bm), jnp.exp(bm - b), jnp.exp(bT - b), jnp.exp(bT)
        qd, qm, km, kd = q * e_b, q * e_qm, k * e_km, k * e_kd
        qdb, qmb, kmb, kdb, vb, dob = (t.astype(_BF) for t in (qd, qm, km, kd, v, do))
        ST = jnp.tile(st_ref[0], (N_GLA, 1)) * bd
        dST = dS_scr[...]
        dSTb = dST.astype(_BF)
        dqd = _nn(dob, ST.astype(_BF))
        dkd = _nn(vb, dSTb)
        dv_t, dqm_t, dkm_t = [], [None, None], [None, None]
        for p in range(N_GLA // 2):
            t = p // 2
            qs = slice(128 * t, 128 * t + 128)
            vs = slice(128 * p, 128 * p + 128)
            j0 = (2 * p) % 4
            QS = _pair_stack(qm[:, qs], qmask[j0], qmask[j0 + 1])
            KS = _pair_stack(km[:, qs], qmask[j0], qmask[j0 + 1])
            VS = _pair_stack(v[:, vs], vmask[0], vmask[1])
            DS = _pair_stack(do[:, vs], vmask[0], vmask[1])
            ATT = jnp.where(mskT2, _nt(kmb[:, qs], QS), 0.0).astype(_BF)
            dAA = jnp.where(msk2, _nt(dob[:, vs], VS), 0.0).astype(_BF)
            dATT = jnp.where(mskT2, _nt(vb[:, vs], DS), 0.0).astype(_BF)
            dv_t.append(_nn(ATT, DS))
            dq_p = _nn(dAA, KS)
            dk_p = _nn(dATT, QS)
            dqm_t[t] = dq_p if dqm_t[t] is None else dqm_t[t] + dq_p
            dkm_t[t] = dk_p if dkm_t[t] is None else dkm_t[t] + dk_p
        dqm = jnp.concatenate(dqm_t, axis=1)
        dkm = jnp.concatenate(dkm_t, axis=1)
        dq_ref[...] = dqm * e_qm + dqd * e_b
        dk_ref[...] = dkm * e_km + dkd * e_kd
        dv_ref[...] = _nt(kdb, dSTb) + jnp.concatenate(dv_t, axis=1)
        db = dqm * qm - dkm * km + dqd * qd - dkd * kd
        dbT = jnp.sum(dkd * kd, axis=0, keepdims=True) + e_T * jnp.sum(dST * ST, axis=0, keepdims=True)
        dla_ref[...] = _tri_sum(mskT2[:, 0:T], db) + dbT
        dS_scr[...] = dST * e_T + bd * _tn(dob, qdb)

    blk = lambda w, c=0: pl.BlockSpec((T, w), lambda s: (order(s), c))
    do_spec = pl.BlockSpec((T, 512), lambda s: (jnp.maximum(order(s) - nc, 0), 0))
    return _hosted_call(
        body, (gq, gk, gv, la, st, do), slabs, False,
        name="gla_bwd_rev" if reverse else "gla_bwd", grid=(n,),
        in_specs=[blk(256), blk(256), blk(512), blk(256, col),
                  pl.BlockSpec((1, DV, 256), lambda s: (order(s), 0, 0)), do_spec],
        out_specs=[blk(256), blk(256), blk(512), blk(256)],
        out_shape=[jax.ShapeDtypeStruct((E, 256), F32), jax.ShapeDtypeStruct((E, 256), F32),
                   jax.ShapeDtypeStruct((E, 512), F32), jax.ShapeDtypeStruct((E, 256), F32)],
        scratch_shapes=[pltpu.VMEM((512, 256), F32)],
        compiler_params=_cp(("arbitrary",)))


def _gla_out(o_f, o_b, gg, ggla, mavg):
    o = o_f + o_b
    rr = lax.rsqrt(_head_mean(o * o, mavg) + EPS)
    oh = o * rr
    sg = _sigmoid(gg)
    return oh, rr, sg


def _mix_fwd(x, attn, o_f, o_b, gg, ggla, mavg, wout, gt1, g2):
    S = x.shape[0]
    TM = 256

    def body(x_ref, a_ref, of_ref, ob_ref, gg_ref, ggla_ref, mavg_ref, w_ref, gt1_ref, g2_ref, x1_ref, mix_ref):
        gg_t = gg_ref[...]
        oh, _, sg = _gla_out(of_ref[...], ob_ref[...], gg_t, ggla_ref[...], mavg_ref[...])
        mix_ref[:, 0:512] = a_ref[...]
        mix_ref[:, 512:1024] = (oh * ggla_ref[...] * (gg_t * sg)).astype(_BF)
        y = _nn(mix_ref[...], w_ref[...])
        ry = lax.rsqrt(jnp.mean(y * y, axis=-1, keepdims=True) + EPS)
        x1_ref[...] = x_ref[...] + gt1_ref[...] * ((y * ry) * g2_ref[...])

    return pl.pallas_call(
        body, name="mix_fwd", grid=(S // TM,),
        in_specs=[_rows(TM, D), _rows(TM, 512), _rows(TM, 512, 1), _rows(TM, 512, 1), _rows(TM, 512, 1),
                  _full((1, 512)), _full((512, 512)), _full((D, D)), _full((1, D)), _full((1, D))],
        out_specs=[_rows(TM, D), _rows(TM, D)],
        out_shape=[jax.ShapeDtypeStruct((S, D), F32), jax.ShapeDtypeStruct((S, D), _BF)],
        compiler_params=_cp(("arbitrary",), 40 * 1024 * 1024),
    )(x, attn, o_f, o_b, gg, ggla, mavg, wout, gt1, g2)


def _mix_bwd(dx1, mix, o_f, o_b, gg, ggla, mavg, wout, gt1, g2):
    S = dx1.shape[0]
    TM = 256

    def body(dx_ref, mix_ref, of_ref, ob_ref, gg_ref, ggla_ref, mavg_ref, w_ref, gt1_ref, g2_ref,
             da_ref, do_ref, dgg_ref, dy_ref, sums_ref):
        @pl.when(pl.program_id(0) == 0)
        def _():
            sums_ref[...] = jnp.zeros_like(sums_ref)

        dx = dx_ref[...]
        y = _nn(mix_ref[...], w_ref[...])
        ry = lax.rsqrt(jnp.mean(y * y, axis=-1, keepdims=True) + EPS)
        yh = y * ry
        sums_ref[0:1, :] += jnp.sum(dx * yh, axis=0, keepdims=True)
        dyh = dx * (gt1_ref[...] * g2_ref[...])
        dy = (ry * (dyh - yh * jnp.mean(dyh * yh, axis=-1, keepdims=True))).astype(_BF)
        dy_ref[...] = dy
        dmix = _nt(dy, w_ref[...])
        da_ref[...] = dmix[:, 0:512].astype(_BF)
        dgla = dmix[:, 512:1024]
        gg_t = gg_ref[...]
        ggla_t = ggla_ref[...]
        oh, rr, sg = _gla_out(of_ref[...], ob_ref[...], gg_t, ggla_t, mavg_ref[...])
        dgg_ref[...] = (dgla * oh * ggla_t * (sg * (1.0 + gg_t * (1.0 - sg)))).astype(_BF)
        don = dgla * (gg_t * sg)
        sums_ref[1:2, 0:512] += jnp.sum(don * oh, axis=0, keepdims=True)
        doh = don * ggla_t
        do_ref[...] = (rr * (doh - oh * _head_mean(doh * oh, mavg_ref[...]))).astype(_BF)

    return pl.pallas_call(
        body, name="mix_bwd", grid=(S // TM,),
        in_specs=[_rows(TM, D), _rows(TM, D), _rows(TM, 512, 1), _rows(TM, 512, 1), _rows(TM, 512, 1),
                  _full((1, 512)), _full((512, 512)), _full((D, D)), _full((1, D)), _full((1, D))],
        out_specs=[_rows(TM, 512), _rows(TM, 512), _rows(TM, 512), _rows(TM, D), _full((8, D))],
        out_shape=[jax.ShapeDtypeStruct((S, 512), _BF), jax.ShapeDtypeStruct((S, 512), _BF),
                   jax.ShapeDtypeStruct((S, 512), _BF), jax.ShapeDtypeStruct((S, D), _BF),
                   jax.ShapeDtypeStruct((8, D), F32)],
        compiler_params=_cp(("arbitrary",), 40 * 1024 * 1024),
    )(dx1, mix, o_f, o_b, gg, ggla, mavg, wout, gt1, g2)


def _ffn(x1, target, gm2, sh2, gt2, g4, wffi, wffo):
    S = x1.shape[0]
    TF = 256

    def body(x_ref, t_ref, gm_ref, sh_ref, gt_ref, g4_ref, wi_hbm, wo_hbm,
             dx_ref, h_ref, du_ref, act_ref, df_ref, sums_ref, loss_ref, wi, wo, sem):
        @pl.when(pl.program_id(0) == 0)
        def _():
            c1 = pltpu.make_async_copy(wi_hbm, wi, sem.at[0])
            c2 = pltpu.make_async_copy(wo_hbm, wo, sem.at[1])
            c1.start()
            c2.start()
            sums_ref[...] = jnp.zeros_like(sums_ref)
            loss_ref[...] = jnp.zeros_like(loss_ref)
            c1.wait()
            c2.wait()

        x = x_ref[...]
        gm = gm_ref[...]
        r = lax.rsqrt(jnp.mean(x * x, axis=-1, keepdims=True) + EPS)
        xh = x * r
        hb = (xh * gm + sh_ref[...]).astype(_BF)
        h_ref[...] = hb
        u = _nt(hb, wi[...])
        g = u[:, 0:FFN]
        up = u[:, FFN:2 * FFN]
        sg = _sigmoid(g)
        sl = g * sg
        ab = (sl * up).astype(_BF)
        act_ref[...] = ab
        f = _nn(ab, wo[...])
        rf = lax.rsqrt(jnp.mean(f * f, axis=-1, keepdims=True) + EPS)
        fh = f * rf
        gt, g4v = gt_ref[...], g4_ref[...]
        err = x + gt * (fh * g4v) - t_ref[...]
        loss_ref[...] += jnp.sum(err * err) * (0.5 / D)
        dout = err * (1.0 / D)
        sums_ref[2:3, :] += jnp.sum(dout * fh, axis=0, keepdims=True)
        dfh = dout * (gt * g4v)
        dfb = (rf * (dfh - fh * jnp.mean(dfh * fh, axis=-1, keepdims=True))).astype(_BF)
        df_ref[...] = dfb
        dact = _nt(dfb, wo[...])
        du_ref[:, 0:FFN] = (dact * up * (sg * (1.0 + g * (1.0 - sg)))).astype(_BF)
        du_ref[:, FFN:2 * FFN] = (dact * sl).astype(_BF)
        dh = _nn(du_ref[...], wi[...])
        sums_ref[0:1, :] += jnp.sum(dh, axis=0, keepdims=True)
        sums_ref[1:2, :] += jnp.sum(dh * xh, axis=0, keepdims=True)
        dxh = dh * gm
        dx_ref[...] = dout + r * (dxh - xh * jnp.mean(dxh * xh, axis=-1, keepdims=True))

    vec = _full((1, D))
    anyspec = pl.BlockSpec(memory_space=pl.ANY)
    return pl.pallas_call(
        body, name="ffn_fwd_bwd", grid=(S // TF,),
        in_specs=[_rows(TF, D), _rows(TF, D), vec, vec, vec, vec, anyspec, anyspec],
        out_specs=[_rows(TF, D), _rows(TF, D), _rows(TF, 2 * FFN), _rows(TF, FFN), _rows(TF, D),
                   _full((8, D)), _full((8, 128))],
        out_shape=[jax.ShapeDtypeStruct((S, D), F32), jax.ShapeDtypeStruct((S, D), _BF),
                   jax.ShapeDtypeStruct((S, 2 * FFN), _BF), jax.ShapeDtypeStruct((S, FFN), _BF),
                   jax.ShapeDtypeStruct((S, D), _BF), jax.ShapeDtypeStruct((8, D), F32),
                   jax.ShapeDtypeStruct((8, 128), F32)],
        scratch_shapes=[pltpu.VMEM((2 * FFN, D), _BF), pltpu.VMEM((FFN, D), _BF), pltpu.SemaphoreType.DMA((2,))],
        compiler_params=_cp(("arbitrary",), VMEM_BIG),
    )(x1, target, gm2, sh2, gt2, g4, wffi, wffo)


def _inproj_bwd(x, ctx, gml, gmc, win, wg, cos, sa, sb, la, dq, dk, dv, dgq_f, dgq_b, dgk_f, dgk_b, dgv_f, dgv_b,
                dgg, dla_f, dla_b, dx1):
    S = x.shape[0]
    E = S + CTX
    TE = CTX

    def body(x_ref, c_ref, gml_ref, gmc_ref, w_ref, wg_ref, cos_ref, sa_ref, sb_ref, la_ref, dq_ref, dk_ref, dv_ref,
             gqf, gqb, gkf, gkb, gvf, gvb, dgg_ref, dlf, dlb, dx1_ref,
             dp_ref, dlg_ref, gx_ref, sums_ref, bsum_ref):
        i = pl.program_id(0)
        is_ctx = i == 0

        @pl.when(is_ctx)
        def _():
            sums_ref[...] = jnp.zeros_like(sums_ref)
            bsum_ref[...] = jnp.zeros_like(bsum_ref)

        lat = jnp.where(is_ctx, 0.0, 1.0)
        cos_t, sa_t, sb_t = cos_ref[...], sa_ref[...], sb_ref[...]
        dp_ref[:, O_Q:O_K] = (_unrope(dq_ref[...].astype(F32), cos_t, sa_t, sb_t) * lat).astype(_BF)
        dp_ref[:, O_K:O_V] = _unrope(dk_ref[...].T, cos_t, sa_t, sb_t).astype(_BF)
        dp_ref[:, O_V:O_GQ] = dv_ref[...].T.astype(_BF)
        dp_ref[:, O_GQ:O_GK] = ((gqf[...] + gqb[...]) * (DK ** -0.5)).astype(_BF)
        dp_ref[:, O_GK:O_GV] = (gkf[...] + gkb[...]).astype(_BF)
        dp_ref[:, O_GV:O_GG] = (gvf[...] + gvb[...]).astype(_BF)
        dp_ref[:, O_GG:O_Z] = (dgg_ref[...].astype(F32) * lat).astype(_BF)
        la_t = la_ref[...]
        dlg = (jnp.concatenate([dlf[...], dlb[...]], axis=1) * (1.0 - jnp.exp(GATE_TAU * la_t)) * (1.0 / GATE_TAU))
        bsum_ref[0:1, :] += jnp.sum(dlg, axis=0, keepdims=True)
        dlgb = dlg.astype(_BF)
        dlg_ref[...] = dlgb
        dp_ref[:, O_Z:NP] = _nt(dlgb, wg_ref[...]).astype(_BF)
        dh = _nn(dp_ref[...], w_ref[...])
        x = jnp.where(is_ctx, c_ref[...], x_ref[...])
        r = lax.rsqrt(jnp.mean(x * x, axis=-1, keepdims=True) + EPS)
        xh = x * r
        sdh = jnp.sum(dh, axis=0, keepdims=True)
        sdx = jnp.sum(dh * xh, axis=0, keepdims=True)
        sums_ref[0:1, :] += sdh * lat
        sums_ref[1:2, :] += sdx * lat
        sums_ref[2:3, :] += sdh * (1.0 - lat)
        sums_ref[3:4, :] += sdx * (1.0 - lat)
        dxh = dh * jnp.where(is_ctx, gmc_ref[...], gml_ref[...])
        gx_ref[...] = dx1_ref[...] + r * (dxh - xh * jnp.mean(dxh * xh, axis=-1, keepdims=True))

    vec = _full((1, D))
    tab = _rows(TE, 128)
    return pl.pallas_call(
        body, name="inproj_bwd", grid=(E // TE,),
        in_specs=[_rows_lat(TE, D), _full((CTX, D)), vec, vec, _full((NP, D)), _full((128, 512)), tab, tab, tab,
                  _rows(TE, 512),
                  _rows_lat(TE, QP), pl.BlockSpec((KP, TE), lambda i: (0, i)), pl.BlockSpec((KP, TE), lambda i: (0, i)),
                  _rows(TE, 256), _rows(TE, 256), _rows(TE, 256), _rows(TE, 256), _rows(TE, 512), _rows(TE, 512),
                  _rows_lat(TE, 512), _rows(TE, 256), _rows(TE, 256), _rows_lat(TE, D)],
        out_specs=[_rows(TE, NP), _rows(TE, 512), _rows_lat(TE, D), _full((8, D)), _full((8, 512))],
        out_shape=[jax.ShapeDtypeStruct((E, NP), _BF), jax.ShapeDtypeStruct((E, 512), _BF),
                   jax.ShapeDtypeStruct((S, D), F32), jax.ShapeDtypeStruct((8, D), F32),
                   jax.ShapeDtypeStruct((8, 512), F32)],
        compiler_params=_cp(("arbitrary",), VMEM_BIG),
    )(x, ctx, gml, gmc, win, wg, cos, sa, sb, la, dq, dk, dv, dgq_f, dgq_b, dgk_f, dgk_b, dgv_f, dgv_b,
      dgg, dla_f, dla_b, dx1)


def _matmul_tn(a, b, tk, tt, name, out_dtype, transpose_out=False):
    T, KA = a.shape
    N = b.shape[1]
    nt = T // tt

    def body(a_ref, b_ref, o_ref, acc):
        t = pl.program_id(1)

        @pl.when(t == 0)
        def _():
            acc[...] = jnp.zeros_like(acc)

        acc[...] += _tn(a_ref[...], b_ref[...])

        @pl.when(t == nt - 1)
        def _():
            o_ref[...] = (acc[...].T if transpose_out else acc[...]).astype(out_dtype)

    if transpose_out:
        out_spec, out_shape = pl.BlockSpec((N, tk), lambda i, t: (0, i)), (N, KA)
    else:
        out_spec, out_shape = pl.BlockSpec((tk, N), lambda i, t: (i, 0)), (KA, N)
    return pl.pallas_call(
        body, name=name, grid=(KA // tk, nt),
        in_specs=[pl.BlockSpec((tt, tk), lambda i, t: (t, i)), pl.BlockSpec((tt, N), lambda i, t: (t, 0))],
        out_specs=out_spec, out_shape=jax.ShapeDtypeStruct(out_shape, out_dtype),
        scratch_shapes=[pltpu.VMEM((tk, N), F32)],
        compiler_params=_cp(("parallel", "arbitrary"), VMEM_BIG),
    )(a, b)


def _ada_fwd(c_all, c_ctx, w_ada):
    n = w_ada.shape[1]

    def body(c_ref, cc_ref, w_ref, o_ref):
        c = jnp.concatenate([c_ref[...], jnp.broadcast_to(cc_ref[...], (8, D))], axis=0)
        o_ref[...] = _nn((c * _sigmoid(c)).astype(_BF), w_ref[...].astype(_BF))

    return pl.pallas_call(
        body, name="ada_fwd", in_specs=[_full((8, D)), _full((1, D)), _full((D, n))], out_specs=_full((16, n)),
        out_shape=jax.ShapeDtypeStruct((16, n), F32), grid=(1,), compiler_params=_cp(("arbitrary",)),
    )(c_all, c_ctx, w_ada)


def _ada_bwd(c_all, c_ctx, w_ada, d_all):
    n = w_ada.shape[1]

    def body(c_ref, cc_ref, w_ref, d_ref, gw_ref, t_ref):
        c = jnp.concatenate([c_ref[...], jnp.broadcast_to(cc_ref[...], (8, D))], axis=0)
        db = d_ref[...].astype(_BF)
        gw_ref[0] = _tn((c * _sigmoid(c)).astype(_BF), db)
        t_ref[...] = _nt(db[8:16], w_ref[...].astype(_BF))

    return pl.pallas_call(
        body, name="ada_bwd", in_specs=[_full((8, D)), _full((1, D)), _full((D, n)), _full((16, n))],
        out_specs=[_full((1, D, n)), _full((8, D))],
        out_shape=[jax.ShapeDtypeStruct((1, D, n), F32), jax.ShapeDtypeStruct((8, D), F32)], grid=(1,),
        compiler_params=_cp(("arbitrary",)),
    )(c_all, c_ctx, w_ada, d_all)


PART_ROWS = 56
R_ADA, R_ADA_C, R_GAIN, R_SINK, R_BG, R_GGLA, R_LOSS, R_WG = 0, 6, 12, 16, 17, 18, 19, 24


def _small_grads(s_in, s_ffn, s_mix, ada_l, ada_c, gains, dsink, s_bg, g_wg, loss):
    def body(si, sf, sm, al, ac, g, ds, sbg, gwg, loss_ref, o_ref):
        o_ref[...] = jnp.zeros_like(o_ref)
        o_ref[R_LOSS:R_LOSS + 1, 0:128] = loss_ref[0:1, :]
        sub = lax.broadcasted_iota(jnp.int32, (8, 128), 0)
        lane = lax.broadcasted_iota(jnp.int32, (8, 128), 1)
        o_ref[R_SINK:R_SINK + 1, 0:128] = jnp.sum(jnp.where(sub == lane, ds[...], 0.0), axis=0, keepdims=True)
        o_ref[R_BG:R_BG + 1, 0:512] = sbg[0:1, :]
        y = sm[1:2, 0:128] + sm[1:2, 128:256] + sm[1:2, 256:384] + sm[1:2, 384:512]
        y = y + pltpu.roll(y, 64, 1)
        o_ref[R_GGLA:R_GGLA + 1, 0:128] = jnp.where(lane[0:1] < DV, y, 0.0)
        o_ref[R_WG:R_WG + 16, 0:256] = gwg[0:16, 0:256]
        o_ref[R_WG + 16:R_WG + 32, 0:256] = gwg[16:32, 256:512]
        sdh_l, sdx_l, sdh_c, sdx_c = si[0:1], si[1:2], si[2:3], si[3:4]
        sdh2, sdx2, a2 = sf[0:1], sf[1:2], sf[2:3]
        a1 = sm[0:1]
        g1, g2, g3, g4 = g[0:1], g[1:2], g[2:3], g[3:4]
        sc1, gt1, sc2, gt2 = al[1:2], al[2:3], al[4:5], al[5:6]
        sc1c = ac[1:2]
        z = jnp.zeros((1, D), F32)
        rows = [sdh_l, sdx_l * g1, a1 * g2, sdh2, sdx2 * g3, a2 * g4,
                sdh_c, sdx_c * g1, z, z, z, z,
                sdx_l * (1.0 + sc1) + sdx_c * (1.0 + sc1c), a1 * gt1, sdx2 * (1.0 + sc2), a2 * gt2]
        for r, v in enumerate(rows):
            o_ref[r:r + 1, :] = v

    v8 = _full((8, D))
    return pl.pallas_call(
        body, name="small_grads",
        in_specs=[v8] * 6 + [_full((8, 128)), _full((8, 512)), _full((128, 512)), _full((8, 128))],
        out_specs=_full((PART_ROWS, D)), out_shape=jax.ShapeDtypeStruct((PART_ROWS, D), F32), grid=(1,),
        compiler_params=_cp(("arbitrary",)),
    )(s_in, s_ffn, s_mix, ada_l, ada_c, gains, dsink, s_bg, g_wg, loss)


def _row_tile(R):
    for cand in (256, 128, 64, 32, 16):
        if R % cand == 0 and R > cand:
            return cand
    return R


def _adamw(w, g, m, v, name):
    _, R, C = w.shape
    tr = _row_tile(R)
    ns = g.shape[0]
    c1 = 1.0 / (1.0 - ADAM_B1 ** ADAM_STEP)
    c2 = 1.0 / (1.0 - ADAM_B2 ** ADAM_STEP)

    def body(w_ref, g_ref, m_ref, v_ref, go_ref, d_ref, nm_ref, nv_ref):
        gg = g_ref[0].astype(F32)
        for j in range(1, ns):
            gg = gg + g_ref[j].astype(F32)
        go_ref[0] = gg
        nm = ADAM_B1 * m_ref[0] + (1.0 - ADAM_B1) * gg
        nv = ADAM_B2 * v_ref[0] + (1.0 - ADAM_B2) * (gg * gg)
        nm_ref[0] = nm
        nv_ref[0] = nv
        d_ref[0] = -ADAM_LR * ((nm * c1) / (jnp.sqrt(nv * c2) + ADAM_EPS) + ADAM_WD * w_ref[0])

    spec = pl.BlockSpec((1, tr, C), lambda i: (0, i, 0))
    sds = jax.ShapeDtypeStruct((1, R, C), F32)
    return pl.pallas_call(
        body, name=name, grid=(R // tr,),
        in_specs=[spec, pl.BlockSpec((ns, tr, C), lambda i: (0, i, 0)), spec, spec],
        out_specs=[spec] * 4, out_shape=[sds] * 4, compiler_params=_cp(("parallel",), 48 * 1024 * 1024),
    )(w, g, m, v)


def _sum_slots(slots, name):
    _, R, C = slots.shape
    tr = _row_tile(R)

    def body(s_ref, o_ref):
        acc = s_ref[0].astype(F32)
        for j in range(1, N_DEV):
            acc = acc + s_ref[j].astype(F32)
        o_ref[...] = acc

    return pl.pallas_call(
        body, name=name, grid=(R // tr,), in_specs=[pl.BlockSpec((N_DEV, tr, C), lambda i: (0, i, 0))],
        out_specs=_rows(tr, C), out_shape=jax.ShapeDtypeStruct((R, C), F32), compiler_params=_cp(("parallel",)),
    )(slots)


def _allgather(x_shard, name):
    m_per, n = x_shard.shape

    def body(x_ref, out_ref, send_sems, recv_sems, local_sem):
        x, y, c = lax.axis_index("x"), lax.axis_index("y"), lax.axis_index("c")
        me, sibling = (x, y, c), (x, y, 1 - c)
        chips = [(1 - x, y), (x, 1 - y), (1 - x, 1 - y)]

        def rows(px, py, pc):
            return out_ref.at[4 * px + 2 * py + pc]

        def copy(k, block, to, src=None):
            return pltpu.make_async_remote_copy(
                src_ref=rows(*block) if src is None else src, dst_ref=rows(*block),
                send_sem=send_sems.at[k], recv_sem=recv_sems.at[k], device_id=to, device_id_type=MESH)

        mine = pltpu.make_async_copy(x_ref, rows(*me), local_sem)
        mine.start()
        first = [copy(0, me, sibling, src=x_ref)]
        first += [copy(1 + j, me, (*chip, c), src=x_ref) for j, chip in enumerate(chips)]
        for cp in first:
            cp.start()
        passed = [copy(4 + j, (*chip, c), sibling) for j, chip in enumerate(chips)]
        for j, chip in enumerate(chips):
            copy(1 + j, (*chip, c), me).wait_recv()
            passed[j].start()
        copy(0, sibling, me).wait_recv()
        for j, chip in enumerate(chips):
            copy(4 + j, (*chip, 1 - c), me).wait_recv()
        for cp in first + passed:
            cp.wait_send()
        mine.wait()

    return pl.pallas_call(
        body, name=name, out_shape=jax.ShapeDtypeStruct((N_DEV, m_per, n), x_shard.dtype),
        in_specs=[pl.BlockSpec(memory_space=pltpu.VMEM)], out_specs=pl.BlockSpec(memory_space=pltpu.VMEM),
        scratch_shapes=[pltpu.SemaphoreType.DMA((7,)), pltpu.SemaphoreType.DMA((7,)), pltpu.SemaphoreType.DMA],
        compiler_params=pltpu.CompilerParams(vmem_limit_bytes=VMEM_BIG),
    )(x_shard)


def _alltoall(slabs, name):
    na = len(slabs)

    def body(*refs):
        ins, outs = refs[:na], refs[na:2 * na]
        send_sems, recv_sems, local_sems = refs[2 * na:]
        x, y, c = lax.axis_index("x"), lax.axis_index("y"), lax.axis_index("c")
        me = 4 * x + 2 * y + c
        copies = []
        for a in range(na):
            loc = pltpu.make_async_copy(ins[a].at[me], outs[a].at[me], local_sems.at[a])
            loc.start()
            copies.append(loc)
        rem = []
        for k in range(1, N_DEV):
            px, py, pc = x ^ (k >> 2), y ^ ((k >> 1) & 1), c ^ (k & 1)
            peer = 4 * px + 2 * py + pc
            for a in range(na):
                cp = pltpu.make_async_remote_copy(
                    src_ref=ins[a].at[peer], dst_ref=outs[a].at[me],
                    send_sem=send_sems.at[a, k - 1], recv_sem=recv_sems.at[a, k - 1],
                    device_id=(px, py, pc), device_id_type=MESH)
                cp.start()
                rem.append((a, k, peer, cp))
        for a, k, peer, cp in rem:
            pltpu.make_async_remote_copy(
                src_ref=ins[a].at[me], dst_ref=outs[a].at[peer],
                send_sem=send_sems.at[a, k - 1], recv_sem=recv_sems.at[a, k - 1],
                device_id=(x, y, c), device_id_type=MESH).wait_recv()
        for a, k, peer, cp in rem:
            cp.wait_send()
        for loc in copies:
            loc.wait()

    anyspec = pl.BlockSpec(memory_space=pl.ANY)
    return pl.pallas_call(
        body, name=name, out_shape=[jax.ShapeDtypeStruct(s.shape, s.dtype) for s in slabs],
        in_specs=[anyspec] * na, out_specs=[anyspec] * na,
        scratch_shapes=[pltpu.SemaphoreType.DMA((na, 7)), pltpu.SemaphoreType.DMA((na, 7)),
                        pltpu.SemaphoreType.DMA((na,))],
    )(*slabs)


def _rope_tables(S):
    t = np.arange(S)
    row = (t // GRID_W).astype(np.float32)
    colp = (t % GRID_W).astype(np.float32)
    half = HD // 2
    inv = (ROPE_BASE ** (-np.arange(0, half, 2, dtype=np.float32) / half)).astype(np.float32)
    ar = row[:, None] * inv[None, :]
    ac = colp[:, None] * inv[None, :]
    ang = np.concatenate([ar, ar, ac, ac], axis=-1).astype(np.float32)
    cos = np.cos(ang).astype(np.float32)
    sin = np.sin(ang).astype(np.float32)
    lane = np.arange(HD)
    first = (lane % 32) < 16
    sa = np.where(first[None, :], -sin, 0.0)
    sb = np.where(first[None, :], 0.0, sin)

    def ext(tab, ctx_val):
        full = np.zeros((CTX + S, 128), np.float32)
        full[:CTX, :HD] = ctx_val
        full[CTX:, :HD] = tab
        return jnp.asarray(full)

    return ext(cos, 1.0), ext(sa, 0.0), ext(sb, 0.0)


def _pad_rows_win(wt):
    q, k, v, rest, z = wt[0:512], wt[512:640], wt[640:768], wt[768:2304], wt[2304:2336]

    def padh(t, nh):
        t = t.reshape(nh, HD, D)
        return jnp.pad(t, ((0, 0), (0, 128 - HD), (0, 0))).reshape(nh * 128, D)

    return jnp.concatenate([padh(q, N_ATT), padh(k, N_KV), padh(v, N_KV), rest, jnp.pad(z, ((0, 96), (0, 0)))], axis=0)


def _unpad_rows_win(g):
    def unp(t, nh):
        return t.reshape(nh, 128, D)[:, :HD].reshape(nh * HD, D)

    return jnp.concatenate([unp(g[O_Q:O_K], N_ATT), unp(g[O_K:O_V], N_KV), unp(g[O_V:O_GQ], N_KV),
                            g[O_GQ:O_Z], g[O_Z:O_Z + 32]], axis=0)


def _local_step(x, ctx, target, ada_l, ada_c, gains, sink, win_p, wg_bd, bg, ggla, wout_sh, wffi_sh, wffo_sh):
    S = x.shape[0]
    cos, sa, sb = _rope_tables(S)
    g1, g2, g3, g4 = (gains[i:i + 1] for i in range(4))
    sh1, sc1, gt1, sh2, sc2, gt2 = (ada_l[i:i + 1] for i in range(6))
    sh1c, sc1c = ada_c[0:1], ada_c[1:2]
    gml, gmc, gm2 = g1 * (1.0 + sc1), g1 * (1.0 + sc1c), g3 * (1.0 + sc2)
    mavg = jnp.asarray(np.kron(np.eye(N_GLA, dtype=np.float32), np.full((DV, DV), 1.0 / DV, np.float32))).astype(_BF)

    n_ffi, r_ffo, r_out = wffi_sh.shape[0], wffo_sh.shape[0], wout_sh.shape[0]
    tt_e = 768 if (S + CTX) % 768 == 0 else 256
    tt_s = 512 if S % 512 == 0 else 256
    h, q, k, v, gq, gk, gv, gg, z, la, wout_g = _inproj_fwd(x, ctx, gml, sh1, gmc, sh1c, win_p, wg_bd, bg,
                                                            cos, sa, sb, [wout_sh])
    attn, lse, wffi_g = _attn_fwd(q, k, v, sink, [wffi_sh])
    o_f, st_f, wffo_g = _gla_fwd(gq, gk, gv, la, False, [wffo_sh])
    o_b, st_b = _gla_fwd(gq, gk, gv, la, True)
    wout = wout_g.reshape(N_DEV * r_out, D)
    wffi = wffi_g.reshape(N_DEV * n_ffi, D)
    wffo = wffo_g.reshape(N_DEV * r_ffo, D)
    x1, mix = _mix_fwd(x, attn, o_f, o_b, gg, ggla, mavg, wout, gt1, g2)
    dx1, h2, du, act, df, s_ffn, loss = _ffn(x1, target, gm2, sh2, gt2, g4, wffi, wffo)
    slab_ffi = _matmul_tn(h2, du, 512, tt_s, "grad_w_ffn_in", _BF, True).reshape(N_DEV, n_ffi, D)
    slab_ffo = _matmul_tn(act, df, FFN, tt_s, "grad_w_ffn_out", _BF).reshape(N_DEV, r_ffo, D)
    d_attn, do_gla, dgg, dy, s_mix = _mix_bwd(dx1, mix, o_f, o_b, gg, ggla, mavg, wout, gt1, g2)
    slab_out = _matmul_tn(mix, dy, D, tt_s, "grad_w_out", _BF).reshape(N_DEV, r_out, D)
    dq, dk, dv, dsink, got_ffi, got_ffo = _attn_bwd(q, k, v, sink, lse, d_attn, [slab_ffi, slab_ffo])
    dgq_f, dgk_f, dgv_f, dla_f, got_out = _gla_bwd(gq, gk, gv, la, st_f, do_gla, False, [slab_out])
    dgq_b, dgk_b, dgv_b, dla_b = _gla_bwd(gq, gk, gv, la, st_b, do_gla, True)
    dp, dlg, grad_x, s_in, s_bg = _inproj_bwd(x, ctx, gml, gmc, win_p, wg_bd, cos, sa, sb, la, dq, dk, dv,
                                              dgq_f, dgq_b, dgk_f, dgk_b, dgv_f, dgv_b, dgg, dla_f, dla_b, dx1)
    g_win_t = _matmul_tn(h, dp, 512, tt_e, "grad_w_in", _BF, True)
    g_wg = _matmul_tn(z, dlg, 128, tt_e, "grad_w_gate", F32)
    small = _small_grads(s_in, s_ffn, s_mix, ada_l, ada_c, gains, dsink, s_bg, g_wg, loss)
    return dict(grad_x=grad_x, g_win_t=g_win_t, got_out=got_out, got_ffi=got_ffi, got_ffo=got_ffo, small=small)


SMALL_NAMES = ["c_ctx", "b_ada", "g_pre_mix", "g_post_mix", "g_pre_ffn", "g_post_ffn", "attn_sink",
               "b_gate_fwd", "b_gate_bwd", "g_gla_norm", "w_gate_fwd", "w_gate_bwd"]


def _small_update(tot, t_tot, wg_g, w, m, v):
    c1 = 1.0 / (1.0 - ADAM_B1 ** ADAM_STEP)
    c2 = 1.0 / (1.0 - ADAM_B2 ** ADAM_STEP)
    n = len(SMALL_NAMES)

    def body(tot_ref, t_ref, wg_ref, *refs):
        w_r, m_r, v_r = refs[0:n], refs[n:2 * n], refs[2 * n:3 * n]
        g_o, d_o, nm_o, nv_o = refs[3 * n:4 * n], refs[4 * n:5 * n], refs[5 * n:6 * n], refs[6 * n:7 * n]

        def upd(i, idx, g):
            nm = ADAM_B1 * m_r[i][idx] + (1.0 - ADAM_B1) * g
            nv = ADAM_B2 * v_r[i][idx] + (1.0 - ADAM_B2) * (g * g)
            g_o[i][idx] = g
            nm_o[i][idx] = nm
            nv_o[i][idx] = nv
            d_o[i][idx] = -ADAM_LR * ((nm * c1) / (jnp.sqrt(nv * c2) + ADAM_EPS) + ADAM_WD * w_r[i][idx])

        everything = (slice(None), slice(None))
        cc = w_r[0][...]
        sc = _sigmoid(cc)
        upd(0, everything, t_ref[0:1, :] * (sc * (1.0 + cc * (1.0 - sc))))
        for j in range(6):
            upd(1, (slice(None), slice(D * j, D * j + D)),
                tot_ref[R_ADA + j:R_ADA + j + 1, :] + tot_ref[R_ADA_C + j:R_ADA_C + j + 1, :])
        for j in range(4):
            upd(2 + j, everything, tot_ref[R_GAIN + j:R_GAIN + j + 1, :])
        upd(6, everything, tot_ref[R_SINK:R_SINK + 1, 0:N_ATT])
        upd(7, everything, tot_ref[R_BG:R_BG + 1, 0:256])
        upd(8, everything, tot_ref[R_BG:R_BG + 1, 256:512])
        upd(9, everything, tot_ref[R_GGLA:R_GGLA + 1, 0:DV])
        upd(10, (0,), wg_ref[0:GATE_RANK, :])
        upd(11, (0,), wg_ref[GATE_RANK:2 * GATE_RANK, :])

    params = [w[k] for k in SMALL_NAMES] + [m[k] for k in SMALL_NAMES] + [v[k] for k in SMALL_NAMES]
    outs = pl.pallas_call(
        body, name="small_update", grid=(1,),
        in_specs=[_full(tot.shape), _full(t_tot.shape), _full(wg_g.shape)] + [_full(p.shape) for p in params],
        out_specs=[_full(w[k].shape) for k in SMALL_NAMES] * 4,
        out_shape=[jax.ShapeDtypeStruct(w[k].shape, F32) for k in SMALL_NAMES] * 4,
        compiler_params=_cp(("arbitrary",)),
    )(tot, t_tot, wg_g, *params)
    return tuple(dict(zip(SMALL_NAMES, outs[i * n:(i + 1) * n])) for i in range(4))


def kernel(x, c, ctx, c_ctx, w_ada, b_ada, g_pre_mix, g_post_mix, g_pre_ffn, g_post_ffn, w_in, attn_sink, w_gate_fwd, b_gate_fwd, w_gate_bwd, b_gate_bwd, g_gla_norm, w_out, w_ffn_in, w_ffn_out, loss_target, m_c_ctx, m_w_ada, m_b_ada, m_g_pre_mix, m_g_post_mix, m_g_pre_ffn, m_g_post_ffn, m_w_in, m_attn_sink, m_w_gate_fwd, m_b_gate_fwd, m_w_gate_bwd, m_b_gate_bwd, m_g_gla_norm, m_w_out, m_w_ffn_in, m_w_ffn_out, v_c_ctx, v_w_ada, v_b_ada, v_g_pre_mix, v_g_post_mix, v_g_pre_ffn, v_g_post_ffn, v_w_in, v_attn_sink, v_w_gate_fwd, v_b_gate_fwd, v_w_gate_bwd, v_b_gate_bwd, v_g_gla_norm, v_w_out, v_w_ffn_in, v_w_ffn_out):
    me = 4 * lax.axis_index("x") + 2 * lax.axis_index("y") + lax.axis_index("c")
    S = x.shape[1]
    n_in = w_in.shape[2]
    n_ffi = w_ffn_in.shape[2]
    r_out = w_out.shape[1]
    r_ffo = w_ffn_out.shape[1]
    n_ada = w_ada.shape[2]

    small_in = jnp.concatenate([c.reshape(8, 128), w_gate_fwd.reshape(4, 128), w_gate_bwd.reshape(4, 128)], axis=0)
    sg = _allgather(small_in, "gather_small")
    c_all = sg[:, 0:8].reshape(N_DEV, D)
    wgf = sg[:, 8:12].reshape(N_DEV, GATE_RANK, 32).transpose(1, 0, 2).reshape(GATE_RANK, 256)
    wgb = sg[:, 12:16].reshape(N_DEV, GATE_RANK, 32).transpose(1, 0, 2).reshape(GATE_RANK, 256)

    win_t = _allgather(w_in[0].T.astype(_BF), "gather_weights").reshape(N_DEV * n_in, D)
    win_p = _pad_rows_win(win_t)
    wg_bd = jnp.zeros((128, 512), F32).at[0:16, 0:256].set(wgf).at[16:32, 256:512].set(wgb).astype(_BF)

    ada_part = _ada_fwd(c_all, c_ctx.reshape(1, D), w_ada[0])
    ada_all = _allgather(ada_part, "gather_ada")
    ada_full = ada_all.transpose(1, 0, 2).reshape(16, N_DEV * n_ada) + b_ada
    ada_l = jnp.pad(lax.dynamic_slice_in_dim(ada_full, me, 1, 0).reshape(6, D), ((0, 2), (0, 0)))
    ada_c = jnp.pad(ada_full[8].reshape(6, D), ((0, 2), (0, 0)))
    gains = jnp.pad(jnp.concatenate([g_pre_mix, g_post_mix, g_pre_ffn, g_post_ffn], axis=0), ((0, 4), (0, 0)))
    sink = jnp.broadcast_to(attn_sink.reshape(8, 1), (8, 128))
    bg = jnp.concatenate([b_gate_fwd, b_gate_bwd], axis=1)
    ggla = jnp.tile(g_gla_norm, (1, N_GLA))

    r = _local_step(x[0], ctx[0], loss_target[0], ada_l, ada_c, gains, sink, win_p, wg_bd, bg, ggla,
                    w_out[0].astype(_BF), w_ffn_in[0].T.astype(_BF), w_ffn_out[0].astype(_BF))

    slots = _alltoall([_unpad_rows_win(r["g_win_t"]).reshape(N_DEV, n_in, D)], "scatter_grads")

    parts = _allgather(r["small"], "gather_small_grads")
    tot = _sum_slots(parts, "sum_small_grads")
    loss = tot[R_LOSS, 0]
    d_ada_rows = parts[:, R_ADA:R_ADA + 6].reshape(N_DEV, 6 * D)
    d_ada_c = tot[R_ADA_C:R_ADA_C + 6].reshape(1, 6 * D)
    my_cols = lax.dynamic_slice_in_dim(jnp.concatenate([d_ada_rows, jnp.broadcast_to(d_ada_c, (1, 6 * D)),
                                                        jnp.zeros((7, 6 * D), F32)], axis=0), me * n_ada, n_ada, 1)
    grad_w_ada, t_part = _ada_bwd(c_all, c_ctx.reshape(1, D), w_ada[0], my_cols)
    t_all = _allgather(t_part, "gather_c_ctx")
    t_tot = _sum_slots(t_all, "sum_c_ctx")
    wg_g = lax.dynamic_slice(tot, (R_WG, me * 32), (2 * GATE_RANK, 32))

    w_small = dict(c_ctx=c_ctx.reshape(1, D), b_ada=b_ada, g_pre_mix=g_pre_mix, g_post_mix=g_post_mix, g_pre_ffn=g_pre_ffn,
                   g_post_ffn=g_post_ffn, attn_sink=attn_sink, b_gate_fwd=b_gate_fwd, b_gate_bwd=b_gate_bwd,
                   g_gla_norm=g_gla_norm, w_gate_fwd=w_gate_fwd, w_gate_bwd=w_gate_bwd)
    m_small = dict(c_ctx=m_c_ctx.reshape(1, D), b_ada=m_b_ada, g_pre_mix=m_g_pre_mix, g_post_mix=m_g_post_mix,
                   g_pre_ffn=m_g_pre_ffn, g_post_ffn=m_g_post_ffn, attn_sink=m_attn_sink, b_gate_fwd=m_b_gate_fwd,
                   b_gate_bwd=m_b_gate_bwd, g_gla_norm=m_g_gla_norm, w_gate_fwd=m_w_gate_fwd, w_gate_bwd=m_w_gate_bwd)
    v_small = dict(c_ctx=v_c_ctx.reshape(1, D), b_ada=v_b_ada, g_pre_mix=v_g_pre_mix, g_post_mix=v_g_post_mix,
                   g_pre_ffn=v_g_pre_ffn, g_post_ffn=v_g_post_ffn, attn_sink=v_attn_sink, b_gate_fwd=v_b_gate_fwd,
                   b_gate_bwd=v_b_gate_bwd, g_gla_norm=v_g_gla_norm, w_gate_fwd=v_w_gate_fwd, w_gate_bwd=v_w_gate_bwd)
    grads_small, d_s, nm_s, nv_s = _small_update(tot, t_tot, wg_g, w_small, m_small, v_small)
    for dd in (grads_small, d_s, nm_s, nv_s):
        dd["c_ctx"] = dd["c_ctx"].reshape(D)

    big = {}
    for nm, w, g, m, v in [("w_ada", w_ada, grad_w_ada, m_w_ada, v_w_ada),
                           ("w_out", w_out, r["got_out"], m_w_out, v_w_out),
                           ("w_ffn_out", w_ffn_out, r["got_ffo"], m_w_ffn_out, v_w_ffn_out)]:
        big[nm] = _adamw(w, g, m, v, "adamw_" + nm)
    tr = lambda a: jnp.transpose(a, (0, 2, 1))
    for nm, w, g, m, v in [("w_in", w_in, slots[0], m_w_in, v_w_in),
                           ("w_ffn_in", w_ffn_in, r["got_ffi"], m_w_ffn_in, v_w_ffn_in)]:
        big[nm] = tuple(tr(o) for o in _adamw(tr(w), g, tr(m), tr(v), "adamw_" + nm))

    order = ["c_ctx", "w_ada", "b_ada", "g_pre_mix", "g_post_mix", "g_pre_ffn", "g_post_ffn", "w_in", "attn_sink",
             "w_gate_fwd", "b_gate_fwd", "w_gate_bwd", "b_gate_bwd", "g_gla_norm", "w_out", "w_ffn_in", "w_ffn_out"]
    grads, deltas, new_m, new_v = [], [], [], []
    for nm in order:
        if nm in big:
            g_, d_, m_, v_ = big[nm]
        else:
            g_, d_, m_, v_ = grads_small[nm], d_s[nm], nm_s[nm], nv_s[nm]
        grads.append(g_)
        deltas.append(d_)
        new_m.append(m_)
        new_v.append(v_)
    return (loss, r["grad_x"][None], *grads, *deltas, *new_m, *new_v)
```

```python
import functools
import math

import numpy as np
import jax
import jax.numpy as jnp
from jax import lax
from jax.experimental import pallas as pl
from jax.experimental.pallas import tpu as pltpu

F32 = jnp.float32
_BF = jnp.bfloat16

N_DEV = 8
D = 1024
CTX = 256
HD = 64
N_ATT = 8
N_KV = 2
GRP = N_ATT // N_KV
WIN = 128
GRID_W = 64
ROPE_BASE = 10000.0
N_GLA = 8
DK = 32
DV = 64
GATE_RANK = 16
GATE_TAU = 16.0
FFN = 2816
EPS = 1e-6
NEG = -1e30
GLA_T = 128

QP = N_ATT * 128
KP = N_KV * 128
O_Q, O_K, O_V = 0, QP, QP + KP
O_GQ = O_V + KP
O_GK = O_GQ + N_GLA * DK
O_GV = O_GK + N_GLA * DK
O_GG = O_GV + N_GLA * DV
O_Z = O_GG + N_GLA * DV
NP = O_Z + 128
IN_COLS = 2336

ADAM_LR, ADAM_B1, ADAM_B2, ADAM_EPS, ADAM_WD, ADAM_STEP = 0.001, 0.9, 0.999, 1e-08, 0.01, 10

VMEM_BIG = 56 * 1024 * 1024
MESH = pl.DeviceIdType.MESH


def _cp(sem, vmem=None):
    return pltpu.CompilerParams(dimension_semantics=sem, vmem_limit_bytes=vmem)


def _full(shape):
    nd = len(shape)
    return pl.BlockSpec(shape, lambda *a: (0,) * nd)


def _rows(tile, width, off=0):
    return pl.BlockSpec((tile, width), lambda i: (i + off, 0))


def _rows_lat(tile, width):
    return pl.BlockSpec((tile, width), lambda i: (jnp.maximum(i - 1, 0), 0))


def _nt(a, b):
    return lax.dot_general(a, b, (((1,), (1,)), ((), ())), preferred_element_type=F32)


def _tn(a, b):
    return lax.dot_general(a, b, (((0,), (0,)), ((), ())), preferred_element_type=F32)


def _nn(a, b):
    return jnp.dot(a, b, preferred_element_type=F32)


def _head_mean(x, mavg):
    n = x.shape[0]
    hi = x.astype(_BF)
    lo = (x - hi.astype(F32)).astype(_BF)
    y = _nn(jnp.concatenate([hi, lo], axis=0), mavg)
    return y[0:n] + y[n:2 * n]


def _rope(t, cos, sa, sb):
    n = t.shape[1]
    reps = n // 128
    c = jnp.tile(cos, (1, reps))
    a = jnp.tile(sa, (1, reps))
    b = jnp.tile(sb, (1, reps))
    return t * c + pltpu.roll(t, n - 16, 1) * a + pltpu.roll(t, 16, 1) * b


def _unrope(t, cos, sa, sb):
    n = t.shape[1]
    reps = n // 128
    c = jnp.tile(cos, (1, reps))
    a = jnp.tile(sa, (1, reps))
    b = jnp.tile(sb, (1, reps))
    return t * c + pltpu.roll(t * a, 16, 1) + pltpu.roll(t * b, n - 16, 1)


def _sigmoid(x):
    return 1.0 / (1.0 + jnp.exp(-x))


def _inproj_fwd(x, ctx, gml, shl, gmc, shc, win, wg, bg, cos, sa, sb, shards):
    E = x.shape[0] + CTX
    TE = CTX

    def body(x_ref, c_ref, gml_ref, shl_ref, gmc_ref, shc_ref, w_ref, wg_ref, bg_ref, cos_ref, sa_ref, sb_ref,
             h_ref, q_ref, k_ref, v_ref, gq_ref, gk_ref, gv_ref, gg_ref, z_ref, la_ref):
        is_ctx = pl.program_id(0) == 0
        gm = jnp.where(is_ctx, gmc_ref[...], gml_ref[...])
        sh = jnp.where(is_ctx, shc_ref[...], shl_ref[...])
        x = jnp.where(is_ctx, c_ref[...], x_ref[...])
        r = lax.rsqrt(jnp.mean(x * x, axis=-1, keepdims=True) + EPS)
        hb = ((x * r) * gm + sh).astype(_BF)
        h_ref[...] = hb
        p = _nt(hb, w_ref[...])
        cos_t, sa_t, sb_t = cos_ref[...], sa_ref[...], sb_ref[...]
        q_ref[...] = (_rope(p[:, O_Q:O_K], cos_t, sa_t, sb_t) * (HD ** -0.5)).astype(_BF)
        k_ref[...] = _rope(p[:, O_K:O_V], cos_t, sa_t, sb_t).astype(_BF)
        v_ref[...] = p[:, O_V:O_GQ].astype(_BF)
        gq_ref[...] = p[:, O_GQ:O_GK] * (DK ** -0.5)
        gk_ref[...] = p[:, O_GK:O_GV]
        gv_ref[...] = p[:, O_GV:O_GG]
        gg_ref[...] = p[:, O_GG:O_Z]
        zb = p[:, O_Z:NP].astype(_BF)
        z_ref[...] = zb
        lg = _nn(zb, wg_ref[...]) + bg_ref[...]
        la_ref[...] = (jnp.minimum(lg, 0.0) - jnp.log(1.0 + jnp.exp(-jnp.abs(lg)))) * (1.0 / GATE_TAU)

    vec = _full((1, D))
    tab = _rows(TE, 128)
    outs = [(D, _BF), (QP, _BF), (KP, _BF), (KP, _BF), (256, F32), (256, F32), (512, F32), (512, F32),
            (128, _BF), (512, F32)]
    return _hosted_call(
        body, (x, ctx, gml, shl, gmc, shc, win, wg, bg, cos, sa, sb), shards, True,
        name="inproj_fwd", grid=(E // TE,),
        in_specs=[_rows_lat(TE, D), _full((CTX, D)), vec, vec, vec, vec, _full((NP, D)), _full((128, 512)),
                  _full((1, 512)), tab, tab, tab],
        out_specs=[_rows(TE, w) for w, _ in outs],
        out_shape=[jax.ShapeDtypeStruct((E, w), dt) for w, dt in outs],
        compiler_params=_cp(("arbitrary",), 40 * 1024 * 1024))


def _xchg_scratch(na):
    return [pltpu.SemaphoreType.DMA((na, N_DEV - 1)), pltpu.SemaphoreType.DMA((na, N_DEV - 1)),
            pltpu.SemaphoreType.DMA((na,))]


def _xchg_copies(ins, outs, send_sems, recv_sems, local_sems, gather):
    x, y, c = lax.axis_index("x"), lax.axis_index("y"), lax.axis_index("c")
    me = 4 * x + 2 * y + c
    local, sends, recvs = [], [], []
    for a in range(len(ins)):
        local.append(pltpu.make_async_copy(ins[a] if gather else ins[a].at[me], outs[a].at[me], local_sems.at[a]))
    for k in range(1, N_DEV):
        px, py, pc = x ^ (k >> 2), y ^ ((k >> 1) & 1), c ^ (k & 1)
        peer = 4 * px + 2 * py + pc
        for a in range(len(ins)):
            sems = dict(send_sem=send_sems.at[a, k - 1], recv_sem=recv_sems.at[a, k - 1], device_id_type=MESH)
            sends.append(pltpu.make_async_remote_copy(
                src_ref=ins[a] if gather else ins[a].at[peer], dst_ref=outs[a].at[me], device_id=(px, py, pc), **sems))
            recvs.append(pltpu.make_async_remote_copy(
                src_ref=ins[a] if gather else ins[a].at[me], dst_ref=outs[a].at[peer], device_id=(x, y, c), **sems))
    return local, sends, recvs


def _xchg_start(cps):
    local, sends, _ = cps
    for cp in local + sends:
        cp.start()


def _xchg_finish(cps):
    local, sends, recvs = cps
    for cp in recvs:
        cp.wait_recv()
    for cp in sends:
        cp.wait_send()
    for cp in local:
        cp.wait()


def _xchg_out_shapes(ins, gather):
    return [jax.ShapeDtypeStruct(((N_DEV,) + s.shape) if gather else s.shape, s.dtype) for s in ins]


def _hosted_call(body, args, hosted, gather, *, grid, in_specs, out_specs, out_shape, scratch_shapes=(), **kw):
    na = len(hosted)
    if na == 0:
        return pl.pallas_call(body, grid=grid, in_specs=in_specs, out_specs=out_specs, out_shape=out_shape,
                              scratch_shapes=list(scratch_shapes), **kw)(*args)
    n_in, n_out, n_scr = len(in_specs), len(out_specs), len(scratch_shapes)

    def wrapped(*refs):
        ins, h_in = refs[:n_in], refs[n_in:n_in + na]
        outs, h_out = refs[n_in + na:n_in + na + n_out], refs[n_in + na + n_out:n_in + 2 * na + n_out]
        scr = refs[n_in + 2 * na + n_out:]
        cps = _xchg_copies(h_in, h_out, *scr[n_scr:], gather=gather)
        pids = [pl.program_id(a) for a in range(len(grid))]
        first = functools.reduce(jnp.logical_and, [p == 0 for p in pids])
        last = functools.reduce(jnp.logical_and, [p == g - 1 for p, g in zip(pids, grid)])

        @pl.when(first)
        def _():
            _xchg_start(cps)

        body(*ins, *outs, *scr[:n_scr])

        @pl.when(last)
        def _():
            _xchg_finish(cps)

    anyspec = pl.BlockSpec(memory_space=pl.ANY)
    return pl.pallas_call(
        wrapped, grid=grid, in_specs=list(in_specs) + [anyspec] * na, out_specs=list(out_specs) + [anyspec] * na,
        out_shape=list(out_shape) + _xchg_out_shapes(hosted, gather),
        scratch_shapes=list(scratch_shapes) + _xchg_scratch(na), **kw)(*args, *hosted)


def _attn_specs(E):
    nb = (E - CTX) // WIN
    last = E // WIN - 1
    kc = pl.BlockSpec((CTX, KP), lambda n: (0, 0))
    kp = pl.BlockSpec((WIN, KP), lambda n: (n + 1, 0))
    kk = pl.BlockSpec((WIN, KP), lambda n: (n + 2, 0))
    kn = pl.BlockSpec((WIN, KP), lambda n: (jnp.minimum(n + 3, last), 0))
    return nb, [kc, kp, kk, kn]


def _attn_bias(nb):
    rows = np.arange(GRP * WIN)[:, None] % WIN
    cols = np.arange(CTX + 3 * WIN)[None, :]
    j = cols - CTX
    band = np.abs(j - WIN - rows) <= WIN
    out = []
    for first, last in ((True, False), (False, False), (False, True)):
        ok = (cols < CTX) | (band & ((j >= WIN) | (not first)) & ((j < 2 * WIN) | (not last)))
        out.append(np.where(ok, 0.0, NEG).astype(np.float32))
    bias = jnp.asarray(np.stack(out))
    spec = pl.BlockSpec((1, GRP * WIN, CTX + 3 * WIN),
                        lambda n: (jnp.where(n == 0, 0, jnp.where(n == nb - 1, 2, 1)), 0, 0))
    return bias, spec


def _attn_fwd(q, k, v, sink, shards):
    E = q.shape[0]
    S = E - CTX
    nb, kspecs = _attn_specs(E)
    na = len(shards)

    def body(q_ref, kc, kp, kk, kn, vc, vp, vk, vn, sink_ref, bias_ref, *rest):
        shard_refs, (o_ref, lse_ref), got_refs = rest[:na], rest[na:na + 2], rest[na + 2:2 * na + 2]
        n = pl.program_id(0)
        cps = _xchg_copies(shard_refs, got_refs, *rest[2 * na + 2:], gather=True)

        @pl.when(n == 0)
        def _():
            _xchg_start(cps)

        lane = lax.broadcasted_iota(jnp.int32, (WIN, 128), 1)
        lse_t = jnp.zeros((WIN, 128), F32)
        hs = [slice(128 * h, 128 * h + 128) for h in range(N_KV)]
        K = [jnp.concatenate([kc[:, s_], kp[:, s_], kk[:, s_], kn[:, s_]], axis=0) for s_ in hs]
        Q = [jnp.concatenate([q_ref[:, 128 * (GRP * h + g):128 * (GRP * h + g) + 128] for g in range(GRP)], axis=0)
             for h in range(N_KV)]
        sk = [jnp.concatenate([jnp.broadcast_to(sink_ref[GRP * h + g:GRP * h + g + 1, 0:1], (WIN, 1))
                               for g in range(GRP)], axis=0) for h in range(N_KV)]
        s = [_nt(Q[h], K[h]) + bias_ref[0] for h in range(N_KV)]
        m = [jnp.maximum(jnp.max(s[h], axis=1, keepdims=True), sk[h]) for h in range(N_KV)]
        e = [jnp.exp(s[h] - m[h]) for h in range(N_KV)]
        den = [jnp.sum(e[h], axis=1, keepdims=True) + jnp.exp(sk[h] - m[h]) for h in range(N_KV)]
        V = [jnp.concatenate([vc[:, s_], vp[:, s_], vk[:, s_], vn[:, s_]], axis=0) for s_ in hs]
        o = [_nn((e[h] * (1.0 / den[h])).astype(_BF), V[h]) for h in range(N_KV)]
        for h in range(N_KV):
            lse = m[h] + jnp.log(den[h])
            for g in range(GRP):
                lse_t = jnp.where(lane == GRP * h + g, lse[WIN * g:WIN * g + WIN], lse_t)
            for pp in range(GRP // 2):
                a = o[h][WIN * 2 * pp:WIN * 2 * pp + WIN]
                b = o[h][WIN * (2 * pp + 1):WIN * (2 * pp + 1) + WIN]
                t = 2 * h + pp
                o_ref[:, 128 * t:128 * t + 128] = (a + pltpu.roll(b, 64, 1)).astype(_BF)
        lse_ref[...] = lse_t

        @pl.when(n == nb - 1)
        def _():
            _xchg_finish(cps)

    qs = pl.BlockSpec((WIN, QP), lambda n: (n + 2, 0))
    anyspec = pl.BlockSpec(memory_space=pl.ANY)
    bias, bias_spec = _attn_bias(nb)
    return pl.pallas_call(
        body, name="attn_fwd", grid=(nb,),
        in_specs=[qs] + kspecs + kspecs + [_full((8, 128)), bias_spec] + [anyspec] * na,
        out_specs=[_rows(WIN, 512), _rows(WIN, 128)] + [anyspec] * na,
        out_shape=[jax.ShapeDtypeStruct((S, 512), _BF), jax.ShapeDtypeStruct((S, 128), F32)]
        + _xchg_out_shapes(shards, True),
        scratch_shapes=_xchg_scratch(na),
        compiler_params=_cp(("arbitrary",)),
    )(q, k, k, k, k, v, v, v, v, sink, bias, *shards)


def _attn_bwd(q, k, v, sink, lse, d_attn, slabs):
    E = q.shape[0]
    S = E - CTX
    nb, kspecs = _attn_specs(E)
    last = E // WIN - 1
    na = len(slabs)

    def body(q_ref, kc, kp, kk, kn, vc, vp, vk, vn, sink_ref, bias_ref, lse_ref, do_ref, *rest):
        slab_refs, (dq_ref, dk_ref, dv_ref, ds_ref), got_refs = rest[:na], rest[na:na + 4], rest[na + 4:2 * na + 4]
        n = pl.program_id(0)
        cps = _xchg_copies(slab_refs, got_refs, *rest[2 * na + 4:], gather=False)

        @pl.when(n == 0)
        def _():
            _xchg_start(cps)
            dk_ref[...] = jnp.zeros_like(dk_ref)
            dv_ref[...] = jnp.zeros_like(dv_ref)
            ds_ref[...] = jnp.zeros_like(ds_ref)

        lane = lax.broadcasted_iota(jnp.int32, (WIN, 128), 1)
        lse_t = lse_ref[...]
        starts = [None, pl.multiple_of((n + 1) * WIN, WIN), pl.multiple_of((n + 2) * WIN, WIN),
                  pl.multiple_of(jnp.minimum(n + 3, last) * WIN, WIN)]
        for h in range(N_KV):
            hs = slice(128 * h, 128 * h + 128)
            K = jnp.concatenate([kc[:, hs], kp[:, hs], kk[:, hs], kn[:, hs]], axis=0)
            V = jnp.concatenate([vc[:, hs], vp[:, hs], vk[:, hs], vn[:, hs]], axis=0)
            Q = jnp.concatenate([q_ref[:, 128 * (GRP * h + g):128 * (GRP * h + g) + 128] for g in range(GRP)], axis=0)
            sk = jnp.concatenate([jnp.broadcast_to(sink_ref[GRP * h + g:GRP * h + g + 1, 0:1], (WIN, 1))
                                  for g in range(GRP)], axis=0)
            ls = jnp.concatenate([jnp.sum(jnp.where(lane == GRP * h + g, lse_t, 0.0), axis=1, keepdims=True)
                                  for g in range(GRP)], axis=0)
            dos = []
            for g in range(GRP):
                j = GRP * h + g
                t = do_ref[:, 128 * (j // 2):128 * (j // 2) + 128].astype(F32)
                if j % 2:
                    t = pltpu.roll(t, 64, 1)
                dos.append(jnp.where(lane < HD, t, 0.0))
            do = jnp.concatenate(dos, axis=0).astype(_BF)
            p = jnp.exp(_nt(Q, K) + bias_ref[0] - ls)
            dp = _nt(do, V)
            delta = jnp.sum(p * dp, axis=1, keepdims=True)
            dsc = (p * (dp - delta)).astype(_BF)
            dq = _nn(dsc, K) * (HD ** -0.5)
            for g in range(GRP):
                j = GRP * h + g
                dq_ref[:, 128 * j:128 * j + 128] = dq[WIN * g:WIN * g + WIN].astype(_BF)
            dK = _tn(Q, dsc)
            dV = _tn(do, p.astype(_BF))
            dk_ref[hs, 0:CTX] += dK[:, 0:CTX]
            dv_ref[hs, 0:CTX] += dV[:, 0:CTX]
            for w in range(1, 4):
                lo = CTX + WIN * (w - 1)
                dk_ref[hs, pl.ds(starts[w], WIN)] += dK[:, lo:lo + WIN]
                dv_ref[hs, pl.ds(starts[w], WIN)] += dV[:, lo:lo + WIN]
            psk = -jnp.exp(sk - ls) * delta
            for g in range(GRP):
                j = GRP * h + g
                tot = jnp.sum(psk[WIN * g:WIN * g + WIN], axis=0, keepdims=True)
                ds_ref[j:j + 1, :] += jnp.broadcast_to(tot, (1, 128))

        @pl.when(n == nb - 1)
        def _():
            _xchg_finish(cps)

    qs = pl.BlockSpec((WIN, QP), lambda n: (n + 2, 0))
    anyspec = pl.BlockSpec(memory_space=pl.ANY)
    bias, bias_spec = _attn_bias(nb)
    return pl.pallas_call(
        body, name="attn_bwd", grid=(nb,),
        in_specs=[qs] + kspecs + kspecs + [_full((8, 128)), bias_spec, _rows(WIN, 128), _rows(WIN, 512)]
        + [anyspec] * na,
        out_specs=[_rows(WIN, QP), _full((KP, E)), _full((KP, E)), _full((8, 128))] + [anyspec] * na,
        out_shape=[jax.ShapeDtypeStruct((S, QP), _BF), jax.ShapeDtypeStruct((KP, E), F32),
                   jax.ShapeDtypeStruct((KP, E), F32), jax.ShapeDtypeStruct((8, 128), F32)]
        + _xchg_out_shapes(slabs, False),
        scratch_shapes=_xchg_scratch(na),
        compiler_params=_cp(("arbitrary",), 48 * 1024 * 1024),
    )(q, k, k, k, k, v, v, v, v, sink, bias, lse, d_attn, *slabs)


def _gla_order(E, reverse, backward):
    nc = CTX // GLA_T
    n = E // GLA_T
    if not reverse:
        fwd = lambda s: s
    else:
        fwd = lambda s: jnp.where(s < nc, nc - 1 - s, n - 1 + nc - s)
    if backward:
        return lambda s: fwd(n - 1 - s)
    return fwd


def _gla_masks():
    T = GLA_T
    l128 = lax.broadcasted_iota(jnp.int32, (1, 128), 1)
    qmask = [((l128 >> 5) == j).astype(F32) for j in range(4)]
    vmask = [((l128 >> 6) == j).astype(F32) for j in range(2)]
    bd = ((lax.broadcasted_iota(jnp.int32, (512, 256), 0) >> 6)
          == (lax.broadcasted_iota(jnp.int32, (512, 256), 1) >> 5)).astype(F32)
    ri = lax.broadcasted_iota(jnp.int32, (T, 2 * T), 0)
    ci = lax.broadcasted_iota(jnp.int32, (T, 2 * T), 1) & (T - 1)
    return qmask, vmask, bd, ri, ci


def _tri_sum(tri, x):
    hi = x.astype(_BF)
    r1 = x - hi.astype(F32)
    mid = r1.astype(_BF)
    lo = (r1 - mid.astype(F32)).astype(_BF)
    n = x.shape[1]
    y = _nn(tri.astype(_BF), jnp.concatenate([hi, mid, lo], axis=1))
    return y[:, 0:n] + y[:, n:2 * n] + y[:, 2 * n:3 * n]


def _gla_decays(la, reverse, ri, ci):
    T = GLA_T
    msk2 = (ri <= ci) if reverse else (ri >= ci)
    mskT2 = (ri >= ci) if reverse else (ri <= ci)
    b = _tri_sum(msk2[:, 0:T], la)
    bT = b[0:1] if reverse else b[T - 1:T]
    bm = b[T // 2:T // 2 + 1]
    return msk2, mskT2, b, bT, bm


def _pair_stack(tile, m0, m1):
    return jnp.concatenate([(tile * m0).astype(_BF), (tile * m1).astype(_BF)], axis=0)


def _gla_fwd(gq, gk, gv, la, reverse, shards=()):
    E = gq.shape[0]
    T = GLA_T
    n = E // T
    order = _gla_order(E, reverse, False)
    col = 1 if reverse else 0

    def body(gq_ref, gk_ref, gv_ref, la_ref, o_ref, st_ref, S_scr):
        @pl.when(pl.program_id(0) == 0)
        def _():
            S_scr[...] = jnp.zeros_like(S_scr)

        qmask, vmask, bd, ri, ci = _gla_masks()
        msk2, _, b, bT, bm = _gla_decays(la_ref[...], reverse, ri, ci)
        q, k, v = gq_ref[...], gk_ref[...], gv_ref[...]
        qd = (q * jnp.exp(b)).astype(_BF)
        qm = (q * jnp.exp(b - bm)).astype(_BF)
        km = k * jnp.exp(bm - b)
        kd = (k * jnp.exp(bT - b)).astype(_BF)
        ST = S_scr[...]
        comp = ST[0:DV]
        for h in range(1, N_GLA):
            comp = comp + ST[DV * h:DV * h + DV]
        st_ref[0] = comp
        inter = _nt(qd, ST.astype(_BF))
        tiles = []
        for p in range(N_GLA // 2):
            qs = slice(128 * (p // 2), 128 * (p // 2) + 128)
            vs = slice(128 * p, 128 * p + 128)
            j0 = (2 * p) % 4
            KS = _pair_stack(km[:, qs], qmask[j0], qmask[j0 + 1])
            VS = _pair_stack(v[:, vs], vmask[0], vmask[1])
            AA = jnp.where(msk2, _nt(qm[:, qs], KS), 0.0).astype(_BF)
            tiles.append(_nn(AA, VS))
        o_ref[...] = inter + jnp.concatenate(tiles, axis=1)
        S_scr[...] = ST * jnp.exp(bT) + bd * _tn(v.astype(_BF), kd)

    blk = lambda w, c=0: pl.BlockSpec((T, w), lambda s: (order(s), c))
    return _hosted_call(
        body, (gq, gk, gv, la), shards, True,
        name="gla_fwd_rev" if reverse else "gla_fwd", grid=(n,),
        in_specs=[blk(256), blk(256), blk(512), blk(256, col)],
        out_specs=[blk(512), pl.BlockSpec((1, DV, 256), lambda s: (order(s), 0, 0))],
        out_shape=[jax.ShapeDtypeStruct((E, 512), F32), jax.ShapeDtypeStruct((n, DV, 256), F32)],
        scratch_shapes=[pltpu.VMEM((512, 256), F32)],
        compiler_params=_cp(("arbitrary",)))


def _gla_bwd(gq, gk, gv, la, st, do, reverse, slabs=()):
    E = gq.shape[0]
    T = GLA_T
    n = E // T
    nc = CTX // T
    order = _gla_order(E, reverse, True)
    col = 1 if reverse else 0

    def body(gq_ref, gk_ref, gv_ref, la_ref, st_ref, do_ref, dq_ref, dk_ref, dv_ref, dla_ref, dS_scr):
        @pl.when(pl.program_id(0) == 0)
        def _():
            dS_scr[...] = jnp.zeros_like(dS_scr)

        is_lat = order(pl.program_id(0)) >= nc
        qmask, vmask, bd, ri, ci = _gla_masks()
        msk2, mskT2, b, bT, bm = _gla_decays(la_ref[...], reverse, ri, ci)
        q, k, v = gq_ref[...], gk_ref[...], gv_ref[...]
        do = jnp.where(is_lat, do_ref[...].astype(F32), 0.0)
        e_b, e_qm, e_km, e_kd, e_T = jnp.exp(b), jnp.exp(b - bm), jnp.exp(bm - b), jnp.exp(bT - b), jnp.exp(bT)
        qd, qm, km, kd = q * e_b, q * e_qm, k * e_km, k * e_kd
        qdb, qmb, kmb, kdb, vb, dob = (t.astype(_BF) for t in (qd, qm, km, kd, v, do))
        ST = jnp.tile(st_ref[0], (N_GLA, 1)) * bd
        dST = dS_scr[...]
        dSTb = dST.astype(_BF)
        dqd = _nn(dob, ST.astype(_BF))
        dkd = _nn(vb, dSTb)
        dv_t, dqm_t, dkm_t = [], [None, None], [None, None]
        for p in range(N_GLA // 2):
            t = p // 2
            qs = slice(128 * t, 128 * t + 128)
            vs = slice(128 * p, 128 * p + 128)
            j0 = (2 * p) % 4
            QS = _pair_stack(qm[:, qs], qmask[j0], qmask[j0 + 1])
            KS = _pair_stack(km[:, qs], qmask[j0], qmask[j0 + 1])
            VS = _pair_stack(v[:, vs], vmask[0], vmask[1])
            DS = _pair_stack(do[:, vs], vmask[0], vmask[1])
            ATT = jnp.where(mskT2, _nt(kmb[:, qs], QS), 0.0).astype(_BF)
            dAA = jnp.where(msk2, _nt(dob[:, vs], VS), 0.0).astype(_BF)
            dATT = jnp.where(mskT2, _nt(vb[:, vs], DS), 0.0).astype(_BF)
            dv_t.append(_nn(ATT, DS))
            dq_p = _nn(dAA, KS)
            dk_p = _nn(dATT, QS)
            dqm_t[t] = dq_p if dqm_t[t] is None else dqm_t[t] + dq_p
            dkm_t[t] = dk_p if dkm_t[t] is None else dkm_t[t] + dk_p
        dqm = jnp.concatenate(dqm_t, axis=1)
        dkm = jnp.concatenate(dkm_t, axis=1)
        dq_ref[...] = dqm * e_qm + dqd * e_b
        dk_ref[...] = dkm * e_km + dkd * e_kd
        dv_ref[...] = _nt(kdb, dSTb) + jnp.concatenate(dv_t, axis=1)
        db = dqm * qm - dkm * km + dqd * qd - dkd * kd
        dbT = jnp.sum(dkd * kd, axis=0, keepdims=True) + e_T * jnp.sum(dST * ST, axis=0, keepdims=True)
        dla_ref[...] = _tri_sum(mskT2[:, 0:T], db) + dbT
        dS_scr[...] = dST * e_T + bd * _tn(dob, qdb)

    blk = lambda w, c=0: pl.BlockSpec((T, w), lambda s: (order(s), c))
    do_spec = pl.BlockSpec((T, 512), lambda s: (jnp.maximum(order(s) - nc, 0), 0))
    return _hosted_call(
        body, (gq, gk, gv, la, st, do), slabs, False,
        name="gla_bwd_rev" if reverse else "gla_bwd", grid=(n,),
        in_specs=[blk(256), blk(256), blk(512), blk(256, col),
                  pl.BlockSpec((1, DV, 256), lambda s: (order(s), 0, 0)), do_spec],
        out_specs=[blk(256), blk(256), blk(512), blk(256)],
        out_shape=[jax.ShapeDtypeStruct((E, 256), F32), jax.ShapeDtypeStruct((E, 256), F32),
                   jax.ShapeDtypeStruct((E, 512), F32), jax.ShapeDtypeStruct((E, 256), F32)],
        scratch_shapes=[pltpu.VMEM((512, 256), F32)],
        compiler_params=_cp(("arbitrary",)))


def _gla_out(o_f, o_b, gg, ggla, mavg):
    o = o_f + o_b
    rr = lax.rsqrt(_head_mean(o * o, mavg) + EPS)
    oh = o * rr
    sg = _sigmoid(gg)
    return oh, rr, sg


def _mix_fwd(x, attn, o_f, o_b, gg, ggla, mavg, wout, gt1, g2):
    S = x.shape[0]
    TM = 256

    def body(x_ref, a_ref, of_ref, ob_ref, gg_ref, ggla_ref, mavg_ref, w_ref, gt1_ref, g2_ref, x1_ref, mix_ref):
        gg_t = gg_ref[...]
        oh, _, sg = _gla_out(of_ref[...], ob_ref[...], gg_t, ggla_ref[...], mavg_ref[...])
        mix_ref[:, 0:512] = a_ref[...]
        mix_ref[:, 512:1024] = (oh * ggla_ref[...] * (gg_t * sg)).astype(_BF)
        y = _nn(mix_ref[...], w_ref[...])
        ry = lax.rsqrt(jnp.mean(y * y, axis=-1, keepdims=True) + EPS)
        x1_ref[...] = x_ref[...] + gt1_ref[...] * ((y * ry) * g2_ref[...])

    return pl.pallas_call(
        body, name="mix_fwd", grid=(S // TM,),
        in_specs=[_rows(TM, D), _rows(TM, 512), _rows(TM, 512, 1), _rows(TM, 512, 1), _rows(TM, 512, 1),
                  _full((1, 512)), _full((512, 512)), _full((D, D)), _full((1, D)), _full((1, D))],
        out_specs=[_rows(TM, D), _rows(TM, D)],
        out_shape=[jax.ShapeDtypeStruct((S, D), F32), jax.ShapeDtypeStruct((S, D), _BF)],
        compiler_params=_cp(("arbitrary",), 40 * 1024 * 1024),
    )(x, attn, o_f, o_b, gg, ggla, mavg, wout, gt1, g2)


def _mix_bwd(dx1, mix, o_f, o_b, gg, ggla, mavg, wout, gt1, g2):
    S = dx1.shape[0]
    TM = 256

    def body(dx_ref, mix_ref, of_ref, ob_ref, gg_ref, ggla_ref, mavg_ref, w_ref, gt1_ref, g2_ref,
             da_ref, do_ref, dgg_ref, dy_ref, sums_ref):
        @pl.when(pl.program_id(0) == 0)
        def _():
            sums_ref[...] = jnp.zeros_like(sums_ref)

        dx = dx_ref[...]
        y = _nn(mix_ref[...], w_ref[...])
        ry = lax.rsqrt(jnp.mean(y * y, axis=-1, keepdims=True) + EPS)
        yh = y * ry
        sums_ref[0:1, :] += jnp.sum(dx * yh, axis=0, keepdims=True)
        dyh = dx * (gt1_ref[...] * g2_ref[...])
        dy = (ry * (dyh - yh * jnp.mean(dyh * yh, axis=-1, keepdims=True))).astype(_BF)
        dy_ref[...] = dy
        dmix = _nt(dy, w_ref[...])
        da_ref[...] = dmix[:, 0:512].astype(_BF)
        dgla = dmix[:, 512:1024]
        gg_t = gg_ref[...]
        ggla_t = ggla_ref[...]
        oh, rr, sg = _gla_out(of_ref[...], ob_ref[...], gg_t, ggla_t, mavg_ref[...])
        dgg_ref[...] = (dgla * oh * ggla_t * (sg * (1.0 + gg_t * (1.0 - sg)))).astype(_BF)
        don = dgla * (gg_t * sg)
        sums_ref[1:2, 0:512] += jnp.sum(don * oh, axis=0, keepdims=True)
        doh = don * ggla_t
        do_ref[...] = (rr * (doh - oh * _head_mean(doh * oh, mavg_ref[...]))).astype(_BF)

    return pl.pallas_call(
        body, name="mix_bwd", grid=(S // TM,),
        in_specs=[_rows(TM, D), _rows(TM, D), _rows(TM, 512, 1), _rows(TM, 512, 1), _rows(TM, 512, 1),
                  _full((1, 512)), _full((512, 512)), _full((D, D)), _full((1, D)), _full((1, D))],
        out_specs=[_rows(TM, 512), _rows(TM, 512), _rows(TM, 512), _rows(TM, D), _full((8, D))],
        out_shape=[jax.ShapeDtypeStruct((S, 512), _BF), jax.ShapeDtypeStruct((S, 512), _BF),
                   jax.ShapeDtypeStruct((S, 512), _BF), jax.ShapeDtypeStruct((S, D), _BF),
                   jax.ShapeDtypeStruct((8, D), F32)],
        compiler_params=_cp(("arbitrary",), 40 * 1024 * 1024),
    )(dx1, mix, o_f, o_b, gg, ggla, mavg, wout, gt1, g2)


def _ffn(x1, target, gm2, sh2, gt2, g4, wffi, wffo):
    S = x1.shape[0]
    TF = 256

    def body(x_ref, t_ref, gm_ref, sh_ref, gt_ref, g4_ref, wi_hbm, wo_hbm,
             dx_ref, h_ref, du_ref, act_ref, df_ref, sums_ref, loss_ref, wi, wo, sem):
        @pl.when(pl.program_id(0) == 0)
        def _():
            c1 = pltpu.make_async_copy(wi_hbm, wi, sem.at[0])
            c2 = pltpu.make_async_copy(wo_hbm, wo, sem.at[1])
            c1.start()
            c2.start()
            sums_ref[...] = jnp.zeros_like(sums_ref)
            loss_ref[...] = jnp.zeros_like(loss_ref)
            c1.wait()
            c2.wait()

        x = x_ref[...]
        gm = gm_ref[...]
        r = lax.rsqrt(jnp.mean(x * x, axis=-1, keepdims=True) + EPS)
        xh = x * r
        hb = (xh * gm + sh_ref[...]).astype(_BF)
        h_ref[...] = hb
        u = _nt(hb, wi[...])
        g = u[:, 0:FFN]
        up = u[:, FFN:2 * FFN]
        sg = _sigmoid(g)
        sl = g * sg
        ab = (sl * up).astype(_BF)
        act_ref[...] = ab
        f = _nn(ab, wo[...])
        rf = lax.rsqrt(jnp.mean(f * f, axis=-1, keepdims=True) + EPS)
        fh = f * rf
        gt, g4v = gt_ref[...], g4_ref[...]
        err = x + gt * (fh * g4v) - t_ref[...]
        loss_ref[...] += jnp.sum(err * err) * (0.5 / D)
        dout = err * (1.0 / D)
        sums_ref[2:3, :] += jnp.sum(dout * fh, axis=0, keepdims=True)
        dfh = dout * (gt * g4v)
        dfb = (rf * (dfh - fh * jnp.mean(dfh * fh, axis=-1, keepdims=True))).astype(_BF)
        df_ref[...] = dfb
        dact = _nt(dfb, wo[...])
        du_ref[:, 0:FFN] = (dact * up * (sg * (1.0 + g * (1.0 - sg)))).astype(_BF)
        du_ref[:, FFN:2 * FFN] = (dact * sl).astype(_BF)
        dh = _nn(du_ref[...], wi[...])
        sums_ref[0:1, :] += jnp.sum(dh, axis=0, keepdims=True)
        sums_ref[1:2, :] += jnp.sum(dh * xh, axis=0, keepdims=True)
        dxh = dh * gm
        dx_ref[...] = dout + r * (dxh - xh * jnp.mean(dxh * xh, axis=-1, keepdims=True))

    vec = _full((1, D))
    anyspec = pl.BlockSpec(memory_space=pl.ANY)
    return pl.pallas_call(
        body, name="ffn_fwd_bwd", grid=(S // TF,),
        in_specs=[_rows(TF, D), _rows(TF, D), vec, vec, vec, vec, anyspec, anyspec],
        out_specs=[_rows(TF, D), _rows(TF, D), _rows(TF, 2 * FFN), _rows(TF, FFN), _rows(TF, D),
                   _full((8, D)), _full((8, 128))],
        out_shape=[jax.ShapeDtypeStruct((S, D), F32), jax.ShapeDtypeStruct((S, D), _BF),
                   jax.ShapeDtypeStruct((S, 2 * FFN), _BF), jax.ShapeDtypeStruct((S, FFN), _BF),
                   jax.ShapeDtypeStruct((S, D), _BF), jax.ShapeDtypeStruct((8, D), F32),
                   jax.ShapeDtypeStruct((8, 128), F32)],
        scratch_shapes=[pltpu.VMEM((2 * FFN, D), _BF), pltpu.VMEM((FFN, D), _BF), pltpu.SemaphoreType.DMA((2,))],
        compiler_params=_cp(("arbitrary",), VMEM_BIG),
    )(x1, target, gm2, sh2, gt2, g4, wffi, wffo)


def _inproj_bwd(x, ctx, gml, gmc, win, wg, cos, sa, sb, la, dq, dk, dv, dgq_f, dgq_b, dgk_f, dgk_b, dgv_f, dgv_b,
                dgg, dla_f, dla_b, dx1):
    S = x.shape[0]
    E = S + CTX
    TE = CTX

    def body(x_ref, c_ref, gml_ref, gmc_ref, w_ref, wg_ref, cos_ref, sa_ref, sb_ref, la_ref, dq_ref, dk_ref, dv_ref,
             gqf, gqb, gkf, gkb, gvf, gvb, dgg_ref, dlf, dlb, dx1_ref,
             dp_ref, dlg_ref, gx_ref, sums_ref, bsum_ref):
        i = pl.program_id(0)
        is_ctx = i == 0

        @pl.when(is_ctx)
        def _():
            sums_ref[...] = jnp.zeros_like(sums_ref)
            bsum_ref[...] = jnp.zeros_like(bsum_ref)

        lat = jnp.where(is_ctx, 0.0, 1.0)
        cos_t, sa_t, sb_t = cos_ref[...], sa_ref[...], sb_ref[...]
        dp_ref[:, O_Q:O_K] = (_unrope(dq_ref[...].astype(F32), cos_t, sa_t, sb_t) * lat).astype(_BF)
        dp_ref[:, O_K:O_V] = _unrope(dk_ref[...].T, cos_t, sa_t, sb_t).astype(_BF)
        dp_ref[:, O_V:O_GQ] = dv_ref[...].T.astype(_BF)
        dp_ref[:, O_GQ:O_GK] = ((gqf[...] + gqb[...]) * (DK ** -0.5)).astype(_BF)
        dp_ref[:, O_GK:O_GV] = (gkf[...] + gkb[...]).astype(_BF)
        dp_ref[:, O_GV:O_GG] = (gvf[...] + gvb[...]).astype(_BF)
        dp_ref[:, O_GG:O_Z] = (dgg_ref[...].astype(F32) * lat).astype(_BF)
        la_t = la_ref[...]
        dlg = (jnp.concatenate([dlf[...], dlb[...]], axis=1) * (1.0 - jnp.exp(GATE_TAU * la_t)) * (1.0 / GATE_TAU))
        bsum_ref[0:1, :] += jnp.sum(dlg, axis=0, keepdims=True)
        dlgb = dlg.astype(_BF)
        dlg_ref[...] = dlgb
        dp_ref[:, O_Z:NP] = _nt(dlgb, wg_ref[...]).astype(_BF)
        dh = _nn(dp_ref[...], w_ref[...])
        x = jnp.where(is_ctx, c_ref[...], x_ref[...])
        r = lax.rsqrt(jnp.mean(x * x, axis=-1, keepdims=True) + EPS)
        xh = x * r
        sdh = jnp.sum(dh, axis=0, keepdims=True)
        sdx = jnp.sum(dh * xh, axis=0, keepdims=True)
        sums_ref[0:1, :] += sdh * lat
        sums_ref[1:2, :] += sdx * lat
        sums_ref[2:3, :] += sdh * (1.0 - lat)
        sums_ref[3:4, :] += sdx * (1.0 - lat)
        dxh = dh * jnp.where(is_ctx, gmc_ref[...], gml_ref[...])
        gx_ref[...] = dx1_ref[...] + r * (dxh - xh * jnp.mean(dxh * xh, axis=-1, keepdims=True))

    vec = _full((1, D))
    tab = _rows(TE, 128)
    return pl.pallas_call(
        body, name="inproj_bwd", grid=(E // TE,),
        in_specs=[_rows_lat(TE, D), _full((CTX, D)), vec, vec, _full((NP, D)), _full((128, 512)), tab, tab, tab,
                  _rows(TE, 512),
                  _rows_lat(TE, QP), pl.BlockSpec((KP, TE), lambda i: (0, i)), pl.BlockSpec((KP, TE), lambda i: (0, i)),
                  _rows(TE, 256), _rows(TE, 256), _rows(TE, 256), _rows(TE, 256), _rows(TE, 512), _rows(TE, 512),
                  _rows_lat(TE, 512), _rows(TE, 256), _rows(TE, 256), _rows_lat(TE, D)],
        out_specs=[_rows(TE, NP), _rows(TE, 512), _rows_lat(TE, D), _full((8, D)), _full((8, 512))],
        out_shape=[jax.ShapeDtypeStruct((E, NP), _BF), jax.ShapeDtypeStruct((E, 512), _BF),
                   jax.ShapeDtypeStruct((S, D), F32), jax.ShapeDtypeStruct((8, D), F32),
                   jax.ShapeDtypeStruct((8, 512), F32)],
        compiler_params=_cp(("arbitrary",), VMEM_BIG),
    )(x, ctx, gml, gmc, win, wg, cos, sa, sb, la, dq, dk, dv, dgq_f, dgq_b, dgk_f, dgk_b, dgv_f, dgv_b,
      dgg, dla_f, dla_b, dx1)


def _matmul_tn(a, b, tk, tt, name, out_dtype, transpose_out=False, shards=()):
    T, KA = a.shape
    N = b.shape[1]
    nt = T // tt

    def body(a_ref, b_ref, o_ref, acc):
        t = pl.program_id(1)

        @pl.when(t == 0)
        def _():
            acc[...] = jnp.zeros_like(acc)

        acc[...] += _tn(a_ref[...], b_ref[...])

        @pl.when(t == nt - 1)
        def _():
            o_ref[...] = (acc[...].T if transpose_out else acc[...]).astype(out_dtype)

    if transpose_out:
        out_spec, out_shape = pl.BlockSpec((N, tk), lambda i, t: (0, i)), (N, KA)
    else:
        out_spec, out_shape = pl.BlockSpec((tk, N), lambda i, t: (i, 0)), (KA, N)
    res = _hosted_call(
        body, (a, b), shards, True, name=name, grid=(KA // tk, nt),
        in_specs=[pl.BlockSpec((tt, tk), lambda i, t: (t, i)), pl.BlockSpec((tt, N), lambda i, t: (t, 0))],
        out_specs=[out_spec], out_shape=[jax.ShapeDtypeStruct(out_shape, out_dtype)],
        scratch_shapes=[pltpu.VMEM((tk, N), F32)],
        compiler_params=_cp(("arbitrary", "arbitrary"), VMEM_BIG))
    return res if shards else res[0]


def _ada_bwd(c_all, c_ctx, w_ada, d_all):
    n = w_ada.shape[1]

    def body(c_ref, cc_ref, w_ref, d_ref, gw_ref, t_ref):
        c = jnp.concatenate([c_ref[...], jnp.broadcast_to(cc_ref[...], (8, D))], axis=0)
        db = d_ref[...].astype(_BF)
        gw_ref[0] = _tn((c * _sigmoid(c)).astype(_BF), db)
        t_ref[...] = _nt(db[8:16], w_ref[...].astype(_BF))

    return pl.pallas_call(
        body, name="ada_bwd", in_specs=[_full((8, D)), _full((1, D)), _full((D, n)), _full((16, n))],
        out_specs=[_full((1, D, n)), _full((8, D))],
        out_shape=[jax.ShapeDtypeStruct((1, D, n), F32), jax.ShapeDtypeStruct((8, D), F32)], grid=(1,),
        compiler_params=_cp(("arbitrary",)),
    )(c_all, c_ctx, w_ada, d_all)


PART_ROWS = 56
R_ADA, R_ADA_C, R_GAIN, R_SINK, R_BG, R_GGLA, R_LOSS, R_WG = 0, 6, 12, 16, 17, 18, 19, 24


def _small_grads(s_in, s_ffn, s_mix, ada_l, ada_c, gains, dsink, s_bg, g_wg, loss):
    def body(si, sf, sm, al, ac, g, ds, sbg, gwg, loss_ref, o_ref):
        o_ref[...] = jnp.zeros_like(o_ref)
        o_ref[R_LOSS:R_LOSS + 1, 0:128] = loss_ref[0:1, :]
        sub = lax.broadcasted_iota(jnp.int32, (8, 128), 0)
        lane = lax.broadcasted_iota(jnp.int32, (8, 128), 1)
        o_ref[R_SINK:R_SINK + 1, 0:128] = jnp.sum(jnp.where(sub == lane, ds[...], 0.0), axis=0, keepdims=True)
        o_ref[R_BG:R_BG + 1, 0:512] = sbg[0:1, :]
        y = sm[1:2, 0:128] + sm[1:2, 128:256] + sm[1:2, 256:384] + sm[1:2, 384:512]
        y = y + pltpu.roll(y, 64, 1)
        o_ref[R_GGLA:R_GGLA + 1, 0:128] = jnp.where(lane[0:1] < DV, y, 0.0)
        o_ref[R_WG:R_WG + 16, 0:256] = gwg[0:16, 0:256]
        o_ref[R_WG + 16:R_WG + 32, 0:256] = gwg[16:32, 256:512]
        sdh_l, sdx_l, sdh_c, sdx_c = si[0:1], si[1:2], si[2:3], si[3:4]
        sdh2, sdx2, a2 = sf[0:1], sf[1:2], sf[2:3]
        a1 = sm[0:1]
        g1, g2, g3, g4 = g[0:1], g[1:2], g[2:3], g[3:4]
        sc1, gt1, sc2, gt2 = al[1:2], al[2:3], al[4:5], al[5:6]
        sc1c = ac[1:2]
        z = jnp.zeros((1, D), F32)
        rows = [sdh_l, sdx_l * g1, a1 * g2, sdh2, sdx2 * g3, a2 * g4,
                sdh_c, sdx_c * g1, z, z, z, z,
                sdx_l * (1.0 + sc1) + sdx_c * (1.0 + sc1c), a1 * gt1, sdx2 * (1.0 + sc2), a2 * gt2]
        for r, v in enumerate(rows):
            o_ref[r:r + 1, :] = v

    v8 = _full((8, D))
    return pl.pallas_call(
        body, name="small_grads",
        in_specs=[v8] * 6 + [_full((8, 128)), _full((8, 512)), _full((128, 512)), _full((8, 128))],
        out_specs=_full((PART_ROWS, D)), out_shape=jax.ShapeDtypeStruct((PART_ROWS, D), F32), grid=(1,),
        compiler_params=_cp(("arbitrary",)),
    )(s_in, s_ffn, s_mix, ada_l, ada_c, gains, dsink, s_bg, g_wg, loss)


def _row_tile(R):
    for cand in (256, 128, 64, 32, 16):
        if R % cand == 0 and R > cand:
            return cand
    return R


def _adamw(w, g, m, v, name):
    _, R, C = w.shape
    tr = _row_tile(R)
    ns = g.shape[0]
    c1 = 1.0 / (1.0 - ADAM_B1 ** ADAM_STEP)
    c2 = 1.0 / (1.0 - ADAM_B2 ** ADAM_STEP)

    def body(w_ref, g_ref, m_ref, v_ref, go_ref, d_ref, nm_ref, nv_ref):
        gg = g_ref[0].astype(F32)
        for j in range(1, ns):
            gg = gg + g_ref[j].astype(F32)
        go_ref[0] = gg
        nm = ADAM_B1 * m_ref[0] + (1.0 - ADAM_B1) * gg
        nv = ADAM_B2 * v_ref[0] + (1.0 - ADAM_B2) * (gg * gg)
        nm_ref[0] = nm
        nv_ref[0] = nv
        d_ref[0] = -ADAM_LR * ((nm * c1) / (jnp.sqrt(nv * c2) + ADAM_EPS) + ADAM_WD * w_ref[0])

    spec = pl.BlockSpec((1, tr, C), lambda i: (0, i, 0))
    sds = jax.ShapeDtypeStruct((1, R, C), F32)
    return pl.pallas_call(
        body, name=name, grid=(R // tr,),
        in_specs=[spec, pl.BlockSpec((ns, tr, C), lambda i: (0, i, 0)), spec, spec],
        out_specs=[spec] * 4, out_shape=[sds] * 4, compiler_params=_cp(("parallel",), 48 * 1024 * 1024),
    )(w, g, m, v)


def _sum_slots(slots, name):
    _, R, C = slots.shape
    tr = _row_tile(R)

    def body(s_ref, o_ref):
        acc = s_ref[0].astype(F32)
        for j in range(1, N_DEV):
            acc = acc + s_ref[j].astype(F32)
        o_ref[...] = acc

    return pl.pallas_call(
        body, name=name, grid=(R // tr,), in_specs=[pl.BlockSpec((N_DEV, tr, C), lambda i: (0, i, 0))],
        out_specs=_rows(tr, C), out_shape=jax.ShapeDtypeStruct((R, C), F32), compiler_params=_cp(("parallel",)),
    )(slots)


def _ag2_start(x_ref, out_ref, send_sems, recv_sems, local_sem):
    x, y, c = lax.axis_index("x"), lax.axis_index("y"), lax.axis_index("c")
    me, sibling = (x, y, c), (x, y, 1 - c)
    chips = [(1 - x, y), (x, 1 - y), (1 - x, 1 - y)]

    def rows(px, py, pc):
        return out_ref.at[4 * px + 2 * py + pc]

    def copy(k, block, to, src=None):
        return pltpu.make_async_remote_copy(
            src_ref=rows(*block) if src is None else src, dst_ref=rows(*block),
            send_sem=send_sems.at[k], recv_sem=recv_sems.at[k], device_id=to, device_id_type=MESH)

    mine = pltpu.make_async_copy(x_ref, rows(*me), local_sem)
    mine.start()
    first = [copy(0, me, sibling, src=x_ref)]
    first += [copy(1 + j, me, (*chip, c), src=x_ref) for j, chip in enumerate(chips)]
    for cp in first:
        cp.start()
    return copy, mine, first, me, sibling, chips, c


def _ag2_finish(state):
    copy, mine, first, me, sibling, chips, c = state
    passed = [copy(4 + j, (*chip, c), sibling) for j, chip in enumerate(chips)]
    for j, chip in enumerate(chips):
        copy(1 + j, (*chip, c), me).wait_recv()
        passed[j].start()
    copy(0, sibling, me).wait_recv()
    for j, chip in enumerate(chips):
        copy(4 + j, (*chip, 1 - c), me).wait_recv()
    for cp in first + passed:
        cp.wait_send()
    mine.wait()


def _allgather(x_shard, name):
    m_per, n = x_shard.shape

    def body(x_ref, out_ref, send_sems, recv_sems, local_sem):
        _ag2_finish(_ag2_start(x_ref, out_ref, send_sems, recv_sems, local_sem))

    return pl.pallas_call(
        body, name=name, out_shape=jax.ShapeDtypeStruct((N_DEV, m_per, n), x_shard.dtype),
        in_specs=[pl.BlockSpec(memory_space=pltpu.VMEM)], out_specs=pl.BlockSpec(memory_space=pltpu.VMEM),
        scratch_shapes=[pltpu.SemaphoreType.DMA((7,)), pltpu.SemaphoreType.DMA((7,)), pltpu.SemaphoreType.DMA],
        compiler_params=pltpu.CompilerParams(vmem_limit_bytes=VMEM_BIG),
    )(x_shard)


def _entry(c, wg_sh, win_sh, c_ctx, w_ada):
    n = w_ada.shape[1]

    def body(c_ref, g_ref, w_ref, cc_ref, wa_ref, call_ref, gall_ref, wall_ref, ada_ref, part,
             s_send, s_recv, s_loc, w_send, w_recv, w_loc, a_send, a_recv, a_loc):
        big = _ag2_start(w_ref, wall_ref, w_send, w_recv, w_loc)
        small = _xchg_copies([c_ref, g_ref], [call_ref, gall_ref], s_send, s_recv, s_loc, gather=True)
        _xchg_start(small)
        _xchg_finish(small)
        cs = jnp.concatenate([call_ref[:, 0, :], jnp.broadcast_to(cc_ref[...], (8, D))], axis=0)
        part[...] = _nn((cs * _sigmoid(cs)).astype(_BF), wa_ref[...].astype(_BF))
        ada = _xchg_copies([part], [ada_ref], a_send, a_recv, a_loc, gather=True)
        _xchg_start(ada)
        _xchg_finish(ada)
        _ag2_finish(big)

    vm = pl.BlockSpec(memory_space=pltpu.VMEM)
    return pl.pallas_call(
        body, name="entry_gather",
        out_shape=[jax.ShapeDtypeStruct((N_DEV,) + c.shape, F32), jax.ShapeDtypeStruct((N_DEV,) + wg_sh.shape, F32),
                   jax.ShapeDtypeStruct((N_DEV,) + win_sh.shape, win_sh.dtype),
                   jax.ShapeDtypeStruct((N_DEV, 16, n), F32)],
        in_specs=[vm] * 5, out_specs=[vm] * 4,
        scratch_shapes=[pltpu.VMEM((16, n), F32)] + _xchg_scratch(2)
        + [pltpu.SemaphoreType.DMA((7,)), pltpu.SemaphoreType.DMA((7,)), pltpu.SemaphoreType.DMA] + _xchg_scratch(1),
        compiler_params=pltpu.CompilerParams(vmem_limit_bytes=VMEM_BIG),
    )(c, wg_sh, win_sh, c_ctx, w_ada)


def _alltoall(slabs, name):
    na = len(slabs)

    def body(*refs):
        ins, outs = refs[:na], refs[na:2 * na]
        send_sems, recv_sems, local_sems = refs[2 * na:]
        x, y, c = lax.axis_index("x"), lax.axis_index("y"), lax.axis_index("c")
        me = 4 * x + 2 * y + c
        copies = []
        for a in range(na):
            loc = pltpu.make_async_copy(ins[a].at[me], outs[a].at[me], local_sems.at[a])
            loc.start()
            copies.append(loc)
        rem = []
        for k in range(1, N_DEV):
            px, py, pc = x ^ (k >> 2), y ^ ((k >> 1) & 1), c ^ (k & 1)
            peer = 4 * px + 2 * py + pc
            for a in range(na):
                cp = pltpu.make_async_remote_copy(
                    src_ref=ins[a].at[peer], dst_ref=outs[a].at[me],
                    send_sem=send_sems.at[a, k - 1], recv_sem=recv_sems.at[a, k - 1],
                    device_id=(px, py, pc), device_id_type=MESH)
                cp.start()
                rem.append((a, k, peer, cp))
        for a, k, peer, cp in rem:
            pltpu.make_async_remote_copy(
                src_ref=ins[a].at[me], dst_ref=outs[a].at[peer],
                send_sem=send_sems.at[a, k - 1], recv_sem=recv_sems.at[a, k - 1],
                device_id=(x, y, c), device_id_type=MESH).wait_recv()
        for a, k, peer, cp in rem:
            cp.wait_send()
        for loc in copies:
            loc.wait()

    anyspec = pl.BlockSpec(memory_space=pl.ANY)
    return pl.pallas_call(
        body, name=name, out_shape=[jax.ShapeDtypeStruct(s.shape, s.dtype) for s in slabs],
        in_specs=[anyspec] * na, out_specs=[anyspec] * na,
        scratch_shapes=[pltpu.SemaphoreType.DMA((na, 7)), pltpu.SemaphoreType.DMA((na, 7)),
                        pltpu.SemaphoreType.DMA((na,))],
    )(*slabs)


def _rope_tables(S):
    t = np.arange(S)
    row = (t // GRID_W).astype(np.float32)
    colp = (t % GRID_W).astype(np.float32)
    half = HD // 2
    inv = (ROPE_BASE ** (-np.arange(0, half, 2, dtype=np.float32) / half)).astype(np.float32)
    ar = row[:, None] * inv[None, :]
    ac = colp[:, None] * inv[None, :]
    ang = np.concatenate([ar, ar, ac, ac], axis=-1).astype(np.float32)
    cos = np.cos(ang).astype(np.float32)
    sin = np.sin(ang).astype(np.float32)
    lane = np.arange(HD)
    first = (lane % 32) < 16
    sa = np.where(first[None, :], -sin, 0.0)
    sb = np.where(first[None, :], 0.0, sin)

    def ext(tab, ctx_val):
        full = np.zeros((CTX + S, 128), np.float32)
        full[:CTX, :HD] = ctx_val
        full[CTX:, :HD] = tab
        return jnp.asarray(full)

    return ext(cos, 1.0), ext(sa, 0.0), ext(sb, 0.0)


def _pad_rows_win(wt):
    q, k, v, rest, z = wt[0:512], wt[512:640], wt[640:768], wt[768:2304], wt[2304:2336]

    def padh(t, nh):
        t = t.reshape(nh, HD, D)
        return jnp.pad(t, ((0, 0), (0, 128 - HD), (0, 0))).reshape(nh * 128, D)

    return jnp.concatenate([padh(q, N_ATT), padh(k, N_KV), padh(v, N_KV), rest, jnp.pad(z, ((0, 96), (0, 0)))], axis=0)


def _unpad_rows_win(g):
    def unp(t, nh):
        return t.reshape(nh, 128, D)[:, :HD].reshape(nh * HD, D)

    return jnp.concatenate([unp(g[O_Q:O_K], N_ATT), unp(g[O_K:O_V], N_KV), unp(g[O_V:O_GQ], N_KV),
                            g[O_GQ:O_Z], g[O_Z:O_Z + 32]], axis=0)


def _local_step(x, ctx, target, ada_l, ada_c, gains, sink, win_p, wg_bd, bg, ggla, wout_sh, wffi_sh, wffo_sh):
    S = x.shape[0]
    cos, sa, sb = _rope_tables(S)
    g1, g2, g3, g4 = (gains[i:i + 1] for i in range(4))
    sh1, sc1, gt1, sh2, sc2, gt2 = (ada_l[i:i + 1] for i in range(6))
    sh1c, sc1c = ada_c[0:1], ada_c[1:2]
    gml, gmc, gm2 = g1 * (1.0 + sc1), g1 * (1.0 + sc1c), g3 * (1.0 + sc2)
    mavg = jnp.asarray(np.kron(np.eye(N_GLA, dtype=np.float32), np.full((DV, DV), 1.0 / DV, np.float32))).astype(_BF)

    n_ffi, r_ffo, r_out = wffi_sh.shape[0], wffo_sh.shape[0], wout_sh.shape[0]
    tt_e = 768 if (S + CTX) % 768 == 0 else 256
    tt_s = 512 if S % 512 == 0 else 256
    h, q, k, v, gq, gk, gv, gg, z, la, wout_g = _inproj_fwd(x, ctx, gml, sh1, gmc, sh1c, win_p, wg_bd, bg,
                                                            cos, sa, sb, [wout_sh])
    attn, lse, wffi_g = _attn_fwd(q, k, v, sink, [wffi_sh])
    o_f, st_f, wffo_g = _gla_fwd(gq, gk, gv, la, False, [wffo_sh])
    o_b, st_b = _gla_fwd(gq, gk, gv, la, True)
    wout = wout_g.reshape(N_DEV * r_out, D)
    wffi = wffi_g.reshape(N_DEV * n_ffi, D)
    wffo = wffo_g.reshape(N_DEV * r_ffo, D)
    x1, mix = _mix_fwd(x, attn, o_f, o_b, gg, ggla, mavg, wout, gt1, g2)
    dx1, h2, du, act, df, s_ffn, loss = _ffn(x1, target, gm2, sh2, gt2, g4, wffi, wffo)
    slab_ffi = _matmul_tn(h2, du, 512, tt_s, "grad_w_ffn_in", _BF, True).reshape(N_DEV, n_ffi, D)
    slab_ffo = _matmul_tn(act, df, FFN, tt_s, "grad_w_ffn_out", _BF).reshape(N_DEV, r_ffo, D)
    d_attn, do_gla, dgg, dy, s_mix = _mix_bwd(dx1, mix, o_f, o_b, gg, ggla, mavg, wout, gt1, g2)
    slab_out = _matmul_tn(mix, dy, D, tt_s, "grad_w_out", _BF).reshape(N_DEV, r_out, D)
    dq, dk, dv, dsink, got_ffi, got_ffo = _attn_bwd(q, k, v, sink, lse, d_attn, [slab_ffi, slab_ffo])
    dgq_f, dgk_f, dgv_f, dla_f, got_out = _gla_bwd(gq, gk, gv, la, st_f, do_gla, False, [slab_out])
    dgq_b, dgk_b, dgv_b, dla_b = _gla_bwd(gq, gk, gv, la, st_b, do_gla, True)
    dp, dlg, grad_x, s_in, s_bg = _inproj_bwd(x, ctx, gml, gmc, win_p, wg_bd, cos, sa, sb, la, dq, dk, dv,
                                              dgq_f, dgq_b, dgk_f, dgk_b, dgv_f, dgv_b, dgg, dla_f, dla_b, dx1)
    g_wg = _matmul_tn(z, dlg, 128, tt_e, "grad_w_gate", F32)
    small = _small_grads(s_in, s_ffn, s_mix, ada_l, ada_c, gains, dsink, s_bg, g_wg, loss)
    g_win_t, parts = _matmul_tn(h, dp, 512, tt_e, "grad_w_in", _BF, True, [small])
    return dict(grad_x=grad_x, g_win_t=g_win_t, got_out=got_out, got_ffi=got_ffi, got_ffo=got_ffo, parts=parts)


SMALL_NAMES = ["c_ctx", "b_ada", "g_pre_mix", "g_post_mix", "g_pre_ffn", "g_post_ffn", "attn_sink",
               "b_gate_fwd", "b_gate_bwd", "g_gla_norm", "w_gate_fwd", "w_gate_bwd"]


def _small_update(tot, t_tot, wg_g, w, m, v):
    c1 = 1.0 / (1.0 - ADAM_B1 ** ADAM_STEP)
    c2 = 1.0 / (1.0 - ADAM_B2 ** ADAM_STEP)
    n = len(SMALL_NAMES)

    def body(tot_ref, t_ref, wg_ref, *refs):
        w_r, m_r, v_r = refs[0:n], refs[n:2 * n], refs[2 * n:3 * n]
        g_o, d_o, nm_o, nv_o = refs[3 * n:4 * n], refs[4 * n:5 * n], refs[5 * n:6 * n], refs[6 * n:7 * n]

        def upd(i, idx, g):
            nm = ADAM_B1 * m_r[i][idx] + (1.0 - ADAM_B1) * g
            nv = ADAM_B2 * v_r[i][idx] + (1.0 - ADAM_B2) * (g * g)
            g_o[i][idx] = g
            nm_o[i][idx] = nm
            nv_o[i][idx] = nv
            d_o[i][idx] = -ADAM_LR * ((nm * c1) / (jnp.sqrt(nv * c2) + ADAM_EPS) + ADAM_WD * w_r[i][idx])

        everything = (slice(None), slice(None))
        cc = w_r[0][...]
        sc = _sigmoid(cc)
        upd(0, everything, t_ref[0:1, :] * (sc * (1.0 + cc * (1.0 - sc))))
        for j in range(6):
            upd(1, (slice(None), slice(D * j, D * j + D)),
                tot_ref[R_ADA + j:R_ADA + j + 1, :] + tot_ref[R_ADA_C + j:R_ADA_C + j + 1, :])
        for j in range(4):
            upd(2 + j, everything, tot_ref[R_GAIN + j:R_GAIN + j + 1, :])
        upd(6, everything, tot_ref[R_SINK:R_SINK + 1, 0:N_ATT])
        upd(7, everything, tot_ref[R_BG:R_BG + 1, 0:256])
        upd(8, everything, tot_ref[R_BG:R_BG + 1, 256:512])
        upd(9, everything, tot_ref[R_GGLA:R_GGLA + 1, 0:DV])
        upd(10, (0,), wg_ref[0:GATE_RANK, :])
        upd(11, (0,), wg_ref[GATE_RANK:2 * GATE_RANK, :])

    params = [w[k] for k in SMALL_NAMES] + [m[k] for k in SMALL_NAMES] + [v[k] for k in SMALL_NAMES]
    outs = pl.pallas_call(
        body, name="small_update", grid=(1,),
        in_specs=[_full(tot.shape), _full(t_tot.shape), _full(wg_g.shape)] + [_full(p.shape) for p in params],
        out_specs=[_full(w[k].shape) for k in SMALL_NAMES] * 4,
        out_shape=[jax.ShapeDtypeStruct(w[k].shape, F32) for k in SMALL_NAMES] * 4,
        compiler_params=_cp(("arbitrary",)),
    )(tot, t_tot, wg_g, *params)
    return tuple(dict(zip(SMALL_NAMES, outs[i * n:(i + 1) * n])) for i in range(4))


def kernel(x, c, ctx, c_ctx, w_ada, b_ada, g_pre_mix, g_post_mix, g_pre_ffn, g_post_ffn, w_in, attn_sink, w_gate_fwd, b_gate_fwd, w_gate_bwd, b_gate_bwd, g_gla_norm, w_out, w_ffn_in, w_ffn_out, loss_target, m_c_ctx, m_w_ada, m_b_ada, m_g_pre_mix, m_g_post_mix, m_g_pre_ffn, m_g_post_ffn, m_w_in, m_attn_sink, m_w_gate_fwd, m_b_gate_fwd, m_w_gate_bwd, m_b_gate_bwd, m_g_gla_norm, m_w_out, m_w_ffn_in, m_w_ffn_out, v_c_ctx, v_w_ada, v_b_ada, v_g_pre_mix, v_g_post_mix, v_g_pre_ffn, v_g_post_ffn, v_w_in, v_attn_sink, v_w_gate_fwd, v_b_gate_fwd, v_w_gate_bwd, v_b_gate_bwd, v_g_gla_norm, v_w_out, v_w_ffn_in, v_w_ffn_out):
    me = 4 * lax.axis_index("x") + 2 * lax.axis_index("y") + lax.axis_index("c")
    S = x.shape[1]
    n_in = w_in.shape[2]
    n_ffi = w_ffn_in.shape[2]
    r_out = w_out.shape[1]
    r_ffo = w_ffn_out.shape[1]
    n_ada = w_ada.shape[2]

    wg_sh = jnp.concatenate([w_gate_fwd.reshape(4, 128), w_gate_bwd.reshape(4, 128)], axis=0)
    c_all3, g_all, w_all, ada_all = _entry(c, wg_sh, w_in[0].T.astype(_BF), c_ctx.reshape(1, D), w_ada[0])
    c_all = c_all3.reshape(N_DEV, D)
    wgf = g_all[:, 0:4].reshape(N_DEV, GATE_RANK, 32).transpose(1, 0, 2).reshape(GATE_RANK, 256)
    wgb = g_all[:, 4:8].reshape(N_DEV, GATE_RANK, 32).transpose(1, 0, 2).reshape(GATE_RANK, 256)
    win_p = _pad_rows_win(w_all.reshape(N_DEV * n_in, D))
    wg_bd = jnp.zeros((128, 512), F32).at[0:16, 0:256].set(wgf).at[16:32, 256:512].set(wgb).astype(_BF)
    ada_full = ada_all.transpose(1, 0, 2).reshape(16, N_DEV * n_ada) + b_ada
    ada_l = jnp.pad(lax.dynamic_slice_in_dim(ada_full, me, 1, 0).reshape(6, D), ((0, 2), (0, 0)))
    ada_c = jnp.pad(ada_full[8].reshape(6, D), ((0, 2), (0, 0)))
    gains = jnp.pad(jnp.concatenate([g_pre_mix, g_post_mix, g_pre_ffn, g_post_ffn], axis=0), ((0, 4), (0, 0)))
    sink = jnp.broadcast_to(attn_sink.reshape(8, 1), (8, 128))
    bg = jnp.concatenate([b_gate_fwd, b_gate_bwd], axis=1)
    ggla = jnp.tile(g_gla_norm, (1, N_GLA))

    r = _local_step(x[0], ctx[0], loss_target[0], ada_l, ada_c, gains, sink, win_p, wg_bd, bg, ggla,
                    w_out[0].astype(_BF), w_ffn_in[0].T.astype(_BF), w_ffn_out[0].astype(_BF))

    slots = _alltoall([_unpad_rows_win(r["g_win_t"]).reshape(N_DEV, n_in, D)], "scatter_grads")

    parts = r["parts"]
    tot = _sum_slots(parts, "sum_small_grads")
    loss = tot[R_LOSS, 0]
    d_ada_rows = parts[:, R_ADA:R_ADA + 6].reshape(N_DEV, 6 * D)
    d_ada_c = tot[R_ADA_C:R_ADA_C + 6].reshape(1, 6 * D)
    my_cols = lax.dynamic_slice_in_dim(jnp.concatenate([d_ada_rows, jnp.broadcast_to(d_ada_c, (1, 6 * D)),
                                                        jnp.zeros((7, 6 * D), F32)], axis=0), me * n_ada, n_ada, 1)
    grad_w_ada, t_part = _ada_bwd(c_all, c_ctx.reshape(1, D), w_ada[0], my_cols)
    t_all = _allgather(t_part, "gather_c_ctx")
    t_tot = _sum_slots(t_all, "sum_c_ctx")
    wg_g = lax.dynamic_slice(tot, (R_WG, me * 32), (2 * GATE_RANK, 32))

    w_small = dict(c_ctx=c_ctx.reshape(1, D), b_ada=b_ada, g_pre_mix=g_pre_mix, g_post_mix=g_post_mix, g_pre_ffn=g_pre_ffn,
                   g_post_ffn=g_post_ffn, attn_sink=attn_sink, b_gate_fwd=b_gate_fwd, b_gate_bwd=b_gate_bwd,
                   g_gla_norm=g_gla_norm, w_gate_fwd=w_gate_fwd, w_gate_bwd=w_gate_bwd)
    m_small = dict(c_ctx=m_c_ctx.reshape(1, D), b_ada=m_b_ada, g_pre_mix=m_g_pre_mix, g_post_mix=m_g_post_mix,
                   g_pre_ffn=m_g_pre_ffn, g_post_ffn=m_g_post_ffn, attn_sink=m_attn_sink, b_gate_fwd=m_b_gate_fwd,
                   b_gate_bwd=m_b_gate_bwd, g_gla_norm=m_g_gla_norm, w_gate_fwd=m_w_gate_fwd, w_gate_bwd=m_w_gate_bwd)
    v_small = dict(c_ctx=v_c_ctx.reshape(1, D), b_ada=v_b_ada, g_pre_mix=v_g_pre_mix, g_post_mix=v_g_post_mix,
                   g_pre_ffn=v_g_pre_ffn, g_post_ffn=v_g_post_ffn, attn_sink=v_attn_sink, b_gate_fwd=v_b_gate_fwd,
                   b_gate_bwd=v_b_gate_bwd, g_gla_norm=v_g_gla_norm, w_gate_fwd=v_w_gate_fwd, w_gate_bwd=v_w_gate_bwd)
    grads_small, d_s, nm_s, nv_s = _small_update(tot, t_tot, wg_g, w_small, m_small, v_small)
    for dd in (grads_small, d_s, nm_s, nv_s):
        dd["c_ctx"] = dd["c_ctx"].reshape(D)

    big = {}
    for nm, w, g, m, v in [("w_ada", w_ada, grad_w_ada, m_w_ada, v_w_ada),
                           ("w_out", w_out, r["got_out"], m_w_out, v_w_out),
                           ("w_ffn_out", w_ffn_out, r["got_ffo"], m_w_ffn_out, v_w_ffn_out)]:
        big[nm] = _adamw(w, g, m, v, "adamw_" + nm)
    tr = lambda a: jnp.transpose(a, (0, 2, 1))
    for nm, w, g, m, v in [("w_in", w_in, slots[0], m_w_in, v_w_in),
                           ("w_ffn_in", w_ffn_in, r["got_ffi"], m_w_ffn_in, v_w_ffn_in)]:
        big[nm] = tuple(tr(o) for o in _adamw(tr(w), g, tr(m), tr(v), "adamw_" + nm))

    order = ["c_ctx", "w_ada", "b_ada", "g_pre_mix", "g_post_mix", "g_pre_ffn", "g_post_ffn", "w_in", "attn_sink",
             "w_gate_fwd", "b_gate_fwd", "w_gate_bwd", "b_gate_bwd", "g_gla_norm", "w_out", "w_ffn_in", "w_ffn_out"]
    grads, deltas, new_m, new_v = [], [], [], []
    for nm in order:
        if nm in big:
            g_, d_, m_, v_ = big[nm]
        else:
            g_, d_, m_, v_ = grads_small[nm], d_s[nm], nm_s[nm], nv_s[nm]
        grads.append(g_)
        deltas.append(d_)
        new_m.append(m_)
        new_v.append(v_)
    return (loss, r["grad_x"][None], *grads, *deltas, *new_m, *new_v)
```

```python
import functools
import math

import numpy as np
import jax
import jax.numpy as jnp
from jax import lax
from jax.experimental import pallas as pl
from jax.experimental.pallas import tpu as pltpu

F32 = jnp.float32
_BF = jnp.bfloat16

N_DEV = 8
D = 1024
CTX = 256
HD = 64
N_ATT = 8
N_KV = 2
GRP = N_ATT // N_KV
WIN = 128
GRID_W = 64
ROPE_BASE = 10000.0
N_GLA = 8
DK = 32
DV = 64
GATE_RANK = 16
GATE_TAU = 16.0
FFN = 2816
EPS = 1e-6
NEG = -1e30
GLA_T = 128

QP = N_ATT * HD
KP = N_KV * HD
O_Q, O_K, O_V = 0, QP, QP + KP
O_GQ = O_V + KP
O_GK = O_GQ + N_GLA * DK
O_GV = O_GK + N_GLA * DK
O_GG = O_GV + N_GLA * DV
O_Z = O_GG + N_GLA * DV
NP = O_Z + 128
IN_COLS = 2336

ADAM_LR, ADAM_B1, ADAM_B2, ADAM_EPS, ADAM_WD, ADAM_STEP = 0.001, 0.9, 0.999, 1e-08, 0.01, 10

VMEM_BIG = 56 * 1024 * 1024
MESH = pl.DeviceIdType.MESH


def _cp(sem, vmem=None):
    return pltpu.CompilerParams(dimension_semantics=sem, vmem_limit_bytes=vmem)


def _full(shape):
    nd = len(shape)
    return pl.BlockSpec(shape, lambda *a: (0,) * nd)


def _rows(tile, width, off=0):
    return pl.BlockSpec((tile, width), lambda i: (i + off, 0))


def _rows_lat(tile, width):
    return pl.BlockSpec((tile, width), lambda i: (jnp.maximum(i - 1, 0), 0))


def _nt(a, b):
    return lax.dot_general(a, b, (((1,), (1,)), ((), ())), preferred_element_type=F32)


def _tn(a, b):
    return lax.dot_general(a, b, (((0,), (0,)), ((), ())), preferred_element_type=F32)


def _nn(a, b):
    return jnp.dot(a, b, preferred_element_type=F32)


def _head_mean(x, mavg):
    n = x.shape[0]
    hi = x.astype(_BF)
    lo = (x - hi.astype(F32)).astype(_BF)
    y = _nn(jnp.concatenate([hi, lo], axis=0), mavg)
    return y[0:n] + y[n:2 * n]


def _rope(t, cos, sa, sb):
    n = t.shape[1]
    reps = n // 128
    c = jnp.tile(cos, (1, reps))
    a = jnp.tile(sa, (1, reps))
    b = jnp.tile(sb, (1, reps))
    return t * c + pltpu.roll(t, n - 16, 1) * a + pltpu.roll(t, 16, 1) * b


def _unrope(t, cos, sa, sb):
    n = t.shape[1]
    reps = n // 128
    c = jnp.tile(cos, (1, reps))
    a = jnp.tile(sa, (1, reps))
    b = jnp.tile(sb, (1, reps))
    return t * c + pltpu.roll(t * a, 16, 1) + pltpu.roll(t * b, n - 16, 1)


def _sigmoid(x):
    return 1.0 / (1.0 + jnp.exp(-x))


def _inproj_fwd(x, ctx, gml, shl, gmc, shc, win, wg, bg, cos, sa, sb, shards):
    E = x.shape[0] + CTX
    TE = CTX

    def body(x_ref, c_ref, gml_ref, shl_ref, gmc_ref, shc_ref, w_ref, wg_ref, bg_ref, cos_ref, sa_ref, sb_ref,
             h_ref, q_ref, k_ref, v_ref, gq_ref, gk_ref, gv_ref, gg_ref, z_ref, la_ref):
        is_ctx = pl.program_id(0) == 0
        gm = jnp.where(is_ctx, gmc_ref[...], gml_ref[...])
        sh = jnp.where(is_ctx, shc_ref[...], shl_ref[...])
        x = jnp.where(is_ctx, c_ref[...], x_ref[...])
        r = lax.rsqrt(jnp.mean(x * x, axis=-1, keepdims=True) + EPS)
        hb = ((x * r) * gm + sh).astype(_BF)
        h_ref[...] = hb
        p = _nt(hb, w_ref[...])
        cos_t, sa_t, sb_t = cos_ref[...], sa_ref[...], sb_ref[...]
        q_ref[...] = (_rope(p[:, O_Q:O_K], cos_t, sa_t, sb_t) * (HD ** -0.5)).astype(_BF)
        k_ref[...] = _rope(p[:, O_K:O_V], cos_t, sa_t, sb_t).astype(_BF)
        v_ref[...] = p[:, O_V:O_GQ].astype(_BF)
        gq_ref[...] = p[:, O_GQ:O_GK] * (DK ** -0.5)
        gk_ref[...] = p[:, O_GK:O_GV]
        gv_ref[...] = p[:, O_GV:O_GG]
        gg_ref[...] = p[:, O_GG:O_Z]
        zb = p[:, O_Z:NP].astype(_BF)
        z_ref[...] = zb
        lg = _nn(zb, wg_ref[...]) + bg_ref[...]
        la_ref[...] = (jnp.minimum(lg, 0.0) - jnp.log(1.0 + jnp.exp(-jnp.abs(lg)))) * (1.0 / GATE_TAU)

    vec = _full((1, D))
    tab = _rows(TE, 128)
    outs = [(D, _BF), (QP, _BF), (KP, _BF), (KP, _BF), (256, F32), (256, F32), (512, F32), (512, F32),
            (128, _BF), (512, F32)]
    return _hosted_call(
        body, (x, ctx, gml, shl, gmc, shc, win, wg, bg, cos, sa, sb), shards, True,
        name="inproj_fwd", grid=(E // TE,),
        in_specs=[_rows_lat(TE, D), _full((CTX, D)), vec, vec, vec, vec, _full((NP, D)), _full((128, 512)),
                  _full((1, 512)), tab, tab, tab],
        out_specs=[_rows(TE, w) for w, _ in outs],
        out_shape=[jax.ShapeDtypeStruct((E, w), dt) for w, dt in outs],
        compiler_params=_cp(("arbitrary",), 40 * 1024 * 1024))


def _xchg_scratch(na):
    return [pltpu.SemaphoreType.DMA((na, N_DEV - 1)), pltpu.SemaphoreType.DMA((na, N_DEV - 1)),
            pltpu.SemaphoreType.DMA((na,))]


def _xchg_copies(ins, outs, send_sems, recv_sems, local_sems, gather):
    x, y, c = lax.axis_index("x"), lax.axis_index("y"), lax.axis_index("c")
    me = 4 * x + 2 * y + c
    local, sends, recvs = [], [], []
    for a in range(len(ins)):
        local.append(pltpu.make_async_copy(ins[a] if gather else ins[a].at[me], outs[a].at[me], local_sems.at[a]))
    for k in range(1, N_DEV):
        px, py, pc = x ^ (k >> 2), y ^ ((k >> 1) & 1), c ^ (k & 1)
        peer = 4 * px + 2 * py + pc
        for a in range(len(ins)):
            sems = dict(send_sem=send_sems.at[a, k - 1], recv_sem=recv_sems.at[a, k - 1], device_id_type=MESH)
            sends.append(pltpu.make_async_remote_copy(
                src_ref=ins[a] if gather else ins[a].at[peer], dst_ref=outs[a].at[me], device_id=(px, py, pc), **sems))
            recvs.append(pltpu.make_async_remote_copy(
                src_ref=ins[a] if gather else ins[a].at[me], dst_ref=outs[a].at[peer], device_id=(x, y, c), **sems))
    return local, sends, recvs


def _xchg_start(cps):
    local, sends, _ = cps
    for cp in local + sends:
        cp.start()


def _xchg_finish(cps):
    local, sends, recvs = cps
    for cp in recvs:
        cp.wait_recv()
    for cp in sends:
        cp.wait_send()
    for cp in local:
        cp.wait()


def _xchg_out_shapes(ins, gather):
    return [jax.ShapeDtypeStruct(((N_DEV,) + s.shape) if gather else s.shape, s.dtype) for s in ins]


def _hosted_call(body, args, hosted, gather, *, grid, in_specs, out_specs, out_shape, scratch_shapes=(), **kw):
    na = len(hosted)
    if na == 0:
        return pl.pallas_call(body, grid=grid, in_specs=in_specs, out_specs=out_specs, out_shape=out_shape,
                              scratch_shapes=list(scratch_shapes), **kw)(*args)
    n_in, n_out, n_scr = len(in_specs), len(out_specs), len(scratch_shapes)

    def wrapped(*refs):
        ins, h_in = refs[:n_in], refs[n_in:n_in + na]
        outs, h_out = refs[n_in + na:n_in + na + n_out], refs[n_in + na + n_out:n_in + 2 * na + n_out]
        scr = refs[n_in + 2 * na + n_out:]
        cps = _xchg_copies(h_in, h_out, *scr[n_scr:], gather=gather)
        pids = [pl.program_id(a) for a in range(len(grid))]
        first = functools.reduce(jnp.logical_and, [p == 0 for p in pids])
        last = functools.reduce(jnp.logical_and, [p == g - 1 for p, g in zip(pids, grid)])

        @pl.when(first)
        def _():
            _xchg_start(cps)

        body(*ins, *outs, *scr[:n_scr])

        @pl.when(last)
        def _():
            _xchg_finish(cps)

    anyspec = pl.BlockSpec(memory_space=pl.ANY)
    return pl.pallas_call(
        wrapped, grid=grid, in_specs=list(in_specs) + [anyspec] * na, out_specs=list(out_specs) + [anyspec] * na,
        out_shape=list(out_shape) + _xchg_out_shapes(hosted, gather),
        scratch_shapes=list(scratch_shapes) + _xchg_scratch(na), **kw)(*args, *hosted)


def _attn_specs(E):
    nb = (E - CTX) // WIN
    last = E // WIN - 1
    kc = pl.BlockSpec((CTX, KP), lambda n: (0, 0))
    kp = pl.BlockSpec((WIN, KP), lambda n: (n + 1, 0))
    kk = pl.BlockSpec((WIN, KP), lambda n: (n + 2, 0))
    kn = pl.BlockSpec((WIN, KP), lambda n: (jnp.minimum(n + 3, last), 0))
    return nb, [kc, kp, kk, kn]


def _attn_bias(nb):
    rows = np.arange(GRP * WIN)[:, None] % WIN
    cols = np.arange(CTX + 3 * WIN)[None, :]
    j = cols - CTX
    band = np.abs(j - WIN - rows) <= WIN
    out = []
    for first, last in ((True, False), (False, False), (False, True)):
        ok = (cols < CTX) | (band & ((j >= WIN) | (not first)) & ((j < 2 * WIN) | (not last)))
        out.append(np.where(ok, 0.0, NEG).astype(np.float32))
    bias = jnp.asarray(np.stack(out))
    spec = pl.BlockSpec((1, GRP * WIN, CTX + 3 * WIN),
                        lambda n: (jnp.where(n == 0, 0, jnp.where(n == nb - 1, 2, 1)), 0, 0))
    return bias, spec


def _both_halves(t, h):
    tf = t.astype(F32)
    r = pltpu.roll(tf, HD, 1)
    lo = lax.broadcasted_iota(jnp.int32, tf.shape, 1) < HD
    return (jnp.where(lo, tf, r) if h == 0 else jnp.where(lo, r, tf)).astype(t.dtype)


def _stack_heads(ref, h):
    lo = lax.broadcasted_iota(jnp.int32, (WIN, 128), 1) < HD
    parts = []
    for g in range(GRP):
        j = GRP * h + g
        t = ref[:, 128 * (j // 2):128 * (j // 2) + 128].astype(F32)
        parts.append(jnp.where(lo if j % 2 == 0 else jnp.logical_not(lo), t, 0.0))
    return jnp.concatenate(parts, axis=0)


def _unstack_pair(o, pp):
    lo = lax.broadcasted_iota(jnp.int32, (WIN, 128), 1) < HD
    return jnp.where(lo, o[WIN * 2 * pp:WIN * 2 * pp + WIN], o[WIN * (2 * pp + 1):WIN * (2 * pp + 1) + WIN])


def _attn_fwd(q, k, v, sink, shards):
    E = q.shape[0]
    S = E - CTX
    nb, kspecs = _attn_specs(E)
    na = len(shards)

    def body(q_ref, kc, kp, kk, kn, vc, vp, vk, vn, sink_ref, bias_ref, *rest):
        shard_refs, (o_ref, lse_ref), got_refs = rest[:na], rest[na:na + 2], rest[na + 2:2 * na + 2]
        n = pl.program_id(0)
        cps = _xchg_copies(shard_refs, got_refs, *rest[2 * na + 2:], gather=True)

        @pl.when(n == 0)
        def _():
            _xchg_start(cps)

        lane = lax.broadcasted_iota(jnp.int32, (WIN, 128), 1)
        lse_t = jnp.zeros((WIN, 128), F32)
        kall = jnp.concatenate([kc[...], kp[...], kk[...], kn[...]], axis=0)
        vall = jnp.concatenate([vc[...], vp[...], vk[...], vn[...]], axis=0)
        K = [_both_halves(kall, h) for h in range(N_KV)]
        Q = [_stack_heads(q_ref, h).astype(_BF) for h in range(N_KV)]
        sk = [jnp.concatenate([jnp.broadcast_to(sink_ref[GRP * h + g:GRP * h + g + 1, 0:1], (WIN, 1))
                               for g in range(GRP)], axis=0) for h in range(N_KV)]
        s = [_nt(Q[h], K[h]) + bias_ref[0] for h in range(N_KV)]
        m = [jnp.maximum(jnp.max(s[h], axis=1, keepdims=True), sk[h]) for h in range(N_KV)]
        e = [jnp.exp(s[h] - m[h]) for h in range(N_KV)]
        den = [jnp.sum(e[h], axis=1, keepdims=True) + jnp.exp(sk[h] - m[h]) for h in range(N_KV)]
        V = [_both_halves(vall, h) for h in range(N_KV)]
        o = [_nn((e[h] * (1.0 / den[h])).astype(_BF), V[h]) for h in range(N_KV)]
        for h in range(N_KV):
            lse = m[h] + jnp.log(den[h])
            for g in range(GRP):
                lse_t = jnp.where(lane == GRP * h + g, lse[WIN * g:WIN * g + WIN], lse_t)
            for pp in range(GRP // 2):
                t = 2 * h + pp
                o_ref[:, 128 * t:128 * t + 128] = _unstack_pair(o[h], pp).astype(_BF)
        lse_ref[...] = lse_t

        @pl.when(n == nb - 1)
        def _():
            _xchg_finish(cps)

    qs = pl.BlockSpec((WIN, QP), lambda n: (n + 2, 0))
    anyspec = pl.BlockSpec(memory_space=pl.ANY)
    bias, bias_spec = _attn_bias(nb)
    return pl.pallas_call(
        body, name="attn_fwd", grid=(nb,),
        in_specs=[qs] + kspecs + kspecs + [_full((8, 128)), bias_spec] + [anyspec] * na,
        out_specs=[_rows(WIN, 512), _rows(WIN, 128)] + [anyspec] * na,
        out_shape=[jax.ShapeDtypeStruct((S, 512), _BF), jax.ShapeDtypeStruct((S, 128), F32)]
        + _xchg_out_shapes(shards, True),
        scratch_shapes=_xchg_scratch(na),
        compiler_params=_cp(("arbitrary",)),
    )(q, k, k, k, k, v, v, v, v, sink, bias, *shards)


def _attn_bwd(q, k, v, sink, lse, d_attn, slabs):
    E = q.shape[0]
    S = E - CTX
    nb, kspecs = _attn_specs(E)
    last = E // WIN - 1
    na = len(slabs)

    def body(q_ref, kc, kp, kk, kn, vc, vp, vk, vn, sink_ref, bias_ref, lse_ref, do_ref, *rest):
        slab_refs, (dq_ref, dk_ref, dv_ref, ds_ref), got_refs = rest[:na], rest[na:na + 4], rest[na + 4:2 * na + 4]
        n = pl.program_id(0)
        cps = _xchg_copies(slab_refs, got_refs, *rest[2 * na + 4:], gather=False)

        @pl.when(n == 0)
        def _():
            _xchg_start(cps)
            dk_ref[...] = jnp.zeros_like(dk_ref)
            dv_ref[...] = jnp.zeros_like(dv_ref)
            ds_ref[...] = jnp.zeros_like(ds_ref)

        lane = lax.broadcasted_iota(jnp.int32, (WIN, 128), 1)
        lse_t = lse_ref[...]
        starts = [None, pl.multiple_of((n + 1) * WIN, WIN), pl.multiple_of((n + 2) * WIN, WIN),
                  pl.multiple_of(jnp.minimum(n + 3, last) * WIN, WIN)]
        kall = jnp.concatenate([kc[...], kp[...], kk[...], kn[...]], axis=0)
        vall = jnp.concatenate([vc[...], vp[...], vk[...], vn[...]], axis=0)
        for h in range(N_KV):
            hs = slice(HD * h, HD * h + HD)
            K = _both_halves(kall, h)
            V = _both_halves(vall, h)
            Q = _stack_heads(q_ref, h).astype(_BF)
            sk = jnp.concatenate([jnp.broadcast_to(sink_ref[GRP * h + g:GRP * h + g + 1, 0:1], (WIN, 1))
                                  for g in range(GRP)], axis=0)
            ls = jnp.concatenate([jnp.sum(jnp.where(lane == GRP * h + g, lse_t, 0.0), axis=1, keepdims=True)
                                  for g in range(GRP)], axis=0)
            do = _stack_heads(do_ref, h).astype(_BF)
            p = jnp.exp(_nt(Q, K) + bias_ref[0] - ls)
            dp = _nt(do, V)
            delta = jnp.sum(p * dp, axis=1, keepdims=True)
            dsc = (p * (dp - delta)).astype(_BF)
            dq = _nn(dsc, K) * (HD ** -0.5)
            for pp in range(GRP // 2):
                t = 2 * h + pp
                dq_ref[:, 128 * t:128 * t + 128] = _unstack_pair(dq, pp).astype(_BF)
            dK2 = _tn(Q, dsc)
            dV2 = _tn(do, p.astype(_BF))
            dK = dK2[0:HD] + dK2[HD:2 * HD]
            dV = dV2[0:HD] + dV2[HD:2 * HD]
            dk_ref[hs, 0:CTX] += dK[:, 0:CTX]
            dv_ref[hs, 0:CTX] += dV[:, 0:CTX]
            for w in range(1, 4):
                lo = CTX + WIN * (w - 1)
                dk_ref[hs, pl.ds(starts[w], WIN)] += dK[:, lo:lo + WIN]
                dv_ref[hs, pl.ds(starts[w], WIN)] += dV[:, lo:lo + WIN]
            psk = -jnp.exp(sk - ls) * delta
            for g in range(GRP):
                j = GRP * h + g
                tot = jnp.sum(psk[WIN * g:WIN * g + WIN], axis=0, keepdims=True)
                ds_ref[j:j + 1, :] += jnp.broadcast_to(tot, (1, 128))

        @pl.when(n == nb - 1)
        def _():
            _xchg_finish(cps)

    qs = pl.BlockSpec((WIN, QP), lambda n: (n + 2, 0))
    anyspec = pl.BlockSpec(memory_space=pl.ANY)
    bias, bias_spec = _attn_bias(nb)
    return pl.pallas_call(
        body, name="attn_bwd", grid=(nb,),
        in_specs=[qs] + kspecs + kspecs + [_full((8, 128)), bias_spec, _rows(WIN, 128), _rows(WIN, 512)]
        + [anyspec] * na,
        out_specs=[_rows(WIN, QP), _full((KP, E)), _full((KP, E)), _full((8, 128))] + [anyspec] * na,
        out_shape=[jax.ShapeDtypeStruct((S, QP), _BF), jax.ShapeDtypeStruct((KP, E), F32),
                   jax.ShapeDtypeStruct((KP, E), F32), jax.ShapeDtypeStruct((8, 128), F32)]
        + _xchg_out_shapes(slabs, False),
        scratch_shapes=_xchg_scratch(na),
        compiler_params=_cp(("arbitrary",), 48 * 1024 * 1024),
    )(q, k, k, k, k, v, v, v, v, sink, bias, lse, d_attn, *slabs)


def _gla_order(E, reverse, backward):
    nc = CTX // GLA_T
    n = E // GLA_T
    if not reverse:
        fwd = lambda s: s
    else:
        fwd = lambda s: jnp.where(s < nc, nc - 1 - s, n - 1 + nc - s)
    if backward:
        return lambda s: fwd(n - 1 - s)
    return fwd


def _gla_masks():
    T = GLA_T
    l128 = lax.broadcasted_iota(jnp.int32, (1, 128), 1)
    qmask = [((l128 >> 5) == j).astype(F32) for j in range(4)]
    vmask = [((l128 >> 6) == j).astype(F32) for j in range(2)]
    bd = ((lax.broadcasted_iota(jnp.int32, (512, 256), 0) >> 6)
          == (lax.broadcasted_iota(jnp.int32, (512, 256), 1) >> 5)).astype(F32)
    ri = lax.broadcasted_iota(jnp.int32, (T, 2 * T), 0)
    ci = lax.broadcasted_iota(jnp.int32, (T, 2 * T), 1) & (T - 1)
    return qmask, vmask, bd, ri, ci


def _tri_sum(tri, x):
    hi = x.astype(_BF)
    r1 = x - hi.astype(F32)
    mid = r1.astype(_BF)
    lo = (r1 - mid.astype(F32)).astype(_BF)
    n = x.shape[1]
    y = _nn(tri.astype(_BF), jnp.concatenate([hi, mid, lo], axis=1))
    return y[:, 0:n] + y[:, n:2 * n] + y[:, 2 * n:3 * n]


def _gla_decays(la, reverse, ri, ci):
    T = GLA_T
    msk2 = (ri <= ci) if reverse else (ri >= ci)
    mskT2 = (ri >= ci) if reverse else (ri <= ci)
    b = _tri_sum(msk2[:, 0:T], la)
    bT = b[0:1] if reverse else b[T - 1:T]
    bm = b[T // 2:T // 2 + 1]
    return msk2, mskT2, b, bT, bm


def _pair_stack(tile, m0, m1):
    return jnp.concatenate([(tile * m0).astype(_BF), (tile * m1).astype(_BF)], axis=0)


def _gla_fwd(gq, gk, gv, la, reverse, shards=()):
    E = gq.shape[0]
    T = GLA_T
    n = E // T
    order = _gla_order(E, reverse, False)
    col = 1 if reverse else 0

    def body(gq_ref, gk_ref, gv_ref, la_ref, o_ref, st_ref, S_scr):
        @pl.when(pl.program_id(0) == 0)
        def _():
            S_scr[...] = jnp.zeros_like(S_scr)

        qmask, vmask, bd, ri, ci = _gla_masks()
        msk2, _, b, bT, bm = _gla_decays(la_ref[...], reverse, ri, ci)
        q, k, v = gq_ref[...], gk_ref[...], gv_ref[...]
        qd = (q * jnp.exp(b)).astype(_BF)
        qm = (q * jnp.exp(b - bm)).astype(_BF)
        km = k * jnp.exp(bm - b)
        kd = (k * jnp.exp(bT - b)).astype(_BF)
        ST = S_scr[...]
        comp = ST[0:DV]
        for h in range(1, N_GLA):
            comp = comp + ST[DV * h:DV * h + DV]
        st_ref[0] = comp
        inter = _nt(qd, ST.astype(_BF))
        tiles = []
        for p in range(N_GLA // 2):
            qs = slice(128 * (p // 2), 128 * (p // 2) + 128)
            vs = slice(128 * p, 128 * p + 128)
            j0 = (2 * p) % 4
            KS = _pair_stack(km[:, qs], qmask[j0], qmask[j0 + 1])
            VS = _pair_stack(v[:, vs], vmask[0], vmask[1])
            AA = jnp.where(msk2, _nt(qm[:, qs], KS), 0.0).astype(_BF)
            tiles.append(_nn(AA, VS))
        o_ref[...] = inter + jnp.concatenate(tiles, axis=1)
        S_scr[...] = ST * jnp.exp(bT) + bd * _tn(v.astype(_BF), kd)

    blk = lambda w, c=0: pl.BlockSpec((T, w), lambda s: (order(s), c))
    return _hosted_call(
        body, (gq, gk, gv, la), shards, True,
        name="gla_fwd_rev" if reverse else "gla_fwd", grid=(n,),
        in_specs=[blk(256), blk(256), blk(512), blk(256, col)],
        out_specs=[blk(512), pl.BlockSpec((1, DV, 256), lambda s: (order(s), 0, 0))],
        out_shape=[jax.ShapeDtypeStruct((E, 512), F32), jax.ShapeDtypeStruct((n, DV, 256), F32)],
        scratch_shapes=[pltpu.VMEM((512, 256), F32)],
        compiler_params=_cp(("arbitrary",)))


def _gla_bwd(gq, gk, gv, la, st, do, reverse, slabs=()):
    E = gq.shape[0]
    T = GLA_T
    n = E // T
    nc = CTX // T
    order = _gla_order(E, reverse, True)
    col = 1 if reverse else 0

    def body(gq_ref, gk_ref, gv_ref, la_ref, st_ref, do_ref, dq_ref, dk_ref, dv_ref, dla_ref, dS_scr):
        @pl.when(pl.program_id(0) == 0)
        def _():
            dS_scr[...] = jnp.zeros_like(dS_scr)

        is_lat = order(pl.program_id(0)) >= nc
        qmask, vmask, bd, ri, ci = _gla_masks()
        msk2, mskT2, b, bT, bm = _gla_decays(la_ref[...], reverse, ri, ci)
        q, k, v = gq_ref[...], gk_ref[...], gv_ref[...]
        do = jnp.where(is_lat, do_ref[...].astype(F32), 0.0)
        e_b, e_qm, e_km, e_kd, e_T = jnp.exp(b), jnp.exp(b - bm), jnp.exp(bm - b), jnp.exp(bT - b), jnp.exp(bT)
        qd, qm, km, kd = q * e_b, q * e_qm, k * e_km, k * e_kd
        qdb, qmb, kmb, kdb, vb, dob = (t.astype(_BF) for t in (qd, qm, km, kd, v, do))
        ST = jnp.tile(st_ref[0], (N_GLA, 1)) * bd
        dST = dS_scr[...]
        dSTb = dST.astype(_BF)
        dqd = _nn(dob, ST.astype(_BF))
        dkd = _nn(vb, dSTb)
        dv_t, dqm_t, dkm_t = [], [None, None], [None, None]
        for p in range(N_GLA // 2):
            t = p // 2
            qs = slice(128 * t, 128 * t + 128)
            vs = slice(128 * p, 128 * p + 128)
            j0 = (2 * p) % 4
            QS = _pair_stack(qm[:, qs], qmask[j0], qmask[j0 + 1])
            KS = _pair_stack(km[:, qs], qmask[j0], qmask[j0 + 1])
            VS = _pair_stack(v[:, vs], vmask[0], vmask[1])
            DS = _pair_stack(do[:, vs], vmask[0], vmask[1])
            ATT = jnp.where(mskT2, _nt(kmb[:, qs], QS), 0.0).astype(_BF)
            dAA = jnp.where(msk2, _nt(dob[:, vs], VS), 0.0).astype(_BF)
            dATT = jnp.where(mskT2, _nt(vb[:, vs], DS), 0.0).astype(_BF)
            dv_t.append(_nn(ATT, DS))
            dq_p = _nn(dAA, KS)
            dk_p = _nn(dATT, QS)
            dqm_t[t] = dq_p if dqm_t[t] is None else dqm_t[t] + dq_p
            dkm_t[t] = dk_p if dkm_t[t] is None else dkm_t[t] + dk_p
        dqm = jnp.concatenate(dqm_t, axis=1)
        dkm = jnp.concatenate(dkm_t, axis=1)
        dq_ref[...] = dqm * e_qm + dqd * e_b
        dk_ref[...] = dkm * e_km + dkd * e_kd
        dv_ref[...] = _nt(kdb, dSTb) + jnp.concatenate(dv_t, axis=1)
        db = dqm * qm - dkm * km + dqd * qd - dkd * kd
        dbT = jnp.sum(dkd * kd, axis=0, keepdims=True) + e_T * jnp.sum(dST * ST, axis=0, keepdims=True)
        dla_ref[...] = _tri_sum(mskT2[:, 0:T], db) + dbT
        dS_scr[...] = dST * e_T + bd * _tn(dob, qdb)

    blk = lambda w, c=0: pl.BlockSpec((T, w), lambda s: (order(s), c))
    do_spec = pl.BlockSpec((T, 512), lambda s: (jnp.maximum(order(s) - nc, 0), 0))
    return _hosted_call(
        body, (gq, gk, gv, la, st, do), slabs, False,
        name="gla_bwd_rev" if reverse else "gla_bwd", grid=(n,),
        in_specs=[blk(256), blk(256), blk(512), blk(256, col),
                  pl.BlockSpec((1, DV, 256), lambda s: (order(s), 0, 0)), do_spec],
        out_specs=[blk(256), blk(256), blk(512), blk(256)],
        out_shape=[jax.ShapeDtypeStruct((E, 256), F32), jax.ShapeDtypeStruct((E, 256), F32),
                   jax.ShapeDtypeStruct((E, 512), F32), jax.ShapeDtypeStruct((E, 256), F32)],
        scratch_shapes=[pltpu.VMEM((512, 256), F32)],
        compiler_params=_cp(("arbitrary",)))


def _gla_out(o_f, o_b, gg, ggla, mavg):
    o = o_f + o_b
    rr = lax.rsqrt(_head_mean(o * o, mavg) + EPS)
    oh = o * rr
    sg = _sigmoid(gg)
    return oh, rr, sg


def _mix_fwd(x, attn, o_f, o_b, gg, ggla, mavg, wout, gt1, g2):
    S = x.shape[0]
    TM = 256

    def body(x_ref, a_ref, of_ref, ob_ref, gg_ref, ggla_ref, mavg_ref, w_ref, gt1_ref, g2_ref, x1_ref, mix_ref):
        gg_t = gg_ref[...]
        oh, _, sg = _gla_out(of_ref[...], ob_ref[...], gg_t, ggla_ref[...], mavg_ref[...])
        mix_ref[:, 0:512] = a_ref[...]
        mix_ref[:, 512:1024] = (oh * ggla_ref[...] * (gg_t * sg)).astype(_BF)
        y = _nn(mix_ref[...], w_ref[...])
        ry = lax.rsqrt(jnp.mean(y * y, axis=-1, keepdims=True) + EPS)
        x1_ref[...] = x_ref[...] + gt1_ref[...] * ((y * ry) * g2_ref[...])

    return pl.pallas_call(
        body, name="mix_fwd", grid=(S // TM,),
        in_specs=[_rows(TM, D), _rows(TM, 512), _rows(TM, 512, 1), _rows(TM, 512, 1), _rows(TM, 512, 1),
                  _full((1, 512)), _full((512, 512)), _full((D, D)), _full((1, D)), _full((1, D))],
        out_specs=[_rows(TM, D), _rows(TM, D)],
        out_shape=[jax.ShapeDtypeStruct((S, D), F32), jax.ShapeDtypeStruct((S, D), _BF)],
        compiler_params=_cp(("arbitrary",), 40 * 1024 * 1024),
    )(x, attn, o_f, o_b, gg, ggla, mavg, wout, gt1, g2)


def _mix_bwd(dx1, mix, o_f, o_b, gg, ggla, mavg, wout, gt1, g2):
    S = dx1.shape[0]
    TM = 256

    def body(dx_ref, mix_ref, of_ref, ob_ref, gg_ref, ggla_ref, mavg_ref, w_ref, gt1_ref, g2_ref,
             da_ref, do_ref, dgg_ref, dy_ref, sums_ref):
        @pl.when(pl.program_id(0) == 0)
        def _():
            sums_ref[...] = jnp.zeros_like(sums_ref)

        dx = dx_ref[...]
        y = _nn(mix_ref[...], w_ref[...])
        ry = lax.rsqrt(jnp.mean(y * y, axis=-1, keepdims=True) + EPS)
        yh = y * ry
        sums_ref[0:1, :] += jnp.sum(dx * yh, axis=0, keepdims=True)
        dyh = dx * (gt1_ref[...] * g2_ref[...])
        dy = (ry * (dyh - yh * jnp.mean(dyh * yh, axis=-1, keepdims=True))).astype(_BF)
        dy_ref[...] = dy
        dmix = _nt(dy, w_ref[...])
        da_ref[...] = dmix[:, 0:512].astype(_BF)
        dgla = dmix[:, 512:1024]
        gg_t = gg_ref[...]
        ggla_t = ggla_ref[...]
        oh, rr, sg = _gla_out(of_ref[...], ob_ref[...], gg_t, ggla_t, mavg_ref[...])
        dgg_ref[...] = (dgla * oh * ggla_t * (sg * (1.0 + gg_t * (1.0 - sg)))).astype(_BF)
        don = dgla * (gg_t * sg)
        sums_ref[1:2, 0:512] += jnp.sum(don * oh, axis=0, keepdims=True)
        doh = don * ggla_t
        do_ref[...] = (rr * (doh - oh * _head_mean(doh * oh, mavg_ref[...]))).astype(_BF)

    return pl.pallas_call(
        body, name="mix_bwd", grid=(S // TM,),
        in_specs=[_rows(TM, D), _rows(TM, D), _rows(TM, 512, 1), _rows(TM, 512, 1), _rows(TM, 512, 1),
                  _full((1, 512)), _full((512, 512)), _full((D, D)), _full((1, D)), _full((1, D))],
        out_specs=[_rows(TM, 512), _rows(TM, 512), _rows(TM, 512), _rows(TM, D), _full((8, D))],
        out_shape=[jax.ShapeDtypeStruct((S, 512), _BF), jax.ShapeDtypeStruct((S, 512), _BF),
                   jax.ShapeDtypeStruct((S, 512), _BF), jax.ShapeDtypeStruct((S, D), _BF),
                   jax.ShapeDtypeStruct((8, D), F32)],
        compiler_params=_cp(("arbitrary",), 40 * 1024 * 1024),
    )(dx1, mix, o_f, o_b, gg, ggla, mavg, wout, gt1, g2)


def _ffn(x1, target, gm2, sh2, gt2, g4, wffi, wffo):
    S = x1.shape[0]
    TF = 256

    def body(x_ref, t_ref, gm_ref, sh_ref, gt_ref, g4_ref, wi_hbm, wo_hbm,
             dx_ref, h_ref, du_ref, act_ref, df_ref, sums_ref, loss_ref, wi, wo, sem):
        @pl.when(pl.program_id(0) == 0)
        def _():
            c1 = pltpu.make_async_copy(wi_hbm, wi, sem.at[0])
            c2 = pltpu.make_async_copy(wo_hbm, wo, sem.at[1])
            c1.start()
            c2.start()
            sums_ref[...] = jnp.zeros_like(sums_ref)
            loss_ref[...] = jnp.zeros_like(loss_ref)
            c1.wait()
            c2.wait()

        x = x_ref[...]
        gm = gm_ref[...]
        r = lax.rsqrt(jnp.mean(x * x, axis=-1, keepdims=True) + EPS)
        xh = x * r
        hb = (xh * gm + sh_ref[...]).astype(_BF)
        h_ref[...] = hb
        u = _nt(hb, wi[...])
        g = u[:, 0:FFN]
        up = u[:, FFN:2 * FFN]
        sg = _sigmoid(g)
        sl = g * sg
        ab = (sl * up).astype(_BF)
        act_ref[...] = ab
        f = _nn(ab, wo[...])
        rf = lax.rsqrt(jnp.mean(f * f, axis=-1, keepdims=True) + EPS)
        fh = f * rf
        gt, g4v = gt_ref[...], g4_ref[...]
        err = x + gt * (fh * g4v) - t_ref[...]
        loss_ref[...] += jnp.sum(err * err) * (0.5 / D)
        dout = err * (1.0 / D)
        sums_ref[2:3, :] += jnp.sum(dout * fh, axis=0, keepdims=True)
        dfh = dout * (gt * g4v)
        dfb = (rf * (dfh - fh * jnp.mean(dfh * fh, axis=-1, keepdims=True))).astype(_BF)
        df_ref[...] = dfb
        dact = _nt(dfb, wo[...])
        du_ref[:, 0:FFN] = (dact * up * (sg * (1.0 + g * (1.0 - sg)))).astype(_BF)
        du_ref[:, FFN:2 * FFN] = (dact * sl).astype(_BF)
        dh = _nn(du_ref[...], wi[...])
        sums_ref[0:1, :] += jnp.sum(dh, axis=0, keepdims=True)
        sums_ref[1:2, :] += jnp.sum(dh * xh, axis=0, keepdims=True)
        dxh = dh * gm
        dx_ref[...] = dout + r * (dxh - xh * jnp.mean(dxh * xh, axis=-1, keepdims=True))

    vec = _full((1, D))
    anyspec = pl.BlockSpec(memory_space=pl.ANY)
    return pl.pallas_call(
        body, name="ffn_fwd_bwd", grid=(S // TF,),
        in_specs=[_rows(TF, D), _rows(TF, D), vec, vec, vec, vec, anyspec, anyspec],
        out_specs=[_rows(TF, D), _rows(TF, D), _rows(TF, 2 * FFN), _rows(TF, FFN), _rows(TF, D),
                   _full((8, D)), _full((8, 128))],
        out_shape=[jax.ShapeDtypeStruct((S, D), F32), jax.ShapeDtypeStruct((S, D), _BF),
                   jax.ShapeDtypeStruct((S, 2 * FFN), _BF), jax.ShapeDtypeStruct((S, FFN), _BF),
                   jax.ShapeDtypeStruct((S, D), _BF), jax.ShapeDtypeStruct((8, D), F32),
                   jax.ShapeDtypeStruct((8, 128), F32)],
        scratch_shapes=[pltpu.VMEM((2 * FFN, D), _BF), pltpu.VMEM((FFN, D), _BF), pltpu.SemaphoreType.DMA((2,))],
        compiler_params=_cp(("arbitrary",), VMEM_BIG),
    )(x1, target, gm2, sh2, gt2, g4, wffi, wffo)


def _inproj_bwd(x, ctx, gml, gmc, win, wg, cos, sa, sb, la, dq, dk, dv, dgq_f, dgq_b, dgk_f, dgk_b, dgv_f, dgv_b,
                dgg, dla_f, dla_b, dx1):
    S = x.shape[0]
    E = S + CTX
    TE = CTX

    def body(x_ref, c_ref, gml_ref, gmc_ref, w_ref, wg_ref, cos_ref, sa_ref, sb_ref, la_ref, dq_ref, dk_ref, dv_ref,
             gqf, gqb, gkf, gkb, gvf, gvb, dgg_ref, dlf, dlb, dx1_ref,
             dp_ref, dlg_ref, gx_ref, sums_ref, bsum_ref):
        i = pl.program_id(0)
        is_ctx = i == 0

        @pl.when(is_ctx)
        def _():
            sums_ref[...] = jnp.zeros_like(sums_ref)
            bsum_ref[...] = jnp.zeros_like(bsum_ref)

        lat = jnp.where(is_ctx, 0.0, 1.0)
        cos_t, sa_t, sb_t = cos_ref[...], sa_ref[...], sb_ref[...]
        dp_ref[:, O_Q:O_K] = (_unrope(dq_ref[...].astype(F32), cos_t, sa_t, sb_t) * lat).astype(_BF)
        dp_ref[:, O_K:O_V] = _unrope(dk_ref[...].T, cos_t, sa_t, sb_t).astype(_BF)
        dp_ref[:, O_V:O_GQ] = dv_ref[...].T.astype(_BF)
        dp_ref[:, O_GQ:O_GK] = ((gqf[...] + gqb[...]) * (DK ** -0.5)).astype(_BF)
        dp_ref[:, O_GK:O_GV] = (gkf[...] + gkb[...]).astype(_BF)
        dp_ref[:, O_GV:O_GG] = (gvf[...] + gvb[...]).astype(_BF)
        dp_ref[:, O_GG:O_Z] = (dgg_ref[...].astype(F32) * lat).astype(_BF)
        la_t = la_ref[...]
        dlg = (jnp.concatenate([dlf[...], dlb[...]], axis=1) * (1.0 - jnp.exp(GATE_TAU * la_t)) * (1.0 / GATE_TAU))
        bsum_ref[0:1, :] += jnp.sum(dlg, axis=0, keepdims=True)
        dlgb = dlg.astype(_BF)
        dlg_ref[...] = dlgb
        dp_ref[:, O_Z:NP] = _nt(dlgb, wg_ref[...]).astype(_BF)
        dh = _nn(dp_ref[...], w_ref[...])
        x = jnp.where(is_ctx, c_ref[...], x_ref[...])
        r = lax.rsqrt(jnp.mean(x * x, axis=-1, keepdims=True) + EPS)
        xh = x * r
        sdh = jnp.sum(dh, axis=0, keepdims=True)
        sdx = jnp.sum(dh * xh, axis=0, keepdims=True)
        sums_ref[0:1, :] += sdh * lat
        sums_ref[1:2, :] += sdx * lat
        sums_ref[2:3, :] += sdh * (1.0 - lat)
        sums_ref[3:4, :] += sdx * (1.0 - lat)
        dxh = dh * jnp.where(is_ctx, gmc_ref[...], gml_ref[...])
        gx_ref[...] = dx1_ref[...] + r * (dxh - xh * jnp.mean(dxh * xh, axis=-1, keepdims=True))

    vec = _full((1, D))
    tab = _rows(TE, 128)
    return pl.pallas_call(
        body, name="inproj_bwd", grid=(E // TE,),
        in_specs=[_rows_lat(TE, D), _full((CTX, D)), vec, vec, _full((NP, D)), _full((128, 512)), tab, tab, tab,
                  _rows(TE, 512),
                  _rows_lat(TE, QP), pl.BlockSpec((KP, TE), lambda i: (0, i)), pl.BlockSpec((KP, TE), lambda i: (0, i)),
                  _rows(TE, 256), _rows(TE, 256), _rows(TE, 256), _rows(TE, 256), _rows(TE, 512), _rows(TE, 512),
                  _rows_lat(TE, 512), _rows(TE, 256), _rows(TE, 256), _rows_lat(TE, D)],
        out_specs=[_rows(TE, NP), _rows(TE, 512), _rows_lat(TE, D), _full((8, D)), _full((8, 512))],
        out_shape=[jax.ShapeDtypeStruct((E, NP), _BF), jax.ShapeDtypeStruct((E, 512), _BF),
                   jax.ShapeDtypeStruct((S, D), F32), jax.ShapeDtypeStruct((8, D), F32),
                   jax.ShapeDtypeStruct((8, 512), F32)],
        compiler_params=_cp(("arbitrary",), VMEM_BIG),
    )(x, ctx, gml, gmc, win, wg, cos, sa, sb, la, dq, dk, dv, dgq_f, dgq_b, dgk_f, dgk_b, dgv_f, dgv_b,
      dgg, dla_f, dla_b, dx1)


def _matmul_tn(a, b, tk, tt, name, out_dtype, transpose_out=False, shards=()):
    T, KA = a.shape
    N = b.shape[1]
    nt = T // tt

    def body(a_ref, b_ref, o_ref, acc):
        t = pl.program_id(1)

        @pl.when(t == 0)
        def _():
            acc[...] = jnp.zeros_like(acc)

        acc[...] += _tn(a_ref[...], b_ref[...])

        @pl.when(t == nt - 1)
        def _():
            o_ref[...] = (acc[...].T if transpose_out else acc[...]).astype(out_dtype)

    if transpose_out:
        out_spec, out_shape = pl.BlockSpec((N, tk), lambda i, t: (0, i)), (N, KA)
    else:
        out_spec, out_shape = pl.BlockSpec((tk, N), lambda i, t: (i, 0)), (KA, N)
    res = _hosted_call(
        body, (a, b), shards, True, name=name, grid=(KA // tk, nt),
        in_specs=[pl.BlockSpec((tt, tk), lambda i, t: (t, i)), pl.BlockSpec((tt, N), lambda i, t: (t, 0))],
        out_specs=[out_spec], out_shape=[jax.ShapeDtypeStruct(out_shape, out_dtype)],
        scratch_shapes=[pltpu.VMEM((tk, N), F32)],
        compiler_params=_cp(("arbitrary", "arbitrary"), VMEM_BIG))
    return res if shards else res[0]


def _ada_bwd(c_all, c_ctx, w_ada, d_all):
    n = w_ada.shape[1]

    def body(c_ref, cc_ref, w_ref, d_ref, gw_ref, t_ref):
        c = jnp.concatenate([c_ref[...], jnp.broadcast_to(cc_ref[...], (8, D))], axis=0)
        db = d_ref[...].astype(_BF)
        gw_ref[0] = _tn((c * _sigmoid(c)).astype(_BF), db)
        t_ref[...] = _nt(db[8:16], w_ref[...].astype(_BF))

    return pl.pallas_call(
        body, name="ada_bwd", in_specs=[_full((8, D)), _full((1, D)), _full((D, n)), _full((16, n))],
        out_specs=[_full((1, D, n)), _full((8, D))],
        out_shape=[jax.ShapeDtypeStruct((1, D, n), F32), jax.ShapeDtypeStruct((8, D), F32)], grid=(1,),
        compiler_params=_cp(("arbitrary",)),
    )(c_all, c_ctx, w_ada, d_all)


PART_ROWS = 56
R_ADA, R_ADA_C, R_GAIN, R_SINK, R_BG, R_GGLA, R_LOSS, R_WG = 0, 6, 12, 16, 17, 18, 19, 24


def _small_grads(s_in, s_ffn, s_mix, ada_l, ada_c, gains, dsink, s_bg, g_wg, loss):
    def body(si, sf, sm, al, ac, g, ds, sbg, gwg, loss_ref, o_ref):
        o_ref[...] = jnp.zeros_like(o_ref)
        o_ref[R_LOSS:R_LOSS + 1, 0:128] = loss_ref[0:1, :]
        sub = lax.broadcasted_iota(jnp.int32, (8, 128), 0)
        lane = lax.broadcasted_iota(jnp.int32, (8, 128), 1)
        o_ref[R_SINK:R_SINK + 1, 0:128] = jnp.sum(jnp.where(sub == lane, ds[...], 0.0), axis=0, keepdims=True)
        o_ref[R_BG:R_BG + 1, 0:512] = sbg[0:1, :]
        y = sm[1:2, 0:128] + sm[1:2, 128:256] + sm[1:2, 256:384] + sm[1:2, 384:512]
        y = y + pltpu.roll(y, 64, 1)
        o_ref[R_GGLA:R_GGLA + 1, 0:128] = jnp.where(lane[0:1] < DV, y, 0.0)
        o_ref[R_WG:R_WG + 16, 0:256] = gwg[0:16, 0:256]
        o_ref[R_WG + 16:R_WG + 32, 0:256] = gwg[16:32, 256:512]
        sdh_l, sdx_l, sdh_c, sdx_c = si[0:1], si[1:2], si[2:3], si[3:4]
        sdh2, sdx2, a2 = sf[0:1], sf[1:2], sf[2:3]
        a1 = sm[0:1]
        g1, g2, g3, g4 = g[0:1], g[1:2], g[2:3], g[3:4]
        sc1, gt1, sc2, gt2 = al[1:2], al[2:3], al[4:5], al[5:6]
        sc1c = ac[1:2]
        z = jnp.zeros((1, D), F32)
        rows = [sdh_l, sdx_l * g1, a1 * g2, sdh2, sdx2 * g3, a2 * g4,
                sdh_c, sdx_c * g1, z, z, z, z,
                sdx_l * (1.0 + sc1) + sdx_c * (1.0 + sc1c), a1 * gt1, sdx2 * (1.0 + sc2), a2 * gt2]
        for r, v in enumerate(rows):
            o_ref[r:r + 1, :] = v

    v8 = _full((8, D))
    return pl.pallas_call(
        body, name="small_grads",
        in_specs=[v8] * 6 + [_full((8, 128)), _full((8, 512)), _full((128, 512)), _full((8, 128))],
        out_specs=_full((PART_ROWS, D)), out_shape=jax.ShapeDtypeStruct((PART_ROWS, D), F32), grid=(1,),
        compiler_params=_cp(("arbitrary",)),
    )(s_in, s_ffn, s_mix, ada_l, ada_c, gains, dsink, s_bg, g_wg, loss)


def _row_tile(R):
    for cand in (256, 128, 64, 32, 16):
        if R % cand == 0 and R > cand:
            return cand
    return R


def _adamw(w, g, m, v, name):
    _, R, C = w.shape
    tr = _row_tile(R)
    ns = g.shape[0]
    c1 = 1.0 / (1.0 - ADAM_B1 ** ADAM_STEP)
    c2 = 1.0 / (1.0 - ADAM_B2 ** ADAM_STEP)

    def body(w_ref, g_ref, m_ref, v_ref, go_ref, d_ref, nm_ref, nv_ref):
        gg = g_ref[0].astype(F32)
        for j in range(1, ns):
            gg = gg + g_ref[j].astype(F32)
        go_ref[0] = gg
        nm = ADAM_B1 * m_ref[0] + (1.0 - ADAM_B1) * gg
        nv = ADAM_B2 * v_ref[0] + (1.0 - ADAM_B2) * (gg * gg)
        nm_ref[0] = nm
        nv_ref[0] = nv
        d_ref[0] = -ADAM_LR * ((nm * c1) / (jnp.sqrt(nv * c2) + ADAM_EPS) + ADAM_WD * w_ref[0])

    spec = pl.BlockSpec((1, tr, C), lambda i: (0, i, 0))
    sds = jax.ShapeDtypeStruct((1, R, C), F32)
    return pl.pallas_call(
        body, name=name, grid=(R // tr,),
        in_specs=[spec, pl.BlockSpec((ns, tr, C), lambda i: (0, i, 0)), spec, spec],
        out_specs=[spec] * 4, out_shape=[sds] * 4, compiler_params=_cp(("parallel",), 48 * 1024 * 1024),
    )(w, g, m, v)


def _sum_slots(slots, name):
    _, R, C = slots.shape
    tr = _row_tile(R)

    def body(s_ref, o_ref):
        acc = s_ref[0].astype(F32)
        for j in range(1, N_DEV):
            acc = acc + s_ref[j].astype(F32)
        o_ref[...] = acc

    return pl.pallas_call(
        body, name=name, grid=(R // tr,), in_specs=[pl.BlockSpec((N_DEV, tr, C), lambda i: (0, i, 0))],
        out_specs=_rows(tr, C), out_shape=jax.ShapeDtypeStruct((R, C), F32), compiler_params=_cp(("parallel",)),
    )(slots)


def _ag2_start(x_ref, out_ref, send_sems, recv_sems, local_sem):
    x, y, c = lax.axis_index("x"), lax.axis_index("y"), lax.axis_index("c")
    me, sibling = (x, y, c), (x, y, 1 - c)
    chips = [(1 - x, y), (x, 1 - y), (1 - x, 1 - y)]

    def rows(px, py, pc):
        return out_ref.at[4 * px + 2 * py + pc]

    def copy(k, block, to, src=None):
        return pltpu.make_async_remote_copy(
            src_ref=rows(*block) if src is None else src, dst_ref=rows(*block),
            send_sem=send_sems.at[k], recv_sem=recv_sems.at[k], device_id=to, device_id_type=MESH)

    mine = pltpu.make_async_copy(x_ref, rows(*me), local_sem)
    mine.start()
    first = [copy(0, me, sibling, src=x_ref)]
    first += [copy(1 + j, me, (*chip, c), src=x_ref) for j, chip in enumerate(chips)]
    for cp in first:
        cp.start()
    return copy, mine, first, me, sibling, chips, c


def _ag2_finish(state):
    copy, mine, first, me, sibling, chips, c = state
    passed = [copy(4 + j, (*chip, c), sibling) for j, chip in enumerate(chips)]
    for j, chip in enumerate(chips):
        copy(1 + j, (*chip, c), me).wait_recv()
        passed[j].start()
    copy(0, sibling, me).wait_recv()
    for j, chip in enumerate(chips):
        copy(4 + j, (*chip, 1 - c), me).wait_recv()
    for cp in first + passed:
        cp.wait_send()
    mine.wait()


def _allgather(x_shard, name):
    m_per, n = x_shard.shape

    def body(x_ref, out_ref, send_sems, recv_sems, local_sem):
        _ag2_finish(_ag2_start(x_ref, out_ref, send_sems, recv_sems, local_sem))

    return pl.pallas_call(
        body, name=name, out_shape=jax.ShapeDtypeStruct((N_DEV, m_per, n), x_shard.dtype),
        in_specs=[pl.BlockSpec(memory_space=pltpu.VMEM)], out_specs=pl.BlockSpec(memory_space=pltpu.VMEM),
        scratch_shapes=[pltpu.SemaphoreType.DMA((7,)), pltpu.SemaphoreType.DMA((7,)), pltpu.SemaphoreType.DMA],
        compiler_params=pltpu.CompilerParams(vmem_limit_bytes=VMEM_BIG),
    )(x_shard)


def _entry(c, wg_sh, win_sh, c_ctx, w_ada):
    n = w_ada.shape[1]

    def body(c_ref, g_ref, w_ref, cc_ref, wa_ref, call_ref, gall_ref, wall_ref, ada_ref, part,
             s_send, s_recv, s_loc, w_send, w_recv, w_loc, a_send, a_recv, a_loc):
        big = _ag2_start(w_ref, wall_ref, w_send, w_recv, w_loc)
        small = _xchg_copies([c_ref, g_ref], [call_ref, gall_ref], s_send, s_recv, s_loc, gather=True)
        _xchg_start(small)
        _xchg_finish(small)
        cs = jnp.concatenate([call_ref[:, 0, :], jnp.broadcast_to(cc_ref[...], (8, D))], axis=0)
        part[...] = _nn((cs * _sigmoid(cs)).astype(_BF), wa_ref[...].astype(_BF))
        ada = _xchg_copies([part], [ada_ref], a_send, a_recv, a_loc, gather=True)
        _xchg_start(ada)
        _xchg_finish(ada)
        _ag2_finish(big)

    vm = pl.BlockSpec(memory_space=pltpu.VMEM)
    return pl.pallas_call(
        body, name="entry_gather",
        out_shape=[jax.ShapeDtypeStruct((N_DEV,) + c.shape, F32), jax.ShapeDtypeStruct((N_DEV,) + wg_sh.shape, F32),
                   jax.ShapeDtypeStruct((N_DEV,) + win_sh.shape, win_sh.dtype),
                   jax.ShapeDtypeStruct((N_DEV, 16, n), F32)],
        in_specs=[vm] * 5, out_specs=[vm] * 4,
        scratch_shapes=[pltpu.VMEM((16, n), F32)] + _xchg_scratch(2)
        + [pltpu.SemaphoreType.DMA((7,)), pltpu.SemaphoreType.DMA((7,)), pltpu.SemaphoreType.DMA] + _xchg_scratch(1),
        compiler_params=pltpu.CompilerParams(vmem_limit_bytes=VMEM_BIG),
    )(c, wg_sh, win_sh, c_ctx, w_ada)


def _alltoall(slabs, name):
    na = len(slabs)

    def body(*refs):
        ins, outs = refs[:na], refs[na:2 * na]
        send_sems, recv_sems, local_sems = refs[2 * na:]
        x, y, c = lax.axis_index("x"), lax.axis_index("y"), lax.axis_index("c")
        me = 4 * x + 2 * y + c
        copies = []
        for a in range(na):
            loc = pltpu.make_async_copy(ins[a].at[me], outs[a].at[me], local_sems.at[a])
            loc.start()
            copies.append(loc)
        rem = []
        for k in range(1, N_DEV):
            px, py, pc = x ^ (k >> 2), y ^ ((k >> 1) & 1), c ^ (k & 1)
            peer = 4 * px + 2 * py + pc
            for a in range(na):
                cp = pltpu.make_async_remote_copy(
                    src_ref=ins[a].at[peer], dst_ref=outs[a].at[me],
                    send_sem=send_sems.at[a, k - 1], recv_sem=recv_sems.at[a, k - 1],
                    device_id=(px, py, pc), device_id_type=MESH)
                cp.start()
                rem.append((a, k, peer, cp))
        for a, k, peer, cp in rem:
            pltpu.make_async_remote_copy(
                src_ref=ins[a].at[me], dst_ref=outs[a].at[peer],
                send_sem=send_sems.at[a, k - 1], recv_sem=recv_sems.at[a, k - 1],
                device_id=(x, y, c), device_id_type=MESH).wait_recv()
        for a, k, peer, cp in rem:
            cp.wait_send()
        for loc in copies:
            loc.wait()

    anyspec = pl.BlockSpec(memory_space=pl.ANY)
    return pl.pallas_call(
        body, name=name, out_shape=[jax.ShapeDtypeStruct(s.shape, s.dtype) for s in slabs],
        in_specs=[anyspec] * na, out_specs=[anyspec] * na,
        scratch_shapes=[pltpu.SemaphoreType.DMA((na, 7)), pltpu.SemaphoreType.DMA((na, 7)),
                        pltpu.SemaphoreType.DMA((na,))],
    )(*slabs)


def _rope_tables(S):
    t = np.arange(S)
    row = (t // GRID_W).astype(np.float32)
    colp = (t % GRID_W).astype(np.float32)
    half = HD // 2
    inv = (ROPE_BASE ** (-np.arange(0, half, 2, dtype=np.float32) / half)).astype(np.float32)
    ar = row[:, None] * inv[None, :]
    ac = colp[:, None] * inv[None, :]
    ang = np.concatenate([ar, ar, ac, ac], axis=-1).astype(np.float32)
    cos = np.cos(ang).astype(np.float32)
    sin = np.sin(ang).astype(np.float32)
    lane = np.arange(HD)
    first = (lane % 32) < 16
    sa = np.where(first[None, :], -sin, 0.0)
    sb = np.where(first[None, :], 0.0, sin)

    def ext(tab, ctx_val):
        full = np.zeros((CTX + S, 128), np.float32)
        full[:CTX, :] = ctx_val
        full[CTX:, :HD] = tab
        full[CTX:, HD:] = tab
        return jnp.asarray(full)

    return ext(cos, 1.0), ext(sa, 0.0), ext(sb, 0.0)


def _pad_rows_win(wt):
    return jnp.pad(wt, ((0, NP - IN_COLS), (0, 0)))


def _unpad_rows_win(g):
    return g[0:IN_COLS]


def _local_step(x, ctx, target, ada_l, ada_c, gains, sink, win_p, wg_bd, bg, ggla, wout_sh, wffi_sh, wffo_sh):
    S = x.shape[0]
    cos, sa, sb = _rope_tables(S)
    g1, g2, g3, g4 = (gains[i:i + 1] for i in range(4))
    sh1, sc1, gt1, sh2, sc2, gt2 = (ada_l[i:i + 1] for i in range(6))
    sh1c, sc1c = ada_c[0:1], ada_c[1:2]
    gml, gmc, gm2 = g1 * (1.0 + sc1), g1 * (1.0 + sc1c), g3 * (1.0 + sc2)
    mavg = jnp.asarray(np.kron(np.eye(N_GLA, dtype=np.float32), np.full((DV, DV), 1.0 / DV, np.float32))).astype(_BF)

    n_ffi, r_ffo, r_out = wffi_sh.shape[0], wffo_sh.shape[0], wout_sh.shape[0]
    tt_e = 768 if (S + CTX) % 768 == 0 else 256
    tt_s = 512 if S % 512 == 0 else 256
    h, q, k, v, gq, gk, gv, gg, z, la, wout_g = _inproj_fwd(x, ctx, gml, sh1, gmc, sh1c, win_p, wg_bd, bg,
                                                            cos, sa, sb, [wout_sh])
    attn, lse, wffi_g = _attn_fwd(q, k, v, sink, [wffi_sh])
    o_f, st_f, wffo_g = _gla_fwd(gq, gk, gv, la, False, [wffo_sh])
    o_b, st_b = _gla_fwd(gq, gk, gv, la, True)
    wout = wout_g.reshape(N_DEV * r_out, D)
    wffi = wffi_g.reshape(N_DEV * n_ffi, D)
    wffo = wffo_g.reshape(N_DEV * r_ffo, D)
    x1, mix = _mix_fwd(x, attn, o_f, o_b, gg, ggla, mavg, wout, gt1, g2)
    dx1, h2, du, act, df, s_ffn, loss = _ffn(x1, target, gm2, sh2, gt2, g4, wffi, wffo)
    slab_ffi = _matmul_tn(h2, du, 512, tt_s, "grad_w_ffn_in", _BF, True).reshape(N_DEV, n_ffi, D)
    slab_ffo = _matmul_tn(act, df, FFN, tt_s, "grad_w_ffn_out", _BF).reshape(N_DEV, r_ffo, D)
    d_attn, do_gla, dgg, dy, s_mix = _mix_bwd(dx1, mix, o_f, o_b, gg, ggla, mavg, wout, gt1, g2)
    slab_out = _matmul_tn(mix, dy, D, tt_s, "grad_w_out", _BF).reshape(N_DEV, r_out, D)
    dq, dk, dv, dsink, got_ffi, got_ffo = _attn_bwd(q, k, v, sink, lse, d_attn, [slab_ffi, slab_ffo])
    dgq_f, dgk_f, dgv_f, dla_f, got_out = _gla_bwd(gq, gk, gv, la, st_f, do_gla, False, [slab_out])
    dgq_b, dgk_b, dgv_b, dla_b = _gla_bwd(gq, gk, gv, la, st_b, do_gla, True)
    dp, dlg, grad_x, s_in, s_bg = _inproj_bwd(x, ctx, gml, gmc, win_p, wg_bd, cos, sa, sb, la, dq, dk, dv,
                                              dgq_f, dgq_b, dgk_f, dgk_b, dgv_f, dgv_b, dgg, dla_f, dla_b, dx1)
    g_wg = _matmul_tn(z, dlg, 128, tt_e, "grad_w_gate", F32)
    small = _small_grads(s_in, s_ffn, s_mix, ada_l, ada_c, gains, dsink, s_bg, g_wg, loss)
    g_win_t, parts = _matmul_tn(h, dp, 512, tt_e, "grad_w_in", _BF, True, [small])
    return dict(grad_x=grad_x, g_win_t=g_win_t, got_out=got_out, got_ffi=got_ffi, got_ffo=got_ffo, parts=parts)


SMALL_NAMES = ["c_ctx", "b_ada", "g_pre_mix", "g_post_mix", "g_pre_ffn", "g_post_ffn", "attn_sink",
               "b_gate_fwd", "b_gate_bwd", "g_gla_norm", "w_gate_fwd", "w_gate_bwd"]


def _small_update(tot, t_tot, wg_g, w, m, v):
    c1 = 1.0 / (1.0 - ADAM_B1 ** ADAM_STEP)
    c2 = 1.0 / (1.0 - ADAM_B2 ** ADAM_STEP)
    n = len(SMALL_NAMES)

    def body(tot_ref, t_ref, wg_ref, *refs):
        w_r, m_r, v_r = refs[0:n], refs[n:2 * n], refs[2 * n:3 * n]
        g_o, d_o, nm_o, nv_o = refs[3 * n:4 * n], refs[4 * n:5 * n], refs[5 * n:6 * n], refs[6 * n:7 * n]

        def upd(i, idx, g):
            nm = ADAM_B1 * m_r[i][idx] + (1.0 - ADAM_B1) * g
            nv = ADAM_B2 * v_r[i][idx] + (1.0 - ADAM_B2) * (g * g)
            g_o[i][idx] = g
            nm_o[i][idx] = nm
            nv_o[i][idx] = nv
            d_o[i][idx] = -ADAM_LR * ((nm * c1) / (jnp.sqrt(nv * c2) + ADAM_EPS) + ADAM_WD * w_r[i][idx])

        everything = (slice(None), slice(None))
        cc = w_r[0][...]
        sc = _sigmoid(cc)
        upd(0, everything, t_ref[0:1, :] * (sc * (1.0 + cc * (1.0 - sc))))
        for j in range(6):
            upd(1, (slice(None), slice(D * j, D * j + D)),
                tot_ref[R_ADA + j:R_ADA + j + 1, :] + tot_ref[R_ADA_C + j:R_ADA_C + j + 1, :])
        for j in range(4):
            upd(2 + j, everything, tot_ref[R_GAIN + j:R_GAIN + j + 1, :])
        upd(6, everything, tot_ref[R_SINK:R_SINK + 1, 0:N_ATT])
        upd(7, everything, tot_ref[R_BG:R_BG + 1, 0:256])
        upd(8, everything, tot_ref[R_BG:R_BG + 1, 256:512])
        upd(9, everything, tot_ref[R_GGLA:R_GGLA + 1, 0:DV])
        upd(10, (0,), wg_ref[0:GATE_RANK, :])
        upd(11, (0,), wg_ref[GATE_RANK:2 * GATE_RANK, :])

    params = [w[k] for k in SMALL_NAMES] + [m[k] for k in SMALL_NAMES] + [v[k] for k in SMALL_NAMES]
    outs = pl.pallas_call(
        body, name="small_update", grid=(1,),
        in_specs=[_full(tot.shape), _full(t_tot.shape), _full(wg_g.shape)] + [_full(p.shape) for p in params],
        out_specs=[_full(w[k].shape) for k in SMALL_NAMES] * 4,
        out_shape=[jax.ShapeDtypeStruct(w[k].shape, F32) for k in SMALL_NAMES] * 4,
        compiler_params=_cp(("arbitrary",)),
    )(tot, t_tot, wg_g, *params)
    return tuple(dict(zip(SMALL_NAMES, outs[i * n:(i + 1) * n])) for i in range(4))


def kernel(x, c, ctx, c_ctx, w_ada, b_ada, g_pre_mix, g_post_mix, g_pre_ffn, g_post_ffn, w_in, attn_sink, w_gate_fwd, b_gate_fwd, w_gate_bwd, b_gate_bwd, g_gla_norm, w_out, w_ffn_in, w_ffn_out, loss_target, m_c_ctx, m_w_ada, m_b_ada, m_g_pre_mix, m_g_post_mix, m_g_pre_ffn, m_g_post_ffn, m_w_in, m_attn_sink, m_w_gate_fwd, m_b_gate_fwd, m_w_gate_bwd, m_b_gate_bwd, m_g_gla_norm, m_w_out, m_w_ffn_in, m_w_ffn_out, v_c_ctx, v_w_ada, v_b_ada, v_g_pre_mix, v_g_post_mix, v_g_pre_ffn, v_g_post_ffn, v_w_in, v_attn_sink, v_w_gate_fwd, v_b_gate_fwd, v_w_gate_bwd, v_b_gate_bwd, v_g_gla_norm, v_w_out, v_w_ffn_in, v_w_ffn_out):
    me = 4 * lax.axis_index("x") + 2 * lax.axis_index("y") + lax.axis_index("c")
    S = x.shape[1]
    n_in = w_in.shape[2]
    n_ffi = w_ffn_in.shape[2]
    r_out = w_out.shape[1]
    r_ffo = w_ffn_out.shape[1]
    n_ada = w_ada.shape[2]

    wg_sh = jnp.concatenate([w_gate_fwd.reshape(4, 128), w_gate_bwd.reshape(4, 128)], axis=0)
    c_all3, g_all, w_all, ada_all = _entry(c, wg_sh, w_in[0].T.astype(_BF), c_ctx.reshape(1, D), w_ada[0])
    c_all = c_all3.reshape(N_DEV, D)
    wgf = g_all[:, 0:4].reshape(N_DEV, GATE_RANK, 32).transpose(1, 0, 2).reshape(GATE_RANK, 256)
    wgb = g_all[:, 4:8].reshape(N_DEV, GATE_RANK, 32).transpose(1, 0, 2).reshape(GATE_RANK, 256)
    win_p = _pad_rows_win(w_all.reshape(N_DEV * n_in, D))
    wg_bd = jnp.zeros((128, 512), F32).at[0:16, 0:256].set(wgf).at[16:32, 256:512].set(wgb).astype(_BF)
    ada_full = ada_all.transpose(1, 0, 2).reshape(16, N_DEV * n_ada) + b_ada
    ada_l = jnp.pad(lax.dynamic_slice_in_dim(ada_full, me, 1, 0).reshape(6, D), ((0, 2), (0, 0)))
    ada_c = jnp.pad(ada_full[8].reshape(6, D), ((0, 2), (0, 0)))
    gains = jnp.pad(jnp.concatenate([g_pre_mix, g_post_mix, g_pre_ffn, g_post_ffn], axis=0), ((0, 4), (0, 0)))
    sink = jnp.broadcast_to(attn_sink.reshape(8, 1), (8, 128))
    bg = jnp.concatenate([b_gate_fwd, b_gate_bwd], axis=1)
    ggla = jnp.tile(g_gla_norm, (1, N_GLA))

    r = _local_step(x[0], ctx[0], loss_target[0], ada_l, ada_c, gains, sink, win_p, wg_bd, bg, ggla,
                    w_out[0].astype(_BF), w_ffn_in[0].T.astype(_BF), w_ffn_out[0].astype(_BF))

    slots = _alltoall([_unpad_rows_win(r["g_win_t"]).reshape(N_DEV, n_in, D)], "scatter_grads")

    parts = r["parts"]
    tot = _sum_slots(parts, "sum_small_grads")
    loss = tot[R_LOSS, 0]
    d_ada_rows = parts[:, R_ADA:R_ADA + 6].reshape(N_DEV, 6 * D)
    d_ada_c = tot[R_ADA_C:R_ADA_C + 6].reshape(1, 6 * D)
    my_cols = lax.dynamic_slice_in_dim(jnp.concatenate([d_ada_rows, jnp.broadcast_to(d_ada_c, (1, 6 * D)),
                                                        jnp.zeros((7, 6 * D), F32)], axis=0), me * n_ada, n_ada, 1)
    grad_w_ada, t_part = _ada_bwd(c_all, c_ctx.reshape(1, D), w_ada[0], my_cols)
    t_all = _allgather(t_part, "gather_c_ctx")
    t_tot = _sum_slots(t_all, "sum_c_ctx")
    wg_g = lax.dynamic_slice(tot, (R_WG, me * 32), (2 * GATE_RANK, 32))

    w_small = dict(c_ctx=c_ctx.reshape(1, D), b_ada=b_ada, g_pre_mix=g_pre_mix, g_post_mix=g_post_mix, g_pre_ffn=g_pre_ffn,
                   g_post_ffn=g_post_ffn, attn_sink=attn_sink, b_gate_fwd=b_gate_fwd, b_gate_bwd=b_gate_bwd,
                   g_gla_norm=g_gla_norm, w_gate_fwd=w_gate_fwd, w_gate_bwd=w_gate_bwd)
    m_small = dict(c_ctx=m_c_ctx.reshape(1, D), b_ada=m_b_ada, g_pre_mix=m_g_pre_mix, g_post_mix=m_g_post_mix,
                   g_pre_ffn=m_g_pre_ffn, g_post_ffn=m_g_post_ffn, attn_sink=m_attn_sink, b_gate_fwd=m_b_gate_fwd,
                   b_gate_bwd=m_b_gate_bwd, g_gla_norm=m_g_gla_norm, w_gate_fwd=m_w_gate_fwd, w_gate_bwd=m_w_gate_bwd)
    v_small = dict(c_ctx=v_c_ctx.reshape(1, D), b_ada=v_b_ada, g_pre_mix=v_g_pre_mix, g_post_mix=v_g_post_mix,
                   g_pre_ffn=v_g_pre_ffn, g_post_ffn=v_g_post_ffn, attn_sink=v_attn_sink, b_gate_fwd=v_b_gate_fwd,
                   b_gate_bwd=v_b_gate_bwd, g_gla_norm=v_g_gla_norm, w_gate_fwd=v_w_gate_fwd, w_gate_bwd=v_w_gate_bwd)
    grads_small, d_s, nm_s, nv_s = _small_update(tot, t_tot, wg_g, w_small, m_small, v_small)
    for dd in (grads_small, d_s, nm_s, nv_s):
        dd["c_ctx"] = dd["c_ctx"].reshape(D)

    big = {}
    for nm, w, g, m, v in [("w_ada", w_ada, grad_w_ada, m_w_ada, v_w_ada),
                           ("w_out", w_out, r["got_out"], m_w_out, v_w_out),
                           ("w_ffn_out", w_ffn_out, r["got_ffo"], m_w_ffn_out, v_w_ffn_out)]:
        big[nm] = _adamw(w, g, m, v, "adamw_" + nm)
    tr = lambda a: jnp.transpose(a, (0, 2, 1))
    for nm, w, g, m, v in [("w_in", w_in, slots[0], m_w_in, v_w_in),
                           ("w_ffn_in", w_ffn_in, r["got_ffi"], m_w_ffn_in, v_w_ffn_in)]:
        big[nm] = tuple(tr(o) for o in _adamw(tr(w), g, tr(m), tr(v), "adamw_" + nm))

    order = ["c_ctx", "w_ada", "b_ada", "g_pre_mix", "g_post_mix", "g_pre_ffn", "g_post_ffn", "w_in", "attn_sink",
             "w_gate_fwd", "b_gate_fwd", "w_gate_bwd", "b_gate_bwd", "g_gla_norm", "w_out", "w_ffn_in", "w_ffn_out"]
    grads, deltas, new_m, new_v = [], [], [], []
    for nm in order:
        if nm in big:
            g_, d_, m_, v_ = big[nm]
        else:
            g_, d_, m_, v_ = grads_small[nm], d_s[nm], nm_s[nm], nv_s[nm]
        grads.append(g_)
        deltas.append(d_)
        new_m.append(m_)
        new_v.append(v_)
    return (loss, r["grad_x"][None], *grads, *deltas, *new_m, *new_v)
```

```python
import functools
import math

import numpy as np
import jax
import jax.numpy as jnp
from jax import lax
from jax.experimental import pallas as pl
from jax.experimental.pallas import tpu as pltpu

F32 = jnp.float32
_BF = jnp.bfloat16

N_DEV = 8
D = 1024
CTX = 256
HD = 64
N_ATT = 8
N_KV = 2
GRP = N_ATT // N_KV
WIN = 128
GRID_W = 64
ROPE_BASE = 10000.0
N_GLA = 8
DK = 32
DV = 64
GATE_RANK = 16
GATE_TAU = 16.0
FFN = 2816
EPS = 1e-6
NEG = -1e30
GLA_T = 128

QP = N_ATT * HD
KP = N_KV * HD
O_Q, O_K, O_V = 0, QP, QP + KP
O_GQ = O_V + KP
O_GK = O_GQ + N_GLA * DK
O_GV = O_GK + N_GLA * DK
O_GG = O_GV + N_GLA * DV
O_Z = O_GG + N_GLA * DV
NP = O_Z + 128
IN_COLS = 2336

ADAM_LR, ADAM_B1, ADAM_B2, ADAM_EPS, ADAM_WD, ADAM_STEP = 0.001, 0.9, 0.999, 1e-08, 0.01, 10

VMEM_BIG = 56 * 1024 * 1024
MESH = pl.DeviceIdType.MESH


def _cp(sem, vmem=None):
    return pltpu.CompilerParams(dimension_semantics=sem, vmem_limit_bytes=vmem)


def _full(shape):
    nd = len(shape)
    return pl.BlockSpec(shape, lambda *a: (0,) * nd)


def _rows(tile, width, off=0):
    return pl.BlockSpec((tile, width), lambda i: (i + off, 0))


def _rows_lat(tile, width):
    return pl.BlockSpec((tile, width), lambda i: (jnp.maximum(i - 1, 0), 0))


def _nt(a, b):
    return lax.dot_general(a, b, (((1,), (1,)), ((), ())), preferred_element_type=F32)


def _tn(a, b):
    return lax.dot_general(a, b, (((0,), (0,)), ((), ())), preferred_element_type=F32)


def _nn(a, b):
    return jnp.dot(a, b, preferred_element_type=F32)


def _head_mean(x, mavg):
    n = x.shape[0]
    hi = x.astype(_BF)
    lo = (x - hi.astype(F32)).astype(_BF)
    y = _nn(jnp.concatenate([hi, lo], axis=0), mavg)
    return y[0:n] + y[n:2 * n]


def _rope(t, cos, sa, sb):
    n = t.shape[1]
    reps = n // 128
    c = jnp.tile(cos, (1, reps))
    a = jnp.tile(sa, (1, reps))
    b = jnp.tile(sb, (1, reps))
    return t * c + pltpu.roll(t, n - 16, 1) * a + pltpu.roll(t, 16, 1) * b


def _unrope(t, cos, sa, sb):
    n = t.shape[1]
    reps = n // 128
    c = jnp.tile(cos, (1, reps))
    a = jnp.tile(sa, (1, reps))
    b = jnp.tile(sb, (1, reps))
    return t * c + pltpu.roll(t * a, 16, 1) + pltpu.roll(t * b, n - 16, 1)


def _sigmoid(x):
    return 1.0 / (1.0 + jnp.exp(-x))


def _inproj_fwd(x, ctx, gml, shl, gmc, shc, win, wg, bg, cos, sa, sb, shards):
    E = x.shape[0] + CTX
    TE = CTX

    def body(x_ref, c_ref, gml_ref, shl_ref, gmc_ref, shc_ref, w_ref, wg_ref, bg_ref, cos_ref, sa_ref, sb_ref,
             h_ref, q_ref, k_ref, v_ref, gq_ref, gk_ref, gv_ref, gg_ref, z_ref, la_ref):
        is_ctx = pl.program_id(0) == 0
        gm = jnp.where(is_ctx, gmc_ref[...], gml_ref[...])
        sh = jnp.where(is_ctx, shc_ref[...], shl_ref[...])
        x = jnp.where(is_ctx, c_ref[...], x_ref[...])
        r = lax.rsqrt(jnp.mean(x * x, axis=-1, keepdims=True) + EPS)
        hb = ((x * r) * gm + sh).astype(_BF)
        h_ref[...] = hb
        p = _nt(hb, w_ref[...])
        cos_t, sa_t, sb_t = cos_ref[...], sa_ref[...], sb_ref[...]
        q_ref[...] = (_rope(p[:, O_Q:O_K], cos_t, sa_t, sb_t) * (HD ** -0.5)).astype(_BF)
        k_ref[...] = _rope(p[:, O_K:O_V], cos_t, sa_t, sb_t).astype(_BF)
        v_ref[...] = p[:, O_V:O_GQ].astype(_BF)
        gq_ref[...] = p[:, O_GQ:O_GK] * (DK ** -0.5)
        gk_ref[...] = p[:, O_GK:O_GV]
        gv_ref[...] = p[:, O_GV:O_GG]
        gg_ref[...] = p[:, O_GG:O_Z]
        zb = p[:, O_Z:NP].astype(_BF)
        z_ref[...] = zb
        lg = _nn(zb, wg_ref[...]) + bg_ref[...]
        la_ref[...] = (jnp.minimum(lg, 0.0) - jnp.log(1.0 + jnp.exp(-jnp.abs(lg)))) * (1.0 / GATE_TAU)

    vec = _full((1, D))
    tab = _rows(TE, 128)
    outs = [(D, _BF), (QP, _BF), (KP, _BF), (KP, _BF), (256, F32), (256, F32), (512, F32), (512, F32),
            (128, _BF), (512, F32)]
    return _hosted_call(
        body, (x, ctx, gml, shl, gmc, shc, win, wg, bg, cos, sa, sb), shards, True,
        name="inproj_fwd", grid=(E // TE,),
        in_specs=[_rows_lat(TE, D), _full((CTX, D)), vec, vec, vec, vec, _full((NP, D)), _full((128, 512)),
                  _full((1, 512)), tab, tab, tab],
        out_specs=[_rows(TE, w) for w, _ in outs],
        out_shape=[jax.ShapeDtypeStruct((E, w), dt) for w, dt in outs],
        compiler_params=_cp(("arbitrary",), 40 * 1024 * 1024))


def _xchg_scratch(na):
    return [pltpu.SemaphoreType.DMA((na, N_DEV - 1)), pltpu.SemaphoreType.DMA((na, N_DEV - 1)),
            pltpu.SemaphoreType.DMA((na,))]


def _xchg_copies(ins, outs, send_sems, recv_sems, local_sems, gather):
    x, y, c = lax.axis_index("x"), lax.axis_index("y"), lax.axis_index("c")
    me = 4 * x + 2 * y + c
    local, sends, recvs = [], [], []
    for a in range(len(ins)):
        local.append(pltpu.make_async_copy(ins[a] if gather else ins[a].at[me], outs[a].at[me], local_sems.at[a]))
    for k in range(1, N_DEV):
        px, py, pc = x ^ (k >> 2), y ^ ((k >> 1) & 1), c ^ (k & 1)
        peer = 4 * px + 2 * py + pc
        for a in range(len(ins)):
            sems = dict(send_sem=send_sems.at[a, k - 1], recv_sem=recv_sems.at[a, k - 1], device_id_type=MESH)
            sends.append(pltpu.make_async_remote_copy(
                src_ref=ins[a] if gather else ins[a].at[peer], dst_ref=outs[a].at[me], device_id=(px, py, pc), **sems))
            recvs.append(pltpu.make_async_remote_copy(
                src_ref=ins[a] if gather else ins[a].at[me], dst_ref=outs[a].at[peer], device_id=(x, y, c), **sems))
    return local, sends, recvs


def _xchg_start(cps):
    local, sends, _ = cps
    for cp in local + sends:
        cp.start()


def _xchg_finish(cps):
    local, sends, recvs = cps
    for cp in recvs:
        cp.wait_recv()
    for cp in sends:
        cp.wait_send()
    for cp in local:
        cp.wait()


def _xchg_out_shapes(ins, gather):
    return [jax.ShapeDtypeStruct(((N_DEV,) + s.shape) if gather else s.shape, s.dtype) for s in ins]


def _hosted_call(body, args, hosted, gather, *, grid, in_specs, out_specs, out_shape, scratch_shapes=(), **kw):
    na = len(hosted)
    if na == 0:
        return pl.pallas_call(body, grid=grid, in_specs=in_specs, out_specs=out_specs, out_shape=out_shape,
                              scratch_shapes=list(scratch_shapes), **kw)(*args)
    n_in, n_out, n_scr = len(in_specs), len(out_specs), len(scratch_shapes)

    def wrapped(*refs):
        ins, h_in = refs[:n_in], refs[n_in:n_in + na]
        outs, h_out = refs[n_in + na:n_in + na + n_out], refs[n_in + na + n_out:n_in + 2 * na + n_out]
        scr = refs[n_in + 2 * na + n_out:]
        cps = _xchg_copies(h_in, h_out, *scr[n_scr:], gather=gather)
        pids = [pl.program_id(a) for a in range(len(grid))]
        first = functools.reduce(jnp.logical_and, [p == 0 for p in pids])
        last = functools.reduce(jnp.logical_and, [p == g - 1 for p, g in zip(pids, grid)])

        @pl.when(first)
        def _():
            _xchg_start(cps)

        body(*ins, *outs, *scr[:n_scr])

        @pl.when(last)
        def _():
            _xchg_finish(cps)

    anyspec = pl.BlockSpec(memory_space=pl.ANY)
    return pl.pallas_call(
        wrapped, grid=grid, in_specs=list(in_specs) + [anyspec] * na, out_specs=list(out_specs) + [anyspec] * na,
        out_shape=list(out_shape) + _xchg_out_shapes(hosted, gather),
        scratch_shapes=list(scratch_shapes) + _xchg_scratch(na), **kw)(*args, *hosted)


def _attn_specs(E):
    nb = (E - CTX) // WIN
    last = E // WIN - 1
    kc = pl.BlockSpec((CTX, KP), lambda n: (0, 0))
    kp = pl.BlockSpec((WIN, KP), lambda n: (n + 1, 0))
    kk = pl.BlockSpec((WIN, KP), lambda n: (n + 2, 0))
    kn = pl.BlockSpec((WIN, KP), lambda n: (jnp.minimum(n + 3, last), 0))
    return nb, [kc, kp, kk, kn]


def _attn_bias(nb):
    rows = np.arange(GRP * WIN)[:, None] % WIN
    cols = np.arange(CTX + 3 * WIN)[None, :]
    j = cols - CTX
    band = np.abs(j - WIN - rows) <= WIN
    out = []
    for first, last in ((True, False), (False, False), (False, True)):
        ok = (cols < CTX) | (band & ((j >= WIN) | (not first)) & ((j < 2 * WIN) | (not last)))
        out.append(np.where(ok, 0.0, NEG).astype(np.float32))
    bias = jnp.asarray(np.stack(out))
    spec = pl.BlockSpec((1, GRP * WIN, CTX + 3 * WIN),
                        lambda n: (jnp.where(n == 0, 0, jnp.where(n == nb - 1, 2, 1)), 0, 0))
    return bias, spec


def _both_halves(t, h):
    tf = t.astype(F32)
    r = pltpu.roll(tf, HD, 1)
    lo = lax.broadcasted_iota(jnp.int32, tf.shape, 1) < HD
    return (jnp.where(lo, tf, r) if h == 0 else jnp.where(lo, r, tf)).astype(t.dtype)


def _stack_heads(ref, h):
    lo = lax.broadcasted_iota(jnp.int32, (WIN, 128), 1) < HD
    parts = []
    for g in range(GRP):
        j = GRP * h + g
        t = ref[:, 128 * (j // 2):128 * (j // 2) + 128].astype(F32)
        parts.append(jnp.where(lo if j % 2 == 0 else jnp.logical_not(lo), t, 0.0))
    return jnp.concatenate(parts, axis=0)


def _unstack_pair(o, pp):
    lo = lax.broadcasted_iota(jnp.int32, (WIN, 128), 1) < HD
    return jnp.where(lo, o[WIN * 2 * pp:WIN * 2 * pp + WIN], o[WIN * (2 * pp + 1):WIN * (2 * pp + 1) + WIN])


def _attn_fwd(q, k, v, sink, shards):
    E = q.shape[0]
    S = E - CTX
    nb, kspecs = _attn_specs(E)
    na = len(shards)

    def body(q_ref, kc, kp, kk, kn, vc, vp, vk, vn, sink_ref, bias_ref, *rest):
        shard_refs, (o_ref, lse_ref), got_refs = rest[:na], rest[na:na + 2], rest[na + 2:2 * na + 2]
        n = pl.program_id(0)
        cps = _xchg_copies(shard_refs, got_refs, *rest[2 * na + 2:], gather=True)

        @pl.when(n == 0)
        def _():
            _xchg_start(cps)

        lane = lax.broadcasted_iota(jnp.int32, (WIN, 128), 1)
        lse_t = jnp.zeros((WIN, 128), F32)
        kall = jnp.concatenate([kc[...], kp[...], kk[...], kn[...]], axis=0)
        vall = jnp.concatenate([vc[...], vp[...], vk[...], vn[...]], axis=0)
        K = [_both_halves(kall, h) for h in range(N_KV)]
        Q = [_stack_heads(q_ref, h).astype(_BF) for h in range(N_KV)]
        sk = [jnp.concatenate([jnp.broadcast_to(sink_ref[GRP * h + g:GRP * h + g + 1, 0:1], (WIN, 1))
                               for g in range(GRP)], axis=0) for h in range(N_KV)]
        s = [_nt(Q[h], K[h]) + bias_ref[0] for h in range(N_KV)]
        m = [jnp.maximum(jnp.max(s[h], axis=1, keepdims=True), sk[h]) for h in range(N_KV)]
        e = [jnp.exp(s[h] - m[h]) for h in range(N_KV)]
        den = [jnp.sum(e[h], axis=1, keepdims=True) + jnp.exp(sk[h] - m[h]) for h in range(N_KV)]
        V = [_both_halves(vall, h) for h in range(N_KV)]
        o = [_nn((e[h] * (1.0 / den[h])).astype(_BF), V[h]) for h in range(N_KV)]
        for h in range(N_KV):
            lse = m[h] + jnp.log(den[h])
            for g in range(GRP):
                lse_t = jnp.where(lane == GRP * h + g, lse[WIN * g:WIN * g + WIN], lse_t)
            for pp in range(GRP // 2):
                t = 2 * h + pp
                o_ref[:, 128 * t:128 * t + 128] = _unstack_pair(o[h], pp).astype(_BF)
        lse_ref[...] = lse_t

        @pl.when(n == nb - 1)
        def _():
            _xchg_finish(cps)

    qs = pl.BlockSpec((WIN, QP), lambda n: (n + 2, 0))
    anyspec = pl.BlockSpec(memory_space=pl.ANY)
    bias, bias_spec = _attn_bias(nb)
    return pl.pallas_call(
        body, name="attn_fwd", grid=(nb,),
        in_specs=[qs] + kspecs + kspecs + [_full((8, 128)), bias_spec] + [anyspec] * na,
        out_specs=[_rows(WIN, 512), _rows(WIN, 128)] + [anyspec] * na,
        out_shape=[jax.ShapeDtypeStruct((S, 512), _BF), jax.ShapeDtypeStruct((S, 128), F32)]
        + _xchg_out_shapes(shards, True),
        scratch_shapes=_xchg_scratch(na),
        compiler_params=_cp(("arbitrary",)),
    )(q, k, k, k, k, v, v, v, v, sink, bias, *shards)


def _attn_bwd(q, k, v, sink, lse, d_attn, slabs):
    E = q.shape[0]
    S = E - CTX
    nb, kspecs = _attn_specs(E)
    last = E // WIN - 1
    na = len(slabs)

    def body(q_ref, kc, kp, kk, kn, vc, vp, vk, vn, sink_ref, bias_ref, lse_ref, do_ref, *rest):
        slab_refs, (dq_ref, dk_ref, dv_ref, ds_ref), got_refs = rest[:na], rest[na:na + 4], rest[na + 4:2 * na + 4]
        n = pl.program_id(0)
        cps = _xchg_copies(slab_refs, got_refs, *rest[2 * na + 4:], gather=False)

        @pl.when(n == 0)
        def _():
            _xchg_start(cps)
            dk_ref[...] = jnp.zeros_like(dk_ref)
            dv_ref[...] = jnp.zeros_like(dv_ref)
            ds_ref[...] = jnp.zeros_like(ds_ref)

        lane = lax.broadcasted_iota(jnp.int32, (WIN, 128), 1)
        lse_t = lse_ref[...]
        starts = [None, pl.multiple_of((n + 1) * WIN, WIN), pl.multiple_of((n + 2) * WIN, WIN),
                  pl.multiple_of(jnp.minimum(n + 3, last) * WIN, WIN)]
        kall = jnp.concatenate([kc[...], kp[...], kk[...], kn[...]], axis=0)
        vall = jnp.concatenate([vc[...], vp[...], vk[...], vn[...]], axis=0)
        for h in range(N_KV):
            hs = slice(HD * h, HD * h + HD)
            K = _both_halves(kall, h)
            V = _both_halves(vall, h)
            Q = _stack_heads(q_ref, h).astype(_BF)
            sk = jnp.concatenate([jnp.broadcast_to(sink_ref[GRP * h + g:GRP * h + g + 1, 0:1], (WIN, 1))
                                  for g in range(GRP)], axis=0)
            ls = jnp.concatenate([jnp.sum(jnp.where(lane == GRP * h + g, lse_t, 0.0), axis=1, keepdims=True)
                                  for g in range(GRP)], axis=0)
            do = _stack_heads(do_ref, h).astype(_BF)
            p = jnp.exp(_nt(Q, K) + bias_ref[0] - ls)
            dp = _nt(do, V)
            delta = jnp.sum(p * dp, axis=1, keepdims=True)
            dsc = (p * (dp - delta)).astype(_BF)
            dq = _nn(dsc, K) * (HD ** -0.5)
            for pp in range(GRP // 2):
                t = 2 * h + pp
                dq_ref[:, 128 * t:128 * t + 128] = _unstack_pair(dq, pp).astype(_BF)
            dK2 = _tn(Q, dsc)
            dV2 = _tn(do, p.astype(_BF))
            dK = dK2[0:HD] + dK2[HD:2 * HD]
            dV = dV2[0:HD] + dV2[HD:2 * HD]
            dk_ref[hs, 0:CTX] += dK[:, 0:CTX]
            dv_ref[hs, 0:CTX] += dV[:, 0:CTX]
            for w in range(1, 4):
                lo = CTX + WIN * (w - 1)
                dk_ref[hs, pl.ds(starts[w], WIN)] += dK[:, lo:lo + WIN]
                dv_ref[hs, pl.ds(starts[w], WIN)] += dV[:, lo:lo + WIN]
            psk = -jnp.exp(sk - ls) * delta
            for g in range(GRP):
                j = GRP * h + g
                tot = jnp.sum(psk[WIN * g:WIN * g + WIN], axis=0, keepdims=True)
                ds_ref[j:j + 1, :] += jnp.broadcast_to(tot, (1, 128))

        @pl.when(n == nb - 1)
        def _():
            _xchg_finish(cps)

    qs = pl.BlockSpec((WIN, QP), lambda n: (n + 2, 0))
    anyspec = pl.BlockSpec(memory_space=pl.ANY)
    bias, bias_spec = _attn_bias(nb)
    return pl.pallas_call(
        body, name="attn_bwd", grid=(nb,),
        in_specs=[qs] + kspecs + kspecs + [_full((8, 128)), bias_spec, _rows(WIN, 128), _rows(WIN, 512)]
        + [anyspec] * na,
        out_specs=[_rows(WIN, QP), _full((KP, E)), _full((KP, E)), _full((8, 128))] + [anyspec] * na,
        out_shape=[jax.ShapeDtypeStruct((S, QP), _BF), jax.ShapeDtypeStruct((KP, E), F32),
                   jax.ShapeDtypeStruct((KP, E), F32), jax.ShapeDtypeStruct((8, 128), F32)]
        + _xchg_out_shapes(slabs, False),
        scratch_shapes=_xchg_scratch(na),
        compiler_params=_cp(("arbitrary",), 48 * 1024 * 1024),
    )(q, k, k, k, k, v, v, v, v, sink, bias, lse, d_attn, *slabs)


def _gla_order(E, reverse, backward):
    nc = CTX // GLA_T
    n = E // GLA_T
    if not reverse:
        fwd = lambda s: s
    else:
        fwd = lambda s: jnp.where(s < nc, nc - 1 - s, n - 1 + nc - s)
    if backward:
        return lambda s: fwd(n - 1 - s)
    return fwd


def _gla_masks():
    T = GLA_T
    l128 = lax.broadcasted_iota(jnp.int32, (1, 128), 1)
    qmask = [((l128 >> 5) == j).astype(F32) for j in range(4)]
    vmask = [((l128 >> 6) == j).astype(F32) for j in range(2)]
    bd = ((lax.broadcasted_iota(jnp.int32, (512, 256), 0) >> 6)
          == (lax.broadcasted_iota(jnp.int32, (512, 256), 1) >> 5)).astype(F32)
    ri = lax.broadcasted_iota(jnp.int32, (T, 2 * T), 0)
    ci = lax.broadcasted_iota(jnp.int32, (T, 2 * T), 1) & (T - 1)
    return qmask, vmask, bd, ri, ci


def _tri_sum(tri, x):
    hi = x.astype(_BF)
    lo = (x - hi.astype(F32)).astype(_BF)
    n = x.shape[1]
    y = _nn(tri.astype(_BF), jnp.concatenate([hi, lo], axis=1))
    return y[:, 0:n] + y[:, n:2 * n]


def _gla_decays(la, reverse, ri, ci):
    T = GLA_T
    msk2 = (ri <= ci) if reverse else (ri >= ci)
    mskT2 = (ri >= ci) if reverse else (ri <= ci)
    b = _tri_sum(msk2[:, 0:T], la)
    bT = b[0:1] if reverse else b[T - 1:T]
    bm = b[T // 2:T // 2 + 1]
    return msk2, mskT2, b, bT, bm


def _pair_stack(tile, m0, m1):
    return jnp.concatenate([(tile * m0).astype(_BF), (tile * m1).astype(_BF)], axis=0)


def _gla_fwd(gq, gk, gv, la, reverse, shards=()):
    E = gq.shape[0]
    T = GLA_T
    n = E // T
    order = _gla_order(E, reverse, False)
    col = 1 if reverse else 0

    def body(gq_ref, gk_ref, gv_ref, la_ref, o_ref, st_ref, S_scr):
        @pl.when(pl.program_id(0) == 0)
        def _():
            S_scr[...] = jnp.zeros_like(S_scr)

        qmask, vmask, bd, ri, ci = _gla_masks()
        msk2, _, b, bT, bm = _gla_decays(la_ref[...], reverse, ri, ci)
        q, k, v = gq_ref[...], gk_ref[...], gv_ref[...]
        qd = (q * jnp.exp(b)).astype(_BF)
        qm = (q * jnp.exp(b - bm)).astype(_BF)
        km = k * jnp.exp(bm - b)
        kd = (k * jnp.exp(bT - b)).astype(_BF)
        ST = S_scr[...]
        comp = ST[0:DV]
        for h in range(1, N_GLA):
            comp = comp + ST[DV * h:DV * h + DV]
        st_ref[0] = comp
        inter = _nt(qd, ST.astype(_BF))
        tiles = []
        for p in range(N_GLA // 2):
            qs = slice(128 * (p // 2), 128 * (p // 2) + 128)
            vs = slice(128 * p, 128 * p + 128)
            j0 = (2 * p) % 4
            KS = _pair_stack(km[:, qs], qmask[j0], qmask[j0 + 1])
            VS = _pair_stack(v[:, vs], vmask[0], vmask[1])
            AA = jnp.where(msk2, _nt(qm[:, qs], KS), 0.0).astype(_BF)
            tiles.append(_nn(AA, VS))
        o_ref[...] = inter + jnp.concatenate(tiles, axis=1)
        S_scr[...] = ST * jnp.exp(bT) + bd * _tn(v.astype(_BF), kd)

    blk = lambda w, c=0: pl.BlockSpec((T, w), lambda s: (order(s), c))
    return _hosted_call(
        body, (gq, gk, gv, la), shards, True,
        name="gla_fwd_rev" if reverse else "gla_fwd", grid=(n,),
        in_specs=[blk(256), blk(256), blk(512), blk(256, col)],
        out_specs=[blk(512), pl.BlockSpec((1, DV, 256), lambda s: (order(s), 0, 0))],
        out_shape=[jax.ShapeDtypeStruct((E, 512), F32), jax.ShapeDtypeStruct((n, DV, 256), F32)],
        scratch_shapes=[pltpu.VMEM((512, 256), F32)],
        compiler_params=_cp(("arbitrary",)))


def _gla_bwd(gq, gk, gv, la, st, do, reverse, slabs=()):
    E = gq.shape[0]
    T = GLA_T
    n = E // T
    nc = CTX // T
    order = _gla_order(E, reverse, True)
    col = 1 if reverse else 0

    def body(gq_ref, gk_ref, gv_ref, la_ref, st_ref, do_ref, dq_ref, dk_ref, dv_ref, dla_ref, dS_scr):
        @pl.when(pl.program_id(0) == 0)
        def _():
            dS_scr[...] = jnp.zeros_like(dS_scr)

        is_lat = order(pl.program_id(0)) >= nc
        qmask, vmask, bd, ri, ci = _gla_masks()
        msk2, mskT2, b, bT, bm = _gla_decays(la_ref[...], reverse, ri, ci)
        q, k, v = gq_ref[...], gk_ref[...], gv_ref[...]
        do = jnp.where(is_lat, do_ref[...].astype(F32), 0.0)
        e_b, e_qm, e_km, e_kd, e_T = jnp.exp(b), jnp.exp(b - bm), jnp.exp(bm - b), jnp.exp(bT - b), jnp.exp(bT)
        qd, qm, km, kd = q * e_b, q * e_qm, k * e_km, k * e_kd
        qdb, qmb, kmb, kdb, vb, dob = (t.astype(_BF) for t in (qd, qm, km, kd, v, do))
        ST = jnp.tile(st_ref[0], (N_GLA, 1)) * bd
        dST = dS_scr[...]
        dSTb = dST.astype(_BF)
        dqd = _nn(dob, ST.astype(_BF))
        dkd = _nn(vb, dSTb)
        dv_t, dqm_t, dkm_t = [], [None, None], [None, None]
        for p in range(N_GLA // 2):
            t = p // 2
            qs = slice(128 * t, 128 * t + 128)
            vs = slice(128 * p, 128 * p + 128)
            j0 = (2 * p) % 4
            QS = _pair_stack(qm[:, qs], qmask[j0], qmask[j0 + 1])
            KS = _pair_stack(km[:, qs], qmask[j0], qmask[j0 + 1])
            VS = _pair_stack(v[:, vs], vmask[0], vmask[1])
            DS = _pair_stack(do[:, vs], vmask[0], vmask[1])
            ATT = jnp.where(mskT2, _nt(kmb[:, qs], QS), 0.0).astype(_BF)
            dAA = jnp.where(msk2, _nt(dob[:, vs], VS), 0.0).astype(_BF)
            dATT = jnp.where(mskT2, _nt(vb[:, vs], DS), 0.0).astype(_BF)
            dv_t.append(_nn(ATT, DS))
            dq_p = _nn(dAA, KS)
            dk_p = _nn(dATT, QS)
            dqm_t[t] = dq_p if dqm_t[t] is None else dqm_t[t] + dq_p
            dkm_t[t] = dk_p if dkm_t[t] is None else dkm_t[t] + dk_p
        dqm = jnp.concatenate(dqm_t, axis=1)
        dkm = jnp.concatenate(dkm_t, axis=1)
        dq_ref[...] = dqm * e_qm + dqd * e_b
        dk_ref[...] = dkm * e_km + dkd * e_kd
        dv_ref[...] = _nt(kdb, dSTb) + jnp.concatenate(dv_t, axis=1)
        db = dqm * qm - dkm * km + dqd * qd - dkd * kd
        dbT = jnp.sum(dkd * kd, axis=0, keepdims=True) + e_T * jnp.sum(dST * ST, axis=0, keepdims=True)
        dla_ref[...] = _tri_sum(mskT2[:, 0:T], db) + dbT
        dS_scr[...] = dST * e_T + bd * _tn(dob, qdb)

    blk = lambda w, c=0: pl.BlockSpec((T, w), lambda s: (order(s), c))
    do_spec = pl.BlockSpec((T, 512), lambda s: (jnp.maximum(order(s) - nc, 0), 0))
    return _hosted_call(
        body, (gq, gk, gv, la, st, do), slabs, False,
        name="gla_bwd_rev" if reverse else "gla_bwd", grid=(n,),
        in_specs=[blk(256), blk(256), blk(512), blk(256, col),
                  pl.BlockSpec((1, DV, 256), lambda s: (order(s), 0, 0)), do_spec],
        out_specs=[blk(256), blk(256), blk(512), blk(256)],
        out_shape=[jax.ShapeDtypeStruct((E, 256), F32), jax.ShapeDtypeStruct((E, 256), F32),
                   jax.ShapeDtypeStruct((E, 512), F32), jax.ShapeDtypeStruct((E, 256), F32)],
        scratch_shapes=[pltpu.VMEM((512, 256), F32)],
        compiler_params=_cp(("arbitrary",)))


def _gla_out(o_f, o_b, gg, ggla, mavg):
    o = o_f + o_b
    rr = lax.rsqrt(_head_mean(o * o, mavg) + EPS)
    oh = o * rr
    sg = _sigmoid(gg)
    return oh, rr, sg


def _mix_fwd(x, attn, o_f, o_b, gg, ggla, mavg, wout, gt1, g2):
    S = x.shape[0]
    TM = 256

    def body(x_ref, a_ref, of_ref, ob_ref, gg_ref, ggla_ref, mavg_ref, w_ref, gt1_ref, g2_ref, x1_ref, mix_ref):
        gg_t = gg_ref[...]
        oh, _, sg = _gla_out(of_ref[...], ob_ref[...], gg_t, ggla_ref[...], mavg_ref[...])
        mix_ref[:, 0:512] = a_ref[...]
        mix_ref[:, 512:1024] = (oh * ggla_ref[...] * (gg_t * sg)).astype(_BF)
        y = _nn(mix_ref[...], w_ref[...])
        ry = lax.rsqrt(jnp.mean(y * y, axis=-1, keepdims=True) + EPS)
        x1_ref[...] = x_ref[...] + gt1_ref[...] * ((y * ry) * g2_ref[...])

    return pl.pallas_call(
        body, name="mix_fwd", grid=(S // TM,),
        in_specs=[_rows(TM, D), _rows(TM, 512), _rows(TM, 512, 1), _rows(TM, 512, 1), _rows(TM, 512, 1),
                  _full((1, 512)), _full((512, 512)), _full((D, D)), _full((1, D)), _full((1, D))],
        out_specs=[_rows(TM, D), _rows(TM, D)],
        out_shape=[jax.ShapeDtypeStruct((S, D), F32), jax.ShapeDtypeStruct((S, D), _BF)],
        compiler_params=_cp(("arbitrary",), 40 * 1024 * 1024),
    )(x, attn, o_f, o_b, gg, ggla, mavg, wout, gt1, g2)


def _mix_bwd(dx1, mix, o_f, o_b, gg, ggla, mavg, wout, gt1, g2):
    S = dx1.shape[0]
    TM = 256

    def body(dx_ref, mix_ref, of_ref, ob_ref, gg_ref, ggla_ref, mavg_ref, w_ref, gt1_ref, g2_ref,
             da_ref, do_ref, dgg_ref, dy_ref, sums_ref):
        @pl.when(pl.program_id(0) == 0)
        def _():
            sums_ref[...] = jnp.zeros_like(sums_ref)

        dx = dx_ref[...]
        y = _nn(mix_ref[...], w_ref[...])
        ry = lax.rsqrt(jnp.mean(y * y, axis=-1, keepdims=True) + EPS)
        yh = y * ry
        sums_ref[0:1, :] += jnp.sum(dx * yh, axis=0, keepdims=True)
        dyh = dx * (gt1_ref[...] * g2_ref[...])
        dy = (ry * (dyh - yh * jnp.mean(dyh * yh, axis=-1, keepdims=True))).astype(_BF)
        dy_ref[...] = dy
        dmix = _nt(dy, w_ref[...])
        da_ref[...] = dmix[:, 0:512].astype(_BF)
        dgla = dmix[:, 512:1024]
        gg_t = gg_ref[...]
        ggla_t = ggla_ref[...]
        oh, rr, sg = _gla_out(of_ref[...], ob_ref[...], gg_t, ggla_t, mavg_ref[...])
        dgg_ref[...] = (dgla * oh * ggla_t * (sg * (1.0 + gg_t * (1.0 - sg)))).astype(_BF)
        don = dgla * (gg_t * sg)
        sums_ref[1:2, 0:512] += jnp.sum(don * oh, axis=0, keepdims=True)
        doh = don * ggla_t
        do_ref[...] = (rr * (doh - oh * _head_mean(doh * oh, mavg_ref[...]))).astype(_BF)

    return pl.pallas_call(
        body, name="mix_bwd", grid=(S // TM,),
        in_specs=[_rows(TM, D), _rows(TM, D), _rows(TM, 512, 1), _rows(TM, 512, 1), _rows(TM, 512, 1),
                  _full((1, 512)), _full((512, 512)), _full((D, D)), _full((1, D)), _full((1, D))],
        out_specs=[_rows(TM, 512), _rows(TM, 512), _rows(TM, 512), _rows(TM, D), _full((8, D))],
        out_shape=[jax.ShapeDtypeStruct((S, 512), _BF), jax.ShapeDtypeStruct((S, 512), _BF),
                   jax.ShapeDtypeStruct((S, 512), _BF), jax.ShapeDtypeStruct((S, D), _BF),
                   jax.ShapeDtypeStruct((8, D), F32)],
        compiler_params=_cp(("arbitrary",), 40 * 1024 * 1024),
    )(dx1, mix, o_f, o_b, gg, ggla, mavg, wout, gt1, g2)


def _ffn(x1, target, gm2, sh2, gt2, g4, wffi, wffo):
    S = x1.shape[0]
    TF = 256

    def body(x_ref, t_ref, gm_ref, sh_ref, gt_ref, g4_ref, wi_hbm, wo_hbm,
             dx_ref, h_ref, du_ref, act_ref, df_ref, sums_ref, loss_ref, wi, wo, sem):
        @pl.when(pl.program_id(0) == 0)
        def _():
            c1 = pltpu.make_async_copy(wi_hbm, wi, sem.at[0])
            c2 = pltpu.make_async_copy(wo_hbm, wo, sem.at[1])
            c1.start()
            c2.start()
            sums_ref[...] = jnp.zeros_like(sums_ref)
            loss_ref[...] = jnp.zeros_like(loss_ref)
            c1.wait()
            c2.wait()

        x = x_ref[...]
        gm = gm_ref[...]
        r = lax.rsqrt(jnp.mean(x * x, axis=-1, keepdims=True) + EPS)
        xh = x * r
        hb = (xh * gm + sh_ref[...]).astype(_BF)
        h_ref[...] = hb
        u = _nt(hb, wi[...])
        g = u[:, 0:FFN]
        up = u[:, FFN:2 * FFN]
        sg = _sigmoid(g)
        sl = g * sg
        ab = (sl * up).astype(_BF)
        act_ref[...] = ab
        f = _nn(ab, wo[...])
        rf = lax.rsqrt(jnp.mean(f * f, axis=-1, keepdims=True) + EPS)
        fh = f * rf
        gt, g4v = gt_ref[...], g4_ref[...]
        err = x + gt * (fh * g4v) - t_ref[...]
        loss_ref[...] += jnp.sum(err * err) * (0.5 / D)
        dout = err * (1.0 / D)
        sums_ref[2:3, :] += jnp.sum(dout * fh, axis=0, keepdims=True)
        dfh = dout * (gt * g4v)
        dfb = (rf * (dfh - fh * jnp.mean(dfh * fh, axis=-1, keepdims=True))).astype(_BF)
        df_ref[...] = dfb
        dact = _nt(dfb, wo[...])
        du_ref[:, 0:FFN] = (dact * up * (sg * (1.0 + g * (1.0 - sg)))).astype(_BF)
        du_ref[:, FFN:2 * FFN] = (dact * sl).astype(_BF)
        dh = _nn(du_ref[...], wi[...])
        sums_ref[0:1, :] += jnp.sum(dh, axis=0, keepdims=True)
        sums_ref[1:2, :] += jnp.sum(dh * xh, axis=0, keepdims=True)
        dxh = dh * gm
        dx_ref[...] = dout + r * (dxh - xh * jnp.mean(dxh * xh, axis=-1, keepdims=True))

    vec = _full((1, D))
    anyspec = pl.BlockSpec(memory_space=pl.ANY)
    return pl.pallas_call(
        body, name="ffn_fwd_bwd", grid=(S // TF,),
        in_specs=[_rows(TF, D), _rows(TF, D), vec, vec, vec, vec, anyspec, anyspec],
        out_specs=[_rows(TF, D), _rows(TF, D), _rows(TF, 2 * FFN), _rows(TF, FFN), _rows(TF, D),
                   _full((8, D)), _full((8, 128))],
        out_shape=[jax.ShapeDtypeStruct((S, D), F32), jax.ShapeDtypeStruct((S, D), _BF),
                   jax.ShapeDtypeStruct((S, 2 * FFN), _BF), jax.ShapeDtypeStruct((S, FFN), _BF),
                   jax.ShapeDtypeStruct((S, D), _BF), jax.ShapeDtypeStruct((8, D), F32),
                   jax.ShapeDtypeStruct((8, 128), F32)],
        scratch_shapes=[pltpu.VMEM((2 * FFN, D), _BF), pltpu.VMEM((FFN, D), _BF), pltpu.SemaphoreType.DMA((2,))],
        compiler_params=_cp(("arbitrary",), VMEM_BIG),
    )(x1, target, gm2, sh2, gt2, g4, wffi, wffo)


def _inproj_bwd(x, ctx, gml, gmc, win, wg, cos, sa, sb, la, dq, dk, dv, dgq_f, dgq_b, dgk_f, dgk_b, dgv_f, dgv_b,
                dgg, dla_f, dla_b, dx1):
    S = x.shape[0]
    E = S + CTX
    TE = CTX

    def body(x_ref, c_ref, gml_ref, gmc_ref, w_ref, wg_ref, cos_ref, sa_ref, sb_ref, la_ref, dq_ref, dk_ref, dv_ref,
             gqf, gqb, gkf, gkb, gvf, gvb, dgg_ref, dlf, dlb, dx1_ref,
             dp_ref, dlg_ref, gx_ref, sums_ref, bsum_ref):
        i = pl.program_id(0)
        is_ctx = i == 0

        @pl.when(is_ctx)
        def _():
            sums_ref[...] = jnp.zeros_like(sums_ref)
            bsum_ref[...] = jnp.zeros_like(bsum_ref)

        lat = jnp.where(is_ctx, 0.0, 1.0)
        cos_t, sa_t, sb_t = cos_ref[...], sa_ref[...], sb_ref[...]
        dp_ref[:, O_Q:O_K] = (_unrope(dq_ref[...].astype(F32), cos_t, sa_t, sb_t) * lat).astype(_BF)
        dp_ref[:, O_K:O_V] = _unrope(dk_ref[...].T, cos_t, sa_t, sb_t).astype(_BF)
        dp_ref[:, O_V:O_GQ] = dv_ref[...].T.astype(_BF)
        dp_ref[:, O_GQ:O_GK] = ((gqf[...] + gqb[...]) * (DK ** -0.5)).astype(_BF)
        dp_ref[:, O_GK:O_GV] = (gkf[...] + gkb[...]).astype(_BF)
        dp_ref[:, O_GV:O_GG] = (gvf[...] + gvb[...]).astype(_BF)
        dp_ref[:, O_GG:O_Z] = (dgg_ref[...].astype(F32) * lat).astype(_BF)
        la_t = la_ref[...]
        dlg = (jnp.concatenate([dlf[...], dlb[...]], axis=1) * (1.0 - jnp.exp(GATE_TAU * la_t)) * (1.0 / GATE_TAU))
        bsum_ref[0:1, :] += jnp.sum(dlg, axis=0, keepdims=True)
        dlgb = dlg.astype(_BF)
        dlg_ref[...] = dlgb
        dp_ref[:, O_Z:NP] = _nt(dlgb, wg_ref[...]).astype(_BF)
        dh = _nn(dp_ref[...], w_ref[...])
        x = jnp.where(is_ctx, c_ref[...], x_ref[...])
        r = lax.rsqrt(jnp.mean(x * x, axis=-1, keepdims=True) + EPS)
        xh = x * r
        sdh = jnp.sum(dh, axis=0, keepdims=True)
        sdx = jnp.sum(dh * xh, axis=0, keepdims=True)
        sums_ref[0:1, :] += sdh * lat
        sums_ref[1:2, :] += sdx * lat
        sums_ref[2:3, :] += sdh * (1.0 - lat)
        sums_ref[3:4, :] += sdx * (1.0 - lat)
        dxh = dh * jnp.where(is_ctx, gmc_ref[...], gml_ref[...])
        gx_ref[...] = dx1_ref[...] + r * (dxh - xh * jnp.mean(dxh * xh, axis=-1, keepdims=True))

    vec = _full((1, D))
    tab = _rows(TE, 128)
    return pl.pallas_call(
        body, name="inproj_bwd", grid=(E // TE,),
        in_specs=[_rows_lat(TE, D), _full((CTX, D)), vec, vec, _full((NP, D)), _full((128, 512)), tab, tab, tab,
                  _rows(TE, 512),
                  _rows_lat(TE, QP), pl.BlockSpec((KP, TE), lambda i: (0, i)), pl.BlockSpec((KP, TE), lambda i: (0, i)),
                  _rows(TE, 256), _rows(TE, 256), _rows(TE, 256), _rows(TE, 256), _rows(TE, 512), _rows(TE, 512),
                  _rows_lat(TE, 512), _rows(TE, 256), _rows(TE, 256), _rows_lat(TE, D)],
        out_specs=[_rows(TE, NP), _rows(TE, 512), _rows_lat(TE, D), _full((8, D)), _full((8, 512))],
        out_shape=[jax.ShapeDtypeStruct((E, NP), _BF), jax.ShapeDtypeStruct((E, 512), _BF),
                   jax.ShapeDtypeStruct((S, D), F32), jax.ShapeDtypeStruct((8, D), F32),
                   jax.ShapeDtypeStruct((8, 512), F32)],
        compiler_params=_cp(("arbitrary",), VMEM_BIG),
    )(x, ctx, gml, gmc, win, wg, cos, sa, sb, la, dq, dk, dv, dgq_f, dgq_b, dgk_f, dgk_b, dgv_f, dgv_b,
      dgg, dla_f, dla_b, dx1)


def _matmul_tn(a, b, tk, tt, name, out_dtype, transpose_out=False, shards=()):
    T, KA = a.shape
    N = b.shape[1]
    nt = T // tt

    def body(a_ref, b_ref, o_ref, acc):
        t = pl.program_id(1)

        @pl.when(t == 0)
        def _():
            acc[...] = jnp.zeros_like(acc)

        acc[...] += _tn(a_ref[...], b_ref[...])

        @pl.when(t == nt - 1)
        def _():
            o_ref[...] = (acc[...].T if transpose_out else acc[...]).astype(out_dtype)

    if transpose_out:
        out_spec, out_shape = pl.BlockSpec((N, tk), lambda i, t: (0, i)), (N, KA)
    else:
        out_spec, out_shape = pl.BlockSpec((tk, N), lambda i, t: (i, 0)), (KA, N)
    res = _hosted_call(
        body, (a, b), shards, True, name=name, grid=(KA // tk, nt),
        in_specs=[pl.BlockSpec((tt, tk), lambda i, t: (t, i)), pl.BlockSpec((tt, N), lambda i, t: (t, 0))],
        out_specs=[out_spec], out_shape=[jax.ShapeDtypeStruct(out_shape, out_dtype)],
        scratch_shapes=[pltpu.VMEM((tk, N), F32)],
        compiler_params=_cp(("arbitrary", "arbitrary"), VMEM_BIG))
    return res if shards else res[0]


def _ada_bwd(c_all, c_ctx, w_ada, d_all):
    n = w_ada.shape[1]

    def body(c_ref, cc_ref, w_ref, d_ref, gw_ref, t_ref):
        c = jnp.concatenate([c_ref[...], jnp.broadcast_to(cc_ref[...], (8, D))], axis=0)
        db = d_ref[...].astype(_BF)
        gw_ref[0] = _tn((c * _sigmoid(c)).astype(_BF), db)
        t_ref[...] = _nt(db[8:16], w_ref[...].astype(_BF))

    return pl.pallas_call(
        body, name="ada_bwd", in_specs=[_full((8, D)), _full((1, D)), _full((D, n)), _full((16, n))],
        out_specs=[_full((1, D, n)), _full((8, D))],
        out_shape=[jax.ShapeDtypeStruct((1, D, n), F32), jax.ShapeDtypeStruct((8, D), F32)], grid=(1,),
        compiler_params=_cp(("arbitrary",)),
    )(c_all, c_ctx, w_ada, d_all)


PART_ROWS = 56
R_ADA, R_ADA_C, R_GAIN, R_SINK, R_BG, R_GGLA, R_LOSS, R_WG = 0, 6, 12, 16, 17, 18, 19, 24


def _small_grads(s_in, s_ffn, s_mix, ada_l, ada_c, gains, dsink, s_bg, g_wg, loss):
    def body(si, sf, sm, al, ac, g, ds, sbg, gwg, loss_ref, o_ref):
        o_ref[...] = jnp.zeros_like(o_ref)
        o_ref[R_LOSS:R_LOSS + 1, 0:128] = loss_ref[0:1, :]
        sub = lax.broadcasted_iota(jnp.int32, (8, 128), 0)
        lane = lax.broadcasted_iota(jnp.int32, (8, 128), 1)
        o_ref[R_SINK:R_SINK + 1, 0:128] = jnp.sum(jnp.where(sub == lane, ds[...], 0.0), axis=0, keepdims=True)
        o_ref[R_BG:R_BG + 1, 0:512] = sbg[0:1, :]
        y = sm[1:2, 0:128] + sm[1:2, 128:256] + sm[1:2, 256:384] + sm[1:2, 384:512]
        y = y + pltpu.roll(y, 64, 1)
        o_ref[R_GGLA:R_GGLA + 1, 0:128] = jnp.where(lane[0:1] < DV, y, 0.0)
        o_ref[R_WG:R_WG + 16, 0:256] = gwg[0:16, 0:256]
        o_ref[R_WG + 16:R_WG + 32, 0:256] = gwg[16:32, 256:512]
        sdh_l, sdx_l, sdh_c, sdx_c = si[0:1], si[1:2], si[2:3], si[3:4]
        sdh2, sdx2, a2 = sf[0:1], sf[1:2], sf[2:3]
        a1 = sm[0:1]
        g1, g2, g3, g4 = g[0:1], g[1:2], g[2:3], g[3:4]
        sc1, gt1, sc2, gt2 = al[1:2], al[2:3], al[4:5], al[5:6]
        sc1c = ac[1:2]
        z = jnp.zeros((1, D), F32)
        rows = [sdh_l, sdx_l * g1, a1 * g2, sdh2, sdx2 * g3, a2 * g4,
                sdh_c, sdx_c * g1, z, z, z, z,
                sdx_l * (1.0 + sc1) + sdx_c * (1.0 + sc1c), a1 * gt1, sdx2 * (1.0 + sc2), a2 * gt2]
        for r, v in enumerate(rows):
            o_ref[r:r + 1, :] = v

    v8 = _full((8, D))
    return pl.pallas_call(
        body, name="small_grads",
        in_specs=[v8] * 6 + [_full((8, 128)), _full((8, 512)), _full((128, 512)), _full((8, 128))],
        out_specs=_full((PART_ROWS, D)), out_shape=jax.ShapeDtypeStruct((PART_ROWS, D), F32), grid=(1,),
        compiler_params=_cp(("arbitrary",)),
    )(s_in, s_ffn, s_mix, ada_l, ada_c, gains, dsink, s_bg, g_wg, loss)


def _row_tile(R):
    for cand in (256, 128, 64, 32, 16):
        if R % cand == 0 and R > cand:
            return cand
    return R


def _adamw(w, gs, m, v, name, hosted=(), gather=False):
    _, R, C = w.shape
    tr = _row_tile(R) if len(gs) == 1 else R
    c1 = 1.0 / (1.0 - ADAM_B1 ** ADAM_STEP)
    c2 = 1.0 / (1.0 - ADAM_B2 ** ADAM_STEP)
    ng = len(gs)

    def body(w_ref, *refs):
        g_refs, (m_ref, v_ref, go_ref, d_ref, nm_ref, nv_ref) = refs[:ng], refs[ng:]
        r0 = 0
        for g_ref in g_refs:
            rows = slice(r0, r0 + g_ref.shape[1])
            r0 += g_ref.shape[1]
            gg = g_ref[0].astype(F32)
            for j in range(1, g_ref.shape[0]):
                gg = gg + g_ref[j].astype(F32)
            go_ref[0, rows, :] = gg
            nm = ADAM_B1 * m_ref[0, rows, :] + (1.0 - ADAM_B1) * gg
            nv = ADAM_B2 * v_ref[0, rows, :] + (1.0 - ADAM_B2) * (gg * gg)
            nm_ref[0, rows, :] = nm
            nv_ref[0, rows, :] = nv
            d_ref[0, rows, :] = -ADAM_LR * ((nm * c1) / (jnp.sqrt(nv * c2) + ADAM_EPS) + ADAM_WD * w_ref[0, rows, :])

    spec = pl.BlockSpec((1, tr, C), lambda i: (0, i, 0))
    sds = jax.ShapeDtypeStruct((1, R, C), F32)
    g_specs = [pl.BlockSpec((g.shape[0], tr if ng == 1 else g.shape[1], C), lambda i: (0, i, 0)) for g in gs]
    return _hosted_call(
        body, (w, *gs, m, v), hosted, gather, name=name, grid=(R // tr,),
        in_specs=[spec] + g_specs + [spec, spec], out_specs=[spec] * 4, out_shape=[sds] * 4,
        compiler_params=_cp(("arbitrary",), 48 * 1024 * 1024))


def _sum_slots(slots, name):
    _, R, C = slots.shape
    tr = _row_tile(R)

    def body(s_ref, o_ref):
        acc = s_ref[0].astype(F32)
        for j in range(1, N_DEV):
            acc = acc + s_ref[j].astype(F32)
        o_ref[...] = acc

    return pl.pallas_call(
        body, name=name, grid=(R // tr,), in_specs=[pl.BlockSpec((N_DEV, tr, C), lambda i: (0, i, 0))],
        out_specs=_rows(tr, C), out_shape=jax.ShapeDtypeStruct((R, C), F32), compiler_params=_cp(("parallel",)),
    )(slots)


def _ag2_start(x_ref, out_ref, send_sems, recv_sems, local_sem):
    x, y, c = lax.axis_index("x"), lax.axis_index("y"), lax.axis_index("c")
    me, sibling = (x, y, c), (x, y, 1 - c)
    chips = [(1 - x, y), (x, 1 - y), (1 - x, 1 - y)]

    def rows(px, py, pc):
        return out_ref.at[4 * px + 2 * py + pc]

    def copy(k, block, to, src=None):
        return pltpu.make_async_remote_copy(
            src_ref=rows(*block) if src is None else src, dst_ref=rows(*block),
            send_sem=send_sems.at[k], recv_sem=recv_sems.at[k], device_id=to, device_id_type=MESH)

    mine = pltpu.make_async_copy(x_ref, rows(*me), local_sem)
    mine.start()
    first = [copy(0, me, sibling, src=x_ref)]
    first += [copy(1 + j, me, (*chip, c), src=x_ref) for j, chip in enumerate(chips)]
    for cp in first:
        cp.start()
    return copy, mine, first, me, sibling, chips, c


def _ag2_finish(state):
    copy, mine, first, me, sibling, chips, c = state
    passed = [copy(4 + j, (*chip, c), sibling) for j, chip in enumerate(chips)]
    for j, chip in enumerate(chips):
        copy(1 + j, (*chip, c), me).wait_recv()
        passed[j].start()
    copy(0, sibling, me).wait_recv()
    for j, chip in enumerate(chips):
        copy(4 + j, (*chip, 1 - c), me).wait_recv()
    for cp in first + passed:
        cp.wait_send()
    mine.wait()


def _entry(c, wg_sh, win_sh, c_ctx, w_ada):
    n = w_ada.shape[1]

    def body(c_ref, g_ref, w_ref, cc_ref, wa_ref, call_ref, gall_ref, wall_ref, ada_ref, part,
             s_send, s_recv, s_loc, w_send, w_recv, w_loc, a_send, a_recv, a_loc):
        big = _ag2_start(w_ref, wall_ref, w_send, w_recv, w_loc)
        small = _xchg_copies([c_ref, g_ref], [call_ref, gall_ref], s_send, s_recv, s_loc, gather=True)
        _xchg_start(small)
        _xchg_finish(small)
        cs = jnp.concatenate([call_ref[:, 0, :], jnp.broadcast_to(cc_ref[...], (8, D))], axis=0)
        part[...] = _nn((cs * _sigmoid(cs)).astype(_BF), wa_ref[...].astype(_BF))
        ada = _xchg_copies([part], [ada_ref], a_send, a_recv, a_loc, gather=True)
        _xchg_start(ada)
        _xchg_finish(ada)
        _ag2_finish(big)

    vm = pl.BlockSpec(memory_space=pltpu.VMEM)
    return pl.pallas_call(
        body, name="entry_gather",
        out_shape=[jax.ShapeDtypeStruct((N_DEV,) + c.shape, F32), jax.ShapeDtypeStruct((N_DEV,) + wg_sh.shape, F32),
                   jax.ShapeDtypeStruct((N_DEV,) + win_sh.shape, win_sh.dtype),
                   jax.ShapeDtypeStruct((N_DEV, 16, n), F32)],
        in_specs=[vm] * 5, out_specs=[vm] * 4,
        scratch_shapes=[pltpu.VMEM((16, n), F32)] + _xchg_scratch(2)
        + [pltpu.SemaphoreType.DMA((7,)), pltpu.SemaphoreType.DMA((7,)), pltpu.SemaphoreType.DMA] + _xchg_scratch(1),
        compiler_params=pltpu.CompilerParams(vmem_limit_bytes=VMEM_BIG),
    )(c, wg_sh, win_sh, c_ctx, w_ada)


def _rope_tables(S):
    t = np.arange(S)
    row = (t // GRID_W).astype(np.float32)
    colp = (t % GRID_W).astype(np.float32)
    half = HD // 2
    inv = (ROPE_BASE ** (-np.arange(0, half, 2, dtype=np.float32) / half)).astype(np.float32)
    ar = row[:, None] * inv[None, :]
    ac = colp[:, None] * inv[None, :]
    ang = np.concatenate([ar, ar, ac, ac], axis=-1).astype(np.float32)
    cos = np.cos(ang).astype(np.float32)
    sin = np.sin(ang).astype(np.float32)
    lane = np.arange(HD)
    first = (lane % 32) < 16
    sa = np.where(first[None, :], -sin, 0.0)
    sb = np.where(first[None, :], 0.0, sin)

    def ext(tab, ctx_val):
        full = np.zeros((CTX + S, 128), np.float32)
        full[:CTX, :] = ctx_val
        full[CTX:, :HD] = tab
        full[CTX:, HD:] = tab
        return jnp.asarray(full)

    return ext(cos, 1.0), ext(sa, 0.0), ext(sb, 0.0)


def _pad_rows_win(wt):
    return jnp.pad(wt, ((0, NP - IN_COLS), (0, 0)))


def _unpad_rows_win(g):
    return g[0:IN_COLS]


def _local_step(x, ctx, target, ada_l, ada_c, gains, sink, win_p, wg_bd, bg, ggla, wout_sh, wffi_sh, wffo_sh):
    S = x.shape[0]
    cos, sa, sb = _rope_tables(S)
    g1, g2, g3, g4 = (gains[i:i + 1] for i in range(4))
    sh1, sc1, gt1, sh2, sc2, gt2 = (ada_l[i:i + 1] for i in range(6))
    sh1c, sc1c = ada_c[0:1], ada_c[1:2]
    gml, gmc, gm2 = g1 * (1.0 + sc1), g1 * (1.0 + sc1c), g3 * (1.0 + sc2)
    mavg = jnp.asarray(np.kron(np.eye(N_GLA, dtype=np.float32), np.full((DV, DV), 1.0 / DV, np.float32))).astype(_BF)

    n_ffi, r_ffo, r_out = wffi_sh.shape[0], wffo_sh.shape[0], wout_sh.shape[0]
    tt_e = 768 if (S + CTX) % 768 == 0 else 256
    tt_s = 512 if S % 512 == 0 else 256
    h, q, k, v, gq, gk, gv, gg, z, la, wout_g = _inproj_fwd(x, ctx, gml, sh1, gmc, sh1c, win_p, wg_bd, bg,
                                                            cos, sa, sb, [wout_sh])
    attn, lse, wffi_g = _attn_fwd(q, k, v, sink, [wffi_sh])
    o_f, st_f, wffo_g = _gla_fwd(gq, gk, gv, la, False, [wffo_sh])
    o_b, st_b = _gla_fwd(gq, gk, gv, la, True)
    wout = wout_g.reshape(N_DEV * r_out, D)
    wffi = wffi_g.reshape(N_DEV * n_ffi, D)
    wffo = wffo_g.reshape(N_DEV * r_ffo, D)
    x1, mix = _mix_fwd(x, attn, o_f, o_b, gg, ggla, mavg, wout, gt1, g2)
    dx1, h2, du, act, df, s_ffn, loss = _ffn(x1, target, gm2, sh2, gt2, g4, wffi, wffo)
    slab_ffi = _matmul_tn(h2, du, 512, tt_s, "grad_w_ffn_in", _BF, True).reshape(N_DEV, n_ffi, D)
    slab_ffo = _matmul_tn(act, df, FFN, tt_s, "grad_w_ffn_out", _BF).reshape(N_DEV, r_ffo, D)
    d_attn, do_gla, dgg, dy, s_mix = _mix_bwd(dx1, mix, o_f, o_b, gg, ggla, mavg, wout, gt1, g2)
    slab_out = _matmul_tn(mix, dy, D, tt_s, "grad_w_out", _BF).reshape(N_DEV, r_out, D)
    dq, dk, dv, dsink, got_ffi, got_ffo = _attn_bwd(q, k, v, sink, lse, d_attn, [slab_ffi, slab_ffo])
    dgq_f, dgk_f, dgv_f, dla_f, got_out = _gla_bwd(gq, gk, gv, la, st_f, do_gla, False, [slab_out])
    dgq_b, dgk_b, dgv_b, dla_b = _gla_bwd(gq, gk, gv, la, st_b, do_gla, True)
    dp, dlg, grad_x, s_in, s_bg = _inproj_bwd(x, ctx, gml, gmc, win_p, wg_bd, cos, sa, sb, la, dq, dk, dv,
                                              dgq_f, dgq_b, dgk_f, dgk_b, dgv_f, dgv_b, dgg, dla_f, dla_b, dx1)
    g_wg = _matmul_tn(z, dlg, 128, tt_e, "grad_w_gate", F32)
    small = _small_grads(s_in, s_ffn, s_mix, ada_l, ada_c, gains, dsink, s_bg, g_wg, loss)
    g_win_t, parts = _matmul_tn(h, dp, 512, tt_e, "grad_w_in", _BF, True, [small])
    return dict(grad_x=grad_x, g_win_t=g_win_t, got_out=got_out, got_ffi=got_ffi, got_ffo=got_ffo, parts=parts)


SMALL_NAMES = ["c_ctx", "b_ada", "g_pre_mix", "g_post_mix", "g_pre_ffn", "g_post_ffn", "attn_sink",
               "b_gate_fwd", "b_gate_bwd", "g_gla_norm", "w_gate_fwd", "w_gate_bwd"]


def _small_update(tot, t_tot, wg_g, w, m, v):
    c1 = 1.0 / (1.0 - ADAM_B1 ** ADAM_STEP)
    c2 = 1.0 / (1.0 - ADAM_B2 ** ADAM_STEP)
    n = len(SMALL_NAMES)

    def body(tot_ref, t_ref, wg_ref, *refs):
        w_r, m_r, v_r = refs[0:n], refs[n:2 * n], refs[2 * n:3 * n]
        g_o, d_o, nm_o, nv_o = refs[3 * n:4 * n], refs[4 * n:5 * n], refs[5 * n:6 * n], refs[6 * n:7 * n]

        def upd(i, idx, g):
            nm = ADAM_B1 * m_r[i][idx] + (1.0 - ADAM_B1) * g
            nv = ADAM_B2 * v_r[i][idx] + (1.0 - ADAM_B2) * (g * g)
            g_o[i][idx] = g
            nm_o[i][idx] = nm
            nv_o[i][idx] = nv
            d_o[i][idx] = -ADAM_LR * ((nm * c1) / (jnp.sqrt(nv * c2) + ADAM_EPS) + ADAM_WD * w_r[i][idx])

        everything = (slice(None), slice(None))
        cc = w_r[0][...]
        sc = _sigmoid(cc)
        upd(0, everything, t_ref[0:1, :] * (sc * (1.0 + cc * (1.0 - sc))))
        for j in range(6):
            upd(1, (slice(None), slice(D * j, D * j + D)),
                tot_ref[R_ADA + j:R_ADA + j + 1, :] + tot_ref[R_ADA_C + j:R_ADA_C + j + 1, :])
        for j in range(4):
            upd(2 + j, everything, tot_ref[R_GAIN + j:R_GAIN + j + 1, :])
        upd(6, everything, tot_ref[R_SINK:R_SINK + 1, 0:N_ATT])
        upd(7, everything, tot_ref[R_BG:R_BG + 1, 0:256])
        upd(8, everything, tot_ref[R_BG:R_BG + 1, 256:512])
        upd(9, everything, tot_ref[R_GGLA:R_GGLA + 1, 0:DV])
        upd(10, (0,), wg_ref[0:GATE_RANK, :])
        upd(11, (0,), wg_ref[GATE_RANK:2 * GATE_RANK, :])

    params = [w[k] for k in SMALL_NAMES] + [m[k] for k in SMALL_NAMES] + [v[k] for k in SMALL_NAMES]
    outs = pl.pallas_call(
        body, name="small_update", grid=(1,),
        in_specs=[_full(tot.shape), _full(t_tot.shape), _full(wg_g.shape)] + [_full(p.shape) for p in params],
        out_specs=[_full(w[k].shape) for k in SMALL_NAMES] * 4,
        out_shape=[jax.ShapeDtypeStruct(w[k].shape, F32) for k in SMALL_NAMES] * 4,
        compiler_params=_cp(("arbitrary",)),
    )(tot, t_tot, wg_g, *params)
    return tuple(dict(zip(SMALL_NAMES, outs[i * n:(i + 1) * n])) for i in range(4))


def kernel(x, c, ctx, c_ctx, w_ada, b_ada, g_pre_mix, g_post_mix, g_pre_ffn, g_post_ffn, w_in, attn_sink, w_gate_fwd, b_gate_fwd, w_gate_bwd, b_gate_bwd, g_gla_norm, w_out, w_ffn_in, w_ffn_out, loss_target, m_c_ctx, m_w_ada, m_b_ada, m_g_pre_mix, m_g_post_mix, m_g_pre_ffn, m_g_post_ffn, m_w_in, m_attn_sink, m_w_gate_fwd, m_b_gate_fwd, m_w_gate_bwd, m_b_gate_bwd, m_g_gla_norm, m_w_out, m_w_ffn_in, m_w_ffn_out, v_c_ctx, v_w_ada, v_b_ada, v_g_pre_mix, v_g_post_mix, v_g_pre_ffn, v_g_post_ffn, v_w_in, v_attn_sink, v_w_gate_fwd, v_b_gate_fwd, v_w_gate_bwd, v_b_gate_bwd, v_g_gla_norm, v_w_out, v_w_ffn_in, v_w_ffn_out):
    me = 4 * lax.axis_index("x") + 2 * lax.axis_index("y") + lax.axis_index("c")
    S = x.shape[1]
    n_in = w_in.shape[2]
    n_ffi = w_ffn_in.shape[2]
    r_out = w_out.shape[1]
    r_ffo = w_ffn_out.shape[1]
    n_ada = w_ada.shape[2]

    wg_sh = jnp.concatenate([w_gate_fwd.reshape(4, 128), w_gate_bwd.reshape(4, 128)], axis=0)
    c_all3, g_all, w_all, ada_all = _entry(c, wg_sh, w_in[0].T.astype(_BF), c_ctx.reshape(1, D), w_ada[0])
    c_all = c_all3.reshape(N_DEV, D)
    wgf = g_all[:, 0:4].reshape(N_DEV, GATE_RANK, 32).transpose(1, 0, 2).reshape(GATE_RANK, 256)
    wgb = g_all[:, 4:8].reshape(N_DEV, GATE_RANK, 32).transpose(1, 0, 2).reshape(GATE_RANK, 256)
    win_p = _pad_rows_win(w_all.reshape(N_DEV * n_in, D))
    wg_bd = jnp.zeros((128, 512), F32).at[0:16, 0:256].set(wgf).at[16:32, 256:512].set(wgb).astype(_BF)
    ada_full = ada_all.transpose(1, 0, 2).reshape(16, N_DEV * n_ada) + b_ada
    ada_l = jnp.pad(lax.dynamic_slice_in_dim(ada_full, me, 1, 0).reshape(6, D), ((0, 2), (0, 0)))
    ada_c = jnp.pad(ada_full[8].reshape(6, D), ((0, 2), (0, 0)))
    gains = jnp.pad(jnp.concatenate([g_pre_mix, g_post_mix, g_pre_ffn, g_post_ffn], axis=0), ((0, 4), (0, 0)))
    sink = jnp.broadcast_to(attn_sink.reshape(8, 1), (8, 128))
    bg = jnp.concatenate([b_gate_fwd, b_gate_bwd], axis=1)
    ggla = jnp.tile(g_gla_norm, (1, N_GLA))

    r = _local_step(x[0], ctx[0], loss_target[0], ada_l, ada_c, gains, sink, win_p, wg_bd, bg, ggla,
                    w_out[0].astype(_BF), w_ffn_in[0].T.astype(_BF), w_ffn_out[0].astype(_BF))

    g_in = _unpad_rows_win(r["g_win_t"]).reshape(N_DEV, n_in, D)
    cut = (n_in // 3) // 16 * 16
    g_in_rows = [g_in[:, 0:cut], g_in[:, cut:2 * cut], g_in[:, 2 * cut:]]

    parts = r["parts"]
    tot = _sum_slots(parts, "sum_small_grads")
    loss = tot[R_LOSS, 0]
    d_ada_rows = parts[:, R_ADA:R_ADA + 6].reshape(N_DEV, 6 * D)
    d_ada_c = tot[R_ADA_C:R_ADA_C + 6].reshape(1, 6 * D)
    my_cols = lax.dynamic_slice_in_dim(jnp.concatenate([d_ada_rows, jnp.broadcast_to(d_ada_c, (1, 6 * D)),
                                                        jnp.zeros((7, 6 * D), F32)], axis=0), me * n_ada, n_ada, 1)
    grad_w_ada, t_part = _ada_bwd(c_all, c_ctx.reshape(1, D), w_ada[0], my_cols)
    wg_g = lax.dynamic_slice(tot, (R_WG, me * 32), (2 * GATE_RANK, 32))

    tr = lambda a: jnp.transpose(a, (0, 2, 1))
    big = {}
    res = _adamw(tr(w_ffn_in), [r["got_ffi"]], tr(m_w_ffn_in), tr(v_w_ffn_in), "adamw_w_ffn_in", [g_in_rows[0]])
    big["w_ffn_in"], slot_a = tuple(tr(o) for o in res[:4]), res[4]
    res = _adamw(w_ffn_out, [r["got_ffo"]], m_w_ffn_out, v_w_ffn_out, "adamw_w_ffn_out", [g_in_rows[1]])
    big["w_ffn_out"], slot_b = tuple(res[:4]), res[4]
    res = _adamw(w_ada, [grad_w_ada], m_w_ada, v_w_ada, "adamw_w_ada", [g_in_rows[2]])
    big["w_ada"], slot_c = tuple(res[:4]), res[4]
    res = _adamw(w_out, [r["got_out"]], m_w_out, v_w_out, "adamw_w_out", [t_part], gather=True)
    big["w_out"], t_all = tuple(res[:4]), res[4]
    big["w_in"] = tuple(tr(o) for o in _adamw(tr(w_in), [slot_a, slot_b, slot_c], tr(m_w_in), tr(v_w_in), "adamw_w_in"))
    t_tot = _sum_slots(t_all, "sum_c_ctx")

    w_small = dict(c_ctx=c_ctx.reshape(1, D), b_ada=b_ada, g_pre_mix=g_pre_mix, g_post_mix=g_post_mix, g_pre_ffn=g_pre_ffn,
                   g_post_ffn=g_post_ffn, attn_sink=attn_sink, b_gate_fwd=b_gate_fwd, b_gate_bwd=b_gate_bwd,
                   g_gla_norm=g_gla_norm, w_gate_fwd=w_gate_fwd, w_gate_bwd=w_gate_bwd)
    m_small = dict(c_ctx=m_c_ctx.reshape(1, D), b_ada=m_b_ada, g_pre_mix=m_g_pre_mix, g_post_mix=m_g_post_mix,
                   g_pre_ffn=m_g_pre_ffn, g_post_ffn=m_g_post_ffn, attn_sink=m_attn_sink, b_gate_fwd=m_b_gate_fwd,
                   b_gate_bwd=m_b_gate_bwd, g_gla_norm=m_g_gla_norm, w_gate_fwd=m_w_gate_fwd, w_gate_bwd=m_w_gate_bwd)
    v_small = dict(c_ctx=v_c_ctx.reshape(1, D), b_ada=v_b_ada, g_pre_mix=v_g_pre_mix, g_post_mix=v_g_post_mix,
                   g_pre_ffn=v_g_pre_ffn, g_post_ffn=v_g_post_ffn, attn_sink=v_attn_sink, b_gate_fwd=v_b_gate_fwd,
                   b_gate_bwd=v_b_gate_bwd, g_gla_norm=v_g_gla_norm, w_gate_fwd=v_w_gate_fwd, w_gate_bwd=v_w_gate_bwd)
    grads_small, d_s, nm_s, nv_s = _small_update(tot, t_tot, wg_g, w_small, m_small, v_small)
    for dd in (grads_small, d_s, nm_s, nv_s):
        dd["c_ctx"] = dd["c_ctx"].reshape(D)

    order = ["c_ctx", "w_ada", "b_ada", "g_pre_mix", "g_post_mix", "g_pre_ffn", "g_post_ffn", "w_in", "attn_sink",
             "w_gate_fwd", "b_gate_fwd", "w_gate_bwd", "b_gate_bwd", "g_gla_norm", "w_out", "w_ffn_in", "w_ffn_out"]
    grads, deltas, new_m, new_v = [], [], [], []
    for nm in order:
        if nm in big:
            g_, d_, m_, v_ = big[nm]
        else:
            g_, d_, m_, v_ = grads_small[nm], d_s[nm], nm_s[nm], nv_s[nm]
        grads.append(g_)
        deltas.append(d_)
        new_m.append(m_)
        new_v.append(v_)
    return (loss, r["grad_x"][None], *grads, *deltas, *new_m, *new_v)
```

```python
import functools
import math

import numpy as np
import jax
import jax.numpy as jnp
from jax import lax
from jax.experimental import pallas as pl
from jax.experimental.pallas import tpu as pltpu

F32 = jnp.float32
_BF = jnp.bfloat16

N_DEV = 8
D = 1024
CTX = 256
HD = 64
N_ATT = 8
N_KV = 2
GRP = N_ATT // N_KV
WIN = 128
GRID_W = 64
ROPE_BASE = 10000.0
N_GLA = 8
DK = 32
DV = 64
GATE_RANK = 16
GATE_TAU = 16.0
FFN = 2816
EPS = 1e-6
NEG = -1e30
GLA_T = 128

QP = N_ATT * HD
KP = N_KV * HD
O_Q, O_K, O_V = 0, QP, QP + KP
O_GQ = O_V + KP
O_GK = O_GQ + N_GLA * DK
O_GV = O_GK + N_GLA * DK
O_GG = O_GV + N_GLA * DV
O_Z = O_GG + N_GLA * DV
NP = O_Z + 128
IN_COLS = 2336

ADAM_LR, ADAM_B1, ADAM_B2, ADAM_EPS, ADAM_WD, ADAM_STEP = 0.001, 0.9, 0.999, 1e-08, 0.01, 10

VMEM_BIG = 56 * 1024 * 1024
MESH = pl.DeviceIdType.MESH


def _cp(sem, vmem=None):
    return pltpu.CompilerParams(dimension_semantics=sem, vmem_limit_bytes=vmem)


def _full(shape):
    nd = len(shape)
    return pl.BlockSpec(shape, lambda *a: (0,) * nd)


def _rows(tile, width, off=0):
    return pl.BlockSpec((tile, width), lambda i: (i + off, 0))


def _rows_lat(tile, width):
    return pl.BlockSpec((tile, width), lambda i: (jnp.maximum(i - 1, 0), 0))


def _nt(a, b):
    return lax.dot_general(a, b, (((1,), (1,)), ((), ())), preferred_element_type=F32)


def _tn(a, b):
    return lax.dot_general(a, b, (((0,), (0,)), ((), ())), preferred_element_type=F32)


def _nn(a, b):
    return jnp.dot(a, b, preferred_element_type=F32)


def _head_mean(x, mavg):
    n = x.shape[0]
    hi = x.astype(_BF)
    lo = (x - hi.astype(F32)).astype(_BF)
    y = _nn(jnp.concatenate([hi, lo], axis=0), mavg)
    return y[0:n] + y[n:2 * n]


def _rope(t, cos, sa, sb):
    n = t.shape[1]
    reps = n // 128
    c = jnp.tile(cos, (1, reps))
    a = jnp.tile(sa, (1, reps))
    b = jnp.tile(sb, (1, reps))
    return t * c + pltpu.roll(t, n - 16, 1) * a + pltpu.roll(t, 16, 1) * b


def _unrope(t, cos, sa, sb):
    n = t.shape[1]
    reps = n // 128
    c = jnp.tile(cos, (1, reps))
    a = jnp.tile(sa, (1, reps))
    b = jnp.tile(sb, (1, reps))
    return t * c + pltpu.roll(t * a, 16, 1) + pltpu.roll(t * b, n - 16, 1)


def _sigmoid(x):
    return 1.0 / (1.0 + jnp.exp(-x))


def _inproj_fwd(x, ctx, gml, shl, gmc, shc, win, wg, bg, cos, sa, sb, shards):
    E = x.shape[0] + CTX
    TE = CTX

    def body(x_ref, c_ref, gml_ref, shl_ref, gmc_ref, shc_ref, w_ref, wg_ref, bg_ref, cos_ref, sa_ref, sb_ref,
             h_ref, q_ref, k_ref, v_ref, gq_ref, gk_ref, gv_ref, gg_ref, z_ref, la_ref):
        is_ctx = pl.program_id(0) == 0
        gm = jnp.where(is_ctx, gmc_ref[...], gml_ref[...])
        sh = jnp.where(is_ctx, shc_ref[...], shl_ref[...])
        x = jnp.where(is_ctx, c_ref[...], x_ref[...])
        r = lax.rsqrt(jnp.mean(x * x, axis=-1, keepdims=True) + EPS)
        hb = ((x * r) * gm + sh).astype(_BF)
        h_ref[...] = hb
        p = _nt(hb, w_ref[...])
        cos_t, sa_t, sb_t = cos_ref[...], sa_ref[...], sb_ref[...]
        q_ref[...] = (_rope(p[:, O_Q:O_K], cos_t, sa_t, sb_t) * (HD ** -0.5)).astype(_BF)
        k_ref[...] = _rope(p[:, O_K:O_V], cos_t, sa_t, sb_t).astype(_BF)
        v_ref[...] = p[:, O_V:O_GQ].astype(_BF)
        gq_ref[...] = p[:, O_GQ:O_GK] * (DK ** -0.5)
        gk_ref[...] = p[:, O_GK:O_GV]
        gv_ref[...] = p[:, O_GV:O_GG]
        gg_ref[...] = p[:, O_GG:O_Z]
        zb = p[:, O_Z:NP].astype(_BF)
        z_ref[...] = zb
        lg = _nn(zb, wg_ref[...]) + bg_ref[...]
        la_ref[...] = (jnp.minimum(lg, 0.0) - jnp.log(1.0 + jnp.exp(-jnp.abs(lg)))) * (1.0 / GATE_TAU)

    vec = _full((1, D))
    tab = _rows(TE, 128)
    outs = [(D, _BF), (QP, _BF), (KP, _BF), (KP, _BF), (256, F32), (256, F32), (512, F32), (512, F32),
            (128, _BF), (512, F32)]
    return _hosted_call(
        body, (x, ctx, gml, shl, gmc, shc, win, wg, bg, cos, sa, sb), shards, True,
        name="inproj_fwd", grid=(E // TE,),
        in_specs=[_rows_lat(TE, D), _full((CTX, D)), vec, vec, vec, vec, _full((NP, D)), _full((128, 512)),
                  _full((1, 512)), tab, tab, tab],
        out_specs=[_rows(TE, w) for w, _ in outs],
        out_shape=[jax.ShapeDtypeStruct((E, w), dt) for w, dt in outs],
        compiler_params=_cp(("arbitrary",), 40 * 1024 * 1024))


def _xchg_scratch(na):
    return [pltpu.SemaphoreType.DMA((na, N_DEV - 1)), pltpu.SemaphoreType.DMA((na, N_DEV - 1)),
            pltpu.SemaphoreType.DMA((na,))]


def _xchg_copies(ins, outs, send_sems, recv_sems, local_sems, gather):
    x, y, c = lax.axis_index("x"), lax.axis_index("y"), lax.axis_index("c")
    me = 4 * x + 2 * y + c
    local, sends, recvs = [], [], []
    for a in range(len(ins)):
        local.append(pltpu.make_async_copy(ins[a] if gather else ins[a].at[me], outs[a].at[me], local_sems.at[a]))
    for k in range(1, N_DEV):
        px, py, pc = x ^ (k >> 2), y ^ ((k >> 1) & 1), c ^ (k & 1)
        peer = 4 * px + 2 * py + pc
        for a in range(len(ins)):
            sems = dict(send_sem=send_sems.at[a, k - 1], recv_sem=recv_sems.at[a, k - 1], device_id_type=MESH)
            sends.append(pltpu.make_async_remote_copy(
                src_ref=ins[a] if gather else ins[a].at[peer], dst_ref=outs[a].at[me], device_id=(px, py, pc), **sems))
            recvs.append(pltpu.make_async_remote_copy(
                src_ref=ins[a] if gather else ins[a].at[me], dst_ref=outs[a].at[peer], device_id=(x, y, c), **sems))
    return local, sends, recvs


def _xchg_start(cps):
    local, sends, _ = cps
    for cp in local + sends:
        cp.start()


def _xchg_finish(cps):
    local, sends, recvs = cps
    for cp in recvs:
        cp.wait_recv()
    for cp in sends:
        cp.wait_send()
    for cp in local:
        cp.wait()


def _xchg_out_shapes(ins, gather):
    return [jax.ShapeDtypeStruct(((N_DEV,) + s.shape) if gather else s.shape, s.dtype) for s in ins]


def _hosted_call(body, args, hosted, gather, *, grid, in_specs, out_specs, out_shape, scratch_shapes=(), **kw):
    na = len(hosted)
    if na == 0:
        return pl.pallas_call(body, grid=grid, in_specs=in_specs, out_specs=out_specs, out_shape=out_shape,
                              scratch_shapes=list(scratch_shapes), **kw)(*args)
    n_in, n_out, n_scr = len(in_specs), len(out_specs), len(scratch_shapes)

    def wrapped(*refs):
        ins, h_in = refs[:n_in], refs[n_in:n_in + na]
        outs, h_out = refs[n_in + na:n_in + na + n_out], refs[n_in + na + n_out:n_in + 2 * na + n_out]
        scr = refs[n_in + 2 * na + n_out:]
        cps = _xchg_copies(h_in, h_out, *scr[n_scr:], gather=gather)
        pids = [pl.program_id(a) for a in range(len(grid))]
        first = functools.reduce(jnp.logical_and, [p == 0 for p in pids])
        last = functools.reduce(jnp.logical_and, [p == g - 1 for p, g in zip(pids, grid)])

        @pl.when(first)
        def _():
            _xchg_start(cps)

        body(*ins, *outs, *scr[:n_scr])

        @pl.when(last)
        def _():
            _xchg_finish(cps)

    anyspec = pl.BlockSpec(memory_space=pl.ANY)
    return pl.pallas_call(
        wrapped, grid=grid, in_specs=list(in_specs) + [anyspec] * na, out_specs=list(out_specs) + [anyspec] * na,
        out_shape=list(out_shape) + _xchg_out_shapes(hosted, gather),
        scratch_shapes=list(scratch_shapes) + _xchg_scratch(na), **kw)(*args, *hosted)


def _attn_specs(E):
    nb = (E - CTX) // WIN
    last = E // WIN - 1
    kc = pl.BlockSpec((CTX, KP), lambda n: (0, 0))
    kp = pl.BlockSpec((WIN, KP), lambda n: (n + 1, 0))
    kk = pl.BlockSpec((WIN, KP), lambda n: (n + 2, 0))
    kn = pl.BlockSpec((WIN, KP), lambda n: (jnp.minimum(n + 3, last), 0))
    return nb, [kc, kp, kk, kn]


def _attn_bias(nb):
    rows = np.arange(GRP * WIN)[:, None] % WIN
    cols = np.arange(CTX + 3 * WIN)[None, :]
    j = cols - CTX
    band = np.abs(j - WIN - rows) <= WIN
    out = []
    for first, last in ((True, False), (False, False), (False, True)):
        ok = (cols < CTX) | (band & ((j >= WIN) | (not first)) & ((j < 2 * WIN) | (not last)))
        out.append(np.where(ok, 0.0, NEG).astype(np.float32))
    bias = jnp.asarray(np.stack(out))
    spec = pl.BlockSpec((1, GRP * WIN, CTX + 3 * WIN),
                        lambda n: (jnp.where(n == 0, 0, jnp.where(n == nb - 1, 2, 1)), 0, 0))
    return bias, spec


def _both_halves(t, h):
    tf = t.astype(F32)
    r = pltpu.roll(tf, HD, 1)
    lo = lax.broadcasted_iota(jnp.int32, tf.shape, 1) < HD
    return (jnp.where(lo, tf, r) if h == 0 else jnp.where(lo, r, tf)).astype(t.dtype)


def _stack_heads(ref, h):
    lo = lax.broadcasted_iota(jnp.int32, (WIN, 128), 1) < HD
    parts = []
    for g in range(GRP):
        j = GRP * h + g
        t = ref[:, 128 * (j // 2):128 * (j // 2) + 128].astype(F32)
        parts.append(jnp.where(lo if j % 2 == 0 else jnp.logical_not(lo), t, 0.0))
    return jnp.concatenate(parts, axis=0)


def _unstack_pair(o, pp):
    lo = lax.broadcasted_iota(jnp.int32, (WIN, 128), 1) < HD
    return jnp.where(lo, o[WIN * 2 * pp:WIN * 2 * pp + WIN], o[WIN * (2 * pp + 1):WIN * (2 * pp + 1) + WIN])


def _attn_fwd(q, k, v, sink, shards):
    E = q.shape[0]
    S = E - CTX
    nb, kspecs = _attn_specs(E)
    na = len(shards)

    def body(q_ref, kc, kp, kk, kn, vc, vp, vk, vn, sink_ref, bias_ref, *rest):
        shard_refs, (o_ref, lse_ref), got_refs = rest[:na], rest[na:na + 2], rest[na + 2:2 * na + 2]
        n = pl.program_id(0)
        cps = _xchg_copies(shard_refs, got_refs, *rest[2 * na + 2:], gather=True)

        @pl.when(n == 0)
        def _():
            _xchg_start(cps)

        lane = lax.broadcasted_iota(jnp.int32, (WIN, 128), 1)
        lse_t = jnp.zeros((WIN, 128), F32)
        kall = jnp.concatenate([kc[...], kp[...], kk[...], kn[...]], axis=0)
        vall = jnp.concatenate([vc[...], vp[...], vk[...], vn[...]], axis=0)
        K = [_both_halves(kall, h) for h in range(N_KV)]
        Q = [_stack_heads(q_ref, h).astype(_BF) for h in range(N_KV)]
        sk = [jnp.concatenate([jnp.broadcast_to(sink_ref[GRP * h + g:GRP * h + g + 1, 0:1], (WIN, 1))
                               for g in range(GRP)], axis=0) for h in range(N_KV)]
        s = [_nt(Q[h], K[h]) + bias_ref[0] for h in range(N_KV)]
        m = [jnp.maximum(jnp.max(s[h], axis=1, keepdims=True), sk[h]) for h in range(N_KV)]
        e = [jnp.exp(s[h] - m[h]) for h in range(N_KV)]
        den = [jnp.sum(e[h], axis=1, keepdims=True) + jnp.exp(sk[h] - m[h]) for h in range(N_KV)]
        V = [_both_halves(vall, h) for h in range(N_KV)]
        o = [_nn((e[h] * (1.0 / den[h])).astype(_BF), V[h]) for h in range(N_KV)]
        for h in range(N_KV):
            lse = m[h] + jnp.log(den[h])
            for g in range(GRP):
                lse_t = jnp.where(lane == GRP * h + g, lse[WIN * g:WIN * g + WIN], lse_t)
            for pp in range(GRP // 2):
                t = 2 * h + pp
                o_ref[:, 128 * t:128 * t + 128] = _unstack_pair(o[h], pp).astype(_BF)
        lse_ref[...] = lse_t

        @pl.when(n == nb - 1)
        def _():
            _xchg_finish(cps)

    qs = pl.BlockSpec((WIN, QP), lambda n: (n + 2, 0))
    anyspec = pl.BlockSpec(memory_space=pl.ANY)
    bias, bias_spec = _attn_bias(nb)
    return pl.pallas_call(
        body, name="attn_fwd", grid=(nb,),
        in_specs=[qs] + kspecs + kspecs + [_full((8, 128)), bias_spec] + [anyspec] * na,
        out_specs=[_rows(WIN, 512), _rows(WIN, 128)] + [anyspec] * na,
        out_shape=[jax.ShapeDtypeStruct((S, 512), _BF), jax.ShapeDtypeStruct((S, 128), F32)]
        + _xchg_out_shapes(shards, True),
        scratch_shapes=_xchg_scratch(na),
        compiler_params=_cp(("arbitrary",)),
    )(q, k, k, k, k, v, v, v, v, sink, bias, *shards)


def _attn_bwd(q, k, v, sink, lse, d_attn, slabs):
    E = q.shape[0]
    S = E - CTX
    nb, kspecs = _attn_specs(E)
    last = E // WIN - 1
    na = len(slabs)

    def body(q_ref, kc, kp, kk, kn, vc, vp, vk, vn, sink_ref, bias_ref, lse_ref, do_ref, *rest):
        slab_refs, (dq_ref, dk_ref, dv_ref, ds_ref), got_refs = rest[:na], rest[na:na + 4], rest[na + 4:2 * na + 4]
        n = pl.program_id(0)
        cps = _xchg_copies(slab_refs, got_refs, *rest[2 * na + 4:], gather=False)

        @pl.when(n == 0)
        def _():
            _xchg_start(cps)
            dk_ref[...] = jnp.zeros_like(dk_ref)
            dv_ref[...] = jnp.zeros_like(dv_ref)
            ds_ref[...] = jnp.zeros_like(ds_ref)

        lane = lax.broadcasted_iota(jnp.int32, (WIN, 128), 1)
        lse_t = lse_ref[...]
        starts = [None, pl.multiple_of((n + 1) * WIN, WIN), pl.multiple_of((n + 2) * WIN, WIN),
                  pl.multiple_of(jnp.minimum(n + 3, last) * WIN, WIN)]
        kall = jnp.concatenate([kc[...], kp[...], kk[...], kn[...]], axis=0)
        vall = jnp.concatenate([vc[...], vp[...], vk[...], vn[...]], axis=0)
        for h in range(N_KV):
            hs = slice(HD * h, HD * h + HD)
            K = _both_halves(kall, h)
            V = _both_halves(vall, h)
            Q = _stack_heads(q_ref, h).astype(_BF)
            sk = jnp.concatenate([jnp.broadcast_to(sink_ref[GRP * h + g:GRP * h + g + 1, 0:1], (WIN, 1))
                                  for g in range(GRP)], axis=0)
            ls = jnp.concatenate([jnp.sum(jnp.where(lane == GRP * h + g, lse_t, 0.0), axis=1, keepdims=True)
                                  for g in range(GRP)], axis=0)
            do = _stack_heads(do_ref, h).astype(_BF)
            p = jnp.exp(_nt(Q, K) + bias_ref[0] - ls)
            dp = _nt(do, V)
            delta = jnp.sum(p * dp, axis=1, keepdims=True)
            dsc = (p * (dp - delta)).astype(_BF)
            dq = _nn(dsc, K) * (HD ** -0.5)
            for pp in range(GRP // 2):
                t = 2 * h + pp
                dq_ref[:, 128 * t:128 * t + 128] = _unstack_pair(dq, pp).astype(_BF)
            dK2 = _tn(Q, dsc)
            dV2 = _tn(do, p.astype(_BF))
            dK = dK2[0:HD] + dK2[HD:2 * HD]
            dV = dV2[0:HD] + dV2[HD:2 * HD]
            dk_ref[hs, 0:CTX] += dK[:, 0:CTX]
            dv_ref[hs, 0:CTX] += dV[:, 0:CTX]
            for w in range(1, 4):
                lo = CTX + WIN * (w - 1)
                dk_ref[hs, pl.ds(starts[w], WIN)] += dK[:, lo:lo + WIN]
                dv_ref[hs, pl.ds(starts[w], WIN)] += dV[:, lo:lo + WIN]
            psk = -jnp.exp(sk - ls) * delta
            for g in range(GRP):
                j = GRP * h + g
                tot = jnp.sum(psk[WIN * g:WIN * g + WIN], axis=0, keepdims=True)
                ds_ref[j:j + 1, :] += jnp.broadcast_to(tot, (1, 128))

        @pl.when(n == nb - 1)
        def _():
            _xchg_finish(cps)

    qs = pl.BlockSpec((WIN, QP), lambda n: (n + 2, 0))
    anyspec = pl.BlockSpec(memory_space=pl.ANY)
    bias, bias_spec = _attn_bias(nb)
    return pl.pallas_call(
        body, name="attn_bwd", grid=(nb,),
        in_specs=[qs] + kspecs + kspecs + [_full((8, 128)), bias_spec, _rows(WIN, 128), _rows(WIN, 512)]
        + [anyspec] * na,
        out_specs=[_rows(WIN, QP), _full((KP, E)), _full((KP, E)), _full((8, 128))] + [anyspec] * na,
        out_shape=[jax.ShapeDtypeStruct((S, QP), _BF), jax.ShapeDtypeStruct((KP, E), F32),
                   jax.ShapeDtypeStruct((KP, E), F32), jax.ShapeDtypeStruct((8, 128), F32)]
        + _xchg_out_shapes(slabs, False),
        scratch_shapes=_xchg_scratch(na),
        compiler_params=_cp(("arbitrary",), 48 * 1024 * 1024),
    )(q, k, k, k, k, v, v, v, v, sink, bias, lse, d_attn, *slabs)


def _gla_order(E, reverse, backward):
    nc = CTX // GLA_T
    n = E // GLA_T
    if not reverse:
        fwd = lambda s: s
    else:
        fwd = lambda s: jnp.where(s < nc, nc - 1 - s, n - 1 + nc - s)
    if backward:
        return lambda s: fwd(n - 1 - s)
    return fwd


def _gla_masks():
    T = GLA_T
    l128 = lax.broadcasted_iota(jnp.int32, (1, 128), 1)
    qmask = [((l128 >> 5) == j).astype(F32) for j in range(4)]
    vmask = [((l128 >> 6) == j).astype(F32) for j in range(2)]
    bd = ((lax.broadcasted_iota(jnp.int32, (512, 256), 0) >> 6)
          == (lax.broadcasted_iota(jnp.int32, (512, 256), 1) >> 5)).astype(F32)
    ri = lax.broadcasted_iota(jnp.int32, (T, 2 * T), 0)
    ci = lax.broadcasted_iota(jnp.int32, (T, 2 * T), 1) & (T - 1)
    return qmask, vmask, bd, ri, ci


def _tri_sum(tri, x):
    hi = x.astype(_BF)
    lo = (x - hi.astype(F32)).astype(_BF)
    n = x.shape[1]
    y = _nn(tri.astype(_BF), jnp.concatenate([hi, lo], axis=1))
    return y[:, 0:n] + y[:, n:2 * n]


def _gla_decays(la, reverse, ri, ci):
    T = GLA_T
    msk2 = (ri <= ci) if reverse else (ri >= ci)
    mskT2 = (ri >= ci) if reverse else (ri <= ci)
    b = _tri_sum(msk2[:, 0:T], la)
    bT = b[0:1] if reverse else b[T - 1:T]
    bm = b[T // 2:T // 2 + 1]
    return msk2, mskT2, b, bT, bm


def _pair_stack(tile, m0, m1):
    return jnp.concatenate([(tile * m0).astype(_BF), (tile * m1).astype(_BF)], axis=0)


def _gla_fwd(gq, gk, gv, la, reverse, shards=()):
    E = gq.shape[0]
    T = GLA_T
    n = E // T
    order = _gla_order(E, reverse, False)
    col = 1 if reverse else 0

    def body(gq_ref, gk_ref, gv_ref, la_ref, o_ref, st_ref, S_scr):
        @pl.when(pl.program_id(0) == 0)
        def _():
            S_scr[...] = jnp.zeros_like(S_scr)

        qmask, vmask, bd, ri, ci = _gla_masks()
        msk2, _, b, bT, bm = _gla_decays(la_ref[...], reverse, ri, ci)
        q, k, v = gq_ref[...], gk_ref[...], gv_ref[...]
        qd = (q * jnp.exp(b)).astype(_BF)
        qm = (q * jnp.exp(b - bm)).astype(_BF)
        km = k * jnp.exp(bm - b)
        kd = (k * jnp.exp(bT - b)).astype(_BF)
        ST = S_scr[...]
        comp = ST[0:DV]
        for h in range(1, N_GLA):
            comp = comp + ST[DV * h:DV * h + DV]
        st_ref[0] = comp
        inter = _nt(qd, ST.astype(_BF))
        tiles = []
        for p in range(N_GLA // 2):
            qs = slice(128 * (p // 2), 128 * (p // 2) + 128)
            vs = slice(128 * p, 128 * p + 128)
            j0 = (2 * p) % 4
            KS = _pair_stack(km[:, qs], qmask[j0], qmask[j0 + 1])
            VS = _pair_stack(v[:, vs], vmask[0], vmask[1])
            AA = jnp.where(msk2, _nt(qm[:, qs], KS), 0.0).astype(_BF)
            tiles.append(_nn(AA, VS))
        o_ref[...] = inter + jnp.concatenate(tiles, axis=1)
        S_scr[...] = ST * jnp.exp(bT) + bd * _tn(v.astype(_BF), kd)

    blk = lambda w, c=0: pl.BlockSpec((T, w), lambda s: (order(s), c))
    return _hosted_call(
        body, (gq, gk, gv, la), shards, True,
        name="gla_fwd_rev" if reverse else "gla_fwd", grid=(n,),
        in_specs=[blk(256), blk(256), blk(512), blk(256, col)],
        out_specs=[blk(512), pl.BlockSpec((1, DV, 256), lambda s: (order(s), 0, 0))],
        out_shape=[jax.ShapeDtypeStruct((E, 512), F32), jax.ShapeDtypeStruct((n, DV, 256), F32)],
        scratch_shapes=[pltpu.VMEM((512, 256), F32)],
        compiler_params=_cp(("arbitrary",)))


def _gla_bwd(gq, gk, gv, la, st, do, reverse, slabs=()):
    E = gq.shape[0]
    T = GLA_T
    n = E // T
    nc = CTX // T
    order = _gla_order(E, reverse, True)
    col = 1 if reverse else 0

    def body(gq_ref, gk_ref, gv_ref, la_ref, st_ref, do_ref, dq_ref, dk_ref, dv_ref, dla_ref, dS_scr):
        @pl.when(pl.program_id(0) == 0)
        def _():
            dS_scr[...] = jnp.zeros_like(dS_scr)

        is_lat = order(pl.program_id(0)) >= nc
        qmask, vmask, bd, ri, ci = _gla_masks()
        msk2, mskT2, b, bT, bm = _gla_decays(la_ref[...], reverse, ri, ci)
        q, k, v = gq_ref[...], gk_ref[...], gv_ref[...]
        do = jnp.where(is_lat, do_ref[...].astype(F32), 0.0)
        e_b, e_qm, e_km, e_kd, e_T = jnp.exp(b), jnp.exp(b - bm), jnp.exp(bm - b), jnp.exp(bT - b), jnp.exp(bT)
        qd, qm, km, kd = q * e_b, q * e_qm, k * e_km, k * e_kd
        qdb, qmb, kmb, kdb, vb, dob = (t.astype(_BF) for t in (qd, qm, km, kd, v, do))
        ST = jnp.tile(st_ref[0], (N_GLA, 1)) * bd
        dST = dS_scr[...]
        dSTb = dST.astype(_BF)
        dqd = _nn(dob, ST.astype(_BF))
        dkd = _nn(vb, dSTb)
        dv_t, dqm_t, dkm_t = [], [None, None], [None, None]
        for p in range(N_GLA // 2):
            t = p // 2
            qs = slice(128 * t, 128 * t + 128)
            vs = slice(128 * p, 128 * p + 128)
            j0 = (2 * p) % 4
            QS = _pair_stack(qm[:, qs], qmask[j0], qmask[j0 + 1])
            KS = _pair_stack(km[:, qs], qmask[j0], qmask[j0 + 1])
            VS = _pair_stack(v[:, vs], vmask[0], vmask[1])
            DS = _pair_stack(do[:, vs], vmask[0], vmask[1])
            ATT = jnp.where(mskT2, _nt(kmb[:, qs], QS), 0.0).astype(_BF)
            dAA = jnp.where(msk2, _nt(dob[:, vs], VS), 0.0).astype(_BF)
            dATT = jnp.where(mskT2, _nt(vb[:, vs], DS), 0.0).astype(_BF)
            dv_t.append(_nn(ATT, DS))
            dq_p = _nn(dAA, KS)
            dk_p = _nn(dATT, QS)
            dqm_t[t] = dq_p if dqm_t[t] is None else dqm_t[t] + dq_p
            dkm_t[t] = dk_p if dkm_t[t] is None else dkm_t[t] + dk_p
        dqm = jnp.concatenate(dqm_t, axis=1)
        dkm = jnp.concatenate(dkm_t, axis=1)
        dq_ref[...] = dqm * e_qm + dqd * e_b
        dk_ref[...] = dkm * e_km + dkd * e_kd
        dv_ref[...] = _nt(kdb, dSTb) + jnp.concatenate(dv_t, axis=1)
        db = dqm * qm - dkm * km + dqd * qd - dkd * kd
        dbT = jnp.sum(dkd * kd, axis=0, keepdims=True) + e_T * jnp.sum(dST * ST, axis=0, keepdims=True)
        dla_ref[...] = _tri_sum(mskT2[:, 0:T], db) + dbT
        dS_scr[...] = dST * e_T + bd * _tn(dob, qdb)

    blk = lambda w, c=0: pl.BlockSpec((T, w), lambda s: (order(s), c))
    do_spec = pl.BlockSpec((T, 512), lambda s: (jnp.maximum(order(s) - nc, 0), 0))
    return _hosted_call(
        body, (gq, gk, gv, la, st, do), slabs, False,
        name="gla_bwd_rev" if reverse else "gla_bwd", grid=(n,),
        in_specs=[blk(256), blk(256), blk(512), blk(256, col),
                  pl.BlockSpec((1, DV, 256), lambda s: (order(s), 0, 0)), do_spec],
        out_specs=[blk(256), blk(256), blk(512), blk(256)],
        out_shape=[jax.ShapeDtypeStruct((E, 256), F32), jax.ShapeDtypeStruct((E, 256), F32),
                   jax.ShapeDtypeStruct((E, 512), F32), jax.ShapeDtypeStruct((E, 256), F32)],
        scratch_shapes=[pltpu.VMEM((512, 256), F32)],
        compiler_params=_cp(("arbitrary",)))


def _gla_out(o_f, o_b, gg, ggla, mavg):
    o = o_f + o_b
    rr = lax.rsqrt(_head_mean(o * o, mavg) + EPS)
    oh = o * rr
    sg = _sigmoid(gg)
    return oh, rr, sg


def _mix_fwd(x, attn, o_f, o_b, gg, ggla, mavg, wout, gt1, g2):
    S = x.shape[0]
    TM = 256

    def body(x_ref, a_ref, of_ref, ob_ref, gg_ref, ggla_ref, mavg_ref, w_ref, gt1_ref, g2_ref, x1_ref, mix_ref):
        gg_t = gg_ref[...]
        oh, _, sg = _gla_out(of_ref[...], ob_ref[...], gg_t, ggla_ref[...], mavg_ref[...])
        mix_ref[:, 0:512] = a_ref[...]
        mix_ref[:, 512:1024] = (oh * ggla_ref[...] * (gg_t * sg)).astype(_BF)
        y = _nn(mix_ref[...], w_ref[...])
        ry = lax.rsqrt(jnp.mean(y * y, axis=-1, keepdims=True) + EPS)
        x1_ref[...] = x_ref[...] + gt1_ref[...] * ((y * ry) * g2_ref[...])

    return pl.pallas_call(
        body, name="mix_fwd", grid=(S // TM,),
        in_specs=[_rows(TM, D), _rows(TM, 512), _rows(TM, 512, 1), _rows(TM, 512, 1), _rows(TM, 512, 1),
                  _full((1, 512)), _full((512, 512)), _full((D, D)), _full((1, D)), _full((1, D))],
        out_specs=[_rows(TM, D), _rows(TM, D)],
        out_shape=[jax.ShapeDtypeStruct((S, D), F32), jax.ShapeDtypeStruct((S, D), _BF)],
        compiler_params=_cp(("arbitrary",), 40 * 1024 * 1024),
    )(x, attn, o_f, o_b, gg, ggla, mavg, wout, gt1, g2)


def _mix_bwd(dx1, mix, o_f, o_b, gg, ggla, mavg, wout, gt1, g2):
    S = dx1.shape[0]
    TM = 256

    def body(dx_ref, mix_ref, of_ref, ob_ref, gg_ref, ggla_ref, mavg_ref, w_ref, gt1_ref, g2_ref,
             da_ref, do_ref, dgg_ref, dy_ref, sums_ref):
        @pl.when(pl.program_id(0) == 0)
        def _():
            sums_ref[...] = jnp.zeros_like(sums_ref)

        dx = dx_ref[...]
        y = _nn(mix_ref[...], w_ref[...])
        ry = lax.rsqrt(jnp.mean(y * y, axis=-1, keepdims=True) + EPS)
        yh = y * ry
        sums_ref[0:1, :] += jnp.sum(dx * yh, axis=0, keepdims=True)
        dyh = dx * (gt1_ref[...] * g2_ref[...])
        dy = (ry * (dyh - yh * jnp.mean(dyh * yh, axis=-1, keepdims=True))).astype(_BF)
        dy_ref[...] = dy
        dmix = _nt(dy, w_ref[...])
        da_ref[...] = dmix[:, 0:512].astype(_BF)
        dgla = dmix[:, 512:1024]
        gg_t = gg_ref[...]
        ggla_t = ggla_ref[...]
        oh, rr, sg = _gla_out(of_ref[...], ob_ref[...], gg_t, ggla_t, mavg_ref[...])
        dgg_ref[...] = (dgla * oh * ggla_t * (sg * (1.0 + gg_t * (1.0 - sg)))).astype(_BF)
        don = dgla * (gg_t * sg)
        sums_ref[1:2, 0:512] += jnp.sum(don * oh, axis=0, keepdims=True)
        doh = don * ggla_t
        do_ref[...] = (rr * (doh - oh * _head_mean(doh * oh, mavg_ref[...]))).astype(_BF)

    return pl.pallas_call(
        body, name="mix_bwd", grid=(S // TM,),
        in_specs=[_rows(TM, D), _rows(TM, D), _rows(TM, 512, 1), _rows(TM, 512, 1), _rows(TM, 512, 1),
                  _full((1, 512)), _full((512, 512)), _full((D, D)), _full((1, D)), _full((1, D))],
        out_specs=[_rows(TM, 512), _rows(TM, 512), _rows(TM, 512), _rows(TM, D), _full((8, D))],
        out_shape=[jax.ShapeDtypeStruct((S, 512), _BF), jax.ShapeDtypeStruct((S, 512), _BF),
                   jax.ShapeDtypeStruct((S, 512), _BF), jax.ShapeDtypeStruct((S, D), _BF),
                   jax.ShapeDtypeStruct((8, D), F32)],
        compiler_params=_cp(("arbitrary",), 40 * 1024 * 1024),
    )(dx1, mix, o_f, o_b, gg, ggla, mavg, wout, gt1, g2)


def _ffn(x1, target, gm2, sh2, gt2, g4, wffi, wffo):
    S = x1.shape[0]
    TF = 256

    def body(x_ref, t_ref, gm_ref, sh_ref, gt_ref, g4_ref, wi_hbm, wo_hbm,
             dx_ref, h_ref, du_ref, act_ref, df_ref, sums_ref, loss_ref, wi, wo, sem):
        @pl.when(pl.program_id(0) == 0)
        def _():
            c1 = pltpu.make_async_copy(wi_hbm, wi, sem.at[0])
            c2 = pltpu.make_async_copy(wo_hbm, wo, sem.at[1])
            c1.start()
            c2.start()
            sums_ref[...] = jnp.zeros_like(sums_ref)
            loss_ref[...] = jnp.zeros_like(loss_ref)
            c1.wait()
            c2.wait()

        x = x_ref[...]
        gm = gm_ref[...]
        r = lax.rsqrt(jnp.mean(x * x, axis=-1, keepdims=True) + EPS)
        xh = x * r
        hb = (xh * gm + sh_ref[...]).astype(_BF)
        h_ref[...] = hb
        u = _nt(hb, wi[...])
        g = u[:, 0:FFN]
        up = u[:, FFN:2 * FFN]
        sg = _sigmoid(g)
        sl = g * sg
        ab = (sl * up).astype(_BF)
        act_ref[...] = ab
        f = _nn(ab, wo[...])
        rf = lax.rsqrt(jnp.mean(f * f, axis=-1, keepdims=True) + EPS)
        fh = f * rf
        gt, g4v = gt_ref[...], g4_ref[...]
        err = x + gt * (fh * g4v) - t_ref[...]
        loss_ref[...] += jnp.sum(err * err) * (0.5 / D)
        dout = err * (1.0 / D)
        sums_ref[2:3, :] += jnp.sum(dout * fh, axis=0, keepdims=True)
        dfh = dout * (gt * g4v)
        dfb = (rf * (dfh - fh * jnp.mean(dfh * fh, axis=-1, keepdims=True))).astype(_BF)
        df_ref[...] = dfb
        dact = _nt(dfb, wo[...])
        du_ref[:, 0:FFN] = (dact * up * (sg * (1.0 + g * (1.0 - sg)))).astype(_BF)
        du_ref[:, FFN:2 * FFN] = (dact * sl).astype(_BF)
        dh = _nn(du_ref[...], wi[...])
        sums_ref[0:1, :] += jnp.sum(dh, axis=0, keepdims=True)
        sums_ref[1:2, :] += jnp.sum(dh * xh, axis=0, keepdims=True)
        dxh = dh * gm
        dx_ref[...] = dout + r * (dxh - xh * jnp.mean(dxh * xh, axis=-1, keepdims=True))

    vec = _full((1, D))
    anyspec = pl.BlockSpec(memory_space=pl.ANY)
    return pl.pallas_call(
        body, name="ffn_fwd_bwd", grid=(S // TF,),
        in_specs=[_rows(TF, D), _rows(TF, D), vec, vec, vec, vec, anyspec, anyspec],
        out_specs=[_rows(TF, D), _rows(TF, D), _rows(TF, 2 * FFN), _rows(TF, FFN), _rows(TF, D),
                   _full((8, D)), _full((8, 128))],
        out_shape=[jax.ShapeDtypeStruct((S, D), F32), jax.ShapeDtypeStruct((S, D), _BF),
                   jax.ShapeDtypeStruct((S, 2 * FFN), _BF), jax.ShapeDtypeStruct((S, FFN), _BF),
                   jax.ShapeDtypeStruct((S, D), _BF), jax.ShapeDtypeStruct((8, D), F32),
                   jax.ShapeDtypeStruct((8, 128), F32)],
        scratch_shapes=[pltpu.VMEM((2 * FFN, D), _BF), pltpu.VMEM((FFN, D), _BF), pltpu.SemaphoreType.DMA((2,))],
        compiler_params=_cp(("arbitrary",), VMEM_BIG),
    )(x1, target, gm2, sh2, gt2, g4, wffi, wffo)


def _inproj_bwd(x, ctx, gml, gmc, win, wg, cos, sa, sb, la, dq, dk, dv, dgq_f, dgq_b, dgk_f, dgk_b, dgv_f, dgv_b,
                dgg, dla_f, dla_b, dx1):
    S = x.shape[0]
    E = S + CTX
    TE = CTX

    def body(x_ref, c_ref, gml_ref, gmc_ref, w_ref, wg_ref, cos_ref, sa_ref, sb_ref, la_ref, dq_ref, dk_ref, dv_ref,
             gqf, gqb, gkf, gkb, gvf, gvb, dgg_ref, dlf, dlb, dx1_ref,
             dp_ref, dlg_ref, gx_ref, sums_ref, bsum_ref):
        i = pl.program_id(0)
        is_ctx = i == 0

        @pl.when(is_ctx)
        def _():
            sums_ref[...] = jnp.zeros_like(sums_ref)
            bsum_ref[...] = jnp.zeros_like(bsum_ref)

        lat = jnp.where(is_ctx, 0.0, 1.0)
        cos_t, sa_t, sb_t = cos_ref[...], sa_ref[...], sb_ref[...]
        dp_ref[:, O_Q:O_K] = (_unrope(dq_ref[...].astype(F32), cos_t, sa_t, sb_t) * lat).astype(_BF)
        dp_ref[:, O_K:O_V] = _unrope(dk_ref[...].T, cos_t, sa_t, sb_t).astype(_BF)
        dp_ref[:, O_V:O_GQ] = dv_ref[...].T.astype(_BF)
        dp_ref[:, O_GQ:O_GK] = ((gqf[...] + gqb[...]) * (DK ** -0.5)).astype(_BF)
        dp_ref[:, O_GK:O_GV] = (gkf[...] + gkb[...]).astype(_BF)
        dp_ref[:, O_GV:O_GG] = (gvf[...] + gvb[...]).astype(_BF)
        dp_ref[:, O_GG:O_Z] = (dgg_ref[...].astype(F32) * lat).astype(_BF)
        la_t = la_ref[...]
        dlg = (jnp.concatenate([dlf[...], dlb[...]], axis=1) * (1.0 - jnp.exp(GATE_TAU * la_t)) * (1.0 / GATE_TAU))
        bsum_ref[0:1, :] += jnp.sum(dlg, axis=0, keepdims=True)
        dlgb = dlg.astype(_BF)
        dlg_ref[...] = dlgb
        dp_ref[:, O_Z:NP] = _nt(dlgb, wg_ref[...]).astype(_BF)
        dh = _nn(dp_ref[...], w_ref[...])
        x = jnp.where(is_ctx, c_ref[...], x_ref[...])
        r = lax.rsqrt(jnp.mean(x * x, axis=-1, keepdims=True) + EPS)
        xh = x * r
        sdh = jnp.sum(dh, axis=0, keepdims=True)
        sdx = jnp.sum(dh * xh, axis=0, keepdims=True)
        sums_ref[0:1, :] += sdh * lat
        sums_ref[1:2, :] += sdx * lat
        sums_ref[2:3, :] += sdh * (1.0 - lat)
        sums_ref[3:4, :] += sdx * (1.0 - lat)
        dxh = dh * jnp.where(is_ctx, gmc_ref[...], gml_ref[...])
        gx_ref[...] = dx1_ref[...] + r * (dxh - xh * jnp.mean(dxh * xh, axis=-1, keepdims=True))

    vec = _full((1, D))
    tab = _rows(TE, 128)
    return pl.pallas_call(
        body, name="inproj_bwd", grid=(E // TE,),
        in_specs=[_rows_lat(TE, D), _full((CTX, D)), vec, vec, _full((NP, D)), _full((128, 512)), tab, tab, tab,
                  _rows(TE, 512),
                  _rows_lat(TE, QP), pl.BlockSpec((KP, TE), lambda i: (0, i)), pl.BlockSpec((KP, TE), lambda i: (0, i)),
                  _rows(TE, 256), _rows(TE, 256), _rows(TE, 256), _rows(TE, 256), _rows(TE, 512), _rows(TE, 512),
                  _rows_lat(TE, 512), _rows(TE, 256), _rows(TE, 256), _rows_lat(TE, D)],
        out_specs=[_rows(TE, NP), _rows(TE, 512), _rows_lat(TE, D), _full((8, D)), _full((8, 512))],
        out_shape=[jax.ShapeDtypeStruct((E, NP), _BF), jax.ShapeDtypeStruct((E, 512), _BF),
                   jax.ShapeDtypeStruct((S, D), F32), jax.ShapeDtypeStruct((8, D), F32),
                   jax.ShapeDtypeStruct((8, 512), F32)],
        compiler_params=_cp(("arbitrary",), VMEM_BIG),
    )(x, ctx, gml, gmc, win, wg, cos, sa, sb, la, dq, dk, dv, dgq_f, dgq_b, dgk_f, dgk_b, dgv_f, dgv_b,
      dgg, dla_f, dla_b, dx1)


def _matmul_tn(a, b, tk, tt, name, out_dtype, transpose_out=False, a_cols=None, hosted=(), gather=True):
    T, KA = a.shape
    N = b.shape[1]
    nt = T // tt
    k0 = 0
    if a_cols is not None:
        KA, k0 = tk, a_cols

    def body(a_ref, b_ref, o_ref, acc):
        t = pl.program_id(1)

        @pl.when(t == 0)
        def _():
            acc[...] = jnp.zeros_like(acc)

        acc[...] += _tn(a_ref[...], b_ref[...])

        @pl.when(t == nt - 1)
        def _():
            o_ref[...] = (acc[...].T if transpose_out else acc[...]).astype(out_dtype)

    if transpose_out:
        out_spec, out_shape = pl.BlockSpec((N, tk), lambda i, t: (0, i)), (N, KA)
    else:
        out_spec, out_shape = pl.BlockSpec((tk, N), lambda i, t: (i, 0)), (KA, N)
    res = _hosted_call(
        body, (a, b), hosted, gather, name=name, grid=(KA // tk, nt),
        in_specs=[pl.BlockSpec((tt, tk), lambda i, t: (t, i + k0)), pl.BlockSpec((tt, N), lambda i, t: (t, 0))],
        out_specs=[out_spec], out_shape=[jax.ShapeDtypeStruct(out_shape, out_dtype)],
        scratch_shapes=[pltpu.VMEM((tk, N), F32)],
        compiler_params=_cp(("arbitrary", "arbitrary"), VMEM_BIG))
    return res if hosted else res[0]


def _ada_bwd(c_all, c_ctx, w_ada, d_all):
    n = w_ada.shape[1]

    def body(c_ref, cc_ref, w_ref, d_ref, gw_ref, t_ref):
        c = jnp.concatenate([c_ref[...], jnp.broadcast_to(cc_ref[...], (8, D))], axis=0)
        db = d_ref[...].astype(_BF)
        gw_ref[0] = _tn((c * _sigmoid(c)).astype(_BF), db)
        t_ref[...] = _nt(db[8:16], w_ref[...].astype(_BF))

    return pl.pallas_call(
        body, name="ada_bwd", in_specs=[_full((8, D)), _full((1, D)), _full((D, n)), _full((16, n))],
        out_specs=[_full((1, D, n)), _full((8, D))],
        out_shape=[jax.ShapeDtypeStruct((1, D, n), F32), jax.ShapeDtypeStruct((8, D), F32)], grid=(1,),
        compiler_params=_cp(("arbitrary",)),
    )(c_all, c_ctx, w_ada, d_all)


PART_ROWS = 56
R_ADA, R_ADA_C, R_GAIN, R_SINK, R_BG, R_GGLA, R_LOSS, R_WG = 0, 6, 12, 16, 17, 18, 19, 24


def _small_grads(s_in, s_ffn, s_mix, ada_l, ada_c, gains, dsink, s_bg, g_wg, loss):
    def body(si, sf, sm, al, ac, g, ds, sbg, gwg, loss_ref, o_ref):
        o_ref[...] = jnp.zeros_like(o_ref)
        o_ref[R_LOSS:R_LOSS + 1, 0:128] = loss_ref[0:1, :]
        sub = lax.broadcasted_iota(jnp.int32, (8, 128), 0)
        lane = lax.broadcasted_iota(jnp.int32, (8, 128), 1)
        o_ref[R_SINK:R_SINK + 1, 0:128] = jnp.sum(jnp.where(sub == lane, ds[...], 0.0), axis=0, keepdims=True)
        o_ref[R_BG:R_BG + 1, 0:512] = sbg[0:1, :]
        y = sm[1:2, 0:128] + sm[1:2, 128:256] + sm[1:2, 256:384] + sm[1:2, 384:512]
        y = y + pltpu.roll(y, 64, 1)
        o_ref[R_GGLA:R_GGLA + 1, 0:128] = jnp.where(lane[0:1] < DV, y, 0.0)
        o_ref[R_WG:R_WG + 16, 0:256] = gwg[0:16, 0:256]
        o_ref[R_WG + 16:R_WG + 32, 0:256] = gwg[16:32, 256:512]
        sdh_l, sdx_l, sdh_c, sdx_c = si[0:1], si[1:2], si[2:3], si[3:4]
        sdh2, sdx2, a2 = sf[0:1], sf[1:2], sf[2:3]
        a1 = sm[0:1]
        g1, g2, g3, g4 = g[0:1], g[1:2], g[2:3], g[3:4]
        sc1, gt1, sc2, gt2 = al[1:2], al[2:3], al[4:5], al[5:6]
        sc1c = ac[1:2]
        z = jnp.zeros((1, D), F32)
        rows = [sdh_l, sdx_l * g1, a1 * g2, sdh2, sdx2 * g3, a2 * g4,
                sdh_c, sdx_c * g1, z, z, z, z,
                sdx_l * (1.0 + sc1) + sdx_c * (1.0 + sc1c), a1 * gt1, sdx2 * (1.0 + sc2), a2 * gt2]
        for r, v in enumerate(rows):
            o_ref[r:r + 1, :] = v

    v8 = _full((8, D))
    return pl.pallas_call(
        body, name="small_grads",
        in_specs=[v8] * 6 + [_full((8, 128)), _full((8, 512)), _full((128, 512)), _full((8, 128))],
        out_specs=_full((PART_ROWS, D)), out_shape=jax.ShapeDtypeStruct((PART_ROWS, D), F32), grid=(1,),
        compiler_params=_cp(("arbitrary",)),
    )(s_in, s_ffn, s_mix, ada_l, ada_c, gains, dsink, s_bg, g_wg, loss)


def _row_tile(R):
    for cand in (256, 128, 64, 32, 16):
        if R % cand == 0 and R > cand:
            return cand
    return R


def _adamw(w, gs, m, v, name):
    _, R, C = w.shape
    tr = _row_tile(R)
    c1 = 1.0 / (1.0 - ADAM_B1 ** ADAM_STEP)
    c2 = 1.0 / (1.0 - ADAM_B2 ** ADAM_STEP)
    ng = len(gs)

    def body(w_ref, *refs):
        g_refs, (m_ref, v_ref, go_ref, d_ref, nm_ref, nv_ref) = refs[:ng], refs[ng:]
        c0 = 0
        for g_ref in g_refs:
            cols = slice(c0, c0 + g_ref.shape[2])
            c0 += g_ref.shape[2]
            gg = g_ref[0].astype(F32)
            for j in range(1, g_ref.shape[0]):
                gg = gg + g_ref[j].astype(F32)
            go_ref[0, :, cols] = gg
            nm = ADAM_B1 * m_ref[0, :, cols] + (1.0 - ADAM_B1) * gg
            nv = ADAM_B2 * v_ref[0, :, cols] + (1.0 - ADAM_B2) * (gg * gg)
            nm_ref[0, :, cols] = nm
            nv_ref[0, :, cols] = nv
            d_ref[0, :, cols] = -ADAM_LR * ((nm * c1) / (jnp.sqrt(nv * c2) + ADAM_EPS) + ADAM_WD * w_ref[0, :, cols])

    spec = pl.BlockSpec((1, tr, C), lambda i: (0, i, 0))
    sds = jax.ShapeDtypeStruct((1, R, C), F32)
    g_specs = [pl.BlockSpec((g.shape[0], tr, g.shape[2]), lambda i: (0, i, 0)) for g in gs]
    return pl.pallas_call(
        body, name=name, grid=(R // tr,), in_specs=[spec] + g_specs + [spec, spec], out_specs=[spec] * 4,
        out_shape=[sds] * 4, compiler_params=_cp(("parallel",), 48 * 1024 * 1024),
    )(w, *gs, m, v)


def _sum_slots(slots, name):
    _, R, C = slots.shape
    tr = _row_tile(R)

    def body(s_ref, o_ref):
        acc = s_ref[0].astype(F32)
        for j in range(1, N_DEV):
            acc = acc + s_ref[j].astype(F32)
        o_ref[...] = acc

    return pl.pallas_call(
        body, name=name, grid=(R // tr,), in_specs=[pl.BlockSpec((N_DEV, tr, C), lambda i: (0, i, 0))],
        out_specs=_rows(tr, C), out_shape=jax.ShapeDtypeStruct((R, C), F32), compiler_params=_cp(("parallel",)),
    )(slots)


def _ag2_start(x_ref, out_ref, send_sems, recv_sems, local_sem):
    x, y, c = lax.axis_index("x"), lax.axis_index("y"), lax.axis_index("c")
    me, sibling = (x, y, c), (x, y, 1 - c)
    chips = [(1 - x, y), (x, 1 - y), (1 - x, 1 - y)]

    def rows(px, py, pc):
        return out_ref.at[4 * px + 2 * py + pc]

    def copy(k, block, to, src=None):
        return pltpu.make_async_remote_copy(
            src_ref=rows(*block) if src is None else src, dst_ref=rows(*block),
            send_sem=send_sems.at[k], recv_sem=recv_sems.at[k], device_id=to, device_id_type=MESH)

    mine = pltpu.make_async_copy(x_ref, rows(*me), local_sem)
    mine.start()
    first = [copy(0, me, sibling, src=x_ref)]
    first += [copy(1 + j, me, (*chip, c), src=x_ref) for j, chip in enumerate(chips)]
    for cp in first:
        cp.start()
    return copy, mine, first, me, sibling, chips, c


def _ag2_finish(state):
    copy, mine, first, me, sibling, chips, c = state
    passed = [copy(4 + j, (*chip, c), sibling) for j, chip in enumerate(chips)]
    for j, chip in enumerate(chips):
        copy(1 + j, (*chip, c), me).wait_recv()
        passed[j].start()
    copy(0, sibling, me).wait_recv()
    for j, chip in enumerate(chips):
        copy(4 + j, (*chip, 1 - c), me).wait_recv()
    for cp in first + passed:
        cp.wait_send()
    mine.wait()


def _exchange(arrays, name, gather):
    na = len(arrays)

    def body(*refs):
        cps = _xchg_copies(refs[:na], refs[na:2 * na], *refs[2 * na:], gather=gather)
        _xchg_start(cps)
        _xchg_finish(cps)

    anyspec = pl.BlockSpec(memory_space=pl.ANY)
    return pl.pallas_call(
        body, name=name, out_shape=_xchg_out_shapes(arrays, gather), in_specs=[anyspec] * na,
        out_specs=[anyspec] * na, scratch_shapes=_xchg_scratch(na),
    )(*arrays)


def _entry(c, wg_sh, win_sh, c_ctx, w_ada):
    n = w_ada.shape[1]

    def body(c_ref, g_ref, w_ref, cc_ref, wa_ref, call_ref, gall_ref, wall_ref, ada_ref, part,
             s_send, s_recv, s_loc, w_send, w_recv, w_loc, a_send, a_recv, a_loc):
        big = _ag2_start(w_ref, wall_ref, w_send, w_recv, w_loc)
        small = _xchg_copies([c_ref, g_ref], [call_ref, gall_ref], s_send, s_recv, s_loc, gather=True)
        _xchg_start(small)
        _xchg_finish(small)
        cs = jnp.concatenate([call_ref[:, 0, :], jnp.broadcast_to(cc_ref[...], (8, D))], axis=0)
        part[...] = _nn((cs * _sigmoid(cs)).astype(_BF), wa_ref[...].astype(_BF))
        ada = _xchg_copies([part], [ada_ref], a_send, a_recv, a_loc, gather=True)
        _xchg_start(ada)
        _xchg_finish(ada)
        _ag2_finish(big)

    vm = pl.BlockSpec(memory_space=pltpu.VMEM)
    return pl.pallas_call(
        body, name="entry_gather",
        out_shape=[jax.ShapeDtypeStruct((N_DEV,) + c.shape, F32), jax.ShapeDtypeStruct((N_DEV,) + wg_sh.shape, F32),
                   jax.ShapeDtypeStruct((N_DEV,) + win_sh.shape, win_sh.dtype),
                   jax.ShapeDtypeStruct((N_DEV, 16, n), F32)],
        in_specs=[vm] * 5, out_specs=[vm] * 4,
        scratch_shapes=[pltpu.VMEM((16, n), F32)] + _xchg_scratch(2)
        + [pltpu.SemaphoreType.DMA((7,)), pltpu.SemaphoreType.DMA((7,)), pltpu.SemaphoreType.DMA] + _xchg_scratch(1),
        compiler_params=pltpu.CompilerParams(vmem_limit_bytes=VMEM_BIG),
    )(c, wg_sh, win_sh, c_ctx, w_ada)


def _rope_tables(S):
    t = np.arange(S)
    row = (t // GRID_W).astype(np.float32)
    colp = (t % GRID_W).astype(np.float32)
    half = HD // 2
    inv = (ROPE_BASE ** (-np.arange(0, half, 2, dtype=np.float32) / half)).astype(np.float32)
    ar = row[:, None] * inv[None, :]
    ac = colp[:, None] * inv[None, :]
    ang = np.concatenate([ar, ar, ac, ac], axis=-1).astype(np.float32)
    cos = np.cos(ang).astype(np.float32)
    sin = np.sin(ang).astype(np.float32)
    lane = np.arange(HD)
    first = (lane % 32) < 16
    sa = np.where(first[None, :], -sin, 0.0)
    sb = np.where(first[None, :], 0.0, sin)

    def ext(tab, ctx_val):
        full = np.zeros((CTX + S, 128), np.float32)
        full[:CTX, :] = ctx_val
        full[CTX:, :HD] = tab
        full[CTX:, HD:] = tab
        return jnp.asarray(full)

    return ext(cos, 1.0), ext(sa, 0.0), ext(sb, 0.0)


def _pad_rows_win(wt):
    return jnp.pad(wt, ((0, NP - IN_COLS), (0, 0)))


def _unpad_rows_win(g):
    return g[0:IN_COLS]


def _local_step(x, ctx, target, ada_l, ada_c, gains, sink, win_p, wg_bd, bg, ggla, wout_sh, wffi_sh, wffo_sh):
    S = x.shape[0]
    cos, sa, sb = _rope_tables(S)
    g1, g2, g3, g4 = (gains[i:i + 1] for i in range(4))
    sh1, sc1, gt1, sh2, sc2, gt2 = (ada_l[i:i + 1] for i in range(6))
    sh1c, sc1c = ada_c[0:1], ada_c[1:2]
    gml, gmc, gm2 = g1 * (1.0 + sc1), g1 * (1.0 + sc1c), g3 * (1.0 + sc2)
    mavg = jnp.asarray(np.kron(np.eye(N_GLA, dtype=np.float32), np.full((DV, DV), 1.0 / DV, np.float32))).astype(_BF)

    n_ffi, r_ffo, r_out = wffi_sh.shape[0], wffo_sh.shape[0], wout_sh.shape[0]
    tt_e = 768 if (S + CTX) % 768 == 0 else 256
    tt_s = 512 if S % 512 == 0 else 256
    h, q, k, v, gq, gk, gv, gg, z, la, wout_g = _inproj_fwd(x, ctx, gml, sh1, gmc, sh1c, win_p, wg_bd, bg,
                                                            cos, sa, sb, [wout_sh])
    attn, lse, wffi_g = _attn_fwd(q, k, v, sink, [wffi_sh])
    o_f, st_f, wffo_g = _gla_fwd(gq, gk, gv, la, False, [wffo_sh])
    o_b, st_b = _gla_fwd(gq, gk, gv, la, True)
    wout = wout_g.reshape(N_DEV * r_out, D)
    wffi = wffi_g.reshape(N_DEV * n_ffi, D)
    wffo = wffo_g.reshape(N_DEV * r_ffo, D)
    x1, mix = _mix_fwd(x, attn, o_f, o_b, gg, ggla, mavg, wout, gt1, g2)
    dx1, h2, du, act, df, s_ffn, loss = _ffn(x1, target, gm2, sh2, gt2, g4, wffi, wffo)
    slab_ffi = _matmul_tn(h2, du, 512, tt_s, "grad_w_ffn_in", _BF, True).reshape(N_DEV, n_ffi, D)
    slab_ffo = _matmul_tn(act, df, FFN, tt_s, "grad_w_ffn_out", _BF).reshape(N_DEV, r_ffo, D)
    d_attn, do_gla, dgg, dy, s_mix = _mix_bwd(dx1, mix, o_f, o_b, gg, ggla, mavg, wout, gt1, g2)
    slab_out = _matmul_tn(mix, dy, D, tt_s, "grad_w_out", _BF).reshape(N_DEV, r_out, D)
    dq, dk, dv, dsink, got_ffi, got_ffo = _attn_bwd(q, k, v, sink, lse, d_attn, [slab_ffi, slab_ffo])
    dgq_f, dgk_f, dgv_f, dla_f, got_out = _gla_bwd(gq, gk, gv, la, st_f, do_gla, False, [slab_out])
    dgq_b, dgk_b, dgv_b, dla_b = _gla_bwd(gq, gk, gv, la, st_b, do_gla, True)
    dp, dlg, grad_x, s_in, s_bg = _inproj_bwd(x, ctx, gml, gmc, win_p, wg_bd, cos, sa, sb, la, dq, dk, dv,
                                              dgq_f, dgq_b, dgk_f, dgk_b, dgv_f, dgv_b, dgg, dla_f, dla_b, dx1)
    g_wg = _matmul_tn(z, dlg, 128, tt_e, "grad_w_gate", F32)
    small = _small_grads(s_in, s_ffn, s_mix, ada_l, ada_c, gains, dsink, s_bg, g_wg, loss)
    n_in, half = IN_COLS // N_DEV, D // 2
    g_a, parts = _matmul_tn(h, dp, half, tt_e, "grad_w_in_a", _BF, True, a_cols=0, hosted=[small])
    slab_a = _unpad_rows_win(g_a).reshape(N_DEV, n_in, half)
    g_b, got_a = _matmul_tn(h, dp, half, tt_e, "grad_w_in_b", _BF, True, a_cols=1, hosted=[slab_a], gather=False)
    got_b, = _exchange([_unpad_rows_win(g_b).reshape(N_DEV, n_in, half)], "scatter_grads", False)
    return dict(grad_x=grad_x, got_in=[got_a, got_b], got_out=got_out, got_ffi=got_ffi, got_ffo=got_ffo, parts=parts)


SMALL_NAMES = ["c_ctx", "b_ada", "g_pre_mix", "g_post_mix", "g_pre_ffn", "g_post_ffn", "attn_sink",
               "b_gate_fwd", "b_gate_bwd", "g_gla_norm", "w_gate_fwd", "w_gate_bwd"]


def _small_update(tot, t_tot, wg_g, w, m, v):
    c1 = 1.0 / (1.0 - ADAM_B1 ** ADAM_STEP)
    c2 = 1.0 / (1.0 - ADAM_B2 ** ADAM_STEP)
    n = len(SMALL_NAMES)

    def body(tot_ref, t_ref, wg_ref, *refs):
        w_r, m_r, v_r = refs[0:n], refs[n:2 * n], refs[2 * n:3 * n]
        g_o, d_o, nm_o, nv_o = refs[3 * n:4 * n], refs[4 * n:5 * n], refs[5 * n:6 * n], refs[6 * n:7 * n]

        def upd(i, idx, g):
            nm = ADAM_B1 * m_r[i][idx] + (1.0 - ADAM_B1) * g
            nv = ADAM_B2 * v_r[i][idx] + (1.0 - ADAM_B2) * (g * g)
            g_o[i][idx] = g
            nm_o[i][idx] = nm
            nv_o[i][idx] = nv
            d_o[i][idx] = -ADAM_LR * ((nm * c1) / (jnp.sqrt(nv * c2) + ADAM_EPS) + ADAM_WD * w_r[i][idx])

        everything = (slice(None), slice(None))
        cc = w_r[0][...]
        sc = _sigmoid(cc)
        upd(0, everything, t_ref[0:1, :] * (sc * (1.0 + cc * (1.0 - sc))))
        for j in range(6):
            upd(1, (slice(None), slice(D * j, D * j + D)),
                tot_ref[R_ADA + j:R_ADA + j + 1, :] + tot_ref[R_ADA_C + j:R_ADA_C + j + 1, :])
        for j in range(4):
            upd(2 + j, everything, tot_ref[R_GAIN + j:R_GAIN + j + 1, :])
        upd(6, everything, tot_ref[R_SINK:R_SINK + 1, 0:N_ATT])
        upd(7, everything, tot_ref[R_BG:R_BG + 1, 0:256])
        upd(8, everything, tot_ref[R_BG:R_BG + 1, 256:512])
        upd(9, everything, tot_ref[R_GGLA:R_GGLA + 1, 0:DV])
        upd(10, (0,), wg_ref[0:GATE_RANK, :])
        upd(11, (0,), wg_ref[GATE_RANK:2 * GATE_RANK, :])

    params = [w[k] for k in SMALL_NAMES] + [m[k] for k in SMALL_NAMES] + [v[k] for k in SMALL_NAMES]
    outs = pl.pallas_call(
        body, name="small_update", grid=(1,),
        in_specs=[_full(tot.shape), _full(t_tot.shape), _full(wg_g.shape)] + [_full(p.shape) for p in params],
        out_specs=[_full(w[k].shape) for k in SMALL_NAMES] * 4,
        out_shape=[jax.ShapeDtypeStruct(w[k].shape, F32) for k in SMALL_NAMES] * 4,
        compiler_params=_cp(("arbitrary",)),
    )(tot, t_tot, wg_g, *params)
    return tuple(dict(zip(SMALL_NAMES, outs[i * n:(i + 1) * n])) for i in range(4))


def kernel(x, c, ctx, c_ctx, w_ada, b_ada, g_pre_mix, g_post_mix, g_pre_ffn, g_post_ffn, w_in, attn_sink, w_gate_fwd, b_gate_fwd, w_gate_bwd, b_gate_bwd, g_gla_norm, w_out, w_ffn_in, w_ffn_out, loss_target, m_c_ctx, m_w_ada, m_b_ada, m_g_pre_mix, m_g_post_mix, m_g_pre_ffn, m_g_post_ffn, m_w_in, m_attn_sink, m_w_gate_fwd, m_b_gate_fwd, m_w_gate_bwd, m_b_gate_bwd, m_g_gla_norm, m_w_out, m_w_ffn_in, m_w_ffn_out, v_c_ctx, v_w_ada, v_b_ada, v_g_pre_mix, v_g_post_mix, v_g_pre_ffn, v_g_post_ffn, v_w_in, v_attn_sink, v_w_gate_fwd, v_b_gate_fwd, v_w_gate_bwd, v_b_gate_bwd, v_g_gla_norm, v_w_out, v_w_ffn_in, v_w_ffn_out):
    me = 4 * lax.axis_index("x") + 2 * lax.axis_index("y") + lax.axis_index("c")
    S = x.shape[1]
    n_in = w_in.shape[2]
    n_ffi = w_ffn_in.shape[2]
    r_out = w_out.shape[1]
    r_ffo = w_ffn_out.shape[1]
    n_ada = w_ada.shape[2]

    wg_sh = jnp.concatenate([w_gate_fwd.reshape(4, 128), w_gate_bwd.reshape(4, 128)], axis=0)
    c_all3, g_all, w_all, ada_all = _entry(c, wg_sh, w_in[0].T.astype(_BF), c_ctx.reshape(1, D), w_ada[0])
    c_all = c_all3.reshape(N_DEV, D)
    wgf = g_all[:, 0:4].reshape(N_DEV, GATE_RANK, 32).transpose(1, 0, 2).reshape(GATE_RANK, 256)
    wgb = g_all[:, 4:8].reshape(N_DEV, GATE_RANK, 32).transpose(1, 0, 2).reshape(GATE_RANK, 256)
    win_p = _pad_rows_win(w_all.reshape(N_DEV * n_in, D))
    wg_bd = jnp.zeros((128, 512), F32).at[0:16, 0:256].set(wgf).at[16:32, 256:512].set(wgb).astype(_BF)
    ada_full = ada_all.transpose(1, 0, 2).reshape(16, N_DEV * n_ada) + b_ada
    ada_l = jnp.pad(lax.dynamic_slice_in_dim(ada_full, me, 1, 0).reshape(6, D), ((0, 2), (0, 0)))
    ada_c = jnp.pad(ada_full[8].reshape(6, D), ((0, 2), (0, 0)))
    gains = jnp.pad(jnp.concatenate([g_pre_mix, g_post_mix, g_pre_ffn, g_post_ffn], axis=0), ((0, 4), (0, 0)))
    sink = jnp.broadcast_to(attn_sink.reshape(8, 1), (8, 128))
    bg = jnp.concatenate([b_gate_fwd, b_gate_bwd], axis=1)
    ggla = jnp.tile(g_gla_norm, (1, N_GLA))

    r = _local_step(x[0], ctx[0], loss_target[0], ada_l, ada_c, gains, sink, win_p, wg_bd, bg, ggla,
                    w_out[0].astype(_BF), w_ffn_in[0].T.astype(_BF), w_ffn_out[0].astype(_BF))

    parts = r["parts"]
    tot = _sum_slots(parts, "sum_small_grads")
    loss = tot[R_LOSS, 0]
    d_ada_rows = parts[:, R_ADA:R_ADA + 6].reshape(N_DEV, 6 * D)
    d_ada_c = tot[R_ADA_C:R_ADA_C + 6].reshape(1, 6 * D)
    my_cols = lax.dynamic_slice_in_dim(jnp.concatenate([d_ada_rows, jnp.broadcast_to(d_ada_c, (1, 6 * D)),
                                                        jnp.zeros((7, 6 * D), F32)], axis=0), me * n_ada, n_ada, 1)
    grad_w_ada, t_part = _ada_bwd(c_all, c_ctx.reshape(1, D), w_ada[0], my_cols)
    wg_g = lax.dynamic_slice(tot, (R_WG, me * 32), (2 * GATE_RANK, 32))

    tr = lambda a: jnp.transpose(a, (0, 2, 1))
    big = {}
    t_all, = _exchange([t_part], "gather_c_ctx", True)
    t_tot = _sum_slots(t_all, "sum_c_ctx")
    for nm, w, g, m, v in [("w_ada", w_ada, grad_w_ada, m_w_ada, v_w_ada),
                           ("w_out", w_out, r["got_out"], m_w_out, v_w_out),
                           ("w_ffn_out", w_ffn_out, r["got_ffo"], m_w_ffn_out, v_w_ffn_out)]:
        big[nm] = _adamw(w, [g], m, v, "adamw_" + nm)
    big["w_ffn_in"] = tuple(tr(o) for o in _adamw(tr(w_ffn_in), [r["got_ffi"]], tr(m_w_ffn_in), tr(v_w_ffn_in),
                                                  "adamw_w_ffn_in"))
    big["w_in"] = tuple(tr(o) for o in _adamw(tr(w_in), r["got_in"], tr(m_w_in), tr(v_w_in), "adamw_w_in"))

    w_small = dict(c_ctx=c_ctx.reshape(1, D), b_ada=b_ada, g_pre_mix=g_pre_mix, g_post_mix=g_post_mix, g_pre_ffn=g_pre_ffn,
                   g_post_ffn=g_post_ffn, attn_sink=attn_sink, b_gate_fwd=b_gate_fwd, b_gate_bwd=b_gate_bwd,
                   g_gla_norm=g_gla_norm, w_gate_fwd=w_gate_fwd, w_gate_bwd=w_gate_bwd)
    m_small = dict(c_ctx=m_c_ctx.reshape(1, D), b_ada=m_b_ada, g_pre_mix=m_g_pre_mix, g_post_mix=m_g_post_mix,
                   g_pre_ffn=m_g_pre_ffn, g_post_ffn=m_g_post_ffn, attn_sink=m_attn_sink, b_gate_fwd=m_b_gate_fwd,
                   b_gate_bwd=m_b_gate_bwd, g_gla_norm=m_g_gla_norm, w_gate_fwd=m_w_gate_fwd, w_gate_bwd=m_w_gate_bwd)
    v_small = dict(c_ctx=v_c_ctx.reshape(1, D), b_ada=v_b_ada, g_pre_mix=v_g_pre_mix, g_post_mix=v_g_post_mix,
                   g_pre_ffn=v_g_pre_ffn, g_post_ffn=v_g_post_ffn, attn_sink=v_attn_sink, b_gate_fwd=v_b_gate_fwd,
                   b_gate_bwd=v_b_gate_bwd, g_gla_norm=v_g_gla_norm, w_gate_fwd=v_w_gate_fwd, w_gate_bwd=v_w_gate_bwd)
    grads_small, d_s, nm_s, nv_s = _small_update(tot, t_tot, wg_g, w_small, m_small, v_small)
    for dd in (grads_small, d_s, nm_s, nv_s):
        dd["c_ctx"] = dd["c_ctx"].reshape(D)

    order = ["c_ctx", "w_ada", "b_ada", "g_pre_mix", "g_post_mix", "g_pre_ffn", "g_post_ffn", "w_in", "attn_sink",
             "w_gate_fwd", "b_gate_fwd", "w_gate_bwd", "b_gate_bwd", "g_gla_norm", "w_out", "w_ffn_in", "w_ffn_out"]
    grads, deltas, new_m, new_v = [], [], [], []
    for nm in order:
        if nm in big:
            g_, d_, m_, v_ = big[nm]
        else:
            g_, d_, m_, v_ = grads_small[nm], d_s[nm], nm_s[nm], nv_s[nm]
        grads.append(g_)
        deltas.append(d_)
        new_m.append(m_)
        new_v.append(v_)
    return (loss, r["grad_x"][None], *grads, *deltas, *new_m, *new_v)
```

```python
import functools
import math

import numpy as np
import jax
import jax.numpy as jnp
from jax import lax
from jax.experimental import pallas as pl
from jax.experimental.pallas import tpu as pltpu

F32 = jnp.float32
_BF = jnp.bfloat16

N_DEV = 8
D = 1024
CTX = 256
HD = 64
N_ATT = 8
N_KV = 2
GRP = N_ATT // N_KV
WIN = 128
GRID_W = 64
ROPE_BASE = 10000.0
N_GLA = 8
DK = 32
DV = 64
GATE_RANK = 16
GATE_TAU = 16.0
FFN = 2816
EPS = 1e-6
NEG = -1e30
GLA_T = 128

QP = N_ATT * HD
KP = N_KV * HD
O_Q, O_K, O_V = 0, QP, QP + KP
O_GQ = O_V + KP
O_GK = O_GQ + N_GLA * DK
O_GV = O_GK + N_GLA * DK
O_GG = O_GV + N_GLA * DV
O_Z = O_GG + N_GLA * DV
NP = O_Z + 128
IN_COLS = 2336

ADAM_LR, ADAM_B1, ADAM_B2, ADAM_EPS, ADAM_WD, ADAM_STEP = 0.001, 0.9, 0.999, 1e-08, 0.01, 10

VMEM_BIG = 56 * 1024 * 1024
MESH = pl.DeviceIdType.MESH


def _cp(sem, vmem=None):
    return pltpu.CompilerParams(dimension_semantics=sem, vmem_limit_bytes=vmem)


def _full(shape):
    nd = len(shape)
    return pl.BlockSpec(shape, lambda *a: (0,) * nd)


def _rows(tile, width, off=0):
    return pl.BlockSpec((tile, width), lambda i: (i + off, 0))


def _rows_lat(tile, width):
    return pl.BlockSpec((tile, width), lambda i: (jnp.maximum(i - 1, 0), 0))


def _nt(a, b):
    return lax.dot_general(a, b, (((1,), (1,)), ((), ())), preferred_element_type=F32)


def _tn(a, b):
    return lax.dot_general(a, b, (((0,), (0,)), ((), ())), preferred_element_type=F32)


def _nn(a, b):
    return jnp.dot(a, b, preferred_element_type=F32)


def _head_mean(x, mavg):
    n = x.shape[0]
    hi = x.astype(_BF)
    lo = (x - hi.astype(F32)).astype(_BF)
    y = _nn(jnp.concatenate([hi, lo], axis=0), mavg)
    return y[0:n] + y[n:2 * n]


def _rope(t, cos, sa, sb):
    n = t.shape[1]
    reps = n // 128
    c = jnp.tile(cos, (1, reps))
    a = jnp.tile(sa, (1, reps))
    b = jnp.tile(sb, (1, reps))
    return t * c + pltpu.roll(t, n - 16, 1) * a + pltpu.roll(t, 16, 1) * b


def _unrope(t, cos, sa, sb):
    n = t.shape[1]
    reps = n // 128
    c = jnp.tile(cos, (1, reps))
    a = jnp.tile(sa, (1, reps))
    b = jnp.tile(sb, (1, reps))
    return t * c + pltpu.roll(t * a, 16, 1) + pltpu.roll(t * b, n - 16, 1)


def _sigmoid(x):
    return 1.0 / (1.0 + jnp.exp(-x))


def _inproj_fwd(x, ctx, gml, shl, gmc, shc, win, wg, bg, cos, sa, sb, shards):
    E = x.shape[0] + CTX
    TE = CTX

    def body(x_ref, c_ref, gml_ref, shl_ref, gmc_ref, shc_ref, w_ref, wg_ref, bg_ref, cos_ref, sa_ref, sb_ref,
             h_ref, q_ref, k_ref, v_ref, gq_ref, gk_ref, gv_ref, gg_ref, z_ref, la_ref):
        is_ctx = pl.program_id(0) == 0
        gm = jnp.where(is_ctx, gmc_ref[...], gml_ref[...])
        sh = jnp.where(is_ctx, shc_ref[...], shl_ref[...])
        x = jnp.where(is_ctx, c_ref[...], x_ref[...])
        r = lax.rsqrt(jnp.mean(x * x, axis=-1, keepdims=True) + EPS)
        hb = ((x * r) * gm + sh).astype(_BF)
        h_ref[...] = hb
        p = _nt(hb, w_ref[...])
        cos_t, sa_t, sb_t = cos_ref[...], sa_ref[...], sb_ref[...]
        q_ref[...] = (_rope(p[:, O_Q:O_K], cos_t, sa_t, sb_t) * (HD ** -0.5)).astype(_BF)
        k_ref[...] = _rope(p[:, O_K:O_V], cos_t, sa_t, sb_t).astype(_BF)
        v_ref[...] = p[:, O_V:O_GQ].astype(_BF)
        gq_ref[...] = p[:, O_GQ:O_GK] * (DK ** -0.5)
        gk_ref[...] = p[:, O_GK:O_GV]
        gv_ref[...] = p[:, O_GV:O_GG]
        gg_ref[...] = p[:, O_GG:O_Z]
        zb = p[:, O_Z:NP].astype(_BF)
        z_ref[...] = zb
        lg = _nn(zb, wg_ref[...]) + bg_ref[...]
        la_ref[...] = (jnp.minimum(lg, 0.0) - jnp.log(1.0 + jnp.exp(-jnp.abs(lg)))) * (1.0 / GATE_TAU)

    vec = _full((1, D))
    tab = _rows(TE, 128)
    outs = [(D, _BF), (QP, _BF), (KP, _BF), (KP, _BF), (256, F32), (256, F32), (512, F32), (512, F32),
            (128, _BF), (512, F32)]
    return _hosted_call(
        body, (x, ctx, gml, shl, gmc, shc, win, wg, bg, cos, sa, sb), shards, True,
        name="inproj_fwd", grid=(E // TE,),
        in_specs=[_rows_lat(TE, D), _full((CTX, D)), vec, vec, vec, vec, _full((NP, D)), _full((128, 512)),
                  _full((1, 512)), tab, tab, tab],
        out_specs=[_rows(TE, w) for w, _ in outs],
        out_shape=[jax.ShapeDtypeStruct((E, w), dt) for w, dt in outs],
        compiler_params=_cp(("arbitrary",), 40 * 1024 * 1024))


def _xchg_scratch(na):
    return [pltpu.SemaphoreType.DMA((na, N_DEV - 1)), pltpu.SemaphoreType.DMA((na, N_DEV - 1)),
            pltpu.SemaphoreType.DMA((na,))]


def _xchg_copies(ins, outs, send_sems, recv_sems, local_sems, gather):
    x, y, c = lax.axis_index("x"), lax.axis_index("y"), lax.axis_index("c")
    me = 4 * x + 2 * y + c
    local, sends, recvs = [], [], []
    for a in range(len(ins)):
        local.append(pltpu.make_async_copy(ins[a] if gather else ins[a].at[me], outs[a].at[me], local_sems.at[a]))
    for k in range(1, N_DEV):
        px, py, pc = x ^ (k >> 2), y ^ ((k >> 1) & 1), c ^ (k & 1)
        peer = 4 * px + 2 * py + pc
        for a in range(len(ins)):
            sems = dict(send_sem=send_sems.at[a, k - 1], recv_sem=recv_sems.at[a, k - 1], device_id_type=MESH)
            sends.append(pltpu.make_async_remote_copy(
                src_ref=ins[a] if gather else ins[a].at[peer], dst_ref=outs[a].at[me], device_id=(px, py, pc), **sems))
            recvs.append(pltpu.make_async_remote_copy(
                src_ref=ins[a] if gather else ins[a].at[me], dst_ref=outs[a].at[peer], device_id=(x, y, c), **sems))
    return local, sends, recvs


def _xchg_start(cps):
    local, sends, _ = cps
    for cp in local + sends:
        cp.start()


def _xchg_finish(cps):
    local, sends, recvs = cps
    for cp in recvs:
        cp.wait_recv()
    for cp in sends:
        cp.wait_send()
    for cp in local:
        cp.wait()


def _xchg_out_shapes(ins, gather):
    return [jax.ShapeDtypeStruct(((N_DEV,) + s.shape) if gather else s.shape, s.dtype) for s in ins]


def _hosted_call(body, args, hosted, gather, *, grid, in_specs, out_specs, out_shape, scratch_shapes=(), **kw):
    na = len(hosted)
    if na == 0:
        return pl.pallas_call(body, grid=grid, in_specs=in_specs, out_specs=out_specs, out_shape=out_shape,
                              scratch_shapes=list(scratch_shapes), **kw)(*args)
    n_in, n_out, n_scr = len(in_specs), len(out_specs), len(scratch_shapes)

    def wrapped(*refs):
        ins, h_in = refs[:n_in], refs[n_in:n_in + na]
        outs, h_out = refs[n_in + na:n_in + na + n_out], refs[n_in + na + n_out:n_in + 2 * na + n_out]
        scr = refs[n_in + 2 * na + n_out:]
        cps = _xchg_copies(h_in, h_out, *scr[n_scr:], gather=gather)
        pids = [pl.program_id(a) for a in range(len(grid))]
        first = functools.reduce(jnp.logical_and, [p == 0 for p in pids])
        last = functools.reduce(jnp.logical_and, [p == g - 1 for p, g in zip(pids, grid)])

        @pl.when(first)
        def _():
            _xchg_start(cps)

        body(*ins, *outs, *scr[:n_scr])

        @pl.when(last)
        def _():
            _xchg_finish(cps)

    anyspec = pl.BlockSpec(memory_space=pl.ANY)
    return pl.pallas_call(
        wrapped, grid=grid, in_specs=list(in_specs) + [anyspec] * na, out_specs=list(out_specs) + [anyspec] * na,
        out_shape=list(out_shape) + _xchg_out_shapes(hosted, gather),
        scratch_shapes=list(scratch_shapes) + _xchg_scratch(na), **kw)(*args, *hosted)


def _attn_specs(E):
    nb = (E - CTX) // WIN
    last = E // WIN - 1
    kc = pl.BlockSpec((CTX, KP), lambda n: (0, 0))
    kp = pl.BlockSpec((WIN, KP), lambda n: (n + 1, 0))
    kk = pl.BlockSpec((WIN, KP), lambda n: (n + 2, 0))
    kn = pl.BlockSpec((WIN, KP), lambda n: (jnp.minimum(n + 3, last), 0))
    return nb, [kc, kp, kk, kn]


def _attn_bias(nb):
    rows = np.arange(GRP * WIN)[:, None] % WIN
    cols = np.arange(CTX + 3 * WIN)[None, :]
    j = cols - CTX
    band = np.abs(j - WIN - rows) <= WIN
    out = []
    for first, last in ((True, False), (False, False), (False, True)):
        ok = (cols < CTX) | (band & ((j >= WIN) | (not first)) & ((j < 2 * WIN) | (not last)))
        out.append(np.where(ok, 0.0, NEG).astype(np.float32))
    bias = jnp.asarray(np.stack(out))
    spec = pl.BlockSpec((1, GRP * WIN, CTX + 3 * WIN),
                        lambda n: (jnp.where(n == 0, 0, jnp.where(n == nb - 1, 2, 1)), 0, 0))
    return bias, spec


def _both_halves(t, h):
    tf = t.astype(F32)
    r = pltpu.roll(tf, HD, 1)
    lo = lax.broadcasted_iota(jnp.int32, tf.shape, 1) < HD
    return (jnp.where(lo, tf, r) if h == 0 else jnp.where(lo, r, tf)).astype(t.dtype)


def _stack_heads(ref, h):
    lo = lax.broadcasted_iota(jnp.int32, (WIN, 128), 1) < HD
    parts = []
    for g in range(GRP):
        j = GRP * h + g
        t = ref[:, 128 * (j // 2):128 * (j // 2) + 128].astype(F32)
        parts.append(jnp.where(lo if j % 2 == 0 else jnp.logical_not(lo), t, 0.0))
    return jnp.concatenate(parts, axis=0)


def _unstack_pair(o, pp):
    lo = lax.broadcasted_iota(jnp.int32, (WIN, 128), 1) < HD
    return jnp.where(lo, o[WIN * 2 * pp:WIN * 2 * pp + WIN], o[WIN * (2 * pp + 1):WIN * (2 * pp + 1) + WIN])


def _attn_fwd(q, k, v, sink, shards):
    E = q.shape[0]
    S = E - CTX
    nb, kspecs = _attn_specs(E)
    na = len(shards)

    def body(q_ref, kc, kp, kk, kn, vc, vp, vk, vn, sink_ref, bias_ref, *rest):
        shard_refs, (o_ref, lse_ref), got_refs = rest[:na], rest[na:na + 2], rest[na + 2:2 * na + 2]
        n = pl.program_id(0)
        cps = _xchg_copies(shard_refs, got_refs, *rest[2 * na + 2:], gather=True)

        @pl.when(n == 0)
        def _():
            _xchg_start(cps)

        lane = lax.broadcasted_iota(jnp.int32, (WIN, 128), 1)
        lse_t = jnp.zeros((WIN, 128), F32)
        kall = jnp.concatenate([kc[...], kp[...], kk[...], kn[...]], axis=0)
        vall = jnp.concatenate([vc[...], vp[...], vk[...], vn[...]], axis=0)
        K = [_both_halves(kall, h) for h in range(N_KV)]
        Q = [_stack_heads(q_ref, h).astype(_BF) for h in range(N_KV)]
        sk = [jnp.concatenate([jnp.broadcast_to(sink_ref[GRP * h + g:GRP * h + g + 1, 0:1], (WIN, 1))
                               for g in range(GRP)], axis=0) for h in range(N_KV)]
        s = [_nt(Q[h], K[h]) + bias_ref[0] for h in range(N_KV)]
        m = [jnp.maximum(jnp.max(s[h], axis=1, keepdims=True), sk[h]) for h in range(N_KV)]
        e = [jnp.exp(s[h] - m[h]) for h in range(N_KV)]
        den = [jnp.sum(e[h], axis=1, keepdims=True) + jnp.exp(sk[h] - m[h]) for h in range(N_KV)]
        V = [_both_halves(vall, h) for h in range(N_KV)]
        o = [_nn((e[h] * (1.0 / den[h])).astype(_BF), V[h]) for h in range(N_KV)]
        for h in range(N_KV):
            lse = m[h] + jnp.log(den[h])
            for g in range(GRP):
                lse_t = jnp.where(lane == GRP * h + g, lse[WIN * g:WIN * g + WIN], lse_t)
            for pp in range(GRP // 2):
                t = 2 * h + pp
                o_ref[:, 128 * t:128 * t + 128] = _unstack_pair(o[h], pp).astype(_BF)
        lse_ref[...] = lse_t

        @pl.when(n == nb - 1)
        def _():
            _xchg_finish(cps)

    qs = pl.BlockSpec((WIN, QP), lambda n: (n + 2, 0))
    anyspec = pl.BlockSpec(memory_space=pl.ANY)
    bias, bias_spec = _attn_bias(nb)
    return pl.pallas_call(
        body, name="attn_fwd", grid=(nb,),
        in_specs=[qs] + kspecs + kspecs + [_full((8, 128)), bias_spec] + [anyspec] * na,
        out_specs=[_rows(WIN, 512), _rows(WIN, 128)] + [anyspec] * na,
        out_shape=[jax.ShapeDtypeStruct((S, 512), _BF), jax.ShapeDtypeStruct((S, 128), F32)]
        + _xchg_out_shapes(shards, True),
        scratch_shapes=_xchg_scratch(na),
        compiler_params=_cp(("arbitrary",)),
    )(q, k, k, k, k, v, v, v, v, sink, bias, *shards)


def _attn_bwd(q, k, v, sink, lse, d_attn, slabs):
    E = q.shape[0]
    S = E - CTX
    nb, kspecs = _attn_specs(E)
    last = E // WIN - 1
    na = len(slabs)

    def body(q_ref, kc, kp, kk, kn, vc, vp, vk, vn, sink_ref, bias_ref, lse_ref, do_ref, *rest):
        slab_refs, (dq_ref, dk_ref, dv_ref, ds_ref), got_refs = rest[:na], rest[na:na + 4], rest[na + 4:2 * na + 4]
        n = pl.program_id(0)
        cps = _xchg_copies(slab_refs, got_refs, *rest[2 * na + 4:], gather=False)

        @pl.when(n == 0)
        def _():
            _xchg_start(cps)
            dk_ref[...] = jnp.zeros_like(dk_ref)
            dv_ref[...] = jnp.zeros_like(dv_ref)
            ds_ref[...] = jnp.zeros_like(ds_ref)

        lane = lax.broadcasted_iota(jnp.int32, (WIN, 128), 1)
        lse_t = lse_ref[...]
        starts = [None, pl.multiple_of((n + 1) * WIN, WIN), pl.multiple_of((n + 2) * WIN, WIN),
                  pl.multiple_of(jnp.minimum(n + 3, last) * WIN, WIN)]
        kall = jnp.concatenate([kc[...], kp[...], kk[...], kn[...]], axis=0)
        vall = jnp.concatenate([vc[...], vp[...], vk[...], vn[...]], axis=0)
        for h in range(N_KV):
            hs = slice(HD * h, HD * h + HD)
            K = _both_halves(kall, h)
            V = _both_halves(vall, h)
            Q = _stack_heads(q_ref, h).astype(_BF)
            sk = jnp.concatenate([jnp.broadcast_to(sink_ref[GRP * h + g:GRP * h + g + 1, 0:1], (WIN, 1))
                                  for g in range(GRP)], axis=0)
            ls = jnp.concatenate([jnp.sum(jnp.where(lane == GRP * h + g, lse_t, 0.0), axis=1, keepdims=True)
                                  for g in range(GRP)], axis=0)
            do = _stack_heads(do_ref, h).astype(_BF)
            p = jnp.exp(_nt(Q, K) + bias_ref[0] - ls)
            dp = _nt(do, V)
            delta = jnp.sum(p * dp, axis=1, keepdims=True)
            dsc = (p * (dp - delta)).astype(_BF)
            dq = _nn(dsc, K) * (HD ** -0.5)
            for pp in range(GRP // 2):
                t = 2 * h + pp
                dq_ref[:, 128 * t:128 * t + 128] = _unstack_pair(dq, pp).astype(_BF)
            dK2 = _tn(Q, dsc)
            dV2 = _tn(do, p.astype(_BF))
            dK = dK2[0:HD] + dK2[HD:2 * HD]
            dV = dV2[0:HD] + dV2[HD:2 * HD]
            dk_ref[hs, 0:CTX] += dK[:, 0:CTX]
            dv_ref[hs, 0:CTX] += dV[:, 0:CTX]
            for w in range(1, 4):
                lo = CTX + WIN * (w - 1)
                dk_ref[hs, pl.ds(starts[w], WIN)] += dK[:, lo:lo + WIN]
                dv_ref[hs, pl.ds(starts[w], WIN)] += dV[:, lo:lo + WIN]
            psk = -jnp.exp(sk - ls) * delta
            for g in range(GRP):
                j = GRP * h + g
                tot = jnp.sum(psk[WIN * g:WIN * g + WIN], axis=0, keepdims=True)
                ds_ref[j:j + 1, :] += jnp.broadcast_to(tot, (1, 128))

        @pl.when(n == nb - 1)
        def _():
            _xchg_finish(cps)

    qs = pl.BlockSpec((WIN, QP), lambda n: (n + 2, 0))
    anyspec = pl.BlockSpec(memory_space=pl.ANY)
    bias, bias_spec = _attn_bias(nb)
    return pl.pallas_call(
        body, name="attn_bwd", grid=(nb,),
        in_specs=[qs] + kspecs + kspecs + [_full((8, 128)), bias_spec, _rows(WIN, 128), _rows(WIN, 512)]
        + [anyspec] * na,
        out_specs=[_rows(WIN, QP), _full((KP, E)), _full((KP, E)), _full((8, 128))] + [anyspec] * na,
        out_shape=[jax.ShapeDtypeStruct((S, QP), _BF), jax.ShapeDtypeStruct((KP, E), F32),
                   jax.ShapeDtypeStruct((KP, E), F32), jax.ShapeDtypeStruct((8, 128), F32)]
        + _xchg_out_shapes(slabs, False),
        scratch_shapes=_xchg_scratch(na),
        compiler_params=_cp(("arbitrary",), 48 * 1024 * 1024),
    )(q, k, k, k, k, v, v, v, v, sink, bias, lse, d_attn, *slabs)


def _gla_order(E, reverse, backward):
    nc = CTX // GLA_T
    n = E // GLA_T
    if not reverse:
        fwd = lambda s: s
    else:
        fwd = lambda s: jnp.where(s < nc, nc - 1 - s, n - 1 + nc - s)
    if backward:
        return lambda s: fwd(n - 1 - s)
    return fwd


def _gla_masks():
    T = GLA_T
    l128 = lax.broadcasted_iota(jnp.int32, (1, 128), 1)
    qmask = [((l128 >> 5) == j).astype(F32) for j in range(4)]
    vmask = [((l128 >> 6) == j).astype(F32) for j in range(2)]
    bd = ((lax.broadcasted_iota(jnp.int32, (512, 256), 0) >> 6)
          == (lax.broadcasted_iota(jnp.int32, (512, 256), 1) >> 5)).astype(F32)
    ri = lax.broadcasted_iota(jnp.int32, (T, 2 * T), 0)
    ci = lax.broadcasted_iota(jnp.int32, (T, 2 * T), 1) & (T - 1)
    return qmask, vmask, bd, ri, ci


def _tri_sum(tri, x):
    hi = x.astype(_BF)
    lo = (x - hi.astype(F32)).astype(_BF)
    n = x.shape[1]
    y = _nn(tri.astype(_BF), jnp.concatenate([hi, lo], axis=1))
    return y[:, 0:n] + y[:, n:2 * n]


def _gla_decays(la, reverse, ri, ci):
    T = GLA_T
    msk2 = (ri <= ci) if reverse else (ri >= ci)
    mskT2 = (ri >= ci) if reverse else (ri <= ci)
    b = _tri_sum(msk2[:, 0:T], la)
    bT = b[0:1] if reverse else b[T - 1:T]
    bm = b[T // 2:T // 2 + 1]
    return msk2, mskT2, b, bT, bm


def _pair_stack(tile, m0, m1):
    return jnp.concatenate([(tile * m0).astype(_BF), (tile * m1).astype(_BF)], axis=0)


def _gla_fwd(gq, gk, gv, la, reverse, shards=()):
    E = gq.shape[0]
    T = GLA_T
    n = E // T
    order = _gla_order(E, reverse, False)
    col = 1 if reverse else 0

    def body(gq_ref, gk_ref, gv_ref, la_ref, o_ref, st_ref, S_scr):
        @pl.when(pl.program_id(0) == 0)
        def _():
            S_scr[...] = jnp.zeros_like(S_scr)

        qmask, vmask, bd, ri, ci = _gla_masks()
        msk2, _, b, bT, bm = _gla_decays(la_ref[...], reverse, ri, ci)
        q, k, v = gq_ref[...], gk_ref[...], gv_ref[...]
        qd = (q * jnp.exp(b)).astype(_BF)
        qm = (q * jnp.exp(b - bm)).astype(_BF)
        km = k * jnp.exp(bm - b)
        kd = (k * jnp.exp(bT - b)).astype(_BF)
        ST = S_scr[...]
        comp = ST[0:DV]
        for h in range(1, N_GLA):
            comp = comp + ST[DV * h:DV * h + DV]
        st_ref[0] = comp
        inter = _nt(qd, ST.astype(_BF))
        tiles = []
        for p in range(N_GLA // 2):
            qs = slice(128 * (p // 2), 128 * (p // 2) + 128)
            vs = slice(128 * p, 128 * p + 128)
            j0 = (2 * p) % 4
            KS = _pair_stack(km[:, qs], qmask[j0], qmask[j0 + 1])
            VS = _pair_stack(v[:, vs], vmask[0], vmask[1])
            AA = jnp.where(msk2, _nt(qm[:, qs], KS), 0.0).astype(_BF)
            tiles.append(_nn(AA, VS))
        o_ref[...] = inter + jnp.concatenate(tiles, axis=1)
        S_scr[...] = ST * jnp.exp(bT) + bd * _tn(v.astype(_BF), kd)

    blk = lambda w, c=0: pl.BlockSpec((T, w), lambda s: (order(s), c))
    return _hosted_call(
        body, (gq, gk, gv, la), shards, True,
        name="gla_fwd_rev" if reverse else "gla_fwd", grid=(n,),
        in_specs=[blk(256), blk(256), blk(512), blk(256, col)],
        out_specs=[blk(512), pl.BlockSpec((1, DV, 256), lambda s: (order(s), 0, 0))],
        out_shape=[jax.ShapeDtypeStruct((E, 512), F32), jax.ShapeDtypeStruct((n, DV, 256), F32)],
        scratch_shapes=[pltpu.VMEM((512, 256), F32)],
        compiler_params=_cp(("arbitrary",)))


def _gla_bwd(gq, gk, gv, la, z, st, do, reverse, prev=None, slabs=()):
    E = gq.shape[0]
    T = GLA_T
    n = E // T
    nc = CTX // T
    order = _gla_order(E, reverse, True)
    col = 1 if reverse else 0
    np_ = 0 if prev is None else 3

    def body(gq_ref, gk_ref, gv_ref, la_ref, z_ref, st_ref, do_ref, *rest):
        prev_refs = rest[:np_]
        dq_ref, dk_ref, dv_ref, dlg_ref, gwg_ref, bsum_ref, dS_scr = rest[np_:]

        @pl.when(pl.program_id(0) == 0)
        def _():
            dS_scr[...] = jnp.zeros_like(dS_scr)
            gwg_ref[...] = jnp.zeros_like(gwg_ref)
            bsum_ref[...] = jnp.zeros_like(bsum_ref)

        is_lat = order(pl.program_id(0)) >= nc
        qmask, vmask, bd, ri, ci = _gla_masks()
        msk2, mskT2, b, bT, bm = _gla_decays(la_ref[...], reverse, ri, ci)
        q, k, v = gq_ref[...], gk_ref[...], gv_ref[...]
        do = jnp.where(is_lat, do_ref[...].astype(F32), 0.0)
        e_b, e_qm, e_km, e_kd, e_T = jnp.exp(b), jnp.exp(b - bm), jnp.exp(bm - b), jnp.exp(bT - b), jnp.exp(bT)
        qd, qm, km, kd = q * e_b, q * e_qm, k * e_km, k * e_kd
        qdb, qmb, kmb, kdb, vb, dob = (t.astype(_BF) for t in (qd, qm, km, kd, v, do))
        ST = jnp.tile(st_ref[0], (N_GLA, 1)) * bd
        dST = dS_scr[...]
        dSTb = dST.astype(_BF)
        dqd = _nn(dob, ST.astype(_BF))
        dkd = _nn(vb, dSTb)
        dv_t, dqm_t, dkm_t = [], [None, None], [None, None]
        for p in range(N_GLA // 2):
            t = p // 2
            qs = slice(128 * t, 128 * t + 128)
            vs = slice(128 * p, 128 * p + 128)
            j0 = (2 * p) % 4
            QS = _pair_stack(qm[:, qs], qmask[j0], qmask[j0 + 1])
            KS = _pair_stack(km[:, qs], qmask[j0], qmask[j0 + 1])
            VS = _pair_stack(v[:, vs], vmask[0], vmask[1])
            DS = _pair_stack(do[:, vs], vmask[0], vmask[1])
            ATT = jnp.where(mskT2, _nt(kmb[:, qs], QS), 0.0).astype(_BF)
            dAA = jnp.where(msk2, _nt(dob[:, vs], VS), 0.0).astype(_BF)
            dATT = jnp.where(mskT2, _nt(vb[:, vs], DS), 0.0).astype(_BF)
            dv_t.append(_nn(ATT, DS))
            dq_p = _nn(dAA, KS)
            dk_p = _nn(dATT, QS)
            dqm_t[t] = dq_p if dqm_t[t] is None else dqm_t[t] + dq_p
            dkm_t[t] = dk_p if dkm_t[t] is None else dkm_t[t] + dk_p
        dqm = jnp.concatenate(dqm_t, axis=1)
        dkm = jnp.concatenate(dkm_t, axis=1)
        dq = dqm * e_qm + dqd * e_b
        dk = dkm * e_km + dkd * e_kd
        dv = _nt(kdb, dSTb) + jnp.concatenate(dv_t, axis=1)
        if prev is None:
            dq_ref[...], dk_ref[...], dv_ref[...] = dq, dk, dv
        else:
            dq_ref[...] = ((dq + prev_refs[0][...]) * (DK ** -0.5)).astype(_BF)
            dk_ref[...] = (dk + prev_refs[1][...]).astype(_BF)
            dv_ref[...] = (dv + prev_refs[2][...]).astype(_BF)
        db = dqm * qm - dkm * km + dqd * qd - dkd * kd
        dbT = jnp.sum(dkd * kd, axis=0, keepdims=True) + e_T * jnp.sum(dST * ST, axis=0, keepdims=True)
        dla = _tri_sum(mskT2[:, 0:T], db) + dbT
        dlg = dla * (1.0 - jnp.exp(GATE_TAU * la_ref[...])) * (1.0 / GATE_TAU)
        bsum_ref[0:1, :] += jnp.sum(dlg, axis=0, keepdims=True)
        dlgb = dlg.astype(_BF)
        dlg_ref[...] = dlgb
        gwg_ref[...] += _tn(z_ref[...], dlgb)
        dS_scr[...] = dST * e_T + bd * _tn(dob, qdb)

    blk = lambda w, c=0: pl.BlockSpec((T, w), lambda s: (order(s), c))
    do_spec = pl.BlockSpec((T, 512), lambda s: (jnp.maximum(order(s) - nc, 0), 0))
    odt = F32 if prev is None else _BF
    return _hosted_call(
        body, (gq, gk, gv, la, z, st, do) + (() if prev is None else tuple(prev)), slabs, False,
        name="gla_bwd_rev" if reverse else "gla_bwd", grid=(n,),
        in_specs=[blk(256), blk(256), blk(512), blk(256, col), blk(128),
                  pl.BlockSpec((1, DV, 256), lambda s: (order(s), 0, 0)), do_spec]
        + ([] if prev is None else [blk(256), blk(256), blk(512)]),
        out_specs=[blk(256), blk(256), blk(512), blk(256), _full((128, 256)), _full((8, 256))],
        out_shape=[jax.ShapeDtypeStruct((E, 256), odt), jax.ShapeDtypeStruct((E, 256), odt),
                   jax.ShapeDtypeStruct((E, 512), odt), jax.ShapeDtypeStruct((E, 256), _BF),
                   jax.ShapeDtypeStruct((128, 256), F32), jax.ShapeDtypeStruct((8, 256), F32)],
        scratch_shapes=[pltpu.VMEM((512, 256), F32)],
        compiler_params=_cp(("arbitrary",)))


def _gla_out(o_f, o_b, gg, ggla, mavg):
    o = o_f + o_b
    rr = lax.rsqrt(_head_mean(o * o, mavg) + EPS)
    oh = o * rr
    sg = _sigmoid(gg)
    return oh, rr, sg


def _mix_fwd(x, attn, o_f, o_b, gg, ggla, mavg, wout, gt1, g2):
    S = x.shape[0]
    TM = 256

    def body(x_ref, a_ref, of_ref, ob_ref, gg_ref, ggla_ref, mavg_ref, w_ref, gt1_ref, g2_ref, x1_ref, mix_ref):
        gg_t = gg_ref[...]
        oh, _, sg = _gla_out(of_ref[...], ob_ref[...], gg_t, ggla_ref[...], mavg_ref[...])
        mix_ref[:, 0:512] = a_ref[...]
        mix_ref[:, 512:1024] = (oh * ggla_ref[...] * (gg_t * sg)).astype(_BF)
        y = _nn(mix_ref[...], w_ref[...])
        ry = lax.rsqrt(jnp.mean(y * y, axis=-1, keepdims=True) + EPS)
        x1_ref[...] = x_ref[...] + gt1_ref[...] * ((y * ry) * g2_ref[...])

    return pl.pallas_call(
        body, name="mix_fwd", grid=(S // TM,),
        in_specs=[_rows(TM, D), _rows(TM, 512), _rows(TM, 512, 1), _rows(TM, 512, 1), _rows(TM, 512, 1),
                  _full((1, 512)), _full((512, 512)), _full((D, D)), _full((1, D)), _full((1, D))],
        out_specs=[_rows(TM, D), _rows(TM, D)],
        out_shape=[jax.ShapeDtypeStruct((S, D), F32), jax.ShapeDtypeStruct((S, D), _BF)],
        compiler_params=_cp(("arbitrary",), 40 * 1024 * 1024),
    )(x, attn, o_f, o_b, gg, ggla, mavg, wout, gt1, g2)


def _mix_bwd(dx1, mix, o_f, o_b, gg, ggla, mavg, wout, gt1, g2):
    S = dx1.shape[0]
    TM = 256

    def body(dx_ref, mix_ref, of_ref, ob_ref, gg_ref, ggla_ref, mavg_ref, w_ref, gt1_ref, g2_ref,
             da_ref, do_ref, dgg_ref, dy_ref, sums_ref):
        @pl.when(pl.program_id(0) == 0)
        def _():
            sums_ref[...] = jnp.zeros_like(sums_ref)

        dx = dx_ref[...]
        y = _nn(mix_ref[...], w_ref[...])
        ry = lax.rsqrt(jnp.mean(y * y, axis=-1, keepdims=True) + EPS)
        yh = y * ry
        sums_ref[0:1, :] += jnp.sum(dx * yh, axis=0, keepdims=True)
        dyh = dx * (gt1_ref[...] * g2_ref[...])
        dy = (ry * (dyh - yh * jnp.mean(dyh * yh, axis=-1, keepdims=True))).astype(_BF)
        dy_ref[...] = dy
        dmix = _nt(dy, w_ref[...])
        da_ref[...] = dmix[:, 0:512].astype(_BF)
        dgla = dmix[:, 512:1024]
        gg_t = gg_ref[...]
        ggla_t = ggla_ref[...]
        oh, rr, sg = _gla_out(of_ref[...], ob_ref[...], gg_t, ggla_t, mavg_ref[...])
        dgg_ref[...] = (dgla * oh * ggla_t * (sg * (1.0 + gg_t * (1.0 - sg)))).astype(_BF)
        don = dgla * (gg_t * sg)
        sums_ref[1:2, 0:512] += jnp.sum(don * oh, axis=0, keepdims=True)
        doh = don * ggla_t
        do_ref[...] = (rr * (doh - oh * _head_mean(doh * oh, mavg_ref[...]))).astype(_BF)

    return pl.pallas_call(
        body, name="mix_bwd", grid=(S // TM,),
        in_specs=[_rows(TM, D), _rows(TM, D), _rows(TM, 512, 1), _rows(TM, 512, 1), _rows(TM, 512, 1),
                  _full((1, 512)), _full((512, 512)), _full((D, D)), _full((1, D)), _full((1, D))],
        out_specs=[_rows(TM, 512), _rows(TM, 512), _rows(TM, 512), _rows(TM, D), _full((8, D))],
        out_shape=[jax.ShapeDtypeStruct((S, 512), _BF), jax.ShapeDtypeStruct((S, 512), _BF),
                   jax.ShapeDtypeStruct((S, 512), _BF), jax.ShapeDtypeStruct((S, D), _BF),
                   jax.ShapeDtypeStruct((8, D), F32)],
        compiler_params=_cp(("arbitrary",), 40 * 1024 * 1024),
    )(dx1, mix, o_f, o_b, gg, ggla, mavg, wout, gt1, g2)


def _ffn(x1, target, gm2, sh2, gt2, g4, wffi, wffo):
    S = x1.shape[0]
    TF = 256

    def body(x_ref, t_ref, gm_ref, sh_ref, gt_ref, g4_ref, wi_hbm, wo_hbm,
             dx_ref, h_ref, du_ref, act_ref, df_ref, sums_ref, loss_ref, wi, wo, sem):
        @pl.when(pl.program_id(0) == 0)
        def _():
            c1 = pltpu.make_async_copy(wi_hbm, wi, sem.at[0])
            c2 = pltpu.make_async_copy(wo_hbm, wo, sem.at[1])
            c1.start()
            c2.start()
            sums_ref[...] = jnp.zeros_like(sums_ref)
            loss_ref[...] = jnp.zeros_like(loss_ref)
            c1.wait()
            c2.wait()

        x = x_ref[...]
        gm = gm_ref[...]
        r = lax.rsqrt(jnp.mean(x * x, axis=-1, keepdims=True) + EPS)
        xh = x * r
        hb = (xh * gm + sh_ref[...]).astype(_BF)
        h_ref[...] = hb
        u = _nt(hb, wi[...])
        g = u[:, 0:FFN]
        up = u[:, FFN:2 * FFN]
        sg = _sigmoid(g)
        sl = g * sg
        ab = (sl * up).astype(_BF)
        act_ref[...] = ab
        f = _nn(ab, wo[...])
        rf = lax.rsqrt(jnp.mean(f * f, axis=-1, keepdims=True) + EPS)
        fh = f * rf
        gt, g4v = gt_ref[...], g4_ref[...]
        err = x + gt * (fh * g4v) - t_ref[...]
        loss_ref[...] += jnp.sum(err * err) * (0.5 / D)
        dout = err * (1.0 / D)
        sums_ref[2:3, :] += jnp.sum(dout * fh, axis=0, keepdims=True)
        dfh = dout * (gt * g4v)
        dfb = (rf * (dfh - fh * jnp.mean(dfh * fh, axis=-1, keepdims=True))).astype(_BF)
        df_ref[...] = dfb
        dact = _nt(dfb, wo[...])
        du_ref[:, 0:FFN] = (dact * up * (sg * (1.0 + g * (1.0 - sg)))).astype(_BF)
        du_ref[:, FFN:2 * FFN] = (dact * sl).astype(_BF)
        dh = _nn(du_ref[...], wi[...])
        sums_ref[0:1, :] += jnp.sum(dh, axis=0, keepdims=True)
        sums_ref[1:2, :] += jnp.sum(dh * xh, axis=0, keepdims=True)
        dxh = dh * gm
        dx_ref[...] = dout + r * (dxh - xh * jnp.mean(dxh * xh, axis=-1, keepdims=True))

    vec = _full((1, D))
    anyspec = pl.BlockSpec(memory_space=pl.ANY)
    return pl.pallas_call(
        body, name="ffn_fwd_bwd", grid=(S // TF,),
        in_specs=[_rows(TF, D), _rows(TF, D), vec, vec, vec, vec, anyspec, anyspec],
        out_specs=[_rows(TF, D), _rows(TF, D), _rows(TF, 2 * FFN), _rows(TF, FFN), _rows(TF, D),
                   _full((8, D)), _full((8, 128))],
        out_shape=[jax.ShapeDtypeStruct((S, D), F32), jax.ShapeDtypeStruct((S, D), _BF),
                   jax.ShapeDtypeStruct((S, 2 * FFN), _BF), jax.ShapeDtypeStruct((S, FFN), _BF),
                   jax.ShapeDtypeStruct((S, D), _BF), jax.ShapeDtypeStruct((8, D), F32),
                   jax.ShapeDtypeStruct((8, 128), F32)],
        scratch_shapes=[pltpu.VMEM((2 * FFN, D), _BF), pltpu.VMEM((FFN, D), _BF), pltpu.SemaphoreType.DMA((2,))],
        compiler_params=_cp(("arbitrary",), VMEM_BIG),
    )(x1, target, gm2, sh2, gt2, g4, wffi, wffo)


def _inproj_bwd(x, ctx, gml, gmc, win, wg, cos, sa, sb, dq, dk, dv, dgq, dgk, dgv, dgg, dlg_f, dlg_b, dx1):
    S = x.shape[0]
    E = S + CTX
    TE = CTX

    def body(x_ref, c_ref, gml_ref, gmc_ref, w_ref, wg_ref, cos_ref, sa_ref, sb_ref, dq_ref, dk_ref, dv_ref,
             gq_ref, gk_ref, gv_ref, dgg_ref, dlf, dlb, dx1_ref, dp_ref, gx_ref, sums_ref):
        i = pl.program_id(0)
        is_ctx = i == 0

        @pl.when(is_ctx)
        def _():
            sums_ref[...] = jnp.zeros_like(sums_ref)

        lat = jnp.where(is_ctx, 0.0, 1.0)
        cos_t, sa_t, sb_t = cos_ref[...], sa_ref[...], sb_ref[...]
        dp_ref[:, O_Q:O_K] = (_unrope(dq_ref[...].astype(F32), cos_t, sa_t, sb_t) * lat).astype(_BF)
        dp_ref[:, O_K:O_V] = _unrope(dk_ref[...].T, cos_t, sa_t, sb_t).astype(_BF)
        dp_ref[:, O_V:O_GQ] = dv_ref[...].T.astype(_BF)
        dp_ref[:, O_GQ:O_GK] = gq_ref[...]
        dp_ref[:, O_GK:O_GV] = gk_ref[...]
        dp_ref[:, O_GV:O_GG] = gv_ref[...]
        dp_ref[:, O_GG:O_Z] = (dgg_ref[...].astype(F32) * lat).astype(_BF)
        dlg = jnp.concatenate([dlf[...], dlb[...]], axis=1)
        dp_ref[:, O_Z:NP] = _nt(dlg, wg_ref[...]).astype(_BF)
        dh = _nn(dp_ref[...], w_ref[...])
        x = jnp.where(is_ctx, c_ref[...], x_ref[...])
        r = lax.rsqrt(jnp.mean(x * x, axis=-1, keepdims=True) + EPS)
        xh = x * r
        sdh = jnp.sum(dh, axis=0, keepdims=True)
        sdx = jnp.sum(dh * xh, axis=0, keepdims=True)
        sums_ref[0:1, :] += sdh * lat
        sums_ref[1:2, :] += sdx * lat
        sums_ref[2:3, :] += sdh * (1.0 - lat)
        sums_ref[3:4, :] += sdx * (1.0 - lat)
        dxh = dh * jnp.where(is_ctx, gmc_ref[...], gml_ref[...])
        gx_ref[...] = dx1_ref[...] + r * (dxh - xh * jnp.mean(dxh * xh, axis=-1, keepdims=True))

    vec = _full((1, D))
    tab = _rows(TE, 128)
    return pl.pallas_call(
        body, name="inproj_bwd", grid=(E // TE,),
        in_specs=[_rows_lat(TE, D), _full((CTX, D)), vec, vec, _full((NP, D)), _full((128, 512)), tab, tab, tab,
                  _rows_lat(TE, QP), pl.BlockSpec((KP, TE), lambda i: (0, i)), pl.BlockSpec((KP, TE), lambda i: (0, i)),
                  _rows(TE, 256), _rows(TE, 256), _rows(TE, 512), _rows_lat(TE, 512), _rows(TE, 256), _rows(TE, 256),
                  _rows_lat(TE, D)],
        out_specs=[_rows(TE, NP), _rows_lat(TE, D), _full((8, D))],
        out_shape=[jax.ShapeDtypeStruct((E, NP), _BF), jax.ShapeDtypeStruct((S, D), F32),
                   jax.ShapeDtypeStruct((8, D), F32)],
        compiler_params=_cp(("arbitrary",), VMEM_BIG),
    )(x, ctx, gml, gmc, win, wg, cos, sa, sb, dq, dk, dv, dgq, dgk, dgv, dgg, dlg_f, dlg_b, dx1)


def _matmul_tn(a, b, tk, tt, name, out_dtype, transpose_out=False, a_cols=None, hosted=(), gather=True):
    T, KA = a.shape
    N = b.shape[1]
    nt = T // tt
    k0 = 0
    if a_cols is not None:
        KA, k0 = tk, a_cols

    def body(a_ref, b_ref, o_ref, acc):
        t = pl.program_id(1)

        @pl.when(t == 0)
        def _():
            acc[...] = jnp.zeros_like(acc)

        acc[...] += _tn(a_ref[...], b_ref[...])

        @pl.when(t == nt - 1)
        def _():
            o_ref[...] = (acc[...].T if transpose_out else acc[...]).astype(out_dtype)

    if transpose_out:
        out_spec, out_shape = pl.BlockSpec((N, tk), lambda i, t: (0, i)), (N, KA)
    else:
        out_spec, out_shape = pl.BlockSpec((tk, N), lambda i, t: (i, 0)), (KA, N)
    res = _hosted_call(
        body, (a, b), hosted, gather, name=name, grid=(KA // tk, nt),
        in_specs=[pl.BlockSpec((tt, tk), lambda i, t: (t, i + k0)), pl.BlockSpec((tt, N), lambda i, t: (t, 0))],
        out_specs=[out_spec], out_shape=[jax.ShapeDtypeStruct(out_shape, out_dtype)],
        scratch_shapes=[pltpu.VMEM((tk, N), F32)],
        compiler_params=_cp(("arbitrary", "arbitrary"), VMEM_BIG))
    return res if hosted else res[0]


def _ada_bwd(c_all, c_ctx, w_ada, d_all):
    n = w_ada.shape[1]

    def body(c_ref, cc_ref, w_ref, d_ref, gw_ref, t_ref):
        c = jnp.concatenate([c_ref[...], jnp.broadcast_to(cc_ref[...], (8, D))], axis=0)
        db = d_ref[...].astype(_BF)
        gw_ref[0] = _tn((c * _sigmoid(c)).astype(_BF), db)
        t_ref[...] = _nt(db[8:16], w_ref[...].astype(_BF))

    return pl.pallas_call(
        body, name="ada_bwd", in_specs=[_full((8, D)), _full((1, D)), _full((D, n)), _full((16, n))],
        out_specs=[_full((1, D, n)), _full((8, D))],
        out_shape=[jax.ShapeDtypeStruct((1, D, n), F32), jax.ShapeDtypeStruct((8, D), F32)], grid=(1,),
        compiler_params=_cp(("arbitrary",)),
    )(c_all, c_ctx, w_ada, d_all)


PART_ROWS = 56
R_ADA, R_ADA_C, R_GAIN, R_SINK, R_BG, R_GGLA, R_LOSS, R_WG = 0, 6, 12, 16, 17, 18, 19, 24


def _small_grads(s_in, s_ffn, s_mix, ada_l, ada_c, gains, dsink, s_bg, g_wg, loss):
    def body(si, sf, sm, al, ac, g, ds, sbg, gwg, loss_ref, o_ref):
        o_ref[...] = jnp.zeros_like(o_ref)
        o_ref[R_LOSS:R_LOSS + 1, 0:128] = loss_ref[0:1, :]
        sub = lax.broadcasted_iota(jnp.int32, (8, 128), 0)
        lane = lax.broadcasted_iota(jnp.int32, (8, 128), 1)
        o_ref[R_SINK:R_SINK + 1, 0:128] = jnp.sum(jnp.where(sub == lane, ds[...], 0.0), axis=0, keepdims=True)
        o_ref[R_BG:R_BG + 1, 0:512] = sbg[0:1, :]
        y = sm[1:2, 0:128] + sm[1:2, 128:256] + sm[1:2, 256:384] + sm[1:2, 384:512]
        y = y + pltpu.roll(y, 64, 1)
        o_ref[R_GGLA:R_GGLA + 1, 0:128] = jnp.where(lane[0:1] < DV, y, 0.0)
        o_ref[R_WG:R_WG + 16, 0:256] = gwg[0:16, 0:256]
        o_ref[R_WG + 16:R_WG + 32, 0:256] = gwg[16:32, 256:512]
        sdh_l, sdx_l, sdh_c, sdx_c = si[0:1], si[1:2], si[2:3], si[3:4]
        sdh2, sdx2, a2 = sf[0:1], sf[1:2], sf[2:3]
        a1 = sm[0:1]
        g1, g2, g3, g4 = g[0:1], g[1:2], g[2:3], g[3:4]
        sc1, gt1, sc2, gt2 = al[1:2], al[2:3], al[4:5], al[5:6]
        sc1c = ac[1:2]
        z = jnp.zeros((1, D), F32)
        rows = [sdh_l, sdx_l * g1, a1 * g2, sdh2, sdx2 * g3, a2 * g4,
                sdh_c, sdx_c * g1, z, z, z, z,
                sdx_l * (1.0 + sc1) + sdx_c * (1.0 + sc1c), a1 * gt1, sdx2 * (1.0 + sc2), a2 * gt2]
        for r, v in enumerate(rows):
            o_ref[r:r + 1, :] = v

    v8 = _full((8, D))
    return pl.pallas_call(
        body, name="small_grads",
        in_specs=[v8] * 6 + [_full((8, 128)), _full((8, 512)), _full((128, 512)), _full((8, 128))],
        out_specs=_full((PART_ROWS, D)), out_shape=jax.ShapeDtypeStruct((PART_ROWS, D), F32), grid=(1,),
        compiler_params=_cp(("arbitrary",)),
    )(s_in, s_ffn, s_mix, ada_l, ada_c, gains, dsink, s_bg, g_wg, loss)


def _row_tile(R):
    for cand in (256, 128, 64, 32, 16):
        if R % cand == 0 and R > cand:
            return cand
    return R


def _adamw(w, gs, m, v, name):
    _, R, C = w.shape
    tr = _row_tile(R)
    c1 = 1.0 / (1.0 - ADAM_B1 ** ADAM_STEP)
    c2 = 1.0 / (1.0 - ADAM_B2 ** ADAM_STEP)
    ng = len(gs)

    def body(w_ref, *refs):
        g_refs, (m_ref, v_ref, go_ref, d_ref, nm_ref, nv_ref) = refs[:ng], refs[ng:]
        c0 = 0
        for g_ref in g_refs:
            cols = slice(c0, c0 + g_ref.shape[2])
            c0 += g_ref.shape[2]
            gg = g_ref[0].astype(F32)
            for j in range(1, g_ref.shape[0]):
                gg = gg + g_ref[j].astype(F32)
            go_ref[0, :, cols] = gg
            nm = ADAM_B1 * m_ref[0, :, cols] + (1.0 - ADAM_B1) * gg
            nv = ADAM_B2 * v_ref[0, :, cols] + (1.0 - ADAM_B2) * (gg * gg)
            nm_ref[0, :, cols] = nm
            nv_ref[0, :, cols] = nv
            d_ref[0, :, cols] = -ADAM_LR * ((nm * c1) / (jnp.sqrt(nv * c2) + ADAM_EPS) + ADAM_WD * w_ref[0, :, cols])

    spec = pl.BlockSpec((1, tr, C), lambda i: (0, i, 0))
    sds = jax.ShapeDtypeStruct((1, R, C), F32)
    g_specs = [pl.BlockSpec((g.shape[0], tr, g.shape[2]), lambda i: (0, i, 0)) for g in gs]
    return pl.pallas_call(
        body, name=name, grid=(R // tr,), in_specs=[spec] + g_specs + [spec, spec], out_specs=[spec] * 4,
        out_shape=[sds] * 4, compiler_params=_cp(("parallel",), 48 * 1024 * 1024),
    )(w, *gs, m, v)


def _sum_slots(slots, name):
    _, R, C = slots.shape
    tr = _row_tile(R)

    def body(s_ref, o_ref):
        acc = s_ref[0].astype(F32)
        for j in range(1, N_DEV):
            acc = acc + s_ref[j].astype(F32)
        o_ref[...] = acc

    return pl.pallas_call(
        body, name=name, grid=(R // tr,), in_specs=[pl.BlockSpec((N_DEV, tr, C), lambda i: (0, i, 0))],
        out_specs=_rows(tr, C), out_shape=jax.ShapeDtypeStruct((R, C), F32), compiler_params=_cp(("parallel",)),
    )(slots)


def _ag2_start(x_ref, out_ref, send_sems, recv_sems, local_sem):
    x, y, c = lax.axis_index("x"), lax.axis_index("y"), lax.axis_index("c")
    me, sibling = (x, y, c), (x, y, 1 - c)
    chips = [(1 - x, y), (x, 1 - y), (1 - x, 1 - y)]

    def rows(px, py, pc):
        return out_ref.at[4 * px + 2 * py + pc]

    def copy(k, block, to, src=None):
        return pltpu.make_async_remote_copy(
            src_ref=rows(*block) if src is None else src, dst_ref=rows(*block),
            send_sem=send_sems.at[k], recv_sem=recv_sems.at[k], device_id=to, device_id_type=MESH)

    mine = pltpu.make_async_copy(x_ref, rows(*me), local_sem)
    mine.start()
    first = [copy(0, me, sibling, src=x_ref)]
    first += [copy(1 + j, me, (*chip, c), src=x_ref) for j, chip in enumerate(chips)]
    for cp in first:
        cp.start()
    return copy, mine, first, me, sibling, chips, c


def _ag2_finish(state):
    copy, mine, first, me, sibling, chips, c = state
    passed = [copy(4 + j, (*chip, c), sibling) for j, chip in enumerate(chips)]
    for j, chip in enumerate(chips):
        copy(1 + j, (*chip, c), me).wait_recv()
        passed[j].start()
    copy(0, sibling, me).wait_recv()
    for j, chip in enumerate(chips):
        copy(4 + j, (*chip, 1 - c), me).wait_recv()
    for cp in first + passed:
        cp.wait_send()
    mine.wait()


def _exchange(arrays, name, gather):
    na = len(arrays)

    def body(*refs):
        cps = _xchg_copies(refs[:na], refs[na:2 * na], *refs[2 * na:], gather=gather)
        _xchg_start(cps)
        _xchg_finish(cps)

    anyspec = pl.BlockSpec(memory_space=pl.ANY)
    return pl.pallas_call(
        body, name=name, out_shape=_xchg_out_shapes(arrays, gather), in_specs=[anyspec] * na,
        out_specs=[anyspec] * na, scratch_shapes=_xchg_scratch(na),
    )(*arrays)


def _entry(c, wg_sh, win_sh, c_ctx, w_ada):
    n = w_ada.shape[1]

    def body(c_ref, g_ref, w_ref, cc_ref, wa_ref, call_ref, gall_ref, wall_ref, ada_ref, part,
             s_send, s_recv, s_loc, w_send, w_recv, w_loc, a_send, a_recv, a_loc):
        big = _ag2_start(w_ref, wall_ref, w_send, w_recv, w_loc)
        small = _xchg_copies([c_ref, g_ref], [call_ref, gall_ref], s_send, s_recv, s_loc, gather=True)
        _xchg_start(small)
        _xchg_finish(small)
        cs = jnp.concatenate([call_ref[:, 0, :], jnp.broadcast_to(cc_ref[...], (8, D))], axis=0)
        part[...] = _nn((cs * _sigmoid(cs)).astype(_BF), wa_ref[...].astype(_BF))
        ada = _xchg_copies([part], [ada_ref], a_send, a_recv, a_loc, gather=True)
        _xchg_start(ada)
        _xchg_finish(ada)
        _ag2_finish(big)

    vm = pl.BlockSpec(memory_space=pltpu.VMEM)
    return pl.pallas_call(
        body, name="entry_gather",
        out_shape=[jax.ShapeDtypeStruct((N_DEV,) + c.shape, F32), jax.ShapeDtypeStruct((N_DEV,) + wg_sh.shape, F32),
                   jax.ShapeDtypeStruct((N_DEV,) + win_sh.shape, win_sh.dtype),
                   jax.ShapeDtypeStruct((N_DEV, 16, n), F32)],
        in_specs=[vm] * 5, out_specs=[vm] * 4,
        scratch_shapes=[pltpu.VMEM((16, n), F32)] + _xchg_scratch(2)
        + [pltpu.SemaphoreType.DMA((7,)), pltpu.SemaphoreType.DMA((7,)), pltpu.SemaphoreType.DMA] + _xchg_scratch(1),
        compiler_params=pltpu.CompilerParams(vmem_limit_bytes=VMEM_BIG),
    )(c, wg_sh, win_sh, c_ctx, w_ada)


def _rope_tables(S):
    t = np.arange(S)
    row = (t // GRID_W).astype(np.float32)
    colp = (t % GRID_W).astype(np.float32)
    half = HD // 2
    inv = (ROPE_BASE ** (-np.arange(0, half, 2, dtype=np.float32) / half)).astype(np.float32)
    ar = row[:, None] * inv[None, :]
    ac = colp[:, None] * inv[None, :]
    ang = np.concatenate([ar, ar, ac, ac], axis=-1).astype(np.float32)
    cos = np.cos(ang).astype(np.float32)
    sin = np.sin(ang).astype(np.float32)
    lane = np.arange(HD)
    first = (lane % 32) < 16
    sa = np.where(first[None, :], -sin, 0.0)
    sb = np.where(first[None, :], 0.0, sin)

    def ext(tab, ctx_val):
        full = np.zeros((CTX + S, 128), np.float32)
        full[:CTX, :] = ctx_val
        full[CTX:, :HD] = tab
        full[CTX:, HD:] = tab
        return jnp.asarray(full)

    return ext(cos, 1.0), ext(sa, 0.0), ext(sb, 0.0)


def _pad_rows_win(wt):
    return jnp.pad(wt, ((0, NP - IN_COLS), (0, 0)))


def _unpad_rows_win(g):
    return g[0:IN_COLS]


def _local_step(x, ctx, target, ada_l, ada_c, gains, sink, win_p, wg_bd, bg, ggla, wout_sh, wffi_sh, wffo_sh):
    S = x.shape[0]
    cos, sa, sb = _rope_tables(S)
    g1, g2, g3, g4 = (gains[i:i + 1] for i in range(4))
    sh1, sc1, gt1, sh2, sc2, gt2 = (ada_l[i:i + 1] for i in range(6))
    sh1c, sc1c = ada_c[0:1], ada_c[1:2]
    gml, gmc, gm2 = g1 * (1.0 + sc1), g1 * (1.0 + sc1c), g3 * (1.0 + sc2)
    mavg = jnp.asarray(np.kron(np.eye(N_GLA, dtype=np.float32), np.full((DV, DV), 1.0 / DV, np.float32))).astype(_BF)

    n_ffi, r_ffo, r_out = wffi_sh.shape[0], wffo_sh.shape[0], wout_sh.shape[0]
    tt_e = 768 if (S + CTX) % 768 == 0 else 256
    tt_s = 512 if S % 512 == 0 else 256
    h, q, k, v, gq, gk, gv, gg, z, la, wout_g = _inproj_fwd(x, ctx, gml, sh1, gmc, sh1c, win_p, wg_bd, bg,
                                                            cos, sa, sb, [wout_sh])
    attn, lse, wffi_g = _attn_fwd(q, k, v, sink, [wffi_sh])
    o_f, st_f, wffo_g = _gla_fwd(gq, gk, gv, la, False, [wffo_sh])
    o_b, st_b = _gla_fwd(gq, gk, gv, la, True)
    wout = wout_g.reshape(N_DEV * r_out, D)
    wffi = wffi_g.reshape(N_DEV * n_ffi, D)
    wffo = wffo_g.reshape(N_DEV * r_ffo, D)
    x1, mix = _mix_fwd(x, attn, o_f, o_b, gg, ggla, mavg, wout, gt1, g2)
    dx1, h2, du, act, df, s_ffn, loss = _ffn(x1, target, gm2, sh2, gt2, g4, wffi, wffo)
    slab_ffi = _matmul_tn(h2, du, 512, tt_s, "grad_w_ffn_in", _BF, True).reshape(N_DEV, n_ffi, D)
    slab_ffo = _matmul_tn(act, df, FFN, tt_s, "grad_w_ffn_out", _BF).reshape(N_DEV, r_ffo, D)
    d_attn, do_gla, dgg, dy, s_mix = _mix_bwd(dx1, mix, o_f, o_b, gg, ggla, mavg, wout, gt1, g2)
    slab_out = _matmul_tn(mix, dy, D, tt_s, "grad_w_out", _BF).reshape(N_DEV, r_out, D)
    dq, dk, dv, dsink, got_ffi, got_ffo = _attn_bwd(q, k, v, sink, lse, d_attn, [slab_ffi, slab_ffo])
    pq, pk, pv, dlg_f, gwg_f, sbg_f, got_out = _gla_bwd(gq, gk, gv, la, z, st_f, do_gla, False, None, [slab_out])
    dgq, dgk, dgv, dlg_b, gwg_b, sbg_b = _gla_bwd(gq, gk, gv, la, z, st_b, do_gla, True, (pq, pk, pv))
    dp, grad_x, s_in = _inproj_bwd(x, ctx, gml, gmc, win_p, wg_bd, cos, sa, sb, dq, dk, dv,
                                   dgq, dgk, dgv, dgg, dlg_f, dlg_b, dx1)
    g_wg = jnp.concatenate([gwg_f, gwg_b], axis=1)
    s_bg = jnp.concatenate([sbg_f, sbg_b], axis=1)
    small = _small_grads(s_in, s_ffn, s_mix, ada_l, ada_c, gains, dsink, s_bg, g_wg, loss)
    n_in, half = IN_COLS // N_DEV, D // 2
    g_a, parts = _matmul_tn(h, dp, half, tt_e, "grad_w_in_a", _BF, True, a_cols=0, hosted=[small])
    slab_a = _unpad_rows_win(g_a).reshape(N_DEV, n_in, half)
    g_b, got_a = _matmul_tn(h, dp, half, tt_e, "grad_w_in_b", _BF, True, a_cols=1, hosted=[slab_a], gather=False)
    got_b, = _exchange([_unpad_rows_win(g_b).reshape(N_DEV, n_in, half)], "scatter_grads", False)
    return dict(grad_x=grad_x, got_in=[got_a, got_b], got_out=got_out, got_ffi=got_ffi, got_ffo=got_ffo, parts=parts)


SMALL_NAMES = ["c_ctx", "b_ada", "g_pre_mix", "g_post_mix", "g_pre_ffn", "g_post_ffn", "attn_sink",
               "b_gate_fwd", "b_gate_bwd", "g_gla_norm", "w_gate_fwd", "w_gate_bwd"]


def _small_update(tot, t_tot, wg_g, w, m, v):
    c1 = 1.0 / (1.0 - ADAM_B1 ** ADAM_STEP)
    c2 = 1.0 / (1.0 - ADAM_B2 ** ADAM_STEP)
    n = len(SMALL_NAMES)

    def body(tot_ref, t_ref, wg_ref, *refs):
        w_r, m_r, v_r = refs[0:n], refs[n:2 * n], refs[2 * n:3 * n]
        g_o, d_o, nm_o, nv_o = refs[3 * n:4 * n], refs[4 * n:5 * n], refs[5 * n:6 * n], refs[6 * n:7 * n]

        def upd(i, idx, g):
            nm = ADAM_B1 * m_r[i][idx] + (1.0 - ADAM_B1) * g
            nv = ADAM_B2 * v_r[i][idx] + (1.0 - ADAM_B2) * (g * g)
            g_o[i][idx] = g
            nm_o[i][idx] = nm
            nv_o[i][idx] = nv
            d_o[i][idx] = -ADAM_LR * ((nm * c1) / (jnp.sqrt(nv * c2) + ADAM_EPS) + ADAM_WD * w_r[i][idx])

        everything = (slice(None), slice(None))
        cc = w_r[0][...]
        sc = _sigmoid(cc)
        upd(0, everything, t_ref[0:1, :] * (sc * (1.0 + cc * (1.0 - sc))))
        for j in range(6):
            upd(1, (slice(None), slice(D * j, D * j + D)),
                tot_ref[R_ADA + j:R_ADA + j + 1, :] + tot_ref[R_ADA_C + j:R_ADA_C + j + 1, :])
        for j in range(4):
            upd(2 + j, everything, tot_ref[R_GAIN + j:R_GAIN + j + 1, :])
        upd(6, everything, tot_ref[R_SINK:R_SINK + 1, 0:N_ATT])
        upd(7, everything, tot_ref[R_BG:R_BG + 1, 0:256])
        upd(8, everything, tot_ref[R_BG:R_BG + 1, 256:512])
        upd(9, everything, tot_ref[R_GGLA:R_GGLA + 1, 0:DV])
        upd(10, (0,), wg_ref[0:GATE_RANK, :])
        upd(11, (0,), wg_ref[GATE_RANK:2 * GATE_RANK, :])

    params = [w[k] for k in SMALL_NAMES] + [m[k] for k in SMALL_NAMES] + [v[k] for k in SMALL_NAMES]
    outs = pl.pallas_call(
        body, name="small_update", grid=(1,),
        in_specs=[_full(tot.shape), _full(t_tot.shape), _full(wg_g.shape)] + [_full(p.shape) for p in params],
        out_specs=[_full(w[k].shape) for k in SMALL_NAMES] * 4,
        out_shape=[jax.ShapeDtypeStruct(w[k].shape, F32) for k in SMALL_NAMES] * 4,
        compiler_params=_cp(("arbitrary",)),
    )(tot, t_tot, wg_g, *params)
    return tuple(dict(zip(SMALL_NAMES, outs[i * n:(i + 1) * n])) for i in range(4))


def kernel(x, c, ctx, c_ctx, w_ada, b_ada, g_pre_mix, g_post_mix, g_pre_ffn, g_post_ffn, w_in, attn_sink, w_gate_fwd, b_gate_fwd, w_gate_bwd, b_gate_bwd, g_gla_norm, w_out, w_ffn_in, w_ffn_out, loss_target, m_c_ctx, m_w_ada, m_b_ada, m_g_pre_mix, m_g_post_mix, m_g_pre_ffn, m_g_post_ffn, m_w_in, m_attn_sink, m_w_gate_fwd, m_b_gate_fwd, m_w_gate_bwd, m_b_gate_bwd, m_g_gla_norm, m_w_out, m_w_ffn_in, m_w_ffn_out, v_c_ctx, v_w_ada, v_b_ada, v_g_pre_mix, v_g_post_mix, v_g_pre_ffn, v_g_post_ffn, v_w_in, v_attn_sink, v_w_gate_fwd, v_b_gate_fwd, v_w_gate_bwd, v_b_gate_bwd, v_g_gla_norm, v_w_out, v_w_ffn_in, v_w_ffn_out):
    me = 4 * lax.axis_index("x") + 2 * lax.axis_index("y") + lax.axis_index("c")
    S = x.shape[1]
    n_in = w_in.shape[2]
    n_ffi = w_ffn_in.shape[2]
    r_out = w_out.shape[1]
    r_ffo = w_ffn_out.shape[1]
    n_ada = w_ada.shape[2]

    wg_sh = jnp.concatenate([w_gate_fwd.reshape(4, 128), w_gate_bwd.reshape(4, 128)], axis=0)
    c_all3, g_all, w_all, ada_all = _entry(c, wg_sh, w_in[0].T.astype(_BF), c_ctx.reshape(1, D), w_ada[0])
    c_all = c_all3.reshape(N_DEV, D)
    wgf = g_all[:, 0:4].reshape(N_DEV, GATE_RANK, 32).transpose(1, 0, 2).reshape(GATE_RANK, 256)
    wgb = g_all[:, 4:8].reshape(N_DEV, GATE_RANK, 32).transpose(1, 0, 2).reshape(GATE_RANK, 256)
    win_p = _pad_rows_win(w_all.reshape(N_DEV * n_in, D))
    wg_bd = jnp.zeros((128, 512), F32).at[0:16, 0:256].set(wgf).at[16:32, 256:512].set(wgb).astype(_BF)
    ada_full = ada_all.transpose(1, 0, 2).reshape(16, N_DEV * n_ada) + b_ada
    ada_l = jnp.pad(lax.dynamic_slice_in_dim(ada_full, me, 1, 0).reshape(6, D), ((0, 2), (0, 0)))
    ada_c = jnp.pad(ada_full[8].reshape(6, D), ((0, 2), (0, 0)))
    gains = jnp.pad(jnp.concatenate([g_pre_mix, g_post_mix, g_pre_ffn, g_post_ffn], axis=0), ((0, 4), (0, 0)))
    sink = jnp.broadcast_to(attn_sink.reshape(8, 1), (8, 128))
    bg = jnp.concatenate([b_gate_fwd, b_gate_bwd], axis=1)
    ggla = jnp.tile(g_gla_norm, (1, N_GLA))

    r = _local_step(x[0], ctx[0], loss_target[0], ada_l, ada_c, gains, sink, win_p, wg_bd, bg, ggla,
                    w_out[0].astype(_BF), w_ffn_in[0].T.astype(_BF), w_ffn_out[0].astype(_BF))

    parts = r["parts"]
    tot = _sum_slots(parts, "sum_small_grads")
    loss = tot[R_LOSS, 0]
    d_ada_rows = parts[:, R_ADA:R_ADA + 6].reshape(N_DEV, 6 * D)
    d_ada_c = tot[R_ADA_C:R_ADA_C + 6].reshape(1, 6 * D)
    my_cols = lax.dynamic_slice_in_dim(jnp.concatenate([d_ada_rows, jnp.broadcast_to(d_ada_c, (1, 6 * D)),
                                                        jnp.zeros((7, 6 * D), F32)], axis=0), me * n_ada, n_ada, 1)
    grad_w_ada, t_part = _ada_bwd(c_all, c_ctx.reshape(1, D), w_ada[0], my_cols)
    wg_g = lax.dynamic_slice(tot, (R_WG, me * 32), (2 * GATE_RANK, 32))

    tr = lambda a: jnp.transpose(a, (0, 2, 1))
    big = {}
    t_all, = _exchange([t_part], "gather_c_ctx", True)
    t_tot = _sum_slots(t_all, "sum_c_ctx")
    for nm, w, g, m, v in [("w_ada", w_ada, grad_w_ada, m_w_ada, v_w_ada),
                           ("w_out", w_out, r["got_out"], m_w_out, v_w_out),
                           ("w_ffn_out", w_ffn_out, r["got_ffo"], m_w_ffn_out, v_w_ffn_out)]:
        big[nm] = _adamw(w, [g], m, v, "adamw_" + nm)
    big["w_ffn_in"] = tuple(tr(o) for o in _adamw(tr(w_ffn_in), [r["got_ffi"]], tr(m_w_ffn_in), tr(v_w_ffn_in),
                                                  "adamw_w_ffn_in"))
    big["w_in"] = tuple(tr(o) for o in _adamw(tr(w_in), r["got_in"], tr(m_w_in), tr(v_w_in), "adamw_w_in"))

    w_small = dict(c_ctx=c_ctx.reshape(1, D), b_ada=b_ada, g_pre_mix=g_pre_mix, g_post_mix=g_post_mix, g_pre_ffn=g_pre_ffn,
                   g_post_ffn=g_post_ffn, attn_sink=attn_sink, b_gate_fwd=b_gate_fwd, b_gate_bwd=b_gate_bwd,
                   g_gla_norm=g_gla_norm, w_gate_fwd=w_gate_fwd, w_gate_bwd=w_gate_bwd)
    m_small = dict(c_ctx=m_c_ctx.reshape(1, D), b_ada=m_b_ada, g_pre_mix=m_g_pre_mix, g_post_mix=m_g_post_mix,
                   g_pre_ffn=m_g_pre_ffn, g_post_ffn=m_g_post_ffn, attn_sink=m_attn_sink, b_gate_fwd=m_b_gate_fwd,
                   b_gate_bwd=m_b_gate_bwd, g_gla_norm=m_g_gla_norm, w_gate_fwd=m_w_gate_fwd, w_gate_bwd=m_w_gate_bwd)
    v_small = dict(c_ctx=v_c_ctx.reshape(1, D), b_ada=v_b_ada, g_pre_mix=v_g_pre_mix, g_post_mix=v_g_post_mix,
                   g_pre_ffn=v_g_pre_ffn, g_post_ffn=v_g_post_ffn, attn_sink=v_attn_sink, b_gate_fwd=v_b_gate_fwd,
                   b_gate_bwd=v_b_gate_bwd, g_gla_norm=v_g_gla_norm, w_gate_fwd=v_w_gate_fwd, w_gate_bwd=v_w_gate_bwd)
    grads_small, d_s, nm_s, nv_s = _small_update(tot, t_tot, wg_g, w_small, m_small, v_small)
    for dd in (grads_small, d_s, nm_s, nv_s):
        dd["c_ctx"] = dd["c_ctx"].reshape(D)

    order = ["c_ctx", "w_ada", "b_ada", "g_pre_mix", "g_post_mix", "g_pre_ffn", "g_post_ffn", "w_in", "attn_sink",
             "w_gate_fwd", "b_gate_fwd", "w_gate_bwd", "b_gate_bwd", "g_gla_norm", "w_out", "w_ffn_in", "w_ffn_out"]
    grads, deltas, new_m, new_v = [], [], [], []
    for nm in order:
        if nm in big:
            g_, d_, m_, v_ = big[nm]
        else:
            g_, d_, m_, v_ = grads_small[nm], d_s[nm], nm_s[nm], nv_s[nm]
        grads.append(g_)
        deltas.append(d_)
        new_m.append(m_)
        new_v.append(v_)
    return (loss, r["grad_x"][None], *grads, *deltas, *new_m, *new_v)
```

```python
import functools
import math

import numpy as np
import jax
import jax.numpy as jnp
from jax import lax
from jax.experimental import pallas as pl
from jax.experimental.pallas import tpu as pltpu

F32 = jnp.float32
_BF = jnp.bfloat16

N_DEV = 8
D = 1024
CTX = 256
HD = 64
N_ATT = 8
N_KV = 2
GRP = N_ATT // N_KV
WIN = 128
GRID_W = 64
ROPE_BASE = 10000.0
N_GLA = 8
DK = 32
DV = 64
GATE_RANK = 16
GATE_TAU = 16.0
FFN = 2816
EPS = 1e-6
NEG = -1e30
GLA_T = 128

QP = N_ATT * HD
KP = N_KV * HD
O_Q, O_K, O_V = 0, QP, QP + KP
O_GQ = O_V + KP
O_GK = O_GQ + N_GLA * DK
O_GV = O_GK + N_GLA * DK
O_GG = O_GV + N_GLA * DV
O_Z = O_GG + N_GLA * DV
NP = O_Z + 128
IN_COLS = 2336

ADAM_LR, ADAM_B1, ADAM_B2, ADAM_EPS, ADAM_WD, ADAM_STEP = 0.001, 0.9, 0.999, 1e-08, 0.01, 10

VMEM_BIG = 56 * 1024 * 1024
MESH = pl.DeviceIdType.MESH


def _cp(sem, vmem=None):
    return pltpu.CompilerParams(dimension_semantics=sem, vmem_limit_bytes=vmem)


def _full(shape):
    nd = len(shape)
    return pl.BlockSpec(shape, lambda *a: (0,) * nd)


def _rows(tile, width, off=0):
    return pl.BlockSpec((tile, width), lambda i: (i + off, 0))


def _rows_lat(tile, width):
    return pl.BlockSpec((tile, width), lambda i: (jnp.maximum(i - 1, 0), 0))


def _nt(a, b):
    return lax.dot_general(a, b, (((1,), (1,)), ((), ())), preferred_element_type=F32)


def _tn(a, b):
    return lax.dot_general(a, b, (((0,), (0,)), ((), ())), preferred_element_type=F32)


def _nn(a, b):
    return jnp.dot(a, b, preferred_element_type=F32)


def _head_mean(x, mavg):
    n = x.shape[0]
    hi = x.astype(_BF)
    lo = (x - hi.astype(F32)).astype(_BF)
    y = _nn(jnp.concatenate([hi, lo], axis=0), mavg)
    return y[0:n] + y[n:2 * n]


def _rope(t, cos, sa, sb):
    n = t.shape[1]
    reps = n // 128
    c = jnp.tile(cos, (1, reps))
    a = jnp.tile(sa, (1, reps))
    b = jnp.tile(sb, (1, reps))
    return t * c + pltpu.roll(t, n - 16, 1) * a + pltpu.roll(t, 16, 1) * b


def _unrope(t, cos, sa, sb):
    n = t.shape[1]
    reps = n // 128
    c = jnp.tile(cos, (1, reps))
    a = jnp.tile(sa, (1, reps))
    b = jnp.tile(sb, (1, reps))
    return t * c + pltpu.roll(t * a, 16, 1) + pltpu.roll(t * b, n - 16, 1)


def _sigmoid(x):
    return 1.0 / (1.0 + jnp.exp(-x))


def _inproj_fwd(x, ctx, gml, shl, gmc, shc, win, wg, bg, cos, sa, sb, shards):
    E = x.shape[0] + CTX
    TE = CTX

    def body(x_ref, c_ref, gml_ref, shl_ref, gmc_ref, shc_ref, w_ref, wg_ref, bg_ref, cos_ref, sa_ref, sb_ref,
             h_ref, q_ref, k_ref, v_ref, gq_ref, gk_ref, gv_ref, gg_ref, z_ref, la_ref):
        is_ctx = pl.program_id(0) == 0
        gm = jnp.where(is_ctx, gmc_ref[...], gml_ref[...])
        sh = jnp.where(is_ctx, shc_ref[...], shl_ref[...])
        x = jnp.where(is_ctx, c_ref[...], x_ref[...])
        r = lax.rsqrt(jnp.mean(x * x, axis=-1, keepdims=True) + EPS)
        hb = ((x * r) * gm + sh).astype(_BF)
        h_ref[...] = hb
        p = _nt(hb, w_ref[...])
        cos_t, sa_t, sb_t = cos_ref[...], sa_ref[...], sb_ref[...]
        q_ref[...] = (_rope(p[:, O_Q:O_K], cos_t, sa_t, sb_t) * (HD ** -0.5)).astype(_BF)
        k_ref[...] = _rope(p[:, O_K:O_V], cos_t, sa_t, sb_t).astype(_BF)
        v_ref[...] = p[:, O_V:O_GQ].astype(_BF)
        gq_ref[...] = p[:, O_GQ:O_GK] * (DK ** -0.5)
        gk_ref[...] = p[:, O_GK:O_GV]
        gv_ref[...] = p[:, O_GV:O_GG]
        gg_ref[...] = p[:, O_GG:O_Z]
        zb = p[:, O_Z:NP].astype(_BF)
        z_ref[...] = zb
        lg = _nn(zb, wg_ref[...]) + bg_ref[...]
        la_ref[...] = (jnp.minimum(lg, 0.0) - jnp.log(1.0 + jnp.exp(-jnp.abs(lg)))) * (1.0 / GATE_TAU)

    vec = _full((1, D))
    tab = _rows(TE, 128)
    outs = [(D, _BF), (QP, _BF), (KP, _BF), (KP, _BF), (256, F32), (256, F32), (512, F32), (512, F32),
            (128, _BF), (512, F32)]
    return _hosted_call(
        body, (x, ctx, gml, shl, gmc, shc, win, wg, bg, cos, sa, sb), shards, True,
        name="inproj_fwd", grid=(E // TE,),
        in_specs=[_rows_lat(TE, D), _full((CTX, D)), vec, vec, vec, vec, _full((NP, D)), _full((128, 512)),
                  _full((1, 512)), tab, tab, tab],
        out_specs=[_rows(TE, w) for w, _ in outs],
        out_shape=[jax.ShapeDtypeStruct((E, w), dt) for w, dt in outs],
        compiler_params=_cp(("arbitrary",), 40 * 1024 * 1024))


def _xchg_scratch(na):
    return [pltpu.SemaphoreType.DMA((na, N_DEV - 1)), pltpu.SemaphoreType.DMA((na, N_DEV - 1)),
            pltpu.SemaphoreType.DMA((na,))]


def _xchg_copies(ins, outs, send_sems, recv_sems, local_sems, gather):
    x, y, c = lax.axis_index("x"), lax.axis_index("y"), lax.axis_index("c")
    me = 4 * x + 2 * y + c
    local, sends, recvs = [], [], []
    for a in range(len(ins)):
        local.append(pltpu.make_async_copy(ins[a] if gather else ins[a].at[me], outs[a].at[me], local_sems.at[a]))
    for k in range(1, N_DEV):
        px, py, pc = x ^ (k >> 2), y ^ ((k >> 1) & 1), c ^ (k & 1)
        peer = 4 * px + 2 * py + pc
        for a in range(len(ins)):
            sems = dict(send_sem=send_sems.at[a, k - 1], recv_sem=recv_sems.at[a, k - 1], device_id_type=MESH)
            sends.append(pltpu.make_async_remote_copy(
                src_ref=ins[a] if gather else ins[a].at[peer], dst_ref=outs[a].at[me], device_id=(px, py, pc), **sems))
            recvs.append(pltpu.make_async_remote_copy(
                src_ref=ins[a] if gather else ins[a].at[me], dst_ref=outs[a].at[peer], device_id=(x, y, c), **sems))
    return local, sends, recvs


def _xchg_start(cps):
    local, sends, _ = cps
    for cp in local + sends:
        cp.start()


def _xchg_finish(cps):
    local, sends, recvs = cps
    for cp in recvs:
        cp.wait_recv()
    for cp in sends:
        cp.wait_send()
    for cp in local:
        cp.wait()


def _xchg_out_shapes(ins, gather):
    return [jax.ShapeDtypeStruct(((N_DEV,) + s.shape) if gather else s.shape, s.dtype) for s in ins]


def _hosted_call(body, args, hosted, gather, *, grid, in_specs, out_specs, out_shape, scratch_shapes=(), **kw):
    na = len(hosted)
    if na == 0:
        return pl.pallas_call(body, grid=grid, in_specs=in_specs, out_specs=out_specs, out_shape=out_shape,
                              scratch_shapes=list(scratch_shapes), **kw)(*args)
    n_in, n_out, n_scr = len(in_specs), len(out_specs), len(scratch_shapes)

    def wrapped(*refs):
        ins, h_in = refs[:n_in], refs[n_in:n_in + na]
        outs, h_out = refs[n_in + na:n_in + na + n_out], refs[n_in + na + n_out:n_in + 2 * na + n_out]
        scr = refs[n_in + 2 * na + n_out:]
        cps = _xchg_copies(h_in, h_out, *scr[n_scr:], gather=gather)
        pids = [pl.program_id(a) for a in range(len(grid))]
        first = functools.reduce(jnp.logical_and, [p == 0 for p in pids])
        last = functools.reduce(jnp.logical_and, [p == g - 1 for p, g in zip(pids, grid)])

        @pl.when(first)
        def _():
            _xchg_start(cps)

        body(*ins, *outs, *scr[:n_scr])

        @pl.when(last)
        def _():
            _xchg_finish(cps)

    anyspec = pl.BlockSpec(memory_space=pl.ANY)
    return pl.pallas_call(
        wrapped, grid=grid, in_specs=list(in_specs) + [anyspec] * na, out_specs=list(out_specs) + [anyspec] * na,
        out_shape=list(out_shape) + _xchg_out_shapes(hosted, gather),
        scratch_shapes=list(scratch_shapes) + _xchg_scratch(na), **kw)(*args, *hosted)


def _attn_specs(E):
    nb = (E - CTX) // WIN
    last = E // WIN - 1
    kc = pl.BlockSpec((CTX, KP), lambda n: (0, 0))
    kp = pl.BlockSpec((WIN, KP), lambda n: (n + 1, 0))
    kk = pl.BlockSpec((WIN, KP), lambda n: (n + 2, 0))
    kn = pl.BlockSpec((WIN, KP), lambda n: (jnp.minimum(n + 3, last), 0))
    return nb, [kc, kp, kk, kn]


def _attn_bias(nb):
    rows = np.arange(GRP * WIN)[:, None] % WIN
    cols = np.arange(CTX + 3 * WIN)[None, :]
    j = cols - CTX
    band = np.abs(j - WIN - rows) <= WIN
    out = []
    for first, last in ((True, False), (False, False), (False, True)):
        ok = (cols < CTX) | (band & ((j >= WIN) | (not first)) & ((j < 2 * WIN) | (not last)))
        out.append(np.where(ok, 0.0, NEG).astype(np.float32))
    bias = jnp.asarray(np.stack(out))
    spec = pl.BlockSpec((1, GRP * WIN, CTX + 3 * WIN),
                        lambda n: (jnp.where(n == 0, 0, jnp.where(n == nb - 1, 2, 1)), 0, 0))
    return bias, spec


def _both_halves(t, h):
    tf = t.astype(F32)
    r = pltpu.roll(tf, HD, 1)
    lo = lax.broadcasted_iota(jnp.int32, tf.shape, 1) < HD
    return (jnp.where(lo, tf, r) if h == 0 else jnp.where(lo, r, tf)).astype(t.dtype)


def _stack_heads(ref, h):
    lo = lax.broadcasted_iota(jnp.int32, (WIN, 128), 1) < HD
    parts = []
    for g in range(GRP):
        j = GRP * h + g
        t = ref[:, 128 * (j // 2):128 * (j // 2) + 128].astype(F32)
        parts.append(jnp.where(lo if j % 2 == 0 else jnp.logical_not(lo), t, 0.0))
    return jnp.concatenate(parts, axis=0)


def _unstack_pair(o, pp):
    lo = lax.broadcasted_iota(jnp.int32, (WIN, 128), 1) < HD
    return jnp.where(lo, o[WIN * 2 * pp:WIN * 2 * pp + WIN], o[WIN * (2 * pp + 1):WIN * (2 * pp + 1) + WIN])


def _attn_fwd(q, k, v, sink, shards):
    E = q.shape[0]
    S = E - CTX
    nb, kspecs = _attn_specs(E)
    na = len(shards)

    def body(q_ref, kc, kp, kk, kn, vc, vp, vk, vn, sink_ref, bias_ref, *rest):
        shard_refs, (o_ref, lse_ref, p_ref), got_refs = rest[:na], rest[na:na + 3], rest[na + 3:2 * na + 3]
        n = pl.program_id(0)
        cps = _xchg_copies(shard_refs, got_refs, *rest[2 * na + 3:], gather=True)

        @pl.when(n == 0)
        def _():
            _xchg_start(cps)

        lane = lax.broadcasted_iota(jnp.int32, (WIN, 128), 1)
        lse_t = jnp.zeros((WIN, 128), F32)
        kall = jnp.concatenate([kc[...], kp[...], kk[...], kn[...]], axis=0)
        vall = jnp.concatenate([vc[...], vp[...], vk[...], vn[...]], axis=0)
        K = [_both_halves(kall, h) for h in range(N_KV)]
        Q = [_stack_heads(q_ref, h).astype(_BF) for h in range(N_KV)]
        sk = [jnp.concatenate([jnp.broadcast_to(sink_ref[GRP * h + g:GRP * h + g + 1, 0:1], (WIN, 1))
                               for g in range(GRP)], axis=0) for h in range(N_KV)]
        s = [_nt(Q[h], K[h]) + bias_ref[0] for h in range(N_KV)]
        m = [jnp.maximum(jnp.max(s[h], axis=1, keepdims=True), sk[h]) for h in range(N_KV)]
        e = [jnp.exp(s[h] - m[h]) for h in range(N_KV)]
        den = [jnp.sum(e[h], axis=1, keepdims=True) + jnp.exp(sk[h] - m[h]) for h in range(N_KV)]
        V = [_both_halves(vall, h) for h in range(N_KV)]
        pb = [(e[h] * (1.0 / den[h])).astype(_BF) for h in range(N_KV)]
        for h in range(N_KV):
            p_ref[GRP * WIN * h:GRP * WIN * (h + 1), :] = pb[h]
        o = [_nn(pb[h], V[h]) for h in range(N_KV)]
        for h in range(N_KV):
            lse = m[h] + jnp.log(den[h])
            for g in range(GRP):
                lse_t = jnp.where(lane == GRP * h + g, lse[WIN * g:WIN * g + WIN], lse_t)
            for pp in range(GRP // 2):
                t = 2 * h + pp
                o_ref[:, 128 * t:128 * t + 128] = _unstack_pair(o[h], pp).astype(_BF)
        lse_ref[...] = lse_t

        @pl.when(n == nb - 1)
        def _():
            _xchg_finish(cps)

    qs = pl.BlockSpec((WIN, QP), lambda n: (n + 2, 0))
    anyspec = pl.BlockSpec(memory_space=pl.ANY)
    bias, bias_spec = _attn_bias(nb)
    return pl.pallas_call(
        body, name="attn_fwd", grid=(nb,),
        in_specs=[qs] + kspecs + kspecs + [_full((8, 128)), bias_spec] + [anyspec] * na,
        out_specs=[_rows(WIN, 512), _rows(WIN, 128), _rows(N_KV * GRP * WIN, CTX + 3 * WIN)] + [anyspec] * na,
        out_shape=[jax.ShapeDtypeStruct((S, 512), _BF), jax.ShapeDtypeStruct((S, 128), F32),
                   jax.ShapeDtypeStruct((nb * N_KV * GRP * WIN, CTX + 3 * WIN), _BF)]
        + _xchg_out_shapes(shards, True),
        scratch_shapes=_xchg_scratch(na),
        compiler_params=_cp(("arbitrary",)),
    )(q, k, k, k, k, v, v, v, v, sink, bias, *shards)


def _attn_bwd(q, k, v, sink, probs, lse, d_attn, slabs):
    E = q.shape[0]
    S = E - CTX
    nb, kspecs = _attn_specs(E)
    last = E // WIN - 1
    na = len(slabs)

    def body(q_ref, kc, kp, kk, kn, vc, vp, vk, vn, sink_ref, p_ref, lse_ref, do_ref, *rest):
        slab_refs, (dq_ref, dk_ref, dv_ref, ds_ref), got_refs = rest[:na], rest[na:na + 4], rest[na + 4:2 * na + 4]
        n = pl.program_id(0)
        cps = _xchg_copies(slab_refs, got_refs, *rest[2 * na + 4:], gather=False)

        @pl.when(n == 0)
        def _():
            _xchg_start(cps)
            dk_ref[...] = jnp.zeros_like(dk_ref)
            dv_ref[...] = jnp.zeros_like(dv_ref)
            ds_ref[...] = jnp.zeros_like(ds_ref)

        lane = lax.broadcasted_iota(jnp.int32, (WIN, 128), 1)
        lse_t = lse_ref[...]
        starts = [None, pl.multiple_of((n + 1) * WIN, WIN), pl.multiple_of((n + 2) * WIN, WIN),
                  pl.multiple_of(jnp.minimum(n + 3, last) * WIN, WIN)]
        kall = jnp.concatenate([kc[...], kp[...], kk[...], kn[...]], axis=0)
        vall = jnp.concatenate([vc[...], vp[...], vk[...], vn[...]], axis=0)
        for h in range(N_KV):
            hs = slice(HD * h, HD * h + HD)
            K = _both_halves(kall, h)
            V = _both_halves(vall, h)
            Q = _stack_heads(q_ref, h).astype(_BF)
            sk = jnp.concatenate([jnp.broadcast_to(sink_ref[GRP * h + g:GRP * h + g + 1, 0:1], (WIN, 1))
                                  for g in range(GRP)], axis=0)
            ls = jnp.concatenate([jnp.sum(jnp.where(lane == GRP * h + g, lse_t, 0.0), axis=1, keepdims=True)
                                  for g in range(GRP)], axis=0)
            do = _stack_heads(do_ref, h).astype(_BF)
            pb = p_ref[GRP * WIN * h:GRP * WIN * (h + 1), :]
            p = pb.astype(F32)
            dp = _nt(do, V)
            delta = jnp.sum(p * dp, axis=1, keepdims=True)
            dsc = (p * (dp - delta)).astype(_BF)
            dq = _nn(dsc, K) * (HD ** -0.5)
            for pp in range(GRP // 2):
                t = 2 * h + pp
                dq_ref[:, 128 * t:128 * t + 128] = _unstack_pair(dq, pp).astype(_BF)
            dK2 = _tn(Q, dsc)
            dV2 = _tn(do, pb)
            dK = dK2[0:HD] + dK2[HD:2 * HD]
            dV = dV2[0:HD] + dV2[HD:2 * HD]
            dk_ref[hs, 0:CTX] += dK[:, 0:CTX]
            dv_ref[hs, 0:CTX] += dV[:, 0:CTX]
            for w in range(1, 4):
                lo = CTX + WIN * (w - 1)
                dk_ref[hs, pl.ds(starts[w], WIN)] += dK[:, lo:lo + WIN]
                dv_ref[hs, pl.ds(starts[w], WIN)] += dV[:, lo:lo + WIN]
            psk = -jnp.exp(sk - ls) * delta
            for g in range(GRP):
                j = GRP * h + g
                tot = jnp.sum(psk[WIN * g:WIN * g + WIN], axis=0, keepdims=True)
                ds_ref[j:j + 1, :] += jnp.broadcast_to(tot, (1, 128))

        @pl.when(n == nb - 1)
        def _():
            _xchg_finish(cps)

    qs = pl.BlockSpec((WIN, QP), lambda n: (n + 2, 0))
    anyspec = pl.BlockSpec(memory_space=pl.ANY)
    return pl.pallas_call(
        body, name="attn_bwd", grid=(nb,),
        in_specs=[qs] + kspecs + kspecs + [_full((8, 128)), _rows(N_KV * GRP * WIN, CTX + 3 * WIN), _rows(WIN, 128),
                                            _rows(WIN, 512)] + [anyspec] * na,
        out_specs=[_rows(WIN, QP), _full((KP, E)), _full((KP, E)), _full((8, 128))] + [anyspec] * na,
        out_shape=[jax.ShapeDtypeStruct((S, QP), _BF), jax.ShapeDtypeStruct((KP, E), F32),
                   jax.ShapeDtypeStruct((KP, E), F32), jax.ShapeDtypeStruct((8, 128), F32)]
        + _xchg_out_shapes(slabs, False),
        scratch_shapes=_xchg_scratch(na),
        compiler_params=_cp(("arbitrary",), 48 * 1024 * 1024),
    )(q, k, k, k, k, v, v, v, v, sink, probs, lse, d_attn, *slabs)


def _gla_order(E, reverse, backward):
    nc = CTX // GLA_T
    n = E // GLA_T
    if not reverse:
        fwd = lambda s: s
    else:
        fwd = lambda s: jnp.where(s < nc, nc - 1 - s, n - 1 + nc - s)
    if backward:
        return lambda s: fwd(n - 1 - s)
    return fwd


def _gla_masks():
    T = GLA_T
    l128 = lax.broadcasted_iota(jnp.int32, (1, 128), 1)
    qmask = [((l128 >> 5) == j).astype(F32) for j in range(4)]
    vmask = [((l128 >> 6) == j).astype(F32) for j in range(2)]
    bd = ((lax.broadcasted_iota(jnp.int32, (512, 256), 0) >> 6)
          == (lax.broadcasted_iota(jnp.int32, (512, 256), 1) >> 5)).astype(F32)
    ri = lax.broadcasted_iota(jnp.int32, (T, 2 * T), 0)
    ci = lax.broadcasted_iota(jnp.int32, (T, 2 * T), 1) & (T - 1)
    return qmask, vmask, bd, ri, ci


def _tri_sum(tri, x):
    hi = x.astype(_BF)
    lo = (x - hi.astype(F32)).astype(_BF)
    n = x.shape[1]
    y = _nn(tri.astype(_BF), jnp.concatenate([hi, lo], axis=1))
    return y[:, 0:n] + y[:, n:2 * n]


def _gla_decays(la, reverse, ri, ci):
    T = GLA_T
    msk2 = (ri <= ci) if reverse else (ri >= ci)
    mskT2 = (ri >= ci) if reverse else (ri <= ci)
    b = _tri_sum(msk2[:, 0:T], la)
    bT = b[0:1] if reverse else b[T - 1:T]
    bm = b[T // 2:T // 2 + 1]
    return msk2, mskT2, b, bT, bm


def _pair_stack(tile, m0, m1):
    return jnp.concatenate([(tile * m0).astype(_BF), (tile * m1).astype(_BF)], axis=0)


def _gla_fwd(gq, gk, gv, la, reverse, shards=()):
    E = gq.shape[0]
    T = GLA_T
    n = E // T
    order = _gla_order(E, reverse, False)
    col = 1 if reverse else 0

    def body(gq_ref, gk_ref, gv_ref, la_ref, o_ref, st_ref, S_scr):
        @pl.when(pl.program_id(0) == 0)
        def _():
            S_scr[...] = jnp.zeros_like(S_scr)

        qmask, vmask, bd, ri, ci = _gla_masks()
        msk2, _, b, bT, bm = _gla_decays(la_ref[...], reverse, ri, ci)
        q, k, v = gq_ref[...], gk_ref[...], gv_ref[...]
        qd = (q * jnp.exp(b)).astype(_BF)
        qm = (q * jnp.exp(b - bm)).astype(_BF)
        km = k * jnp.exp(bm - b)
        kd = (k * jnp.exp(bT - b)).astype(_BF)
        ST = S_scr[...]
        comp = ST[0:DV]
        for h in range(1, N_GLA):
            comp = comp + ST[DV * h:DV * h + DV]
        st_ref[0] = comp
        inter = _nt(qd, ST.astype(_BF))
        tiles = []
        for p in range(N_GLA // 2):
            qs = slice(128 * (p // 2), 128 * (p // 2) + 128)
            vs = slice(128 * p, 128 * p + 128)
            j0 = (2 * p) % 4
            KS = _pair_stack(km[:, qs], qmask[j0], qmask[j0 + 1])
            VS = _pair_stack(v[:, vs], vmask[0], vmask[1])
            AA = jnp.where(msk2, _nt(qm[:, qs], KS), 0.0).astype(_BF)
            tiles.append(_nn(AA, VS))
        o_ref[...] = inter + jnp.concatenate(tiles, axis=1)
        S_scr[...] = ST * jnp.exp(bT) + bd * _tn(v.astype(_BF), kd)

    blk = lambda w, c=0: pl.BlockSpec((T, w), lambda s: (order(s), c))
    return _hosted_call(
        body, (gq, gk, gv, la), shards, True,
        name="gla_fwd_rev" if reverse else "gla_fwd", grid=(n,),
        in_specs=[blk(256), blk(256), blk(512), blk(256, col)],
        out_specs=[blk(512), pl.BlockSpec((1, DV, 256), lambda s: (order(s), 0, 0))],
        out_shape=[jax.ShapeDtypeStruct((E, 512), F32), jax.ShapeDtypeStruct((n, DV, 256), F32)],
        scratch_shapes=[pltpu.VMEM((512, 256), F32)],
        compiler_params=_cp(("arbitrary",)))


def _gla_bwd(gq, gk, gv, la, z, st, do, reverse, prev=None, slabs=()):
    E = gq.shape[0]
    T = GLA_T
    n = E // T
    nc = CTX // T
    order = _gla_order(E, reverse, True)
    col = 1 if reverse else 0
    np_ = 0 if prev is None else 3

    def body(gq_ref, gk_ref, gv_ref, la_ref, z_ref, st_ref, do_ref, *rest):
        prev_refs = rest[:np_]
        dq_ref, dk_ref, dv_ref, dlg_ref, gwg_ref, bsum_ref, dS_scr = rest[np_:]

        @pl.when(pl.program_id(0) == 0)
        def _():
            dS_scr[...] = jnp.zeros_like(dS_scr)
            gwg_ref[...] = jnp.zeros_like(gwg_ref)
            bsum_ref[...] = jnp.zeros_like(bsum_ref)

        is_lat = order(pl.program_id(0)) >= nc
        qmask, vmask, bd, ri, ci = _gla_masks()
        msk2, mskT2, b, bT, bm = _gla_decays(la_ref[...], reverse, ri, ci)
        q, k, v = gq_ref[...], gk_ref[...], gv_ref[...]
        do = jnp.where(is_lat, do_ref[...].astype(F32), 0.0)
        e_b, e_qm, e_km, e_kd, e_T = jnp.exp(b), jnp.exp(b - bm), jnp.exp(bm - b), jnp.exp(bT - b), jnp.exp(bT)
        qd, qm, km, kd = q * e_b, q * e_qm, k * e_km, k * e_kd
        qdb, qmb, kmb, kdb, vb, dob = (t.astype(_BF) for t in (qd, qm, km, kd, v, do))
        ST = jnp.tile(st_ref[0], (N_GLA, 1)) * bd
        dST = dS_scr[...]
        dSTb = dST.astype(_BF)
        dqd = _nn(dob, ST.astype(_BF))
        dkd = _nn(vb, dSTb)
        dv_t, dqm_t, dkm_t = [], [None, None], [None, None]
        for p in range(N_GLA // 2):
            t = p // 2
            qs = slice(128 * t, 128 * t + 128)
            vs = slice(128 * p, 128 * p + 128)
            j0 = (2 * p) % 4
            QS = _pair_stack(qm[:, qs], qmask[j0], qmask[j0 + 1])
            KS = _pair_stack(km[:, qs], qmask[j0], qmask[j0 + 1])
            VS = _pair_stack(v[:, vs], vmask[0], vmask[1])
            DS = _pair_stack(do[:, vs], vmask[0], vmask[1])
            ATT = jnp.where(mskT2, _nt(kmb[:, qs], QS), 0.0).astype(_BF)
            dAA = jnp.where(msk2, _nt(dob[:, vs], VS), 0.0).astype(_BF)
            dATT = jnp.where(mskT2, _nt(vb[:, vs], DS), 0.0).astype(_BF)
            dv_t.append(_nn(ATT, DS))
            dq_p = _nn(dAA, KS)
            dk_p = _nn(dATT, QS)
            dqm_t[t] = dq_p if dqm_t[t] is None else dqm_t[t] + dq_p
            dkm_t[t] = dk_p if dkm_t[t] is None else dkm_t[t] + dk_p
        dqm = jnp.concatenate(dqm_t, axis=1)
        dkm = jnp.concatenate(dkm_t, axis=1)
        dq = dqm * e_qm + dqd * e_b
        dk = dkm * e_km + dkd * e_kd
        dv = _nt(kdb, dSTb) + jnp.concatenate(dv_t, axis=1)
        if prev is None:
            dq_ref[...], dk_ref[...], dv_ref[...] = dq, dk, dv
        else:
            dq_ref[...] = ((dq + prev_refs[0][...]) * (DK ** -0.5)).astype(_BF)
            dk_ref[...] = (dk + prev_refs[1][...]).astype(_BF)
            dv_ref[...] = (dv + prev_refs[2][...]).astype(_BF)
        db = dqm * qm - dkm * km + dqd * qd - dkd * kd
        dbT = jnp.sum(dkd * kd, axis=0, keepdims=True) + e_T * jnp.sum(dST * ST, axis=0, keepdims=True)
        dla = _tri_sum(mskT2[:, 0:T], db) + dbT
        dlg = dla * (1.0 - jnp.exp(GATE_TAU * la_ref[...])) * (1.0 / GATE_TAU)
        bsum_ref[0:1, :] += jnp.sum(dlg, axis=0, keepdims=True)
        dlgb = dlg.astype(_BF)
        dlg_ref[...] = dlgb
        gwg_ref[...] += _tn(z_ref[...], dlgb)
        dS_scr[...] = dST * e_T + bd * _tn(dob, qdb)

    blk = lambda w, c=0: pl.BlockSpec((T, w), lambda s: (order(s), c))
    do_spec = pl.BlockSpec((T, 512), lambda s: (jnp.maximum(order(s) - nc, 0), 0))
    odt = F32 if prev is None else _BF
    return _hosted_call(
        body, (gq, gk, gv, la, z, st, do) + (() if prev is None else tuple(prev)), slabs, False,
        name="gla_bwd_rev" if reverse else "gla_bwd", grid=(n,),
        in_specs=[blk(256), blk(256), blk(512), blk(256, col), blk(128),
                  pl.BlockSpec((1, DV, 256), lambda s: (order(s), 0, 0)), do_spec]
        + ([] if prev is None else [blk(256), blk(256), blk(512)]),
        out_specs=[blk(256), blk(256), blk(512), blk(256), _full((128, 256)), _full((8, 256))],
        out_shape=[jax.ShapeDtypeStruct((E, 256), odt), jax.ShapeDtypeStruct((E, 256), odt),
                   jax.ShapeDtypeStruct((E, 512), odt), jax.ShapeDtypeStruct((E, 256), _BF),
                   jax.ShapeDtypeStruct((128, 256), F32), jax.ShapeDtypeStruct((8, 256), F32)],
        scratch_shapes=[pltpu.VMEM((512, 256), F32)],
        compiler_params=_cp(("arbitrary",)))


def _gla_out(o_f, o_b, gg, ggla, mavg):
    o = o_f + o_b
    rr = lax.rsqrt(_head_mean(o * o, mavg) + EPS)
    oh = o * rr
    sg = _sigmoid(gg)
    return oh, rr, sg


def _mix_fwd(x, attn, o_f, o_b, gg, ggla, mavg, wout, gt1, g2):
    S = x.shape[0]
    TM = 256

    def body(x_ref, a_ref, of_ref, ob_ref, gg_ref, ggla_ref, mavg_ref, w_ref, gt1_ref, g2_ref, x1_ref, mix_ref):
        gg_t = gg_ref[...]
        oh, _, sg = _gla_out(of_ref[...], ob_ref[...], gg_t, ggla_ref[...], mavg_ref[...])
        mix_ref[:, 0:512] = a_ref[...]
        mix_ref[:, 512:1024] = (oh * ggla_ref[...] * (gg_t * sg)).astype(_BF)
        y = _nn(mix_ref[...], w_ref[...])
        ry = lax.rsqrt(jnp.mean(y * y, axis=-1, keepdims=True) + EPS)
        x1_ref[...] = x_ref[...] + gt1_ref[...] * ((y * ry) * g2_ref[...])

    return pl.pallas_call(
        body, name="mix_fwd", grid=(S // TM,),
        in_specs=[_rows(TM, D), _rows(TM, 512), _rows(TM, 512, 1), _rows(TM, 512, 1), _rows(TM, 512, 1),
                  _full((1, 512)), _full((512, 512)), _full((D, D)), _full((1, D)), _full((1, D))],
        out_specs=[_rows(TM, D), _rows(TM, D)],
        out_shape=[jax.ShapeDtypeStruct((S, D), F32), jax.ShapeDtypeStruct((S, D), _BF)],
        compiler_params=_cp(("arbitrary",), 40 * 1024 * 1024),
    )(x, attn, o_f, o_b, gg, ggla, mavg, wout, gt1, g2)


def _mix_bwd(dx1, mix, o_f, o_b, gg, ggla, mavg, wout, gt1, g2):
    S = dx1.shape[0]
    TM = 256

    def body(dx_ref, mix_ref, of_ref, ob_ref, gg_ref, ggla_ref, mavg_ref, w_ref, gt1_ref, g2_ref,
             da_ref, do_ref, dgg_ref, dy_ref, sums_ref):
        @pl.when(pl.program_id(0) == 0)
        def _():
            sums_ref[...] = jnp.zeros_like(sums_ref)

        dx = dx_ref[...]
        y = _nn(mix_ref[...], w_ref[...])
        ry = lax.rsqrt(jnp.mean(y * y, axis=-1, keepdims=True) + EPS)
        yh = y * ry
        sums_ref[0:1, :] += jnp.sum(dx * yh, axis=0, keepdims=True)
        dyh = dx * (gt1_ref[...] * g2_ref[...])
        dy = (ry * (dyh - yh * jnp.mean(dyh * yh, axis=-1, keepdims=True))).astype(_BF)
        dy_ref[...] = dy
        dmix = _nt(dy, w_ref[...])
        da_ref[...] = dmix[:, 0:512].astype(_BF)
        dgla = dmix[:, 512:1024]
        gg_t = gg_ref[...]
        ggla_t = ggla_ref[...]
        oh, rr, sg = _gla_out(of_ref[...], ob_ref[...], gg_t, ggla_t, mavg_ref[...])
        dgg_ref[...] = (dgla * oh * ggla_t * (sg * (1.0 + gg_t * (1.0 - sg)))).astype(_BF)
        don = dgla * (gg_t * sg)
        sums_ref[1:2, 0:512] += jnp.sum(don * oh, axis=0, keepdims=True)
        doh = don * ggla_t
        do_ref[...] = (rr * (doh - oh * _head_mean(doh * oh, mavg_ref[...]))).astype(_BF)

    return pl.pallas_call(
        body, name="mix_bwd", grid=(S // TM,),
        in_specs=[_rows(TM, D), _rows(TM, D), _rows(TM, 512, 1), _rows(TM, 512, 1), _rows(TM, 512, 1),
                  _full((1, 512)), _full((512, 512)), _full((D, D)), _full((1, D)), _full((1, D))],
        out_specs=[_rows(TM, 512), _rows(TM, 512), _rows(TM, 512), _rows(TM, D), _full((8, D))],
        out_shape=[jax.ShapeDtypeStruct((S, 512), _BF), jax.ShapeDtypeStruct((S, 512), _BF),
                   jax.ShapeDtypeStruct((S, 512), _BF), jax.ShapeDtypeStruct((S, D), _BF),
                   jax.ShapeDtypeStruct((8, D), F32)],
        compiler_params=_cp(("arbitrary",), 40 * 1024 * 1024),
    )(dx1, mix, o_f, o_b, gg, ggla, mavg, wout, gt1, g2)


def _ffn(x1, target, gm2, sh2, gt2, g4, wffi, wffo):
    S = x1.shape[0]
    TF = 256

    def body(x_ref, t_ref, gm_ref, sh_ref, gt_ref, g4_ref, wi_hbm, wo_hbm,
             dx_ref, h_ref, du_ref, act_ref, df_ref, sums_ref, loss_ref, wi, wo, sem):
        @pl.when(pl.program_id(0) == 0)
        def _():
            c1 = pltpu.make_async_copy(wi_hbm, wi, sem.at[0])
            c2 = pltpu.make_async_copy(wo_hbm, wo, sem.at[1])
            c1.start()
            c2.start()
            sums_ref[...] = jnp.zeros_like(sums_ref)
            loss_ref[...] = jnp.zeros_like(loss_ref)
            c1.wait()
            c2.wait()

        x = x_ref[...]
        gm = gm_ref[...]
        r = lax.rsqrt(jnp.mean(x * x, axis=-1, keepdims=True) + EPS)
        xh = x * r
        hb = (xh * gm + sh_ref[...]).astype(_BF)
        h_ref[...] = hb
        u = _nt(hb, wi[...])
        g = u[:, 0:FFN]
        up = u[:, FFN:2 * FFN]
        sg = _sigmoid(g)
        sl = g * sg
        ab = (sl * up).astype(_BF)
        act_ref[...] = ab
        f = _nn(ab, wo[...])
        rf = lax.rsqrt(jnp.mean(f * f, axis=-1, keepdims=True) + EPS)
        fh = f * rf
        gt, g4v = gt_ref[...], g4_ref[...]
        err = x + gt * (fh * g4v) - t_ref[...]
        loss_ref[...] += jnp.sum(err * err) * (0.5 / D)
        dout = err * (1.0 / D)
        sums_ref[2:3, :] += jnp.sum(dout * fh, axis=0, keepdims=True)
        dfh = dout * (gt * g4v)
        dfb = (rf * (dfh - fh * jnp.mean(dfh * fh, axis=-1, keepdims=True))).astype(_BF)
        df_ref[...] = dfb
        dact = _nt(dfb, wo[...])
        du_ref[:, 0:FFN] = (dact * up * (sg * (1.0 + g * (1.0 - sg)))).astype(_BF)
        du_ref[:, FFN:2 * FFN] = (dact * sl).astype(_BF)
        dh = _nn(du_ref[...], wi[...])
        sums_ref[0:1, :] += jnp.sum(dh, axis=0, keepdims=True)
        sums_ref[1:2, :] += jnp.sum(dh * xh, axis=0, keepdims=True)
        dxh = dh * gm
        dx_ref[...] = dout + r * (dxh - xh * jnp.mean(dxh * xh, axis=-1, keepdims=True))

    vec = _full((1, D))
    anyspec = pl.BlockSpec(memory_space=pl.ANY)
    return pl.pallas_call(
        body, name="ffn_fwd_bwd", grid=(S // TF,),
        in_specs=[_rows(TF, D), _rows(TF, D), vec, vec, vec, vec, anyspec, anyspec],
        out_specs=[_rows(TF, D), _rows(TF, D), _rows(TF, 2 * FFN), _rows(TF, FFN), _rows(TF, D),
                   _full((8, D)), _full((8, 128))],
        out_shape=[jax.ShapeDtypeStruct((S, D), F32), jax.ShapeDtypeStruct((S, D), _BF),
                   jax.ShapeDtypeStruct((S, 2 * FFN), _BF), jax.ShapeDtypeStruct((S, FFN), _BF),
                   jax.ShapeDtypeStruct((S, D), _BF), jax.ShapeDtypeStruct((8, D), F32),
                   jax.ShapeDtypeStruct((8, 128), F32)],
        scratch_shapes=[pltpu.VMEM((2 * FFN, D), _BF), pltpu.VMEM((FFN, D), _BF), pltpu.SemaphoreType.DMA((2,))],
        compiler_params=_cp(("arbitrary",), VMEM_BIG),
    )(x1, target, gm2, sh2, gt2, g4, wffi, wffo)


def _inproj_bwd(x, ctx, gml, gmc, win, wg, cos, sa, sb, dq, dk, dv, dgq, dgk, dgv, dgg, dlg_f, dlg_b, dx1):
    S = x.shape[0]
    E = S + CTX
    TE = CTX

    def body(x_ref, c_ref, gml_ref, gmc_ref, w_ref, wg_ref, cos_ref, sa_ref, sb_ref, dq_ref, dk_ref, dv_ref,
             gq_ref, gk_ref, gv_ref, dgg_ref, dlf, dlb, dx1_ref, dp_ref, gx_ref, sums_ref):
        i = pl.program_id(0)
        is_ctx = i == 0

        @pl.when(is_ctx)
        def _():
            sums_ref[...] = jnp.zeros_like(sums_ref)

        lat = jnp.where(is_ctx, 0.0, 1.0)
        cos_t, sa_t, sb_t = cos_ref[...], sa_ref[...], sb_ref[...]
        dp_ref[:, O_Q:O_K] = (_unrope(dq_ref[...].astype(F32), cos_t, sa_t, sb_t) * lat).astype(_BF)
        dp_ref[:, O_K:O_V] = _unrope(dk_ref[...].T, cos_t, sa_t, sb_t).astype(_BF)
        dp_ref[:, O_V:O_GQ] = dv_ref[...].T.astype(_BF)
        dp_ref[:, O_GQ:O_GK] = gq_ref[...]
        dp_ref[:, O_GK:O_GV] = gk_ref[...]
        dp_ref[:, O_GV:O_GG] = gv_ref[...]
        dp_ref[:, O_GG:O_Z] = (dgg_ref[...].astype(F32) * lat).astype(_BF)
        dlg = jnp.concatenate([dlf[...], dlb[...]], axis=1)
        dp_ref[:, O_Z:NP] = _nt(dlg, wg_ref[...]).astype(_BF)
        dh = _nn(dp_ref[...], w_ref[...])
        x = jnp.where(is_ctx, c_ref[...], x_ref[...])
        r = lax.rsqrt(jnp.mean(x * x, axis=-1, keepdims=True) + EPS)
        xh = x * r
        sdh = jnp.sum(dh, axis=0, keepdims=True)
        sdx = jnp.sum(dh * xh, axis=0, keepdims=True)
        sums_ref[0:1, :] += sdh * lat
        sums_ref[1:2, :] += sdx * lat
        sums_ref[2:3, :] += sdh * (1.0 - lat)
        sums_ref[3:4, :] += sdx * (1.0 - lat)
        dxh = dh * jnp.where(is_ctx, gmc_ref[...], gml_ref[...])
        gx_ref[...] = dx1_ref[...] + r * (dxh - xh * jnp.mean(dxh * xh, axis=-1, keepdims=True))

    vec = _full((1, D))
    tab = _rows(TE, 128)
    return pl.pallas_call(
        body, name="inproj_bwd", grid=(E // TE,),
        in_specs=[_rows_lat(TE, D), _full((CTX, D)), vec, vec, _full((NP, D)), _full((128, 512)), tab, tab, tab,
                  _rows_lat(TE, QP), pl.BlockSpec((KP, TE), lambda i: (0, i)), pl.BlockSpec((KP, TE), lambda i: (0, i)),
                  _rows(TE, 256), _rows(TE, 256), _rows(TE, 512), _rows_lat(TE, 512), _rows(TE, 256), _rows(TE, 256),
                  _rows_lat(TE, D)],
        out_specs=[_rows(TE, NP), _rows_lat(TE, D), _full((8, D))],
        out_shape=[jax.ShapeDtypeStruct((E, NP), _BF), jax.ShapeDtypeStruct((S, D), F32),
                   jax.ShapeDtypeStruct((8, D), F32)],
        compiler_params=_cp(("arbitrary",), VMEM_BIG),
    )(x, ctx, gml, gmc, win, wg, cos, sa, sb, dq, dk, dv, dgq, dgk, dgv, dgg, dlg_f, dlg_b, dx1)


def _matmul_tn(a, b, tk, tt, name, out_dtype, transpose_out=False, a_cols=None, hosted=(), gather=True):
    T, KA = a.shape
    N = b.shape[1]
    nt = T // tt
    k0 = 0
    if a_cols is not None:
        KA, k0 = tk, a_cols

    def body(a_ref, b_ref, o_ref, acc):
        t = pl.program_id(1)

        @pl.when(t == 0)
        def _():
            acc[...] = jnp.zeros_like(acc)

        acc[...] += _tn(a_ref[...], b_ref[...])

        @pl.when(t == nt - 1)
        def _():
            o_ref[...] = (acc[...].T if transpose_out else acc[...]).astype(out_dtype)

    if transpose_out:
        out_spec, out_shape = pl.BlockSpec((N, tk), lambda i, t: (0, i)), (N, KA)
    else:
        out_spec, out_shape = pl.BlockSpec((tk, N), lambda i, t: (i, 0)), (KA, N)
    res = _hosted_call(
        body, (a, b), hosted, gather, name=name, grid=(KA // tk, nt),
        in_specs=[pl.BlockSpec((tt, tk), lambda i, t: (t, i + k0)), pl.BlockSpec((tt, N), lambda i, t: (t, 0))],
        out_specs=[out_spec], out_shape=[jax.ShapeDtypeStruct(out_shape, out_dtype)],
        scratch_shapes=[pltpu.VMEM((tk, N), F32)],
        compiler_params=_cp(("arbitrary", "arbitrary"), VMEM_BIG))
    return res if hosted else res[0]


def _ada_bwd(c_all, c_ctx, w_ada, d_all):
    n = w_ada.shape[1]

    def body(c_ref, cc_ref, w_ref, d_ref, gw_ref, t_ref):
        c = jnp.concatenate([c_ref[...], jnp.broadcast_to(cc_ref[...], (8, D))], axis=0)
        db = d_ref[...].astype(_BF)
        gw_ref[0] = _tn((c * _sigmoid(c)).astype(_BF), db)
        t_ref[...] = _nt(db[8:16], w_ref[...].astype(_BF))

    return pl.pallas_call(
        body, name="ada_bwd", in_specs=[_full((8, D)), _full((1, D)), _full((D, n)), _full((16, n))],
        out_specs=[_full((1, D, n)), _full((8, D))],
        out_shape=[jax.ShapeDtypeStruct((1, D, n), F32), jax.ShapeDtypeStruct((8, D), F32)], grid=(1,),
        compiler_params=_cp(("arbitrary",)),
    )(c_all, c_ctx, w_ada, d_all)


PART_ROWS = 56
R_ADA, R_ADA_C, R_GAIN, R_SINK, R_BG, R_GGLA, R_LOSS, R_WG = 0, 6, 12, 16, 17, 18, 19, 24


def _small_grads(s_in, s_ffn, s_mix, ada_l, ada_c, gains, dsink, s_bg, g_wg, loss):
    def body(si, sf, sm, al, ac, g, ds, sbg, gwg, loss_ref, o_ref):
        o_ref[...] = jnp.zeros_like(o_ref)
        o_ref[R_LOSS:R_LOSS + 1, 0:128] = loss_ref[0:1, :]
        sub = lax.broadcasted_iota(jnp.int32, (8, 128), 0)
        lane = lax.broadcasted_iota(jnp.int32, (8, 128), 1)
        o_ref[R_SINK:R_SINK + 1, 0:128] = jnp.sum(jnp.where(sub == lane, ds[...], 0.0), axis=0, keepdims=True)
        o_ref[R_BG:R_BG + 1, 0:512] = sbg[0:1, :]
        y = sm[1:2, 0:128] + sm[1:2, 128:256] + sm[1:2, 256:384] + sm[1:2, 384:512]
        y = y + pltpu.roll(y, 64, 1)
        o_ref[R_GGLA:R_GGLA + 1, 0:128] = jnp.where(lane[0:1] < DV, y, 0.0)
        o_ref[R_WG:R_WG + 16, 0:256] = gwg[0:16, 0:256]
        o_ref[R_WG + 16:R_WG + 32, 0:256] = gwg[16:32, 256:512]
        sdh_l, sdx_l, sdh_c, sdx_c = si[0:1], si[1:2], si[2:3], si[3:4]
        sdh2, sdx2, a2 = sf[0:1], sf[1:2], sf[2:3]
        a1 = sm[0:1]
        g1, g2, g3, g4 = g[0:1], g[1:2], g[2:3], g[3:4]
        sc1, gt1, sc2, gt2 = al[1:2], al[2:3], al[4:5], al[5:6]
        sc1c = ac[1:2]
        z = jnp.zeros((1, D), F32)
        rows = [sdh_l, sdx_l * g1, a1 * g2, sdh2, sdx2 * g3, a2 * g4,
                sdh_c, sdx_c * g1, z, z, z, z,
                sdx_l * (1.0 + sc1) + sdx_c * (1.0 + sc1c), a1 * gt1, sdx2 * (1.0 + sc2), a2 * gt2]
        for r, v in enumerate(rows):
            o_ref[r:r + 1, :] = v

    v8 = _full((8, D))
    return pl.pallas_call(
        body, name="small_grads",
        in_specs=[v8] * 6 + [_full((8, 128)), _full((8, 512)), _full((128, 512)), _full((8, 128))],
        out_specs=_full((PART_ROWS, D)), out_shape=jax.ShapeDtypeStruct((PART_ROWS, D), F32), grid=(1,),
        compiler_params=_cp(("arbitrary",)),
    )(s_in, s_ffn, s_mix, ada_l, ada_c, gains, dsink, s_bg, g_wg, loss)


def _row_tile(R):
    for cand in (256, 128, 64, 32, 16):
        if R % cand == 0 and R > cand:
            return cand
    return R


def _adamw(w, gs, m, v, name):
    _, R, C = w.shape
    tr = _row_tile(R)
    c1 = 1.0 / (1.0 - ADAM_B1 ** ADAM_STEP)
    c2 = 1.0 / (1.0 - ADAM_B2 ** ADAM_STEP)
    ng = len(gs)

    def body(w_ref, *refs):
        g_refs, (m_ref, v_ref, go_ref, d_ref, nm_ref, nv_ref) = refs[:ng], refs[ng:]
        c0 = 0
        for g_ref in g_refs:
            cols = slice(c0, c0 + g_ref.shape[2])
            c0 += g_ref.shape[2]
            gg = g_ref[0].astype(F32)
            for j in range(1, g_ref.shape[0]):
                gg = gg + g_ref[j].astype(F32)
            go_ref[0, :, cols] = gg
            nm = ADAM_B1 * m_ref[0, :, cols] + (1.0 - ADAM_B1) * gg
            nv = ADAM_B2 * v_ref[0, :, cols] + (1.0 - ADAM_B2) * (gg * gg)
            nm_ref[0, :, cols] = nm
            nv_ref[0, :, cols] = nv
            d_ref[0, :, cols] = -ADAM_LR * ((nm * c1) / (jnp.sqrt(nv * c2) + ADAM_EPS) + ADAM_WD * w_ref[0, :, cols])

    spec = pl.BlockSpec((1, tr, C), lambda i: (0, i, 0))
    sds = jax.ShapeDtypeStruct((1, R, C), F32)
    g_specs = [pl.BlockSpec((g.shape[0], tr, g.shape[2]), lambda i: (0, i, 0)) for g in gs]
    return pl.pallas_call(
        body, name=name, grid=(R // tr,), in_specs=[spec] + g_specs + [spec, spec], out_specs=[spec] * 4,
        out_shape=[sds] * 4, compiler_params=_cp(("parallel",), 48 * 1024 * 1024),
    )(w, *gs, m, v)


def _sum_slots(slots, name):
    _, R, C = slots.shape
    tr = _row_tile(R)

    def body(s_ref, o_ref):
        acc = s_ref[0].astype(F32)
        for j in range(1, N_DEV):
            acc = acc + s_ref[j].astype(F32)
        o_ref[...] = acc

    return pl.pallas_call(
        body, name=name, grid=(R // tr,), in_specs=[pl.BlockSpec((N_DEV, tr, C), lambda i: (0, i, 0))],
        out_specs=_rows(tr, C), out_shape=jax.ShapeDtypeStruct((R, C), F32), compiler_params=_cp(("parallel",)),
    )(slots)


def _ag2_start(x_ref, out_ref, send_sems, recv_sems, local_sem):
    x, y, c = lax.axis_index("x"), lax.axis_index("y"), lax.axis_index("c")
    me, sibling = (x, y, c), (x, y, 1 - c)
    chips = [(1 - x, y), (x, 1 - y), (1 - x, 1 - y)]

    def rows(px, py, pc):
        return out_ref.at[4 * px + 2 * py + pc]

    def copy(k, block, to, src=None):
        return pltpu.make_async_remote_copy(
            src_ref=rows(*block) if src is None else src, dst_ref=rows(*block),
            send_sem=send_sems.at[k], recv_sem=recv_sems.at[k], device_id=to, device_id_type=MESH)

    mine = pltpu.make_async_copy(x_ref, rows(*me), local_sem)
    mine.start()
    first = [copy(0, me, sibling, src=x_ref)]
    first += [copy(1 + j, me, (*chip, c), src=x_ref) for j, chip in enumerate(chips)]
    for cp in first:
        cp.start()
    return copy, mine, first, me, sibling, chips, c


def _ag2_finish(state):
    copy, mine, first, me, sibling, chips, c = state
    passed = [copy(4 + j, (*chip, c), sibling) for j, chip in enumerate(chips)]
    for j, chip in enumerate(chips):
        copy(1 + j, (*chip, c), me).wait_recv()
        passed[j].start()
    copy(0, sibling, me).wait_recv()
    for j, chip in enumerate(chips):
        copy(4 + j, (*chip, 1 - c), me).wait_recv()
    for cp in first + passed:
        cp.wait_send()
    mine.wait()


def _exchange(arrays, name, gather):
    na = len(arrays)

    def body(*refs):
        cps = _xchg_copies(refs[:na], refs[na:2 * na], *refs[2 * na:], gather=gather)
        _xchg_start(cps)
        _xchg_finish(cps)

    anyspec = pl.BlockSpec(memory_space=pl.ANY)
    return pl.pallas_call(
        body, name=name, out_shape=_xchg_out_shapes(arrays, gather), in_specs=[anyspec] * na,
        out_specs=[anyspec] * na, scratch_shapes=_xchg_scratch(na),
    )(*arrays)


def _entry(c, wg_sh, win_sh, c_ctx, w_ada):
    n = w_ada.shape[1]

    def body(c_ref, g_ref, w_ref, cc_ref, wa_ref, call_ref, gall_ref, wall_ref, ada_ref, part,
             s_send, s_recv, s_loc, w_send, w_recv, w_loc, a_send, a_recv, a_loc):
        big = _ag2_start(w_ref, wall_ref, w_send, w_recv, w_loc)
        small = _xchg_copies([c_ref, g_ref], [call_ref, gall_ref], s_send, s_recv, s_loc, gather=True)
        _xchg_start(small)
        _xchg_finish(small)
        cs = jnp.concatenate([call_ref[:, 0, :], jnp.broadcast_to(cc_ref[...], (8, D))], axis=0)
        part[...] = _nn((cs * _sigmoid(cs)).astype(_BF), wa_ref[...].astype(_BF))
        ada = _xchg_copies([part], [ada_ref], a_send, a_recv, a_loc, gather=True)
        _xchg_start(ada)
        _xchg_finish(ada)
        _ag2_finish(big)

    vm = pl.BlockSpec(memory_space=pltpu.VMEM)
    return pl.pallas_call(
        body, name="entry_gather",
        out_shape=[jax.ShapeDtypeStruct((N_DEV,) + c.shape, F32), jax.ShapeDtypeStruct((N_DEV,) + wg_sh.shape, F32),
                   jax.ShapeDtypeStruct((N_DEV,) + win_sh.shape, win_sh.dtype),
                   jax.ShapeDtypeStruct((N_DEV, 16, n), F32)],
        in_specs=[vm] * 5, out_specs=[vm] * 4,
        scratch_shapes=[pltpu.VMEM((16, n), F32)] + _xchg_scratch(2)
        + [pltpu.SemaphoreType.DMA((7,)), pltpu.SemaphoreType.DMA((7,)), pltpu.SemaphoreType.DMA] + _xchg_scratch(1),
        compiler_params=pltpu.CompilerParams(vmem_limit_bytes=VMEM_BIG),
    )(c, wg_sh, win_sh, c_ctx, w_ada)


def _rope_tables(S):
    t = np.arange(S)
    row = (t // GRID_W).astype(np.float32)
    colp = (t % GRID_W).astype(np.float32)
    half = HD // 2
    inv = (ROPE_BASE ** (-np.arange(0, half, 2, dtype=np.float32) / half)).astype(np.float32)
    ar = row[:, None] * inv[None, :]
    ac = colp[:, None] * inv[None, :]
    ang = np.concatenate([ar, ar, ac, ac], axis=-1).astype(np.float32)
    cos = np.cos(ang).astype(np.float32)
    sin = np.sin(ang).astype(np.float32)
    lane = np.arange(HD)
    first = (lane % 32) < 16
    sa = np.where(first[None, :], -sin, 0.0)
    sb = np.where(first[None, :], 0.0, sin)

    def ext(tab, ctx_val):
        full = np.zeros((CTX + S, 128), np.float32)
        full[:CTX, :] = ctx_val
        full[CTX:, :HD] = tab
        full[CTX:, HD:] = tab
        return jnp.asarray(full)

    return ext(cos, 1.0), ext(sa, 0.0), ext(sb, 0.0)


def _pad_rows_win(wt):
    return jnp.pad(wt, ((0, NP - IN_COLS), (0, 0)))


def _unpad_rows_win(g):
    return g[0:IN_COLS]


def _local_step(x, ctx, target, ada_l, ada_c, gains, sink, win_p, wg_bd, bg, ggla, wout_sh, wffi_sh, wffo_sh):
    S = x.shape[0]
    cos, sa, sb = _rope_tables(S)
    g1, g2, g3, g4 = (gains[i:i + 1] for i in range(4))
    sh1, sc1, gt1, sh2, sc2, gt2 = (ada_l[i:i + 1] for i in range(6))
    sh1c, sc1c = ada_c[0:1], ada_c[1:2]
    gml, gmc, gm2 = g1 * (1.0 + sc1), g1 * (1.0 + sc1c), g3 * (1.0 + sc2)
    mavg = jnp.asarray(np.kron(np.eye(N_GLA, dtype=np.float32), np.full((DV, DV), 1.0 / DV, np.float32))).astype(_BF)

    n_ffi, r_ffo, r_out = wffi_sh.shape[0], wffo_sh.shape[0], wout_sh.shape[0]
    tt_e = 768 if (S + CTX) % 768 == 0 else 256
    tt_s = 512 if S % 512 == 0 else 256
    h, q, k, v, gq, gk, gv, gg, z, la, wout_g = _inproj_fwd(x, ctx, gml, sh1, gmc, sh1c, win_p, wg_bd, bg,
                                                            cos, sa, sb, [wout_sh])
    attn, lse, probs, wffi_g = _attn_fwd(q, k, v, sink, [wffi_sh])
    o_f, st_f, wffo_g = _gla_fwd(gq, gk, gv, la, False, [wffo_sh])
    o_b, st_b = _gla_fwd(gq, gk, gv, la, True)
    wout = wout_g.reshape(N_DEV * r_out, D)
    wffi = wffi_g.reshape(N_DEV * n_ffi, D)
    wffo = wffo_g.reshape(N_DEV * r_ffo, D)
    x1, mix = _mix_fwd(x, attn, o_f, o_b, gg, ggla, mavg, wout, gt1, g2)
    dx1, h2, du, act, df, s_ffn, loss = _ffn(x1, target, gm2, sh2, gt2, g4, wffi, wffo)
    slab_ffi = _matmul_tn(h2, du, 512, tt_s, "grad_w_ffn_in", _BF, True).reshape(N_DEV, n_ffi, D)
    slab_ffo = _matmul_tn(act, df, FFN, tt_s, "grad_w_ffn_out", _BF).reshape(N_DEV, r_ffo, D)
    d_attn, do_gla, dgg, dy, s_mix = _mix_bwd(dx1, mix, o_f, o_b, gg, ggla, mavg, wout, gt1, g2)
    slab_out = _matmul_tn(mix, dy, D, tt_s, "grad_w_out", _BF).reshape(N_DEV, r_out, D)
    dq, dk, dv, dsink, got_ffi, got_ffo = _attn_bwd(q, k, v, sink, probs, lse, d_attn, [slab_ffi, slab_ffo])
    pq, pk, pv, dlg_f, gwg_f, sbg_f, got_out = _gla_bwd(gq, gk, gv, la, z, st_f, do_gla, False, None, [slab_out])
    dgq, dgk, dgv, dlg_b, gwg_b, sbg_b = _gla_bwd(gq, gk, gv, la, z, st_b, do_gla, True, (pq, pk, pv))
    dp, grad_x, s_in = _inproj_bwd(x, ctx, gml, gmc, win_p, wg_bd, cos, sa, sb, dq, dk, dv,
                                   dgq, dgk, dgv, dgg, dlg_f, dlg_b, dx1)
    g_wg = jnp.concatenate([gwg_f, gwg_b], axis=1)
    s_bg = jnp.concatenate([sbg_f, sbg_b], axis=1)
    small = _small_grads(s_in, s_ffn, s_mix, ada_l, ada_c, gains, dsink, s_bg, g_wg, loss)
    n_in, half = IN_COLS // N_DEV, D // 2
    g_a, parts = _matmul_tn(h, dp, half, tt_e, "grad_w_in_a", _BF, True, a_cols=0, hosted=[small])
    slab_a = _unpad_rows_win(g_a).reshape(N_DEV, n_in, half)
    g_b, got_a = _matmul_tn(h, dp, half, tt_e, "grad_w_in_b", _BF, True, a_cols=1, hosted=[slab_a], gather=False)
    got_b, = _exchange([_unpad_rows_win(g_b).reshape(N_DEV, n_in, half)], "scatter_grads", False)
    return dict(grad_x=grad_x, got_in=[got_a, got_b], got_out=got_out, got_ffi=got_ffi, got_ffo=got_ffo, parts=parts)


SMALL_NAMES = ["c_ctx", "b_ada", "g_pre_mix", "g_post_mix", "g_pre_ffn", "g_post_ffn", "attn_sink",
               "b_gate_fwd", "b_gate_bwd", "g_gla_norm", "w_gate_fwd", "w_gate_bwd"]


def _small_update(tot, t_tot, wg_g, w, m, v):
    c1 = 1.0 / (1.0 - ADAM_B1 ** ADAM_STEP)
    c2 = 1.0 / (1.0 - ADAM_B2 ** ADAM_STEP)
    n = len(SMALL_NAMES)

    def body(tot_ref, t_ref, wg_ref, *refs):
        w_r, m_r, v_r = refs[0:n], refs[n:2 * n], refs[2 * n:3 * n]
        g_o, d_o, nm_o, nv_o = refs[3 * n:4 * n], refs[4 * n:5 * n], refs[5 * n:6 * n], refs[6 * n:7 * n]

        def upd(i, idx, g):
            nm = ADAM_B1 * m_r[i][idx] + (1.0 - ADAM_B1) * g
            nv = ADAM_B2 * v_r[i][idx] + (1.0 - ADAM_B2) * (g * g)
            g_o[i][idx] = g
            nm_o[i][idx] = nm
            nv_o[i][idx] = nv
            d_o[i][idx] = -ADAM_LR * ((nm * c1) / (jnp.sqrt(nv * c2) + ADAM_EPS) + ADAM_WD * w_r[i][idx])

        everything = (slice(None), slice(None))
        cc = w_r[0][...]
        sc = _sigmoid(cc)
        upd(0, everything, t_ref[0:1, :] * (sc * (1.0 + cc * (1.0 - sc))))
        for j in range(6):
            upd(1, (slice(None), slice(D * j, D * j + D)),
                tot_ref[R_ADA + j:R_ADA + j + 1, :] + tot_ref[R_ADA_C + j:R_ADA_C + j + 1, :])
        for j in range(4):
            upd(2 + j, everything, tot_ref[R_GAIN + j:R_GAIN + j + 1, :])
        upd(6, everything, tot_ref[R_SINK:R_SINK + 1, 0:N_ATT])
        upd(7, everything, tot_ref[R_BG:R_BG + 1, 0:256])
        upd(8, everything, tot_ref[R_BG:R_BG + 1, 256:512])
        upd(9, everything, tot_ref[R_GGLA:R_GGLA + 1, 0:DV])
        upd(10, (0,), wg_ref[0:GATE_RANK, :])
        upd(11, (0,), wg_ref[GATE_RANK:2 * GATE_RANK, :])

    params = [w[k] for k in SMALL_NAMES] + [m[k] for k in SMALL_NAMES] + [v[k] for k in SMALL_NAMES]
    outs = pl.pallas_call(
        body, name="small_update", grid=(1,),
        in_specs=[_full(tot.shape), _full(t_tot.shape), _full(wg_g.shape)] + [_full(p.shape) for p in params],
        out_specs=[_full(w[k].shape) for k in SMALL_NAMES] * 4,
        out_shape=[jax.ShapeDtypeStruct(w[k].shape, F32) for k in SMALL_NAMES] * 4,
        compiler_params=_cp(("arbitrary",)),
    )(tot, t_tot, wg_g, *params)
    return tuple(dict(zip(SMALL_NAMES, outs[i * n:(i + 1) * n])) for i in range(4))


def kernel(x, c, ctx, c_ctx, w_ada, b_ada, g_pre_mix, g_post_mix, g_pre_ffn, g_post_ffn, w_in, attn_sink, w_gate_fwd, b_gate_fwd, w_gate_bwd, b_gate_bwd, g_gla_norm, w_out, w_ffn_in, w_ffn_out, loss_target, m_c_ctx, m_w_ada, m_b_ada, m_g_pre_mix, m_g_post_mix, m_g_pre_ffn, m_g_post_ffn, m_w_in, m_attn_sink, m_w_gate_fwd, m_b_gate_fwd, m_w_gate_bwd, m_b_gate_bwd, m_g_gla_norm, m_w_out, m_w_ffn_in, m_w_ffn_out, v_c_ctx, v_w_ada, v_b_ada, v_g_pre_mix, v_g_post_mix, v_g_pre_ffn, v_g_post_ffn, v_w_in, v_attn_sink, v_w_gate_fwd, v_b_gate_fwd, v_w_gate_bwd, v_b_gate_bwd, v_g_gla_norm, v_w_out, v_w_ffn_in, v_w_ffn_out):
    me = 4 * lax.axis_index("x") + 2 * lax.axis_index("y") + lax.axis_index("c")
    S = x.shape[1]
    n_in = w_in.shape[2]
    n_ffi = w_ffn_in.shape[2]
    r_out = w_out.shape[1]
    r_ffo = w_ffn_out.shape[1]
    n_ada = w_ada.shape[2]

    wg_sh = jnp.concatenate([w_gate_fwd.reshape(4, 128), w_gate_bwd.reshape(4, 128)], axis=0)
    c_all3, g_all, w_all, ada_all = _entry(c, wg_sh, w_in[0].T.astype(_BF), c_ctx.reshape(1, D), w_ada[0])
    c_all = c_all3.reshape(N_DEV, D)
    wgf = g_all[:, 0:4].reshape(N_DEV, GATE_RANK, 32).transpose(1, 0, 2).reshape(GATE_RANK, 256)
    wgb = g_all[:, 4:8].reshape(N_DEV, GATE_RANK, 32).transpose(1, 0, 2).reshape(GATE_RANK, 256)
    win_p = _pad_rows_win(w_all.reshape(N_DEV * n_in, D))
    wg_bd = jnp.zeros((128, 512), F32).at[0:16, 0:256].set(wgf).at[16:32, 256:512].set(wgb).astype(_BF)
    ada_full = ada_all.transpose(1, 0, 2).reshape(16, N_DEV * n_ada) + b_ada
    ada_l = jnp.pad(lax.dynamic_slice_in_dim(ada_full, me, 1, 0).reshape(6, D), ((0, 2), (0, 0)))
    ada_c = jnp.pad(ada_full[8].reshape(6, D), ((0, 2), (0, 0)))
    gains = jnp.pad(jnp.concatenate([g_pre_mix, g_post_mix, g_pre_ffn, g_post_ffn], axis=0), ((0, 4), (0, 0)))
    sink = jnp.broadcast_to(attn_sink.reshape(8, 1), (8, 128))
    bg = jnp.concatenate([b_gate_fwd, b_gate_bwd], axis=1)
    ggla = jnp.tile(g_gla_norm, (1, N_GLA))

    r = _local_step(x[0], ctx[0], loss_target[0], ada_l, ada_c, gains, sink, win_p, wg_bd, bg, ggla,
                    w_out[0].astype(_BF), w_ffn_in[0].T.astype(_BF), w_ffn_out[0].astype(_BF))

    parts = r["parts"]
    tot = _sum_slots(parts, "sum_small_grads")
    loss = tot[R_LOSS, 0]
    d_ada_rows = parts[:, R_ADA:R_ADA + 6].reshape(N_DEV, 6 * D)
    d_ada_c = tot[R_ADA_C:R_ADA_C + 6].reshape(1, 6 * D)
    my_cols = lax.dynamic_slice_in_dim(jnp.concatenate([d_ada_rows, jnp.broadcast_to(d_ada_c, (1, 6 * D)),
                                                        jnp.zeros((7, 6 * D), F32)], axis=0), me * n_ada, n_ada, 1)
    grad_w_ada, t_part = _ada_bwd(c_all, c_ctx.reshape(1, D), w_ada[0], my_cols)
    wg_g = lax.dynamic_slice(tot, (R_WG, me * 32), (2 * GATE_RANK, 32))

    tr = lambda a: jnp.transpose(a, (0, 2, 1))
    big = {}
    t_all, = _exchange([t_part], "gather_c_ctx", True)
    t_tot = _sum_slots(t_all, "sum_c_ctx")
    for nm, w, g, m, v in [("w_ada", w_ada, grad_w_ada, m_w_ada, v_w_ada),
                           ("w_out", w_out, r["got_out"], m_w_out, v_w_out),
                           ("w_ffn_out", w_ffn_out, r["got_ffo"], m_w_ffn_out, v_w_ffn_out)]:
        big[nm] = _adamw(w, [g], m, v, "adamw_" + nm)
    big["w_ffn_in"] = tuple(tr(o) for o in _adamw(tr(w_ffn_in), [r["got_ffi"]], tr(m_w_ffn_in), tr(v_w_ffn_in),
                                                  "adamw_w_ffn_in"))
    big["w_in"] = tuple(tr(o) for o in _adamw(tr(w_in), r["got_in"], tr(m_w_in), tr(v_w_in), "adamw_w_in"))

    w_small = dict(c_ctx=c_ctx.reshape(1, D), b_ada=b_ada, g_pre_mix=g_pre_mix, g_post_mix=g_post_mix, g_pre_ffn=g_pre_ffn,
                   g_post_ffn=g_post_ffn, attn_sink=attn_sink, b_gate_fwd=b_gate_fwd, b_gate_bwd=b_gate_bwd,
                   g_gla_norm=g_gla_norm, w_gate_fwd=w_gate_fwd, w_gate_bwd=w_gate_bwd)
    m_small = dict(c_ctx=m_c_ctx.reshape(1, D), b_ada=m_b_ada, g_pre_mix=m_g_pre_mix, g_post_mix=m_g_post_mix,
                   g_pre_ffn=m_g_pre_ffn, g_post_ffn=m_g_post_ffn, attn_sink=m_attn_sink, b_gate_fwd=m_b_gate_fwd,
                   b_gate_bwd=m_b_gate_bwd, g_gla_norm=m_g_gla_norm, w_gate_fwd=m_w_gate_fwd, w_gate_bwd=m_w_gate_bwd)
    v_small = dict(c_ctx=v_c_ctx.reshape(1, D), b_ada=v_b_ada, g_pre_mix=v_g_pre_mix, g_post_mix=v_g_post_mix,
                   g_pre_ffn=v_g_pre_ffn, g_post_ffn=v_g_post_ffn, attn_sink=v_attn_sink, b_gate_fwd=v_b_gate_fwd,
                   b_gate_bwd=v_b_gate_bwd, g_gla_norm=v_g_gla_norm, w_gate_fwd=v_w_gate_fwd, w_gate_bwd=v_w_gate_bwd)
    grads_small, d_s, nm_s, nv_s = _small_update(tot, t_tot, wg_g, w_small, m_small, v_small)
    for dd in (grads_small, d_s, nm_s, nv_s):
        dd["c_ctx"] = dd["c_ctx"].reshape(D)

    order = ["c_ctx", "w_ada", "b_ada", "g_pre_mix", "g_post_mix", "g_pre_ffn", "g_post_ffn", "w_in", "attn_sink",
             "w_gate_fwd", "b_gate_fwd", "w_gate_bwd", "b_gate_bwd", "g_gla_norm", "w_out", "w_ffn_in", "w_ffn_out"]
    grads, deltas, new_m, new_v = [], [], [], []
    for nm in order:
        if nm in big:
            g_, d_, m_, v_ = big[nm]
        else:
            g_, d_, m_, v_ = grads_small[nm], d_s[nm], nm_s[nm], nv_s[nm]
        grads.append(g_)
        deltas.append(d_)
        new_m.append(m_)
        new_v.append(v_)
    return (loss, r["grad_x"][None], *grads, *deltas, *new_m, *new_v)
```

```python
import functools
import math

import numpy as np
import jax
import jax.numpy as jnp
from jax import lax
from jax.experimental import pallas as pl
from jax.experimental.pallas import tpu as pltpu

F32 = jnp.float32
_BF = jnp.bfloat16

N_DEV = 8
D = 1024
CTX = 256
HD = 64
N_ATT = 8
N_KV = 2
GRP = N_ATT // N_KV
WIN = 128
GRID_W = 64
ROPE_BASE = 10000.0
N_GLA = 8
DK = 32
DV = 64
GATE_RANK = 16
GATE_TAU = 16.0
FFN = 2816
EPS = 1e-6
NEG = -1e30
GLA_T = 128

QP = N_ATT * HD
KP = N_KV * HD
O_Q, O_K, O_V = 0, QP, QP + KP
O_GQ = O_V + KP
O_GK = O_GQ + N_GLA * DK
O_GV = O_GK + N_GLA * DK
O_GG = O_GV + N_GLA * DV
O_Z = O_GG + N_GLA * DV
NP = O_Z + 128
IN_COLS = 2336

ADAM_LR, ADAM_B1, ADAM_B2, ADAM_EPS, ADAM_WD, ADAM_STEP = 0.001, 0.9, 0.999, 1e-08, 0.01, 10

VMEM_BIG = 56 * 1024 * 1024
MESH = pl.DeviceIdType.MESH


def _cp(sem, vmem=None):
    return pltpu.CompilerParams(dimension_semantics=sem, vmem_limit_bytes=vmem)


def _full(shape):
    nd = len(shape)
    return pl.BlockSpec(shape, lambda *a: (0,) * nd)


def _rows(tile, width, off=0):
    return pl.BlockSpec((tile, width), lambda i: (i + off, 0))


def _rows_lat(tile, width):
    return pl.BlockSpec((tile, width), lambda i: (jnp.maximum(i - 1, 0), 0))


def _nt(a, b):
    return lax.dot_general(a, b, (((1,), (1,)), ((), ())), preferred_element_type=F32)


def _tn(a, b):
    return lax.dot_general(a, b, (((0,), (0,)), ((), ())), preferred_element_type=F32)


def _nn(a, b):
    return jnp.dot(a, b, preferred_element_type=F32)


def _head_mean(x, mavg):
    n = x.shape[0]
    hi = x.astype(_BF)
    lo = (x - hi.astype(F32)).astype(_BF)
    y = _nn(jnp.concatenate([hi, lo], axis=0), mavg)
    return y[0:n] + y[n:2 * n]


def _rope(t, cos, sa, sb):
    n = t.shape[1]
    reps = n // 128
    c = jnp.tile(cos, (1, reps))
    a = jnp.tile(sa, (1, reps))
    b = jnp.tile(sb, (1, reps))
    return t * c + pltpu.roll(t, n - 16, 1) * a + pltpu.roll(t, 16, 1) * b


def _unrope(t, cos, sa, sb):
    n = t.shape[1]
    reps = n // 128
    c = jnp.tile(cos, (1, reps))
    a = jnp.tile(sa, (1, reps))
    b = jnp.tile(sb, (1, reps))
    return t * c + pltpu.roll(t * a, 16, 1) + pltpu.roll(t * b, n - 16, 1)


def _sigmoid(x):
    return 1.0 / (1.0 + jnp.exp(-x))


def _inproj_fwd(x, ctx, gml, shl, gmc, shc, win, wg, bg, cos, sa, sb, shards):
    E = x.shape[0] + CTX
    TE = CTX

    def body(x_ref, c_ref, gml_ref, shl_ref, gmc_ref, shc_ref, w_ref, wg_ref, bg_ref, cos_ref, sa_ref, sb_ref,
             h_ref, q_ref, k_ref, v_ref, gq_ref, gk_ref, gv_ref, gg_ref, z_ref, la_ref):
        is_ctx = pl.program_id(0) == 0
        gm = jnp.where(is_ctx, gmc_ref[...], gml_ref[...])
        sh = jnp.where(is_ctx, shc_ref[...], shl_ref[...])
        x = jnp.where(is_ctx, c_ref[...], x_ref[...])
        r = lax.rsqrt(jnp.mean(x * x, axis=-1, keepdims=True) + EPS)
        hb = ((x * r) * gm + sh).astype(_BF)
        h_ref[...] = hb
        p = _nt(hb, w_ref[...])
        cos_t, sa_t, sb_t = cos_ref[...], sa_ref[...], sb_ref[...]
        q_ref[...] = (_rope(p[:, O_Q:O_K], cos_t, sa_t, sb_t) * (HD ** -0.5)).astype(_BF)
        k_ref[...] = _rope(p[:, O_K:O_V], cos_t, sa_t, sb_t).astype(_BF)
        v_ref[...] = p[:, O_V:O_GQ].astype(_BF)
        gq_ref[...] = p[:, O_GQ:O_GK] * (DK ** -0.5)
        gk_ref[...] = p[:, O_GK:O_GV]
        gv_ref[...] = p[:, O_GV:O_GG]
        gg_ref[...] = p[:, O_GG:O_Z]
        zb = p[:, O_Z:NP].astype(_BF)
        z_ref[...] = zb
        lg = _nn(zb, wg_ref[...]) + bg_ref[...]
        la_ref[...] = (jnp.minimum(lg, 0.0) - jnp.log(1.0 + jnp.exp(-jnp.abs(lg)))) * (1.0 / GATE_TAU)

    vec = _full((1, D))
    tab = _rows(TE, 128)
    outs = [(D, _BF), (QP, _BF), (KP, _BF), (KP, _BF), (256, F32), (256, F32), (512, F32), (512, F32),
            (128, _BF), (512, F32)]
    return _hosted_call(
        body, (x, ctx, gml, shl, gmc, shc, win, wg, bg, cos, sa, sb), shards, True,
        name="inproj_fwd", grid=(E // TE,),
        in_specs=[_rows_lat(TE, D), _full((CTX, D)), vec, vec, vec, vec, _full((NP, D)), _full((128, 512)),
                  _full((1, 512)), tab, tab, tab],
        out_specs=[_rows(TE, w) for w, _ in outs],
        out_shape=[jax.ShapeDtypeStruct((E, w), dt) for w, dt in outs],
        compiler_params=_cp(("arbitrary",), 40 * 1024 * 1024))


def _xchg_scratch(na):
    return [pltpu.SemaphoreType.DMA((na, N_DEV - 1)), pltpu.SemaphoreType.DMA((na, N_DEV - 1)),
            pltpu.SemaphoreType.DMA((na,))]


def _xchg_copies(ins, outs, send_sems, recv_sems, local_sems, gather):
    x, y, c = lax.axis_index("x"), lax.axis_index("y"), lax.axis_index("c")
    me = 4 * x + 2 * y + c
    local, sends, recvs = [], [], []
    for a in range(len(ins)):
        local.append(pltpu.make_async_copy(ins[a] if gather else ins[a].at[me], outs[a].at[me], local_sems.at[a]))
    for k in range(1, N_DEV):
        px, py, pc = x ^ (k >> 2), y ^ ((k >> 1) & 1), c ^ (k & 1)
        peer = 4 * px + 2 * py + pc
        for a in range(len(ins)):
            sems = dict(send_sem=send_sems.at[a, k - 1], recv_sem=recv_sems.at[a, k - 1], device_id_type=MESH)
            sends.append(pltpu.make_async_remote_copy(
                src_ref=ins[a] if gather else ins[a].at[peer], dst_ref=outs[a].at[me], device_id=(px, py, pc), **sems))
            recvs.append(pltpu.make_async_remote_copy(
                src_ref=ins[a] if gather else ins[a].at[me], dst_ref=outs[a].at[peer], device_id=(x, y, c), **sems))
    return local, sends, recvs


def _xchg_start(cps):
    local, sends, _ = cps
    for cp in local + sends:
        cp.start()


def _xchg_finish(cps):
    local, sends, recvs = cps
    for cp in recvs:
        cp.wait_recv()
    for cp in sends:
        cp.wait_send()
    for cp in local:
        cp.wait()


def _xchg_out_shapes(ins, gather):
    return [jax.ShapeDtypeStruct(((N_DEV,) + s.shape) if gather else s.shape, s.dtype) for s in ins]


def _hosted_call(body, args, hosted, gather, *, grid, in_specs, out_specs, out_shape, scratch_shapes=(), **kw):
    na = len(hosted)
    if na == 0:
        return pl.pallas_call(body, grid=grid, in_specs=in_specs, out_specs=out_specs, out_shape=out_shape,
                              scratch_shapes=list(scratch_shapes), **kw)(*args)
    n_in, n_out, n_scr = len(in_specs), len(out_specs), len(scratch_shapes)

    def wrapped(*refs):
        ins, h_in = refs[:n_in], refs[n_in:n_in + na]
        outs, h_out = refs[n_in + na:n_in + na + n_out], refs[n_in + na + n_out:n_in + 2 * na + n_out]
        scr = refs[n_in + 2 * na + n_out:]
        cps = _xchg_copies(h_in, h_out, *scr[n_scr:], gather=gather)
        pids = [pl.program_id(a) for a in range(len(grid))]
        first = functools.reduce(jnp.logical_and, [p == 0 for p in pids])
        last = functools.reduce(jnp.logical_and, [p == g - 1 for p, g in zip(pids, grid)])

        @pl.when(first)
        def _():
            _xchg_start(cps)

        body(*ins, *outs, *scr[:n_scr])

        @pl.when(last)
        def _():
            _xchg_finish(cps)

    anyspec = pl.BlockSpec(memory_space=pl.ANY)
    return pl.pallas_call(
        wrapped, grid=grid, in_specs=list(in_specs) + [anyspec] * na, out_specs=list(out_specs) + [anyspec] * na,
        out_shape=list(out_shape) + _xchg_out_shapes(hosted, gather),
        scratch_shapes=list(scratch_shapes) + _xchg_scratch(na), **kw)(*args, *hosted)


def _attn_specs(E):
    nb = (E - CTX) // WIN
    last = E // WIN - 1
    kc = pl.BlockSpec((CTX, KP), lambda n: (0, 0))
    kp = pl.BlockSpec((WIN, KP), lambda n: (n + 1, 0))
    kk = pl.BlockSpec((WIN, KP), lambda n: (n + 2, 0))
    kn = pl.BlockSpec((WIN, KP), lambda n: (jnp.minimum(n + 3, last), 0))
    return nb, [kc, kp, kk, kn]


def _attn_bias(nb):
    rows = np.arange(GRP * WIN)[:, None] % WIN
    cols = np.arange(CTX + 3 * WIN)[None, :]
    j = cols - CTX
    band = np.abs(j - WIN - rows) <= WIN
    out = []
    for first, last in ((True, False), (False, False), (False, True)):
        ok = (cols < CTX) | (band & ((j >= WIN) | (not first)) & ((j < 2 * WIN) | (not last)))
        out.append(np.where(ok, 0.0, NEG).astype(np.float32))
    bias = jnp.asarray(np.stack(out))
    spec = pl.BlockSpec((1, GRP * WIN, CTX + 3 * WIN),
                        lambda n: (jnp.where(n == 0, 0, jnp.where(n == nb - 1, 2, 1)), 0, 0))
    return bias, spec


def _both_halves(t, h):
    tf = t.astype(F32)
    r = pltpu.roll(tf, HD, 1)
    lo = lax.broadcasted_iota(jnp.int32, tf.shape, 1) < HD
    return (jnp.where(lo, tf, r) if h == 0 else jnp.where(lo, r, tf)).astype(t.dtype)


def _stack_heads(ref, h):
    lo = lax.broadcasted_iota(jnp.int32, (WIN, 128), 1) < HD
    parts = []
    for g in range(GRP):
        j = GRP * h + g
        t = ref[:, 128 * (j // 2):128 * (j // 2) + 128].astype(F32)
        parts.append(jnp.where(lo if j % 2 == 0 else jnp.logical_not(lo), t, 0.0))
    return jnp.concatenate(parts, axis=0)


def _unstack_pair(o, pp):
    lo = lax.broadcasted_iota(jnp.int32, (WIN, 128), 1) < HD
    return jnp.where(lo, o[WIN * 2 * pp:WIN * 2 * pp + WIN], o[WIN * (2 * pp + 1):WIN * (2 * pp + 1) + WIN])


def _attn_fwd(q, k, v, sink, shards):
    E = q.shape[0]
    S = E - CTX
    nb, kspecs = _attn_specs(E)
    na = len(shards)

    def body(q_ref, kc, kp, kk, kn, vc, vp, vk, vn, sink_ref, bias_ref, *rest):
        shard_refs, (o_ref, lse_ref, p_ref), got_refs = rest[:na], rest[na:na + 3], rest[na + 3:2 * na + 3]
        n = pl.program_id(0)
        cps = _xchg_copies(shard_refs, got_refs, *rest[2 * na + 3:], gather=True)

        @pl.when(n == 0)
        def _():
            _xchg_start(cps)

        lane = lax.broadcasted_iota(jnp.int32, (WIN, 128), 1)
        lse_t = jnp.zeros((WIN, 128), F32)
        kall = jnp.concatenate([kc[...], kp[...], kk[...], kn[...]], axis=0)
        vall = jnp.concatenate([vc[...], vp[...], vk[...], vn[...]], axis=0)
        K = [_both_halves(kall, h) for h in range(N_KV)]
        Q = [_stack_heads(q_ref, h).astype(_BF) for h in range(N_KV)]
        sk = [jnp.concatenate([jnp.broadcast_to(sink_ref[GRP * h + g:GRP * h + g + 1, 0:1], (WIN, 1))
                               for g in range(GRP)], axis=0) for h in range(N_KV)]
        s = [_nt(Q[h], K[h]) + bias_ref[0] for h in range(N_KV)]
        m = [jnp.maximum(jnp.max(s[h], axis=1, keepdims=True), sk[h]) for h in range(N_KV)]
        e = [jnp.exp(s[h] - m[h]) for h in range(N_KV)]
        den = [jnp.sum(e[h], axis=1, keepdims=True) + jnp.exp(sk[h] - m[h]) for h in range(N_KV)]
        V = [_both_halves(vall, h) for h in range(N_KV)]
        pb = [(e[h] * (1.0 / den[h])).astype(_BF) for h in range(N_KV)]
        for h in range(N_KV):
            p_ref[GRP * WIN * h:GRP * WIN * (h + 1), :] = pb[h]
        o = [_nn(pb[h], V[h]) for h in range(N_KV)]
        for h in range(N_KV):
            lse = m[h] + jnp.log(den[h])
            for g in range(GRP):
                lse_t = jnp.where(lane == GRP * h + g, lse[WIN * g:WIN * g + WIN], lse_t)
            for pp in range(GRP // 2):
                t = 2 * h + pp
                o_ref[:, 128 * t:128 * t + 128] = _unstack_pair(o[h], pp).astype(_BF)
        lse_ref[...] = lse_t

        @pl.when(n == nb - 1)
        def _():
            _xchg_finish(cps)

    qs = pl.BlockSpec((WIN, QP), lambda n: (n + 2, 0))
    anyspec = pl.BlockSpec(memory_space=pl.ANY)
    bias, bias_spec = _attn_bias(nb)
    return pl.pallas_call(
        body, name="attn_fwd", grid=(nb,),
        in_specs=[qs] + kspecs + kspecs + [_full((8, 128)), bias_spec] + [anyspec] * na,
        out_specs=[_rows(WIN, 512), _rows(WIN, 128), _rows(N_KV * GRP * WIN, CTX + 3 * WIN)] + [anyspec] * na,
        out_shape=[jax.ShapeDtypeStruct((S, 512), _BF), jax.ShapeDtypeStruct((S, 128), F32),
                   jax.ShapeDtypeStruct((nb * N_KV * GRP * WIN, CTX + 3 * WIN), _BF)]
        + _xchg_out_shapes(shards, True),
        scratch_shapes=_xchg_scratch(na),
        compiler_params=_cp(("arbitrary",)),
    )(q, k, k, k, k, v, v, v, v, sink, bias, *shards)


def _attn_bwd(q, k, v, sink, probs, lse, d_attn, slabs):
    E = q.shape[0]
    S = E - CTX
    nb, kspecs = _attn_specs(E)
    last = E // WIN - 1
    na = len(slabs)

    def body(q_ref, kc, kp, kk, kn, vc, vp, vk, vn, sink_ref, p_ref, lse_ref, do_ref, *rest):
        slab_refs, (dq_ref, dk_ref, dv_ref, ds_ref), got_refs = rest[:na], rest[na:na + 4], rest[na + 4:2 * na + 4]
        n = pl.program_id(0)
        cps = _xchg_copies(slab_refs, got_refs, *rest[2 * na + 4:], gather=False)

        @pl.when(n == 0)
        def _():
            _xchg_start(cps)
            dk_ref[...] = jnp.zeros_like(dk_ref)
            dv_ref[...] = jnp.zeros_like(dv_ref)
            ds_ref[...] = jnp.zeros_like(ds_ref)

        lane = lax.broadcasted_iota(jnp.int32, (WIN, 128), 1)
        lse_t = lse_ref[...]
        starts = [None, pl.multiple_of((n + 1) * WIN, WIN), pl.multiple_of((n + 2) * WIN, WIN),
                  pl.multiple_of(jnp.minimum(n + 3, last) * WIN, WIN)]
        kall = jnp.concatenate([kc[...], kp[...], kk[...], kn[...]], axis=0)
        vall = jnp.concatenate([vc[...], vp[...], vk[...], vn[...]], axis=0)
        for h in range(N_KV):
            hs = slice(HD * h, HD * h + HD)
            K = _both_halves(kall, h)
            V = _both_halves(vall, h)
            Q = _stack_heads(q_ref, h).astype(_BF)
            sk = jnp.concatenate([jnp.broadcast_to(sink_ref[GRP * h + g:GRP * h + g + 1, 0:1], (WIN, 1))
                                  for g in range(GRP)], axis=0)
            ls = jnp.concatenate([jnp.sum(jnp.where(lane == GRP * h + g, lse_t, 0.0), axis=1, keepdims=True)
                                  for g in range(GRP)], axis=0)
            do = _stack_heads(do_ref, h).astype(_BF)
            pb = p_ref[GRP * WIN * h:GRP * WIN * (h + 1), :]
            p = pb.astype(F32)
            dp = _nt(do, V)
            delta = jnp.sum(p * dp, axis=1, keepdims=True)
            dsc = (p * (dp - delta)).astype(_BF)
            dq = _nn(dsc, K) * (HD ** -0.5)
            for pp in range(GRP // 2):
                t = 2 * h + pp
                dq_ref[:, 128 * t:128 * t + 128] = _unstack_pair(dq, pp).astype(_BF)
            dK2 = _tn(Q, dsc)
            dV2 = _tn(do, pb)
            dK = dK2[0:HD] + dK2[HD:2 * HD]
            dV = dV2[0:HD] + dV2[HD:2 * HD]
            dk_ref[hs, 0:CTX] += dK[:, 0:CTX]
            dv_ref[hs, 0:CTX] += dV[:, 0:CTX]
            for w in range(1, 4):
                lo = CTX + WIN * (w - 1)
                dk_ref[hs, pl.ds(starts[w], WIN)] += dK[:, lo:lo + WIN]
                dv_ref[hs, pl.ds(starts[w], WIN)] += dV[:, lo:lo + WIN]
            psk = -jnp.exp(sk - ls) * delta
            for g in range(GRP):
                j = GRP * h + g
                tot = jnp.sum(psk[WIN * g:WIN * g + WIN], axis=0, keepdims=True)
                ds_ref[j:j + 1, :] += jnp.broadcast_to(tot, (1, 128))

        @pl.when(n == nb - 1)
        def _():
            _xchg_finish(cps)

    qs = pl.BlockSpec((WIN, QP), lambda n: (n + 2, 0))
    anyspec = pl.BlockSpec(memory_space=pl.ANY)
    return pl.pallas_call(
        body, name="attn_bwd", grid=(nb,),
        in_specs=[qs] + kspecs + kspecs + [_full((8, 128)), _rows(N_KV * GRP * WIN, CTX + 3 * WIN), _rows(WIN, 128),
                                            _rows(WIN, 512)] + [anyspec] * na,
        out_specs=[_rows(WIN, QP), _full((KP, E)), _full((KP, E)), _full((8, 128))] + [anyspec] * na,
        out_shape=[jax.ShapeDtypeStruct((S, QP), _BF), jax.ShapeDtypeStruct((KP, E), F32),
                   jax.ShapeDtypeStruct((KP, E), F32), jax.ShapeDtypeStruct((8, 128), F32)]
        + _xchg_out_shapes(slabs, False),
        scratch_shapes=_xchg_scratch(na),
        compiler_params=_cp(("arbitrary",), 48 * 1024 * 1024),
    )(q, k, k, k, k, v, v, v, v, sink, probs, lse, d_attn, *slabs)


def _gla_order(E, reverse, backward):
    nc = CTX // GLA_T
    n = E // GLA_T
    if not reverse:
        fwd = lambda s: s
    else:
        fwd = lambda s: jnp.where(s < nc, nc - 1 - s, n - 1 + nc - s)
    if backward:
        return lambda s: fwd(n - 1 - s)
    return fwd


def _gla_masks():
    T = GLA_T
    l128 = lax.broadcasted_iota(jnp.int32, (1, 128), 1)
    qmask = [((l128 >> 5) == j).astype(F32) for j in range(4)]
    vmask = [((l128 >> 6) == j).astype(F32) for j in range(2)]
    bd = ((lax.broadcasted_iota(jnp.int32, (512, 256), 0) >> 6)
          == (lax.broadcasted_iota(jnp.int32, (512, 256), 1) >> 5)).astype(F32)
    ri = lax.broadcasted_iota(jnp.int32, (T, 2 * T), 0)
    ci = lax.broadcasted_iota(jnp.int32, (T, 2 * T), 1) & (T - 1)
    return qmask, vmask, bd, ri, ci


def _tri_sum(tri, x):
    hi = x.astype(_BF)
    lo = (x - hi.astype(F32)).astype(_BF)
    n = x.shape[1]
    y = _nn(tri.astype(_BF), jnp.concatenate([hi, lo], axis=1))
    return y[:, 0:n] + y[:, n:2 * n]


def _gla_decays(la, reverse, ri, ci):
    T = GLA_T
    msk2 = (ri <= ci) if reverse else (ri >= ci)
    mskT2 = (ri >= ci) if reverse else (ri <= ci)
    b = _tri_sum(msk2[:, 0:T], la)
    bT = b[0:1] if reverse else b[T - 1:T]
    bm = b[T // 2:T // 2 + 1]
    return msk2, mskT2, b, bT, bm


def _pair_stack(tile, m0, m1):
    return jnp.concatenate([(tile * m0).astype(_BF), (tile * m1).astype(_BF)], axis=0)


def _gla_fwd(gq, gk, gv, la, reverse, shards=()):
    E = gq.shape[0]
    T = GLA_T
    n = E // T
    order = _gla_order(E, reverse, False)
    col = 1 if reverse else 0

    def body(gq_ref, gk_ref, gv_ref, la_ref, o_ref, st_ref, S_scr):
        @pl.when(pl.program_id(0) == 0)
        def _():
            S_scr[...] = jnp.zeros_like(S_scr)

        qmask, vmask, bd, ri, ci = _gla_masks()
        msk2, _, b, bT, bm = _gla_decays(la_ref[...], reverse, ri, ci)
        q, k, v = gq_ref[...], gk_ref[...], gv_ref[...]
        qd = (q * jnp.exp(b)).astype(_BF)
        qm = (q * jnp.exp(b - bm)).astype(_BF)
        km = k * jnp.exp(bm - b)
        kd = (k * jnp.exp(bT - b)).astype(_BF)
        ST = S_scr[...]
        comp = ST[0:DV]
        for h in range(1, N_GLA):
            comp = comp + ST[DV * h:DV * h + DV]
        st_ref[0] = comp
        inter = _nt(qd, ST.astype(_BF))
        tiles = []
        for p in range(N_GLA // 2):
            qs = slice(128 * (p // 2), 128 * (p // 2) + 128)
            vs = slice(128 * p, 128 * p + 128)
            j0 = (2 * p) % 4
            KS = _pair_stack(km[:, qs], qmask[j0], qmask[j0 + 1])
            VS = _pair_stack(v[:, vs], vmask[0], vmask[1])
            AA = jnp.where(msk2, _nt(qm[:, qs], KS), 0.0).astype(_BF)
            tiles.append(_nn(AA, VS))
        o_ref[...] = inter + jnp.concatenate(tiles, axis=1)
        S_scr[...] = ST * jnp.exp(bT) + bd * _tn(v.astype(_BF), kd)

    blk = lambda w, c=0: pl.BlockSpec((T, w), lambda s: (order(s), c))
    return _hosted_call(
        body, (gq, gk, gv, la), shards, True,
        name="gla_fwd_rev" if reverse else "gla_fwd", grid=(n,),
        in_specs=[blk(256), blk(256), blk(512), blk(256, col)],
        out_specs=[blk(512), pl.BlockSpec((1, DV, 256), lambda s: (order(s), 0, 0))],
        out_shape=[jax.ShapeDtypeStruct((E, 512), F32), jax.ShapeDtypeStruct((n, DV, 256), F32)],
        scratch_shapes=[pltpu.VMEM((512, 256), F32)],
        compiler_params=_cp(("arbitrary",)))


def _gla_bwd(gq, gk, gv, la, z, st, do, reverse, prev=None, slabs=()):
    E = gq.shape[0]
    T = GLA_T
    n = E // T
    nc = CTX // T
    order = _gla_order(E, reverse, True)
    col = 1 if reverse else 0
    np_ = 0 if prev is None else 3

    def body(gq_ref, gk_ref, gv_ref, la_ref, z_ref, st_ref, do_ref, *rest):
        prev_refs = rest[:np_]
        dq_ref, dk_ref, dv_ref, dlg_ref, gwg_ref, bsum_ref, dS_scr = rest[np_:]

        @pl.when(pl.program_id(0) == 0)
        def _():
            dS_scr[...] = jnp.zeros_like(dS_scr)
            gwg_ref[...] = jnp.zeros_like(gwg_ref)
            bsum_ref[...] = jnp.zeros_like(bsum_ref)

        is_lat = order(pl.program_id(0)) >= nc
        qmask, vmask, bd, ri, ci = _gla_masks()
        msk2, mskT2, b, bT, bm = _gla_decays(la_ref[...], reverse, ri, ci)
        q, k, v = gq_ref[...], gk_ref[...], gv_ref[...]
        do = jnp.where(is_lat, do_ref[...].astype(F32), 0.0)
        e_b, e_qm, e_km, e_kd, e_T = jnp.exp(b), jnp.exp(b - bm), jnp.exp(bm - b), jnp.exp(bT - b), jnp.exp(bT)
        qd, qm, km, kd = q * e_b, q * e_qm, k * e_km, k * e_kd
        qdb, qmb, kmb, kdb, vb, dob = (t.astype(_BF) for t in (qd, qm, km, kd, v, do))
        ST = jnp.tile(st_ref[0], (N_GLA, 1)) * bd
        dST = dS_scr[...]
        dSTb = dST.astype(_BF)
        dqd = _nn(dob, ST.astype(_BF))
        dkd = _nn(vb, dSTb)
        dv_t, dqm_t, dkm_t = [], [None, None], [None, None]
        for p in range(N_GLA // 2):
            t = p // 2
            qs = slice(128 * t, 128 * t + 128)
            vs = slice(128 * p, 128 * p + 128)
            j0 = (2 * p) % 4
            QS = _pair_stack(qm[:, qs], qmask[j0], qmask[j0 + 1])
            KS = _pair_stack(km[:, qs], qmask[j0], qmask[j0 + 1])
            VS = _pair_stack(v[:, vs], vmask[0], vmask[1])
            DS = _pair_stack(do[:, vs], vmask[0], vmask[1])
            ATT = jnp.where(mskT2, _nt(kmb[:, qs], QS), 0.0).astype(_BF)
            dAA = jnp.where(msk2, _nt(dob[:, vs], VS), 0.0).astype(_BF)
            dATT = jnp.where(mskT2, _nt(vb[:, vs], DS), 0.0).astype(_BF)
            dv_t.append(_nn(ATT, DS))
            dq_p = _nn(dAA, KS)
            dk_p = _nn(dATT, QS)
            dqm_t[t] = dq_p if dqm_t[t] is None else dqm_t[t] + dq_p
            dkm_t[t] = dk_p if dkm_t[t] is None else dkm_t[t] + dk_p
        dqm = jnp.concatenate(dqm_t, axis=1)
        dkm = jnp.concatenate(dkm_t, axis=1)
        dq = dqm * e_qm + dqd * e_b
        dk = dkm * e_km + dkd * e_kd
        dv = _nt(kdb, dSTb) + jnp.concatenate(dv_t, axis=1)
        if prev is None:
            dq_ref[...], dk_ref[...], dv_ref[...] = dq, dk, dv
        else:
            dq_ref[...] = ((dq + prev_refs[0][...]) * (DK ** -0.5)).astype(_BF)
            dk_ref[...] = (dk + prev_refs[1][...]).astype(_BF)
            dv_ref[...] = (dv + prev_refs[2][...]).astype(_BF)
        db = dqm * qm - dkm * km + dqd * qd - dkd * kd
        dbT = jnp.sum(dkd * kd, axis=0, keepdims=True) + e_T * jnp.sum(dST * ST, axis=0, keepdims=True)
        dla = _tri_sum(mskT2[:, 0:T], db) + dbT
        dlg = dla * (1.0 - jnp.exp(GATE_TAU * la_ref[...])) * (1.0 / GATE_TAU)
        bsum_ref[0:1, :] += jnp.sum(dlg, axis=0, keepdims=True)
        dlgb = dlg.astype(_BF)
        dlg_ref[...] = dlgb
        gwg_ref[...] += _tn(z_ref[...], dlgb)
        dS_scr[...] = dST * e_T + bd * _tn(dob, qdb)

    blk = lambda w, c=0: pl.BlockSpec((T, w), lambda s: (order(s), c))
    do_spec = pl.BlockSpec((T, 512), lambda s: (jnp.maximum(order(s) - nc, 0), 0))
    odt = F32 if prev is None else _BF
    return _hosted_call(
        body, (gq, gk, gv, la, z, st, do) + (() if prev is None else tuple(prev)), slabs, False,
        name="gla_bwd_rev" if reverse else "gla_bwd", grid=(n,),
        in_specs=[blk(256), blk(256), blk(512), blk(256, col), blk(128),
                  pl.BlockSpec((1, DV, 256), lambda s: (order(s), 0, 0)), do_spec]
        + ([] if prev is None else [blk(256), blk(256), blk(512)]),
        out_specs=[blk(256), blk(256), blk(512), blk(256), _full((128, 256)), _full((8, 256))],
        out_shape=[jax.ShapeDtypeStruct((E, 256), odt), jax.ShapeDtypeStruct((E, 256), odt),
                   jax.ShapeDtypeStruct((E, 512), odt), jax.ShapeDtypeStruct((E, 256), _BF),
                   jax.ShapeDtypeStruct((128, 256), F32), jax.ShapeDtypeStruct((8, 256), F32)],
        scratch_shapes=[pltpu.VMEM((512, 256), F32)],
        compiler_params=_cp(("arbitrary",)))


def _gla_out(o_f, o_b, gg, ggla, mavg):
    o = o_f + o_b
    rr = lax.rsqrt(_head_mean(o * o, mavg) + EPS)
    oh = o * rr
    sg = _sigmoid(gg)
    return oh, rr, sg


def _mix_fwd(x, attn, o_f, o_b, gg, ggla, mavg, wout, gt1, g2):
    S = x.shape[0]
    TM = 256

    def body(x_ref, a_ref, of_ref, ob_ref, gg_ref, ggla_ref, mavg_ref, w_ref, gt1_ref, g2_ref, x1_ref, mix_ref):
        gg_t = gg_ref[...]
        oh, _, sg = _gla_out(of_ref[...], ob_ref[...], gg_t, ggla_ref[...], mavg_ref[...])
        mix_ref[:, 0:512] = a_ref[...]
        mix_ref[:, 512:1024] = (oh * ggla_ref[...] * (gg_t * sg)).astype(_BF)
        y = _nn(mix_ref[...], w_ref[...])
        ry = lax.rsqrt(jnp.mean(y * y, axis=-1, keepdims=True) + EPS)
        x1_ref[...] = x_ref[...] + gt1_ref[...] * ((y * ry) * g2_ref[...])

    return pl.pallas_call(
        body, name="mix_fwd", grid=(S // TM,),
        in_specs=[_rows(TM, D), _rows(TM, 512), _rows(TM, 512, 1), _rows(TM, 512, 1), _rows(TM, 512, 1),
                  _full((1, 512)), _full((512, 512)), _full((D, D)), _full((1, D)), _full((1, D))],
        out_specs=[_rows(TM, D), _rows(TM, D)],
        out_shape=[jax.ShapeDtypeStruct((S, D), F32), jax.ShapeDtypeStruct((S, D), _BF)],
        compiler_params=_cp(("arbitrary",), 40 * 1024 * 1024),
    )(x, attn, o_f, o_b, gg, ggla, mavg, wout, gt1, g2)


def _mix_bwd(dx1, mix, o_f, o_b, gg, ggla, mavg, wout, gt1, g2, slabs):
    S = dx1.shape[0]
    TM = 256

    def body(dx_ref, mix_ref, of_ref, ob_ref, gg_ref, ggla_ref, mavg_ref, w_ref, gt1_ref, g2_ref,
             da_ref, do_ref, dgg_ref, dy_ref, sums_ref):
        @pl.when(pl.program_id(0) == 0)
        def _():
            sums_ref[...] = jnp.zeros_like(sums_ref)

        dx = dx_ref[...]
        y = _nn(mix_ref[...], w_ref[...])
        ry = lax.rsqrt(jnp.mean(y * y, axis=-1, keepdims=True) + EPS)
        yh = y * ry
        sums_ref[0:1, :] += jnp.sum(dx * yh, axis=0, keepdims=True)
        dyh = dx * (gt1_ref[...] * g2_ref[...])
        dy = (ry * (dyh - yh * jnp.mean(dyh * yh, axis=-1, keepdims=True))).astype(_BF)
        dy_ref[...] = dy
        dmix = _nt(dy, w_ref[...])
        da_ref[...] = dmix[:, 0:512].astype(_BF)
        dgla = dmix[:, 512:1024]
        gg_t = gg_ref[...]
        ggla_t = ggla_ref[...]
        oh, rr, sg = _gla_out(of_ref[...], ob_ref[...], gg_t, ggla_t, mavg_ref[...])
        dgg_ref[...] = (dgla * oh * ggla_t * (sg * (1.0 + gg_t * (1.0 - sg)))).astype(_BF)
        don = dgla * (gg_t * sg)
        sums_ref[1:2, 0:512] += jnp.sum(don * oh, axis=0, keepdims=True)
        doh = don * ggla_t
        do_ref[...] = (rr * (doh - oh * _head_mean(doh * oh, mavg_ref[...]))).astype(_BF)

    return _hosted_call(
        body, (dx1, mix, o_f, o_b, gg, ggla, mavg, wout, gt1, g2), slabs, False,
        name="mix_bwd", grid=(S // TM,),
        in_specs=[_rows(TM, D), _rows(TM, D), _rows(TM, 512, 1), _rows(TM, 512, 1), _rows(TM, 512, 1),
                  _full((1, 512)), _full((512, 512)), _full((D, D)), _full((1, D)), _full((1, D))],
        out_specs=[_rows(TM, 512), _rows(TM, 512), _rows(TM, 512), _rows(TM, D), _full((8, D))],
        out_shape=[jax.ShapeDtypeStruct((S, 512), _BF), jax.ShapeDtypeStruct((S, 512), _BF),
                   jax.ShapeDtypeStruct((S, 512), _BF), jax.ShapeDtypeStruct((S, D), _BF),
                   jax.ShapeDtypeStruct((8, D), F32)],
        compiler_params=_cp(("arbitrary",), 40 * 1024 * 1024))


def _ffn(x1, target, gm2, sh2, gt2, g4, wffi, wffo):
    S = x1.shape[0]
    TF = 256

    def body(x_ref, t_ref, gm_ref, sh_ref, gt_ref, g4_ref, wi_hbm, wo_hbm,
             dx_ref, h_ref, du_ref, act_ref, df_ref, sums_ref, loss_ref, wi, wo, sem):
        @pl.when(pl.program_id(0) == 0)
        def _():
            c1 = pltpu.make_async_copy(wi_hbm, wi, sem.at[0])
            c2 = pltpu.make_async_copy(wo_hbm, wo, sem.at[1])
            c1.start()
            c2.start()
            sums_ref[...] = jnp.zeros_like(sums_ref)
            loss_ref[...] = jnp.zeros_like(loss_ref)
            c1.wait()
            c2.wait()

        x = x_ref[...]
        gm = gm_ref[...]
        r = lax.rsqrt(jnp.mean(x * x, axis=-1, keepdims=True) + EPS)
        xh = x * r
        hb = (xh * gm + sh_ref[...]).astype(_BF)
        h_ref[...] = hb
        u = _nt(hb, wi[...])
        g = u[:, 0:FFN]
        up = u[:, FFN:2 * FFN]
        sg = _sigmoid(g)
        sl = g * sg
        ab = (sl * up).astype(_BF)
        act_ref[...] = ab
        f = _nn(ab, wo[...])
        rf = lax.rsqrt(jnp.mean(f * f, axis=-1, keepdims=True) + EPS)
        fh = f * rf
        gt, g4v = gt_ref[...], g4_ref[...]
        err = x + gt * (fh * g4v) - t_ref[...]
        loss_ref[...] += jnp.sum(err * err) * (0.5 / D)
        dout = err * (1.0 / D)
        sums_ref[2:3, :] += jnp.sum(dout * fh, axis=0, keepdims=True)
        dfh = dout * (gt * g4v)
        dfb = (rf * (dfh - fh * jnp.mean(dfh * fh, axis=-1, keepdims=True))).astype(_BF)
        df_ref[...] = dfb
        dact = _nt(dfb, wo[...])
        du_ref[:, 0:FFN] = (dact * up * (sg * (1.0 + g * (1.0 - sg)))).astype(_BF)
        du_ref[:, FFN:2 * FFN] = (dact * sl).astype(_BF)
        dh = _nn(du_ref[...], wi[...])
        sums_ref[0:1, :] += jnp.sum(dh, axis=0, keepdims=True)
        sums_ref[1:2, :] += jnp.sum(dh * xh, axis=0, keepdims=True)
        dxh = dh * gm
        dx_ref[...] = dout + r * (dxh - xh * jnp.mean(dxh * xh, axis=-1, keepdims=True))

    vec = _full((1, D))
    anyspec = pl.BlockSpec(memory_space=pl.ANY)
    return pl.pallas_call(
        body, name="ffn_fwd_bwd", grid=(S // TF,),
        in_specs=[_rows(TF, D), _rows(TF, D), vec, vec, vec, vec, anyspec, anyspec],
        out_specs=[_rows(TF, D), _rows(TF, D), _rows(TF, 2 * FFN), _rows(TF, FFN), _rows(TF, D),
                   _full((8, D)), _full((8, 128))],
        out_shape=[jax.ShapeDtypeStruct((S, D), F32), jax.ShapeDtypeStruct((S, D), _BF),
                   jax.ShapeDtypeStruct((S, 2 * FFN), _BF), jax.ShapeDtypeStruct((S, FFN), _BF),
                   jax.ShapeDtypeStruct((S, D), _BF), jax.ShapeDtypeStruct((8, D), F32),
                   jax.ShapeDtypeStruct((8, 128), F32)],
        scratch_shapes=[pltpu.VMEM((2 * FFN, D), _BF), pltpu.VMEM((FFN, D), _BF), pltpu.SemaphoreType.DMA((2,))],
        compiler_params=_cp(("arbitrary",), VMEM_BIG),
    )(x1, target, gm2, sh2, gt2, g4, wffi, wffo)


def _inproj_bwd(x, ctx, gml, gmc, win, wg, cos, sa, sb, dq, dk, dv, dgq, dgk, dgv, dgg, dlg_f, dlg_b, dx1):
    S = x.shape[0]
    E = S + CTX
    TE = CTX

    def body(x_ref, c_ref, gml_ref, gmc_ref, w_ref, wg_ref, cos_ref, sa_ref, sb_ref, dq_ref, dk_ref, dv_ref,
             gq_ref, gk_ref, gv_ref, dgg_ref, dlf, dlb, dx1_ref, dp_ref, gx_ref, sums_ref):
        i = pl.program_id(0)
        is_ctx = i == 0

        @pl.when(is_ctx)
        def _():
            sums_ref[...] = jnp.zeros_like(sums_ref)

        lat = jnp.where(is_ctx, 0.0, 1.0)
        cos_t, sa_t, sb_t = cos_ref[...], sa_ref[...], sb_ref[...]
        dp_ref[:, O_Q:O_K] = (_unrope(dq_ref[...].astype(F32), cos_t, sa_t, sb_t) * lat).astype(_BF)
        dp_ref[:, O_K:O_V] = _unrope(dk_ref[...].T, cos_t, sa_t, sb_t).astype(_BF)
        dp_ref[:, O_V:O_GQ] = dv_ref[...].T.astype(_BF)
        dp_ref[:, O_GQ:O_GK] = gq_ref[...]
        dp_ref[:, O_GK:O_GV] = gk_ref[...]
        dp_ref[:, O_GV:O_GG] = gv_ref[...]
        dp_ref[:, O_GG:O_Z] = (dgg_ref[...].astype(F32) * lat).astype(_BF)
        dlg = jnp.concatenate([dlf[...], dlb[...]], axis=1)
        dp_ref[:, O_Z:NP] = _nt(dlg, wg_ref[...]).astype(_BF)
        dh = _nn(dp_ref[...], w_ref[...])
        x = jnp.where(is_ctx, c_ref[...], x_ref[...])
        r = lax.rsqrt(jnp.mean(x * x, axis=-1, keepdims=True) + EPS)
        xh = x * r
        sdh = jnp.sum(dh, axis=0, keepdims=True)
        sdx = jnp.sum(dh * xh, axis=0, keepdims=True)
        sums_ref[0:1, :] += sdh * lat
        sums_ref[1:2, :] += sdx * lat
        sums_ref[2:3, :] += sdh * (1.0 - lat)
        sums_ref[3:4, :] += sdx * (1.0 - lat)
        dxh = dh * jnp.where(is_ctx, gmc_ref[...], gml_ref[...])
        gx_ref[...] = dx1_ref[...] + r * (dxh - xh * jnp.mean(dxh * xh, axis=-1, keepdims=True))

    vec = _full((1, D))
    tab = _rows(TE, 128)
    return pl.pallas_call(
        body, name="inproj_bwd", grid=(E // TE,),
        in_specs=[_rows_lat(TE, D), _full((CTX, D)), vec, vec, _full((NP, D)), _full((128, 512)), tab, tab, tab,
                  _rows_lat(TE, QP), pl.BlockSpec((KP, TE), lambda i: (0, i)), pl.BlockSpec((KP, TE), lambda i: (0, i)),
                  _rows(TE, 256), _rows(TE, 256), _rows(TE, 512), _rows_lat(TE, 512), _rows(TE, 256), _rows(TE, 256),
                  _rows_lat(TE, D)],
        out_specs=[_rows(TE, NP), _rows_lat(TE, D), _full((8, D))],
        out_shape=[jax.ShapeDtypeStruct((E, NP), _BF), jax.ShapeDtypeStruct((S, D), F32),
                   jax.ShapeDtypeStruct((8, D), F32)],
        compiler_params=_cp(("arbitrary",), VMEM_BIG),
    )(x, ctx, gml, gmc, win, wg, cos, sa, sb, dq, dk, dv, dgq, dgk, dgv, dgg, dlg_f, dlg_b, dx1)


def _matmul_tn(a, b, tk, tt, name, out_dtype, transpose_out=False, a_cols=None, hosted=(), gather=True):
    T, KA = a.shape
    N = b.shape[1]
    nt = T // tt
    k0 = 0
    if a_cols is not None:
        KA, k0 = tk, a_cols

    def body(a_ref, b_ref, o_ref, acc):
        t = pl.program_id(1)

        @pl.when(t == 0)
        def _():
            acc[...] = jnp.zeros_like(acc)

        acc[...] += _tn(a_ref[...], b_ref[...])

        @pl.when(t == nt - 1)
        def _():
            o_ref[...] = (acc[...].T if transpose_out else acc[...]).astype(out_dtype)

    if transpose_out:
        out_spec, out_shape = pl.BlockSpec((N, tk), lambda i, t: (0, i)), (N, KA)
    else:
        out_spec, out_shape = pl.BlockSpec((tk, N), lambda i, t: (i, 0)), (KA, N)
    res = _hosted_call(
        body, (a, b), hosted, gather, name=name, grid=(KA // tk, nt),
        in_specs=[pl.BlockSpec((tt, tk), lambda i, t: (t, i + k0)), pl.BlockSpec((tt, N), lambda i, t: (t, 0))],
        out_specs=[out_spec], out_shape=[jax.ShapeDtypeStruct(out_shape, out_dtype)],
        scratch_shapes=[pltpu.VMEM((tk, N), F32)],
        compiler_params=_cp(("arbitrary", "arbitrary"), VMEM_BIG))
    return res if hosted else res[0]


def _ada_bwd(c_all, c_ctx, w_ada, d_all):
    n = w_ada.shape[1]

    def body(c_ref, cc_ref, w_ref, d_ref, gw_ref, t_ref):
        c = jnp.concatenate([c_ref[...], jnp.broadcast_to(cc_ref[...], (8, D))], axis=0)
        db = d_ref[...].astype(_BF)
        gw_ref[0] = _tn((c * _sigmoid(c)).astype(_BF), db)
        t_ref[...] = _nt(db[8:16], w_ref[...].astype(_BF))

    return pl.pallas_call(
        body, name="ada_bwd", in_specs=[_full((8, D)), _full((1, D)), _full((D, n)), _full((16, n))],
        out_specs=[_full((1, D, n)), _full((8, D))],
        out_shape=[jax.ShapeDtypeStruct((1, D, n), F32), jax.ShapeDtypeStruct((8, D), F32)], grid=(1,),
        compiler_params=_cp(("arbitrary",)),
    )(c_all, c_ctx, w_ada, d_all)


PART_ROWS = 56
R_ADA, R_ADA_C, R_GAIN, R_SINK, R_BG, R_GGLA, R_LOSS, R_WG = 0, 6, 12, 16, 17, 18, 19, 24


def _small_grads(s_in, s_ffn, s_mix, ada_l, ada_c, gains, dsink, s_bg, g_wg, loss):
    def body(si, sf, sm, al, ac, g, ds, sbg, gwg, loss_ref, o_ref):
        o_ref[...] = jnp.zeros_like(o_ref)
        o_ref[R_LOSS:R_LOSS + 1, 0:128] = loss_ref[0:1, :]
        sub = lax.broadcasted_iota(jnp.int32, (8, 128), 0)
        lane = lax.broadcasted_iota(jnp.int32, (8, 128), 1)
        o_ref[R_SINK:R_SINK + 1, 0:128] = jnp.sum(jnp.where(sub == lane, ds[...], 0.0), axis=0, keepdims=True)
        o_ref[R_BG:R_BG + 1, 0:512] = sbg[0:1, :]
        y = sm[1:2, 0:128] + sm[1:2, 128:256] + sm[1:2, 256:384] + sm[1:2, 384:512]
        y = y + pltpu.roll(y, 64, 1)
        o_ref[R_GGLA:R_GGLA + 1, 0:128] = jnp.where(lane[0:1] < DV, y, 0.0)
        o_ref[R_WG:R_WG + 16, 0:256] = gwg[0:16, 0:256]
        o_ref[R_WG + 16:R_WG + 32, 0:256] = gwg[16:32, 256:512]
        sdh_l, sdx_l, sdh_c, sdx_c = si[0:1], si[1:2], si[2:3], si[3:4]
        sdh2, sdx2, a2 = sf[0:1], sf[1:2], sf[2:3]
        a1 = sm[0:1]
        g1, g2, g3, g4 = g[0:1], g[1:2], g[2:3], g[3:4]
        sc1, gt1, sc2, gt2 = al[1:2], al[2:3], al[4:5], al[5:6]
        sc1c = ac[1:2]
        z = jnp.zeros((1, D), F32)
        rows = [sdh_l, sdx_l * g1, a1 * g2, sdh2, sdx2 * g3, a2 * g4,
                sdh_c, sdx_c * g1, z, z, z, z,
                sdx_l * (1.0 + sc1) + sdx_c * (1.0 + sc1c), a1 * gt1, sdx2 * (1.0 + sc2), a2 * gt2]
        for r, v in enumerate(rows):
            o_ref[r:r + 1, :] = v

    v8 = _full((8, D))
    return pl.pallas_call(
        body, name="small_grads",
        in_specs=[v8] * 6 + [_full((8, 128)), _full((8, 512)), _full((128, 512)), _full((8, 128))],
        out_specs=_full((PART_ROWS, D)), out_shape=jax.ShapeDtypeStruct((PART_ROWS, D), F32), grid=(1,),
        compiler_params=_cp(("arbitrary",)),
    )(s_in, s_ffn, s_mix, ada_l, ada_c, gains, dsink, s_bg, g_wg, loss)


def _row_tile(R):
    for cand in (256, 128, 64, 32, 16):
        if R % cand == 0 and R > cand:
            return cand
    return R


def _adamw(w, gs, m, v, name):
    _, R, C = w.shape
    tr = _row_tile(R)
    c1 = 1.0 / (1.0 - ADAM_B1 ** ADAM_STEP)
    c2 = 1.0 / (1.0 - ADAM_B2 ** ADAM_STEP)
    ng = len(gs)

    def body(w_ref, *refs):
        g_refs, (m_ref, v_ref, go_ref, d_ref, nm_ref, nv_ref) = refs[:ng], refs[ng:]
        c0 = 0
        for g_ref in g_refs:
            cols = slice(c0, c0 + g_ref.shape[2])
            c0 += g_ref.shape[2]
            gg = g_ref[0].astype(F32)
            for j in range(1, g_ref.shape[0]):
                gg = gg + g_ref[j].astype(F32)
            go_ref[0, :, cols] = gg
            nm = ADAM_B1 * m_ref[0, :, cols] + (1.0 - ADAM_B1) * gg
            nv = ADAM_B2 * v_ref[0, :, cols] + (1.0 - ADAM_B2) * (gg * gg)
            nm_ref[0, :, cols] = nm
            nv_ref[0, :, cols] = nv
            d_ref[0, :, cols] = -ADAM_LR * ((nm * c1) / (jnp.sqrt(nv * c2) + ADAM_EPS) + ADAM_WD * w_ref[0, :, cols])

    spec = pl.BlockSpec((1, tr, C), lambda i: (0, i, 0))
    sds = jax.ShapeDtypeStruct((1, R, C), F32)
    g_specs = [pl.BlockSpec((g.shape[0], tr, g.shape[2]), lambda i: (0, i, 0)) for g in gs]
    return pl.pallas_call(
        body, name=name, grid=(R // tr,), in_specs=[spec] + g_specs + [spec, spec], out_specs=[spec] * 4,
        out_shape=[sds] * 4, compiler_params=_cp(("parallel",), 48 * 1024 * 1024),
    )(w, *gs, m, v)


def _sum_slots(slots, name):
    _, R, C = slots.shape
    tr = _row_tile(R)

    def body(s_ref, o_ref):
        acc = s_ref[0].astype(F32)
        for j in range(1, N_DEV):
            acc = acc + s_ref[j].astype(F32)
        o_ref[...] = acc

    return pl.pallas_call(
        body, name=name, grid=(R // tr,), in_specs=[pl.BlockSpec((N_DEV, tr, C), lambda i: (0, i, 0))],
        out_specs=_rows(tr, C), out_shape=jax.ShapeDtypeStruct((R, C), F32), compiler_params=_cp(("parallel",)),
    )(slots)


def _ag2_start(x_ref, out_ref, send_sems, recv_sems, local_sem):
    x, y, c = lax.axis_index("x"), lax.axis_index("y"), lax.axis_index("c")
    me, sibling = (x, y, c), (x, y, 1 - c)
    chips = [(1 - x, y), (x, 1 - y), (1 - x, 1 - y)]

    def rows(px, py, pc):
        return out_ref.at[4 * px + 2 * py + pc]

    def copy(k, block, to, src=None):
        return pltpu.make_async_remote_copy(
            src_ref=rows(*block) if src is None else src, dst_ref=rows(*block),
            send_sem=send_sems.at[k], recv_sem=recv_sems.at[k], device_id=to, device_id_type=MESH)

    mine = pltpu.make_async_copy(x_ref, rows(*me), local_sem)
    mine.start()
    first = [copy(0, me, sibling, src=x_ref)]
    first += [copy(1 + j, me, (*chip, c), src=x_ref) for j, chip in enumerate(chips)]
    for cp in first:
        cp.start()
    return copy, mine, first, me, sibling, chips, c


def _ag2_finish(state):
    copy, mine, first, me, sibling, chips, c = state
    passed = [copy(4 + j, (*chip, c), sibling) for j, chip in enumerate(chips)]
    for j, chip in enumerate(chips):
        copy(1 + j, (*chip, c), me).wait_recv()
        passed[j].start()
    copy(0, sibling, me).wait_recv()
    for j, chip in enumerate(chips):
        copy(4 + j, (*chip, 1 - c), me).wait_recv()
    for cp in first + passed:
        cp.wait_send()
    mine.wait()


def _exchange(arrays, name, gather):
    na = len(arrays)

    def body(*refs):
        cps = _xchg_copies(refs[:na], refs[na:2 * na], *refs[2 * na:], gather=gather)
        _xchg_start(cps)
        _xchg_finish(cps)

    anyspec = pl.BlockSpec(memory_space=pl.ANY)
    return pl.pallas_call(
        body, name=name, out_shape=_xchg_out_shapes(arrays, gather), in_specs=[anyspec] * na,
        out_specs=[anyspec] * na, scratch_shapes=_xchg_scratch(na),
    )(*arrays)


def _entry(c, wg_sh, win_sh, c_ctx, w_ada):
    n = w_ada.shape[1]

    def body(c_ref, g_ref, w_ref, cc_ref, wa_ref, call_ref, gall_ref, wall_ref, ada_ref, part,
             s_send, s_recv, s_loc, w_send, w_recv, w_loc, a_send, a_recv, a_loc):
        big = _ag2_start(w_ref, wall_ref, w_send, w_recv, w_loc)
        small = _xchg_copies([c_ref, g_ref], [call_ref, gall_ref], s_send, s_recv, s_loc, gather=True)
        _xchg_start(small)
        _xchg_finish(small)
        cs = jnp.concatenate([call_ref[:, 0, :], jnp.broadcast_to(cc_ref[...], (8, D))], axis=0)
        part[...] = _nn((cs * _sigmoid(cs)).astype(_BF), wa_ref[...].astype(_BF))
        ada = _xchg_copies([part], [ada_ref], a_send, a_recv, a_loc, gather=True)
        _xchg_start(ada)
        _xchg_finish(ada)
        _ag2_finish(big)

    vm = pl.BlockSpec(memory_space=pltpu.VMEM)
    return pl.pallas_call(
        body, name="entry_gather",
        out_shape=[jax.ShapeDtypeStruct((N_DEV,) + c.shape, F32), jax.ShapeDtypeStruct((N_DEV,) + wg_sh.shape, F32),
                   jax.ShapeDtypeStruct((N_DEV,) + win_sh.shape, win_sh.dtype),
                   jax.ShapeDtypeStruct((N_DEV, 16, n), F32)],
        in_specs=[vm] * 5, out_specs=[vm] * 4,
        scratch_shapes=[pltpu.VMEM((16, n), F32)] + _xchg_scratch(2)
        + [pltpu.SemaphoreType.DMA((7,)), pltpu.SemaphoreType.DMA((7,)), pltpu.SemaphoreType.DMA] + _xchg_scratch(1),
        compiler_params=pltpu.CompilerParams(vmem_limit_bytes=VMEM_BIG),
    )(c, wg_sh, win_sh, c_ctx, w_ada)


def _rope_tables(S):
    t = np.arange(S)
    row = (t // GRID_W).astype(np.float32)
    colp = (t % GRID_W).astype(np.float32)
    half = HD // 2
    inv = (ROPE_BASE ** (-np.arange(0, half, 2, dtype=np.float32) / half)).astype(np.float32)
    ar = row[:, None] * inv[None, :]
    ac = colp[:, None] * inv[None, :]
    ang = np.concatenate([ar, ar, ac, ac], axis=-1).astype(np.float32)
    cos = np.cos(ang).astype(np.float32)
    sin = np.sin(ang).astype(np.float32)
    lane = np.arange(HD)
    first = (lane % 32) < 16
    sa = np.where(first[None, :], -sin, 0.0)
    sb = np.where(first[None, :], 0.0, sin)

    def ext(tab, ctx_val):
        full = np.zeros((CTX + S, 128), np.float32)
        full[:CTX, :] = ctx_val
        full[CTX:, :HD] = tab
        full[CTX:, HD:] = tab
        return jnp.asarray(full)

    return ext(cos, 1.0), ext(sa, 0.0), ext(sb, 0.0)


def _pad_rows_win(wt):
    return jnp.pad(wt, ((0, NP - IN_COLS), (0, 0)))


def _unpad_rows_win(g):
    return g[0:IN_COLS]


def _local_step(x, ctx, target, ada_l, ada_c, gains, sink, win_p, wg_bd, bg, ggla, wout_sh, wffi_sh, wffo_sh):
    S = x.shape[0]
    cos, sa, sb = _rope_tables(S)
    g1, g2, g3, g4 = (gains[i:i + 1] for i in range(4))
    sh1, sc1, gt1, sh2, sc2, gt2 = (ada_l[i:i + 1] for i in range(6))
    sh1c, sc1c = ada_c[0:1], ada_c[1:2]
    gml, gmc, gm2 = g1 * (1.0 + sc1), g1 * (1.0 + sc1c), g3 * (1.0 + sc2)
    mavg = jnp.asarray(np.kron(np.eye(N_GLA, dtype=np.float32), np.full((DV, DV), 1.0 / DV, np.float32))).astype(_BF)

    n_ffi, r_ffo, r_out = wffi_sh.shape[0], wffo_sh.shape[0], wout_sh.shape[0]
    tt_e = 768 if (S + CTX) % 768 == 0 else 256
    tt_s = 512 if S % 512 == 0 else 256
    h, q, k, v, gq, gk, gv, gg, z, la, wout_g = _inproj_fwd(x, ctx, gml, sh1, gmc, sh1c, win_p, wg_bd, bg,
                                                            cos, sa, sb, [wout_sh])
    attn, lse, probs, wffi_g = _attn_fwd(q, k, v, sink, [wffi_sh])
    o_f, st_f, wffo_g = _gla_fwd(gq, gk, gv, la, False, [wffo_sh])
    o_b, st_b = _gla_fwd(gq, gk, gv, la, True)
    wout = wout_g.reshape(N_DEV * r_out, D)
    wffi = wffi_g.reshape(N_DEV * n_ffi, D)
    wffo = wffo_g.reshape(N_DEV * r_ffo, D)
    x1, mix = _mix_fwd(x, attn, o_f, o_b, gg, ggla, mavg, wout, gt1, g2)
    dx1, h2, du, act, df, s_ffn, loss = _ffn(x1, target, gm2, sh2, gt2, g4, wffi, wffo)
    slab_ffi = _matmul_tn(h2, du, 512, tt_s, "grad_w_ffn_in", _BF, True).reshape(N_DEV, n_ffi, D)
    slab_ffo = _matmul_tn(act, df, FFN, tt_s, "grad_w_ffn_out", _BF).reshape(N_DEV, r_ffo, D)
    d_attn, do_gla, dgg, dy, s_mix, got_ffo = _mix_bwd(dx1, mix, o_f, o_b, gg, ggla, mavg, wout, gt1, g2, [slab_ffo])
    slab_out = _matmul_tn(mix, dy, D, tt_s, "grad_w_out", _BF).reshape(N_DEV, r_out, D)
    dq, dk, dv, dsink, got_ffi = _attn_bwd(q, k, v, sink, probs, lse, d_attn, [slab_ffi])
    pq, pk, pv, dlg_f, gwg_f, sbg_f, got_out = _gla_bwd(gq, gk, gv, la, z, st_f, do_gla, False, None, [slab_out])
    dgq, dgk, dgv, dlg_b, gwg_b, sbg_b = _gla_bwd(gq, gk, gv, la, z, st_b, do_gla, True, (pq, pk, pv))
    dp, grad_x, s_in = _inproj_bwd(x, ctx, gml, gmc, win_p, wg_bd, cos, sa, sb, dq, dk, dv,
                                   dgq, dgk, dgv, dgg, dlg_f, dlg_b, dx1)
    g_wg = jnp.concatenate([gwg_f, gwg_b], axis=1)
    s_bg = jnp.concatenate([sbg_f, sbg_b], axis=1)
    small = _small_grads(s_in, s_ffn, s_mix, ada_l, ada_c, gains, dsink, s_bg, g_wg, loss)
    n_in, half = IN_COLS // N_DEV, D // 2
    g_a, parts = _matmul_tn(h, dp, half, tt_e, "grad_w_in_a", _BF, True, a_cols=0, hosted=[small])
    slab_a = _unpad_rows_win(g_a).reshape(N_DEV, n_in, half)
    g_b, got_a = _matmul_tn(h, dp, half, tt_e, "grad_w_in_b", _BF, True, a_cols=1, hosted=[slab_a], gather=False)
    got_b, = _exchange([_unpad_rows_win(g_b).reshape(N_DEV, n_in, half)], "scatter_grads", False)
    return dict(grad_x=grad_x, got_in=[got_a, got_b], got_out=got_out, got_ffi=got_ffi, got_ffo=got_ffo, parts=parts)


SMALL_NAMES = ["c_ctx", "b_ada", "g_pre_mix", "g_post_mix", "g_pre_ffn", "g_post_ffn", "attn_sink",
               "b_gate_fwd", "b_gate_bwd", "g_gla_norm", "w_gate_fwd", "w_gate_bwd"]


def _small_update(tot, t_tot, wg_g, w, m, v):
    c1 = 1.0 / (1.0 - ADAM_B1 ** ADAM_STEP)
    c2 = 1.0 / (1.0 - ADAM_B2 ** ADAM_STEP)
    n = len(SMALL_NAMES)

    def body(tot_ref, t_ref, wg_ref, *refs):
        w_r, m_r, v_r = refs[0:n], refs[n:2 * n], refs[2 * n:3 * n]
        g_o, d_o, nm_o, nv_o = refs[3 * n:4 * n], refs[4 * n:5 * n], refs[5 * n:6 * n], refs[6 * n:7 * n]

        def upd(i, idx, g):
            nm = ADAM_B1 * m_r[i][idx] + (1.0 - ADAM_B1) * g
            nv = ADAM_B2 * v_r[i][idx] + (1.0 - ADAM_B2) * (g * g)
            g_o[i][idx] = g
            nm_o[i][idx] = nm
            nv_o[i][idx] = nv
            d_o[i][idx] = -ADAM_LR * ((nm * c1) / (jnp.sqrt(nv * c2) + ADAM_EPS) + ADAM_WD * w_r[i][idx])

        everything = (slice(None), slice(None))
        cc = w_r[0][...]
        sc = _sigmoid(cc)
        upd(0, everything, t_ref[0:1, :] * (sc * (1.0 + cc * (1.0 - sc))))
        for j in range(6):
            upd(1, (slice(None), slice(D * j, D * j + D)),
                tot_ref[R_ADA + j:R_ADA + j + 1, :] + tot_ref[R_ADA_C + j:R_ADA_C + j + 1, :])
        for j in range(4):
            upd(2 + j, everything, tot_ref[R_GAIN + j:R_GAIN + j + 1, :])
        upd(6, everything, tot_ref[R_SINK:R_SINK + 1, 0:N_ATT])
        upd(7, everything, tot_ref[R_BG:R_BG + 1, 0:256])
        upd(8, everything, tot_ref[R_BG:R_BG + 1, 256:512])
        upd(9, everything, tot_ref[R_GGLA:R_GGLA + 1, 0:DV])
        upd(10, (0,), wg_ref[0:GATE_RANK, :])
        upd(11, (0,), wg_ref[GATE_RANK:2 * GATE_RANK, :])

    params = [w[k] for k in SMALL_NAMES] + [m[k] for k in SMALL_NAMES] + [v[k] for k in SMALL_NAMES]
    outs = pl.pallas_call(
        body, name="small_update", grid=(1,),
        in_specs=[_full(tot.shape), _full(t_tot.shape), _full(wg_g.shape)] + [_full(p.shape) for p in params],
        out_specs=[_full(w[k].shape) for k in SMALL_NAMES] * 4,
        out_shape=[jax.ShapeDtypeStruct(w[k].shape, F32) for k in SMALL_NAMES] * 4,
        compiler_params=_cp(("arbitrary",)),
    )(tot, t_tot, wg_g, *params)
    return tuple(dict(zip(SMALL_NAMES, outs[i * n:(i + 1) * n])) for i in range(4))


def kernel(x, c, ctx, c_ctx, w_ada, b_ada, g_pre_mix, g_post_mix, g_pre_ffn, g_post_ffn, w_in, attn_sink, w_gate_fwd, b_gate_fwd, w_gate_bwd, b_gate_bwd, g_gla_norm, w_out, w_ffn_in, w_ffn_out, loss_target, m_c_ctx, m_w_ada, m_b_ada, m_g_pre_mix, m_g_post_mix, m_g_pre_ffn, m_g_post_ffn, m_w_in, m_attn_sink, m_w_gate_fwd, m_b_gate_fwd, m_w_gate_bwd, m_b_gate_bwd, m_g_gla_norm, m_w_out, m_w_ffn_in, m_w_ffn_out, v_c_ctx, v_w_ada, v_b_ada, v_g_pre_mix, v_g_post_mix, v_g_pre_ffn, v_g_post_ffn, v_w_in, v_attn_sink, v_w_gate_fwd, v_b_gate_fwd, v_w_gate_bwd, v_b_gate_bwd, v_g_gla_norm, v_w_out, v_w_ffn_in, v_w_ffn_out):
    me = 4 * lax.axis_index("x") + 2 * lax.axis_index("y") + lax.axis_index("c")
    S = x.shape[1]
    n_in = w_in.shape[2]
    n_ffi = w_ffn_in.shape[2]
    r_out = w_out.shape[1]
    r_ffo = w_ffn_out.shape[1]
    n_ada = w_ada.shape[2]

    wg_sh = jnp.concatenate([w_gate_fwd.reshape(4, 128), w_gate_bwd.reshape(4, 128)], axis=0)
    c_all3, g_all, w_all, ada_all = _entry(c, wg_sh, w_in[0].T.astype(_BF), c_ctx.reshape(1, D), w_ada[0])
    c_all = c_all3.reshape(N_DEV, D)
    wgf = g_all[:, 0:4].reshape(N_DEV, GATE_RANK, 32).transpose(1, 0, 2).reshape(GATE_RANK, 256)
    wgb = g_all[:, 4:8].reshape(N_DEV, GATE_RANK, 32).transpose(1, 0, 2).reshape(GATE_RANK, 256)
    win_p = _pad_rows_win(w_all.reshape(N_DEV * n_in, D))
    wg_bd = jnp.zeros((128, 512), F32).at[0:16, 0:256].set(wgf).at[16:32, 256:512].set(wgb).astype(_BF)
    ada_full = ada_all.transpose(1, 0, 2).reshape(16, N_DEV * n_ada) + b_ada
    ada_l = jnp.pad(lax.dynamic_slice_in_dim(ada_full, me, 1, 0).reshape(6, D), ((0, 2), (0, 0)))
    ada_c = jnp.pad(ada_full[8].reshape(6, D), ((0, 2), (0, 0)))
    gains = jnp.pad(jnp.concatenate([g_pre_mix, g_post_mix, g_pre_ffn, g_post_ffn], axis=0), ((0, 4), (0, 0)))
    sink = jnp.broadcast_to(attn_sink.reshape(8, 1), (8, 128))
    bg = jnp.concatenate([b_gate_fwd, b_gate_bwd], axis=1)
    ggla = jnp.tile(g_gla_norm, (1, N_GLA))

    r = _local_step(x[0], ctx[0], loss_target[0], ada_l, ada_c, gains, sink, win_p, wg_bd, bg, ggla,
                    w_out[0].astype(_BF), w_ffn_in[0].T.astype(_BF), w_ffn_out[0].astype(_BF))

    parts = r["parts"]
    tot = _sum_slots(parts, "sum_small_grads")
    loss = tot[R_LOSS, 0]
    d_ada_rows = parts[:, R_ADA:R_ADA + 6].reshape(N_DEV, 6 * D)
    d_ada_c = tot[R_ADA_C:R_ADA_C + 6].reshape(1, 6 * D)
    my_cols = lax.dynamic_slice_in_dim(jnp.concatenate([d_ada_rows, jnp.broadcast_to(d_ada_c, (1, 6 * D)),
                                                        jnp.zeros((7, 6 * D), F32)], axis=0), me * n_ada, n_ada, 1)
    grad_w_ada, t_part = _ada_bwd(c_all, c_ctx.reshape(1, D), w_ada[0], my_cols)
    wg_g = lax.dynamic_slice(tot, (R_WG, me * 32), (2 * GATE_RANK, 32))

    tr = lambda a: jnp.transpose(a, (0, 2, 1))
    big = {}
    t_all, = _exchange([t_part], "gather_c_ctx", True)
    t_tot = _sum_slots(t_all, "sum_c_ctx")
    for nm, w, g, m, v in [("w_ada", w_ada, grad_w_ada, m_w_ada, v_w_ada),
                           ("w_out", w_out, r["got_out"], m_w_out, v_w_out),
                           ("w_ffn_out", w_ffn_out, r["got_ffo"], m_w_ffn_out, v_w_ffn_out)]:
        big[nm] = _adamw(w, [g], m, v, "adamw_" + nm)
    big["w_ffn_in"] = tuple(tr(o) for o in _adamw(tr(w_ffn_in), [r["got_ffi"]], tr(m_w_ffn_in), tr(v_w_ffn_in),
                                                  "adamw_w_ffn_in"))
    big["w_in"] = tuple(tr(o) for o in _adamw(tr(w_in), r["got_in"], tr(m_w_in), tr(v_w_in), "adamw_w_in"))

    w_small = dict(c_ctx=c_ctx.reshape(1, D), b_ada=b_ada, g_pre_mix=g_pre_mix, g_post_mix=g_post_mix, g_pre_ffn=g_pre_ffn,
                   g_post_ffn=g_post_ffn, attn_sink=attn_sink, b_gate_fwd=b_gate_fwd, b_gate_bwd=b_gate_bwd,
                   g_gla_norm=g_gla_norm, w_gate_fwd=w_gate_fwd, w_gate_bwd=w_gate_bwd)
    m_small = dict(c_ctx=m_c_ctx.reshape(1, D), b_ada=m_b_ada, g_pre_mix=m_g_pre_mix, g_post_mix=m_g_post_mix,
                   g_pre_ffn=m_g_pre_ffn, g_post_ffn=m_g_post_ffn, attn_sink=m_attn_sink, b_gate_fwd=m_b_gate_fwd,
                   b_gate_bwd=m_b_gate_bwd, g_gla_norm=m_g_gla_norm, w_gate_fwd=m_w_gate_fwd, w_gate_bwd=m_w_gate_bwd)
    v_small = dict(c_ctx=v_c_ctx.reshape(1, D), b_ada=v_b_ada, g_pre_mix=v_g_pre_mix, g_post_mix=v_g_post_mix,
                   g_pre_ffn=v_g_pre_ffn, g_post_ffn=v_g_post_ffn, attn_sink=v_attn_sink, b_gate_fwd=v_b_gate_fwd,
                   b_gate_bwd=v_b_gate_bwd, g_gla_norm=v_g_gla_norm, w_gate_fwd=v_w_gate_fwd, w_gate_bwd=v_w_gate_bwd)
    grads_small, d_s, nm_s, nv_s = _small_update(tot, t_tot, wg_g, w_small, m_small, v_small)
    for dd in (grads_small, d_s, nm_s, nv_s):
        dd["c_ctx"] = dd["c_ctx"].reshape(D)

    order = ["c_ctx", "w_ada", "b_ada", "g_pre_mix", "g_post_mix", "g_pre_ffn", "g_post_ffn", "w_in", "attn_sink",
             "w_gate_fwd", "b_gate_fwd", "w_gate_bwd", "b_gate_bwd", "g_gla_norm", "w_out", "w_ffn_in", "w_ffn_out"]
    grads, deltas, new_m, new_v = [], [], [], []
    for nm in order:
        if nm in big:
            g_, d_, m_, v_ = big[nm]
        else:
            g_, d_, m_, v_ = grads_small[nm], d_s[nm], nm_s[nm], nv_s[nm]
        grads.append(g_)
        deltas.append(d_)
        new_m.append(m_)
        new_v.append(v_)
    return (loss, r["grad_x"][None], *grads, *deltas, *new_m, *new_v)
```

```python
import functools
import math

import numpy as np
import jax
import jax.numpy as jnp
from jax import lax
from jax.experimental import pallas as pl
from jax.experimental.pallas import tpu as pltpu

F32 = jnp.float32
_BF = jnp.bfloat16

N_DEV = 8
D = 1024
CTX = 256
HD = 64
N_ATT = 8
N_KV = 2
GRP = N_ATT // N_KV
WIN = 128
GRID_W = 64
ROPE_BASE = 10000.0
N_GLA = 8
DK = 32
DV = 64
GATE_RANK = 16
GATE_TAU = 16.0
FFN = 2816
EPS = 1e-6
NEG = -1e30
GLA_T = 128

QP = N_ATT * HD
KP = N_KV * HD
O_Q, O_K, O_V = 0, QP, QP + KP
O_GQ = O_V + KP
O_GK = O_GQ + N_GLA * DK
O_GV = O_GK + N_GLA * DK
O_GG = O_GV + N_GLA * DV
O_Z = O_GG + N_GLA * DV
NP = O_Z + 128
IN_COLS = 2336

ADAM_LR, ADAM_B1, ADAM_B2, ADAM_EPS, ADAM_WD, ADAM_STEP = 0.001, 0.9, 0.999, 1e-08, 0.01, 10

VMEM_BIG = 56 * 1024 * 1024
MESH = pl.DeviceIdType.MESH


def _cp(sem, vmem=None):
    return pltpu.CompilerParams(dimension_semantics=sem, vmem_limit_bytes=vmem)


def _full(shape):
    nd = len(shape)
    return pl.BlockSpec(shape, lambda *a: (0,) * nd)


def _rows(tile, width, off=0):
    return pl.BlockSpec((tile, width), lambda i: (i + off, 0))


def _rows_lat(tile, width):
    return pl.BlockSpec((tile, width), lambda i: (jnp.maximum(i - 1, 0), 0))


def _nt(a, b):
    return lax.dot_general(a, b, (((1,), (1,)), ((), ())), preferred_element_type=F32)


def _tn(a, b):
    return lax.dot_general(a, b, (((0,), (0,)), ((), ())), preferred_element_type=F32)


def _nn(a, b):
    return jnp.dot(a, b, preferred_element_type=F32)


def _head_mean(x, mavg):
    n = x.shape[0]
    hi = x.astype(_BF)
    lo = (x - hi.astype(F32)).astype(_BF)
    y = _nn(jnp.concatenate([hi, lo], axis=0), mavg)
    return y[0:n] + y[n:2 * n]


def _rope(t, cos, sa, sb):
    n = t.shape[1]
    reps = n // 128
    c = jnp.tile(cos, (1, reps))
    a = jnp.tile(sa, (1, reps))
    b = jnp.tile(sb, (1, reps))
    return t * c + pltpu.roll(t, n - 16, 1) * a + pltpu.roll(t, 16, 1) * b


def _unrope(t, cos, sa, sb):
    n = t.shape[1]
    reps = n // 128
    c = jnp.tile(cos, (1, reps))
    a = jnp.tile(sa, (1, reps))
    b = jnp.tile(sb, (1, reps))
    return t * c + pltpu.roll(t * a, 16, 1) + pltpu.roll(t * b, n - 16, 1)


def _sigmoid(x):
    return 1.0 / (1.0 + jnp.exp(-x))


def _inproj_fwd(x, ctx, gml, shl, gmc, shc, win, wg, bg, cos, sa, sb, shards):
    E = x.shape[0] + CTX
    TE = CTX

    def body(x_ref, c_ref, gml_ref, shl_ref, gmc_ref, shc_ref, w_ref, wg_ref, bg_ref, cos_ref, sa_ref, sb_ref,
             h_ref, q_ref, k_ref, v_ref, gq_ref, gk_ref, gv_ref, gg_ref, z_ref, la_ref):
        is_ctx = pl.program_id(0) == 0
        gm = jnp.where(is_ctx, gmc_ref[...], gml_ref[...])
        sh = jnp.where(is_ctx, shc_ref[...], shl_ref[...])
        x = jnp.where(is_ctx, c_ref[...], x_ref[...])
        r = lax.rsqrt(jnp.mean(x * x, axis=-1, keepdims=True) + EPS)
        hb = ((x * r) * gm + sh).astype(_BF)
        h_ref[...] = hb
        p = _nt(hb, w_ref[...])
        cos_t, sa_t, sb_t = cos_ref[...], sa_ref[...], sb_ref[...]
        q_ref[...] = (_rope(p[:, O_Q:O_K], cos_t, sa_t, sb_t) * (HD ** -0.5)).astype(_BF)
        k_ref[...] = _rope(p[:, O_K:O_V], cos_t, sa_t, sb_t).astype(_BF)
        v_ref[...] = p[:, O_V:O_GQ].astype(_BF)
        gq_ref[...] = p[:, O_GQ:O_GK] * (DK ** -0.5)
        gk_ref[...] = p[:, O_GK:O_GV]
        gv_ref[...] = p[:, O_GV:O_GG]
        gg_ref[...] = p[:, O_GG:O_Z]
        zb = p[:, O_Z:NP].astype(_BF)
        z_ref[...] = zb
        lg = _nn(zb, wg_ref[...]) + bg_ref[...]
        la_ref[...] = (jnp.minimum(lg, 0.0) - jnp.log(1.0 + jnp.exp(-jnp.abs(lg)))) * (1.0 / GATE_TAU)

    vec = _full((1, D))
    tab = _rows(TE, 128)
    outs = [(D, _BF), (QP, _BF), (KP, _BF), (KP, _BF), (256, F32), (256, F32), (512, F32), (512, F32),
            (128, _BF), (512, F32)]
    return _hosted_call(
        body, (x, ctx, gml, shl, gmc, shc, win, wg, bg, cos, sa, sb), shards, True,
        name="inproj_fwd", grid=(E // TE,),
        in_specs=[_rows_lat(TE, D), _full((CTX, D)), vec, vec, vec, vec, _full((NP, D)), _full((128, 512)),
                  _full((1, 512)), tab, tab, tab],
        out_specs=[_rows(TE, w) for w, _ in outs],
        out_shape=[jax.ShapeDtypeStruct((E, w), dt) for w, dt in outs],
        compiler_params=_cp(("arbitrary",), 40 * 1024 * 1024))


def _xchg_scratch(na):
    return [pltpu.SemaphoreType.DMA((na, N_DEV - 1)), pltpu.SemaphoreType.DMA((na, N_DEV - 1)),
            pltpu.SemaphoreType.DMA((na,))]


def _xchg_copies(ins, outs, send_sems, recv_sems, local_sems, gather):
    x, y, c = lax.axis_index("x"), lax.axis_index("y"), lax.axis_index("c")
    me = 4 * x + 2 * y + c
    local, sends, recvs = [], [], []
    for a in range(len(ins)):
        local.append(pltpu.make_async_copy(ins[a] if gather else ins[a].at[me], outs[a].at[me], local_sems.at[a]))
    for k in range(1, N_DEV):
        px, py, pc = x ^ (k >> 2), y ^ ((k >> 1) & 1), c ^ (k & 1)
        peer = 4 * px + 2 * py + pc
        for a in range(len(ins)):
            sems = dict(send_sem=send_sems.at[a, k - 1], recv_sem=recv_sems.at[a, k - 1], device_id_type=MESH)
            sends.append(pltpu.make_async_remote_copy(
                src_ref=ins[a] if gather else ins[a].at[peer], dst_ref=outs[a].at[me], device_id=(px, py, pc), **sems))
            recvs.append(pltpu.make_async_remote_copy(
                src_ref=ins[a] if gather else ins[a].at[me], dst_ref=outs[a].at[peer], device_id=(x, y, c), **sems))
    return local, sends, recvs


def _xchg_start(cps):
    local, sends, _ = cps
    for cp in local + sends:
        cp.start()


def _xchg_finish(cps):
    local, sends, recvs = cps
    for cp in recvs:
        cp.wait_recv()
    for cp in sends:
        cp.wait_send()
    for cp in local:
        cp.wait()


def _xchg_out_shapes(ins, gather):
    return [jax.ShapeDtypeStruct(((N_DEV,) + s.shape) if gather else s.shape, s.dtype) for s in ins]


def _hosted_call(body, args, hosted, gather, *, grid, in_specs, out_specs, out_shape, scratch_shapes=(), **kw):
    na = len(hosted)
    if na == 0:
        return pl.pallas_call(body, grid=grid, in_specs=in_specs, out_specs=out_specs, out_shape=out_shape,
                              scratch_shapes=list(scratch_shapes), **kw)(*args)
    n_in, n_out, n_scr = len(in_specs), len(out_specs), len(scratch_shapes)

    def wrapped(*refs):
        ins, h_in = refs[:n_in], refs[n_in:n_in + na]
        outs, h_out = refs[n_in + na:n_in + na + n_out], refs[n_in + na + n_out:n_in + 2 * na + n_out]
        scr = refs[n_in + 2 * na + n_out:]
        cps = _xchg_copies(h_in, h_out, *scr[n_scr:], gather=gather)
        pids = [pl.program_id(a) for a in range(len(grid))]
        first = functools.reduce(jnp.logical_and, [p == 0 for p in pids])
        last = functools.reduce(jnp.logical_and, [p == g - 1 for p, g in zip(pids, grid)])

        @pl.when(first)
        def _():
            _xchg_start(cps)

        body(*ins, *outs, *scr[:n_scr])

        @pl.when(last)
        def _():
            _xchg_finish(cps)

    anyspec = pl.BlockSpec(memory_space=pl.ANY)
    return pl.pallas_call(
        wrapped, grid=grid, in_specs=list(in_specs) + [anyspec] * na, out_specs=list(out_specs) + [anyspec] * na,
        out_shape=list(out_shape) + _xchg_out_shapes(hosted, gather),
        scratch_shapes=list(scratch_shapes) + _xchg_scratch(na), **kw)(*args, *hosted)


def _attn_specs(E):
    nb = (E - CTX) // WIN
    last = E // WIN - 1
    kc = pl.BlockSpec((CTX, KP), lambda n: (0, 0))
    kp = pl.BlockSpec((WIN, KP), lambda n: (n + 1, 0))
    kk = pl.BlockSpec((WIN, KP), lambda n: (n + 2, 0))
    kn = pl.BlockSpec((WIN, KP), lambda n: (jnp.minimum(n + 3, last), 0))
    return nb, [kc, kp, kk, kn]


def _attn_bias(nb):
    rows = np.arange(GRP * WIN)[:, None] % WIN
    cols = np.arange(CTX + 3 * WIN)[None, :]
    j = cols - CTX
    band = np.abs(j - WIN - rows) <= WIN
    out = []
    for first, last in ((True, False), (False, False), (False, True)):
        ok = (cols < CTX) | (band & ((j >= WIN) | (not first)) & ((j < 2 * WIN) | (not last)))
        out.append(np.where(ok, 0.0, NEG).astype(np.float32))
    bias = jnp.asarray(np.stack(out))
    spec = pl.BlockSpec((1, GRP * WIN, CTX + 3 * WIN),
                        lambda n: (jnp.where(n == 0, 0, jnp.where(n == nb - 1, 2, 1)), 0, 0))
    return bias, spec


def _both_halves(t, h):
    tf = t.astype(F32)
    r = pltpu.roll(tf, HD, 1)
    lo = lax.broadcasted_iota(jnp.int32, tf.shape, 1) < HD
    return (jnp.where(lo, tf, r) if h == 0 else jnp.where(lo, r, tf)).astype(t.dtype)


def _stack_heads(ref, h):
    lo = lax.broadcasted_iota(jnp.int32, (WIN, 128), 1) < HD
    parts = []
    for g in range(GRP):
        j = GRP * h + g
        t = ref[:, 128 * (j // 2):128 * (j // 2) + 128].astype(F32)
        parts.append(jnp.where(lo if j % 2 == 0 else jnp.logical_not(lo), t, 0.0))
    return jnp.concatenate(parts, axis=0)


def _unstack_pair(o, pp):
    lo = lax.broadcasted_iota(jnp.int32, (WIN, 128), 1) < HD
    return jnp.where(lo, o[WIN * 2 * pp:WIN * 2 * pp + WIN], o[WIN * (2 * pp + 1):WIN * (2 * pp + 1) + WIN])


def _attn_fwd(q, k, v, sink, shards):
    E = q.shape[0]
    S = E - CTX
    nb, kspecs = _attn_specs(E)
    na = len(shards)

    def body(q_ref, kc, kp, kk, kn, vc, vp, vk, vn, sink_ref, bias_ref, *rest):
        shard_refs, (o_ref, lse_ref, p_ref), got_refs = rest[:na], rest[na:na + 3], rest[na + 3:2 * na + 3]
        n = pl.program_id(0)
        cps = _xchg_copies(shard_refs, got_refs, *rest[2 * na + 3:], gather=True)

        @pl.when(n == 0)
        def _():
            _xchg_start(cps)

        lane = lax.broadcasted_iota(jnp.int32, (WIN, 128), 1)
        lse_t = jnp.zeros((WIN, 128), F32)
        kall = jnp.concatenate([kc[...], kp[...], kk[...], kn[...]], axis=0)
        vall = jnp.concatenate([vc[...], vp[...], vk[...], vn[...]], axis=0)
        K = [_both_halves(kall, h) for h in range(N_KV)]
        Q = [_stack_heads(q_ref, h).astype(_BF) for h in range(N_KV)]
        sk = [jnp.concatenate([jnp.broadcast_to(sink_ref[GRP * h + g:GRP * h + g + 1, 0:1], (WIN, 1))
                               for g in range(GRP)], axis=0) for h in range(N_KV)]
        s = [_nt(Q[h], K[h]) + bias_ref[0] for h in range(N_KV)]
        m = [jnp.maximum(jnp.max(s[h], axis=1, keepdims=True), sk[h]) for h in range(N_KV)]
        e = [jnp.exp(s[h] - m[h]) for h in range(N_KV)]
        den = [jnp.sum(e[h], axis=1, keepdims=True) + jnp.exp(sk[h] - m[h]) for h in range(N_KV)]
        V = [_both_halves(vall, h) for h in range(N_KV)]
        pb = [(e[h] * (1.0 / den[h])).astype(_BF) for h in range(N_KV)]
        for h in range(N_KV):
            p_ref[GRP * WIN * h:GRP * WIN * (h + 1), :] = pb[h]
        o = [_nn(pb[h], V[h]) for h in range(N_KV)]
        for h in range(N_KV):
            lse = m[h] + jnp.log(den[h])
            for g in range(GRP):
                lse_t = jnp.where(lane == GRP * h + g, lse[WIN * g:WIN * g + WIN], lse_t)
            for pp in range(GRP // 2):
                t = 2 * h + pp
                o_ref[:, 128 * t:128 * t + 128] = _unstack_pair(o[h], pp).astype(_BF)
        lse_ref[...] = lse_t

        @pl.when(n == nb - 1)
        def _():
            _xchg_finish(cps)

    qs = pl.BlockSpec((WIN, QP), lambda n: (n + 2, 0))
    anyspec = pl.BlockSpec(memory_space=pl.ANY)
    bias, bias_spec = _attn_bias(nb)
    return pl.pallas_call(
        body, name="attn_fwd", grid=(nb,),
        in_specs=[qs] + kspecs + kspecs + [_full((8, 128)), bias_spec] + [anyspec] * na,
        out_specs=[_rows(WIN, 512), _rows(WIN, 128), _rows(N_KV * GRP * WIN, CTX + 3 * WIN)] + [anyspec] * na,
        out_shape=[jax.ShapeDtypeStruct((S, 512), _BF), jax.ShapeDtypeStruct((S, 128), F32),
                   jax.ShapeDtypeStruct((nb * N_KV * GRP * WIN, CTX + 3 * WIN), _BF)]
        + _xchg_out_shapes(shards, True),
        scratch_shapes=_xchg_scratch(na),
        compiler_params=_cp(("arbitrary",)),
    )(q, k, k, k, k, v, v, v, v, sink, bias, *shards)


def _attn_bwd(q, k, v, sink, probs, lse, d_attn, slabs):
    E = q.shape[0]
    S = E - CTX
    nb, kspecs = _attn_specs(E)
    last = E // WIN - 1
    na = len(slabs)

    def body(q_ref, kc, kp, kk, kn, vc, vp, vk, vn, sink_ref, p_ref, lse_ref, do_ref, *rest):
        slab_refs, (dq_ref, dk_ref, dv_ref, ds_ref), got_refs = rest[:na], rest[na:na + 4], rest[na + 4:2 * na + 4]
        n = pl.program_id(0)
        cps = _xchg_copies(slab_refs, got_refs, *rest[2 * na + 4:], gather=False)

        @pl.when(n == 0)
        def _():
            _xchg_start(cps)
            dk_ref[...] = jnp.zeros_like(dk_ref)
            dv_ref[...] = jnp.zeros_like(dv_ref)
            ds_ref[...] = jnp.zeros_like(ds_ref)

        lane = lax.broadcasted_iota(jnp.int32, (WIN, 128), 1)
        lse_t = lse_ref[...]
        starts = [None, pl.multiple_of((n + 1) * WIN, WIN), pl.multiple_of((n + 2) * WIN, WIN),
                  pl.multiple_of(jnp.minimum(n + 3, last) * WIN, WIN)]
        kall = jnp.concatenate([kc[...], kp[...], kk[...], kn[...]], axis=0)
        vall = jnp.concatenate([vc[...], vp[...], vk[...], vn[...]], axis=0)
        for h in range(N_KV):
            hs = slice(HD * h, HD * h + HD)
            K = _both_halves(kall, h)
            V = _both_halves(vall, h)
            Q = _stack_heads(q_ref, h).astype(_BF)
            sk = jnp.concatenate([jnp.broadcast_to(sink_ref[GRP * h + g:GRP * h + g + 1, 0:1], (WIN, 1))
                                  for g in range(GRP)], axis=0)
            ls = jnp.concatenate([jnp.sum(jnp.where(lane == GRP * h + g, lse_t, 0.0), axis=1, keepdims=True)
                                  for g in range(GRP)], axis=0)
            do = _stack_heads(do_ref, h).astype(_BF)
            pb = p_ref[GRP * WIN * h:GRP * WIN * (h + 1), :]
            p = pb.astype(F32)
            dp = _nt(do, V)
            delta = jnp.sum(p * dp, axis=1, keepdims=True)
            dsc = (p * (dp - delta)).astype(_BF)
            dq = _nn(dsc, K) * (HD ** -0.5)
            for pp in range(GRP // 2):
                t = 2 * h + pp
                dq_ref[:, 128 * t:128 * t + 128] = _unstack_pair(dq, pp).astype(_BF)
            dK2 = _tn(Q, dsc)
            dV2 = _tn(do, pb)
            dK = dK2[0:HD] + dK2[HD:2 * HD]
            dV = dV2[0:HD] + dV2[HD:2 * HD]
            dk_ref[hs, 0:CTX] += dK[:, 0:CTX]
            dv_ref[hs, 0:CTX] += dV[:, 0:CTX]
            for w in range(1, 4):
                lo = CTX + WIN * (w - 1)
                dk_ref[hs, pl.ds(starts[w], WIN)] += dK[:, lo:lo + WIN]
                dv_ref[hs, pl.ds(starts[w], WIN)] += dV[:, lo:lo + WIN]
            psk = -jnp.exp(sk - ls) * delta
            for g in range(GRP):
                j = GRP * h + g
                tot = jnp.sum(psk[WIN * g:WIN * g + WIN], axis=0, keepdims=True)
                ds_ref[j:j + 1, :] += jnp.broadcast_to(tot, (1, 128))

        @pl.when(n == nb - 1)
        def _():
            _xchg_finish(cps)

    qs = pl.BlockSpec((WIN, QP), lambda n: (n + 2, 0))
    anyspec = pl.BlockSpec(memory_space=pl.ANY)
    return pl.pallas_call(
        body, name="attn_bwd", grid=(nb,),
        in_specs=[qs] + kspecs + kspecs + [_full((8, 128)), _rows(N_KV * GRP * WIN, CTX + 3 * WIN), _rows(WIN, 128),
                                            _rows(WIN, 512)] + [anyspec] * na,
        out_specs=[_rows(WIN, QP), _full((KP, E)), _full((KP, E)), _full((8, 128))] + [anyspec] * na,
        out_shape=[jax.ShapeDtypeStruct((S, QP), _BF), jax.ShapeDtypeStruct((KP, E), F32),
                   jax.ShapeDtypeStruct((KP, E), F32), jax.ShapeDtypeStruct((8, 128), F32)]
        + _xchg_out_shapes(slabs, False),
        scratch_shapes=_xchg_scratch(na),
        compiler_params=_cp(("arbitrary",), 48 * 1024 * 1024),
    )(q, k, k, k, k, v, v, v, v, sink, probs, lse, d_attn, *slabs)


def _gla_order(E, reverse, backward):
    nc = CTX // GLA_T
    n = E // GLA_T
    if not reverse:
        fwd = lambda s: s
    else:
        fwd = lambda s: jnp.where(s < nc, nc - 1 - s, n - 1 + nc - s)
    if backward:
        return lambda s: fwd(n - 1 - s)
    return fwd


def _gla_masks():
    T = GLA_T
    l128 = lax.broadcasted_iota(jnp.int32, (1, 128), 1)
    qmask = [((l128 >> 5) == j).astype(F32) for j in range(4)]
    vmask = [((l128 >> 6) == j).astype(F32) for j in range(2)]
    bd = ((lax.broadcasted_iota(jnp.int32, (512, 256), 0) >> 6)
          == (lax.broadcasted_iota(jnp.int32, (512, 256), 1) >> 5)).astype(F32)
    ri = lax.broadcasted_iota(jnp.int32, (T, 2 * T), 0)
    ci = lax.broadcasted_iota(jnp.int32, (T, 2 * T), 1) & (T - 1)
    return qmask, vmask, bd, ri, ci


def _tri_sum(tri, x):
    hi = x.astype(_BF)
    lo = (x - hi.astype(F32)).astype(_BF)
    n = x.shape[1]
    y = _nn(tri.astype(_BF), jnp.concatenate([hi, lo], axis=1))
    return y[:, 0:n] + y[:, n:2 * n]


def _gla_decays(la, reverse, ri, ci):
    T = GLA_T
    msk2 = (ri <= ci) if reverse else (ri >= ci)
    mskT2 = (ri >= ci) if reverse else (ri <= ci)
    b = _tri_sum(msk2[:, 0:T], la)
    bT = b[0:1] if reverse else b[T - 1:T]
    bm = b[T // 2:T // 2 + 1]
    return msk2, mskT2, b, bT, bm


def _pair_stack(tile, m0, m1):
    return jnp.concatenate([(tile * m0).astype(_BF), (tile * m1).astype(_BF)], axis=0)


def _gla_fwd(gq, gk, gv, la, reverse, shards=(), other=None):
    E = gq.shape[0]
    T = GLA_T
    n = E // T
    order = _gla_order(E, reverse, False)
    col = 1 if reverse else 0

    def body(gq_ref, gk_ref, gv_ref, la_ref, *rest):
        o_ref, st_ref, S_scr = rest[-3:]

        @pl.when(pl.program_id(0) == 0)
        def _():
            S_scr[...] = jnp.zeros_like(S_scr)

        qmask, vmask, bd, ri, ci = _gla_masks()
        msk2, _, b, bT, bm = _gla_decays(la_ref[...], reverse, ri, ci)
        q, k, v = gq_ref[...], gk_ref[...], gv_ref[...]
        qd = (q * jnp.exp(b)).astype(_BF)
        qm = (q * jnp.exp(b - bm)).astype(_BF)
        km = k * jnp.exp(bm - b)
        kd = (k * jnp.exp(bT - b)).astype(_BF)
        ST = S_scr[...]
        comp = ST[0:DV]
        for h in range(1, N_GLA):
            comp = comp + ST[DV * h:DV * h + DV]
        st_ref[0] = comp
        inter = _nt(qd, ST.astype(_BF))
        tiles = []
        for p in range(N_GLA // 2):
            qs = slice(128 * (p // 2), 128 * (p // 2) + 128)
            vs = slice(128 * p, 128 * p + 128)
            j0 = (2 * p) % 4
            KS = _pair_stack(km[:, qs], qmask[j0], qmask[j0 + 1])
            VS = _pair_stack(v[:, vs], vmask[0], vmask[1])
            AA = jnp.where(msk2, _nt(qm[:, qs], KS), 0.0).astype(_BF)
            tiles.append(_nn(AA, VS))
        o = inter + jnp.concatenate(tiles, axis=1)
        o_ref[...] = o if other is None else o + rest[0][...]
        S_scr[...] = ST * jnp.exp(bT) + bd * _tn(v.astype(_BF), kd)

    blk = lambda w, c=0: pl.BlockSpec((T, w), lambda s: (order(s), c))
    return _hosted_call(
        body, (gq, gk, gv, la) + (() if other is None else (other,)), shards, True,
        name="gla_fwd_rev" if reverse else "gla_fwd", grid=(n,),
        in_specs=[blk(256), blk(256), blk(512), blk(256, col)] + ([] if other is None else [blk(512)]),
        out_specs=[blk(512), pl.BlockSpec((1, DV, 256), lambda s: (order(s), 0, 0))],
        out_shape=[jax.ShapeDtypeStruct((E, 512), F32), jax.ShapeDtypeStruct((n, DV, 256), F32)],
        scratch_shapes=[pltpu.VMEM((512, 256), F32)],
        compiler_params=_cp(("arbitrary",)))


def _gla_bwd(gq, gk, gv, la, z, st, do, reverse, prev=None, slabs=()):
    E = gq.shape[0]
    T = GLA_T
    n = E // T
    nc = CTX // T
    order = _gla_order(E, reverse, True)
    col = 1 if reverse else 0
    np_ = 0 if prev is None else 3

    def body(gq_ref, gk_ref, gv_ref, la_ref, z_ref, st_ref, do_ref, *rest):
        prev_refs = rest[:np_]
        dq_ref, dk_ref, dv_ref, dlg_ref, gwg_ref, bsum_ref, dS_scr = rest[np_:]

        @pl.when(pl.program_id(0) == 0)
        def _():
            dS_scr[...] = jnp.zeros_like(dS_scr)
            gwg_ref[...] = jnp.zeros_like(gwg_ref)
            bsum_ref[...] = jnp.zeros_like(bsum_ref)

        is_lat = order(pl.program_id(0)) >= nc
        qmask, vmask, bd, ri, ci = _gla_masks()
        msk2, mskT2, b, bT, bm = _gla_decays(la_ref[...], reverse, ri, ci)
        q, k, v = gq_ref[...], gk_ref[...], gv_ref[...]
        do = jnp.where(is_lat, do_ref[...].astype(F32), 0.0)
        e_b, e_qm, e_km, e_kd, e_T = jnp.exp(b), jnp.exp(b - bm), jnp.exp(bm - b), jnp.exp(bT - b), jnp.exp(bT)
        qd, qm, km, kd = q * e_b, q * e_qm, k * e_km, k * e_kd
        qdb, qmb, kmb, kdb, vb, dob = (t.astype(_BF) for t in (qd, qm, km, kd, v, do))
        ST = jnp.tile(st_ref[0], (N_GLA, 1)) * bd
        dST = dS_scr[...]
        dSTb = dST.astype(_BF)
        dqd = _nn(dob, ST.astype(_BF))
        dkd = _nn(vb, dSTb)
        dv_t, dqm_t, dkm_t = [], [None, None], [None, None]
        for p in range(N_GLA // 2):
            t = p // 2
            qs = slice(128 * t, 128 * t + 128)
            vs = slice(128 * p, 128 * p + 128)
            j0 = (2 * p) % 4
            QS = _pair_stack(qm[:, qs], qmask[j0], qmask[j0 + 1])
            KS = _pair_stack(km[:, qs], qmask[j0], qmask[j0 + 1])
            VS = _pair_stack(v[:, vs], vmask[0], vmask[1])
            DS = _pair_stack(do[:, vs], vmask[0], vmask[1])
            ATT = jnp.where(mskT2, _nt(kmb[:, qs], QS), 0.0).astype(_BF)
            dAA = jnp.where(msk2, _nt(dob[:, vs], VS), 0.0).astype(_BF)
            dATT = jnp.where(mskT2, _nt(vb[:, vs], DS), 0.0).astype(_BF)
            dv_t.append(_nn(ATT, DS))
            dq_p = _nn(dAA, KS)
            dk_p = _nn(dATT, QS)
            dqm_t[t] = dq_p if dqm_t[t] is None else dqm_t[t] + dq_p
            dkm_t[t] = dk_p if dkm_t[t] is None else dkm_t[t] + dk_p
        dqm = jnp.concatenate(dqm_t, axis=1)
        dkm = jnp.concatenate(dkm_t, axis=1)
        dq = dqm * e_qm + dqd * e_b
        dk = dkm * e_km + dkd * e_kd
        dv = _nt(kdb, dSTb) + jnp.concatenate(dv_t, axis=1)
        if prev is None:
            dq_ref[...], dk_ref[...], dv_ref[...] = dq, dk, dv
        else:
            dq_ref[...] = ((dq + prev_refs[0][...]) * (DK ** -0.5)).astype(_BF)
            dk_ref[...] = (dk + prev_refs[1][...]).astype(_BF)
            dv_ref[...] = (dv + prev_refs[2][...]).astype(_BF)
        db = dqm * qm - dkm * km + dqd * qd - dkd * kd
        dbT = jnp.sum(dkd * kd, axis=0, keepdims=True) + e_T * jnp.sum(dST * ST, axis=0, keepdims=True)
        dla = _tri_sum(mskT2[:, 0:T], db) + dbT
        dlg = dla * (1.0 - jnp.exp(GATE_TAU * la_ref[...])) * (1.0 / GATE_TAU)
        bsum_ref[0:1, :] += jnp.sum(dlg, axis=0, keepdims=True)
        dlgb = dlg.astype(_BF)
        dlg_ref[...] = dlgb
        gwg_ref[...] += _tn(z_ref[...], dlgb)
        dS_scr[...] = dST * e_T + bd * _tn(dob, qdb)

    blk = lambda w, c=0: pl.BlockSpec((T, w), lambda s: (order(s), c))
    do_spec = pl.BlockSpec((T, 512), lambda s: (jnp.maximum(order(s) - nc, 0), 0))
    odt = F32 if prev is None else _BF
    return _hosted_call(
        body, (gq, gk, gv, la, z, st, do) + (() if prev is None else tuple(prev)), slabs, False,
        name="gla_bwd_rev" if reverse else "gla_bwd", grid=(n,),
        in_specs=[blk(256), blk(256), blk(512), blk(256, col), blk(128),
                  pl.BlockSpec((1, DV, 256), lambda s: (order(s), 0, 0)), do_spec]
        + ([] if prev is None else [blk(256), blk(256), blk(512)]),
        out_specs=[blk(256), blk(256), blk(512), blk(256), _full((128, 256)), _full((8, 256))],
        out_shape=[jax.ShapeDtypeStruct((E, 256), odt), jax.ShapeDtypeStruct((E, 256), odt),
                   jax.ShapeDtypeStruct((E, 512), odt), jax.ShapeDtypeStruct((E, 256), _BF),
                   jax.ShapeDtypeStruct((128, 256), F32), jax.ShapeDtypeStruct((8, 256), F32)],
        scratch_shapes=[pltpu.VMEM((512, 256), F32)],
        compiler_params=_cp(("arbitrary",)))


def _gla_out(o, gg, ggla, mavg):
    rr = lax.rsqrt(_head_mean(o * o, mavg) + EPS)
    oh = o * rr
    sg = _sigmoid(gg)
    return oh, rr, sg


def _mix_fwd(x, attn, o, gg, ggla, mavg, wout, gt1, g2):
    S = x.shape[0]
    TM = 256

    def body(x_ref, a_ref, o_ref, gg_ref, ggla_ref, mavg_ref, w_ref, gt1_ref, g2_ref, x1_ref, mix_ref):
        gg_t = gg_ref[...]
        oh, _, sg = _gla_out(o_ref[...], gg_t, ggla_ref[...], mavg_ref[...])
        mix_ref[:, 0:512] = a_ref[...]
        mix_ref[:, 512:1024] = (oh * ggla_ref[...] * (gg_t * sg)).astype(_BF)
        y = _nn(mix_ref[...], w_ref[...])
        ry = lax.rsqrt(jnp.mean(y * y, axis=-1, keepdims=True) + EPS)
        x1_ref[...] = x_ref[...] + gt1_ref[...] * ((y * ry) * g2_ref[...])

    return pl.pallas_call(
        body, name="mix_fwd", grid=(S // TM,),
        in_specs=[_rows(TM, D), _rows(TM, 512), _rows(TM, 512, 1), _rows(TM, 512, 1),
                  _full((1, 512)), _full((512, 512)), _full((D, D)), _full((1, D)), _full((1, D))],
        out_specs=[_rows(TM, D), _rows(TM, D)],
        out_shape=[jax.ShapeDtypeStruct((S, D), F32), jax.ShapeDtypeStruct((S, D), _BF)],
        compiler_params=_cp(("arbitrary",), 40 * 1024 * 1024),
    )(x, attn, o, gg, ggla, mavg, wout, gt1, g2)


def _mix_bwd(dx1, mix, o, gg, ggla, mavg, wout, gt1, g2, slabs):
    S = dx1.shape[0]
    TM = 256

    def body(dx_ref, mix_ref, o_ref, gg_ref, ggla_ref, mavg_ref, w_ref, gt1_ref, g2_ref,
             da_ref, do_ref, dgg_ref, dy_ref, sums_ref):
        @pl.when(pl.program_id(0) == 0)
        def _():
            sums_ref[...] = jnp.zeros_like(sums_ref)

        dx = dx_ref[...]
        y = _nn(mix_ref[...], w_ref[...])
        ry = lax.rsqrt(jnp.mean(y * y, axis=-1, keepdims=True) + EPS)
        yh = y * ry
        sums_ref[0:1, :] += jnp.sum(dx * yh, axis=0, keepdims=True)
        dyh = dx * (gt1_ref[...] * g2_ref[...])
        dy = (ry * (dyh - yh * jnp.mean(dyh * yh, axis=-1, keepdims=True))).astype(_BF)
        dy_ref[...] = dy
        dmix = _nt(dy, w_ref[...])
        da_ref[...] = dmix[:, 0:512].astype(_BF)
        dgla = dmix[:, 512:1024]
        gg_t = gg_ref[...]
        ggla_t = ggla_ref[...]
        oh, rr, sg = _gla_out(o_ref[...], gg_t, ggla_t, mavg_ref[...])
        dgg_ref[...] = (dgla * oh * ggla_t * (sg * (1.0 + gg_t * (1.0 - sg)))).astype(_BF)
        don = dgla * (gg_t * sg)
        sums_ref[1:2, 0:512] += jnp.sum(don * oh, axis=0, keepdims=True)
        doh = don * ggla_t
        do_ref[...] = (rr * (doh - oh * _head_mean(doh * oh, mavg_ref[...]))).astype(_BF)

    return _hosted_call(
        body, (dx1, mix, o, gg, ggla, mavg, wout, gt1, g2), slabs, False,
        name="mix_bwd", grid=(S // TM,),
        in_specs=[_rows(TM, D), _rows(TM, D), _rows(TM, 512, 1), _rows(TM, 512, 1),
                  _full((1, 512)), _full((512, 512)), _full((D, D)), _full((1, D)), _full((1, D))],
        out_specs=[_rows(TM, 512), _rows(TM, 512), _rows(TM, 512), _rows(TM, D), _full((8, D))],
        out_shape=[jax.ShapeDtypeStruct((S, 512), _BF), jax.ShapeDtypeStruct((S, 512), _BF),
                   jax.ShapeDtypeStruct((S, 512), _BF), jax.ShapeDtypeStruct((S, D), _BF),
                   jax.ShapeDtypeStruct((8, D), F32)],
        compiler_params=_cp(("arbitrary",), 40 * 1024 * 1024))


def _ffn(x1, target, gm2, sh2, gt2, g4, wffi, wffo):
    S = x1.shape[0]
    TF = 256

    def body(x_ref, t_ref, gm_ref, sh_ref, gt_ref, g4_ref, wi_hbm, wo_hbm,
             dx_ref, h_ref, du_ref, act_ref, df_ref, sums_ref, loss_ref, wi, wo, sem):
        @pl.when(pl.program_id(0) == 0)
        def _():
            c1 = pltpu.make_async_copy(wi_hbm, wi, sem.at[0])
            c2 = pltpu.make_async_copy(wo_hbm, wo, sem.at[1])
            c1.start()
            c2.start()
            sums_ref[...] = jnp.zeros_like(sums_ref)
            loss_ref[...] = jnp.zeros_like(loss_ref)
            c1.wait()
            c2.wait()

        x = x_ref[...]
        gm = gm_ref[...]
        r = lax.rsqrt(jnp.mean(x * x, axis=-1, keepdims=True) + EPS)
        xh = x * r
        hb = (xh * gm + sh_ref[...]).astype(_BF)
        h_ref[...] = hb
        u = _nt(hb, wi[...])
        g = u[:, 0:FFN]
        up = u[:, FFN:2 * FFN]
        sg = _sigmoid(g)
        sl = g * sg
        ab = (sl * up).astype(_BF)
        act_ref[...] = ab
        f = _nn(ab, wo[...])
        rf = lax.rsqrt(jnp.mean(f * f, axis=-1, keepdims=True) + EPS)
        fh = f * rf
        gt, g4v = gt_ref[...], g4_ref[...]
        err = x + gt * (fh * g4v) - t_ref[...]
        loss_ref[...] += jnp.sum(err * err) * (0.5 / D)
        dout = err * (1.0 / D)
        sums_ref[2:3, :] += jnp.sum(dout * fh, axis=0, keepdims=True)
        dfh = dout * (gt * g4v)
        dfb = (rf * (dfh - fh * jnp.mean(dfh * fh, axis=-1, keepdims=True))).astype(_BF)
        df_ref[...] = dfb
        dact = _nt(dfb, wo[...])
        du_ref[:, 0:FFN] = (dact * up * (sg * (1.0 + g * (1.0 - sg)))).astype(_BF)
        du_ref[:, FFN:2 * FFN] = (dact * sl).astype(_BF)
        dh = _nn(du_ref[...], wi[...])
        sums_ref[0:1, :] += jnp.sum(dh, axis=0, keepdims=True)
        sums_ref[1:2, :] += jnp.sum(dh * xh, axis=0, keepdims=True)
        dxh = dh * gm
        dx_ref[...] = dout + r * (dxh - xh * jnp.mean(dxh * xh, axis=-1, keepdims=True))

    vec = _full((1, D))
    anyspec = pl.BlockSpec(memory_space=pl.ANY)
    return pl.pallas_call(
        body, name="ffn_fwd_bwd", grid=(S // TF,),
        in_specs=[_rows(TF, D), _rows(TF, D), vec, vec, vec, vec, anyspec, anyspec],
        out_specs=[_rows(TF, D), _rows(TF, D), _rows(TF, 2 * FFN), _rows(TF, FFN), _rows(TF, D),
                   _full((8, D)), _full((8, 128))],
        out_shape=[jax.ShapeDtypeStruct((S, D), F32), jax.ShapeDtypeStruct((S, D), _BF),
                   jax.ShapeDtypeStruct((S, 2 * FFN), _BF), jax.ShapeDtypeStruct((S, FFN), _BF),
                   jax.ShapeDtypeStruct((S, D), _BF), jax.ShapeDtypeStruct((8, D), F32),
                   jax.ShapeDtypeStruct((8, 128), F32)],
        scratch_shapes=[pltpu.VMEM((2 * FFN, D), _BF), pltpu.VMEM((FFN, D), _BF), pltpu.SemaphoreType.DMA((2,))],
        compiler_params=_cp(("arbitrary",), VMEM_BIG),
    )(x1, target, gm2, sh2, gt2, g4, wffi, wffo)


def _inproj_bwd(x, ctx, gml, gmc, win, wg, cos, sa, sb, dq, dk, dv, dgq, dgk, dgv, dgg, dlg_f, dlg_b, dx1):
    S = x.shape[0]
    E = S + CTX
    TE = CTX

    def body(x_ref, c_ref, gml_ref, gmc_ref, w_ref, wg_ref, cos_ref, sa_ref, sb_ref, dq_ref, dk_ref, dv_ref,
             gq_ref, gk_ref, gv_ref, dgg_ref, dlf, dlb, dx1_ref, dp_ref, gx_ref, sums_ref):
        i = pl.program_id(0)
        is_ctx = i == 0

        @pl.when(is_ctx)
        def _():
            sums_ref[...] = jnp.zeros_like(sums_ref)

        lat = jnp.where(is_ctx, 0.0, 1.0)
        cos_t, sa_t, sb_t = cos_ref[...], sa_ref[...], sb_ref[...]
        dp_ref[:, O_Q:O_K] = (_unrope(dq_ref[...].astype(F32), cos_t, sa_t, sb_t) * lat).astype(_BF)
        dp_ref[:, O_K:O_V] = _unrope(dk_ref[...].T, cos_t, sa_t, sb_t).astype(_BF)
        dp_ref[:, O_V:O_GQ] = dv_ref[...].T.astype(_BF)
        dp_ref[:, O_GQ:O_GK] = gq_ref[...]
        dp_ref[:, O_GK:O_GV] = gk_ref[...]
        dp_ref[:, O_GV:O_GG] = gv_ref[...]
        dp_ref[:, O_GG:O_Z] = (dgg_ref[...].astype(F32) * lat).astype(_BF)
        dlg = jnp.concatenate([dlf[...], dlb[...]], axis=1)
        dp_ref[:, O_Z:NP] = _nt(dlg, wg_ref[...]).astype(_BF)
        dh = _nn(dp_ref[...], w_ref[...])
        x = jnp.where(is_ctx, c_ref[...], x_ref[...])
        r = lax.rsqrt(jnp.mean(x * x, axis=-1, keepdims=True) + EPS)
        xh = x * r
        sdh = jnp.sum(dh, axis=0, keepdims=True)
        sdx = jnp.sum(dh * xh, axis=0, keepdims=True)
        sums_ref[0:1, :] += sdh * lat
        sums_ref[1:2, :] += sdx * lat
        sums_ref[2:3, :] += sdh * (1.0 - lat)
        sums_ref[3:4, :] += sdx * (1.0 - lat)
        dxh = dh * jnp.where(is_ctx, gmc_ref[...], gml_ref[...])
        gx_ref[...] = dx1_ref[...] + r * (dxh - xh * jnp.mean(dxh * xh, axis=-1, keepdims=True))

    vec = _full((1, D))
    tab = _rows(TE, 128)
    return pl.pallas_call(
        body, name="inproj_bwd", grid=(E // TE,),
        in_specs=[_rows_lat(TE, D), _full((CTX, D)), vec, vec, _full((NP, D)), _full((128, 512)), tab, tab, tab,
                  _rows_lat(TE, QP), pl.BlockSpec((KP, TE), lambda i: (0, i)), pl.BlockSpec((KP, TE), lambda i: (0, i)),
                  _rows(TE, 256), _rows(TE, 256), _rows(TE, 512), _rows_lat(TE, 512), _rows(TE, 256), _rows(TE, 256),
                  _rows_lat(TE, D)],
        out_specs=[_rows(TE, NP), _rows_lat(TE, D), _full((8, D))],
        out_shape=[jax.ShapeDtypeStruct((E, NP), _BF), jax.ShapeDtypeStruct((S, D), F32),
                   jax.ShapeDtypeStruct((8, D), F32)],
        compiler_params=_cp(("arbitrary",), VMEM_BIG),
    )(x, ctx, gml, gmc, win, wg, cos, sa, sb, dq, dk, dv, dgq, dgk, dgv, dgg, dlg_f, dlg_b, dx1)


def _matmul_tn(a, b, tk, tt, name, out_dtype, transpose_out=False, a_cols=None, hosted=(), gather=True):
    T, KA = a.shape
    N = b.shape[1]
    nt = T // tt
    k0 = 0
    if a_cols is not None:
        KA, k0 = tk, a_cols

    def body(a_ref, b_ref, o_ref, acc):
        t = pl.program_id(1)

        @pl.when(t == 0)
        def _():
            acc[...] = jnp.zeros_like(acc)

        acc[...] += _tn(a_ref[...], b_ref[...])

        @pl.when(t == nt - 1)
        def _():
            o_ref[...] = (acc[...].T if transpose_out else acc[...]).astype(out_dtype)

    if transpose_out:
        out_spec, out_shape = pl.BlockSpec((N, tk), lambda i, t: (0, i)), (N, KA)
    else:
        out_spec, out_shape = pl.BlockSpec((tk, N), lambda i, t: (i, 0)), (KA, N)
    res = _hosted_call(
        body, (a, b), hosted, gather, name=name, grid=(KA // tk, nt),
        in_specs=[pl.BlockSpec((tt, tk), lambda i, t: (t, i + k0)), pl.BlockSpec((tt, N), lambda i, t: (t, 0))],
        out_specs=[out_spec], out_shape=[jax.ShapeDtypeStruct(out_shape, out_dtype)],
        scratch_shapes=[pltpu.VMEM((tk, N), F32)],
        compiler_params=_cp(("arbitrary", "arbitrary"), VMEM_BIG))
    return res if hosted else res[0]


def _ada_bwd(c_all, c_ctx, w_ada, d_all):
    n = w_ada.shape[1]

    def body(c_ref, cc_ref, w_ref, d_ref, gw_ref, t_ref):
        c = jnp.concatenate([c_ref[...], jnp.broadcast_to(cc_ref[...], (8, D))], axis=0)
        db = d_ref[...].astype(_BF)
        gw_ref[0] = _tn((c * _sigmoid(c)).astype(_BF), db)
        t_ref[...] = _nt(db[8:16], w_ref[...].astype(_BF))

    return pl.pallas_call(
        body, name="ada_bwd", in_specs=[_full((8, D)), _full((1, D)), _full((D, n)), _full((16, n))],
        out_specs=[_full((1, D, n)), _full((8, D))],
        out_shape=[jax.ShapeDtypeStruct((1, D, n), F32), jax.ShapeDtypeStruct((8, D), F32)], grid=(1,),
        compiler_params=_cp(("arbitrary",)),
    )(c_all, c_ctx, w_ada, d_all)


PART_ROWS = 56
R_ADA, R_ADA_C, R_GAIN, R_SINK, R_BG, R_GGLA, R_LOSS, R_WG = 0, 6, 12, 16, 17, 18, 19, 24


def _small_grads(s_in, s_ffn, s_mix, ada_l, ada_c, gains, dsink, s_bg, g_wg, loss):
    def body(si, sf, sm, al, ac, g, ds, sbg, gwg, loss_ref, o_ref):
        o_ref[...] = jnp.zeros_like(o_ref)
        o_ref[R_LOSS:R_LOSS + 1, 0:128] = loss_ref[0:1, :]
        sub = lax.broadcasted_iota(jnp.int32, (8, 128), 0)
        lane = lax.broadcasted_iota(jnp.int32, (8, 128), 1)
        o_ref[R_SINK:R_SINK + 1, 0:128] = jnp.sum(jnp.where(sub == lane, ds[...], 0.0), axis=0, keepdims=True)
        o_ref[R_BG:R_BG + 1, 0:512] = sbg[0:1, :]
        y = sm[1:2, 0:128] + sm[1:2, 128:256] + sm[1:2, 256:384] + sm[1:2, 384:512]
        y = y + pltpu.roll(y, 64, 1)
        o_ref[R_GGLA:R_GGLA + 1, 0:128] = jnp.where(lane[0:1] < DV, y, 0.0)
        o_ref[R_WG:R_WG + 16, 0:256] = gwg[0:16, 0:256]
        o_ref[R_WG + 16:R_WG + 32, 0:256] = gwg[16:32, 256:512]
        sdh_l, sdx_l, sdh_c, sdx_c = si[0:1], si[1:2], si[2:3], si[3:4]
        sdh2, sdx2, a2 = sf[0:1], sf[1:2], sf[2:3]
        a1 = sm[0:1]
        g1, g2, g3, g4 = g[0:1], g[1:2], g[2:3], g[3:4]
        sc1, gt1, sc2, gt2 = al[1:2], al[2:3], al[4:5], al[5:6]
        sc1c = ac[1:2]
        z = jnp.zeros((1, D), F32)
        rows = [sdh_l, sdx_l * g1, a1 * g2, sdh2, sdx2 * g3, a2 * g4,
                sdh_c, sdx_c * g1, z, z, z, z,
                sdx_l * (1.0 + sc1) + sdx_c * (1.0 + sc1c), a1 * gt1, sdx2 * (1.0 + sc2), a2 * gt2]
        for r, v in enumerate(rows):
            o_ref[r:r + 1, :] = v

    v8 = _full((8, D))
    return pl.pallas_call(
        body, name="small_grads",
        in_specs=[v8] * 6 + [_full((8, 128)), _full((8, 512)), _full((128, 512)), _full((8, 128))],
        out_specs=_full((PART_ROWS, D)), out_shape=jax.ShapeDtypeStruct((PART_ROWS, D), F32), grid=(1,),
        compiler_params=_cp(("arbitrary",)),
    )(s_in, s_ffn, s_mix, ada_l, ada_c, gains, dsink, s_bg, g_wg, loss)


def _row_tile(R):
    for cand in (256, 128, 64, 32, 16):
        if R % cand == 0 and R > cand:
            return cand
    return R


def _adamw(w, gs, m, v, name):
    _, R, C = w.shape
    tr = _row_tile(R)
    c1 = 1.0 / (1.0 - ADAM_B1 ** ADAM_STEP)
    c2 = 1.0 / (1.0 - ADAM_B2 ** ADAM_STEP)
    ng = len(gs)

    def body(w_ref, *refs):
        g_refs, (m_ref, v_ref, go_ref, d_ref, nm_ref, nv_ref) = refs[:ng], refs[ng:]
        c0 = 0
        for g_ref in g_refs:
            cols = slice(c0, c0 + g_ref.shape[2])
            c0 += g_ref.shape[2]
            gg = g_ref[0].astype(F32)
            for j in range(1, g_ref.shape[0]):
                gg = gg + g_ref[j].astype(F32)
            go_ref[0, :, cols] = gg
            nm = ADAM_B1 * m_ref[0, :, cols] + (1.0 - ADAM_B1) * gg
            nv = ADAM_B2 * v_ref[0, :, cols] + (1.0 - ADAM_B2) * (gg * gg)
            nm_ref[0, :, cols] = nm
            nv_ref[0, :, cols] = nv
            d_ref[0, :, cols] = -ADAM_LR * ((nm * c1) / (jnp.sqrt(nv * c2) + ADAM_EPS) + ADAM_WD * w_ref[0, :, cols])

    spec = pl.BlockSpec((1, tr, C), lambda i: (0, i, 0))
    sds = jax.ShapeDtypeStruct((1, R, C), F32)
    g_specs = [pl.BlockSpec((g.shape[0], tr, g.shape[2]), lambda i: (0, i, 0)) for g in gs]
    return pl.pallas_call(
        body, name=name, grid=(R // tr,), in_specs=[spec] + g_specs + [spec, spec], out_specs=[spec] * 4,
        out_shape=[sds] * 4, compiler_params=_cp(("parallel",), 48 * 1024 * 1024),
    )(w, *gs, m, v)


def _sum_slots(slots, name):
    _, R, C = slots.shape
    tr = _row_tile(R)

    def body(s_ref, o_ref):
        acc = s_ref[0].astype(F32)
        for j in range(1, N_DEV):
            acc = acc + s_ref[j].astype(F32)
        o_ref[...] = acc

    return pl.pallas_call(
        body, name=name, grid=(R // tr,), in_specs=[pl.BlockSpec((N_DEV, tr, C), lambda i: (0, i, 0))],
        out_specs=_rows(tr, C), out_shape=jax.ShapeDtypeStruct((R, C), F32), compiler_params=_cp(("parallel",)),
    )(slots)


def _ag2_start(x_ref, out_ref, send_sems, recv_sems, local_sem):
    x, y, c = lax.axis_index("x"), lax.axis_index("y"), lax.axis_index("c")
    me, sibling = (x, y, c), (x, y, 1 - c)
    chips = [(1 - x, y), (x, 1 - y), (1 - x, 1 - y)]

    def rows(px, py, pc):
        return out_ref.at[4 * px + 2 * py + pc]

    def copy(k, block, to, src=None):
        return pltpu.make_async_remote_copy(
            src_ref=rows(*block) if src is None else src, dst_ref=rows(*block),
            send_sem=send_sems.at[k], recv_sem=recv_sems.at[k], device_id=to, device_id_type=MESH)

    mine = pltpu.make_async_copy(x_ref, rows(*me), local_sem)
    mine.start()
    first = [copy(0, me, sibling, src=x_ref)]
    first += [copy(1 + j, me, (*chip, c), src=x_ref) for j, chip in enumerate(chips)]
    for cp in first:
        cp.start()
    return copy, mine, first, me, sibling, chips, c


def _ag2_finish(state):
    copy, mine, first, me, sibling, chips, c = state
    passed = [copy(4 + j, (*chip, c), sibling) for j, chip in enumerate(chips)]
    for j, chip in enumerate(chips):
        copy(1 + j, (*chip, c), me).wait_recv()
        passed[j].start()
    copy(0, sibling, me).wait_recv()
    for j, chip in enumerate(chips):
        copy(4 + j, (*chip, 1 - c), me).wait_recv()
    for cp in first + passed:
        cp.wait_send()
    mine.wait()


def _exchange(arrays, name, gather):
    na = len(arrays)

    def body(*refs):
        cps = _xchg_copies(refs[:na], refs[na:2 * na], *refs[2 * na:], gather=gather)
        _xchg_start(cps)
        _xchg_finish(cps)

    anyspec = pl.BlockSpec(memory_space=pl.ANY)
    return pl.pallas_call(
        body, name=name, out_shape=_xchg_out_shapes(arrays, gather), in_specs=[anyspec] * na,
        out_specs=[anyspec] * na, scratch_shapes=_xchg_scratch(na),
    )(*arrays)


def _entry(c, wg_sh, win_sh, c_ctx, w_ada):
    n = w_ada.shape[1]

    def body(c_ref, g_ref, w_ref, cc_ref, wa_ref, call_ref, gall_ref, wall_ref, ada_ref, part,
             s_send, s_recv, s_loc, w_send, w_recv, w_loc, a_send, a_recv, a_loc):
        big = _ag2_start(w_ref, wall_ref, w_send, w_recv, w_loc)
        small = _xchg_copies([c_ref, g_ref], [call_ref, gall_ref], s_send, s_recv, s_loc, gather=True)
        _xchg_start(small)
        _xchg_finish(small)
        cs = jnp.concatenate([call_ref[:, 0, :], jnp.broadcast_to(cc_ref[...], (8, D))], axis=0)
        part[...] = _nn((cs * _sigmoid(cs)).astype(_BF), wa_ref[...].astype(_BF))
        ada = _xchg_copies([part], [ada_ref], a_send, a_recv, a_loc, gather=True)
        _xchg_start(ada)
        _xchg_finish(ada)
        _ag2_finish(big)

    vm = pl.BlockSpec(memory_space=pltpu.VMEM)
    return pl.pallas_call(
        body, name="entry_gather",
        out_shape=[jax.ShapeDtypeStruct((N_DEV,) + c.shape, F32), jax.ShapeDtypeStruct((N_DEV,) + wg_sh.shape, F32),
                   jax.ShapeDtypeStruct((N_DEV,) + win_sh.shape, win_sh.dtype),
                   jax.ShapeDtypeStruct((N_DEV, 16, n), F32)],
        in_specs=[vm] * 5, out_specs=[vm] * 4,
        scratch_shapes=[pltpu.VMEM((16, n), F32)] + _xchg_scratch(2)
        + [pltpu.SemaphoreType.DMA((7,)), pltpu.SemaphoreType.DMA((7,)), pltpu.SemaphoreType.DMA] + _xchg_scratch(1),
        compiler_params=pltpu.CompilerParams(vmem_limit_bytes=VMEM_BIG),
    )(c, wg_sh, win_sh, c_ctx, w_ada)


def _rope_tables(S):
    t = np.arange(S)
    row = (t // GRID_W).astype(np.float32)
    colp = (t % GRID_W).astype(np.float32)
    half = HD // 2
    inv = (ROPE_BASE ** (-np.arange(0, half, 2, dtype=np.float32) / half)).astype(np.float32)
    ar = row[:, None] * inv[None, :]
    ac = colp[:, None] * inv[None, :]
    ang = np.concatenate([ar, ar, ac, ac], axis=-1).astype(np.float32)
    cos = np.cos(ang).astype(np.float32)
    sin = np.sin(ang).astype(np.float32)
    lane = np.arange(HD)
    first = (lane % 32) < 16
    sa = np.where(first[None, :], -sin, 0.0)
    sb = np.where(first[None, :], 0.0, sin)

    def ext(tab, ctx_val):
        full = np.zeros((CTX + S, 128), np.float32)
        full[:CTX, :] = ctx_val
        full[CTX:, :HD] = tab
        full[CTX:, HD:] = tab
        return jnp.asarray(full)

    return ext(cos, 1.0), ext(sa, 0.0), ext(sb, 0.0)


def _pad_rows_win(wt):
    return jnp.pad(wt, ((0, NP - IN_COLS), (0, 0)))


def _unpad_rows_win(g):
    return g[0:IN_COLS]


def _local_step(x, ctx, target, ada_l, ada_c, gains, sink, win_p, wg_bd, bg, ggla, wout_sh, wffi_sh, wffo_sh):
    S = x.shape[0]
    cos, sa, sb = _rope_tables(S)
    g1, g2, g3, g4 = (gains[i:i + 1] for i in range(4))
    sh1, sc1, gt1, sh2, sc2, gt2 = (ada_l[i:i + 1] for i in range(6))
    sh1c, sc1c = ada_c[0:1], ada_c[1:2]
    gml, gmc, gm2 = g1 * (1.0 + sc1), g1 * (1.0 + sc1c), g3 * (1.0 + sc2)
    mavg = jnp.asarray(np.kron(np.eye(N_GLA, dtype=np.float32), np.full((DV, DV), 1.0 / DV, np.float32))).astype(_BF)

    n_ffi, r_ffo, r_out = wffi_sh.shape[0], wffo_sh.shape[0], wout_sh.shape[0]
    tt_e = 768 if (S + CTX) % 768 == 0 else 256
    tt_s = 512 if S % 512 == 0 else 256
    h, q, k, v, gq, gk, gv, gg, z, la, wout_g = _inproj_fwd(x, ctx, gml, sh1, gmc, sh1c, win_p, wg_bd, bg,
                                                            cos, sa, sb, [wout_sh])
    attn, lse, probs, wffi_g = _attn_fwd(q, k, v, sink, [wffi_sh])
    o_f, st_f, wffo_g = _gla_fwd(gq, gk, gv, la, False, [wffo_sh])
    o, st_b = _gla_fwd(gq, gk, gv, la, True, (), o_f)
    wout = wout_g.reshape(N_DEV * r_out, D)
    wffi = wffi_g.reshape(N_DEV * n_ffi, D)
    wffo = wffo_g.reshape(N_DEV * r_ffo, D)
    x1, mix = _mix_fwd(x, attn, o, gg, ggla, mavg, wout, gt1, g2)
    dx1, h2, du, act, df, s_ffn, loss = _ffn(x1, target, gm2, sh2, gt2, g4, wffi, wffo)
    slab_ffi = _matmul_tn(h2, du, 512, tt_s, "grad_w_ffn_in", _BF, True).reshape(N_DEV, n_ffi, D)
    slab_ffo = _matmul_tn(act, df, FFN, tt_s, "grad_w_ffn_out", _BF).reshape(N_DEV, r_ffo, D)
    d_attn, do_gla, dgg, dy, s_mix, got_ffo = _mix_bwd(dx1, mix, o, gg, ggla, mavg, wout, gt1, g2, [slab_ffo])
    slab_out = _matmul_tn(mix, dy, D, tt_s, "grad_w_out", _BF).reshape(N_DEV, r_out, D)
    dq, dk, dv, dsink, got_ffi = _attn_bwd(q, k, v, sink, probs, lse, d_attn, [slab_ffi])
    pq, pk, pv, dlg_f, gwg_f, sbg_f, got_out = _gla_bwd(gq, gk, gv, la, z, st_f, do_gla, False, None, [slab_out])
    dgq, dgk, dgv, dlg_b, gwg_b, sbg_b = _gla_bwd(gq, gk, gv, la, z, st_b, do_gla, True, (pq, pk, pv))
    dp, grad_x, s_in = _inproj_bwd(x, ctx, gml, gmc, win_p, wg_bd, cos, sa, sb, dq, dk, dv,
                                   dgq, dgk, dgv, dgg, dlg_f, dlg_b, dx1)
    g_wg = jnp.concatenate([gwg_f, gwg_b], axis=1)
    s_bg = jnp.concatenate([sbg_f, sbg_b], axis=1)
    small = _small_grads(s_in, s_ffn, s_mix, ada_l, ada_c, gains, dsink, s_bg, g_wg, loss)
    n_in, n_grp = IN_COLS // N_DEV, 4
    got_in, slab = [], None
    for j in range(n_grp):
        g_j, got = _matmul_tn(h, dp, D // n_grp, tt_e, "grad_w_in_%d" % j, _BF, True, a_cols=j,
                              hosted=[small] if j == 0 else [slab], gather=(j == 0))
        if j == 0:
            parts = got
        else:
            got_in.append(got)
        slab = _unpad_rows_win(g_j).reshape(N_DEV, n_in, D // n_grp)
    got_in.append(_exchange([slab], "scatter_grads", False)[0])
    return dict(grad_x=grad_x, got_in=got_in, got_out=got_out, got_ffi=got_ffi, got_ffo=got_ffo, parts=parts)


SMALL_NAMES = ["c_ctx", "b_ada", "g_pre_mix", "g_post_mix", "g_pre_ffn", "g_post_ffn", "attn_sink",
               "b_gate_fwd", "b_gate_bwd", "g_gla_norm", "w_gate_fwd", "w_gate_bwd"]


def _small_update(tot, t_tot, wg_g, w, m, v):
    c1 = 1.0 / (1.0 - ADAM_B1 ** ADAM_STEP)
    c2 = 1.0 / (1.0 - ADAM_B2 ** ADAM_STEP)
    n = len(SMALL_NAMES)

    def body(tot_ref, t_ref, wg_ref, *refs):
        w_r, m_r, v_r = refs[0:n], refs[n:2 * n], refs[2 * n:3 * n]
        g_o, d_o, nm_o, nv_o = refs[3 * n:4 * n], refs[4 * n:5 * n], refs[5 * n:6 * n], refs[6 * n:7 * n]

        def upd(i, idx, g):
            nm = ADAM_B1 * m_r[i][idx] + (1.0 - ADAM_B1) * g
            nv = ADAM_B2 * v_r[i][idx] + (1.0 - ADAM_B2) * (g * g)
            g_o[i][idx] = g
            nm_o[i][idx] = nm
            nv_o[i][idx] = nv
            d_o[i][idx] = -ADAM_LR * ((nm * c1) / (jnp.sqrt(nv * c2) + ADAM_EPS) + ADAM_WD * w_r[i][idx])

        everything = (slice(None), slice(None))
        cc = w_r[0][...]
        sc = _sigmoid(cc)
        upd(0, everything, t_ref[0:1, :] * (sc * (1.0 + cc * (1.0 - sc))))
        for j in range(6):
            upd(1, (slice(None), slice(D * j, D * j + D)),
                tot_ref[R_ADA + j:R_ADA + j + 1, :] + tot_ref[R_ADA_C + j:R_ADA_C + j + 1, :])
        for j in range(4):
            upd(2 + j, everything, tot_ref[R_GAIN + j:R_GAIN + j + 1, :])
        upd(6, everything, tot_ref[R_SINK:R_SINK + 1, 0:N_ATT])
        upd(7, everything, tot_ref[R_BG:R_BG + 1, 0:256])
        upd(8, everything, tot_ref[R_BG:R_BG + 1, 256:512])
        upd(9, everything, tot_ref[R_GGLA:R_GGLA + 1, 0:DV])
        upd(10, (0,), wg_ref[0:GATE_RANK, :])
        upd(11, (0,), wg_ref[GATE_RANK:2 * GATE_RANK, :])

    params = [w[k] for k in SMALL_NAMES] + [m[k] for k in SMALL_NAMES] + [v[k] for k in SMALL_NAMES]
    outs = pl.pallas_call(
        body, name="small_update", grid=(1,),
        in_specs=[_full(tot.shape), _full(t_tot.shape), _full(wg_g.shape)] + [_full(p.shape) for p in params],
        out_specs=[_full(w[k].shape) for k in SMALL_NAMES] * 4,
        out_shape=[jax.ShapeDtypeStruct(w[k].shape, F32) for k in SMALL_NAMES] * 4,
        compiler_params=_cp(("arbitrary",)),
    )(tot, t_tot, wg_g, *params)
    return tuple(dict(zip(SMALL_NAMES, outs[i * n:(i + 1) * n])) for i in range(4))


def kernel(x, c, ctx, c_ctx, w_ada, b_ada, g_pre_mix, g_post_mix, g_pre_ffn, g_post_ffn, w_in, attn_sink, w_gate_fwd, b_gate_fwd, w_gate_bwd, b_gate_bwd, g_gla_norm, w_out, w_ffn_in, w_ffn_out, loss_target, m_c_ctx, m_w_ada, m_b_ada, m_g_pre_mix, m_g_post_mix, m_g_pre_ffn, m_g_post_ffn, m_w_in, m_attn_sink, m_w_gate_fwd, m_b_gate_fwd, m_w_gate_bwd, m_b_gate_bwd, m_g_gla_norm, m_w_out, m_w_ffn_in, m_w_ffn_out, v_c_ctx, v_w_ada, v_b_ada, v_g_pre_mix, v_g_post_mix, v_g_pre_ffn, v_g_post_ffn, v_w_in, v_attn_sink, v_w_gate_fwd, v_b_gate_fwd, v_w_gate_bwd, v_b_gate_bwd, v_g_gla_norm, v_w_out, v_w_ffn_in, v_w_ffn_out):
    me = 4 * lax.axis_index("x") + 2 * lax.axis_index("y") + lax.axis_index("c")
    S = x.shape[1]
    n_in = w_in.shape[2]
    n_ffi = w_ffn_in.shape[2]
    r_out = w_out.shape[1]
    r_ffo = w_ffn_out.shape[1]
    n_ada = w_ada.shape[2]

    wg_sh = jnp.concatenate([w_gate_fwd.reshape(4, 128), w_gate_bwd.reshape(4, 128)], axis=0)
    c_all3, g_all, w_all, ada_all = _entry(c, wg_sh, w_in[0].T.astype(_BF), c_ctx.reshape(1, D), w_ada[0])
    c_all = c_all3.reshape(N_DEV, D)
    wgf = g_all[:, 0:4].reshape(N_DEV, GATE_RANK, 32).transpose(1, 0, 2).reshape(GATE_RANK, 256)
    wgb = g_all[:, 4:8].reshape(N_DEV, GATE_RANK, 32).transpose(1, 0, 2).reshape(GATE_RANK, 256)
    win_p = _pad_rows_win(w_all.reshape(N_DEV * n_in, D))
    wg_bd = jnp.zeros((128, 512), F32).at[0:16, 0:256].set(wgf).at[16:32, 256:512].set(wgb).astype(_BF)
    ada_full = ada_all.transpose(1, 0, 2).reshape(16, N_DEV * n_ada) + b_ada
    ada_l = jnp.pad(lax.dynamic_slice_in_dim(ada_full, me, 1, 0).reshape(6, D), ((0, 2), (0, 0)))
    ada_c = jnp.pad(ada_full[8].reshape(6, D), ((0, 2), (0, 0)))
    gains = jnp.pad(jnp.concatenate([g_pre_mix, g_post_mix, g_pre_ffn, g_post_ffn], axis=0), ((0, 4), (0, 0)))
    sink = jnp.broadcast_to(attn_sink.reshape(8, 1), (8, 128))
    bg = jnp.concatenate([b_gate_fwd, b_gate_bwd], axis=1)
    ggla = jnp.tile(g_gla_norm, (1, N_GLA))

    r = _local_step(x[0], ctx[0], loss_target[0], ada_l, ada_c, gains, sink, win_p, wg_bd, bg, ggla,
                    w_out[0].astype(_BF), w_ffn_in[0].T.astype(_BF), w_ffn_out[0].astype(_BF))

    parts = r["parts"]
    tot = _sum_slots(parts, "sum_small_grads")
    loss = tot[R_LOSS, 0]
    d_ada_rows = parts[:, R_ADA:R_ADA + 6].reshape(N_DEV, 6 * D)
    d_ada_c = tot[R_ADA_C:R_ADA_C + 6].reshape(1, 6 * D)
    my_cols = lax.dynamic_slice_in_dim(jnp.concatenate([d_ada_rows, jnp.broadcast_to(d_ada_c, (1, 6 * D)),
                                                        jnp.zeros((7, 6 * D), F32)], axis=0), me * n_ada, n_ada, 1)
    grad_w_ada, t_part = _ada_bwd(c_all, c_ctx.reshape(1, D), w_ada[0], my_cols)
    wg_g = lax.dynamic_slice(tot, (R_WG, me * 32), (2 * GATE_RANK, 32))

    tr = lambda a: jnp.transpose(a, (0, 2, 1))
    big = {}
    t_all, = _exchange([t_part], "gather_c_ctx", True)
    t_tot = _sum_slots(t_all, "sum_c_ctx")
    for nm, w, g, m, v in [("w_ada", w_ada, grad_w_ada, m_w_ada, v_w_ada),
                           ("w_out", w_out, r["got_out"], m_w_out, v_w_out),
                           ("w_ffn_out", w_ffn_out, r["got_ffo"], m_w_ffn_out, v_w_ffn_out)]:
        big[nm] = _adamw(w, [g], m, v, "adamw_" + nm)
    big["w_ffn_in"] = tuple(tr(o) for o in _adamw(tr(w_ffn_in), [r["got_ffi"]], tr(m_w_ffn_in), tr(v_w_ffn_in),
                                                  "adamw_w_ffn_in"))
    big["w_in"] = tuple(tr(o) for o in _adamw(tr(w_in), r["got_in"], tr(m_w_in), tr(v_w_in), "adamw_w_in"))

    w_small = dict(c_ctx=c_ctx.reshape(1, D), b_ada=b_ada, g_pre_mix=g_pre_mix, g_post_mix=g_post_mix, g_pre_ffn=g_pre_ffn,
                   g_post_ffn=g_post_ffn, attn_sink=attn_sink, b_gate_fwd=b_gate_fwd, b_gate_bwd=b_gate_bwd,
                   g_gla_norm=g_gla_norm, w_gate_fwd=w_gate_fwd, w_gate_bwd=w_gate_bwd)
    m_small = dict(c_ctx=m_c_ctx.reshape(1, D), b_ada=m_b_ada, g_pre_mix=m_g_pre_mix, g_post_mix=m_g_post_mix,
                   g_pre_ffn=m_g_pre_ffn, g_post_ffn=m_g_post_ffn, attn_sink=m_attn_sink, b_gate_fwd=m_b_gate_fwd,
                   b_gate_bwd=m_b_gate_bwd, g_gla_norm=m_g_gla_norm, w_gate_fwd=m_w_gate_fwd, w_gate_bwd=m_w_gate_bwd)
    v_small = dict(c_ctx=v_c_ctx.reshape(1, D), b_ada=v_b_ada, g_pre_mix=v_g_pre_mix, g_post_mix=v_g_post_mix,
                   g_pre_ffn=v_g_pre_ffn, g_post_ffn=v_g_post_ffn, attn_sink=v_attn_sink, b_gate_fwd=v_b_gate_fwd,
                   b_gate_bwd=v_b_gate_bwd, g_gla_norm=v_g_gla_norm, w_gate_fwd=v_w_gate_fwd, w_gate_bwd=v_w_gate_bwd)
    grads_small, d_s, nm_s, nv_s = _small_update(tot, t_tot, wg_g, w_small, m_small, v_small)
    for dd in (grads_small, d_s, nm_s, nv_s):
        dd["c_ctx"] = dd["c_ctx"].reshape(D)

    order = ["c_ctx", "w_ada", "b_ada", "g_pre_mix", "g_post_mix", "g_pre_ffn", "g_post_ffn", "w_in", "attn_sink",
             "w_gate_fwd", "b_gate_fwd", "w_gate_bwd", "b_gate_bwd", "g_gla_norm", "w_out", "w_ffn_in", "w_ffn_out"]
    grads, deltas, new_m, new_v = [], [], [], []
    for nm in order:
        if nm in big:
            g_, d_, m_, v_ = big[nm]
        else:
            g_, d_, m_, v_ = grads_small[nm], d_s[nm], nm_s[nm], nv_s[nm]
        grads.append(g_)
        deltas.append(d_)
        new_m.append(m_)
        new_v.append(v_)
    return (loss, r["grad_x"][None], *grads, *deltas, *new_m, *new_v)
```

```python
import functools
import math

import numpy as np
import jax
import jax.numpy as jnp
from jax import lax
from jax.experimental import pallas as pl
from jax.experimental.pallas import tpu as pltpu

F32 = jnp.float32
_BF = jnp.bfloat16

N_DEV = 8
D = 1024
CTX = 256
HD = 64
N_ATT = 8
N_KV = 2
GRP = N_ATT // N_KV
WIN = 128
GRID_W = 64
ROPE_BASE = 10000.0
N_GLA = 8
DK = 32
DV = 64
GATE_RANK = 16
GATE_TAU = 16.0
FFN = 2816
EPS = 1e-6
NEG = -1e30
GLA_T = 128

QP = N_ATT * HD
KP = N_KV * HD
O_Q, O_K, O_V = 0, QP, QP + KP
O_GQ = O_V + KP
O_GK = O_GQ + N_GLA * DK
O_GV = O_GK + N_GLA * DK
O_GG = O_GV + N_GLA * DV
O_Z = O_GG + N_GLA * DV
NP = O_Z + 128
IN_COLS = 2336

ADAM_LR, ADAM_B1, ADAM_B2, ADAM_EPS, ADAM_WD, ADAM_STEP = 0.001, 0.9, 0.999, 1e-08, 0.01, 10

VMEM_BIG = 56 * 1024 * 1024
MESH = pl.DeviceIdType.MESH


def _cp(sem, vmem=None):
    return pltpu.CompilerParams(dimension_semantics=sem, vmem_limit_bytes=vmem)


def _full(shape):
    nd = len(shape)
    return pl.BlockSpec(shape, lambda *a: (0,) * nd)


def _rows(tile, width, off=0):
    return pl.BlockSpec((tile, width), lambda i: (i + off, 0))


def _rows_lat(tile, width):
    return pl.BlockSpec((tile, width), lambda i: (jnp.maximum(i - 1, 0), 0))


def _nt(a, b):
    return lax.dot_general(a, b, (((1,), (1,)), ((), ())), preferred_element_type=F32)


def _tn(a, b):
    return lax.dot_general(a, b, (((0,), (0,)), ((), ())), preferred_element_type=F32)


def _nn(a, b):
    return jnp.dot(a, b, preferred_element_type=F32)


def _head_mean(x, mavg):
    n = x.shape[0]
    hi = x.astype(_BF)
    lo = (x - hi.astype(F32)).astype(_BF)
    y = _nn(jnp.concatenate([hi, lo], axis=0), mavg)
    return y[0:n] + y[n:2 * n]


def _rope(t, cos, sa, sb):
    n = t.shape[1]
    reps = n // 128
    c = jnp.tile(cos, (1, reps))
    a = jnp.tile(sa, (1, reps))
    b = jnp.tile(sb, (1, reps))
    return t * c + pltpu.roll(t, n - 16, 1) * a + pltpu.roll(t, 16, 1) * b


def _unrope(t, cos, sa, sb):
    n = t.shape[1]
    reps = n // 128
    c = jnp.tile(cos, (1, reps))
    a = jnp.tile(sa, (1, reps))
    b = jnp.tile(sb, (1, reps))
    return t * c + pltpu.roll(t * a, 16, 1) + pltpu.roll(t * b, n - 16, 1)


def _sigmoid(x):
    return 1.0 / (1.0 + jnp.exp(-x))


def _inproj_fwd(x, ctx, gml, shl, gmc, shc, win, wg, bg, cos, sa, sb, shards):
    E = x.shape[0] + CTX
    TE = CTX

    def body(x_ref, c_ref, gml_ref, shl_ref, gmc_ref, shc_ref, w_ref, wg_ref, bg_ref, cos_ref, sa_ref, sb_ref,
             h_ref, q_ref, k_ref, v_ref, gq_ref, gk_ref, gv_ref, gg_ref, z_ref, la_ref):
        is_ctx = pl.program_id(0) == 0
        gm = jnp.where(is_ctx, gmc_ref[...], gml_ref[...])
        sh = jnp.where(is_ctx, shc_ref[...], shl_ref[...])
        x = jnp.where(is_ctx, c_ref[...], x_ref[...])
        r = lax.rsqrt(jnp.mean(x * x, axis=-1, keepdims=True) + EPS)
        hb = ((x * r) * gm + sh).astype(_BF)
        h_ref[...] = hb
        p = _nt(hb, w_ref[...])
        cos_t, sa_t, sb_t = cos_ref[...], sa_ref[...], sb_ref[...]
        q_ref[...] = (_rope(p[:, O_Q:O_K], cos_t, sa_t, sb_t) * (HD ** -0.5)).astype(_BF)
        k_ref[...] = _rope(p[:, O_K:O_V], cos_t, sa_t, sb_t).astype(_BF)
        v_ref[...] = p[:, O_V:O_GQ].astype(_BF)
        gq_ref[...] = p[:, O_GQ:O_GK] * (DK ** -0.5)
        gk_ref[...] = p[:, O_GK:O_GV]
        gv_ref[...] = p[:, O_GV:O_GG]
        gg_ref[...] = p[:, O_GG:O_Z]
        zb = p[:, O_Z:NP].astype(_BF)
        z_ref[...] = zb
        lg = _nn(zb, wg_ref[...]) + bg_ref[...]
        la_ref[...] = (jnp.minimum(lg, 0.0) - jnp.log(1.0 + jnp.exp(-jnp.abs(lg)))) * (1.0 / GATE_TAU)

    vec = _full((1, D))
    tab = _rows(TE, 128)
    outs = [(D, _BF), (QP, _BF), (KP, _BF), (KP, _BF), (256, F32), (256, F32), (512, F32), (512, F32),
            (128, _BF), (512, F32)]
    return _hosted_call(
        body, (x, ctx, gml, shl, gmc, shc, win, wg, bg, cos, sa, sb), shards, True,
        name="inproj_fwd", grid=(E // TE,),
        in_specs=[_rows_lat(TE, D), _full((CTX, D)), vec, vec, vec, vec, _full((NP, D)), _full((128, 512)),
                  _full((1, 512)), tab, tab, tab],
        out_specs=[_rows(TE, w) for w, _ in outs],
        out_shape=[jax.ShapeDtypeStruct((E, w), dt) for w, dt in outs],
        compiler_params=_cp(("arbitrary",), 40 * 1024 * 1024))


def _xchg_scratch(na):
    return [pltpu.SemaphoreType.DMA((na, N_DEV - 1)), pltpu.SemaphoreType.DMA((na, N_DEV - 1)),
            pltpu.SemaphoreType.DMA((na,))]


def _xchg_copies(ins, outs, send_sems, recv_sems, local_sems, gather):
    x, y, c = lax.axis_index("x"), lax.axis_index("y"), lax.axis_index("c")
    me = 4 * x + 2 * y + c
    local, sends, recvs = [], [], []
    for a in range(len(ins)):
        local.append(pltpu.make_async_copy(ins[a] if gather else ins[a].at[me], outs[a].at[me], local_sems.at[a]))
    for k in range(1, N_DEV):
        px, py, pc = x ^ (k >> 2), y ^ ((k >> 1) & 1), c ^ (k & 1)
        peer = 4 * px + 2 * py + pc
        for a in range(len(ins)):
            sems = dict(send_sem=send_sems.at[a, k - 1], recv_sem=recv_sems.at[a, k - 1], device_id_type=MESH)
            sends.append(pltpu.make_async_remote_copy(
                src_ref=ins[a] if gather else ins[a].at[peer], dst_ref=outs[a].at[me], device_id=(px, py, pc), **sems))
            recvs.append(pltpu.make_async_remote_copy(
                src_ref=ins[a] if gather else ins[a].at[me], dst_ref=outs[a].at[peer], device_id=(x, y, c), **sems))
    return local, sends, recvs


def _xchg_start(cps):
    local, sends, _ = cps
    for cp in local + sends:
        cp.start()


def _xchg_finish(cps):
    local, sends, recvs = cps
    for cp in recvs:
        cp.wait_recv()
    for cp in sends:
        cp.wait_send()
    for cp in local:
        cp.wait()


def _xchg_out_shapes(ins, gather):
    return [jax.ShapeDtypeStruct(((N_DEV,) + s.shape) if gather else s.shape, s.dtype) for s in ins]


def _hosted_call(body, args, hosted, gather, *, grid, in_specs, out_specs, out_shape, scratch_shapes=(), **kw):
    na = len(hosted)
    if na == 0:
        return pl.pallas_call(body, grid=grid, in_specs=in_specs, out_specs=out_specs, out_shape=out_shape,
                              scratch_shapes=list(scratch_shapes), **kw)(*args)
    n_in, n_out, n_scr = len(in_specs), len(out_specs), len(scratch_shapes)

    def wrapped(*refs):
        ins, h_in = refs[:n_in], refs[n_in:n_in + na]
        outs, h_out = refs[n_in + na:n_in + na + n_out], refs[n_in + na + n_out:n_in + 2 * na + n_out]
        scr = refs[n_in + 2 * na + n_out:]
        cps = _xchg_copies(h_in, h_out, *scr[n_scr:], gather=gather)
        pids = [pl.program_id(a) for a in range(len(grid))]
        first = functools.reduce(jnp.logical_and, [p == 0 for p in pids])
        last = functools.reduce(jnp.logical_and, [p == g - 1 for p, g in zip(pids, grid)])

        @pl.when(first)
        def _():
            _xchg_start(cps)

        body(*ins, *outs, *scr[:n_scr])

        @pl.when(last)
        def _():
            _xchg_finish(cps)

    anyspec = pl.BlockSpec(memory_space=pl.ANY)
    return pl.pallas_call(
        wrapped, grid=grid, in_specs=list(in_specs) + [anyspec] * na, out_specs=list(out_specs) + [anyspec] * na,
        out_shape=list(out_shape) + _xchg_out_shapes(hosted, gather),
        scratch_shapes=list(scratch_shapes) + _xchg_scratch(na), **kw)(*args, *hosted)


def _attn_specs(E):
    nb = (E - CTX) // WIN
    last = E // WIN - 1
    kc = pl.BlockSpec((CTX, KP), lambda n: (0, 0))
    kp = pl.BlockSpec((WIN, KP), lambda n: (n + 1, 0))
    kk = pl.BlockSpec((WIN, KP), lambda n: (n + 2, 0))
    kn = pl.BlockSpec((WIN, KP), lambda n: (jnp.minimum(n + 3, last), 0))
    return nb, [kc, kp, kk, kn]


def _attn_bias(nb):
    rows = np.arange(GRP * WIN)[:, None] % WIN
    cols = np.arange(CTX + 3 * WIN)[None, :]
    j = cols - CTX
    band = np.abs(j - WIN - rows) <= WIN
    out = []
    for first, last in ((True, False), (False, False), (False, True)):
        ok = (cols < CTX) | (band & ((j >= WIN) | (not first)) & ((j < 2 * WIN) | (not last)))
        out.append(np.where(ok, 0.0, NEG).astype(np.float32))
    bias = jnp.asarray(np.stack(out))
    spec = pl.BlockSpec((1, GRP * WIN, CTX + 3 * WIN),
                        lambda n: (jnp.where(n == 0, 0, jnp.where(n == nb - 1, 2, 1)), 0, 0))
    return bias, spec


def _both_halves(t, h):
    tf = t.astype(F32)
    r = pltpu.roll(tf, HD, 1)
    lo = lax.broadcasted_iota(jnp.int32, tf.shape, 1) < HD
    return (jnp.where(lo, tf, r) if h == 0 else jnp.where(lo, r, tf)).astype(t.dtype)


def _stack_heads(ref, h):
    lo = lax.broadcasted_iota(jnp.int32, (WIN, 128), 1) < HD
    parts = []
    for g in range(GRP):
        j = GRP * h + g
        t = ref[:, 128 * (j // 2):128 * (j // 2) + 128].astype(F32)
        parts.append(jnp.where(lo if j % 2 == 0 else jnp.logical_not(lo), t, 0.0))
    return jnp.concatenate(parts, axis=0)


def _unstack_pair(o, pp):
    lo = lax.broadcasted_iota(jnp.int32, (WIN, 128), 1) < HD
    return jnp.where(lo, o[WIN * 2 * pp:WIN * 2 * pp + WIN], o[WIN * (2 * pp + 1):WIN * (2 * pp + 1) + WIN])


def _attn_fwd(q, k, v, sink, shards):
    E = q.shape[0]
    S = E - CTX
    nb, kspecs = _attn_specs(E)
    na = len(shards)

    def body(q_ref, kc, kp, kk, kn, vc, vp, vk, vn, sink_ref, bias_ref, *rest):
        shard_refs, (o_ref, lse_ref, p_ref), got_refs = rest[:na], rest[na:na + 3], rest[na + 3:2 * na + 3]
        n = pl.program_id(0)
        cps = _xchg_copies(shard_refs, got_refs, *rest[2 * na + 3:], gather=True)

        @pl.when(n == 0)
        def _():
            _xchg_start(cps)

        lane = lax.broadcasted_iota(jnp.int32, (WIN, 128), 1)
        lse_t = jnp.zeros((WIN, 128), F32)
        kall = jnp.concatenate([kc[...], kp[...], kk[...], kn[...]], axis=0)
        vall = jnp.concatenate([vc[...], vp[...], vk[...], vn[...]], axis=0)
        K = [_both_halves(kall, h) for h in range(N_KV)]
        Q = [_stack_heads(q_ref, h).astype(_BF) for h in range(N_KV)]
        sk = [jnp.concatenate([jnp.broadcast_to(sink_ref[GRP * h + g:GRP * h + g + 1, 0:1], (WIN, 1))
                               for g in range(GRP)], axis=0) for h in range(N_KV)]
        s = [_nt(Q[h], K[h]) + bias_ref[0] for h in range(N_KV)]
        m = [jnp.maximum(jnp.max(s[h], axis=1, keepdims=True), sk[h]) for h in range(N_KV)]
        e = [jnp.exp(s[h] - m[h]) for h in range(N_KV)]
        den = [jnp.sum(e[h], axis=1, keepdims=True) + jnp.exp(sk[h] - m[h]) for h in range(N_KV)]
        V = [_both_halves(vall, h) for h in range(N_KV)]
        pb = [(e[h] * (1.0 / den[h])).astype(_BF) for h in range(N_KV)]
        for h in range(N_KV):
            p_ref[GRP * WIN * h:GRP * WIN * (h + 1), :] = pb[h]
        o = [_nn(pb[h], V[h]) for h in range(N_KV)]
        for h in range(N_KV):
            lse = m[h] + jnp.log(den[h])
            for g in range(GRP):
                lse_t = jnp.where(lane == GRP * h + g, lse[WIN * g:WIN * g + WIN], lse_t)
            for pp in range(GRP // 2):
                t = 2 * h + pp
                o_ref[:, 128 * t:128 * t + 128] = _unstack_pair(o[h], pp).astype(_BF)
        lse_ref[...] = lse_t

        @pl.when(n == nb - 1)
        def _():
            _xchg_finish(cps)

    qs = pl.BlockSpec((WIN, QP), lambda n: (n + 2, 0))
    anyspec = pl.BlockSpec(memory_space=pl.ANY)
    bias, bias_spec = _attn_bias(nb)
    return pl.pallas_call(
        body, name="attn_fwd", grid=(nb,),
        in_specs=[qs] + kspecs + kspecs + [_full((8, 128)), bias_spec] + [anyspec] * na,
        out_specs=[_rows(WIN, 512), _rows(WIN, 128), _rows(N_KV * GRP * WIN, CTX + 3 * WIN)] + [anyspec] * na,
        out_shape=[jax.ShapeDtypeStruct((S, 512), _BF), jax.ShapeDtypeStruct((S, 128), F32),
                   jax.ShapeDtypeStruct((nb * N_KV * GRP * WIN, CTX + 3 * WIN), _BF)]
        + _xchg_out_shapes(shards, True),
        scratch_shapes=_xchg_scratch(na),
        compiler_params=_cp(("arbitrary",)),
    )(q, k, k, k, k, v, v, v, v, sink, bias, *shards)


def _attn_bwd(q, k, v, sink, probs, lse, d_attn, slabs):
    E = q.shape[0]
    S = E - CTX
    nb, kspecs = _attn_specs(E)
    last = E // WIN - 1
    na = len(slabs)

    def body(q_ref, kc, kp, kk, kn, vc, vp, vk, vn, sink_ref, p_ref, lse_ref, do_ref, *rest):
        slab_refs, (dq_ref, dk_ref, dv_ref, ds_ref), got_refs = rest[:na], rest[na:na + 4], rest[na + 4:2 * na + 4]
        n = pl.program_id(0)
        cps = _xchg_copies(slab_refs, got_refs, *rest[2 * na + 4:], gather=False)

        @pl.when(n == 0)
        def _():
            _xchg_start(cps)
            dk_ref[...] = jnp.zeros_like(dk_ref)
            dv_ref[...] = jnp.zeros_like(dv_ref)
            ds_ref[...] = jnp.zeros_like(ds_ref)

        lane = lax.broadcasted_iota(jnp.int32, (WIN, 128), 1)
        lse_t = lse_ref[...]
        starts = [None, pl.multiple_of((n + 1) * WIN, WIN), pl.multiple_of((n + 2) * WIN, WIN),
                  pl.multiple_of(jnp.minimum(n + 3, last) * WIN, WIN)]
        kall = jnp.concatenate([kc[...], kp[...], kk[...], kn[...]], axis=0)
        vall = jnp.concatenate([vc[...], vp[...], vk[...], vn[...]], axis=0)
        for h in range(N_KV):
            hs = slice(HD * h, HD * h + HD)
            K = _both_halves(kall, h)
            V = _both_halves(vall, h)
            Q = _stack_heads(q_ref, h).astype(_BF)
            sk = jnp.concatenate([jnp.broadcast_to(sink_ref[GRP * h + g:GRP * h + g + 1, 0:1], (WIN, 1))
                                  for g in range(GRP)], axis=0)
            ls = jnp.concatenate([jnp.sum(jnp.where(lane == GRP * h + g, lse_t, 0.0), axis=1, keepdims=True)
                                  for g in range(GRP)], axis=0)
            do = _stack_heads(do_ref, h).astype(_BF)
            pb = p_ref[GRP * WIN * h:GRP * WIN * (h + 1), :]
            p = pb.astype(F32)
            dp = _nt(do, V)
            delta = jnp.sum(p * dp, axis=1, keepdims=True)
            dsc = (p * (dp - delta)).astype(_BF)
            dq = _nn(dsc, K) * (HD ** -0.5)
            for pp in range(GRP // 2):
                t = 2 * h + pp
                dq_ref[:, 128 * t:128 * t + 128] = _unstack_pair(dq, pp).astype(_BF)
            dK2 = _tn(Q, dsc)
            dV2 = _tn(do, pb)
            dK = dK2[0:HD] + dK2[HD:2 * HD]
            dV = dV2[0:HD] + dV2[HD:2 * HD]
            dk_ref[hs, 0:CTX] += dK[:, 0:CTX]
            dv_ref[hs, 0:CTX] += dV[:, 0:CTX]
            for w in range(1, 4):
                lo = CTX + WIN * (w - 1)
                dk_ref[hs, pl.ds(starts[w], WIN)] += dK[:, lo:lo + WIN]
                dv_ref[hs, pl.ds(starts[w], WIN)] += dV[:, lo:lo + WIN]
            psk = -jnp.exp(sk - ls) * delta
            for g in range(GRP):
                j = GRP * h + g
                tot = jnp.sum(psk[WIN * g:WIN * g + WIN], axis=0, keepdims=True)
                ds_ref[j:j + 1, :] += jnp.broadcast_to(tot, (1, 128))

        @pl.when(n == nb - 1)
        def _():
            _xchg_finish(cps)

    qs = pl.BlockSpec((WIN, QP), lambda n: (n + 2, 0))
    anyspec = pl.BlockSpec(memory_space=pl.ANY)
    return pl.pallas_call(
        body, name="attn_bwd", grid=(nb,),
        in_specs=[qs] + kspecs + kspecs + [_full((8, 128)), _rows(N_KV * GRP * WIN, CTX + 3 * WIN), _rows(WIN, 128),
                                            _rows(WIN, 512)] + [anyspec] * na,
        out_specs=[_rows(WIN, QP), _full((KP, E)), _full((KP, E)), _full((8, 128))] + [anyspec] * na,
        out_shape=[jax.ShapeDtypeStruct((S, QP), _BF), jax.ShapeDtypeStruct((KP, E), F32),
                   jax.ShapeDtypeStruct((KP, E), F32), jax.ShapeDtypeStruct((8, 128), F32)]
        + _xchg_out_shapes(slabs, False),
        scratch_shapes=_xchg_scratch(na),
        compiler_params=_cp(("arbitrary",), 48 * 1024 * 1024),
    )(q, k, k, k, k, v, v, v, v, sink, probs, lse, d_attn, *slabs)


def _gla_order(E, reverse, backward):
    nc = CTX // GLA_T
    n = E // GLA_T
    if not reverse:
        fwd = lambda s: s
    else:
        fwd = lambda s: jnp.where(s < nc, nc - 1 - s, n - 1 + nc - s)
    if backward:
        return lambda s: fwd(n - 1 - s)
    return fwd


def _gla_masks():
    T = GLA_T
    l128 = lax.broadcasted_iota(jnp.int32, (1, 128), 1)
    qmask = [((l128 >> 5) == j).astype(F32) for j in range(4)]
    vmask = [((l128 >> 6) == j).astype(F32) for j in range(2)]
    bd = ((lax.broadcasted_iota(jnp.int32, (512, 256), 0) >> 6)
          == (lax.broadcasted_iota(jnp.int32, (512, 256), 1) >> 5)).astype(F32)
    ri = lax.broadcasted_iota(jnp.int32, (T, 2 * T), 0)
    ci = lax.broadcasted_iota(jnp.int32, (T, 2 * T), 1) & (T - 1)
    return qmask, vmask, bd, ri, ci


def _tri_sum(tri, x):
    hi = x.astype(_BF)
    lo = (x - hi.astype(F32)).astype(_BF)
    n = x.shape[1]
    y = _nn(tri.astype(_BF), jnp.concatenate([hi, lo], axis=1))
    return y[:, 0:n] + y[:, n:2 * n]


def _gla_decays(la, reverse, ri, ci):
    T = GLA_T
    msk2 = (ri <= ci) if reverse else (ri >= ci)
    mskT2 = (ri >= ci) if reverse else (ri <= ci)
    b = _tri_sum(msk2[:, 0:T], la)
    bT = b[0:1] if reverse else b[T - 1:T]
    bm = b[T // 2:T // 2 + 1]
    return msk2, mskT2, b, bT, bm


def _pair_stack(tile, m0, m1):
    return jnp.concatenate([(tile * m0).astype(_BF), (tile * m1).astype(_BF)], axis=0)


def _gla_fwd(gq, gk, gv, la, reverse, shards=(), other=None):
    E = gq.shape[0]
    T = GLA_T
    n = E // T
    order = _gla_order(E, reverse, False)
    col = 1 if reverse else 0

    def body(gq_ref, gk_ref, gv_ref, la_ref, *rest):
        o_ref, st_ref, S_scr = rest[-3:]

        @pl.when(pl.program_id(0) == 0)
        def _():
            S_scr[...] = jnp.zeros_like(S_scr)

        qmask, vmask, bd, ri, ci = _gla_masks()
        msk2, _, b, bT, bm = _gla_decays(la_ref[...], reverse, ri, ci)
        q, k, v = gq_ref[...], gk_ref[...], gv_ref[...]
        qd = (q * jnp.exp(b)).astype(_BF)
        qm = (q * jnp.exp(b - bm)).astype(_BF)
        km = k * jnp.exp(bm - b)
        kd = (k * jnp.exp(bT - b)).astype(_BF)
        ST = S_scr[...]
        comp = ST[0:DV]
        for h in range(1, N_GLA):
            comp = comp + ST[DV * h:DV * h + DV]
        st_ref[0] = comp
        inter = _nt(qd, ST.astype(_BF))
        tiles = []
        for p in range(N_GLA // 2):
            qs = slice(128 * (p // 2), 128 * (p // 2) + 128)
            vs = slice(128 * p, 128 * p + 128)
            j0 = (2 * p) % 4
            KS = _pair_stack(km[:, qs], qmask[j0], qmask[j0 + 1])
            VS = _pair_stack(v[:, vs], vmask[0], vmask[1])
            AA = jnp.where(msk2, _nt(qm[:, qs], KS), 0.0).astype(_BF)
            tiles.append(_nn(AA, VS))
        o = inter + jnp.concatenate(tiles, axis=1)
        o_ref[...] = o if other is None else o + rest[0][...]
        S_scr[...] = ST * jnp.exp(bT) + bd * _tn(v.astype(_BF), kd)

    blk = lambda w, c=0: pl.BlockSpec((T, w), lambda s: (order(s), c))
    return _hosted_call(
        body, (gq, gk, gv, la) + (() if other is None else (other,)), shards, True,
        name="gla_fwd_rev" if reverse else "gla_fwd", grid=(n,),
        in_specs=[blk(256), blk(256), blk(512), blk(256, col)] + ([] if other is None else [blk(512)]),
        out_specs=[blk(512), pl.BlockSpec((1, DV, 256), lambda s: (order(s), 0, 0))],
        out_shape=[jax.ShapeDtypeStruct((E, 512), F32), jax.ShapeDtypeStruct((n, DV, 256), F32)],
        scratch_shapes=[pltpu.VMEM((512, 256), F32)],
        compiler_params=_cp(("arbitrary",)))


def _gla_bwd(gq, gk, gv, la, z, st, do, reverse, prev=None, slabs=()):
    E = gq.shape[0]
    T = GLA_T
    n = E // T
    nc = CTX // T
    order = _gla_order(E, reverse, True)
    col = 1 if reverse else 0
    np_ = 0 if prev is None else 3

    def body(gq_ref, gk_ref, gv_ref, la_ref, z_ref, st_ref, do_ref, *rest):
        prev_refs = rest[:np_]
        dq_ref, dk_ref, dv_ref, dlg_ref, gwg_ref, bsum_ref, dS_scr = rest[np_:]

        @pl.when(pl.program_id(0) == 0)
        def _():
            dS_scr[...] = jnp.zeros_like(dS_scr)
            gwg_ref[...] = jnp.zeros_like(gwg_ref)
            bsum_ref[...] = jnp.zeros_like(bsum_ref)

        is_lat = order(pl.program_id(0)) >= nc
        qmask, vmask, bd, ri, ci = _gla_masks()
        msk2, mskT2, b, bT, bm = _gla_decays(la_ref[...], reverse, ri, ci)
        q, k, v = gq_ref[...], gk_ref[...], gv_ref[...]
        do = jnp.where(is_lat, do_ref[...].astype(F32), 0.0)
        e_b, e_qm, e_km, e_kd, e_T = jnp.exp(b), jnp.exp(b - bm), jnp.exp(bm - b), jnp.exp(bT - b), jnp.exp(bT)
        qd, qm, km, kd = q * e_b, q * e_qm, k * e_km, k * e_kd
        qdb, qmb, kmb, kdb, vb, dob = (t.astype(_BF) for t in (qd, qm, km, kd, v, do))
        ST = jnp.tile(st_ref[0], (N_GLA, 1)) * bd
        dST = dS_scr[...]
        dSTb = dST.astype(_BF)
        dqd = _nn(dob, ST.astype(_BF))
        dkd = _nn(vb, dSTb)
        dv_t, dqm_t, dkm_t = [], [None, None], [None, None]
        for p in range(N_GLA // 2):
            t = p // 2
            qs = slice(128 * t, 128 * t + 128)
            vs = slice(128 * p, 128 * p + 128)
            j0 = (2 * p) % 4
            QS = _pair_stack(qm[:, qs], qmask[j0], qmask[j0 + 1])
            KS = _pair_stack(km[:, qs], qmask[j0], qmask[j0 + 1])
            VS = _pair_stack(v[:, vs], vmask[0], vmask[1])
            DS = _pair_stack(do[:, vs], vmask[0], vmask[1])
            ATT = jnp.where(mskT2, _nt(kmb[:, qs], QS), 0.0).astype(_BF)
            dAA = jnp.where(msk2, _nt(dob[:, vs], VS), 0.0).astype(_BF)
            dATT = jnp.where(mskT2, _nt(vb[:, vs], DS), 0.0).astype(_BF)
            dv_t.append(_nn(ATT, DS))
            dq_p = _nn(dAA, KS)
            dk_p = _nn(dATT, QS)
            dqm_t[t] = dq_p if dqm_t[t] is None else dqm_t[t] + dq_p
            dkm_t[t] = dk_p if dkm_t[t] is None else dkm_t[t] + dk_p
        dqm = jnp.concatenate(dqm_t, axis=1)
        dkm = jnp.concatenate(dkm_t, axis=1)
        dq = dqm * e_qm + dqd * e_b
        dk = dkm * e_km + dkd * e_kd
        dv = _nt(kdb, dSTb) + jnp.concatenate(dv_t, axis=1)
        if prev is None:
            dq_ref[...], dk_ref[...], dv_ref[...] = dq, dk, dv
        else:
            dq_ref[...] = ((dq + prev_refs[0][...]) * (DK ** -0.5)).astype(_BF)
            dk_ref[...] = (dk + prev_refs[1][...]).astype(_BF)
            dv_ref[...] = (dv + prev_refs[2][...]).astype(_BF)
        db = dqm * qm - dkm * km + dqd * qd - dkd * kd
        dbT = jnp.sum(dkd * kd, axis=0, keepdims=True) + e_T * jnp.sum(dST * ST, axis=0, keepdims=True)
        dla = _tri_sum(mskT2[:, 0:T], db) + dbT
        dlg = dla * (1.0 - jnp.exp(GATE_TAU * la_ref[...])) * (1.0 / GATE_TAU)
        bsum_ref[0:1, :] += jnp.sum(dlg, axis=0, keepdims=True)
        dlgb = dlg.astype(_BF)
        dlg_ref[...] = dlgb
        gwg_ref[...] += _tn(z_ref[...], dlgb)
        dS_scr[...] = dST * e_T + bd * _tn(dob, qdb)

    blk = lambda w, c=0: pl.BlockSpec((T, w), lambda s: (order(s), c))
    do_spec = pl.BlockSpec((T, 512), lambda s: (jnp.maximum(order(s) - nc, 0), 0))
    odt = F32 if prev is None else _BF
    return _hosted_call(
        body, (gq, gk, gv, la, z, st, do) + (() if prev is None else tuple(prev)), slabs, False,
        name="gla_bwd_rev" if reverse else "gla_bwd", grid=(n,),
        in_specs=[blk(256), blk(256), blk(512), blk(256, col), blk(128),
                  pl.BlockSpec((1, DV, 256), lambda s: (order(s), 0, 0)), do_spec]
        + ([] if prev is None else [blk(256), blk(256), blk(512)]),
        out_specs=[blk(256), blk(256), blk(512), blk(256), _full((128, 256)), _full((8, 256))],
        out_shape=[jax.ShapeDtypeStruct((E, 256), odt), jax.ShapeDtypeStruct((E, 256), odt),
                   jax.ShapeDtypeStruct((E, 512), odt), jax.ShapeDtypeStruct((E, 256), _BF),
                   jax.ShapeDtypeStruct((128, 256), F32), jax.ShapeDtypeStruct((8, 256), F32)],
        scratch_shapes=[pltpu.VMEM((512, 256), F32)],
        compiler_params=_cp(("arbitrary",)))


def _gla_out(o, gg, ggla, mavg):
    rr = lax.rsqrt(_head_mean(o * o, mavg) + EPS)
    oh = o * rr
    sg = _sigmoid(gg)
    return oh, rr, sg


def _mix_fwd(x, attn, o, gg, ggla, mavg, wout, gt1, g2):
    S = x.shape[0]
    TM = 256

    def body(x_ref, a_ref, o_ref, gg_ref, ggla_ref, mavg_ref, w_ref, gt1_ref, g2_ref, x1_ref, mix_ref):
        gg_t = gg_ref[...]
        oh, _, sg = _gla_out(o_ref[...], gg_t, ggla_ref[...], mavg_ref[...])
        mix_ref[:, 0:512] = a_ref[...]
        mix_ref[:, 512:1024] = (oh * ggla_ref[...] * (gg_t * sg)).astype(_BF)
        y = _nn(mix_ref[...], w_ref[...])
        ry = lax.rsqrt(jnp.mean(y * y, axis=-1, keepdims=True) + EPS)
        x1_ref[...] = x_ref[...] + gt1_ref[...] * ((y * ry) * g2_ref[...])

    return pl.pallas_call(
        body, name="mix_fwd", grid=(S // TM,),
        in_specs=[_rows(TM, D), _rows(TM, 512), _rows(TM, 512, 1), _rows(TM, 512, 1),
                  _full((1, 512)), _full((512, 512)), _full((D, D)), _full((1, D)), _full((1, D))],
        out_specs=[_rows(TM, D), _rows(TM, D)],
        out_shape=[jax.ShapeDtypeStruct((S, D), F32), jax.ShapeDtypeStruct((S, D), _BF)],
        compiler_params=_cp(("arbitrary",), 40 * 1024 * 1024),
    )(x, attn, o, gg, ggla, mavg, wout, gt1, g2)


def _mix_bwd(dx1, mix, o, gg, ggla, mavg, wout, gt1, g2, slabs):
    S = dx1.shape[0]
    TM = 256

    def body(dx_ref, mix_ref, o_ref, gg_ref, ggla_ref, mavg_ref, w_ref, gt1_ref, g2_ref,
             da_ref, do_ref, dgg_ref, dy_ref, sums_ref):
        @pl.when(pl.program_id(0) == 0)
        def _():
            sums_ref[...] = jnp.zeros_like(sums_ref)

        dx = dx_ref[...]
        y = _nn(mix_ref[...], w_ref[...])
        ry = lax.rsqrt(jnp.mean(y * y, axis=-1, keepdims=True) + EPS)
        yh = y * ry
        sums_ref[0:1, :] += jnp.sum(dx * yh, axis=0, keepdims=True)
        dyh = dx * (gt1_ref[...] * g2_ref[...])
        dy = (ry * (dyh - yh * jnp.mean(dyh * yh, axis=-1, keepdims=True))).astype(_BF)
        dy_ref[...] = dy
        dmix = _nt(dy, w_ref[...])
        da_ref[...] = dmix[:, 0:512].astype(_BF)
        dgla = dmix[:, 512:1024]
        gg_t = gg_ref[...]
        ggla_t = ggla_ref[...]
        oh, rr, sg = _gla_out(o_ref[...], gg_t, ggla_t, mavg_ref[...])
        dgg_ref[...] = (dgla * oh * ggla_t * (sg * (1.0 + gg_t * (1.0 - sg)))).astype(_BF)
        don = dgla * (gg_t * sg)
        sums_ref[1:2, 0:512] += jnp.sum(don * oh, axis=0, keepdims=True)
        doh = don * ggla_t
        do_ref[...] = (rr * (doh - oh * _head_mean(doh * oh, mavg_ref[...]))).astype(_BF)

    return _hosted_call(
        body, (dx1, mix, o, gg, ggla, mavg, wout, gt1, g2), slabs, False,
        name="mix_bwd", grid=(S // TM,),
        in_specs=[_rows(TM, D), _rows(TM, D), _rows(TM, 512, 1), _rows(TM, 512, 1),
                  _full((1, 512)), _full((512, 512)), _full((D, D)), _full((1, D)), _full((1, D))],
        out_specs=[_rows(TM, 512), _rows(TM, 512), _rows(TM, 512), _rows(TM, D), _full((8, D))],
        out_shape=[jax.ShapeDtypeStruct((S, 512), _BF), jax.ShapeDtypeStruct((S, 512), _BF),
                   jax.ShapeDtypeStruct((S, 512), _BF), jax.ShapeDtypeStruct((S, D), _BF),
                   jax.ShapeDtypeStruct((8, D), F32)],
        compiler_params=_cp(("arbitrary",), 40 * 1024 * 1024))


def _ffn(x1, target, gm2, sh2, gt2, g4, wffi, wffo):
    S = x1.shape[0]
    TF = 256

    def body(x_ref, t_ref, gm_ref, sh_ref, gt_ref, g4_ref, wi_hbm, wo_hbm,
             dx_ref, h_ref, du_ref, act_ref, df_ref, sums_ref, loss_ref, wi, wo, sem):
        @pl.when(pl.program_id(0) == 0)
        def _():
            c1 = pltpu.make_async_copy(wi_hbm, wi, sem.at[0])
            c2 = pltpu.make_async_copy(wo_hbm, wo, sem.at[1])
            c1.start()
            c2.start()
            sums_ref[...] = jnp.zeros_like(sums_ref)
            loss_ref[...] = jnp.zeros_like(loss_ref)
            c1.wait()
            c2.wait()

        x = x_ref[...]
        gm = gm_ref[...]
        r = lax.rsqrt(jnp.mean(x * x, axis=-1, keepdims=True) + EPS)
        xh = x * r
        hb = (xh * gm + sh_ref[...]).astype(_BF)
        h_ref[...] = hb
        u = _nt(hb, wi[...])
        g = u[:, 0:FFN]
        up = u[:, FFN:2 * FFN]
        sg = _sigmoid(g)
        sl = g * sg
        ab = (sl * up).astype(_BF)
        act_ref[...] = ab
        f = _nn(ab, wo[...])
        rf = lax.rsqrt(jnp.mean(f * f, axis=-1, keepdims=True) + EPS)
        fh = f * rf
        gt, g4v = gt_ref[...], g4_ref[...]
        err = x + gt * (fh * g4v) - t_ref[...]
        loss_ref[...] += jnp.sum(err * err) * (0.5 / D)
        dout = err * (1.0 / D)
        sums_ref[2:3, :] += jnp.sum(dout * fh, axis=0, keepdims=True)
        dfh = dout * (gt * g4v)
        dfb = (rf * (dfh - fh * jnp.mean(dfh * fh, axis=-1, keepdims=True))).astype(_BF)
        df_ref[...] = dfb
        dact = _nt(dfb, wo[...])
        du_ref[:, 0:FFN] = (dact * up * (sg * (1.0 + g * (1.0 - sg)))).astype(_BF)
        du_ref[:, FFN:2 * FFN] = (dact * sl).astype(_BF)
        dh = _nn(du_ref[...], wi[...])
        sums_ref[0:1, :] += jnp.sum(dh, axis=0, keepdims=True)
        sums_ref[1:2, :] += jnp.sum(dh * xh, axis=0, keepdims=True)
        dxh = dh * gm
        dx_ref[...] = dout + r * (dxh - xh * jnp.mean(dxh * xh, axis=-1, keepdims=True))

    vec = _full((1, D))
    anyspec = pl.BlockSpec(memory_space=pl.ANY)
    return pl.pallas_call(
        body, name="ffn_fwd_bwd", grid=(S // TF,),
        in_specs=[_rows(TF, D), _rows(TF, D), vec, vec, vec, vec, anyspec, anyspec],
        out_specs=[_rows(TF, D), _rows(TF, D), _rows(TF, 2 * FFN), _rows(TF, FFN), _rows(TF, D),
                   _full((8, D)), _full((8, 128))],
        out_shape=[jax.ShapeDtypeStruct((S, D), F32), jax.ShapeDtypeStruct((S, D), _BF),
                   jax.ShapeDtypeStruct((S, 2 * FFN), _BF), jax.ShapeDtypeStruct((S, FFN), _BF),
                   jax.ShapeDtypeStruct((S, D), _BF), jax.ShapeDtypeStruct((8, D), F32),
                   jax.ShapeDtypeStruct((8, 128), F32)],
        scratch_shapes=[pltpu.VMEM((2 * FFN, D), _BF), pltpu.VMEM((FFN, D), _BF), pltpu.SemaphoreType.DMA((2,))],
        compiler_params=_cp(("arbitrary",), VMEM_BIG),
    )(x1, target, gm2, sh2, gt2, g4, wffi, wffo)


def _inproj_bwd(x, ctx, gml, gmc, win, wg, cos, sa, sb, dq, dk, dv, dgq, dgk, dgv, dgg, dlg_f, dlg_b, dx1):
    S = x.shape[0]
    E = S + CTX
    TE = CTX

    def body(x_ref, c_ref, gml_ref, gmc_ref, w_ref, wg_ref, cos_ref, sa_ref, sb_ref, dq_ref, dk_ref, dv_ref,
             gq_ref, gk_ref, gv_ref, dgg_ref, dlf, dlb, dx1_ref, dp_ref, gx_ref, sums_ref):
        i = pl.program_id(0)
        is_ctx = i == 0

        @pl.when(is_ctx)
        def _():
            sums_ref[...] = jnp.zeros_like(sums_ref)

        lat = jnp.where(is_ctx, 0.0, 1.0)
        cos_t, sa_t, sb_t = cos_ref[...], sa_ref[...], sb_ref[...]
        dp_ref[:, O_Q:O_K] = (_unrope(dq_ref[...].astype(F32), cos_t, sa_t, sb_t) * lat).astype(_BF)
        dp_ref[:, O_K:O_V] = _unrope(dk_ref[...].T, cos_t, sa_t, sb_t).astype(_BF)
        dp_ref[:, O_V:O_GQ] = dv_ref[...].T.astype(_BF)
        dp_ref[:, O_GQ:O_GK] = gq_ref[...]
        dp_ref[:, O_GK:O_GV] = gk_ref[...]
        dp_ref[:, O_GV:O_GG] = gv_ref[...]
        dp_ref[:, O_GG:O_Z] = (dgg_ref[...].astype(F32) * lat).astype(_BF)
        dlg = jnp.concatenate([dlf[...], dlb[...]], axis=1)
        dp_ref[:, O_Z:NP] = _nt(dlg, wg_ref[...]).astype(_BF)
        dh = _nn(dp_ref[...], w_ref[...])
        x = jnp.where(is_ctx, c_ref[...], x_ref[...])
        r = lax.rsqrt(jnp.mean(x * x, axis=-1, keepdims=True) + EPS)
        xh = x * r
        sdh = jnp.sum(dh, axis=0, keepdims=True)
        sdx = jnp.sum(dh * xh, axis=0, keepdims=True)
        sums_ref[0:1, :] += sdh * lat
        sums_ref[1:2, :] += sdx * lat
        sums_ref[2:3, :] += sdh * (1.0 - lat)
        sums_ref[3:4, :] += sdx * (1.0 - lat)
        dxh = dh * jnp.where(is_ctx, gmc_ref[...], gml_ref[...])
        gx_ref[...] = dx1_ref[...] + r * (dxh - xh * jnp.mean(dxh * xh, axis=-1, keepdims=True))

    vec = _full((1, D))
    tab = _rows(TE, 128)
    return pl.pallas_call(
        body, name="inproj_bwd", grid=(E // TE,),
        in_specs=[_rows_lat(TE, D), _full((CTX, D)), vec, vec, _full((NP, D)), _full((128, 512)), tab, tab, tab,
                  _rows_lat(TE, QP), pl.BlockSpec((KP, TE), lambda i: (0, i)), pl.BlockSpec((KP, TE), lambda i: (0, i)),
                  _rows(TE, 256), _rows(TE, 256), _rows(TE, 512), _rows_lat(TE, 512), _rows(TE, 256), _rows(TE, 256),
                  _rows_lat(TE, D)],
        out_specs=[_rows(TE, NP), _rows_lat(TE, D), _full((8, D))],
        out_shape=[jax.ShapeDtypeStruct((E, NP), _BF), jax.ShapeDtypeStruct((S, D), F32),
                   jax.ShapeDtypeStruct((8, D), F32)],
        compiler_params=_cp(("arbitrary",), VMEM_BIG),
    )(x, ctx, gml, gmc, win, wg, cos, sa, sb, dq, dk, dv, dgq, dgk, dgv, dgg, dlg_f, dlg_b, dx1)


def _matmul_tn(a, b, tk, tt, name, out_dtype, transpose_out=False, a_cols=None, hosted=(), gather=True):
    T, KA = a.shape
    N = b.shape[1]
    nt = T // tt
    k0 = 0
    if a_cols is not None:
        KA, k0 = tk, a_cols

    def body(a_ref, b_ref, o_ref, acc):
        t = pl.program_id(1)

        @pl.when(t == 0)
        def _():
            acc[...] = jnp.zeros_like(acc)

        acc[...] += _tn(a_ref[...], b_ref[...])

        @pl.when(t == nt - 1)
        def _():
            o_ref[...] = (acc[...].T if transpose_out else acc[...]).astype(out_dtype)

    if transpose_out:
        out_spec, out_shape = pl.BlockSpec((N, tk), lambda i, t: (0, i)), (N, KA)
    else:
        out_spec, out_shape = pl.BlockSpec((tk, N), lambda i, t: (i, 0)), (KA, N)
    res = _hosted_call(
        body, (a, b), hosted, gather, name=name, grid=(KA // tk, nt),
        in_specs=[pl.BlockSpec((tt, tk), lambda i, t: (t, i + k0)), pl.BlockSpec((tt, N), lambda i, t: (t, 0))],
        out_specs=[out_spec], out_shape=[jax.ShapeDtypeStruct(out_shape, out_dtype)],
        scratch_shapes=[pltpu.VMEM((tk, N), F32)],
        compiler_params=_cp(("arbitrary", "arbitrary"), VMEM_BIG))
    return res if hosted else res[0]


def _ada_bwd(c_all, c_ctx, w_ada, d_all):
    n = w_ada.shape[1]

    def body(c_ref, cc_ref, w_ref, d_ref, gw_ref, t_ref):
        c = jnp.concatenate([c_ref[...], jnp.broadcast_to(cc_ref[...], (8, D))], axis=0)
        db = d_ref[...].astype(_BF)
        gw_ref[0] = _tn((c * _sigmoid(c)).astype(_BF), db)
        t_ref[...] = _nt(db[8:16], w_ref[...].astype(_BF))

    return pl.pallas_call(
        body, name="ada_bwd", in_specs=[_full((8, D)), _full((1, D)), _full((D, n)), _full((16, n))],
        out_specs=[_full((1, D, n)), _full((8, D))],
        out_shape=[jax.ShapeDtypeStruct((1, D, n), F32), jax.ShapeDtypeStruct((8, D), F32)], grid=(1,),
        compiler_params=_cp(("arbitrary",)),
    )(c_all, c_ctx, w_ada, d_all)


PART_ROWS = 56
R_ADA, R_ADA_C, R_GAIN, R_SINK, R_BG, R_GGLA, R_LOSS, R_WG = 0, 6, 12, 16, 17, 18, 19, 24


def _small_grads(s_in, s_ffn, s_mix, ada_l, ada_c, gains, dsink, s_bg, g_wg, loss):
    def body(si, sf, sm, al, ac, g, ds, sbg, gwg, loss_ref, o_ref):
        o_ref[...] = jnp.zeros_like(o_ref)
        o_ref[R_LOSS:R_LOSS + 1, 0:128] = loss_ref[0:1, :]
        sub = lax.broadcasted_iota(jnp.int32, (8, 128), 0)
        lane = lax.broadcasted_iota(jnp.int32, (8, 128), 1)
        o_ref[R_SINK:R_SINK + 1, 0:128] = jnp.sum(jnp.where(sub == lane, ds[...], 0.0), axis=0, keepdims=True)
        o_ref[R_BG:R_BG + 1, 0:512] = sbg[0:1, :]
        y = sm[1:2, 0:128] + sm[1:2, 128:256] + sm[1:2, 256:384] + sm[1:2, 384:512]
        y = y + pltpu.roll(y, 64, 1)
        o_ref[R_GGLA:R_GGLA + 1, 0:128] = jnp.where(lane[0:1] < DV, y, 0.0)
        o_ref[R_WG:R_WG + 16, 0:256] = gwg[0:16, 0:256]
        o_ref[R_WG + 16:R_WG + 32, 0:256] = gwg[16:32, 256:512]
        sdh_l, sdx_l, sdh_c, sdx_c = si[0:1], si[1:2], si[2:3], si[3:4]
        sdh2, sdx2, a2 = sf[0:1], sf[1:2], sf[2:3]
        a1 = sm[0:1]
        g1, g2, g3, g4 = g[0:1], g[1:2], g[2:3], g[3:4]
        sc1, gt1, sc2, gt2 = al[1:2], al[2:3], al[4:5], al[5:6]
        sc1c = ac[1:2]
        z = jnp.zeros((1, D), F32)
        rows = [sdh_l, sdx_l * g1, a1 * g2, sdh2, sdx2 * g3, a2 * g4,
                sdh_c, sdx_c * g1, z, z, z, z,
                sdx_l * (1.0 + sc1) + sdx_c * (1.0 + sc1c), a1 * gt1, sdx2 * (1.0 + sc2), a2 * gt2]
        for r, v in enumerate(rows):
            o_ref[r:r + 1, :] = v

    v8 = _full((8, D))
    return pl.pallas_call(
        body, name="small_grads",
        in_specs=[v8] * 6 + [_full((8, 128)), _full((8, 512)), _full((128, 512)), _full((8, 128))],
        out_specs=_full((PART_ROWS, D)), out_shape=jax.ShapeDtypeStruct((PART_ROWS, D), F32), grid=(1,),
        compiler_params=_cp(("arbitrary",)),
    )(s_in, s_ffn, s_mix, ada_l, ada_c, gains, dsink, s_bg, g_wg, loss)


def _row_tile(R):
    for cand in (256, 128, 64, 32, 16):
        if R % cand == 0 and R > cand:
            return cand
    return R


def _adamw(w, gs, m, v, name):
    _, R, C = w.shape
    tr = _row_tile(R)
    c1 = 1.0 / (1.0 - ADAM_B1 ** ADAM_STEP)
    c2 = 1.0 / (1.0 - ADAM_B2 ** ADAM_STEP)
    ng = len(gs)

    def body(w_ref, *refs):
        g_refs, (m_ref, v_ref, go_ref, d_ref, nm_ref, nv_ref) = refs[:ng], refs[ng:]
        c0 = 0
        for g_ref in g_refs:
            cols = slice(c0, c0 + g_ref.shape[2])
            c0 += g_ref.shape[2]
            gg = g_ref[0].astype(F32)
            for j in range(1, g_ref.shape[0]):
                gg = gg + g_ref[j].astype(F32)
            go_ref[0, :, cols] = gg
            nm = ADAM_B1 * m_ref[0, :, cols] + (1.0 - ADAM_B1) * gg
            nv = ADAM_B2 * v_ref[0, :, cols] + (1.0 - ADAM_B2) * (gg * gg)
            nm_ref[0, :, cols] = nm
            nv_ref[0, :, cols] = nv
            d_ref[0, :, cols] = -ADAM_LR * ((nm * c1) / (jnp.sqrt(nv * c2) + ADAM_EPS) + ADAM_WD * w_ref[0, :, cols])

    spec = pl.BlockSpec((1, tr, C), lambda i: (0, i, 0))
    sds = jax.ShapeDtypeStruct((1, R, C), F32)
    g_specs = [pl.BlockSpec((g.shape[0], tr, g.shape[2]), lambda i: (0, i, 0)) for g in gs]
    return pl.pallas_call(
        body, name=name, grid=(R // tr,), in_specs=[spec] + g_specs + [spec, spec], out_specs=[spec] * 4,
        out_shape=[sds] * 4, compiler_params=_cp(("parallel",), 48 * 1024 * 1024),
    )(w, *gs, m, v)


def _sum_slots(slots, name):
    _, R, C = slots.shape
    tr = _row_tile(R)

    def body(s_ref, o_ref):
        acc = s_ref[0].astype(F32)
        for j in range(1, N_DEV):
            acc = acc + s_ref[j].astype(F32)
        o_ref[...] = acc

    return pl.pallas_call(
        body, name=name, grid=(R // tr,), in_specs=[pl.BlockSpec((N_DEV, tr, C), lambda i: (0, i, 0))],
        out_specs=_rows(tr, C), out_shape=jax.ShapeDtypeStruct((R, C), F32), compiler_params=_cp(("parallel",)),
    )(slots)


def _ag2_start(x_ref, out_ref, send_sems, recv_sems, local_sem):
    x, y, c = lax.axis_index("x"), lax.axis_index("y"), lax.axis_index("c")
    me, sibling = (x, y, c), (x, y, 1 - c)
    chips = [(1 - x, y), (x, 1 - y), (1 - x, 1 - y)]

    def rows(px, py, pc):
        return out_ref.at[4 * px + 2 * py + pc]

    def copy(k, block, to, src=None):
        return pltpu.make_async_remote_copy(
            src_ref=rows(*block) if src is None else src, dst_ref=rows(*block),
            send_sem=send_sems.at[k], recv_sem=recv_sems.at[k], device_id=to, device_id_type=MESH)

    mine = pltpu.make_async_copy(x_ref, rows(*me), local_sem)
    mine.start()
    first = [copy(0, me, sibling, src=x_ref)]
    first += [copy(1 + j, me, (*chip, c), src=x_ref) for j, chip in enumerate(chips)]
    for cp in first:
        cp.start()
    return copy, mine, first, me, sibling, chips, c


def _ag2_finish(state):
    copy, mine, first, me, sibling, chips, c = state
    passed = [copy(4 + j, (*chip, c), sibling) for j, chip in enumerate(chips)]
    for j, chip in enumerate(chips):
        copy(1 + j, (*chip, c), me).wait_recv()
        passed[j].start()
    copy(0, sibling, me).wait_recv()
    for j, chip in enumerate(chips):
        copy(4 + j, (*chip, 1 - c), me).wait_recv()
    for cp in first + passed:
        cp.wait_send()
    mine.wait()


def _exchange(arrays, name, gather):
    na = len(arrays)

    def body(*refs):
        cps = _xchg_copies(refs[:na], refs[na:2 * na], *refs[2 * na:], gather=gather)
        _xchg_start(cps)
        _xchg_finish(cps)

    anyspec = pl.BlockSpec(memory_space=pl.ANY)
    return pl.pallas_call(
        body, name=name, out_shape=_xchg_out_shapes(arrays, gather), in_specs=[anyspec] * na,
        out_specs=[anyspec] * na, scratch_shapes=_xchg_scratch(na),
    )(*arrays)


def _entry(c, wg_sh, win_sh, c_ctx, w_ada):
    n = w_ada.shape[1]

    def body(c_ref, g_ref, w_ref, cc_ref, wa_ref, call_ref, gall_ref, wall_ref, ada_ref, part,
             s_send, s_recv, s_loc, w_send, w_recv, w_loc, a_send, a_recv, a_loc):
        big = _ag2_start(w_ref, wall_ref, w_send, w_recv, w_loc)
        small = _xchg_copies([c_ref, g_ref], [call_ref, gall_ref], s_send, s_recv, s_loc, gather=True)
        _xchg_start(small)
        _xchg_finish(small)
        cs = jnp.concatenate([call_ref[:, 0, :], jnp.broadcast_to(cc_ref[...], (8, D))], axis=0)
        part[...] = _nn((cs * _sigmoid(cs)).astype(_BF), wa_ref[...].astype(_BF))
        ada = _xchg_copies([part], [ada_ref], a_send, a_recv, a_loc, gather=True)
        _xchg_start(ada)
        _xchg_finish(ada)
        _ag2_finish(big)

    vm = pl.BlockSpec(memory_space=pltpu.VMEM)
    return pl.pallas_call(
        body, name="entry_gather",
        out_shape=[jax.ShapeDtypeStruct((N_DEV,) + c.shape, F32), jax.ShapeDtypeStruct((N_DEV,) + wg_sh.shape, F32),
                   jax.ShapeDtypeStruct((N_DEV,) + win_sh.shape, win_sh.dtype),
                   jax.ShapeDtypeStruct((N_DEV, 16, n), F32)],
        in_specs=[vm] * 5, out_specs=[vm] * 4,
        scratch_shapes=[pltpu.VMEM((16, n), F32)] + _xchg_scratch(2)
        + [pltpu.SemaphoreType.DMA((7,)), pltpu.SemaphoreType.DMA((7,)), pltpu.SemaphoreType.DMA] + _xchg_scratch(1),
        compiler_params=pltpu.CompilerParams(vmem_limit_bytes=VMEM_BIG),
    )(c, wg_sh, win_sh, c_ctx, w_ada)


def _rope_tables(S):
    t = np.arange(S)
    row = (t // GRID_W).astype(np.float32)
    colp = (t % GRID_W).astype(np.float32)
    half = HD // 2
    inv = (ROPE_BASE ** (-np.arange(0, half, 2, dtype=np.float32) / half)).astype(np.float32)
    ar = row[:, None] * inv[None, :]
    ac = colp[:, None] * inv[None, :]
    ang = np.concatenate([ar, ar, ac, ac], axis=-1).astype(np.float32)
    cos = np.cos(ang).astype(np.float32)
    sin = np.sin(ang).astype(np.float32)
    lane = np.arange(HD)
    first = (lane % 32) < 16
    sa = np.where(first[None, :], -sin, 0.0)
    sb = np.where(first[None, :], 0.0, sin)

    def ext(tab, ctx_val):
        full = np.zeros((CTX + S, 128), np.float32)
        full[:CTX, :] = ctx_val
        full[CTX:, :HD] = tab
        full[CTX:, HD:] = tab
        return jnp.asarray(full)

    return ext(cos, 1.0), ext(sa, 0.0), ext(sb, 0.0)


def _pad_rows_win(wt):
    return jnp.pad(wt, ((0, NP - IN_COLS), (0, 0)))


def _unpad_rows_win(g):
    return g[0:IN_COLS]


def _local_step(x, ctx, target, ada_l, ada_c, gains, sink, win_p, wg_bd, bg, ggla, wout_sh, wffi_sh, wffo_sh):
    S = x.shape[0]
    cos, sa, sb = _rope_tables(S)
    g1, g2, g3, g4 = (gains[i:i + 1] for i in range(4))
    sh1, sc1, gt1, sh2, sc2, gt2 = (ada_l[i:i + 1] for i in range(6))
    sh1c, sc1c = ada_c[0:1], ada_c[1:2]
    gml, gmc, gm2 = g1 * (1.0 + sc1), g1 * (1.0 + sc1c), g3 * (1.0 + sc2)
    mavg = jnp.asarray(np.kron(np.eye(N_GLA, dtype=np.float32), np.full((DV, DV), 1.0 / DV, np.float32))).astype(_BF)

    n_ffi, r_ffo, r_out = wffi_sh.shape[0], wffo_sh.shape[0], wout_sh.shape[0]
    tt_e = 768 if (S + CTX) % 768 == 0 else 256
    tt_s = 512 if S % 512 == 0 else 256
    h, q, k, v, gq, gk, gv, gg, z, la, wout_g = _inproj_fwd(x, ctx, gml, sh1, gmc, sh1c, win_p, wg_bd, bg,
                                                            cos, sa, sb, [wout_sh])
    attn, lse, probs, wffi_g = _attn_fwd(q, k, v, sink, [wffi_sh])
    o_f, st_f, wffo_g = _gla_fwd(gq, gk, gv, la, False, [wffo_sh])
    o, st_b = _gla_fwd(gq, gk, gv, la, True, (), o_f)
    wout = wout_g.reshape(N_DEV * r_out, D)
    wffi = wffi_g.reshape(N_DEV * n_ffi, D)
    wffo = wffo_g.reshape(N_DEV * r_ffo, D)
    x1, mix = _mix_fwd(x, attn, o, gg, ggla, mavg, wout, gt1, g2)
    dx1, h2, du, act, df, s_ffn, loss = _ffn(x1, target, gm2, sh2, gt2, g4, wffi, wffo)
    slab_ffi = _matmul_tn(h2, du, 512, tt_s, "grad_w_ffn_in", _BF, True).reshape(N_DEV, n_ffi, D)
    slab_ffo = _matmul_tn(act, df, FFN, tt_s, "grad_w_ffn_out", _BF).reshape(N_DEV, r_ffo, D)
    d_attn, do_gla, dgg, dy, s_mix, got_ffo = _mix_bwd(dx1, mix, o, gg, ggla, mavg, wout, gt1, g2, [slab_ffo])
    slab_out = _matmul_tn(mix, dy, D, tt_s, "grad_w_out", _BF).reshape(N_DEV, r_out, D)
    dq, dk, dv, dsink, got_ffi = _attn_bwd(q, k, v, sink, probs, lse, d_attn, [slab_ffi])
    pq, pk, pv, dlg_f, gwg_f, sbg_f, got_out = _gla_bwd(gq, gk, gv, la, z, st_f, do_gla, False, None, [slab_out])
    dgq, dgk, dgv, dlg_b, gwg_b, sbg_b = _gla_bwd(gq, gk, gv, la, z, st_b, do_gla, True, (pq, pk, pv))
    dp, grad_x, s_in = _inproj_bwd(x, ctx, gml, gmc, win_p, wg_bd, cos, sa, sb, dq, dk, dv,
                                   dgq, dgk, dgv, dgg, dlg_f, dlg_b, dx1)
    g_wg = jnp.concatenate([gwg_f, gwg_b], axis=1)
    s_bg = jnp.concatenate([sbg_f, sbg_b], axis=1)
    small = _small_grads(s_in, s_ffn, s_mix, ada_l, ada_c, gains, dsink, s_bg, g_wg, loss)
    n_in, n_grp = IN_COLS // N_DEV, 2
    got_in, slab = [], None
    for j in range(n_grp):
        g_j, got = _matmul_tn(h, dp, D // n_grp, tt_e, "grad_w_in_%d" % j, _BF, True, a_cols=j,
                              hosted=[small] if j == 0 else [slab], gather=(j == 0))
        if j == 0:
            parts = got
        else:
            got_in.append(got)
        slab = _unpad_rows_win(g_j).reshape(N_DEV, n_in, D // n_grp)
    got_in.append(_exchange([slab], "scatter_grads", False)[0])
    return dict(grad_x=grad_x, got_in=got_in, got_out=got_out, got_ffi=got_ffi, got_ffo=got_ffo, parts=parts)


SMALL_NAMES = ["c_ctx", "b_ada", "g_pre_mix", "g_post_mix", "g_pre_ffn", "g_post_ffn", "attn_sink",
               "b_gate_fwd", "b_gate_bwd", "g_gla_norm", "w_gate_fwd", "w_gate_bwd"]


def _small_update(tot, t_tot, wg_g, w, m, v):
    c1 = 1.0 / (1.0 - ADAM_B1 ** ADAM_STEP)
    c2 = 1.0 / (1.0 - ADAM_B2 ** ADAM_STEP)
    n = len(SMALL_NAMES)

    def body(tot_ref, t_ref, wg_ref, *refs):
        w_r, m_r, v_r = refs[0:n], refs[n:2 * n], refs[2 * n:3 * n]
        g_o, d_o, nm_o, nv_o = refs[3 * n:4 * n], refs[4 * n:5 * n], refs[5 * n:6 * n], refs[6 * n:7 * n]

        def upd(i, idx, g):
            nm = ADAM_B1 * m_r[i][idx] + (1.0 - ADAM_B1) * g
            nv = ADAM_B2 * v_r[i][idx] + (1.0 - ADAM_B2) * (g * g)
            g_o[i][idx] = g
            nm_o[i][idx] = nm
            nv_o[i][idx] = nv
            d_o[i][idx] = -ADAM_LR * ((nm * c1) / (jnp.sqrt(nv * c2) + ADAM_EPS) + ADAM_WD * w_r[i][idx])

        everything = (slice(None), slice(None))
        cc = w_r[0][...]
        sc = _sigmoid(cc)
        upd(0, everything, t_ref[0:1, :] * (sc * (1.0 + cc * (1.0 - sc))))
        for j in range(6):
            upd(1, (slice(None), slice(D * j, D * j + D)),
                tot_ref[R_ADA + j:R_ADA + j + 1, :] + tot_ref[R_ADA_C + j:R_ADA_C + j + 1, :])
        for j in range(4):
            upd(2 + j, everything, tot_ref[R_GAIN + j:R_GAIN + j + 1, :])
        upd(6, everything, tot_ref[R_SINK:R_SINK + 1, 0:N_ATT])
        upd(7, everything, tot_ref[R_BG:R_BG + 1, 0:256])
        upd(8, everything, tot_ref[R_BG:R_BG + 1, 256:512])
        upd(9, everything, tot_ref[R_GGLA:R_GGLA + 1, 0:DV])
        upd(10, (0,), wg_ref[0:GATE_RANK, :])
        upd(11, (0,), wg_ref[GATE_RANK:2 * GATE_RANK, :])

    params = [w[k] for k in SMALL_NAMES] + [m[k] for k in SMALL_NAMES] + [v[k] for k in SMALL_NAMES]
    outs = pl.pallas_call(
        body, name="small_update", grid=(1,),
        in_specs=[_full(tot.shape), _full(t_tot.shape), _full(wg_g.shape)] + [_full(p.shape) for p in params],
        out_specs=[_full(w[k].shape) for k in SMALL_NAMES] * 4,
        out_shape=[jax.ShapeDtypeStruct(w[k].shape, F32) for k in SMALL_NAMES] * 4,
        compiler_params=_cp(("arbitrary",)),
    )(tot, t_tot, wg_g, *params)
    return tuple(dict(zip(SMALL_NAMES, outs[i * n:(i + 1) * n])) for i in range(4))


def kernel(x, c, ctx, c_ctx, w_ada, b_ada, g_pre_mix, g_post_mix, g_pre_ffn, g_post_ffn, w_in, attn_sink, w_gate_fwd, b_gate_fwd, w_gate_bwd, b_gate_bwd, g_gla_norm, w_out, w_ffn_in, w_ffn_out, loss_target, m_c_ctx, m_w_ada, m_b_ada, m_g_pre_mix, m_g_post_mix, m_g_pre_ffn, m_g_post_ffn, m_w_in, m_attn_sink, m_w_gate_fwd, m_b_gate_fwd, m_w_gate_bwd, m_b_gate_bwd, m_g_gla_norm, m_w_out, m_w_ffn_in, m_w_ffn_out, v_c_ctx, v_w_ada, v_b_ada, v_g_pre_mix, v_g_post_mix, v_g_pre_ffn, v_g_post_ffn, v_w_in, v_attn_sink, v_w_gate_fwd, v_b_gate_fwd, v_w_gate_bwd, v_b_gate_bwd, v_g_gla_norm, v_w_out, v_w_ffn_in, v_w_ffn_out):
    me = 4 * lax.axis_index("x") + 2 * lax.axis_index("y") + lax.axis_index("c")
    S = x.shape[1]
    n_in = w_in.shape[2]
    n_ffi = w_ffn_in.shape[2]
    r_out = w_out.shape[1]
    r_ffo = w_ffn_out.shape[1]
    n_ada = w_ada.shape[2]

    wg_sh = jnp.concatenate([w_gate_fwd.reshape(4, 128), w_gate_bwd.reshape(4, 128)], axis=0)
    c_all3, g_all, w_all, ada_all = _entry(c, wg_sh, w_in[0].T.astype(_BF), c_ctx.reshape(1, D), w_ada[0])
    c_all = c_all3.reshape(N_DEV, D)
    wgf = g_all[:, 0:4].reshape(N_DEV, GATE_RANK, 32).transpose(1, 0, 2).reshape(GATE_RANK, 256)
    wgb = g_all[:, 4:8].reshape(N_DEV, GATE_RANK, 32).transpose(1, 0, 2).reshape(GATE_RANK, 256)
    win_p = _pad_rows_win(w_all.reshape(N_DEV * n_in, D))
    wg_bd = jnp.zeros((128, 512), F32).at[0:16, 0:256].set(wgf).at[16:32, 256:512].set(wgb).astype(_BF)
    ada_full = ada_all.transpose(1, 0, 2).reshape(16, N_DEV * n_ada) + b_ada
    ada_l = jnp.pad(lax.dynamic_slice_in_dim(ada_full, me, 1, 0).reshape(6, D), ((0, 2), (0, 0)))
    ada_c = jnp.pad(ada_full[8].reshape(6, D), ((0, 2), (0, 0)))
    gains = jnp.pad(jnp.concatenate([g_pre_mix, g_post_mix, g_pre_ffn, g_post_ffn], axis=0), ((0, 4), (0, 0)))
    sink = jnp.broadcast_to(attn_sink.reshape(8, 1), (8, 128))
    bg = jnp.concatenate([b_gate_fwd, b_gate_bwd], axis=1)
    ggla = jnp.tile(g_gla_norm, (1, N_GLA))

    r = _local_step(x[0], ctx[0], loss_target[0], ada_l, ada_c, gains, sink, win_p, wg_bd, bg, ggla,
                    w_out[0].astype(_BF), w_ffn_in[0].T.astype(_BF), w_ffn_out[0].astype(_BF))

    parts = r["parts"]
    tot = _sum_slots(parts, "sum_small_grads")
    loss = tot[R_LOSS, 0]
    d_ada_rows = parts[:, R_ADA:R_ADA + 6].reshape(N_DEV, 6 * D)
    d_ada_c = tot[R_ADA_C:R_ADA_C + 6].reshape(1, 6 * D)
    my_cols = lax.dynamic_slice_in_dim(jnp.concatenate([d_ada_rows, jnp.broadcast_to(d_ada_c, (1, 6 * D)),
                                                        jnp.zeros((7, 6 * D), F32)], axis=0), me * n_ada, n_ada, 1)
    grad_w_ada, t_part = _ada_bwd(c_all, c_ctx.reshape(1, D), w_ada[0], my_cols)
    wg_g = lax.dynamic_slice(tot, (R_WG, me * 32), (2 * GATE_RANK, 32))

    tr = lambda a: jnp.transpose(a, (0, 2, 1))
    big = {}
    t_all, = _exchange([t_part], "gather_c_ctx", True)
    t_tot = _sum_slots(t_all, "sum_c_ctx")
    for nm, w, g, m, v in [("w_ada", w_ada, grad_w_ada, m_w_ada, v_w_ada),
                           ("w_out", w_out, r["got_out"], m_w_out, v_w_out),
                           ("w_ffn_out", w_ffn_out, r["got_ffo"], m_w_ffn_out, v_w_ffn_out)]:
        big[nm] = _adamw(w, [g], m, v, "adamw_" + nm)
    big["w_ffn_in"] = tuple(tr(o) for o in _adamw(tr(w_ffn_in), [r["got_ffi"]], tr(m_w_ffn_in), tr(v_w_ffn_in),
                                                  "adamw_w_ffn_in"))
    big["w_in"] = tuple(tr(o) for o in _adamw(tr(w_in), r["got_in"], tr(m_w_in), tr(v_w_in), "adamw_w_in"))

    w_small = dict(c_ctx=c_ctx.reshape(1, D), b_ada=b_ada, g_pre_mix=g_pre_mix, g_post_mix=g_post_mix, g_pre_ffn=g_pre_ffn,
                   g_post_ffn=g_post_ffn, attn_sink=attn_sink, b_gate_fwd=b_gate_fwd, b_gate_bwd=b_gate_bwd,
                   g_gla_norm=g_gla_norm, w_gate_fwd=w_gate_fwd, w_gate_bwd=w_gate_bwd)
    m_small = dict(c_ctx=m_c_ctx.reshape(1, D), b_ada=m_b_ada, g_pre_mix=m_g_pre_mix, g_post_mix=m_g_post_mix,
                   g_pre_ffn=m_g_pre_ffn, g_post_ffn=m_g_post_ffn, attn_sink=m_attn_sink, b_gate_fwd=m_b_gate_fwd,
                   b_gate_bwd=m_b_gate_bwd, g_gla_norm=m_g_gla_norm, w_gate_fwd=m_w_gate_fwd, w_gate_bwd=m_w_gate_bwd)
    v_small = dict(c_ctx=v_c_ctx.reshape(1, D), b_ada=v_b_ada, g_pre_mix=v_g_pre_mix, g_post_mix=v_g_post_mix,
                   g_pre_ffn=v_g_pre_ffn, g_post_ffn=v_g_post_ffn, attn_sink=v_attn_sink, b_gate_fwd=v_b_gate_fwd,
                   b_gate_bwd=v_b_gate_bwd, g_gla_norm=v_g_gla_norm, w_gate_fwd=v_w_gate_fwd, w_gate_bwd=v_w_gate_bwd)
    grads_small, d_s, nm_s, nv_s = _small_update(tot, t_tot, wg_g, w_small, m_small, v_small)
    for dd in (grads_small, d_s, nm_s, nv_s):
        dd["c_ctx"] = dd["c_ctx"].reshape(D)

    order = ["c_ctx", "w_ada", "b_ada", "g_pre_mix", "g_post_mix", "g_pre_ffn", "g_post_ffn", "w_in", "attn_sink",
             "w_gate_fwd", "b_gate_fwd", "w_gate_bwd", "b_gate_bwd", "g_gla_norm", "w_out", "w_ffn_in", "w_ffn_out"]
    grads, deltas, new_m, new_v = [], [], [], []
    for nm in order:
        if nm in big:
            g_, d_, m_, v_ = big[nm]
        else:
            g_, d_, m_, v_ = grads_small[nm], d_s[nm], nm_s[nm], nv_s[nm]
        grads.append(g_)
        deltas.append(d_)
        new_m.append(m_)
        new_v.append(v_)
    return (loss, r["grad_x"][None], *grads, *deltas, *new_m, *new_v)
```

```python
import functools
import math

import numpy as np
import jax
import jax.numpy as jnp
from jax import lax
from jax.experimental import pallas as pl
from jax.experimental.pallas import tpu as pltpu

F32 = jnp.float32
_BF = jnp.bfloat16

N_DEV = 8
D = 1024
CTX = 256
HD = 64
N_ATT = 8
N_KV = 2
GRP = N_ATT // N_KV
WIN = 128
GRID_W = 64
ROPE_BASE = 10000.0
N_GLA = 8
DK = 32
DV = 64
GATE_RANK = 16
GATE_TAU = 16.0
FFN = 2816
EPS = 1e-6
NEG = -1e30
GLA_T = 128

QP = N_ATT * HD
KP = N_KV * HD
O_Q, O_K, O_V = 0, QP, QP + KP
O_GQ = O_V + KP
O_GK = O_GQ + N_GLA * DK
O_GV = O_GK + N_GLA * DK
O_GG = O_GV + N_GLA * DV
O_Z = O_GG + N_GLA * DV
NP = O_Z + 128
IN_COLS = 2336

ADAM_LR, ADAM_B1, ADAM_B2, ADAM_EPS, ADAM_WD, ADAM_STEP = 0.001, 0.9, 0.999, 1e-08, 0.01, 10

VMEM_BIG = 56 * 1024 * 1024
MESH = pl.DeviceIdType.MESH


def _cp(sem, vmem=None):
    return pltpu.CompilerParams(dimension_semantics=sem, vmem_limit_bytes=vmem)


def _full(shape):
    nd = len(shape)
    return pl.BlockSpec(shape, lambda *a: (0,) * nd)


def _rows(tile, width, off=0):
    return pl.BlockSpec((tile, width), lambda i: (i + off, 0))


def _rows_lat(tile, width):
    return pl.BlockSpec((tile, width), lambda i: (jnp.maximum(i - 1, 0), 0))


def _nt(a, b):
    return lax.dot_general(a, b, (((1,), (1,)), ((), ())), preferred_element_type=F32)


def _tn(a, b):
    return lax.dot_general(a, b, (((0,), (0,)), ((), ())), preferred_element_type=F32)


def _nn(a, b):
    return jnp.dot(a, b, preferred_element_type=F32)


def _head_mean(x, mavg):
    n = x.shape[0]
    hi = x.astype(_BF)
    lo = (x - hi.astype(F32)).astype(_BF)
    y = _nn(jnp.concatenate([hi, lo], axis=0), mavg)
    return y[0:n] + y[n:2 * n]


def _rope(t, cos, sa, sb):
    n = t.shape[1]
    reps = n // 128
    c = jnp.tile(cos, (1, reps))
    a = jnp.tile(sa, (1, reps))
    b = jnp.tile(sb, (1, reps))
    return t * c + pltpu.roll(t, n - 16, 1) * a + pltpu.roll(t, 16, 1) * b


def _unrope(t, cos, sa, sb):
    n = t.shape[1]
    reps = n // 128
    c = jnp.tile(cos, (1, reps))
    a = jnp.tile(sa, (1, reps))
    b = jnp.tile(sb, (1, reps))
    return t * c + pltpu.roll(t * a, 16, 1) + pltpu.roll(t * b, n - 16, 1)


def _sigmoid(x):
    return 1.0 / (1.0 + jnp.exp(-x))


def _inproj_fwd(x, ctx, gml, shl, gmc, shc, win, wg, bg, cos, sa, sb, shards):
    E = x.shape[0] + CTX
    TE = CTX

    def body(x_ref, c_ref, gml_ref, shl_ref, gmc_ref, shc_ref, w_ref, wg_ref, bg_ref, cos_ref, sa_ref, sb_ref,
             h_ref, q_ref, k_ref, v_ref, gq_ref, gk_ref, gv_ref, gg_ref, z_ref, la_ref):
        is_ctx = pl.program_id(0) == 0
        gm = jnp.where(is_ctx, gmc_ref[...], gml_ref[...])
        sh = jnp.where(is_ctx, shc_ref[...], shl_ref[...])
        x = jnp.where(is_ctx, c_ref[...], x_ref[...])
        r = lax.rsqrt(jnp.mean(x * x, axis=-1, keepdims=True) + EPS)
        hb = ((x * r) * gm + sh).astype(_BF)
        h_ref[...] = hb
        p = _nt(hb, w_ref[...])
        cos_t, sa_t, sb_t = cos_ref[...], sa_ref[...], sb_ref[...]
        q_ref[...] = (_rope(p[:, O_Q:O_K], cos_t, sa_t, sb_t) * (HD ** -0.5)).astype(_BF)
        k_ref[...] = _rope(p[:, O_K:O_V], cos_t, sa_t, sb_t).astype(_BF)
        v_ref[...] = p[:, O_V:O_GQ].astype(_BF)
        gq_ref[...] = p[:, O_GQ:O_GK] * (DK ** -0.5)
        gk_ref[...] = p[:, O_GK:O_GV]
        gv_ref[...] = p[:, O_GV:O_GG]
        gg_ref[...] = p[:, O_GG:O_Z]
        zb = p[:, O_Z:NP].astype(_BF)
        z_ref[...] = zb
        lg = _nn(zb, wg_ref[...]) + bg_ref[...]
        la_ref[...] = (jnp.minimum(lg, 0.0) - jnp.log(1.0 + jnp.exp(-jnp.abs(lg)))) * (1.0 / GATE_TAU)

    vec = _full((1, D))
    tab = _rows(TE, 128)
    outs = [(D, _BF), (QP, _BF), (KP, _BF), (KP, _BF), (256, F32), (256, F32), (512, F32), (512, F32),
            (128, _BF), (512, F32)]
    return _hosted_call(
        body, (x, ctx, gml, shl, gmc, shc, win, wg, bg, cos, sa, sb), shards, True,
        name="inproj_fwd", grid=(E // TE,),
        in_specs=[_rows_lat(TE, D), _full((CTX, D)), vec, vec, vec, vec, _full((NP, D)), _full((128, 512)),
                  _full((1, 512)), tab, tab, tab],
        out_specs=[_rows(TE, w) for w, _ in outs],
        out_shape=[jax.ShapeDtypeStruct((E, w), dt) for w, dt in outs],
        compiler_params=_cp(("arbitrary",), 40 * 1024 * 1024))


def _xchg_scratch(na):
    return [pltpu.SemaphoreType.DMA((na, N_DEV - 1)), pltpu.SemaphoreType.DMA((na, N_DEV - 1)),
            pltpu.SemaphoreType.DMA((na,))]


def _xchg_copies(ins, outs, send_sems, recv_sems, local_sems, gather, rows=None):
    x, y, c = lax.axis_index("x"), lax.axis_index("y"), lax.axis_index("c")
    me = 4 * x + 2 * y + c

    def slab(a, j):
        if gather:
            return ins[a]
        ref = ins[a].at[j]
        return ref if rows is None or rows[a] is None else ref.at[pl.ds(rows[a][0], rows[a][1])]

    local, sends, recvs = [], [], []
    for a in range(len(ins)):
        local.append(pltpu.make_async_copy(slab(a, me), outs[a].at[me], local_sems.at[a]))
    for k in range(1, N_DEV):
        px, py, pc = x ^ (k >> 2), y ^ ((k >> 1) & 1), c ^ (k & 1)
        peer = 4 * px + 2 * py + pc
        for a in range(len(ins)):
            sems = dict(send_sem=send_sems.at[a, k - 1], recv_sem=recv_sems.at[a, k - 1], device_id_type=MESH)
            sends.append(pltpu.make_async_remote_copy(
                src_ref=slab(a, peer), dst_ref=outs[a].at[me], device_id=(px, py, pc), **sems))
            recvs.append(pltpu.make_async_remote_copy(
                src_ref=slab(a, me), dst_ref=outs[a].at[peer], device_id=(x, y, c), **sems))
    return local, sends, recvs


def _xchg_start(cps):
    local, sends, _ = cps
    for cp in local + sends:
        cp.start()


def _xchg_finish(cps):
    local, sends, recvs = cps
    for cp in recvs:
        cp.wait_recv()
    for cp in sends:
        cp.wait_send()
    for cp in local:
        cp.wait()


def _xchg_out_shapes(ins, gather, rows=None):
    out = []
    for a, s in enumerate(ins):
        shape = ((N_DEV,) + s.shape) if gather else s.shape
        if rows is not None and rows[a] is not None:
            shape = (shape[0], rows[a][1]) + shape[2:]
        out.append(jax.ShapeDtypeStruct(shape, s.dtype))
    return out


def _hosted_call(body, args, hosted, gather, *, grid, in_specs, out_specs, out_shape, scratch_shapes=(), **kw):
    na = len(hosted)
    if na == 0:
        return pl.pallas_call(body, grid=grid, in_specs=in_specs, out_specs=out_specs, out_shape=out_shape,
                              scratch_shapes=list(scratch_shapes), **kw)(*args)
    rows = [h[1] if isinstance(h, tuple) else None for h in hosted]
    hosted = [h[0] if isinstance(h, tuple) else h for h in hosted]
    n_in, n_out, n_scr = len(in_specs), len(out_specs), len(scratch_shapes)

    def wrapped(*refs):
        ins, h_in = refs[:n_in], refs[n_in:n_in + na]
        outs, h_out = refs[n_in + na:n_in + na + n_out], refs[n_in + na + n_out:n_in + 2 * na + n_out]
        scr = refs[n_in + 2 * na + n_out:]
        cps = _xchg_copies(h_in, h_out, *scr[n_scr:], gather=gather, rows=rows)
        pids = [pl.program_id(a) for a in range(len(grid))]
        first = functools.reduce(jnp.logical_and, [p == 0 for p in pids])
        last = functools.reduce(jnp.logical_and, [p == g - 1 for p, g in zip(pids, grid)])

        @pl.when(first)
        def _():
            _xchg_start(cps)

        body(*ins, *outs, *scr[:n_scr])

        @pl.when(last)
        def _():
            _xchg_finish(cps)

    anyspec = pl.BlockSpec(memory_space=pl.ANY)
    return pl.pallas_call(
        wrapped, grid=grid, in_specs=list(in_specs) + [anyspec] * na, out_specs=list(out_specs) + [anyspec] * na,
        out_shape=list(out_shape) + _xchg_out_shapes(hosted, gather, rows),
        scratch_shapes=list(scratch_shapes) + _xchg_scratch(na), **kw)(*args, *hosted)


def _attn_specs(E):
    nb = (E - CTX) // WIN
    last = E // WIN - 1
    kc = pl.BlockSpec((CTX, KP), lambda n: (0, 0))
    kp = pl.BlockSpec((WIN, KP), lambda n: (n + 1, 0))
    kk = pl.BlockSpec((WIN, KP), lambda n: (n + 2, 0))
    kn = pl.BlockSpec((WIN, KP), lambda n: (jnp.minimum(n + 3, last), 0))
    return nb, [kc, kp, kk, kn]


def _attn_bias(nb):
    rows = np.arange(GRP * WIN)[:, None] % WIN
    cols = np.arange(CTX + 3 * WIN)[None, :]
    j = cols - CTX
    band = np.abs(j - WIN - rows) <= WIN
    out = []
    for first, last in ((True, False), (False, False), (False, True)):
        ok = (cols < CTX) | (band & ((j >= WIN) | (not first)) & ((j < 2 * WIN) | (not last)))
        out.append(np.where(ok, 0.0, NEG).astype(np.float32))
    bias = jnp.asarray(np.stack(out))
    spec = pl.BlockSpec((1, GRP * WIN, CTX + 3 * WIN),
                        lambda n: (jnp.where(n == 0, 0, jnp.where(n == nb - 1, 2, 1)), 0, 0))
    return bias, spec


def _both_halves(t, h):
    tf = t.astype(F32)
    r = pltpu.roll(tf, HD, 1)
    lo = lax.broadcasted_iota(jnp.int32, tf.shape, 1) < HD
    return (jnp.where(lo, tf, r) if h == 0 else jnp.where(lo, r, tf)).astype(t.dtype)


def _stack_heads(ref, h):
    lo = lax.broadcasted_iota(jnp.int32, (WIN, 128), 1) < HD
    parts = []
    for g in range(GRP):
        j = GRP * h + g
        t = ref[:, 128 * (j // 2):128 * (j // 2) + 128].astype(F32)
        parts.append(jnp.where(lo if j % 2 == 0 else jnp.logical_not(lo), t, 0.0))
    return jnp.concatenate(parts, axis=0)


def _unstack_pair(o, pp):
    lo = lax.broadcasted_iota(jnp.int32, (WIN, 128), 1) < HD
    return jnp.where(lo, o[WIN * 2 * pp:WIN * 2 * pp + WIN], o[WIN * (2 * pp + 1):WIN * (2 * pp + 1) + WIN])


def _attn_fwd(q, k, v, sink, shards):
    E = q.shape[0]
    S = E - CTX
    nb, kspecs = _attn_specs(E)
    na = len(shards)

    def body(q_ref, kc, kp, kk, kn, vc, vp, vk, vn, sink_ref, bias_ref, *rest):
        shard_refs, (o_ref, lse_ref, p_ref), got_refs = rest[:na], rest[na:na + 3], rest[na + 3:2 * na + 3]
        n = pl.program_id(0)
        cps = _xchg_copies(shard_refs, got_refs, *rest[2 * na + 3:], gather=True)

        @pl.when(n == 0)
        def _():
            _xchg_start(cps)

        lane = lax.broadcasted_iota(jnp.int32, (WIN, 128), 1)
        lse_t = jnp.zeros((WIN, 128), F32)
        kall = jnp.concatenate([kc[...], kp[...], kk[...], kn[...]], axis=0)
        vall = jnp.concatenate([vc[...], vp[...], vk[...], vn[...]], axis=0)
        K = [_both_halves(kall, h) for h in range(N_KV)]
        Q = [_stack_heads(q_ref, h).astype(_BF) for h in range(N_KV)]
        sk = [jnp.concatenate([jnp.broadcast_to(sink_ref[GRP * h + g:GRP * h + g + 1, 0:1], (WIN, 1))
                               for g in range(GRP)], axis=0) for h in range(N_KV)]
        s = [_nt(Q[h], K[h]) + bias_ref[0] for h in range(N_KV)]
        m = [jnp.maximum(jnp.max(s[h], axis=1, keepdims=True), sk[h]) for h in range(N_KV)]
        e = [jnp.exp(s[h] - m[h]) for h in range(N_KV)]
        den = [jnp.sum(e[h], axis=1, keepdims=True) + jnp.exp(sk[h] - m[h]) for h in range(N_KV)]
        V = [_both_halves(vall, h) for h in range(N_KV)]
        pb = [(e[h] * (1.0 / den[h])).astype(_BF) for h in range(N_KV)]
        for h in range(N_KV):
            p_ref[GRP * WIN * h:GRP * WIN * (h + 1), :] = pb[h]
        o = [_nn(pb[h], V[h]) for h in range(N_KV)]
        for h in range(N_KV):
            lse = m[h] + jnp.log(den[h])
            for g in range(GRP):
                lse_t = jnp.where(lane == GRP * h + g, lse[WIN * g:WIN * g + WIN], lse_t)
            for pp in range(GRP // 2):
                t = 2 * h + pp
                o_ref[:, 128 * t:128 * t + 128] = _unstack_pair(o[h], pp).astype(_BF)
        lse_ref[...] = lse_t

        @pl.when(n == nb - 1)
        def _():
            _xchg_finish(cps)

    qs = pl.BlockSpec((WIN, QP), lambda n: (n + 2, 0))
    anyspec = pl.BlockSpec(memory_space=pl.ANY)
    bias, bias_spec = _attn_bias(nb)
    return pl.pallas_call(
        body, name="attn_fwd", grid=(nb,),
        in_specs=[qs] + kspecs + kspecs + [_full((8, 128)), bias_spec] + [anyspec] * na,
        out_specs=[_rows(WIN, 512), _rows(WIN, 128), _rows(N_KV * GRP * WIN, CTX + 3 * WIN)] + [anyspec] * na,
        out_shape=[jax.ShapeDtypeStruct((S, 512), _BF), jax.ShapeDtypeStruct((S, 128), F32),
                   jax.ShapeDtypeStruct((nb * N_KV * GRP * WIN, CTX + 3 * WIN), _BF)]
        + _xchg_out_shapes(shards, True),
        scratch_shapes=_xchg_scratch(na),
        compiler_params=_cp(("arbitrary",)),
    )(q, k, k, k, k, v, v, v, v, sink, bias, *shards)


def _attn_bwd(q, k, v, sink, probs, lse, d_attn, slabs):
    E = q.shape[0]
    S = E - CTX
    nb, kspecs = _attn_specs(E)
    last = E // WIN - 1
    na = len(slabs)
    rows = [h[1] if isinstance(h, tuple) else None for h in slabs]
    slabs = [h[0] if isinstance(h, tuple) else h for h in slabs]

    def body(q_ref, kc, kp, kk, kn, vc, vp, vk, vn, sink_ref, p_ref, lse_ref, do_ref, *rest):
        slab_refs, (dq_ref, dk_ref, dv_ref, ds_ref), got_refs = rest[:na], rest[na:na + 4], rest[na + 4:2 * na + 4]
        n = pl.program_id(0)
        cps = _xchg_copies(slab_refs, got_refs, *rest[2 * na + 4:], gather=False, rows=rows)

        @pl.when(n == 0)
        def _():
            _xchg_start(cps)
            dk_ref[...] = jnp.zeros_like(dk_ref)
            dv_ref[...] = jnp.zeros_like(dv_ref)
            ds_ref[...] = jnp.zeros_like(ds_ref)

        lane = lax.broadcasted_iota(jnp.int32, (WIN, 128), 1)
        lse_t = lse_ref[...]
        starts = [None, pl.multiple_of((n + 1) * WIN, WIN), pl.multiple_of((n + 2) * WIN, WIN),
                  pl.multiple_of(jnp.minimum(n + 3, last) * WIN, WIN)]
        kall = jnp.concatenate([kc[...], kp[...], kk[...], kn[...]], axis=0)
        vall = jnp.concatenate([vc[...], vp[...], vk[...], vn[...]], axis=0)
        for h in range(N_KV):
            hs = slice(HD * h, HD * h + HD)
            K = _both_halves(kall, h)
            V = _both_halves(vall, h)
            Q = _stack_heads(q_ref, h).astype(_BF)
            sk = jnp.concatenate([jnp.broadcast_to(sink_ref[GRP * h + g:GRP * h + g + 1, 0:1], (WIN, 1))
                                  for g in range(GRP)], axis=0)
            ls = jnp.concatenate([jnp.sum(jnp.where(lane == GRP * h + g, lse_t, 0.0), axis=1, keepdims=True)
                                  for g in range(GRP)], axis=0)
            do = _stack_heads(do_ref, h).astype(_BF)
            pb = p_ref[GRP * WIN * h:GRP * WIN * (h + 1), :]
            p = pb.astype(F32)
            dp = _nt(do, V)
            delta = jnp.sum(p * dp, axis=1, keepdims=True)
            dsc = (p * (dp - delta)).astype(_BF)
            dq = _nn(dsc, K) * (HD ** -0.5)
            for pp in range(GRP // 2):
                t = 2 * h + pp
                dq_ref[:, 128 * t:128 * t + 128] = _unstack_pair(dq, pp).astype(_BF)
            dK2 = _tn(Q, dsc)
            dV2 = _tn(do, pb)
            dK = dK2[0:HD] + dK2[HD:2 * HD]
            dV = dV2[0:HD] + dV2[HD:2 * HD]
            dk_ref[hs, 0:CTX] += dK[:, 0:CTX]
            dv_ref[hs, 0:CTX] += dV[:, 0:CTX]
            for w in range(1, 4):
                lo = CTX + WIN * (w - 1)
                dk_ref[hs, pl.ds(starts[w], WIN)] += dK[:, lo:lo + WIN]
                dv_ref[hs, pl.ds(starts[w], WIN)] += dV[:, lo:lo + WIN]
            psk = -jnp.exp(sk - ls) * delta
            for g in range(GRP):
                j = GRP * h + g
                tot = jnp.sum(psk[WIN * g:WIN * g + WIN], axis=0, keepdims=True)
                ds_ref[j:j + 1, :] += jnp.broadcast_to(tot, (1, 128))

        @pl.when(n == nb - 1)
        def _():
            _xchg_finish(cps)

    qs = pl.BlockSpec((WIN, QP), lambda n: (n + 2, 0))
    anyspec = pl.BlockSpec(memory_space=pl.ANY)
    return pl.pallas_call(
        body, name="attn_bwd", grid=(nb,),
        in_specs=[qs] + kspecs + kspecs + [_full((8, 128)), _rows(N_KV * GRP * WIN, CTX + 3 * WIN), _rows(WIN, 128),
                                            _rows(WIN, 512)] + [anyspec] * na,
        out_specs=[_rows(WIN, QP), _full((KP, E)), _full((KP, E)), _full((8, 128))] + [anyspec] * na,
        out_shape=[jax.ShapeDtypeStruct((S, QP), _BF), jax.ShapeDtypeStruct((KP, E), F32),
                   jax.ShapeDtypeStruct((KP, E), F32), jax.ShapeDtypeStruct((8, 128), F32)]
        + _xchg_out_shapes(slabs, False, rows),
        scratch_shapes=_xchg_scratch(na),
        compiler_params=_cp(("arbitrary",), 48 * 1024 * 1024),
    )(q, k, k, k, k, v, v, v, v, sink, probs, lse, d_attn, *slabs)


def _gla_order(E, reverse, backward):
    nc = CTX // GLA_T
    n = E // GLA_T
    if not reverse:
        fwd = lambda s: s
    else:
        fwd = lambda s: jnp.where(s < nc, nc - 1 - s, n - 1 + nc - s)
    if backward:
        return lambda s: fwd(n - 1 - s)
    return fwd


def _gla_masks():
    T = GLA_T
    l128 = lax.broadcasted_iota(jnp.int32, (1, 128), 1)
    qmask = [((l128 >> 5) == j).astype(F32) for j in range(4)]
    vmask = [((l128 >> 6) == j).astype(F32) for j in range(2)]
    bd = ((lax.broadcasted_iota(jnp.int32, (512, 256), 0) >> 6)
          == (lax.broadcasted_iota(jnp.int32, (512, 256), 1) >> 5)).astype(F32)
    ri = lax.broadcasted_iota(jnp.int32, (T, 2 * T), 0)
    ci = lax.broadcasted_iota(jnp.int32, (T, 2 * T), 1) & (T - 1)
    return qmask, vmask, bd, ri, ci


def _tri_sum(tri, x):
    hi = x.astype(_BF)
    lo = (x - hi.astype(F32)).astype(_BF)
    n = x.shape[1]
    y = _nn(tri.astype(_BF), jnp.concatenate([hi, lo], axis=1))
    return y[:, 0:n] + y[:, n:2 * n]


def _gla_decays(la, reverse, ri, ci):
    T = GLA_T
    msk2 = (ri <= ci) if reverse else (ri >= ci)
    mskT2 = (ri >= ci) if reverse else (ri <= ci)
    b = _tri_sum(msk2[:, 0:T], la)
    bT = b[0:1] if reverse else b[T - 1:T]
    bm = b[T // 2:T // 2 + 1]
    return msk2, mskT2, b, bT, bm


def _pair_stack(tile, m0, m1):
    return jnp.concatenate([(tile * m0).astype(_BF), (tile * m1).astype(_BF)], axis=0)


def _gla_fwd(gq, gk, gv, la, reverse, shards=(), other=None):
    E = gq.shape[0]
    T = GLA_T
    n = E // T
    order = _gla_order(E, reverse, False)
    col = 1 if reverse else 0

    def body(gq_ref, gk_ref, gv_ref, la_ref, *rest):
        o_ref, st_ref, S_scr = rest[-3:]

        @pl.when(pl.program_id(0) == 0)
        def _():
            S_scr[...] = jnp.zeros_like(S_scr)

        qmask, vmask, bd, ri, ci = _gla_masks()
        msk2, _, b, bT, bm = _gla_decays(la_ref[...], reverse, ri, ci)
        q, k, v = gq_ref[...], gk_ref[...], gv_ref[...]
        qd = (q * jnp.exp(b)).astype(_BF)
        qm = (q * jnp.exp(b - bm)).astype(_BF)
        km = k * jnp.exp(bm - b)
        kd = (k * jnp.exp(bT - b)).astype(_BF)
        ST = S_scr[...]
        comp = ST[0:DV]
        for h in range(1, N_GLA):
            comp = comp + ST[DV * h:DV * h + DV]
        st_ref[0] = comp
        inter = _nt(qd, ST.astype(_BF))
        tiles = []
        for p in range(N_GLA // 2):
            qs = slice(128 * (p // 2), 128 * (p // 2) + 128)
            vs = slice(128 * p, 128 * p + 128)
            j0 = (2 * p) % 4
            KS = _pair_stack(km[:, qs], qmask[j0], qmask[j0 + 1])
            VS = _pair_stack(v[:, vs], vmask[0], vmask[1])
            AA = jnp.where(msk2, _nt(qm[:, qs], KS), 0.0).astype(_BF)
            tiles.append(_nn(AA, VS))
        o = inter + jnp.concatenate(tiles, axis=1)
        o_ref[...] = o if other is None else o + rest[0][...]
        S_scr[...] = ST * jnp.exp(bT) + bd * _tn(v.astype(_BF), kd)

    blk = lambda w, c=0: pl.BlockSpec((T, w), lambda s: (order(s), c))
    return _hosted_call(
        body, (gq, gk, gv, la) + (() if other is None else (other,)), shards, True,
        name="gla_fwd_rev" if reverse else "gla_fwd", grid=(n,),
        in_specs=[blk(256), blk(256), blk(512), blk(256, col)] + ([] if other is None else [blk(512)]),
        out_specs=[blk(512), pl.BlockSpec((1, DV, 256), lambda s: (order(s), 0, 0))],
        out_shape=[jax.ShapeDtypeStruct((E, 512), F32), jax.ShapeDtypeStruct((n, DV, 256), F32)],
        scratch_shapes=[pltpu.VMEM((512, 256), F32)],
        compiler_params=_cp(("arbitrary",)))


def _gla_bwd(gq, gk, gv, la, z, st, do, reverse, prev=None, slabs=()):
    E = gq.shape[0]
    T = GLA_T
    n = E // T
    nc = CTX // T
    order = _gla_order(E, reverse, True)
    col = 1 if reverse else 0
    np_ = 0 if prev is None else 3

    def body(gq_ref, gk_ref, gv_ref, la_ref, z_ref, st_ref, do_ref, *rest):
        prev_refs = rest[:np_]
        dq_ref, dk_ref, dv_ref, dlg_ref, gwg_ref, bsum_ref, dS_scr = rest[np_:]

        @pl.when(pl.program_id(0) == 0)
        def _():
            dS_scr[...] = jnp.zeros_like(dS_scr)
            gwg_ref[...] = jnp.zeros_like(gwg_ref)
            bsum_ref[...] = jnp.zeros_like(bsum_ref)

        is_lat = order(pl.program_id(0)) >= nc
        qmask, vmask, bd, ri, ci = _gla_masks()
        msk2, mskT2, b, bT, bm = _gla_decays(la_ref[...], reverse, ri, ci)
        q, k, v = gq_ref[...], gk_ref[...], gv_ref[...]
        do = jnp.where(is_lat, do_ref[...].astype(F32), 0.0)
        e_b, e_qm, e_km, e_kd, e_T = jnp.exp(b), jnp.exp(b - bm), jnp.exp(bm - b), jnp.exp(bT - b), jnp.exp(bT)
        qd, qm, km, kd = q * e_b, q * e_qm, k * e_km, k * e_kd
        qdb, qmb, kmb, kdb, vb, dob = (t.astype(_BF) for t in (qd, qm, km, kd, v, do))
        ST = jnp.tile(st_ref[0], (N_GLA, 1)) * bd
        dST = dS_scr[...]
        dSTb = dST.astype(_BF)
        dqd = _nn(dob, ST.astype(_BF))
        dkd = _nn(vb, dSTb)
        dv_t, dqm_t, dkm_t = [], [None, None], [None, None]
        for p in range(N_GLA // 2):
            t = p // 2
            qs = slice(128 * t, 128 * t + 128)
            vs = slice(128 * p, 128 * p + 128)
            j0 = (2 * p) % 4
            QS = _pair_stack(qm[:, qs], qmask[j0], qmask[j0 + 1])
            KS = _pair_stack(km[:, qs], qmask[j0], qmask[j0 + 1])
            VS = _pair_stack(v[:, vs], vmask[0], vmask[1])
            DS = _pair_stack(do[:, vs], vmask[0], vmask[1])
            ATT = jnp.where(mskT2, _nt(kmb[:, qs], QS), 0.0).astype(_BF)
            dAA = jnp.where(msk2, _nt(dob[:, vs], VS), 0.0).astype(_BF)
            dATT = jnp.where(mskT2, _nt(vb[:, vs], DS), 0.0).astype(_BF)
            dv_t.append(_nn(ATT, DS))
            dq_p = _nn(dAA, KS)
            dk_p = _nn(dATT, QS)
            dqm_t[t] = dq_p if dqm_t[t] is None else dqm_t[t] + dq_p
            dkm_t[t] = dk_p if dkm_t[t] is None else dkm_t[t] + dk_p
        dqm = jnp.concatenate(dqm_t, axis=1)
        dkm = jnp.concatenate(dkm_t, axis=1)
        dq = dqm * e_qm + dqd * e_b
        dk = dkm * e_km + dkd * e_kd
        dv = _nt(kdb, dSTb) + jnp.concatenate(dv_t, axis=1)
        if prev is None:
            dq_ref[...], dk_ref[...], dv_ref[...] = dq, dk, dv
        else:
            dq_ref[...] = ((dq + prev_refs[0][...]) * (DK ** -0.5)).astype(_BF)
            dk_ref[...] = (dk + prev_refs[1][...]).astype(_BF)
            dv_ref[...] = (dv + prev_refs[2][...]).astype(_BF)
        db = dqm * qm - dkm * km + dqd * qd - dkd * kd
        dbT = jnp.sum(dkd * kd, axis=0, keepdims=True) + e_T * jnp.sum(dST * ST, axis=0, keepdims=True)
        dla = _tri_sum(mskT2[:, 0:T], db) + dbT
        dlg = dla * (1.0 - jnp.exp(GATE_TAU * la_ref[...])) * (1.0 / GATE_TAU)
        bsum_ref[0:1, :] += jnp.sum(dlg, axis=0, keepdims=True)
        dlgb = dlg.astype(_BF)
        dlg_ref[...] = dlgb
        gwg_ref[...] += _tn(z_ref[...], dlgb)
        dS_scr[...] = dST * e_T + bd * _tn(dob, qdb)

    blk = lambda w, c=0: pl.BlockSpec((T, w), lambda s: (order(s), c))
    do_spec = pl.BlockSpec((T, 512), lambda s: (jnp.maximum(order(s) - nc, 0), 0))
    odt = F32 if prev is None else _BF
    return _hosted_call(
        body, (gq, gk, gv, la, z, st, do) + (() if prev is None else tuple(prev)), slabs, False,
        name="gla_bwd_rev" if reverse else "gla_bwd", grid=(n,),
        in_specs=[blk(256), blk(256), blk(512), blk(256, col), blk(128),
                  pl.BlockSpec((1, DV, 256), lambda s: (order(s), 0, 0)), do_spec]
        + ([] if prev is None else [blk(256), blk(256), blk(512)]),
        out_specs=[blk(256), blk(256), blk(512), blk(256), _full((128, 256)), _full((8, 256))],
        out_shape=[jax.ShapeDtypeStruct((E, 256), odt), jax.ShapeDtypeStruct((E, 256), odt),
                   jax.ShapeDtypeStruct((E, 512), odt), jax.ShapeDtypeStruct((E, 256), _BF),
                   jax.ShapeDtypeStruct((128, 256), F32), jax.ShapeDtypeStruct((8, 256), F32)],
        scratch_shapes=[pltpu.VMEM((512, 256), F32)],
        compiler_params=_cp(("arbitrary",)))


def _gla_out(o, gg, ggla, mavg):
    rr = lax.rsqrt(_head_mean(o * o, mavg) + EPS)
    oh = o * rr
    sg = _sigmoid(gg)
    return oh, rr, sg


def _mix_fwd(x, attn, o, gg, ggla, mavg, wout, gt1, g2):
    S = x.shape[0]
    TM = 256

    def body(x_ref, a_ref, o_ref, gg_ref, ggla_ref, mavg_ref, w_ref, gt1_ref, g2_ref, x1_ref, mix_ref):
        gg_t = gg_ref[...]
        oh, _, sg = _gla_out(o_ref[...], gg_t, ggla_ref[...], mavg_ref[...])
        mix_ref[:, 0:512] = a_ref[...]
        mix_ref[:, 512:1024] = (oh * ggla_ref[...] * (gg_t * sg)).astype(_BF)
        y = _nn(mix_ref[...], w_ref[...])
        ry = lax.rsqrt(jnp.mean(y * y, axis=-1, keepdims=True) + EPS)
        x1_ref[...] = x_ref[...] + gt1_ref[...] * ((y * ry) * g2_ref[...])

    return pl.pallas_call(
        body, name="mix_fwd", grid=(S // TM,),
        in_specs=[_rows(TM, D), _rows(TM, 512), _rows(TM, 512, 1), _rows(TM, 512, 1),
                  _full((1, 512)), _full((512, 512)), _full((D, D)), _full((1, D)), _full((1, D))],
        out_specs=[_rows(TM, D), _rows(TM, D)],
        out_shape=[jax.ShapeDtypeStruct((S, D), F32), jax.ShapeDtypeStruct((S, D), _BF)],
        compiler_params=_cp(("arbitrary",), 40 * 1024 * 1024),
    )(x, attn, o, gg, ggla, mavg, wout, gt1, g2)


def _mix_bwd(dx1, mix, o, gg, ggla, mavg, wout, gt1, g2, slabs):
    S = dx1.shape[0]
    TM = 256

    def body(dx_ref, mix_ref, o_ref, gg_ref, ggla_ref, mavg_ref, w_ref, gt1_ref, g2_ref,
             da_ref, do_ref, dgg_ref, dy_ref, sums_ref):
        @pl.when(pl.program_id(0) == 0)
        def _():
            sums_ref[...] = jnp.zeros_like(sums_ref)

        dx = dx_ref[...]
        y = _nn(mix_ref[...], w_ref[...])
        ry = lax.rsqrt(jnp.mean(y * y, axis=-1, keepdims=True) + EPS)
        yh = y * ry
        sums_ref[0:1, :] += jnp.sum(dx * yh, axis=0, keepdims=True)
        dyh = dx * (gt1_ref[...] * g2_ref[...])
        dy = (ry * (dyh - yh * jnp.mean(dyh * yh, axis=-1, keepdims=True))).astype(_BF)
        dy_ref[...] = dy
        dmix = _nt(dy, w_ref[...])
        da_ref[...] = dmix[:, 0:512].astype(_BF)
        dgla = dmix[:, 512:1024]
        gg_t = gg_ref[...]
        ggla_t = ggla_ref[...]
        oh, rr, sg = _gla_out(o_ref[...], gg_t, ggla_t, mavg_ref[...])
        dgg_ref[...] = (dgla * oh * ggla_t * (sg * (1.0 + gg_t * (1.0 - sg)))).astype(_BF)
        don = dgla * (gg_t * sg)
        sums_ref[1:2, 0:512] += jnp.sum(don * oh, axis=0, keepdims=True)
        doh = don * ggla_t
        do_ref[...] = (rr * (doh - oh * _head_mean(doh * oh, mavg_ref[...]))).astype(_BF)

    return _hosted_call(
        body, (dx1, mix, o, gg, ggla, mavg, wout, gt1, g2), slabs, False,
        name="mix_bwd", grid=(S // TM,),
        in_specs=[_rows(TM, D), _rows(TM, D), _rows(TM, 512, 1), _rows(TM, 512, 1),
                  _full((1, 512)), _full((512, 512)), _full((D, D)), _full((1, D)), _full((1, D))],
        out_specs=[_rows(TM, 512), _rows(TM, 512), _rows(TM, 512), _rows(TM, D), _full((8, D))],
        out_shape=[jax.ShapeDtypeStruct((S, 512), _BF), jax.ShapeDtypeStruct((S, 512), _BF),
                   jax.ShapeDtypeStruct((S, 512), _BF), jax.ShapeDtypeStruct((S, D), _BF),
                   jax.ShapeDtypeStruct((8, D), F32)],
        compiler_params=_cp(("arbitrary",), 40 * 1024 * 1024))


def _ffn(x1, target, gm2, sh2, gt2, g4, wffi, wffo):
    S = x1.shape[0]
    TF = 256

    def body(x_ref, t_ref, gm_ref, sh_ref, gt_ref, g4_ref, wi_hbm, wo_hbm,
             dx_ref, h_ref, du_ref, act_ref, df_ref, sums_ref, loss_ref, wi, wo, sem):
        @pl.when(pl.program_id(0) == 0)
        def _():
            c1 = pltpu.make_async_copy(wi_hbm, wi, sem.at[0])
            c2 = pltpu.make_async_copy(wo_hbm, wo, sem.at[1])
            c1.start()
            c2.start()
            sums_ref[...] = jnp.zeros_like(sums_ref)
            loss_ref[...] = jnp.zeros_like(loss_ref)
            c1.wait()
            c2.wait()

        x = x_ref[...]
        gm = gm_ref[...]
        r = lax.rsqrt(jnp.mean(x * x, axis=-1, keepdims=True) + EPS)
        xh = x * r
        hb = (xh * gm + sh_ref[...]).astype(_BF)
        h_ref[...] = hb
        u = _nt(hb, wi[...])
        g = u[:, 0:FFN]
        up = u[:, FFN:2 * FFN]
        sg = _sigmoid(g)
        sl = g * sg
        ab = (sl * up).astype(_BF)
        act_ref[...] = ab
        f = _nn(ab, wo[...])
        rf = lax.rsqrt(jnp.mean(f * f, axis=-1, keepdims=True) + EPS)
        fh = f * rf
        gt, g4v = gt_ref[...], g4_ref[...]
        err = x + gt * (fh * g4v) - t_ref[...]
        loss_ref[...] += jnp.sum(err * err) * (0.5 / D)
        dout = err * (1.0 / D)
        sums_ref[2:3, :] += jnp.sum(dout * fh, axis=0, keepdims=True)
        dfh = dout * (gt * g4v)
        dfb = (rf * (dfh - fh * jnp.mean(dfh * fh, axis=-1, keepdims=True))).astype(_BF)
        df_ref[...] = dfb
        dact = _nt(dfb, wo[...])
        du_ref[:, 0:FFN] = (dact * up * (sg * (1.0 + g * (1.0 - sg)))).astype(_BF)
        du_ref[:, FFN:2 * FFN] = (dact * sl).astype(_BF)
        dh = _nn(du_ref[...], wi[...])
        sums_ref[0:1, :] += jnp.sum(dh, axis=0, keepdims=True)
        sums_ref[1:2, :] += jnp.sum(dh * xh, axis=0, keepdims=True)
        dxh = dh * gm
        dx_ref[...] = dout + r * (dxh - xh * jnp.mean(dxh * xh, axis=-1, keepdims=True))

    vec = _full((1, D))
    anyspec = pl.BlockSpec(memory_space=pl.ANY)
    return pl.pallas_call(
        body, name="ffn_fwd_bwd", grid=(S // TF,),
        in_specs=[_rows(TF, D), _rows(TF, D), vec, vec, vec, vec, anyspec, anyspec],
        out_specs=[_rows(TF, D), _rows(TF, D), _rows(TF, 2 * FFN), _rows(TF, FFN), _rows(TF, D),
                   _full((8, D)), _full((8, 128))],
        out_shape=[jax.ShapeDtypeStruct((S, D), F32), jax.ShapeDtypeStruct((S, D), _BF),
                   jax.ShapeDtypeStruct((S, 2 * FFN), _BF), jax.ShapeDtypeStruct((S, FFN), _BF),
                   jax.ShapeDtypeStruct((S, D), _BF), jax.ShapeDtypeStruct((8, D), F32),
                   jax.ShapeDtypeStruct((8, 128), F32)],
        scratch_shapes=[pltpu.VMEM((2 * FFN, D), _BF), pltpu.VMEM((FFN, D), _BF), pltpu.SemaphoreType.DMA((2,))],
        compiler_params=_cp(("arbitrary",), VMEM_BIG),
    )(x1, target, gm2, sh2, gt2, g4, wffi, wffo)


def _inproj_bwd(x, ctx, gml, gmc, win, wg, cos, sa, sb, dq, dk, dv, dgq, dgk, dgv, dgg, dlg_f, dlg_b, dx1):
    S = x.shape[0]
    E = S + CTX
    TE = CTX

    def body(x_ref, c_ref, gml_ref, gmc_ref, w_ref, wg_ref, cos_ref, sa_ref, sb_ref, dq_ref, dk_ref, dv_ref,
             gq_ref, gk_ref, gv_ref, dgg_ref, dlf, dlb, dx1_ref, dp_ref, gx_ref, sums_ref):
        i = pl.program_id(0)
        is_ctx = i == 0

        @pl.when(is_ctx)
        def _():
            sums_ref[...] = jnp.zeros_like(sums_ref)

        lat = jnp.where(is_ctx, 0.0, 1.0)
        cos_t, sa_t, sb_t = cos_ref[...], sa_ref[...], sb_ref[...]
        dp_ref[:, O_Q:O_K] = (_unrope(dq_ref[...].astype(F32), cos_t, sa_t, sb_t) * lat).astype(_BF)
        dp_ref[:, O_K:O_V] = _unrope(dk_ref[...].T, cos_t, sa_t, sb_t).astype(_BF)
        dp_ref[:, O_V:O_GQ] = dv_ref[...].T.astype(_BF)
        dp_ref[:, O_GQ:O_GK] = gq_ref[...]
        dp_ref[:, O_GK:O_GV] = gk_ref[...]
        dp_ref[:, O_GV:O_GG] = gv_ref[...]
        dp_ref[:, O_GG:O_Z] = (dgg_ref[...].astype(F32) * lat).astype(_BF)
        dlg = jnp.concatenate([dlf[...], dlb[...]], axis=1)
        dp_ref[:, O_Z:NP] = _nt(dlg, wg_ref[...]).astype(_BF)
        dh = _nn(dp_ref[...], w_ref[...])
        x = jnp.where(is_ctx, c_ref[...], x_ref[...])
        r = lax.rsqrt(jnp.mean(x * x, axis=-1, keepdims=True) + EPS)
        xh = x * r
        sdh = jnp.sum(dh, axis=0, keepdims=True)
        sdx = jnp.sum(dh * xh, axis=0, keepdims=True)
        sums_ref[0:1, :] += sdh * lat
        sums_ref[1:2, :] += sdx * lat
        sums_ref[2:3, :] += sdh * (1.0 - lat)
        sums_ref[3:4, :] += sdx * (1.0 - lat)
        dxh = dh * jnp.where(is_ctx, gmc_ref[...], gml_ref[...])
        gx_ref[...] = dx1_ref[...] + r * (dxh - xh * jnp.mean(dxh * xh, axis=-1, keepdims=True))

    vec = _full((1, D))
    tab = _rows(TE, 128)
    return pl.pallas_call(
        body, name="inproj_bwd", grid=(E // TE,),
        in_specs=[_rows_lat(TE, D), _full((CTX, D)), vec, vec, _full((NP, D)), _full((128, 512)), tab, tab, tab,
                  _rows_lat(TE, QP), pl.BlockSpec((KP, TE), lambda i: (0, i)), pl.BlockSpec((KP, TE), lambda i: (0, i)),
                  _rows(TE, 256), _rows(TE, 256), _rows(TE, 512), _rows_lat(TE, 512), _rows(TE, 256), _rows(TE, 256),
                  _rows_lat(TE, D)],
        out_specs=[_rows(TE, NP), _rows_lat(TE, D), _full((8, D))],
        out_shape=[jax.ShapeDtypeStruct((E, NP), _BF), jax.ShapeDtypeStruct((S, D), F32),
                   jax.ShapeDtypeStruct((8, D), F32)],
        compiler_params=_cp(("arbitrary",), VMEM_BIG),
    )(x, ctx, gml, gmc, win, wg, cos, sa, sb, dq, dk, dv, dgq, dgk, dgv, dgg, dlg_f, dlg_b, dx1)


def _matmul_tn(a, b, tk, tt, name, out_dtype, transpose_out=False, a_cols=None, hosted=(), gather=True):
    T, KA = a.shape
    N = b.shape[1]
    nt = T // tt
    k0 = 0
    if a_cols is not None:
        KA, k0 = tk, a_cols

    def body(a_ref, b_ref, o_ref, acc):
        t = pl.program_id(1)

        @pl.when(t == 0)
        def _():
            acc[...] = jnp.zeros_like(acc)

        acc[...] += _tn(a_ref[...], b_ref[...])

        @pl.when(t == nt - 1)
        def _():
            o_ref[...] = (acc[...].T if transpose_out else acc[...]).astype(out_dtype)

    if transpose_out:
        out_spec, out_shape = pl.BlockSpec((N, tk), lambda i, t: (0, i)), (N, KA)
    else:
        out_spec, out_shape = pl.BlockSpec((tk, N), lambda i, t: (i, 0)), (KA, N)
    res = _hosted_call(
        body, (a, b), hosted, gather, name=name, grid=(KA // tk, nt),
        in_specs=[pl.BlockSpec((tt, tk), lambda i, t: (t, i + k0)), pl.BlockSpec((tt, N), lambda i, t: (t, 0))],
        out_specs=[out_spec], out_shape=[jax.ShapeDtypeStruct(out_shape, out_dtype)],
        scratch_shapes=[pltpu.VMEM((tk, N), F32)],
        compiler_params=_cp(("arbitrary", "arbitrary"), VMEM_BIG))
    return res if hosted else res[0]


def _ada_bwd(c_all, c_ctx, w_ada, d_all):
    n = w_ada.shape[1]

    def body(c_ref, cc_ref, w_ref, d_ref, gw_ref, t_ref):
        c = jnp.concatenate([c_ref[...], jnp.broadcast_to(cc_ref[...], (8, D))], axis=0)
        db = d_ref[...].astype(_BF)
        gw_ref[0] = _tn((c * _sigmoid(c)).astype(_BF), db)
        t_ref[...] = _nt(db[8:16], w_ref[...].astype(_BF))

    return pl.pallas_call(
        body, name="ada_bwd", in_specs=[_full((8, D)), _full((1, D)), _full((D, n)), _full((16, n))],
        out_specs=[_full((1, D, n)), _full((8, D))],
        out_shape=[jax.ShapeDtypeStruct((1, D, n), F32), jax.ShapeDtypeStruct((8, D), F32)], grid=(1,),
        compiler_params=_cp(("arbitrary",)),
    )(c_all, c_ctx, w_ada, d_all)


PART_ROWS = 56
R_ADA, R_ADA_C, R_GAIN, R_SINK, R_BG, R_GGLA, R_LOSS, R_WG = 0, 6, 12, 16, 17, 18, 19, 24


def _small_grads(s_in, s_ffn, s_mix, ada_l, ada_c, gains, dsink, s_bg, g_wg, loss):
    def body(si, sf, sm, al, ac, g, ds, sbg, gwg, loss_ref, o_ref):
        o_ref[...] = jnp.zeros_like(o_ref)
        o_ref[R_LOSS:R_LOSS + 1, 0:128] = loss_ref[0:1, :]
        sub = lax.broadcasted_iota(jnp.int32, (8, 128), 0)
        lane = lax.broadcasted_iota(jnp.int32, (8, 128), 1)
        o_ref[R_SINK:R_SINK + 1, 0:128] = jnp.sum(jnp.where(sub == lane, ds[...], 0.0), axis=0, keepdims=True)
        o_ref[R_BG:R_BG + 1, 0:512] = sbg[0:1, :]
        y = sm[1:2, 0:128] + sm[1:2, 128:256] + sm[1:2, 256:384] + sm[1:2, 384:512]
        y = y + pltpu.roll(y, 64, 1)
        o_ref[R_GGLA:R_GGLA + 1, 0:128] = jnp.where(lane[0:1] < DV, y, 0.0)
        o_ref[R_WG:R_WG + 16, 0:256] = gwg[0:16, 0:256]
        o_ref[R_WG + 16:R_WG + 32, 0:256] = gwg[16:32, 256:512]
        sdh_l, sdx_l, sdh_c, sdx_c = si[0:1], si[1:2], si[2:3], si[3:4]
        sdh2, sdx2, a2 = sf[0:1], sf[1:2], sf[2:3]
        a1 = sm[0:1]
        g1, g2, g3, g4 = g[0:1], g[1:2], g[2:3], g[3:4]
        sc1, gt1, sc2, gt2 = al[1:2], al[2:3], al[4:5], al[5:6]
        sc1c = ac[1:2]
        z = jnp.zeros((1, D), F32)
        rows = [sdh_l, sdx_l * g1, a1 * g2, sdh2, sdx2 * g3, a2 * g4,
                sdh_c, sdx_c * g1, z, z, z, z,
                sdx_l * (1.0 + sc1) + sdx_c * (1.0 + sc1c), a1 * gt1, sdx2 * (1.0 + sc2), a2 * gt2]
        for r, v in enumerate(rows):
            o_ref[r:r + 1, :] = v

    v8 = _full((8, D))
    return pl.pallas_call(
        body, name="small_grads",
        in_specs=[v8] * 6 + [_full((8, 128)), _full((8, 512)), _full((128, 512)), _full((8, 128))],
        out_specs=_full((PART_ROWS, D)), out_shape=jax.ShapeDtypeStruct((PART_ROWS, D), F32), grid=(1,),
        compiler_params=_cp(("arbitrary",)),
    )(s_in, s_ffn, s_mix, ada_l, ada_c, gains, dsink, s_bg, g_wg, loss)


def _row_tile(R):
    for cand in (256, 128, 64, 32, 16):
        if R % cand == 0 and R > cand:
            return cand
    return R


def _adamw(w, gs, m, v, name, by_rows=False):
    _, R, C = w.shape
    c1 = 1.0 / (1.0 - ADAM_B1 ** ADAM_STEP)
    c2 = 1.0 / (1.0 - ADAM_B2 ** ADAM_STEP)
    ng = len(gs)
    if by_rows:
        tr = _row_tile(math.gcd(*[g.shape[1] for g in gs]))
        first = [sum(g.shape[1] for g in gs[:p]) // tr for p in range(ng)]
        g_specs = [pl.BlockSpec((g.shape[0], tr, C), functools.partial(
            lambda i, f, n: (0, jnp.clip(i - f, 0, n - 1), 0), f=first[p], n=g.shape[1] // tr))
            for p, g in enumerate(gs)]
    else:
        tr = _row_tile(R)
        g_specs = [pl.BlockSpec((g.shape[0], tr, g.shape[2]), lambda i: (0, i, 0)) for g in gs]

    def slab_sum(g_ref):
        gg = g_ref[0].astype(F32)
        for j in range(1, g_ref.shape[0]):
            gg = gg + g_ref[j].astype(F32)
        return gg

    def body(w_ref, *refs):
        g_refs, (m_ref, v_ref, go_ref, d_ref, nm_ref, nv_ref) = refs[:ng], refs[ng:]
        if by_rows:
            gg = slab_sum(g_refs[0])
            for p in range(1, ng):
                gg = jnp.where(pl.program_id(0) >= first[p], slab_sum(g_refs[p]), gg)
            pieces = [(slice(0, C), gg)]
        else:
            pieces, c0 = [], 0
            for g_ref in g_refs:
                pieces.append((slice(c0, c0 + g_ref.shape[2]), slab_sum(g_ref)))
                c0 += g_ref.shape[2]
        for cols, gg in pieces:
            go_ref[0, :, cols] = gg
            nm = ADAM_B1 * m_ref[0, :, cols] + (1.0 - ADAM_B1) * gg
            nv = ADAM_B2 * v_ref[0, :, cols] + (1.0 - ADAM_B2) * (gg * gg)
            nm_ref[0, :, cols] = nm
            nv_ref[0, :, cols] = nv
            d_ref[0, :, cols] = -ADAM_LR * ((nm * c1) / (jnp.sqrt(nv * c2) + ADAM_EPS) + ADAM_WD * w_ref[0, :, cols])

    spec = pl.BlockSpec((1, tr, C), lambda i: (0, i, 0))
    sds = jax.ShapeDtypeStruct((1, R, C), F32)
    return pl.pallas_call(
        body, name=name, grid=(R // tr,), in_specs=[spec] + g_specs + [spec, spec], out_specs=[spec] * 4,
        out_shape=[sds] * 4, compiler_params=_cp(("parallel",), 48 * 1024 * 1024),
    )(w, *gs, m, v)


def _sum_slots(slots, name):
    _, R, C = slots.shape
    tr = _row_tile(R)

    def body(s_ref, o_ref):
        acc = s_ref[0].astype(F32)
        for j in range(1, N_DEV):
            acc = acc + s_ref[j].astype(F32)
        o_ref[...] = acc

    return pl.pallas_call(
        body, name=name, grid=(R // tr,), in_specs=[pl.BlockSpec((N_DEV, tr, C), lambda i: (0, i, 0))],
        out_specs=_rows(tr, C), out_shape=jax.ShapeDtypeStruct((R, C), F32), compiler_params=_cp(("parallel",)),
    )(slots)


def _ag2_start(x_ref, out_ref, send_sems, recv_sems, local_sem):
    x, y, c = lax.axis_index("x"), lax.axis_index("y"), lax.axis_index("c")
    me, sibling = (x, y, c), (x, y, 1 - c)
    chips = [(1 - x, y), (x, 1 - y), (1 - x, 1 - y)]

    def rows(px, py, pc):
        return out_ref.at[4 * px + 2 * py + pc]

    def copy(k, block, to, src=None):
        return pltpu.make_async_remote_copy(
            src_ref=rows(*block) if src is None else src, dst_ref=rows(*block),
            send_sem=send_sems.at[k], recv_sem=recv_sems.at[k], device_id=to, device_id_type=MESH)

    mine = pltpu.make_async_copy(x_ref, rows(*me), local_sem)
    mine.start()
    first = [copy(0, me, sibling, src=x_ref)]
    first += [copy(1 + j, me, (*chip, c), src=x_ref) for j, chip in enumerate(chips)]
    for cp in first:
        cp.start()
    return copy, mine, first, me, sibling, chips, c


def _ag2_finish(state):
    copy, mine, first, me, sibling, chips, c = state
    passed = [copy(4 + j, (*chip, c), sibling) for j, chip in enumerate(chips)]
    for j, chip in enumerate(chips):
        copy(1 + j, (*chip, c), me).wait_recv()
        passed[j].start()
    copy(0, sibling, me).wait_recv()
    for j, chip in enumerate(chips):
        copy(4 + j, (*chip, 1 - c), me).wait_recv()
    for cp in first + passed:
        cp.wait_send()
    mine.wait()


def _exchange(arrays, name, gather):
    na = len(arrays)

    def body(*refs):
        cps = _xchg_copies(refs[:na], refs[na:2 * na], *refs[2 * na:], gather=gather)
        _xchg_start(cps)
        _xchg_finish(cps)

    anyspec = pl.BlockSpec(memory_space=pl.ANY)
    return pl.pallas_call(
        body, name=name, out_shape=_xchg_out_shapes(arrays, gather), in_specs=[anyspec] * na,
        out_specs=[anyspec] * na, scratch_shapes=_xchg_scratch(na),
    )(*arrays)


def _entry(c, wg_sh, win_sh, c_ctx, w_ada):
    n = w_ada.shape[1]

    def body(c_ref, g_ref, w_ref, cc_ref, wa_ref, call_ref, gall_ref, wall_ref, ada_ref, part,
             s_send, s_recv, s_loc, w_send, w_recv, w_loc, a_send, a_recv, a_loc):
        big = _ag2_start(w_ref, wall_ref, w_send, w_recv, w_loc)
        small = _xchg_copies([c_ref, g_ref], [call_ref, gall_ref], s_send, s_recv, s_loc, gather=True)
        _xchg_start(small)
        _xchg_finish(small)
        cs = jnp.concatenate([call_ref[:, 0, :], jnp.broadcast_to(cc_ref[...], (8, D))], axis=0)
        part[...] = _nn((cs * _sigmoid(cs)).astype(_BF), wa_ref[...].astype(_BF))
        ada = _xchg_copies([part], [ada_ref], a_send, a_recv, a_loc, gather=True)
        _xchg_start(ada)
        _xchg_finish(ada)
        _ag2_finish(big)

    vm = pl.BlockSpec(memory_space=pltpu.VMEM)
    return pl.pallas_call(
        body, name="entry_gather",
        out_shape=[jax.ShapeDtypeStruct((N_DEV,) + c.shape, F32), jax.ShapeDtypeStruct((N_DEV,) + wg_sh.shape, F32),
                   jax.ShapeDtypeStruct((N_DEV,) + win_sh.shape, win_sh.dtype),
                   jax.ShapeDtypeStruct((N_DEV, 16, n), F32)],
        in_specs=[vm] * 5, out_specs=[vm] * 4,
        scratch_shapes=[pltpu.VMEM((16, n), F32)] + _xchg_scratch(2)
        + [pltpu.SemaphoreType.DMA((7,)), pltpu.SemaphoreType.DMA((7,)), pltpu.SemaphoreType.DMA] + _xchg_scratch(1),
        compiler_params=pltpu.CompilerParams(vmem_limit_bytes=VMEM_BIG),
    )(c, wg_sh, win_sh, c_ctx, w_ada)


def _rope_tables(S):
    t = np.arange(S)
    row = (t // GRID_W).astype(np.float32)
    colp = (t % GRID_W).astype(np.float32)
    half = HD // 2
    inv = (ROPE_BASE ** (-np.arange(0, half, 2, dtype=np.float32) / half)).astype(np.float32)
    ar = row[:, None] * inv[None, :]
    ac = colp[:, None] * inv[None, :]
    ang = np.concatenate([ar, ar, ac, ac], axis=-1).astype(np.float32)
    cos = np.cos(ang).astype(np.float32)
    sin = np.sin(ang).astype(np.float32)
    lane = np.arange(HD)
    first = (lane % 32) < 16
    sa = np.where(first[None, :], -sin, 0.0)
    sb = np.where(first[None, :], 0.0, sin)

    def ext(tab, ctx_val):
        full = np.zeros((CTX + S, 128), np.float32)
        full[:CTX, :] = ctx_val
        full[CTX:, :HD] = tab
        full[CTX:, HD:] = tab
        return jnp.asarray(full)

    return ext(cos, 1.0), ext(sa, 0.0), ext(sb, 0.0)


def _pad_rows_win(wt):
    return jnp.pad(wt, ((0, NP - IN_COLS), (0, 0)))


def _unpad_rows_win(g):
    return g[0:IN_COLS]


def _local_step(x, ctx, target, ada_l, ada_c, gains, sink, win_p, wg_bd, bg, ggla, wout_sh, wffi_sh, wffo_sh):
    S = x.shape[0]
    cos, sa, sb = _rope_tables(S)
    g1, g2, g3, g4 = (gains[i:i + 1] for i in range(4))
    sh1, sc1, gt1, sh2, sc2, gt2 = (ada_l[i:i + 1] for i in range(6))
    sh1c, sc1c = ada_c[0:1], ada_c[1:2]
    gml, gmc, gm2 = g1 * (1.0 + sc1), g1 * (1.0 + sc1c), g3 * (1.0 + sc2)
    mavg = jnp.asarray(np.kron(np.eye(N_GLA, dtype=np.float32), np.full((DV, DV), 1.0 / DV, np.float32))).astype(_BF)

    n_ffi, r_ffo, r_out = wffi_sh.shape[0], wffo_sh.shape[0], wout_sh.shape[0]
    tt_e = 768 if (S + CTX) % 768 == 0 else 256
    tt_s = 512 if S % 512 == 0 else 256
    h, q, k, v, gq, gk, gv, gg, z, la, wout_g = _inproj_fwd(x, ctx, gml, sh1, gmc, sh1c, win_p, wg_bd, bg,
                                                            cos, sa, sb, [wout_sh])
    attn, lse, probs, wffi_g = _attn_fwd(q, k, v, sink, [wffi_sh])
    o_f, st_f, wffo_g = _gla_fwd(gq, gk, gv, la, False, [wffo_sh])
    o, st_b = _gla_fwd(gq, gk, gv, la, True, (), o_f)
    wout = wout_g.reshape(N_DEV * r_out, D)
    wffi = wffi_g.reshape(N_DEV * n_ffi, D)
    wffo = wffo_g.reshape(N_DEV * r_ffo, D)
    x1, mix = _mix_fwd(x, attn, o, gg, ggla, mavg, wout, gt1, g2)
    dx1, h2, du, act, df, s_ffn, loss = _ffn(x1, target, gm2, sh2, gt2, g4, wffi, wffo)
    slab_ffi = _matmul_tn(h2, du, 512, tt_s, "grad_w_ffn_in", _BF, True).reshape(N_DEV, n_ffi, D)
    slab_ffo = _matmul_tn(act, df, FFN, tt_s, "grad_w_ffn_out", _BF).reshape(N_DEV, r_ffo, D)
    d_attn, do_gla, dgg, dy, s_mix, got_ffo = _mix_bwd(dx1, mix, o, gg, ggla, mavg, wout, gt1, g2, [slab_ffo])
    slab_out = _matmul_tn(mix, dy, D, tt_s, "grad_w_out", _BF).reshape(N_DEV, r_out, D)
    hr = n_ffi // 2
    dq, dk, dv, dsink, got_ffi_a = _attn_bwd(q, k, v, sink, probs, lse, d_attn, [(slab_ffi, (0, hr))])
    pq, pk, pv, dlg_f, gwg_f, sbg_f, got_out, got_ffi_b = _gla_bwd(
        gq, gk, gv, la, z, st_f, do_gla, False, None, [slab_out, (slab_ffi, (hr, n_ffi - hr))])
    got_ffi = [got_ffi_a, got_ffi_b]
    dgq, dgk, dgv, dlg_b, gwg_b, sbg_b = _gla_bwd(gq, gk, gv, la, z, st_b, do_gla, True, (pq, pk, pv))
    dp, grad_x, s_in = _inproj_bwd(x, ctx, gml, gmc, win_p, wg_bd, cos, sa, sb, dq, dk, dv,
                                   dgq, dgk, dgv, dgg, dlg_f, dlg_b, dx1)
    g_wg = jnp.concatenate([gwg_f, gwg_b], axis=1)
    s_bg = jnp.concatenate([sbg_f, sbg_b], axis=1)
    small = _small_grads(s_in, s_ffn, s_mix, ada_l, ada_c, gains, dsink, s_bg, g_wg, loss)
    n_in, n_grp = IN_COLS // N_DEV, 2
    got_in, slab = [], None
    for j in range(n_grp):
        g_j, got = _matmul_tn(h, dp, D // n_grp, tt_e, "grad_w_in_%d" % j, _BF, True, a_cols=j,
                              hosted=[small] if j == 0 else [slab], gather=(j == 0))
        if j == 0:
            parts = got
        else:
            got_in.append(got)
        slab = _unpad_rows_win(g_j).reshape(N_DEV, n_in, D // n_grp)
    got_in.append(_exchange([slab], "scatter_grads", False)[0])
    return dict(grad_x=grad_x, got_in=got_in, got_out=got_out, got_ffi=got_ffi, got_ffo=got_ffo, parts=parts)


SMALL_NAMES = ["c_ctx", "b_ada", "g_pre_mix", "g_post_mix", "g_pre_ffn", "g_post_ffn", "attn_sink",
               "b_gate_fwd", "b_gate_bwd", "g_gla_norm", "w_gate_fwd", "w_gate_bwd"]


def _small_update(tot, t_tot, wg_g, w, m, v):
    c1 = 1.0 / (1.0 - ADAM_B1 ** ADAM_STEP)
    c2 = 1.0 / (1.0 - ADAM_B2 ** ADAM_STEP)
    n = len(SMALL_NAMES)

    def body(tot_ref, t_ref, wg_ref, *refs):
        w_r, m_r, v_r = refs[0:n], refs[n:2 * n], refs[2 * n:3 * n]
        g_o, d_o, nm_o, nv_o = refs[3 * n:4 * n], refs[4 * n:5 * n], refs[5 * n:6 * n], refs[6 * n:7 * n]

        def upd(i, idx, g):
            nm = ADAM_B1 * m_r[i][idx] + (1.0 - ADAM_B1) * g
            nv = ADAM_B2 * v_r[i][idx] + (1.0 - ADAM_B2) * (g * g)
            g_o[i][idx] = g
            nm_o[i][idx] = nm
            nv_o[i][idx] = nv
            d_o[i][idx] = -ADAM_LR * ((nm * c1) / (jnp.sqrt(nv * c2) + ADAM_EPS) + ADAM_WD * w_r[i][idx])

        everything = (slice(None), slice(None))
        cc = w_r[0][...]
        sc = _sigmoid(cc)
        upd(0, everything, t_ref[0:1, :] * (sc * (1.0 + cc * (1.0 - sc))))
        for j in range(6):
            upd(1, (slice(None), slice(D * j, D * j + D)),
                tot_ref[R_ADA + j:R_ADA + j + 1, :] + tot_ref[R_ADA_C + j:R_ADA_C + j + 1, :])
        for j in range(4):
            upd(2 + j, everything, tot_ref[R_GAIN + j:R_GAIN + j + 1, :])
        upd(6, everything, tot_ref[R_SINK:R_SINK + 1, 0:N_ATT])
        upd(7, everything, tot_ref[R_BG:R_BG + 1, 0:256])
        upd(8, everything, tot_ref[R_BG:R_BG + 1, 256:512])
        upd(9, everything, tot_ref[R_GGLA:R_GGLA + 1, 0:DV])
        upd(10, (0,), wg_ref[0:GATE_RANK, :])
        upd(11, (0,), wg_ref[GATE_RANK:2 * GATE_RANK, :])

    params = [w[k] for k in SMALL_NAMES] + [m[k] for k in SMALL_NAMES] + [v[k] for k in SMALL_NAMES]
    outs = pl.pallas_call(
        body, name="small_update", grid=(1,),
        in_specs=[_full(tot.shape), _full(t_tot.shape), _full(wg_g.shape)] + [_full(p.shape) for p in params],
        out_specs=[_full(w[k].shape) for k in SMALL_NAMES] * 4,
        out_shape=[jax.ShapeDtypeStruct(w[k].shape, F32) for k in SMALL_NAMES] * 4,
        compiler_params=_cp(("arbitrary",)),
    )(tot, t_tot, wg_g, *params)
    return tuple(dict(zip(SMALL_NAMES, outs[i * n:(i + 1) * n])) for i in range(4))


def kernel(x, c, ctx, c_ctx, w_ada, b_ada, g_pre_mix, g_post_mix, g_pre_ffn, g_post_ffn, w_in, attn_sink, w_gate_fwd, b_gate_fwd, w_gate_bwd, b_gate_bwd, g_gla_norm, w_out, w_ffn_in, w_ffn_out, loss_target, m_c_ctx, m_w_ada, m_b_ada, m_g_pre_mix, m_g_post_mix, m_g_pre_ffn, m_g_post_ffn, m_w_in, m_attn_sink, m_w_gate_fwd, m_b_gate_fwd, m_w_gate_bwd, m_b_gate_bwd, m_g_gla_norm, m_w_out, m_w_ffn_in, m_w_ffn_out, v_c_ctx, v_w_ada, v_b_ada, v_g_pre_mix, v_g_post_mix, v_g_pre_ffn, v_g_post_ffn, v_w_in, v_attn_sink, v_w_gate_fwd, v_b_gate_fwd, v_w_gate_bwd, v_b_gate_bwd, v_g_gla_norm, v_w_out, v_w_ffn_in, v_w_ffn_out):
    me = 4 * lax.axis_index("x") + 2 * lax.axis_index("y") + lax.axis_index("c")
    S = x.shape[1]
    n_in = w_in.shape[2]
    n_ffi = w_ffn_in.shape[2]
    r_out = w_out.shape[1]
    r_ffo = w_ffn_out.shape[1]
    n_ada = w_ada.shape[2]

    wg_sh = jnp.concatenate([w_gate_fwd.reshape(4, 128), w_gate_bwd.reshape(4, 128)], axis=0)
    c_all3, g_all, w_all, ada_all = _entry(c, wg_sh, w_in[0].T.astype(_BF), c_ctx.reshape(1, D), w_ada[0])
    c_all = c_all3.reshape(N_DEV, D)
    wgf = g_all[:, 0:4].reshape(N_DEV, GATE_RANK, 32).transpose(1, 0, 2).reshape(GATE_RANK, 256)
    wgb = g_all[:, 4:8].reshape(N_DEV, GATE_RANK, 32).transpose(1, 0, 2).reshape(GATE_RANK, 256)
    win_p = _pad_rows_win(w_all.reshape(N_DEV * n_in, D))
    wg_bd = jnp.zeros((128, 512), F32).at[0:16, 0:256].set(wgf).at[16:32, 256:512].set(wgb).astype(_BF)
    ada_full = ada_all.transpose(1, 0, 2).reshape(16, N_DEV * n_ada) + b_ada
    ada_l = jnp.pad(lax.dynamic_slice_in_dim(ada_full, me, 1, 0).reshape(6, D), ((0, 2), (0, 0)))
    ada_c = jnp.pad(ada_full[8].reshape(6, D), ((0, 2), (0, 0)))
    gains = jnp.pad(jnp.concatenate([g_pre_mix, g_post_mix, g_pre_ffn, g_post_ffn], axis=0), ((0, 4), (0, 0)))
    sink = jnp.broadcast_to(attn_sink.reshape(8, 1), (8, 128))
    bg = jnp.concatenate([b_gate_fwd, b_gate_bwd], axis=1)
    ggla = jnp.tile(g_gla_norm, (1, N_GLA))

    r = _local_step(x[0], ctx[0], loss_target[0], ada_l, ada_c, gains, sink, win_p, wg_bd, bg, ggla,
                    w_out[0].astype(_BF), w_ffn_in[0].T.astype(_BF), w_ffn_out[0].astype(_BF))

    parts = r["parts"]
    tot = _sum_slots(parts, "sum_small_grads")
    loss = tot[R_LOSS, 0]
    d_ada_rows = parts[:, R_ADA:R_ADA + 6].reshape(N_DEV, 6 * D)
    d_ada_c = tot[R_ADA_C:R_ADA_C + 6].reshape(1, 6 * D)
    my_cols = lax.dynamic_slice_in_dim(jnp.concatenate([d_ada_rows, jnp.broadcast_to(d_ada_c, (1, 6 * D)),
                                                        jnp.zeros((7, 6 * D), F32)], axis=0), me * n_ada, n_ada, 1)
    grad_w_ada, t_part = _ada_bwd(c_all, c_ctx.reshape(1, D), w_ada[0], my_cols)
    wg_g = lax.dynamic_slice(tot, (R_WG, me * 32), (2 * GATE_RANK, 32))

    tr = lambda a: jnp.transpose(a, (0, 2, 1))
    big = {}
    t_all, = _exchange([t_part], "gather_c_ctx", True)
    t_tot = _sum_slots(t_all, "sum_c_ctx")
    for nm, w, g, m, v in [("w_ada", w_ada, grad_w_ada, m_w_ada, v_w_ada),
                           ("w_out", w_out, r["got_out"], m_w_out, v_w_out),
                           ("w_ffn_out", w_ffn_out, r["got_ffo"], m_w_ffn_out, v_w_ffn_out)]:
        big[nm] = _adamw(w, [g], m, v, "adamw_" + nm)
    big["w_ffn_in"] = tuple(tr(o) for o in _adamw(tr(w_ffn_in), r["got_ffi"], tr(m_w_ffn_in), tr(v_w_ffn_in),
                                                  "adamw_w_ffn_in", by_rows=True))
    big["w_in"] = tuple(tr(o) for o in _adamw(tr(w_in), r["got_in"], tr(m_w_in), tr(v_w_in), "adamw_w_in"))

    w_small = dict(c_ctx=c_ctx.reshape(1, D), b_ada=b_ada, g_pre_mix=g_pre_mix, g_post_mix=g_post_mix, g_pre_ffn=g_pre_ffn,
                   g_post_ffn=g_post_ffn, attn_sink=attn_sink, b_gate_fwd=b_gate_fwd, b_gate_bwd=b_gate_bwd,
                   g_gla_norm=g_gla_norm, w_gate_fwd=w_gate_fwd, w_gate_bwd=w_gate_bwd)
    m_small = dict(c_ctx=m_c_ctx.reshape(1, D), b_ada=m_b_ada, g_pre_mix=m_g_pre_mix, g_post_mix=m_g_post_mix,
                   g_pre_ffn=m_g_pre_ffn, g_post_ffn=m_g_post_ffn, attn_sink=m_attn_sink, b_gate_fwd=m_b_gate_fwd,
                   b_gate_bwd=m_b_gate_bwd, g_gla_norm=m_g_gla_norm, w_gate_fwd=m_w_gate_fwd, w_gate_bwd=m_w_gate_bwd)
    v_small = dict(c_ctx=v_c_ctx.reshape(1, D), b_ada=v_b_ada, g_pre_mix=v_g_pre_mix, g_post_mix=v_g_post_mix,
                   g_pre_ffn=v_g_pre_ffn, g_post_ffn=v_g_post_ffn, attn_sink=v_attn_sink, b_gate_fwd=v_b_gate_fwd,
                   b_gate_bwd=v_b_gate_bwd, g_gla_norm=v_g_gla_norm, w_gate_fwd=v_w_gate_fwd, w_gate_bwd=v_w_gate_bwd)
    grads_small, d_s, nm_s, nv_s = _small_update(tot, t_tot, wg_g, w_small, m_small, v_small)
    for dd in (grads_small, d_s, nm_s, nv_s):
        dd["c_ctx"] = dd["c_ctx"].reshape(D)

    order = ["c_ctx", "w_ada", "b_ada", "g_pre_mix", "g_post_mix", "g_pre_ffn", "g_post_ffn", "w_in", "attn_sink",
             "w_gate_fwd", "b_gate_fwd", "w_gate_bwd", "b_gate_bwd", "g_gla_norm", "w_out", "w_ffn_in", "w_ffn_out"]
    grads, deltas, new_m, new_v = [], [], [], []
    for nm in order:
        if nm in big:
            g_, d_, m_, v_ = big[nm]
        else:
            g_, d_, m_, v_ = grads_small[nm], d_s[nm], nm_s[nm], nv_s[nm]
        grads.append(g_)
        deltas.append(d_)
        new_m.append(m_)
        new_v.append(v_)
    return (loss, r["grad_x"][None], *grads, *deltas, *new_m, *new_v)
```

```python
import functools
import math

import numpy as np
import jax
import jax.numpy as jnp
from jax import lax
from jax.experimental import pallas as pl
from jax.experimental.pallas import tpu as pltpu

F32 = jnp.float32
_BF = jnp.bfloat16

N_DEV = 8
D = 1024
CTX = 256
HD = 64
N_ATT = 8
N_KV = 2
GRP = N_ATT // N_KV
WIN = 128
GRID_W = 64
ROPE_BASE = 10000.0
N_GLA = 8
DK = 32
DV = 64
GATE_RANK = 16
GATE_TAU = 16.0
FFN = 2816
EPS = 1e-6
NEG = -1e30
GLA_T = 128

QP = N_ATT * HD
KP = N_KV * HD
O_Q, O_K, O_V = 0, QP, QP + KP
O_GQ = O_V + KP
O_GK = O_GQ + N_GLA * DK
O_GV = O_GK + N_GLA * DK
O_GG = O_GV + N_GLA * DV
O_Z = O_GG + N_GLA * DV
NP = O_Z + 128
IN_COLS = 2336

ADAM_LR, ADAM_B1, ADAM_B2, ADAM_EPS, ADAM_WD, ADAM_STEP = 0.001, 0.9, 0.999, 1e-08, 0.01, 10

VMEM_BIG = 56 * 1024 * 1024
MESH = pl.DeviceIdType.MESH


def _cp(sem, vmem=None):
    return pltpu.CompilerParams(dimension_semantics=sem, vmem_limit_bytes=vmem)


def _full(shape):
    nd = len(shape)
    return pl.BlockSpec(shape, lambda *a: (0,) * nd)


def _rows(tile, width, off=0):
    return pl.BlockSpec((tile, width), lambda i: (i + off, 0))


def _rows_lat(tile, width):
    return pl.BlockSpec((tile, width), lambda i: (jnp.maximum(i - 1, 0), 0))


def _nt(a, b):
    return lax.dot_general(a, b, (((1,), (1,)), ((), ())), preferred_element_type=F32)


def _tn(a, b):
    return lax.dot_general(a, b, (((0,), (0,)), ((), ())), preferred_element_type=F32)


def _nn(a, b):
    return jnp.dot(a, b, preferred_element_type=F32)


def _head_mean(x, mavg):
    n = x.shape[0]
    hi = x.astype(_BF)
    lo = (x - hi.astype(F32)).astype(_BF)
    y = _nn(jnp.concatenate([hi, lo], axis=0), mavg)
    return y[0:n] + y[n:2 * n]


def _rope(t, cos, sa, sb):
    n = t.shape[1]
    reps = n // 128
    c = jnp.tile(cos, (1, reps))
    a = jnp.tile(sa, (1, reps))
    b = jnp.tile(sb, (1, reps))
    return t * c + pltpu.roll(t, n - 16, 1) * a + pltpu.roll(t, 16, 1) * b


def _unrope(t, cos, sa, sb):
    n = t.shape[1]
    reps = n // 128
    c = jnp.tile(cos, (1, reps))
    a = jnp.tile(sa, (1, reps))
    b = jnp.tile(sb, (1, reps))
    return t * c + pltpu.roll(t * a, 16, 1) + pltpu.roll(t * b, n - 16, 1)


def _sigmoid(x):
    return 1.0 / (1.0 + jnp.exp(-x))


def _inproj_fwd(x, ctx, gml, shl, gmc, shc, win, wg, bg, cos, sa, sb, shards):
    E = x.shape[0] + CTX
    TE = CTX

    def body(x_ref, c_ref, gml_ref, shl_ref, gmc_ref, shc_ref, w_ref, wg_ref, bg_ref, cos_ref, sa_ref, sb_ref,
             h_ref, q_ref, k_ref, v_ref, gq_ref, gk_ref, gv_ref, gg_ref, z_ref, la_ref):
        is_ctx = pl.program_id(0) == 0
        gm = jnp.where(is_ctx, gmc_ref[...], gml_ref[...])
        sh = jnp.where(is_ctx, shc_ref[...], shl_ref[...])
        x = jnp.where(is_ctx, c_ref[...], x_ref[...])
        r = lax.rsqrt(jnp.mean(x * x, axis=-1, keepdims=True) + EPS)
        hb = ((x * r) * gm + sh).astype(_BF)
        h_ref[...] = hb
        p = _nt(hb, w_ref[...])
        cos_t, sa_t, sb_t = cos_ref[...], sa_ref[...], sb_ref[...]
        q_ref[...] = (_rope(p[:, O_Q:O_K], cos_t, sa_t, sb_t) * (HD ** -0.5)).astype(_BF)
        k_ref[...] = _rope(p[:, O_K:O_V], cos_t, sa_t, sb_t).astype(_BF)
        v_ref[...] = p[:, O_V:O_GQ].astype(_BF)
        gq_ref[...] = p[:, O_GQ:O_GK] * (DK ** -0.5)
        gk_ref[...] = p[:, O_GK:O_GV]
        gv_ref[...] = p[:, O_GV:O_GG]
        gg_ref[...] = p[:, O_GG:O_Z]
        zb = p[:, O_Z:NP].astype(_BF)
        z_ref[...] = zb
        lg = _nn(zb, wg_ref[...]) + bg_ref[...]
        la_ref[...] = (jnp.minimum(lg, 0.0) - jnp.log(1.0 + jnp.exp(-jnp.abs(lg)))) * (1.0 / GATE_TAU)

    vec = _full((1, D))
    tab = _rows(TE, 128)
    outs = [(D, _BF), (QP, _BF), (KP, _BF), (KP, _BF), (256, F32), (256, F32), (512, F32), (512, F32),
            (128, _BF), (512, F32)]
    return _hosted_call(
        body, (x, ctx, gml, shl, gmc, shc, win, wg, bg, cos, sa, sb), shards, True,
        name="inproj_fwd", grid=(E // TE,),
        in_specs=[_rows_lat(TE, D), _full((CTX, D)), vec, vec, vec, vec, _full((NP, D)), _full((128, 512)),
                  _full((1, 512)), tab, tab, tab],
        out_specs=[_rows(TE, w) for w, _ in outs],
        out_shape=[jax.ShapeDtypeStruct((E, w), dt) for w, dt in outs],
        compiler_params=_cp(("arbitrary",), 40 * 1024 * 1024))


def _xchg_scratch(na):
    return [pltpu.SemaphoreType.DMA((na, N_DEV - 1)), pltpu.SemaphoreType.DMA((na, N_DEV - 1)),
            pltpu.SemaphoreType.DMA((na,))]


def _xchg_copies(ins, outs, send_sems, recv_sems, local_sems, gather):
    x, y, c = lax.axis_index("x"), lax.axis_index("y"), lax.axis_index("c")
    me = 4 * x + 2 * y + c
    local, sends, recvs = [], [], []
    for a in range(len(ins)):
        local.append(pltpu.make_async_copy(ins[a] if gather else ins[a].at[me], outs[a].at[me], local_sems.at[a]))
    for k in range(1, N_DEV):
        px, py, pc = x ^ (k >> 2), y ^ ((k >> 1) & 1), c ^ (k & 1)
        peer = 4 * px + 2 * py + pc
        for a in range(len(ins)):
            sems = dict(send_sem=send_sems.at[a, k - 1], recv_sem=recv_sems.at[a, k - 1], device_id_type=MESH)
            sends.append(pltpu.make_async_remote_copy(
                src_ref=ins[a] if gather else ins[a].at[peer], dst_ref=outs[a].at[me], device_id=(px, py, pc), **sems))
            recvs.append(pltpu.make_async_remote_copy(
                src_ref=ins[a] if gather else ins[a].at[me], dst_ref=outs[a].at[peer], device_id=(x, y, c), **sems))
    return local, sends, recvs


def _xchg_start(cps):
    local, sends, _ = cps
    for cp in local + sends:
        cp.start()


def _xchg_finish(cps):
    local, sends, recvs = cps
    for cp in recvs:
        cp.wait_recv()
    for cp in sends:
        cp.wait_send()
    for cp in local:
        cp.wait()


def _xchg_out_shapes(ins, gather):
    return [jax.ShapeDtypeStruct(((N_DEV,) + s.shape) if gather else s.shape, s.dtype) for s in ins]


def _hosted_call(body, args, hosted, gather, *, grid, in_specs, out_specs, out_shape, scratch_shapes=(), **kw):
    na = len(hosted)
    if na == 0:
        return pl.pallas_call(body, grid=grid, in_specs=in_specs, out_specs=out_specs, out_shape=out_shape,
                              scratch_shapes=list(scratch_shapes), **kw)(*args)
    n_in, n_out, n_scr = len(in_specs), len(out_specs), len(scratch_shapes)

    def wrapped(*refs):
        ins, h_in = refs[:n_in], refs[n_in:n_in + na]
        outs, h_out = refs[n_in + na:n_in + na + n_out], refs[n_in + na + n_out:n_in + 2 * na + n_out]
        scr = refs[n_in + 2 * na + n_out:]
        cps = _xchg_copies(h_in, h_out, *scr[n_scr:], gather=gather)
        pids = [pl.program_id(a) for a in range(len(grid))]
        first = functools.reduce(jnp.logical_and, [p == 0 for p in pids])
        last = functools.reduce(jnp.logical_and, [p == g - 1 for p, g in zip(pids, grid)])

        @pl.when(first)
        def _():
            _xchg_start(cps)

        body(*ins, *outs, *scr[:n_scr])

        @pl.when(last)
        def _():
            _xchg_finish(cps)

    anyspec = pl.BlockSpec(memory_space=pl.ANY)
    return pl.pallas_call(
        wrapped, grid=grid, in_specs=list(in_specs) + [anyspec] * na, out_specs=list(out_specs) + [anyspec] * na,
        out_shape=list(out_shape) + _xchg_out_shapes(hosted, gather),
        scratch_shapes=list(scratch_shapes) + _xchg_scratch(na), **kw)(*args, *hosted)


def _attn_specs(E):
    nb = (E - CTX) // WIN
    last = E // WIN - 1
    kc = pl.BlockSpec((CTX, KP), lambda n: (0, 0))
    kp = pl.BlockSpec((WIN, KP), lambda n: (n + 1, 0))
    kk = pl.BlockSpec((WIN, KP), lambda n: (n + 2, 0))
    kn = pl.BlockSpec((WIN, KP), lambda n: (jnp.minimum(n + 3, last), 0))
    return nb, [kc, kp, kk, kn]


def _attn_bias(nb):
    rows = np.arange(GRP * WIN)[:, None] % WIN
    cols = np.arange(CTX + 3 * WIN)[None, :]
    j = cols - CTX
    band = np.abs(j - WIN - rows) <= WIN
    out = []
    for first, last in ((True, False), (False, False), (False, True)):
        ok = (cols < CTX) | (band & ((j >= WIN) | (not first)) & ((j < 2 * WIN) | (not last)))
        out.append(np.where(ok, 0.0, NEG).astype(np.float32))
    bias = jnp.asarray(np.stack(out))
    spec = pl.BlockSpec((1, GRP * WIN, CTX + 3 * WIN),
                        lambda n: (jnp.where(n == 0, 0, jnp.where(n == nb - 1, 2, 1)), 0, 0))
    return bias, spec


def _both_halves(t, h):
    tf = t.astype(F32)
    r = pltpu.roll(tf, HD, 1)
    lo = lax.broadcasted_iota(jnp.int32, tf.shape, 1) < HD
    return (jnp.where(lo, tf, r) if h == 0 else jnp.where(lo, r, tf)).astype(t.dtype)


def _stack_heads(ref, h):
    lo = lax.broadcasted_iota(jnp.int32, (WIN, 128), 1) < HD
    parts = []
    for g in range(GRP):
        j = GRP * h + g
        t = ref[:, 128 * (j // 2):128 * (j // 2) + 128].astype(F32)
        parts.append(jnp.where(lo if j % 2 == 0 else jnp.logical_not(lo), t, 0.0))
    return jnp.concatenate(parts, axis=0)


def _unstack_pair(o, pp):
    lo = lax.broadcasted_iota(jnp.int32, (WIN, 128), 1) < HD
    return jnp.where(lo, o[WIN * 2 * pp:WIN * 2 * pp + WIN], o[WIN * (2 * pp + 1):WIN * (2 * pp + 1) + WIN])


def _attn_fwd(q, k, v, sink, shards):
    E = q.shape[0]
    S = E - CTX
    nb, kspecs = _attn_specs(E)
    na = len(shards)

    def body(q_ref, kc, kp, kk, kn, vc, vp, vk, vn, sink_ref, bias_ref, *rest):
        shard_refs, (o_ref, lse_ref, p_ref), got_refs = rest[:na], rest[na:na + 3], rest[na + 3:2 * na + 3]
        n = pl.program_id(0)
        cps = _xchg_copies(shard_refs, got_refs, *rest[2 * na + 3:], gather=True)

        @pl.when(n == 0)
        def _():
            _xchg_start(cps)

        lane = lax.broadcasted_iota(jnp.int32, (WIN, 128), 1)
        lse_t = jnp.zeros((WIN, 128), F32)
        kall = jnp.concatenate([kc[...], kp[...], kk[...], kn[...]], axis=0)
        vall = jnp.concatenate([vc[...], vp[...], vk[...], vn[...]], axis=0)
        K = [_both_halves(kall, h) for h in range(N_KV)]
        Q = [_stack_heads(q_ref, h).astype(_BF) for h in range(N_KV)]
        sk = [jnp.concatenate([jnp.broadcast_to(sink_ref[GRP * h + g:GRP * h + g + 1, 0:1], (WIN, 1))
                               for g in range(GRP)], axis=0) for h in range(N_KV)]
        s = [_nt(Q[h], K[h]) + bias_ref[0] for h in range(N_KV)]
        m = [jnp.maximum(jnp.max(s[h], axis=1, keepdims=True), sk[h]) for h in range(N_KV)]
        e = [jnp.exp(s[h] - m[h]) for h in range(N_KV)]
        den = [jnp.sum(e[h], axis=1, keepdims=True) + jnp.exp(sk[h] - m[h]) for h in range(N_KV)]
        V = [_both_halves(vall, h) for h in range(N_KV)]
        pb = [(e[h] * (1.0 / den[h])).astype(_BF) for h in range(N_KV)]
        for h in range(N_KV):
            p_ref[GRP * WIN * h:GRP * WIN * (h + 1), :] = pb[h]
        o = [_nn(pb[h], V[h]) for h in range(N_KV)]
        for h in range(N_KV):
            lse = m[h] + jnp.log(den[h])
            for g in range(GRP):
                lse_t = jnp.where(lane == GRP * h + g, lse[WIN * g:WIN * g + WIN], lse_t)
            for pp in range(GRP // 2):
                t = 2 * h + pp
                o_ref[:, 128 * t:128 * t + 128] = _unstack_pair(o[h], pp).astype(_BF)
        lse_ref[...] = lse_t

        @pl.when(n == nb - 1)
        def _():
            _xchg_finish(cps)

    qs = pl.BlockSpec((WIN, QP), lambda n: (n + 2, 0))
    anyspec = pl.BlockSpec(memory_space=pl.ANY)
    bias, bias_spec = _attn_bias(nb)
    return pl.pallas_call(
        body, name="attn_fwd", grid=(nb,),
        in_specs=[qs] + kspecs + kspecs + [_full((8, 128)), bias_spec] + [anyspec] * na,
        out_specs=[_rows(WIN, 512), _rows(WIN, 128), _rows(N_KV * GRP * WIN, CTX + 3 * WIN)] + [anyspec] * na,
        out_shape=[jax.ShapeDtypeStruct((S, 512), _BF), jax.ShapeDtypeStruct((S, 128), F32),
                   jax.ShapeDtypeStruct((nb * N_KV * GRP * WIN, CTX + 3 * WIN), _BF)]
        + _xchg_out_shapes(shards, True),
        scratch_shapes=_xchg_scratch(na),
        compiler_params=_cp(("arbitrary",)),
    )(q, k, k, k, k, v, v, v, v, sink, bias, *shards)


def _attn_bwd(q, k, v, sink, probs, lse, d_attn, slabs):
    E = q.shape[0]
    S = E - CTX
    nb, kspecs = _attn_specs(E)
    last = E // WIN - 1
    na = len(slabs)

    def body(q_ref, kc, kp, kk, kn, vc, vp, vk, vn, sink_ref, p_ref, lse_ref, do_ref, *rest):
        slab_refs, (dq_ref, dk_ref, dv_ref, ds_ref), got_refs = rest[:na], rest[na:na + 4], rest[na + 4:2 * na + 4]
        n = pl.program_id(0)
        cps = _xchg_copies(slab_refs, got_refs, *rest[2 * na + 4:], gather=False)

        @pl.when(n == 0)
        def _():
            _xchg_start(cps)
            dk_ref[...] = jnp.zeros_like(dk_ref)
            dv_ref[...] = jnp.zeros_like(dv_ref)
            ds_ref[...] = jnp.zeros_like(ds_ref)

        lane = lax.broadcasted_iota(jnp.int32, (WIN, 128), 1)
        lse_t = lse_ref[...]
        starts = [None, pl.multiple_of((n + 1) * WIN, WIN), pl.multiple_of((n + 2) * WIN, WIN),
                  pl.multiple_of(jnp.minimum(n + 3, last) * WIN, WIN)]
        kall = jnp.concatenate([kc[...], kp[...], kk[...], kn[...]], axis=0)
        vall = jnp.concatenate([vc[...], vp[...], vk[...], vn[...]], axis=0)
        for h in range(N_KV):
            hs = slice(HD * h, HD * h + HD)
            K = _both_halves(kall, h)
            V = _both_halves(vall, h)
            Q = _stack_heads(q_ref, h).astype(_BF)
            sk = jnp.concatenate([jnp.broadcast_to(sink_ref[GRP * h + g:GRP * h + g + 1, 0:1], (WIN, 1))
                                  for g in range(GRP)], axis=0)
            ls = jnp.concatenate([jnp.sum(jnp.where(lane == GRP * h + g, lse_t, 0.0), axis=1, keepdims=True)
                                  for g in range(GRP)], axis=0)
            do = _stack_heads(do_ref, h).astype(_BF)
            pb = p_ref[GRP * WIN * h:GRP * WIN * (h + 1), :]
            p = pb.astype(F32)
            dp = _nt(do, V)
            delta = jnp.sum(p * dp, axis=1, keepdims=True)
            dsc = (p * (dp - delta)).astype(_BF)
            dq = _nn(dsc, K) * (HD ** -0.5)
            for pp in range(GRP // 2):
                t = 2 * h + pp
                dq_ref[:, 128 * t:128 * t + 128] = _unstack_pair(dq, pp).astype(_BF)
            dK2 = _tn(Q, dsc)
            dV2 = _tn(do, pb)
            dK = dK2[0:HD] + dK2[HD:2 * HD]
            dV = dV2[0:HD] + dV2[HD:2 * HD]
            dk_ref[hs, 0:CTX] += dK[:, 0:CTX]
            dv_ref[hs, 0:CTX] += dV[:, 0:CTX]
            for w in range(1, 4):
                lo = CTX + WIN * (w - 1)
                dk_ref[hs, pl.ds(starts[w], WIN)] += dK[:, lo:lo + WIN]
                dv_ref[hs, pl.ds(starts[w], WIN)] += dV[:, lo:lo + WIN]
            psk = -jnp.exp(sk - ls) * delta
            for g in range(GRP):
                j = GRP * h + g
                tot = jnp.sum(psk[WIN * g:WIN * g + WIN], axis=0, keepdims=True)
                ds_ref[j:j + 1, :] += jnp.broadcast_to(tot, (1, 128))

        @pl.when(n == nb - 1)
        def _():
            _xchg_finish(cps)

    qs = pl.BlockSpec((WIN, QP), lambda n: (n + 2, 0))
    anyspec = pl.BlockSpec(memory_space=pl.ANY)
    return pl.pallas_call(
        body, name="attn_bwd", grid=(nb,),
        in_specs=[qs] + kspecs + kspecs + [_full((8, 128)), _rows(N_KV * GRP * WIN, CTX + 3 * WIN), _rows(WIN, 128),
                                            _rows(WIN, 512)] + [anyspec] * na,
        out_specs=[_rows(WIN, QP), _full((KP, E)), _full((KP, E)), _full((8, 128))] + [anyspec] * na,
        out_shape=[jax.ShapeDtypeStruct((S, QP), _BF), jax.ShapeDtypeStruct((KP, E), F32),
                   jax.ShapeDtypeStruct((KP, E), F32), jax.ShapeDtypeStruct((8, 128), F32)]
        + _xchg_out_shapes(slabs, False),
        scratch_shapes=_xchg_scratch(na),
        compiler_params=_cp(("arbitrary",), 48 * 1024 * 1024),
    )(q, k, k, k, k, v, v, v, v, sink, probs, lse, d_attn, *slabs)


def _gla_order(E, reverse, backward):
    nc = CTX // GLA_T
    n = E // GLA_T
    if not reverse:
        fwd = lambda s: s
    else:
        fwd = lambda s: jnp.where(s < nc, nc - 1 - s, n - 1 + nc - s)
    if backward:
        return lambda s: fwd(n - 1 - s)
    return fwd


def _gla_masks():
    T = GLA_T
    l128 = lax.broadcasted_iota(jnp.int32, (1, 128), 1)
    qmask = [((l128 >> 5) == j).astype(F32) for j in range(4)]
    vmask = [((l128 >> 6) == j).astype(F32) for j in range(2)]
    bd = ((lax.broadcasted_iota(jnp.int32, (512, 256), 0) >> 6)
          == (lax.broadcasted_iota(jnp.int32, (512, 256), 1) >> 5)).astype(F32)
    ri = lax.broadcasted_iota(jnp.int32, (T, 2 * T), 0)
    ci = lax.broadcasted_iota(jnp.int32, (T, 2 * T), 1) & (T - 1)
    return qmask, vmask, bd, ri, ci


def _tri_sum(tri, x):
    hi = x.astype(_BF)
    lo = (x - hi.astype(F32)).astype(_BF)
    n = x.shape[1]
    y = _nn(tri.astype(_BF), jnp.concatenate([hi, lo], axis=1))
    return y[:, 0:n] + y[:, n:2 * n]


def _gla_decays(la, reverse, ri, ci):
    T = GLA_T
    msk2 = (ri <= ci) if reverse else (ri >= ci)
    mskT2 = (ri >= ci) if reverse else (ri <= ci)
    b = _tri_sum(msk2[:, 0:T], la)
    bT = b[0:1] if reverse else b[T - 1:T]
    bm = b[T // 2:T // 2 + 1]
    return msk2, mskT2, b, bT, bm


def _pair_stack(tile, m0, m1):
    return jnp.concatenate([(tile * m0).astype(_BF), (tile * m1).astype(_BF)], axis=0)


def _gla_fwd(gq, gk, gv, la, shards=()):
    E = gq.shape[0]
    T = GLA_T
    n = E // T
    orders = [_gla_order(E, False, False), _gla_order(E, True, False)]

    def one_direction(reverse, gq_ref, gk_ref, gv_ref, la_ref, o_ref, st_ref, S_scr):
        qmask, vmask, bd, ri, ci = _gla_masks()
        msk2, _, b, bT, bm = _gla_decays(la_ref[...], reverse, ri, ci)
        q, k, v = gq_ref[...], gk_ref[...], gv_ref[...]
        qd = (q * jnp.exp(b)).astype(_BF)
        qm = (q * jnp.exp(b - bm)).astype(_BF)
        km = k * jnp.exp(bm - b)
        kd = (k * jnp.exp(bT - b)).astype(_BF)
        ST = S_scr[...]
        comp = ST[0:DV]
        for h in range(1, N_GLA):
            comp = comp + ST[DV * h:DV * h + DV]
        st_ref[0] = comp
        inter = _nt(qd, ST.astype(_BF))
        tiles = []
        for p in range(N_GLA // 2):
            qs = slice(128 * (p // 2), 128 * (p // 2) + 128)
            vs = slice(128 * p, 128 * p + 128)
            j0 = (2 * p) % 4
            KS = _pair_stack(km[:, qs], qmask[j0], qmask[j0 + 1])
            VS = _pair_stack(v[:, vs], vmask[0], vmask[1])
            AA = jnp.where(msk2, _nt(qm[:, qs], KS), 0.0).astype(_BF)
            tiles.append(_nn(AA, VS))
        o_ref[...] = inter + jnp.concatenate(tiles, axis=1)
        S_scr[...] = ST * jnp.exp(bT) + bd * _tn(v.astype(_BF), kd)

    def body(qf, kf, vf, lf, qr, kr, vr, lr, of, sf, orr, sr, S_f, S_r):
        @pl.when(pl.program_id(0) == 0)
        def _():
            S_f[...] = jnp.zeros_like(S_f)
            S_r[...] = jnp.zeros_like(S_r)

        one_direction(False, qf, kf, vf, lf, of, sf, S_f)
        one_direction(True, qr, kr, vr, lr, orr, sr, S_r)

    def blk(d, w, c=0):
        return pl.BlockSpec((T, w), lambda s: (orders[d](s), c))

    def st_spec(d):
        return pl.BlockSpec((1, DV, 256), lambda s: (orders[d](s), 0, 0))

    return _hosted_call(
        body, (gq, gk, gv, la, gq, gk, gv, la), shards, True, name="gla_fwd", grid=(n,),
        in_specs=[blk(0, 256), blk(0, 256), blk(0, 512), blk(0, 256, 0), blk(1, 256), blk(1, 256), blk(1, 512),
                  blk(1, 256, 1)],
        out_specs=[blk(0, 512), st_spec(0), blk(1, 512), st_spec(1)],
        out_shape=[jax.ShapeDtypeStruct((E, 512), F32), jax.ShapeDtypeStruct((n, DV, 256), F32)] * 2,
        scratch_shapes=[pltpu.VMEM((512, 256), F32)] * 2,
        compiler_params=_cp(("arbitrary",)))


def _gla_bwd(gq, gk, gv, la, z, st, do, reverse, prev=None, slabs=()):
    E = gq.shape[0]
    T = GLA_T
    n = E // T
    nc = CTX // T
    order = _gla_order(E, reverse, True)
    col = 1 if reverse else 0
    np_ = 0 if prev is None else 3

    def body(gq_ref, gk_ref, gv_ref, la_ref, z_ref, st_ref, do_ref, *rest):
        prev_refs = rest[:np_]
        dq_ref, dk_ref, dv_ref, dlg_ref, gwg_ref, bsum_ref, dS_scr = rest[np_:]

        @pl.when(pl.program_id(0) == 0)
        def _():
            dS_scr[...] = jnp.zeros_like(dS_scr)
            gwg_ref[...] = jnp.zeros_like(gwg_ref)
            bsum_ref[...] = jnp.zeros_like(bsum_ref)

        is_lat = order(pl.program_id(0)) >= nc
        qmask, vmask, bd, ri, ci = _gla_masks()
        msk2, mskT2, b, bT, bm = _gla_decays(la_ref[...], reverse, ri, ci)
        q, k, v = gq_ref[...], gk_ref[...], gv_ref[...]
        do = jnp.where(is_lat, do_ref[...].astype(F32), 0.0)
        e_b, e_qm, e_km, e_kd, e_T = jnp.exp(b), jnp.exp(b - bm), jnp.exp(bm - b), jnp.exp(bT - b), jnp.exp(bT)
        qd, qm, km, kd = q * e_b, q * e_qm, k * e_km, k * e_kd
        qdb, qmb, kmb, kdb, vb, dob = (t.astype(_BF) for t in (qd, qm, km, kd, v, do))
        ST = jnp.tile(st_ref[0], (N_GLA, 1)) * bd
        dST = dS_scr[...]
        dSTb = dST.astype(_BF)
        dqd = _nn(dob, ST.astype(_BF))
        dkd = _nn(vb, dSTb)
        dv_t, dqm_t, dkm_t = [], [None, None], [None, None]
        for p in range(N_GLA // 2):
            t = p // 2
            qs = slice(128 * t, 128 * t + 128)
            vs = slice(128 * p, 128 * p + 128)
            j0 = (2 * p) % 4
            QS = _pair_stack(qm[:, qs], qmask[j0], qmask[j0 + 1])
            KS = _pair_stack(km[:, qs], qmask[j0], qmask[j0 + 1])
            VS = _pair_stack(v[:, vs], vmask[0], vmask[1])
            DS = _pair_stack(do[:, vs], vmask[0], vmask[1])
            ATT = jnp.where(mskT2, _nt(kmb[:, qs], QS), 0.0).astype(_BF)
            dAA = jnp.where(msk2, _nt(dob[:, vs], VS), 0.0).astype(_BF)
            dATT = jnp.where(mskT2, _nt(vb[:, vs], DS), 0.0).astype(_BF)
            dv_t.append(_nn(ATT, DS))
            dq_p = _nn(dAA, KS)
            dk_p = _nn(dATT, QS)
            dqm_t[t] = dq_p if dqm_t[t] is None else dqm_t[t] + dq_p
            dkm_t[t] = dk_p if dkm_t[t] is None else dkm_t[t] + dk_p
        dqm = jnp.concatenate(dqm_t, axis=1)
        dkm = jnp.concatenate(dkm_t, axis=1)
        dq = dqm * e_qm + dqd * e_b
        dk = dkm * e_km + dkd * e_kd
        dv = _nt(kdb, dSTb) + jnp.concatenate(dv_t, axis=1)
        if prev is None:
            dq_ref[...], dk_ref[...], dv_ref[...] = dq, dk, dv
        else:
            dq_ref[...] = ((dq + prev_refs[0][...]) * (DK ** -0.5)).astype(_BF)
            dk_ref[...] = (dk + prev_refs[1][...]).astype(_BF)
            dv_ref[...] = (dv + prev_refs[2][...]).astype(_BF)
        db = dqm * qm - dkm * km + dqd * qd - dkd * kd
        dbT = jnp.sum(dkd * kd, axis=0, keepdims=True) + e_T * jnp.sum(dST * ST, axis=0, keepdims=True)
        dla = _tri_sum(mskT2[:, 0:T], db) + dbT
        dlg = dla * (1.0 - jnp.exp(GATE_TAU * la_ref[...])) * (1.0 / GATE_TAU)
        bsum_ref[0:1, :] += jnp.sum(dlg, axis=0, keepdims=True)
        dlgb = dlg.astype(_BF)
        dlg_ref[...] = dlgb
        gwg_ref[...] += _tn(z_ref[...], dlgb)
        dS_scr[...] = dST * e_T + bd * _tn(dob, qdb)

    blk = lambda w, c=0: pl.BlockSpec((T, w), lambda s: (order(s), c))
    do_spec = pl.BlockSpec((T, 512), lambda s: (jnp.maximum(order(s) - nc, 0), 0))
    odt = F32 if prev is None else _BF
    return _hosted_call(
        body, (gq, gk, gv, la, z, st, do) + (() if prev is None else tuple(prev)), slabs, False,
        name="gla_bwd_rev" if reverse else "gla_bwd", grid=(n,),
        in_specs=[blk(256), blk(256), blk(512), blk(256, col), blk(128),
                  pl.BlockSpec((1, DV, 256), lambda s: (order(s), 0, 0)), do_spec]
        + ([] if prev is None else [blk(256), blk(256), blk(512)]),
        out_specs=[blk(256), blk(256), blk(512), blk(256), _full((128, 256)), _full((8, 256))],
        out_shape=[jax.ShapeDtypeStruct((E, 256), odt), jax.ShapeDtypeStruct((E, 256), odt),
                   jax.ShapeDtypeStruct((E, 512), odt), jax.ShapeDtypeStruct((E, 256), _BF),
                   jax.ShapeDtypeStruct((128, 256), F32), jax.ShapeDtypeStruct((8, 256), F32)],
        scratch_shapes=[pltpu.VMEM((512, 256), F32)],
        compiler_params=_cp(("arbitrary",)))


def _gla_out(o, gg, ggla, mavg):
    rr = lax.rsqrt(_head_mean(o * o, mavg) + EPS)
    oh = o * rr
    sg = _sigmoid(gg)
    return oh, rr, sg


def _mix_fwd(x, attn, o_f, o_b, gg, ggla, mavg, wout, gt1, g2):
    S = x.shape[0]
    TM = 256

    def body(x_ref, a_ref, of_ref, ob_ref, gg_ref, ggla_ref, mavg_ref, w_ref, gt1_ref, g2_ref, x1_ref, mix_ref):
        gg_t = gg_ref[...]
        oh, _, sg = _gla_out(of_ref[...] + ob_ref[...], gg_t, ggla_ref[...], mavg_ref[...])
        mix_ref[:, 0:512] = a_ref[...]
        mix_ref[:, 512:1024] = (oh * ggla_ref[...] * (gg_t * sg)).astype(_BF)
        y = _nn(mix_ref[...], w_ref[...])
        ry = lax.rsqrt(jnp.mean(y * y, axis=-1, keepdims=True) + EPS)
        x1_ref[...] = x_ref[...] + gt1_ref[...] * ((y * ry) * g2_ref[...])

    return pl.pallas_call(
        body, name="mix_fwd", grid=(S // TM,),
        in_specs=[_rows(TM, D), _rows(TM, 512), _rows(TM, 512, 1), _rows(TM, 512, 1), _rows(TM, 512, 1),
                  _full((1, 512)), _full((512, 512)), _full((D, D)), _full((1, D)), _full((1, D))],
        out_specs=[_rows(TM, D), _rows(TM, D)],
        out_shape=[jax.ShapeDtypeStruct((S, D), F32), jax.ShapeDtypeStruct((S, D), _BF)],
        compiler_params=_cp(("arbitrary",), 40 * 1024 * 1024),
    )(x, attn, o_f, o_b, gg, ggla, mavg, wout, gt1, g2)


def _mix_bwd(dx1, mix, o_f, o_b, gg, ggla, mavg, wout, gt1, g2, slabs):
    S = dx1.shape[0]
    TM = 256

    def body(dx_ref, mix_ref, of_ref, ob_ref, gg_ref, ggla_ref, mavg_ref, w_ref, gt1_ref, g2_ref,
             da_ref, do_ref, dgg_ref, dy_ref, sums_ref):
        @pl.when(pl.program_id(0) == 0)
        def _():
            sums_ref[...] = jnp.zeros_like(sums_ref)

        dx = dx_ref[...]
        y = _nn(mix_ref[...], w_ref[...])
        ry = lax.rsqrt(jnp.mean(y * y, axis=-1, keepdims=True) + EPS)
        yh = y * ry
        sums_ref[0:1, :] += jnp.sum(dx * yh, axis=0, keepdims=True)
        dyh = dx * (gt1_ref[...] * g2_ref[...])
        dy = (ry * (dyh - yh * jnp.mean(dyh * yh, axis=-1, keepdims=True))).astype(_BF)
        dy_ref[...] = dy
        dmix = _nt(dy, w_ref[...])
        da_ref[...] = dmix[:, 0:512].astype(_BF)
        dgla = dmix[:, 512:1024]
        gg_t = gg_ref[...]
        ggla_t = ggla_ref[...]
        oh, rr, sg = _gla_out(of_ref[...] + ob_ref[...], gg_t, ggla_t, mavg_ref[...])
        dgg_ref[...] = (dgla * oh * ggla_t * (sg * (1.0 + gg_t * (1.0 - sg)))).astype(_BF)
        don = dgla * (gg_t * sg)
        sums_ref[1:2, 0:512] += jnp.sum(don * oh, axis=0, keepdims=True)
        doh = don * ggla_t
        do_ref[...] = (rr * (doh - oh * _head_mean(doh * oh, mavg_ref[...]))).astype(_BF)

    return _hosted_call(
        body, (dx1, mix, o_f, o_b, gg, ggla, mavg, wout, gt1, g2), slabs, False,
        name="mix_bwd", grid=(S // TM,),
        in_specs=[_rows(TM, D), _rows(TM, D), _rows(TM, 512, 1), _rows(TM, 512, 1), _rows(TM, 512, 1),
                  _full((1, 512)), _full((512, 512)), _full((D, D)), _full((1, D)), _full((1, D))],
        out_specs=[_rows(TM, 512), _rows(TM, 512), _rows(TM, 512), _rows(TM, D), _full((8, D))],
        out_shape=[jax.ShapeDtypeStruct((S, 512), _BF), jax.ShapeDtypeStruct((S, 512), _BF),
                   jax.ShapeDtypeStruct((S, 512), _BF), jax.ShapeDtypeStruct((S, D), _BF),
                   jax.ShapeDtypeStruct((8, D), F32)],
        compiler_params=_cp(("arbitrary",), 40 * 1024 * 1024))


def _ffn(x1, target, gm2, sh2, gt2, g4, wffi, wffo):
    S = x1.shape[0]
    TF = 256

    def body(x_ref, t_ref, gm_ref, sh_ref, gt_ref, g4_ref, wi_hbm, wo_hbm,
             dx_ref, h_ref, du_ref, act_ref, df_ref, sums_ref, loss_ref, wi, wo, sem):
        @pl.when(pl.program_id(0) == 0)
        def _():
            c1 = pltpu.make_async_copy(wi_hbm, wi, sem.at[0])
            c2 = pltpu.make_async_copy(wo_hbm, wo, sem.at[1])
            c1.start()
            c2.start()
            sums_ref[...] = jnp.zeros_like(sums_ref)
            loss_ref[...] = jnp.zeros_like(loss_ref)
            c1.wait()
            c2.wait()

        x = x_ref[...]
        gm = gm_ref[...]
        r = lax.rsqrt(jnp.mean(x * x, axis=-1, keepdims=True) + EPS)
        xh = x * r
        hb = (xh * gm + sh_ref[...]).astype(_BF)
        h_ref[...] = hb
        u = _nt(hb, wi[...])
        g = u[:, 0:FFN]
        up = u[:, FFN:2 * FFN]
        sg = _sigmoid(g)
        sl = g * sg
        ab = (sl * up).astype(_BF)
        act_ref[...] = ab
        f = _nn(ab, wo[...])
        rf = lax.rsqrt(jnp.mean(f * f, axis=-1, keepdims=True) + EPS)
        fh = f * rf
        gt, g4v = gt_ref[...], g4_ref[...]
        err = x + gt * (fh * g4v) - t_ref[...]
        loss_ref[...] += jnp.sum(err * err) * (0.5 / D)
        dout = err * (1.0 / D)
        sums_ref[2:3, :] += jnp.sum(dout * fh, axis=0, keepdims=True)
        dfh = dout * (gt * g4v)
        dfb = (rf * (dfh - fh * jnp.mean(dfh * fh, axis=-1, keepdims=True))).astype(_BF)
        df_ref[...] = dfb
        dact = _nt(dfb, wo[...])
        du_ref[:, 0:FFN] = (dact * up * (sg * (1.0 + g * (1.0 - sg)))).astype(_BF)
        du_ref[:, FFN:2 * FFN] = (dact * sl).astype(_BF)
        dh = _nn(du_ref[...], wi[...])
        sums_ref[0:1, :] += jnp.sum(dh, axis=0, keepdims=True)
        sums_ref[1:2, :] += jnp.sum(dh * xh, axis=0, keepdims=True)
        dxh = dh * gm
        dx_ref[...] = dout + r * (dxh - xh * jnp.mean(dxh * xh, axis=-1, keepdims=True))

    vec = _full((1, D))
    anyspec = pl.BlockSpec(memory_space=pl.ANY)
    return pl.pallas_call(
        body, name="ffn_fwd_bwd", grid=(S // TF,),
        in_specs=[_rows(TF, D), _rows(TF, D), vec, vec, vec, vec, anyspec, anyspec],
        out_specs=[_rows(TF, D), _rows(TF, D), _rows(TF, 2 * FFN), _rows(TF, FFN), _rows(TF, D),
                   _full((8, D)), _full((8, 128))],
        out_shape=[jax.ShapeDtypeStruct((S, D), F32), jax.ShapeDtypeStruct((S, D), _BF),
                   jax.ShapeDtypeStruct((S, 2 * FFN), _BF), jax.ShapeDtypeStruct((S, FFN), _BF),
                   jax.ShapeDtypeStruct((S, D), _BF), jax.ShapeDtypeStruct((8, D), F32),
                   jax.ShapeDtypeStruct((8, 128), F32)],
        scratch_shapes=[pltpu.VMEM((2 * FFN, D), _BF), pltpu.VMEM((FFN, D), _BF), pltpu.SemaphoreType.DMA((2,))],
        compiler_params=_cp(("arbitrary",), VMEM_BIG),
    )(x1, target, gm2, sh2, gt2, g4, wffi, wffo)


def _inproj_bwd(x, ctx, gml, gmc, win, wg, cos, sa, sb, dq, dk, dv, dgq, dgk, dgv, dgg, dlg_f, dlg_b, dx1):
    S = x.shape[0]
    E = S + CTX
    TE = CTX

    def body(x_ref, c_ref, gml_ref, gmc_ref, w_ref, wg_ref, cos_ref, sa_ref, sb_ref, dq_ref, dk_ref, dv_ref,
             gq_ref, gk_ref, gv_ref, dgg_ref, dlf, dlb, dx1_ref, dp_ref, gx_ref, sums_ref):
        i = pl.program_id(0)
        is_ctx = i == 0

        @pl.when(is_ctx)
        def _():
            sums_ref[...] = jnp.zeros_like(sums_ref)

        lat = jnp.where(is_ctx, 0.0, 1.0)
        cos_t, sa_t, sb_t = cos_ref[...], sa_ref[...], sb_ref[...]
        dp_ref[:, O_Q:O_K] = (_unrope(dq_ref[...].astype(F32), cos_t, sa_t, sb_t) * lat).astype(_BF)
        dp_ref[:, O_K:O_V] = _unrope(dk_ref[...].T, cos_t, sa_t, sb_t).astype(_BF)
        dp_ref[:, O_V:O_GQ] = dv_ref[...].T.astype(_BF)
        dp_ref[:, O_GQ:O_GK] = gq_ref[...]
        dp_ref[:, O_GK:O_GV] = gk_ref[...]
        dp_ref[:, O_GV:O_GG] = gv_ref[...]
        dp_ref[:, O_GG:O_Z] = (dgg_ref[...].astype(F32) * lat).astype(_BF)
        dlg = jnp.concatenate([dlf[...], dlb[...]], axis=1)
        dp_ref[:, O_Z:NP] = _nt(dlg, wg_ref[...]).astype(_BF)
        dh = _nn(dp_ref[...], w_ref[...])
        x = jnp.where(is_ctx, c_ref[...], x_ref[...])
        r = lax.rsqrt(jnp.mean(x * x, axis=-1, keepdims=True) + EPS)
        xh = x * r
        sdh = jnp.sum(dh, axis=0, keepdims=True)
        sdx = jnp.sum(dh * xh, axis=0, keepdims=True)
        sums_ref[0:1, :] += sdh * lat
        sums_ref[1:2, :] += sdx * lat
        sums_ref[2:3, :] += sdh * (1.0 - lat)
        sums_ref[3:4, :] += sdx * (1.0 - lat)
        dxh = dh * jnp.where(is_ctx, gmc_ref[...], gml_ref[...])
        gx_ref[...] = dx1_ref[...] + r * (dxh - xh * jnp.mean(dxh * xh, axis=-1, keepdims=True))

    vec = _full((1, D))
    tab = _rows(TE, 128)
    return pl.pallas_call(
        body, name="inproj_bwd", grid=(E // TE,),
        in_specs=[_rows_lat(TE, D), _full((CTX, D)), vec, vec, _full((NP, D)), _full((128, 512)), tab, tab, tab,
                  _rows_lat(TE, QP), pl.BlockSpec((KP, TE), lambda i: (0, i)), pl.BlockSpec((KP, TE), lambda i: (0, i)),
                  _rows(TE, 256), _rows(TE, 256), _rows(TE, 512), _rows_lat(TE, 512), _rows(TE, 256), _rows(TE, 256),
                  _rows_lat(TE, D)],
        out_specs=[_rows(TE, NP), _rows_lat(TE, D), _full((8, D))],
        out_shape=[jax.ShapeDtypeStruct((E, NP), _BF), jax.ShapeDtypeStruct((S, D), F32),
                   jax.ShapeDtypeStruct((8, D), F32)],
        compiler_params=_cp(("arbitrary",), VMEM_BIG),
    )(x, ctx, gml, gmc, win, wg, cos, sa, sb, dq, dk, dv, dgq, dgk, dgv, dgg, dlg_f, dlg_b, dx1)


def _matmul_tn(a, b, tk, tt, name, out_dtype, transpose_out=False, a_cols=None, hosted=(), gather=True):
    T, KA = a.shape
    N = b.shape[1]
    nt = T // tt
    k0 = 0
    if a_cols is not None:
        KA, k0 = tk, a_cols

    def body(a_ref, b_ref, o_ref, acc):
        t = pl.program_id(1)

        @pl.when(t == 0)
        def _():
            acc[...] = jnp.zeros_like(acc)

        acc[...] += _tn(a_ref[...], b_ref[...])

        @pl.when(t == nt - 1)
        def _():
            o_ref[...] = (acc[...].T if transpose_out else acc[...]).astype(out_dtype)

    if transpose_out:
        out_spec, out_shape = pl.BlockSpec((N, tk), lambda i, t: (0, i)), (N, KA)
    else:
        out_spec, out_shape = pl.BlockSpec((tk, N), lambda i, t: (i, 0)), (KA, N)
    res = _hosted_call(
        body, (a, b), hosted, gather, name=name, grid=(KA // tk, nt),
        in_specs=[pl.BlockSpec((tt, tk), lambda i, t: (t, i + k0)), pl.BlockSpec((tt, N), lambda i, t: (t, 0))],
        out_specs=[out_spec], out_shape=[jax.ShapeDtypeStruct(out_shape, out_dtype)],
        scratch_shapes=[pltpu.VMEM((tk, N), F32)],
        compiler_params=_cp(("arbitrary", "arbitrary"), VMEM_BIG))
    return res if hosted else res[0]


def _ada_bwd(c_all, c_ctx, w_ada, d_all):
    n = w_ada.shape[1]

    def body(c_ref, cc_ref, w_ref, d_ref, gw_ref, t_ref):
        c = jnp.concatenate([c_ref[...], jnp.broadcast_to(cc_ref[...], (8, D))], axis=0)
        db = d_ref[...].astype(_BF)
        gw_ref[0] = _tn((c * _sigmoid(c)).astype(_BF), db)
        t_ref[...] = _nt(db[8:16], w_ref[...].astype(_BF))

    return pl.pallas_call(
        body, name="ada_bwd", in_specs=[_full((8, D)), _full((1, D)), _full((D, n)), _full((16, n))],
        out_specs=[_full((1, D, n)), _full((8, D))],
        out_shape=[jax.ShapeDtypeStruct((1, D, n), F32), jax.ShapeDtypeStruct((8, D), F32)], grid=(1,),
        compiler_params=_cp(("arbitrary",)),
    )(c_all, c_ctx, w_ada, d_all)


PART_ROWS = 56
R_ADA, R_ADA_C, R_GAIN, R_SINK, R_BG, R_GGLA, R_LOSS, R_WG = 0, 6, 12, 16, 17, 18, 19, 24


def _small_grads(s_in, s_ffn, s_mix, ada_l, ada_c, gains, dsink, s_bg, g_wg, loss):
    def body(si, sf, sm, al, ac, g, ds, sbg, gwg, loss_ref, o_ref):
        o_ref[...] = jnp.zeros_like(o_ref)
        o_ref[R_LOSS:R_LOSS + 1, 0:128] = loss_ref[0:1, :]
        sub = lax.broadcasted_iota(jnp.int32, (8, 128), 0)
        lane = lax.broadcasted_iota(jnp.int32, (8, 128), 1)
        o_ref[R_SINK:R_SINK + 1, 0:128] = jnp.sum(jnp.where(sub == lane, ds[...], 0.0), axis=0, keepdims=True)
        o_ref[R_BG:R_BG + 1, 0:512] = sbg[0:1, :]
        y = sm[1:2, 0:128] + sm[1:2, 128:256] + sm[1:2, 256:384] + sm[1:2, 384:512]
        y = y + pltpu.roll(y, 64, 1)
        o_ref[R_GGLA:R_GGLA + 1, 0:128] = jnp.where(lane[0:1] < DV, y, 0.0)
        o_ref[R_WG:R_WG + 16, 0:256] = gwg[0:16, 0:256]
        o_ref[R_WG + 16:R_WG + 32, 0:256] = gwg[16:32, 256:512]
        sdh_l, sdx_l, sdh_c, sdx_c = si[0:1], si[1:2], si[2:3], si[3:4]
        sdh2, sdx2, a2 = sf[0:1], sf[1:2], sf[2:3]
        a1 = sm[0:1]
        g1, g2, g3, g4 = g[0:1], g[1:2], g[2:3], g[3:4]
        sc1, gt1, sc2, gt2 = al[1:2], al[2:3], al[4:5], al[5:6]
        sc1c = ac[1:2]
        z = jnp.zeros((1, D), F32)
        rows = [sdh_l, sdx_l * g1, a1 * g2, sdh2, sdx2 * g3, a2 * g4,
                sdh_c, sdx_c * g1, z, z, z, z,
                sdx_l * (1.0 + sc1) + sdx_c * (1.0 + sc1c), a1 * gt1, sdx2 * (1.0 + sc2), a2 * gt2]
        for r, v in enumerate(rows):
            o_ref[r:r + 1, :] = v

    v8 = _full((8, D))
    return pl.pallas_call(
        body, name="small_grads",
        in_specs=[v8] * 6 + [_full((8, 128)), _full((8, 512)), _full((128, 512)), _full((8, 128))],
        out_specs=_full((PART_ROWS, D)), out_shape=jax.ShapeDtypeStruct((PART_ROWS, D), F32), grid=(1,),
        compiler_params=_cp(("arbitrary",)),
    )(s_in, s_ffn, s_mix, ada_l, ada_c, gains, dsink, s_bg, g_wg, loss)


def _row_tile(R):
    for cand in (256, 128, 64, 32, 16):
        if R % cand == 0 and R > cand:
            return cand
    return R


def _adamw(w, gs, m, v, name):
    _, R, C = w.shape
    tr = _row_tile(R)
    c1 = 1.0 / (1.0 - ADAM_B1 ** ADAM_STEP)
    c2 = 1.0 / (1.0 - ADAM_B2 ** ADAM_STEP)
    ng = len(gs)

    def body(w_ref, *refs):
        g_refs, (m_ref, v_ref, go_ref, d_ref, nm_ref, nv_ref) = refs[:ng], refs[ng:]
        c0 = 0
        for g_ref in g_refs:
            cols = slice(c0, c0 + g_ref.shape[2])
            c0 += g_ref.shape[2]
            gg = g_ref[0].astype(F32)
            for j in range(1, g_ref.shape[0]):
                gg = gg + g_ref[j].astype(F32)
            go_ref[0, :, cols] = gg
            nm = ADAM_B1 * m_ref[0, :, cols] + (1.0 - ADAM_B1) * gg
            nv = ADAM_B2 * v_ref[0, :, cols] + (1.0 - ADAM_B2) * (gg * gg)
            nm_ref[0, :, cols] = nm
            nv_ref[0, :, cols] = nv
            d_ref[0, :, cols] = -ADAM_LR * ((nm * c1) / (jnp.sqrt(nv * c2) + ADAM_EPS) + ADAM_WD * w_ref[0, :, cols])

    spec = pl.BlockSpec((1, tr, C), lambda i: (0, i, 0))
    sds = jax.ShapeDtypeStruct((1, R, C), F32)
    g_specs = [pl.BlockSpec((g.shape[0], tr, g.shape[2]), lambda i: (0, i, 0)) for g in gs]
    return pl.pallas_call(
        body, name=name, grid=(R // tr,), in_specs=[spec] + g_specs + [spec, spec], out_specs=[spec] * 4,
        out_shape=[sds] * 4, compiler_params=_cp(("parallel",), 48 * 1024 * 1024),
    )(w, *gs, m, v)


def _sum_slots(slots, name):
    _, R, C = slots.shape
    tr = _row_tile(R)

    def body(s_ref, o_ref):
        acc = s_ref[0].astype(F32)
        for j in range(1, N_DEV):
            acc = acc + s_ref[j].astype(F32)
        o_ref[...] = acc

    return pl.pallas_call(
        body, name=name, grid=(R // tr,), in_specs=[pl.BlockSpec((N_DEV, tr, C), lambda i: (0, i, 0))],
        out_specs=_rows(tr, C), out_shape=jax.ShapeDtypeStruct((R, C), F32), compiler_params=_cp(("parallel",)),
    )(slots)


def _ag2_start(x_ref, out_ref, send_sems, recv_sems, local_sem):
    x, y, c = lax.axis_index("x"), lax.axis_index("y"), lax.axis_index("c")
    me, sibling = (x, y, c), (x, y, 1 - c)
    chips = [(1 - x, y), (x, 1 - y), (1 - x, 1 - y)]

    def rows(px, py, pc):
        return out_ref.at[4 * px + 2 * py + pc]

    def copy(k, block, to, src=None):
        return pltpu.make_async_remote_copy(
            src_ref=rows(*block) if src is None else src, dst_ref=rows(*block),
            send_sem=send_sems.at[k], recv_sem=recv_sems.at[k], device_id=to, device_id_type=MESH)

    mine = pltpu.make_async_copy(x_ref, rows(*me), local_sem)
    mine.start()
    first = [copy(0, me, sibling, src=x_ref)]
    first += [copy(1 + j, me, (*chip, c), src=x_ref) for j, chip in enumerate(chips)]
    for cp in first:
        cp.start()
    return copy, mine, first, me, sibling, chips, c


def _ag2_finish(state):
    copy, mine, first, me, sibling, chips, c = state
    passed = [copy(4 + j, (*chip, c), sibling) for j, chip in enumerate(chips)]
    for j, chip in enumerate(chips):
        copy(1 + j, (*chip, c), me).wait_recv()
        passed[j].start()
    copy(0, sibling, me).wait_recv()
    for j, chip in enumerate(chips):
        copy(4 + j, (*chip, 1 - c), me).wait_recv()
    for cp in first + passed:
        cp.wait_send()
    mine.wait()


def _exchange(arrays, name, gather):
    na = len(arrays)

    def body(*refs):
        cps = _xchg_copies(refs[:na], refs[na:2 * na], *refs[2 * na:], gather=gather)
        _xchg_start(cps)
        _xchg_finish(cps)

    anyspec = pl.BlockSpec(memory_space=pl.ANY)
    return pl.pallas_call(
        body, name=name, out_shape=_xchg_out_shapes(arrays, gather), in_specs=[anyspec] * na,
        out_specs=[anyspec] * na, scratch_shapes=_xchg_scratch(na),
    )(*arrays)


def _entry(c, wg_sh, win_sh, c_ctx, w_ada):
    n = w_ada.shape[1]

    def body(c_ref, g_ref, w_ref, cc_ref, wa_ref, call_ref, gall_ref, wall_ref, ada_ref, part,
             s_send, s_recv, s_loc, w_send, w_recv, w_loc, a_send, a_recv, a_loc):
        big = _ag2_start(w_ref, wall_ref, w_send, w_recv, w_loc)
        small = _xchg_copies([c_ref, g_ref], [call_ref, gall_ref], s_send, s_recv, s_loc, gather=True)
        _xchg_start(small)
        _xchg_finish(small)
        cs = jnp.concatenate([call_ref[:, 0, :], jnp.broadcast_to(cc_ref[...], (8, D))], axis=0)
        part[...] = _nn((cs * _sigmoid(cs)).astype(_BF), wa_ref[...].astype(_BF))
        ada = _xchg_copies([part], [ada_ref], a_send, a_recv, a_loc, gather=True)
        _xchg_start(ada)
        _xchg_finish(ada)
        _ag2_finish(big)

    vm = pl.BlockSpec(memory_space=pltpu.VMEM)
    return pl.pallas_call(
        body, name="entry_gather",
        out_shape=[jax.ShapeDtypeStruct((N_DEV,) + c.shape, F32), jax.ShapeDtypeStruct((N_DEV,) + wg_sh.shape, F32),
                   jax.ShapeDtypeStruct((N_DEV,) + win_sh.shape, win_sh.dtype),
                   jax.ShapeDtypeStruct((N_DEV, 16, n), F32)],
        in_specs=[vm] * 5, out_specs=[vm] * 4,
        scratch_shapes=[pltpu.VMEM((16, n), F32)] + _xchg_scratch(2)
        + [pltpu.SemaphoreType.DMA((7,)), pltpu.SemaphoreType.DMA((7,)), pltpu.SemaphoreType.DMA] + _xchg_scratch(1),
        compiler_params=pltpu.CompilerParams(vmem_limit_bytes=VMEM_BIG),
    )(c, wg_sh, win_sh, c_ctx, w_ada)


def _rope_tables(S):
    t = np.arange(S)
    row = (t // GRID_W).astype(np.float32)
    colp = (t % GRID_W).astype(np.float32)
    half = HD // 2
    inv = (ROPE_BASE ** (-np.arange(0, half, 2, dtype=np.float32) / half)).astype(np.float32)
    ar = row[:, None] * inv[None, :]
    ac = colp[:, None] * inv[None, :]
    ang = np.concatenate([ar, ar, ac, ac], axis=-1).astype(np.float32)
    cos = np.cos(ang).astype(np.float32)
    sin = np.sin(ang).astype(np.float32)
    lane = np.arange(HD)
    first = (lane % 32) < 16
    sa = np.where(first[None, :], -sin, 0.0)
    sb = np.where(first[None, :], 0.0, sin)

    def ext(tab, ctx_val):
        full = np.zeros((CTX + S, 128), np.float32)
        full[:CTX, :] = ctx_val
        full[CTX:, :HD] = tab
        full[CTX:, HD:] = tab
        return jnp.asarray(full)

    return ext(cos, 1.0), ext(sa, 0.0), ext(sb, 0.0)


def _pad_rows_win(wt):
    return jnp.pad(wt, ((0, NP - IN_COLS), (0, 0)))


def _unpad_rows_win(g):
    return g[0:IN_COLS]


def _local_step(x, ctx, target, ada_l, ada_c, gains, sink, win_p, wg_bd, bg, ggla, wout_sh, wffi_sh, wffo_sh):
    S = x.shape[0]
    cos, sa, sb = _rope_tables(S)
    g1, g2, g3, g4 = (gains[i:i + 1] for i in range(4))
    sh1, sc1, gt1, sh2, sc2, gt2 = (ada_l[i:i + 1] for i in range(6))
    sh1c, sc1c = ada_c[0:1], ada_c[1:2]
    gml, gmc, gm2 = g1 * (1.0 + sc1), g1 * (1.0 + sc1c), g3 * (1.0 + sc2)
    mavg = jnp.asarray(np.kron(np.eye(N_GLA, dtype=np.float32), np.full((DV, DV), 1.0 / DV, np.float32))).astype(_BF)

    n_ffi, r_ffo, r_out = wffi_sh.shape[0], wffo_sh.shape[0], wout_sh.shape[0]
    tt_e = 768 if (S + CTX) % 768 == 0 else 256
    tt_s = 512 if S % 512 == 0 else 256
    h, q, k, v, gq, gk, gv, gg, z, la, wout_g = _inproj_fwd(x, ctx, gml, sh1, gmc, sh1c, win_p, wg_bd, bg,
                                                            cos, sa, sb, [wout_sh])
    attn, lse, probs, wffi_g = _attn_fwd(q, k, v, sink, [wffi_sh])
    o_f, st_f, o_b, st_b, wffo_g = _gla_fwd(gq, gk, gv, la, [wffo_sh])
    wout = wout_g.reshape(N_DEV * r_out, D)
    wffi = wffi_g.reshape(N_DEV * n_ffi, D)
    wffo = wffo_g.reshape(N_DEV * r_ffo, D)
    x1, mix = _mix_fwd(x, attn, o_f, o_b, gg, ggla, mavg, wout, gt1, g2)
    dx1, h2, du, act, df, s_ffn, loss = _ffn(x1, target, gm2, sh2, gt2, g4, wffi, wffo)
    slab_ffi = _matmul_tn(h2, du, 512, tt_s, "grad_w_ffn_in", _BF, True).reshape(N_DEV, n_ffi, D)
    slab_ffo = _matmul_tn(act, df, FFN, tt_s, "grad_w_ffn_out", _BF).reshape(N_DEV, r_ffo, D)
    d_attn, do_gla, dgg, dy, s_mix, got_ffo = _mix_bwd(dx1, mix, o_f, o_b, gg, ggla, mavg, wout, gt1, g2, [slab_ffo])
    slab_out = _matmul_tn(mix, dy, D, tt_s, "grad_w_out", _BF).reshape(N_DEV, r_out, D)
    dq, dk, dv, dsink, got_ffi = _attn_bwd(q, k, v, sink, probs, lse, d_attn, [slab_ffi])
    pq, pk, pv, dlg_f, gwg_f, sbg_f, got_out = _gla_bwd(gq, gk, gv, la, z, st_f, do_gla, False, None, [slab_out])
    dgq, dgk, dgv, dlg_b, gwg_b, sbg_b = _gla_bwd(gq, gk, gv, la, z, st_b, do_gla, True, (pq, pk, pv))
    dp, grad_x, s_in = _inproj_bwd(x, ctx, gml, gmc, win_p, wg_bd, cos, sa, sb, dq, dk, dv,
                                   dgq, dgk, dgv, dgg, dlg_f, dlg_b, dx1)
    g_wg = jnp.concatenate([gwg_f, gwg_b], axis=1)
    s_bg = jnp.concatenate([sbg_f, sbg_b], axis=1)
    small = _small_grads(s_in, s_ffn, s_mix, ada_l, ada_c, gains, dsink, s_bg, g_wg, loss)
    n_in, n_grp = IN_COLS // N_DEV, 2
    got_in, slab = [], None
    for j in range(n_grp):
        g_j, got = _matmul_tn(h, dp, D // n_grp, tt_e, "grad_w_in_%d" % j, _BF, True, a_cols=j,
                              hosted=[small] if j == 0 else [slab], gather=(j == 0))
        if j == 0:
            parts = got
        else:
            got_in.append(got)
        slab = _unpad_rows_win(g_j).reshape(N_DEV, n_in, D // n_grp)
    got_in.append(_exchange([slab], "scatter_grads", False)[0])
    return dict(grad_x=grad_x, got_in=got_in, got_out=got_out, got_ffi=got_ffi, got_ffo=got_ffo, parts=parts)


SMALL_NAMES = ["c_ctx", "b_ada", "g_pre_mix", "g_post_mix", "g_pre_ffn", "g_post_ffn", "attn_sink",
               "b_gate_fwd", "b_gate_bwd", "g_gla_norm", "w_gate_fwd", "w_gate_bwd"]


def _small_update(tot, t_tot, wg_g, w, m, v):
    c1 = 1.0 / (1.0 - ADAM_B1 ** ADAM_STEP)
    c2 = 1.0 / (1.0 - ADAM_B2 ** ADAM_STEP)
    n = len(SMALL_NAMES)

    def body(tot_ref, t_ref, wg_ref, *refs):
        w_r, m_r, v_r = refs[0:n], refs[n:2 * n], refs[2 * n:3 * n]
        g_o, d_o, nm_o, nv_o = refs[3 * n:4 * n], refs[4 * n:5 * n], refs[5 * n:6 * n], refs[6 * n:7 * n]

        def upd(i, idx, g):
            nm = ADAM_B1 * m_r[i][idx] + (1.0 - ADAM_B1) * g
            nv = ADAM_B2 * v_r[i][idx] + (1.0 - ADAM_B2) * (g * g)
            g_o[i][idx] = g
            nm_o[i][idx] = nm
            nv_o[i][idx] = nv
            d_o[i][idx] = -ADAM_LR * ((nm * c1) / (jnp.sqrt(nv * c2) + ADAM_EPS) + ADAM_WD * w_r[i][idx])

        everything = (slice(None), slice(None))
        cc = w_r[0][...]
        sc = _sigmoid(cc)
        upd(0, everything, t_ref[0:1, :] * (sc * (1.0 + cc * (1.0 - sc))))
        for j in range(6):
            upd(1, (slice(None), slice(D * j, D * j + D)),
                tot_ref[R_ADA + j:R_ADA + j + 1, :] + tot_ref[R_ADA_C + j:R_ADA_C + j + 1, :])
        for j in range(4):
            upd(2 + j, everything, tot_ref[R_GAIN + j:R_GAIN + j + 1, :])
        upd(6, everything, tot_ref[R_SINK:R_SINK + 1, 0:N_ATT])
        upd(7, everything, tot_ref[R_BG:R_BG + 1, 0:256])
        upd(8, everything, tot_ref[R_BG:R_BG + 1, 256:512])
        upd(9, everything, tot_ref[R_GGLA:R_GGLA + 1, 0:DV])
        upd(10, (0,), wg_ref[0:GATE_RANK, :])
        upd(11, (0,), wg_ref[GATE_RANK:2 * GATE_RANK, :])

    params = [w[k] for k in SMALL_NAMES] + [m[k] for k in SMALL_NAMES] + [v[k] for k in SMALL_NAMES]
    outs = pl.pallas_call(
        body, name="small_update", grid=(1,),
        in_specs=[_full(tot.shape), _full(t_tot.shape), _full(wg_g.shape)] + [_full(p.shape) for p in params],
        out_specs=[_full(w[k].shape) for k in SMALL_NAMES] * 4,
        out_shape=[jax.ShapeDtypeStruct(w[k].shape, F32) for k in SMALL_NAMES] * 4,
        compiler_params=_cp(("arbitrary",)),
    )(tot, t_tot, wg_g, *params)
    return tuple(dict(zip(SMALL_NAMES, outs[i * n:(i + 1) * n])) for i in range(4))


def kernel(x, c, ctx, c_ctx, w_ada, b_ada, g_pre_mix, g_post_mix, g_pre_ffn, g_post_ffn, w_in, attn_sink, w_gate_fwd, b_gate_fwd, w_gate_bwd, b_gate_bwd, g_gla_norm, w_out, w_ffn_in, w_ffn_out, loss_target, m_c_ctx, m_w_ada, m_b_ada, m_g_pre_mix, m_g_post_mix, m_g_pre_ffn, m_g_post_ffn, m_w_in, m_attn_sink, m_w_gate_fwd, m_b_gate_fwd, m_w_gate_bwd, m_b_gate_bwd, m_g_gla_norm, m_w_out, m_w_ffn_in, m_w_ffn_out, v_c_ctx, v_w_ada, v_b_ada, v_g_pre_mix, v_g_post_mix, v_g_pre_ffn, v_g_post_ffn, v_w_in, v_attn_sink, v_w_gate_fwd, v_b_gate_fwd, v_w_gate_bwd, v_b_gate_bwd, v_g_gla_norm, v_w_out, v_w_ffn_in, v_w_ffn_out):
    me = 4 * lax.axis_index("x") + 2 * lax.axis_index("y") + lax.axis_index("c")
    S = x.shape[1]
    n_in = w_in.shape[2]
    n_ffi = w_ffn_in.shape[2]
    r_out = w_out.shape[1]
    r_ffo = w_ffn_out.shape[1]
    n_ada = w_ada.shape[2]

    wg_sh = jnp.concatenate([w_gate_fwd.reshape(4, 128), w_gate_bwd.reshape(4, 128)], axis=0)
    c_all3, g_all, w_all, ada_all = _entry(c, wg_sh, w_in[0].T.astype(_BF), c_ctx.reshape(1, D), w_ada[0])
    c_all = c_all3.reshape(N_DEV, D)
    wgf = g_all[:, 0:4].reshape(N_DEV, GATE_RANK, 32).transpose(1, 0, 2).reshape(GATE_RANK, 256)
    wgb = g_all[:, 4:8].reshape(N_DEV, GATE_RANK, 32).transpose(1, 0, 2).reshape(GATE_RANK, 256)
    win_p = _pad_rows_win(w_all.reshape(N_DEV * n_in, D))
    wg_bd = jnp.zeros((128, 512), F32).at[0:16, 0:256].set(wgf).at[16:32, 256:512].set(wgb).astype(_BF)
    ada_full = ada_all.transpose(1, 0, 2).reshape(16, N_DEV * n_ada) + b_ada
    ada_l = jnp.pad(lax.dynamic_slice_in_dim(ada_full, me, 1, 0).reshape(6, D), ((0, 2), (0, 0)))
    ada_c = jnp.pad(ada_full[8].reshape(6, D), ((0, 2), (0, 0)))
    gains = jnp.pad(jnp.concatenate([g_pre_mix, g_post_mix, g_pre_ffn, g_post_ffn], axis=0), ((0, 4), (0, 0)))
    sink = jnp.broadcast_to(attn_sink.reshape(8, 1), (8, 128))
    bg = jnp.concatenate([b_gate_fwd, b_gate_bwd], axis=1)
    ggla = jnp.tile(g_gla_norm, (1, N_GLA))

    r = _local_step(x[0], ctx[0], loss_target[0], ada_l, ada_c, gains, sink, win_p, wg_bd, bg, ggla,
                    w_out[0].astype(_BF), w_ffn_in[0].T.astype(_BF), w_ffn_out[0].astype(_BF))

    parts = r["parts"]
    tot = _sum_slots(parts, "sum_small_grads")
    loss = tot[R_LOSS, 0]
    d_ada_rows = parts[:, R_ADA:R_ADA + 6].reshape(N_DEV, 6 * D)
    d_ada_c = tot[R_ADA_C:R_ADA_C + 6].reshape(1, 6 * D)
    my_cols = lax.dynamic_slice_in_dim(jnp.concatenate([d_ada_rows, jnp.broadcast_to(d_ada_c, (1, 6 * D)),
                                                        jnp.zeros((7, 6 * D), F32)], axis=0), me * n_ada, n_ada, 1)
    grad_w_ada, t_part = _ada_bwd(c_all, c_ctx.reshape(1, D), w_ada[0], my_cols)
    wg_g = lax.dynamic_slice(tot, (R_WG, me * 32), (2 * GATE_RANK, 32))

    tr = lambda a: jnp.transpose(a, (0, 2, 1))
    big = {}
    t_all, = _exchange([t_part], "gather_c_ctx", True)
    t_tot = _sum_slots(t_all, "sum_c_ctx")
    for nm, w, g, m, v in [("w_ada", w_ada, grad_w_ada, m_w_ada, v_w_ada),
                           ("w_out", w_out, r["got_out"], m_w_out, v_w_out),
                           ("w_ffn_out", w_ffn_out, r["got_ffo"], m_w_ffn_out, v_w_ffn_out)]:
        big[nm] = _adamw(w, [g], m, v, "adamw_" + nm)
    big["w_ffn_in"] = tuple(tr(o) for o in _adamw(tr(w_ffn_in), [r["got_ffi"]], tr(m_w_ffn_in), tr(v_w_ffn_in),
                                                  "adamw_w_ffn_in"))
    big["w_in"] = tuple(tr(o) for o in _adamw(tr(w_in), r["got_in"], tr(m_w_in), tr(v_w_in), "adamw_w_in"))

    w_small = dict(c_ctx=c_ctx.reshape(1, D), b_ada=b_ada, g_pre_mix=g_pre_mix, g_post_mix=g_post_mix, g_pre_ffn=g_pre_ffn,
                   g_post_ffn=g_post_ffn, attn_sink=attn_sink, b_gate_fwd=b_gate_fwd, b_gate_bwd=b_gate_bwd,
                   g_gla_norm=g_gla_norm, w_gate_fwd=w_gate_fwd, w_gate_bwd=w_gate_bwd)
    m_small = dict(c_ctx=m_c_ctx.reshape(1, D), b_ada=m_b_ada, g_pre_mix=m_g_pre_mix, g_post_mix=m_g_post_mix,
                   g_pre_ffn=m_g_pre_ffn, g_post_ffn=m_g_post_ffn, attn_sink=m_attn_sink, b_gate_fwd=m_b_gate_fwd,
                   b_gate_bwd=m_b_gate_bwd, g_gla_norm=m_g_gla_norm, w_gate_fwd=m_w_gate_fwd, w_gate_bwd=m_w_gate_bwd)
    v_small = dict(c_ctx=v_c_ctx.reshape(1, D), b_ada=v_b_ada, g_pre_mix=v_g_pre_mix, g_post_mix=v_g_post_mix,
                   g_pre_ffn=v_g_pre_ffn, g_post_ffn=v_g_post_ffn, attn_sink=v_attn_sink, b_gate_fwd=v_b_gate_fwd,
                   b_gate_bwd=v_b_gate_bwd, g_gla_norm=v_g_gla_norm, w_gate_fwd=v_w_gate_fwd, w_gate_bwd=v_w_gate_bwd)
    grads_small, d_s, nm_s, nv_s = _small_update(tot, t_tot, wg_g, w_small, m_small, v_small)
    for dd in (grads_small, d_s, nm_s, nv_s):
        dd["c_ctx"] = dd["c_ctx"].reshape(D)

    order = ["c_ctx", "w_ada", "b_ada", "g_pre_mix", "g_post_mix", "g_pre_ffn", "g_post_ffn", "w_in", "attn_sink",
             "w_gate_fwd", "b_gate_fwd", "w_gate_bwd", "b_gate_bwd", "g_gla_norm", "w_out", "w_ffn_in", "w_ffn_out"]
    grads, deltas, new_m, new_v = [], [], [], []
    for nm in order:
        if nm in big:
            g_, d_, m_, v_ = big[nm]
        else:
            g_, d_, m_, v_ = grads_small[nm], d_s[nm], nm_s[nm], nv_s[nm]
        grads.append(g_)
        deltas.append(d_)
        new_m.append(m_)
        new_v.append(v_)
    return (loss, r["grad_x"][None], *grads, *deltas, *new_m, *new_v)
```

```python
import functools
import math

import numpy as np
import jax
import jax.numpy as jnp
from jax import lax
from jax.experimental import pallas as pl
from jax.experimental.pallas import tpu as pltpu

F32 = jnp.float32
_BF = jnp.bfloat16

N_DEV = 8
D = 1024
CTX = 256
HD = 64
N_ATT = 8
N_KV = 2
GRP = N_ATT // N_KV
WIN = 128
GRID_W = 64
ROPE_BASE = 10000.0
N_GLA = 8
DK = 32
DV = 64
GATE_RANK = 16
GATE_TAU = 16.0
FFN = 2816
EPS = 1e-6
NEG = -1e30
GLA_T = 128

QP = N_ATT * HD
KP = N_KV * HD
O_Q, O_K, O_V = 0, QP, QP + KP
O_GQ = O_V + KP
O_GK = O_GQ + N_GLA * DK
O_GV = O_GK + N_GLA * DK
O_GG = O_GV + N_GLA * DV
O_Z = O_GG + N_GLA * DV
NP = O_Z + 128
IN_COLS = 2336

ADAM_LR, ADAM_B1, ADAM_B2, ADAM_EPS, ADAM_WD, ADAM_STEP = 0.001, 0.9, 0.999, 1e-08, 0.01, 10

VMEM_BIG = 56 * 1024 * 1024
MESH = pl.DeviceIdType.MESH


def _cp(sem, vmem=None):
    return pltpu.CompilerParams(dimension_semantics=sem, vmem_limit_bytes=vmem)


def _full(shape):
    nd = len(shape)
    return pl.BlockSpec(shape, lambda *a: (0,) * nd)


def _rows(tile, width, off=0):
    return pl.BlockSpec((tile, width), lambda i: (i + off, 0))


def _rows_lat(tile, width):
    return pl.BlockSpec((tile, width), lambda i: (jnp.maximum(i - 1, 0), 0))


def _nt(a, b):
    return lax.dot_general(a, b, (((1,), (1,)), ((), ())), preferred_element_type=F32)


def _tn(a, b):
    return lax.dot_general(a, b, (((0,), (0,)), ((), ())), preferred_element_type=F32)


def _nn(a, b):
    return jnp.dot(a, b, preferred_element_type=F32)


def _head_mean(x, mavg):
    n = x.shape[0]
    hi = x.astype(_BF)
    lo = (x - hi.astype(F32)).astype(_BF)
    y = _nn(jnp.concatenate([hi, lo], axis=0), mavg)
    return y[0:n] + y[n:2 * n]


def _rope(t, cos, sa, sb):
    n = t.shape[1]
    reps = n // 128
    c = jnp.tile(cos, (1, reps))
    a = jnp.tile(sa, (1, reps))
    b = jnp.tile(sb, (1, reps))
    return t * c + pltpu.roll(t, n - 16, 1) * a + pltpu.roll(t, 16, 1) * b


def _unrope(t, cos, sa, sb):
    n = t.shape[1]
    reps = n // 128
    c = jnp.tile(cos, (1, reps))
    a = jnp.tile(sa, (1, reps))
    b = jnp.tile(sb, (1, reps))
    return t * c + pltpu.roll(t * a, 16, 1) + pltpu.roll(t * b, n - 16, 1)


def _sigmoid(x):
    return 1.0 / (1.0 + jnp.exp(-x))


def _inproj_fwd(x, ctx, gml, shl, gmc, shc, win, wg, bg, cos, sa, sb, shards):
    E = x.shape[0] + CTX
    TE = CTX

    def body(x_ref, c_ref, gml_ref, shl_ref, gmc_ref, shc_ref, w_ref, wg_ref, bg_ref, cos_ref, sa_ref, sb_ref,
             h_ref, q_ref, k_ref, v_ref, gq_ref, gk_ref, gv_ref, gg_ref, z_ref, la_ref):
        is_ctx = pl.program_id(0) == 0
        gm = jnp.where(is_ctx, gmc_ref[...], gml_ref[...])
        sh = jnp.where(is_ctx, shc_ref[...], shl_ref[...])
        x = jnp.where(is_ctx, c_ref[...], x_ref[...])
        r = lax.rsqrt(jnp.mean(x * x, axis=-1, keepdims=True) + EPS)
        hb = ((x * r) * gm + sh).astype(_BF)
        h_ref[...] = hb
        p = _nt(hb, w_ref[...])
        cos_t, sa_t, sb_t = cos_ref[...], sa_ref[...], sb_ref[...]
        q_ref[...] = (_rope(p[:, O_Q:O_K], cos_t, sa_t, sb_t) * (HD ** -0.5)).astype(_BF)
        k_ref[...] = _rope(p[:, O_K:O_V], cos_t, sa_t, sb_t).astype(_BF)
        v_ref[...] = p[:, O_V:O_GQ].astype(_BF)
        gq_ref[...] = p[:, O_GQ:O_GK] * (DK ** -0.5)
        gk_ref[...] = p[:, O_GK:O_GV]
        gv_ref[...] = p[:, O_GV:O_GG]
        gg_ref[...] = p[:, O_GG:O_Z]
        zb = p[:, O_Z:NP].astype(_BF)
        z_ref[...] = zb
        lg = _nn(zb, wg_ref[...]) + bg_ref[...]
        la_ref[...] = (jnp.minimum(lg, 0.0) - jnp.log(1.0 + jnp.exp(-jnp.abs(lg)))) * (1.0 / GATE_TAU)

    vec = _full((1, D))
    tab = _rows(TE, 128)
    outs = [(D, _BF), (QP, _BF), (KP, _BF), (KP, _BF), (256, F32), (256, F32), (512, F32), (512, F32),
            (128, _BF), (512, F32)]
    return _hosted_call(
        body, (x, ctx, gml, shl, gmc, shc, win, wg, bg, cos, sa, sb), shards, True,
        name="inproj_fwd", grid=(E // TE,),
        in_specs=[_rows_lat(TE, D), _full((CTX, D)), vec, vec, vec, vec, _full((NP, D)), _full((128, 512)),
                  _full((1, 512)), tab, tab, tab],
        out_specs=[_rows(TE, w) for w, _ in outs],
        out_shape=[jax.ShapeDtypeStruct((E, w), dt) for w, dt in outs],
        compiler_params=_cp(("arbitrary",), 40 * 1024 * 1024))


def _xchg_scratch(na):
    return [pltpu.SemaphoreType.DMA((na, N_DEV - 1)), pltpu.SemaphoreType.DMA((na, N_DEV - 1)),
            pltpu.SemaphoreType.DMA((na,))]


def _xchg_copies(ins, outs, send_sems, recv_sems, local_sems, gather):
    x, y, c = lax.axis_index("x"), lax.axis_index("y"), lax.axis_index("c")
    me = 4 * x + 2 * y + c
    local, sends, recvs = [], [], []
    for a in range(len(ins)):
        local.append(pltpu.make_async_copy(ins[a] if gather else ins[a].at[me], outs[a].at[me], local_sems.at[a]))
    for k in range(1, N_DEV):
        px, py, pc = x ^ (k >> 2), y ^ ((k >> 1) & 1), c ^ (k & 1)
        peer = 4 * px + 2 * py + pc
        for a in range(len(ins)):
            sems = dict(send_sem=send_sems.at[a, k - 1], recv_sem=recv_sems.at[a, k - 1], device_id_type=MESH)
            sends.append(pltpu.make_async_remote_copy(
                src_ref=ins[a] if gather else ins[a].at[peer], dst_ref=outs[a].at[me], device_id=(px, py, pc), **sems))
            recvs.append(pltpu.make_async_remote_copy(
                src_ref=ins[a] if gather else ins[a].at[me], dst_ref=outs[a].at[peer], device_id=(x, y, c), **sems))
    return local, sends, recvs


def _xchg_start(cps):
    local, sends, _ = cps
    for cp in local + sends:
        cp.start()


def _xchg_finish(cps):
    local, sends, recvs = cps
    for cp in recvs:
        cp.wait_recv()
    for cp in sends:
        cp.wait_send()
    for cp in local:
        cp.wait()


def _xchg_out_shapes(ins, gather):
    return [jax.ShapeDtypeStruct(((N_DEV,) + s.shape) if gather else s.shape, s.dtype) for s in ins]


def _hosted_call(body, args, hosted, gather, *, grid, in_specs, out_specs, out_shape, scratch_shapes=(), **kw):
    na = len(hosted)
    if na == 0:
        return pl.pallas_call(body, grid=grid, in_specs=in_specs, out_specs=out_specs, out_shape=out_shape,
                              scratch_shapes=list(scratch_shapes), **kw)(*args)
    n_in, n_out, n_scr = len(in_specs), len(out_specs), len(scratch_shapes)

    def wrapped(*refs):
        ins, h_in = refs[:n_in], refs[n_in:n_in + na]
        outs, h_out = refs[n_in + na:n_in + na + n_out], refs[n_in + na + n_out:n_in + 2 * na + n_out]
        scr = refs[n_in + 2 * na + n_out:]
        cps = _xchg_copies(h_in, h_out, *scr[n_scr:], gather=gather)
        pids = [pl.program_id(a) for a in range(len(grid))]
        first = functools.reduce(jnp.logical_and, [p == 0 for p in pids])
        last = functools.reduce(jnp.logical_and, [p == g - 1 for p, g in zip(pids, grid)])

        @pl.when(first)
        def _():
            _xchg_start(cps)

        body(*ins, *outs, *scr[:n_scr])

        @pl.when(last)
        def _():
            _xchg_finish(cps)

    anyspec = pl.BlockSpec(memory_space=pl.ANY)
    return pl.pallas_call(
        wrapped, grid=grid, in_specs=list(in_specs) + [anyspec] * na, out_specs=list(out_specs) + [anyspec] * na,
        out_shape=list(out_shape) + _xchg_out_shapes(hosted, gather),
        scratch_shapes=list(scratch_shapes) + _xchg_scratch(na), **kw)(*args, *hosted)


def _attn_specs(E):
    nb = (E - CTX) // WIN
    last = E // WIN - 1
    kc = pl.BlockSpec((CTX, KP), lambda n: (0, 0))
    kp = pl.BlockSpec((WIN, KP), lambda n: (n + 1, 0))
    kk = pl.BlockSpec((WIN, KP), lambda n: (n + 2, 0))
    kn = pl.BlockSpec((WIN, KP), lambda n: (jnp.minimum(n + 3, last), 0))
    return nb, [kc, kp, kk, kn]


def _attn_bias(nb):
    rows = np.arange(GRP * WIN)[:, None] % WIN
    cols = np.arange(CTX + 3 * WIN)[None, :]
    j = cols - CTX
    band = np.abs(j - WIN - rows) <= WIN
    out = []
    for first, last in ((True, False), (False, False), (False, True)):
        ok = (cols < CTX) | (band & ((j >= WIN) | (not first)) & ((j < 2 * WIN) | (not last)))
        out.append(np.where(ok, 0.0, NEG).astype(np.float32))
    bias = jnp.asarray(np.stack(out))
    spec = pl.BlockSpec((1, GRP * WIN, CTX + 3 * WIN),
                        lambda n: (jnp.where(n == 0, 0, jnp.where(n == nb - 1, 2, 1)), 0, 0))
    return bias, spec


def _both_halves(t, h):
    tf = t.astype(F32)
    r = pltpu.roll(tf, HD, 1)
    lo = lax.broadcasted_iota(jnp.int32, tf.shape, 1) < HD
    return (jnp.where(lo, tf, r) if h == 0 else jnp.where(lo, r, tf)).astype(t.dtype)


def _stack_heads(ref, h):
    lo = lax.broadcasted_iota(jnp.int32, (WIN, 128), 1) < HD
    parts = []
    for g in range(GRP):
        j = GRP * h + g
        t = ref[:, 128 * (j // 2):128 * (j // 2) + 128].astype(F32)
        parts.append(jnp.where(lo if j % 2 == 0 else jnp.logical_not(lo), t, 0.0))
    return jnp.concatenate(parts, axis=0)


def _unstack_pair(o, pp):
    lo = lax.broadcasted_iota(jnp.int32, (WIN, 128), 1) < HD
    return jnp.where(lo, o[WIN * 2 * pp:WIN * 2 * pp + WIN], o[WIN * (2 * pp + 1):WIN * (2 * pp + 1) + WIN])


def _attn_fwd(q, k, v, sink, shards):
    E = q.shape[0]
    S = E - CTX
    nb, kspecs = _attn_specs(E)
    na = len(shards)

    def body(q_ref, kc, kp, kk, kn, vc, vp, vk, vn, sink_ref, bias_ref, *rest):
        shard_refs, (o_ref, lse_ref, p_ref), got_refs = rest[:na], rest[na:na + 3], rest[na + 3:2 * na + 3]
        n = pl.program_id(0)
        cps = _xchg_copies(shard_refs, got_refs, *rest[2 * na + 3:], gather=True)

        @pl.when(n == 0)
        def _():
            _xchg_start(cps)

        lane = lax.broadcasted_iota(jnp.int32, (WIN, 128), 1)
        lse_t = jnp.zeros((WIN, 128), F32)
        kall = jnp.concatenate([kc[...], kp[...], kk[...], kn[...]], axis=0)
        vall = jnp.concatenate([vc[...], vp[...], vk[...], vn[...]], axis=0)
        K = [_both_halves(kall, h) for h in range(N_KV)]
        Q = [_stack_heads(q_ref, h).astype(_BF) for h in range(N_KV)]
        sk = [jnp.concatenate([jnp.broadcast_to(sink_ref[GRP * h + g:GRP * h + g + 1, 0:1], (WIN, 1))
                               for g in range(GRP)], axis=0) for h in range(N_KV)]
        s = [_nt(Q[h], K[h]) + bias_ref[0] for h in range(N_KV)]
        m = [jnp.maximum(jnp.max(s[h], axis=1, keepdims=True), sk[h]) for h in range(N_KV)]
        e = [jnp.exp(s[h] - m[h]) for h in range(N_KV)]
        den = [jnp.sum(e[h], axis=1, keepdims=True) + jnp.exp(sk[h] - m[h]) for h in range(N_KV)]
        V = [_both_halves(vall, h) for h in range(N_KV)]
        pb = [(e[h] * (1.0 / den[h])).astype(_BF) for h in range(N_KV)]
        for h in range(N_KV):
            p_ref[GRP * WIN * h:GRP * WIN * (h + 1), :] = pb[h]
        o = [_nn(pb[h], V[h]) for h in range(N_KV)]
        for h in range(N_KV):
            lse = m[h] + jnp.log(den[h])
            for g in range(GRP):
                lse_t = jnp.where(lane == GRP * h + g, lse[WIN * g:WIN * g + WIN], lse_t)
            for pp in range(GRP // 2):
                t = 2 * h + pp
                o_ref[:, 128 * t:128 * t + 128] = _unstack_pair(o[h], pp).astype(_BF)
        lse_ref[...] = lse_t

        @pl.when(n == nb - 1)
        def _():
            _xchg_finish(cps)

    qs = pl.BlockSpec((WIN, QP), lambda n: (n + 2, 0))
    anyspec = pl.BlockSpec(memory_space=pl.ANY)
    bias, bias_spec = _attn_bias(nb)
    return pl.pallas_call(
        body, name="attn_fwd", grid=(nb,),
        in_specs=[qs] + kspecs + kspecs + [_full((8, 128)), bias_spec] + [anyspec] * na,
        out_specs=[_rows(WIN, 512), _rows(WIN, 128), _rows(N_KV * GRP * WIN, CTX + 3 * WIN)] + [anyspec] * na,
        out_shape=[jax.ShapeDtypeStruct((S, 512), _BF), jax.ShapeDtypeStruct((S, 128), F32),
                   jax.ShapeDtypeStruct((nb * N_KV * GRP * WIN, CTX + 3 * WIN), _BF)]
        + _xchg_out_shapes(shards, True),
        scratch_shapes=_xchg_scratch(na),
        compiler_params=_cp(("arbitrary",)),
    )(q, k, k, k, k, v, v, v, v, sink, bias, *shards)


def _attn_bwd(q, k, v, sink, probs, lse, d_attn, slabs):
    E = q.shape[0]
    S = E - CTX
    nb, kspecs = _attn_specs(E)
    last = E // WIN - 1
    na = len(slabs)

    def body(q_ref, kc, kp, kk, kn, vc, vp, vk, vn, sink_ref, p_ref, lse_ref, do_ref, *rest):
        slab_refs, (dq_ref, dk_ref, dv_ref, ds_ref), got_refs = rest[:na], rest[na:na + 4], rest[na + 4:2 * na + 4]
        n = pl.program_id(0)
        cps = _xchg_copies(slab_refs, got_refs, *rest[2 * na + 4:], gather=False)

        @pl.when(n == 0)
        def _():
            _xchg_start(cps)
            dk_ref[...] = jnp.zeros_like(dk_ref)
            dv_ref[...] = jnp.zeros_like(dv_ref)
            ds_ref[...] = jnp.zeros_like(ds_ref)

        lane = lax.broadcasted_iota(jnp.int32, (WIN, 128), 1)
        lse_t = lse_ref[...]
        starts = [None, pl.multiple_of((n + 1) * WIN, WIN), pl.multiple_of((n + 2) * WIN, WIN),
                  pl.multiple_of(jnp.minimum(n + 3, last) * WIN, WIN)]
        kall = jnp.concatenate([kc[...], kp[...], kk[...], kn[...]], axis=0)
        vall = jnp.concatenate([vc[...], vp[...], vk[...], vn[...]], axis=0)
        for h in range(N_KV):
            hs = slice(HD * h, HD * h + HD)
            K = _both_halves(kall, h)
            V = _both_halves(vall, h)
            Q = _stack_heads(q_ref, h).astype(_BF)
            sk = jnp.concatenate([jnp.broadcast_to(sink_ref[GRP * h + g:GRP * h + g + 1, 0:1], (WIN, 1))
                                  for g in range(GRP)], axis=0)
            ls = jnp.concatenate([jnp.sum(jnp.where(lane == GRP * h + g, lse_t, 0.0), axis=1, keepdims=True)
                                  for g in range(GRP)], axis=0)
            do = _stack_heads(do_ref, h).astype(_BF)
            pb = p_ref[GRP * WIN * h:GRP * WIN * (h + 1), :]
            p = pb.astype(F32)
            dp = _nt(do, V)
            delta = jnp.sum(p * dp, axis=1, keepdims=True)
            dsc = (p * (dp - delta)).astype(_BF)
            dq = _nn(dsc, K) * (HD ** -0.5)
            for pp in range(GRP // 2):
                t = 2 * h + pp
                dq_ref[:, 128 * t:128 * t + 128] = _unstack_pair(dq, pp).astype(_BF)
            dK2 = _tn(Q, dsc)
            dV2 = _tn(do, pb)
            dK = dK2[0:HD] + dK2[HD:2 * HD]
            dV = dV2[0:HD] + dV2[HD:2 * HD]
            dk_ref[hs, 0:CTX] += dK[:, 0:CTX]
            dv_ref[hs, 0:CTX] += dV[:, 0:CTX]
            for w in range(1, 4):
                lo = CTX + WIN * (w - 1)
                dk_ref[hs, pl.ds(starts[w], WIN)] += dK[:, lo:lo + WIN]
                dv_ref[hs, pl.ds(starts[w], WIN)] += dV[:, lo:lo + WIN]
            psk = -jnp.exp(sk - ls) * delta
            for g in range(GRP):
                j = GRP * h + g
                tot = jnp.sum(psk[WIN * g:WIN * g + WIN], axis=0, keepdims=True)
                ds_ref[j:j + 1, :] += jnp.broadcast_to(tot, (1, 128))

        @pl.when(n == nb - 1)
        def _():
            _xchg_finish(cps)

    qs = pl.BlockSpec((WIN, QP), lambda n: (n + 2, 0))
    anyspec = pl.BlockSpec(memory_space=pl.ANY)
    return pl.pallas_call(
        body, name="attn_bwd", grid=(nb,),
        in_specs=[qs] + kspecs + kspecs + [_full((8, 128)), _rows(N_KV * GRP * WIN, CTX + 3 * WIN), _rows(WIN, 128),
                                            _rows(WIN, 512)] + [anyspec] * na,
        out_specs=[_rows(WIN, QP), _full((KP, E)), _full((KP, E)), _full((8, 128))] + [anyspec] * na,
        out_shape=[jax.ShapeDtypeStruct((S, QP), _BF), jax.ShapeDtypeStruct((KP, E), F32),
                   jax.ShapeDtypeStruct((KP, E), F32), jax.ShapeDtypeStruct((8, 128), F32)]
        + _xchg_out_shapes(slabs, False),
        scratch_shapes=_xchg_scratch(na),
        compiler_params=_cp(("arbitrary",), 48 * 1024 * 1024),
    )(q, k, k, k, k, v, v, v, v, sink, probs, lse, d_attn, *slabs)


def _gla_order(E, reverse, backward):
    nc = CTX // GLA_T
    n = E // GLA_T
    if not reverse:
        fwd = lambda s: s
    else:
        fwd = lambda s: jnp.where(s < nc, nc - 1 - s, n - 1 + nc - s)
    if backward:
        return lambda s: fwd(n - 1 - s)
    return fwd


def _gla_masks():
    T = GLA_T
    l128 = lax.broadcasted_iota(jnp.int32, (1, 128), 1)
    qmask = [((l128 >> 5) == j).astype(F32) for j in range(4)]
    vmask = [((l128 >> 6) == j).astype(F32) for j in range(2)]
    bd = ((lax.broadcasted_iota(jnp.int32, (512, 256), 0) >> 6)
          == (lax.broadcasted_iota(jnp.int32, (512, 256), 1) >> 5)).astype(F32)
    ri = lax.broadcasted_iota(jnp.int32, (T, 2 * T), 0)
    ci = lax.broadcasted_iota(jnp.int32, (T, 2 * T), 1) & (T - 1)
    return qmask, vmask, bd, ri, ci


def _tri_sum(tri, x):
    hi = x.astype(_BF)
    lo = (x - hi.astype(F32)).astype(_BF)
    n = x.shape[1]
    y = _nn(tri.astype(_BF), jnp.concatenate([hi, lo], axis=1))
    return y[:, 0:n] + y[:, n:2 * n]


def _gla_decays(la, reverse, ri, ci):
    T = GLA_T
    msk2 = (ri <= ci) if reverse else (ri >= ci)
    mskT2 = (ri >= ci) if reverse else (ri <= ci)
    b = _tri_sum(msk2[:, 0:T], la)
    bT = b[0:1] if reverse else b[T - 1:T]
    bm = b[T // 2:T // 2 + 1]
    return msk2, mskT2, b, bT, bm


def _pair_stack(tile, m0, m1):
    return jnp.concatenate([(tile * m0).astype(_BF), (tile * m1).astype(_BF)], axis=0)


def _gla_fwd(gq, gk, gv, la, shards=()):
    E = gq.shape[0]
    T = GLA_T
    n = E // T
    orders = [_gla_order(E, False, False), _gla_order(E, True, False)]

    def one_direction(reverse, gq_ref, gk_ref, gv_ref, la_ref, o_ref, st_ref, S_scr):
        qmask, vmask, bd, ri, ci = _gla_masks()
        msk2, _, b, bT, bm = _gla_decays(la_ref[...], reverse, ri, ci)
        q, k, v = gq_ref[...], gk_ref[...], gv_ref[...]
        qd = (q * jnp.exp(b)).astype(_BF)
        qm = (q * jnp.exp(b - bm)).astype(_BF)
        km = k * jnp.exp(bm - b)
        kd = (k * jnp.exp(bT - b)).astype(_BF)
        ST = S_scr[...]
        comp = ST[0:DV]
        for h in range(1, N_GLA):
            comp = comp + ST[DV * h:DV * h + DV]
        st_ref[0] = comp
        inter = _nt(qd, ST.astype(_BF))
        tiles = []
        for p in range(N_GLA // 2):
            qs = slice(128 * (p // 2), 128 * (p // 2) + 128)
            vs = slice(128 * p, 128 * p + 128)
            j0 = (2 * p) % 4
            KS = _pair_stack(km[:, qs], qmask[j0], qmask[j0 + 1])
            VS = _pair_stack(v[:, vs], vmask[0], vmask[1])
            AA = jnp.where(msk2, _nt(qm[:, qs], KS), 0.0).astype(_BF)
            tiles.append(_nn(AA, VS))
        o_ref[...] = inter + jnp.concatenate(tiles, axis=1)
        S_scr[...] = ST * jnp.exp(bT) + bd * _tn(v.astype(_BF), kd)

    def body(qf, kf, vf, lf, qr, kr, vr, lr, of, sf, orr, sr, S_f, S_r):
        @pl.when(pl.program_id(0) == 0)
        def _():
            S_f[...] = jnp.zeros_like(S_f)
            S_r[...] = jnp.zeros_like(S_r)

        one_direction(False, qf, kf, vf, lf, of, sf, S_f)
        one_direction(True, qr, kr, vr, lr, orr, sr, S_r)

    def blk(d, w, c=0):
        return pl.BlockSpec((T, w), lambda s: (orders[d](s), c))

    def st_spec(d):
        return pl.BlockSpec((1, DV, 256), lambda s: (orders[d](s), 0, 0))

    return _hosted_call(
        body, (gq, gk, gv, la, gq, gk, gv, la), shards, True, name="gla_fwd", grid=(n,),
        in_specs=[blk(0, 256), blk(0, 256), blk(0, 512), blk(0, 256, 0), blk(1, 256), blk(1, 256), blk(1, 512),
                  blk(1, 256, 1)],
        out_specs=[blk(0, 512), st_spec(0), blk(1, 512), st_spec(1)],
        out_shape=[jax.ShapeDtypeStruct((E, 512), F32), jax.ShapeDtypeStruct((n, DV, 256), F32)] * 2,
        scratch_shapes=[pltpu.VMEM((512, 256), F32)] * 2,
        compiler_params=_cp(("arbitrary",)))


def _gla_bwd(gq, gk, gv, la, z, st_f, st_r, do, slabs=()):
    E = gq.shape[0]
    T = GLA_T
    n = E // T
    nc = CTX // T
    orders = [_gla_order(E, False, True), _gla_order(E, True, True)]

    def one_direction(reverse, gq_ref, gk_ref, gv_ref, la_ref, z_ref, st_ref, do_ref,
                      dq_ref, dk_ref, dv_ref, dlg_ref, gwg_ref, bsum_ref, dS_scr):
        cols = slice(256, 512) if reverse else slice(0, 256)
        is_lat = orders[1 if reverse else 0](pl.program_id(0)) >= nc
        qmask, vmask, bd, ri, ci = _gla_masks()
        msk2, mskT2, b, bT, bm = _gla_decays(la_ref[...], reverse, ri, ci)
        q, k, v = gq_ref[...], gk_ref[...], gv_ref[...]
        do = jnp.where(is_lat, do_ref[...].astype(F32), 0.0)
        e_b, e_qm, e_km, e_kd, e_T = jnp.exp(b), jnp.exp(b - bm), jnp.exp(bm - b), jnp.exp(bT - b), jnp.exp(bT)
        qd, qm, km, kd = q * e_b, q * e_qm, k * e_km, k * e_kd
        qdb, qmb, kmb, kdb, vb, dob = (t.astype(_BF) for t in (qd, qm, km, kd, v, do))
        ST = jnp.tile(st_ref[0], (N_GLA, 1)) * bd
        dST = dS_scr[...]
        dSTb = dST.astype(_BF)
        dqd = _nn(dob, ST.astype(_BF))
        dkd = _nn(vb, dSTb)
        dv_t, dqm_t, dkm_t = [], [None, None], [None, None]
        for p in range(N_GLA // 2):
            t = p // 2
            qs = slice(128 * t, 128 * t + 128)
            vs = slice(128 * p, 128 * p + 128)
            j0 = (2 * p) % 4
            QS = _pair_stack(qm[:, qs], qmask[j0], qmask[j0 + 1])
            KS = _pair_stack(km[:, qs], qmask[j0], qmask[j0 + 1])
            VS = _pair_stack(v[:, vs], vmask[0], vmask[1])
            DS = _pair_stack(do[:, vs], vmask[0], vmask[1])
            ATT = jnp.where(mskT2, _nt(kmb[:, qs], QS), 0.0).astype(_BF)
            dAA = jnp.where(msk2, _nt(dob[:, vs], VS), 0.0).astype(_BF)
            dATT = jnp.where(mskT2, _nt(vb[:, vs], DS), 0.0).astype(_BF)
            dv_t.append(_nn(ATT, DS))
            dq_p = _nn(dAA, KS)
            dk_p = _nn(dATT, QS)
            dqm_t[t] = dq_p if dqm_t[t] is None else dqm_t[t] + dq_p
            dkm_t[t] = dk_p if dkm_t[t] is None else dkm_t[t] + dk_p
        dqm = jnp.concatenate(dqm_t, axis=1)
        dkm = jnp.concatenate(dkm_t, axis=1)
        dq = dqm * e_qm + dqd * e_b
        dk = dkm * e_km + dkd * e_kd
        dv = _nt(kdb, dSTb) + jnp.concatenate(dv_t, axis=1)
        dq_ref[...] = (dq * (DK ** -0.5)).astype(_BF)
        dk_ref[...] = dk.astype(_BF)
        dv_ref[...] = dv.astype(_BF)
        db = dqm * qm - dkm * km + dqd * qd - dkd * kd
        dbT = jnp.sum(dkd * kd, axis=0, keepdims=True) + e_T * jnp.sum(dST * ST, axis=0, keepdims=True)
        dla = _tri_sum(mskT2[:, 0:T], db) + dbT
        dlg = dla * (1.0 - jnp.exp(GATE_TAU * la_ref[...])) * (1.0 / GATE_TAU)
        bsum_ref[0:1, cols] += jnp.sum(dlg, axis=0, keepdims=True)
        dlgb = dlg.astype(_BF)
        dlg_ref[...] = dlgb
        gwg_ref[:, cols] += _tn(z_ref[...], dlgb)
        dS_scr[...] = dST * e_T + bd * _tn(dob, qdb)

    def body(*refs):
        ins_f, ins_r = refs[0:7], refs[7:14]
        outs_f, outs_r = refs[14:18], refs[18:22]
        gwg_ref, bsum_ref, dS_f, dS_r = refs[22:26]

        @pl.when(pl.program_id(0) == 0)
        def _():
            dS_f[...] = jnp.zeros_like(dS_f)
            dS_r[...] = jnp.zeros_like(dS_r)
            gwg_ref[...] = jnp.zeros_like(gwg_ref)
            bsum_ref[...] = jnp.zeros_like(bsum_ref)

        one_direction(False, *ins_f, *outs_f, gwg_ref, bsum_ref, dS_f)
        one_direction(True, *ins_r, *outs_r, gwg_ref, bsum_ref, dS_r)

    def specs(d, st):
        order = orders[d]
        blk = lambda w, c=0: pl.BlockSpec((T, w), lambda s: (order(s), c))
        ins = [blk(256), blk(256), blk(512), blk(256, d), blk(128),
               pl.BlockSpec((1, DV, 256), lambda s: (order(s), 0, 0)),
               pl.BlockSpec((T, 512), lambda s: (jnp.maximum(order(s) - nc, 0), 0))]
        return ins, [blk(256), blk(256), blk(512), blk(256)], (gq, gk, gv, la, z, st, do)

    in_f, out_f, args_f = specs(0, st_f)
    in_r, out_r, args_r = specs(1, st_r)
    dir_shapes = [jax.ShapeDtypeStruct((E, 256), _BF), jax.ShapeDtypeStruct((E, 256), _BF),
                  jax.ShapeDtypeStruct((E, 512), _BF), jax.ShapeDtypeStruct((E, 256), _BF)]
    return _hosted_call(
        body, args_f + args_r, slabs, False, name="gla_bwd", grid=(n,),
        in_specs=in_f + in_r, out_specs=out_f + out_r + [_full((128, 512)), _full((8, 512))],
        out_shape=dir_shapes * 2 + [jax.ShapeDtypeStruct((128, 512), F32), jax.ShapeDtypeStruct((8, 512), F32)],
        scratch_shapes=[pltpu.VMEM((512, 256), F32)] * 2,
        compiler_params=_cp(("arbitrary",)))


def _gla_out(o, gg, ggla, mavg):
    rr = lax.rsqrt(_head_mean(o * o, mavg) + EPS)
    oh = o * rr
    sg = _sigmoid(gg)
    return oh, rr, sg


def _mix_fwd(x, attn, o_f, o_b, gg, ggla, mavg, wout, gt1, g2):
    S = x.shape[0]
    TM = 256

    def body(x_ref, a_ref, of_ref, ob_ref, gg_ref, ggla_ref, mavg_ref, w_ref, gt1_ref, g2_ref, x1_ref, mix_ref):
        gg_t = gg_ref[...]
        oh, _, sg = _gla_out(of_ref[...] + ob_ref[...], gg_t, ggla_ref[...], mavg_ref[...])
        mix_ref[:, 0:512] = a_ref[...]
        mix_ref[:, 512:1024] = (oh * ggla_ref[...] * (gg_t * sg)).astype(_BF)
        y = _nn(mix_ref[...], w_ref[...])
        ry = lax.rsqrt(jnp.mean(y * y, axis=-1, keepdims=True) + EPS)
        x1_ref[...] = x_ref[...] + gt1_ref[...] * ((y * ry) * g2_ref[...])

    return pl.pallas_call(
        body, name="mix_fwd", grid=(S // TM,),
        in_specs=[_rows(TM, D), _rows(TM, 512), _rows(TM, 512, 1), _rows(TM, 512, 1), _rows(TM, 512, 1),
                  _full((1, 512)), _full((512, 512)), _full((D, D)), _full((1, D)), _full((1, D))],
        out_specs=[_rows(TM, D), _rows(TM, D)],
        out_shape=[jax.ShapeDtypeStruct((S, D), F32), jax.ShapeDtypeStruct((S, D), _BF)],
        compiler_params=_cp(("arbitrary",), 40 * 1024 * 1024),
    )(x, attn, o_f, o_b, gg, ggla, mavg, wout, gt1, g2)


def _mix_bwd(dx1, mix, o_f, o_b, gg, ggla, mavg, wout, gt1, g2, slabs):
    S = dx1.shape[0]
    TM = 256

    def body(dx_ref, mix_ref, of_ref, ob_ref, gg_ref, ggla_ref, mavg_ref, w_ref, gt1_ref, g2_ref,
             da_ref, do_ref, dgg_ref, dy_ref, sums_ref):
        @pl.when(pl.program_id(0) == 0)
        def _():
            sums_ref[...] = jnp.zeros_like(sums_ref)

        dx = dx_ref[...]
        y = _nn(mix_ref[...], w_ref[...])
        ry = lax.rsqrt(jnp.mean(y * y, axis=-1, keepdims=True) + EPS)
        yh = y * ry
        sums_ref[0:1, :] += jnp.sum(dx * yh, axis=0, keepdims=True)
        dyh = dx * (gt1_ref[...] * g2_ref[...])
        dy = (ry * (dyh - yh * jnp.mean(dyh * yh, axis=-1, keepdims=True))).astype(_BF)
        dy_ref[...] = dy
        dmix = _nt(dy, w_ref[...])
        da_ref[...] = dmix[:, 0:512].astype(_BF)
        dgla = dmix[:, 512:1024]
        gg_t = gg_ref[...]
        ggla_t = ggla_ref[...]
        oh, rr, sg = _gla_out(of_ref[...] + ob_ref[...], gg_t, ggla_t, mavg_ref[...])
        dgg_ref[...] = (dgla * oh * ggla_t * (sg * (1.0 + gg_t * (1.0 - sg)))).astype(_BF)
        don = dgla * (gg_t * sg)
        sums_ref[1:2, 0:512] += jnp.sum(don * oh, axis=0, keepdims=True)
        doh = don * ggla_t
        do_ref[...] = (rr * (doh - oh * _head_mean(doh * oh, mavg_ref[...]))).astype(_BF)

    return _hosted_call(
        body, (dx1, mix, o_f, o_b, gg, ggla, mavg, wout, gt1, g2), slabs, False,
        name="mix_bwd", grid=(S // TM,),
        in_specs=[_rows(TM, D), _rows(TM, D), _rows(TM, 512, 1), _rows(TM, 512, 1), _rows(TM, 512, 1),
                  _full((1, 512)), _full((512, 512)), _full((D, D)), _full((1, D)), _full((1, D))],
        out_specs=[_rows(TM, 512), _rows(TM, 512), _rows(TM, 512), _rows(TM, D), _full((8, D))],
        out_shape=[jax.ShapeDtypeStruct((S, 512), _BF), jax.ShapeDtypeStruct((S, 512), _BF),
                   jax.ShapeDtypeStruct((S, 512), _BF), jax.ShapeDtypeStruct((S, D), _BF),
                   jax.ShapeDtypeStruct((8, D), F32)],
        compiler_params=_cp(("arbitrary",), 40 * 1024 * 1024))


def _ffn(x1, target, gm2, sh2, gt2, g4, wffi, wffo):
    S = x1.shape[0]
    TF = 256

    def body(x_ref, t_ref, gm_ref, sh_ref, gt_ref, g4_ref, wi_hbm, wo_hbm,
             dx_ref, h_ref, du_ref, act_ref, df_ref, sums_ref, loss_ref, wi, wo, sem):
        @pl.when(pl.program_id(0) == 0)
        def _():
            c1 = pltpu.make_async_copy(wi_hbm, wi, sem.at[0])
            c2 = pltpu.make_async_copy(wo_hbm, wo, sem.at[1])
            c1.start()
            c2.start()
            sums_ref[...] = jnp.zeros_like(sums_ref)
            loss_ref[...] = jnp.zeros_like(loss_ref)
            c1.wait()
            c2.wait()

        x = x_ref[...]
        gm = gm_ref[...]
        r = lax.rsqrt(jnp.mean(x * x, axis=-1, keepdims=True) + EPS)
        xh = x * r
        hb = (xh * gm + sh_ref[...]).astype(_BF)
        h_ref[...] = hb
        u = _nt(hb, wi[...])
        g = u[:, 0:FFN]
        up = u[:, FFN:2 * FFN]
        sg = _sigmoid(g)
        sl = g * sg
        ab = (sl * up).astype(_BF)
        act_ref[...] = ab
        f = _nn(ab, wo[...])
        rf = lax.rsqrt(jnp.mean(f * f, axis=-1, keepdims=True) + EPS)
        fh = f * rf
        gt, g4v = gt_ref[...], g4_ref[...]
        err = x + gt * (fh * g4v) - t_ref[...]
        loss_ref[...] += jnp.sum(err * err) * (0.5 / D)
        dout = err * (1.0 / D)
        sums_ref[2:3, :] += jnp.sum(dout * fh, axis=0, keepdims=True)
        dfh = dout * (gt * g4v)
        dfb = (rf * (dfh - fh * jnp.mean(dfh * fh, axis=-1, keepdims=True))).astype(_BF)
        df_ref[...] = dfb
        dact = _nt(dfb, wo[...])
        du_ref[:, 0:FFN] = (dact * up * (sg * (1.0 + g * (1.0 - sg)))).astype(_BF)
        du_ref[:, FFN:2 * FFN] = (dact * sl).astype(_BF)
        dh = _nn(du_ref[...], wi[...])
        sums_ref[0:1, :] += jnp.sum(dh, axis=0, keepdims=True)
        sums_ref[1:2, :] += jnp.sum(dh * xh, axis=0, keepdims=True)
        dxh = dh * gm
        dx_ref[...] = dout + r * (dxh - xh * jnp.mean(dxh * xh, axis=-1, keepdims=True))

    vec = _full((1, D))
    anyspec = pl.BlockSpec(memory_space=pl.ANY)
    return pl.pallas_call(
        body, name="ffn_fwd_bwd", grid=(S // TF,),
        in_specs=[_rows(TF, D), _rows(TF, D), vec, vec, vec, vec, anyspec, anyspec],
        out_specs=[_rows(TF, D), _rows(TF, D), _rows(TF, 2 * FFN), _rows(TF, FFN), _rows(TF, D),
                   _full((8, D)), _full((8, 128))],
        out_shape=[jax.ShapeDtypeStruct((S, D), F32), jax.ShapeDtypeStruct((S, D), _BF),
                   jax.ShapeDtypeStruct((S, 2 * FFN), _BF), jax.ShapeDtypeStruct((S, FFN), _BF),
                   jax.ShapeDtypeStruct((S, D), _BF), jax.ShapeDtypeStruct((8, D), F32),
                   jax.ShapeDtypeStruct((8, 128), F32)],
        scratch_shapes=[pltpu.VMEM((2 * FFN, D), _BF), pltpu.VMEM((FFN, D), _BF), pltpu.SemaphoreType.DMA((2,))],
        compiler_params=_cp(("arbitrary",), VMEM_BIG),
    )(x1, target, gm2, sh2, gt2, g4, wffi, wffo)


def _inproj_bwd(x, ctx, gml, gmc, win, wg, cos, sa, sb, dq, dk, dv, dgq, dgk, dgv, dgg, dlg_f, dlg_b, dx1):
    S = x.shape[0]
    E = S + CTX
    TE = CTX

    def body(x_ref, c_ref, gml_ref, gmc_ref, w_ref, wg_ref, cos_ref, sa_ref, sb_ref, dq_ref, dk_ref, dv_ref,
             gqf, gqb, gkf, gkb, gvf, gvb, dgg_ref, dlf, dlb, dx1_ref, dp_ref, gx_ref, sums_ref):
        i = pl.program_id(0)
        is_ctx = i == 0

        @pl.when(is_ctx)
        def _():
            sums_ref[...] = jnp.zeros_like(sums_ref)

        lat = jnp.where(is_ctx, 0.0, 1.0)
        cos_t, sa_t, sb_t = cos_ref[...], sa_ref[...], sb_ref[...]
        dp_ref[:, O_Q:O_K] = (_unrope(dq_ref[...].astype(F32), cos_t, sa_t, sb_t) * lat).astype(_BF)
        dp_ref[:, O_K:O_V] = _unrope(dk_ref[...].T, cos_t, sa_t, sb_t).astype(_BF)
        dp_ref[:, O_V:O_GQ] = dv_ref[...].T.astype(_BF)
        dp_ref[:, O_GQ:O_GK] = (gqf[...].astype(F32) + gqb[...].astype(F32)).astype(_BF)
        dp_ref[:, O_GK:O_GV] = (gkf[...].astype(F32) + gkb[...].astype(F32)).astype(_BF)
        dp_ref[:, O_GV:O_GG] = (gvf[...].astype(F32) + gvb[...].astype(F32)).astype(_BF)
        dp_ref[:, O_GG:O_Z] = (dgg_ref[...].astype(F32) * lat).astype(_BF)
        dlg = jnp.concatenate([dlf[...], dlb[...]], axis=1)
        dp_ref[:, O_Z:NP] = _nt(dlg, wg_ref[...]).astype(_BF)
        dh = _nn(dp_ref[...], w_ref[...])
        x = jnp.where(is_ctx, c_ref[...], x_ref[...])
        r = lax.rsqrt(jnp.mean(x * x, axis=-1, keepdims=True) + EPS)
        xh = x * r
        sdh = jnp.sum(dh, axis=0, keepdims=True)
        sdx = jnp.sum(dh * xh, axis=0, keepdims=True)
        sums_ref[0:1, :] += sdh * lat
        sums_ref[1:2, :] += sdx * lat
        sums_ref[2:3, :] += sdh * (1.0 - lat)
        sums_ref[3:4, :] += sdx * (1.0 - lat)
        dxh = dh * jnp.where(is_ctx, gmc_ref[...], gml_ref[...])
        gx_ref[...] = dx1_ref[...] + r * (dxh - xh * jnp.mean(dxh * xh, axis=-1, keepdims=True))

    vec = _full((1, D))
    tab = _rows(TE, 128)
    return pl.pallas_call(
        body, name="inproj_bwd", grid=(E // TE,),
        in_specs=[_rows_lat(TE, D), _full((CTX, D)), vec, vec, _full((NP, D)), _full((128, 512)), tab, tab, tab,
                  _rows_lat(TE, QP), pl.BlockSpec((KP, TE), lambda i: (0, i)), pl.BlockSpec((KP, TE), lambda i: (0, i)),
                  _rows(TE, 256), _rows(TE, 256), _rows(TE, 256), _rows(TE, 256), _rows(TE, 512), _rows(TE, 512),
                  _rows_lat(TE, 512), _rows(TE, 256), _rows(TE, 256), _rows_lat(TE, D)],
        out_specs=[_rows(TE, NP), _rows_lat(TE, D), _full((8, D))],
        out_shape=[jax.ShapeDtypeStruct((E, NP), _BF), jax.ShapeDtypeStruct((S, D), F32),
                   jax.ShapeDtypeStruct((8, D), F32)],
        compiler_params=_cp(("arbitrary",), VMEM_BIG),
    )(x, ctx, gml, gmc, win, wg, cos, sa, sb, dq, dk, dv, *dgq, *dgk, *dgv, dgg, dlg_f, dlg_b, dx1)


def _matmul_tn(a, b, tk, tt, name, out_dtype, transpose_out=False, a_cols=None, hosted=(), gather=True):
    T, KA = a.shape
    N = b.shape[1]
    nt = T // tt
    k0 = 0
    if a_cols is not None:
        KA, k0 = tk, a_cols

    def body(a_ref, b_ref, o_ref, acc):
        t = pl.program_id(1)

        @pl.when(t == 0)
        def _():
            acc[...] = jnp.zeros_like(acc)

        acc[...] += _tn(a_ref[...], b_ref[...])

        @pl.when(t == nt - 1)
        def _():
            o_ref[...] = (acc[...].T if transpose_out else acc[...]).astype(out_dtype)

    if transpose_out:
        out_spec, out_shape = pl.BlockSpec((N, tk), lambda i, t: (0, i)), (N, KA)
    else:
        out_spec, out_shape = pl.BlockSpec((tk, N), lambda i, t: (i, 0)), (KA, N)
    res = _hosted_call(
        body, (a, b), hosted, gather, name=name, grid=(KA // tk, nt),
        in_specs=[pl.BlockSpec((tt, tk), lambda i, t: (t, i + k0)), pl.BlockSpec((tt, N), lambda i, t: (t, 0))],
        out_specs=[out_spec], out_shape=[jax.ShapeDtypeStruct(out_shape, out_dtype)],
        scratch_shapes=[pltpu.VMEM((tk, N), F32)],
        compiler_params=_cp(("arbitrary", "arbitrary"), VMEM_BIG))
    return res if hosted else res[0]


def _ada_bwd(c_all, c_ctx, w_ada, d_all):
    n = w_ada.shape[1]

    def body(c_ref, cc_ref, w_ref, d_ref, gw_ref, t_ref):
        c = jnp.concatenate([c_ref[...], jnp.broadcast_to(cc_ref[...], (8, D))], axis=0)
        db = d_ref[...].astype(_BF)
        gw_ref[0] = _tn((c * _sigmoid(c)).astype(_BF), db)
        t_ref[...] = _nt(db[8:16], w_ref[...].astype(_BF))

    return pl.pallas_call(
        body, name="ada_bwd", in_specs=[_full((8, D)), _full((1, D)), _full((D, n)), _full((16, n))],
        out_specs=[_full((1, D, n)), _full((8, D))],
        out_shape=[jax.ShapeDtypeStruct((1, D, n), F32), jax.ShapeDtypeStruct((8, D), F32)], grid=(1,),
        compiler_params=_cp(("arbitrary",)),
    )(c_all, c_ctx, w_ada, d_all)


PART_ROWS = 56
R_ADA, R_ADA_C, R_GAIN, R_SINK, R_BG, R_GGLA, R_LOSS, R_WG = 0, 6, 12, 16, 17, 18, 19, 24


def _small_grads(s_in, s_ffn, s_mix, ada_l, ada_c, gains, dsink, s_bg, g_wg, loss):
    def body(si, sf, sm, al, ac, g, ds, sbg, gwg, loss_ref, o_ref):
        o_ref[...] = jnp.zeros_like(o_ref)
        o_ref[R_LOSS:R_LOSS + 1, 0:128] = loss_ref[0:1, :]
        sub = lax.broadcasted_iota(jnp.int32, (8, 128), 0)
        lane = lax.broadcasted_iota(jnp.int32, (8, 128), 1)
        o_ref[R_SINK:R_SINK + 1, 0:128] = jnp.sum(jnp.where(sub == lane, ds[...], 0.0), axis=0, keepdims=True)
        o_ref[R_BG:R_BG + 1, 0:512] = sbg[0:1, :]
        y = sm[1:2, 0:128] + sm[1:2, 128:256] + sm[1:2, 256:384] + sm[1:2, 384:512]
        y = y + pltpu.roll(y, 64, 1)
        o_ref[R_GGLA:R_GGLA + 1, 0:128] = jnp.where(lane[0:1] < DV, y, 0.0)
        o_ref[R_WG:R_WG + 16, 0:256] = gwg[0:16, 0:256]
        o_ref[R_WG + 16:R_WG + 32, 0:256] = gwg[16:32, 256:512]
        sdh_l, sdx_l, sdh_c, sdx_c = si[0:1], si[1:2], si[2:3], si[3:4]
        sdh2, sdx2, a2 = sf[0:1], sf[1:2], sf[2:3]
        a1 = sm[0:1]
        g1, g2, g3, g4 = g[0:1], g[1:2], g[2:3], g[3:4]
        sc1, gt1, sc2, gt2 = al[1:2], al[2:3], al[4:5], al[5:6]
        sc1c = ac[1:2]
        z = jnp.zeros((1, D), F32)
        rows = [sdh_l, sdx_l * g1, a1 * g2, sdh2, sdx2 * g3, a2 * g4,
                sdh_c, sdx_c * g1, z, z, z, z,
                sdx_l * (1.0 + sc1) + sdx_c * (1.0 + sc1c), a1 * gt1, sdx2 * (1.0 + sc2), a2 * gt2]
        for r, v in enumerate(rows):
            o_ref[r:r + 1, :] = v

    v8 = _full((8, D))
    return pl.pallas_call(
        body, name="small_grads",
        in_specs=[v8] * 6 + [_full((8, 128)), _full((8, 512)), _full((128, 512)), _full((8, 128))],
        out_specs=_full((PART_ROWS, D)), out_shape=jax.ShapeDtypeStruct((PART_ROWS, D), F32), grid=(1,),
        compiler_params=_cp(("arbitrary",)),
    )(s_in, s_ffn, s_mix, ada_l, ada_c, gains, dsink, s_bg, g_wg, loss)


def _row_tile(R):
    for cand in (256, 128, 64, 32, 16):
        if R % cand == 0 and R > cand:
            return cand
    return R


def _adamw(w, gs, m, v, name):
    _, R, C = w.shape
    tr = _row_tile(R)
    c1 = 1.0 / (1.0 - ADAM_B1 ** ADAM_STEP)
    c2 = 1.0 / (1.0 - ADAM_B2 ** ADAM_STEP)
    ng = len(gs)

    def body(w_ref, *refs):
        g_refs, (m_ref, v_ref, go_ref, d_ref, nm_ref, nv_ref) = refs[:ng], refs[ng:]
        c0 = 0
        for g_ref in g_refs:
            cols = slice(c0, c0 + g_ref.shape[2])
            c0 += g_ref.shape[2]
            gg = g_ref[0].astype(F32)
            for j in range(1, g_ref.shape[0]):
                gg = gg + g_ref[j].astype(F32)
            go_ref[0, :, cols] = gg
            nm = ADAM_B1 * m_ref[0, :, cols] + (1.0 - ADAM_B1) * gg
            nv = ADAM_B2 * v_ref[0, :, cols] + (1.0 - ADAM_B2) * (gg * gg)
            nm_ref[0, :, cols] = nm
            nv_ref[0, :, cols] = nv
            d_ref[0, :, cols] = -ADAM_LR * ((nm * c1) / (jnp.sqrt(nv * c2) + ADAM_EPS) + ADAM_WD * w_ref[0, :, cols])

    spec = pl.BlockSpec((1, tr, C), lambda i: (0, i, 0))
    sds = jax.ShapeDtypeStruct((1, R, C), F32)
    g_specs = [pl.BlockSpec((g.shape[0], tr, g.shape[2]), lambda i: (0, i, 0)) for g in gs]
    return pl.pallas_call(
        body, name=name, grid=(R // tr,), in_specs=[spec] + g_specs + [spec, spec], out_specs=[spec] * 4,
        out_shape=[sds] * 4, compiler_params=_cp(("parallel",), 48 * 1024 * 1024),
    )(w, *gs, m, v)


def _sum_slots(slots, name):
    _, R, C = slots.shape
    tr = _row_tile(R)

    def body(s_ref, o_ref):
        acc = s_ref[0].astype(F32)
        for j in range(1, N_DEV):
            acc = acc + s_ref[j].astype(F32)
        o_ref[...] = acc

    return pl.pallas_call(
        body, name=name, grid=(R // tr,), in_specs=[pl.BlockSpec((N_DEV, tr, C), lambda i: (0, i, 0))],
        out_specs=_rows(tr, C), out_shape=jax.ShapeDtypeStruct((R, C), F32), compiler_params=_cp(("parallel",)),
    )(slots)


def _ag2_start(x_ref, out_ref, send_sems, recv_sems, local_sem):
    x, y, c = lax.axis_index("x"), lax.axis_index("y"), lax.axis_index("c")
    me, sibling = (x, y, c), (x, y, 1 - c)
    chips = [(1 - x, y), (x, 1 - y), (1 - x, 1 - y)]

    def rows(px, py, pc):
        return out_ref.at[4 * px + 2 * py + pc]

    def copy(k, block, to, src=None):
        return pltpu.make_async_remote_copy(
            src_ref=rows(*block) if src is None else src, dst_ref=rows(*block),
            send_sem=send_sems.at[k], recv_sem=recv_sems.at[k], device_id=to, device_id_type=MESH)

    mine = pltpu.make_async_copy(x_ref, rows(*me), local_sem)
    mine.start()
    first = [copy(0, me, sibling, src=x_ref)]
    first += [copy(1 + j, me, (*chip, c), src=x_ref) for j, chip in enumerate(chips)]
    for cp in first:
        cp.start()
    return copy, mine, first, me, sibling, chips, c


def _ag2_finish(state):
    copy, mine, first, me, sibling, chips, c = state
    passed = [copy(4 + j, (*chip, c), sibling) for j, chip in enumerate(chips)]
    for j, chip in enumerate(chips):
        copy(1 + j, (*chip, c), me).wait_recv()
        passed[j].start()
    copy(0, sibling, me).wait_recv()
    for j, chip in enumerate(chips):
        copy(4 + j, (*chip, 1 - c), me).wait_recv()
    for cp in first + passed:
        cp.wait_send()
    mine.wait()


def _exchange(arrays, name, gather):
    na = len(arrays)

    def body(*refs):
        cps = _xchg_copies(refs[:na], refs[na:2 * na], *refs[2 * na:], gather=gather)
        _xchg_start(cps)
        _xchg_finish(cps)

    anyspec = pl.BlockSpec(memory_space=pl.ANY)
    return pl.pallas_call(
        body, name=name, out_shape=_xchg_out_shapes(arrays, gather), in_specs=[anyspec] * na,
        out_specs=[anyspec] * na, scratch_shapes=_xchg_scratch(na),
    )(*arrays)


def _entry(c, wg_sh, win_sh, c_ctx, w_ada):
    n = w_ada.shape[1]

    def body(c_ref, g_ref, w_ref, cc_ref, wa_ref, call_ref, gall_ref, wall_ref, ada_ref, part,
             s_send, s_recv, s_loc, w_send, w_recv, w_loc, a_send, a_recv, a_loc):
        big = _ag2_start(w_ref, wall_ref, w_send, w_recv, w_loc)
        small = _xchg_copies([c_ref, g_ref], [call_ref, gall_ref], s_send, s_recv, s_loc, gather=True)
        _xchg_start(small)
        _xchg_finish(small)
        cs = jnp.concatenate([call_ref[:, 0, :], jnp.broadcast_to(cc_ref[...], (8, D))], axis=0)
        part[...] = _nn((cs * _sigmoid(cs)).astype(_BF), wa_ref[...].astype(_BF))
        ada = _xchg_copies([part], [ada_ref], a_send, a_recv, a_loc, gather=True)
        _xchg_start(ada)
        _xchg_finish(ada)
        _ag2_finish(big)

    vm = pl.BlockSpec(memory_space=pltpu.VMEM)
    return pl.pallas_call(
        body, name="entry_gather",
        out_shape=[jax.ShapeDtypeStruct((N_DEV,) + c.shape, F32), jax.ShapeDtypeStruct((N_DEV,) + wg_sh.shape, F32),
                   jax.ShapeDtypeStruct((N_DEV,) + win_sh.shape, win_sh.dtype),
                   jax.ShapeDtypeStruct((N_DEV, 16, n), F32)],
        in_specs=[vm] * 5, out_specs=[vm] * 4,
        scratch_shapes=[pltpu.VMEM((16, n), F32)] + _xchg_scratch(2)
        + [pltpu.SemaphoreType.DMA((7,)), pltpu.SemaphoreType.DMA((7,)), pltpu.SemaphoreType.DMA] + _xchg_scratch(1),
        compiler_params=pltpu.CompilerParams(vmem_limit_bytes=VMEM_BIG),
    )(c, wg_sh, win_sh, c_ctx, w_ada)


def _rope_tables(S):
    t = np.arange(S)
    row = (t // GRID_W).astype(np.float32)
    colp = (t % GRID_W).astype(np.float32)
    half = HD // 2
    inv = (ROPE_BASE ** (-np.arange(0, half, 2, dtype=np.float32) / half)).astype(np.float32)
    ar = row[:, None] * inv[None, :]
    ac = colp[:, None] * inv[None, :]
    ang = np.concatenate([ar, ar, ac, ac], axis=-1).astype(np.float32)
    cos = np.cos(ang).astype(np.float32)
    sin = np.sin(ang).astype(np.float32)
    lane = np.arange(HD)
    first = (lane % 32) < 16
    sa = np.where(first[None, :], -sin, 0.0)
    sb = np.where(first[None, :], 0.0, sin)

    def ext(tab, ctx_val):
        full = np.zeros((CTX + S, 128), np.float32)
        full[:CTX, :] = ctx_val
        full[CTX:, :HD] = tab
        full[CTX:, HD:] = tab
        return jnp.asarray(full)

    return ext(cos, 1.0), ext(sa, 0.0), ext(sb, 0.0)


def _pad_rows_win(wt):
    return jnp.pad(wt, ((0, NP - IN_COLS), (0, 0)))


def _unpad_rows_win(g):
    return g[0:IN_COLS]


def _local_step(x, ctx, target, ada_l, ada_c, gains, sink, win_p, wg_bd, bg, ggla, wout_sh, wffi_sh, wffo_sh):
    S = x.shape[0]
    cos, sa, sb = _rope_tables(S)
    g1, g2, g3, g4 = (gains[i:i + 1] for i in range(4))
    sh1, sc1, gt1, sh2, sc2, gt2 = (ada_l[i:i + 1] for i in range(6))
    sh1c, sc1c = ada_c[0:1], ada_c[1:2]
    gml, gmc, gm2 = g1 * (1.0 + sc1), g1 * (1.0 + sc1c), g3 * (1.0 + sc2)
    mavg = jnp.asarray(np.kron(np.eye(N_GLA, dtype=np.float32), np.full((DV, DV), 1.0 / DV, np.float32))).astype(_BF)

    n_ffi, r_ffo, r_out = wffi_sh.shape[0], wffo_sh.shape[0], wout_sh.shape[0]
    tt_e = 768 if (S + CTX) % 768 == 0 else 256
    tt_s = 512 if S % 512 == 0 else 256
    h, q, k, v, gq, gk, gv, gg, z, la, wout_g = _inproj_fwd(x, ctx, gml, sh1, gmc, sh1c, win_p, wg_bd, bg,
                                                            cos, sa, sb, [wout_sh])
    attn, lse, probs, wffi_g = _attn_fwd(q, k, v, sink, [wffi_sh])
    o_f, st_f, o_b, st_b, wffo_g = _gla_fwd(gq, gk, gv, la, [wffo_sh])
    wout = wout_g.reshape(N_DEV * r_out, D)
    wffi = wffi_g.reshape(N_DEV * n_ffi, D)
    wffo = wffo_g.reshape(N_DEV * r_ffo, D)
    x1, mix = _mix_fwd(x, attn, o_f, o_b, gg, ggla, mavg, wout, gt1, g2)
    dx1, h2, du, act, df, s_ffn, loss = _ffn(x1, target, gm2, sh2, gt2, g4, wffi, wffo)
    slab_ffi = _matmul_tn(h2, du, 512, tt_s, "grad_w_ffn_in", _BF, True).reshape(N_DEV, n_ffi, D)
    slab_ffo = _matmul_tn(act, df, FFN, tt_s, "grad_w_ffn_out", _BF).reshape(N_DEV, r_ffo, D)
    d_attn, do_gla, dgg, dy, s_mix, got_ffo = _mix_bwd(dx1, mix, o_f, o_b, gg, ggla, mavg, wout, gt1, g2, [slab_ffo])
    slab_out = _matmul_tn(mix, dy, D, tt_s, "grad_w_out", _BF).reshape(N_DEV, r_out, D)
    dq, dk, dv, dsink, got_ffi = _attn_bwd(q, k, v, sink, probs, lse, d_attn, [slab_ffi])
    (dgq_f, dgk_f, dgv_f, dlg_f, dgq_b, dgk_b, dgv_b, dlg_b, g_wg, s_bg,
     got_out) = _gla_bwd(gq, gk, gv, la, z, st_f, st_b, do_gla, [slab_out])
    dp, grad_x, s_in = _inproj_bwd(x, ctx, gml, gmc, win_p, wg_bd, cos, sa, sb, dq, dk, dv,
                                   (dgq_f, dgq_b), (dgk_f, dgk_b), (dgv_f, dgv_b), dgg, dlg_f, dlg_b, dx1)
    small = _small_grads(s_in, s_ffn, s_mix, ada_l, ada_c, gains, dsink, s_bg, g_wg, loss)
    n_in, n_grp = IN_COLS // N_DEV, 2
    got_in, slab = [], None
    for j in range(n_grp):
        g_j, got = _matmul_tn(h, dp, D // n_grp, tt_e, "grad_w_in_%d" % j, _BF, True, a_cols=j,
                              hosted=[small] if j == 0 else [slab], gather=(j == 0))
        if j == 0:
            parts = got
        else:
            got_in.append(got)
        slab = _unpad_rows_win(g_j).reshape(N_DEV, n_in, D // n_grp)
    got_in.append(_exchange([slab], "scatter_grads", False)[0])
    return dict(grad_x=grad_x, got_in=got_in, got_out=got_out, got_ffi=got_ffi, got_ffo=got_ffo, parts=parts)


SMALL_NAMES = ["c_ctx", "b_ada", "g_pre_mix", "g_post_mix", "g_pre_ffn", "g_post_ffn", "attn_sink",
               "b_gate_fwd", "b_gate_bwd", "g_gla_norm", "w_gate_fwd", "w_gate_bwd"]


def _small_update(tot, t_tot, wg_g, w, m, v):
    c1 = 1.0 / (1.0 - ADAM_B1 ** ADAM_STEP)
    c2 = 1.0 / (1.0 - ADAM_B2 ** ADAM_STEP)
    n = len(SMALL_NAMES)

    def body(tot_ref, t_ref, wg_ref, *refs):
        w_r, m_r, v_r = refs[0:n], refs[n:2 * n], refs[2 * n:3 * n]
        g_o, d_o, nm_o, nv_o = refs[3 * n:4 * n], refs[4 * n:5 * n], refs[5 * n:6 * n], refs[6 * n:7 * n]

        def upd(i, idx, g):
            nm = ADAM_B1 * m_r[i][idx] + (1.0 - ADAM_B1) * g
            nv = ADAM_B2 * v_r[i][idx] + (1.0 - ADAM_B2) * (g * g)
            g_o[i][idx] = g
            nm_o[i][idx] = nm
            nv_o[i][idx] = nv
            d_o[i][idx] = -ADAM_LR * ((nm * c1) / (jnp.sqrt(nv * c2) + ADAM_EPS) + ADAM_WD * w_r[i][idx])

        everything = (slice(None), slice(None))
        cc = w_r[0][...]
        sc = _sigmoid(cc)
        upd(0, everything, t_ref[0:1, :] * (sc * (1.0 + cc * (1.0 - sc))))
        for j in range(6):
            upd(1, (slice(None), slice(D * j, D * j + D)),
                tot_ref[R_ADA + j:R_ADA + j + 1, :] + tot_ref[R_ADA_C + j:R_ADA_C + j + 1, :])
        for j in range(4):
            upd(2 + j, everything, tot_ref[R_GAIN + j:R_GAIN + j + 1, :])
        upd(6, everything, tot_ref[R_SINK:R_SINK + 1, 0:N_ATT])
        upd(7, everything, tot_ref[R_BG:R_BG + 1, 0:256])
        upd(8, everything, tot_ref[R_BG:R_BG + 1, 256:512])
        upd(9, everything, tot_ref[R_GGLA:R_GGLA + 1, 0:DV])
        upd(10, (0,), wg_ref[0:GATE_RANK, :])
        upd(11, (0,), wg_ref[GATE_RANK:2 * GATE_RANK, :])

    params = [w[k] for k in SMALL_NAMES] + [m[k] for k in SMALL_NAMES] + [v[k] for k in SMALL_NAMES]
    outs = pl.pallas_call(
        body, name="small_update", grid=(1,),
        in_specs=[_full(tot.shape), _full(t_tot.shape), _full(wg_g.shape)] + [_full(p.shape) for p in params],
        out_specs=[_full(w[k].shape) for k in SMALL_NAMES] * 4,
        out_shape=[jax.ShapeDtypeStruct(w[k].shape, F32) for k in SMALL_NAMES] * 4,
        compiler_params=_cp(("arbitrary",)),
    )(tot, t_tot, wg_g, *params)
    return tuple(dict(zip(SMALL_NAMES, outs[i * n:(i + 1) * n])) for i in range(4))


def kernel(x, c, ctx, c_ctx, w_ada, b_ada, g_pre_mix, g_post_mix, g_pre_ffn, g_post_ffn, w_in, attn_sink, w_gate_fwd, b_gate_fwd, w_gate_bwd, b_gate_bwd, g_gla_norm, w_out, w_ffn_in, w_ffn_out, loss_target, m_c_ctx, m_w_ada, m_b_ada, m_g_pre_mix, m_g_post_mix, m_g_pre_ffn, m_g_post_ffn, m_w_in, m_attn_sink, m_w_gate_fwd, m_b_gate_fwd, m_w_gate_bwd, m_b_gate_bwd, m_g_gla_norm, m_w_out, m_w_ffn_in, m_w_ffn_out, v_c_ctx, v_w_ada, v_b_ada, v_g_pre_mix, v_g_post_mix, v_g_pre_ffn, v_g_post_ffn, v_w_in, v_attn_sink, v_w_gate_fwd, v_b_gate_fwd, v_w_gate_bwd, v_b_gate_bwd, v_g_gla_norm, v_w_out, v_w_ffn_in, v_w_ffn_out):
    me = 4 * lax.axis_index("x") + 2 * lax.axis_index("y") + lax.axis_index("c")
    S = x.shape[1]
    n_in = w_in.shape[2]
    n_ffi = w_ffn_in.shape[2]
    r_out = w_out.shape[1]
    r_ffo = w_ffn_out.shape[1]
    n_ada = w_ada.shape[2]

    wg_sh = jnp.concatenate([w_gate_fwd.reshape(4, 128), w_gate_bwd.reshape(4, 128)], axis=0)
    c_all3, g_all, w_all, ada_all = _entry(c, wg_sh, w_in[0].T.astype(_BF), c_ctx.reshape(1, D), w_ada[0])
    c_all = c_all3.reshape(N_DEV, D)
    wgf = g_all[:, 0:4].reshape(N_DEV, GATE_RANK, 32).transpose(1, 0, 2).reshape(GATE_RANK, 256)
    wgb = g_all[:, 4:8].reshape(N_DEV, GATE_RANK, 32).transpose(1, 0, 2).reshape(GATE_RANK, 256)
    win_p = _pad_rows_win(w_all.reshape(N_DEV * n_in, D))
    wg_bd = jnp.zeros((128, 512), F32).at[0:16, 0:256].set(wgf).at[16:32, 256:512].set(wgb).astype(_BF)
    ada_full = ada_all.transpose(1, 0, 2).reshape(16, N_DEV * n_ada) + b_ada
    ada_l = jnp.pad(lax.dynamic_slice_in_dim(ada_full, me, 1, 0).reshape(6, D), ((0, 2), (0, 0)))
    ada_c = jnp.pad(ada_full[8].reshape(6, D), ((0, 2), (0, 0)))
    gains = jnp.pad(jnp.concatenate([g_pre_mix, g_post_mix, g_pre_ffn, g_post_ffn], axis=0), ((0, 4), (0, 0)))
    sink = jnp.broadcast_to(attn_sink.reshape(8, 1), (8, 128))
    bg = jnp.concatenate([b_gate_fwd, b_gate_bwd], axis=1)
    ggla = jnp.tile(g_gla_norm, (1, N_GLA))

    r = _local_step(x[0], ctx[0], loss_target[0], ada_l, ada_c, gains, sink, win_p, wg_bd, bg, ggla,
                    w_out[0].astype(_BF), w_ffn_in[0].T.astype(_BF), w_ffn_out[0].astype(_BF))

    parts = r["parts"]
    tot = _sum_slots(parts, "sum_small_grads")
    loss = tot[R_LOSS, 0]
    d_ada_rows = parts[:, R_ADA:R_ADA + 6].reshape(N_DEV, 6 * D)
    d_ada_c = tot[R_ADA_C:R_ADA_C + 6].reshape(1, 6 * D)
    my_cols = lax.dynamic_slice_in_dim(jnp.concatenate([d_ada_rows, jnp.broadcast_to(d_ada_c, (1, 6 * D)),
                                                        jnp.zeros((7, 6 * D), F32)], axis=0), me * n_ada, n_ada, 1)
    grad_w_ada, t_part = _ada_bwd(c_all, c_ctx.reshape(1, D), w_ada[0], my_cols)
    wg_g = lax.dynamic_slice(tot, (R_WG, me * 32), (2 * GATE_RANK, 32))

    tr = lambda a: jnp.transpose(a, (0, 2, 1))
    big = {}
    t_all, = _exchange([t_part], "gather_c_ctx", True)
    t_tot = _sum_slots(t_all, "sum_c_ctx")
    for nm, w, g, m, v in [("w_ada", w_ada, grad_w_ada, m_w_ada, v_w_ada),
                           ("w_out", w_out, r["got_out"], m_w_out, v_w_out),
                           ("w_ffn_out", w_ffn_out, r["got_ffo"], m_w_ffn_out, v_w_ffn_out)]:
        big[nm] = _adamw(w, [g], m, v, "adamw_" + nm)
    big["w_ffn_in"] = tuple(tr(o) for o in _adamw(tr(w_ffn_in), [r["got_ffi"]], tr(m_w_ffn_in), tr(v_w_ffn_in),
                                                  "adamw_w_ffn_in"))
    big["w_in"] = tuple(tr(o) for o in _adamw(tr(w_in), r["got_in"], tr(m_w_in), tr(v_w_in), "adamw_w_in"))

    w_small = dict(c_ctx=c_ctx.reshape(1, D), b_ada=b_ada, g_pre_mix=g_pre_mix, g_post_mix=g_post_mix, g_pre_ffn=g_pre_ffn,
                   g_post_ffn=g_post_ffn, attn_sink=attn_sink, b_gate_fwd=b_gate_fwd, b_gate_bwd=b_gate_bwd,
                   g_gla_norm=g_gla_norm, w_gate_fwd=w_gate_fwd, w_gate_bwd=w_gate_bwd)
    m_small = dict(c_ctx=m_c_ctx.reshape(1, D), b_ada=m_b_ada, g_pre_mix=m_g_pre_mix, g_post_mix=m_g_post_mix,
                   g_pre_ffn=m_g_pre_ffn, g_post_ffn=m_g_post_ffn, attn_sink=m_attn_sink, b_gate_fwd=m_b_gate_fwd,
                   b_gate_bwd=m_b_gate_bwd, g_gla_norm=m_g_gla_norm, w_gate_fwd=m_w_gate_fwd, w_gate_bwd=m_w_gate_bwd)
    v_small = dict(c_ctx=v_c_ctx.reshape(1, D), b_ada=v_b_ada, g_pre_mix=v_g_pre_mix, g_post_mix=v_g_post_mix,
                   g_pre_ffn=v_g_pre_ffn, g_post_ffn=v_g_post_ffn, attn_sink=v_attn_sink, b_gate_fwd=v_b_gate_fwd,
                   b_gate_bwd=v_b_gate_bwd, g_gla_norm=v_g_gla_norm, w_gate_fwd=v_w_gate_fwd, w_gate_bwd=v_w_gate_bwd)
    grads_small, d_s, nm_s, nv_s = _small_update(tot, t_tot, wg_g, w_small, m_small, v_small)
    for dd in (grads_small, d_s, nm_s, nv_s):
        dd["c_ctx"] = dd["c_ctx"].reshape(D)

    order = ["c_ctx", "w_ada", "b_ada", "g_pre_mix", "g_post_mix", "g_pre_ffn", "g_post_ffn", "w_in", "attn_sink",
             "w_gate_fwd", "b_gate_fwd", "w_gate_bwd", "b_gate_bwd", "g_gla_norm", "w_out", "w_ffn_in", "w_ffn_out"]
    grads, deltas, new_m, new_v = [], [], [], []
    for nm in order:
        if nm in big:
            g_, d_, m_, v_ = big[nm]
        else:
            g_, d_, m_, v_ = grads_small[nm], d_s[nm], nm_s[nm], nv_s[nm]
        grads.append(g_)
        deltas.append(d_)
        new_m.append(m_)
        new_v.append(v_)
    return (loss, r["grad_x"][None], *grads, *deltas, *new_m, *new_v)
```

```python
import functools
import math

import numpy as np
import jax
import jax.numpy as jnp
from jax import lax
from jax.experimental import pallas as pl
from jax.experimental.pallas import tpu as pltpu

F32 = jnp.float32
_BF = jnp.bfloat16

N_DEV = 8
D = 1024
CTX = 256
HD = 64
N_ATT = 8
N_KV = 2
GRP = N_ATT // N_KV
WIN = 128
GRID_W = 64
ROPE_BASE = 10000.0
N_GLA = 8
DK = 32
DV = 64
GATE_RANK = 16
GATE_TAU = 16.0
FFN = 2816
EPS = 1e-6
NEG = -1e30
GLA_T = 128

QP = N_ATT * HD
KP = N_KV * HD
O_Q, O_K, O_V = 0, QP, QP + KP
O_GQ = O_V + KP
O_GK = O_GQ + N_GLA * DK
O_GV = O_GK + N_GLA * DK
O_GG = O_GV + N_GLA * DV
O_Z = O_GG + N_GLA * DV
NP = O_Z + 128
IN_COLS = 2336

ADAM_LR, ADAM_B1, ADAM_B2, ADAM_EPS, ADAM_WD, ADAM_STEP = 0.001, 0.9, 0.999, 1e-08, 0.01, 10

VMEM_BIG = 56 * 1024 * 1024
MESH = pl.DeviceIdType.MESH


def _cp(sem, vmem=None):
    return pltpu.CompilerParams(dimension_semantics=sem, vmem_limit_bytes=vmem)


def _full(shape):
    nd = len(shape)
    return pl.BlockSpec(shape, lambda *a: (0,) * nd)


def _rows(tile, width, off=0):
    return pl.BlockSpec((tile, width), lambda i: (i + off, 0))


def _rows_lat(tile, width):
    return pl.BlockSpec((tile, width), lambda i: (jnp.maximum(i - 1, 0), 0))


def _nt(a, b):
    return lax.dot_general(a, b, (((1,), (1,)), ((), ())), preferred_element_type=F32)


def _tn(a, b):
    return lax.dot_general(a, b, (((0,), (0,)), ((), ())), preferred_element_type=F32)


def _nn(a, b):
    return jnp.dot(a, b, preferred_element_type=F32)


def _head_mean(x, mavg):
    n = x.shape[0]
    hi = x.astype(_BF)
    lo = (x - hi.astype(F32)).astype(_BF)
    y = _nn(jnp.concatenate([hi, lo], axis=0), mavg)
    return y[0:n] + y[n:2 * n]


def _rope(t, cos, sa, sb):
    n = t.shape[1]
    reps = n // 128
    c = jnp.tile(cos, (1, reps))
    a = jnp.tile(sa, (1, reps))
    b = jnp.tile(sb, (1, reps))
    return t * c + pltpu.roll(t, n - 16, 1) * a + pltpu.roll(t, 16, 1) * b


def _unrope(t, cos, sa, sb):
    n = t.shape[1]
    reps = n // 128
    c = jnp.tile(cos, (1, reps))
    a = jnp.tile(sa, (1, reps))
    b = jnp.tile(sb, (1, reps))
    return t * c + pltpu.roll(t * a, 16, 1) + pltpu.roll(t * b, n - 16, 1)


def _sigmoid(x):
    return 1.0 / (1.0 + jnp.exp(-x))


def _inproj_fwd(x, ctx, gml, shl, gmc, shc, win, wg, bg, cos, sa, sb, shards):
    E = x.shape[0] + CTX
    TE = CTX

    def body(x_ref, c_ref, gml_ref, shl_ref, gmc_ref, shc_ref, w_ref, wg_ref, bg_ref, cos_ref, sa_ref, sb_ref,
             h_ref, q_ref, k_ref, v_ref, gq_ref, gk_ref, gv_ref, gg_ref, z_ref, la_ref):
        is_ctx = pl.program_id(0) == 0
        gm = jnp.where(is_ctx, gmc_ref[...], gml_ref[...])
        sh = jnp.where(is_ctx, shc_ref[...], shl_ref[...])
        x = jnp.where(is_ctx, c_ref[...], x_ref[...])
        r = lax.rsqrt(jnp.mean(x * x, axis=-1, keepdims=True) + EPS)
        hb = ((x * r) * gm + sh).astype(_BF)
        h_ref[...] = hb
        p = _nt(hb, w_ref[...])
        cos_t, sa_t, sb_t = cos_ref[...], sa_ref[...], sb_ref[...]
        q_ref[...] = (_rope(p[:, O_Q:O_K], cos_t, sa_t, sb_t) * (HD ** -0.5)).astype(_BF)
        k_ref[...] = _rope(p[:, O_K:O_V], cos_t, sa_t, sb_t).astype(_BF)
        v_ref[...] = p[:, O_V:O_GQ].astype(_BF)
        gq_ref[...] = p[:, O_GQ:O_GK] * (DK ** -0.5)
        gk_ref[...] = p[:, O_GK:O_GV]
        gv_ref[...] = p[:, O_GV:O_GG]
        gg_ref[...] = p[:, O_GG:O_Z]
        zb = p[:, O_Z:NP].astype(_BF)
        z_ref[...] = zb
        lg = _nn(zb, wg_ref[...]) + bg_ref[...]
        la_ref[...] = (jnp.minimum(lg, 0.0) - jnp.log(1.0 + jnp.exp(-jnp.abs(lg)))) * (1.0 / GATE_TAU)

    vec = _full((1, D))
    tab = _rows(TE, 128)
    outs = [(D, _BF), (QP, _BF), (KP, _BF), (KP, _BF), (256, F32), (256, F32), (512, F32), (512, F32),
            (128, _BF), (512, F32)]
    return _hosted_call(
        body, (x, ctx, gml, shl, gmc, shc, win, wg, bg, cos, sa, sb), shards, True,
        name="inproj_fwd", grid=(E // TE,),
        in_specs=[_rows_lat(TE, D), _full((CTX, D)), vec, vec, vec, vec, _full((NP, D)), _full((128, 512)),
                  _full((1, 512)), tab, tab, tab],
        out_specs=[_rows(TE, w) for w, _ in outs],
        out_shape=[jax.ShapeDtypeStruct((E, w), dt) for w, dt in outs],
        compiler_params=_cp(("arbitrary",), 40 * 1024 * 1024))


def _xchg_scratch(na):
    return [pltpu.SemaphoreType.DMA((na, N_DEV - 1)), pltpu.SemaphoreType.DMA((na, N_DEV - 1)),
            pltpu.SemaphoreType.DMA((na,))]


def _xchg_copies(ins, outs, send_sems, recv_sems, local_sems, gather):
    x, y, c = lax.axis_index("x"), lax.axis_index("y"), lax.axis_index("c")
    me = 4 * x + 2 * y + c
    local, sends, recvs = [], [], []
    for a in range(len(ins)):
        local.append(pltpu.make_async_copy(ins[a] if gather else ins[a].at[me], outs[a].at[me], local_sems.at[a]))
    for k in range(1, N_DEV):
        px, py, pc = x ^ (k >> 2), y ^ ((k >> 1) & 1), c ^ (k & 1)
        peer = 4 * px + 2 * py + pc
        for a in range(len(ins)):
            sems = dict(send_sem=send_sems.at[a, k - 1], recv_sem=recv_sems.at[a, k - 1], device_id_type=MESH)
            sends.append(pltpu.make_async_remote_copy(
                src_ref=ins[a] if gather else ins[a].at[peer], dst_ref=outs[a].at[me], device_id=(px, py, pc), **sems))
            recvs.append(pltpu.make_async_remote_copy(
                src_ref=ins[a] if gather else ins[a].at[me], dst_ref=outs[a].at[peer], device_id=(x, y, c), **sems))
    return local, sends, recvs


def _xchg_start(cps):
    local, sends, _ = cps
    for cp in local + sends:
        cp.start()


def _xchg_finish(cps):
    local, sends, recvs = cps
    for cp in recvs:
        cp.wait_recv()
    for cp in sends:
        cp.wait_send()
    for cp in local:
        cp.wait()


def _xchg_out_shapes(ins, gather):
    return [jax.ShapeDtypeStruct(((N_DEV,) + s.shape) if gather else s.shape, s.dtype) for s in ins]


def _hosted_call(body, args, hosted, gather, *, grid, in_specs, out_specs, out_shape, scratch_shapes=(), **kw):
    na = len(hosted)
    if na == 0:
        return pl.pallas_call(body, grid=grid, in_specs=in_specs, out_specs=out_specs, out_shape=out_shape,
                              scratch_shapes=list(scratch_shapes), **kw)(*args)
    n_in, n_out, n_scr = len(in_specs), len(out_specs), len(scratch_shapes)

    def wrapped(*refs):
        ins, h_in = refs[:n_in], refs[n_in:n_in + na]
        outs, h_out = refs[n_in + na:n_in + na + n_out], refs[n_in + na + n_out:n_in + 2 * na + n_out]
        scr = refs[n_in + 2 * na + n_out:]
        cps = _xchg_copies(h_in, h_out, *scr[n_scr:], gather=gather)
        pids = [pl.program_id(a) for a in range(len(grid))]
        first = functools.reduce(jnp.logical_and, [p == 0 for p in pids])
        last = functools.reduce(jnp.logical_and, [p == g - 1 for p, g in zip(pids, grid)])

        @pl.when(first)
        def _():
            _xchg_start(cps)

        body(*ins, *outs, *scr[:n_scr])

        @pl.when(last)
        def _():
            _xchg_finish(cps)

    anyspec = pl.BlockSpec(memory_space=pl.ANY)
    return pl.pallas_call(
        wrapped, grid=grid, in_specs=list(in_specs) + [anyspec] * na, out_specs=list(out_specs) + [anyspec] * na,
        out_shape=list(out_shape) + _xchg_out_shapes(hosted, gather),
        scratch_shapes=list(scratch_shapes) + _xchg_scratch(na), **kw)(*args, *hosted)


ATT_BLOCKS = 2
PROB_ROWS = N_KV * GRP * WIN


def _attn_specs(E):
    nb = (E - CTX) // WIN
    last = E // WIN - 1
    kc = pl.BlockSpec((CTX, KP), lambda m: (0, 0))
    ks = [pl.BlockSpec((WIN, KP), functools.partial(lambda m, j: (jnp.minimum(ATT_BLOCKS * m + j, last), 0), j=j))
          for j in range(1, ATT_BLOCKS + 3)]
    return nb, [kc] + ks


def _attn_bias(nb):
    rows = np.arange(GRP * WIN)[:, None] % WIN
    cols = np.arange(CTX + 3 * WIN)[None, :]
    j = cols - CTX
    band = np.abs(j - WIN - rows) <= WIN
    out = []
    for first, last in ((True, False), (False, False), (False, True)):
        ok = (cols < CTX) | (band & ((j >= WIN) | (not first)) & ((j < 2 * WIN) | (not last)))
        out.append(np.where(ok, 0.0, NEG).astype(np.float32))
    bias = jnp.asarray(np.stack(out))
    steps = nb // ATT_BLOCKS
    shape = (1, GRP * WIN, CTX + 3 * WIN)
    specs = [pl.BlockSpec(shape, lambda m: (jnp.where(m == 0, 0, 1), 0, 0))]
    specs += [pl.BlockSpec(shape, lambda m: (1, 0, 0))] * (ATT_BLOCKS - 2)
    specs += [pl.BlockSpec(shape, lambda m: (jnp.where(m == steps - 1, 2, 1), 0, 0))]
    return bias, specs


def _both_halves(t, h):
    tf = t.astype(F32)
    r = pltpu.roll(tf, HD, 1)
    lo = lax.broadcasted_iota(jnp.int32, tf.shape, 1) < HD
    return (jnp.where(lo, tf, r) if h == 0 else jnp.where(lo, r, tf)).astype(t.dtype)


def _stack_heads(ref, h):
    lo = lax.broadcasted_iota(jnp.int32, (WIN, 128), 1) < HD
    parts = []
    for g in range(GRP):
        j = GRP * h + g
        t = ref[:, 128 * (j // 2):128 * (j // 2) + 128].astype(F32)
        parts.append(jnp.where(lo if j % 2 == 0 else jnp.logical_not(lo), t, 0.0))
    return jnp.concatenate(parts, axis=0)


def _unstack_pair(o, pp):
    lo = lax.broadcasted_iota(jnp.int32, (WIN, 128), 1) < HD
    return jnp.where(lo, o[WIN * 2 * pp:WIN * 2 * pp + WIN], o[WIN * (2 * pp + 1):WIN * (2 * pp + 1) + WIN])


def _attn_fwd(q, k, v, sink, shards):
    E = q.shape[0]
    S = E - CTX
    nb, kspecs = _attn_specs(E)
    na = len(shards)

    nk = ATT_BLOCKS + 3

    def one_block(q_ref, kw, vw, sink_ref, bias_ref, o_ref, lse_ref, p_ref):
        lane = lax.broadcasted_iota(jnp.int32, (WIN, 128), 1)
        lse_t = jnp.zeros((WIN, 128), F32)
        kall = jnp.concatenate([r[...] for r in kw], axis=0)
        vall = jnp.concatenate([r[...] for r in vw], axis=0)
        K = [_both_halves(kall, h) for h in range(N_KV)]
        Q = [_stack_heads(q_ref, h).astype(_BF) for h in range(N_KV)]
        sk = [jnp.concatenate([jnp.broadcast_to(sink_ref[GRP * h + g:GRP * h + g + 1, 0:1], (WIN, 1))
                               for g in range(GRP)], axis=0) for h in range(N_KV)]
        s = [_nt(Q[h], K[h]) + bias_ref[0] for h in range(N_KV)]
        m = [jnp.maximum(jnp.max(s[h], axis=1, keepdims=True), sk[h]) for h in range(N_KV)]
        e = [jnp.exp(s[h] - m[h]) for h in range(N_KV)]
        den = [jnp.sum(e[h], axis=1, keepdims=True) + jnp.exp(sk[h] - m[h]) for h in range(N_KV)]
        V = [_both_halves(vall, h) for h in range(N_KV)]
        pb = [(e[h] * (1.0 / den[h])).astype(_BF) for h in range(N_KV)]
        for h in range(N_KV):
            p_ref[GRP * WIN * h:GRP * WIN * (h + 1), :] = pb[h]
        o = [_nn(pb[h], V[h]) for h in range(N_KV)]
        for h in range(N_KV):
            lse = m[h] + jnp.log(den[h])
            for g in range(GRP):
                lse_t = jnp.where(lane == GRP * h + g, lse[WIN * g:WIN * g + WIN], lse_t)
            for pp in range(GRP // 2):
                t = 2 * h + pp
                o_ref[:, 128 * t:128 * t + 128] = _unstack_pair(o[h], pp).astype(_BF)
        lse_ref[...] = lse_t

    def body(q_ref, *rest):
        kr, vr, sink_ref = rest[0:nk], rest[nk:2 * nk], rest[2 * nk]
        bias_refs = rest[2 * nk + 1:2 * nk + 1 + ATT_BLOCKS]
        rest = rest[2 * nk + 1 + ATT_BLOCKS:]
        shard_refs, (o_ref, lse_ref, p_ref), got_refs = rest[:na], rest[na:na + 3], rest[na + 3:2 * na + 3]
        cps = _xchg_copies(shard_refs, got_refs, *rest[2 * na + 3:], gather=True)

        @pl.when(pl.program_id(0) == 0)
        def _():
            _xchg_start(cps)

        for j in range(ATT_BLOCKS):
            rows = pl.ds(WIN * j, WIN)
            one_block(q_ref.at[rows], [kr[0]] + list(kr[1 + j:4 + j]), [vr[0]] + list(vr[1 + j:4 + j]), sink_ref,
                      bias_refs[j], o_ref.at[rows], lse_ref.at[rows], p_ref.at[pl.ds(PROB_ROWS * j, PROB_ROWS)])

        @pl.when(pl.program_id(0) == nb // ATT_BLOCKS - 1)
        def _():
            _xchg_finish(cps)

    tq = ATT_BLOCKS * WIN
    qs = pl.BlockSpec((tq, QP), lambda m: (m + CTX // tq, 0))
    anyspec = pl.BlockSpec(memory_space=pl.ANY)
    bias, bias_specs = _attn_bias(nb)
    return pl.pallas_call(
        body, name="attn_fwd", grid=(nb // ATT_BLOCKS,),
        in_specs=[qs] + kspecs + kspecs + [_full((8, 128))] + bias_specs + [anyspec] * na,
        out_specs=[_rows(tq, 512), _rows(tq, 128), _rows(ATT_BLOCKS * PROB_ROWS, CTX + 3 * WIN)] + [anyspec] * na,
        out_shape=[jax.ShapeDtypeStruct((S, 512), _BF), jax.ShapeDtypeStruct((S, 128), F32),
                   jax.ShapeDtypeStruct((nb * PROB_ROWS, CTX + 3 * WIN), _BF)]
        + _xchg_out_shapes(shards, True),
        scratch_shapes=_xchg_scratch(na),
        compiler_params=_cp(("arbitrary",), 48 * 1024 * 1024),
    )(q, *([k] * nk), *([v] * nk), sink, *([bias] * ATT_BLOCKS), *shards)


def _attn_bwd(q, k, v, sink, probs, lse, d_attn, slabs):
    E = q.shape[0]
    S = E - CTX
    nb, kspecs = _attn_specs(E)
    last = E // WIN - 1
    na = len(slabs)

    nk = ATT_BLOCKS + 3

    def one_block(n, q_ref, kw, vw, sink_ref, p_ref, lse_ref, do_ref, dq_ref, dk_ref, dv_ref, ds_ref):
        lane = lax.broadcasted_iota(jnp.int32, (WIN, 128), 1)
        lse_t = lse_ref[...]
        starts = [None, pl.multiple_of((n + 1) * WIN, WIN), pl.multiple_of((n + 2) * WIN, WIN),
                  pl.multiple_of(jnp.minimum(n + 3, last) * WIN, WIN)]
        kall = jnp.concatenate([r[...] for r in kw], axis=0)
        vall = jnp.concatenate([r[...] for r in vw], axis=0)
        for h in range(N_KV):
            hs = slice(HD * h, HD * h + HD)
            K = _both_halves(kall, h)
            V = _both_halves(vall, h)
            Q = _stack_heads(q_ref, h).astype(_BF)
            sk = jnp.concatenate([jnp.broadcast_to(sink_ref[GRP * h + g:GRP * h + g + 1, 0:1], (WIN, 1))
                                  for g in range(GRP)], axis=0)
            ls = jnp.concatenate([jnp.sum(jnp.where(lane == GRP * h + g, lse_t, 0.0), axis=1, keepdims=True)
                                  for g in range(GRP)], axis=0)
            do = _stack_heads(do_ref, h).astype(_BF)
            pb = p_ref[GRP * WIN * h:GRP * WIN * (h + 1), :]
            p = pb.astype(F32)
            dp = _nt(do, V)
            delta = jnp.sum(p * dp, axis=1, keepdims=True)
            dsc = (p * (dp - delta)).astype(_BF)
            dq = _nn(dsc, K) * (HD ** -0.5)
            for pp in range(GRP // 2):
                t = 2 * h + pp
                dq_ref[:, 128 * t:128 * t + 128] = _unstack_pair(dq, pp).astype(_BF)
            dK2 = _tn(Q, dsc)
            dV2 = _tn(do, pb)
            dK = dK2[0:HD] + dK2[HD:2 * HD]
            dV = dV2[0:HD] + dV2[HD:2 * HD]
            dk_ref[hs, 0:CTX] += dK[:, 0:CTX]
            dv_ref[hs, 0:CTX] += dV[:, 0:CTX]
            for w in range(1, 4):
                lo = CTX + WIN * (w - 1)
                dk_ref[hs, pl.ds(starts[w], WIN)] += dK[:, lo:lo + WIN]
                dv_ref[hs, pl.ds(starts[w], WIN)] += dV[:, lo:lo + WIN]
            psk = -jnp.exp(sk - ls) * delta
            for g in range(GRP):
                j = GRP * h + g
                tot = jnp.sum(psk[WIN * g:WIN * g + WIN], axis=0, keepdims=True)
                ds_ref[j:j + 1, :] += jnp.broadcast_to(tot, (1, 128))

    def body(q_ref, *rest):
        kr, vr = rest[0:nk], rest[nk:2 * nk]
        sink_ref, p_ref, lse_ref, do_ref = rest[2 * nk:2 * nk + 4]
        rest = rest[2 * nk + 4:]
        slab_refs, (dq_ref, dk_ref, dv_ref, ds_ref), got_refs = rest[:na], rest[na:na + 4], rest[na + 4:2 * na + 4]
        m = pl.program_id(0)
        cps = _xchg_copies(slab_refs, got_refs, *rest[2 * na + 4:], gather=False)

        @pl.when(m == 0)
        def _():
            _xchg_start(cps)
            dk_ref[...] = jnp.zeros_like(dk_ref)
            dv_ref[...] = jnp.zeros_like(dv_ref)
            ds_ref[...] = jnp.zeros_like(ds_ref)

        for j in range(ATT_BLOCKS):
            rows = pl.ds(WIN * j, WIN)
            one_block(ATT_BLOCKS * m + j, q_ref.at[rows], [kr[0]] + list(kr[1 + j:4 + j]),
                      [vr[0]] + list(vr[1 + j:4 + j]), sink_ref, p_ref.at[pl.ds(PROB_ROWS * j, PROB_ROWS)],
                      lse_ref.at[rows], do_ref.at[rows], dq_ref.at[rows], dk_ref, dv_ref, ds_ref)

        @pl.when(m == nb // ATT_BLOCKS - 1)
        def _():
            _xchg_finish(cps)

    tq = ATT_BLOCKS * WIN
    qs = pl.BlockSpec((tq, QP), lambda m: (m + CTX // tq, 0))
    anyspec = pl.BlockSpec(memory_space=pl.ANY)
    return pl.pallas_call(
        body, name="attn_bwd", grid=(nb // ATT_BLOCKS,),
        in_specs=[qs] + kspecs + kspecs + [_full((8, 128)), _rows(ATT_BLOCKS * PROB_ROWS, CTX + 3 * WIN),
                                            _rows(tq, 128), _rows(tq, 512)] + [anyspec] * na,
        out_specs=[_rows(tq, QP), _full((KP, E)), _full((KP, E)), _full((8, 128))] + [anyspec] * na,
        out_shape=[jax.ShapeDtypeStruct((S, QP), _BF), jax.ShapeDtypeStruct((KP, E), F32),
                   jax.ShapeDtypeStruct((KP, E), F32), jax.ShapeDtypeStruct((8, 128), F32)]
        + _xchg_out_shapes(slabs, False),
        scratch_shapes=_xchg_scratch(na),
        compiler_params=_cp(("arbitrary",), 48 * 1024 * 1024),
    )(q, *([k] * nk), *([v] * nk), sink, probs, lse, d_attn, *slabs)


def _gla_order(E, reverse, backward):
    nc = CTX // GLA_T
    n = E // GLA_T
    if not reverse:
        fwd = lambda s: s
    else:
        fwd = lambda s: jnp.where(s < nc, nc - 1 - s, n - 1 + nc - s)
    if backward:
        return lambda s: fwd(n - 1 - s)
    return fwd


def _gla_masks():
    T = GLA_T
    l128 = lax.broadcasted_iota(jnp.int32, (1, 128), 1)
    qmask = [((l128 >> 5) == j).astype(F32) for j in range(4)]
    vmask = [((l128 >> 6) == j).astype(F32) for j in range(2)]
    bd = ((lax.broadcasted_iota(jnp.int32, (512, 256), 0) >> 6)
          == (lax.broadcasted_iota(jnp.int32, (512, 256), 1) >> 5)).astype(F32)
    ri = lax.broadcasted_iota(jnp.int32, (T, 2 * T), 0)
    ci = lax.broadcasted_iota(jnp.int32, (T, 2 * T), 1) & (T - 1)
    return qmask, vmask, bd, ri, ci


def _tri_sum(tri, x):
    hi = x.astype(_BF)
    lo = (x - hi.astype(F32)).astype(_BF)
    n = x.shape[1]
    y = _nn(tri.astype(_BF), jnp.concatenate([hi, lo], axis=1))
    return y[:, 0:n] + y[:, n:2 * n]


def _gla_decays(la, reverse, ri, ci):
    T = GLA_T
    msk2 = (ri <= ci) if reverse else (ri >= ci)
    mskT2 = (ri >= ci) if reverse else (ri <= ci)
    b = _tri_sum(msk2[:, 0:T], la)
    bT = b[0:1] if reverse else b[T - 1:T]
    bm = b[T // 2:T // 2 + 1]
    return msk2, mskT2, b, bT, bm


def _pair_stack(tile, m0, m1):
    return jnp.concatenate([(tile * m0).astype(_BF), (tile * m1).astype(_BF)], axis=0)


def _gla_fwd(gq, gk, gv, la, shards=()):
    E = gq.shape[0]
    T = GLA_T
    n = E // T
    orders = [_gla_order(E, False, False), _gla_order(E, True, False)]

    def one_direction(reverse, gq_ref, gk_ref, gv_ref, la_ref, o_ref, st_ref, S_scr):
        qmask, vmask, bd, ri, ci = _gla_masks()
        msk2, _, b, bT, bm = _gla_decays(la_ref[...], reverse, ri, ci)
        q, k, v = gq_ref[...], gk_ref[...], gv_ref[...]
        qd = (q * jnp.exp(b)).astype(_BF)
        qm = (q * jnp.exp(b - bm)).astype(_BF)
        km = k * jnp.exp(bm - b)
        kd = (k * jnp.exp(bT - b)).astype(_BF)
        ST = S_scr[...]
        comp = ST[0:DV]
        for h in range(1, N_GLA):
            comp = comp + ST[DV * h:DV * h + DV]
        st_ref[0] = comp
        inter = _nt(qd, ST.astype(_BF))
        tiles = []
        for p in range(N_GLA // 2):
            qs = slice(128 * (p // 2), 128 * (p // 2) + 128)
            vs = slice(128 * p, 128 * p + 128)
            j0 = (2 * p) % 4
            KS = _pair_stack(km[:, qs], qmask[j0], qmask[j0 + 1])
            VS = _pair_stack(v[:, vs], vmask[0], vmask[1])
            AA = jnp.where(msk2, _nt(qm[:, qs], KS), 0.0).astype(_BF)
            tiles.append(_nn(AA, VS))
        o_ref[...] = inter + jnp.concatenate(tiles, axis=1)
        S_scr[...] = ST * jnp.exp(bT) + bd * _tn(v.astype(_BF), kd)

    def body(qf, kf, vf, lf, qr, kr, vr, lr, of, sf, orr, sr, S_f, S_r):
        @pl.when(pl.program_id(0) == 0)
        def _():
            S_f[...] = jnp.zeros_like(S_f)
            S_r[...] = jnp.zeros_like(S_r)

        one_direction(False, qf, kf, vf, lf, of, sf, S_f)
        one_direction(True, qr, kr, vr, lr, orr, sr, S_r)

    def blk(d, w, c=0):
        return pl.BlockSpec((T, w), lambda s: (orders[d](s), c))

    def st_spec(d):
        return pl.BlockSpec((1, DV, 256), lambda s: (orders[d](s), 0, 0))

    return _hosted_call(
        body, (gq, gk, gv, la, gq, gk, gv, la), shards, True, name="gla_fwd", grid=(n,),
        in_specs=[blk(0, 256), blk(0, 256), blk(0, 512), blk(0, 256, 0), blk(1, 256), blk(1, 256), blk(1, 512),
                  blk(1, 256, 1)],
        out_specs=[blk(0, 512), st_spec(0), blk(1, 512), st_spec(1)],
        out_shape=[jax.ShapeDtypeStruct((E, 512), F32), jax.ShapeDtypeStruct((n, DV, 256), F32)] * 2,
        scratch_shapes=[pltpu.VMEM((512, 256), F32)] * 2,
        compiler_params=_cp(("arbitrary",)))


def _gla_bwd(gq, gk, gv, la, z, st_f, st_r, do, slabs=()):
    E = gq.shape[0]
    T = GLA_T
    n = E // T
    nc = CTX // T
    orders = [_gla_order(E, False, True), _gla_order(E, True, True)]

    def one_direction(reverse, gq_ref, gk_ref, gv_ref, la_ref, z_ref, st_ref, do_ref,
                      dq_ref, dk_ref, dv_ref, dlg_ref, gwg_ref, bsum_ref, dS_scr):
        cols = slice(256, 512) if reverse else slice(0, 256)
        is_lat = orders[1 if reverse else 0](pl.program_id(0)) >= nc
        qmask, vmask, bd, ri, ci = _gla_masks()
        msk2, mskT2, b, bT, bm = _gla_decays(la_ref[...], reverse, ri, ci)
        q, k, v = gq_ref[...], gk_ref[...], gv_ref[...]
        do = jnp.where(is_lat, do_ref[...].astype(F32), 0.0)
        e_b, e_qm, e_km, e_kd, e_T = jnp.exp(b), jnp.exp(b - bm), jnp.exp(bm - b), jnp.exp(bT - b), jnp.exp(bT)
        qd, qm, km, kd = q * e_b, q * e_qm, k * e_km, k * e_kd
        qdb, qmb, kmb, kdb, vb, dob = (t.astype(_BF) for t in (qd, qm, km, kd, v, do))
        ST = jnp.tile(st_ref[0], (N_GLA, 1)) * bd
        dST = dS_scr[...]
        dSTb = dST.astype(_BF)
        dqd = _nn(dob, ST.astype(_BF))
        dkd = _nn(vb, dSTb)
        dv_t, dqm_t, dkm_t = [], [None, None], [None, None]
        for p in range(N_GLA // 2):
            t = p // 2
            qs = slice(128 * t, 128 * t + 128)
            vs = slice(128 * p, 128 * p + 128)
            j0 = (2 * p) % 4
            QS = _pair_stack(qm[:, qs], qmask[j0], qmask[j0 + 1])
            KS = _pair_stack(km[:, qs], qmask[j0], qmask[j0 + 1])
            VS = _pair_stack(v[:, vs], vmask[0], vmask[1])
            DS = _pair_stack(do[:, vs], vmask[0], vmask[1])
            ATT = jnp.where(mskT2, _nt(kmb[:, qs], QS), 0.0).astype(_BF)
            dAA = jnp.where(msk2, _nt(dob[:, vs], VS), 0.0).astype(_BF)
            dATT = jnp.where(mskT2, _nt(vb[:, vs], DS), 0.0).astype(_BF)
            dv_t.append(_nn(ATT, DS))
            dq_p = _nn(dAA, KS)
            dk_p = _nn(dATT, QS)
            dqm_t[t] = dq_p if dqm_t[t] is None else dqm_t[t] + dq_p
            dkm_t[t] = dk_p if dkm_t[t] is None else dkm_t[t] + dk_p
        dqm = jnp.concatenate(dqm_t, axis=1)
        dkm = jnp.concatenate(dkm_t, axis=1)
        dq = dqm * e_qm + dqd * e_b
        dk = dkm * e_km + dkd * e_kd
        dv = _nt(kdb, dSTb) + jnp.concatenate(dv_t, axis=1)
        dq_ref[...] = (dq * (DK ** -0.5)).astype(_BF)
        dk_ref[...] = dk.astype(_BF)
        dv_ref[...] = dv.astype(_BF)
        db = dqm * qm - dkm * km + dqd * qd - dkd * kd
        dbT = jnp.sum(dkd * kd, axis=0, keepdims=True) + e_T * jnp.sum(dST * ST, axis=0, keepdims=True)
        dla = _tri_sum(mskT2[:, 0:T], db) + dbT
        dlg = dla * (1.0 - jnp.exp(GATE_TAU * la_ref[...])) * (1.0 / GATE_TAU)
        bsum_ref[0:1, cols] += jnp.sum(dlg, axis=0, keepdims=True)
        dlgb = dlg.astype(_BF)
        dlg_ref[...] = dlgb
        gwg_ref[:, cols] += _tn(z_ref[...], dlgb)
        dS_scr[...] = dST * e_T + bd * _tn(dob, qdb)

    def body(*refs):
        ins_f, ins_r = refs[0:7], refs[7:14]
        outs_f, outs_r = refs[14:18], refs[18:22]
        gwg_ref, bsum_ref, dS_f, dS_r = refs[22:26]

        @pl.when(pl.program_id(0) == 0)
        def _():
            dS_f[...] = jnp.zeros_like(dS_f)
            dS_r[...] = jnp.zeros_like(dS_r)
            gwg_ref[...] = jnp.zeros_like(gwg_ref)
            bsum_ref[...] = jnp.zeros_like(bsum_ref)

        one_direction(False, *ins_f, *outs_f, gwg_ref, bsum_ref, dS_f)
        one_direction(True, *ins_r, *outs_r, gwg_ref, bsum_ref, dS_r)

    def specs(d, st):
        order = orders[d]
        blk = lambda w, c=0: pl.BlockSpec((T, w), lambda s: (order(s), c))
        ins = [blk(256), blk(256), blk(512), blk(256, d), blk(128),
               pl.BlockSpec((1, DV, 256), lambda s: (order(s), 0, 0)),
               pl.BlockSpec((T, 512), lambda s: (jnp.maximum(order(s) - nc, 0), 0))]
        return ins, [blk(256), blk(256), blk(512), blk(256)], (gq, gk, gv, la, z, st, do)

    in_f, out_f, args_f = specs(0, st_f)
    in_r, out_r, args_r = specs(1, st_r)
    dir_shapes = [jax.ShapeDtypeStruct((E, 256), _BF), jax.ShapeDtypeStruct((E, 256), _BF),
                  jax.ShapeDtypeStruct((E, 512), _BF), jax.ShapeDtypeStruct((E, 256), _BF)]
    return _hosted_call(
        body, args_f + args_r, slabs, False, name="gla_bwd", grid=(n,),
        in_specs=in_f + in_r, out_specs=out_f + out_r + [_full((128, 512)), _full((8, 512))],
        out_shape=dir_shapes * 2 + [jax.ShapeDtypeStruct((128, 512), F32), jax.ShapeDtypeStruct((8, 512), F32)],
        scratch_shapes=[pltpu.VMEM((512, 256), F32)] * 2,
        compiler_params=_cp(("arbitrary",)))


def _gla_out(o, gg, ggla, mavg):
    rr = lax.rsqrt(_head_mean(o * o, mavg) + EPS)
    oh = o * rr
    sg = _sigmoid(gg)
    return oh, rr, sg


def _mix_fwd(x, attn, o_f, o_b, gg, ggla, mavg, wout, gt1, g2):
    S = x.shape[0]
    TM = 256

    def body(x_ref, a_ref, of_ref, ob_ref, gg_ref, ggla_ref, mavg_ref, w_ref, gt1_ref, g2_ref, x1_ref, mix_ref):
        gg_t = gg_ref[...]
        oh, _, sg = _gla_out(of_ref[...] + ob_ref[...], gg_t, ggla_ref[...], mavg_ref[...])
        mix_ref[:, 0:512] = a_ref[...]
        mix_ref[:, 512:1024] = (oh * ggla_ref[...] * (gg_t * sg)).astype(_BF)
        y = _nn(mix_ref[...], w_ref[...])
        ry = lax.rsqrt(jnp.mean(y * y, axis=-1, keepdims=True) + EPS)
        x1_ref[...] = x_ref[...] + gt1_ref[...] * ((y * ry) * g2_ref[...])

    return pl.pallas_call(
        body, name="mix_fwd", grid=(S // TM,),
        in_specs=[_rows(TM, D), _rows(TM, 512), _rows(TM, 512, 1), _rows(TM, 512, 1), _rows(TM, 512, 1),
                  _full((1, 512)), _full((512, 512)), _full((D, D)), _full((1, D)), _full((1, D))],
        out_specs=[_rows(TM, D), _rows(TM, D)],
        out_shape=[jax.ShapeDtypeStruct((S, D), F32), jax.ShapeDtypeStruct((S, D), _BF)],
        compiler_params=_cp(("arbitrary",), 40 * 1024 * 1024),
    )(x, attn, o_f, o_b, gg, ggla, mavg, wout, gt1, g2)


def _mix_bwd(dx1, mix, o_f, o_b, gg, ggla, mavg, wout, gt1, g2, slabs):
    S = dx1.shape[0]
    TM = 256

    def body(dx_ref, mix_ref, of_ref, ob_ref, gg_ref, ggla_ref, mavg_ref, w_ref, gt1_ref, g2_ref,
             da_ref, do_ref, dgg_ref, dy_ref, sums_ref):
        @pl.when(pl.program_id(0) == 0)
        def _():
            sums_ref[...] = jnp.zeros_like(sums_ref)

        dx = dx_ref[...]
        y = _nn(mix_ref[...], w_ref[...])
        ry = lax.rsqrt(jnp.mean(y * y, axis=-1, keepdims=True) + EPS)
        yh = y * ry
        sums_ref[0:1, :] += jnp.sum(dx * yh, axis=0, keepdims=True)
        dyh = dx * (gt1_ref[...] * g2_ref[...])
        dy = (ry * (dyh - yh * jnp.mean(dyh * yh, axis=-1, keepdims=True))).astype(_BF)
        dy_ref[...] = dy
        dmix = _nt(dy, w_ref[...])
        da_ref[...] = dmix[:, 0:512].astype(_BF)
        dgla = dmix[:, 512:1024]
        gg_t = gg_ref[...]
        ggla_t = ggla_ref[...]
        oh, rr, sg = _gla_out(of_ref[...] + ob_ref[...], gg_t, ggla_t, mavg_ref[...])
        dgg_ref[...] = (dgla * oh * ggla_t * (sg * (1.0 + gg_t * (1.0 - sg)))).astype(_BF)
        don = dgla * (gg_t * sg)
        sums_ref[1:2, 0:512] += jnp.sum(don * oh, axis=0, keepdims=True)
        doh = don * ggla_t
        do_ref[...] = (rr * (doh - oh * _head_mean(doh * oh, mavg_ref[...]))).astype(_BF)

    return _hosted_call(
        body, (dx1, mix, o_f, o_b, gg, ggla, mavg, wout, gt1, g2), slabs, False,
        name="mix_bwd", grid=(S // TM,),
        in_specs=[_rows(TM, D), _rows(TM, D), _rows(TM, 512, 1), _rows(TM, 512, 1), _rows(TM, 512, 1),
                  _full((1, 512)), _full((512, 512)), _full((D, D)), _full((1, D)), _full((1, D))],
        out_specs=[_rows(TM, 512), _rows(TM, 512), _rows(TM, 512), _rows(TM, D), _full((8, D))],
        out_shape=[jax.ShapeDtypeStruct((S, 512), _BF), jax.ShapeDtypeStruct((S, 512), _BF),
                   jax.ShapeDtypeStruct((S, 512), _BF), jax.ShapeDtypeStruct((S, D), _BF),
                   jax.ShapeDtypeStruct((8, D), F32)],
        compiler_params=_cp(("arbitrary",), 40 * 1024 * 1024))


def _ffn(x1, target, gm2, sh2, gt2, g4, wffi, wffo):
    S = x1.shape[0]
    TF = 256

    def body(x_ref, t_ref, gm_ref, sh_ref, gt_ref, g4_ref, wi_hbm, wo_hbm,
             dx_ref, h_ref, du_ref, act_ref, df_ref, sums_ref, loss_ref, wi, wo, sem):
        @pl.when(pl.program_id(0) == 0)
        def _():
            c1 = pltpu.make_async_copy(wi_hbm, wi, sem.at[0])
            c2 = pltpu.make_async_copy(wo_hbm, wo, sem.at[1])
            c1.start()
            c2.start()
            sums_ref[...] = jnp.zeros_like(sums_ref)
            loss_ref[...] = jnp.zeros_like(loss_ref)
            c1.wait()
            c2.wait()

        x = x_ref[...]
        gm = gm_ref[...]
        r = lax.rsqrt(jnp.mean(x * x, axis=-1, keepdims=True) + EPS)
        xh = x * r
        hb = (xh * gm + sh_ref[...]).astype(_BF)
        h_ref[...] = hb
        u = _nt(hb, wi[...])
        g = u[:, 0:FFN]
        up = u[:, FFN:2 * FFN]
        sg = _sigmoid(g)
        sl = g * sg
        ab = (sl * up).astype(_BF)
        act_ref[...] = ab
        f = _nn(ab, wo[...])
        rf = lax.rsqrt(jnp.mean(f * f, axis=-1, keepdims=True) + EPS)
        fh = f * rf
        gt, g4v = gt_ref[...], g4_ref[...]
        err = x + gt * (fh * g4v) - t_ref[...]
        loss_ref[...] += jnp.sum(err * err) * (0.5 / D)
        dout = err * (1.0 / D)
        sums_ref[2:3, :] += jnp.sum(dout * fh, axis=0, keepdims=True)
        dfh = dout * (gt * g4v)
        dfb = (rf * (dfh - fh * jnp.mean(dfh * fh, axis=-1, keepdims=True))).astype(_BF)
        df_ref[...] = dfb
        dact = _nt(dfb, wo[...])
        du_ref[:, 0:FFN] = (dact * up * (sg * (1.0 + g * (1.0 - sg)))).astype(_BF)
        du_ref[:, FFN:2 * FFN] = (dact * sl).astype(_BF)
        dh = _nn(du_ref[...], wi[...])
        sums_ref[0:1, :] += jnp.sum(dh, axis=0, keepdims=True)
        sums_ref[1:2, :] += jnp.sum(dh * xh, axis=0, keepdims=True)
        dxh = dh * gm
        dx_ref[...] = dout + r * (dxh - xh * jnp.mean(dxh * xh, axis=-1, keepdims=True))

    vec = _full((1, D))
    anyspec = pl.BlockSpec(memory_space=pl.ANY)
    return pl.pallas_call(
        body, name="ffn_fwd_bwd", grid=(S // TF,),
        in_specs=[_rows(TF, D), _rows(TF, D), vec, vec, vec, vec, anyspec, anyspec],
        out_specs=[_rows(TF, D), _rows(TF, D), _rows(TF, 2 * FFN), _rows(TF, FFN), _rows(TF, D),
                   _full((8, D)), _full((8, 128))],
        out_shape=[jax.ShapeDtypeStruct((S, D), F32), jax.ShapeDtypeStruct((S, D), _BF),
                   jax.ShapeDtypeStruct((S, 2 * FFN), _BF), jax.ShapeDtypeStruct((S, FFN), _BF),
                   jax.ShapeDtypeStruct((S, D), _BF), jax.ShapeDtypeStruct((8, D), F32),
                   jax.ShapeDtypeStruct((8, 128), F32)],
        scratch_shapes=[pltpu.VMEM((2 * FFN, D), _BF), pltpu.VMEM((FFN, D), _BF), pltpu.SemaphoreType.DMA((2,))],
        compiler_params=_cp(("arbitrary",), VMEM_BIG),
    )(x1, target, gm2, sh2, gt2, g4, wffi, wffo)


def _inproj_bwd(x, ctx, gml, gmc, win, wg, cos, sa, sb, dq, dk, dv, dgq, dgk, dgv, dgg, dlg_f, dlg_b, dx1):
    S = x.shape[0]
    E = S + CTX
    TE = CTX

    def body(x_ref, c_ref, gml_ref, gmc_ref, w_ref, wg_ref, cos_ref, sa_ref, sb_ref, dq_ref, dk_ref, dv_ref,
             gqf, gqb, gkf, gkb, gvf, gvb, dgg_ref, dlf, dlb, dx1_ref, dp_ref, gx_ref, sums_ref):
        i = pl.program_id(0)
        is_ctx = i == 0

        @pl.when(is_ctx)
        def _():
            sums_ref[...] = jnp.zeros_like(sums_ref)

        lat = jnp.where(is_ctx, 0.0, 1.0)
        cos_t, sa_t, sb_t = cos_ref[...], sa_ref[...], sb_ref[...]
        dp_ref[:, O_Q:O_K] = (_unrope(dq_ref[...].astype(F32), cos_t, sa_t, sb_t) * lat).astype(_BF)
        dp_ref[:, O_K:O_V] = _unrope(dk_ref[...].T, cos_t, sa_t, sb_t).astype(_BF)
        dp_ref[:, O_V:O_GQ] = dv_ref[...].T.astype(_BF)
        dp_ref[:, O_GQ:O_GK] = (gqf[...].astype(F32) + gqb[...].astype(F32)).astype(_BF)
        dp_ref[:, O_GK:O_GV] = (gkf[...].astype(F32) + gkb[...].astype(F32)).astype(_BF)
        dp_ref[:, O_GV:O_GG] = (gvf[...].astype(F32) + gvb[...].astype(F32)).astype(_BF)
        dp_ref[:, O_GG:O_Z] = (dgg_ref[...].astype(F32) * lat).astype(_BF)
        dlg = jnp.concatenate([dlf[...], dlb[...]], axis=1)
        dp_ref[:, O_Z:NP] = _nt(dlg, wg_ref[...]).astype(_BF)
        dh = _nn(dp_ref[...], w_ref[...])
        x = jnp.where(is_ctx, c_ref[...], x_ref[...])
        r = lax.rsqrt(jnp.mean(x * x, axis=-1, keepdims=True) + EPS)
        xh = x * r
        sdh = jnp.sum(dh, axis=0, keepdims=True)
        sdx = jnp.sum(dh * xh, axis=0, keepdims=True)
        sums_ref[0:1, :] += sdh * lat
        sums_ref[1:2, :] += sdx * lat
        sums_ref[2:3, :] += sdh * (1.0 - lat)
        sums_ref[3:4, :] += sdx * (1.0 - lat)
        dxh = dh * jnp.where(is_ctx, gmc_ref[...], gml_ref[...])
        gx_ref[...] = dx1_ref[...] + r * (dxh - xh * jnp.mean(dxh * xh, axis=-1, keepdims=True))

    vec = _full((1, D))
    tab = _rows(TE, 128)
    return pl.pallas_call(
        body, name="inproj_bwd", grid=(E // TE,),
        in_specs=[_rows_lat(TE, D), _full((CTX, D)), vec, vec, _full((NP, D)), _full((128, 512)), tab, tab, tab,
                  _rows_lat(TE, QP), pl.BlockSpec((KP, TE), lambda i: (0, i)), pl.BlockSpec((KP, TE), lambda i: (0, i)),
                  _rows(TE, 256), _rows(TE, 256), _rows(TE, 256), _rows(TE, 256), _rows(TE, 512), _rows(TE, 512),
                  _rows_lat(TE, 512), _rows(TE, 256), _rows(TE, 256), _rows_lat(TE, D)],
        out_specs=[_rows(TE, NP), _rows_lat(TE, D), _full((8, D))],
        out_shape=[jax.ShapeDtypeStruct((E, NP), _BF), jax.ShapeDtypeStruct((S, D), F32),
                   jax.ShapeDtypeStruct((8, D), F32)],
        compiler_params=_cp(("arbitrary",), VMEM_BIG),
    )(x, ctx, gml, gmc, win, wg, cos, sa, sb, dq, dk, dv, *dgq, *dgk, *dgv, dgg, dlg_f, dlg_b, dx1)


def _matmul_tn(a, b, tk, tt, name, out_dtype, transpose_out=False, a_cols=None, hosted=(), gather=True):
    T, KA = a.shape
    N = b.shape[1]
    nt = T // tt
    k0 = 0
    if a_cols is not None:
        KA, k0 = tk, a_cols

    def body(a_ref, b_ref, o_ref, acc):
        t = pl.program_id(1)

        @pl.when(t == 0)
        def _():
            acc[...] = jnp.zeros_like(acc)

        acc[...] += _tn(a_ref[...], b_ref[...])

        @pl.when(t == nt - 1)
        def _():
            o_ref[...] = (acc[...].T if transpose_out else acc[...]).astype(out_dtype)

    if transpose_out:
        out_spec, out_shape = pl.BlockSpec((N, tk), lambda i, t: (0, i)), (N, KA)
    else:
        out_spec, out_shape = pl.BlockSpec((tk, N), lambda i, t: (i, 0)), (KA, N)
    res = _hosted_call(
        body, (a, b), hosted, gather, name=name, grid=(KA // tk, nt),
        in_specs=[pl.BlockSpec((tt, tk), lambda i, t: (t, i + k0)), pl.BlockSpec((tt, N), lambda i, t: (t, 0))],
        out_specs=[out_spec], out_shape=[jax.ShapeDtypeStruct(out_shape, out_dtype)],
        scratch_shapes=[pltpu.VMEM((tk, N), F32)],
        compiler_params=_cp(("arbitrary", "arbitrary"), VMEM_BIG))
    return res if hosted else res[0]


def _ada_bwd(c_all, c_ctx, w_ada, d_all):
    n = w_ada.shape[1]

    def body(c_ref, cc_ref, w_ref, d_ref, gw_ref, t_ref):
        c = jnp.concatenate([c_ref[...], jnp.broadcast_to(cc_ref[...], (8, D))], axis=0)
        db = d_ref[...].astype(_BF)
        gw_ref[0] = _tn((c * _sigmoid(c)).astype(_BF), db)
        t_ref[...] = _nt(db[8:16], w_ref[...].astype(_BF))

    return pl.pallas_call(
        body, name="ada_bwd", in_specs=[_full((8, D)), _full((1, D)), _full((D, n)), _full((16, n))],
        out_specs=[_full((1, D, n)), _full((8, D))],
        out_shape=[jax.ShapeDtypeStruct((1, D, n), F32), jax.ShapeDtypeStruct((8, D), F32)], grid=(1,),
        compiler_params=_cp(("arbitrary",)),
    )(c_all, c_ctx, w_ada, d_all)


PART_ROWS = 56
R_ADA, R_ADA_C, R_GAIN, R_SINK, R_BG, R_GGLA, R_LOSS, R_WG = 0, 6, 12, 16, 17, 18, 19, 24


def _small_grads(s_in, s_ffn, s_mix, ada_l, ada_c, gains, dsink, s_bg, g_wg, loss):
    def body(si, sf, sm, al, ac, g, ds, sbg, gwg, loss_ref, o_ref):
        o_ref[...] = jnp.zeros_like(o_ref)
        o_ref[R_LOSS:R_LOSS + 1, 0:128] = loss_ref[0:1, :]
        sub = lax.broadcasted_iota(jnp.int32, (8, 128), 0)
        lane = lax.broadcasted_iota(jnp.int32, (8, 128), 1)
        o_ref[R_SINK:R_SINK + 1, 0:128] = jnp.sum(jnp.where(sub == lane, ds[...], 0.0), axis=0, keepdims=True)
        o_ref[R_BG:R_BG + 1, 0:512] = sbg[0:1, :]
        y = sm[1:2, 0:128] + sm[1:2, 128:256] + sm[1:2, 256:384] + sm[1:2, 384:512]
        y = y + pltpu.roll(y, 64, 1)
        o_ref[R_GGLA:R_GGLA + 1, 0:128] = jnp.where(lane[0:1] < DV, y, 0.0)
        o_ref[R_WG:R_WG + 16, 0:256] = gwg[0:16, 0:256]
        o_ref[R_WG + 16:R_WG + 32, 0:256] = gwg[16:32, 256:512]
        sdh_l, sdx_l, sdh_c, sdx_c = si[0:1], si[1:2], si[2:3], si[3:4]
        sdh2, sdx2, a2 = sf[0:1], sf[1:2], sf[2:3]
        a1 = sm[0:1]
        g1, g2, g3, g4 = g[0:1], g[1:2], g[2:3], g[3:4]
        sc1, gt1, sc2, gt2 = al[1:2], al[2:3], al[4:5], al[5:6]
        sc1c = ac[1:2]
        z = jnp.zeros((1, D), F32)
        rows = [sdh_l, sdx_l * g1, a1 * g2, sdh2, sdx2 * g3, a2 * g4,
                sdh_c, sdx_c * g1, z, z, z, z,
                sdx_l * (1.0 + sc1) + sdx_c * (1.0 + sc1c), a1 * gt1, sdx2 * (1.0 + sc2), a2 * gt2]
        for r, v in enumerate(rows):
            o_ref[r:r + 1, :] = v

    v8 = _full((8, D))
    return pl.pallas_call(
        body, name="small_grads",
        in_specs=[v8] * 6 + [_full((8, 128)), _full((8, 512)), _full((128, 512)), _full((8, 128))],
        out_specs=_full((PART_ROWS, D)), out_shape=jax.ShapeDtypeStruct((PART_ROWS, D), F32), grid=(1,),
        compiler_params=_cp(("arbitrary",)),
    )(s_in, s_ffn, s_mix, ada_l, ada_c, gains, dsink, s_bg, g_wg, loss)


def _row_tile(R):
    for cand in (256, 128, 64, 32, 16):
        if R % cand == 0 and R > cand:
            return cand
    return R


def _adamw(w, gs, m, v, name):
    _, R, C = w.shape
    tr = _row_tile(R)
    c1 = 1.0 / (1.0 - ADAM_B1 ** ADAM_STEP)
    c2 = 1.0 / (1.0 - ADAM_B2 ** ADAM_STEP)
    ng = len(gs)

    def body(w_ref, *refs):
        g_refs, (m_ref, v_ref, go_ref, d_ref, nm_ref, nv_ref) = refs[:ng], refs[ng:]
        c0 = 0
        for g_ref in g_refs:
            cols = slice(c0, c0 + g_ref.shape[2])
            c0 += g_ref.shape[2]
            gg = g_ref[0].astype(F32)
            for j in range(1, g_ref.shape[0]):
                gg = gg + g_ref[j].astype(F32)
            go_ref[0, :, cols] = gg
            nm = ADAM_B1 * m_ref[0, :, cols] + (1.0 - ADAM_B1) * gg
            nv = ADAM_B2 * v_ref[0, :, cols] + (1.0 - ADAM_B2) * (gg * gg)
            nm_ref[0, :, cols] = nm
            nv_ref[0, :, cols] = nv
            d_ref[0, :, cols] = -ADAM_LR * ((nm * c1) / (jnp.sqrt(nv * c2) + ADAM_EPS) + ADAM_WD * w_ref[0, :, cols])

    spec = pl.BlockSpec((1, tr, C), lambda i: (0, i, 0))
    sds = jax.ShapeDtypeStruct((1, R, C), F32)
    g_specs = [pl.BlockSpec((g.shape[0], tr, g.shape[2]), lambda i: (0, i, 0)) for g in gs]
    return pl.pallas_call(
        body, name=name, grid=(R // tr,), in_specs=[spec] + g_specs + [spec, spec], out_specs=[spec] * 4,
        out_shape=[sds] * 4, compiler_params=_cp(("parallel",), 48 * 1024 * 1024),
    )(w, *gs, m, v)


def _sum_slots(slots, name):
    _, R, C = slots.shape
    tr = _row_tile(R)

    def body(s_ref, o_ref):
        acc = s_ref[0].astype(F32)
        for j in range(1, N_DEV):
            acc = acc + s_ref[j].astype(F32)
        o_ref[...] = acc

    return pl.pallas_call(
        body, name=name, grid=(R // tr,), in_specs=[pl.BlockSpec((N_DEV, tr, C), lambda i: (0, i, 0))],
        out_specs=_rows(tr, C), out_shape=jax.ShapeDtypeStruct((R, C), F32), compiler_params=_cp(("parallel",)),
    )(slots)


def _ag2_start(x_ref, out_ref, send_sems, recv_sems, local_sem):
    x, y, c = lax.axis_index("x"), lax.axis_index("y"), lax.axis_index("c")
    me, sibling = (x, y, c), (x, y, 1 - c)
    chips = [(1 - x, y), (x, 1 - y), (1 - x, 1 - y)]

    def rows(px, py, pc):
        return out_ref.at[4 * px + 2 * py + pc]

    def copy(k, block, to, src=None):
        return pltpu.make_async_remote_copy(
            src_ref=rows(*block) if src is None else src, dst_ref=rows(*block),
            send_sem=send_sems.at[k], recv_sem=recv_sems.at[k], device_id=to, device_id_type=MESH)

    mine = pltpu.make_async_copy(x_ref, rows(*me), local_sem)
    mine.start()
    first = [copy(0, me, sibling, src=x_ref)]
    first += [copy(1 + j, me, (*chip, c), src=x_ref) for j, chip in enumerate(chips)]
    for cp in first:
        cp.start()
    return copy, mine, first, me, sibling, chips, c


def _ag2_finish(state):
    copy, mine, first, me, sibling, chips, c = state
    passed = [copy(4 + j, (*chip, c), sibling) for j, chip in enumerate(chips)]
    for j, chip in enumerate(chips):
        copy(1 + j, (*chip, c), me).wait_recv()
        passed[j].start()
    copy(0, sibling, me).wait_recv()
    for j, chip in enumerate(chips):
        copy(4 + j, (*chip, 1 - c), me).wait_recv()
    for cp in first + passed:
        cp.wait_send()
    mine.wait()


def _exchange(arrays, name, gather):
    na = len(arrays)

    def body(*refs):
        cps = _xchg_copies(refs[:na], refs[na:2 * na], *refs[2 * na:], gather=gather)
        _xchg_start(cps)
        _xchg_finish(cps)

    anyspec = pl.BlockSpec(memory_space=pl.ANY)
    return pl.pallas_call(
        body, name=name, out_shape=_xchg_out_shapes(arrays, gather), in_specs=[anyspec] * na,
        out_specs=[anyspec] * na, scratch_shapes=_xchg_scratch(na),
    )(*arrays)


def _entry(c, wg_sh, win_sh, c_ctx, w_ada):
    n = w_ada.shape[1]

    def body(c_ref, g_ref, w_ref, cc_ref, wa_ref, call_ref, gall_ref, wall_ref, ada_ref, part,
             s_send, s_recv, s_loc, w_send, w_recv, w_loc, a_send, a_recv, a_loc):
        big = _ag2_start(w_ref, wall_ref, w_send, w_recv, w_loc)
        small = _xchg_copies([c_ref, g_ref], [call_ref, gall_ref], s_send, s_recv, s_loc, gather=True)
        _xchg_start(small)
        _xchg_finish(small)
        cs = jnp.concatenate([call_ref[:, 0, :], jnp.broadcast_to(cc_ref[...], (8, D))], axis=0)
        part[...] = _nn((cs * _sigmoid(cs)).astype(_BF), wa_ref[...].astype(_BF))
        ada = _xchg_copies([part], [ada_ref], a_send, a_recv, a_loc, gather=True)
        _xchg_start(ada)
        _xchg_finish(ada)
        _ag2_finish(big)

    vm = pl.BlockSpec(memory_space=pltpu.VMEM)
    return pl.pallas_call(
        body, name="entry_gather",
        out_shape=[jax.ShapeDtypeStruct((N_DEV,) + c.shape, F32), jax.ShapeDtypeStruct((N_DEV,) + wg_sh.shape, F32),
                   jax.ShapeDtypeStruct((N_DEV,) + win_sh.shape, win_sh.dtype),
                   jax.ShapeDtypeStruct((N_DEV, 16, n), F32)],
        in_specs=[vm] * 5, out_specs=[vm] * 4,
        scratch_shapes=[pltpu.VMEM((16, n), F32)] + _xchg_scratch(2)
        + [pltpu.SemaphoreType.DMA((7,)), pltpu.SemaphoreType.DMA((7,)), pltpu.SemaphoreType.DMA] + _xchg_scratch(1),
        compiler_params=pltpu.CompilerParams(vmem_limit_bytes=VMEM_BIG),
    )(c, wg_sh, win_sh, c_ctx, w_ada)


def _rope_tables(S):
    t = np.arange(S)
    row = (t // GRID_W).astype(np.float32)
    colp = (t % GRID_W).astype(np.float32)
    half = HD // 2
    inv = (ROPE_BASE ** (-np.arange(0, half, 2, dtype=np.float32) / half)).astype(np.float32)
    ar = row[:, None] * inv[None, :]
    ac = colp[:, None] * inv[None, :]
    ang = np.concatenate([ar, ar, ac, ac], axis=-1).astype(np.float32)
    cos = np.cos(ang).astype(np.float32)
    sin = np.sin(ang).astype(np.float32)
    lane = np.arange(HD)
    first = (lane % 32) < 16
    sa = np.where(first[None, :], -sin, 0.0)
    sb = np.where(first[None, :], 0.0, sin)

    def ext(tab, ctx_val):
        full = np.zeros((CTX + S, 128), np.float32)
        full[:CTX, :] = ctx_val
        full[CTX:, :HD] = tab
        full[CTX:, HD:] = tab
        return jnp.asarray(full)

    return ext(cos, 1.0), ext(sa, 0.0), ext(sb, 0.0)


def _pad_rows_win(wt):
    return jnp.pad(wt, ((0, NP - IN_COLS), (0, 0)))


def _unpad_rows_win(g):
    return g[0:IN_COLS]


def _local_step(x, ctx, target, ada_l, ada_c, gains, sink, win_p, wg_bd, bg, ggla, wout_sh, wffi_sh, wffo_sh):
    S = x.shape[0]
    cos, sa, sb = _rope_tables(S)
    g1, g2, g3, g4 = (gains[i:i + 1] for i in range(4))
    sh1, sc1, gt1, sh2, sc2, gt2 = (ada_l[i:i + 1] for i in range(6))
    sh1c, sc1c = ada_c[0:1], ada_c[1:2]
    gml, gmc, gm2 = g1 * (1.0 + sc1), g1 * (1.0 + sc1c), g3 * (1.0 + sc2)
    mavg = jnp.asarray(np.kron(np.eye(N_GLA, dtype=np.float32), np.full((DV, DV), 1.0 / DV, np.float32))).astype(_BF)

    n_ffi, r_ffo, r_out = wffi_sh.shape[0], wffo_sh.shape[0], wout_sh.shape[0]
    tt_e = 768 if (S + CTX) % 768 == 0 else 256
    tt_s = 512 if S % 512 == 0 else 256
    h, q, k, v, gq, gk, gv, gg, z, la, wout_g = _inproj_fwd(x, ctx, gml, sh1, gmc, sh1c, win_p, wg_bd, bg,
                                                            cos, sa, sb, [wout_sh])
    attn, lse, probs, wffi_g = _attn_fwd(q, k, v, sink, [wffi_sh])
    o_f, st_f, o_b, st_b, wffo_g = _gla_fwd(gq, gk, gv, la, [wffo_sh])
    wout = wout_g.reshape(N_DEV * r_out, D)
    wffi = wffi_g.reshape(N_DEV * n_ffi, D)
    wffo = wffo_g.reshape(N_DEV * r_ffo, D)
    x1, mix = _mix_fwd(x, attn, o_f, o_b, gg, ggla, mavg, wout, gt1, g2)
    dx1, h2, du, act, df, s_ffn, loss = _ffn(x1, target, gm2, sh2, gt2, g4, wffi, wffo)
    slab_ffi = _matmul_tn(h2, du, 512, tt_s, "grad_w_ffn_in", _BF, True).reshape(N_DEV, n_ffi, D)
    tt_l = 1024 if S % 1024 == 0 else tt_s
    slab_ffo = _matmul_tn(act, df, FFN, tt_l, "grad_w_ffn_out", _BF).reshape(N_DEV, r_ffo, D)
    d_attn, do_gla, dgg, dy, s_mix, got_ffo = _mix_bwd(dx1, mix, o_f, o_b, gg, ggla, mavg, wout, gt1, g2, [slab_ffo])
    slab_out = _matmul_tn(mix, dy, D, tt_l, "grad_w_out", _BF).reshape(N_DEV, r_out, D)
    dq, dk, dv, dsink, got_ffi = _attn_bwd(q, k, v, sink, probs, lse, d_attn, [slab_ffi])
    (dgq_f, dgk_f, dgv_f, dlg_f, dgq_b, dgk_b, dgv_b, dlg_b, g_wg, s_bg,
     got_out) = _gla_bwd(gq, gk, gv, la, z, st_f, st_b, do_gla, [slab_out])
    dp, grad_x, s_in = _inproj_bwd(x, ctx, gml, gmc, win_p, wg_bd, cos, sa, sb, dq, dk, dv,
                                   (dgq_f, dgq_b), (dgk_f, dgk_b), (dgv_f, dgv_b), dgg, dlg_f, dlg_b, dx1)
    small = _small_grads(s_in, s_ffn, s_mix, ada_l, ada_c, gains, dsink, s_bg, g_wg, loss)
    n_in, n_grp = IN_COLS // N_DEV, 2
    got_in, slab = [], None
    for j in range(n_grp):
        g_j, got = _matmul_tn(h, dp, D // n_grp, tt_e, "grad_w_in_%d" % j, _BF, True, a_cols=j,
                              hosted=[small] if j == 0 else [slab], gather=(j == 0))
        if j == 0:
            parts = got
        else:
            got_in.append(got)
        slab = _unpad_rows_win(g_j).reshape(N_DEV, n_in, D // n_grp)
    got_in.append(_exchange([slab], "scatter_grads", False)[0])
    return dict(grad_x=grad_x, got_in=got_in, got_out=got_out, got_ffi=got_ffi, got_ffo=got_ffo, parts=parts)


SMALL_NAMES = ["c_ctx", "b_ada", "g_pre_mix", "g_post_mix", "g_pre_ffn", "g_post_ffn", "attn_sink",
               "b_gate_fwd", "b_gate_bwd", "g_gla_norm", "w_gate_fwd", "w_gate_bwd"]


def _small_update(tot, t_tot, wg_g, w, m, v):
    c1 = 1.0 / (1.0 - ADAM_B1 ** ADAM_STEP)
    c2 = 1.0 / (1.0 - ADAM_B2 ** ADAM_STEP)
    n = len(SMALL_NAMES)

    def body(tot_ref, t_ref, wg_ref, *refs):
        w_r, m_r, v_r = refs[0:n], refs[n:2 * n], refs[2 * n:3 * n]
        g_o, d_o, nm_o, nv_o = refs[3 * n:4 * n], refs[4 * n:5 * n], refs[5 * n:6 * n], refs[6 * n:7 * n]

        def upd(i, idx, g):
            nm = ADAM_B1 * m_r[i][idx] + (1.0 - ADAM_B1) * g
            nv = ADAM_B2 * v_r[i][idx] + (1.0 - ADAM_B2) * (g * g)
            g_o[i][idx] = g
            nm_o[i][idx] = nm
            nv_o[i][idx] = nv
            d_o[i][idx] = -ADAM_LR * ((nm * c1) / (jnp.sqrt(nv * c2) + ADAM_EPS) + ADAM_WD * w_r[i][idx])

        everything = (slice(None), slice(None))
        cc = w_r[0][...]
        sc = _sigmoid(cc)
        upd(0, everything, t_ref[0:1, :] * (sc * (1.0 + cc * (1.0 - sc))))
        for j in range(6):
            upd(1, (slice(None), slice(D * j, D * j + D)),
                tot_ref[R_ADA + j:R_ADA + j + 1, :] + tot_ref[R_ADA_C + j:R_ADA_C + j + 1, :])
        for j in range(4):
            upd(2 + j, everything, tot_ref[R_GAIN + j:R_GAIN + j + 1, :])
        upd(6, everything, tot_ref[R_SINK:R_SINK + 1, 0:N_ATT])
        upd(7, everything, tot_ref[R_BG:R_BG + 1, 0:256])
        upd(8, everything, tot_ref[R_BG:R_BG + 1, 256:512])
        upd(9, everything, tot_ref[R_GGLA:R_GGLA + 1, 0:DV])
        upd(10, (0,), wg_ref[0:GATE_RANK, :])
        upd(11, (0,), wg_ref[GATE_RANK:2 * GATE_RANK, :])

    params = [w[k] for k in SMALL_NAMES] + [m[k] for k in SMALL_NAMES] + [v[k] for k in SMALL_NAMES]
    outs = pl.pallas_call(
        body, name="small_update", grid=(1,),
        in_specs=[_full(tot.shape), _full(t_tot.shape), _full(wg_g.shape)] + [_full(p.shape) for p in params],
        out_specs=[_full(w[k].shape) for k in SMALL_NAMES] * 4,
        out_shape=[jax.ShapeDtypeStruct(w[k].shape, F32) for k in SMALL_NAMES] * 4,
        compiler_params=_cp(("arbitrary",)),
    )(tot, t_tot, wg_g, *params)
    return tuple(dict(zip(SMALL_NAMES, outs[i * n:(i + 1) * n])) for i in range(4))


def kernel(x, c, ctx, c_ctx, w_ada, b_ada, g_pre_mix, g_post_mix, g_pre_ffn, g_post_ffn, w_in, attn_sink, w_gate_fwd, b_gate_fwd, w_gate_bwd, b_gate_bwd, g_gla_norm, w_out, w_ffn_in, w_ffn_out, loss_target, m_c_ctx, m_w_ada, m_b_ada, m_g_pre_mix, m_g_post_mix, m_g_pre_ffn, m_g_post_ffn, m_w_in, m_attn_sink, m_w_gate_fwd, m_b_gate_fwd, m_w_gate_bwd, m_b_gate_bwd, m_g_gla_norm, m_w_out, m_w_ffn_in, m_w_ffn_out, v_c_ctx, v_w_ada, v_b_ada, v_g_pre_mix, v_g_post_mix, v_g_pre_ffn, v_g_post_ffn, v_w_in, v_attn_sink, v_w_gate_fwd, v_b_gate_fwd, v_w_gate_bwd, v_b_gate_bwd, v_g_gla_norm, v_w_out, v_w_ffn_in, v_w_ffn_out):
    me = 4 * lax.axis_index("x") + 2 * lax.axis_index("y") + lax.axis_index("c")
    S = x.shape[1]
    n_in = w_in.shape[2]
    n_ffi = w_ffn_in.shape[2]
    r_out = w_out.shape[1]
    r_ffo = w_ffn_out.shape[1]
    n_ada = w_ada.shape[2]

    wg_sh = jnp.concatenate([w_gate_fwd.reshape(4, 128), w_gate_bwd.reshape(4, 128)], axis=0)
    c_all3, g_all, w_all, ada_all = _entry(c, wg_sh, w_in[0].T.astype(_BF), c_ctx.reshape(1, D), w_ada[0])
    c_all = c_all3.reshape(N_DEV, D)
    wgf = g_all[:, 0:4].reshape(N_DEV, GATE_RANK, 32).transpose(1, 0, 2).reshape(GATE_RANK, 256)
    wgb = g_all[:, 4:8].reshape(N_DEV, GATE_RANK, 32).transpose(1, 0, 2).reshape(GATE_RANK, 256)
    win_p = _pad_rows_win(w_all.reshape(N_DEV * n_in, D))
    wg_bd = jnp.zeros((128, 512), F32).at[0:16, 0:256].set(wgf).at[16:32, 256:512].set(wgb).astype(_BF)
    ada_full = ada_all.transpose(1, 0, 2).reshape(16, N_DEV * n_ada) + b_ada
    ada_l = jnp.pad(lax.dynamic_slice_in_dim(ada_full, me, 1, 0).reshape(6, D), ((0, 2), (0, 0)))
    ada_c = jnp.pad(ada_full[8].reshape(6, D), ((0, 2), (0, 0)))
    gains = jnp.pad(jnp.concatenate([g_pre_mix, g_post_mix, g_pre_ffn, g_post_ffn], axis=0), ((0, 4), (0, 0)))
    sink = jnp.broadcast_to(attn_sink.reshape(8, 1), (8, 128))
    bg = jnp.concatenate([b_gate_fwd, b_gate_bwd], axis=1)
    ggla = jnp.tile(g_gla_norm, (1, N_GLA))

    r = _local_step(x[0], ctx[0], loss_target[0], ada_l, ada_c, gains, sink, win_p, wg_bd, bg, ggla,
                    w_out[0].astype(_BF), w_ffn_in[0].T.astype(_BF), w_ffn_out[0].astype(_BF))

    parts = r["parts"]
    tot = _sum_slots(parts, "sum_small_grads")
    loss = tot[R_LOSS, 0]
    d_ada_rows = parts[:, R_ADA:R_ADA + 6].reshape(N_DEV, 6 * D)
    d_ada_c = tot[R_ADA_C:R_ADA_C + 6].reshape(1, 6 * D)
    my_cols = lax.dynamic_slice_in_dim(jnp.concatenate([d_ada_rows, jnp.broadcast_to(d_ada_c, (1, 6 * D)),
                                                        jnp.zeros((7, 6 * D), F32)], axis=0), me * n_ada, n_ada, 1)
    grad_w_ada, t_part = _ada_bwd(c_all, c_ctx.reshape(1, D), w_ada[0], my_cols)
    wg_g = lax.dynamic_slice(tot, (R_WG, me * 32), (2 * GATE_RANK, 32))

    tr = lambda a: jnp.transpose(a, (0, 2, 1))
    big = {}
    t_all, = _exchange([t_part], "gather_c_ctx", True)
    t_tot = _sum_slots(t_all, "sum_c_ctx")
    for nm, w, g, m, v in [("w_ada", w_ada, grad_w_ada, m_w_ada, v_w_ada),
                           ("w_out", w_out, r["got_out"], m_w_out, v_w_out),
                           ("w_ffn_out", w_ffn_out, r["got_ffo"], m_w_ffn_out, v_w_ffn_out)]:
        big[nm] = _adamw(w, [g], m, v, "adamw_" + nm)
    big["w_ffn_in"] = tuple(tr(o) for o in _adamw(tr(w_ffn_in), [r["got_ffi"]], tr(m_w_ffn_in), tr(v_w_ffn_in),
                                                  "adamw_w_ffn_in"))
    big["w_in"] = tuple(tr(o) for o in _adamw(tr(w_in), r["got_in"], tr(m_w_in), tr(v_w_in), "adamw_w_in"))

    w_small = dict(c_ctx=c_ctx.reshape(1, D), b_ada=b_ada, g_pre_mix=g_pre_mix, g_post_mix=g_post_mix, g_pre_ffn=g_pre_ffn,
                   g_post_ffn=g_post_ffn, attn_sink=attn_sink, b_gate_fwd=b_gate_fwd, b_gate_bwd=b_gate_bwd,
                   g_gla_norm=g_gla_norm, w_gate_fwd=w_gate_fwd, w_gate_bwd=w_gate_bwd)
    m_small = dict(c_ctx=m_c_ctx.reshape(1, D), b_ada=m_b_ada, g_pre_mix=m_g_pre_mix, g_post_mix=m_g_post_mix,
                   g_pre_ffn=m_g_pre_ffn, g_post_ffn=m_g_post_ffn, attn_sink=m_attn_sink, b_gate_fwd=m_b_gate_fwd,
                   b_gate_bwd=m_b_gate_bwd, g_gla_norm=m_g_gla_norm, w_gate_fwd=m_w_gate_fwd, w_gate_bwd=m_w_gate_bwd)
    v_small = dict(c_ctx=v_c_ctx.reshape(1, D), b_ada=v_b_ada, g_pre_mix=v_g_pre_mix, g_post_mix=v_g_post_mix,
                   g_pre_ffn=v_g_pre_ffn, g_post_ffn=v_g_post_ffn, attn_sink=v_attn_sink, b_gate_fwd=v_b_gate_fwd,
                   b_gate_bwd=v_b_gate_bwd, g_gla_norm=v_g_gla_norm, w_gate_fwd=v_w_gate_fwd, w_gate_bwd=v_w_gate_bwd)
    grads_small, d_s, nm_s, nv_s = _small_update(tot, t_tot, wg_g, w_small, m_small, v_small)
    for dd in (grads_small, d_s, nm_s, nv_s):
        dd["c_ctx"] = dd["c_ctx"].reshape(D)

    order = ["c_ctx", "w_ada", "b_ada", "g_pre_mix", "g_post_mix", "g_pre_ffn", "g_post_ffn", "w_in", "attn_sink",
             "w_gate_fwd", "b_gate_fwd", "w_gate_bwd", "b_gate_bwd", "g_gla_norm", "w_out", "w_ffn_in", "w_ffn_out"]
    grads, deltas, new_m, new_v = [], [], [], []
    for nm in order:
        if nm in big:
            g_, d_, m_, v_ = big[nm]
        else:
            g_, d_, m_, v_ = grads_small[nm], d_s[nm], nm_s[nm], nv_s[nm]
        grads.append(g_)
        deltas.append(d_)
        new_m.append(m_)
        new_v.append(v_)
    return (loss, r["grad_x"][None], *grads, *deltas, *new_m, *new_v)
```

```python
import functools
import math

import numpy as np
import jax
import jax.numpy as jnp
from jax import lax
from jax.experimental import pallas as pl
from jax.experimental.pallas import tpu as pltpu

F32 = jnp.float32
_BF = jnp.bfloat16

N_DEV = 8
D = 1024
CTX = 256
HD = 64
N_ATT = 8
N_KV = 2
GRP = N_ATT // N_KV
WIN = 128
GRID_W = 64
ROPE_BASE = 10000.0
N_GLA = 8
DK = 32
DV = 64
GATE_RANK = 16
GATE_TAU = 16.0
FFN = 2816
EPS = 1e-6
NEG = -1e30
GLA_T = 128

QP = N_ATT * HD
KP = N_KV * HD
O_Q, O_K, O_V = 0, QP, QP + KP
O_GQ = O_V + KP
O_GK = O_GQ + N_GLA * DK
O_GV = O_GK + N_GLA * DK
O_GG = O_GV + N_GLA * DV
O_Z = O_GG + N_GLA * DV
NP = O_Z + 128
IN_COLS = 2336

ADAM_LR, ADAM_B1, ADAM_B2, ADAM_EPS, ADAM_WD, ADAM_STEP = 0.001, 0.9, 0.999, 1e-08, 0.01, 10

VMEM_BIG = 56 * 1024 * 1024
MESH = pl.DeviceIdType.MESH


def _cp(sem, vmem=None):
    return pltpu.CompilerParams(dimension_semantics=sem, vmem_limit_bytes=vmem)


def _full(shape):
    nd = len(shape)
    return pl.BlockSpec(shape, lambda *a: (0,) * nd)


def _rows(tile, width, off=0):
    return pl.BlockSpec((tile, width), lambda i: (i + off, 0))


def _rows_lat(tile, width):
    return pl.BlockSpec((tile, width), lambda i: (jnp.maximum(i - 1, 0), 0))


def _nt(a, b):
    return lax.dot_general(a, b, (((1,), (1,)), ((), ())), preferred_element_type=F32)


def _tn(a, b):
    return lax.dot_general(a, b, (((0,), (0,)), ((), ())), preferred_element_type=F32)


def _nn(a, b):
    return jnp.dot(a, b, preferred_element_type=F32)


def _head_mean(x, mavg):
    n = x.shape[0]
    hi = x.astype(_BF)
    lo = (x - hi.astype(F32)).astype(_BF)
    y = _nn(jnp.concatenate([hi, lo], axis=0), mavg)
    return y[0:n] + y[n:2 * n]


def _rope(t, cos, sa, sb):
    n = t.shape[1]
    reps = n // 128
    c = jnp.tile(cos, (1, reps))
    a = jnp.tile(sa, (1, reps))
    b = jnp.tile(sb, (1, reps))
    return t * c + pltpu.roll(t, n - 16, 1) * a + pltpu.roll(t, 16, 1) * b


def _unrope(t, cos, sa, sb):
    n = t.shape[1]
    reps = n // 128
    c = jnp.tile(cos, (1, reps))
    a = jnp.tile(sa, (1, reps))
    b = jnp.tile(sb, (1, reps))
    return t * c + pltpu.roll(t * a, 16, 1) + pltpu.roll(t * b, n - 16, 1)


def _sigmoid(x):
    return 1.0 / (1.0 + jnp.exp(-x))


def _inproj_fwd(x, ctx, gml, shl, gmc, shc, win, wg, bg, cos, sa, sb, shards):
    E = x.shape[0] + CTX
    TE = CTX

    def body(x_ref, c_ref, gml_ref, shl_ref, gmc_ref, shc_ref, w_ref, wg_ref, bg_ref, cos_ref, sa_ref, sb_ref,
             h_ref, q_ref, k_ref, v_ref, gq_ref, gk_ref, gv_ref, gg_ref, z_ref, la_ref):
        is_ctx = pl.program_id(0) == 0
        gm = jnp.where(is_ctx, gmc_ref[...], gml_ref[...])
        sh = jnp.where(is_ctx, shc_ref[...], shl_ref[...])
        x = jnp.where(is_ctx, c_ref[...], x_ref[...])
        r = lax.rsqrt(jnp.mean(x * x, axis=-1, keepdims=True) + EPS)
        hb = ((x * r) * gm + sh).astype(_BF)
        h_ref[...] = hb
        p = _nt(hb, w_ref[...])
        cos_t, sa_t, sb_t = cos_ref[...], sa_ref[...], sb_ref[...]
        q_ref[...] = (_rope(p[:, O_Q:O_K], cos_t, sa_t, sb_t) * (HD ** -0.5)).astype(_BF)
        k_ref[...] = _rope(p[:, O_K:O_V], cos_t, sa_t, sb_t).astype(_BF)
        v_ref[...] = p[:, O_V:O_GQ].astype(_BF)
        gq_ref[...] = p[:, O_GQ:O_GK] * (DK ** -0.5)
        gk_ref[...] = p[:, O_GK:O_GV]
        gv_ref[...] = p[:, O_GV:O_GG]
        gg_ref[...] = p[:, O_GG:O_Z]
        zb = p[:, O_Z:NP].astype(_BF)
        z_ref[...] = zb
        lg = _nn(zb, wg_ref[...]) + bg_ref[...]
        la_ref[...] = (jnp.minimum(lg, 0.0) - jnp.log(1.0 + jnp.exp(-jnp.abs(lg)))) * (1.0 / GATE_TAU)

    vec = _full((1, D))
    tab = _rows(TE, 128)
    outs = [(D, _BF), (QP, _BF), (KP, _BF), (KP, _BF), (256, F32), (256, F32), (512, F32), (512, F32),
            (128, _BF), (512, F32)]
    return _hosted_call(
        body, (x, ctx, gml, shl, gmc, shc, win, wg, bg, cos, sa, sb), shards, True,
        name="inproj_fwd", grid=(E // TE,),
        in_specs=[_rows_lat(TE, D), _full((CTX, D)), vec, vec, vec, vec, _full((NP, D)), _full((128, 512)),
                  _full((1, 512)), tab, tab, tab],
        out_specs=[_rows(TE, w) for w, _ in outs],
        out_shape=[jax.ShapeDtypeStruct((E, w), dt) for w, dt in outs],
        compiler_params=_cp(("arbitrary",), 40 * 1024 * 1024))


def _xchg_scratch(na):
    return [pltpu.SemaphoreType.DMA((na, N_DEV - 1)), pltpu.SemaphoreType.DMA((na, N_DEV - 1)),
            pltpu.SemaphoreType.DMA((na,))]


def _xchg_copies(ins, outs, send_sems, recv_sems, local_sems, gather):
    x, y, c = lax.axis_index("x"), lax.axis_index("y"), lax.axis_index("c")
    me = 4 * x + 2 * y + c
    local, sends, recvs = [], [], []
    for a in range(len(ins)):
        local.append(pltpu.make_async_copy(ins[a] if gather else ins[a].at[me], outs[a].at[me], local_sems.at[a]))
    for k in range(1, N_DEV):
        px, py, pc = x ^ (k >> 2), y ^ ((k >> 1) & 1), c ^ (k & 1)
        peer = 4 * px + 2 * py + pc
        for a in range(len(ins)):
            sems = dict(send_sem=send_sems.at[a, k - 1], recv_sem=recv_sems.at[a, k - 1], device_id_type=MESH)
            sends.append(pltpu.make_async_remote_copy(
                src_ref=ins[a] if gather else ins[a].at[peer], dst_ref=outs[a].at[me], device_id=(px, py, pc), **sems))
            recvs.append(pltpu.make_async_remote_copy(
                src_ref=ins[a] if gather else ins[a].at[me], dst_ref=outs[a].at[peer], device_id=(x, y, c), **sems))
    return local, sends, recvs


def _xchg_start(cps):
    local, sends, _ = cps
    for cp in local + sends:
        cp.start()


def _xchg_finish(cps):
    local, sends, recvs = cps
    for cp in recvs:
        cp.wait_recv()
    for cp in sends:
        cp.wait_send()
    for cp in local:
        cp.wait()


def _xchg_out_shapes(ins, gather):
    return [jax.ShapeDtypeStruct(((N_DEV,) + s.shape) if gather else s.shape, s.dtype) for s in ins]


def _hosted_call(body, args, hosted, gather, *, grid, in_specs, out_specs, out_shape, scratch_shapes=(), **kw):
    na = len(hosted)
    if na == 0:
        return pl.pallas_call(body, grid=grid, in_specs=in_specs, out_specs=out_specs, out_shape=out_shape,
                              scratch_shapes=list(scratch_shapes), **kw)(*args)
    n_in, n_out, n_scr = len(in_specs), len(out_specs), len(scratch_shapes)

    def wrapped(*refs):
        ins, h_in = refs[:n_in], refs[n_in:n_in + na]
        outs, h_out = refs[n_in + na:n_in + na + n_out], refs[n_in + na + n_out:n_in + 2 * na + n_out]
        scr = refs[n_in + 2 * na + n_out:]
        cps = _xchg_copies(h_in, h_out, *scr[n_scr:], gather=gather)
        pids = [pl.program_id(a) for a in range(len(grid))]
        first = functools.reduce(jnp.logical_and, [p == 0 for p in pids])
        last = functools.reduce(jnp.logical_and, [p == g - 1 for p, g in zip(pids, grid)])

        @pl.when(first)
        def _():
            _xchg_start(cps)

        body(*ins, *outs, *scr[:n_scr])

        @pl.when(last)
        def _():
            _xchg_finish(cps)

    anyspec = pl.BlockSpec(memory_space=pl.ANY)
    return pl.pallas_call(
        wrapped, grid=grid, in_specs=list(in_specs) + [anyspec] * na, out_specs=list(out_specs) + [anyspec] * na,
        out_shape=list(out_shape) + _xchg_out_shapes(hosted, gather),
        scratch_shapes=list(scratch_shapes) + _xchg_scratch(na), **kw)(*args, *hosted)


ATT_BLOCKS = 2
PROB_ROWS = N_KV * GRP * WIN


def _attn_specs(E):
    nb = (E - CTX) // WIN
    last = E // WIN - 1
    kc = pl.BlockSpec((CTX, KP), lambda m: (0, 0))
    ks = [pl.BlockSpec((WIN, KP), functools.partial(lambda m, j: (jnp.minimum(ATT_BLOCKS * m + j, last), 0), j=j))
          for j in range(1, ATT_BLOCKS + 3)]
    return nb, [kc] + ks


def _attn_bias(nb):
    rows = np.arange(GRP * WIN)[:, None] % WIN
    cols = np.arange(CTX + 3 * WIN)[None, :]
    j = cols - CTX
    band = np.abs(j - WIN - rows) <= WIN
    out = []
    for first, last in ((True, False), (False, False), (False, True)):
        ok = (cols < CTX) | (band & ((j >= WIN) | (not first)) & ((j < 2 * WIN) | (not last)))
        out.append(np.where(ok, 0.0, NEG).astype(np.float32))
    bias = jnp.asarray(np.stack(out))
    steps = nb // ATT_BLOCKS
    shape = (1, GRP * WIN, CTX + 3 * WIN)
    specs = [pl.BlockSpec(shape, lambda m: (jnp.where(m == 0, 0, 1), 0, 0))]
    specs += [pl.BlockSpec(shape, lambda m: (1, 0, 0))] * (ATT_BLOCKS - 2)
    specs += [pl.BlockSpec(shape, lambda m: (jnp.where(m == steps - 1, 2, 1), 0, 0))]
    return bias, specs


def _both_halves(t, h):
    tf = t.astype(F32)
    r = pltpu.roll(tf, HD, 1)
    lo = lax.broadcasted_iota(jnp.int32, tf.shape, 1) < HD
    return (jnp.where(lo, tf, r) if h == 0 else jnp.where(lo, r, tf)).astype(t.dtype)


def _stack_heads(ref, h):
    lo = lax.broadcasted_iota(jnp.int32, (WIN, 128), 1) < HD
    parts = []
    for g in range(GRP):
        j = GRP * h + g
        t = ref[:, 128 * (j // 2):128 * (j // 2) + 128].astype(F32)
        parts.append(jnp.where(lo if j % 2 == 0 else jnp.logical_not(lo), t, 0.0))
    return jnp.concatenate(parts, axis=0)


def _unstack_pair(o, pp):
    lo = lax.broadcasted_iota(jnp.int32, (WIN, 128), 1) < HD
    return jnp.where(lo, o[WIN * 2 * pp:WIN * 2 * pp + WIN], o[WIN * (2 * pp + 1):WIN * (2 * pp + 1) + WIN])


def _attn_fwd(q, k, v, sink, shards):
    E = q.shape[0]
    S = E - CTX
    nb, kspecs = _attn_specs(E)
    na = len(shards)

    nk = ATT_BLOCKS + 3

    def one_block(q_ref, kw, vw, sink_ref, bias_ref, o_ref, lse_ref, p_ref):
        lane = lax.broadcasted_iota(jnp.int32, (WIN, 128), 1)
        lse_t = jnp.zeros((WIN, 128), F32)
        kall = jnp.concatenate([r[...] for r in kw], axis=0)
        vall = jnp.concatenate([r[...] for r in vw], axis=0)
        K = [_both_halves(kall, h) for h in range(N_KV)]
        Q = [_stack_heads(q_ref, h).astype(_BF) for h in range(N_KV)]
        sk = [jnp.concatenate([jnp.broadcast_to(sink_ref[GRP * h + g:GRP * h + g + 1, 0:1], (WIN, 1))
                               for g in range(GRP)], axis=0) for h in range(N_KV)]
        s = [_nt(Q[h], K[h]) + bias_ref[0] for h in range(N_KV)]
        m = [jnp.maximum(jnp.max(s[h], axis=1, keepdims=True), sk[h]) for h in range(N_KV)]
        e = [jnp.exp(s[h] - m[h]) for h in range(N_KV)]
        den = [jnp.sum(e[h], axis=1, keepdims=True) + jnp.exp(sk[h] - m[h]) for h in range(N_KV)]
        V = [_both_halves(vall, h) for h in range(N_KV)]
        pb = [(e[h] * (1.0 / den[h])).astype(_BF) for h in range(N_KV)]
        for h in range(N_KV):
            p_ref[GRP * WIN * h:GRP * WIN * (h + 1), :] = pb[h]
        o = [_nn(pb[h], V[h]) for h in range(N_KV)]
        for h in range(N_KV):
            lse = m[h] + jnp.log(den[h])
            for g in range(GRP):
                lse_t = jnp.where(lane == GRP * h + g, lse[WIN * g:WIN * g + WIN], lse_t)
            for pp in range(GRP // 2):
                t = 2 * h + pp
                o_ref[:, 128 * t:128 * t + 128] = _unstack_pair(o[h], pp).astype(_BF)
        lse_ref[...] = lse_t

    def body(q_ref, *rest):
        kr, vr, sink_ref = rest[0:nk], rest[nk:2 * nk], rest[2 * nk]
        bias_refs = rest[2 * nk + 1:2 * nk + 1 + ATT_BLOCKS]
        rest = rest[2 * nk + 1 + ATT_BLOCKS:]
        shard_refs, (o_ref, lse_ref, p_ref), got_refs = rest[:na], rest[na:na + 3], rest[na + 3:2 * na + 3]
        cps = _xchg_copies(shard_refs, got_refs, *rest[2 * na + 3:], gather=True)

        @pl.when(pl.program_id(0) == 0)
        def _():
            _xchg_start(cps)

        for j in range(ATT_BLOCKS):
            rows = pl.ds(WIN * j, WIN)
            one_block(q_ref.at[rows], [kr[0]] + list(kr[1 + j:4 + j]), [vr[0]] + list(vr[1 + j:4 + j]), sink_ref,
                      bias_refs[j], o_ref.at[rows], lse_ref.at[rows], p_ref.at[pl.ds(PROB_ROWS * j, PROB_ROWS)])

        @pl.when(pl.program_id(0) == nb // ATT_BLOCKS - 1)
        def _():
            _xchg_finish(cps)

    tq = ATT_BLOCKS * WIN
    qs = pl.BlockSpec((tq, QP), lambda m: (m + CTX // tq, 0))
    anyspec = pl.BlockSpec(memory_space=pl.ANY)
    bias, bias_specs = _attn_bias(nb)
    return pl.pallas_call(
        body, name="attn_fwd", grid=(nb // ATT_BLOCKS,),
        in_specs=[qs] + kspecs + kspecs + [_full((8, 128))] + bias_specs + [anyspec] * na,
        out_specs=[_rows(tq, 512), _rows(tq, 128), _rows(ATT_BLOCKS * PROB_ROWS, CTX + 3 * WIN)] + [anyspec] * na,
        out_shape=[jax.ShapeDtypeStruct((S, 512), _BF), jax.ShapeDtypeStruct((S, 128), F32),
                   jax.ShapeDtypeStruct((nb * PROB_ROWS, CTX + 3 * WIN), _BF)]
        + _xchg_out_shapes(shards, True),
        scratch_shapes=_xchg_scratch(na),
        compiler_params=_cp(("arbitrary",), 48 * 1024 * 1024),
    )(q, *([k] * nk), *([v] * nk), sink, *([bias] * ATT_BLOCKS), *shards)


def _attn_bwd(q, k, v, sink, probs, lse, d_attn, slabs):
    E = q.shape[0]
    S = E - CTX
    nb, kspecs = _attn_specs(E)
    last = E // WIN - 1
    na = len(slabs)

    nk = ATT_BLOCKS + 3

    def one_block(n, q_ref, kw, vw, sink_ref, p_ref, lse_ref, do_ref, dq_ref, dk_ref, dv_ref, ds_ref):
        lane = lax.broadcasted_iota(jnp.int32, (WIN, 128), 1)
        lse_t = lse_ref[...]
        starts = [None, pl.multiple_of((n + 1) * WIN, WIN), pl.multiple_of((n + 2) * WIN, WIN),
                  pl.multiple_of(jnp.minimum(n + 3, last) * WIN, WIN)]
        kall = jnp.concatenate([r[...] for r in kw], axis=0)
        vall = jnp.concatenate([r[...] for r in vw], axis=0)
        for h in range(N_KV):
            hs = slice(HD * h, HD * h + HD)
            K = _both_halves(kall, h)
            V = _both_halves(vall, h)
            Q = _stack_heads(q_ref, h).astype(_BF)
            sk = jnp.concatenate([jnp.broadcast_to(sink_ref[GRP * h + g:GRP * h + g + 1, 0:1], (WIN, 1))
                                  for g in range(GRP)], axis=0)
            ls = jnp.concatenate([jnp.sum(jnp.where(lane == GRP * h + g, lse_t, 0.0), axis=1, keepdims=True)
                                  for g in range(GRP)], axis=0)
            do = _stack_heads(do_ref, h).astype(_BF)
            pb = p_ref[GRP * WIN * h:GRP * WIN * (h + 1), :]
            p = pb.astype(F32)
            dp = _nt(do, V)
            delta = jnp.sum(p * dp, axis=1, keepdims=True)
            dsc = (p * (dp - delta)).astype(_BF)
            dq = _nn(dsc, K) * (HD ** -0.5)
            for pp in range(GRP // 2):
                t = 2 * h + pp
                dq_ref[:, 128 * t:128 * t + 128] = _unstack_pair(dq, pp).astype(_BF)
            dK2 = _tn(Q, dsc)
            dV2 = _tn(do, pb)
            dK = dK2[0:HD] + dK2[HD:2 * HD]
            dV = dV2[0:HD] + dV2[HD:2 * HD]
            dk_ref[hs, 0:CTX] += dK[:, 0:CTX]
            dv_ref[hs, 0:CTX] += dV[:, 0:CTX]
            for w in range(1, 4):
                lo = CTX + WIN * (w - 1)
                dk_ref[hs, pl.ds(starts[w], WIN)] += dK[:, lo:lo + WIN]
                dv_ref[hs, pl.ds(starts[w], WIN)] += dV[:, lo:lo + WIN]
            psk = -jnp.exp(sk - ls) * delta
            for g in range(GRP):
                j = GRP * h + g
                tot = jnp.sum(psk[WIN * g:WIN * g + WIN], axis=0, keepdims=True)
                ds_ref[j:j + 1, :] += jnp.broadcast_to(tot, (1, 128))

    def body(q_ref, *rest):
        kr, vr = rest[0:nk], rest[nk:2 * nk]
        sink_ref, p_ref, lse_ref, do_ref = rest[2 * nk:2 * nk + 4]
        rest = rest[2 * nk + 4:]
        slab_refs, (dq_ref, dk_ref, dv_ref, ds_ref), got_refs = rest[:na], rest[na:na + 4], rest[na + 4:2 * na + 4]
        m = pl.program_id(0)
        cps = _xchg_copies(slab_refs, got_refs, *rest[2 * na + 4:], gather=False)

        @pl.when(m == 0)
        def _():
            _xchg_start(cps)
            dk_ref[...] = jnp.zeros_like(dk_ref)
            dv_ref[...] = jnp.zeros_like(dv_ref)
            ds_ref[...] = jnp.zeros_like(ds_ref)

        for j in range(ATT_BLOCKS):
            rows = pl.ds(WIN * j, WIN)
            one_block(ATT_BLOCKS * m + j, q_ref.at[rows], [kr[0]] + list(kr[1 + j:4 + j]),
                      [vr[0]] + list(vr[1 + j:4 + j]), sink_ref, p_ref.at[pl.ds(PROB_ROWS * j, PROB_ROWS)],
                      lse_ref.at[rows], do_ref.at[rows], dq_ref.at[rows], dk_ref, dv_ref, ds_ref)

        @pl.when(m == nb // ATT_BLOCKS - 1)
        def _():
            _xchg_finish(cps)

    tq = ATT_BLOCKS * WIN
    qs = pl.BlockSpec((tq, QP), lambda m: (m + CTX // tq, 0))
    anyspec = pl.BlockSpec(memory_space=pl.ANY)
    return pl.pallas_call(
        body, name="attn_bwd", grid=(nb // ATT_BLOCKS,),
        in_specs=[qs] + kspecs + kspecs + [_full((8, 128)), _rows(ATT_BLOCKS * PROB_ROWS, CTX + 3 * WIN),
                                            _rows(tq, 128), _rows(tq, 512)] + [anyspec] * na,
        out_specs=[_rows(tq, QP), _full((KP, E)), _full((KP, E)), _full((8, 128))] + [anyspec] * na,
        out_shape=[jax.ShapeDtypeStruct((S, QP), _BF), jax.ShapeDtypeStruct((KP, E), F32),
                   jax.ShapeDtypeStruct((KP, E), F32), jax.ShapeDtypeStruct((8, 128), F32)]
        + _xchg_out_shapes(slabs, False),
        scratch_shapes=_xchg_scratch(na),
        compiler_params=_cp(("arbitrary",), 48 * 1024 * 1024),
    )(q, *([k] * nk), *([v] * nk), sink, probs, lse, d_attn, *slabs)


GLA_STEP = 2


def _gla_subs(reverse, backward):
    subs = list(range(GLA_STEP))
    return subs[::-1] if reverse != backward else subs


def _gla_order(E, reverse, backward):
    nc = CTX // (GLA_STEP * GLA_T)
    n = E // (GLA_STEP * GLA_T)
    if not reverse:
        fwd = lambda s: s
    else:
        fwd = lambda s: jnp.where(s < nc, nc - 1 - s, n - 1 + nc - s)
    if backward:
        return lambda s: fwd(n - 1 - s)
    return fwd


def _gla_masks():
    T = GLA_T
    l128 = lax.broadcasted_iota(jnp.int32, (1, 128), 1)
    qmask = [((l128 >> 5) == j).astype(F32) for j in range(4)]
    vmask = [((l128 >> 6) == j).astype(F32) for j in range(2)]
    bd = ((lax.broadcasted_iota(jnp.int32, (512, 256), 0) >> 6)
          == (lax.broadcasted_iota(jnp.int32, (512, 256), 1) >> 5)).astype(F32)
    ri = lax.broadcasted_iota(jnp.int32, (T, 2 * T), 0)
    ci = lax.broadcasted_iota(jnp.int32, (T, 2 * T), 1) & (T - 1)
    return qmask, vmask, bd, ri, ci


def _tri_sum(tri, x):
    hi = x.astype(_BF)
    lo = (x - hi.astype(F32)).astype(_BF)
    n = x.shape[1]
    y = _nn(tri.astype(_BF), jnp.concatenate([hi, lo], axis=1))
    return y[:, 0:n] + y[:, n:2 * n]


def _gla_decays(la, reverse, ri, ci):
    T = GLA_T
    msk2 = (ri <= ci) if reverse else (ri >= ci)
    mskT2 = (ri >= ci) if reverse else (ri <= ci)
    b = _tri_sum(msk2[:, 0:T], la)
    bT = b[0:1] if reverse else b[T - 1:T]
    bm = b[T // 2:T // 2 + 1]
    return msk2, mskT2, b, bT, bm


def _pair_stack(tile, m0, m1):
    return jnp.concatenate([(tile * m0).astype(_BF), (tile * m1).astype(_BF)], axis=0)


def _gla_fwd(gq, gk, gv, la, shards=()):
    E = gq.shape[0]
    T = GLA_T
    n = E // T
    TB = GLA_STEP * T
    orders = [_gla_order(E, False, False), _gla_order(E, True, False)]

    def one_direction(reverse, *refs):
        for j in _gla_subs(reverse, False):
            rows = pl.ds(T * j, T)
            one_chunk(reverse, *[r.at[rows] for r in refs[:5]], refs[5].at[j], refs[6])

    def one_chunk(reverse, gq_ref, gk_ref, gv_ref, la_ref, o_ref, st_ref, S_scr):
        qmask, vmask, bd, ri, ci = _gla_masks()
        msk2, _, b, bT, bm = _gla_decays(la_ref[...], reverse, ri, ci)
        q, k, v = gq_ref[...], gk_ref[...], gv_ref[...]
        qd = (q * jnp.exp(b)).astype(_BF)
        qm = (q * jnp.exp(b - bm)).astype(_BF)
        km = k * jnp.exp(bm - b)
        kd = (k * jnp.exp(bT - b)).astype(_BF)
        ST = S_scr[...]
        comp = ST[0:DV]
        for h in range(1, N_GLA):
            comp = comp + ST[DV * h:DV * h + DV]
        st_ref[...] = comp
        inter = _nt(qd, ST.astype(_BF))
        tiles = []
        for p in range(N_GLA // 2):
            qs = slice(128 * (p // 2), 128 * (p // 2) + 128)
            vs = slice(128 * p, 128 * p + 128)
            j0 = (2 * p) % 4
            KS = _pair_stack(km[:, qs], qmask[j0], qmask[j0 + 1])
            VS = _pair_stack(v[:, vs], vmask[0], vmask[1])
            AA = jnp.where(msk2, _nt(qm[:, qs], KS), 0.0).astype(_BF)
            tiles.append(_nn(AA, VS))
        o_ref[...] = inter + jnp.concatenate(tiles, axis=1)
        S_scr[...] = ST * jnp.exp(bT) + bd * _tn(v.astype(_BF), kd)

    def body(qf, kf, vf, lf, qr, kr, vr, lr, of, sf, orr, sr, S_f, S_r):
        @pl.when(pl.program_id(0) == 0)
        def _():
            S_f[...] = jnp.zeros_like(S_f)
            S_r[...] = jnp.zeros_like(S_r)

        one_direction(False, qf, kf, vf, lf, of, sf, S_f)
        one_direction(True, qr, kr, vr, lr, orr, sr, S_r)

    def blk(d, w, c=0):
        return pl.BlockSpec((TB, w), lambda s: (orders[d](s), c))

    def st_spec(d):
        return pl.BlockSpec((GLA_STEP, DV, 256), lambda s: (orders[d](s), 0, 0))

    return _hosted_call(
        body, (gq, gk, gv, la, gq, gk, gv, la), shards, True, name="gla_fwd", grid=(n // GLA_STEP,),
        in_specs=[blk(0, 256), blk(0, 256), blk(0, 512), blk(0, 256, 0), blk(1, 256), blk(1, 256), blk(1, 512),
                  blk(1, 256, 1)],
        out_specs=[blk(0, 512), st_spec(0), blk(1, 512), st_spec(1)],
        out_shape=[jax.ShapeDtypeStruct((E, 512), F32), jax.ShapeDtypeStruct((n, DV, 256), F32)] * 2,
        scratch_shapes=[pltpu.VMEM((512, 256), F32)] * 2,
        compiler_params=_cp(("arbitrary",)))


def _gla_bwd(gq, gk, gv, la, z, st_f, st_r, do, slabs=()):
    E = gq.shape[0]
    T = GLA_T
    n = E // T
    TB = GLA_STEP * T
    nc = CTX // TB
    orders = [_gla_order(E, False, True), _gla_order(E, True, True)]

    def one_direction(reverse, *refs):
        ins, outs, shared = refs[0:7], refs[7:11], refs[11:14]
        for j in _gla_subs(reverse, True):
            rows = pl.ds(T * j, T)
            views = [r.at[rows] for r in ins[0:5]] + [ins[5].at[j], ins[6].at[rows]] + [r.at[rows] for r in outs]
            one_chunk(reverse, *views, *shared)

    def one_chunk(reverse, gq_ref, gk_ref, gv_ref, la_ref, z_ref, st_ref, do_ref,
                  dq_ref, dk_ref, dv_ref, dlg_ref, gwg_ref, bsum_ref, dS_scr):
        cols = slice(256, 512) if reverse else slice(0, 256)
        is_lat = orders[1 if reverse else 0](pl.program_id(0)) >= nc
        qmask, vmask, bd, ri, ci = _gla_masks()
        msk2, mskT2, b, bT, bm = _gla_decays(la_ref[...], reverse, ri, ci)
        q, k, v = gq_ref[...], gk_ref[...], gv_ref[...]
        do = jnp.where(is_lat, do_ref[...].astype(F32), 0.0)
        e_b, e_qm, e_km, e_kd, e_T = jnp.exp(b), jnp.exp(b - bm), jnp.exp(bm - b), jnp.exp(bT - b), jnp.exp(bT)
        qd, qm, km, kd = q * e_b, q * e_qm, k * e_km, k * e_kd
        qdb, qmb, kmb, kdb, vb, dob = (t.astype(_BF) for t in (qd, qm, km, kd, v, do))
        ST = jnp.tile(st_ref[...], (N_GLA, 1)) * bd
        dST = dS_scr[...]
        dSTb = dST.astype(_BF)
        dqd = _nn(dob, ST.astype(_BF))
        dkd = _nn(vb, dSTb)
        dv_t, dqm_t, dkm_t = [], [None, None], [None, None]
        for p in range(N_GLA // 2):
            t = p // 2
            qs = slice(128 * t, 128 * t + 128)
            vs = slice(128 * p, 128 * p + 128)
            j0 = (2 * p) % 4
            QS = _pair_stack(qm[:, qs], qmask[j0], qmask[j0 + 1])
            KS = _pair_stack(km[:, qs], qmask[j0], qmask[j0 + 1])
            VS = _pair_stack(v[:, vs], vmask[0], vmask[1])
            DS = _pair_stack(do[:, vs], vmask[0], vmask[1])
            ATT = jnp.where(mskT2, _nt(kmb[:, qs], QS), 0.0).astype(_BF)
            dAA = jnp.where(msk2, _nt(dob[:, vs], VS), 0.0).astype(_BF)
            dATT = jnp.where(mskT2, _nt(vb[:, vs], DS), 0.0).astype(_BF)
            dv_t.append(_nn(ATT, DS))
            dq_p = _nn(dAA, KS)
            dk_p = _nn(dATT, QS)
            dqm_t[t] = dq_p if dqm_t[t] is None else dqm_t[t] + dq_p
            dkm_t[t] = dk_p if dkm_t[t] is None else dkm_t[t] + dk_p
        dqm = jnp.concatenate(dqm_t, axis=1)
        dkm = jnp.concatenate(dkm_t, axis=1)
        dq = dqm * e_qm + dqd * e_b
        dk = dkm * e_km + dkd * e_kd
        dv = _nt(kdb, dSTb) + jnp.concatenate(dv_t, axis=1)
        dq_ref[...] = (dq * (DK ** -0.5)).astype(_BF)
        dk_ref[...] = dk.astype(_BF)
        dv_ref[...] = dv.astype(_BF)
        db = dqm * qm - dkm * km + dqd * qd - dkd * kd
        dbT = jnp.sum(dkd * kd, axis=0, keepdims=True) + e_T * jnp.sum(dST * ST, axis=0, keepdims=True)
        dla = _tri_sum(mskT2[:, 0:T], db) + dbT
        dlg = dla * (1.0 - jnp.exp(GATE_TAU * la_ref[...])) * (1.0 / GATE_TAU)
        bsum_ref[0:1, cols] += jnp.sum(dlg, axis=0, keepdims=True)
        dlgb = dlg.astype(_BF)
        dlg_ref[...] = dlgb
        gwg_ref[:, cols] += _tn(z_ref[...], dlgb)
        dS_scr[...] = dST * e_T + bd * _tn(dob, qdb)

    def body(*refs):
        ins_f, ins_r = refs[0:7], refs[7:14]
        outs_f, outs_r = refs[14:18], refs[18:22]
        gwg_ref, bsum_ref, dS_f, dS_r = refs[22:26]

        @pl.when(pl.program_id(0) == 0)
        def _():
            dS_f[...] = jnp.zeros_like(dS_f)
            dS_r[...] = jnp.zeros_like(dS_r)
            gwg_ref[...] = jnp.zeros_like(gwg_ref)
            bsum_ref[...] = jnp.zeros_like(bsum_ref)

        one_direction(False, *ins_f, *outs_f, gwg_ref, bsum_ref, dS_f)
        one_direction(True, *ins_r, *outs_r, gwg_ref, bsum_ref, dS_r)

    def specs(d, st):
        order = orders[d]
        blk = lambda w, c=0: pl.BlockSpec((TB, w), lambda s: (order(s), c))
        ins = [blk(256), blk(256), blk(512), blk(256, d), blk(128),
               pl.BlockSpec((GLA_STEP, DV, 256), lambda s: (order(s), 0, 0)),
               pl.BlockSpec((TB, 512), lambda s: (jnp.maximum(order(s) - nc, 0), 0))]
        return ins, [blk(256), blk(256), blk(512), blk(256)], (gq, gk, gv, la, z, st, do)

    in_f, out_f, args_f = specs(0, st_f)
    in_r, out_r, args_r = specs(1, st_r)
    dir_shapes = [jax.ShapeDtypeStruct((E, 256), _BF), jax.ShapeDtypeStruct((E, 256), _BF),
                  jax.ShapeDtypeStruct((E, 512), _BF), jax.ShapeDtypeStruct((E, 256), _BF)]
    return _hosted_call(
        body, args_f + args_r, slabs, False, name="gla_bwd", grid=(n // GLA_STEP,),
        in_specs=in_f + in_r, out_specs=out_f + out_r + [_full((128, 512)), _full((8, 512))],
        out_shape=dir_shapes * 2 + [jax.ShapeDtypeStruct((128, 512), F32), jax.ShapeDtypeStruct((8, 512), F32)],
        scratch_shapes=[pltpu.VMEM((512, 256), F32)] * 2,
        compiler_params=_cp(("arbitrary",)))


def _gla_out(o, gg, ggla, mavg):
    rr = lax.rsqrt(_head_mean(o * o, mavg) + EPS)
    oh = o * rr
    sg = _sigmoid(gg)
    return oh, rr, sg


def _mix_fwd(x, attn, o_f, o_b, gg, ggla, mavg, wout, gt1, g2):
    S = x.shape[0]
    TM = 256

    def body(x_ref, a_ref, of_ref, ob_ref, gg_ref, ggla_ref, mavg_ref, w_ref, gt1_ref, g2_ref, x1_ref, mix_ref):
        gg_t = gg_ref[...]
        oh, _, sg = _gla_out(of_ref[...] + ob_ref[...], gg_t, ggla_ref[...], mavg_ref[...])
        mix_ref[:, 0:512] = a_ref[...]
        mix_ref[:, 512:1024] = (oh * ggla_ref[...] * (gg_t * sg)).astype(_BF)
        y = _nn(mix_ref[...], w_ref[...])
        ry = lax.rsqrt(jnp.mean(y * y, axis=-1, keepdims=True) + EPS)
        x1_ref[...] = x_ref[...] + gt1_ref[...] * ((y * ry) * g2_ref[...])

    return pl.pallas_call(
        body, name="mix_fwd", grid=(S // TM,),
        in_specs=[_rows(TM, D), _rows(TM, 512), _rows(TM, 512, 1), _rows(TM, 512, 1), _rows(TM, 512, 1),
                  _full((1, 512)), _full((512, 512)), _full((D, D)), _full((1, D)), _full((1, D))],
        out_specs=[_rows(TM, D), _rows(TM, D)],
        out_shape=[jax.ShapeDtypeStruct((S, D), F32), jax.ShapeDtypeStruct((S, D), _BF)],
        compiler_params=_cp(("arbitrary",), 40 * 1024 * 1024),
    )(x, attn, o_f, o_b, gg, ggla, mavg, wout, gt1, g2)


def _mix_bwd(dx1, mix, o_f, o_b, gg, ggla, mavg, wout, gt1, g2, slabs):
    S = dx1.shape[0]
    TM = 256

    def body(dx_ref, mix_ref, of_ref, ob_ref, gg_ref, ggla_ref, mavg_ref, w_ref, gt1_ref, g2_ref,
             da_ref, do_ref, dgg_ref, dy_ref, sums_ref):
        @pl.when(pl.program_id(0) == 0)
        def _():
            sums_ref[...] = jnp.zeros_like(sums_ref)

        dx = dx_ref[...]
        y = _nn(mix_ref[...], w_ref[...])
        ry = lax.rsqrt(jnp.mean(y * y, axis=-1, keepdims=True) + EPS)
        yh = y * ry
        sums_ref[0:1, :] += jnp.sum(dx * yh, axis=0, keepdims=True)
        dyh = dx * (gt1_ref[...] * g2_ref[...])
        dy = (ry * (dyh - yh * jnp.mean(dyh * yh, axis=-1, keepdims=True))).astype(_BF)
        dy_ref[...] = dy
        dmix = _nt(dy, w_ref[...])
        da_ref[...] = dmix[:, 0:512].astype(_BF)
        dgla = dmix[:, 512:1024]
        gg_t = gg_ref[...]
        ggla_t = ggla_ref[...]
        oh, rr, sg = _gla_out(of_ref[...] + ob_ref[...], gg_t, ggla_t, mavg_ref[...])
        dgg_ref[...] = (dgla * oh * ggla_t * (sg * (1.0 + gg_t * (1.0 - sg)))).astype(_BF)
        don = dgla * (gg_t * sg)
        sums_ref[1:2, 0:512] += jnp.sum(don * oh, axis=0, keepdims=True)
        doh = don * ggla_t
        do_ref[...] = (rr * (doh - oh * _head_mean(doh * oh, mavg_ref[...]))).astype(_BF)

    return _hosted_call(
        body, (dx1, mix, o_f, o_b, gg, ggla, mavg, wout, gt1, g2), slabs, False,
        name="mix_bwd", grid=(S // TM,),
        in_specs=[_rows(TM, D), _rows(TM, D), _rows(TM, 512, 1), _rows(TM, 512, 1), _rows(TM, 512, 1),
                  _full((1, 512)), _full((512, 512)), _full((D, D)), _full((1, D)), _full((1, D))],
        out_specs=[_rows(TM, 512), _rows(TM, 512), _rows(TM, 512), _rows(TM, D), _full((8, D))],
        out_shape=[jax.ShapeDtypeStruct((S, 512), _BF), jax.ShapeDtypeStruct((S, 512), _BF),
                   jax.ShapeDtypeStruct((S, 512), _BF), jax.ShapeDtypeStruct((S, D), _BF),
                   jax.ShapeDtypeStruct((8, D), F32)],
        compiler_params=_cp(("arbitrary",), 40 * 1024 * 1024))


def _ffn(x1, target, gm2, sh2, gt2, g4, wffi, wffo):
    S = x1.shape[0]
    TF = 256

    def body(x_ref, t_ref, gm_ref, sh_ref, gt_ref, g4_ref, wi_hbm, wo_hbm,
             dx_ref, h_ref, du_ref, act_ref, df_ref, sums_ref, loss_ref, wi, wo, sem):
        @pl.when(pl.program_id(0) == 0)
        def _():
            c1 = pltpu.make_async_copy(wi_hbm, wi, sem.at[0])
            c2 = pltpu.make_async_copy(wo_hbm, wo, sem.at[1])
            c1.start()
            c2.start()
            sums_ref[...] = jnp.zeros_like(sums_ref)
            loss_ref[...] = jnp.zeros_like(loss_ref)
            c1.wait()
            c2.wait()

        x = x_ref[...]
        gm = gm_ref[...]
        r = lax.rsqrt(jnp.mean(x * x, axis=-1, keepdims=True) + EPS)
        xh = x * r
        hb = (xh * gm + sh_ref[...]).astype(_BF)
        h_ref[...] = hb
        u = _nt(hb, wi[...])
        g = u[:, 0:FFN]
        up = u[:, FFN:2 * FFN]
        sg = _sigmoid(g)
        sl = g * sg
        ab = (sl * up).astype(_BF)
        act_ref[...] = ab
        f = _nn(ab, wo[...])
        rf = lax.rsqrt(jnp.mean(f * f, axis=-1, keepdims=True) + EPS)
        fh = f * rf
        gt, g4v = gt_ref[...], g4_ref[...]
        err = x + gt * (fh * g4v) - t_ref[...]
        loss_ref[...] += jnp.sum(err * err) * (0.5 / D)
        dout = err * (1.0 / D)
        sums_ref[2:3, :] += jnp.sum(dout * fh, axis=0, keepdims=True)
        dfh = dout * (gt * g4v)
        dfb = (rf * (dfh - fh * jnp.mean(dfh * fh, axis=-1, keepdims=True))).astype(_BF)
        df_ref[...] = dfb
        dact = _nt(dfb, wo[...])
        du_ref[:, 0:FFN] = (dact * up * (sg * (1.0 + g * (1.0 - sg)))).astype(_BF)
        du_ref[:, FFN:2 * FFN] = (dact * sl).astype(_BF)
        dh = _nn(du_ref[...], wi[...])
        sums_ref[0:1, :] += jnp.sum(dh, axis=0, keepdims=True)
        sums_ref[1:2, :] += jnp.sum(dh * xh, axis=0, keepdims=True)
        dxh = dh * gm
        dx_ref[...] = dout + r * (dxh - xh * jnp.mean(dxh * xh, axis=-1, keepdims=True))

    vec = _full((1, D))
    anyspec = pl.BlockSpec(memory_space=pl.ANY)
    return pl.pallas_call(
        body, name="ffn_fwd_bwd", grid=(S // TF,),
        in_specs=[_rows(TF, D), _rows(TF, D), vec, vec, vec, vec, anyspec, anyspec],
        out_specs=[_rows(TF, D), _rows(TF, D), _rows(TF, 2 * FFN), _rows(TF, FFN), _rows(TF, D),
                   _full((8, D)), _full((8, 128))],
        out_shape=[jax.ShapeDtypeStruct((S, D), F32), jax.ShapeDtypeStruct((S, D), _BF),
                   jax.ShapeDtypeStruct((S, 2 * FFN), _BF), jax.ShapeDtypeStruct((S, FFN), _BF),
                   jax.ShapeDtypeStruct((S, D), _BF), jax.ShapeDtypeStruct((8, D), F32),
                   jax.ShapeDtypeStruct((8, 128), F32)],
        scratch_shapes=[pltpu.VMEM((2 * FFN, D), _BF), pltpu.VMEM((FFN, D), _BF), pltpu.SemaphoreType.DMA((2,))],
        compiler_params=_cp(("arbitrary",), VMEM_BIG),
    )(x1, target, gm2, sh2, gt2, g4, wffi, wffo)


def _inproj_bwd(x, ctx, gml, gmc, win, wg, cos, sa, sb, dq, dk, dv, dgq, dgk, dgv, dgg, dlg_f, dlg_b, dx1):
    S = x.shape[0]
    E = S + CTX
    TE = CTX

    def body(x_ref, c_ref, gml_ref, gmc_ref, w_ref, wg_ref, cos_ref, sa_ref, sb_ref, dq_ref, dk_ref, dv_ref,
             gqf, gqb, gkf, gkb, gvf, gvb, dgg_ref, dlf, dlb, dx1_ref, dp_ref, gx_ref, sums_ref):
        i = pl.program_id(0)
        is_ctx = i == 0

        @pl.when(is_ctx)
        def _():
            sums_ref[...] = jnp.zeros_like(sums_ref)

        lat = jnp.where(is_ctx, 0.0, 1.0)
        cos_t, sa_t, sb_t = cos_ref[...], sa_ref[...], sb_ref[...]
        dp_ref[:, O_Q:O_K] = (_unrope(dq_ref[...].astype(F32), cos_t, sa_t, sb_t) * lat).astype(_BF)
        dp_ref[:, O_K:O_V] = _unrope(dk_ref[...].T, cos_t, sa_t, sb_t).astype(_BF)
        dp_ref[:, O_V:O_GQ] = dv_ref[...].T.astype(_BF)
        dp_ref[:, O_GQ:O_GK] = (gqf[...].astype(F32) + gqb[...].astype(F32)).astype(_BF)
        dp_ref[:, O_GK:O_GV] = (gkf[...].astype(F32) + gkb[...].astype(F32)).astype(_BF)
        dp_ref[:, O_GV:O_GG] = (gvf[...].astype(F32) + gvb[...].astype(F32)).astype(_BF)
        dp_ref[:, O_GG:O_Z] = (dgg_ref[...].astype(F32) * lat).astype(_BF)
        dlg = jnp.concatenate([dlf[...], dlb[...]], axis=1)
        dp_ref[:, O_Z:NP] = _nt(dlg, wg_ref[...]).astype(_BF)
        dh = _nn(dp_ref[...], w_ref[...])
        x = jnp.where(is_ctx, c_ref[...], x_ref[...])
        r = lax.rsqrt(jnp.mean(x * x, axis=-1, keepdims=True) + EPS)
        xh = x * r
        sdh = jnp.sum(dh, axis=0, keepdims=True)
        sdx = jnp.sum(dh * xh, axis=0, keepdims=True)
        sums_ref[0:1, :] += sdh * lat
        sums_ref[1:2, :] += sdx * lat
        sums_ref[2:3, :] += sdh * (1.0 - lat)
        sums_ref[3:4, :] += sdx * (1.0 - lat)
        dxh = dh * jnp.where(is_ctx, gmc_ref[...], gml_ref[...])
        gx_ref[...] = dx1_ref[...] + r * (dxh - xh * jnp.mean(dxh * xh, axis=-1, keepdims=True))

    vec = _full((1, D))
    tab = _rows(TE, 128)
    return pl.pallas_call(
        body, name="inproj_bwd", grid=(E // TE,),
        in_specs=[_rows_lat(TE, D), _full((CTX, D)), vec, vec, _full((NP, D)), _full((128, 512)), tab, tab, tab,
                  _rows_lat(TE, QP), pl.BlockSpec((KP, TE), lambda i: (0, i)), pl.BlockSpec((KP, TE), lambda i: (0, i)),
                  _rows(TE, 256), _rows(TE, 256), _rows(TE, 256), _rows(TE, 256), _rows(TE, 512), _rows(TE, 512),
                  _rows_lat(TE, 512), _rows(TE, 256), _rows(TE, 256), _rows_lat(TE, D)],
        out_specs=[_rows(TE, NP), _rows_lat(TE, D), _full((8, D))],
        out_shape=[jax.ShapeDtypeStruct((E, NP), _BF), jax.ShapeDtypeStruct((S, D), F32),
                   jax.ShapeDtypeStruct((8, D), F32)],
        compiler_params=_cp(("arbitrary",), VMEM_BIG),
    )(x, ctx, gml, gmc, win, wg, cos, sa, sb, dq, dk, dv, *dgq, *dgk, *dgv, dgg, dlg_f, dlg_b, dx1)


def _matmul_tn(a, b, tk, tt, name, out_dtype, transpose_out=False, a_cols=None, hosted=(), gather=True):
    T, KA = a.shape
    N = b.shape[1]
    nt = T // tt
    k0 = 0
    if a_cols is not None:
        KA, k0 = tk, a_cols

    def body(a_ref, b_ref, o_ref, acc):
        t = pl.program_id(1)

        @pl.when(t == 0)
        def _():
            acc[...] = jnp.zeros_like(acc)

        acc[...] += _tn(a_ref[...], b_ref[...])

        @pl.when(t == nt - 1)
        def _():
            o_ref[...] = (acc[...].T if transpose_out else acc[...]).astype(out_dtype)

    if transpose_out:
        out_spec, out_shape = pl.BlockSpec((N, tk), lambda i, t: (0, i)), (N, KA)
    else:
        out_spec, out_shape = pl.BlockSpec((tk, N), lambda i, t: (i, 0)), (KA, N)
    res = _hosted_call(
        body, (a, b), hosted, gather, name=name, grid=(KA // tk, nt),
        in_specs=[pl.BlockSpec((tt, tk), lambda i, t: (t, i + k0)), pl.BlockSpec((tt, N), lambda i, t: (t, 0))],
        out_specs=[out_spec], out_shape=[jax.ShapeDtypeStruct(out_shape, out_dtype)],
        scratch_shapes=[pltpu.VMEM((tk, N), F32)],
        compiler_params=_cp(("arbitrary", "arbitrary"), VMEM_BIG))
    return res if hosted else res[0]


def _ada_bwd(c_all, c_ctx, w_ada, d_all):
    n = w_ada.shape[1]

    def body(c_ref, cc_ref, w_ref, d_ref, gw_ref, t_ref):
        c = jnp.concatenate([c_ref[...], jnp.broadcast_to(cc_ref[...], (8, D))], axis=0)
        db = d_ref[...].astype(_BF)
        gw_ref[0] = _tn((c * _sigmoid(c)).astype(_BF), db)
        t_ref[...] = _nt(db[8:16], w_ref[...].astype(_BF))

    return pl.pallas_call(
        body, name="ada_bwd", in_specs=[_full((8, D)), _full((1, D)), _full((D, n)), _full((16, n))],
        out_specs=[_full((1, D, n)), _full((8, D))],
        out_shape=[jax.ShapeDtypeStruct((1, D, n), F32), jax.ShapeDtypeStruct((8, D), F32)], grid=(1,),
        compiler_params=_cp(("arbitrary",)),
    )(c_all, c_ctx, w_ada, d_all)


PART_ROWS = 56
R_ADA, R_ADA_C, R_GAIN, R_SINK, R_BG, R_GGLA, R_LOSS, R_WG = 0, 6, 12, 16, 17, 18, 19, 24


def _small_grads(s_in, s_ffn, s_mix, ada_l, ada_c, gains, dsink, s_bg, g_wg, loss):
    def body(si, sf, sm, al, ac, g, ds, sbg, gwg, loss_ref, o_ref):
        o_ref[...] = jnp.zeros_like(o_ref)
        o_ref[R_LOSS:R_LOSS + 1, 0:128] = loss_ref[0:1, :]
        sub = lax.broadcasted_iota(jnp.int32, (8, 128), 0)
        lane = lax.broadcasted_iota(jnp.int32, (8, 128), 1)
        o_ref[R_SINK:R_SINK + 1, 0:128] = jnp.sum(jnp.where(sub == lane, ds[...], 0.0), axis=0, keepdims=True)
        o_ref[R_BG:R_BG + 1, 0:512] = sbg[0:1, :]
        y = sm[1:2, 0:128] + sm[1:2, 128:256] + sm[1:2, 256:384] + sm[1:2, 384:512]
        y = y + pltpu.roll(y, 64, 1)
        o_ref[R_GGLA:R_GGLA + 1, 0:128] = jnp.where(lane[0:1] < DV, y, 0.0)
        o_ref[R_WG:R_WG + 16, 0:256] = gwg[0:16, 0:256]
        o_ref[R_WG + 16:R_WG + 32, 0:256] = gwg[16:32, 256:512]
        sdh_l, sdx_l, sdh_c, sdx_c = si[0:1], si[1:2], si[2:3], si[3:4]
        sdh2, sdx2, a2 = sf[0:1], sf[1:2], sf[2:3]
        a1 = sm[0:1]
        g1, g2, g3, g4 = g[0:1], g[1:2], g[2:3], g[3:4]
        sc1, gt1, sc2, gt2 = al[1:2], al[2:3], al[4:5], al[5:6]
        sc1c = ac[1:2]
        z = jnp.zeros((1, D), F32)
        rows = [sdh_l, sdx_l * g1, a1 * g2, sdh2, sdx2 * g3, a2 * g4,
                sdh_c, sdx_c * g1, z, z, z, z,
                sdx_l * (1.0 + sc1) + sdx_c * (1.0 + sc1c), a1 * gt1, sdx2 * (1.0 + sc2), a2 * gt2]
        for r, v in enumerate(rows):
            o_ref[r:r + 1, :] = v

    v8 = _full((8, D))
    return pl.pallas_call(
        body, name="small_grads",
        in_specs=[v8] * 6 + [_full((8, 128)), _full((8, 512)), _full((128, 512)), _full((8, 128))],
        out_specs=_full((PART_ROWS, D)), out_shape=jax.ShapeDtypeStruct((PART_ROWS, D), F32), grid=(1,),
        compiler_params=_cp(("arbitrary",)),
    )(s_in, s_ffn, s_mix, ada_l, ada_c, gains, dsink, s_bg, g_wg, loss)


def _row_tile(R):
    for cand in (256, 128, 64, 32, 16):
        if R % cand == 0 and R > cand:
            return cand
    return R


def _adamw(w, gs, m, v, name):
    _, R, C = w.shape
    tr = _row_tile(R)
    c1 = 1.0 / (1.0 - ADAM_B1 ** ADAM_STEP)
    c2 = 1.0 / (1.0 - ADAM_B2 ** ADAM_STEP)
    ng = len(gs)

    def body(w_ref, *refs):
        g_refs, (m_ref, v_ref, go_ref, d_ref, nm_ref, nv_ref) = refs[:ng], refs[ng:]
        c0 = 0
        for g_ref in g_refs:
            cols = slice(c0, c0 + g_ref.shape[2])
            c0 += g_ref.shape[2]
            gg = g_ref[0].astype(F32)
            for j in range(1, g_ref.shape[0]):
                gg = gg + g_ref[j].astype(F32)
            go_ref[0, :, cols] = gg
            nm = ADAM_B1 * m_ref[0, :, cols] + (1.0 - ADAM_B1) * gg
            nv = ADAM_B2 * v_ref[0, :, cols] + (1.0 - ADAM_B2) * (gg * gg)
            nm_ref[0, :, cols] = nm
            nv_ref[0, :, cols] = nv
            d_ref[0, :, cols] = -ADAM_LR * ((nm * c1) / (jnp.sqrt(nv * c2) + ADAM_EPS) + ADAM_WD * w_ref[0, :, cols])

    spec = pl.BlockSpec((1, tr, C), lambda i: (0, i, 0))
    sds = jax.ShapeDtypeStruct((1, R, C), F32)
    g_specs = [pl.BlockSpec((g.shape[0], tr, g.shape[2]), lambda i: (0, i, 0)) for g in gs]
    return pl.pallas_call(
        body, name=name, grid=(R // tr,), in_specs=[spec] + g_specs + [spec, spec], out_specs=[spec] * 4,
        out_shape=[sds] * 4, compiler_params=_cp(("parallel",), 48 * 1024 * 1024),
    )(w, *gs, m, v)


def _sum_slots(slots, name):
    _, R, C = slots.shape
    tr = _row_tile(R)

    def body(s_ref, o_ref):
        acc = s_ref[0].astype(F32)
        for j in range(1, N_DEV):
            acc = acc + s_ref[j].astype(F32)
        o_ref[...] = acc

    return pl.pallas_call(
        body, name=name, grid=(R // tr,), in_specs=[pl.BlockSpec((N_DEV, tr, C), lambda i: (0, i, 0))],
        out_specs=_rows(tr, C), out_shape=jax.ShapeDtypeStruct((R, C), F32), compiler_params=_cp(("parallel",)),
    )(slots)


def _ag2_start(x_ref, out_ref, send_sems, recv_sems, local_sem):
    x, y, c = lax.axis_index("x"), lax.axis_index("y"), lax.axis_index("c")
    me, sibling = (x, y, c), (x, y, 1 - c)
    chips = [(1 - x, y), (x, 1 - y), (1 - x, 1 - y)]

    def rows(px, py, pc):
        return out_ref.at[4 * px + 2 * py + pc]

    def copy(k, block, to, src=None):
        return pltpu.make_async_remote_copy(
            src_ref=rows(*block) if src is None else src, dst_ref=rows(*block),
            send_sem=send_sems.at[k], recv_sem=recv_sems.at[k], device_id=to, device_id_type=MESH)

    mine = pltpu.make_async_copy(x_ref, rows(*me), local_sem)
    mine.start()
    first = [copy(0, me, sibling, src=x_ref)]
    first += [copy(1 + j, me, (*chip, c), src=x_ref) for j, chip in enumerate(chips)]
    for cp in first:
        cp.start()
    return copy, mine, first, me, sibling, chips, c


def _ag2_finish(state):
    copy, mine, first, me, sibling, chips, c = state
    passed = [copy(4 + j, (*chip, c), sibling) for j, chip in enumerate(chips)]
    for j, chip in enumerate(chips):
        copy(1 + j, (*chip, c), me).wait_recv()
        passed[j].start()
    copy(0, sibling, me).wait_recv()
    for j, chip in enumerate(chips):
        copy(4 + j, (*chip, 1 - c), me).wait_recv()
    for cp in first + passed:
        cp.wait_send()
    mine.wait()


def _exchange(arrays, name, gather):
    na = len(arrays)

    def body(*refs):
        cps = _xchg_copies(refs[:na], refs[na:2 * na], *refs[2 * na:], gather=gather)
        _xchg_start(cps)
        _xchg_finish(cps)

    anyspec = pl.BlockSpec(memory_space=pl.ANY)
    return pl.pallas_call(
        body, name=name, out_shape=_xchg_out_shapes(arrays, gather), in_specs=[anyspec] * na,
        out_specs=[anyspec] * na, scratch_shapes=_xchg_scratch(na),
    )(*arrays)


def _entry(c, wg_sh, win_sh, c_ctx, w_ada):
    n = w_ada.shape[1]

    def body(c_ref, g_ref, w_ref, cc_ref, wa_ref, call_ref, gall_ref, wall_ref, ada_ref, part,
             s_send, s_recv, s_loc, w_send, w_recv, w_loc, a_send, a_recv, a_loc):
        big = _ag2_start(w_ref, wall_ref, w_send, w_recv, w_loc)
        small = _xchg_copies([c_ref, g_ref], [call_ref, gall_ref], s_send, s_recv, s_loc, gather=True)
        _xchg_start(small)
        _xchg_finish(small)
        cs = jnp.concatenate([call_ref[:, 0, :], jnp.broadcast_to(cc_ref[...], (8, D))], axis=0)
        part[...] = _nn((cs * _sigmoid(cs)).astype(_BF), wa_ref[...].astype(_BF))
        ada = _xchg_copies([part], [ada_ref], a_send, a_recv, a_loc, gather=True)
        _xchg_start(ada)
        _xchg_finish(ada)
        _ag2_finish(big)

    vm = pl.BlockSpec(memory_space=pltpu.VMEM)
    return pl.pallas_call(
        body, name="entry_gather",
        out_shape=[jax.ShapeDtypeStruct((N_DEV,) + c.shape, F32), jax.ShapeDtypeStruct((N_DEV,) + wg_sh.shape, F32),
                   jax.ShapeDtypeStruct((N_DEV,) + win_sh.shape, win_sh.dtype),
                   jax.ShapeDtypeStruct((N_DEV, 16, n), F32)],
        in_specs=[vm] * 5, out_specs=[vm] * 4,
        scratch_shapes=[pltpu.VMEM((16, n), F32)] + _xchg_scratch(2)
        + [pltpu.SemaphoreType.DMA((7,)), pltpu.SemaphoreType.DMA((7,)), pltpu.SemaphoreType.DMA] + _xchg_scratch(1),
        compiler_params=pltpu.CompilerParams(vmem_limit_bytes=VMEM_BIG),
    )(c, wg_sh, win_sh, c_ctx, w_ada)


def _rope_tables(S):
    t = np.arange(S)
    row = (t // GRID_W).astype(np.float32)
    colp = (t % GRID_W).astype(np.float32)
    half = HD // 2
    inv = (ROPE_BASE ** (-np.arange(0, half, 2, dtype=np.float32) / half)).astype(np.float32)
    ar = row[:, None] * inv[None, :]
    ac = colp[:, None] * inv[None, :]
    ang = np.concatenate([ar, ar, ac, ac], axis=-1).astype(np.float32)
    cos = np.cos(ang).astype(np.float32)
    sin = np.sin(ang).astype(np.float32)
    lane = np.arange(HD)
    first = (lane % 32) < 16
    sa = np.where(first[None, :], -sin, 0.0)
    sb = np.where(first[None, :], 0.0, sin)

    def ext(tab, ctx_val):
        full = np.zeros((CTX + S, 128), np.float32)
        full[:CTX, :] = ctx_val
        full[CTX:, :HD] = tab
        full[CTX:, HD:] = tab
        return jnp.asarray(full)

    return ext(cos, 1.0), ext(sa, 0.0), ext(sb, 0.0)


def _pad_rows_win(wt):
    return jnp.pad(wt, ((0, NP - IN_COLS), (0, 0)))


def _unpad_rows_win(g):
    return g[0:IN_COLS]


def _local_step(x, ctx, target, ada_l, ada_c, gains, sink, win_p, wg_bd, bg, ggla, wout_sh, wffi_sh, wffo_sh):
    S = x.shape[0]
    cos, sa, sb = _rope_tables(S)
    g1, g2, g3, g4 = (gains[i:i + 1] for i in range(4))
    sh1, sc1, gt1, sh2, sc2, gt2 = (ada_l[i:i + 1] for i in range(6))
    sh1c, sc1c = ada_c[0:1], ada_c[1:2]
    gml, gmc, gm2 = g1 * (1.0 + sc1), g1 * (1.0 + sc1c), g3 * (1.0 + sc2)
    mavg = jnp.asarray(np.kron(np.eye(N_GLA, dtype=np.float32), np.full((DV, DV), 1.0 / DV, np.float32))).astype(_BF)

    n_ffi, r_ffo, r_out = wffi_sh.shape[0], wffo_sh.shape[0], wout_sh.shape[0]
    tt_e = 768 if (S + CTX) % 768 == 0 else 256
    tt_s = 512 if S % 512 == 0 else 256
    h, q, k, v, gq, gk, gv, gg, z, la, wout_g = _inproj_fwd(x, ctx, gml, sh1, gmc, sh1c, win_p, wg_bd, bg,
                                                            cos, sa, sb, [wout_sh])
    attn, lse, probs, wffi_g = _attn_fwd(q, k, v, sink, [wffi_sh])
    o_f, st_f, o_b, st_b, wffo_g = _gla_fwd(gq, gk, gv, la, [wffo_sh])
    wout = wout_g.reshape(N_DEV * r_out, D)
    wffi = wffi_g.reshape(N_DEV * n_ffi, D)
    wffo = wffo_g.reshape(N_DEV * r_ffo, D)
    x1, mix = _mix_fwd(x, attn, o_f, o_b, gg, ggla, mavg, wout, gt1, g2)
    dx1, h2, du, act, df, s_ffn, loss = _ffn(x1, target, gm2, sh2, gt2, g4, wffi, wffo)
    slab_ffi = _matmul_tn(h2, du, 512, tt_s, "grad_w_ffn_in", _BF, True).reshape(N_DEV, n_ffi, D)
    tt_l = 1024 if S % 1024 == 0 else tt_s
    slab_ffo = _matmul_tn(act, df, FFN, tt_l, "grad_w_ffn_out", _BF).reshape(N_DEV, r_ffo, D)
    d_attn, do_gla, dgg, dy, s_mix, got_ffo = _mix_bwd(dx1, mix, o_f, o_b, gg, ggla, mavg, wout, gt1, g2, [slab_ffo])
    slab_out = _matmul_tn(mix, dy, D, tt_l, "grad_w_out", _BF).reshape(N_DEV, r_out, D)
    dq, dk, dv, dsink, got_ffi = _attn_bwd(q, k, v, sink, probs, lse, d_attn, [slab_ffi])
    (dgq_f, dgk_f, dgv_f, dlg_f, dgq_b, dgk_b, dgv_b, dlg_b, g_wg, s_bg,
     got_out) = _gla_bwd(gq, gk, gv, la, z, st_f, st_b, do_gla, [slab_out])
    dp, grad_x, s_in = _inproj_bwd(x, ctx, gml, gmc, win_p, wg_bd, cos, sa, sb, dq, dk, dv,
                                   (dgq_f, dgq_b), (dgk_f, dgk_b), (dgv_f, dgv_b), dgg, dlg_f, dlg_b, dx1)
    small = _small_grads(s_in, s_ffn, s_mix, ada_l, ada_c, gains, dsink, s_bg, g_wg, loss)
    n_in, n_grp = IN_COLS // N_DEV, 2
    got_in, slab = [], None
    for j in range(n_grp):
        g_j, got = _matmul_tn(h, dp, D // n_grp, tt_e, "grad_w_in_%d" % j, _BF, True, a_cols=j,
                              hosted=[small] if j == 0 else [slab], gather=(j == 0))
        if j == 0:
            parts = got
        else:
            got_in.append(got)
        slab = _unpad_rows_win(g_j).reshape(N_DEV, n_in, D // n_grp)
    got_in.append(_exchange([slab], "scatter_grads", False)[0])
    return dict(grad_x=grad_x, got_in=got_in, got_out=got_out, got_ffi=got_ffi, got_ffo=got_ffo, parts=parts)


SMALL_NAMES = ["c_ctx", "b_ada", "g_pre_mix", "g_post_mix", "g_pre_ffn", "g_post_ffn", "attn_sink",
               "b_gate_fwd", "b_gate_bwd", "g_gla_norm", "w_gate_fwd", "w_gate_bwd"]


def _small_update(tot, t_tot, wg_g, w, m, v):
    c1 = 1.0 / (1.0 - ADAM_B1 ** ADAM_STEP)
    c2 = 1.0 / (1.0 - ADAM_B2 ** ADAM_STEP)
    n = len(SMALL_NAMES)

    def body(tot_ref, t_ref, wg_ref, *refs):
        w_r, m_r, v_r = refs[0:n], refs[n:2 * n], refs[2 * n:3 * n]
        g_o, d_o, nm_o, nv_o = refs[3 * n:4 * n], refs[4 * n:5 * n], refs[5 * n:6 * n], refs[6 * n:7 * n]

        def upd(i, idx, g):
            nm = ADAM_B1 * m_r[i][idx] + (1.0 - ADAM_B1) * g
            nv = ADAM_B2 * v_r[i][idx] + (1.0 - ADAM_B2) * (g * g)
            g_o[i][idx] = g
            nm_o[i][idx] = nm
            nv_o[i][idx] = nv
            d_o[i][idx] = -ADAM_LR * ((nm * c1) / (jnp.sqrt(nv * c2) + ADAM_EPS) + ADAM_WD * w_r[i][idx])

        everything = (slice(None), slice(None))
        cc = w_r[0][...]
        sc = _sigmoid(cc)
        upd(0, everything, t_ref[0:1, :] * (sc * (1.0 + cc * (1.0 - sc))))
        for j in range(6):
            upd(1, (slice(None), slice(D * j, D * j + D)),
                tot_ref[R_ADA + j:R_ADA + j + 1, :] + tot_ref[R_ADA_C + j:R_ADA_C + j + 1, :])
        for j in range(4):
            upd(2 + j, everything, tot_ref[R_GAIN + j:R_GAIN + j + 1, :])
        upd(6, everything, tot_ref[R_SINK:R_SINK + 1, 0:N_ATT])
        upd(7, everything, tot_ref[R_BG:R_BG + 1, 0:256])
        upd(8, everything, tot_ref[R_BG:R_BG + 1, 256:512])
        upd(9, everything, tot_ref[R_GGLA:R_GGLA + 1, 0:DV])
        upd(10, (0,), wg_ref[0:GATE_RANK, :])
        upd(11, (0,), wg_ref[GATE_RANK:2 * GATE_RANK, :])

    params = [w[k] for k in SMALL_NAMES] + [m[k] for k in SMALL_NAMES] + [v[k] for k in SMALL_NAMES]
    outs = pl.pallas_call(
        body, name="small_update", grid=(1,),
        in_specs=[_full(tot.shape), _full(t_tot.shape), _full(wg_g.shape)] + [_full(p.shape) for p in params],
        out_specs=[_full(w[k].shape) for k in SMALL_NAMES] * 4,
        out_shape=[jax.ShapeDtypeStruct(w[k].shape, F32) for k in SMALL_NAMES] * 4,
        compiler_params=_cp(("arbitrary",)),
    )(tot, t_tot, wg_g, *params)
    return tuple(dict(zip(SMALL_NAMES, outs[i * n:(i + 1) * n])) for i in range(4))


def kernel(x, c, ctx, c_ctx, w_ada, b_ada, g_pre_mix, g_post_mix, g_pre_ffn, g_post_ffn, w_in, attn_sink, w_gate_fwd, b_gate_fwd, w_gate_bwd, b_gate_bwd, g_gla_norm, w_out, w_ffn_in, w_ffn_out, loss_target, m_c_ctx, m_w_ada, m_b_ada, m_g_pre_mix, m_g_post_mix, m_g_pre_ffn, m_g_post_ffn, m_w_in, m_attn_sink, m_w_gate_fwd, m_b_gate_fwd, m_w_gate_bwd, m_b_gate_bwd, m_g_gla_norm, m_w_out, m_w_ffn_in, m_w_ffn_out, v_c_ctx, v_w_ada, v_b_ada, v_g_pre_mix, v_g_post_mix, v_g_pre_ffn, v_g_post_ffn, v_w_in, v_attn_sink, v_w_gate_fwd, v_b_gate_fwd, v_w_gate_bwd, v_b_gate_bwd, v_g_gla_norm, v_w_out, v_w_ffn_in, v_w_ffn_out):
    me = 4 * lax.axis_index("x") + 2 * lax.axis_index("y") + lax.axis_index("c")
    S = x.shape[1]
    n_in = w_in.shape[2]
    n_ffi = w_ffn_in.shape[2]
    r_out = w_out.shape[1]
    r_ffo = w_ffn_out.shape[1]
    n_ada = w_ada.shape[2]

    wg_sh = jnp.concatenate([w_gate_fwd.reshape(4, 128), w_gate_bwd.reshape(4, 128)], axis=0)
    c_all3, g_all, w_all, ada_all = _entry(c, wg_sh, w_in[0].T.astype(_BF), c_ctx.reshape(1, D), w_ada[0])
    c_all = c_all3.reshape(N_DEV, D)
    wgf = g_all[:, 0:4].reshape(N_DEV, GATE_RANK, 32).transpose(1, 0, 2).reshape(GATE_RANK, 256)
    wgb = g_all[:, 4:8].reshape(N_DEV, GATE_RANK, 32).transpose(1, 0, 2).reshape(GATE_RANK, 256)
    win_p = _pad_rows_win(w_all.reshape(N_DEV * n_in, D))
    wg_bd = jnp.zeros((128, 512), F32).at[0:16, 0:256].set(wgf).at[16:32, 256:512].set(wgb).astype(_BF)
    ada_full = ada_all.transpose(1, 0, 2).reshape(16, N_DEV * n_ada) + b_ada
    ada_l = jnp.pad(lax.dynamic_slice_in_dim(ada_full, me, 1, 0).reshape(6, D), ((0, 2), (0, 0)))
    ada_c = jnp.pad(ada_full[8].reshape(6, D), ((0, 2), (0, 0)))
    gains = jnp.pad(jnp.concatenate([g_pre_mix, g_post_mix, g_pre_ffn, g_post_ffn], axis=0), ((0, 4), (0, 0)))
    sink = jnp.broadcast_to(attn_sink.reshape(8, 1), (8, 128))
    bg = jnp.concatenate([b_gate_fwd, b_gate_bwd], axis=1)
    ggla = jnp.tile(g_gla_norm, (1, N_GLA))

    r = _local_step(x[0], ctx[0], loss_target[0], ada_l, ada_c, gains, sink, win_p, wg_bd, bg, ggla,
                    w_out[0].astype(_BF), w_ffn_in[0].T.astype(_BF), w_ffn_out[0].astype(_BF))

    parts = r["parts"]
    tot = _sum_slots(parts, "sum_small_grads")
    loss = tot[R_LOSS, 0]
    d_ada_rows = parts[:, R_ADA:R_ADA + 6].reshape(N_DEV, 6 * D)
    d_ada_c = tot[R_ADA_C:R_ADA_C + 6].reshape(1, 6 * D)
    my_cols = lax.dynamic_slice_in_dim(jnp.concatenate([d_ada_rows, jnp.broadcast_to(d_ada_c, (1, 6 * D)),
                                                        jnp.zeros((7, 6 * D), F32)], axis=0), me * n_ada, n_ada, 1)
    grad_w_ada, t_part = _ada_bwd(c_all, c_ctx.reshape(1, D), w_ada[0], my_cols)
    wg_g = lax.dynamic_slice(tot, (R_WG, me * 32), (2 * GATE_RANK, 32))

    tr = lambda a: jnp.transpose(a, (0, 2, 1))
    big = {}
    t_all, = _exchange([t_part], "gather_c_ctx", True)
    t_tot = _sum_slots(t_all, "sum_c_ctx")
    for nm, w, g, m, v in [("w_ada", w_ada, grad_w_ada, m_w_ada, v_w_ada),
                           ("w_out", w_out, r["got_out"], m_w_out, v_w_out),
                           ("w_ffn_out", w_ffn_out, r["got_ffo"], m_w_ffn_out, v_w_ffn_out)]:
        big[nm] = _adamw(w, [g], m, v, "adamw_" + nm)
    big["w_ffn_in"] = tuple(tr(o) for o in _adamw(tr(w_ffn_in), [r["got_ffi"]], tr(m_w_ffn_in), tr(v_w_ffn_in),
                                                  "adamw_w_ffn_in"))
    big["w_in"] = tuple(tr(o) for o in _adamw(tr(w_in), r["got_in"], tr(m_w_in), tr(v_w_in), "adamw_w_in"))

    w_small = dict(c_ctx=c_ctx.reshape(1, D), b_ada=b_ada, g_pre_mix=g_pre_mix, g_post_mix=g_post_mix, g_pre_ffn=g_pre_ffn,
                   g_post_ffn=g_post_ffn, attn_sink=attn_sink, b_gate_fwd=b_gate_fwd, b_gate_bwd=b_gate_bwd,
                   g_gla_norm=g_gla_norm, w_gate_fwd=w_gate_fwd, w_gate_bwd=w_gate_bwd)
    m_small = dict(c_ctx=m_c_ctx.reshape(1, D), b_ada=m_b_ada, g_pre_mix=m_g_pre_mix, g_post_mix=m_g_post_mix,
                   g_pre_ffn=m_g_pre_ffn, g_post_ffn=m_g_post_ffn, attn_sink=m_attn_sink, b_gate_fwd=m_b_gate_fwd,
                   b_gate_bwd=m_b_gate_bwd, g_gla_norm=m_g_gla_norm, w_gate_fwd=m_w_gate_fwd, w_gate_bwd=m_w_gate_bwd)
    v_small = dict(c_ctx=v_c_ctx.reshape(1, D), b_ada=v_b_ada, g_pre_mix=v_g_pre_mix, g_post_mix=v_g_post_mix,
                   g_pre_ffn=v_g_pre_ffn, g_post_ffn=v_g_post_ffn, attn_sink=v_attn_sink, b_gate_fwd=v_b_gate_fwd,
                   b_gate_bwd=v_b_gate_bwd, g_gla_norm=v_g_gla_norm, w_gate_fwd=v_w_gate_fwd, w_gate_bwd=v_w_gate_bwd)
    grads_small, d_s, nm_s, nv_s = _small_update(tot, t_tot, wg_g, w_small, m_small, v_small)
    for dd in (grads_small, d_s, nm_s, nv_s):
        dd["c_ctx"] = dd["c_ctx"].reshape(D)

    order = ["c_ctx", "w_ada", "b_ada", "g_pre_mix", "g_post_mix", "g_pre_ffn", "g_post_ffn", "w_in", "attn_sink",
             "w_gate_fwd", "b_gate_fwd", "w_gate_bwd", "b_gate_bwd", "g_gla_norm", "w_out", "w_ffn_in", "w_ffn_out"]
    grads, deltas, new_m, new_v = [], [], [], []
    for nm in order:
        if nm in big:
            g_, d_, m_, v_ = big[nm]
        else:
            g_, d_, m_, v_ = grads_small[nm], d_s[nm], nm_s[nm], nv_s[nm]
        grads.append(g_)
        deltas.append(d_)
        new_m.append(m_)
        new_v.append(v_)
    return (loss, r["grad_x"][None], *grads, *deltas, *new_m, *new_v)
```

```python
import functools
import math

import numpy as np
import jax
import jax.numpy as jnp
from jax import lax
from jax.experimental import pallas as pl
from jax.experimental.pallas import tpu as pltpu

F32 = jnp.float32
_BF = jnp.bfloat16

N_DEV = 8
D = 1024
CTX = 256
HD = 64
N_ATT = 8
N_KV = 2
GRP = N_ATT // N_KV
WIN = 128
GRID_W = 64
ROPE_BASE = 10000.0
N_GLA = 8
DK = 32
DV = 64
GATE_RANK = 16
GATE_TAU = 16.0
FFN = 2816
EPS = 1e-6
NEG = -1e30
GLA_T = 128

QP = N_ATT * HD
KP = N_KV * HD
O_Q, O_K, O_V = 0, QP, QP + KP
O_GQ = O_V + KP
O_GK = O_GQ + N_GLA * DK
O_GV = O_GK + N_GLA * DK
O_GG = O_GV + N_GLA * DV
O_Z = O_GG + N_GLA * DV
NP = O_Z + 128
IN_COLS = 2336

ADAM_LR, ADAM_B1, ADAM_B2, ADAM_EPS, ADAM_WD, ADAM_STEP = 0.001, 0.9, 0.999, 1e-08, 0.01, 10

VMEM_BIG = 56 * 1024 * 1024
MESH = pl.DeviceIdType.MESH


def _cp(sem, vmem=None):
    return pltpu.CompilerParams(dimension_semantics=sem, vmem_limit_bytes=vmem)


def _full(shape):
    nd = len(shape)
    return pl.BlockSpec(shape, lambda *a: (0,) * nd)


def _rows(tile, width, off=0):
    return pl.BlockSpec((tile, width), lambda i: (i + off, 0))


def _rows_lat(tile, width):
    return pl.BlockSpec((tile, width), lambda i: (jnp.maximum(i - 1, 0), 0))


def _nt(a, b):
    return lax.dot_general(a, b, (((1,), (1,)), ((), ())), preferred_element_type=F32)


def _tn(a, b):
    return lax.dot_general(a, b, (((0,), (0,)), ((), ())), preferred_element_type=F32)


def _nn(a, b):
    return jnp.dot(a, b, preferred_element_type=F32)


def _head_mean(x, mavg):
    n = x.shape[0]
    hi = x.astype(_BF)
    lo = (x - hi.astype(F32)).astype(_BF)
    y = _nn(jnp.concatenate([hi, lo], axis=0), mavg)
    return y[0:n] + y[n:2 * n]


def _rope(t, cos, sa, sb):
    n = t.shape[1]
    reps = n // 128
    c = jnp.tile(cos, (1, reps))
    a = jnp.tile(sa, (1, reps))
    b = jnp.tile(sb, (1, reps))
    return t * c + pltpu.roll(t, n - 16, 1) * a + pltpu.roll(t, 16, 1) * b


def _unrope(t, cos, sa, sb):
    n = t.shape[1]
    reps = n // 128
    c = jnp.tile(cos, (1, reps))
    a = jnp.tile(sa, (1, reps))
    b = jnp.tile(sb, (1, reps))
    return t * c + pltpu.roll(t * a, 16, 1) + pltpu.roll(t * b, n - 16, 1)


def _sigmoid(x):
    return 1.0 / (1.0 + jnp.exp(-x))


def _inproj_fwd(x, ctx, gml, shl, gmc, shc, win, wg, bg, cos, sa, sb, shards):
    E = x.shape[0] + CTX
    TE = CTX

    def body(x_ref, c_ref, gml_ref, shl_ref, gmc_ref, shc_ref, w_ref, wg_ref, bg_ref, cos_ref, sa_ref, sb_ref,
             h_ref, q_ref, k_ref, v_ref, gq_ref, gk_ref, gv_ref, gg_ref, z_ref, la_ref):
        is_ctx = pl.program_id(0) == 0
        gm = jnp.where(is_ctx, gmc_ref[...], gml_ref[...])
        sh = jnp.where(is_ctx, shc_ref[...], shl_ref[...])
        x = jnp.where(is_ctx, c_ref[...], x_ref[...])
        r = lax.rsqrt(jnp.mean(x * x, axis=-1, keepdims=True) + EPS)
        hb = ((x * r) * gm + sh).astype(_BF)
        h_ref[...] = hb
        p = _nt(hb, w_ref[...])
        cos_t, sa_t, sb_t = cos_ref[...], sa_ref[...], sb_ref[...]
        q_ref[...] = (_rope(p[:, O_Q:O_K], cos_t, sa_t, sb_t) * (HD ** -0.5)).astype(_BF)
        k_ref[...] = _rope(p[:, O_K:O_V], cos_t, sa_t, sb_t).astype(_BF)
        v_ref[...] = p[:, O_V:O_GQ].astype(_BF)
        gq_ref[...] = p[:, O_GQ:O_GK] * (DK ** -0.5)
        gk_ref[...] = p[:, O_GK:O_GV]
        gv_ref[...] = p[:, O_GV:O_GG]
        gg_ref[...] = p[:, O_GG:O_Z]
        zb = p[:, O_Z:NP].astype(_BF)
        z_ref[...] = zb
        lg = _nn(zb, wg_ref[...]) + bg_ref[...]
        la_ref[...] = (jnp.minimum(lg, 0.0) - jnp.log(1.0 + jnp.exp(-jnp.abs(lg)))) * (1.0 / GATE_TAU)

    vec = _full((1, D))
    tab = _rows(TE, 128)
    outs = [(D, _BF), (QP, _BF), (KP, _BF), (KP, _BF), (256, F32), (256, F32), (512, F32), (512, F32),
            (128, _BF), (512, F32)]
    return _hosted_call(
        body, (x, ctx, gml, shl, gmc, shc, win, wg, bg, cos, sa, sb), shards, True,
        name="inproj_fwd", grid=(E // TE,),
        in_specs=[_rows_lat(TE, D), _full((CTX, D)), vec, vec, vec, vec, _full((NP, D)), _full((128, 512)),
                  _full((1, 512)), tab, tab, tab],
        out_specs=[_rows_lat(TE, w) if i == 7 else _rows(TE, w) for i, (w, _) in enumerate(outs)],
        out_shape=[jax.ShapeDtypeStruct((E - CTX if i == 7 else E, w), dt) for i, (w, dt) in enumerate(outs)],
        compiler_params=_cp(("arbitrary",), 40 * 1024 * 1024))


def _xchg_scratch(na):
    return [pltpu.SemaphoreType.DMA((na, N_DEV - 1)), pltpu.SemaphoreType.DMA((na, N_DEV - 1)),
            pltpu.SemaphoreType.DMA((na,))]


def _xchg_copies(ins, outs, send_sems, recv_sems, local_sems, gather):
    x, y, c = lax.axis_index("x"), lax.axis_index("y"), lax.axis_index("c")
    me = 4 * x + 2 * y + c
    local, sends, recvs = [], [], []
    for a in range(len(ins)):
        local.append(pltpu.make_async_copy(ins[a] if gather else ins[a].at[me], outs[a].at[me], local_sems.at[a]))
    for k in range(1, N_DEV):
        px, py, pc = x ^ (k >> 2), y ^ ((k >> 1) & 1), c ^ (k & 1)
        peer = 4 * px + 2 * py + pc
        for a in range(len(ins)):
            sems = dict(send_sem=send_sems.at[a, k - 1], recv_sem=recv_sems.at[a, k - 1], device_id_type=MESH)
            sends.append(pltpu.make_async_remote_copy(
                src_ref=ins[a] if gather else ins[a].at[peer], dst_ref=outs[a].at[me], device_id=(px, py, pc), **sems))
            recvs.append(pltpu.make_async_remote_copy(
                src_ref=ins[a] if gather else ins[a].at[me], dst_ref=outs[a].at[peer], device_id=(x, y, c), **sems))
    return local, sends, recvs


def _xchg_start(cps):
    local, sends, _ = cps
    for cp in local + sends:
        cp.start()


def _xchg_finish(cps):
    local, sends, recvs = cps
    for cp in recvs:
        cp.wait_recv()
    for cp in sends:
        cp.wait_send()
    for cp in local:
        cp.wait()


def _xchg_out_shapes(ins, gather):
    return [jax.ShapeDtypeStruct(((N_DEV,) + s.shape) if gather else s.shape, s.dtype) for s in ins]


def _hosted_call(body, args, hosted, gather, *, grid, in_specs, out_specs, out_shape, scratch_shapes=(), **kw):
    na = len(hosted)
    if na == 0:
        return pl.pallas_call(body, grid=grid, in_specs=in_specs, out_specs=out_specs, out_shape=out_shape,
                              scratch_shapes=list(scratch_shapes), **kw)(*args)
    n_in, n_out, n_scr = len(in_specs), len(out_specs), len(scratch_shapes)

    def wrapped(*refs):
        ins, h_in = refs[:n_in], refs[n_in:n_in + na]
        outs, h_out = refs[n_in + na:n_in + na + n_out], refs[n_in + na + n_out:n_in + 2 * na + n_out]
        scr = refs[n_in + 2 * na + n_out:]
        cps = _xchg_copies(h_in, h_out, *scr[n_scr:], gather=gather)
        pids = [pl.program_id(a) for a in range(len(grid))]
        first = functools.reduce(jnp.logical_and, [p == 0 for p in pids])
        last = functools.reduce(jnp.logical_and, [p == g - 1 for p, g in zip(pids, grid)])

        @pl.when(first)
        def _():
            _xchg_start(cps)

        body(*ins, *outs, *scr[:n_scr])

        @pl.when(last)
        def _():
            _xchg_finish(cps)

    anyspec = pl.BlockSpec(memory_space=pl.ANY)
    return pl.pallas_call(
        wrapped, grid=grid, in_specs=list(in_specs) + [anyspec] * na, out_specs=list(out_specs) + [anyspec] * na,
        out_shape=list(out_shape) + _xchg_out_shapes(hosted, gather),
        scratch_shapes=list(scratch_shapes) + _xchg_scratch(na), **kw)(*args, *hosted)


ATT_BLOCKS = 2
PROB_ROWS = N_KV * GRP * WIN


def _attn_specs(E):
    nb = (E - CTX) // WIN
    last = E // WIN - 1
    kc = pl.BlockSpec((CTX, KP), lambda m: (0, 0))
    ks = [pl.BlockSpec((WIN, KP), functools.partial(lambda m, j: (jnp.minimum(ATT_BLOCKS * m + j, last), 0), j=j))
          for j in range(1, ATT_BLOCKS + 3)]
    return nb, [kc] + ks


def _attn_bias(nb):
    rows = np.arange(GRP * WIN)[:, None] % WIN
    cols = np.arange(CTX + 3 * WIN)[None, :]
    j = cols - CTX
    band = np.abs(j - WIN - rows) <= WIN
    out = []
    for first, last in ((True, False), (False, False), (False, True)):
        ok = (cols < CTX) | (band & ((j >= WIN) | (not first)) & ((j < 2 * WIN) | (not last)))
        out.append(np.where(ok, 0.0, NEG).astype(np.float32))
    bias = jnp.asarray(np.stack(out))
    steps = nb // ATT_BLOCKS
    shape = (1, GRP * WIN, CTX + 3 * WIN)
    specs = [pl.BlockSpec(shape, lambda m: (jnp.where(m == 0, 0, 1), 0, 0))]
    specs += [pl.BlockSpec(shape, lambda m: (1, 0, 0))] * (ATT_BLOCKS - 2)
    specs += [pl.BlockSpec(shape, lambda m: (jnp.where(m == steps - 1, 2, 1), 0, 0))]
    return bias, specs


def _both_halves(t, h):
    tf = t.astype(F32)
    r = pltpu.roll(tf, HD, 1)
    lo = lax.broadcasted_iota(jnp.int32, tf.shape, 1) < HD
    return (jnp.where(lo, tf, r) if h == 0 else jnp.where(lo, r, tf)).astype(t.dtype)


def _stack_heads(ref, h):
    lo = lax.broadcasted_iota(jnp.int32, (WIN, 128), 1) < HD
    parts = []
    for g in range(GRP):
        j = GRP * h + g
        t = ref[:, 128 * (j // 2):128 * (j // 2) + 128].astype(F32)
        parts.append(jnp.where(lo if j % 2 == 0 else jnp.logical_not(lo), t, 0.0))
    return jnp.concatenate(parts, axis=0)


def _unstack_pair(o, pp):
    lo = lax.broadcasted_iota(jnp.int32, (WIN, 128), 1) < HD
    return jnp.where(lo, o[WIN * 2 * pp:WIN * 2 * pp + WIN], o[WIN * (2 * pp + 1):WIN * (2 * pp + 1) + WIN])


def _attn_fwd(q, k, v, sink, shards):
    E = q.shape[0]
    S = E - CTX
    nb, kspecs = _attn_specs(E)
    na = len(shards)

    nk = ATT_BLOCKS + 3

    def one_block(q_ref, kw, vw, sink_ref, bias_ref, o_ref, lse_ref, p_ref):
        lane = lax.broadcasted_iota(jnp.int32, (WIN, 128), 1)
        lse_t = jnp.zeros((WIN, 128), F32)
        kall = jnp.concatenate([r[...] for r in kw], axis=0)
        vall = jnp.concatenate([r[...] for r in vw], axis=0)
        K = [_both_halves(kall, h) for h in range(N_KV)]
        Q = [_stack_heads(q_ref, h).astype(_BF) for h in range(N_KV)]
        sk = [jnp.concatenate([jnp.broadcast_to(sink_ref[GRP * h + g:GRP * h + g + 1, 0:1], (WIN, 1))
                               for g in range(GRP)], axis=0) for h in range(N_KV)]
        s = [_nt(Q[h], K[h]) + bias_ref[0] for h in range(N_KV)]
        m = [jnp.maximum(jnp.max(s[h], axis=1, keepdims=True), sk[h]) for h in range(N_KV)]
        e = [jnp.exp(s[h] - m[h]) for h in range(N_KV)]
        den = [jnp.sum(e[h], axis=1, keepdims=True) + jnp.exp(sk[h] - m[h]) for h in range(N_KV)]
        V = [_both_halves(vall, h) for h in range(N_KV)]
        pb = [(e[h] * (1.0 / den[h])).astype(_BF) for h in range(N_KV)]
        for h in range(N_KV):
            p_ref[GRP * WIN * h:GRP * WIN * (h + 1), :] = pb[h]
        o = [_nn(pb[h], V[h]) for h in range(N_KV)]
        for h in range(N_KV):
            lse = m[h] + jnp.log(den[h])
            for g in range(GRP):
                lse_t = jnp.where(lane == GRP * h + g, lse[WIN * g:WIN * g + WIN], lse_t)
            for pp in range(GRP // 2):
                t = 2 * h + pp
                o_ref[:, 128 * t:128 * t + 128] = _unstack_pair(o[h], pp).astype(_BF)
        lse_ref[...] = lse_t

    def body(q_ref, *rest):
        kr, vr, sink_ref = rest[0:nk], rest[nk:2 * nk], rest[2 * nk]
        bias_refs = rest[2 * nk + 1:2 * nk + 1 + ATT_BLOCKS]
        rest = rest[2 * nk + 1 + ATT_BLOCKS:]
        shard_refs, (o_ref, lse_ref, p_ref), got_refs = rest[:na], rest[na:na + 3], rest[na + 3:2 * na + 3]
        cps = _xchg_copies(shard_refs, got_refs, *rest[2 * na + 3:], gather=True)

        @pl.when(pl.program_id(0) == 0)
        def _():
            _xchg_start(cps)

        for j in range(ATT_BLOCKS):
            rows = pl.ds(WIN * j, WIN)
            one_block(q_ref.at[rows], [kr[0]] + list(kr[1 + j:4 + j]), [vr[0]] + list(vr[1 + j:4 + j]), sink_ref,
                      bias_refs[j], o_ref.at[rows], lse_ref.at[rows], p_ref.at[pl.ds(PROB_ROWS * j, PROB_ROWS)])

        @pl.when(pl.program_id(0) == nb // ATT_BLOCKS - 1)
        def _():
            _xchg_finish(cps)

    tq = ATT_BLOCKS * WIN
    qs = pl.BlockSpec((tq, QP), lambda m: (m + CTX // tq, 0))
    anyspec = pl.BlockSpec(memory_space=pl.ANY)
    bias, bias_specs = _attn_bias(nb)
    return pl.pallas_call(
        body, name="attn_fwd", grid=(nb // ATT_BLOCKS,),
        in_specs=[qs] + kspecs + kspecs + [_full((8, 128))] + bias_specs + [anyspec] * na,
        out_specs=[_rows(tq, 512), _rows(tq, 128), _rows(ATT_BLOCKS * PROB_ROWS, CTX + 3 * WIN)] + [anyspec] * na,
        out_shape=[jax.ShapeDtypeStruct((S, 512), _BF), jax.ShapeDtypeStruct((S, 128), F32),
                   jax.ShapeDtypeStruct((nb * PROB_ROWS, CTX + 3 * WIN), _BF)]
        + _xchg_out_shapes(shards, True),
        scratch_shapes=_xchg_scratch(na),
        compiler_params=_cp(("arbitrary",), 48 * 1024 * 1024),
    )(q, *([k] * nk), *([v] * nk), sink, *([bias] * ATT_BLOCKS), *shards)


def _attn_bwd(q, k, v, sink, probs, lse, d_attn, slabs):
    E = q.shape[0]
    S = E - CTX
    nb, kspecs = _attn_specs(E)
    last = E // WIN - 1
    na = len(slabs)

    nk = ATT_BLOCKS + 3

    def one_block(n, q_ref, kw, vw, sink_ref, p_ref, lse_ref, do_ref, dq_ref, dk_ref, dv_ref, ds_ref):
        lane = lax.broadcasted_iota(jnp.int32, (WIN, 128), 1)
        lse_t = lse_ref[...]
        starts = [None, pl.multiple_of((n + 1) * WIN, WIN), pl.multiple_of((n + 2) * WIN, WIN),
                  pl.multiple_of(jnp.minimum(n + 3, last) * WIN, WIN)]
        kall = jnp.concatenate([r[...] for r in kw], axis=0)
        vall = jnp.concatenate([r[...] for r in vw], axis=0)
        for h in range(N_KV):
            hs = slice(HD * h, HD * h + HD)
            K = _both_halves(kall, h)
            V = _both_halves(vall, h)
            Q = _stack_heads(q_ref, h).astype(_BF)
            sk = jnp.concatenate([jnp.broadcast_to(sink_ref[GRP * h + g:GRP * h + g + 1, 0:1], (WIN, 1))
                                  for g in range(GRP)], axis=0)
            ls = jnp.concatenate([jnp.sum(jnp.where(lane == GRP * h + g, lse_t, 0.0), axis=1, keepdims=True)
                                  for g in range(GRP)], axis=0)
            do = _stack_heads(do_ref, h).astype(_BF)
            pb = p_ref[GRP * WIN * h:GRP * WIN * (h + 1), :]
            p = pb.astype(F32)
            dp = _nt(do, V)
            delta = jnp.sum(p * dp, axis=1, keepdims=True)
            dsc = (p * (dp - delta)).astype(_BF)
            dq = _nn(dsc, K) * (HD ** -0.5)
            for pp in range(GRP // 2):
                t = 2 * h + pp
                dq_ref[:, 128 * t:128 * t + 128] = _unstack_pair(dq, pp).astype(_BF)
            dK2 = _tn(Q, dsc)
            dV2 = _tn(do, pb)
            dK = dK2[0:HD] + dK2[HD:2 * HD]
            dV = dV2[0:HD] + dV2[HD:2 * HD]
            dk_ref[hs, 0:CTX] += dK[:, 0:CTX]
            dv_ref[hs, 0:CTX] += dV[:, 0:CTX]
            for w in range(1, 4):
                lo = CTX + WIN * (w - 1)
                dk_ref[hs, pl.ds(starts[w], WIN)] += dK[:, lo:lo + WIN]
                dv_ref[hs, pl.ds(starts[w], WIN)] += dV[:, lo:lo + WIN]
            psk = -jnp.exp(sk - ls) * delta
            for g in range(GRP):
                j = GRP * h + g
                tot = jnp.sum(psk[WIN * g:WIN * g + WIN], axis=0, keepdims=True)
                ds_ref[j:j + 1, :] += jnp.broadcast_to(tot, (1, 128))

    def body(q_ref, *rest):
        kr, vr = rest[0:nk], rest[nk:2 * nk]
        sink_ref, p_ref, lse_ref, do_ref = rest[2 * nk:2 * nk + 4]
        rest = rest[2 * nk + 4:]
        slab_refs, (dq_ref, dk_ref, dv_ref, ds_ref), got_refs = rest[:na], rest[na:na + 4], rest[na + 4:2 * na + 4]
        m = pl.program_id(0)
        cps = _xchg_copies(slab_refs, got_refs, *rest[2 * na + 4:], gather=False)

        @pl.when(m == 0)
        def _():
            _xchg_start(cps)
            dk_ref[...] = jnp.zeros_like(dk_ref)
            dv_ref[...] = jnp.zeros_like(dv_ref)
            ds_ref[...] = jnp.zeros_like(ds_ref)

        for j in range(ATT_BLOCKS):
            rows = pl.ds(WIN * j, WIN)
            one_block(ATT_BLOCKS * m + j, q_ref.at[rows], [kr[0]] + list(kr[1 + j:4 + j]),
                      [vr[0]] + list(vr[1 + j:4 + j]), sink_ref, p_ref.at[pl.ds(PROB_ROWS * j, PROB_ROWS)],
                      lse_ref.at[rows], do_ref.at[rows], dq_ref.at[rows], dk_ref, dv_ref, ds_ref)

        @pl.when(m == nb // ATT_BLOCKS - 1)
        def _():
            _xchg_finish(cps)

    tq = ATT_BLOCKS * WIN
    qs = pl.BlockSpec((tq, QP), lambda m: (m + CTX // tq, 0))
    anyspec = pl.BlockSpec(memory_space=pl.ANY)
    return pl.pallas_call(
        body, name="attn_bwd", grid=(nb // ATT_BLOCKS,),
        in_specs=[qs] + kspecs + kspecs + [_full((8, 128)), _rows(ATT_BLOCKS * PROB_ROWS, CTX + 3 * WIN),
                                            _rows(tq, 128), _rows(tq, 512)] + [anyspec] * na,
        out_specs=[_rows(tq, QP), _full((KP, E)), _full((KP, E)), _full((8, 128))] + [anyspec] * na,
        out_shape=[jax.ShapeDtypeStruct((S, QP), _BF), jax.ShapeDtypeStruct((KP, E), F32),
                   jax.ShapeDtypeStruct((KP, E), F32), jax.ShapeDtypeStruct((8, 128), F32)]
        + _xchg_out_shapes(slabs, False),
        scratch_shapes=_xchg_scratch(na),
        compiler_params=_cp(("arbitrary",), 48 * 1024 * 1024),
    )(q, *([k] * nk), *([v] * nk), sink, probs, lse, d_attn, *slabs)


GLA_STEP = 2


def _gla_subs(reverse, backward):
    subs = list(range(GLA_STEP))
    return subs[::-1] if reverse != backward else subs


def _gla_order(E, reverse, backward):
    nc = CTX // (GLA_STEP * GLA_T)
    n = E // (GLA_STEP * GLA_T)
    if not reverse:
        fwd = lambda s: s
    else:
        fwd = lambda s: jnp.where(s < nc, nc - 1 - s, n - 1 + nc - s)
    if backward:
        return lambda s: fwd(n - 1 - s)
    return fwd


def _gla_masks():
    T = GLA_T
    l128 = lax.broadcasted_iota(jnp.int32, (1, 128), 1)
    qmask = [((l128 >> 5) == j).astype(F32) for j in range(4)]
    vmask = [((l128 >> 6) == j).astype(F32) for j in range(2)]
    bd = ((lax.broadcasted_iota(jnp.int32, (512, 256), 0) >> 6)
          == (lax.broadcasted_iota(jnp.int32, (512, 256), 1) >> 5)).astype(F32)
    ri = lax.broadcasted_iota(jnp.int32, (T, 2 * T), 0)
    ci = lax.broadcasted_iota(jnp.int32, (T, 2 * T), 1) & (T - 1)
    return qmask, vmask, bd, ri, ci


def _tri_sum(tri, x):
    hi = x.astype(_BF)
    lo = (x - hi.astype(F32)).astype(_BF)
    n = x.shape[1]
    y = _nn(tri.astype(_BF), jnp.concatenate([hi, lo], axis=1))
    return y[:, 0:n] + y[:, n:2 * n]


def _gla_decays(la, reverse, ri, ci):
    T = GLA_T
    msk2 = (ri <= ci) if reverse else (ri >= ci)
    mskT2 = (ri >= ci) if reverse else (ri <= ci)
    b = _tri_sum(msk2[:, 0:T], la)
    bT = b[0:1] if reverse else b[T - 1:T]
    bm = b[T // 2:T // 2 + 1]
    return msk2, mskT2, b, bT, bm


def _pair_stack(tile, m0, m1):
    return jnp.concatenate([(tile * m0).astype(_BF), (tile * m1).astype(_BF)], axis=0)


def _gla_fwd(gq, gk, gv, la, shards=()):
    E = gq.shape[0]
    T = GLA_T
    n = E // T
    TB = GLA_STEP * T
    orders = [_gla_order(E, False, False), _gla_order(E, True, False)]

    def one_direction(reverse, *refs):
        for j in _gla_subs(reverse, False):
            rows = pl.ds(T * j, T)
            one_chunk(reverse, *[r.at[rows] for r in refs[:5]], refs[5].at[j], refs[6])

    def one_chunk(reverse, gq_ref, gk_ref, gv_ref, la_ref, o_ref, st_ref, S_scr):
        qmask, vmask, bd, ri, ci = _gla_masks()
        msk2, _, b, bT, bm = _gla_decays(la_ref[...], reverse, ri, ci)
        q, k, v = gq_ref[...], gk_ref[...], gv_ref[...]
        qd = (q * jnp.exp(b)).astype(_BF)
        qm = (q * jnp.exp(b - bm)).astype(_BF)
        km = k * jnp.exp(bm - b)
        kd = (k * jnp.exp(bT - b)).astype(_BF)
        ST = S_scr[...]
        comp = ST[0:DV]
        for h in range(1, N_GLA):
            comp = comp + ST[DV * h:DV * h + DV]
        st_ref[...] = comp
        inter = _nt(qd, ST.astype(_BF))
        tiles = []
        for p in range(N_GLA // 2):
            qs = slice(128 * (p // 2), 128 * (p // 2) + 128)
            vs = slice(128 * p, 128 * p + 128)
            j0 = (2 * p) % 4
            KS = _pair_stack(km[:, qs], qmask[j0], qmask[j0 + 1])
            VS = _pair_stack(v[:, vs], vmask[0], vmask[1])
            AA = jnp.where(msk2, _nt(qm[:, qs], KS), 0.0).astype(_BF)
            tiles.append(_nn(AA, VS))
        o_ref[...] = inter + jnp.concatenate(tiles, axis=1)
        S_scr[...] = ST * jnp.exp(bT) + bd * _tn(v.astype(_BF), kd)

    def body(qf, kf, vf, lf, qr, kr, vr, lr, of, sf, orr, sr, S_f, S_r):
        @pl.when(pl.program_id(0) == 0)
        def _():
            S_f[...] = jnp.zeros_like(S_f)
            S_r[...] = jnp.zeros_like(S_r)

        one_direction(False, qf, kf, vf, lf, of, sf, S_f)
        one_direction(True, qr, kr, vr, lr, orr, sr, S_r)

    def blk(d, w, c=0):
        return pl.BlockSpec((TB, w), lambda s: (orders[d](s), c))

    def st_spec(d):
        return pl.BlockSpec((GLA_STEP, DV, 256), lambda s: (orders[d](s), 0, 0))

    def o_spec(d):
        return pl.BlockSpec((TB, 512), lambda s: (orders[d](jnp.maximum(s, CTX // TB)) - CTX // TB, 0))

    return _hosted_call(
        body, (gq, gk, gv, la, gq, gk, gv, la), shards, True, name="gla_fwd", grid=(n // GLA_STEP,),
        in_specs=[blk(0, 256), blk(0, 256), blk(0, 512), blk(0, 256, 0), blk(1, 256), blk(1, 256), blk(1, 512),
                  blk(1, 256, 1)],
        out_specs=[o_spec(0), st_spec(0), o_spec(1), st_spec(1)],
        out_shape=[jax.ShapeDtypeStruct((E - CTX, 512), F32), jax.ShapeDtypeStruct((n, DV, 256), F32)] * 2,
        scratch_shapes=[pltpu.VMEM((512, 256), F32)] * 2,
        compiler_params=_cp(("arbitrary",)))


def _gla_bwd(gq, gk, gv, la, z, st_f, st_r, do, slabs=()):
    E = gq.shape[0]
    T = GLA_T
    n = E // T
    TB = GLA_STEP * T
    nc = CTX // TB
    orders = [_gla_order(E, False, True), _gla_order(E, True, True)]

    def one_direction(reverse, *refs):
        ins, outs, shared = refs[0:7], refs[7:11], refs[11:14]
        for j in _gla_subs(reverse, True):
            rows = pl.ds(T * j, T)
            views = [r.at[rows] for r in ins[0:5]] + [ins[5].at[j], ins[6].at[rows]] + [r.at[rows] for r in outs]
            one_chunk(reverse, *views, *shared)

    def one_chunk(reverse, gq_ref, gk_ref, gv_ref, la_ref, z_ref, st_ref, do_ref,
                  dq_ref, dk_ref, dv_ref, dlg_ref, gwg_ref, bsum_ref, dS_scr):
        cols = slice(256, 512) if reverse else slice(0, 256)
        is_lat = orders[1 if reverse else 0](pl.program_id(0)) >= nc
        qmask, vmask, bd, ri, ci = _gla_masks()
        msk2, mskT2, b, bT, bm = _gla_decays(la_ref[...], reverse, ri, ci)
        q, k, v = gq_ref[...], gk_ref[...], gv_ref[...]
        do = jnp.where(is_lat, do_ref[...].astype(F32), 0.0)
        e_b, e_qm, e_km, e_kd, e_T = jnp.exp(b), jnp.exp(b - bm), jnp.exp(bm - b), jnp.exp(bT - b), jnp.exp(bT)
        qd, qm, km, kd = q * e_b, q * e_qm, k * e_km, k * e_kd
        qdb, qmb, kmb, kdb, vb, dob = (t.astype(_BF) for t in (qd, qm, km, kd, v, do))
        ST = jnp.tile(st_ref[...], (N_GLA, 1)) * bd
        dST = dS_scr[...]
        dSTb = dST.astype(_BF)
        dqd = _nn(dob, ST.astype(_BF))
        dkd = _nn(vb, dSTb)
        dv_t, dqm_t, dkm_t = [], [None, None], [None, None]
        for p in range(N_GLA // 2):
            t = p // 2
            qs = slice(128 * t, 128 * t + 128)
            vs = slice(128 * p, 128 * p + 128)
            j0 = (2 * p) % 4
            QS = _pair_stack(qm[:, qs], qmask[j0], qmask[j0 + 1])
            KS = _pair_stack(km[:, qs], qmask[j0], qmask[j0 + 1])
            VS = _pair_stack(v[:, vs], vmask[0], vmask[1])
            DS = _pair_stack(do[:, vs], vmask[0], vmask[1])
            ATT = jnp.where(mskT2, _nt(kmb[:, qs], QS), 0.0).astype(_BF)
            dAA = jnp.where(msk2, _nt(dob[:, vs], VS), 0.0).astype(_BF)
            dATT = jnp.where(mskT2, _nt(vb[:, vs], DS), 0.0).astype(_BF)
            dv_t.append(_nn(ATT, DS))
            dq_p = _nn(dAA, KS)
            dk_p = _nn(dATT, QS)
            dqm_t[t] = dq_p if dqm_t[t] is None else dqm_t[t] + dq_p
            dkm_t[t] = dk_p if dkm_t[t] is None else dkm_t[t] + dk_p
        dqm = jnp.concatenate(dqm_t, axis=1)
        dkm = jnp.concatenate(dkm_t, axis=1)
        dq = dqm * e_qm + dqd * e_b
        dk = dkm * e_km + dkd * e_kd
        dv = _nt(kdb, dSTb) + jnp.concatenate(dv_t, axis=1)
        dq_ref[...] = (dq * (DK ** -0.5)).astype(_BF)
        dk_ref[...] = dk.astype(_BF)
        dv_ref[...] = dv.astype(_BF)
        db = dqm * qm - dkm * km + dqd * qd - dkd * kd
        dbT = jnp.sum(dkd * kd, axis=0, keepdims=True) + e_T * jnp.sum(dST * ST, axis=0, keepdims=True)
        dla = _tri_sum(mskT2[:, 0:T], db) + dbT
        dlg = dla * (1.0 - jnp.exp(GATE_TAU * la_ref[...])) * (1.0 / GATE_TAU)
        bsum_ref[0:1, cols] += jnp.sum(dlg, axis=0, keepdims=True)
        dlgb = dlg.astype(_BF)
        dlg_ref[...] = dlgb
        gwg_ref[:, cols] += _tn(z_ref[...], dlgb)
        dS_scr[...] = dST * e_T + bd * _tn(dob, qdb)

    def body(*refs):
        ins_f, ins_r = refs[0:7], refs[7:14]
        outs_f, outs_r = refs[14:18], refs[18:22]
        gwg_ref, bsum_ref, dS_f, dS_r = refs[22:26]

        @pl.when(pl.program_id(0) == 0)
        def _():
            dS_f[...] = jnp.zeros_like(dS_f)
            dS_r[...] = jnp.zeros_like(dS_r)
            gwg_ref[...] = jnp.zeros_like(gwg_ref)
            bsum_ref[...] = jnp.zeros_like(bsum_ref)

        one_direction(False, *ins_f, *outs_f, gwg_ref, bsum_ref, dS_f)
        one_direction(True, *ins_r, *outs_r, gwg_ref, bsum_ref, dS_r)

    def specs(d, st):
        order = orders[d]
        blk = lambda w, c=0: pl.BlockSpec((TB, w), lambda s: (order(s), c))
        ins = [blk(256), blk(256), blk(512), blk(256, d), blk(128),
               pl.BlockSpec((GLA_STEP, DV, 256), lambda s: (order(s), 0, 0)),
               pl.BlockSpec((TB, 512), lambda s: (jnp.maximum(order(s) - nc, 0), 0))]
        return ins, [blk(256), blk(256), blk(512), blk(256)], (gq, gk, gv, la, z, st, do)

    in_f, out_f, args_f = specs(0, st_f)
    in_r, out_r, args_r = specs(1, st_r)
    dir_shapes = [jax.ShapeDtypeStruct((E, 256), _BF), jax.ShapeDtypeStruct((E, 256), _BF),
                  jax.ShapeDtypeStruct((E, 512), _BF), jax.ShapeDtypeStruct((E, 256), _BF)]
    return _hosted_call(
        body, args_f + args_r, slabs, False, name="gla_bwd", grid=(n // GLA_STEP,),
        in_specs=in_f + in_r, out_specs=out_f + out_r + [_full((128, 512)), _full((8, 512))],
        out_shape=dir_shapes * 2 + [jax.ShapeDtypeStruct((128, 512), F32), jax.ShapeDtypeStruct((8, 512), F32)],
        scratch_shapes=[pltpu.VMEM((512, 256), F32)] * 2,
        compiler_params=_cp(("arbitrary",)))


def _gla_out(o, gg, ggla, mavg):
    rr = lax.rsqrt(_head_mean(o * o, mavg) + EPS)
    oh = o * rr
    sg = _sigmoid(gg)
    return oh, rr, sg


def _mix_fwd(x, attn, o_f, o_b, gg, ggla, mavg, wout, gt1, g2):
    S = x.shape[0]
    TM = 512 if S % 512 == 0 else 256

    def body(x_ref, a_ref, of_ref, ob_ref, gg_ref, ggla_ref, mavg_ref, w_ref, gt1_ref, g2_ref, x1_ref, mix_ref):
        gg_t = gg_ref[...]
        oh, _, sg = _gla_out(of_ref[...] + ob_ref[...], gg_t, ggla_ref[...], mavg_ref[...])
        mix_ref[:, 0:512] = a_ref[...]
        mix_ref[:, 512:1024] = (oh * ggla_ref[...] * (gg_t * sg)).astype(_BF)
        y = _nn(mix_ref[...], w_ref[...])
        ry = lax.rsqrt(jnp.mean(y * y, axis=-1, keepdims=True) + EPS)
        x1_ref[...] = x_ref[...] + gt1_ref[...] * ((y * ry) * g2_ref[...])

    return pl.pallas_call(
        body, name="mix_fwd", grid=(S // TM,),
        in_specs=[_rows(TM, D), _rows(TM, 512), _rows(TM, 512), _rows(TM, 512), _rows(TM, 512),
                  _full((1, 512)), _full((512, 512)), _full((D, D)), _full((1, D)), _full((1, D))],
        out_specs=[_rows(TM, D), _rows(TM, D)],
        out_shape=[jax.ShapeDtypeStruct((S, D), F32), jax.ShapeDtypeStruct((S, D), _BF)],
        compiler_params=_cp(("arbitrary",), 48 * 1024 * 1024),
    )(x, attn, o_f, o_b, gg, ggla, mavg, wout, gt1, g2)


def _mix_bwd(dx1, mix, o_f, o_b, gg, ggla, mavg, wout, gt1, g2, slabs):
    S = dx1.shape[0]
    TM = 512 if S % 512 == 0 else 256

    def body(dx_ref, mix_ref, of_ref, ob_ref, gg_ref, ggla_ref, mavg_ref, w_ref, gt1_ref, g2_ref,
             da_ref, do_ref, dgg_ref, dy_ref, sums_ref):
        @pl.when(pl.program_id(0) == 0)
        def _():
            sums_ref[...] = jnp.zeros_like(sums_ref)

        dx = dx_ref[...]
        y = _nn(mix_ref[...], w_ref[...])
        ry = lax.rsqrt(jnp.mean(y * y, axis=-1, keepdims=True) + EPS)
        yh = y * ry
        sums_ref[0:1, :] += jnp.sum(dx * yh, axis=0, keepdims=True)
        dyh = dx * (gt1_ref[...] * g2_ref[...])
        dy = (ry * (dyh - yh * jnp.mean(dyh * yh, axis=-1, keepdims=True))).astype(_BF)
        dy_ref[...] = dy
        dmix = _nt(dy, w_ref[...])
        da_ref[...] = dmix[:, 0:512].astype(_BF)
        dgla = dmix[:, 512:1024]
        gg_t = gg_ref[...]
        ggla_t = ggla_ref[...]
        oh, rr, sg = _gla_out(of_ref[...] + ob_ref[...], gg_t, ggla_t, mavg_ref[...])
        dgg_ref[...] = (dgla * oh * ggla_t * (sg * (1.0 + gg_t * (1.0 - sg)))).astype(_BF)
        don = dgla * (gg_t * sg)
        sums_ref[1:2, 0:512] += jnp.sum(don * oh, axis=0, keepdims=True)
        doh = don * ggla_t
        do_ref[...] = (rr * (doh - oh * _head_mean(doh * oh, mavg_ref[...]))).astype(_BF)

    return _hosted_call(
        body, (dx1, mix, o_f, o_b, gg, ggla, mavg, wout, gt1, g2), slabs, False,
        name="mix_bwd", grid=(S // TM,),
        in_specs=[_rows(TM, D), _rows(TM, D), _rows(TM, 512), _rows(TM, 512), _rows(TM, 512),
                  _full((1, 512)), _full((512, 512)), _full((D, D)), _full((1, D)), _full((1, D))],
        out_specs=[_rows(TM, 512), _rows(TM, 512), _rows(TM, 512), _rows(TM, D), _full((8, D))],
        out_shape=[jax.ShapeDtypeStruct((S, 512), _BF), jax.ShapeDtypeStruct((S, 512), _BF),
                   jax.ShapeDtypeStruct((S, 512), _BF), jax.ShapeDtypeStruct((S, D), _BF),
                   jax.ShapeDtypeStruct((8, D), F32)],
        compiler_params=_cp(("arbitrary",), 48 * 1024 * 1024))


def _ffn(x1, target, gm2, sh2, gt2, g4, wffi, wffo):
    S = x1.shape[0]
    TF = 256

    def body(x_ref, t_ref, gm_ref, sh_ref, gt_ref, g4_ref, wi_hbm, wo_hbm,
             dx_ref, h_ref, du_ref, act_ref, df_ref, sums_ref, loss_ref, wi, wo, sem):
        @pl.when(pl.program_id(0) == 0)
        def _():
            c1 = pltpu.make_async_copy(wi_hbm, wi, sem.at[0])
            c2 = pltpu.make_async_copy(wo_hbm, wo, sem.at[1])
            c1.start()
            c2.start()
            sums_ref[...] = jnp.zeros_like(sums_ref)
            loss_ref[...] = jnp.zeros_like(loss_ref)
            c1.wait()
            c2.wait()

        x = x_ref[...]
        gm = gm_ref[...]
        r = lax.rsqrt(jnp.mean(x * x, axis=-1, keepdims=True) + EPS)
        xh = x * r
        hb = (xh * gm + sh_ref[...]).astype(_BF)
        h_ref[...] = hb
        u = _nt(hb, wi[...])
        g = u[:, 0:FFN]
        up = u[:, FFN:2 * FFN]
        sg = _sigmoid(g)
        sl = g * sg
        ab = (sl * up).astype(_BF)
        act_ref[...] = ab
        f = _nn(ab, wo[...])
        rf = lax.rsqrt(jnp.mean(f * f, axis=-1, keepdims=True) + EPS)
        fh = f * rf
        gt, g4v = gt_ref[...], g4_ref[...]
        err = x + gt * (fh * g4v) - t_ref[...]
        loss_ref[...] += jnp.sum(err * err) * (0.5 / D)
        dout = err * (1.0 / D)
        sums_ref[2:3, :] += jnp.sum(dout * fh, axis=0, keepdims=True)
        dfh = dout * (gt * g4v)
        dfb = (rf * (dfh - fh * jnp.mean(dfh * fh, axis=-1, keepdims=True))).astype(_BF)
        df_ref[...] = dfb
        dact = _nt(dfb, wo[...])
        du_ref[:, 0:FFN] = (dact * up * (sg * (1.0 + g * (1.0 - sg)))).astype(_BF)
        du_ref[:, FFN:2 * FFN] = (dact * sl).astype(_BF)
        dh = _nn(du_ref[...], wi[...])
        sums_ref[0:1, :] += jnp.sum(dh, axis=0, keepdims=True)
        sums_ref[1:2, :] += jnp.sum(dh * xh, axis=0, keepdims=True)
        dxh = dh * gm
        dx_ref[...] = dout + r * (dxh - xh * jnp.mean(dxh * xh, axis=-1, keepdims=True))

    vec = _full((1, D))
    anyspec = pl.BlockSpec(memory_space=pl.ANY)
    return pl.pallas_call(
        body, name="ffn_fwd_bwd", grid=(S // TF,),
        in_specs=[_rows(TF, D), _rows(TF, D), vec, vec, vec, vec, anyspec, anyspec],
        out_specs=[_rows(TF, D), _rows(TF, D), _rows(TF, 2 * FFN), _rows(TF, FFN), _rows(TF, D),
                   _full((8, D)), _full((8, 128))],
        out_shape=[jax.ShapeDtypeStruct((S, D), F32), jax.ShapeDtypeStruct((S, D), _BF),
                   jax.ShapeDtypeStruct((S, 2 * FFN), _BF), jax.ShapeDtypeStruct((S, FFN), _BF),
                   jax.ShapeDtypeStruct((S, D), _BF), jax.ShapeDtypeStruct((8, D), F32),
                   jax.ShapeDtypeStruct((8, 128), F32)],
        scratch_shapes=[pltpu.VMEM((2 * FFN, D), _BF), pltpu.VMEM((FFN, D), _BF), pltpu.SemaphoreType.DMA((2,))],
        compiler_params=_cp(("arbitrary",), VMEM_BIG),
    )(x1, target, gm2, sh2, gt2, g4, wffi, wffo)


def _inproj_bwd(x, ctx, gml, gmc, win, wg, cos, sa, sb, dq, dk, dv, dgq, dgk, dgv, dgg, dlg_f, dlg_b, dx1):
    S = x.shape[0]
    E = S + CTX
    TE = CTX

    def body(x_ref, c_ref, gml_ref, gmc_ref, w_ref, wg_ref, cos_ref, sa_ref, sb_ref, dq_ref, dk_ref, dv_ref,
             gqf, gqb, gkf, gkb, gvf, gvb, dgg_ref, dlf, dlb, dx1_ref, dp_ref, gx_ref, sums_ref):
        i = pl.program_id(0)
        is_ctx = i == 0

        @pl.when(is_ctx)
        def _():
            sums_ref[...] = jnp.zeros_like(sums_ref)

        lat = jnp.where(is_ctx, 0.0, 1.0)
        cos_t, sa_t, sb_t = cos_ref[...], sa_ref[...], sb_ref[...]
        dp_ref[:, O_Q:O_K] = (_unrope(dq_ref[...].astype(F32), cos_t, sa_t, sb_t) * lat).astype(_BF)
        dp_ref[:, O_K:O_V] = _unrope(dk_ref[...].T, cos_t, sa_t, sb_t).astype(_BF)
        dp_ref[:, O_V:O_GQ] = dv_ref[...].T.astype(_BF)
        dp_ref[:, O_GQ:O_GK] = (gqf[...].astype(F32) + gqb[...].astype(F32)).astype(_BF)
        dp_ref[:, O_GK:O_GV] = (gkf[...].astype(F32) + gkb[...].astype(F32)).astype(_BF)
        dp_ref[:, O_GV:O_GG] = (gvf[...].astype(F32) + gvb[...].astype(F32)).astype(_BF)
        dp_ref[:, O_GG:O_Z] = (dgg_ref[...].astype(F32) * lat).astype(_BF)
        dlg = jnp.concatenate([dlf[...], dlb[...]], axis=1)
        dp_ref[:, O_Z:NP] = _nt(dlg, wg_ref[...]).astype(_BF)
        dh = _nn(dp_ref[...], w_ref[...])
        x = jnp.where(is_ctx, c_ref[...], x_ref[...])
        r = lax.rsqrt(jnp.mean(x * x, axis=-1, keepdims=True) + EPS)
        xh = x * r
        sdh = jnp.sum(dh, axis=0, keepdims=True)
        sdx = jnp.sum(dh * xh, axis=0, keepdims=True)
        sums_ref[0:1, :] += sdh * lat
        sums_ref[1:2, :] += sdx * lat
        sums_ref[2:3, :] += sdh * (1.0 - lat)
        sums_ref[3:4, :] += sdx * (1.0 - lat)
        dxh = dh * jnp.where(is_ctx, gmc_ref[...], gml_ref[...])
        gx_ref[...] = dx1_ref[...] + r * (dxh - xh * jnp.mean(dxh * xh, axis=-1, keepdims=True))

    vec = _full((1, D))
    tab = _rows(TE, 128)
    return pl.pallas_call(
        body, name="inproj_bwd", grid=(E // TE,),
        in_specs=[_rows_lat(TE, D), _full((CTX, D)), vec, vec, _full((NP, D)), _full((128, 512)), tab, tab, tab,
                  _rows_lat(TE, QP), pl.BlockSpec((KP, TE), lambda i: (0, i)), pl.BlockSpec((KP, TE), lambda i: (0, i)),
                  _rows(TE, 256), _rows(TE, 256), _rows(TE, 256), _rows(TE, 256), _rows(TE, 512), _rows(TE, 512),
                  _rows_lat(TE, 512), _rows(TE, 256), _rows(TE, 256), _rows_lat(TE, D)],
        out_specs=[_rows(TE, NP), _rows_lat(TE, D), _full((8, D))],
        out_shape=[jax.ShapeDtypeStruct((E, NP), _BF), jax.ShapeDtypeStruct((S, D), F32),
                   jax.ShapeDtypeStruct((8, D), F32)],
        compiler_params=_cp(("arbitrary",), VMEM_BIG),
    )(x, ctx, gml, gmc, win, wg, cos, sa, sb, dq, dk, dv, *dgq, *dgk, *dgv, dgg, dlg_f, dlg_b, dx1)


def _matmul_tn(a, b, tk, tt, name, out_dtype, transpose_out=False, a_cols=None, hosted=(), gather=True):
    T, KA = a.shape
    N = b.shape[1]
    nt = T // tt
    k0 = 0
    if a_cols is not None:
        KA, k0 = tk, a_cols

    def body(a_ref, b_ref, o_ref, acc):
        t = pl.program_id(1)

        @pl.when(t == 0)
        def _():
            acc[...] = jnp.zeros_like(acc)

        acc[...] += _tn(a_ref[...], b_ref[...])

        @pl.when(t == nt - 1)
        def _():
            o_ref[...] = (acc[...].T if transpose_out else acc[...]).astype(out_dtype)

    if transpose_out:
        out_spec, out_shape = pl.BlockSpec((N, tk), lambda i, t: (0, i)), (N, KA)
    else:
        out_spec, out_shape = pl.BlockSpec((tk, N), lambda i, t: (i, 0)), (KA, N)
    res = _hosted_call(
        body, (a, b), hosted, gather, name=name, grid=(KA // tk, nt),
        in_specs=[pl.BlockSpec((tt, tk), lambda i, t: (t, i + k0)), pl.BlockSpec((tt, N), lambda i, t: (t, 0))],
        out_specs=[out_spec], out_shape=[jax.ShapeDtypeStruct(out_shape, out_dtype)],
        scratch_shapes=[pltpu.VMEM((tk, N), F32)],
        compiler_params=_cp(("arbitrary", "arbitrary"), VMEM_BIG))
    return res if hosted else res[0]


def _ada_bwd(c_all, c_ctx, w_ada, d_all):
    n = w_ada.shape[1]

    def body(c_ref, cc_ref, w_ref, d_ref, gw_ref, t_ref):
        c = jnp.concatenate([c_ref[...], jnp.broadcast_to(cc_ref[...], (8, D))], axis=0)
        db = d_ref[...].astype(_BF)
        gw_ref[0] = _tn((c * _sigmoid(c)).astype(_BF), db)
        t_ref[...] = _nt(db[8:16], w_ref[...].astype(_BF))

    return pl.pallas_call(
        body, name="ada_bwd", in_specs=[_full((8, D)), _full((1, D)), _full((D, n)), _full((16, n))],
        out_specs=[_full((1, D, n)), _full((8, D))],
        out_shape=[jax.ShapeDtypeStruct((1, D, n), F32), jax.ShapeDtypeStruct((8, D), F32)], grid=(1,),
        compiler_params=_cp(("arbitrary",)),
    )(c_all, c_ctx, w_ada, d_all)


PART_ROWS = 56
R_ADA, R_ADA_C, R_GAIN, R_SINK, R_BG, R_GGLA, R_LOSS, R_WG = 0, 6, 12, 16, 17, 18, 19, 24


def _small_grads(s_in, s_ffn, s_mix, ada_l, ada_c, gains, dsink, s_bg, g_wg, loss):
    def body(si, sf, sm, al, ac, g, ds, sbg, gwg, loss_ref, o_ref):
        o_ref[...] = jnp.zeros_like(o_ref)
        o_ref[R_LOSS:R_LOSS + 1, 0:128] = loss_ref[0:1, :]
        sub = lax.broadcasted_iota(jnp.int32, (8, 128), 0)
        lane = lax.broadcasted_iota(jnp.int32, (8, 128), 1)
        o_ref[R_SINK:R_SINK + 1, 0:128] = jnp.sum(jnp.where(sub == lane, ds[...], 0.0), axis=0, keepdims=True)
        o_ref[R_BG:R_BG + 1, 0:512] = sbg[0:1, :]
        y = sm[1:2, 0:128] + sm[1:2, 128:256] + sm[1:2, 256:384] + sm[1:2, 384:512]
        y = y + pltpu.roll(y, 64, 1)
        o_ref[R_GGLA:R_GGLA + 1, 0:128] = jnp.where(lane[0:1] < DV, y, 0.0)
        o_ref[R_WG:R_WG + 16, 0:256] = gwg[0:16, 0:256]
        o_ref[R_WG + 16:R_WG + 32, 0:256] = gwg[16:32, 256:512]
        sdh_l, sdx_l, sdh_c, sdx_c = si[0:1], si[1:2], si[2:3], si[3:4]
        sdh2, sdx2, a2 = sf[0:1], sf[1:2], sf[2:3]
        a1 = sm[0:1]
        g1, g2, g3, g4 = g[0:1], g[1:2], g[2:3], g[3:4]
        sc1, gt1, sc2, gt2 = al[1:2], al[2:3], al[4:5], al[5:6]
        sc1c = ac[1:2]
        z = jnp.zeros((1, D), F32)
        rows = [sdh_l, sdx_l * g1, a1 * g2, sdh2, sdx2 * g3, a2 * g4,
                sdh_c, sdx_c * g1, z, z, z, z,
                sdx_l * (1.0 + sc1) + sdx_c * (1.0 + sc1c), a1 * gt1, sdx2 * (1.0 + sc2), a2 * gt2]
        for r, v in enumerate(rows):
            o_ref[r:r + 1, :] = v

    v8 = _full((8, D))
    return pl.pallas_call(
        body, name="small_grads",
        in_specs=[v8] * 6 + [_full((8, 128)), _full((8, 512)), _full((128, 512)), _full((8, 128))],
        out_specs=_full((PART_ROWS, D)), out_shape=jax.ShapeDtypeStruct((PART_ROWS, D), F32), grid=(1,),
        compiler_params=_cp(("arbitrary",)),
    )(s_in, s_ffn, s_mix, ada_l, ada_c, gains, dsink, s_bg, g_wg, loss)


def _row_tile(R):
    for cand in range(min(512, R // 2) // 16 * 16, 0, -16):
        if R % cand == 0:
            return cand
    return R


def _adamw(w, gs, m, v, name):
    _, R, C = w.shape
    tr = _row_tile(R)
    c1 = 1.0 / (1.0 - ADAM_B1 ** ADAM_STEP)
    c2 = 1.0 / (1.0 - ADAM_B2 ** ADAM_STEP)
    ng = len(gs)

    def body(w_ref, *refs):
        g_refs, (m_ref, v_ref, go_ref, d_ref, nm_ref, nv_ref) = refs[:ng], refs[ng:]
        c0 = 0
        for g_ref in g_refs:
            cols = slice(c0, c0 + g_ref.shape[2])
            c0 += g_ref.shape[2]
            gg = g_ref[0].astype(F32)
            for j in range(1, g_ref.shape[0]):
                gg = gg + g_ref[j].astype(F32)
            go_ref[0, :, cols] = gg
            nm = ADAM_B1 * m_ref[0, :, cols] + (1.0 - ADAM_B1) * gg
            nv = ADAM_B2 * v_ref[0, :, cols] + (1.0 - ADAM_B2) * (gg * gg)
            nm_ref[0, :, cols] = nm
            nv_ref[0, :, cols] = nv
            d_ref[0, :, cols] = -ADAM_LR * ((nm * c1) / (jnp.sqrt(nv * c2) + ADAM_EPS) + ADAM_WD * w_ref[0, :, cols])

    spec = pl.BlockSpec((1, tr, C), lambda i: (0, i, 0))
    sds = jax.ShapeDtypeStruct((1, R, C), F32)
    g_specs = [pl.BlockSpec((g.shape[0], tr, g.shape[2]), lambda i: (0, i, 0)) for g in gs]
    return pl.pallas_call(
        body, name=name, grid=(R // tr,), in_specs=[spec] + g_specs + [spec, spec], out_specs=[spec] * 4,
        out_shape=[sds] * 4, compiler_params=_cp(("parallel",), 48 * 1024 * 1024),
    )(w, *gs, m, v)


def _sum_slots(slots, name):
    _, R, C = slots.shape
    tr = _row_tile(R)

    def body(s_ref, o_ref):
        acc = s_ref[0].astype(F32)
        for j in range(1, N_DEV):
            acc = acc + s_ref[j].astype(F32)
        o_ref[...] = acc

    return pl.pallas_call(
        body, name=name, grid=(R // tr,), in_specs=[pl.BlockSpec((N_DEV, tr, C), lambda i: (0, i, 0))],
        out_specs=_rows(tr, C), out_shape=jax.ShapeDtypeStruct((R, C), F32), compiler_params=_cp(("parallel",)),
    )(slots)


def _ag2_start(x_ref, out_ref, send_sems, recv_sems, local_sem):
    x, y, c = lax.axis_index("x"), lax.axis_index("y"), lax.axis_index("c")
    me, sibling = (x, y, c), (x, y, 1 - c)
    chips = [(1 - x, y), (x, 1 - y), (1 - x, 1 - y)]

    def rows(px, py, pc):
        return out_ref.at[4 * px + 2 * py + pc]

    def copy(k, block, to, src=None):
        return pltpu.make_async_remote_copy(
            src_ref=rows(*block) if src is None else src, dst_ref=rows(*block),
            send_sem=send_sems.at[k], recv_sem=recv_sems.at[k], device_id=to, device_id_type=MESH)

    mine = pltpu.make_async_copy(x_ref, rows(*me), local_sem)
    mine.start()
    first = [copy(0, me, sibling, src=x_ref)]
    first += [copy(1 + j, me, (*chip, c), src=x_ref) for j, chip in enumerate(chips)]
    for cp in first:
        cp.start()
    return copy, mine, first, me, sibling, chips, c


def _ag2_finish(state):
    copy, mine, first, me, sibling, chips, c = state
    passed = [copy(4 + j, (*chip, c), sibling) for j, chip in enumerate(chips)]
    for j, chip in enumerate(chips):
        copy(1 + j, (*chip, c), me).wait_recv()
        passed[j].start()
    copy(0, sibling, me).wait_recv()
    for j, chip in enumerate(chips):
        copy(4 + j, (*chip, 1 - c), me).wait_recv()
    for cp in first + passed:
        cp.wait_send()
    mine.wait()


def _exchange(arrays, name, gather):
    na = len(arrays)

    def body(*refs):
        cps = _xchg_copies(refs[:na], refs[na:2 * na], *refs[2 * na:], gather=gather)
        _xchg_start(cps)
        _xchg_finish(cps)

    anyspec = pl.BlockSpec(memory_space=pl.ANY)
    return pl.pallas_call(
        body, name=name, out_shape=_xchg_out_shapes(arrays, gather), in_specs=[anyspec] * na,
        out_specs=[anyspec] * na, scratch_shapes=_xchg_scratch(na),
    )(*arrays)


def _entry(c, wg_sh, win_sh, c_ctx, w_ada):
    n = w_ada.shape[1]

    def body(c_ref, g_ref, w_ref, cc_ref, wa_ref, call_ref, gall_ref, wall_ref, ada_ref, part,
             s_send, s_recv, s_loc, w_send, w_recv, w_loc, a_send, a_recv, a_loc):
        big = _ag2_start(w_ref, wall_ref, w_send, w_recv, w_loc)
        small = _xchg_copies([c_ref, g_ref], [call_ref, gall_ref], s_send, s_recv, s_loc, gather=True)
        _xchg_start(small)
        _xchg_finish(small)
        cs = jnp.concatenate([call_ref[:, 0, :], jnp.broadcast_to(cc_ref[...], (8, D))], axis=0)
        part[...] = _nn((cs * _sigmoid(cs)).astype(_BF), wa_ref[...].astype(_BF))
        ada = _xchg_copies([part], [ada_ref], a_send, a_recv, a_loc, gather=True)
        _xchg_start(ada)
        _xchg_finish(ada)
        _ag2_finish(big)

    vm = pl.BlockSpec(memory_space=pltpu.VMEM)
    return pl.pallas_call(
        body, name="entry_gather",
        out_shape=[jax.ShapeDtypeStruct((N_DEV,) + c.shape, F32), jax.ShapeDtypeStruct((N_DEV,) + wg_sh.shape, F32),
                   jax.ShapeDtypeStruct((N_DEV,) + win_sh.shape, win_sh.dtype),
                   jax.ShapeDtypeStruct((N_DEV, 16, n), F32)],
        in_specs=[vm] * 5, out_specs=[vm] * 4,
        scratch_shapes=[pltpu.VMEM((16, n), F32)] + _xchg_scratch(2)
        + [pltpu.SemaphoreType.DMA((7,)), pltpu.SemaphoreType.DMA((7,)), pltpu.SemaphoreType.DMA] + _xchg_scratch(1),
        compiler_params=pltpu.CompilerParams(vmem_limit_bytes=VMEM_BIG),
    )(c, wg_sh, win_sh, c_ctx, w_ada)


def _rope_tables(S):
    t = np.arange(S)
    row = (t // GRID_W).astype(np.float32)
    colp = (t % GRID_W).astype(np.float32)
    half = HD // 2
    inv = (ROPE_BASE ** (-np.arange(0, half, 2, dtype=np.float32) / half)).astype(np.float32)
    ar = row[:, None] * inv[None, :]
    ac = colp[:, None] * inv[None, :]
    ang = np.concatenate([ar, ar, ac, ac], axis=-1).astype(np.float32)
    cos = np.cos(ang).astype(np.float32)
    sin = np.sin(ang).astype(np.float32)
    lane = np.arange(HD)
    first = (lane % 32) < 16
    sa = np.where(first[None, :], -sin, 0.0)
    sb = np.where(first[None, :], 0.0, sin)

    def ext(tab, ctx_val):
        full = np.zeros((CTX + S, 128), np.float32)
        full[:CTX, :] = ctx_val
        full[CTX:, :HD] = tab
        full[CTX:, HD:] = tab
        return jnp.asarray(full)

    return ext(cos, 1.0), ext(sa, 0.0), ext(sb, 0.0)


def _pad_rows_win(wt):
    return jnp.pad(wt, ((0, NP - IN_COLS), (0, 0)))


def _unpad_rows_win(g):
    return g[0:IN_COLS]


def _local_step(x, ctx, target, ada_l, ada_c, gains, sink, win_p, wg_bd, bg, ggla, wout_sh, wffi_sh, wffo_sh):
    S = x.shape[0]
    cos, sa, sb = _rope_tables(S)
    g1, g2, g3, g4 = (gains[i:i + 1] for i in range(4))
    sh1, sc1, gt1, sh2, sc2, gt2 = (ada_l[i:i + 1] for i in range(6))
    sh1c, sc1c = ada_c[0:1], ada_c[1:2]
    gml, gmc, gm2 = g1 * (1.0 + sc1), g1 * (1.0 + sc1c), g3 * (1.0 + sc2)
    mavg = jnp.asarray(np.kron(np.eye(N_GLA, dtype=np.float32), np.full((DV, DV), 1.0 / DV, np.float32))).astype(_BF)

    n_ffi, r_ffo, r_out = wffi_sh.shape[0], wffo_sh.shape[0], wout_sh.shape[0]
    tt_e = 1408 if (S + CTX) % 1408 == 0 else 256
    tt_s = 512 if S % 512 == 0 else 256
    h, q, k, v, gq, gk, gv, gg, z, la, wout_g = _inproj_fwd(x, ctx, gml, sh1, gmc, sh1c, win_p, wg_bd, bg,
                                                            cos, sa, sb, [wout_sh])
    attn, lse, probs, wffi_g = _attn_fwd(q, k, v, sink, [wffi_sh])
    o_f, st_f, o_b, st_b, wffo_g = _gla_fwd(gq, gk, gv, la, [wffo_sh])
    wout = wout_g.reshape(N_DEV * r_out, D)
    wffi = wffi_g.reshape(N_DEV * n_ffi, D)
    wffo = wffo_g.reshape(N_DEV * r_ffo, D)
    x1, mix = _mix_fwd(x, attn, o_f, o_b, gg, ggla, mavg, wout, gt1, g2)
    dx1, h2, du, act, df, s_ffn, loss = _ffn(x1, target, gm2, sh2, gt2, g4, wffi, wffo)
    slab_ffi = _matmul_tn(h2, du, 512, tt_s, "grad_w_ffn_in", _BF, True).reshape(N_DEV, n_ffi, D)
    tt_l = 1024 if S % 1024 == 0 else tt_s
    slab_ffo = _matmul_tn(act, df, FFN, tt_l, "grad_w_ffn_out", _BF).reshape(N_DEV, r_ffo, D)
    d_attn, do_gla, dgg, dy, s_mix, got_ffo = _mix_bwd(dx1, mix, o_f, o_b, gg, ggla, mavg, wout, gt1, g2, [slab_ffo])
    slab_out = _matmul_tn(mix, dy, D, tt_l, "grad_w_out", _BF).reshape(N_DEV, r_out, D)
    dq, dk, dv, dsink, got_ffi = _attn_bwd(q, k, v, sink, probs, lse, d_attn, [slab_ffi])
    (dgq_f, dgk_f, dgv_f, dlg_f, dgq_b, dgk_b, dgv_b, dlg_b, g_wg, s_bg,
     got_out) = _gla_bwd(gq, gk, gv, la, z, st_f, st_b, do_gla, [slab_out])
    dp, grad_x, s_in = _inproj_bwd(x, ctx, gml, gmc, win_p, wg_bd, cos, sa, sb, dq, dk, dv,
                                   (dgq_f, dgq_b), (dgk_f, dgk_b), (dgv_f, dgv_b), dgg, dlg_f, dlg_b, dx1)
    small = _small_grads(s_in, s_ffn, s_mix, ada_l, ada_c, gains, dsink, s_bg, g_wg, loss)
    n_in, n_grp = IN_COLS // N_DEV, 2
    got_in, slab = [], None
    for j in range(n_grp):
        g_j, got = _matmul_tn(h, dp, D // n_grp, tt_e, "grad_w_in_%d" % j, _BF, True, a_cols=j,
                              hosted=[small] if j == 0 else [slab], gather=(j == 0))
        if j == 0:
            parts = got
        else:
            got_in.append(got)
        slab = _unpad_rows_win(g_j).reshape(N_DEV, n_in, D // n_grp)
    got_in.append(_exchange([slab], "scatter_grads", False)[0])
    return dict(grad_x=grad_x, got_in=got_in, got_out=got_out, got_ffi=got_ffi, got_ffo=got_ffo, parts=parts)


SMALL_NAMES = ["c_ctx", "b_ada", "g_pre_mix", "g_post_mix", "g_pre_ffn", "g_post_ffn", "attn_sink",
               "b_gate_fwd", "b_gate_bwd", "g_gla_norm", "w_gate_fwd", "w_gate_bwd"]


def _small_update(tot, t_tot, wg_g, w, m, v):
    c1 = 1.0 / (1.0 - ADAM_B1 ** ADAM_STEP)
    c2 = 1.0 / (1.0 - ADAM_B2 ** ADAM_STEP)
    n = len(SMALL_NAMES)

    def body(tot_ref, t_ref, wg_ref, *refs):
        w_r, m_r, v_r = refs[0:n], refs[n:2 * n], refs[2 * n:3 * n]
        g_o, d_o, nm_o, nv_o = refs[3 * n:4 * n], refs[4 * n:5 * n], refs[5 * n:6 * n], refs[6 * n:7 * n]

        def upd(i, idx, g):
            nm = ADAM_B1 * m_r[i][idx] + (1.0 - ADAM_B1) * g
            nv = ADAM_B2 * v_r[i][idx] + (1.0 - ADAM_B2) * (g * g)
            g_o[i][idx] = g
            nm_o[i][idx] = nm
            nv_o[i][idx] = nv
            d_o[i][idx] = -ADAM_LR * ((nm * c1) / (jnp.sqrt(nv * c2) + ADAM_EPS) + ADAM_WD * w_r[i][idx])

        everything = (slice(None), slice(None))
        cc = w_r[0][...]
        sc = _sigmoid(cc)
        upd(0, everything, t_ref[0:1, :] * (sc * (1.0 + cc * (1.0 - sc))))
        for j in range(6):
            upd(1, (slice(None), slice(D * j, D * j + D)),
                tot_ref[R_ADA + j:R_ADA + j + 1, :] + tot_ref[R_ADA_C + j:R_ADA_C + j + 1, :])
        for j in range(4):
            upd(2 + j, everything, tot_ref[R_GAIN + j:R_GAIN + j + 1, :])
        upd(6, everything, tot_ref[R_SINK:R_SINK + 1, 0:N_ATT])
        upd(7, everything, tot_ref[R_BG:R_BG + 1, 0:256])
        upd(8, everything, tot_ref[R_BG:R_BG + 1, 256:512])
        upd(9, everything, tot_ref[R_GGLA:R_GGLA + 1, 0:DV])
        upd(10, (0,), wg_ref[0:GATE_RANK, :])
        upd(11, (0,), wg_ref[GATE_RANK:2 * GATE_RANK, :])

    params = [w[k] for k in SMALL_NAMES] + [m[k] for k in SMALL_NAMES] + [v[k] for k in SMALL_NAMES]
    outs = pl.pallas_call(
        body, name="small_update", grid=(1,),
        in_specs=[_full(tot.shape), _full(t_tot.shape), _full(wg_g.shape)] + [_full(p.shape) for p in params],
        out_specs=[_full(w[k].shape) for k in SMALL_NAMES] * 4,
        out_shape=[jax.ShapeDtypeStruct(w[k].shape, F32) for k in SMALL_NAMES] * 4,
        compiler_params=_cp(("arbitrary",)),
    )(tot, t_tot, wg_g, *params)
    return tuple(dict(zip(SMALL_NAMES, outs[i * n:(i + 1) * n])) for i in range(4))


def kernel(x, c, ctx, c_ctx, w_ada, b_ada, g_pre_mix, g_post_mix, g_pre_ffn, g_post_ffn, w_in, attn_sink, w_gate_fwd, b_gate_fwd, w_gate_bwd, b_gate_bwd, g_gla_norm, w_out, w_ffn_in, w_ffn_out, loss_target, m_c_ctx, m_w_ada, m_b_ada, m_g_pre_mix, m_g_post_mix, m_g_pre_ffn, m_g_post_ffn, m_w_in, m_attn_sink, m_w_gate_fwd, m_b_gate_fwd, m_w_gate_bwd, m_b_gate_bwd, m_g_gla_norm, m_w_out, m_w_ffn_in, m_w_ffn_out, v_c_ctx, v_w_ada, v_b_ada, v_g_pre_mix, v_g_post_mix, v_g_pre_ffn, v_g_post_ffn, v_w_in, v_attn_sink, v_w_gate_fwd, v_b_gate_fwd, v_w_gate_bwd, v_b_gate_bwd, v_g_gla_norm, v_w_out, v_w_ffn_in, v_w_ffn_out):
    me = 4 * lax.axis_index("x") + 2 * lax.axis_index("y") + lax.axis_index("c")
    S = x.shape[1]
    n_in = w_in.shape[2]
    n_ffi = w_ffn_in.shape[2]
    r_out = w_out.shape[1]
    r_ffo = w_ffn_out.shape[1]
    n_ada = w_ada.shape[2]

    wg_sh = jnp.concatenate([w_gate_fwd.reshape(4, 128), w_gate_bwd.reshape(4, 128)], axis=0)
    c_all3, g_all, w_all, ada_all = _entry(c, wg_sh, w_in[0].T.astype(_BF), c_ctx.reshape(1, D), w_ada[0])
    c_all = c_all3.reshape(N_DEV, D)
    wgf = g_all[:, 0:4].reshape(N_DEV, GATE_RANK, 32).transpose(1, 0, 2).reshape(GATE_RANK, 256)
    wgb = g_all[:, 4:8].reshape(N_DEV, GATE_RANK, 32).transpose(1, 0, 2).reshape(GATE_RANK, 256)
    win_p = _pad_rows_win(w_all.reshape(N_DEV * n_in, D))
    wg_bd = jnp.zeros((128, 512), F32).at[0:16, 0:256].set(wgf).at[16:32, 256:512].set(wgb).astype(_BF)
    ada_full = ada_all.transpose(1, 0, 2).reshape(16, N_DEV * n_ada) + b_ada
    ada_l = jnp.pad(lax.dynamic_slice_in_dim(ada_full, me, 1, 0).reshape(6, D), ((0, 2), (0, 0)))
    ada_c = jnp.pad(ada_full[8].reshape(6, D), ((0, 2), (0, 0)))
    gains = jnp.pad(jnp.concatenate([g_pre_mix, g_post_mix, g_pre_ffn, g_post_ffn], axis=0), ((0, 4), (0, 0)))
    sink = jnp.broadcast_to(attn_sink.reshape(8, 1), (8, 128))
    bg = jnp.concatenate([b_gate_fwd, b_gate_bwd], axis=1)
    ggla = jnp.tile(g_gla_norm, (1, N_GLA))

    r = _local_step(x[0], ctx[0], loss_target[0], ada_l, ada_c, gains, sink, win_p, wg_bd, bg, ggla,
                    w_out[0].astype(_BF), w_ffn_in[0].T.astype(_BF), w_ffn_out[0].astype(_BF))

    parts = r["parts"]
    tot = _sum_slots(parts, "sum_small_grads")
    loss = tot[R_LOSS, 0]
    d_ada_rows = parts[:, R_ADA:R_ADA + 6].reshape(N_DEV, 6 * D)
    d_ada_c = tot[R_ADA_C:R_ADA_C + 6].reshape(1, 6 * D)
    my_cols = lax.dynamic_slice_in_dim(jnp.concatenate([d_ada_rows, jnp.broadcast_to(d_ada_c, (1, 6 * D)),
                                                        jnp.zeros((7, 6 * D), F32)], axis=0), me * n_ada, n_ada, 1)
    grad_w_ada, t_part = _ada_bwd(c_all, c_ctx.reshape(1, D), w_ada[0], my_cols)
    wg_g = lax.dynamic_slice(tot, (R_WG, me * 32), (2 * GATE_RANK, 32))

    tr = lambda a: jnp.transpose(a, (0, 2, 1))
    big = {}
    t_all, = _exchange([t_part], "gather_c_ctx", True)
    t_tot = _sum_slots(t_all, "sum_c_ctx")
    for nm, w, g, m, v in [("w_ada", w_ada, grad_w_ada, m_w_ada, v_w_ada),
                           ("w_out", w_out, r["got_out"], m_w_out, v_w_out),
                           ("w_ffn_out", w_ffn_out, r["got_ffo"], m_w_ffn_out, v_w_ffn_out)]:
        big[nm] = _adamw(w, [g], m, v, "adamw_" + nm)
    big["w_ffn_in"] = tuple(tr(o) for o in _adamw(tr(w_ffn_in), [r["got_ffi"]], tr(m_w_ffn_in), tr(v_w_ffn_in),
                                                  "adamw_w_ffn_in"))
    big["w_in"] = tuple(tr(o) for o in _adamw(tr(w_in), r["got_in"], tr(m_w_in), tr(v_w_in), "adamw_w_in"))

    w_small = dict(c_ctx=c_ctx.reshape(1, D), b_ada=b_ada, g_pre_mix=g_pre_mix, g_post_mix=g_post_mix, g_pre_ffn=g_pre_ffn,
                   g_post_ffn=g_post_ffn, attn_sink=attn_sink, b_gate_fwd=b_gate_fwd, b_gate_bwd=b_gate_bwd,
                   g_gla_norm=g_gla_norm, w_gate_fwd=w_gate_fwd, w_gate_bwd=w_gate_bwd)
    m_small = dict(c_ctx=m_c_ctx.reshape(1, D), b_ada=m_b_ada, g_pre_mix=m_g_pre_mix, g_post_mix=m_g_post_mix,
                   g_pre_ffn=m_g_pre_ffn, g_post_ffn=m_g_post_ffn, attn_sink=m_attn_sink, b_gate_fwd=m_b_gate_fwd,
                   b_gate_bwd=m_b_gate_bwd, g_gla_norm=m_g_gla_norm, w_gate_fwd=m_w_gate_fwd, w_gate_bwd=m_w_gate_bwd)
    v_small = dict(c_ctx=v_c_ctx.reshape(1, D), b_ada=v_b_ada, g_pre_mix=v_g_pre_mix, g_post_mix=v_g_post_mix,
                   g_pre_ffn=v_g_pre_ffn, g_post_ffn=v_g_post_ffn, attn_sink=v_attn_sink, b_gate_fwd=v_b_gate_fwd,
                   b_gate_bwd=v_b_gate_bwd, g_gla_norm=v_g_gla_norm, w_gate_fwd=v_w_gate_fwd, w_gate_bwd=v_w_gate_bwd)
    grads_small, d_s, nm_s, nv_s = _small_update(tot, t_tot, wg_g, w_small, m_small, v_small)
    for dd in (grads_small, d_s, nm_s, nv_s):
        dd["c_ctx"] = dd["c_ctx"].reshape(D)

    order = ["c_ctx", "w_ada", "b_ada", "g_pre_mix", "g_post_mix", "g_pre_ffn", "g_post_ffn", "w_in", "attn_sink",
             "w_gate_fwd", "b_gate_fwd", "w_gate_bwd", "b_gate_bwd", "g_gla_norm", "w_out", "w_ffn_in", "w_ffn_out"]
    grads, deltas, new_m, new_v = [], [], [], []
    for nm in order:
        if nm in big:
            g_, d_, m_, v_ = big[nm]
        else:
            g_, d_, m_, v_ = grads_small[nm], d_s[nm], nm_s[nm], nv_s[nm]
        grads.append(g_)
        deltas.append(d_)
        new_m.append(m_)
        new_v.append(v_)
    return (loss, r["grad_x"][None], *grads, *deltas, *new_m, *new_v)
```

```python
import functools
import math

import numpy as np
import jax
import jax.numpy as jnp
from jax import lax
from jax.experimental import pallas as pl
from jax.experimental.pallas import tpu as pltpu

F32 = jnp.float32
_BF = jnp.bfloat16

N_DEV = 8
D = 1024
CTX = 256
HD = 64
N_ATT = 8
N_KV = 2
GRP = N_ATT // N_KV
WIN = 128
GRID_W = 64
ROPE_BASE = 10000.0
N_GLA = 8
DK = 32
DV = 64
GATE_RANK = 16
GATE_TAU = 16.0
FFN = 2816
EPS = 1e-6
NEG = -1e30
GLA_T = 128

QP = N_ATT * HD
KP = N_KV * HD
O_Q, O_K, O_V = 0, QP, QP + KP
O_GQ = O_V + KP
O_GK = O_GQ + N_GLA * DK
O_GV = O_GK + N_GLA * DK
O_GG = O_GV + N_GLA * DV
O_Z = O_GG + N_GLA * DV
NP = O_Z + 128
IN_COLS = 2336

ADAM_LR, ADAM_B1, ADAM_B2, ADAM_EPS, ADAM_WD, ADAM_STEP = 0.001, 0.9, 0.999, 1e-08, 0.01, 10

VMEM_BIG = 56 * 1024 * 1024
MESH = pl.DeviceIdType.MESH


def _cp(sem, vmem=None):
    return pltpu.CompilerParams(dimension_semantics=sem, vmem_limit_bytes=vmem)


def _full(shape):
    nd = len(shape)
    return pl.BlockSpec(shape, lambda *a: (0,) * nd)


def _rows(tile, width, off=0):
    return pl.BlockSpec((tile, width), lambda i: (i + off, 0))


def _rows_lat(tile, width):
    return pl.BlockSpec((tile, width), lambda i: (jnp.maximum(i - 1, 0), 0))


def _nt(a, b):
    return lax.dot_general(a, b, (((1,), (1,)), ((), ())), preferred_element_type=F32)


def _tn(a, b):
    return lax.dot_general(a, b, (((0,), (0,)), ((), ())), preferred_element_type=F32)


def _nn(a, b):
    return jnp.dot(a, b, preferred_element_type=F32)


def _head_mean(x, mavg):
    n = x.shape[0]
    hi = x.astype(_BF)
    lo = (x - hi.astype(F32)).astype(_BF)
    y = _nn(jnp.concatenate([hi, lo], axis=0), mavg)
    return y[0:n] + y[n:2 * n]


def _rope(t, cos, sa, sb):
    n = t.shape[1]
    reps = n // 128
    c = jnp.tile(cos, (1, reps))
    a = jnp.tile(sa, (1, reps))
    b = jnp.tile(sb, (1, reps))
    return t * c + pltpu.roll(t, n - 16, 1) * a + pltpu.roll(t, 16, 1) * b


def _unrope(t, cos, sa, sb):
    n = t.shape[1]
    reps = n // 128
    c = jnp.tile(cos, (1, reps))
    a = jnp.tile(sa, (1, reps))
    b = jnp.tile(sb, (1, reps))
    return t * c + pltpu.roll(t * a, 16, 1) + pltpu.roll(t * b, n - 16, 1)


def _sigmoid(x):
    return 1.0 / (1.0 + jnp.exp(-x))


def _inproj_fwd(x, ctx, gml, shl, gmc, shc, win, wg, bg, cos, sa, sb, shards):
    E = x.shape[0] + CTX
    TE = CTX

    def body(x_ref, c_ref, gml_ref, shl_ref, gmc_ref, shc_ref, w_ref, wg_ref, bg_ref, cos_ref, sa_ref, sb_ref,
             h_ref, q_ref, k_ref, v_ref, gq_ref, gk_ref, gv_ref, gg_ref, z_ref, la_ref):
        is_ctx = pl.program_id(0) == 0
        gm = jnp.where(is_ctx, gmc_ref[...], gml_ref[...])
        sh = jnp.where(is_ctx, shc_ref[...], shl_ref[...])
        x = jnp.where(is_ctx, c_ref[...], x_ref[...])
        r = lax.rsqrt(jnp.mean(x * x, axis=-1, keepdims=True) + EPS)
        hb = ((x * r) * gm + sh).astype(_BF)
        h_ref[...] = hb
        p = _nt(hb, w_ref[...])
        cos_t, sa_t, sb_t = cos_ref[...], sa_ref[...], sb_ref[...]
        q_ref[...] = (_rope(p[:, O_Q:O_K], cos_t, sa_t, sb_t) * (HD ** -0.5)).astype(_BF)
        k_ref[...] = _rope(p[:, O_K:O_V], cos_t, sa_t, sb_t).astype(_BF)
        v_ref[...] = p[:, O_V:O_GQ].astype(_BF)
        gq_ref[...] = p[:, O_GQ:O_GK] * (DK ** -0.5)
        gk_ref[...] = p[:, O_GK:O_GV]
        gv_ref[...] = p[:, O_GV:O_GG]
        gg_ref[...] = p[:, O_GG:O_Z]
        zb = p[:, O_Z:NP].astype(_BF)
        z_ref[...] = zb
        lg = _nn(zb, wg_ref[...]) + bg_ref[...]
        la_ref[...] = (jnp.minimum(lg, 0.0) - jnp.log(1.0 + jnp.exp(-jnp.abs(lg)))) * (1.0 / GATE_TAU)

    vec = _full((1, D))
    tab = _rows(TE, 128)
    outs = [(D, _BF), (QP, _BF), (KP, _BF), (KP, _BF), (256, F32), (256, F32), (512, F32), (512, F32),
            (128, _BF), (512, F32)]
    return _hosted_call(
        body, (x, ctx, gml, shl, gmc, shc, win, wg, bg, cos, sa, sb), shards, True,
        name="inproj_fwd", grid=(E // TE,),
        in_specs=[_rows_lat(TE, D), _full((CTX, D)), vec, vec, vec, vec, _full((NP, D)), _full((128, 512)),
                  _full((1, 512)), tab, tab, tab],
        out_specs=[_rows_lat(TE, w) if i == 7 else _rows(TE, w) for i, (w, _) in enumerate(outs)],
        out_shape=[jax.ShapeDtypeStruct((E - CTX if i == 7 else E, w), dt) for i, (w, dt) in enumerate(outs)],
        compiler_params=_cp(("arbitrary",), 40 * 1024 * 1024))


def _xchg_scratch(na):
    return [pltpu.SemaphoreType.DMA((na, N_DEV - 1)), pltpu.SemaphoreType.DMA((na, N_DEV - 1)),
            pltpu.SemaphoreType.DMA((na,))]


def _xchg_copies(ins, outs, send_sems, recv_sems, local_sems, gather):
    x, y, c = lax.axis_index("x"), lax.axis_index("y"), lax.axis_index("c")
    me = 4 * x + 2 * y + c
    local, sends, recvs = [], [], []
    for a in range(len(ins)):
        local.append(pltpu.make_async_copy(ins[a] if gather else ins[a].at[me], outs[a].at[me], local_sems.at[a]))
    for k in range(1, N_DEV):
        px, py, pc = x ^ (k >> 2), y ^ ((k >> 1) & 1), c ^ (k & 1)
        peer = 4 * px + 2 * py + pc
        for a in range(len(ins)):
            sems = dict(send_sem=send_sems.at[a, k - 1], recv_sem=recv_sems.at[a, k - 1], device_id_type=MESH)
            sends.append(pltpu.make_async_remote_copy(
                src_ref=ins[a] if gather else ins[a].at[peer], dst_ref=outs[a].at[me], device_id=(px, py, pc), **sems))
            recvs.append(pltpu.make_async_remote_copy(
                src_ref=ins[a] if gather else ins[a].at[me], dst_ref=outs[a].at[peer], device_id=(x, y, c), **sems))
    return local, sends, recvs


def _xchg_start(cps):
    local, sends, _ = cps
    for cp in local + sends:
        cp.start()


def _xchg_finish(cps):
    local, sends, recvs = cps
    for cp in recvs:
        cp.wait_recv()
    for cp in sends:
        cp.wait_send()
    for cp in local:
        cp.wait()


def _xchg_out_shapes(ins, gather):
    return [jax.ShapeDtypeStruct(((N_DEV,) + s.shape) if gather else s.shape, s.dtype) for s in ins]


def _hosted_call(body, args, hosted, gather, *, grid, in_specs, out_specs, out_shape, scratch_shapes=(), **kw):
    na = len(hosted)
    if na == 0:
        return pl.pallas_call(body, grid=grid, in_specs=in_specs, out_specs=out_specs, out_shape=out_shape,
                              scratch_shapes=list(scratch_shapes), **kw)(*args)
    n_in, n_out, n_scr = len(in_specs), len(out_specs), len(scratch_shapes)

    def wrapped(*refs):
        ins, h_in = refs[:n_in], refs[n_in:n_in + na]
        outs, h_out = refs[n_in + na:n_in + na + n_out], refs[n_in + na + n_out:n_in + 2 * na + n_out]
        scr = refs[n_in + 2 * na + n_out:]
        cps = _xchg_copies(h_in, h_out, *scr[n_scr:], gather=gather)
        pids = [pl.program_id(a) for a in range(len(grid))]
        first = functools.reduce(jnp.logical_and, [p == 0 for p in pids])
        last = functools.reduce(jnp.logical_and, [p == g - 1 for p, g in zip(pids, grid)])

        @pl.when(first)
        def _():
            _xchg_start(cps)

        body(*ins, *outs, *scr[:n_scr])

        @pl.when(last)
        def _():
            _xchg_finish(cps)

    anyspec = pl.BlockSpec(memory_space=pl.ANY)
    return pl.pallas_call(
        wrapped, grid=grid, in_specs=list(in_specs) + [anyspec] * na, out_specs=list(out_specs) + [anyspec] * na,
        out_shape=list(out_shape) + _xchg_out_shapes(hosted, gather),
        scratch_shapes=list(scratch_shapes) + _xchg_scratch(na), **kw)(*args, *hosted)


ATT_BLOCKS = 2
PROB_ROWS = N_KV * GRP * WIN


def _attn_specs(E):
    nb = (E - CTX) // WIN
    last = E // WIN - 1
    kc = pl.BlockSpec((CTX, KP), lambda m: (0, 0))
    ks = [pl.BlockSpec((WIN, KP), functools.partial(lambda m, j: (jnp.minimum(ATT_BLOCKS * m + j, last), 0), j=j))
          for j in range(1, ATT_BLOCKS + 3)]
    return nb, [kc] + ks


def _attn_bias(nb):
    rows = np.arange(GRP * WIN)[:, None] % WIN
    cols = np.arange(CTX + 3 * WIN)[None, :]
    j = cols - CTX
    band = np.abs(j - WIN - rows) <= WIN
    out = []
    for first, last in ((True, False), (False, False), (False, True)):
        ok = (cols < CTX) | (band & ((j >= WIN) | (not first)) & ((j < 2 * WIN) | (not last)))
        out.append(np.where(ok, 0.0, NEG).astype(np.float32))
    bias = jnp.asarray(np.stack(out))
    steps = nb // ATT_BLOCKS
    shape = (1, GRP * WIN, CTX + 3 * WIN)
    specs = [pl.BlockSpec(shape, lambda m: (jnp.where(m == 0, 0, 1), 0, 0))]
    specs += [pl.BlockSpec(shape, lambda m: (1, 0, 0))] * (ATT_BLOCKS - 2)
    specs += [pl.BlockSpec(shape, lambda m: (jnp.where(m == steps - 1, 2, 1), 0, 0))]
    return bias, specs


def _both_halves(t, h):
    tf = t.astype(F32)
    r = pltpu.roll(tf, HD, 1)
    lo = lax.broadcasted_iota(jnp.int32, tf.shape, 1) < HD
    return (jnp.where(lo, tf, r) if h == 0 else jnp.where(lo, r, tf)).astype(t.dtype)


def _stack_heads(ref, h):
    lo = lax.broadcasted_iota(jnp.int32, (WIN, 128), 1) < HD
    parts = []
    for g in range(GRP):
        j = GRP * h + g
        t = ref[:, 128 * (j // 2):128 * (j // 2) + 128].astype(F32)
        parts.append(jnp.where(lo if j % 2 == 0 else jnp.logical_not(lo), t, 0.0))
    return jnp.concatenate(parts, axis=0)


def _unstack_pair(o, pp):
    lo = lax.broadcasted_iota(jnp.int32, (WIN, 128), 1) < HD
    return jnp.where(lo, o[WIN * 2 * pp:WIN * 2 * pp + WIN], o[WIN * (2 * pp + 1):WIN * (2 * pp + 1) + WIN])


def _attn_fwd(q, k, v, sink, shards):
    E = q.shape[0]
    S = E - CTX
    nb, kspecs = _attn_specs(E)
    na = len(shards)

    nk = ATT_BLOCKS + 3

    def one_block(q_ref, kw, vw, sink_ref, bias_ref, o_ref, lse_ref, p_ref):
        lane = lax.broadcasted_iota(jnp.int32, (WIN, 128), 1)
        lse_t = jnp.zeros((WIN, 128), F32)
        kall = jnp.concatenate([r[...] for r in kw], axis=0)
        vall = jnp.concatenate([r[...] for r in vw], axis=0)
        K = [_both_halves(kall, h) for h in range(N_KV)]
        Q = [_stack_heads(q_ref, h).astype(_BF) for h in range(N_KV)]
        sk = [jnp.concatenate([jnp.broadcast_to(sink_ref[GRP * h + g:GRP * h + g + 1, 0:1], (WIN, 1))
                               for g in range(GRP)], axis=0) for h in range(N_KV)]
        s = [_nt(Q[h], K[h]) + bias_ref[0] for h in range(N_KV)]
        m = [jnp.maximum(jnp.max(s[h], axis=1, keepdims=True), sk[h]) for h in range(N_KV)]
        e = [jnp.exp(s[h] - m[h]) for h in range(N_KV)]
        den = [jnp.sum(e[h], axis=1, keepdims=True) + jnp.exp(sk[h] - m[h]) for h in range(N_KV)]
        V = [_both_halves(vall, h) for h in range(N_KV)]
        pb = [(e[h] * (1.0 / den[h])).astype(_BF) for h in range(N_KV)]
        for h in range(N_KV):
            p_ref[GRP * WIN * h:GRP * WIN * (h + 1), :] = pb[h]
        o = [_nn(pb[h], V[h]) for h in range(N_KV)]
        for h in range(N_KV):
            lse = m[h] + jnp.log(den[h])
            for g in range(GRP):
                lse_t = jnp.where(lane == GRP * h + g, lse[WIN * g:WIN * g + WIN], lse_t)
            for pp in range(GRP // 2):
                t = 2 * h + pp
                o_ref[:, 128 * t:128 * t + 128] = _unstack_pair(o[h], pp).astype(_BF)
        lse_ref[...] = lse_t

    def body(q_ref, *rest):
        kr, vr, sink_ref = rest[0:nk], rest[nk:2 * nk], rest[2 * nk]
        bias_refs = rest[2 * nk + 1:2 * nk + 1 + ATT_BLOCKS]
        rest = rest[2 * nk + 1 + ATT_BLOCKS:]
        shard_refs, (o_ref, lse_ref, p_ref), got_refs = rest[:na], rest[na:na + 3], rest[na + 3:2 * na + 3]
        cps = _xchg_copies(shard_refs, got_refs, *rest[2 * na + 3:], gather=True)

        @pl.when(pl.program_id(0) == 0)
        def _():
            _xchg_start(cps)

        for j in range(ATT_BLOCKS):
            rows = pl.ds(WIN * j, WIN)
            one_block(q_ref.at[rows], [kr[0]] + list(kr[1 + j:4 + j]), [vr[0]] + list(vr[1 + j:4 + j]), sink_ref,
                      bias_refs[j], o_ref.at[rows], lse_ref.at[rows], p_ref.at[pl.ds(PROB_ROWS * j, PROB_ROWS)])

        @pl.when(pl.program_id(0) == nb // ATT_BLOCKS - 1)
        def _():
            _xchg_finish(cps)

    tq = ATT_BLOCKS * WIN
    qs = pl.BlockSpec((tq, QP), lambda m: (m + CTX // tq, 0))
    anyspec = pl.BlockSpec(memory_space=pl.ANY)
    bias, bias_specs = _attn_bias(nb)
    return pl.pallas_call(
        body, name="attn_fwd", grid=(nb // ATT_BLOCKS,),
        in_specs=[qs] + kspecs + kspecs + [_full((8, 128))] + bias_specs + [anyspec] * na,
        out_specs=[_rows(tq, 512), _rows(tq, 128), _rows(ATT_BLOCKS * PROB_ROWS, CTX + 3 * WIN)] + [anyspec] * na,
        out_shape=[jax.ShapeDtypeStruct((S, 512), _BF), jax.ShapeDtypeStruct((S, 128), F32),
                   jax.ShapeDtypeStruct((nb * PROB_ROWS, CTX + 3 * WIN), _BF)]
        + _xchg_out_shapes(shards, True),
        scratch_shapes=_xchg_scratch(na),
        compiler_params=_cp(("arbitrary",), 48 * 1024 * 1024),
    )(q, *([k] * nk), *([v] * nk), sink, *([bias] * ATT_BLOCKS), *shards)


def _attn_bwd(q, k, v, sink, probs, lse, d_attn, slabs):
    E = q.shape[0]
    S = E - CTX
    nb, kspecs = _attn_specs(E)
    last = E // WIN - 1
    na = len(slabs)

    nk = ATT_BLOCKS + 3

    def one_block(n, q_ref, kw, vw, sink_ref, p_ref, lse_ref, do_ref, dq_ref, dk_ref, dv_ref, ds_ref):
        lane = lax.broadcasted_iota(jnp.int32, (WIN, 128), 1)
        lse_t = lse_ref[...]
        starts = [None, pl.multiple_of((n + 1) * WIN, WIN), pl.multiple_of((n + 2) * WIN, WIN),
                  pl.multiple_of(jnp.minimum(n + 3, last) * WIN, WIN)]
        kall = jnp.concatenate([r[...] for r in kw], axis=0)
        vall = jnp.concatenate([r[...] for r in vw], axis=0)
        for h in range(N_KV):
            hs = slice(HD * h, HD * h + HD)
            K = _both_halves(kall, h)
            V = _both_halves(vall, h)
            Q = _stack_heads(q_ref, h).astype(_BF)
            sk = jnp.concatenate([jnp.broadcast_to(sink_ref[GRP * h + g:GRP * h + g + 1, 0:1], (WIN, 1))
                                  for g in range(GRP)], axis=0)
            ls = jnp.concatenate([jnp.sum(jnp.where(lane == GRP * h + g, lse_t, 0.0), axis=1, keepdims=True)
                                  for g in range(GRP)], axis=0)
            do = _stack_heads(do_ref, h).astype(_BF)
            pb = p_ref[GRP * WIN * h:GRP * WIN * (h + 1), :]
            p = pb.astype(F32)
            dp = _nt(do, V)
            delta = jnp.sum(p * dp, axis=1, keepdims=True)
            dsc = (p * (dp - delta)).astype(_BF)
            dq = _nn(dsc, K) * (HD ** -0.5)
            for pp in range(GRP // 2):
                t = 2 * h + pp
                dq_ref[:, 128 * t:128 * t + 128] = _unstack_pair(dq, pp).astype(_BF)
            dK2 = _tn(Q, dsc)
            dV2 = _tn(do, pb)
            dK = dK2[0:HD] + dK2[HD:2 * HD]
            dV = dV2[0:HD] + dV2[HD:2 * HD]
            dk_ref[hs, 0:CTX] += dK[:, 0:CTX]
            dv_ref[hs, 0:CTX] += dV[:, 0:CTX]
            for w in range(1, 4):
                lo = CTX + WIN * (w - 1)
                dk_ref[hs, pl.ds(starts[w], WIN)] += dK[:, lo:lo + WIN]
                dv_ref[hs, pl.ds(starts[w], WIN)] += dV[:, lo:lo + WIN]
            psk = -jnp.exp(sk - ls) * delta
            for g in range(GRP):
                j = GRP * h + g
                tot = jnp.sum(psk[WIN * g:WIN * g + WIN], axis=0, keepdims=True)
                ds_ref[j:j + 1, :] += jnp.broadcast_to(tot, (1, 128))

    def body(q_ref, *rest):
        kr, vr = rest[0:nk], rest[nk:2 * nk]
        sink_ref, p_ref, lse_ref, do_ref = rest[2 * nk:2 * nk + 4]
        rest = rest[2 * nk + 4:]
        slab_refs, (dq_ref, dk_ref, dv_ref, ds_ref), got_refs = rest[:na], rest[na:na + 4], rest[na + 4:2 * na + 4]
        m = pl.program_id(0)
        cps = _xchg_copies(slab_refs, got_refs, *rest[2 * na + 4:], gather=False)

        @pl.when(m == 0)
        def _():
            _xchg_start(cps)
            dk_ref[...] = jnp.zeros_like(dk_ref)
            dv_ref[...] = jnp.zeros_like(dv_ref)
            ds_ref[...] = jnp.zeros_like(ds_ref)

        for j in range(ATT_BLOCKS):
            rows = pl.ds(WIN * j, WIN)
            one_block(ATT_BLOCKS * m + j, q_ref.at[rows], [kr[0]] + list(kr[1 + j:4 + j]),
                      [vr[0]] + list(vr[1 + j:4 + j]), sink_ref, p_ref.at[pl.ds(PROB_ROWS * j, PROB_ROWS)],
                      lse_ref.at[rows], do_ref.at[rows], dq_ref.at[rows], dk_ref, dv_ref, ds_ref)

        @pl.when(m == nb // ATT_BLOCKS - 1)
        def _():
            _xchg_finish(cps)

    tq = ATT_BLOCKS * WIN
    qs = pl.BlockSpec((tq, QP), lambda m: (m + CTX // tq, 0))
    anyspec = pl.BlockSpec(memory_space=pl.ANY)
    return pl.pallas_call(
        body, name="attn_bwd", grid=(nb // ATT_BLOCKS,),
        in_specs=[qs] + kspecs + kspecs + [_full((8, 128)), _rows(ATT_BLOCKS * PROB_ROWS, CTX + 3 * WIN),
                                            _rows(tq, 128), _rows(tq, 512)] + [anyspec] * na,
        out_specs=[_rows(tq, QP), _full((KP, E)), _full((KP, E)), _full((8, 128))] + [anyspec] * na,
        out_shape=[jax.ShapeDtypeStruct((S, QP), _BF), jax.ShapeDtypeStruct((KP, E), F32),
                   jax.ShapeDtypeStruct((KP, E), F32), jax.ShapeDtypeStruct((8, 128), F32)]
        + _xchg_out_shapes(slabs, False),
        scratch_shapes=_xchg_scratch(na),
        compiler_params=_cp(("arbitrary",), 48 * 1024 * 1024),
    )(q, *([k] * nk), *([v] * nk), sink, probs, lse, d_attn, *slabs)


GLA_STEP = 2


def _gla_subs(reverse, backward):
    subs = list(range(GLA_STEP))
    return subs[::-1] if reverse != backward else subs


def _gla_order(E, reverse, backward):
    nc = CTX // (GLA_STEP * GLA_T)
    n = E // (GLA_STEP * GLA_T)
    if not reverse:
        fwd = lambda s: s
    else:
        fwd = lambda s: jnp.where(s < nc, nc - 1 - s, n - 1 + nc - s)
    if backward:
        return lambda s: fwd(n - 1 - s)
    return fwd


def _gla_masks():
    T = GLA_T
    l128 = lax.broadcasted_iota(jnp.int32, (1, 128), 1)
    qmask = [((l128 >> 5) == j).astype(F32) for j in range(4)]
    vmask = [((l128 >> 6) == j).astype(F32) for j in range(2)]
    bd = ((lax.broadcasted_iota(jnp.int32, (512, 256), 0) >> 6)
          == (lax.broadcasted_iota(jnp.int32, (512, 256), 1) >> 5)).astype(F32)
    ri = lax.broadcasted_iota(jnp.int32, (T, 2 * T), 0)
    ci = lax.broadcasted_iota(jnp.int32, (T, 2 * T), 1) & (T - 1)
    return qmask, vmask, bd, ri, ci


def _tri_sum(tri, x):
    hi = x.astype(_BF)
    lo = (x - hi.astype(F32)).astype(_BF)
    n = x.shape[1]
    y = _nn(tri.astype(_BF), jnp.concatenate([hi, lo], axis=1))
    return y[:, 0:n] + y[:, n:2 * n]


def _gla_decays(la, reverse, ri, ci):
    T = GLA_T
    msk2 = (ri <= ci) if reverse else (ri >= ci)
    mskT2 = (ri >= ci) if reverse else (ri <= ci)
    b = _tri_sum(msk2[:, 0:T], la)
    bT = b[0:1] if reverse else b[T - 1:T]
    bm = b[T // 2:T // 2 + 1]
    return msk2, mskT2, b, bT, bm


def _pair_stack(tile, m0, m1):
    return jnp.concatenate([(tile * m0).astype(_BF), (tile * m1).astype(_BF)], axis=0)


def _gla_fwd(gq, gk, gv, la, shards=()):
    E = gq.shape[0]
    T = GLA_T
    n = E // T
    TB = GLA_STEP * T
    orders = [_gla_order(E, False, False), _gla_order(E, True, False)]

    def one_direction(reverse, *refs):
        for j in _gla_subs(reverse, False):
            rows = pl.ds(T * j, T)
            one_chunk(reverse, *[r.at[rows] for r in refs[:5]], refs[5].at[j], refs[6])

    def one_chunk(reverse, gq_ref, gk_ref, gv_ref, la_ref, o_ref, st_ref, S_scr):
        qmask, vmask, bd, ri, ci = _gla_masks()
        msk2, _, b, bT, bm = _gla_decays(la_ref[...], reverse, ri, ci)
        q, k, v = gq_ref[...], gk_ref[...], gv_ref[...]
        qd = (q * jnp.exp(b)).astype(_BF)
        qm = (q * jnp.exp(b - bm)).astype(_BF)
        km = k * jnp.exp(bm - b)
        kd = (k * jnp.exp(bT - b)).astype(_BF)
        ST = S_scr[...]
        comp = ST[0:DV]
        for h in range(1, N_GLA):
            comp = comp + ST[DV * h:DV * h + DV]
        st_ref[...] = comp
        inter = _nt(qd, ST.astype(_BF))
        tiles = []
        for p in range(N_GLA // 2):
            qs = slice(128 * (p // 2), 128 * (p // 2) + 128)
            vs = slice(128 * p, 128 * p + 128)
            j0 = (2 * p) % 4
            KS = _pair_stack(km[:, qs], qmask[j0], qmask[j0 + 1])
            VS = _pair_stack(v[:, vs], vmask[0], vmask[1])
            AA = jnp.where(msk2, _nt(qm[:, qs], KS), 0.0).astype(_BF)
            tiles.append(_nn(AA, VS))
        o_ref[...] = inter + jnp.concatenate(tiles, axis=1)
        S_scr[...] = ST * jnp.exp(bT) + bd * _tn(v.astype(_BF), kd)

    def body(qf, kf, vf, lf, qr, kr, vr, lr, of, sf, orr, sr, S_f, S_r):
        @pl.when(pl.program_id(0) == 0)
        def _():
            S_f[...] = jnp.zeros_like(S_f)
            S_r[...] = jnp.zeros_like(S_r)

        one_direction(False, qf, kf, vf, lf, of, sf, S_f)
        one_direction(True, qr, kr, vr, lr, orr, sr, S_r)

    def blk(d, w, c=0):
        return pl.BlockSpec((TB, w), lambda s: (orders[d](s), c))

    def st_spec(d):
        return pl.BlockSpec((GLA_STEP, DV, 256), lambda s: (orders[d](s), 0, 0))

    def o_spec(d):
        return pl.BlockSpec((TB, 512), lambda s: (orders[d](jnp.maximum(s, CTX // TB)) - CTX // TB, 0))

    return _hosted_call(
        body, (gq, gk, gv, la, gq, gk, gv, la), shards, True, name="gla_fwd", grid=(n // GLA_STEP,),
        in_specs=[blk(0, 256), blk(0, 256), blk(0, 512), blk(0, 256, 0), blk(1, 256), blk(1, 256), blk(1, 512),
                  blk(1, 256, 1)],
        out_specs=[o_spec(0), st_spec(0), o_spec(1), st_spec(1)],
        out_shape=[jax.ShapeDtypeStruct((E - CTX, 512), F32), jax.ShapeDtypeStruct((n, DV, 256), F32)] * 2,
        scratch_shapes=[pltpu.VMEM((512, 256), F32)] * 2,
        compiler_params=_cp(("arbitrary",)))


def _gla_bwd(gq, gk, gv, la, z, st_f, st_r, do, slabs=()):
    E = gq.shape[0]
    T = GLA_T
    n = E // T
    TB = GLA_STEP * T
    nc = CTX // TB
    orders = [_gla_order(E, False, True), _gla_order(E, True, True)]

    def one_direction(reverse, *refs):
        ins, outs, shared = refs[0:7], refs[7:11], refs[11:14]
        for j in _gla_subs(reverse, True):
            rows = pl.ds(T * j, T)
            views = [r.at[rows] for r in ins[0:5]] + [ins[5].at[j], ins[6].at[rows]] + [r.at[rows] for r in outs]
            one_chunk(reverse, *views, *shared)

    def one_chunk(reverse, gq_ref, gk_ref, gv_ref, la_ref, z_ref, st_ref, do_ref,
                  dq_ref, dk_ref, dv_ref, dlg_ref, gwg_ref, bsum_ref, dS_scr):
        cols = slice(256, 512) if reverse else slice(0, 256)
        is_lat = orders[1 if reverse else 0](pl.program_id(0)) >= nc
        qmask, vmask, bd, ri, ci = _gla_masks()
        msk2, mskT2, b, bT, bm = _gla_decays(la_ref[...], reverse, ri, ci)
        q, k, v = gq_ref[...], gk_ref[...], gv_ref[...]
        do = jnp.where(is_lat, do_ref[...].astype(F32), 0.0)
        e_b, e_qm, e_km, e_kd, e_T = jnp.exp(b), jnp.exp(b - bm), jnp.exp(bm - b), jnp.exp(bT - b), jnp.exp(bT)
        qd, qm, km, kd = q * e_b, q * e_qm, k * e_km, k * e_kd
        qdb, qmb, kmb, kdb, vb, dob = (t.astype(_BF) for t in (qd, qm, km, kd, v, do))
        ST = jnp.tile(st_ref[...], (N_GLA, 1)) * bd
        dST = dS_scr[...]
        dSTb = dST.astype(_BF)
        dqd = _nn(dob, ST.astype(_BF))
        dkd = _nn(vb, dSTb)
        dv_t, dqm_t, dkm_t = [], [None, None], [None, None]
        for p in range(N_GLA // 2):
            t = p // 2
            qs = slice(128 * t, 128 * t + 128)
            vs = slice(128 * p, 128 * p + 128)
            j0 = (2 * p) % 4
            QS = _pair_stack(qm[:, qs], qmask[j0], qmask[j0 + 1])
            KS = _pair_stack(km[:, qs], qmask[j0], qmask[j0 + 1])
            VS = _pair_stack(v[:, vs], vmask[0], vmask[1])
            DS = _pair_stack(do[:, vs], vmask[0], vmask[1])
            ATT = jnp.where(mskT2, _nt(kmb[:, qs], QS), 0.0).astype(_BF)
            dAA = jnp.where(msk2, _nt(dob[:, vs], VS), 0.0).astype(_BF)
            dATT = jnp.where(mskT2, _nt(vb[:, vs], DS), 0.0).astype(_BF)
            dv_t.append(_nn(ATT, DS))
            dq_p = _nn(dAA, KS)
            dk_p = _nn(dATT, QS)
            dqm_t[t] = dq_p if dqm_t[t] is None else dqm_t[t] + dq_p
            dkm_t[t] = dk_p if dkm_t[t] is None else dkm_t[t] + dk_p
        dqm = jnp.concatenate(dqm_t, axis=1)
        dkm = jnp.concatenate(dkm_t, axis=1)
        dq = dqm * e_qm + dqd * e_b
        dk = dkm * e_km + dkd * e_kd
        dv = _nt(kdb, dSTb) + jnp.concatenate(dv_t, axis=1)
        dq_ref[...] = (dq * (DK ** -0.5)).astype(_BF)
        dk_ref[...] = dk.astype(_BF)
        dv_ref[...] = dv.astype(_BF)
        db = dqm * qm - dkm * km + dqd * qd - dkd * kd
        dbT = jnp.sum(dkd * kd, axis=0, keepdims=True) + e_T * jnp.sum(dST * ST, axis=0, keepdims=True)
        dla = _tri_sum(mskT2[:, 0:T], db) + dbT
        dlg = dla * (1.0 - jnp.exp(GATE_TAU * la_ref[...])) * (1.0 / GATE_TAU)
        bsum_ref[0:1, cols] += jnp.sum(dlg, axis=0, keepdims=True)
        dlgb = dlg.astype(_BF)
        dlg_ref[...] = dlgb
        gwg_ref[:, cols] += _tn(z_ref[...], dlgb)
        dS_scr[...] = dST * e_T + bd * _tn(dob, qdb)

    def body(*refs):
        ins_f, ins_r = refs[0:7], refs[7:14]
        outs_f, outs_r = refs[14:18], refs[18:22]
        gwg_ref, bsum_ref, dS_f, dS_r = refs[22:26]

        @pl.when(pl.program_id(0) == 0)
        def _():
            dS_f[...] = jnp.zeros_like(dS_f)
            dS_r[...] = jnp.zeros_like(dS_r)
            gwg_ref[...] = jnp.zeros_like(gwg_ref)
            bsum_ref[...] = jnp.zeros_like(bsum_ref)

        one_direction(False, *ins_f, *outs_f, gwg_ref, bsum_ref, dS_f)
        one_direction(True, *ins_r, *outs_r, gwg_ref, bsum_ref, dS_r)

    def specs(d, st):
        order = orders[d]
        blk = lambda w, c=0: pl.BlockSpec((TB, w), lambda s: (order(s), c))
        ins = [blk(256), blk(256), blk(512), blk(256, d), blk(128),
               pl.BlockSpec((GLA_STEP, DV, 256), lambda s: (order(s), 0, 0)),
               pl.BlockSpec((TB, 512), lambda s: (jnp.maximum(order(s) - nc, 0), 0))]
        return ins, [blk(256), blk(256), blk(512), blk(256)], (gq, gk, gv, la, z, st, do)

    in_f, out_f, args_f = specs(0, st_f)
    in_r, out_r, args_r = specs(1, st_r)
    dir_shapes = [jax.ShapeDtypeStruct((E, 256), _BF), jax.ShapeDtypeStruct((E, 256), _BF),
                  jax.ShapeDtypeStruct((E, 512), _BF), jax.ShapeDtypeStruct((E, 256), _BF)]
    return _hosted_call(
        body, args_f + args_r, slabs, False, name="gla_bwd", grid=(n // GLA_STEP,),
        in_specs=in_f + in_r, out_specs=out_f + out_r + [_full((128, 512)), _full((8, 512))],
        out_shape=dir_shapes * 2 + [jax.ShapeDtypeStruct((128, 512), F32), jax.ShapeDtypeStruct((8, 512), F32)],
        scratch_shapes=[pltpu.VMEM((512, 256), F32)] * 2,
        compiler_params=_cp(("arbitrary",)))


def _gla_out(o, gg, ggla, mavg):
    rr = lax.rsqrt(_head_mean(o * o, mavg) + EPS)
    oh = o * rr
    sg = _sigmoid(gg)
    return oh, rr, sg


def _mix_fwd(x, attn, o_f, o_b, gg, ggla, mavg, wout, gt1, g2):
    S = x.shape[0]
    TM = 512 if S % 512 == 0 else 256

    def body(x_ref, a_ref, of_ref, ob_ref, gg_ref, ggla_ref, mavg_ref, w_ref, gt1_ref, g2_ref, x1_ref, mix_ref):
        gg_t = gg_ref[...]
        oh, _, sg = _gla_out(of_ref[...] + ob_ref[...], gg_t, ggla_ref[...], mavg_ref[...])
        mix_ref[:, 0:512] = a_ref[...]
        mix_ref[:, 512:1024] = (oh * ggla_ref[...] * (gg_t * sg)).astype(_BF)
        y = _nn(mix_ref[...], w_ref[...])
        ry = lax.rsqrt(jnp.mean(y * y, axis=-1, keepdims=True) + EPS)
        x1_ref[...] = x_ref[...] + gt1_ref[...] * ((y * ry) * g2_ref[...])

    return pl.pallas_call(
        body, name="mix_fwd", grid=(S // TM,),
        in_specs=[_rows(TM, D), _rows(TM, 512), _rows(TM, 512), _rows(TM, 512), _rows(TM, 512),
                  _full((1, 512)), _full((512, 512)), _full((D, D)), _full((1, D)), _full((1, D))],
        out_specs=[_rows(TM, D), _rows(TM, D)],
        out_shape=[jax.ShapeDtypeStruct((S, D), F32), jax.ShapeDtypeStruct((S, D), _BF)],
        compiler_params=_cp(("arbitrary",), 48 * 1024 * 1024),
    )(x, attn, o_f, o_b, gg, ggla, mavg, wout, gt1, g2)


def _mix_bwd(dx1, mix, o_f, o_b, gg, ggla, mavg, wout, gt1, g2, slabs):
    S = dx1.shape[0]
    TM = 512 if S % 512 == 0 else 256

    def body(dx_ref, mix_ref, of_ref, ob_ref, gg_ref, ggla_ref, mavg_ref, w_ref, gt1_ref, g2_ref,
             da_ref, do_ref, dgg_ref, dy_ref, sums_ref):
        @pl.when(pl.program_id(0) == 0)
        def _():
            sums_ref[...] = jnp.zeros_like(sums_ref)

        dx = dx_ref[...]
        y = _nn(mix_ref[...], w_ref[...])
        ry = lax.rsqrt(jnp.mean(y * y, axis=-1, keepdims=True) + EPS)
        yh = y * ry
        sums_ref[0:1, :] += jnp.sum(dx * yh, axis=0, keepdims=True)
        dyh = dx * (gt1_ref[...] * g2_ref[...])
        dy = (ry * (dyh - yh * jnp.mean(dyh * yh, axis=-1, keepdims=True))).astype(_BF)
        dy_ref[...] = dy
        dmix = _nt(dy, w_ref[...])
        da_ref[...] = dmix[:, 0:512].astype(_BF)
        dgla = dmix[:, 512:1024]
        gg_t = gg_ref[...]
        ggla_t = ggla_ref[...]
        oh, rr, sg = _gla_out(of_ref[...] + ob_ref[...], gg_t, ggla_t, mavg_ref[...])
        dgg_ref[...] = (dgla * oh * ggla_t * (sg * (1.0 + gg_t * (1.0 - sg)))).astype(_BF)
        don = dgla * (gg_t * sg)
        sums_ref[1:2, 0:512] += jnp.sum(don * oh, axis=0, keepdims=True)
        doh = don * ggla_t
        do_ref[...] = (rr * (doh - oh * _head_mean(doh * oh, mavg_ref[...]))).astype(_BF)

    return _hosted_call(
        body, (dx1, mix, o_f, o_b, gg, ggla, mavg, wout, gt1, g2), slabs, False,
        name="mix_bwd", grid=(S // TM,),
        in_specs=[_rows(TM, D), _rows(TM, D), _rows(TM, 512), _rows(TM, 512), _rows(TM, 512),
                  _full((1, 512)), _full((512, 512)), _full((D, D)), _full((1, D)), _full((1, D))],
        out_specs=[_rows(TM, 512), _rows(TM, 512), _rows(TM, 512), _rows(TM, D), _full((8, D))],
        out_shape=[jax.ShapeDtypeStruct((S, 512), _BF), jax.ShapeDtypeStruct((S, 512), _BF),
                   jax.ShapeDtypeStruct((S, 512), _BF), jax.ShapeDtypeStruct((S, D), _BF),
                   jax.ShapeDtypeStruct((8, D), F32)],
        compiler_params=_cp(("arbitrary",), 48 * 1024 * 1024))


def _ffn(x1, target, gm2, sh2, gt2, g4, wffi, wffo):
    S = x1.shape[0]
    TF = 256

    def body(x_ref, t_ref, gm_ref, sh_ref, gt_ref, g4_ref, wi_hbm, wo_hbm,
             dx_ref, h_ref, du_ref, act_ref, df_ref, sums_ref, loss_ref, wi, wo, sem):
        @pl.when(pl.program_id(0) == 0)
        def _():
            c1 = pltpu.make_async_copy(wi_hbm, wi, sem.at[0])
            c2 = pltpu.make_async_copy(wo_hbm, wo, sem.at[1])
            c1.start()
            c2.start()
            sums_ref[...] = jnp.zeros_like(sums_ref)
            loss_ref[...] = jnp.zeros_like(loss_ref)
            c1.wait()
            c2.wait()

        x = x_ref[...]
        gm = gm_ref[...]
        r = lax.rsqrt(jnp.mean(x * x, axis=-1, keepdims=True) + EPS)
        xh = x * r
        hb = (xh * gm + sh_ref[...]).astype(_BF)
        h_ref[...] = hb
        u = _nt(hb, wi[...])
        g = u[:, 0:FFN]
        up = u[:, FFN:2 * FFN]
        sg = _sigmoid(g)
        sl = g * sg
        ab = (sl * up).astype(_BF)
        act_ref[...] = ab
        f = _nn(ab, wo[...])
        rf = lax.rsqrt(jnp.mean(f * f, axis=-1, keepdims=True) + EPS)
        fh = f * rf
        gt, g4v = gt_ref[...], g4_ref[...]
        err = x + gt * (fh * g4v) - t_ref[...]
        loss_ref[...] += jnp.sum(err * err) * (0.5 / D)
        dout = err * (1.0 / D)
        sums_ref[2:3, :] += jnp.sum(dout * fh, axis=0, keepdims=True)
        dfh = dout * (gt * g4v)
        dfb = (rf * (dfh - fh * jnp.mean(dfh * fh, axis=-1, keepdims=True))).astype(_BF)
        df_ref[...] = dfb
        dact = _nt(dfb, wo[...])
        du_ref[:, 0:FFN] = (dact * up * (sg * (1.0 + g * (1.0 - sg)))).astype(_BF)
        du_ref[:, FFN:2 * FFN] = (dact * sl).astype(_BF)
        dh = _nn(du_ref[...], wi[...])
        sums_ref[0:1, :] += jnp.sum(dh, axis=0, keepdims=True)
        sums_ref[1:2, :] += jnp.sum(dh * xh, axis=0, keepdims=True)
        dxh = dh * gm
        dx_ref[...] = dout + r * (dxh - xh * jnp.mean(dxh * xh, axis=-1, keepdims=True))

    vec = _full((1, D))
    anyspec = pl.BlockSpec(memory_space=pl.ANY)
    return pl.pallas_call(
        body, name="ffn_fwd_bwd", grid=(S // TF,),
        in_specs=[_rows(TF, D), _rows(TF, D), vec, vec, vec, vec, anyspec, anyspec],
        out_specs=[_rows(TF, D), _rows(TF, D), _rows(TF, 2 * FFN), _rows(TF, FFN), _rows(TF, D),
                   _full((8, D)), _full((8, 128))],
        out_shape=[jax.ShapeDtypeStruct((S, D), F32), jax.ShapeDtypeStruct((S, D), _BF),
                   jax.ShapeDtypeStruct((S, 2 * FFN), _BF), jax.ShapeDtypeStruct((S, FFN), _BF),
                   jax.ShapeDtypeStruct((S, D), _BF), jax.ShapeDtypeStruct((8, D), F32),
                   jax.ShapeDtypeStruct((8, 128), F32)],
        scratch_shapes=[pltpu.VMEM((2 * FFN, D), _BF), pltpu.VMEM((FFN, D), _BF), pltpu.SemaphoreType.DMA((2,))],
        compiler_params=_cp(("arbitrary",), VMEM_BIG),
    )(x1, target, gm2, sh2, gt2, g4, wffi, wffo)


def _inproj_bwd(x, ctx, gml, gmc, win, wg, cos, sa, sb, dq, dk, dv, dgq, dgk, dgv, dgg, dlg_f, dlg_b, dx1):
    S = x.shape[0]
    E = S + CTX
    TE = CTX

    def body(x_ref, c_ref, gml_ref, gmc_ref, w_ref, wg_ref, cos_ref, sa_ref, sb_ref, dq_ref, dk_ref, dv_ref,
             gqf, gqb, gkf, gkb, gvf, gvb, dgg_ref, dlf, dlb, dx1_ref, dp_ref, gx_ref, sums_ref):
        i = pl.program_id(0)
        is_ctx = i == 0

        @pl.when(is_ctx)
        def _():
            sums_ref[...] = jnp.zeros_like(sums_ref)

        lat = jnp.where(is_ctx, 0.0, 1.0)
        cos_t, sa_t, sb_t = cos_ref[...], sa_ref[...], sb_ref[...]
        dp_ref[:, O_Q:O_K] = (_unrope(dq_ref[...].astype(F32), cos_t, sa_t, sb_t) * lat).astype(_BF)
        dp_ref[:, O_K:O_V] = _unrope(dk_ref[...].T, cos_t, sa_t, sb_t).astype(_BF)
        dp_ref[:, O_V:O_GQ] = dv_ref[...].T.astype(_BF)
        dp_ref[:, O_GQ:O_GK] = (gqf[...].astype(F32) + gqb[...].astype(F32)).astype(_BF)
        dp_ref[:, O_GK:O_GV] = (gkf[...].astype(F32) + gkb[...].astype(F32)).astype(_BF)
        dp_ref[:, O_GV:O_GG] = (gvf[...].astype(F32) + gvb[...].astype(F32)).astype(_BF)
        dp_ref[:, O_GG:O_Z] = (dgg_ref[...].astype(F32) * lat).astype(_BF)
        dlg = jnp.concatenate([dlf[...], dlb[...]], axis=1)
        dp_ref[:, O_Z:NP] = _nt(dlg, wg_ref[...]).astype(_BF)
        dh = _nn(dp_ref[...], w_ref[...])
        x = jnp.where(is_ctx, c_ref[...], x_ref[...])
        r = lax.rsqrt(jnp.mean(x * x, axis=-1, keepdims=True) + EPS)
        xh = x * r
        sdh = jnp.sum(dh, axis=0, keepdims=True)
        sdx = jnp.sum(dh * xh, axis=0, keepdims=True)
        sums_ref[0:1, :] += sdh * lat
        sums_ref[1:2, :] += sdx * lat
        sums_ref[2:3, :] += sdh * (1.0 - lat)
        sums_ref[3:4, :] += sdx * (1.0 - lat)
        dxh = dh * jnp.where(is_ctx, gmc_ref[...], gml_ref[...])
        gx_ref[...] = dx1_ref[...] + r * (dxh - xh * jnp.mean(dxh * xh, axis=-1, keepdims=True))

    vec = _full((1, D))
    tab = _rows(TE, 128)
    return pl.pallas_call(
        body, name="inproj_bwd", grid=(E // TE,),
        in_specs=[_rows_lat(TE, D), _full((CTX, D)), vec, vec, _full((NP, D)), _full((128, 512)), tab, tab, tab,
                  _rows_lat(TE, QP), pl.BlockSpec((KP, TE), lambda i: (0, i)), pl.BlockSpec((KP, TE), lambda i: (0, i)),
                  _rows(TE, 256), _rows(TE, 256), _rows(TE, 256), _rows(TE, 256), _rows(TE, 512), _rows(TE, 512),
                  _rows_lat(TE, 512), _rows(TE, 256), _rows(TE, 256), _rows_lat(TE, D)],
        out_specs=[_rows(TE, NP), _rows_lat(TE, D), _full((8, D))],
        out_shape=[jax.ShapeDtypeStruct((E, NP), _BF), jax.ShapeDtypeStruct((S, D), F32),
                   jax.ShapeDtypeStruct((8, D), F32)],
        compiler_params=_cp(("arbitrary",), VMEM_BIG),
    )(x, ctx, gml, gmc, win, wg, cos, sa, sb, dq, dk, dv, *dgq, *dgk, *dgv, dgg, dlg_f, dlg_b, dx1)


def _matmul_tn(a, b, tk, tt, name, out_dtype, transpose_out=False, a_cols=None, hosted=(), gather=True):
    T, KA = a.shape
    N = b.shape[1]
    nt = T // tt
    k0 = 0
    if a_cols is not None:
        KA, k0 = tk, a_cols

    def body(a_ref, b_ref, o_ref, acc):
        t = pl.program_id(1)

        @pl.when(t == 0)
        def _():
            acc[...] = jnp.zeros_like(acc)

        acc[...] += _tn(a_ref[...], b_ref[...])

        @pl.when(t == nt - 1)
        def _():
            o_ref[...] = (acc[...].T if transpose_out else acc[...]).astype(out_dtype)

    if transpose_out:
        out_spec, out_shape = pl.BlockSpec((N, tk), lambda i, t: (0, i)), (N, KA)
    else:
        out_spec, out_shape = pl.BlockSpec((tk, N), lambda i, t: (i, 0)), (KA, N)
    res = _hosted_call(
        body, (a, b), hosted, gather, name=name, grid=(KA // tk, nt),
        in_specs=[pl.BlockSpec((tt, tk), lambda i, t: (t, i + k0)), pl.BlockSpec((tt, N), lambda i, t: (t, 0))],
        out_specs=[out_spec], out_shape=[jax.ShapeDtypeStruct(out_shape, out_dtype)],
        scratch_shapes=[pltpu.VMEM((tk, N), F32)],
        compiler_params=_cp(("arbitrary", "arbitrary"), VMEM_BIG))
    return res if hosted else res[0]


def _ada_bwd(c_all, c_ctx, w_ada, d_all):
    n = w_ada.shape[1]

    def body(c_ref, cc_ref, w_ref, d_ref, gw_ref, t_ref):
        c = jnp.concatenate([c_ref[...], jnp.broadcast_to(cc_ref[...], (8, D))], axis=0)
        db = d_ref[...].astype(_BF)
        gw_ref[0] = _tn((c * _sigmoid(c)).astype(_BF), db)
        t_ref[...] = _nt(db[8:16], w_ref[...].astype(_BF))

    return pl.pallas_call(
        body, name="ada_bwd", in_specs=[_full((8, D)), _full((1, D)), _full((D, n)), _full((16, n))],
        out_specs=[_full((1, D, n)), _full((8, D))],
        out_shape=[jax.ShapeDtypeStruct((1, D, n), F32), jax.ShapeDtypeStruct((8, D), F32)], grid=(1,),
        compiler_params=_cp(("arbitrary",)),
    )(c_all, c_ctx, w_ada, d_all)


PART_ROWS = 56
R_ADA, R_ADA_C, R_GAIN, R_SINK, R_BG, R_GGLA, R_LOSS, R_WG = 0, 6, 12, 16, 17, 18, 19, 24


def _small_grads(s_in, s_ffn, s_mix, ada_l, ada_c, gains, dsink, s_bg, g_wg, loss):
    def body(si, sf, sm, al, ac, g, ds, sbg, gwg, loss_ref, o_ref):
        o_ref[...] = jnp.zeros_like(o_ref)
        o_ref[R_LOSS:R_LOSS + 1, 0:128] = loss_ref[0:1, :]
        sub = lax.broadcasted_iota(jnp.int32, (8, 128), 0)
        lane = lax.broadcasted_iota(jnp.int32, (8, 128), 1)
        o_ref[R_SINK:R_SINK + 1, 0:128] = jnp.sum(jnp.where(sub == lane, ds[...], 0.0), axis=0, keepdims=True)
        o_ref[R_BG:R_BG + 1, 0:512] = sbg[0:1, :]
        y = sm[1:2, 0:128] + sm[1:2, 128:256] + sm[1:2, 256:384] + sm[1:2, 384:512]
        y = y + pltpu.roll(y, 64, 1)
        o_ref[R_GGLA:R_GGLA + 1, 0:128] = jnp.where(lane[0:1] < DV, y, 0.0)
        o_ref[R_WG:R_WG + 16, 0:256] = gwg[0:16, 0:256]
        o_ref[R_WG + 16:R_WG + 32, 0:256] = gwg[16:32, 256:512]
        sdh_l, sdx_l, sdh_c, sdx_c = si[0:1], si[1:2], si[2:3], si[3:4]
        sdh2, sdx2, a2 = sf[0:1], sf[1:2], sf[2:3]
        a1 = sm[0:1]
        g1, g2, g3, g4 = g[0:1], g[1:2], g[2:3], g[3:4]
        sc1, gt1, sc2, gt2 = al[1:2], al[2:3], al[4:5], al[5:6]
        sc1c = ac[1:2]
        z = jnp.zeros((1, D), F32)
        rows = [sdh_l, sdx_l * g1, a1 * g2, sdh2, sdx2 * g3, a2 * g4,
                sdh_c, sdx_c * g1, z, z, z, z,
                sdx_l * (1.0 + sc1) + sdx_c * (1.0 + sc1c), a1 * gt1, sdx2 * (1.0 + sc2), a2 * gt2]
        for r, v in enumerate(rows):
            o_ref[r:r + 1, :] = v

    v8 = _full((8, D))
    return pl.pallas_call(
        body, name="small_grads",
        in_specs=[v8] * 6 + [_full((8, 128)), _full((8, 512)), _full((128, 512)), _full((8, 128))],
        out_specs=_full((PART_ROWS, D)), out_shape=jax.ShapeDtypeStruct((PART_ROWS, D), F32), grid=(1,),
        compiler_params=_cp(("arbitrary",)),
    )(s_in, s_ffn, s_mix, ada_l, ada_c, gains, dsink, s_bg, g_wg, loss)


def _row_tile(R):
    for cand in range(min(512, R // 2) // 16 * 16, 0, -16):
        if R % cand == 0:
            return cand
    return R


def _adamw(w, gs, m, v, name):
    _, R, C = w.shape
    tr = _row_tile(R)
    c1 = 1.0 / (1.0 - ADAM_B1 ** ADAM_STEP)
    c2 = 1.0 / (1.0 - ADAM_B2 ** ADAM_STEP)
    ng = len(gs)

    def body(w_ref, *refs):
        g_refs, (m_ref, v_ref, go_ref, d_ref, nm_ref, nv_ref) = refs[:ng], refs[ng:]
        c0 = 0
        for g_ref in g_refs:
            cols = slice(c0, c0 + g_ref.shape[2])
            c0 += g_ref.shape[2]
            gg = g_ref[0].astype(F32)
            for j in range(1, g_ref.shape[0]):
                gg = gg + g_ref[j].astype(F32)
            go_ref[0, :, cols] = gg
            nm = ADAM_B1 * m_ref[0, :, cols] + (1.0 - ADAM_B1) * gg
            nv = ADAM_B2 * v_ref[0, :, cols] + (1.0 - ADAM_B2) * (gg * gg)
            nm_ref[0, :, cols] = nm
            nv_ref[0, :, cols] = nv
            d_ref[0, :, cols] = -ADAM_LR * ((nm * c1) / (jnp.sqrt(nv * c2) + ADAM_EPS) + ADAM_WD * w_ref[0, :, cols])

    spec = pl.BlockSpec((1, tr, C), lambda i: (0, i, 0))
    sds = jax.ShapeDtypeStruct((1, R, C), F32)
    g_specs = [pl.BlockSpec((g.shape[0], tr, g.shape[2]), lambda i: (0, i, 0)) for g in gs]
    return pl.pallas_call(
        body, name=name, grid=(R // tr,), in_specs=[spec] + g_specs + [spec, spec], out_specs=[spec] * 4,
        out_shape=[sds] * 4, compiler_params=_cp(("parallel",), 48 * 1024 * 1024),
    )(w, *gs, m, v)


def _sum_slots(slots, name):
    _, R, C = slots.shape
    tr = _row_tile(R)

    def body(s_ref, o_ref):
        acc = s_ref[0].astype(F32)
        for j in range(1, N_DEV):
            acc = acc + s_ref[j].astype(F32)
        o_ref[...] = acc

    return pl.pallas_call(
        body, name=name, grid=(R // tr,), in_specs=[pl.BlockSpec((N_DEV, tr, C), lambda i: (0, i, 0))],
        out_specs=_rows(tr, C), out_shape=jax.ShapeDtypeStruct((R, C), F32), compiler_params=_cp(("parallel",)),
    )(slots)


def _ag2_start(x_ref, out_ref, send_sems, recv_sems, local_sem):
    x, y, c = lax.axis_index("x"), lax.axis_index("y"), lax.axis_index("c")
    me, sibling = (x, y, c), (x, y, 1 - c)
    chips = [(1 - x, y), (x, 1 - y), (1 - x, 1 - y)]

    def rows(px, py, pc):
        return out_ref.at[4 * px + 2 * py + pc]

    def copy(k, block, to, src=None):
        return pltpu.make_async_remote_copy(
            src_ref=rows(*block) if src is None else src, dst_ref=rows(*block),
            send_sem=send_sems.at[k], recv_sem=recv_sems.at[k], device_id=to, device_id_type=MESH)

    mine = pltpu.make_async_copy(x_ref, rows(*me), local_sem)
    mine.start()
    first = [copy(0, me, sibling, src=x_ref)]
    first += [copy(1 + j, me, (*chip, c), src=x_ref) for j, chip in enumerate(chips)]
    for cp in first:
        cp.start()
    return copy, mine, first, me, sibling, chips, c


def _ag2_finish(state):
    copy, mine, first, me, sibling, chips, c = state
    passed = [copy(4 + j, (*chip, c), sibling) for j, chip in enumerate(chips)]
    for j, chip in enumerate(chips):
        copy(1 + j, (*chip, c), me).wait_recv()
        passed[j].start()
    copy(0, sibling, me).wait_recv()
    for j, chip in enumerate(chips):
        copy(4 + j, (*chip, 1 - c), me).wait_recv()
    for cp in first + passed:
        cp.wait_send()
    mine.wait()


def _exchange(arrays, name, gather):
    na = len(arrays)

    def body(*refs):
        cps = _xchg_copies(refs[:na], refs[na:2 * na], *refs[2 * na:], gather=gather)
        _xchg_start(cps)
        _xchg_finish(cps)

    anyspec = pl.BlockSpec(memory_space=pl.ANY)
    return pl.pallas_call(
        body, name=name, out_shape=_xchg_out_shapes(arrays, gather), in_specs=[anyspec] * na,
        out_specs=[anyspec] * na, scratch_shapes=_xchg_scratch(na),
    )(*arrays)


def _entry(c, wg_sh, win_sh, c_ctx, w_ada):
    n = w_ada.shape[1]

    def body(c_ref, g_ref, w_ref, cc_ref, wa_ref, call_ref, gall_ref, wall_ref, ada_ref, part,
             s_send, s_recv, s_loc, w_send, w_recv, w_loc, a_send, a_recv, a_loc):
        big = _ag2_start(w_ref, wall_ref, w_send, w_recv, w_loc)
        small = _xchg_copies([c_ref, g_ref], [call_ref, gall_ref], s_send, s_recv, s_loc, gather=True)
        _xchg_start(small)
        _xchg_finish(small)
        cs = jnp.concatenate([call_ref[:, 0, :], jnp.broadcast_to(cc_ref[...], (8, D))], axis=0)
        part[...] = _nn((cs * _sigmoid(cs)).astype(_BF), wa_ref[...].astype(_BF))
        ada = _xchg_copies([part], [ada_ref], a_send, a_recv, a_loc, gather=True)
        _xchg_start(ada)
        _xchg_finish(ada)
        _ag2_finish(big)

    vm = pl.BlockSpec(memory_space=pltpu.VMEM)
    return pl.pallas_call(
        body, name="entry_gather",
        out_shape=[jax.ShapeDtypeStruct((N_DEV,) + c.shape, F32), jax.ShapeDtypeStruct((N_DEV,) + wg_sh.shape, F32),
                   jax.ShapeDtypeStruct((N_DEV,) + win_sh.shape, win_sh.dtype),
                   jax.ShapeDtypeStruct((N_DEV, 16, n), F32)],
        in_specs=[vm] * 5, out_specs=[vm] * 4,
        scratch_shapes=[pltpu.VMEM((16, n), F32)] + _xchg_scratch(2)
        + [pltpu.SemaphoreType.DMA((7,)), pltpu.SemaphoreType.DMA((7,)), pltpu.SemaphoreType.DMA] + _xchg_scratch(1),
        compiler_params=pltpu.CompilerParams(vmem_limit_bytes=VMEM_BIG),
    )(c, wg_sh, win_sh, c_ctx, w_ada)


def _rope_tables(S):
    t = np.arange(S)
    row = (t // GRID_W).astype(np.float32)
    colp = (t % GRID_W).astype(np.float32)
    half = HD // 2
    inv = (ROPE_BASE ** (-np.arange(0, half, 2, dtype=np.float32) / half)).astype(np.float32)
    ar = row[:, None] * inv[None, :]
    ac = colp[:, None] * inv[None, :]
    ang = np.concatenate([ar, ar, ac, ac], axis=-1).astype(np.float32)
    cos = np.cos(ang).astype(np.float32)
    sin = np.sin(ang).astype(np.float32)
    lane = np.arange(HD)
    first = (lane % 32) < 16
    sa = np.where(first[None, :], -sin, 0.0)
    sb = np.where(first[None, :], 0.0, sin)

    def ext(tab, ctx_val):
        full = np.zeros((CTX + S, 128), np.float32)
        full[:CTX, :] = ctx_val
        full[CTX:, :HD] = tab
        full[CTX:, HD:] = tab
        return jnp.asarray(full)

    return ext(cos, 1.0), ext(sa, 0.0), ext(sb, 0.0)


def _pad_rows_win(wt):
    return jnp.pad(wt, ((0, NP - IN_COLS), (0, 0)))


def _unpad_rows_win(g):
    return g[0:IN_COLS]


def _local_step(x, ctx, target, ada_l, ada_c, gains, sink, win_p, wg_bd, bg, ggla, wout_sh, wffi_sh, wffo_sh,
                after_small):
    S = x.shape[0]
    cos, sa, sb = _rope_tables(S)
    g1, g2, g3, g4 = (gains[i:i + 1] for i in range(4))
    sh1, sc1, gt1, sh2, sc2, gt2 = (ada_l[i:i + 1] for i in range(6))
    sh1c, sc1c = ada_c[0:1], ada_c[1:2]
    gml, gmc, gm2 = g1 * (1.0 + sc1), g1 * (1.0 + sc1c), g3 * (1.0 + sc2)
    mavg = jnp.asarray(np.kron(np.eye(N_GLA, dtype=np.float32), np.full((DV, DV), 1.0 / DV, np.float32))).astype(_BF)

    n_ffi, r_ffo, r_out = wffi_sh.shape[0], wffo_sh.shape[0], wout_sh.shape[0]
    tt_e = 1408 if (S + CTX) % 1408 == 0 else 256
    tt_s = 512 if S % 512 == 0 else 256
    h, q, k, v, gq, gk, gv, gg, z, la, wout_g = _inproj_fwd(x, ctx, gml, sh1, gmc, sh1c, win_p, wg_bd, bg,
                                                            cos, sa, sb, [wout_sh])
    attn, lse, probs, wffi_g = _attn_fwd(q, k, v, sink, [wffi_sh])
    o_f, st_f, o_b, st_b, wffo_g = _gla_fwd(gq, gk, gv, la, [wffo_sh])
    wout = wout_g.reshape(N_DEV * r_out, D)
    wffi = wffi_g.reshape(N_DEV * n_ffi, D)
    wffo = wffo_g.reshape(N_DEV * r_ffo, D)
    x1, mix = _mix_fwd(x, attn, o_f, o_b, gg, ggla, mavg, wout, gt1, g2)
    dx1, h2, du, act, df, s_ffn, loss = _ffn(x1, target, gm2, sh2, gt2, g4, wffi, wffo)
    slab_ffi = _matmul_tn(h2, du, 512, tt_s, "grad_w_ffn_in", _BF, True).reshape(N_DEV, n_ffi, D)
    tt_l = 1024 if S % 1024 == 0 else tt_s
    slab_ffo = _matmul_tn(act, df, FFN, tt_l, "grad_w_ffn_out", _BF).reshape(N_DEV, r_ffo, D)
    d_attn, do_gla, dgg, dy, s_mix, got_ffo = _mix_bwd(dx1, mix, o_f, o_b, gg, ggla, mavg, wout, gt1, g2, [slab_ffo])
    slab_out = _matmul_tn(mix, dy, D, tt_l, "grad_w_out", _BF).reshape(N_DEV, r_out, D)
    dq, dk, dv, dsink, got_ffi = _attn_bwd(q, k, v, sink, probs, lse, d_attn, [slab_ffi])
    (dgq_f, dgk_f, dgv_f, dlg_f, dgq_b, dgk_b, dgv_b, dlg_b, g_wg, s_bg,
     got_out) = _gla_bwd(gq, gk, gv, la, z, st_f, st_b, do_gla, [slab_out])
    dp, grad_x, s_in = _inproj_bwd(x, ctx, gml, gmc, win_p, wg_bd, cos, sa, sb, dq, dk, dv,
                                   (dgq_f, dgq_b), (dgk_f, dgk_b), (dgv_f, dgv_b), dgg, dlg_f, dlg_b, dx1)
    small = _small_grads(s_in, s_ffn, s_mix, ada_l, ada_c, gains, dsink, s_bg, g_wg, loss)
    n_in, n_grp = IN_COLS // N_DEV, 2
    got_in, slab, extra, got_extra = [], None, [], []
    for j in range(n_grp):
        res = _matmul_tn(h, dp, D // n_grp, tt_e, "grad_w_in_%d" % j, _BF, True, a_cols=j,
                         hosted=[small] if j == 0 else [slab] + extra, gather=(j == 0))
        if j == 0:
            extra = after_small(res[1])
        else:
            got_in.append(res[1])
            got_extra, extra = got_extra + list(res[2:]), []
        slab = _unpad_rows_win(res[0]).reshape(N_DEV, n_in, D // n_grp)
    got_in.append(_exchange([slab], "scatter_grads", False)[0])
    return dict(grad_x=grad_x, got_in=got_in, got_out=got_out, got_ffi=got_ffi, got_ffo=got_ffo, got_extra=got_extra)


SMALL_NAMES = ["c_ctx", "b_ada", "g_pre_mix", "g_post_mix", "g_pre_ffn", "g_post_ffn", "attn_sink",
               "b_gate_fwd", "b_gate_bwd", "g_gla_norm", "w_gate_fwd", "w_gate_bwd"]


def _small_update(tot, t_tot, wg_g, w, m, v):
    c1 = 1.0 / (1.0 - ADAM_B1 ** ADAM_STEP)
    c2 = 1.0 / (1.0 - ADAM_B2 ** ADAM_STEP)
    n = len(SMALL_NAMES)

    def body(tot_ref, t_ref, wg_ref, *refs):
        w_r, m_r, v_r = refs[0:n], refs[n:2 * n], refs[2 * n:3 * n]
        g_o, d_o, nm_o, nv_o = refs[3 * n:4 * n], refs[4 * n:5 * n], refs[5 * n:6 * n], refs[6 * n:7 * n]

        def upd(i, idx, g):
            nm = ADAM_B1 * m_r[i][idx] + (1.0 - ADAM_B1) * g
            nv = ADAM_B2 * v_r[i][idx] + (1.0 - ADAM_B2) * (g * g)
            g_o[i][idx] = g
            nm_o[i][idx] = nm
            nv_o[i][idx] = nv
            d_o[i][idx] = -ADAM_LR * ((nm * c1) / (jnp.sqrt(nv * c2) + ADAM_EPS) + ADAM_WD * w_r[i][idx])

        everything = (slice(None), slice(None))
        cc = w_r[0][...]
        sc = _sigmoid(cc)
        upd(0, everything, t_ref[0:1, :] * (sc * (1.0 + cc * (1.0 - sc))))
        for j in range(6):
            upd(1, (slice(None), slice(D * j, D * j + D)),
                tot_ref[R_ADA + j:R_ADA + j + 1, :] + tot_ref[R_ADA_C + j:R_ADA_C + j + 1, :])
        for j in range(4):
            upd(2 + j, everything, tot_ref[R_GAIN + j:R_GAIN + j + 1, :])
        upd(6, everything, tot_ref[R_SINK:R_SINK + 1, 0:N_ATT])
        upd(7, everything, tot_ref[R_BG:R_BG + 1, 0:256])
        upd(8, everything, tot_ref[R_BG:R_BG + 1, 256:512])
        upd(9, everything, tot_ref[R_GGLA:R_GGLA + 1, 0:DV])
        upd(10, (0,), wg_ref[0:GATE_RANK, :])
        upd(11, (0,), wg_ref[GATE_RANK:2 * GATE_RANK, :])

    params = [w[k] for k in SMALL_NAMES] + [m[k] for k in SMALL_NAMES] + [v[k] for k in SMALL_NAMES]
    outs = pl.pallas_call(
        body, name="small_update", grid=(1,),
        in_specs=[_full(tot.shape), _full(t_tot.shape), _full(wg_g.shape)] + [_full(p.shape) for p in params],
        out_specs=[_full(w[k].shape) for k in SMALL_NAMES] * 4,
        out_shape=[jax.ShapeDtypeStruct(w[k].shape, F32) for k in SMALL_NAMES] * 4,
        compiler_params=_cp(("arbitrary",)),
    )(tot, t_tot, wg_g, *params)
    return tuple(dict(zip(SMALL_NAMES, outs[i * n:(i + 1) * n])) for i in range(4))


def kernel(x, c, ctx, c_ctx, w_ada, b_ada, g_pre_mix, g_post_mix, g_pre_ffn, g_post_ffn, w_in, attn_sink, w_gate_fwd, b_gate_fwd, w_gate_bwd, b_gate_bwd, g_gla_norm, w_out, w_ffn_in, w_ffn_out, loss_target, m_c_ctx, m_w_ada, m_b_ada, m_g_pre_mix, m_g_post_mix, m_g_pre_ffn, m_g_post_ffn, m_w_in, m_attn_sink, m_w_gate_fwd, m_b_gate_fwd, m_w_gate_bwd, m_b_gate_bwd, m_g_gla_norm, m_w_out, m_w_ffn_in, m_w_ffn_out, v_c_ctx, v_w_ada, v_b_ada, v_g_pre_mix, v_g_post_mix, v_g_pre_ffn, v_g_post_ffn, v_w_in, v_attn_sink, v_w_gate_fwd, v_b_gate_fwd, v_w_gate_bwd, v_b_gate_bwd, v_g_gla_norm, v_w_out, v_w_ffn_in, v_w_ffn_out):
    me = 4 * lax.axis_index("x") + 2 * lax.axis_index("y") + lax.axis_index("c")
    S = x.shape[1]
    n_in = w_in.shape[2]
    n_ffi = w_ffn_in.shape[2]
    r_out = w_out.shape[1]
    r_ffo = w_ffn_out.shape[1]
    n_ada = w_ada.shape[2]

    wg_sh = jnp.concatenate([w_gate_fwd.reshape(4, 128), w_gate_bwd.reshape(4, 128)], axis=0)
    c_all3, g_all, w_all, ada_all = _entry(c, wg_sh, w_in[0].T.astype(_BF), c_ctx.reshape(1, D), w_ada[0])
    c_all = c_all3.reshape(N_DEV, D)
    wgf = g_all[:, 0:4].reshape(N_DEV, GATE_RANK, 32).transpose(1, 0, 2).reshape(GATE_RANK, 256)
    wgb = g_all[:, 4:8].reshape(N_DEV, GATE_RANK, 32).transpose(1, 0, 2).reshape(GATE_RANK, 256)
    win_p = _pad_rows_win(w_all.reshape(N_DEV * n_in, D))
    wg_bd = jnp.zeros((128, 512), F32).at[0:16, 0:256].set(wgf).at[16:32, 256:512].set(wgb).astype(_BF)
    ada_full = ada_all.transpose(1, 0, 2).reshape(16, N_DEV * n_ada) + b_ada
    ada_l = jnp.pad(lax.dynamic_slice_in_dim(ada_full, me, 1, 0).reshape(6, D), ((0, 2), (0, 0)))
    ada_c = jnp.pad(ada_full[8].reshape(6, D), ((0, 2), (0, 0)))
    gains = jnp.pad(jnp.concatenate([g_pre_mix, g_post_mix, g_pre_ffn, g_post_ffn], axis=0), ((0, 4), (0, 0)))
    sink = jnp.broadcast_to(attn_sink.reshape(8, 1), (8, 128))
    bg = jnp.concatenate([b_gate_fwd, b_gate_bwd], axis=1)
    ggla = jnp.tile(g_gla_norm, (1, N_GLA))

    tail = {}

    def after_small(parts):
        tot = _sum_slots(parts, "sum_small_grads")
        d_ada_rows = parts[:, R_ADA:R_ADA + 6].reshape(N_DEV, 6 * D)
        d_ada_c = tot[R_ADA_C:R_ADA_C + 6].reshape(1, 6 * D)
        my_cols = lax.dynamic_slice_in_dim(jnp.concatenate([d_ada_rows, jnp.broadcast_to(d_ada_c, (1, 6 * D)),
                                                            jnp.zeros((7, 6 * D), F32)], axis=0), me * n_ada, n_ada, 1)
        grad_w_ada, t_part = _ada_bwd(c_all, c_ctx.reshape(1, D), w_ada[0], my_cols)
        tail.update(tot=tot, grad_w_ada=grad_w_ada)
        return [jnp.broadcast_to(t_part[None], (N_DEV,) + t_part.shape)]

    r = _local_step(x[0], ctx[0], loss_target[0], ada_l, ada_c, gains, sink, win_p, wg_bd, bg, ggla,
                    w_out[0].astype(_BF), w_ffn_in[0].T.astype(_BF), w_ffn_out[0].astype(_BF), after_small)
    tot, grad_w_ada = tail["tot"], tail["grad_w_ada"]
    loss = tot[R_LOSS, 0]
    wg_g = lax.dynamic_slice(tot, (R_WG, me * 32), (2 * GATE_RANK, 32))

    tr = lambda a: jnp.transpose(a, (0, 2, 1))
    big = {}
    t_tot = _sum_slots(r["got_extra"][0], "sum_c_ctx")
    for nm, w, g, m, v in [("w_ada", w_ada, grad_w_ada, m_w_ada, v_w_ada),
                           ("w_out", w_out, r["got_out"], m_w_out, v_w_out),
                           ("w_ffn_out", w_ffn_out, r["got_ffo"], m_w_ffn_out, v_w_ffn_out)]:
        big[nm] = _adamw(w, [g], m, v, "adamw_" + nm)
    big["w_ffn_in"] = tuple(tr(o) for o in _adamw(tr(w_ffn_in), [r["got_ffi"]], tr(m_w_ffn_in), tr(v_w_ffn_in),
                                                  "adamw_w_ffn_in"))
    big["w_in"] = tuple(tr(o) for o in _adamw(tr(w_in), r["got_in"], tr(m_w_in), tr(v_w_in), "adamw_w_in"))

    w_small = dict(c_ctx=c_ctx.reshape(1, D), b_ada=b_ada, g_pre_mix=g_pre_mix, g_post_mix=g_post_mix, g_pre_ffn=g_pre_ffn,
                   g_post_ffn=g_post_ffn, attn_sink=attn_sink, b_gate_fwd=b_gate_fwd, b_gate_bwd=b_gate_bwd,
                   g_gla_norm=g_gla_norm, w_gate_fwd=w_gate_fwd, w_gate_bwd=w_gate_bwd)
    m_small = dict(c_ctx=m_c_ctx.reshape(1, D), b_ada=m_b_ada, g_pre_mix=m_g_pre_mix, g_post_mix=m_g_post_mix,
                   g_pre_ffn=m_g_pre_ffn, g_post_ffn=m_g_post_ffn, attn_sink=m_attn_sink, b_gate_fwd=m_b_gate_fwd,
                   b_gate_bwd=m_b_gate_bwd, g_gla_norm=m_g_gla_norm, w_gate_fwd=m_w_gate_fwd, w_gate_bwd=m_w_gate_bwd)
    v_small = dict(c_ctx=v_c_ctx.reshape(1, D), b_ada=v_b_ada, g_pre_mix=v_g_pre_mix, g_post_mix=v_g_post_mix,
                   g_pre_ffn=v_g_pre_ffn, g_post_ffn=v_g_post_ffn, attn_sink=v_attn_sink, b_gate_fwd=v_b_gate_fwd,
                   b_gate_bwd=v_b_gate_bwd, g_gla_norm=v_g_gla_norm, w_gate_fwd=v_w_gate_fwd, w_gate_bwd=v_w_gate_bwd)
    grads_small, d_s, nm_s, nv_s = _small_update(tot, t_tot, wg_g, w_small, m_small, v_small)
    for dd in (grads_small, d_s, nm_s, nv_s):
        dd["c_ctx"] = dd["c_ctx"].reshape(D)

    order = ["c_ctx", "w_ada", "b_ada", "g_pre_mix", "g_post_mix", "g_pre_ffn", "g_post_ffn", "w_in", "attn_sink",
             "w_gate_fwd", "b_gate_fwd", "w_gate_bwd", "b_gate_bwd", "g_gla_norm", "w_out", "w_ffn_in", "w_ffn_out"]
    grads, deltas, new_m, new_v = [], [], [], []
    for nm in order:
        if nm in big:
            g_, d_, m_, v_ = big[nm]
        else:
            g_, d_, m_, v_ = grads_small[nm], d_s[nm], nm_s[nm], nv_s[nm]
        grads.append(g_)
        deltas.append(d_)
        new_m.append(m_)
        new_v.append(v_)
    return (loss, r["grad_x"][None], *grads, *deltas, *new_m, *new_v)
```

```python
import functools

import numpy as np
import jax
import jax.numpy as jnp
from jax import lax
from jax.experimental import pallas as pl
from jax.experimental.pallas import tpu as pltpu

F32 = jnp.float32
_BF = jnp.bfloat16

N_DEV = 8
D = 1024
CTX = 256
HD = 64
N_ATT = 8
N_KV = 2
GRP = N_ATT // N_KV
WIN = 128
GRID_W = 64
ROPE_BASE = 10000.0
N_GLA = 8
DK = 32
DV = 64
GATE_RANK = 16
GATE_TAU = 16.0
FFN = 2816
EPS = 1e-6
NEG = -1e30
GLA_T = 128

QP = N_ATT * HD
KP = N_KV * HD
O_Q, O_K, O_V = 0, QP, QP + KP
O_GQ = O_V + KP
O_GK = O_GQ + N_GLA * DK
O_GV = O_GK + N_GLA * DK
O_GG = O_GV + N_GLA * DV
O_Z = O_GG + N_GLA * DV
NP = O_Z + 128
IN_COLS = 2336

ADAM_LR, ADAM_B1, ADAM_B2, ADAM_EPS, ADAM_WD, ADAM_STEP = 0.001, 0.9, 0.999, 1e-08, 0.01, 10

VMEM_BIG = 56 * 1024 * 1024
MESH = pl.DeviceIdType.MESH


def _cp(sem, vmem=None):
    return pltpu.CompilerParams(dimension_semantics=sem, vmem_limit_bytes=vmem)


def _full(shape):
    nd = len(shape)
    return pl.BlockSpec(shape, lambda *a: (0,) * nd)


def _rows(tile, width, off=0):
    return pl.BlockSpec((tile, width), lambda i: (i + off, 0))


def _rows_lat(tile, width):
    return pl.BlockSpec((tile, width), lambda i: (jnp.maximum(i - 1, 0), 0))


def _nt(a, b):
    return lax.dot_general(a, b, (((1,), (1,)), ((), ())), preferred_element_type=F32)


def _tn(a, b):
    return lax.dot_general(a, b, (((0,), (0,)), ((), ())), preferred_element_type=F32)


def _nn(a, b):
    return jnp.dot(a, b, preferred_element_type=F32)


def _head_mean(x, mavg):
    n = x.shape[0]
    hi = x.astype(_BF)
    lo = (x - hi.astype(F32)).astype(_BF)
    y = _nn(jnp.concatenate([hi, lo], axis=0), mavg)
    return y[0:n] + y[n:2 * n]


def _rope(t, cos, sa, sb):
    n = t.shape[1]
    reps = n // 128
    c = jnp.tile(cos, (1, reps))
    a = jnp.tile(sa, (1, reps))
    b = jnp.tile(sb, (1, reps))
    return t * c + pltpu.roll(t, n - 16, 1) * a + pltpu.roll(t, 16, 1) * b


def _unrope(t, cos, sa, sb):
    n = t.shape[1]
    reps = n // 128
    c = jnp.tile(cos, (1, reps))
    a = jnp.tile(sa, (1, reps))
    b = jnp.tile(sb, (1, reps))
    return t * c + pltpu.roll(t * a, 16, 1) + pltpu.roll(t * b, n - 16, 1)


def _sigmoid(x):
    return 1.0 / (1.0 + jnp.exp(-x))


def _inproj_fwd(x, ctx, gml, shl, gmc, shc, win, wg, bg, cos, sa, sb, shards):
    E = x.shape[0] + CTX
    TE = CTX

    def body(x_ref, c_ref, gml_ref, shl_ref, gmc_ref, shc_ref, w_ref, wg_ref, bg_ref, cos_ref, sa_ref, sb_ref,
             h_ref, q_ref, k_ref, v_ref, gq_ref, gk_ref, gv_ref, gg_ref, z_ref, la_ref):
        is_ctx = pl.program_id(0) == 0
        gm = jnp.where(is_ctx, gmc_ref[...], gml_ref[...])
        sh = jnp.where(is_ctx, shc_ref[...], shl_ref[...])
        x = jnp.where(is_ctx, c_ref[...], x_ref[...])
        r = lax.rsqrt(jnp.mean(x * x, axis=-1, keepdims=True) + EPS)
        hb = ((x * r) * gm + sh).astype(_BF)
        h_ref[...] = hb
        p = _nt(hb, w_ref[...])
        cos_t, sa_t, sb_t = cos_ref[...], sa_ref[...], sb_ref[...]
        q_ref[...] = (_rope(p[:, O_Q:O_K], cos_t, sa_t, sb_t) * (HD ** -0.5)).astype(_BF)
        k_ref[...] = _rope(p[:, O_K:O_V], cos_t, sa_t, sb_t).astype(_BF)
        v_ref[...] = p[:, O_V:O_GQ].astype(_BF)
        gq_ref[...] = p[:, O_GQ:O_GK] * (DK ** -0.5)
        gk_ref[...] = p[:, O_GK:O_GV]
        gv_ref[...] = p[:, O_GV:O_GG]
        gg_ref[...] = p[:, O_GG:O_Z]
        zb = p[:, O_Z:NP].astype(_BF)
        z_ref[...] = zb
        lg = _nn(zb, wg_ref[...]) + bg_ref[...]
        la_ref[...] = (jnp.minimum(lg, 0.0) - jnp.log(1.0 + jnp.exp(-jnp.abs(lg)))) * (1.0 / GATE_TAU)

    vec = _full((1, D))
    tab = _rows(TE, 128)
    outs = [(D, _BF), (QP, _BF), (KP, _BF), (KP, _BF), (256, F32), (256, F32), (512, F32), (512, F32),
            (128, _BF), (512, F32)]
    return _hosted_call(
        body, (x, ctx, gml, shl, gmc, shc, win, wg, bg, cos, sa, sb), shards, True,
        name="inproj_fwd", grid=(E // TE,),
        in_specs=[_rows_lat(TE, D), _full((CTX, D)), vec, vec, vec, vec, _full((NP, D)), _full((128, 512)),
                  _full((1, 512)), tab, tab, tab],
        out_specs=[_rows_lat(TE, w) if i == 7 else _rows(TE, w) for i, (w, _) in enumerate(outs)],
        out_shape=[jax.ShapeDtypeStruct((E - CTX if i == 7 else E, w), dt) for i, (w, dt) in enumerate(outs)],
        compiler_params=_cp(("arbitrary",), 40 * 1024 * 1024))


def _xchg_scratch(na):
    return [pltpu.SemaphoreType.DMA((na, N_DEV - 1)), pltpu.SemaphoreType.DMA((na, N_DEV - 1)),
            pltpu.SemaphoreType.DMA((na,))]


def _xchg_copies(ins, outs, send_sems, recv_sems, local_sems, gather):
    x, y, c = lax.axis_index("x"), lax.axis_index("y"), lax.axis_index("c")
    me = 4 * x + 2 * y + c
    local, sends, recvs = [], [], []
    for a in range(len(ins)):
        local.append(pltpu.make_async_copy(ins[a] if gather else ins[a].at[me], outs[a].at[me], local_sems.at[a]))
    for k in range(1, N_DEV):
        px, py, pc = x ^ (k >> 2), y ^ ((k >> 1) & 1), c ^ (k & 1)
        peer = 4 * px + 2 * py + pc
        for a in range(len(ins)):
            sems = dict(send_sem=send_sems.at[a, k - 1], recv_sem=recv_sems.at[a, k - 1], device_id_type=MESH)
            sends.append(pltpu.make_async_remote_copy(
                src_ref=ins[a] if gather else ins[a].at[peer], dst_ref=outs[a].at[me], device_id=(px, py, pc), **sems))
            recvs.append(pltpu.make_async_remote_copy(
                src_ref=ins[a] if gather else ins[a].at[me], dst_ref=outs[a].at[peer], device_id=(x, y, c), **sems))
    return local, sends, recvs


def _xchg_start(cps):
    local, sends, _ = cps
    for cp in local + sends:
        cp.start()


def _xchg_finish(cps):
    local, sends, recvs = cps
    for cp in recvs:
        cp.wait_recv()
    for cp in sends:
        cp.wait_send()
    for cp in local:
        cp.wait()


def _xchg_out_shapes(ins, gather):
    return [jax.ShapeDtypeStruct(((N_DEV,) + s.shape) if gather else s.shape, s.dtype) for s in ins]


def _hosted_call(body, args, hosted, gather, *, grid, in_specs, out_specs, out_shape, scratch_shapes=(), **kw):
    na = len(hosted)
    if na == 0:
        return pl.pallas_call(body, grid=grid, in_specs=in_specs, out_specs=out_specs, out_shape=out_shape,
                              scratch_shapes=list(scratch_shapes), **kw)(*args)
    n_in, n_out, n_scr = len(in_specs), len(out_specs), len(scratch_shapes)

    def wrapped(*refs):
        ins, h_in = refs[:n_in], refs[n_in:n_in + na]
        outs, h_out = refs[n_in + na:n_in + na + n_out], refs[n_in + na + n_out:n_in + 2 * na + n_out]
        scr = refs[n_in + 2 * na + n_out:]
        cps = _xchg_copies(h_in, h_out, *scr[n_scr:], gather=gather)
        pids = [pl.program_id(a) for a in range(len(grid))]
        first = functools.reduce(jnp.logical_and, [p == 0 for p in pids])
        last = functools.reduce(jnp.logical_and, [p == g - 1 for p, g in zip(pids, grid)])

        @pl.when(first)
        def _():
            _xchg_start(cps)

        body(*ins, *outs, *scr[:n_scr])

        @pl.when(last)
        def _():
            _xchg_finish(cps)

    anyspec = pl.BlockSpec(memory_space=pl.ANY)
    return pl.pallas_call(
        wrapped, grid=grid, in_specs=list(in_specs) + [anyspec] * na, out_specs=list(out_specs) + [anyspec] * na,
        out_shape=list(out_shape) + _xchg_out_shapes(hosted, gather),
        scratch_shapes=list(scratch_shapes) + _xchg_scratch(na), **kw)(*args, *hosted)


ATT_BLOCKS = 2
PROB_ROWS = N_KV * GRP * WIN


def _attn_specs(E):
    nb = (E - CTX) // WIN
    last = E // WIN - 1
    kc = pl.BlockSpec((CTX, KP), lambda m: (0, 0))
    ks = [pl.BlockSpec((WIN, KP), functools.partial(lambda m, j: (jnp.minimum(ATT_BLOCKS * m + j, last), 0), j=j))
          for j in range(1, ATT_BLOCKS + 3)]
    return nb, [kc] + ks


def _attn_bias(nb):
    rows = np.arange(GRP * WIN)[:, None] % WIN
    cols = np.arange(CTX + 3 * WIN)[None, :]
    j = cols - CTX
    band = np.abs(j - WIN - rows) <= WIN
    out = []
    for first, last in ((True, False), (False, False), (False, True)):
        ok = (cols < CTX) | (band & ((j >= WIN) | (not first)) & ((j < 2 * WIN) | (not last)))
        out.append(np.where(ok, 0.0, NEG).astype(np.float32))
    bias = jnp.asarray(np.stack(out))
    steps = nb // ATT_BLOCKS
    shape = (1, GRP * WIN, CTX + 3 * WIN)
    specs = [pl.BlockSpec(shape, lambda m: (jnp.where(m == 0, 0, 1), 0, 0))]
    specs += [pl.BlockSpec(shape, lambda m: (1, 0, 0))] * (ATT_BLOCKS - 2)
    specs += [pl.BlockSpec(shape, lambda m: (jnp.where(m == steps - 1, 2, 1), 0, 0))]
    return bias, specs


def _both_halves(t, h):
    tf = t.astype(F32)
    r = pltpu.roll(tf, HD, 1)
    lo = lax.broadcasted_iota(jnp.int32, tf.shape, 1) < HD
    return (jnp.where(lo, tf, r) if h == 0 else jnp.where(lo, r, tf)).astype(t.dtype)


def _stack_heads(ref, h):
    lo = lax.broadcasted_iota(jnp.int32, (WIN, 128), 1) < HD
    parts = []
    for g in range(GRP):
        j = GRP * h + g
        t = ref[:, 128 * (j // 2):128 * (j // 2) + 128].astype(F32)
        parts.append(jnp.where(lo if j % 2 == 0 else jnp.logical_not(lo), t, 0.0))
    return jnp.concatenate(parts, axis=0)


def _unstack_pair(o, pp):
    lo = lax.broadcasted_iota(jnp.int32, (WIN, 128), 1) < HD
    return jnp.where(lo, o[WIN * 2 * pp:WIN * 2 * pp + WIN], o[WIN * (2 * pp + 1):WIN * (2 * pp + 1) + WIN])


def _attn_fwd(q, k, v, sink, shards):
    E = q.shape[0]
    S = E - CTX
    nb, kspecs = _attn_specs(E)
    na = len(shards)

    nk = ATT_BLOCKS + 3

    def one_block(q_ref, kw, vw, sink_ref, bias_ref, o_ref, lse_ref, p_ref):
        lane = lax.broadcasted_iota(jnp.int32, (WIN, 128), 1)
        lse_t = jnp.zeros((WIN, 128), F32)
        kall = jnp.concatenate([r[...] for r in kw], axis=0)
        vall = jnp.concatenate([r[...] for r in vw], axis=0)
        K = [_both_halves(kall, h) for h in range(N_KV)]
        Q = [_stack_heads(q_ref, h).astype(_BF) for h in range(N_KV)]
        sk = [jnp.concatenate([jnp.broadcast_to(sink_ref[GRP * h + g:GRP * h + g + 1, 0:1], (WIN, 1))
                               for g in range(GRP)], axis=0) for h in range(N_KV)]
        s = [_nt(Q[h], K[h]) + bias_ref[0] for h in range(N_KV)]
        m = [jnp.maximum(jnp.max(s[h], axis=1, keepdims=True), sk[h]) for h in range(N_KV)]
        e = [jnp.exp(s[h] - m[h]) for h in range(N_KV)]
        den = [jnp.sum(e[h], axis=1, keepdims=True) + jnp.exp(sk[h] - m[h]) for h in range(N_KV)]
        V = [_both_halves(vall, h) for h in range(N_KV)]
        pb = [(e[h] * (1.0 / den[h])).astype(_BF) for h in range(N_KV)]
        for h in range(N_KV):
            p_ref[GRP * WIN * h:GRP * WIN * (h + 1), :] = pb[h]
        o = [_nn(pb[h], V[h]) for h in range(N_KV)]
        for h in range(N_KV):
            lse = m[h] + jnp.log(den[h])
            for g in range(GRP):
                lse_t = jnp.where(lane == GRP * h + g, lse[WIN * g:WIN * g + WIN], lse_t)
            for pp in range(GRP // 2):
                t = 2 * h + pp
                o_ref[:, 128 * t:128 * t + 128] = _unstack_pair(o[h], pp).astype(_BF)
        lse_ref[...] = lse_t

    def body(q_ref, *rest):
        kr, vr, sink_ref = rest[0:nk], rest[nk:2 * nk], rest[2 * nk]
        bias_refs = rest[2 * nk + 1:2 * nk + 1 + ATT_BLOCKS]
        rest = rest[2 * nk + 1 + ATT_BLOCKS:]
        shard_refs, (o_ref, lse_ref, p_ref), got_refs = rest[:na], rest[na:na + 3], rest[na + 3:2 * na + 3]
        cps = _xchg_copies(shard_refs, got_refs, *rest[2 * na + 3:], gather=True)

        @pl.when(pl.program_id(0) == 0)
        def _():
            _xchg_start(cps)

        for j in range(ATT_BLOCKS):
            rows = pl.ds(WIN * j, WIN)
            one_block(q_ref.at[rows], [kr[0]] + list(kr[1 + j:4 + j]), [vr[0]] + list(vr[1 + j:4 + j]), sink_ref,
                      bias_refs[j], o_ref.at[rows], lse_ref.at[rows], p_ref.at[pl.ds(PROB_ROWS * j, PROB_ROWS)])

        @pl.when(pl.program_id(0) == nb // ATT_BLOCKS - 1)
        def _():
            _xchg_finish(cps)

    tq = ATT_BLOCKS * WIN
    qs = pl.BlockSpec((tq, QP), lambda m: (m + CTX // tq, 0))
    anyspec = pl.BlockSpec(memory_space=pl.ANY)
    bias, bias_specs = _attn_bias(nb)
    return pl.pallas_call(
        body, name="attn_fwd", grid=(nb // ATT_BLOCKS,),
        in_specs=[qs] + kspecs + kspecs + [_full((8, 128))] + bias_specs + [anyspec] * na,
        out_specs=[_rows(tq, 512), _rows(tq, 128), _rows(ATT_BLOCKS * PROB_ROWS, CTX + 3 * WIN)] + [anyspec] * na,
        out_shape=[jax.ShapeDtypeStruct((S, 512), _BF), jax.ShapeDtypeStruct((S, 128), F32),
                   jax.ShapeDtypeStruct((nb * PROB_ROWS, CTX + 3 * WIN), _BF)]
        + _xchg_out_shapes(shards, True),
        scratch_shapes=_xchg_scratch(na),
        compiler_params=_cp(("arbitrary",), 48 * 1024 * 1024),
    )(q, *([k] * nk), *([v] * nk), sink, *([bias] * ATT_BLOCKS), *shards)


def _attn_bwd(q, k, v, sink, probs, lse, d_attn, slabs):
    E = q.shape[0]
    S = E - CTX
    nb, kspecs = _attn_specs(E)
    last = E // WIN - 1
    na = len(slabs)

    nk = ATT_BLOCKS + 3

    def one_block(n, q_ref, kw, vw, sink_ref, p_ref, lse_ref, do_ref, dq_ref, dk_ref, dv_ref, ds_ref):
        lane = lax.broadcasted_iota(jnp.int32, (WIN, 128), 1)
        lse_t = lse_ref[...]
        starts = [None, pl.multiple_of((n + 1) * WIN, WIN), pl.multiple_of((n + 2) * WIN, WIN),
                  pl.multiple_of(jnp.minimum(n + 3, last) * WIN, WIN)]
        kall = jnp.concatenate([r[...] for r in kw], axis=0)
        vall = jnp.concatenate([r[...] for r in vw], axis=0)
        for h in range(N_KV):
            hs = slice(HD * h, HD * h + HD)
            K = _both_halves(kall, h)
            V = _both_halves(vall, h)
            Q = _stack_heads(q_ref, h).astype(_BF)
            sk = jnp.concatenate([jnp.broadcast_to(sink_ref[GRP * h + g:GRP * h + g + 1, 0:1], (WIN, 1))
                                  for g in range(GRP)], axis=0)
            ls = jnp.concatenate([jnp.sum(jnp.where(lane == GRP * h + g, lse_t, 0.0), axis=1, keepdims=True)
                                  for g in range(GRP)], axis=0)
            do = _stack_heads(do_ref, h).astype(_BF)
            pb = p_ref[GRP * WIN * h:GRP * WIN * (h + 1), :]
            p = pb.astype(F32)
            dp = _nt(do, V)
            delta = jnp.sum(p * dp, axis=1, keepdims=True)
            dsc = (p * (dp - delta)).astype(_BF)
            dq = _nn(dsc, K) * (HD ** -0.5)
            for pp in range(GRP // 2):
                t = 2 * h + pp
                dq_ref[:, 128 * t:128 * t + 128] = _unstack_pair(dq, pp).astype(_BF)
            dK2 = _tn(Q, dsc)
            dV2 = _tn(do, pb)
            dK = dK2[0:HD] + dK2[HD:2 * HD]
            dV = dV2[0:HD] + dV2[HD:2 * HD]
            dk_ref[hs, 0:CTX] += dK[:, 0:CTX]
            dv_ref[hs, 0:CTX] += dV[:, 0:CTX]
            for w in range(1, 4):
                lo = CTX + WIN * (w - 1)
                dk_ref[hs, pl.ds(starts[w], WIN)] += dK[:, lo:lo + WIN]
                dv_ref[hs, pl.ds(starts[w], WIN)] += dV[:, lo:lo + WIN]
            psk = -jnp.exp(sk - ls) * delta
            for g in range(GRP):
                j = GRP * h + g
                tot = jnp.sum(psk[WIN * g:WIN * g + WIN], axis=0, keepdims=True)
                ds_ref[j:j + 1, :] += jnp.broadcast_to(tot, (1, 128))

    def body(q_ref, *rest):
        kr, vr = rest[0:nk], rest[nk:2 * nk]
        sink_ref, p_ref, lse_ref, do_ref = rest[2 * nk:2 * nk + 4]
        rest = rest[2 * nk + 4:]
        slab_refs, (dq_ref, dk_ref, dv_ref, ds_ref), got_refs = rest[:na], rest[na:na + 4], rest[na + 4:2 * na + 4]
        m = pl.program_id(0)
        cps = _xchg_copies(slab_refs, got_refs, *rest[2 * na + 4:], gather=False)

        @pl.when(m == 0)
        def _():
            _xchg_start(cps)
            dk_ref[...] = jnp.zeros_like(dk_ref)
            dv_ref[...] = jnp.zeros_like(dv_ref)
            ds_ref[...] = jnp.zeros_like(ds_ref)

        for j in range(ATT_BLOCKS):
            rows = pl.ds(WIN * j, WIN)
            one_block(ATT_BLOCKS * m + j, q_ref.at[rows], [kr[0]] + list(kr[1 + j:4 + j]),
                      [vr[0]] + list(vr[1 + j:4 + j]), sink_ref, p_ref.at[pl.ds(PROB_ROWS * j, PROB_ROWS)],
                      lse_ref.at[rows], do_ref.at[rows], dq_ref.at[rows], dk_ref, dv_ref, ds_ref)

        @pl.when(m == nb // ATT_BLOCKS - 1)
        def _():
            _xchg_finish(cps)

    tq = ATT_BLOCKS * WIN
    qs = pl.BlockSpec((tq, QP), lambda m: (m + CTX // tq, 0))
    anyspec = pl.BlockSpec(memory_space=pl.ANY)
    return pl.pallas_call(
        body, name="attn_bwd", grid=(nb // ATT_BLOCKS,),
        in_specs=[qs] + kspecs + kspecs + [_full((8, 128)), _rows(ATT_BLOCKS * PROB_ROWS, CTX + 3 * WIN),
                                            _rows(tq, 128), _rows(tq, 512)] + [anyspec] * na,
        out_specs=[_rows(tq, QP), _full((KP, E)), _full((KP, E)), _full((8, 128))] + [anyspec] * na,
        out_shape=[jax.ShapeDtypeStruct((S, QP), _BF), jax.ShapeDtypeStruct((KP, E), F32),
                   jax.ShapeDtypeStruct((KP, E), F32), jax.ShapeDtypeStruct((8, 128), F32)]
        + _xchg_out_shapes(slabs, False),
        scratch_shapes=_xchg_scratch(na),
        compiler_params=_cp(("arbitrary",), 48 * 1024 * 1024),
    )(q, *([k] * nk), *([v] * nk), sink, probs, lse, d_attn, *slabs)


GLA_STEP = 2


def _gla_subs(reverse, backward):
    subs = list(range(GLA_STEP))
    return subs[::-1] if reverse != backward else subs


def _gla_order(E, reverse, backward):
    nc = CTX // (GLA_STEP * GLA_T)
    n = E // (GLA_STEP * GLA_T)
    if not reverse:
        fwd = lambda s: s
    else:
        fwd = lambda s: jnp.where(s < nc, nc - 1 - s, n - 1 + nc - s)
    if backward:
        return lambda s: fwd(n - 1 - s)
    return fwd


def _gla_masks():
    T = GLA_T
    l128 = lax.broadcasted_iota(jnp.int32, (1, 128), 1)
    qmask = [((l128 >> 5) == j).astype(F32) for j in range(4)]
    vmask = [((l128 >> 6) == j).astype(F32) for j in range(2)]
    bd = ((lax.broadcasted_iota(jnp.int32, (512, 256), 0) >> 6)
          == (lax.broadcasted_iota(jnp.int32, (512, 256), 1) >> 5)).astype(F32)
    ri = lax.broadcasted_iota(jnp.int32, (T, 2 * T), 0)
    ci = lax.broadcasted_iota(jnp.int32, (T, 2 * T), 1) & (T - 1)
    return qmask, vmask, bd, ri, ci


def _tri_sum(tri, x):
    hi = x.astype(_BF)
    lo = (x - hi.astype(F32)).astype(_BF)
    n = x.shape[1]
    y = _nn(tri.astype(_BF), jnp.concatenate([hi, lo], axis=1))
    return y[:, 0:n] + y[:, n:2 * n]


def _gla_decays(la, reverse, ri, ci):
    T = GLA_T
    msk2 = (ri <= ci) if reverse else (ri >= ci)
    mskT2 = (ri >= ci) if reverse else (ri <= ci)
    b = _tri_sum(msk2[:, 0:T], la)
    bT = b[0:1] if reverse else b[T - 1:T]
    bm = b[T // 2:T // 2 + 1]
    return msk2, mskT2, b, bT, bm


def _pair_stack(tile, m0, m1):
    return jnp.concatenate([(tile * m0).astype(_BF), (tile * m1).astype(_BF)], axis=0)


def _gla_fwd(gq, gk, gv, la, shards=()):
    E = gq.shape[0]
    T = GLA_T
    n = E // T
    TB = GLA_STEP * T
    orders = [_gla_order(E, False, False), _gla_order(E, True, False)]

    def one_direction(reverse, *refs):
        for j in _gla_subs(reverse, False):
            rows = pl.ds(T * j, T)
            one_chunk(reverse, *[r.at[rows] for r in refs[:5]], refs[5].at[j], refs[6])

    def one_chunk(reverse, gq_ref, gk_ref, gv_ref, la_ref, o_ref, st_ref, S_scr):
        qmask, vmask, bd, ri, ci = _gla_masks()
        msk2, _, b, bT, bm = _gla_decays(la_ref[...], reverse, ri, ci)
        q, k, v = gq_ref[...], gk_ref[...], gv_ref[...]
        qd = (q * jnp.exp(b)).astype(_BF)
        qm = (q * jnp.exp(b - bm)).astype(_BF)
        km = k * jnp.exp(bm - b)
        kd = (k * jnp.exp(bT - b)).astype(_BF)
        ST = S_scr[...]
        comp = ST[0:DV]
        for h in range(1, N_GLA):
            comp = comp + ST[DV * h:DV * h + DV]
        st_ref[...] = comp
        inter = _nt(qd, ST.astype(_BF))
        tiles = []
        for p in range(N_GLA // 2):
            qs = slice(128 * (p // 2), 128 * (p // 2) + 128)
            vs = slice(128 * p, 128 * p + 128)
            j0 = (2 * p) % 4
            KS = _pair_stack(km[:, qs], qmask[j0], qmask[j0 + 1])
            VS = _pair_stack(v[:, vs], vmask[0], vmask[1])
            AA = jnp.where(msk2, _nt(qm[:, qs], KS), 0.0).astype(_BF)
            tiles.append(_nn(AA, VS))
        o_ref[...] = inter + jnp.concatenate(tiles, axis=1)
        S_scr[...] = ST * jnp.exp(bT) + bd * _tn(v.astype(_BF), kd)

    def body(qf, kf, vf, lf, qr, kr, vr, lr, of, sf, orr, sr, S_f, S_r):
        @pl.when(pl.program_id(0) == 0)
        def _():
            S_f[...] = jnp.zeros_like(S_f)
            S_r[...] = jnp.zeros_like(S_r)

        one_direction(False, qf, kf, vf, lf, of, sf, S_f)
        one_direction(True, qr, kr, vr, lr, orr, sr, S_r)

    def blk(d, w, c=0):
        return pl.BlockSpec((TB, w), lambda s: (orders[d](s), c))

    def st_spec(d):
        return pl.BlockSpec((GLA_STEP, DV, 256), lambda s: (orders[d](s), 0, 0))

    def o_spec(d):
        return pl.BlockSpec((TB, 512), lambda s: (orders[d](jnp.maximum(s, CTX // TB)) - CTX // TB, 0))

    return _hosted_call(
        body, (gq, gk, gv, la, gq, gk, gv, la), shards, True, name="gla_fwd", grid=(n // GLA_STEP,),
        in_specs=[blk(0, 256), blk(0, 256), blk(0, 512), blk(0, 256, 0), blk(1, 256), blk(1, 256), blk(1, 512),
                  blk(1, 256, 1)],
        out_specs=[o_spec(0), st_spec(0), o_spec(1), st_spec(1)],
        out_shape=[jax.ShapeDtypeStruct((E - CTX, 512), F32), jax.ShapeDtypeStruct((n, DV, 256), F32)] * 2,
        scratch_shapes=[pltpu.VMEM((512, 256), F32)] * 2,
        compiler_params=_cp(("arbitrary",)))


def _gla_bwd(gq, gk, gv, la, st_f, st_r, do, slabs=()):
    E = gq.shape[0]
    T = GLA_T
    n = E // T
    TB = GLA_STEP * T
    nc = CTX // TB
    orders = [_gla_order(E, False, True), _gla_order(E, True, True)]

    def one_direction(reverse, *refs):
        ins, outs, shared = refs[0:6], refs[6:10], refs[10:12]
        for j in _gla_subs(reverse, True):
            rows = pl.ds(T * j, T)
            views = [r.at[rows] for r in ins[0:4]] + [ins[4].at[j], ins[5].at[rows]] + [r.at[rows] for r in outs]
            one_chunk(reverse, *views, *shared)

    def one_chunk(reverse, gq_ref, gk_ref, gv_ref, la_ref, st_ref, do_ref,
                  dq_ref, dk_ref, dv_ref, dlg_ref, bsum_ref, dS_scr):
        cols = slice(256, 512) if reverse else slice(0, 256)
        is_lat = orders[1 if reverse else 0](pl.program_id(0)) >= nc
        qmask, vmask, bd, ri, ci = _gla_masks()
        msk2, mskT2, b, bT, bm = _gla_decays(la_ref[...], reverse, ri, ci)
        q, k, v = gq_ref[...], gk_ref[...], gv_ref[...]
        do = jnp.where(is_lat, do_ref[...].astype(F32), 0.0)
        e_b, e_qm, e_km, e_kd, e_T = jnp.exp(b), jnp.exp(b - bm), jnp.exp(bm - b), jnp.exp(bT - b), jnp.exp(bT)
        qd, qm, km, kd = q * e_b, q * e_qm, k * e_km, k * e_kd
        qdb, qmb, kmb, kdb, vb, dob = (t.astype(_BF) for t in (qd, qm, km, kd, v, do))
        ST = jnp.tile(st_ref[...], (N_GLA, 1)) * bd
        dST = dS_scr[...]
        dSTb = dST.astype(_BF)
        dqd = _nn(dob, ST.astype(_BF))
        dkd = _nn(vb, dSTb)
        dv_t, dqm_t, dkm_t = [], [None, None], [None, None]
        for p in range(N_GLA // 2):
            t = p // 2
            qs = slice(128 * t, 128 * t + 128)
            vs = slice(128 * p, 128 * p + 128)
            j0 = (2 * p) % 4
            QS = _pair_stack(qm[:, qs], qmask[j0], qmask[j0 + 1])
            KS = _pair_stack(km[:, qs], qmask[j0], qmask[j0 + 1])
            VS = _pair_stack(v[:, vs], vmask[0], vmask[1])
            DS = _pair_stack(do[:, vs], vmask[0], vmask[1])
            ATT = jnp.where(mskT2, _nt(kmb[:, qs], QS), 0.0).astype(_BF)
            dAA = jnp.where(msk2, _nt(dob[:, vs], VS), 0.0).astype(_BF)
            dATT = jnp.where(mskT2, _nt(vb[:, vs], DS), 0.0).astype(_BF)
            dv_t.append(_nn(ATT, DS))
            dq_p = _nn(dAA, KS)
            dk_p = _nn(dATT, QS)
            dqm_t[t] = dq_p if dqm_t[t] is None else dqm_t[t] + dq_p
            dkm_t[t] = dk_p if dkm_t[t] is None else dkm_t[t] + dk_p
        dqm = jnp.concatenate(dqm_t, axis=1)
        dkm = jnp.concatenate(dkm_t, axis=1)
        dq = dqm * e_qm + dqd * e_b
        dk = dkm * e_km + dkd * e_kd
        dv = _nt(kdb, dSTb) + jnp.concatenate(dv_t, axis=1)
        dq_ref[...] = (dq * (DK ** -0.5)).astype(_BF)
        dk_ref[...] = dk.astype(_BF)
        dv_ref[...] = dv.astype(_BF)
        db = dqm * qm - dkm * km + dqd * qd - dkd * kd
        dbT = jnp.sum(dkd * kd, axis=0, keepdims=True) + e_T * jnp.sum(dST * ST, axis=0, keepdims=True)
        dla = _tri_sum(mskT2[:, 0:T], db) + dbT
        dlg = dla * (1.0 - jnp.exp(GATE_TAU * la_ref[...])) * (1.0 / GATE_TAU)
        bsum_ref[0:1, cols] += jnp.sum(dlg, axis=0, keepdims=True)
        dlg_ref[...] = dlg.astype(_BF)
        dS_scr[...] = dST * e_T + bd * _tn(dob, qdb)

    def body(*refs):
        ins_f, ins_r = refs[0:6], refs[6:12]
        outs_f, outs_r = refs[12:16], refs[16:20]
        bsum_ref, dS_f, dS_r = refs[20:23]

        @pl.when(pl.program_id(0) == 0)
        def _():
            dS_f[...] = jnp.zeros_like(dS_f)
            dS_r[...] = jnp.zeros_like(dS_r)
            bsum_ref[...] = jnp.zeros_like(bsum_ref)

        one_direction(False, *ins_f, *outs_f, bsum_ref, dS_f)
        one_direction(True, *ins_r, *outs_r, bsum_ref, dS_r)

    def specs(d, st):
        order = orders[d]
        blk = lambda w, c=0: pl.BlockSpec((TB, w), lambda s: (order(s), c))
        ins = [blk(256), blk(256), blk(512), blk(256, d),
               pl.BlockSpec((GLA_STEP, DV, 256), lambda s: (order(s), 0, 0)),
               pl.BlockSpec((TB, 512), lambda s: (jnp.maximum(order(s) - nc, 0), 0))]
        return ins, [blk(256), blk(256), blk(512), blk(256)], (gq, gk, gv, la, st, do)

    in_f, out_f, args_f = specs(0, st_f)
    in_r, out_r, args_r = specs(1, st_r)
    dir_shapes = [jax.ShapeDtypeStruct((E, 256), _BF), jax.ShapeDtypeStruct((E, 256), _BF),
                  jax.ShapeDtypeStruct((E, 512), _BF), jax.ShapeDtypeStruct((E, 256), _BF)]
    return _hosted_call(
        body, args_f + args_r, slabs, False, name="gla_bwd", grid=(n // GLA_STEP,),
        in_specs=in_f + in_r, out_specs=out_f + out_r + [_full((8, 512))],
        out_shape=dir_shapes * 2 + [jax.ShapeDtypeStruct((8, 512), F32)],
        scratch_shapes=[pltpu.VMEM((512, 256), F32)] * 2,
        compiler_params=_cp(("arbitrary",)))


def _gla_out(o, gg, ggla, mavg):
    rr = lax.rsqrt(_head_mean(o * o, mavg) + EPS)
    oh = o * rr
    sg = _sigmoid(gg)
    return oh, rr, sg


def _mix_fwd(x, attn, o_f, o_b, gg, ggla, mavg, wout, gt1, g2):
    S = x.shape[0]
    TM = 512 if S % 512 == 0 else 256

    def body(x_ref, a_ref, of_ref, ob_ref, gg_ref, ggla_ref, mavg_ref, w_ref, gt1_ref, g2_ref, x1_ref, mix_ref):
        gg_t = gg_ref[...]
        oh, _, sg = _gla_out(of_ref[...] + ob_ref[...], gg_t, ggla_ref[...], mavg_ref[...])
        mix_ref[:, 0:512] = a_ref[...]
        mix_ref[:, 512:1024] = (oh * ggla_ref[...] * (gg_t * sg)).astype(_BF)
        y = _nn(mix_ref[...], w_ref[...])
        ry = lax.rsqrt(jnp.mean(y * y, axis=-1, keepdims=True) + EPS)
        x1_ref[...] = x_ref[...] + gt1_ref[...] * ((y * ry) * g2_ref[...])

    return pl.pallas_call(
        body, name="mix_fwd", grid=(S // TM,),
        in_specs=[_rows(TM, D), _rows(TM, 512), _rows(TM, 512), _rows(TM, 512), _rows(TM, 512),
                  _full((1, 512)), _full((512, 512)), _full((D, D)), _full((1, D)), _full((1, D))],
        out_specs=[_rows(TM, D), _rows(TM, D)],
        out_shape=[jax.ShapeDtypeStruct((S, D), F32), jax.ShapeDtypeStruct((S, D), _BF)],
        compiler_params=_cp(("arbitrary",), 48 * 1024 * 1024),
    )(x, attn, o_f, o_b, gg, ggla, mavg, wout, gt1, g2)


def _mix_bwd(dx1, mix, o_f, o_b, gg, ggla, mavg, wout, gt1, g2, slabs):
    S = dx1.shape[0]
    TM = 512 if S % 512 == 0 else 256

    def body(dx_ref, mix_ref, of_ref, ob_ref, gg_ref, ggla_ref, mavg_ref, w_ref, gt1_ref, g2_ref,
             da_ref, do_ref, dgg_ref, dy_ref, sums_ref):
        @pl.when(pl.program_id(0) == 0)
        def _():
            sums_ref[...] = jnp.zeros_like(sums_ref)

        dx = dx_ref[...]
        y = _nn(mix_ref[...], w_ref[...])
        ry = lax.rsqrt(jnp.mean(y * y, axis=-1, keepdims=True) + EPS)
        yh = y * ry
        sums_ref[0:1, :] += jnp.sum(dx * yh, axis=0, keepdims=True)
        dyh = dx * (gt1_ref[...] * g2_ref[...])
        dy = (ry * (dyh - yh * jnp.mean(dyh * yh, axis=-1, keepdims=True))).astype(_BF)
        dy_ref[...] = dy
        dmix = _nt(dy, w_ref[...])
        da_ref[...] = dmix[:, 0:512].astype(_BF)
        dgla = dmix[:, 512:1024]
        gg_t = gg_ref[...]
        ggla_t = ggla_ref[...]
        oh, rr, sg = _gla_out(of_ref[...] + ob_ref[...], gg_t, ggla_t, mavg_ref[...])
        dgg_ref[...] = (dgla * oh * ggla_t * (sg * (1.0 + gg_t * (1.0 - sg)))).astype(_BF)
        don = dgla * (gg_t * sg)
        sums_ref[1:2, 0:512] += jnp.sum(don * oh, axis=0, keepdims=True)
        doh = don * ggla_t
        do_ref[...] = (rr * (doh - oh * _head_mean(doh * oh, mavg_ref[...]))).astype(_BF)

    return _hosted_call(
        body, (dx1, mix, o_f, o_b, gg, ggla, mavg, wout, gt1, g2), slabs, False,
        name="mix_bwd", grid=(S // TM,),
        in_specs=[_rows(TM, D), _rows(TM, D), _rows(TM, 512), _rows(TM, 512), _rows(TM, 512),
                  _full((1, 512)), _full((512, 512)), _full((D, D)), _full((1, D)), _full((1, D))],
        out_specs=[_rows(TM, 512), _rows(TM, 512), _rows(TM, 512), _rows(TM, D), _full((8, D))],
        out_shape=[jax.ShapeDtypeStruct((S, 512), _BF), jax.ShapeDtypeStruct((S, 512), _BF),
                   jax.ShapeDtypeStruct((S, 512), _BF), jax.ShapeDtypeStruct((S, D), _BF),
                   jax.ShapeDtypeStruct((8, D), F32)],
        compiler_params=_cp(("arbitrary",), 48 * 1024 * 1024))


def _ffn(x1, target, gm2, sh2, gt2, g4, wffi, wffo):
    S = x1.shape[0]
    TF = 256

    def body(x_ref, t_ref, gm_ref, sh_ref, gt_ref, g4_ref, wi_hbm, wo_hbm,
             dx_ref, h_ref, du_ref, act_ref, df_ref, sums_ref, loss_ref, wi, wo, sem):
        @pl.when(pl.program_id(0) == 0)
        def _():
            c1 = pltpu.make_async_copy(wi_hbm, wi, sem.at[0])
            c2 = pltpu.make_async_copy(wo_hbm, wo, sem.at[1])
            c1.start()
            c2.start()
            sums_ref[...] = jnp.zeros_like(sums_ref)
            loss_ref[...] = jnp.zeros_like(loss_ref)
            c1.wait()
            c2.wait()

        x = x_ref[...]
        gm = gm_ref[...]
        r = lax.rsqrt(jnp.mean(x * x, axis=-1, keepdims=True) + EPS)
        xh = x * r
        hb = (xh * gm + sh_ref[...]).astype(_BF)
        h_ref[...] = hb
        u = _nt(hb, wi[...])
        g = u[:, 0:FFN]
        up = u[:, FFN:2 * FFN]
        sg = _sigmoid(g)
        sl = g * sg
        ab = (sl * up).astype(_BF)
        act_ref[...] = ab
        f = _nn(ab, wo[...])
        rf = lax.rsqrt(jnp.mean(f * f, axis=-1, keepdims=True) + EPS)
        fh = f * rf
        gt, g4v = gt_ref[...], g4_ref[...]
        err = x + gt * (fh * g4v) - t_ref[...]
        loss_ref[...] += jnp.sum(err * err) * (0.5 / D)
        dout = err * (1.0 / D)
        sums_ref[2:3, :] += jnp.sum(dout * fh, axis=0, keepdims=True)
        dfh = dout * (gt * g4v)
        dfb = (rf * (dfh - fh * jnp.mean(dfh * fh, axis=-1, keepdims=True))).astype(_BF)
        df_ref[...] = dfb
        dact = _nt(dfb, wo[...])
        du_ref[:, 0:FFN] = (dact * up * (sg * (1.0 + g * (1.0 - sg)))).astype(_BF)
        du_ref[:, FFN:2 * FFN] = (dact * sl).astype(_BF)
        dh = _nn(du_ref[...], wi[...])
        sums_ref[0:1, :] += jnp.sum(dh, axis=0, keepdims=True)
        sums_ref[1:2, :] += jnp.sum(dh * xh, axis=0, keepdims=True)
        dxh = dh * gm
        dx_ref[...] = dout + r * (dxh - xh * jnp.mean(dxh * xh, axis=-1, keepdims=True))

    vec = _full((1, D))
    anyspec = pl.BlockSpec(memory_space=pl.ANY)
    return pl.pallas_call(
        body, name="ffn_fwd_bwd", grid=(S // TF,),
        in_specs=[_rows(TF, D), _rows(TF, D), vec, vec, vec, vec, anyspec, anyspec],
        out_specs=[_rows(TF, D), _rows(TF, D), _rows(TF, 2 * FFN), _rows(TF, FFN), _rows(TF, D),
                   _full((8, D)), _full((8, 128))],
        out_shape=[jax.ShapeDtypeStruct((S, D), F32), jax.ShapeDtypeStruct((S, D), _BF),
                   jax.ShapeDtypeStruct((S, 2 * FFN), _BF), jax.ShapeDtypeStruct((S, FFN), _BF),
                   jax.ShapeDtypeStruct((S, D), _BF), jax.ShapeDtypeStruct((8, D), F32),
                   jax.ShapeDtypeStruct((8, 128), F32)],
        scratch_shapes=[pltpu.VMEM((2 * FFN, D), _BF), pltpu.VMEM((FFN, D), _BF), pltpu.SemaphoreType.DMA((2,))],
        compiler_params=_cp(("arbitrary",), VMEM_BIG),
    )(x1, target, gm2, sh2, gt2, g4, wffi, wffo)


def _inproj_bwd(x, ctx, gml, gmc, win, wg, cos, sa, sb, dq, dk, dv, dgq, dgk, dgv, dgg, dlg_f, dlg_b, dx1, z):
    S = x.shape[0]
    E = S + CTX
    TE = CTX

    def body(x_ref, c_ref, gml_ref, gmc_ref, w_ref, wg_ref, cos_ref, sa_ref, sb_ref, dq_ref, dk_ref, dv_ref,
             gqf, gqb, gkf, gkb, gvf, gvb, dgg_ref, dlf, dlb, dx1_ref, z_ref, dp_ref, gx_ref, sums_ref, gwg_ref):
        i = pl.program_id(0)
        is_ctx = i == 0

        @pl.when(is_ctx)
        def _():
            sums_ref[...] = jnp.zeros_like(sums_ref)
            gwg_ref[...] = jnp.zeros_like(gwg_ref)

        lat = jnp.where(is_ctx, 0.0, 1.0)
        cos_t, sa_t, sb_t = cos_ref[...], sa_ref[...], sb_ref[...]
        dp_ref[:, O_Q:O_K] = (_unrope(dq_ref[...].astype(F32), cos_t, sa_t, sb_t) * lat).astype(_BF)
        dp_ref[:, O_K:O_V] = _unrope(dk_ref[...].T, cos_t, sa_t, sb_t).astype(_BF)
        dp_ref[:, O_V:O_GQ] = dv_ref[...].T.astype(_BF)
        dp_ref[:, O_GQ:O_GK] = (gqf[...].astype(F32) + gqb[...].astype(F32)).astype(_BF)
        dp_ref[:, O_GK:O_GV] = (gkf[...].astype(F32) + gkb[...].astype(F32)).astype(_BF)
        dp_ref[:, O_GV:O_GG] = (gvf[...].astype(F32) + gvb[...].astype(F32)).astype(_BF)
        dp_ref[:, O_GG:O_Z] = (dgg_ref[...].astype(F32) * lat).astype(_BF)
        dlg = jnp.concatenate([dlf[...], dlb[...]], axis=1)
        gwg_ref[...] += _tn(z_ref[...], dlg)
        dp_ref[:, O_Z:NP] = _nt(dlg, wg_ref[...]).astype(_BF)
        dh = _nn(dp_ref[...], w_ref[...])
        x = jnp.where(is_ctx, c_ref[...], x_ref[...])
        r = lax.rsqrt(jnp.mean(x * x, axis=-1, keepdims=True) + EPS)
        xh = x * r
        sdh = jnp.sum(dh, axis=0, keepdims=True)
        sdx = jnp.sum(dh * xh, axis=0, keepdims=True)
        sums_ref[0:1, :] += sdh * lat
        sums_ref[1:2, :] += sdx * lat
        sums_ref[2:3, :] += sdh * (1.0 - lat)
        sums_ref[3:4, :] += sdx * (1.0 - lat)
        dxh = dh * jnp.where(is_ctx, gmc_ref[...], gml_ref[...])
        gx_ref[...] = dx1_ref[...] + r * (dxh - xh * jnp.mean(dxh * xh, axis=-1, keepdims=True))

    vec = _full((1, D))
    tab = _rows(TE, 128)
    return pl.pallas_call(
        body, name="inproj_bwd", grid=(E // TE,),
        in_specs=[_rows_lat(TE, D), _full((CTX, D)), vec, vec, _full((NP, D)), _full((128, 512)), tab, tab, tab,
                  _rows_lat(TE, QP), pl.BlockSpec((KP, TE), lambda i: (0, i)), pl.BlockSpec((KP, TE), lambda i: (0, i)),
                  _rows(TE, 256), _rows(TE, 256), _rows(TE, 256), _rows(TE, 256), _rows(TE, 512), _rows(TE, 512),
                  _rows_lat(TE, 512), _rows(TE, 256), _rows(TE, 256), _rows_lat(TE, D), _rows(TE, 128)],
        out_specs=[_rows(TE, NP), _rows_lat(TE, D), _full((8, D)), _full((128, 512))],
        out_shape=[jax.ShapeDtypeStruct((E, NP), _BF), jax.ShapeDtypeStruct((S, D), F32),
                   jax.ShapeDtypeStruct((8, D), F32), jax.ShapeDtypeStruct((128, 512), F32)],
        compiler_params=_cp(("arbitrary",), VMEM_BIG),
    )(x, ctx, gml, gmc, win, wg, cos, sa, sb, dq, dk, dv, *dgq, *dgk, *dgv, dgg, dlg_f, dlg_b, dx1, z)


def _matmul_tn(a, b, tk, tt, name, out_dtype, transpose_out=False, a_cols=None, hosted=(), gather=True):
    T, KA = a.shape
    N = b.shape[1]
    nt = T // tt
    k0 = 0
    if a_cols is not None:
        KA, k0 = tk, a_cols

    def body(a_ref, b_ref, o_ref, acc):
        t = pl.program_id(1)

        @pl.when(t == 0)
        def _():
            acc[...] = jnp.zeros_like(acc)

        acc[...] += _tn(a_ref[...], b_ref[...])

        @pl.when(t == nt - 1)
        def _():
            o_ref[...] = (acc[...].T if transpose_out else acc[...]).astype(out_dtype)

    if transpose_out:
        out_spec, out_shape = pl.BlockSpec((N, tk), lambda i, t: (0, i)), (N, KA)
    else:
        out_spec, out_shape = pl.BlockSpec((tk, N), lambda i, t: (i, 0)), (KA, N)
    res = _hosted_call(
        body, (a, b), hosted, gather, name=name, grid=(KA // tk, nt),
        in_specs=[pl.BlockSpec((tt, tk), lambda i, t: (t, i + k0)), pl.BlockSpec((tt, N), lambda i, t: (t, 0))],
        out_specs=[out_spec], out_shape=[jax.ShapeDtypeStruct(out_shape, out_dtype)],
        scratch_shapes=[pltpu.VMEM((tk, N), F32)],
        compiler_params=_cp(("arbitrary", "arbitrary"), VMEM_BIG))
    return res if hosted else res[0]


def _ada_bwd(c_all, c_ctx, w_ada, d_all):
    n = w_ada.shape[1]

    def body(c_ref, cc_ref, w_ref, d_ref, gw_ref, t_ref):
        c = jnp.concatenate([c_ref[...], jnp.broadcast_to(cc_ref[...], (8, D))], axis=0)
        db = d_ref[...].astype(_BF)
        gw_ref[0] = _tn((c * _sigmoid(c)).astype(_BF), db)
        t_ref[...] = _nt(db[8:16], w_ref[...].astype(_BF))

    return pl.pallas_call(
        body, name="ada_bwd", in_specs=[_full((8, D)), _full((1, D)), _full((D, n)), _full((16, n))],
        out_specs=[_full((1, D, n)), _full((8, D))],
        out_shape=[jax.ShapeDtypeStruct((1, D, n), F32), jax.ShapeDtypeStruct((8, D), F32)], grid=(1,),
        compiler_params=_cp(("arbitrary",)),
    )(c_all, c_ctx, w_ada, d_all)


PART_ROWS = 56
R_ADA, R_ADA_C, R_GAIN, R_SINK, R_BG, R_GGLA, R_LOSS, R_WG = 0, 6, 12, 16, 17, 18, 19, 24


def _small_grads(s_in, s_ffn, s_mix, ada_l, ada_c, gains, dsink, s_bg, g_wg, loss):
    def body(si, sf, sm, al, ac, g, ds, sbg, gwg, loss_ref, o_ref):
        o_ref[...] = jnp.zeros_like(o_ref)
        o_ref[R_LOSS:R_LOSS + 1, 0:128] = loss_ref[0:1, :]
        sub = lax.broadcasted_iota(jnp.int32, (8, 128), 0)
        lane = lax.broadcasted_iota(jnp.int32, (8, 128), 1)
        o_ref[R_SINK:R_SINK + 1, 0:128] = jnp.sum(jnp.where(sub == lane, ds[...], 0.0), axis=0, keepdims=True)
        o_ref[R_BG:R_BG + 1, 0:512] = sbg[0:1, :]
        y = sm[1:2, 0:128] + sm[1:2, 128:256] + sm[1:2, 256:384] + sm[1:2, 384:512]
        y = y + pltpu.roll(y, 64, 1)
        o_ref[R_GGLA:R_GGLA + 1, 0:128] = jnp.where(lane[0:1] < DV, y, 0.0)
        o_ref[R_WG:R_WG + 16, 0:256] = gwg[0:16, 0:256]
        o_ref[R_WG + 16:R_WG + 32, 0:256] = gwg[16:32, 256:512]
        sdh_l, sdx_l, sdh_c, sdx_c = si[0:1], si[1:2], si[2:3], si[3:4]
        sdh2, sdx2, a2 = sf[0:1], sf[1:2], sf[2:3]
        a1 = sm[0:1]
        g1, g2, g3, g4 = g[0:1], g[1:2], g[2:3], g[3:4]
        sc1, gt1, sc2, gt2 = al[1:2], al[2:3], al[4:5], al[5:6]
        sc1c = ac[1:2]
        z = jnp.zeros((1, D), F32)
        rows = [sdh_l, sdx_l * g1, a1 * g2, sdh2, sdx2 * g3, a2 * g4,
                sdh_c, sdx_c * g1, z, z, z, z,
                sdx_l * (1.0 + sc1) + sdx_c * (1.0 + sc1c), a1 * gt1, sdx2 * (1.0 + sc2), a2 * gt2]
        for r, v in enumerate(rows):
            o_ref[r:r + 1, :] = v

    v8 = _full((8, D))
    return pl.pallas_call(
        body, name="small_grads",
        in_specs=[v8] * 6 + [_full((8, 128)), _full((8, 512)), _full((128, 512)), _full((8, 128))],
        out_specs=_full((PART_ROWS, D)), out_shape=jax.ShapeDtypeStruct((PART_ROWS, D), F32), grid=(1,),
        compiler_params=_cp(("arbitrary",)),
    )(s_in, s_ffn, s_mix, ada_l, ada_c, gains, dsink, s_bg, g_wg, loss)


def _row_tile(R):
    for cand in range(min(512, R // 2) // 16 * 16, 0, -16):
        if R % cand == 0:
            return cand
    return R


def _adamw(w, gs, m, v, name):
    _, R, C = w.shape
    tr = _row_tile(R)
    c1 = 1.0 / (1.0 - ADAM_B1 ** ADAM_STEP)
    c2 = 1.0 / (1.0 - ADAM_B2 ** ADAM_STEP)
    ng = len(gs)

    def body(w_ref, *refs):
        g_refs, (m_ref, v_ref, go_ref, d_ref, nm_ref, nv_ref) = refs[:ng], refs[ng:]
        c0 = 0
        for g_ref in g_refs:
            cols = slice(c0, c0 + g_ref.shape[2])
            c0 += g_ref.shape[2]
            gg = g_ref[0].astype(F32)
            for j in range(1, g_ref.shape[0]):
                gg = gg + g_ref[j].astype(F32)
            go_ref[0, :, cols] = gg
            nm = ADAM_B1 * m_ref[0, :, cols] + (1.0 - ADAM_B1) * gg
            nv = ADAM_B2 * v_ref[0, :, cols] + (1.0 - ADAM_B2) * (gg * gg)
            nm_ref[0, :, cols] = nm
            nv_ref[0, :, cols] = nv
            d_ref[0, :, cols] = -ADAM_LR * ((nm * c1) / (jnp.sqrt(nv * c2) + ADAM_EPS) + ADAM_WD * w_ref[0, :, cols])

    spec = pl.BlockSpec((1, tr, C), lambda i: (0, i, 0))
    sds = jax.ShapeDtypeStruct((1, R, C), F32)
    g_specs = [pl.BlockSpec((g.shape[0], tr, g.shape[2]), lambda i: (0, i, 0)) for g in gs]
    return pl.pallas_call(
        body, name=name, grid=(R // tr,), in_specs=[spec] + g_specs + [spec, spec], out_specs=[spec] * 4,
        out_shape=[sds] * 4, compiler_params=_cp(("parallel",), 48 * 1024 * 1024),
    )(w, *gs, m, v)


def _sum_slots(slots, name):
    _, R, C = slots.shape
    tr = _row_tile(R)

    def body(s_ref, o_ref):
        acc = s_ref[0].astype(F32)
        for j in range(1, N_DEV):
            acc = acc + s_ref[j].astype(F32)
        o_ref[...] = acc

    return pl.pallas_call(
        body, name=name, grid=(R // tr,), in_specs=[pl.BlockSpec((N_DEV, tr, C), lambda i: (0, i, 0))],
        out_specs=_rows(tr, C), out_shape=jax.ShapeDtypeStruct((R, C), F32), compiler_params=_cp(("parallel",)),
    )(slots)


def _ag2_start(x_ref, out_ref, send_sems, recv_sems, local_sem):
    x, y, c = lax.axis_index("x"), lax.axis_index("y"), lax.axis_index("c")
    me, sibling = (x, y, c), (x, y, 1 - c)
    chips = [(1 - x, y), (x, 1 - y), (1 - x, 1 - y)]

    def rows(px, py, pc):
        return out_ref.at[4 * px + 2 * py + pc]

    def copy(k, block, to, src=None):
        return pltpu.make_async_remote_copy(
            src_ref=rows(*block) if src is None else src, dst_ref=rows(*block),
            send_sem=send_sems.at[k], recv_sem=recv_sems.at[k], device_id=to, device_id_type=MESH)

    mine = pltpu.make_async_copy(x_ref, rows(*me), local_sem)
    mine.start()
    first = [copy(0, me, sibling, src=x_ref)]
    first += [copy(1 + j, me, (*chip, c), src=x_ref) for j, chip in enumerate(chips)]
    for cp in first:
        cp.start()
    return copy, mine, first, me, sibling, chips, c


def _ag2_finish(state):
    copy, mine, first, me, sibling, chips, c = state
    passed = [copy(4 + j, (*chip, c), sibling) for j, chip in enumerate(chips)]
    for j, chip in enumerate(chips):
        copy(1 + j, (*chip, c), me).wait_recv()
        passed[j].start()
    copy(0, sibling, me).wait_recv()
    for j, chip in enumerate(chips):
        copy(4 + j, (*chip, 1 - c), me).wait_recv()
    for cp in first + passed:
        cp.wait_send()
    mine.wait()


def _exchange(arrays, name, gather):
    na = len(arrays)

    def body(*refs):
        cps = _xchg_copies(refs[:na], refs[na:2 * na], *refs[2 * na:], gather=gather)
        _xchg_start(cps)
        _xchg_finish(cps)

    anyspec = pl.BlockSpec(memory_space=pl.ANY)
    return pl.pallas_call(
        body, name=name, out_shape=_xchg_out_shapes(arrays, gather), in_specs=[anyspec] * na,
        out_specs=[anyspec] * na, scratch_shapes=_xchg_scratch(na),
    )(*arrays)


def _entry(c, wg_sh, win_sh, c_ctx, w_ada):
    n = w_ada.shape[1]

    def body(c_ref, g_ref, w_ref, cc_ref, wa_ref, call_ref, gall_ref, wall_ref, ada_ref, part,
             s_send, s_recv, s_loc, w_send, w_recv, w_loc, a_send, a_recv, a_loc):
        big = _ag2_start(w_ref, wall_ref, w_send, w_recv, w_loc)
        small = _xchg_copies([c_ref, g_ref], [call_ref, gall_ref], s_send, s_recv, s_loc, gather=True)
        _xchg_start(small)
        _xchg_finish(small)
        cs = jnp.concatenate([call_ref[:, 0, :], jnp.broadcast_to(cc_ref[...], (8, D))], axis=0)
        part[...] = _nn((cs * _sigmoid(cs)).astype(_BF), wa_ref[...].astype(_BF))
        ada = _xchg_copies([part], [ada_ref], a_send, a_recv, a_loc, gather=True)
        _xchg_start(ada)
        _xchg_finish(ada)
        _ag2_finish(big)

    vm = pl.BlockSpec(memory_space=pltpu.VMEM)
    return pl.pallas_call(
        body, name="entry_gather",
        out_shape=[jax.ShapeDtypeStruct((N_DEV,) + c.shape, F32), jax.ShapeDtypeStruct((N_DEV,) + wg_sh.shape, F32),
                   jax.ShapeDtypeStruct((N_DEV,) + win_sh.shape, win_sh.dtype),
                   jax.ShapeDtypeStruct((N_DEV, 16, n), F32)],
        in_specs=[vm] * 5, out_specs=[vm] * 4,
        scratch_shapes=[pltpu.VMEM((16, n), F32)] + _xchg_scratch(2)
        + [pltpu.SemaphoreType.DMA((7,)), pltpu.SemaphoreType.DMA((7,)), pltpu.SemaphoreType.DMA] + _xchg_scratch(1),
        compiler_params=pltpu.CompilerParams(vmem_limit_bytes=VMEM_BIG),
    )(c, wg_sh, win_sh, c_ctx, w_ada)


def _rope_tables(S):
    t = np.arange(S)
    row = (t // GRID_W).astype(np.float32)
    colp = (t % GRID_W).astype(np.float32)
    half = HD // 2
    inv = (ROPE_BASE ** (-np.arange(0, half, 2, dtype=np.float32) / half)).astype(np.float32)
    ar = row[:, None] * inv[None, :]
    ac = colp[:, None] * inv[None, :]
    ang = np.concatenate([ar, ar, ac, ac], axis=-1).astype(np.float32)
    cos = np.cos(ang).astype(np.float32)
    sin = np.sin(ang).astype(np.float32)
    lane = np.arange(HD)
    first = (lane % 32) < 16
    sa = np.where(first[None, :], -sin, 0.0)
    sb = np.where(first[None, :], 0.0, sin)

    def ext(tab, ctx_val):
        full = np.zeros((CTX + S, 128), np.float32)
        full[:CTX, :] = ctx_val
        full[CTX:, :HD] = tab
        full[CTX:, HD:] = tab
        return jnp.asarray(full)

    return ext(cos, 1.0), ext(sa, 0.0), ext(sb, 0.0)


def _pad_rows_win(wt):
    return jnp.pad(wt, ((0, NP - IN_COLS), (0, 0)))


def _unpad_rows_win(g):
    return g[0:IN_COLS]


def _local_step(x, ctx, target, ada_l, ada_c, gains, sink, win_p, wg_bd, bg, ggla, wout_sh, wffi_sh, wffo_sh,
                after_small):
    S = x.shape[0]
    cos, sa, sb = _rope_tables(S)
    g1, g2, g3, g4 = (gains[i:i + 1] for i in range(4))
    sh1, sc1, gt1, sh2, sc2, gt2 = (ada_l[i:i + 1] for i in range(6))
    sh1c, sc1c = ada_c[0:1], ada_c[1:2]
    gml, gmc, gm2 = g1 * (1.0 + sc1), g1 * (1.0 + sc1c), g3 * (1.0 + sc2)
    mavg = jnp.asarray(np.kron(np.eye(N_GLA, dtype=np.float32), np.full((DV, DV), 1.0 / DV, np.float32))).astype(_BF)

    n_ffi, r_ffo, r_out = wffi_sh.shape[0], wffo_sh.shape[0], wout_sh.shape[0]
    tt_e = 1408 if (S + CTX) % 1408 == 0 else 256
    tt_s = 512 if S % 512 == 0 else 256
    h, q, k, v, gq, gk, gv, gg, z, la, wout_g = _inproj_fwd(x, ctx, gml, sh1, gmc, sh1c, win_p, wg_bd, bg,
                                                            cos, sa, sb, [wout_sh])
    attn, lse, probs, wffi_g = _attn_fwd(q, k, v, sink, [wffi_sh])
    o_f, st_f, o_b, st_b, wffo_g = _gla_fwd(gq, gk, gv, la, [wffo_sh])
    wout = wout_g.reshape(N_DEV * r_out, D)
    wffi = wffi_g.reshape(N_DEV * n_ffi, D)
    wffo = wffo_g.reshape(N_DEV * r_ffo, D)
    x1, mix = _mix_fwd(x, attn, o_f, o_b, gg, ggla, mavg, wout, gt1, g2)
    dx1, h2, du, act, df, s_ffn, loss = _ffn(x1, target, gm2, sh2, gt2, g4, wffi, wffo)
    slab_ffi = _matmul_tn(h2, du, 512, tt_s, "grad_w_ffn_in", _BF, True).reshape(N_DEV, n_ffi, D)
    tt_l = 1024 if S % 1024 == 0 else tt_s
    slab_ffo = _matmul_tn(act, df, FFN, tt_l, "grad_w_ffn_out", _BF).reshape(N_DEV, r_ffo, D)
    d_attn, do_gla, dgg, dy, s_mix, got_ffo = _mix_bwd(dx1, mix, o_f, o_b, gg, ggla, mavg, wout, gt1, g2, [slab_ffo])
    slab_out = _matmul_tn(mix, dy, D, tt_l, "grad_w_out", _BF).reshape(N_DEV, r_out, D)
    dq, dk, dv, dsink, got_ffi = _attn_bwd(q, k, v, sink, probs, lse, d_attn, [slab_ffi])
    (dgq_f, dgk_f, dgv_f, dlg_f, dgq_b, dgk_b, dgv_b, dlg_b, s_bg,
     got_out) = _gla_bwd(gq, gk, gv, la, st_f, st_b, do_gla, [slab_out])
    dp, grad_x, s_in, g_wg = _inproj_bwd(x, ctx, gml, gmc, win_p, wg_bd, cos, sa, sb, dq, dk, dv,
                                         (dgq_f, dgq_b), (dgk_f, dgk_b), (dgv_f, dgv_b), dgg, dlg_f, dlg_b, dx1, z)
    small = _small_grads(s_in, s_ffn, s_mix, ada_l, ada_c, gains, dsink, s_bg, g_wg, loss)
    n_in, n_grp = IN_COLS // N_DEV, 2
    got_in, slab, extra, got_extra = [], None, [], []
    for j in range(n_grp):
        res = _matmul_tn(h, dp, D // n_grp, tt_e, "grad_w_in_%d" % j, _BF, True, a_cols=j,
                         hosted=[small] if j == 0 else [slab] + extra, gather=(j == 0))
        if j == 0:
            extra = after_small(res[1])
        else:
            got_in.append(res[1])
            got_extra, extra = got_extra + list(res[2:]), []
        slab = _unpad_rows_win(res[0]).reshape(N_DEV, n_in, D // n_grp)
    got_in.append(_exchange([slab], "scatter_grads", False)[0])
    return dict(grad_x=grad_x, got_in=got_in, got_out=got_out, got_ffi=got_ffi, got_ffo=got_ffo, got_extra=got_extra)


SMALL_NAMES = ["c_ctx", "b_ada", "g_pre_mix", "g_post_mix", "g_pre_ffn", "g_post_ffn", "attn_sink",
               "b_gate_fwd", "b_gate_bwd", "g_gla_norm", "w_gate_fwd", "w_gate_bwd"]


def _small_update(tot, t_tot, wg_g, w, m, v):
    c1 = 1.0 / (1.0 - ADAM_B1 ** ADAM_STEP)
    c2 = 1.0 / (1.0 - ADAM_B2 ** ADAM_STEP)
    n = len(SMALL_NAMES)

    def body(tot_ref, t_ref, wg_ref, *refs):
        w_r, m_r, v_r = refs[0:n], refs[n:2 * n], refs[2 * n:3 * n]
        g_o, d_o, nm_o, nv_o = refs[3 * n:4 * n], refs[4 * n:5 * n], refs[5 * n:6 * n], refs[6 * n:7 * n]

        def upd(i, idx, g):
            nm = ADAM_B1 * m_r[i][idx] + (1.0 - ADAM_B1) * g
            nv = ADAM_B2 * v_r[i][idx] + (1.0 - ADAM_B2) * (g * g)
            g_o[i][idx] = g
            nm_o[i][idx] = nm
            nv_o[i][idx] = nv
            d_o[i][idx] = -ADAM_LR * ((nm * c1) / (jnp.sqrt(nv * c2) + ADAM_EPS) + ADAM_WD * w_r[i][idx])

        everything = (slice(None), slice(None))
        cc = w_r[0][...]
        sc = _sigmoid(cc)
        upd(0, everything, t_ref[0:1, :] * (sc * (1.0 + cc * (1.0 - sc))))
        for j in range(6):
            upd(1, (slice(None), slice(D * j, D * j + D)),
                tot_ref[R_ADA + j:R_ADA + j + 1, :] + tot_ref[R_ADA_C + j:R_ADA_C + j + 1, :])
        for j in range(4):
            upd(2 + j, everything, tot_ref[R_GAIN + j:R_GAIN + j + 1, :])
        upd(6, everything, tot_ref[R_SINK:R_SINK + 1, 0:N_ATT])
        upd(7, everything, tot_ref[R_BG:R_BG + 1, 0:256])
        upd(8, everything, tot_ref[R_BG:R_BG + 1, 256:512])
        upd(9, everything, tot_ref[R_GGLA:R_GGLA + 1, 0:DV])
        upd(10, (0,), wg_ref[0:GATE_RANK, :])
        upd(11, (0,), wg_ref[GATE_RANK:2 * GATE_RANK, :])

    params = [w[k] for k in SMALL_NAMES] + [m[k] for k in SMALL_NAMES] + [v[k] for k in SMALL_NAMES]
    outs = pl.pallas_call(
        body, name="small_update", grid=(1,),
        in_specs=[_full(tot.shape), _full(t_tot.shape), _full(wg_g.shape)] + [_full(p.shape) for p in params],
        out_specs=[_full(w[k].shape) for k in SMALL_NAMES] * 4,
        out_shape=[jax.ShapeDtypeStruct(w[k].shape, F32) for k in SMALL_NAMES] * 4,
        compiler_params=_cp(("arbitrary",)),
    )(tot, t_tot, wg_g, *params)
    return tuple(dict(zip(SMALL_NAMES, outs[i * n:(i + 1) * n])) for i in range(4))


def kernel(x, c, ctx, c_ctx, w_ada, b_ada, g_pre_mix, g_post_mix, g_pre_ffn, g_post_ffn, w_in, attn_sink, w_gate_fwd, b_gate_fwd, w_gate_bwd, b_gate_bwd, g_gla_norm, w_out, w_ffn_in, w_ffn_out, loss_target, m_c_ctx, m_w_ada, m_b_ada, m_g_pre_mix, m_g_post_mix, m_g_pre_ffn, m_g_post_ffn, m_w_in, m_attn_sink, m_w_gate_fwd, m_b_gate_fwd, m_w_gate_bwd, m_b_gate_bwd, m_g_gla_norm, m_w_out, m_w_ffn_in, m_w_ffn_out, v_c_ctx, v_w_ada, v_b_ada, v_g_pre_mix, v_g_post_mix, v_g_pre_ffn, v_g_post_ffn, v_w_in, v_attn_sink, v_w_gate_fwd, v_b_gate_fwd, v_w_gate_bwd, v_b_gate_bwd, v_g_gla_norm, v_w_out, v_w_ffn_in, v_w_ffn_out):
    me = 4 * lax.axis_index("x") + 2 * lax.axis_index("y") + lax.axis_index("c")
    S = x.shape[1]
    n_in = w_in.shape[2]
    n_ffi = w_ffn_in.shape[2]
    r_out = w_out.shape[1]
    r_ffo = w_ffn_out.shape[1]
    n_ada = w_ada.shape[2]

    wg_sh = jnp.concatenate([w_gate_fwd.reshape(4, 128), w_gate_bwd.reshape(4, 128)], axis=0)
    c_all3, g_all, w_all, ada_all = _entry(c, wg_sh, w_in[0].T.astype(_BF), c_ctx.reshape(1, D), w_ada[0])
    c_all = c_all3.reshape(N_DEV, D)
    wgf = g_all[:, 0:4].reshape(N_DEV, GATE_RANK, 32).transpose(1, 0, 2).reshape(GATE_RANK, 256)
    wgb = g_all[:, 4:8].reshape(N_DEV, GATE_RANK, 32).transpose(1, 0, 2).reshape(GATE_RANK, 256)
    win_p = _pad_rows_win(w_all.reshape(N_DEV * n_in, D))
    wg_bd = jnp.zeros((128, 512), F32).at[0:16, 0:256].set(wgf).at[16:32, 256:512].set(wgb).astype(_BF)
    ada_full = ada_all.transpose(1, 0, 2).reshape(16, N_DEV * n_ada) + b_ada
    ada_l = jnp.pad(lax.dynamic_slice_in_dim(ada_full, me, 1, 0).reshape(6, D), ((0, 2), (0, 0)))
    ada_c = jnp.pad(ada_full[8].reshape(6, D), ((0, 2), (0, 0)))
    gains = jnp.pad(jnp.concatenate([g_pre_mix, g_post_mix, g_pre_ffn, g_post_ffn], axis=0), ((0, 4), (0, 0)))
    sink = jnp.broadcast_to(attn_sink.reshape(8, 1), (8, 128))
    bg = jnp.concatenate([b_gate_fwd, b_gate_bwd], axis=1)
    ggla = jnp.tile(g_gla_norm, (1, N_GLA))

    tail = {}

    def after_small(parts):
        tot = _sum_slots(parts, "sum_small_grads")
        d_ada_rows = parts[:, R_ADA:R_ADA + 6].reshape(N_DEV, 6 * D)
        d_ada_c = tot[R_ADA_C:R_ADA_C + 6].reshape(1, 6 * D)
        my_cols = lax.dynamic_slice_in_dim(jnp.concatenate([d_ada_rows, jnp.broadcast_to(d_ada_c, (1, 6 * D)),
                                                            jnp.zeros((7, 6 * D), F32)], axis=0), me * n_ada, n_ada, 1)
        grad_w_ada, t_part = _ada_bwd(c_all, c_ctx.reshape(1, D), w_ada[0], my_cols)
        tail.update(tot=tot, grad_w_ada=grad_w_ada)
        return [jnp.broadcast_to(t_part[None], (N_DEV,) + t_part.shape)]

    r = _local_step(x[0], ctx[0], loss_target[0], ada_l, ada_c, gains, sink, win_p, wg_bd, bg, ggla,
                    w_out[0].astype(_BF), w_ffn_in[0].T.astype(_BF), w_ffn_out[0].astype(_BF), after_small)
    tot, grad_w_ada = tail["tot"], tail["grad_w_ada"]
    loss = tot[R_LOSS, 0]
    wg_g = lax.dynamic_slice(tot, (R_WG, me * 32), (2 * GATE_RANK, 32))

    tr = lambda a: jnp.transpose(a, (0, 2, 1))
    big = {}
    t_tot = _sum_slots(r["got_extra"][0], "sum_c_ctx")
    for nm, w, g, m, v in [("w_ada", w_ada, grad_w_ada, m_w_ada, v_w_ada),
                           ("w_out", w_out, r["got_out"], m_w_out, v_w_out),
                           ("w_ffn_out", w_ffn_out, r["got_ffo"], m_w_ffn_out, v_w_ffn_out)]:
        big[nm] = _adamw(w, [g], m, v, "adamw_" + nm)
    big["w_ffn_in"] = tuple(tr(o) for o in _adamw(tr(w_ffn_in), [r["got_ffi"]], tr(m_w_ffn_in), tr(v_w_ffn_in),
                                                  "adamw_w_ffn_in"))
    big["w_in"] = tuple(tr(o) for o in _adamw(tr(w_in), r["got_in"], tr(m_w_in), tr(v_w_in), "adamw_w_in"))

    w_small = dict(c_ctx=c_ctx.reshape(1, D), b_ada=b_ada, g_pre_mix=g_pre_mix, g_post_mix=g_post_mix, g_pre_ffn=g_pre_ffn,
                   g_post_ffn=g_post_ffn, attn_sink=attn_sink, b_gate_fwd=b_gate_fwd, b_gate_bwd=b_gate_bwd,
                   g_gla_norm=g_gla_norm, w_gate_fwd=w_gate_fwd, w_gate_bwd=w_gate_bwd)
    m_small = dict(c_ctx=m_c_ctx.reshape(1, D), b_ada=m_b_ada, g_pre_mix=m_g_pre_mix, g_post_mix=m_g_post_mix,
                   g_pre_ffn=m_g_pre_ffn, g_post_ffn=m_g_post_ffn, attn_sink=m_attn_sink, b_gate_fwd=m_b_gate_fwd,
                   b_gate_bwd=m_b_gate_bwd, g_gla_norm=m_g_gla_norm, w_gate_fwd=m_w_gate_fwd, w_gate_bwd=m_w_gate_bwd)
    v_small = dict(c_ctx=v_c_ctx.reshape(1, D), b_ada=v_b_ada, g_pre_mix=v_g_pre_mix, g_post_mix=v_g_post_mix,
                   g_pre_ffn=v_g_pre_ffn, g_post_ffn=v_g_post_ffn, attn_sink=v_attn_sink, b_gate_fwd=v_b_gate_fwd,
                   b_gate_bwd=v_b_gate_bwd, g_gla_norm=v_g_gla_norm, w_gate_fwd=v_w_gate_fwd, w_gate_bwd=v_w_gate_bwd)
    grads_small, d_s, nm_s, nv_s = _small_update(tot, t_tot, wg_g, w_small, m_small, v_small)
    for dd in (grads_small, d_s, nm_s, nv_s):
        dd["c_ctx"] = dd["c_ctx"].reshape(D)

    order = ["c_ctx", "w_ada", "b_ada", "g_pre_mix", "g_post_mix", "g_pre_ffn", "g_post_ffn", "w_in", "attn_sink",
             "w_gate_fwd", "b_gate_fwd", "w_gate_bwd", "b_gate_bwd", "g_gla_norm", "w_out", "w_ffn_in", "w_ffn_out"]
    grads, deltas, new_m, new_v = [], [], [], []
    for nm in order:
        if nm in big:
            g_, d_, m_, v_ = big[nm]
        else:
            g_, d_, m_, v_ = grads_small[nm], d_s[nm], nm_s[nm], nv_s[nm]
        grads.append(g_)
        deltas.append(d_)
        new_m.append(m_)
        new_v.append(v_)
    return (loss, r["grad_x"][None], *grads, *deltas, *new_m, *new_v)
```

```python
import functools

import numpy as np
import jax
import jax.numpy as jnp
from jax import lax
from jax.experimental import pallas as pl
from jax.experimental.pallas import tpu as pltpu

F32 = jnp.float32
_BF = jnp.bfloat16

N_DEV = 8
D = 1024
CTX = 256
HD = 64
N_ATT = 8
N_KV = 2
GRP = N_ATT // N_KV
WIN = 128
GRID_W = 64
ROPE_BASE = 10000.0
N_GLA = 8
DK = 32
DV = 64
GATE_RANK = 16
GATE_TAU = 16.0
FFN = 2816
EPS = 1e-6
NEG = -1e30
GLA_T = 128

QP = N_ATT * HD
KP = N_KV * HD
K2 = N_KV * 128
O_Q, O_K, O_V = 0, QP, QP + KP
O_GQ = O_V + KP
O_GK = O_GQ + N_GLA * DK
O_GV = O_GK + N_GLA * DK
O_GG = O_GV + N_GLA * DV
O_Z = O_GG + N_GLA * DV
NP = O_Z + 128
IN_COLS = 2336

ADAM_LR, ADAM_B1, ADAM_B2, ADAM_EPS, ADAM_WD, ADAM_STEP = 0.001, 0.9, 0.999, 1e-08, 0.01, 10

VMEM_BIG = 56 * 1024 * 1024
MESH = pl.DeviceIdType.MESH


def _cp(sem, vmem=None):
    return pltpu.CompilerParams(dimension_semantics=sem, vmem_limit_bytes=vmem)


def _full(shape):
    nd = len(shape)
    return pl.BlockSpec(shape, lambda *a: (0,) * nd)


def _rows(tile, width, off=0):
    return pl.BlockSpec((tile, width), lambda i: (i + off, 0))


def _rows_lat(tile, width):
    return pl.BlockSpec((tile, width), lambda i: (jnp.maximum(i - 1, 0), 0))


def _nt(a, b):
    return lax.dot_general(a, b, (((1,), (1,)), ((), ())), preferred_element_type=F32)


def _tn(a, b):
    return lax.dot_general(a, b, (((0,), (0,)), ((), ())), preferred_element_type=F32)


def _nn(a, b):
    return jnp.dot(a, b, preferred_element_type=F32)


def _head_mean(x, mavg):
    n = x.shape[0]
    hi = x.astype(_BF)
    lo = (x - hi.astype(F32)).astype(_BF)
    y = _nn(jnp.concatenate([hi, lo], axis=0), mavg)
    return y[0:n] + y[n:2 * n]


def _rope(t, cos, sa, sb):
    n = t.shape[1]
    reps = n // 128
    c = jnp.tile(cos, (1, reps))
    a = jnp.tile(sa, (1, reps))
    b = jnp.tile(sb, (1, reps))
    return t * c + pltpu.roll(t, n - 16, 1) * a + pltpu.roll(t, 16, 1) * b


def _unrope(t, cos, sa, sb):
    n = t.shape[1]
    reps = n // 128
    c = jnp.tile(cos, (1, reps))
    a = jnp.tile(sa, (1, reps))
    b = jnp.tile(sb, (1, reps))
    return t * c + pltpu.roll(t * a, 16, 1) + pltpu.roll(t * b, n - 16, 1)


def _sigmoid(x):
    return 1.0 / (1.0 + jnp.exp(-x))


def _inproj_fwd(x, ctx, gml, shl, gmc, shc, win, wg, bg, cos, sa, sb, shards):
    E = x.shape[0] + CTX
    TE = CTX

    def body(x_ref, c_ref, gml_ref, shl_ref, gmc_ref, shc_ref, w_ref, wg_ref, bg_ref, cos_ref, sa_ref, sb_ref,
             h_ref, q_ref, k_ref, v_ref, gq_ref, gk_ref, gv_ref, gg_ref, z_ref, la_ref):
        is_ctx = pl.program_id(0) == 0
        gm = jnp.where(is_ctx, gmc_ref[...], gml_ref[...])
        sh = jnp.where(is_ctx, shc_ref[...], shl_ref[...])
        x = jnp.where(is_ctx, c_ref[...], x_ref[...])
        r = lax.rsqrt(jnp.mean(x * x, axis=-1, keepdims=True) + EPS)
        hb = ((x * r) * gm + sh).astype(_BF)
        h_ref[...] = hb
        p = _nt(hb, w_ref[...])
        cos_t, sa_t, sb_t = cos_ref[...], sa_ref[...], sb_ref[...]
        q_ref[...] = (_rope(p[:, O_Q:O_K], cos_t, sa_t, sb_t) * (HD ** -0.5)).astype(_BF)
        kr = _rope(p[:, O_K:O_V], cos_t, sa_t, sb_t)
        vv = p[:, O_V:O_GQ]
        k_ref[...] = jnp.concatenate([_both_halves(kr, h) for h in range(N_KV)], axis=1).astype(_BF)
        v_ref[...] = jnp.concatenate([_both_halves(vv, h) for h in range(N_KV)], axis=1).astype(_BF)
        gq_ref[...] = p[:, O_GQ:O_GK] * (DK ** -0.5)
        gk_ref[...] = p[:, O_GK:O_GV]
        gv_ref[...] = p[:, O_GV:O_GG]
        gg_ref[...] = p[:, O_GG:O_Z]
        zb = p[:, O_Z:NP].astype(_BF)
        z_ref[...] = zb
        lg = _nn(zb, wg_ref[...]) + bg_ref[...]
        la_ref[...] = (jnp.minimum(lg, 0.0) - jnp.log(1.0 + jnp.exp(-jnp.abs(lg)))) * (1.0 / GATE_TAU)

    vec = _full((1, D))
    tab = _rows(TE, 128)
    outs = [(D, _BF), (QP, _BF), (K2, _BF), (K2, _BF), (256, F32), (256, F32), (512, F32), (512, F32),
            (128, _BF), (512, F32)]
    return _hosted_call(
        body, (x, ctx, gml, shl, gmc, shc, win, wg, bg, cos, sa, sb), shards, True,
        name="inproj_fwd", grid=(E // TE,),
        in_specs=[_rows_lat(TE, D), _full((CTX, D)), vec, vec, vec, vec, _full((NP, D)), _full((128, 512)),
                  _full((1, 512)), tab, tab, tab],
        out_specs=[_rows_lat(TE, w) if i == 7 else _rows(TE, w) for i, (w, _) in enumerate(outs)],
        out_shape=[jax.ShapeDtypeStruct((E - CTX if i == 7 else E, w), dt) for i, (w, dt) in enumerate(outs)],
        compiler_params=_cp(("arbitrary",), 40 * 1024 * 1024))


def _xchg_scratch(na):
    return [pltpu.SemaphoreType.DMA((na, N_DEV - 1)), pltpu.SemaphoreType.DMA((na, N_DEV - 1)),
            pltpu.SemaphoreType.DMA((na,))]


def _xchg_copies(ins, outs, send_sems, recv_sems, local_sems, gather):
    x, y, c = lax.axis_index("x"), lax.axis_index("y"), lax.axis_index("c")
    me = 4 * x + 2 * y + c
    local, sends, recvs = [], [], []
    for a in range(len(ins)):
        local.append(pltpu.make_async_copy(ins[a] if gather else ins[a].at[me], outs[a].at[me], local_sems.at[a]))
    for k in range(1, N_DEV):
        px, py, pc = x ^ (k >> 2), y ^ ((k >> 1) & 1), c ^ (k & 1)
        peer = 4 * px + 2 * py + pc
        for a in range(len(ins)):
            sems = dict(send_sem=send_sems.at[a, k - 1], recv_sem=recv_sems.at[a, k - 1], device_id_type=MESH)
            sends.append(pltpu.make_async_remote_copy(
                src_ref=ins[a] if gather else ins[a].at[peer], dst_ref=outs[a].at[me], device_id=(px, py, pc), **sems))
            recvs.append(pltpu.make_async_remote_copy(
                src_ref=ins[a] if gather else ins[a].at[me], dst_ref=outs[a].at[peer], device_id=(x, y, c), **sems))
    return local, sends, recvs


def _xchg_start(cps):
    local, sends, _ = cps
    for cp in local + sends:
        cp.start()


def _xchg_finish(cps):
    local, sends, recvs = cps
    for cp in recvs:
        cp.wait_recv()
    for cp in sends:
        cp.wait_send()
    for cp in local:
        cp.wait()


def _xchg_out_shapes(ins, gather):
    return [jax.ShapeDtypeStruct(((N_DEV,) + s.shape) if gather else s.shape, s.dtype) for s in ins]


def _hosted_call(body, args, hosted, gather, *, grid, in_specs, out_specs, out_shape, scratch_shapes=(), **kw):
    na = len(hosted)
    if na == 0:
        return pl.pallas_call(body, grid=grid, in_specs=in_specs, out_specs=out_specs, out_shape=out_shape,
                              scratch_shapes=list(scratch_shapes), **kw)(*args)
    n_in, n_out, n_scr = len(in_specs), len(out_specs), len(scratch_shapes)

    def wrapped(*refs):
        ins, h_in = refs[:n_in], refs[n_in:n_in + na]
        outs, h_out = refs[n_in + na:n_in + na + n_out], refs[n_in + na + n_out:n_in + 2 * na + n_out]
        scr = refs[n_in + 2 * na + n_out:]
        cps = _xchg_copies(h_in, h_out, *scr[n_scr:], gather=gather)
        pids = [pl.program_id(a) for a in range(len(grid))]
        first = functools.reduce(jnp.logical_and, [p == 0 for p in pids])
        last = functools.reduce(jnp.logical_and, [p == g - 1 for p, g in zip(pids, grid)])

        @pl.when(first)
        def _():
            _xchg_start(cps)

        body(*ins, *outs, *scr[:n_scr])

        @pl.when(last)
        def _():
            _xchg_finish(cps)

    anyspec = pl.BlockSpec(memory_space=pl.ANY)
    return pl.pallas_call(
        wrapped, grid=grid, in_specs=list(in_specs) + [anyspec] * na, out_specs=list(out_specs) + [anyspec] * na,
        out_shape=list(out_shape) + _xchg_out_shapes(hosted, gather),
        scratch_shapes=list(scratch_shapes) + _xchg_scratch(na), **kw)(*args, *hosted)


ATT_BLOCKS = 2
PROB_ROWS = N_KV * GRP * WIN


def _attn_specs(E):
    nb = (E - CTX) // WIN
    last = E // WIN - 1
    kc = pl.BlockSpec((CTX, K2), lambda m: (0, 0))
    ks = [pl.BlockSpec((WIN, K2), functools.partial(lambda m, j: (jnp.minimum(ATT_BLOCKS * m + j, last), 0), j=j))
          for j in range(1, ATT_BLOCKS + 3)]
    return nb, [kc] + ks


def _attn_bias(nb):
    rows = np.arange(GRP * WIN)[:, None] % WIN
    cols = np.arange(CTX + 3 * WIN)[None, :]
    j = cols - CTX
    band = np.abs(j - WIN - rows) <= WIN
    out = []
    for first, last in ((True, False), (False, False), (False, True)):
        ok = (cols < CTX) | (band & ((j >= WIN) | (not first)) & ((j < 2 * WIN) | (not last)))
        out.append(np.where(ok, 0.0, NEG).astype(np.float32))
    bias = jnp.asarray(np.stack(out))
    steps = nb // ATT_BLOCKS
    shape = (1, GRP * WIN, CTX + 3 * WIN)
    specs = [pl.BlockSpec(shape, lambda m: (jnp.where(m == 0, 0, 1), 0, 0))]
    specs += [pl.BlockSpec(shape, lambda m: (1, 0, 0))] * (ATT_BLOCKS - 2)
    specs += [pl.BlockSpec(shape, lambda m: (jnp.where(m == steps - 1, 2, 1), 0, 0))]
    return bias, specs


def _both_halves(t, h):
    tf = t.astype(F32)
    r = pltpu.roll(tf, HD, 1)
    lo = lax.broadcasted_iota(jnp.int32, tf.shape, 1) < HD
    return (jnp.where(lo, tf, r) if h == 0 else jnp.where(lo, r, tf)).astype(t.dtype)


def _stack_heads(ref, h):
    lo = lax.broadcasted_iota(jnp.int32, (WIN, 128), 1) < HD
    parts = []
    for g in range(GRP):
        j = GRP * h + g
        t = ref[:, 128 * (j // 2):128 * (j // 2) + 128].astype(F32)
        parts.append(jnp.where(lo if j % 2 == 0 else jnp.logical_not(lo), t, 0.0))
    return jnp.concatenate(parts, axis=0)


def _unstack_pair(o, pp):
    lo = lax.broadcasted_iota(jnp.int32, (WIN, 128), 1) < HD
    return jnp.where(lo, o[WIN * 2 * pp:WIN * 2 * pp + WIN], o[WIN * (2 * pp + 1):WIN * (2 * pp + 1) + WIN])


def _attn_fwd(q, k, v, sink, shards):
    E = q.shape[0]
    S = E - CTX
    nb, kspecs = _attn_specs(E)
    na = len(shards)

    nk = ATT_BLOCKS + 3

    def one_block(q_ref, kw, vw, sink_ref, bias_ref, o_ref, lse_ref, p_ref):
        lane = lax.broadcasted_iota(jnp.int32, (WIN, 128), 1)
        lse_t = jnp.zeros((WIN, 128), F32)
        kall = jnp.concatenate([r[...] for r in kw], axis=0)
        vall = jnp.concatenate([r[...] for r in vw], axis=0)
        K = [kall[:, 128 * h:128 * h + 128] for h in range(N_KV)]
        Q = [_stack_heads(q_ref, h).astype(_BF) for h in range(N_KV)]
        sk = [jnp.concatenate([jnp.broadcast_to(sink_ref[GRP * h + g:GRP * h + g + 1, 0:1], (WIN, 1))
                               for g in range(GRP)], axis=0) for h in range(N_KV)]
        s = [_nt(Q[h], K[h]) + bias_ref[0] for h in range(N_KV)]
        m = [jnp.maximum(jnp.max(s[h], axis=1, keepdims=True), sk[h]) for h in range(N_KV)]
        e = [jnp.exp(s[h] - m[h]) for h in range(N_KV)]
        den = [jnp.sum(e[h], axis=1, keepdims=True) + jnp.exp(sk[h] - m[h]) for h in range(N_KV)]
        V = [vall[:, 128 * h:128 * h + 128] for h in range(N_KV)]
        pb = [(e[h] * (1.0 / den[h])).astype(_BF) for h in range(N_KV)]
        for h in range(N_KV):
            p_ref[GRP * WIN * h:GRP * WIN * (h + 1), :] = pb[h]
        o = [_nn(pb[h], V[h]) for h in range(N_KV)]
        for h in range(N_KV):
            lse = m[h] + jnp.log(den[h])
            for g in range(GRP):
                lse_t = jnp.where(lane == GRP * h + g, lse[WIN * g:WIN * g + WIN], lse_t)
            for pp in range(GRP // 2):
                t = 2 * h + pp
                o_ref[:, 128 * t:128 * t + 128] = _unstack_pair(o[h], pp).astype(_BF)
        lse_ref[...] = lse_t

    def body(q_ref, *rest):
        kr, vr, sink_ref = rest[0:nk], rest[nk:2 * nk], rest[2 * nk]
        bias_refs = rest[2 * nk + 1:2 * nk + 1 + ATT_BLOCKS]
        rest = rest[2 * nk + 1 + ATT_BLOCKS:]
        shard_refs, (o_ref, lse_ref, p_ref), got_refs = rest[:na], rest[na:na + 3], rest[na + 3:2 * na + 3]
        cps = _xchg_copies(shard_refs, got_refs, *rest[2 * na + 3:], gather=True)

        @pl.when(pl.program_id(0) == 0)
        def _():
            _xchg_start(cps)

        for j in range(ATT_BLOCKS):
            rows = pl.ds(WIN * j, WIN)
            one_block(q_ref.at[rows], [kr[0]] + list(kr[1 + j:4 + j]), [vr[0]] + list(vr[1 + j:4 + j]), sink_ref,
                      bias_refs[j], o_ref.at[rows], lse_ref.at[rows], p_ref.at[pl.ds(PROB_ROWS * j, PROB_ROWS)])

        @pl.when(pl.program_id(0) == nb // ATT_BLOCKS - 1)
        def _():
            _xchg_finish(cps)

    tq = ATT_BLOCKS * WIN
    qs = pl.BlockSpec((tq, QP), lambda m: (m + CTX // tq, 0))
    anyspec = pl.BlockSpec(memory_space=pl.ANY)
    bias, bias_specs = _attn_bias(nb)
    return pl.pallas_call(
        body, name="attn_fwd", grid=(nb // ATT_BLOCKS,),
        in_specs=[qs] + kspecs + kspecs + [_full((8, 128))] + bias_specs + [anyspec] * na,
        out_specs=[_rows(tq, 512), _rows(tq, 128), _rows(ATT_BLOCKS * PROB_ROWS, CTX + 3 * WIN)] + [anyspec] * na,
        out_shape=[jax.ShapeDtypeStruct((S, 512), _BF), jax.ShapeDtypeStruct((S, 128), F32),
                   jax.ShapeDtypeStruct((nb * PROB_ROWS, CTX + 3 * WIN), _BF)]
        + _xchg_out_shapes(shards, True),
        scratch_shapes=_xchg_scratch(na),
        compiler_params=_cp(("arbitrary",), 48 * 1024 * 1024),
    )(q, *([k] * nk), *([v] * nk), sink, *([bias] * ATT_BLOCKS), *shards)


def _attn_bwd(q, k, v, sink, probs, lse, d_attn, slabs):
    E = q.shape[0]
    S = E - CTX
    nb, kspecs = _attn_specs(E)
    last = E // WIN - 1
    na = len(slabs)

    nk = ATT_BLOCKS + 3

    def one_block(n, q_ref, kw, vw, sink_ref, p_ref, lse_ref, do_ref, dq_ref, dk_ref, dv_ref, ds_ref):
        lane = lax.broadcasted_iota(jnp.int32, (WIN, 128), 1)
        lse_t = lse_ref[...]
        starts = [None, pl.multiple_of((n + 1) * WIN, WIN), pl.multiple_of((n + 2) * WIN, WIN),
                  pl.multiple_of(jnp.minimum(n + 3, last) * WIN, WIN)]
        kall = jnp.concatenate([r[...] for r in kw], axis=0)
        vall = jnp.concatenate([r[...] for r in vw], axis=0)
        for h in range(N_KV):
            hs = slice(HD * h, HD * h + HD)
            K = kall[:, 128 * h:128 * h + 128]
            V = vall[:, 128 * h:128 * h + 128]
            Q = _stack_heads(q_ref, h).astype(_BF)
            sk = jnp.concatenate([jnp.broadcast_to(sink_ref[GRP * h + g:GRP * h + g + 1, 0:1], (WIN, 1))
                                  for g in range(GRP)], axis=0)
            ls = jnp.concatenate([jnp.sum(jnp.where(lane == GRP * h + g, lse_t, 0.0), axis=1, keepdims=True)
                                  for g in range(GRP)], axis=0)
            do = _stack_heads(do_ref, h).astype(_BF)
            pb = p_ref[GRP * WIN * h:GRP * WIN * (h + 1), :]
            p = pb.astype(F32)
            dp = _nt(do, V)
            delta = jnp.sum(p * dp, axis=1, keepdims=True)
            dsc = (p * (dp - delta)).astype(_BF)
            dq = _nn(dsc, K) * (HD ** -0.5)
            for pp in range(GRP // 2):
                t = 2 * h + pp
                dq_ref[:, 128 * t:128 * t + 128] = _unstack_pair(dq, pp).astype(_BF)
            dK2 = _tn(Q, dsc)
            dV2 = _tn(do, pb)
            dK = dK2[0:HD] + dK2[HD:2 * HD]
            dV = dV2[0:HD] + dV2[HD:2 * HD]
            dk_ref[hs, 0:CTX] += dK[:, 0:CTX]
            dv_ref[hs, 0:CTX] += dV[:, 0:CTX]
            for w in range(1, 4):
                lo = CTX + WIN * (w - 1)
                dk_ref[hs, pl.ds(starts[w], WIN)] += dK[:, lo:lo + WIN]
                dv_ref[hs, pl.ds(starts[w], WIN)] += dV[:, lo:lo + WIN]
            psk = -jnp.exp(sk - ls) * delta
            for g in range(GRP):
                j = GRP * h + g
                tot = jnp.sum(psk[WIN * g:WIN * g + WIN], axis=0, keepdims=True)
                ds_ref[j:j + 1, :] += jnp.broadcast_to(tot, (1, 128))

    def body(q_ref, *rest):
        kr, vr = rest[0:nk], rest[nk:2 * nk]
        sink_ref, p_ref, lse_ref, do_ref = rest[2 * nk:2 * nk + 4]
        rest = rest[2 * nk + 4:]
        slab_refs, (dq_ref, dk_ref, dv_ref, ds_ref), got_refs = rest[:na], rest[na:na + 4], rest[na + 4:2 * na + 4]
        m = pl.program_id(0)
        cps = _xchg_copies(slab_refs, got_refs, *rest[2 * na + 4:], gather=False)

        @pl.when(m == 0)
        def _():
            _xchg_start(cps)
            dk_ref[...] = jnp.zeros_like(dk_ref)
            dv_ref[...] = jnp.zeros_like(dv_ref)
            ds_ref[...] = jnp.zeros_like(ds_ref)

        for j in range(ATT_BLOCKS):
            rows = pl.ds(WIN * j, WIN)
            one_block(ATT_BLOCKS * m + j, q_ref.at[rows], [kr[0]] + list(kr[1 + j:4 + j]),
                      [vr[0]] + list(vr[1 + j:4 + j]), sink_ref, p_ref.at[pl.ds(PROB_ROWS * j, PROB_ROWS)],
                      lse_ref.at[rows], do_ref.at[rows], dq_ref.at[rows], dk_ref, dv_ref, ds_ref)

        @pl.when(m == nb // ATT_BLOCKS - 1)
        def _():
            _xchg_finish(cps)

    tq = ATT_BLOCKS * WIN
    qs = pl.BlockSpec((tq, QP), lambda m: (m + CTX // tq, 0))
    anyspec = pl.BlockSpec(memory_space=pl.ANY)
    return pl.pallas_call(
        body, name="attn_bwd", grid=(nb // ATT_BLOCKS,),
        in_specs=[qs] + kspecs + kspecs + [_full((8, 128)), _rows(ATT_BLOCKS * PROB_ROWS, CTX + 3 * WIN),
                                            _rows(tq, 128), _rows(tq, 512)] + [anyspec] * na,
        out_specs=[_rows(tq, QP), _full((KP, E)), _full((KP, E)), _full((8, 128))] + [anyspec] * na,
        out_shape=[jax.ShapeDtypeStruct((S, QP), _BF), jax.ShapeDtypeStruct((KP, E), F32),
                   jax.ShapeDtypeStruct((KP, E), F32), jax.ShapeDtypeStruct((8, 128), F32)]
        + _xchg_out_shapes(slabs, False),
        scratch_shapes=_xchg_scratch(na),
        compiler_params=_cp(("arbitrary",), 48 * 1024 * 1024),
    )(q, *([k] * nk), *([v] * nk), sink, probs, lse, d_attn, *slabs)


GLA_STEP = 2


def _gla_subs(reverse, backward):
    subs = list(range(GLA_STEP))
    return subs[::-1] if reverse != backward else subs


def _gla_order(E, reverse, backward):
    nc = CTX // (GLA_STEP * GLA_T)
    n = E // (GLA_STEP * GLA_T)
    if not reverse:
        fwd = lambda s: s
    else:
        fwd = lambda s: jnp.where(s < nc, nc - 1 - s, n - 1 + nc - s)
    if backward:
        return lambda s: fwd(n - 1 - s)
    return fwd


def _gla_masks():
    T = GLA_T
    l128 = lax.broadcasted_iota(jnp.int32, (1, 128), 1)
    qmask = [((l128 >> 5) == j).astype(F32) for j in range(4)]
    vmask = [((l128 >> 6) == j).astype(F32) for j in range(2)]
    bd = ((lax.broadcasted_iota(jnp.int32, (512, 256), 0) >> 6)
          == (lax.broadcasted_iota(jnp.int32, (512, 256), 1) >> 5)).astype(F32)
    ri = lax.broadcasted_iota(jnp.int32, (T, 2 * T), 0)
    ci = lax.broadcasted_iota(jnp.int32, (T, 2 * T), 1) & (T - 1)
    return qmask, vmask, bd, ri, ci


def _tri_sum(tri, x):
    hi = x.astype(_BF)
    lo = (x - hi.astype(F32)).astype(_BF)
    n = x.shape[1]
    y = _nn(tri.astype(_BF), jnp.concatenate([hi, lo], axis=1))
    return y[:, 0:n] + y[:, n:2 * n]


def _gla_decays(la, reverse, ri, ci):
    T = GLA_T
    msk2 = (ri <= ci) if reverse else (ri >= ci)
    mskT2 = (ri >= ci) if reverse else (ri <= ci)
    b = _tri_sum(msk2[:, 0:T], la)
    bT = b[0:1] if reverse else b[T - 1:T]
    bm = b[T // 2:T // 2 + 1]
    return msk2, mskT2, b, bT, bm


def _pair_stack(tile, m0, m1):
    return jnp.concatenate([(tile * m0).astype(_BF), (tile * m1).astype(_BF)], axis=0)


def _gla_fwd(gq, gk, gv, la, shards=()):
    E = gq.shape[0]
    T = GLA_T
    n = E // T
    TB = GLA_STEP * T
    orders = [_gla_order(E, False, False), _gla_order(E, True, False)]

    def one_direction(reverse, *refs):
        for j in _gla_subs(reverse, False):
            rows = pl.ds(T * j, T)
            one_chunk(reverse, *[r.at[rows] for r in refs[:5]], refs[5].at[j], refs[6])

    def one_chunk(reverse, gq_ref, gk_ref, gv_ref, la_ref, o_ref, st_ref, S_scr):
        qmask, vmask, bd, ri, ci = _gla_masks()
        msk2, _, b, bT, bm = _gla_decays(la_ref[...], reverse, ri, ci)
        q, k, v = gq_ref[...], gk_ref[...], gv_ref[...]
        qd = (q * jnp.exp(b)).astype(_BF)
        qm = (q * jnp.exp(b - bm)).astype(_BF)
        km = k * jnp.exp(bm - b)
        kd = (k * jnp.exp(bT - b)).astype(_BF)
        ST = S_scr[...]
        comp = ST[0:DV]
        for h in range(1, N_GLA):
            comp = comp + ST[DV * h:DV * h + DV]
        st_ref[...] = comp
        inter = _nt(qd, ST.astype(_BF))
        tiles = []
        for p in range(N_GLA // 2):
            qs = slice(128 * (p // 2), 128 * (p // 2) + 128)
            vs = slice(128 * p, 128 * p + 128)
            j0 = (2 * p) % 4
            KS = _pair_stack(km[:, qs], qmask[j0], qmask[j0 + 1])
            VS = _pair_stack(v[:, vs], vmask[0], vmask[1])
            AA = jnp.where(msk2, _nt(qm[:, qs], KS), 0.0).astype(_BF)
            tiles.append(_nn(AA, VS))
        o_ref[...] = inter + jnp.concatenate(tiles, axis=1)
        S_scr[...] = ST * jnp.exp(bT) + bd * _tn(v.astype(_BF), kd)

    def body(qf, kf, vf, lf, qr, kr, vr, lr, of, sf, orr, sr, S_f, S_r):
        @pl.when(pl.program_id(0) == 0)
        def _():
            S_f[...] = jnp.zeros_like(S_f)
            S_r[...] = jnp.zeros_like(S_r)

        one_direction(False, qf, kf, vf, lf, of, sf, S_f)
        one_direction(True, qr, kr, vr, lr, orr, sr, S_r)

    def blk(d, w, c=0):
        return pl.BlockSpec((TB, w), lambda s: (orders[d](s), c))

    def st_spec(d):
        return pl.BlockSpec((GLA_STEP, DV, 256), lambda s: (orders[d](s), 0, 0))

    def o_spec(d):
        return pl.BlockSpec((TB, 512), lambda s: (orders[d](jnp.maximum(s, CTX // TB)) - CTX // TB, 0))

    return _hosted_call(
        body, (gq, gk, gv, la, gq, gk, gv, la), shards, True, name="gla_fwd", grid=(n // GLA_STEP,),
        in_specs=[blk(0, 256), blk(0, 256), blk(0, 512), blk(0, 256, 0), blk(1, 256), blk(1, 256), blk(1, 512),
                  blk(1, 256, 1)],
        out_specs=[o_spec(0), st_spec(0), o_spec(1), st_spec(1)],
        out_shape=[jax.ShapeDtypeStruct((E - CTX, 512), F32), jax.ShapeDtypeStruct((n, DV, 256), F32)] * 2,
        scratch_shapes=[pltpu.VMEM((512, 256), F32)] * 2,
        compiler_params=_cp(("arbitrary",)))


def _gla_bwd(gq, gk, gv, la, st_f, st_r, do, slabs=()):
    E = gq.shape[0]
    T = GLA_T
    n = E // T
    TB = GLA_STEP * T
    nc = CTX // TB
    orders = [_gla_order(E, False, True), _gla_order(E, True, True)]

    def one_direction(reverse, *refs):
        ins, outs, shared = refs[0:6], refs[6:10], refs[10:12]
        for j in _gla_subs(reverse, True):
            rows = pl.ds(T * j, T)
            views = [r.at[rows] for r in ins[0:4]] + [ins[4].at[j], ins[5].at[rows]] + [r.at[rows] for r in outs]
            one_chunk(reverse, *views, *shared)

    def one_chunk(reverse, gq_ref, gk_ref, gv_ref, la_ref, st_ref, do_ref,
                  dq_ref, dk_ref, dv_ref, dlg_ref, bsum_ref, dS_scr):
        cols = slice(256, 512) if reverse else slice(0, 256)
        is_lat = orders[1 if reverse else 0](pl.program_id(0)) >= nc
        qmask, vmask, bd, ri, ci = _gla_masks()
        msk2, mskT2, b, bT, bm = _gla_decays(la_ref[...], reverse, ri, ci)
        q, k, v = gq_ref[...], gk_ref[...], gv_ref[...]
        do = jnp.where(is_lat, do_ref[...].astype(F32), 0.0)
        e_b, e_qm, e_km, e_kd, e_T = jnp.exp(b), jnp.exp(b - bm), jnp.exp(bm - b), jnp.exp(bT - b), jnp.exp(bT)
        qd, qm, km, kd = q * e_b, q * e_qm, k * e_km, k * e_kd
        qdb, qmb, kmb, kdb, vb, dob = (t.astype(_BF) for t in (qd, qm, km, kd, v, do))
        ST = jnp.tile(st_ref[...], (N_GLA, 1)) * bd
        dST = dS_scr[...]
        dSTb = dST.astype(_BF)
        dqd = _nn(dob, ST.astype(_BF))
        dkd = _nn(vb, dSTb)
        dv_t, dqm_t, dkm_t = [], [None, None], [None, None]
        for p in range(N_GLA // 2):
            t = p // 2
            qs = slice(128 * t, 128 * t + 128)
            vs = slice(128 * p, 128 * p + 128)
            j0 = (2 * p) % 4
            QS = _pair_stack(qm[:, qs], qmask[j0], qmask[j0 + 1])
            KS = _pair_stack(km[:, qs], qmask[j0], qmask[j0 + 1])
            VS = _pair_stack(v[:, vs], vmask[0], vmask[1])
            DS = _pair_stack(do[:, vs], vmask[0], vmask[1])
            ATT = jnp.where(mskT2, _nt(kmb[:, qs], QS), 0.0).astype(_BF)
            dAA = jnp.where(msk2, _nt(dob[:, vs], VS), 0.0).astype(_BF)
            dATT = jnp.where(mskT2, _nt(vb[:, vs], DS), 0.0).astype(_BF)
            dv_t.append(_nn(ATT, DS))
            dq_p = _nn(dAA, KS)
            dk_p = _nn(dATT, QS)
            dqm_t[t] = dq_p if dqm_t[t] is None else dqm_t[t] + dq_p
            dkm_t[t] = dk_p if dkm_t[t] is None else dkm_t[t] + dk_p
        dqm = jnp.concatenate(dqm_t, axis=1)
        dkm = jnp.concatenate(dkm_t, axis=1)
        dq = dqm * e_qm + dqd * e_b
        dk = dkm * e_km + dkd * e_kd
        dv = _nt(kdb, dSTb) + jnp.concatenate(dv_t, axis=1)
        dq_ref[...] = (dq * (DK ** -0.5)).astype(_BF)
        dk_ref[...] = dk.astype(_BF)
        dv_ref[...] = dv.astype(_BF)
        db = dqm * qm - dkm * km + dqd * qd - dkd * kd
        dbT = jnp.sum(dkd * kd, axis=0, keepdims=True) + e_T * jnp.sum(dST * ST, axis=0, keepdims=True)
        dla = _tri_sum(mskT2[:, 0:T], db) + dbT
        dlg = dla * (1.0 - jnp.exp(GATE_TAU * la_ref[...])) * (1.0 / GATE_TAU)
        bsum_ref[0:1, cols] += jnp.sum(dlg, axis=0, keepdims=True)
        dlg_ref[...] = dlg.astype(_BF)
        dS_scr[...] = dST * e_T + bd * _tn(dob, qdb)

    def body(*refs):
        ins_f, ins_r = refs[0:6], refs[6:12]
        outs_f, outs_r = refs[12:16], refs[16:20]
        bsum_ref, dS_f, dS_r = refs[20:23]

        @pl.when(pl.program_id(0) == 0)
        def _():
            dS_f[...] = jnp.zeros_like(dS_f)
            dS_r[...] = jnp.zeros_like(dS_r)
            bsum_ref[...] = jnp.zeros_like(bsum_ref)

        one_direction(False, *ins_f, *outs_f, bsum_ref, dS_f)
        one_direction(True, *ins_r, *outs_r, bsum_ref, dS_r)

    def specs(d, st):
        order = orders[d]
        blk = lambda w, c=0: pl.BlockSpec((TB, w), lambda s: (order(s), c))
        ins = [blk(256), blk(256), blk(512), blk(256, d),
               pl.BlockSpec((GLA_STEP, DV, 256), lambda s: (order(s), 0, 0)),
               pl.BlockSpec((TB, 512), lambda s: (jnp.maximum(order(s) - nc, 0), 0))]
        return ins, [blk(256), blk(256), blk(512), blk(256)], (gq, gk, gv, la, st, do)

    in_f, out_f, args_f = specs(0, st_f)
    in_r, out_r, args_r = specs(1, st_r)
    dir_shapes = [jax.ShapeDtypeStruct((E, 256), _BF), jax.ShapeDtypeStruct((E, 256), _BF),
                  jax.ShapeDtypeStruct((E, 512), _BF), jax.ShapeDtypeStruct((E, 256), _BF)]
    return _hosted_call(
        body, args_f + args_r, slabs, False, name="gla_bwd", grid=(n // GLA_STEP,),
        in_specs=in_f + in_r, out_specs=out_f + out_r + [_full((8, 512))],
        out_shape=dir_shapes * 2 + [jax.ShapeDtypeStruct((8, 512), F32)],
        scratch_shapes=[pltpu.VMEM((512, 256), F32)] * 2,
        compiler_params=_cp(("arbitrary",)))


def _gla_out(o, gg, ggla, mavg):
    rr = lax.rsqrt(_head_mean(o * o, mavg) + EPS)
    oh = o * rr
    sg = _sigmoid(gg)
    return oh, rr, sg


def _mix_fwd(x, attn, o_f, o_b, gg, ggla, mavg, wout, gt1, g2):
    S = x.shape[0]
    TM = 512 if S % 512 == 0 else 256

    def body(x_ref, a_ref, of_ref, ob_ref, gg_ref, ggla_ref, mavg_ref, w_ref, gt1_ref, g2_ref, x1_ref, mix_ref):
        gg_t = gg_ref[...]
        oh, _, sg = _gla_out(of_ref[...] + ob_ref[...], gg_t, ggla_ref[...], mavg_ref[...])
        mix_ref[:, 0:512] = a_ref[...]
        mix_ref[:, 512:1024] = (oh * ggla_ref[...] * (gg_t * sg)).astype(_BF)
        y = _nn(mix_ref[...], w_ref[...])
        ry = lax.rsqrt(jnp.mean(y * y, axis=-1, keepdims=True) + EPS)
        x1_ref[...] = x_ref[...] + gt1_ref[...] * ((y * ry) * g2_ref[...])

    return pl.pallas_call(
        body, name="mix_fwd", grid=(S // TM,),
        in_specs=[_rows(TM, D), _rows(TM, 512), _rows(TM, 512), _rows(TM, 512), _rows(TM, 512),
                  _full((1, 512)), _full((512, 512)), _full((D, D)), _full((1, D)), _full((1, D))],
        out_specs=[_rows(TM, D), _rows(TM, D)],
        out_shape=[jax.ShapeDtypeStruct((S, D), F32), jax.ShapeDtypeStruct((S, D), _BF)],
        compiler_params=_cp(("arbitrary",), 48 * 1024 * 1024),
    )(x, attn, o_f, o_b, gg, ggla, mavg, wout, gt1, g2)


def _mix_bwd(dx1, mix, o_f, o_b, gg, ggla, mavg, wout, gt1, g2, slabs):
    S = dx1.shape[0]
    TM = 512 if S % 512 == 0 else 256

    def body(dx_ref, mix_ref, of_ref, ob_ref, gg_ref, ggla_ref, mavg_ref, w_ref, gt1_ref, g2_ref,
             da_ref, do_ref, dgg_ref, dy_ref, sums_ref):
        @pl.when(pl.program_id(0) == 0)
        def _():
            sums_ref[...] = jnp.zeros_like(sums_ref)

        dx = dx_ref[...]
        y = _nn(mix_ref[...], w_ref[...])
        ry = lax.rsqrt(jnp.mean(y * y, axis=-1, keepdims=True) + EPS)
        yh = y * ry
        sums_ref[0:1, :] += jnp.sum(dx * yh, axis=0, keepdims=True)
        dyh = dx * (gt1_ref[...] * g2_ref[...])
        dy = (ry * (dyh - yh * jnp.mean(dyh * yh, axis=-1, keepdims=True))).astype(_BF)
        dy_ref[...] = dy
        dmix = _nt(dy, w_ref[...])
        da_ref[...] = dmix[:, 0:512].astype(_BF)
        dgla = dmix[:, 512:1024]
        gg_t = gg_ref[...]
        ggla_t = ggla_ref[...]
        oh, rr, sg = _gla_out(of_ref[...] + ob_ref[...], gg_t, ggla_t, mavg_ref[...])
        dgg_ref[...] = (dgla * oh * ggla_t * (sg * (1.0 + gg_t * (1.0 - sg)))).astype(_BF)
        don = dgla * (gg_t * sg)
        sums_ref[1:2, 0:512] += jnp.sum(don * oh, axis=0, keepdims=True)
        doh = don * ggla_t
        do_ref[...] = (rr * (doh - oh * _head_mean(doh * oh, mavg_ref[...]))).astype(_BF)

    return _hosted_call(
        body, (dx1, mix, o_f, o_b, gg, ggla, mavg, wout, gt1, g2), slabs, False,
        name="mix_bwd", grid=(S // TM,),
        in_specs=[_rows(TM, D), _rows(TM, D), _rows(TM, 512), _rows(TM, 512), _rows(TM, 512),
                  _full((1, 512)), _full((512, 512)), _full((D, D)), _full((1, D)), _full((1, D))],
        out_specs=[_rows(TM, 512), _rows(TM, 512), _rows(TM, 512), _rows(TM, D), _full((8, D))],
        out_shape=[jax.ShapeDtypeStruct((S, 512), _BF), jax.ShapeDtypeStruct((S, 512), _BF),
                   jax.ShapeDtypeStruct((S, 512), _BF), jax.ShapeDtypeStruct((S, D), _BF),
                   jax.ShapeDtypeStruct((8, D), F32)],
        compiler_params=_cp(("arbitrary",), 48 * 1024 * 1024))


def _ffn(x1, target, gm2, sh2, gt2, g4, wffi, wffo):
    S = x1.shape[0]
    TF = 256

    def body(x_ref, t_ref, gm_ref, sh_ref, gt_ref, g4_ref, wi_hbm, wo_hbm,
             dx_ref, h_ref, du_ref, act_ref, df_ref, sums_ref, loss_ref, wi, wo, sem):
        @pl.when(pl.program_id(0) == 0)
        def _():
            c1 = pltpu.make_async_copy(wi_hbm, wi, sem.at[0])
            c2 = pltpu.make_async_copy(wo_hbm, wo, sem.at[1])
            c1.start()
            c2.start()
            sums_ref[...] = jnp.zeros_like(sums_ref)
            loss_ref[...] = jnp.zeros_like(loss_ref)
            c1.wait()
            c2.wait()

        x = x_ref[...]
        gm = gm_ref[...]
        r = lax.rsqrt(jnp.mean(x * x, axis=-1, keepdims=True) + EPS)
        xh = x * r
        hb = (xh * gm + sh_ref[...]).astype(_BF)
        h_ref[...] = hb
        u = _nt(hb, wi[...])
        g = u[:, 0:FFN]
        up = u[:, FFN:2 * FFN]
        sg = _sigmoid(g)
        sl = g * sg
        ab = (sl * up).astype(_BF)
        act_ref[...] = ab
        f = _nn(ab, wo[...])
        rf = lax.rsqrt(jnp.mean(f * f, axis=-1, keepdims=True) + EPS)
        fh = f * rf
        gt, g4v = gt_ref[...], g4_ref[...]
        err = x + gt * (fh * g4v) - t_ref[...]
        loss_ref[...] += jnp.sum(err * err) * (0.5 / D)
        dout = err * (1.0 / D)
        sums_ref[2:3, :] += jnp.sum(dout * fh, axis=0, keepdims=True)
        dfh = dout * (gt * g4v)
        dfb = (rf * (dfh - fh * jnp.mean(dfh * fh, axis=-1, keepdims=True))).astype(_BF)
        df_ref[...] = dfb
        dact = _nt(dfb, wo[...])
        du_ref[:, 0:FFN] = (dact * up * (sg * (1.0 + g * (1.0 - sg)))).astype(_BF)
        du_ref[:, FFN:2 * FFN] = (dact * sl).astype(_BF)
        dh = _nn(du_ref[...], wi[...])
        sums_ref[0:1, :] += jnp.sum(dh, axis=0, keepdims=True)
        sums_ref[1:2, :] += jnp.sum(dh * xh, axis=0, keepdims=True)
        dxh = dh * gm
        dx_ref[...] = dout + r * (dxh - xh * jnp.mean(dxh * xh, axis=-1, keepdims=True))

    vec = _full((1, D))
    anyspec = pl.BlockSpec(memory_space=pl.ANY)
    return pl.pallas_call(
        body, name="ffn_fwd_bwd", grid=(S // TF,),
        in_specs=[_rows(TF, D), _rows(TF, D), vec, vec, vec, vec, anyspec, anyspec],
        out_specs=[_rows(TF, D), _rows(TF, D), _rows(TF, 2 * FFN), _rows(TF, FFN), _rows(TF, D),
                   _full((8, D)), _full((8, 128))],
        out_shape=[jax.ShapeDtypeStruct((S, D), F32), jax.ShapeDtypeStruct((S, D), _BF),
                   jax.ShapeDtypeStruct((S, 2 * FFN), _BF), jax.ShapeDtypeStruct((S, FFN), _BF),
                   jax.ShapeDtypeStruct((S, D), _BF), jax.ShapeDtypeStruct((8, D), F32),
                   jax.ShapeDtypeStruct((8, 128), F32)],
        scratch_shapes=[pltpu.VMEM((2 * FFN, D), _BF), pltpu.VMEM((FFN, D), _BF), pltpu.SemaphoreType.DMA((2,))],
        compiler_params=_cp(("arbitrary",), VMEM_BIG),
    )(x1, target, gm2, sh2, gt2, g4, wffi, wffo)


def _inproj_bwd(x, ctx, gml, gmc, win, wg, cos, sa, sb, dq, dk, dv, dgq, dgk, dgv, dgg, dlg_f, dlg_b, dx1, z):
    S = x.shape[0]
    E = S + CTX
    TE = CTX

    def body(x_ref, c_ref, gml_ref, gmc_ref, w_ref, wg_ref, cos_ref, sa_ref, sb_ref, dq_ref, dk_ref, dv_ref,
             gqf, gqb, gkf, gkb, gvf, gvb, dgg_ref, dlf, dlb, dx1_ref, z_ref, dp_ref, gx_ref, sums_ref, gwg_ref):
        i = pl.program_id(0)
        is_ctx = i == 0

        @pl.when(is_ctx)
        def _():
            sums_ref[...] = jnp.zeros_like(sums_ref)
            gwg_ref[...] = jnp.zeros_like(gwg_ref)

        lat = jnp.where(is_ctx, 0.0, 1.0)
        cos_t, sa_t, sb_t = cos_ref[...], sa_ref[...], sb_ref[...]
        dp_ref[:, O_Q:O_K] = (_unrope(dq_ref[...].astype(F32), cos_t, sa_t, sb_t) * lat).astype(_BF)
        dp_ref[:, O_K:O_V] = _unrope(dk_ref[...].T, cos_t, sa_t, sb_t).astype(_BF)
        dp_ref[:, O_V:O_GQ] = dv_ref[...].T.astype(_BF)
        dp_ref[:, O_GQ:O_GK] = (gqf[...].astype(F32) + gqb[...].astype(F32)).astype(_BF)
        dp_ref[:, O_GK:O_GV] = (gkf[...].astype(F32) + gkb[...].astype(F32)).astype(_BF)
        dp_ref[:, O_GV:O_GG] = (gvf[...].astype(F32) + gvb[...].astype(F32)).astype(_BF)
        dp_ref[:, O_GG:O_Z] = (dgg_ref[...].astype(F32) * lat).astype(_BF)
        dlg = jnp.concatenate([dlf[...], dlb[...]], axis=1)
        gwg_ref[...] += _tn(z_ref[...], dlg)
        dp_ref[:, O_Z:NP] = _nt(dlg, wg_ref[...]).astype(_BF)
        dh = _nn(dp_ref[...], w_ref[...])
        x = jnp.where(is_ctx, c_ref[...], x_ref[...])
        r = lax.rsqrt(jnp.mean(x * x, axis=-1, keepdims=True) + EPS)
        xh = x * r
        sdh = jnp.sum(dh, axis=0, keepdims=True)
        sdx = jnp.sum(dh * xh, axis=0, keepdims=True)
        sums_ref[0:1, :] += sdh * lat
        sums_ref[1:2, :] += sdx * lat
        sums_ref[2:3, :] += sdh * (1.0 - lat)
        sums_ref[3:4, :] += sdx * (1.0 - lat)
        dxh = dh * jnp.where(is_ctx, gmc_ref[...], gml_ref[...])
        gx_ref[...] = dx1_ref[...] + r * (dxh - xh * jnp.mean(dxh * xh, axis=-1, keepdims=True))

    vec = _full((1, D))
    tab = _rows(TE, 128)
    return pl.pallas_call(
        body, name="inproj_bwd", grid=(E // TE,),
        in_specs=[_rows_lat(TE, D), _full((CTX, D)), vec, vec, _full((NP, D)), _full((128, 512)), tab, tab, tab,
                  _rows_lat(TE, QP), pl.BlockSpec((KP, TE), lambda i: (0, i)), pl.BlockSpec((KP, TE), lambda i: (0, i)),
                  _rows(TE, 256), _rows(TE, 256), _rows(TE, 256), _rows(TE, 256), _rows(TE, 512), _rows(TE, 512),
                  _rows_lat(TE, 512), _rows(TE, 256), _rows(TE, 256), _rows_lat(TE, D), _rows(TE, 128)],
        out_specs=[_rows(TE, NP), _rows_lat(TE, D), _full((8, D)), _full((128, 512))],
        out_shape=[jax.ShapeDtypeStruct((E, NP), _BF), jax.ShapeDtypeStruct((S, D), F32),
                   jax.ShapeDtypeStruct((8, D), F32), jax.ShapeDtypeStruct((128, 512), F32)],
        compiler_params=_cp(("arbitrary",), VMEM_BIG),
    )(x, ctx, gml, gmc, win, wg, cos, sa, sb, dq, dk, dv, *dgq, *dgk, *dgv, dgg, dlg_f, dlg_b, dx1, z)


def _matmul_tn(a, b, tk, tt, name, out_dtype, transpose_out=False, a_cols=None, hosted=(), gather=True):
    T, KA = a.shape
    N = b.shape[1]
    nt = T // tt
    k0 = 0
    if a_cols is not None:
        KA, k0 = tk, a_cols

    def body(a_ref, b_ref, o_ref, acc):
        t = pl.program_id(1)

        @pl.when(t == 0)
        def _():
            acc[...] = jnp.zeros_like(acc)

        acc[...] += _tn(a_ref[...], b_ref[...])

        @pl.when(t == nt - 1)
        def _():
            o_ref[...] = (acc[...].T if transpose_out else acc[...]).astype(out_dtype)

    if transpose_out:
        out_spec, out_shape = pl.BlockSpec((N, tk), lambda i, t: (0, i)), (N, KA)
    else:
        out_spec, out_shape = pl.BlockSpec((tk, N), lambda i, t: (i, 0)), (KA, N)
    res = _hosted_call(
        body, (a, b), hosted, gather, name=name, grid=(KA // tk, nt),
        in_specs=[pl.BlockSpec((tt, tk), lambda i, t: (t, i + k0)), pl.BlockSpec((tt, N), lambda i, t: (t, 0))],
        out_specs=[out_spec], out_shape=[jax.ShapeDtypeStruct(out_shape, out_dtype)],
        scratch_shapes=[pltpu.VMEM((tk, N), F32)],
        compiler_params=_cp(("arbitrary", "arbitrary"), VMEM_BIG))
    return res if hosted else res[0]


def _ada_bwd(c_all, c_ctx, w_ada, d_all):
    n = w_ada.shape[1]

    def body(c_ref, cc_ref, w_ref, d_ref, gw_ref, t_ref):
        c = jnp.concatenate([c_ref[...], jnp.broadcast_to(cc_ref[...], (8, D))], axis=0)
        db = d_ref[...].astype(_BF)
        gw_ref[0] = _tn((c * _sigmoid(c)).astype(_BF), db)
        t_ref[...] = _nt(db[8:16], w_ref[...].astype(_BF))

    return pl.pallas_call(
        body, name="ada_bwd", in_specs=[_full((8, D)), _full((1, D)), _full((D, n)), _full((16, n))],
        out_specs=[_full((1, D, n)), _full((8, D))],
        out_shape=[jax.ShapeDtypeStruct((1, D, n), F32), jax.ShapeDtypeStruct((8, D), F32)], grid=(1,),
        compiler_params=_cp(("arbitrary",)),
    )(c_all, c_ctx, w_ada, d_all)


PART_ROWS = 56
R_ADA, R_ADA_C, R_GAIN, R_SINK, R_BG, R_GGLA, R_LOSS, R_WG = 0, 6, 12, 16, 17, 18, 19, 24


def _small_grads(s_in, s_ffn, s_mix, ada_l, ada_c, gains, dsink, s_bg, g_wg, loss):
    def body(si, sf, sm, al, ac, g, ds, sbg, gwg, loss_ref, o_ref):
        o_ref[...] = jnp.zeros_like(o_ref)
        o_ref[R_LOSS:R_LOSS + 1, 0:128] = loss_ref[0:1, :]
        sub = lax.broadcasted_iota(jnp.int32, (8, 128), 0)
        lane = lax.broadcasted_iota(jnp.int32, (8, 128), 1)
        o_ref[R_SINK:R_SINK + 1, 0:128] = jnp.sum(jnp.where(sub == lane, ds[...], 0.0), axis=0, keepdims=True)
        o_ref[R_BG:R_BG + 1, 0:512] = sbg[0:1, :]
        y = sm[1:2, 0:128] + sm[1:2, 128:256] + sm[1:2, 256:384] + sm[1:2, 384:512]
        y = y + pltpu.roll(y, 64, 1)
        o_ref[R_GGLA:R_GGLA + 1, 0:128] = jnp.where(lane[0:1] < DV, y, 0.0)
        o_ref[R_WG:R_WG + 16, 0:256] = gwg[0:16, 0:256]
        o_ref[R_WG + 16:R_WG + 32, 0:256] = gwg[16:32, 256:512]
        sdh_l, sdx_l, sdh_c, sdx_c = si[0:1], si[1:2], si[2:3], si[3:4]
        sdh2, sdx2, a2 = sf[0:1], sf[1:2], sf[2:3]
        a1 = sm[0:1]
        g1, g2, g3, g4 = g[0:1], g[1:2], g[2:3], g[3:4]
        sc1, gt1, sc2, gt2 = al[1:2], al[2:3], al[4:5], al[5:6]
        sc1c = ac[1:2]
        z = jnp.zeros((1, D), F32)
        rows = [sdh_l, sdx_l * g1, a1 * g2, sdh2, sdx2 * g3, a2 * g4,
                sdh_c, sdx_c * g1, z, z, z, z,
                sdx_l * (1.0 + sc1) + sdx_c * (1.0 + sc1c), a1 * gt1, sdx2 * (1.0 + sc2), a2 * gt2]
        for r, v in enumerate(rows):
            o_ref[r:r + 1, :] = v

    v8 = _full((8, D))
    return pl.pallas_call(
        body, name="small_grads",
        in_specs=[v8] * 6 + [_full((8, 128)), _full((8, 512)), _full((128, 512)), _full((8, 128))],
        out_specs=_full((PART_ROWS, D)), out_shape=jax.ShapeDtypeStruct((PART_ROWS, D), F32), grid=(1,),
        compiler_params=_cp(("arbitrary",)),
    )(s_in, s_ffn, s_mix, ada_l, ada_c, gains, dsink, s_bg, g_wg, loss)


def _row_tile(R):
    for cand in range(min(512, R // 2) // 16 * 16, 0, -16):
        if R % cand == 0:
            return cand
    return R


def _adamw(w, gs, m, v, name):
    _, R, C = w.shape
    tr = _row_tile(R)
    c1 = 1.0 / (1.0 - ADAM_B1 ** ADAM_STEP)
    c2 = 1.0 / (1.0 - ADAM_B2 ** ADAM_STEP)
    ng = len(gs)

    def body(w_ref, *refs):
        g_refs, (m_ref, v_ref, go_ref, d_ref, nm_ref, nv_ref) = refs[:ng], refs[ng:]
        c0 = 0
        for g_ref in g_refs:
            cols = slice(c0, c0 + g_ref.shape[2])
            c0 += g_ref.shape[2]
            gg = g_ref[0].astype(F32)
            for j in range(1, g_ref.shape[0]):
                gg = gg + g_ref[j].astype(F32)
            go_ref[0, :, cols] = gg
            nm = ADAM_B1 * m_ref[0, :, cols] + (1.0 - ADAM_B1) * gg
            nv = ADAM_B2 * v_ref[0, :, cols] + (1.0 - ADAM_B2) * (gg * gg)
            nm_ref[0, :, cols] = nm
            nv_ref[0, :, cols] = nv
            d_ref[0, :, cols] = -ADAM_LR * ((nm * c1) / (jnp.sqrt(nv * c2) + ADAM_EPS) + ADAM_WD * w_ref[0, :, cols])

    spec = pl.BlockSpec((1, tr, C), lambda i: (0, i, 0))
    sds = jax.ShapeDtypeStruct((1, R, C), F32)
    g_specs = [pl.BlockSpec((g.shape[0], tr, g.shape[2]), lambda i: (0, i, 0)) for g in gs]
    return pl.pallas_call(
        body, name=name, grid=(R // tr,), in_specs=[spec] + g_specs + [spec, spec], out_specs=[spec] * 4,
        out_shape=[sds] * 4, compiler_params=_cp(("parallel",), 48 * 1024 * 1024),
    )(w, *gs, m, v)


def _sum_slots(slots, name):
    _, R, C = slots.shape
    tr = _row_tile(R)

    def body(s_ref, o_ref):
        acc = s_ref[0].astype(F32)
        for j in range(1, N_DEV):
            acc = acc + s_ref[j].astype(F32)
        o_ref[...] = acc

    return pl.pallas_call(
        body, name=name, grid=(R // tr,), in_specs=[pl.BlockSpec((N_DEV, tr, C), lambda i: (0, i, 0))],
        out_specs=_rows(tr, C), out_shape=jax.ShapeDtypeStruct((R, C), F32), compiler_params=_cp(("parallel",)),
    )(slots)


def _ag2_start(x_ref, out_ref, send_sems, recv_sems, local_sem):
    x, y, c = lax.axis_index("x"), lax.axis_index("y"), lax.axis_index("c")
    me, sibling = (x, y, c), (x, y, 1 - c)
    chips = [(1 - x, y), (x, 1 - y), (1 - x, 1 - y)]

    def rows(px, py, pc):
        return out_ref.at[4 * px + 2 * py + pc]

    def copy(k, block, to, src=None):
        return pltpu.make_async_remote_copy(
            src_ref=rows(*block) if src is None else src, dst_ref=rows(*block),
            send_sem=send_sems.at[k], recv_sem=recv_sems.at[k], device_id=to, device_id_type=MESH)

    mine = pltpu.make_async_copy(x_ref, rows(*me), local_sem)
    mine.start()
    first = [copy(0, me, sibling, src=x_ref)]
    first += [copy(1 + j, me, (*chip, c), src=x_ref) for j, chip in enumerate(chips)]
    for cp in first:
        cp.start()
    return copy, mine, first, me, sibling, chips, c


def _ag2_finish(state):
    copy, mine, first, me, sibling, chips, c = state
    passed = [copy(4 + j, (*chip, c), sibling) for j, chip in enumerate(chips)]
    for j, chip in enumerate(chips):
        copy(1 + j, (*chip, c), me).wait_recv()
        passed[j].start()
    copy(0, sibling, me).wait_recv()
    for j, chip in enumerate(chips):
        copy(4 + j, (*chip, 1 - c), me).wait_recv()
    for cp in first + passed:
        cp.wait_send()
    mine.wait()


def _exchange(arrays, name, gather):
    na = len(arrays)

    def body(*refs):
        cps = _xchg_copies(refs[:na], refs[na:2 * na], *refs[2 * na:], gather=gather)
        _xchg_start(cps)
        _xchg_finish(cps)

    anyspec = pl.BlockSpec(memory_space=pl.ANY)
    return pl.pallas_call(
        body, name=name, out_shape=_xchg_out_shapes(arrays, gather), in_specs=[anyspec] * na,
        out_specs=[anyspec] * na, scratch_shapes=_xchg_scratch(na),
    )(*arrays)


def _entry(c, wg_sh, win_sh, c_ctx, w_ada):
    n = w_ada.shape[1]

    def body(c_ref, g_ref, w_ref, cc_ref, wa_ref, call_ref, gall_ref, wall_ref, ada_ref, part,
             s_send, s_recv, s_loc, w_send, w_recv, w_loc, a_send, a_recv, a_loc):
        big = _ag2_start(w_ref, wall_ref, w_send, w_recv, w_loc)
        small = _xchg_copies([c_ref, g_ref], [call_ref, gall_ref], s_send, s_recv, s_loc, gather=True)
        _xchg_start(small)
        _xchg_finish(small)
        cs = jnp.concatenate([call_ref[:, 0, :], jnp.broadcast_to(cc_ref[...], (8, D))], axis=0)
        part[...] = _nn((cs * _sigmoid(cs)).astype(_BF), wa_ref[...].astype(_BF))
        ada = _xchg_copies([part], [ada_ref], a_send, a_recv, a_loc, gather=True)
        _xchg_start(ada)
        _xchg_finish(ada)
        _ag2_finish(big)

    vm = pl.BlockSpec(memory_space=pltpu.VMEM)
    return pl.pallas_call(
        body, name="entry_gather",
        out_shape=[jax.ShapeDtypeStruct((N_DEV,) + c.shape, F32), jax.ShapeDtypeStruct((N_DEV,) + wg_sh.shape, F32),
                   jax.ShapeDtypeStruct((N_DEV,) + win_sh.shape, win_sh.dtype),
                   jax.ShapeDtypeStruct((N_DEV, 16, n), F32)],
        in_specs=[vm] * 5, out_specs=[vm] * 4,
        scratch_shapes=[pltpu.VMEM((16, n), F32)] + _xchg_scratch(2)
        + [pltpu.SemaphoreType.DMA((7,)), pltpu.SemaphoreType.DMA((7,)), pltpu.SemaphoreType.DMA] + _xchg_scratch(1),
        compiler_params=pltpu.CompilerParams(vmem_limit_bytes=VMEM_BIG),
    )(c, wg_sh, win_sh, c_ctx, w_ada)


def _rope_tables(S):
    t = np.arange(S)
    row = (t // GRID_W).astype(np.float32)
    colp = (t % GRID_W).astype(np.float32)
    half = HD // 2
    inv = (ROPE_BASE ** (-np.arange(0, half, 2, dtype=np.float32) / half)).astype(np.float32)
    ar = row[:, None] * inv[None, :]
    ac = colp[:, None] * inv[None, :]
    ang = np.concatenate([ar, ar, ac, ac], axis=-1).astype(np.float32)
    cos = np.cos(ang).astype(np.float32)
    sin = np.sin(ang).astype(np.float32)
    lane = np.arange(HD)
    first = (lane % 32) < 16
    sa = np.where(first[None, :], -sin, 0.0)
    sb = np.where(first[None, :], 0.0, sin)

    def ext(tab, ctx_val):
        full = np.zeros((CTX + S, 128), np.float32)
        full[:CTX, :] = ctx_val
        full[CTX:, :HD] = tab
        full[CTX:, HD:] = tab
        return jnp.asarray(full)

    return ext(cos, 1.0), ext(sa, 0.0), ext(sb, 0.0)


def _pad_rows_win(wt):
    return jnp.pad(wt, ((0, NP - IN_COLS), (0, 0)))


def _unpad_rows_win(g):
    return g[0:IN_COLS]


def _local_step(x, ctx, target, ada_l, ada_c, gains, sink, win_p, wg_bd, bg, ggla, wout_sh, wffi_sh, wffo_sh,
                after_small):
    S = x.shape[0]
    cos, sa, sb = _rope_tables(S)
    g1, g2, g3, g4 = (gains[i:i + 1] for i in range(4))
    sh1, sc1, gt1, sh2, sc2, gt2 = (ada_l[i:i + 1] for i in range(6))
    sh1c, sc1c = ada_c[0:1], ada_c[1:2]
    gml, gmc, gm2 = g1 * (1.0 + sc1), g1 * (1.0 + sc1c), g3 * (1.0 + sc2)
    mavg = jnp.asarray(np.kron(np.eye(N_GLA, dtype=np.float32), np.full((DV, DV), 1.0 / DV, np.float32))).astype(_BF)

    n_ffi, r_ffo, r_out = wffi_sh.shape[0], wffo_sh.shape[0], wout_sh.shape[0]
    tt_e = 1408 if (S + CTX) % 1408 == 0 else 256
    tt_s = 512 if S % 512 == 0 else 256
    h, q, k, v, gq, gk, gv, gg, z, la, wout_g = _inproj_fwd(x, ctx, gml, sh1, gmc, sh1c, win_p, wg_bd, bg,
                                                            cos, sa, sb, [wout_sh])
    attn, lse, probs, wffi_g = _attn_fwd(q, k, v, sink, [wffi_sh])
    o_f, st_f, o_b, st_b, wffo_g = _gla_fwd(gq, gk, gv, la, [wffo_sh])
    wout = wout_g.reshape(N_DEV * r_out, D)
    wffi = wffi_g.reshape(N_DEV * n_ffi, D)
    wffo = wffo_g.reshape(N_DEV * r_ffo, D)
    x1, mix = _mix_fwd(x, attn, o_f, o_b, gg, ggla, mavg, wout, gt1, g2)
    dx1, h2, du, act, df, s_ffn, loss = _ffn(x1, target, gm2, sh2, gt2, g4, wffi, wffo)
    slab_ffi = _matmul_tn(h2, du, 512, tt_s, "grad_w_ffn_in", _BF, True).reshape(N_DEV, n_ffi, D)
    tt_l = 1024 if S % 1024 == 0 else tt_s
    slab_ffo = _matmul_tn(act, df, FFN, tt_l, "grad_w_ffn_out", _BF).reshape(N_DEV, r_ffo, D)
    d_attn, do_gla, dgg, dy, s_mix, got_ffo = _mix_bwd(dx1, mix, o_f, o_b, gg, ggla, mavg, wout, gt1, g2, [slab_ffo])
    slab_out = _matmul_tn(mix, dy, D, tt_l, "grad_w_out", _BF).reshape(N_DEV, r_out, D)
    dq, dk, dv, dsink, got_ffi = _attn_bwd(q, k, v, sink, probs, lse, d_attn, [slab_ffi])
    (dgq_f, dgk_f, dgv_f, dlg_f, dgq_b, dgk_b, dgv_b, dlg_b, s_bg,
     got_out) = _gla_bwd(gq, gk, gv, la, st_f, st_b, do_gla, [slab_out])
    dp, grad_x, s_in, g_wg = _inproj_bwd(x, ctx, gml, gmc, win_p, wg_bd, cos, sa, sb, dq, dk, dv,
                                         (dgq_f, dgq_b), (dgk_f, dgk_b), (dgv_f, dgv_b), dgg, dlg_f, dlg_b, dx1, z)
    small = _small_grads(s_in, s_ffn, s_mix, ada_l, ada_c, gains, dsink, s_bg, g_wg, loss)
    n_in, n_grp = IN_COLS // N_DEV, 2
    got_in, slab, extra, got_extra = [], None, [], []
    for j in range(n_grp):
        res = _matmul_tn(h, dp, D // n_grp, tt_e, "grad_w_in_%d" % j, _BF, True, a_cols=j,
                         hosted=[small] if j == 0 else [slab] + extra, gather=(j == 0))
        if j == 0:
            extra = after_small(res[1])
        else:
            got_in.append(res[1])
            got_extra, extra = got_extra + list(res[2:]), []
        slab = _unpad_rows_win(res[0]).reshape(N_DEV, n_in, D // n_grp)
    got_in.append(_exchange([slab], "scatter_grads", False)[0])
    return dict(grad_x=grad_x, got_in=got_in, got_out=got_out, got_ffi=got_ffi, got_ffo=got_ffo, got_extra=got_extra)


SMALL_NAMES = ["c_ctx", "b_ada", "g_pre_mix", "g_post_mix", "g_pre_ffn", "g_post_ffn", "attn_sink",
               "b_gate_fwd", "b_gate_bwd", "g_gla_norm", "w_gate_fwd", "w_gate_bwd"]


def _small_update(tot, t_tot, wg_g, w, m, v):
    c1 = 1.0 / (1.0 - ADAM_B1 ** ADAM_STEP)
    c2 = 1.0 / (1.0 - ADAM_B2 ** ADAM_STEP)
    n = len(SMALL_NAMES)

    def body(tot_ref, t_ref, wg_ref, *refs):
        w_r, m_r, v_r = refs[0:n], refs[n:2 * n], refs[2 * n:3 * n]
        g_o, d_o, nm_o, nv_o = refs[3 * n:4 * n], refs[4 * n:5 * n], refs[5 * n:6 * n], refs[6 * n:7 * n]

        def upd(i, idx, g):
            nm = ADAM_B1 * m_r[i][idx] + (1.0 - ADAM_B1) * g
            nv = ADAM_B2 * v_r[i][idx] + (1.0 - ADAM_B2) * (g * g)
            g_o[i][idx] = g
            nm_o[i][idx] = nm
            nv_o[i][idx] = nv
            d_o[i][idx] = -ADAM_LR * ((nm * c1) / (jnp.sqrt(nv * c2) + ADAM_EPS) + ADAM_WD * w_r[i][idx])

        everything = (slice(None), slice(None))
        cc = w_r[0][...]
        sc = _sigmoid(cc)
        upd(0, everything, t_ref[0:1, :] * (sc * (1.0 + cc * (1.0 - sc))))
        for j in range(6):
            upd(1, (slice(None), slice(D * j, D * j + D)),
                tot_ref[R_ADA + j:R_ADA + j + 1, :] + tot_ref[R_ADA_C + j:R_ADA_C + j + 1, :])
        for j in range(4):
            upd(2 + j, everything, tot_ref[R_GAIN + j:R_GAIN + j + 1, :])
        upd(6, everything, tot_ref[R_SINK:R_SINK + 1, 0:N_ATT])
        upd(7, everything, tot_ref[R_BG:R_BG + 1, 0:256])
        upd(8, everything, tot_ref[R_BG:R_BG + 1, 256:512])
        upd(9, everything, tot_ref[R_GGLA:R_GGLA + 1, 0:DV])
        upd(10, (0,), wg_ref[0:GATE_RANK, :])
        upd(11, (0,), wg_ref[GATE_RANK:2 * GATE_RANK, :])

    params = [w[k] for k in SMALL_NAMES] + [m[k] for k in SMALL_NAMES] + [v[k] for k in SMALL_NAMES]
    outs = pl.pallas_call(
        body, name="small_update", grid=(1,),
        in_specs=[_full(tot.shape), _full(t_tot.shape), _full(wg_g.shape)] + [_full(p.shape) for p in params],
        out_specs=[_full(w[k].shape) for k in SMALL_NAMES] * 4,
        out_shape=[jax.ShapeDtypeStruct(w[k].shape, F32) for k in SMALL_NAMES] * 4,
        compiler_params=_cp(("arbitrary",)),
    )(tot, t_tot, wg_g, *params)
    return tuple(dict(zip(SMALL_NAMES, outs[i * n:(i + 1) * n])) for i in range(4))


def kernel(x, c, ctx, c_ctx, w_ada, b_ada, g_pre_mix, g_post_mix, g_pre_ffn, g_post_ffn, w_in, attn_sink, w_gate_fwd, b_gate_fwd, w_gate_bwd, b_gate_bwd, g_gla_norm, w_out, w_ffn_in, w_ffn_out, loss_target, m_c_ctx, m_w_ada, m_b_ada, m_g_pre_mix, m_g_post_mix, m_g_pre_ffn, m_g_post_ffn, m_w_in, m_attn_sink, m_w_gate_fwd, m_b_gate_fwd, m_w_gate_bwd, m_b_gate_bwd, m_g_gla_norm, m_w_out, m_w_ffn_in, m_w_ffn_out, v_c_ctx, v_w_ada, v_b_ada, v_g_pre_mix, v_g_post_mix, v_g_pre_ffn, v_g_post_ffn, v_w_in, v_attn_sink, v_w_gate_fwd, v_b_gate_fwd, v_w_gate_bwd, v_b_gate_bwd, v_g_gla_norm, v_w_out, v_w_ffn_in, v_w_ffn_out):
    me = 4 * lax.axis_index("x") + 2 * lax.axis_index("y") + lax.axis_index("c")
    S = x.shape[1]
    n_in = w_in.shape[2]
    n_ffi = w_ffn_in.shape[2]
    r_out = w_out.shape[1]
    r_ffo = w_ffn_out.shape[1]
    n_ada = w_ada.shape[2]

    wg_sh = jnp.concatenate([w_gate_fwd.reshape(4, 128), w_gate_bwd.reshape(4, 128)], axis=0)
    c_all3, g_all, w_all, ada_all = _entry(c, wg_sh, w_in[0].T.astype(_BF), c_ctx.reshape(1, D), w_ada[0])
    c_all = c_all3.reshape(N_DEV, D)
    wgf = g_all[:, 0:4].reshape(N_DEV, GATE_RANK, 32).transpose(1, 0, 2).reshape(GATE_RANK, 256)
    wgb = g_all[:, 4:8].reshape(N_DEV, GATE_RANK, 32).transpose(1, 0, 2).reshape(GATE_RANK, 256)
    win_p = _pad_rows_win(w_all.reshape(N_DEV * n_in, D))
    wg_bd = jnp.zeros((128, 512), F32).at[0:16, 0:256].set(wgf).at[16:32, 256:512].set(wgb).astype(_BF)
    ada_full = ada_all.transpose(1, 0, 2).reshape(16, N_DEV * n_ada) + b_ada
    ada_l = jnp.pad(lax.dynamic_slice_in_dim(ada_full, me, 1, 0).reshape(6, D), ((0, 2), (0, 0)))
    ada_c = jnp.pad(ada_full[8].reshape(6, D), ((0, 2), (0, 0)))
    gains = jnp.pad(jnp.concatenate([g_pre_mix, g_post_mix, g_pre_ffn, g_post_ffn], axis=0), ((0, 4), (0, 0)))
    sink = jnp.broadcast_to(attn_sink.reshape(8, 1), (8, 128))
    bg = jnp.concatenate([b_gate_fwd, b_gate_bwd], axis=1)
    ggla = jnp.tile(g_gla_norm, (1, N_GLA))

    tail = {}

    def after_small(parts):
        tot = _sum_slots(parts, "sum_small_grads")
        d_ada_rows = parts[:, R_ADA:R_ADA + 6].reshape(N_DEV, 6 * D)
        d_ada_c = tot[R_ADA_C:R_ADA_C + 6].reshape(1, 6 * D)
        my_cols = lax.dynamic_slice_in_dim(jnp.concatenate([d_ada_rows, jnp.broadcast_to(d_ada_c, (1, 6 * D)),
                                                            jnp.zeros((7, 6 * D), F32)], axis=0), me * n_ada, n_ada, 1)
        grad_w_ada, t_part = _ada_bwd(c_all, c_ctx.reshape(1, D), w_ada[0], my_cols)
        tail.update(tot=tot, grad_w_ada=grad_w_ada)
        return [jnp.broadcast_to(t_part[None], (N_DEV,) + t_part.shape)]

    r = _local_step(x[0], ctx[0], loss_target[0], ada_l, ada_c, gains, sink, win_p, wg_bd, bg, ggla,
                    w_out[0].astype(_BF), w_ffn_in[0].T.astype(_BF), w_ffn_out[0].astype(_BF), after_small)
    tot, grad_w_ada = tail["tot"], tail["grad_w_ada"]
    loss = tot[R_LOSS, 0]
    wg_g = lax.dynamic_slice(tot, (R_WG, me * 32), (2 * GATE_RANK, 32))

    tr = lambda a: jnp.transpose(a, (0, 2, 1))
    big = {}
    t_tot = _sum_slots(r["got_extra"][0], "sum_c_ctx")
    for nm, w, g, m, v in [("w_ada", w_ada, grad_w_ada, m_w_ada, v_w_ada),
                           ("w_out", w_out, r["got_out"], m_w_out, v_w_out),
                           ("w_ffn_out", w_ffn_out, r["got_ffo"], m_w_ffn_out, v_w_ffn_out)]:
        big[nm] = _adamw(w, [g], m, v, "adamw_" + nm)
    big["w_ffn_in"] = tuple(tr(o) for o in _adamw(tr(w_ffn_in), [r["got_ffi"]], tr(m_w_ffn_in), tr(v_w_ffn_in),
                                                  "adamw_w_ffn_in"))
    big["w_in"] = tuple(tr(o) for o in _adamw(tr(w_in), r["got_in"], tr(m_w_in), tr(v_w_in), "adamw_w_in"))

    w_small = dict(c_ctx=c_ctx.reshape(1, D), b_ada=b_ada, g_pre_mix=g_pre_mix, g_post_mix=g_post_mix, g_pre_ffn=g_pre_ffn,
                   g_post_ffn=g_post_ffn, attn_sink=attn_sink, b_gate_fwd=b_gate_fwd, b_gate_bwd=b_gate_bwd,
                   g_gla_norm=g_gla_norm, w_gate_fwd=w_gate_fwd, w_gate_bwd=w_gate_bwd)
    m_small = dict(c_ctx=m_c_ctx.reshape(1, D), b_ada=m_b_ada, g_pre_mix=m_g_pre_mix, g_post_mix=m_g_post_mix,
                   g_pre_ffn=m_g_pre_ffn, g_post_ffn=m_g_post_ffn, attn_sink=m_attn_sink, b_gate_fwd=m_b_gate_fwd,
                   b_gate_bwd=m_b_gate_bwd, g_gla_norm=m_g_gla_norm, w_gate_fwd=m_w_gate_fwd, w_gate_bwd=m_w_gate_bwd)
    v_small = dict(c_ctx=v_c_ctx.reshape(1, D), b_ada=v_b_ada, g_pre_mix=v_g_pre_mix, g_post_mix=v_g_post_mix,
                   g_pre_ffn=v_g_pre_ffn, g_post_ffn=v_g_post_ffn, attn_sink=v_attn_sink, b_gate_fwd=v_b_gate_fwd,
                   b_gate_bwd=v_b_gate_bwd, g_gla_norm=v_g_gla_norm, w_gate_fwd=v_w_gate_fwd, w_gate_bwd=v_w_gate_bwd)
    grads_small, d_s, nm_s, nv_s = _small_update(tot, t_tot, wg_g, w_small, m_small, v_small)
    for dd in (grads_small, d_s, nm_s, nv_s):
        dd["c_ctx"] = dd["c_ctx"].reshape(D)

    order = ["c_ctx", "w_ada", "b_ada", "g_pre_mix", "g_post_mix", "g_pre_ffn", "g_post_ffn", "w_in", "attn_sink",
             "w_gate_fwd", "b_gate_fwd", "w_gate_bwd", "b_gate_bwd", "g_gla_norm", "w_out", "w_ffn_in", "w_ffn_out"]
    grads, deltas, new_m, new_v = [], [], [], []
    for nm in order:
        if nm in big:
            g_, d_, m_, v_ = big[nm]
        else:
            g_, d_, m_, v_ = grads_small[nm], d_s[nm], nm_s[nm], nv_s[nm]
        grads.append(g_)
        deltas.append(d_)
        new_m.append(m_)
        new_v.append(v_)
    return (loss, r["grad_x"][None], *grads, *deltas, *new_m, *new_v)
```

```python
import functools

import numpy as np
import jax
import jax.numpy as jnp
from jax import lax
from jax.experimental import pallas as pl
from jax.experimental.pallas import tpu as pltpu

F32 = jnp.float32
_BF = jnp.bfloat16

N_DEV = 8
D = 1024
CTX = 256
HD = 64
N_ATT = 8
N_KV = 2
GRP = N_ATT // N_KV
WIN = 128
GRID_W = 64
ROPE_BASE = 10000.0
N_GLA = 8
DK = 32
DV = 64
GATE_RANK = 16
GATE_TAU = 16.0
FFN = 2816
EPS = 1e-6
NEG = -1e30
GLA_T = 128

QP = N_ATT * HD
KP = N_KV * HD
O_Q, O_K, O_V = 0, QP, QP + KP
O_GQ = O_V + KP
O_GK = O_GQ + N_GLA * DK
O_GV = O_GK + N_GLA * DK
O_GG = O_GV + N_GLA * DV
O_Z = O_GG + N_GLA * DV
NP = O_Z + 128
IN_COLS = 2336

ADAM_LR, ADAM_B1, ADAM_B2, ADAM_EPS, ADAM_WD, ADAM_STEP = 0.001, 0.9, 0.999, 1e-08, 0.01, 10

VMEM_BIG = 56 * 1024 * 1024
MESH = pl.DeviceIdType.MESH


def _cp(sem, vmem=None):
    return pltpu.CompilerParams(dimension_semantics=sem, vmem_limit_bytes=vmem)


def _full(shape):
    nd = len(shape)
    return pl.BlockSpec(shape, lambda *a: (0,) * nd)


def _rows(tile, width, off=0):
    return pl.BlockSpec((tile, width), lambda i: (i + off, 0))


def _rows_lat(tile, width):
    return pl.BlockSpec((tile, width), lambda i: (jnp.maximum(i - 1, 0), 0))


def _nt(a, b):
    return lax.dot_general(a, b, (((1,), (1,)), ((), ())), preferred_element_type=F32)


def _tn(a, b):
    return lax.dot_general(a, b, (((0,), (0,)), ((), ())), preferred_element_type=F32)


def _nn(a, b):
    return jnp.dot(a, b, preferred_element_type=F32)


def _head_mean(x, mavg):
    n = x.shape[0]
    hi = x.astype(_BF)
    lo = (x - hi.astype(F32)).astype(_BF)
    y = _nn(jnp.concatenate([hi, lo], axis=0), mavg)
    return y[0:n] + y[n:2 * n]


def _rope(t, cos, sa, sb):
    n = t.shape[1]
    reps = n // 128
    c = jnp.tile(cos, (1, reps))
    a = jnp.tile(sa, (1, reps))
    b = jnp.tile(sb, (1, reps))
    return t * c + pltpu.roll(t, n - 16, 1) * a + pltpu.roll(t, 16, 1) * b


def _unrope(t, cos, sa, sb):
    n = t.shape[1]
    reps = n // 128
    c = jnp.tile(cos, (1, reps))
    a = jnp.tile(sa, (1, reps))
    b = jnp.tile(sb, (1, reps))
    return t * c + pltpu.roll(t * a, 16, 1) + pltpu.roll(t * b, n - 16, 1)


def _sigmoid(x):
    return 1.0 / (1.0 + jnp.exp(-x))


def _inproj_fwd(x, ctx, gml, shl, gmc, shc, win, wg, bg, cos, sa, sb, shards):
    E = x.shape[0] + CTX
    TE = CTX

    def body(x_ref, c_ref, gml_ref, shl_ref, gmc_ref, shc_ref, w_ref, wg_ref, bg_ref, cos_ref, sa_ref, sb_ref,
             h_ref, q_ref, k_ref, v_ref, gq_ref, gk_ref, gv_ref, gg_ref, z_ref, la_ref):
        is_ctx = pl.program_id(0) == 0
        gm = jnp.where(is_ctx, gmc_ref[...], gml_ref[...])
        sh = jnp.where(is_ctx, shc_ref[...], shl_ref[...])
        x = jnp.where(is_ctx, c_ref[...], x_ref[...])
        r = lax.rsqrt(jnp.mean(x * x, axis=-1, keepdims=True) + EPS)
        hb = ((x * r) * gm + sh).astype(_BF)
        h_ref[...] = hb
        p = _nt(hb, w_ref[...])
        cos_t, sa_t, sb_t = cos_ref[...], sa_ref[...], sb_ref[...]
        q_ref[...] = (_rope(p[:, O_Q:O_K], cos_t, sa_t, sb_t) * (HD ** -0.5)).astype(_BF)
        k_ref[...] = _rope(p[:, O_K:O_V], cos_t, sa_t, sb_t).astype(_BF)
        v_ref[...] = p[:, O_V:O_GQ].astype(_BF)
        gq_ref[...] = p[:, O_GQ:O_GK] * (DK ** -0.5)
        gk_ref[...] = p[:, O_GK:O_GV]
        gv_ref[...] = p[:, O_GV:O_GG]
        gg_ref[...] = p[:, O_GG:O_Z]
        zb = p[:, O_Z:NP].astype(_BF)
        z_ref[...] = zb
        lg = _nn(zb, wg_ref[...]) + bg_ref[...]
        la_ref[...] = (jnp.minimum(lg, 0.0) - jnp.log(1.0 + jnp.exp(-jnp.abs(lg)))) * (1.0 / GATE_TAU)

    vec = _full((1, D))
    tab = _rows(TE, 128)
    outs = [(D, _BF), (QP, _BF), (KP, _BF), (KP, _BF), (256, F32), (256, F32), (512, F32), (512, F32),
            (128, _BF), (512, F32)]
    return _hosted_call(
        body, (x, ctx, gml, shl, gmc, shc, win, wg, bg, cos, sa, sb), shards, True,
        name="inproj_fwd", grid=(E // TE,),
        in_specs=[_rows_lat(TE, D), _full((CTX, D)), vec, vec, vec, vec, _full((NP, D)), _full((128, 512)),
                  _full((1, 512)), tab, tab, tab],
        out_specs=[_rows_lat(TE, w) if i == 7 else _rows(TE, w) for i, (w, _) in enumerate(outs)],
        out_shape=[jax.ShapeDtypeStruct((E - CTX if i == 7 else E, w), dt) for i, (w, dt) in enumerate(outs)],
        compiler_params=_cp(("arbitrary",), 40 * 1024 * 1024))


def _xchg_scratch(na):
    return [pltpu.SemaphoreType.DMA((na, N_DEV - 1)), pltpu.SemaphoreType.DMA((na, N_DEV - 1)),
            pltpu.SemaphoreType.DMA((na,))]


def _xchg_copies(ins, outs, send_sems, recv_sems, local_sems, gather, cols=None):
    x, y, c = lax.axis_index("x"), lax.axis_index("y"), lax.axis_index("c")
    me = 4 * x + 2 * y + c

    def slab(a, j):
        if gather:
            return ins[a]
        if cols is None or cols[a] is None:
            return ins[a].at[j]
        return ins[a].at[j, :, pl.ds(cols[a][0], cols[a][1])]

    local, sends, recvs = [], [], []
    for a in range(len(ins)):
        local.append(pltpu.make_async_copy(slab(a, me), outs[a].at[me], local_sems.at[a]))
    for k in range(1, N_DEV):
        px, py, pc = x ^ (k >> 2), y ^ ((k >> 1) & 1), c ^ (k & 1)
        peer = 4 * px + 2 * py + pc
        for a in range(len(ins)):
            sems = dict(send_sem=send_sems.at[a, k - 1], recv_sem=recv_sems.at[a, k - 1], device_id_type=MESH)
            sends.append(pltpu.make_async_remote_copy(
                src_ref=slab(a, peer), dst_ref=outs[a].at[me], device_id=(px, py, pc), **sems))
            recvs.append(pltpu.make_async_remote_copy(
                src_ref=slab(a, me), dst_ref=outs[a].at[peer], device_id=(x, y, c), **sems))
    return local, sends, recvs


def _xchg_start(cps):
    local, sends, _ = cps
    for cp in local + sends:
        cp.start()


def _xchg_finish(cps):
    local, sends, recvs = cps
    for cp in recvs:
        cp.wait_recv()
    for cp in sends:
        cp.wait_send()
    for cp in local:
        cp.wait()


def _xchg_out_shapes(ins, gather, cols=None):
    out = []
    for a, s in enumerate(ins):
        shape = ((N_DEV,) + s.shape) if gather else s.shape
        if cols is not None and cols[a] is not None:
            shape = shape[:-1] + (cols[a][1],)
        out.append(jax.ShapeDtypeStruct(shape, s.dtype))
    return out


def _split_hosted(hosted):
    cols = [h[1] if isinstance(h, tuple) else None for h in hosted]
    return [h[0] if isinstance(h, tuple) else h for h in hosted], cols


def _hosted_call(body, args, hosted, gather, *, grid, in_specs, out_specs, out_shape, scratch_shapes=(), **kw):
    na = len(hosted)
    if na == 0:
        return pl.pallas_call(body, grid=grid, in_specs=in_specs, out_specs=out_specs, out_shape=out_shape,
                              scratch_shapes=list(scratch_shapes), **kw)(*args)
    hosted, cols = _split_hosted(hosted)
    n_in, n_out, n_scr = len(in_specs), len(out_specs), len(scratch_shapes)

    def wrapped(*refs):
        ins, h_in = refs[:n_in], refs[n_in:n_in + na]
        outs, h_out = refs[n_in + na:n_in + na + n_out], refs[n_in + na + n_out:n_in + 2 * na + n_out]
        scr = refs[n_in + 2 * na + n_out:]
        cps = _xchg_copies(h_in, h_out, *scr[n_scr:], gather=gather, cols=cols)
        pids = [pl.program_id(a) for a in range(len(grid))]
        first = functools.reduce(jnp.logical_and, [p == 0 for p in pids])
        last = functools.reduce(jnp.logical_and, [p == g - 1 for p, g in zip(pids, grid)])

        @pl.when(first)
        def _():
            _xchg_start(cps)

        body(*ins, *outs, *scr[:n_scr])

        @pl.when(last)
        def _():
            _xchg_finish(cps)

    anyspec = pl.BlockSpec(memory_space=pl.ANY)
    return pl.pallas_call(
        wrapped, grid=grid, in_specs=list(in_specs) + [anyspec] * na, out_specs=list(out_specs) + [anyspec] * na,
        out_shape=list(out_shape) + _xchg_out_shapes(hosted, gather, cols),
        scratch_shapes=list(scratch_shapes) + _xchg_scratch(na), **kw)(*args, *hosted)


ATT_BLOCKS = 2
PROB_ROWS = N_KV * GRP * WIN


def _attn_specs(E):
    nb = (E - CTX) // WIN
    last = E // WIN - 1
    kc = pl.BlockSpec((CTX, KP), lambda m: (0, 0))
    ks = [pl.BlockSpec((WIN, KP), functools.partial(lambda m, j: (jnp.minimum(ATT_BLOCKS * m + j, last), 0), j=j))
          for j in range(1, ATT_BLOCKS + 3)]
    return nb, [kc] + ks


def _attn_bias(nb):
    rows = np.arange(GRP * WIN)[:, None] % WIN
    cols = np.arange(CTX + 3 * WIN)[None, :]
    j = cols - CTX
    band = np.abs(j - WIN - rows) <= WIN
    out = []
    for first, last in ((True, False), (False, False), (False, True)):
        ok = (cols < CTX) | (band & ((j >= WIN) | (not first)) & ((j < 2 * WIN) | (not last)))
        out.append(np.where(ok, 0.0, NEG).astype(np.float32))
    bias = jnp.asarray(np.stack(out))
    steps = nb // ATT_BLOCKS
    shape = (1, GRP * WIN, CTX + 3 * WIN)
    specs = [pl.BlockSpec(shape, lambda m: (jnp.where(m == 0, 0, 1), 0, 0))]
    specs += [pl.BlockSpec(shape, lambda m: (1, 0, 0))] * (ATT_BLOCKS - 2)
    specs += [pl.BlockSpec(shape, lambda m: (jnp.where(m == steps - 1, 2, 1), 0, 0))]
    return bias, specs


def _both_halves(t, h):
    tf = t.astype(F32)
    r = pltpu.roll(tf, HD, 1)
    lo = lax.broadcasted_iota(jnp.int32, tf.shape, 1) < HD
    return (jnp.where(lo, tf, r) if h == 0 else jnp.where(lo, r, tf)).astype(t.dtype)


def _stack_heads(ref, h):
    lo = lax.broadcasted_iota(jnp.int32, (WIN, 128), 1) < HD
    parts = []
    for g in range(GRP):
        j = GRP * h + g
        t = ref[:, 128 * (j // 2):128 * (j // 2) + 128].astype(F32)
        parts.append(jnp.where(lo if j % 2 == 0 else jnp.logical_not(lo), t, 0.0))
    return jnp.concatenate(parts, axis=0)


def _unstack_pair(o, pp):
    lo = lax.broadcasted_iota(jnp.int32, (WIN, 128), 1) < HD
    return jnp.where(lo, o[WIN * 2 * pp:WIN * 2 * pp + WIN], o[WIN * (2 * pp + 1):WIN * (2 * pp + 1) + WIN])


def _attn_fwd(q, k, v, sink, shards):
    E = q.shape[0]
    S = E - CTX
    nb, kspecs = _attn_specs(E)
    na = len(shards)

    nk = ATT_BLOCKS + 3

    def one_block(q_ref, kw, vw, sink_ref, bias_ref, o_ref, lse_ref, p_ref):
        lane = lax.broadcasted_iota(jnp.int32, (WIN, 128), 1)
        lse_t = jnp.zeros((WIN, 128), F32)
        kall = jnp.concatenate([r[...] for r in kw], axis=0)
        vall = jnp.concatenate([r[...] for r in vw], axis=0)
        K = [_both_halves(kall, h) for h in range(N_KV)]
        Q = [_stack_heads(q_ref, h).astype(_BF) for h in range(N_KV)]
        sk = [jnp.concatenate([jnp.broadcast_to(sink_ref[GRP * h + g:GRP * h + g + 1, 0:1], (WIN, 1))
                               for g in range(GRP)], axis=0) for h in range(N_KV)]
        s = [_nt(Q[h], K[h]) + bias_ref[0] for h in range(N_KV)]
        m = [jnp.maximum(jnp.max(s[h], axis=1, keepdims=True), sk[h]) for h in range(N_KV)]
        e = [jnp.exp(s[h] - m[h]) for h in range(N_KV)]
        den = [jnp.sum(e[h], axis=1, keepdims=True) + jnp.exp(sk[h] - m[h]) for h in range(N_KV)]
        V = [_both_halves(vall, h) for h in range(N_KV)]
        pb = [(e[h] * (1.0 / den[h])).astype(_BF) for h in range(N_KV)]
        for h in range(N_KV):
            p_ref[GRP * WIN * h:GRP * WIN * (h + 1), :] = pb[h]
        o = [_nn(pb[h], V[h]) for h in range(N_KV)]
        for h in range(N_KV):
            lse = m[h] + jnp.log(den[h])
            for g in range(GRP):
                lse_t = jnp.where(lane == GRP * h + g, lse[WIN * g:WIN * g + WIN], lse_t)
            for pp in range(GRP // 2):
                t = 2 * h + pp
                o_ref[:, 128 * t:128 * t + 128] = _unstack_pair(o[h], pp).astype(_BF)
        lse_ref[...] = lse_t

    def body(q_ref, *rest):
        kr, vr, sink_ref = rest[0:nk], rest[nk:2 * nk], rest[2 * nk]
        bias_refs = rest[2 * nk + 1:2 * nk + 1 + ATT_BLOCKS]
        rest = rest[2 * nk + 1 + ATT_BLOCKS:]
        shard_refs, (o_ref, lse_ref, p_ref), got_refs = rest[:na], rest[na:na + 3], rest[na + 3:2 * na + 3]
        cps = _xchg_copies(shard_refs, got_refs, *rest[2 * na + 3:], gather=True)

        @pl.when(pl.program_id(0) == 0)
        def _():
            _xchg_start(cps)

        for j in range(ATT_BLOCKS):
            rows = pl.ds(WIN * j, WIN)
            one_block(q_ref.at[rows], [kr[0]] + list(kr[1 + j:4 + j]), [vr[0]] + list(vr[1 + j:4 + j]), sink_ref,
                      bias_refs[j], o_ref.at[rows], lse_ref.at[rows], p_ref.at[pl.ds(PROB_ROWS * j, PROB_ROWS)])

        @pl.when(pl.program_id(0) == nb // ATT_BLOCKS - 1)
        def _():
            _xchg_finish(cps)

    tq = ATT_BLOCKS * WIN
    qs = pl.BlockSpec((tq, QP), lambda m: (m + CTX // tq, 0))
    anyspec = pl.BlockSpec(memory_space=pl.ANY)
    bias, bias_specs = _attn_bias(nb)
    return pl.pallas_call(
        body, name="attn_fwd", grid=(nb // ATT_BLOCKS,),
        in_specs=[qs] + kspecs + kspecs + [_full((8, 128))] + bias_specs + [anyspec] * na,
        out_specs=[_rows(tq, 512), _rows(tq, 128), _rows(ATT_BLOCKS * PROB_ROWS, CTX + 3 * WIN)] + [anyspec] * na,
        out_shape=[jax.ShapeDtypeStruct((S, 512), _BF), jax.ShapeDtypeStruct((S, 128), F32),
                   jax.ShapeDtypeStruct((nb * PROB_ROWS, CTX + 3 * WIN), _BF)]
        + _xchg_out_shapes(shards, True),
        scratch_shapes=_xchg_scratch(na),
        compiler_params=_cp(("arbitrary",), 48 * 1024 * 1024),
    )(q, *([k] * nk), *([v] * nk), sink, *([bias] * ATT_BLOCKS), *shards)


def _attn_bwd(q, k, v, sink, probs, lse, d_attn, slabs):
    E = q.shape[0]
    S = E - CTX
    nb, kspecs = _attn_specs(E)
    last = E // WIN - 1
    na = len(slabs)
    slabs, cols = _split_hosted(slabs)
    nk = ATT_BLOCKS + 3

    def one_block(n, q_ref, kw, vw, sink_ref, p_ref, lse_ref, do_ref, dq_ref, dk_ref, dv_ref, ds_ref):
        lane = lax.broadcasted_iota(jnp.int32, (WIN, 128), 1)
        lse_t = lse_ref[...]
        starts = [None, pl.multiple_of((n + 1) * WIN, WIN), pl.multiple_of((n + 2) * WIN, WIN),
                  pl.multiple_of(jnp.minimum(n + 3, last) * WIN, WIN)]
        kall = jnp.concatenate([r[...] for r in kw], axis=0)
        vall = jnp.concatenate([r[...] for r in vw], axis=0)
        for h in range(N_KV):
            hs = slice(HD * h, HD * h + HD)
            K = _both_halves(kall, h)
            V = _both_halves(vall, h)
            Q = _stack_heads(q_ref, h).astype(_BF)
            sk = jnp.concatenate([jnp.broadcast_to(sink_ref[GRP * h + g:GRP * h + g + 1, 0:1], (WIN, 1))
                                  for g in range(GRP)], axis=0)
            ls = jnp.concatenate([jnp.sum(jnp.where(lane == GRP * h + g, lse_t, 0.0), axis=1, keepdims=True)
                                  for g in range(GRP)], axis=0)
            do = _stack_heads(do_ref, h).astype(_BF)
            pb = p_ref[GRP * WIN * h:GRP * WIN * (h + 1), :]
            p = pb.astype(F32)
            dp = _nt(do, V)
            delta = jnp.sum(p * dp, axis=1, keepdims=True)
            dsc = (p * (dp - delta)).astype(_BF)
            dq = _nn(dsc, K) * (HD ** -0.5)
            for pp in range(GRP // 2):
                t = 2 * h + pp
                dq_ref[:, 128 * t:128 * t + 128] = _unstack_pair(dq, pp).astype(_BF)
            dK2 = _tn(Q, dsc)
            dV2 = _tn(do, pb)
            dK = dK2[0:HD] + dK2[HD:2 * HD]
            dV = dV2[0:HD] + dV2[HD:2 * HD]
            dk_ref[hs, 0:CTX] += dK[:, 0:CTX]
            dv_ref[hs, 0:CTX] += dV[:, 0:CTX]
            for w in range(1, 4):
                lo = CTX + WIN * (w - 1)
                dk_ref[hs, pl.ds(starts[w], WIN)] += dK[:, lo:lo + WIN]
                dv_ref[hs, pl.ds(starts[w], WIN)] += dV[:, lo:lo + WIN]
            psk = -jnp.exp(sk - ls) * delta
            for g in range(GRP):
                j = GRP * h + g
                tot = jnp.sum(psk[WIN * g:WIN * g + WIN], axis=0, keepdims=True)
                ds_ref[j:j + 1, :] += jnp.broadcast_to(tot, (1, 128))

    def body(q_ref, *rest):
        kr, vr = rest[0:nk], rest[nk:2 * nk]
        sink_ref, p_ref, lse_ref, do_ref = rest[2 * nk:2 * nk + 4]
        rest = rest[2 * nk + 4:]
        slab_refs, (dq_ref, dk_ref, dv_ref, ds_ref), got_refs = rest[:na], rest[na:na + 4], rest[na + 4:2 * na + 4]
        m = pl.program_id(0)
        cps = _xchg_copies(slab_refs, got_refs, *rest[2 * na + 4:], gather=False, cols=cols)

        @pl.when(m == 0)
        def _():
            _xchg_start(cps)
            dk_ref[...] = jnp.zeros_like(dk_ref)
            dv_ref[...] = jnp.zeros_like(dv_ref)
            ds_ref[...] = jnp.zeros_like(ds_ref)

        for j in range(ATT_BLOCKS):
            rows = pl.ds(WIN * j, WIN)
            one_block(ATT_BLOCKS * m + j, q_ref.at[rows], [kr[0]] + list(kr[1 + j:4 + j]),
                      [vr[0]] + list(vr[1 + j:4 + j]), sink_ref, p_ref.at[pl.ds(PROB_ROWS * j, PROB_ROWS)],
                      lse_ref.at[rows], do_ref.at[rows], dq_ref.at[rows], dk_ref, dv_ref, ds_ref)

        @pl.when(m == nb // ATT_BLOCKS - 1)
        def _():
            _xchg_finish(cps)

    tq = ATT_BLOCKS * WIN
    qs = pl.BlockSpec((tq, QP), lambda m: (m + CTX // tq, 0))
    anyspec = pl.BlockSpec(memory_space=pl.ANY)
    return pl.pallas_call(
        body, name="attn_bwd", grid=(nb // ATT_BLOCKS,),
        in_specs=[qs] + kspecs + kspecs + [_full((8, 128)), _rows(ATT_BLOCKS * PROB_ROWS, CTX + 3 * WIN),
                                            _rows(tq, 128), _rows(tq, 512)] + [anyspec] * na,
        out_specs=[_rows(tq, QP), _full((KP, E)), _full((KP, E)), _full((8, 128))] + [anyspec] * na,
        out_shape=[jax.ShapeDtypeStruct((S, QP), _BF), jax.ShapeDtypeStruct((KP, E), F32),
                   jax.ShapeDtypeStruct((KP, E), F32), jax.ShapeDtypeStruct((8, 128), F32)]
        + _xchg_out_shapes(slabs, False, cols),
        scratch_shapes=_xchg_scratch(na),
        compiler_params=_cp(("arbitrary",), 48 * 1024 * 1024),
    )(q, *([k] * nk), *([v] * nk), sink, probs, lse, d_attn, *slabs)


GLA_STEP = 2


def _gla_subs(reverse, backward):
    subs = list(range(GLA_STEP))
    return subs[::-1] if reverse != backward else subs


def _gla_order(E, reverse, backward):
    nc = CTX // (GLA_STEP * GLA_T)
    n = E // (GLA_STEP * GLA_T)
    if not reverse:
        fwd = lambda s: s
    else:
        fwd = lambda s: jnp.where(s < nc, nc - 1 - s, n - 1 + nc - s)
    if backward:
        return lambda s: fwd(n - 1 - s)
    return fwd


def _gla_masks():
    T = GLA_T
    l128 = lax.broadcasted_iota(jnp.int32, (1, 128), 1)
    qmask = [((l128 >> 5) == j).astype(F32) for j in range(4)]
    vmask = [((l128 >> 6) == j).astype(F32) for j in range(2)]
    bd = ((lax.broadcasted_iota(jnp.int32, (512, 256), 0) >> 6)
          == (lax.broadcasted_iota(jnp.int32, (512, 256), 1) >> 5)).astype(F32)
    ri = lax.broadcasted_iota(jnp.int32, (T, 2 * T), 0)
    ci = lax.broadcasted_iota(jnp.int32, (T, 2 * T), 1) & (T - 1)
    return qmask, vmask, bd, ri, ci


def _tri_sum(tri, x):
    hi = x.astype(_BF)
    lo = (x - hi.astype(F32)).astype(_BF)
    n = x.shape[1]
    y = _nn(tri.astype(_BF), jnp.concatenate([hi, lo], axis=1))
    return y[:, 0:n] + y[:, n:2 * n]


def _gla_decays(la, reverse, ri, ci):
    T = GLA_T
    msk2 = (ri <= ci) if reverse else (ri >= ci)
    mskT2 = (ri >= ci) if reverse else (ri <= ci)
    b = _tri_sum(msk2[:, 0:T], la)
    bT = b[0:1] if reverse else b[T - 1:T]
    bm = b[T // 2:T // 2 + 1]
    return msk2, mskT2, b, bT, bm


def _pair_stack(tile, m0, m1):
    return jnp.concatenate([(tile * m0).astype(_BF), (tile * m1).astype(_BF)], axis=0)


def _gla_fwd(gq, gk, gv, la, shards=()):
    E = gq.shape[0]
    T = GLA_T
    n = E // T
    TB = GLA_STEP * T
    orders = [_gla_order(E, False, False), _gla_order(E, True, False)]

    def one_direction(reverse, *refs):
        for j in _gla_subs(reverse, False):
            rows = pl.ds(T * j, T)
            one_chunk(reverse, *[r.at[rows] for r in refs[:5]], refs[5].at[j], refs[6])

    def one_chunk(reverse, gq_ref, gk_ref, gv_ref, la_ref, o_ref, st_ref, S_scr):
        qmask, vmask, bd, ri, ci = _gla_masks()
        msk2, _, b, bT, bm = _gla_decays(la_ref[...], reverse, ri, ci)
        q, k, v = gq_ref[...], gk_ref[...], gv_ref[...]
        qd = (q * jnp.exp(b)).astype(_BF)
        qm = (q * jnp.exp(b - bm)).astype(_BF)
        km = k * jnp.exp(bm - b)
        kd = (k * jnp.exp(bT - b)).astype(_BF)
        ST = S_scr[...]
        comp = ST[0:DV]
        for h in range(1, N_GLA):
            comp = comp + ST[DV * h:DV * h + DV]
        st_ref[...] = comp
        inter = _nt(qd, ST.astype(_BF))
        tiles = []
        for p in range(N_GLA // 2):
            qs = slice(128 * (p // 2), 128 * (p // 2) + 128)
            vs = slice(128 * p, 128 * p + 128)
            j0 = (2 * p) % 4
            KS = _pair_stack(km[:, qs], qmask[j0], qmask[j0 + 1])
            VS = _pair_stack(v[:, vs], vmask[0], vmask[1])
            AA = jnp.where(msk2, _nt(qm[:, qs], KS), 0.0).astype(_BF)
            tiles.append(_nn(AA, VS))
        o_ref[...] = inter + jnp.concatenate(tiles, axis=1)
        S_scr[...] = ST * jnp.exp(bT) + bd * _tn(v.astype(_BF), kd)

    def body(qf, kf, vf, lf, qr, kr, vr, lr, of, sf, orr, sr, S_f, S_r):
        @pl.when(pl.program_id(0) == 0)
        def _():
            S_f[...] = jnp.zeros_like(S_f)
            S_r[...] = jnp.zeros_like(S_r)

        one_direction(False, qf, kf, vf, lf, of, sf, S_f)
        one_direction(True, qr, kr, vr, lr, orr, sr, S_r)

    def blk(d, w, c=0):
        return pl.BlockSpec((TB, w), lambda s: (orders[d](s), c))

    def st_spec(d):
        return pl.BlockSpec((GLA_STEP, DV, 256), lambda s: (orders[d](s), 0, 0))

    def o_spec(d):
        return pl.BlockSpec((TB, 512), lambda s: (orders[d](jnp.maximum(s, CTX // TB)) - CTX // TB, 0))

    return _hosted_call(
        body, (gq, gk, gv, la, gq, gk, gv, la), shards, True, name="gla_fwd", grid=(n // GLA_STEP,),
        in_specs=[blk(0, 256), blk(0, 256), blk(0, 512), blk(0, 256, 0), blk(1, 256), blk(1, 256), blk(1, 512),
                  blk(1, 256, 1)],
        out_specs=[o_spec(0), st_spec(0), o_spec(1), st_spec(1)],
        out_shape=[jax.ShapeDtypeStruct((E - CTX, 512), F32), jax.ShapeDtypeStruct((n, DV, 256), F32)] * 2,
        scratch_shapes=[pltpu.VMEM((512, 256), F32)] * 2,
        compiler_params=_cp(("arbitrary",)))


def _gla_bwd(gq, gk, gv, la, st_f, st_r, do, slabs=()):
    E = gq.shape[0]
    T = GLA_T
    n = E // T
    TB = GLA_STEP * T
    nc = CTX // TB
    orders = [_gla_order(E, False, True), _gla_order(E, True, True)]

    def one_direction(reverse, *refs):
        ins, outs, shared = refs[0:6], refs[6:10], refs[10:12]
        for j in _gla_subs(reverse, True):
            rows = pl.ds(T * j, T)
            views = [r.at[rows] for r in ins[0:4]] + [ins[4].at[j], ins[5].at[rows]] + [r.at[rows] for r in outs]
            one_chunk(reverse, *views, *shared)

    def one_chunk(reverse, gq_ref, gk_ref, gv_ref, la_ref, st_ref, do_ref,
                  dq_ref, dk_ref, dv_ref, dlg_ref, bsum_ref, dS_scr):
        cols = slice(256, 512) if reverse else slice(0, 256)
        is_lat = orders[1 if reverse else 0](pl.program_id(0)) >= nc
        qmask, vmask, bd, ri, ci = _gla_masks()
        msk2, mskT2, b, bT, bm = _gla_decays(la_ref[...], reverse, ri, ci)
        q, k, v = gq_ref[...], gk_ref[...], gv_ref[...]
        do = jnp.where(is_lat, do_ref[...].astype(F32), 0.0)
        e_b, e_qm, e_km, e_kd, e_T = jnp.exp(b), jnp.exp(b - bm), jnp.exp(bm - b), jnp.exp(bT - b), jnp.exp(bT)
        qd, qm, km, kd = q * e_b, q * e_qm, k * e_km, k * e_kd
        qdb, qmb, kmb, kdb, vb, dob = (t.astype(_BF) for t in (qd, qm, km, kd, v, do))
        ST = jnp.tile(st_ref[...], (N_GLA, 1)) * bd
        dST = dS_scr[...]
        dSTb = dST.astype(_BF)
        dqd = _nn(dob, ST.astype(_BF))
        dkd = _nn(vb, dSTb)
        dv_t, dqm_t, dkm_t = [], [None, None], [None, None]
        for p in range(N_GLA // 2):
            t = p // 2
            qs = slice(128 * t, 128 * t + 128)
            vs = slice(128 * p, 128 * p + 128)
            j0 = (2 * p) % 4
            QS = _pair_stack(qm[:, qs], qmask[j0], qmask[j0 + 1])
            KS = _pair_stack(km[:, qs], qmask[j0], qmask[j0 + 1])
            VS = _pair_stack(v[:, vs], vmask[0], vmask[1])
            DS = _pair_stack(do[:, vs], vmask[0], vmask[1])
            ATT = jnp.where(mskT2, _nt(kmb[:, qs], QS), 0.0).astype(_BF)
            dAA = jnp.where(msk2, _nt(dob[:, vs], VS), 0.0).astype(_BF)
            dATT = jnp.where(mskT2, _nt(vb[:, vs], DS), 0.0).astype(_BF)
            dv_t.append(_nn(ATT, DS))
            dq_p = _nn(dAA, KS)
            dk_p = _nn(dATT, QS)
            dqm_t[t] = dq_p if dqm_t[t] is None else dqm_t[t] + dq_p
            dkm_t[t] = dk_p if dkm_t[t] is None else dkm_t[t] + dk_p
        dqm = jnp.concatenate(dqm_t, axis=1)
        dkm = jnp.concatenate(dkm_t, axis=1)
        dq = dqm * e_qm + dqd * e_b
        dk = dkm * e_km + dkd * e_kd
        dv = _nt(kdb, dSTb) + jnp.concatenate(dv_t, axis=1)
        dq_ref[...] = (dq * (DK ** -0.5)).astype(_BF)
        dk_ref[...] = dk.astype(_BF)
        dv_ref[...] = dv.astype(_BF)
        db = dqm * qm - dkm * km + dqd * qd - dkd * kd
        dbT = jnp.sum(dkd * kd, axis=0, keepdims=True) + e_T * jnp.sum(dST * ST, axis=0, keepdims=True)
        dla = _tri_sum(mskT2[:, 0:T], db) + dbT
        dlg = dla * (1.0 - jnp.exp(GATE_TAU * la_ref[...])) * (1.0 / GATE_TAU)
        bsum_ref[0:1, cols] += jnp.sum(dlg, axis=0, keepdims=True)
        dlg_ref[...] = dlg.astype(_BF)
        dS_scr[...] = dST * e_T + bd * _tn(dob, qdb)

    def body(*refs):
        ins_f, ins_r = refs[0:6], refs[6:12]
        outs_f, outs_r = refs[12:16], refs[16:20]
        bsum_ref, dS_f, dS_r = refs[20:23]

        @pl.when(pl.program_id(0) == 0)
        def _():
            dS_f[...] = jnp.zeros_like(dS_f)
            dS_r[...] = jnp.zeros_like(dS_r)
            bsum_ref[...] = jnp.zeros_like(bsum_ref)

        one_direction(False, *ins_f, *outs_f, bsum_ref, dS_f)
        one_direction(True, *ins_r, *outs_r, bsum_ref, dS_r)

    def specs(d, st):
        order = orders[d]
        blk = lambda w, c=0: pl.BlockSpec((TB, w), lambda s: (order(s), c))
        ins = [blk(256), blk(256), blk(512), blk(256, d),
               pl.BlockSpec((GLA_STEP, DV, 256), lambda s: (order(s), 0, 0)),
               pl.BlockSpec((TB, 512), lambda s: (jnp.maximum(order(s) - nc, 0), 0))]
        return ins, [blk(256), blk(256), blk(512), blk(256)], (gq, gk, gv, la, st, do)

    in_f, out_f, args_f = specs(0, st_f)
    in_r, out_r, args_r = specs(1, st_r)
    dir_shapes = [jax.ShapeDtypeStruct((E, 256), _BF), jax.ShapeDtypeStruct((E, 256), _BF),
                  jax.ShapeDtypeStruct((E, 512), _BF), jax.ShapeDtypeStruct((E, 256), _BF)]
    return _hosted_call(
        body, args_f + args_r, slabs, False, name="gla_bwd", grid=(n // GLA_STEP,),
        in_specs=in_f + in_r, out_specs=out_f + out_r + [_full((8, 512))],
        out_shape=dir_shapes * 2 + [jax.ShapeDtypeStruct((8, 512), F32)],
        scratch_shapes=[pltpu.VMEM((512, 256), F32)] * 2,
        compiler_params=_cp(("arbitrary",)))


def _gla_out(o, gg, ggla, mavg):
    rr = lax.rsqrt(_head_mean(o * o, mavg) + EPS)
    oh = o * rr
    sg = _sigmoid(gg)
    return oh, rr, sg


def _mix_fwd(x, attn, o_f, o_b, gg, ggla, mavg, wout, gt1, g2):
    S = x.shape[0]
    TM = 512 if S % 512 == 0 else 256

    def body(x_ref, a_ref, of_ref, ob_ref, gg_ref, ggla_ref, mavg_ref, w_ref, gt1_ref, g2_ref, x1_ref, mix_ref):
        gg_t = gg_ref[...]
        oh, _, sg = _gla_out(of_ref[...] + ob_ref[...], gg_t, ggla_ref[...], mavg_ref[...])
        mix_ref[:, 0:512] = a_ref[...]
        mix_ref[:, 512:1024] = (oh * ggla_ref[...] * (gg_t * sg)).astype(_BF)
        y = _nn(mix_ref[...], w_ref[...])
        ry = lax.rsqrt(jnp.mean(y * y, axis=-1, keepdims=True) + EPS)
        x1_ref[...] = x_ref[...] + gt1_ref[...] * ((y * ry) * g2_ref[...])

    return pl.pallas_call(
        body, name="mix_fwd", grid=(S // TM,),
        in_specs=[_rows(TM, D), _rows(TM, 512), _rows(TM, 512), _rows(TM, 512), _rows(TM, 512),
                  _full((1, 512)), _full((512, 512)), _full((D, D)), _full((1, D)), _full((1, D))],
        out_specs=[_rows(TM, D), _rows(TM, D)],
        out_shape=[jax.ShapeDtypeStruct((S, D), F32), jax.ShapeDtypeStruct((S, D), _BF)],
        compiler_params=_cp(("arbitrary",), 48 * 1024 * 1024),
    )(x, attn, o_f, o_b, gg, ggla, mavg, wout, gt1, g2)


def _mix_bwd(dx1, mix, o_f, o_b, gg, ggla, mavg, wout, gt1, g2, slabs):
    S = dx1.shape[0]
    TM = 512 if S % 512 == 0 else 256

    def body(dx_ref, mix_ref, of_ref, ob_ref, gg_ref, ggla_ref, mavg_ref, w_ref, gt1_ref, g2_ref,
             da_ref, do_ref, dgg_ref, dy_ref, sums_ref):
        @pl.when(pl.program_id(0) == 0)
        def _():
            sums_ref[...] = jnp.zeros_like(sums_ref)

        dx = dx_ref[...]
        y = _nn(mix_ref[...], w_ref[...])
        ry = lax.rsqrt(jnp.mean(y * y, axis=-1, keepdims=True) + EPS)
        yh = y * ry
        sums_ref[0:1, :] += jnp.sum(dx * yh, axis=0, keepdims=True)
        dyh = dx * (gt1_ref[...] * g2_ref[...])
        dy = (ry * (dyh - yh * jnp.mean(dyh * yh, axis=-1, keepdims=True))).astype(_BF)
        dy_ref[...] = dy
        dmix = _nt(dy, w_ref[...])
        da_ref[...] = dmix[:, 0:512].astype(_BF)
        dgla = dmix[:, 512:1024]
        gg_t = gg_ref[...]
        ggla_t = ggla_ref[...]
        oh, rr, sg = _gla_out(of_ref[...] + ob_ref[...], gg_t, ggla_t, mavg_ref[...])
        dgg_ref[...] = (dgla * oh * ggla_t * (sg * (1.0 + gg_t * (1.0 - sg)))).astype(_BF)
        don = dgla * (gg_t * sg)
        sums_ref[1:2, 0:512] += jnp.sum(don * oh, axis=0, keepdims=True)
        doh = don * ggla_t
        do_ref[...] = (rr * (doh - oh * _head_mean(doh * oh, mavg_ref[...]))).astype(_BF)

    return _hosted_call(
        body, (dx1, mix, o_f, o_b, gg, ggla, mavg, wout, gt1, g2), slabs, False,
        name="mix_bwd", grid=(S // TM,),
        in_specs=[_rows(TM, D), _rows(TM, D), _rows(TM, 512), _rows(TM, 512), _rows(TM, 512),
                  _full((1, 512)), _full((512, 512)), _full((D, D)), _full((1, D)), _full((1, D))],
        out_specs=[_rows(TM, 512), _rows(TM, 512), _rows(TM, 512), _rows(TM, D), _full((8, D))],
        out_shape=[jax.ShapeDtypeStruct((S, 512), _BF), jax.ShapeDtypeStruct((S, 512), _BF),
                   jax.ShapeDtypeStruct((S, 512), _BF), jax.ShapeDtypeStruct((S, D), _BF),
                   jax.ShapeDtypeStruct((8, D), F32)],
        compiler_params=_cp(("arbitrary",), 48 * 1024 * 1024))


def _ffn(x1, target, gm2, sh2, gt2, g4, wffi, wffo):
    S = x1.shape[0]
    TF = 256

    def body(x_ref, t_ref, gm_ref, sh_ref, gt_ref, g4_ref, wi_hbm, wo_hbm,
             dx_ref, h_ref, du_ref, act_ref, df_ref, sums_ref, loss_ref, wi, wo, sem):
        @pl.when(pl.program_id(0) == 0)
        def _():
            c1 = pltpu.make_async_copy(wi_hbm, wi, sem.at[0])
            c2 = pltpu.make_async_copy(wo_hbm, wo, sem.at[1])
            c1.start()
            c2.start()
            sums_ref[...] = jnp.zeros_like(sums_ref)
            loss_ref[...] = jnp.zeros_like(loss_ref)
            c1.wait()
            c2.wait()

        x = x_ref[...]
        gm = gm_ref[...]
        r = lax.rsqrt(jnp.mean(x * x, axis=-1, keepdims=True) + EPS)
        xh = x * r
        hb = (xh * gm + sh_ref[...]).astype(_BF)
        h_ref[...] = hb
        u = _nt(hb, wi[...])
        g = u[:, 0:FFN]
        up = u[:, FFN:2 * FFN]
        sg = _sigmoid(g)
        sl = g * sg
        ab = (sl * up).astype(_BF)
        act_ref[...] = ab
        f = _nn(ab, wo[...])
        rf = lax.rsqrt(jnp.mean(f * f, axis=-1, keepdims=True) + EPS)
        fh = f * rf
        gt, g4v = gt_ref[...], g4_ref[...]
        err = x + gt * (fh * g4v) - t_ref[...]
        loss_ref[...] += jnp.sum(err * err) * (0.5 / D)
        dout = err * (1.0 / D)
        sums_ref[2:3, :] += jnp.sum(dout * fh, axis=0, keepdims=True)
        dfh = dout * (gt * g4v)
        dfb = (rf * (dfh - fh * jnp.mean(dfh * fh, axis=-1, keepdims=True))).astype(_BF)
        df_ref[...] = dfb
        dact = _nt(dfb, wo[...])
        du_ref[:, 0:FFN] = (dact * up * (sg * (1.0 + g * (1.0 - sg)))).astype(_BF)
        du_ref[:, FFN:2 * FFN] = (dact * sl).astype(_BF)
        dh = _nn(du_ref[...], wi[...])
        sums_ref[0:1, :] += jnp.sum(dh, axis=0, keepdims=True)
        sums_ref[1:2, :] += jnp.sum(dh * xh, axis=0, keepdims=True)
        dxh = dh * gm
        dx_ref[...] = dout + r * (dxh - xh * jnp.mean(dxh * xh, axis=-1, keepdims=True))

    vec = _full((1, D))
    anyspec = pl.BlockSpec(memory_space=pl.ANY)
    return pl.pallas_call(
        body, name="ffn_fwd_bwd", grid=(S // TF,),
        in_specs=[_rows(TF, D), _rows(TF, D), vec, vec, vec, vec, anyspec, anyspec],
        out_specs=[_rows(TF, D), _rows(TF, D), _rows(TF, 2 * FFN), _rows(TF, FFN), _rows(TF, D),
                   _full((8, D)), _full((8, 128))],
        out_shape=[jax.ShapeDtypeStruct((S, D), F32), jax.ShapeDtypeStruct((S, D), _BF),
                   jax.ShapeDtypeStruct((S, 2 * FFN), _BF), jax.ShapeDtypeStruct((S, FFN), _BF),
                   jax.ShapeDtypeStruct((S, D), _BF), jax.ShapeDtypeStruct((8, D), F32),
                   jax.ShapeDtypeStruct((8, 128), F32)],
        scratch_shapes=[pltpu.VMEM((2 * FFN, D), _BF), pltpu.VMEM((FFN, D), _BF), pltpu.SemaphoreType.DMA((2,))],
        compiler_params=_cp(("arbitrary",), VMEM_BIG),
    )(x1, target, gm2, sh2, gt2, g4, wffi, wffo)


def _inproj_bwd(x, ctx, gml, gmc, win, wg, cos, sa, sb, dq, dk, dv, dgq, dgk, dgv, dgg, dlg_f, dlg_b, dx1, z):
    S = x.shape[0]
    E = S + CTX
    TE = CTX

    def body(x_ref, c_ref, gml_ref, gmc_ref, w_ref, wg_ref, cos_ref, sa_ref, sb_ref, dq_ref, dk_ref, dv_ref,
             gqf, gqb, gkf, gkb, gvf, gvb, dgg_ref, dlf, dlb, dx1_ref, z_ref, dp_ref, gx_ref, sums_ref, gwg_ref):
        i = pl.program_id(0)
        is_ctx = i == 0

        @pl.when(is_ctx)
        def _():
            sums_ref[...] = jnp.zeros_like(sums_ref)
            gwg_ref[...] = jnp.zeros_like(gwg_ref)

        lat = jnp.where(is_ctx, 0.0, 1.0)
        cos_t, sa_t, sb_t = cos_ref[...], sa_ref[...], sb_ref[...]
        dp_ref[:, O_Q:O_K] = (_unrope(dq_ref[...].astype(F32), cos_t, sa_t, sb_t) * lat).astype(_BF)
        dp_ref[:, O_K:O_V] = _unrope(dk_ref[...].T, cos_t, sa_t, sb_t).astype(_BF)
        dp_ref[:, O_V:O_GQ] = dv_ref[...].T.astype(_BF)
        dp_ref[:, O_GQ:O_GK] = (gqf[...].astype(F32) + gqb[...].astype(F32)).astype(_BF)
        dp_ref[:, O_GK:O_GV] = (gkf[...].astype(F32) + gkb[...].astype(F32)).astype(_BF)
        dp_ref[:, O_GV:O_GG] = (gvf[...].astype(F32) + gvb[...].astype(F32)).astype(_BF)
        dp_ref[:, O_GG:O_Z] = (dgg_ref[...].astype(F32) * lat).astype(_BF)
        dlg = jnp.concatenate([dlf[...], dlb[...]], axis=1)
        gwg_ref[...] += _tn(z_ref[...], dlg)
        dp_ref[:, O_Z:NP] = _nt(dlg, wg_ref[...]).astype(_BF)
        dh = _nn(dp_ref[...], w_ref[...])
        x = jnp.where(is_ctx, c_ref[...], x_ref[...])
        r = lax.rsqrt(jnp.mean(x * x, axis=-1, keepdims=True) + EPS)
        xh = x * r
        sdh = jnp.sum(dh, axis=0, keepdims=True)
        sdx = jnp.sum(dh * xh, axis=0, keepdims=True)
        sums_ref[0:1, :] += sdh * lat
        sums_ref[1:2, :] += sdx * lat
        sums_ref[2:3, :] += sdh * (1.0 - lat)
        sums_ref[3:4, :] += sdx * (1.0 - lat)
        dxh = dh * jnp.where(is_ctx, gmc_ref[...], gml_ref[...])
        gx_ref[...] = dx1_ref[...] + r * (dxh - xh * jnp.mean(dxh * xh, axis=-1, keepdims=True))

    vec = _full((1, D))
    tab = _rows(TE, 128)
    return pl.pallas_call(
        body, name="inproj_bwd", grid=(E // TE,),
        in_specs=[_rows_lat(TE, D), _full((CTX, D)), vec, vec, _full((NP, D)), _full((128, 512)), tab, tab, tab,
                  _rows_lat(TE, QP), pl.BlockSpec((KP, TE), lambda i: (0, i)), pl.BlockSpec((KP, TE), lambda i: (0, i)),
                  _rows(TE, 256), _rows(TE, 256), _rows(TE, 256), _rows(TE, 256), _rows(TE, 512), _rows(TE, 512),
                  _rows_lat(TE, 512), _rows(TE, 256), _rows(TE, 256), _rows_lat(TE, D), _rows(TE, 128)],
        out_specs=[_rows(TE, NP), _rows_lat(TE, D), _full((8, D)), _full((128, 512))],
        out_shape=[jax.ShapeDtypeStruct((E, NP), _BF), jax.ShapeDtypeStruct((S, D), F32),
                   jax.ShapeDtypeStruct((8, D), F32), jax.ShapeDtypeStruct((128, 512), F32)],
        compiler_params=_cp(("arbitrary",), VMEM_BIG),
    )(x, ctx, gml, gmc, win, wg, cos, sa, sb, dq, dk, dv, *dgq, *dgk, *dgv, dgg, dlg_f, dlg_b, dx1, z)


def _matmul_tn(a, b, tk, tt, name, out_dtype, transpose_out=False, a_cols=None, hosted=(), gather=True):
    T, KA = a.shape
    N = b.shape[1]
    nt = T // tt
    k0 = 0
    if a_cols is not None:
        KA, k0 = tk, a_cols

    def body(a_ref, b_ref, o_ref, acc):
        t = pl.program_id(1)

        @pl.when(t == 0)
        def _():
            acc[...] = jnp.zeros_like(acc)

        acc[...] += _tn(a_ref[...], b_ref[...])

        @pl.when(t == nt - 1)
        def _():
            o_ref[...] = (acc[...].T if transpose_out else acc[...]).astype(out_dtype)

    if transpose_out:
        out_spec, out_shape = pl.BlockSpec((N, tk), lambda i, t: (0, i)), (N, KA)
    else:
        out_spec, out_shape = pl.BlockSpec((tk, N), lambda i, t: (i, 0)), (KA, N)
    res = _hosted_call(
        body, (a, b), hosted, gather, name=name, grid=(KA // tk, nt),
        in_specs=[pl.BlockSpec((tt, tk), lambda i, t: (t, i + k0)), pl.BlockSpec((tt, N), lambda i, t: (t, 0))],
        out_specs=[out_spec], out_shape=[jax.ShapeDtypeStruct(out_shape, out_dtype)],
        scratch_shapes=[pltpu.VMEM((tk, N), F32)],
        compiler_params=_cp(("arbitrary", "arbitrary"), VMEM_BIG))
    return res if hosted else res[0]


def _ada_bwd(c_all, c_ctx, w_ada, d_all):
    n = w_ada.shape[1]

    def body(c_ref, cc_ref, w_ref, d_ref, gw_ref, t_ref):
        c = jnp.concatenate([c_ref[...], jnp.broadcast_to(cc_ref[...], (8, D))], axis=0)
        db = d_ref[...].astype(_BF)
        gw_ref[0] = _tn((c * _sigmoid(c)).astype(_BF), db)
        t_ref[...] = _nt(db[8:16], w_ref[...].astype(_BF))

    return pl.pallas_call(
        body, name="ada_bwd", in_specs=[_full((8, D)), _full((1, D)), _full((D, n)), _full((16, n))],
        out_specs=[_full((1, D, n)), _full((8, D))],
        out_shape=[jax.ShapeDtypeStruct((1, D, n), F32), jax.ShapeDtypeStruct((8, D), F32)], grid=(1,),
        compiler_params=_cp(("arbitrary",)),
    )(c_all, c_ctx, w_ada, d_all)


PART_ROWS = 56
R_ADA, R_ADA_C, R_GAIN, R_SINK, R_BG, R_GGLA, R_LOSS, R_WG = 0, 6, 12, 16, 17, 18, 19, 24


def _small_grads(s_in, s_ffn, s_mix, ada_l, ada_c, gains, dsink, s_bg, g_wg, loss):
    def body(si, sf, sm, al, ac, g, ds, sbg, gwg, loss_ref, o_ref):
        o_ref[...] = jnp.zeros_like(o_ref)
        o_ref[R_LOSS:R_LOSS + 1, 0:128] = loss_ref[0:1, :]
        sub = lax.broadcasted_iota(jnp.int32, (8, 128), 0)
        lane = lax.broadcasted_iota(jnp.int32, (8, 128), 1)
        o_ref[R_SINK:R_SINK + 1, 0:128] = jnp.sum(jnp.where(sub == lane, ds[...], 0.0), axis=0, keepdims=True)
        o_ref[R_BG:R_BG + 1, 0:512] = sbg[0:1, :]
        y = sm[1:2, 0:128] + sm[1:2, 128:256] + sm[1:2, 256:384] + sm[1:2, 384:512]
        y = y + pltpu.roll(y, 64, 1)
        o_ref[R_GGLA:R_GGLA + 1, 0:128] = jnp.where(lane[0:1] < DV, y, 0.0)
        o_ref[R_WG:R_WG + 16, 0:256] = gwg[0:16, 0:256]
        o_ref[R_WG + 16:R_WG + 32, 0:256] = gwg[16:32, 256:512]
        sdh_l, sdx_l, sdh_c, sdx_c = si[0:1], si[1:2], si[2:3], si[3:4]
        sdh2, sdx2, a2 = sf[0:1], sf[1:2], sf[2:3]
        a1 = sm[0:1]
        g1, g2, g3, g4 = g[0:1], g[1:2], g[2:3], g[3:4]
        sc1, gt1, sc2, gt2 = al[1:2], al[2:3], al[4:5], al[5:6]
        sc1c = ac[1:2]
        z = jnp.zeros((1, D), F32)
        rows = [sdh_l, sdx_l * g1, a1 * g2, sdh2, sdx2 * g3, a2 * g4,
                sdh_c, sdx_c * g1, z, z, z, z,
                sdx_l * (1.0 + sc1) + sdx_c * (1.0 + sc1c), a1 * gt1, sdx2 * (1.0 + sc2), a2 * gt2]
        for r, v in enumerate(rows):
            o_ref[r:r + 1, :] = v

    v8 = _full((8, D))
    return pl.pallas_call(
        body, name="small_grads",
        in_specs=[v8] * 6 + [_full((8, 128)), _full((8, 512)), _full((128, 512)), _full((8, 128))],
        out_specs=_full((PART_ROWS, D)), out_shape=jax.ShapeDtypeStruct((PART_ROWS, D), F32), grid=(1,),
        compiler_params=_cp(("arbitrary",)),
    )(s_in, s_ffn, s_mix, ada_l, ada_c, gains, dsink, s_bg, g_wg, loss)


def _row_tile(R):
    for cand in range(min(512, R // 2) // 16 * 16, 0, -16):
        if R % cand == 0:
            return cand
    return R


def _adamw(w, gs, m, v, name):
    _, R, C = w.shape
    tr = _row_tile(R)
    c1 = 1.0 / (1.0 - ADAM_B1 ** ADAM_STEP)
    c2 = 1.0 / (1.0 - ADAM_B2 ** ADAM_STEP)
    ng = len(gs)

    def body(w_ref, *refs):
        g_refs, (m_ref, v_ref, go_ref, d_ref, nm_ref, nv_ref) = refs[:ng], refs[ng:]
        c0 = 0
        for g_ref in g_refs:
            cols = slice(c0, c0 + g_ref.shape[2])
            c0 += g_ref.shape[2]
            gg = g_ref[0].astype(F32)
            for j in range(1, g_ref.shape[0]):
                gg = gg + g_ref[j].astype(F32)
            go_ref[0, :, cols] = gg
            nm = ADAM_B1 * m_ref[0, :, cols] + (1.0 - ADAM_B1) * gg
            nv = ADAM_B2 * v_ref[0, :, cols] + (1.0 - ADAM_B2) * (gg * gg)
            nm_ref[0, :, cols] = nm
            nv_ref[0, :, cols] = nv
            d_ref[0, :, cols] = -ADAM_LR * ((nm * c1) / (jnp.sqrt(nv * c2) + ADAM_EPS) + ADAM_WD * w_ref[0, :, cols])

    spec = pl.BlockSpec((1, tr, C), lambda i: (0, i, 0))
    sds = jax.ShapeDtypeStruct((1, R, C), F32)
    g_specs = [pl.BlockSpec((g.shape[0], tr, g.shape[2]), lambda i: (0, i, 0)) for g in gs]
    return pl.pallas_call(
        body, name=name, grid=(R // tr,), in_specs=[spec] + g_specs + [spec, spec], out_specs=[spec] * 4,
        out_shape=[sds] * 4, compiler_params=_cp(("parallel",), 48 * 1024 * 1024),
    )(w, *gs, m, v)


def _sum_slots(slots, name):
    _, R, C = slots.shape
    tr = _row_tile(R)

    def body(s_ref, o_ref):
        acc = s_ref[0].astype(F32)
        for j in range(1, N_DEV):
            acc = acc + s_ref[j].astype(F32)
        o_ref[...] = acc

    return pl.pallas_call(
        body, name=name, grid=(R // tr,), in_specs=[pl.BlockSpec((N_DEV, tr, C), lambda i: (0, i, 0))],
        out_specs=_rows(tr, C), out_shape=jax.ShapeDtypeStruct((R, C), F32), compiler_params=_cp(("parallel",)),
    )(slots)


def _ag2_start(x_ref, out_ref, send_sems, recv_sems, local_sem):
    x, y, c = lax.axis_index("x"), lax.axis_index("y"), lax.axis_index("c")
    me, sibling = (x, y, c), (x, y, 1 - c)
    chips = [(1 - x, y), (x, 1 - y), (1 - x, 1 - y)]

    def rows(px, py, pc):
        return out_ref.at[4 * px + 2 * py + pc]

    def copy(k, block, to, src=None):
        return pltpu.make_async_remote_copy(
            src_ref=rows(*block) if src is None else src, dst_ref=rows(*block),
            send_sem=send_sems.at[k], recv_sem=recv_sems.at[k], device_id=to, device_id_type=MESH)

    mine = pltpu.make_async_copy(x_ref, rows(*me), local_sem)
    mine.start()
    first = [copy(0, me, sibling, src=x_ref)]
    first += [copy(1 + j, me, (*chip, c), src=x_ref) for j, chip in enumerate(chips)]
    for cp in first:
        cp.start()
    return copy, mine, first, me, sibling, chips, c


def _ag2_finish(state):
    copy, mine, first, me, sibling, chips, c = state
    passed = [copy(4 + j, (*chip, c), sibling) for j, chip in enumerate(chips)]
    for j, chip in enumerate(chips):
        copy(1 + j, (*chip, c), me).wait_recv()
        passed[j].start()
    copy(0, sibling, me).wait_recv()
    for j, chip in enumerate(chips):
        copy(4 + j, (*chip, 1 - c), me).wait_recv()
    for cp in first + passed:
        cp.wait_send()
    mine.wait()


def _exchange(arrays, name, gather):
    na = len(arrays)

    def body(*refs):
        cps = _xchg_copies(refs[:na], refs[na:2 * na], *refs[2 * na:], gather=gather)
        _xchg_start(cps)
        _xchg_finish(cps)

    anyspec = pl.BlockSpec(memory_space=pl.ANY)
    return pl.pallas_call(
        body, name=name, out_shape=_xchg_out_shapes(arrays, gather), in_specs=[anyspec] * na,
        out_specs=[anyspec] * na, scratch_shapes=_xchg_scratch(na),
    )(*arrays)


def _entry(c, wg_sh, win_sh, c_ctx, w_ada):
    n = w_ada.shape[1]

    def body(c_ref, g_ref, w_ref, cc_ref, wa_ref, call_ref, gall_ref, wall_ref, ada_ref, part,
             s_send, s_recv, s_loc, w_send, w_recv, w_loc, a_send, a_recv, a_loc):
        big = _ag2_start(w_ref, wall_ref, w_send, w_recv, w_loc)
        small = _xchg_copies([c_ref, g_ref], [call_ref, gall_ref], s_send, s_recv, s_loc, gather=True)
        _xchg_start(small)
        _xchg_finish(small)
        cs = jnp.concatenate([call_ref[:, 0, :], jnp.broadcast_to(cc_ref[...], (8, D))], axis=0)
        part[...] = _nn((cs * _sigmoid(cs)).astype(_BF), wa_ref[...].astype(_BF))
        ada = _xchg_copies([part], [ada_ref], a_send, a_recv, a_loc, gather=True)
        _xchg_start(ada)
        _xchg_finish(ada)
        _ag2_finish(big)

    vm = pl.BlockSpec(memory_space=pltpu.VMEM)
    return pl.pallas_call(
        body, name="entry_gather",
        out_shape=[jax.ShapeDtypeStruct((N_DEV,) + c.shape, F32), jax.ShapeDtypeStruct((N_DEV,) + wg_sh.shape, F32),
                   jax.ShapeDtypeStruct((N_DEV,) + win_sh.shape, win_sh.dtype),
                   jax.ShapeDtypeStruct((N_DEV, 16, n), F32)],
        in_specs=[vm] * 5, out_specs=[vm] * 4,
        scratch_shapes=[pltpu.VMEM((16, n), F32)] + _xchg_scratch(2)
        + [pltpu.SemaphoreType.DMA((7,)), pltpu.SemaphoreType.DMA((7,)), pltpu.SemaphoreType.DMA] + _xchg_scratch(1),
        compiler_params=pltpu.CompilerParams(vmem_limit_bytes=VMEM_BIG),
    )(c, wg_sh, win_sh, c_ctx, w_ada)


def _rope_tables(S):
    t = np.arange(S)
    row = (t // GRID_W).astype(np.float32)
    colp = (t % GRID_W).astype(np.float32)
    half = HD // 2
    inv = (ROPE_BASE ** (-np.arange(0, half, 2, dtype=np.float32) / half)).astype(np.float32)
    ar = row[:, None] * inv[None, :]
    ac = colp[:, None] * inv[None, :]
    ang = np.concatenate([ar, ar, ac, ac], axis=-1).astype(np.float32)
    cos = np.cos(ang).astype(np.float32)
    sin = np.sin(ang).astype(np.float32)
    lane = np.arange(HD)
    first = (lane % 32) < 16
    sa = np.where(first[None, :], -sin, 0.0)
    sb = np.where(first[None, :], 0.0, sin)

    def ext(tab, ctx_val):
        full = np.zeros((CTX + S, 128), np.float32)
        full[:CTX, :] = ctx_val
        full[CTX:, :HD] = tab
        full[CTX:, HD:] = tab
        return jnp.asarray(full)

    return ext(cos, 1.0), ext(sa, 0.0), ext(sb, 0.0)


def _pad_rows_win(wt):
    return jnp.pad(wt, ((0, NP - IN_COLS), (0, 0)))


def _unpad_rows_win(g):
    return g[0:IN_COLS]


def _local_step(x, ctx, target, ada_l, ada_c, gains, sink, win_p, wg_bd, bg, ggla, wout_sh, wffi_sh, wffo_sh,
                after_small):
    S = x.shape[0]
    cos, sa, sb = _rope_tables(S)
    g1, g2, g3, g4 = (gains[i:i + 1] for i in range(4))
    sh1, sc1, gt1, sh2, sc2, gt2 = (ada_l[i:i + 1] for i in range(6))
    sh1c, sc1c = ada_c[0:1], ada_c[1:2]
    gml, gmc, gm2 = g1 * (1.0 + sc1), g1 * (1.0 + sc1c), g3 * (1.0 + sc2)
    mavg = jnp.asarray(np.kron(np.eye(N_GLA, dtype=np.float32), np.full((DV, DV), 1.0 / DV, np.float32))).astype(_BF)

    n_ffi, r_ffo, r_out = wffi_sh.shape[0], wffo_sh.shape[0], wout_sh.shape[0]
    tt_e = 1408 if (S + CTX) % 1408 == 0 else 256
    tt_s = 512 if S % 512 == 0 else 256
    h, q, k, v, gq, gk, gv, gg, z, la, wout_g = _inproj_fwd(x, ctx, gml, sh1, gmc, sh1c, win_p, wg_bd, bg,
                                                            cos, sa, sb, [wout_sh])
    attn, lse, probs, wffi_g = _attn_fwd(q, k, v, sink, [wffi_sh])
    o_f, st_f, o_b, st_b, wffo_g = _gla_fwd(gq, gk, gv, la, [wffo_sh])
    wout = wout_g.reshape(N_DEV * r_out, D)
    wffi = wffi_g.reshape(N_DEV * n_ffi, D)
    wffo = wffo_g.reshape(N_DEV * r_ffo, D)
    x1, mix = _mix_fwd(x, attn, o_f, o_b, gg, ggla, mavg, wout, gt1, g2)
    dx1, h2, du, act, df, s_ffn, loss = _ffn(x1, target, gm2, sh2, gt2, g4, wffi, wffo)
    slab_ffi = _matmul_tn(h2, du, 512, tt_s, "grad_w_ffn_in", _BF, True).reshape(N_DEV, n_ffi, D)
    tt_l = 1024 if S % 1024 == 0 else tt_s
    slab_ffo = _matmul_tn(act, df, FFN, tt_l, "grad_w_ffn_out", _BF).reshape(N_DEV, r_ffo, D)
    d_attn, do_gla, dgg, dy, s_mix, got_ffo = _mix_bwd(dx1, mix, o_f, o_b, gg, ggla, mavg, wout, gt1, g2, [slab_ffo])
    slab_out = _matmul_tn(mix, dy, D, tt_l, "grad_w_out", _BF).reshape(N_DEV, r_out, D)
    dq, dk, dv, dsink, got_ffi_a = _attn_bwd(q, k, v, sink, probs, lse, d_attn, [(slab_ffi, (0, D // 2))])
    (dgq_f, dgk_f, dgv_f, dlg_f, dgq_b, dgk_b, dgv_b, dlg_b, s_bg, got_out,
     got_ffi_b) = _gla_bwd(gq, gk, gv, la, st_f, st_b, do_gla, [slab_out, (slab_ffi, (D // 2, D // 2))])
    got_ffi = [got_ffi_a, got_ffi_b]
    dp, grad_x, s_in, g_wg = _inproj_bwd(x, ctx, gml, gmc, win_p, wg_bd, cos, sa, sb, dq, dk, dv,
                                         (dgq_f, dgq_b), (dgk_f, dgk_b), (dgv_f, dgv_b), dgg, dlg_f, dlg_b, dx1, z)
    small = _small_grads(s_in, s_ffn, s_mix, ada_l, ada_c, gains, dsink, s_bg, g_wg, loss)
    n_in, n_grp = IN_COLS // N_DEV, 2
    got_in, slab, extra, got_extra = [], None, [], []
    for j in range(n_grp):
        res = _matmul_tn(h, dp, D // n_grp, tt_e, "grad_w_in_%d" % j, _BF, True, a_cols=j,
                         hosted=[small] if j == 0 else [slab] + extra, gather=(j == 0))
        if j == 0:
            extra = after_small(res[1])
        else:
            got_in.append(res[1])
            got_extra, extra = got_extra + list(res[2:]), []
        slab = _unpad_rows_win(res[0]).reshape(N_DEV, n_in, D // n_grp)
    got_in.append(_exchange([slab], "scatter_grads", False)[0])
    return dict(grad_x=grad_x, got_in=got_in, got_out=got_out, got_ffi=got_ffi, got_ffo=got_ffo, got_extra=got_extra)


SMALL_NAMES = ["c_ctx", "b_ada", "g_pre_mix", "g_post_mix", "g_pre_ffn", "g_post_ffn", "attn_sink",
               "b_gate_fwd", "b_gate_bwd", "g_gla_norm", "w_gate_fwd", "w_gate_bwd"]


def _small_update(tot, t_tot, wg_g, w, m, v):
    c1 = 1.0 / (1.0 - ADAM_B1 ** ADAM_STEP)
    c2 = 1.0 / (1.0 - ADAM_B2 ** ADAM_STEP)
    n = len(SMALL_NAMES)

    def body(tot_ref, t_ref, wg_ref, *refs):
        w_r, m_r, v_r = refs[0:n], refs[n:2 * n], refs[2 * n:3 * n]
        g_o, d_o, nm_o, nv_o = refs[3 * n:4 * n], refs[4 * n:5 * n], refs[5 * n:6 * n], refs[6 * n:7 * n]

        def upd(i, idx, g):
            nm = ADAM_B1 * m_r[i][idx] + (1.0 - ADAM_B1) * g
            nv = ADAM_B2 * v_r[i][idx] + (1.0 - ADAM_B2) * (g * g)
            g_o[i][idx] = g
            nm_o[i][idx] = nm
            nv_o[i][idx] = nv
            d_o[i][idx] = -ADAM_LR * ((nm * c1) / (jnp.sqrt(nv * c2) + ADAM_EPS) + ADAM_WD * w_r[i][idx])

        everything = (slice(None), slice(None))
        cc = w_r[0][...]
        sc = _sigmoid(cc)
        upd(0, everything, t_ref[0:1, :] * (sc * (1.0 + cc * (1.0 - sc))))
        for j in range(6):
            upd(1, (slice(None), slice(D * j, D * j + D)),
                tot_ref[R_ADA + j:R_ADA + j + 1, :] + tot_ref[R_ADA_C + j:R_ADA_C + j + 1, :])
        for j in range(4):
            upd(2 + j, everything, tot_ref[R_GAIN + j:R_GAIN + j + 1, :])
        upd(6, everything, tot_ref[R_SINK:R_SINK + 1, 0:N_ATT])
        upd(7, everything, tot_ref[R_BG:R_BG + 1, 0:256])
        upd(8, everything, tot_ref[R_BG:R_BG + 1, 256:512])
        upd(9, everything, tot_ref[R_GGLA:R_GGLA + 1, 0:DV])
        upd(10, (0,), wg_ref[0:GATE_RANK, :])
        upd(11, (0,), wg_ref[GATE_RANK:2 * GATE_RANK, :])

    params = [w[k] for k in SMALL_NAMES] + [m[k] for k in SMALL_NAMES] + [v[k] for k in SMALL_NAMES]
    outs = pl.pallas_call(
        body, name="small_update", grid=(1,),
        in_specs=[_full(tot.shape), _full(t_tot.shape), _full(wg_g.shape)] + [_full(p.shape) for p in params],
        out_specs=[_full(w[k].shape) for k in SMALL_NAMES] * 4,
        out_shape=[jax.ShapeDtypeStruct(w[k].shape, F32) for k in SMALL_NAMES] * 4,
        compiler_params=_cp(("arbitrary",)),
    )(tot, t_tot, wg_g, *params)
    return tuple(dict(zip(SMALL_NAMES, outs[i * n:(i + 1) * n])) for i in range(4))


def kernel(x, c, ctx, c_ctx, w_ada, b_ada, g_pre_mix, g_post_mix, g_pre_ffn, g_post_ffn, w_in, attn_sink, w_gate_fwd, b_gate_fwd, w_gate_bwd, b_gate_bwd, g_gla_norm, w_out, w_ffn_in, w_ffn_out, loss_target, m_c_ctx, m_w_ada, m_b_ada, m_g_pre_mix, m_g_post_mix, m_g_pre_ffn, m_g_post_ffn, m_w_in, m_attn_sink, m_w_gate_fwd, m_b_gate_fwd, m_w_gate_bwd, m_b_gate_bwd, m_g_gla_norm, m_w_out, m_w_ffn_in, m_w_ffn_out, v_c_ctx, v_w_ada, v_b_ada, v_g_pre_mix, v_g_post_mix, v_g_pre_ffn, v_g_post_ffn, v_w_in, v_attn_sink, v_w_gate_fwd, v_b_gate_fwd, v_w_gate_bwd, v_b_gate_bwd, v_g_gla_norm, v_w_out, v_w_ffn_in, v_w_ffn_out):
    me = 4 * lax.axis_index("x") + 2 * lax.axis_index("y") + lax.axis_index("c")
    S = x.shape[1]
    n_in = w_in.shape[2]
    n_ffi = w_ffn_in.shape[2]
    r_out = w_out.shape[1]
    r_ffo = w_ffn_out.shape[1]
    n_ada = w_ada.shape[2]

    wg_sh = jnp.concatenate([w_gate_fwd.reshape(4, 128), w_gate_bwd.reshape(4, 128)], axis=0)
    c_all3, g_all, w_all, ada_all = _entry(c, wg_sh, w_in[0].T.astype(_BF), c_ctx.reshape(1, D), w_ada[0])
    c_all = c_all3.reshape(N_DEV, D)
    wgf = g_all[:, 0:4].reshape(N_DEV, GATE_RANK, 32).transpose(1, 0, 2).reshape(GATE_RANK, 256)
    wgb = g_all[:, 4:8].reshape(N_DEV, GATE_RANK, 32).transpose(1, 0, 2).reshape(GATE_RANK, 256)
    win_p = _pad_rows_win(w_all.reshape(N_DEV * n_in, D))
    wg_bd = jnp.zeros((128, 512), F32).at[0:16, 0:256].set(wgf).at[16:32, 256:512].set(wgb).astype(_BF)
    ada_full = ada_all.transpose(1, 0, 2).reshape(16, N_DEV * n_ada) + b_ada
    ada_l = jnp.pad(lax.dynamic_slice_in_dim(ada_full, me, 1, 0).reshape(6, D), ((0, 2), (0, 0)))
    ada_c = jnp.pad(ada_full[8].reshape(6, D), ((0, 2), (0, 0)))
    gains = jnp.pad(jnp.concatenate([g_pre_mix, g_post_mix, g_pre_ffn, g_post_ffn], axis=0), ((0, 4), (0, 0)))
    sink = jnp.broadcast_to(attn_sink.reshape(8, 1), (8, 128))
    bg = jnp.concatenate([b_gate_fwd, b_gate_bwd], axis=1)
    ggla = jnp.tile(g_gla_norm, (1, N_GLA))

    tail = {}

    def after_small(parts):
        tot = _sum_slots(parts, "sum_small_grads")
        d_ada_rows = parts[:, R_ADA:R_ADA + 6].reshape(N_DEV, 6 * D)
        d_ada_c = tot[R_ADA_C:R_ADA_C + 6].reshape(1, 6 * D)
        my_cols = lax.dynamic_slice_in_dim(jnp.concatenate([d_ada_rows, jnp.broadcast_to(d_ada_c, (1, 6 * D)),
                                                            jnp.zeros((7, 6 * D), F32)], axis=0), me * n_ada, n_ada, 1)
        grad_w_ada, t_part = _ada_bwd(c_all, c_ctx.reshape(1, D), w_ada[0], my_cols)
        tail.update(tot=tot, grad_w_ada=grad_w_ada)
        return [jnp.broadcast_to(t_part[None], (N_DEV,) + t_part.shape)]

    r = _local_step(x[0], ctx[0], loss_target[0], ada_l, ada_c, gains, sink, win_p, wg_bd, bg, ggla,
                    w_out[0].astype(_BF), w_ffn_in[0].T.astype(_BF), w_ffn_out[0].astype(_BF), after_small)
    tot, grad_w_ada = tail["tot"], tail["grad_w_ada"]
    loss = tot[R_LOSS, 0]
    wg_g = lax.dynamic_slice(tot, (R_WG, me * 32), (2 * GATE_RANK, 32))

    tr = lambda a: jnp.transpose(a, (0, 2, 1))
    big = {}
    t_tot = _sum_slots(r["got_extra"][0], "sum_c_ctx")
    for nm, w, g, m, v in [("w_ada", w_ada, grad_w_ada, m_w_ada, v_w_ada),
                           ("w_out", w_out, r["got_out"], m_w_out, v_w_out),
                           ("w_ffn_out", w_ffn_out, r["got_ffo"], m_w_ffn_out, v_w_ffn_out)]:
        big[nm] = _adamw(w, [g], m, v, "adamw_" + nm)
    big["w_ffn_in"] = tuple(tr(o) for o in _adamw(tr(w_ffn_in), r["got_ffi"], tr(m_w_ffn_in), tr(v_w_ffn_in),
                                                  "adamw_w_ffn_in"))
    big["w_in"] = tuple(tr(o) for o in _adamw(tr(w_in), r["got_in"], tr(m_w_in), tr(v_w_in), "adamw_w_in"))

    w_small = dict(c_ctx=c_ctx.reshape(1, D), b_ada=b_ada, g_pre_mix=g_pre_mix, g_post_mix=g_post_mix, g_pre_ffn=g_pre_ffn,
                   g_post_ffn=g_post_ffn, attn_sink=attn_sink, b_gate_fwd=b_gate_fwd, b_gate_bwd=b_gate_bwd,
                   g_gla_norm=g_gla_norm, w_gate_fwd=w_gate_fwd, w_gate_bwd=w_gate_bwd)
    m_small = dict(c_ctx=m_c_ctx.reshape(1, D), b_ada=m_b_ada, g_pre_mix=m_g_pre_mix, g_post_mix=m_g_post_mix,
                   g_pre_ffn=m_g_pre_ffn, g_post_ffn=m_g_post_ffn, attn_sink=m_attn_sink, b_gate_fwd=m_b_gate_fwd,
                   b_gate_bwd=m_b_gate_bwd, g_gla_norm=m_g_gla_norm, w_gate_fwd=m_w_gate_fwd, w_gate_bwd=m_w_gate_bwd)
    v_small = dict(c_ctx=v_c_ctx.reshape(1, D), b_ada=v_b_ada, g_pre_mix=v_g_pre_mix, g_post_mix=v_g_post_mix,
                   g_pre_ffn=v_g_pre_ffn, g_post_ffn=v_g_post_ffn, attn_sink=v_attn_sink, b_gate_fwd=v_b_gate_fwd,
                   b_gate_bwd=v_b_gate_bwd, g_gla_norm=v_g_gla_norm, w_gate_fwd=v_w_gate_fwd, w_gate_bwd=v_w_gate_bwd)
    grads_small, d_s, nm_s, nv_s = _small_update(tot, t_tot, wg_g, w_small, m_small, v_small)
    for dd in (grads_small, d_s, nm_s, nv_s):
        dd["c_ctx"] = dd["c_ctx"].reshape(D)

    order = ["c_ctx", "w_ada", "b_ada", "g_pre_mix", "g_post_mix", "g_pre_ffn", "g_post_ffn", "w_in", "attn_sink",
             "w_gate_fwd", "b_gate_fwd", "w_gate_bwd", "b_gate_bwd", "g_gla_norm", "w_out", "w_ffn_in", "w_ffn_out"]
    grads, deltas, new_m, new_v = [], [], [], []
    for nm in order:
        if nm in big:
            g_, d_, m_, v_ = big[nm]
        else:
            g_, d_, m_, v_ = grads_small[nm], d_s[nm], nm_s[nm], nv_s[nm]
        grads.append(g_)
        deltas.append(d_)
        new_m.append(m_)
        new_v.append(v_)
    return (loss, r["grad_x"][None], *grads, *deltas, *new_m, *new_v)
```
